```python
import jax, jax.numpy as jnp
from jax import lax
import numpy as np

D_MODEL = 2048
BATCH = 8
SEQ = 4096
DEPTH = 1

HEAD_DIM = 64
A_Q_HEADS = 12
A_KV_HEADS = 4
A_GROUP = A_Q_HEADS // A_KV_HEADS
WINDOW = 128
BLOCK = 128
B_HEADS = 12
C_HEADS = 4
C_HEAD_DIM = 128
MEM_TOKENS = 256
N_BRANCHES = 3
A_WIDTH = A_Q_HEADS * HEAD_DIM
A_KV_WIDTH = A_KV_HEADS * HEAD_DIM
B_WIDTH = B_HEADS * HEAD_DIM
C_WIDTH = C_HEADS * C_HEAD_DIM
EPS = 1e-6
NEG = -1e30

IN_SIZES = (A_WIDTH, A_KV_WIDTH, A_KV_WIDTH, A_WIDTH,
            B_WIDTH, B_WIDTH, B_WIDTH, B_WIDTH, B_HEADS,
            C_WIDTH, C_WIDTH,
            N_BRANCHES * D_MODEL)
IN_WIDTH = sum(IN_SIZES)
IN_OFFSETS = tuple(int(o) for o in np.cumsum(IN_SIZES)[:-1])

kernel_name = "hybrid_swa_fox_memory_gated_block"


def rms_norm(x, gain):
    x32 = x.astype(jnp.float32)
    y = x32 * lax.rsqrt(jnp.mean(x32 * x32, axis=-1, keepdims=True) + EPS)
    return (y * gain.astype(jnp.float32)).astype(x.dtype)


def alibi_slopes(n_heads):
    h = jnp.arange(1, n_heads + 1, dtype=jnp.float32)
    return jnp.exp2(-8.0 * h / n_heads)


def sliding_window_attention(q, k, v, q_gain, k_gain, sinks):
    b, s, _, d = q.shape
    nb = s // BLOCK
    q = rms_norm(q, q_gain).astype(jnp.float32)
    k = rms_norm(k, k_gain).astype(jnp.float32)
    v = v.astype(jnp.float32)
    qb = q.reshape(b, nb, BLOCK, A_KV_HEADS, A_GROUP, d)
    pad = jnp.zeros((b, BLOCK, A_KV_HEADS, d), jnp.float32)
    kp = jnp.concatenate([pad, k], axis=1).reshape(b, nb + 1, BLOCK, A_KV_HEADS, d)
    vp = jnp.concatenate([pad, v], axis=1).reshape(b, nb + 1, BLOCK, A_KV_HEADS, d)
    kb = jnp.concatenate([kp[:, :-1], kp[:, 1:]], axis=2)
    vb = jnp.concatenate([vp[:, :-1], vp[:, 1:]], axis=2)
    scores = jnp.einsum('bnqkgd,bnskd->bnkgqs', qb, kb) * (d ** -0.5)
    qi = jnp.arange(BLOCK)[:, None]
    kj = jnp.arange(2 * BLOCK)[None, :]
    rel = qi + BLOCK - kj
    key_pos = jnp.arange(nb)[:, None, None] * BLOCK - BLOCK + kj[None]
    valid = (rel >= 0) & (rel < WINDOW) & (key_pos >= 0)
    slopes = alibi_slopes(A_Q_HEADS).reshape(A_KV_HEADS, A_GROUP)
    bias = -slopes[:, :, None, None] * rel.astype(jnp.float32)
    scores = jnp.where(valid[None, :, None, None], scores + bias[None, None], NEG)
    sink = sinks.astype(jnp.float32).reshape(A_KV_HEADS, A_GROUP)[None, None, :, :, None, None]
    m = jnp.maximum(jnp.max(scores, axis=-1, keepdims=True), sink)
    p = jnp.exp(scores - m)
    denom = jnp.sum(p, axis=-1, keepdims=True) + jnp.exp(sink - m)
    out = jnp.einsum('bnkgqs,bnskd->bnqkgd', p / denom, vb)
    return out.reshape(b, s, A_Q_HEADS * d)


def forgetting_attention(q, k, v, f_logit, q_gain, k_gain):
    b, s, h, d = q.shape
    nb = s // BLOCK
    q = rms_norm(q, q_gain).astype(jnp.float32)
    k = rms_norm(k, k_gain).astype(jnp.float32)
    v = v.astype(jnp.float32)
    log_f = jax.nn.log_sigmoid(f_logit.astype(jnp.float32))
    c = jnp.cumsum(log_f, axis=1).transpose(0, 2, 1)
    q_blocks = q.reshape(b, nb, BLOCK, h, d).transpose(1, 0, 2, 3, 4)
    cq_blocks = c.reshape(b, h, nb, BLOCK).transpose(2, 0, 1, 3)
    key_pos = jnp.arange(s)
    scale = d ** -0.5

    def block_fn(args):
        qb, cqb, n = args
        sc = jnp.einsum('bqhd,bshd->bhqs', qb, k) * scale + cqb[..., None] - c[:, :, None, :]
        qpos = n * BLOCK + jnp.arange(BLOCK)
        mask = key_pos[None, :] <= qpos[:, None]
        sc = jnp.where(mask[None, None], sc, NEG)
        p = jax.nn.softmax(sc, axis=-1)
        return jnp.einsum('bhqs,bshd->bqhd', p, v)

    out = lax.map(block_fn, (q_blocks, cq_blocks, jnp.arange(nb)))
    return out.transpose(1, 0, 2, 3, 4).reshape(b, s, h * d)


def memory_attention(q, mk, mv, q_gain, k_gain):
    b, s, h, d = q.shape
    q = rms_norm(q, q_gain).astype(jnp.float32)
    mk = rms_norm(mk, k_gain).astype(jnp.float32)
    sc = jnp.einsum('bthd,bmhd->bhtm', q, mk) * (d ** -0.5)
    p = jax.nn.softmax(sc, axis=-1)
    out = jnp.einsum('bhtm,bmhd->bthd', p, mv.astype(jnp.float32))
    return out.reshape(b, s, h * d)


def hybrid_layer(x, mem, norm_gain, mem_norm_gain, w_in, b_forget,
                 q_gain_a, k_gain_a, sinks_a, q_gain_b, k_gain_b, q_gain_c, k_gain_c,
                 w_mem_kv, w_branch_a, w_branch_b, w_branch_c, w_out):
    b, s, _ = x.shape
    hn = rms_norm(x, norm_gain)
    proj = hn @ w_in
    (qa, ka, va, za, qb, kb, vb, zb, fb, qc, zc, gate_logits) = jnp.split(proj, IN_OFFSETS, axis=-1)

    ya = sliding_window_attention(qa.reshape(b, s, A_Q_HEADS, HEAD_DIM),
                                  ka.reshape(b, s, A_KV_HEADS, HEAD_DIM),
                                  va.reshape(b, s, A_KV_HEADS, HEAD_DIM),
                                  q_gain_a, k_gain_a, sinks_a).astype(x.dtype)
    ua = (ya * jax.nn.silu(za)) @ w_branch_a

    yb = forgetting_attention(qb.reshape(b, s, B_HEADS, HEAD_DIM),
                              kb.reshape(b, s, B_HEADS, HEAD_DIM),
                              vb.reshape(b, s, B_HEADS, HEAD_DIM),
                              fb + b_forget, q_gain_b, k_gain_b).astype(x.dtype)
    ub = (yb * jax.nn.silu(zb)) @ w_branch_b

    mkv = rms_norm(mem, mem_norm_gain) @ w_mem_kv
    mk, mv = jnp.split(mkv, 2, axis=-1)
    mlen = mem.shape[1]
    yc = memory_attention(qc.reshape(b, s, C_HEADS, C_HEAD_DIM),
                          mk.reshape(b, mlen, C_HEADS, C_HEAD_DIM),
                          mv.reshape(b, mlen, C_HEADS, C_HEAD_DIM),
                          q_gain_c, k_gain_c).astype(x.dtype)
    uc = (yc * jax.nn.silu(zc)) @ w_branch_c

    g = jax.nn.sigmoid(gate_logits.reshape(b, s, N_BRANCHES, D_MODEL))
    y = g[:, :, 0] * ua + g[:, :, 1] * ub + g[:, :, 2] * uc
    return x + y @ w_out


def _fwd_setup_inputs(seed: int = 0) -> dict:
    key = jax.random.key(seed)
    ks = jax.random.split(key, 20)
    f32 = jnp.float32
    nrm = lambda k, shape: jax.random.normal(k, shape, f32)
    return {
        "x": nrm(ks[0], (BATCH, SEQ, D_MODEL)),
        "mem": nrm(ks[1], (BATCH, MEM_TOKENS, D_MODEL)),
        "norm_gain": 1.0 + 0.02 * nrm(ks[2], (DEPTH, D_MODEL)),
        "mem_norm_gain": 1.0 + 0.02 * nrm(ks[3], (DEPTH, D_MODEL)),
        "w_in": nrm(ks[4], (DEPTH, D_MODEL, IN_WIDTH)) * D_MODEL ** -0.5,
        "b_forget": 3.0 + 0.5 * nrm(ks[5], (DEPTH, B_HEADS)),
        "q_gain_a": 1.0 + 0.02 * nrm(ks[6], (DEPTH, HEAD_DIM)),
        "k_gain_a": 1.0 + 0.02 * nrm(ks[7], (DEPTH, HEAD_DIM)),
        "sinks_a": 0.5 * nrm(ks[8], (DEPTH, A_Q_HEADS)),
        "q_gain_b": 1.0 + 0.02 * nrm(ks[9], (DEPTH, HEAD_DIM)),
        "k_gain_b": 1.0 + 0.02 * nrm(ks[10], (DEPTH, HEAD_DIM)),
        "q_gain_c": 1.0 + 0.02 * nrm(ks[11], (DEPTH, C_HEAD_DIM)),
        "k_gain_c": 1.0 + 0.02 * nrm(ks[12], (DEPTH, C_HEAD_DIM)),
        "w_mem_kv": nrm(ks[13], (DEPTH, D_MODEL, 2 * C_WIDTH)) * D_MODEL ** -0.5,
        "w_branch_a": nrm(ks[14], (DEPTH, A_WIDTH, D_MODEL)) * A_WIDTH ** -0.5,
        "w_branch_b": nrm(ks[15], (DEPTH, B_WIDTH, D_MODEL)) * B_WIDTH ** -0.5,
        "w_branch_c": nrm(ks[16], (DEPTH, C_WIDTH, D_MODEL)) * C_WIDTH ** -0.5,
        "w_out": nrm(ks[17], (DEPTH, D_MODEL, D_MODEL)) * D_MODEL ** -0.5,
    }


def _fwd_reference(x, mem, norm_gain, mem_norm_gain, w_in, b_forget,
              q_gain_a, k_gain_a, sinks_a, q_gain_b, k_gain_b, q_gain_c, k_gain_c,
              w_mem_kv, w_branch_a, w_branch_b, w_branch_c, w_out):
    for layer in range(DEPTH):
        x = hybrid_layer(x, mem, norm_gain[layer], mem_norm_gain[layer], w_in[layer], b_forget[layer],
                         q_gain_a[layer], k_gain_a[layer], sinks_a[layer],
                         q_gain_b[layer], k_gain_b[layer], q_gain_c[layer], k_gain_c[layer],
                         w_mem_kv[layer], w_branch_a[layer], w_branch_b[layer], w_branch_c[layer],
                         w_out[layer])
    return x


import jax as _jax
import jax.numpy as _jnp

TWIN_FORMAT = 'train_step'
FWD_PARAMS = ['x', 'mem', 'norm_gain', 'mem_norm_gain', 'w_in', 'b_forget', 'q_gain_a', 'k_gain_a', 'sinks_a', 'q_gain_b', 'k_gain_b', 'q_gain_c', 'k_gain_c', 'w_mem_kv', 'w_branch_a', 'w_branch_b', 'w_branch_c', 'w_out']
TWIN_WEIGHTS = ['norm_gain', 'mem_norm_gain', 'w_in', 'b_forget', 'q_gain_a', 'k_gain_a', 'sinks_a', 'q_gain_b', 'k_gain_b', 'q_gain_c', 'k_gain_c', 'w_mem_kv', 'w_branch_a', 'w_branch_b', 'w_branch_c', 'w_out']
TWIN_DIFF_INPUT = 'x'
TWIN_INPUTS = ['x', 'mem', 'norm_gain', 'mem_norm_gain', 'w_in', 'b_forget', 'q_gain_a', 'k_gain_a', 'sinks_a', 'q_gain_b', 'k_gain_b', 'q_gain_c', 'k_gain_c', 'w_mem_kv', 'w_branch_a', 'w_branch_b', 'w_branch_c', 'w_out', 'loss_target', 'm_norm_gain', 'm_mem_norm_gain', 'm_w_in', 'm_b_forget', 'm_q_gain_a', 'm_k_gain_a', 'm_sinks_a', 'm_q_gain_b', 'm_k_gain_b', 'm_q_gain_c', 'm_k_gain_c', 'm_w_mem_kv', 'm_w_branch_a', 'm_w_branch_b', 'm_w_branch_c', 'm_w_out', 'v_norm_gain', 'v_mem_norm_gain', 'v_w_in', 'v_b_forget', 'v_q_gain_a', 'v_k_gain_a', 'v_sinks_a', 'v_q_gain_b', 'v_k_gain_b', 'v_q_gain_c', 'v_k_gain_c', 'v_w_mem_kv', 'v_w_branch_a', 'v_w_branch_b', 'v_w_branch_c', 'v_w_out']
TWIN_OUTPUTS = ['loss', 'grad_x', 'grad_norm_gain', 'grad_mem_norm_gain', 'grad_w_in', 'grad_b_forget', 'grad_q_gain_a', 'grad_k_gain_a', 'grad_sinks_a', 'grad_q_gain_b', 'grad_k_gain_b', 'grad_q_gain_c', 'grad_k_gain_c', 'grad_w_mem_kv', 'grad_w_branch_a', 'grad_w_branch_b', 'grad_w_branch_c', 'grad_w_out', 'delta_norm_gain', 'delta_mem_norm_gain', 'delta_w_in', 'delta_b_forget', 'delta_q_gain_a', 'delta_k_gain_a', 'delta_sinks_a', 'delta_q_gain_b', 'delta_k_gain_b', 'delta_q_gain_c', 'delta_k_gain_c', 'delta_w_mem_kv', 'delta_w_branch_a', 'delta_w_branch_b', 'delta_w_branch_c', 'delta_w_out', 'new_m_norm_gain', 'new_m_mem_norm_gain', 'new_m_w_in', 'new_m_b_forget', 'new_m_q_gain_a', 'new_m_k_gain_a', 'new_m_sinks_a', 'new_m_q_gain_b', 'new_m_k_gain_b', 'new_m_q_gain_c', 'new_m_k_gain_c', 'new_m_w_mem_kv', 'new_m_w_branch_a', 'new_m_w_branch_b', 'new_m_w_branch_c', 'new_m_w_out', 'new_v_norm_gain', 'new_v_mem_norm_gain', 'new_v_w_in', 'new_v_b_forget', 'new_v_q_gain_a', 'new_v_k_gain_a', 'new_v_sinks_a', 'new_v_q_gain_b', 'new_v_k_gain_b', 'new_v_q_gain_c', 'new_v_k_gain_c', 'new_v_w_mem_kv', 'new_v_w_branch_a', 'new_v_w_branch_b', 'new_v_w_branch_c', 'new_v_w_out']
TWIN_LEAF_KINDS = {'loss': 'loss', 'grad_x': 'grad_x', 'grad_norm_gain': 'grad_w', 'grad_mem_norm_gain': 'grad_w', 'grad_w_in': 'grad_w', 'grad_b_forget': 'grad_w', 'grad_q_gain_a': 'grad_w', 'grad_k_gain_a': 'grad_w', 'grad_sinks_a': 'grad_w', 'grad_q_gain_b': 'grad_w', 'grad_k_gain_b': 'grad_w', 'grad_q_gain_c': 'grad_w', 'grad_k_gain_c': 'grad_w', 'grad_w_mem_kv': 'grad_w', 'grad_w_branch_a': 'grad_w', 'grad_w_branch_b': 'grad_w', 'grad_w_branch_c': 'grad_w', 'grad_w_out': 'grad_w', 'delta_norm_gain': 'delta_w', 'delta_mem_norm_gain': 'delta_w', 'delta_w_in': 'delta_w', 'delta_b_forget': 'delta_w', 'delta_q_gain_a': 'delta_w', 'delta_k_gain_a': 'delta_w', 'delta_sinks_a': 'delta_w', 'delta_q_gain_b': 'delta_w', 'delta_k_gain_b': 'delta_w', 'delta_q_gain_c': 'delta_w', 'delta_k_gain_c': 'delta_w', 'delta_w_mem_kv': 'delta_w', 'delta_w_branch_a': 'delta_w', 'delta_w_branch_b': 'delta_w', 'delta_w_branch_c': 'delta_w', 'delta_w_out': 'delta_w', 'new_m_norm_gain': 'new_m', 'new_m_mem_norm_gain': 'new_m', 'new_m_w_in': 'new_m', 'new_m_b_forget': 'new_m', 'new_m_q_gain_a': 'new_m', 'new_m_k_gain_a': 'new_m', 'new_m_sinks_a': 'new_m', 'new_m_q_gain_b': 'new_m', 'new_m_k_gain_b': 'new_m', 'new_m_q_gain_c': 'new_m', 'new_m_k_gain_c': 'new_m', 'new_m_w_mem_kv': 'new_m', 'new_m_w_branch_a': 'new_m', 'new_m_w_branch_b': 'new_m', 'new_m_w_branch_c': 'new_m', 'new_m_w_out': 'new_m', 'new_v_norm_gain': 'new_v', 'new_v_mem_norm_gain': 'new_v', 'new_v_w_in': 'new_v', 'new_v_b_forget': 'new_v', 'new_v_q_gain_a': 'new_v', 'new_v_k_gain_a': 'new_v', 'new_v_sinks_a': 'new_v', 'new_v_q_gain_b': 'new_v', 'new_v_k_gain_b': 'new_v', 'new_v_q_gain_c': 'new_v', 'new_v_k_gain_c': 'new_v', 'new_v_w_mem_kv': 'new_v', 'new_v_w_branch_a': 'new_v', 'new_v_w_branch_b': 'new_v', 'new_v_w_branch_c': 'new_v', 'new_v_w_out': 'new_v'}


def _forward(args):
    return _fwd_reference(*[args[k] for k in FWD_PARAMS])


def _output_shape():
    def fwd():
        inp = _fwd_setup_inputs(0)
        return _fwd_reference(*[inp[k] for k in FWD_PARAMS])
    out = _jax.eval_shape(fwd)
    return out.shape, out.dtype

N_MICROBATCH = 1
ADAM_LR = 0.001
ADAM_B1 = 0.9
ADAM_B2 = 0.999
ADAM_EPS = 1e-08
ADAM_WD = 0.01
ADAM_STEP = 10
PER_EXAMPLE_BATCH_AXIS = {'x': 0, 'mem': 0, 'loss_target': 0}
SHARED_INPUTS = []
_WEIGHT_DTYPES = {'norm_gain': _jnp.float32, 'mem_norm_gain': _jnp.float32, 'w_in': _jnp.float32, 'b_forget': _jnp.float32, 'q_gain_a': _jnp.float32, 'k_gain_a': _jnp.float32, 'sinks_a': _jnp.float32, 'q_gain_b': _jnp.float32, 'k_gain_b': _jnp.float32, 'q_gain_c': _jnp.float32, 'k_gain_c': _jnp.float32, 'w_mem_kv': _jnp.float32, 'w_branch_a': _jnp.float32, 'w_branch_b': _jnp.float32, 'w_branch_c': _jnp.float32, 'w_out': _jnp.float32}
MOMENT_SCALE = {'norm_gain': 5.956696e-01, 'mem_norm_gain': 1.669842e-02, 'w_in': 1.588258e-02, 'b_forget': 6.502912e+00, 'q_gain_a': 2.028190e+00, 'k_gain_a': 2.030359e+00, 'sinks_a': 4.303738e+00, 'q_gain_b': 1.947266e+00, 'k_gain_b': 1.944361e+00, 'q_gain_c': 2.559501e-01, 'k_gain_c': 2.550663e-01, 'w_mem_kv': 9.899656e-03, 'w_branch_a': 1.386149e-02, 'w_branch_b': 1.211203e-02, 'w_branch_c': 4.592875e-03, 'w_out': 1.835268e-02}


def _to_microbatches(a, axis):
    t = _jnp.moveaxis(a, axis, 0)
    t = t.reshape((N_MICROBATCH, t.shape[0] // N_MICROBATCH) + t.shape[1:])
    return _jnp.moveaxis(t, 1, axis + 1)


def setup_inputs(seed: int = 0) -> dict:
    inp = _fwd_setup_inputs(seed)
    key = _jax.random.fold_in(_jax.random.key(seed), 7919)
    shape, _ = _output_shape()
    out = dict(inp)
    out["loss_target"] = _jax.random.normal(_jax.random.fold_in(key, 0), shape, _jnp.float32)
    for i, name in enumerate(TWIN_WEIGHTS):
        w = inp[name].astype(_jnp.float32)
        if MOMENT_SCALE is None:
            s = _jnp.sqrt(_jnp.mean(_jnp.square(w)) + 1e-30)
        else:
            s = MOMENT_SCALE[name]
        km, kv = _jax.random.split(_jax.random.fold_in(key, i + 1))
        out[name] = w
        out["m_" + name] = s * _jax.random.normal(km, w.shape, _jnp.float32)
        out["v_" + name] = (s * s) * _jax.random.uniform(kv, w.shape, _jnp.float32, 0.5, 1.5)
    if N_MICROBATCH > 1:
        for name, axis in PER_EXAMPLE_BATCH_AXIS.items():
            out[name] = _to_microbatches(out[name], axis)
    return {'x': out['x'], 'mem': out['mem'], 'norm_gain': out['norm_gain'], 'mem_norm_gain': out['mem_norm_gain'], 'w_in': out['w_in'], 'b_forget': out['b_forget'], 'q_gain_a': out['q_gain_a'], 'k_gain_a': out['k_gain_a'], 'sinks_a': out['sinks_a'], 'q_gain_b': out['q_gain_b'], 'k_gain_b': out['k_gain_b'], 'q_gain_c': out['q_gain_c'], 'k_gain_c': out['k_gain_c'], 'w_mem_kv': out['w_mem_kv'], 'w_branch_a': out['w_branch_a'], 'w_branch_b': out['w_branch_b'], 'w_branch_c': out['w_branch_c'], 'w_out': out['w_out'], 'loss_target': out['loss_target'], 'm_norm_gain': out['m_norm_gain'], 'm_mem_norm_gain': out['m_mem_norm_gain'], 'm_w_in': out['m_w_in'], 'm_b_forget': out['m_b_forget'], 'm_q_gain_a': out['m_q_gain_a'], 'm_k_gain_a': out['m_k_gain_a'], 'm_sinks_a': out['m_sinks_a'], 'm_q_gain_b': out['m_q_gain_b'], 'm_k_gain_b': out['m_k_gain_b'], 'm_q_gain_c': out['m_q_gain_c'], 'm_k_gain_c': out['m_k_gain_c'], 'm_w_mem_kv': out['m_w_mem_kv'], 'm_w_branch_a': out['m_w_branch_a'], 'm_w_branch_b': out['m_w_branch_b'], 'm_w_branch_c': out['m_w_branch_c'], 'm_w_out': out['m_w_out'], 'v_norm_gain': out['v_norm_gain'], 'v_mem_norm_gain': out['v_mem_norm_gain'], 'v_w_in': out['v_w_in'], 'v_b_forget': out['v_b_forget'], 'v_q_gain_a': out['v_q_gain_a'], 'v_k_gain_a': out['v_k_gain_a'], 'v_sinks_a': out['v_sinks_a'], 'v_q_gain_b': out['v_q_gain_b'], 'v_k_gain_b': out['v_k_gain_b'], 'v_q_gain_c': out['v_q_gain_c'], 'v_k_gain_c': out['v_k_gain_c'], 'v_w_mem_kv': out['v_w_mem_kv'], 'v_w_branch_a': out['v_w_branch_a'], 'v_w_branch_b': out['v_w_branch_b'], 'v_w_branch_c': out['v_w_branch_c'], 'v_w_out': out['v_w_out']}


def _loss(weights, diff, rest, loss_target):
    with _jax.named_scope("forward"):
        args = {**rest, TWIN_DIFF_INPUT: diff, **{k: w.astype(_WEIGHT_DTYPES[k]) for k, w in weights.items()}}
        y = _forward(args)
    with _jax.named_scope("loss_head"):
        err = _jnp.square(y.astype(_jnp.float32) - loss_target)
        return 0.5 * _jnp.sum(_jnp.mean(err, axis=-1)) if err.ndim else 0.5 * err


def _adamw(w, g, m, v):
    m = ADAM_B1 * m + (1.0 - ADAM_B1) * g
    v = ADAM_B2 * v + (1.0 - ADAM_B2) * _jnp.square(g)
    m_hat = m / (1.0 - ADAM_B1 ** ADAM_STEP)
    v_hat = v / (1.0 - ADAM_B2 ** ADAM_STEP)
    delta = -ADAM_LR * (m_hat / (_jnp.sqrt(v_hat) + ADAM_EPS) + ADAM_WD * w)
    return delta, m, v


def reference(x, mem, norm_gain, mem_norm_gain, w_in, b_forget, q_gain_a, k_gain_a, sinks_a, q_gain_b, k_gain_b, q_gain_c, k_gain_c, w_mem_kv, w_branch_a, w_branch_b, w_branch_c, w_out, loss_target, m_norm_gain, m_mem_norm_gain, m_w_in, m_b_forget, m_q_gain_a, m_k_gain_a, m_sinks_a, m_q_gain_b, m_k_gain_b, m_q_gain_c, m_k_gain_c, m_w_mem_kv, m_w_branch_a, m_w_branch_b, m_w_branch_c, m_w_out, v_norm_gain, v_mem_norm_gain, v_w_in, v_b_forget, v_q_gain_a, v_k_gain_a, v_sinks_a, v_q_gain_b, v_k_gain_b, v_q_gain_c, v_k_gain_c, v_w_mem_kv, v_w_branch_a, v_w_branch_b, v_w_branch_c, v_w_out):
    given = dict(x=x, mem=mem, norm_gain=norm_gain, mem_norm_gain=mem_norm_gain, w_in=w_in, b_forget=b_forget, q_gain_a=q_gain_a, k_gain_a=k_gain_a, sinks_a=sinks_a, q_gain_b=q_gain_b, k_gain_b=k_gain_b, q_gain_c=q_gain_c, k_gain_c=k_gain_c, w_mem_kv=w_mem_kv, w_branch_a=w_branch_a, w_branch_b=w_branch_b, w_branch_c=w_branch_c, w_out=w_out, loss_target=loss_target, m_norm_gain=m_norm_gain, m_mem_norm_gain=m_mem_norm_gain, m_w_in=m_w_in, m_b_forget=m_b_forget, m_q_gain_a=m_q_gain_a, m_k_gain_a=m_k_gain_a, m_sinks_a=m_sinks_a, m_q_gain_b=m_q_gain_b, m_k_gain_b=m_k_gain_b, m_q_gain_c=m_q_gain_c, m_k_gain_c=m_k_gain_c, m_w_mem_kv=m_w_mem_kv, m_w_branch_a=m_w_branch_a, m_w_branch_b=m_w_branch_b, m_w_branch_c=m_w_branch_c, m_w_out=m_w_out, v_norm_gain=v_norm_gain, v_mem_norm_gain=v_mem_norm_gain, v_w_in=v_w_in, v_b_forget=v_b_forget, v_q_gain_a=v_q_gain_a, v_k_gain_a=v_k_gain_a, v_sinks_a=v_sinks_a, v_q_gain_b=v_q_gain_b, v_k_gain_b=v_k_gain_b, v_q_gain_c=v_q_gain_c, v_k_gain_c=v_k_gain_c, v_w_mem_kv=v_w_mem_kv, v_w_branch_a=v_w_branch_a, v_w_branch_b=v_w_branch_b, v_w_branch_c=v_w_branch_c, v_w_out=v_w_out)
    weights = {n: given[n] for n in TWIN_WEIGHTS}
    shared = {n: given[n] for n in SHARED_INPUTS}
    per_example = {n: given[n] for n in ['x', 'mem']}
    grad_fn = _jax.value_and_grad(_loss, argnums=(0, 1))

    def one_microbatch(ex, loss_target):
        ex = dict(ex)
        diff = ex.pop(TWIN_DIFF_INPUT)
        return grad_fn(weights, diff, {**shared, **ex}, loss_target)

    if N_MICROBATCH == 1:
        loss, (grad_w, grad_x) = one_microbatch(per_example, given["loss_target"])
    else:
        def body(carry, xs):
            loss_sum, grad_sum = carry
            l_k, (gw_k, gx_k) = one_microbatch(xs[0], xs[1])
            with _jax.named_scope("update"):
                return (loss_sum + l_k, _jax.tree.map(_jnp.add, grad_sum, gw_k)), gx_k

        init = (_jnp.zeros((), _jnp.float32), _jax.tree.map(_jnp.zeros_like, weights))
        (loss, grad_w), grad_x = _jax.lax.scan(body, init, (per_example, given["loss_target"]))
    with _jax.named_scope("update"):
        delta_w, new_m, new_v = {}, {}, {}
        for n in TWIN_WEIGHTS:
            delta_w[n], new_m[n], new_v[n] = _adamw(weights[n], grad_w[n], given["m_" + n], given["v_" + n])
    return (loss, grad_x, *[grad_w[n] for n in TWIN_WEIGHTS], *[delta_w[n] for n in TWIN_WEIGHTS],
            *[new_m[n] for n in TWIN_WEIGHTS], *[new_v[n] for n in TWIN_WEIGHTS])
```

```python
import functools

import numpy as np
import jax
import jax.numpy as jnp
from jax import lax
from jax.experimental import pallas as pl
from jax.experimental.pallas import tpu as pltpu

F32 = jnp.float32
BF16 = jnp.bfloat16
HI = lax.Precision.HIGHEST
SDS = jax.ShapeDtypeStruct
MESH = pl.DeviceIdType.MESH

D_MODEL = 2048
HEAD_DIM = 64
A_HEADS = 12
A_GROUP = 3
B_HEADS = 12
C_HEADS = 4
C_HEAD_DIM = 128
WINDOW = 128
EPS = 1e-6
NEG = -1e30
LANE = 128

QA, KA, VA, ZA = 0, 768, 1024, 1280
QB, KB, VB, ZB = 2048, 2816, 3584, 4352
QC, ZC = 5120, 5632
GATE = 6144
P_MAIN = 12288
N_FORGET = 12
FORGET_COL = 5120
SHARD_COLS = 3075
SLAB = 3200
SLAB_START = (0, 3072, 6016, 9088)
SLAB_SHIFT = (0, 3, 122, 125)
N_CHIPS = 4

ADAM_LR = 0.001
ADAM_B1 = 0.9
ADAM_B2 = 0.999
ADAM_EPS = 1e-08
ADAM_WD = 0.01
ADAM_STEP = 10

VMEM_LIMIT = 56 * 1024 * 1024


def _params(sem, vmem=VMEM_LIMIT):
    return pltpu.CompilerParams(dimension_semantics=sem, vmem_limit_bytes=vmem)


def _win(tr, width, off):
    return pl.BlockSpec((pl.Element(tr), pl.Element(width)), lambda i, *_: (i * tr, off))


def _rowblk(tr, width):
    return pl.BlockSpec((tr, width), lambda i, *_: (i, 0))


def _const(shape):
    nd = len(shape)
    return pl.BlockSpec(shape, lambda *_: (0,) * nd)


def _rms(x, g):
    return x * lax.rsqrt(jnp.mean(x * x, axis=-1, keepdims=True) + EPS) * g


def _head_norm(x, g_tiled, bd):
    ms = jnp.dot(x * x, bd, precision=HI, preferred_element_type=F32)
    return x * lax.rsqrt(ms + EPS) * g_tiled


def _silu(z):
    return z * jax.nn.sigmoid(z)


def _dot_nt(a, b):
    return lax.dot_general(a, b, (((1,), (1,)), ((), ())), preferred_element_type=F32)


def _dot_tn(a, b):
    return lax.dot_general(a, b, (((0,), (0,)), ((), ())), preferred_element_type=F32)


def _dot(a, b):
    return jnp.dot(a, b, preferred_element_type=F32)


def _swa_fn(qk, vz, qkp, vzp, qg, kg, sinks, bd, first):
    q = _head_norm(qk[:, :768], qg, bd)
    k2 = jnp.concatenate([qkp[:, 768:], qk[:, 768:]], axis=0)
    k2 = _head_norm(k2, kg, bd[:256, :256])
    v2 = jnp.concatenate([vzp[:, :256], vz[:, :256]], axis=0)
    z = vz[:, 256:]
    qi = lax.broadcasted_iota(jnp.int32, (WINDOW, 2 * WINDOW), 0)
    kj = lax.broadcasted_iota(jnp.int32, (WINDOW, 2 * WINDOW), 1)
    rel = qi + WINDOW - kj
    valid = (rel >= 0) & (rel < WINDOW) & (kj >= WINDOW * first.astype(jnp.int32))
    relf = rel.astype(F32)
    qb = q.astype(BF16)
    kb = k2.astype(BF16)
    vb = v2.astype(BF16)
    outs = []
    for h in range(A_HEADS):
        kv = h // A_GROUP
        s = _dot_nt(qb[:, 64 * h:64 * h + 64], kb[:, 64 * kv:64 * kv + 64]) * (HEAD_DIM ** -0.5)
        slope = float(2.0 ** (-8.0 * (h + 1) / A_HEADS))
        s = jnp.where(valid, s - slope * relf, NEG)
        sink = sinks[:, h:h + 1]
        m = lax.stop_gradient(jnp.maximum(jnp.max(s, axis=-1, keepdims=True), sink))
        p = jnp.exp(s - m)
        den = jnp.sum(p, axis=-1, keepdims=True) + jnp.exp(sink - m)
        outs.append(_dot((p / den).astype(BF16), vb[:, 64 * kv:64 * kv + 64]))
    return jnp.concatenate(outs, axis=1) * _silu(z)


def _mem_fn(qz, mkv, qg, kg, bd):
    q = _head_norm(qz[:, :512], qg, bd).astype(BF16)
    k = _head_norm(mkv[:, :512], kg, bd).astype(BF16)
    v = mkv[:, 512:].astype(BF16)
    z = qz[:, 512:]
    outs = []
    for h in range(C_HEADS):
        sl = slice(128 * h, 128 * h + 128)
        s = _dot_nt(q[:, sl], k[:, sl]) * (C_HEAD_DIM ** -0.5)
        m = lax.stop_gradient(jnp.max(s, axis=-1, keepdims=True))
        p = jnp.exp(s - m)
        den = jnp.sum(p, axis=-1, keepdims=True)
        outs.append(_dot((p / den).astype(BF16), v[:, sl]))
    return jnp.concatenate(outs, axis=1) * _silu(z)


def _qn_fn(q, g, bd):
    return _head_norm(q, g, bd) * (HEAD_DIM ** -0.5)


def _kn_fn(k, g, bd):
    return _head_norm(k, g, bd)


def _block_diag(width, hd):
    i = np.arange(width) // hd
    return jnp.asarray((i[:, None] == i[None, :]).astype(np.float32) / hd)


def _head_sum(width, hd):
    i = np.arange(width) // hd
    return jnp.asarray((i[:, None] == np.arange(LANE)[None, :]).astype(np.float32))


def _rms_fwd(x, g, *, tr, name):
    rows, dm = x.shape

    def body(x_ref, g_ref, o_ref):
        o_ref[...] = _rms(x_ref[...], g_ref[...]).astype(BF16)

    return pl.pallas_call(
        body, grid=(rows // tr,),
        in_specs=[_rowblk(tr, dm), _const((1, dm))],
        out_specs=_rowblk(tr, dm),
        out_shape=SDS((rows, dm), BF16), name=name,
        compiler_params=_params(("parallel",)))(x, g)


def _rms_bwd(x, g, dy, resid, *, tr, name):
    rows, dm = x.shape
    want_dx = resid is not None

    def body(*refs):
        if want_dx:
            x_ref, g_ref, dy_ref, r_ref, dx_ref, dg_ref = refs
        else:
            x_ref, g_ref, dy_ref, dg_ref = refs
        _, vjp = jax.vjp(_rms, x_ref[...], g_ref[...])
        dx, dg = vjp(dy_ref[...])

        @pl.when(pl.program_id(0) == 0)
        def _():
            dg_ref[...] = jnp.zeros_like(dg_ref)

        dg_ref[...] += dg
        if want_dx:
            dx_ref[...] = r_ref[...] + dx

    ins = [x, g, dy] + ([resid] if want_dx else [])
    in_specs = [_rowblk(tr, dm), _const((1, dm)), _rowblk(tr, dm)] + ([_rowblk(tr, dm)] if want_dx else [])
    out_specs = ([_rowblk(tr, dm)] if want_dx else []) + [_const((1, dm))]
    out_shape = ([SDS((rows, dm), F32)] if want_dx else []) + [SDS((1, dm), F32)]
    return pl.pallas_call(
        body, grid=(rows // tr,), in_specs=in_specs, out_specs=out_specs, out_shape=out_shape, name=name,
        compiler_params=_params(("arbitrary",)))(*ins)


def _matmul(a, b, *, dims, out_dtype, tm, tn, tk, name, add=None):
    if dims == "tn":
        kdim, m = a.shape
    else:
        m, kdim = a.shape
    n = b.shape[0] if dims == "nt" else b.shape[1]
    tm, tn, tk = min(tm, m), min(tn, n), min(tk, kdim)
    assert m % tm == 0 and n % tn == 0 and kdim % tk == 0, (name, a.shape, b.shape)
    nk = kdim // tk
    has_add = add is not None

    def body(*refs):
        if has_add:
            a_ref, b_ref, add_ref, o_ref, acc = refs
        else:
            a_ref, b_ref, o_ref, acc = refs
        k = pl.program_id(2)
        if dims == "nn":
            part = _dot(a_ref[...], b_ref[...])
        elif dims == "nt":
            part = _dot_nt(a_ref[...], b_ref[...])
        else:
            part = _dot_tn(a_ref[...], b_ref[...])

        @pl.when(k == 0)
        def _():
            acc[...] = part + add_ref[...] if has_add else part

        @pl.when(k > 0)
        def _():
            acc[...] += part

        @pl.when(k == nk - 1)
        def _():
            o_ref[...] = acc[...].astype(out_dtype)

    a_spec = pl.BlockSpec((tk, tm), lambda i, j, k: (k, i)) if dims == "tn" else pl.BlockSpec((tm, tk), lambda i, j, k: (i, k))
    b_spec = pl.BlockSpec((tn, tk), lambda i, j, k: (j, k)) if dims == "nt" else pl.BlockSpec((tk, tn), lambda i, j, k: (k, j))
    o_spec = pl.BlockSpec((tm, tn), lambda i, j, k: (i, j))
    ins = [a, b] + ([add] if has_add else [])
    in_specs = [a_spec, b_spec] + ([o_spec] if has_add else [])
    return pl.pallas_call(
        body, grid=(m // tm, n // tn, nk), in_specs=in_specs, out_specs=o_spec,
        out_shape=SDS((m, n), out_dtype), scratch_shapes=[pltpu.VMEM((tm, tn), F32)], name=name,
        compiler_params=_params(("parallel", "parallel", "arbitrary")))(*ins)


def _swa_specs(nb):
    prev = lambda off: pl.BlockSpec((pl.Element(WINDOW), pl.Element(1024)),
                                    lambda n: (jnp.maximum(n - 1, 0) * WINDOW, off))
    return [_win(WINDOW, 1024, QA), _win(WINDOW, 1024, VA), prev(QA), prev(VA),
            _const((1, 768)), _const((1, 256)), _const((1, LANE)), _const((768, 768))]


def _swa_fwd(proj, qg, kg, sinks, bd):
    s = proj.shape[0]
    nb = s // WINDOW

    def body(qk_ref, vz_ref, qkp_ref, vzp_ref, qg_ref, kg_ref, sk_ref, bd_ref, o_ref):
        first = pl.program_id(0) == 0
        o_ref[...] = _swa_fn(qk_ref[...], vz_ref[...], qkp_ref[...], vzp_ref[...], qg_ref[...], kg_ref[...],
                             sk_ref[...], bd_ref[...], first).astype(BF16)

    return pl.pallas_call(
        body, grid=(nb,), in_specs=_swa_specs(nb), out_specs=_rowblk(WINDOW, 768),
        out_shape=SDS((s, 768), BF16), name="swa_fwd",
        compiler_params=_params(("parallel",)))(proj, proj, proj, proj, qg, kg, sinks, bd)


def _swa_bwd(proj, qg, kg, sinks, bd, dga):
    s = proj.shape[0]
    nb = s // WINDOW

    def body(qk_ref, vz_ref, qkp_ref, vzp_ref, qg_ref, kg_ref, sk_ref, bd_ref, dg_ref,
             dcur_ref, dprev_ref, dqg_ref, dkg_ref, dsk_ref):
        first = pl.program_id(0) == 0
        bd_v = bd_ref[...]
        fn = lambda qk, vz, qkp, vzp, qg_, kg_, sk: _swa_fn(qk, vz, qkp, vzp, qg_, kg_, sk, bd_v, first)
        _, vjp = jax.vjp(fn, qk_ref[...], vz_ref[...], qkp_ref[...], vzp_ref[...], qg_ref[...], kg_ref[...], sk_ref[...])
        dqk, dvz, dqkp, dvzp, dqg, dkg, dsk = vjp(dg_ref[...])

        @pl.when(first)
        def _():
            dqg_ref[...] = jnp.zeros_like(dqg_ref)
            dkg_ref[...] = jnp.zeros_like(dkg_ref)
            dsk_ref[...] = jnp.zeros_like(dsk_ref)

        dqg_ref[...] += dqg
        dkg_ref[...] += dkg
        dsk_ref[...] += dsk
        dcur_ref[...] = jnp.concatenate([dqk, dvz], axis=1)
        dprev_ref[...] = jnp.concatenate([dqkp[:, 768:], dvzp[:, :256]], axis=1)

    return pl.pallas_call(
        body, grid=(nb,), in_specs=_swa_specs(nb) + [_rowblk(WINDOW, 768)],
        out_specs=[_rowblk(WINDOW, 2048), pl.BlockSpec((None, WINDOW, 512), lambda n: (n, 0, 0)),
                   _const((1, 768)), _const((1, 256)), _const((1, LANE))],
        out_shape=[SDS((s, 2048), F32), SDS((nb, WINDOW, 512), F32), SDS((1, 768), F32), SDS((1, 256), F32),
                   SDS((1, LANE), F32)],
        name="swa_bwd", compiler_params=_params(("arbitrary",)))(proj, proj, proj, proj, qg, kg, sinks, bd, dga)


def _swa_combine(dcur, dprev):
    s = dcur.shape[0]
    nb = s // WINDOW

    def body(c_ref, p_ref, o_ref):
        c = c_ref[...]
        nxt = jnp.where(pl.program_id(0) == nb - 1, 0.0, p_ref[...])
        o_ref[...] = jnp.concatenate([c[:, :768], c[:, 768:1280] + nxt, c[:, 1280:]], axis=1).astype(BF16)

    return pl.pallas_call(
        body, grid=(nb,),
        in_specs=[_rowblk(WINDOW, 2048), pl.BlockSpec((None, WINDOW, 512), lambda n: (jnp.minimum(n + 1, nb - 1), 0, 0))],
        out_specs=_rowblk(WINDOW, 2048), out_shape=SDS((s, 2048), BF16), name="swa_combine",
        compiler_params=_params(("parallel",)))(dcur, dprev)


def _mem_fwd(proj, mkv, qg, kg, bd, *, tr):
    s = proj.shape[0]

    def body(qz_ref, mkv_ref, qg_ref, kg_ref, bd_ref, o_ref):
        o_ref[...] = _mem_fn(qz_ref[...], mkv_ref[...], qg_ref[...], kg_ref[...], bd_ref[...]).astype(BF16)

    return pl.pallas_call(
        body, grid=(s // tr,),
        in_specs=[_win(tr, 1024, QC), _const(mkv.shape), _const((1, 512)), _const((1, 512)), _const((512, 512))],
        out_specs=_rowblk(tr, 512), out_shape=SDS((s, 512), BF16), name="mem_fwd",
        compiler_params=_params(("parallel",)))(proj, mkv, qg, kg, bd)


def _mem_bwd(proj, mkv, qg, kg, bd, dgc, *, tr):
    s = proj.shape[0]

    def body(qz_ref, mkv_ref, qg_ref, kg_ref, bd_ref, dg_ref, dqz_ref, dmkv_ref, dqg_ref, dkg_ref):
        bd_v = bd_ref[...]
        fn = lambda qz, mkv_, qg_, kg_: _mem_fn(qz, mkv_, qg_, kg_, bd_v)
        _, vjp = jax.vjp(fn, qz_ref[...], mkv_ref[...], qg_ref[...], kg_ref[...])
        dqz, dmkv, dqg, dkg = vjp(dg_ref[...])

        @pl.when(pl.program_id(0) == 0)
        def _():
            dmkv_ref[...] = jnp.zeros_like(dmkv_ref)
            dqg_ref[...] = jnp.zeros_like(dqg_ref)
            dkg_ref[...] = jnp.zeros_like(dkg_ref)

        dmkv_ref[...] += dmkv
        dqg_ref[...] += dqg
        dkg_ref[...] += dkg
        dqz_ref[...] = dqz.astype(BF16)

    return pl.pallas_call(
        body, grid=(s // tr,),
        in_specs=[_win(tr, 1024, QC), _const(mkv.shape), _const((1, 512)), _const((1, 512)), _const((512, 512)),
                  _rowblk(tr, 512)],
        out_specs=[_rowblk(tr, 1024), _const(mkv.shape), _const((1, 512)), _const((1, 512))],
        out_shape=[SDS((s, 1024), BF16), SDS(mkv.shape, F32), SDS((1, 512), F32), SDS((1, 512), F32)],
        name="mem_bwd", compiler_params=_params(("arbitrary",)))(proj, mkv, qg, kg, bd, dgc)


def _log_sigmoid(x):
    return jnp.minimum(x, 0.0) - jnp.log1p(jnp.exp(-jnp.abs(x)))


def _fox_prep(proj, fbl, qg, kg, bfor, bd, *, tr):
    s = proj.shape[0]
    tri = jnp.asarray(np.tril(np.ones((tr, tr), np.float32)))

    def body(q_ref, k_ref, fb_ref, qg_ref, kg_ref, bf_ref, bd_ref, tri_ref, qn_ref, kn_ref, cq_ref, ck_ref, carry):
        @pl.when(pl.program_id(0) == 0)
        def _():
            carry[...] = jnp.zeros_like(carry)

        bd_v = bd_ref[...]
        qn_ref[...] = _qn_fn(q_ref[...], qg_ref[...], bd_v).astype(BF16)
        kn_ref[...] = _kn_fn(k_ref[...], kg_ref[...], bd_v).astype(BF16)
        lane = lax.broadcasted_iota(jnp.int32, (tr, LANE), 1)
        logf = jnp.where(lane < N_FORGET, _log_sigmoid(fb_ref[...] + bf_ref[...]), 0.0)
        c = jnp.dot(tri_ref[...], logf, precision=HI, preferred_element_type=F32) + carry[...]
        cq_ref[...] = c
        ck_ref[...] = jnp.transpose(c)
        carry[...] = c[tr - 1:tr, :]

    return pl.pallas_call(
        body, grid=(s // tr,),
        in_specs=[_win(tr, 768, QB), _win(tr, 768, KB), _rowblk(tr, LANE), _const((1, 768)), _const((1, 768)),
                  _const((1, LANE)), _const((768, 768)), _const((tr, tr))],
        out_specs=[_rowblk(tr, 768), _rowblk(tr, 768), _rowblk(tr, LANE), pl.BlockSpec((LANE, tr), lambda i: (0, i))],
        out_shape=[SDS((s, 768), BF16), SDS((s, 768), BF16), SDS((s, LANE), F32), SDS((LANE, s), F32)],
        scratch_shapes=[pltpu.VMEM((1, LANE), F32)], name="fox_prep",
        compiler_params=_params(("arbitrary",)))(proj, proj, fbl, qg, kg, bfor, bd, tri)


def _fox_tiles(s):
    tq = min(256, s)
    tk = min(512, s)
    return tq, tk


def _fox_fwd(proj, qn, kn, cq, ck):
    s = proj.shape[0]
    tq, tk = _fox_tiles(s)
    nq, nk = s // tq, s // tk

    def last_k(i):
        return (i * tq + tq - 1) // tk

    def body(q_ref, k_ref, v_ref, cq_ref, ck_ref, z_ref, gb_ref, yb_ref, lse_ref, acc, m_s, l_s):
        i, j = pl.program_id(0), pl.program_id(1)

        @pl.when(j == 0)
        def _():
            acc[...] = jnp.zeros_like(acc)
            m_s[...] = jnp.full_like(m_s, NEG)
            l_s[...] = jnp.ones_like(l_s)

        @pl.when(j <= last_k(i))
        def _():
            qpos = i * tq + lax.broadcasted_iota(jnp.int32, (tq, tk), 0)
            kpos = j * tk + lax.broadcasted_iota(jnp.int32, (tq, tk), 1)
            mask = kpos <= qpos
            q, k = q_ref[...], k_ref[...]
            v = v_ref[...].astype(BF16)
            cqv, ckv = cq_ref[...], ck_ref[...]
            m_all, l_all = m_s[...], l_s[...]
            lane = lax.broadcasted_iota(jnp.int32, (tq, LANE), 1)
            m_out, l_out = m_all, l_all
            for hp in range(B_HEADS // 2):
                acc_pair = acc[:, 128 * hp:128 * hp + 128]
                new = []
                for u in range(2):
                    h = 2 * hp + u
                    sl = slice(64 * h, 64 * h + 64)
                    sc = _dot_nt(q[:, sl], k[:, sl]) + cqv[:, h:h + 1] - ckv[h:h + 1, :]
                    sc = jnp.where(mask, sc, NEG)
                    m_prev = m_all[:, h:h + 1]
                    m_new = jnp.maximum(m_prev, jnp.max(sc, axis=-1, keepdims=True))
                    alpha = jnp.exp(m_prev - m_new)
                    p = jnp.exp(sc - m_new)
                    l_new = alpha * l_all[:, h:h + 1] + jnp.sum(p, axis=-1, keepdims=True)
                    new.append(alpha * acc_pair[:, 64 * u:64 * u + 64] + _dot(p.astype(BF16), v[:, sl]))
                    m_out = jnp.where(lane == h, m_new, m_out)
                    l_out = jnp.where(lane == h, l_new, l_out)
                acc[:, 128 * hp:128 * hp + 128] = jnp.concatenate(new, axis=1)
            m_s[...] = m_out
            l_s[...] = l_out

        @pl.when(j == nk - 1)
        def _():
            l_all = l_s[...]
            inv = 1.0 / l_all
            a = acc[...]
            y = jnp.concatenate([a[:, 64 * h:64 * h + 64] * inv[:, h:h + 1] for h in range(B_HEADS)], axis=1)
            yb_ref[...] = y
            gb_ref[...] = (y * _silu(z_ref[...])).astype(BF16)
            lse_ref[...] = m_s[...] + jnp.log(l_all)

    kmap = lambda i, j: (jnp.minimum(j, last_k(i)), 0)
    return pl.pallas_call(
        body, grid=(nq, nk),
        in_specs=[pl.BlockSpec((tq, 768), lambda i, j: (i, 0)),
                  pl.BlockSpec((tk, 768), kmap),
                  pl.BlockSpec((pl.Element(tk), pl.Element(768)), lambda i, j: (jnp.minimum(j, last_k(i)) * tk, VB)),
                  pl.BlockSpec((tq, LANE), lambda i, j: (i, 0)),
                  pl.BlockSpec((16, tk), lambda i, j: (0, jnp.minimum(j, last_k(i)))),
                  pl.BlockSpec((pl.Element(tq), pl.Element(768)), lambda i, j: (i * tq, ZB))],
        out_specs=[pl.BlockSpec((tq, 768), lambda i, j: (i, 0)), pl.BlockSpec((tq, 768), lambda i, j: (i, 0)),
                   pl.BlockSpec((tq, LANE), lambda i, j: (i, 0))],
        out_shape=[SDS((s, 768), BF16), SDS((s, 768), F32), SDS((s, LANE), F32)],
        scratch_shapes=[pltpu.VMEM((tq, 768), F32), pltpu.VMEM((tq, LANE), F32), pltpu.VMEM((tq, LANE), F32)],
        name="fox_fwd", compiler_params=_params(("parallel", "arbitrary")))(qn, kn, proj, cq, ck, proj)


def _fox_bwd_pre(proj, yb, dgb, hsum, *, tr):
    s = proj.shape[0]

    def body(z_ref, y_ref, dg_ref, hs_ref, dy_ref, dz_ref, dl_ref):
        z, y, dg = z_ref[...], y_ref[...], dg_ref[...]
        sg = jax.nn.sigmoid(z)
        dy = dg * (z * sg)
        dy_ref[...] = dy.astype(BF16)
        dz_ref[...] = (dg * y * (sg * (1.0 + z * (1.0 - sg)))).astype(BF16)
        dl_ref[...] = jnp.dot(dy * y, hs_ref[...], precision=HI, preferred_element_type=F32)

    return pl.pallas_call(
        body, grid=(s // tr,),
        in_specs=[_win(tr, 768, ZB), _rowblk(tr, 768), _rowblk(tr, 768), _const((768, LANE))],
        out_specs=[_rowblk(tr, 768), _rowblk(tr, 768), _rowblk(tr, LANE)],
        out_shape=[SDS((s, 768), BF16), SDS((s, 768), BF16), SDS((s, LANE), F32)], name="fox_bwd_pre",
        compiler_params=_params(("parallel",)))(proj, yb, dgb, hsum)


def _fox_bwd(proj, qn, kn, cq, ck, lse, delta, dyb):
    s = proj.shape[0]
    tq, tk = _fox_tiles(s)
    nq, nk = s // tq, s // tk

    def first_q(j):
        return (j * tk) // tq

    def body(q_ref, k_ref, v_ref, cq_ref, ck_ref, lse_ref, dl_ref, dy_ref,
             dq_ref, dcq_ref, dk_ref, dv_ref, dck_ref, dk_acc, dv_acc, dck_acc):
        j, i = pl.program_id(0), pl.program_id(1)

        @pl.when((j == 0) & (i == 0))
        def _():
            dq_ref[...] = jnp.zeros_like(dq_ref)
            dcq_ref[...] = jnp.zeros_like(dcq_ref)

        @pl.when(i == 0)
        def _():
            dk_acc[...] = jnp.zeros_like(dk_acc)
            dv_acc[...] = jnp.zeros_like(dv_acc)
            dck_acc[...] = jnp.zeros_like(dck_acc)

        @pl.when(i >= first_q(j))
        def _():
            qpos = i * tq + lax.broadcasted_iota(jnp.int32, (tq, tk), 0)
            kpos = j * tk + lax.broadcasted_iota(jnp.int32, (tq, tk), 1)
            mask = kpos <= qpos
            q, k = q_ref[...], k_ref[...]
            v = v_ref[...].astype(BF16)
            dy = dy_ref[...]
            cqv, ckv, lsev, dlv = cq_ref[...], ck_ref[...], lse_ref[...], dl_ref[...]
            lane = lax.broadcasted_iota(jnp.int32, (tq, LANE), 1)
            rows = pl.ds(pl.multiple_of(i * tq, tq), tq)
            dcq_t = jnp.zeros((tq, LANE), F32)
            for hp in range(B_HEADS // 2):
                dq_new, dk_new, dv_new = [], [], []
                for u in range(2):
                    h = 2 * hp + u
                    sl = slice(64 * h, 64 * h + 64)
                    sc = _dot_nt(q[:, sl], k[:, sl]) + cqv[:, h:h + 1] - ckv[h:h + 1, :]
                    sc = jnp.where(mask, sc, NEG)
                    p = jnp.exp(sc - lsev[:, h:h + 1])
                    dp = _dot_nt(dy[:, sl], v[:, sl])
                    ds = p * (dp - dlv[:, h:h + 1])
                    dsb = ds.astype(BF16)
                    dv_new.append(_dot_tn(p.astype(BF16), dy[:, sl]))
                    dk_new.append(_dot_tn(dsb, q[:, sl]))
                    dq_new.append(_dot(dsb, k[:, sl]))
                    dcq_t = jnp.where(lane == h, jnp.sum(ds, axis=-1, keepdims=True), dcq_t)
                    dck_acc[h:h + 1, :] -= jnp.sum(ds, axis=0, keepdims=True)
                cols = slice(128 * hp, 128 * hp + 128)
                dq_ref[rows, cols] += jnp.concatenate(dq_new, axis=1)
                dk_acc[:, cols] += jnp.concatenate(dk_new, axis=1)
                dv_acc[:, cols] += jnp.concatenate(dv_new, axis=1)
            dcq_ref[rows, :] += dcq_t

        @pl.when(i == nq - 1)
        def _():
            dk_ref[...] = dk_acc[...]
            dv_ref[...] = dv_acc[...].astype(BF16)
            dck_ref[...] = jnp.concatenate([dck_acc[...], jnp.zeros((LANE - 16, tk), F32)], axis=0)

    qmap = lambda j, i: (jnp.maximum(i, first_q(j)), 0)
    return pl.pallas_call(
        body, grid=(nk, nq),
        in_specs=[pl.BlockSpec((tq, 768), qmap),
                  pl.BlockSpec((tk, 768), lambda j, i: (j, 0)),
                  pl.BlockSpec((pl.Element(tk), pl.Element(768)), lambda j, i: (j * tk, VB)),
                  pl.BlockSpec((tq, LANE), qmap),
                  pl.BlockSpec((16, tk), lambda j, i: (0, j)),
                  pl.BlockSpec((tq, LANE), qmap),
                  pl.BlockSpec((tq, LANE), qmap),
                  pl.BlockSpec((tq, 768), qmap)],
        out_specs=[_const((s, 768)), _const((s, LANE)),
                   pl.BlockSpec((tk, 768), lambda j, i: (j, 0)), pl.BlockSpec((tk, 768), lambda j, i: (j, 0)),
                   pl.BlockSpec((LANE, tk), lambda j, i: (0, j))],
        out_shape=[SDS((s, 768), F32), SDS((s, LANE), F32), SDS((s, 768), F32), SDS((s, 768), BF16),
                   SDS((LANE, s), F32)],
        scratch_shapes=[pltpu.VMEM((tk, 768), F32), pltpu.VMEM((tk, 768), F32), pltpu.VMEM((16, tk), F32)],
        name="fox_bwd", compiler_params=_params(("arbitrary", "arbitrary")))(qn, kn, proj, cq, ck, lse, delta, dyb)


def _fox_bwd_post(proj, fbl, qg, kg, bfor, bd, dqn, dkn, dcq, dck, *, tr):
    s = proj.shape[0]
    nb = s // tr
    triu = jnp.asarray(np.triu(np.ones((tr, tr), np.float32)))
    rev = lambda i: nb - 1 - i

    def body(q_ref, k_ref, fb_ref, qg_ref, kg_ref, bf_ref, bd_ref, tri_ref, dqn_ref, dkn_ref, dcq_ref, dck_ref,
             dq_ref, dk_ref, dfb_ref, dqg_ref, dkg_ref, dbf_ref, carry):
        @pl.when(pl.program_id(0) == 0)
        def _():
            carry[...] = jnp.zeros_like(carry)
            dqg_ref[...] = jnp.zeros_like(dqg_ref)
            dkg_ref[...] = jnp.zeros_like(dkg_ref)
            dbf_ref[...] = jnp.zeros_like(dbf_ref)

        bd_v = bd_ref[...]
        _, vjp_q = jax.vjp(lambda q, g: _qn_fn(q, g, bd_v), q_ref[...], qg_ref[...])
        dq, dqg = vjp_q(dqn_ref[...])
        _, vjp_k = jax.vjp(lambda k, g: _kn_fn(k, g, bd_v), k_ref[...], kg_ref[...])
        dk, dkg = vjp_k(dkn_ref[...])
        dq_ref[...] = dq.astype(BF16)
        dk_ref[...] = dk.astype(BF16)
        dqg_ref[...] += dqg
        dkg_ref[...] += dkg

        dc = dcq_ref[...] + jnp.transpose(dck_ref[...])
        dlogf = jnp.dot(tri_ref[...], dc, precision=HI, preferred_element_type=F32) + carry[...]
        carry[...] = dlogf[0:1, :]
        lane = lax.broadcasted_iota(jnp.int32, (tr, LANE), 1)
        xf = fb_ref[...] + bf_ref[...]
        dfb = jnp.where(lane < N_FORGET, dlogf * jax.nn.sigmoid(-xf), 0.0)
        dfb_ref[...] = dfb.astype(BF16)
        dbf_ref[...] += jnp.sum(dfb, axis=0, keepdims=True)

    rb = lambda w: pl.BlockSpec((tr, w), lambda i: (rev(i), 0))
    wn = lambda w, off: pl.BlockSpec((pl.Element(tr), pl.Element(w)), lambda i: (rev(i) * tr, off))
    return pl.pallas_call(
        body, grid=(nb,),
        in_specs=[wn(768, QB), wn(768, KB), rb(LANE), _const((1, 768)), _const((1, 768)), _const((1, LANE)),
                  _const((768, 768)), _const((tr, tr)), rb(768), rb(768), rb(LANE),
                  pl.BlockSpec((LANE, tr), lambda i: (0, rev(i)))],
        out_specs=[rb(768), rb(768), rb(LANE), _const((1, 768)), _const((1, 768)), _const((1, LANE))],
        out_shape=[SDS((s, 768), BF16), SDS((s, 768), BF16), SDS((s, LANE), BF16), SDS((1, 768), F32),
                   SDS((1, 768), F32), SDS((1, LANE), F32)],
        scratch_shapes=[pltpu.VMEM((1, LANE), F32)], name="fox_bwd_post",
        compiler_params=_params(("arbitrary",)))(proj, proj, fbl, qg, kg, bfor, bd, triu, dqn, dkn, dcq, dck)


def _merge_specs(tr):
    row = lambda w: pl.BlockSpec((tr, w), lambda i, j: (i, 0))
    shard = lambda r: pl.BlockSpec((None, r, 512), lambda i, j: (j, 0, 0))
    gate = lambda b: pl.BlockSpec((tr, 512), lambda i, j: (i, (GATE + 2048 * b) // 512 + j))
    return [row(768), row(768), row(512), shard(768), shard(768), shard(512), gate(0), gate(1), gate(2)]


def _merge_fwd(proj, ga, gb, gc, wa, wb, wc, *, tr):
    s = proj.shape[0]

    def body(ga_ref, gb_ref, gc_ref, wa_ref, wb_ref, wc_ref, l0_ref, l1_ref, l2_ref, y_ref):
        ua = _dot(ga_ref[...], wa_ref[...])
        ub = _dot(gb_ref[...], wb_ref[...])
        uc = _dot(gc_ref[...], wc_ref[...])
        y = jax.nn.sigmoid(l0_ref[...]) * ua + jax.nn.sigmoid(l1_ref[...]) * ub + jax.nn.sigmoid(l2_ref[...]) * uc
        y_ref[...] = y.astype(BF16)

    return pl.pallas_call(
        body, grid=(s // tr, N_CHIPS), in_specs=_merge_specs(tr),
        out_specs=pl.BlockSpec((tr, 512), lambda i, j: (i, j)), out_shape=SDS((s, D_MODEL), BF16), name="merge_fwd",
        compiler_params=_params(("parallel", "arbitrary")))(ga, gb, gc, wa, wb, wc, proj, proj, proj)


def _merge_bwd(proj, ga, gb, gc, wa, wb, wc, dy, *, tr):
    s = proj.shape[0]

    def body(ga_ref, gb_ref, gc_ref, wa_ref, wb_ref, wc_ref, l0_ref, l1_ref, l2_ref, dy_ref,
             dl0_ref, dl1_ref, dl2_ref, dua_ref, dub_ref, duc_ref, dga_ref, dgb_ref, dgc_ref):
        j = pl.program_id(1)
        dyv = dy_ref[...]

        @pl.when(j == 0)
        def _():
            dga_ref[...] = jnp.zeros_like(dga_ref)
            dgb_ref[...] = jnp.zeros_like(dgb_ref)
            dgc_ref[...] = jnp.zeros_like(dgc_ref)

        for g_ref, w_ref, l_ref, dl_ref, du_ref, dg_ref in (
                (ga_ref, wa_ref, l0_ref, dl0_ref, dua_ref, dga_ref),
                (gb_ref, wb_ref, l1_ref, dl1_ref, dub_ref, dgb_ref),
                (gc_ref, wc_ref, l2_ref, dl2_ref, duc_ref, dgc_ref)):
            w = w_ref[...]
            u = _dot(g_ref[...], w)
            sg = jax.nn.sigmoid(l_ref[...])
            dl_ref[...] = (dyv * u * sg * (1.0 - sg)).astype(BF16)
            du = (dyv * sg).astype(BF16)
            du_ref[...] = du
            dg_ref[...] += _dot_nt(du, w)

    blk = pl.BlockSpec((tr, 512), lambda i, j: (i, j))
    row = lambda w: pl.BlockSpec((tr, w), lambda i, j: (i, 0))
    big = SDS((s, D_MODEL), BF16)
    return pl.pallas_call(
        body, grid=(s // tr, N_CHIPS), in_specs=_merge_specs(tr) + [blk],
        out_specs=[blk] * 6 + [row(768), row(768), row(512)],
        out_shape=[big] * 6 + [SDS((s, 768), F32), SDS((s, 768), F32), SDS((s, 512), F32)], name="merge_bwd",
        compiler_params=_params(("parallel", "arbitrary")))(ga, gb, gc, wa, wb, wc, proj, proj, proj, dy)


def _out_loss(y, wo, x, tgt, *, tr, tn):
    s = x.shape[0]

    def body(y_ref, w_ref, x_ref, t_ref, d_ref, db_ref, sq_ref):
        @pl.when((pl.program_id(0) == 0) & (pl.program_id(1) == 0))
        def _():
            sq_ref[...] = jnp.zeros_like(sq_ref)

        out = x_ref[...] + _dot(y_ref[...], w_ref[...])
        diff = out - t_ref[...]
        sq_ref[...] += jnp.sum(diff * diff, axis=0, keepdims=True)
        d = diff * (1.0 / D_MODEL)
        d_ref[...] = d
        db_ref[...] = d.astype(BF16)

    blk = pl.BlockSpec((tr, tn), lambda i, j: (i, j))
    return pl.pallas_call(
        body, grid=(s // tr, D_MODEL // tn),
        in_specs=[pl.BlockSpec((tr, D_MODEL), lambda i, j: (i, 0)), pl.BlockSpec((D_MODEL, tn), lambda i, j: (0, j)), blk, blk],
        out_specs=[blk, blk, _const((1, tn))],
        out_shape=[SDS((s, D_MODEL), F32), SDS((s, D_MODEL), BF16), SDS((1, tn), F32)], name="out_loss",
        compiler_params=_params(("arbitrary", "arbitrary")))(y, wo, x, tgt)


def _tile_gain(g, reps):
    return jnp.tile(g.reshape(1, -1), (1, reps))


def _pad_lane(v):
    v = v.reshape(1, -1)
    return jnp.pad(v, ((0, 0), (0, LANE - v.shape[1])))


def _local_step(x, mem, tgt, w_main, w_fb, w_mk, wa, wb, wc, wo, norm_gain, mem_norm_gain, b_forget,
                q_gain_a, k_gain_a, sinks_a, q_gain_b, k_gain_b, q_gain_c, k_gain_c):
    s = x.shape[0]
    tr = min(512, s)
    bd64 = _block_diag(768, HEAD_DIM)
    bd128 = _block_diag(512, C_HEAD_DIM)
    hsum = _head_sum(768, HEAD_DIM)
    qga, kga = _tile_gain(q_gain_a, 12), _tile_gain(k_gain_a, 4)
    qgb, kgb = _tile_gain(q_gain_b, 12), _tile_gain(k_gain_b, 12)
    qgc, kgc = _tile_gain(q_gain_c, 4), _tile_gain(k_gain_c, 4)
    sinks = _pad_lane(sinks_a)
    bfor = _pad_lane(b_forget)

    hn = _rms_fwd(x, norm_gain, tr=tr, name="rms_x")
    proj = _matmul(hn, w_main, dims="nn", out_dtype=F32, tm=1024, tn=512, tk=D_MODEL, name="proj_main")
    fbl = _matmul(hn, w_fb, dims="nn", out_dtype=F32, tm=1024, tn=LANE, tk=D_MODEL, name="proj_forget")
    memn = _rms_fwd(mem, mem_norm_gain, tr=mem.shape[0], name="rms_mem")
    mkv = _matmul(memn, w_mk, dims="nn", out_dtype=F32, tm=256, tn=512, tk=D_MODEL, name="mem_kv")

    ga = _swa_fwd(proj, qga, kga, sinks, bd64)
    qn, kn, cq, ck = _fox_prep(proj, fbl, qgb, kgb, bfor, bd64, tr=tr)
    gb, yb, lse = _fox_fwd(proj, qn, kn, cq, ck)
    gc = _mem_fwd(proj, mkv, qgc, kgc, bd128, tr=tr)
    y = _merge_fwd(proj, ga, gb, gc, wa, wb, wc, tr=tr)
    dout, dout_b, sq = _out_loss(y, wo, x, tgt, tr=tr, tn=512)

    d_wo = _matmul(y, dout_b, dims="tn", out_dtype=F32, tm=1024, tn=512, tk=512, name="dw_out")
    dy = _matmul(dout_b, wo, dims="nt", out_dtype=F32, tm=1024, tn=512, tk=D_MODEL, name="dy")
    dl0, dl1, dl2, dua, dub, duc, dga, dgb, dgc = _merge_bwd(proj, ga, gb, gc, wa, wb, wc, dy, tr=tr)
    d_wa = _matmul(ga, dua, dims="tn", out_dtype=F32, tm=768, tn=512, tk=512, name="dw_branch_a")
    d_wb = _matmul(gb, dub, dims="tn", out_dtype=F32, tm=768, tn=512, tk=512, name="dw_branch_b")
    d_wc = _matmul(gc, duc, dims="tn", out_dtype=F32, tm=512, tn=512, tk=512, name="dw_branch_c")

    dcur, dprev, d_qga, d_kga, d_sinks = _swa_bwd(proj, qga, kga, sinks, bd64, dga)
    dproj_a = _swa_combine(dcur, dprev)

    dyb, dzb, delta = _fox_bwd_pre(proj, yb, dgb, hsum, tr=tr)
    dqn, dcq, dkn, dvb, dck = _fox_bwd(proj, qn, kn, cq, ck, lse, delta, dyb)
    dqb, dkb, dfb, d_qgb, d_kgb, d_bf = _fox_bwd_post(proj, fbl, qgb, kgb, bfor, bd64, dqn, dkn, dcq, dck, tr=tr)

    dproj_c, dmkv, d_qgc, d_kgc = _mem_bwd(proj, mkv, qgc, kgc, bd128, dgc, tr=tr)
    dmkv_b = dmkv.astype(BF16)
    d_wmk = _matmul(memn, dmkv_b, dims="tn", out_dtype=F32, tm=1024, tn=512, tk=256, name="dw_mem_kv")
    dmemn = _matmul(dmkv_b, w_mk, dims="nt", out_dtype=F32, tm=256, tn=512, tk=1024, name="dmemn")
    (d_mem_gain,) = _rms_bwd(mem, mem_norm_gain, dmemn, None, tr=mem.shape[0], name="rms_mem_bwd")

    dproj = jnp.concatenate([dproj_a, dqb, dkb, dvb, dzb, dproj_c, dl0, dl1, dl2], axis=1)
    dhn_f = _matmul(dfb, w_fb, dims="nt", out_dtype=F32, tm=1024, tn=512, tk=LANE, name="dhn_forget")
    dhn = _matmul(dproj, w_main, dims="nt", out_dtype=F32, tm=1024, tn=512, tk=512, name="dhn", add=dhn_f)
    d_wmain = _matmul(hn, dproj, dims="tn", out_dtype=F32, tm=1024, tn=512, tk=512, name="dw_main")
    d_wfb = _matmul(hn, dfb, dims="tn", out_dtype=F32, tm=1024, tn=LANE, tk=512, name="dw_forget")
    grad_x, d_gain = _rms_bwd(x, norm_gain, dhn, dout, tr=tr, name="rms_x_bwd")

    fold = lambda g, reps: jnp.sum(g.reshape(reps, -1), axis=0, keepdims=True)
    return dict(
        sq=sq, grad_x=grad_x, d_wmain=d_wmain, d_wfb=d_wfb, d_wmk=d_wmk, d_wa=d_wa, d_wb=d_wb, d_wc=d_wc, d_wo=d_wo,
        d_gain=d_gain, d_mem_gain=d_mem_gain, d_bf=d_bf[:, :N_FORGET],
        d_qga=fold(d_qga, 12), d_kga=fold(d_kga, 4), d_sinks=d_sinks[:, :A_HEADS],
        d_qgb=fold(d_qgb, 12), d_kgb=fold(d_kgb, 12), d_qgc=fold(d_qgc, 4), d_kgc=fold(d_kgc, 4))


PACK_ROWS = 256
FORGET_IN_SHARD = FORGET_COL - SHARD_COLS
AFTER_FORGET = FORGET_COL - SLAB_START[1]
END_CHIP1 = 2 * SHARD_COLS - N_FORGET - SLAB_START[1]


def _pack_w_in(chip, w):
    rows = w.shape[0]
    tr = PACK_ROWS

    def body(k_ref, w_ref, o_ref, scr):
        scr[...] = jnp.zeros_like(scr)
        scr[:, pl.ds(0, SHARD_COLS)] = w_ref[...]
        v = scr[...]
        k = k_ref[0]
        col = lax.broadcasted_iota(jnp.int32, (tr, SLAB), 1)
        no_forget = jnp.zeros((tr, LANE), BF16)

        @pl.when(k == 0)
        def _():
            o_ref[:, 0:SLAB] = v.astype(BF16)
            o_ref[:, SLAB:] = no_forget

        @pl.when(k == 1)
        def _():
            before = pltpu.roll(v, SLAB_SHIFT[1], axis=1)
            after = pltpu.roll(v, SLAB - (N_FORGET - SLAB_SHIFT[1]), axis=1)
            slab = jnp.where(col < AFTER_FORGET, before, jnp.where(col < END_CHIP1, after, 0.0))
            o_ref[:, 0:SLAB] = slab.astype(BF16)
            f = pltpu.roll(v, SLAB - FORGET_IN_SHARD, axis=1)[:, :LANE]
            o_ref[:, SLAB:] = jnp.where(col[:, :LANE] < N_FORGET, f, 0.0).astype(BF16)

        for kk in (2, 3):
            @pl.when(k == kk)
            def _(kk=kk):
                o_ref[:, 0:SLAB] = pltpu.roll(v, SLAB_SHIFT[kk], axis=1).astype(BF16)
                o_ref[:, SLAB:] = no_forget

    return pl.pallas_call(
        body, grid_spec=pltpu.PrefetchScalarGridSpec(
            num_scalar_prefetch=1, grid=(rows // tr,),
            in_specs=[pl.BlockSpec((tr, SHARD_COLS), lambda i, k: (i, 0))],
            out_specs=pl.BlockSpec((tr, SLAB + LANE), lambda i, k: (i, 0)),
            scratch_shapes=[pltpu.VMEM((tr, SLAB), F32)]),
        out_shape=SDS((rows, SLAB + LANE), BF16), name="pack_w_in",
        compiler_params=_params(("arbitrary",)))(chip, w)


def _merge_slabs(g):
    rows = g.shape[1]
    tr = PACK_ROWS
    t = [s // LANE for s in SLAB_START]
    n_t = SLAB // LANE

    def body(g_ref, m_ref, f_ref):
        for k in range(N_CHIPS):
            lo = t[k] + (1 if k > 0 else 0)
            hi = t[k + 1] if k + 1 < N_CHIPS else t[k] + n_t
            m_ref[:, lo * LANE:hi * LANE] = g_ref[k, :, (lo - t[k]) * LANE:(hi - t[k]) * LANE]
            if k + 1 < N_CHIPS:
                a = g_ref[k, :, (hi - t[k]) * LANE:(hi - t[k] + 1) * LANE].astype(F32)
                b = g_ref[k + 1, :, 0:LANE].astype(F32)
                m_ref[:, hi * LANE:(hi + 1) * LANE] = (a + b).astype(BF16)
        f_ref[...] = g_ref[1, :, SLAB:]

    return pl.pallas_call(
        body, grid=(rows // tr,),
        in_specs=[pl.BlockSpec((N_CHIPS, tr, SLAB + LANE), lambda i: (0, i, 0))],
        out_specs=[_rowblk(tr, P_MAIN), _rowblk(tr, LANE)],
        out_shape=[SDS((rows, P_MAIN), BF16), SDS((rows, LANE), BF16)], name="merge_slabs",
        compiler_params=_params(("parallel",)))(g)


def _adamw_math(w, g, m, v):
    nm = ADAM_B1 * m + (1.0 - ADAM_B1) * g
    nv = ADAM_B2 * v + (1.0 - ADAM_B2) * (g * g)
    m_hat = nm / (1.0 - ADAM_B1 ** ADAM_STEP)
    v_hat = nv / (1.0 - ADAM_B2 ** ADAM_STEP)
    delta = -ADAM_LR * (m_hat / (jnp.sqrt(v_hat) + ADAM_EPS) + ADAM_WD * w)
    return delta, nm, nv


def _adamw(g, w, m, v, *, tr, name):
    rows, cols = w.shape
    tr = min(tr, rows)

    def body(g_ref, w_ref, m_ref, v_ref, d_ref, nm_ref, nv_ref):
        d, nm, nv = _adamw_math(w_ref[...], g_ref[...], m_ref[...], v_ref[...])
        d_ref[...] = d
        nm_ref[...] = nm
        nv_ref[...] = nv

    spec = _rowblk(tr, cols)
    return pl.pallas_call(
        body, grid=(rows // tr,), in_specs=[spec] * 4, out_specs=[spec] * 3,
        out_shape=[SDS((rows, cols), F32)] * 3, name=name, compiler_params=_params(("parallel",)))(g, w, m, v)


def _adamw_w_in(chip, g_slab, g_forget, w, m, v):
    rows = w.shape[0]
    tr = PACK_ROWS

    def body(k_ref, s_ref, f_ref, w_ref, m_ref, v_ref, g_ref, d_ref, nm_ref, nv_ref):
        sl = s_ref[...]
        k = k_ref[0]

        def emit(wide):
            g = wide[:, :SHARD_COLS]
            g_ref[...] = g
            d, nm, nv = _adamw_math(w_ref[...], g, m_ref[...], v_ref[...])
            d_ref[...] = d
            nm_ref[...] = nm
            nv_ref[...] = nv

        @pl.when(k == 0)
        def _():
            emit(sl)

        @pl.when(k == 1)
        def _():
            col = lax.broadcasted_iota(jnp.int32, (tr, SLAB), 1)
            before = pltpu.roll(sl, SLAB - SLAB_SHIFT[1], axis=1)
            after = pltpu.roll(sl, N_FORGET - SLAB_SHIFT[1], axis=1)
            wide_f = jnp.concatenate([f_ref[...], jnp.zeros((tr, SLAB - LANE), F32)], axis=1)
            forget = pltpu.roll(wide_f, FORGET_IN_SHARD, axis=1)
            emit(jnp.where(col < FORGET_IN_SHARD, before, jnp.where(col < FORGET_IN_SHARD + N_FORGET, forget, after)))

        for kk in (2, 3):
            @pl.when(k == kk)
            def _(kk=kk):
                emit(pltpu.roll(sl, SLAB - SLAB_SHIFT[kk], axis=1))

    nat = pl.BlockSpec((tr, SHARD_COLS), lambda i, k: (i, 0))
    return pl.pallas_call(
        body, grid_spec=pltpu.PrefetchScalarGridSpec(
            num_scalar_prefetch=1, grid=(rows // tr,),
            in_specs=[pl.BlockSpec((tr, SLAB), lambda i, k: (i, 0)), pl.BlockSpec((tr, LANE), lambda i, k: (i, 0)),
                      nat, nat, nat],
            out_specs=[nat] * 4),
        out_shape=[SDS((rows, SHARD_COLS), F32)] * 4, name="adamw_w_in",
        compiler_params=_params(("arbitrary",)))(chip, g_slab, g_forget, w, m, v)


ANY = pl.BlockSpec(memory_space=pl.ANY)
HALF_AXIS = (0, 0, 1, 0, 0, 0, 1)


def _me():
    return lax.axis_index("x"), lax.axis_index("y"), lax.axis_index("c")


def _half(ref, which, axis):
    n = ref.shape[axis] // 2
    sl = pl.ds(which * n, n)
    return ref.at[sl] if axis == 0 else ref.at[:, sl]


def _piece(t, ref, j):
    if t == 0:
        return ref.at[:, pl.ds(SLAB_START[j], SLAB)]
    if t == 1:
        return ref
    if t in (2, 6):
        return ref.at[pl.ds(512 * j, 512)]
    return ref.at[:, pl.ds(512 * j, 512)]


def _piece_shape(t, shape):
    if t == 0:
        return (shape[0], SLAB)
    if t == 1:
        return shape
    if t in (2, 6):
        return (512, shape[1])
    return (shape[0], 512)


def _all_gather(parts):
    n = len(parts)

    def body(*refs):
        ins, outs = refs[:n], refs[n:2 * n]
        send, recv, fsend, frecv, lsem = refs[2 * n:]
        x, y, c = _me()
        k = 2 * x + y
        sib = (x, y, 1 - c)
        chips = [(1 - x, y), (x, 1 - y), (1 - x, 1 - y)]

        def rows(t, which):
            h = ins[t].shape[0] // 2
            return pl.ds(which * h, h)

        local = [pltpu.make_async_copy(ins[t], outs[t].at[k], lsem.at[t]) for t in range(n)]
        for cp in local:
            cp.start()

        def first(t, j, chip):
            return pltpu.make_async_remote_copy(
                src_ref=ins[t].at[rows(t, c)], dst_ref=outs[t].at[k, rows(t, c)],
                send_sem=send.at[t, j], recv_sem=recv.at[t, j], device_id=(chip[0], chip[1], c), device_id_type=MESH)

        def passed(t, j, chip, which):
            kj = 2 * chip[0] + chip[1]
            blk = outs[t].at[kj, rows(t, which)]
            return pltpu.make_async_remote_copy(
                src_ref=blk, dst_ref=blk, send_sem=fsend.at[t, j], recv_sem=frecv.at[t, j],
                device_id=sib, device_id_type=MESH)

        firsts = [first(t, j, chip) for j, chip in enumerate(chips) for t in range(n)]
        for cp in firsts:
            cp.start()
        passes = []
        for j, chip in enumerate(chips):
            for t in range(n):
                kj = 2 * chip[0] + chip[1]
                pltpu.make_async_remote_copy(
                    src_ref=ins[t].at[rows(t, c)], dst_ref=outs[t].at[kj, rows(t, c)],
                    send_sem=send.at[t, j], recv_sem=recv.at[t, j], device_id=(chip[0], chip[1], c),
                    device_id_type=MESH).wait_recv()
                cp = passed(t, j, chip, c)
                cp.start()
                passes.append(cp)
        for j, chip in enumerate(chips):
            for t in range(n):
                passed(t, j, chip, 1 - c).wait_recv()
        for cp in firsts + passes:
            cp.wait_send()
        for cp in local:
            cp.wait()

    return pl.pallas_call(
        body, in_specs=[ANY] * n, out_specs=[ANY] * n,
        out_shape=[SDS((N_CHIPS,) + p.shape, p.dtype) for p in parts],
        scratch_shapes=[pltpu.SemaphoreType.DMA((n, 3))] * 4 + [pltpu.SemaphoreType.DMA((n,))],
        name="all_gather_weights")(*parts)


def _exchange_halves(arrs):
    n = len(arrs)

    def body(*refs):
        ins, outs = refs[:n], refs[n:2 * n]
        send, recv = refs[2 * n:]
        x, y, c = _me()
        cps = [pltpu.make_async_remote_copy(
            src_ref=_half(ins[t], 1 - c, HALF_AXIS[t]), dst_ref=outs[t], send_sem=send.at[t], recv_sem=recv.at[t],
            device_id=(x, y, 1 - c), device_id_type=MESH) for t in range(n)]
        for cp in cps:
            cp.start()
        for cp in cps:
            cp.wait()

    def hshape(t):
        s = list(arrs[t].shape)
        s[HALF_AXIS[t]] //= 2
        return tuple(s)

    return pl.pallas_call(
        body, in_specs=[ANY] * n, out_specs=[ANY] * n,
        out_shape=[SDS(hshape(t), arrs[t].dtype) for t in range(n)],
        scratch_shapes=[pltpu.SemaphoreType.DMA((n,))] * 2, name="exchange_halves")(*arrs)


def _add_half(full, got, core, axis, *, name):
    r, c = got.shape
    br, bc = (256 if r % 256 == 0 else 128), min(2048, c)
    off_r = (r // br) if axis == 0 else 0
    off_c = (c // bc) if axis == 1 else 0

    def body(c_ref, a_ref, b_ref, o_ref):
        o_ref[...] = a_ref[...] + b_ref[...]

    return pl.pallas_call(
        body, grid_spec=pltpu.PrefetchScalarGridSpec(
            num_scalar_prefetch=1, grid=(r // br, c // bc),
            in_specs=[pl.BlockSpec((br, bc), lambda i, j, cr: (i + cr[0] * off_r, j + cr[0] * off_c)),
                      pl.BlockSpec((br, bc), lambda i, j, cr: (i, j))],
            out_specs=pl.BlockSpec((br, bc), lambda i, j, cr: (i, j))),
        out_shape=SDS((r, c), F32), name=name, compiler_params=_params(("parallel", "parallel")))(core, full, got)


def _scatter_pieces(halves):
    n = len(halves)

    def body(*refs):
        ins, outs = refs[:n], refs[n:2 * n]
        send, recv, lsem = refs[2 * n:]
        x, y, c = _me()
        k = 2 * x + y

        def to_chip(t, j):
            return pltpu.make_async_remote_copy(
                src_ref=_piece(t, ins[t], j), dst_ref=outs[t].at[k], send_sem=send.at[t, j], recv_sem=recv.at[t, k],
                device_id=(j // 2, j % 2, c), device_id_type=MESH)

        def from_chip(t, j):
            return pltpu.make_async_remote_copy(
                src_ref=_piece(t, ins[t], j), dst_ref=outs[t].at[j], send_sem=send.at[t, j], recv_sem=recv.at[t, j],
                device_id=(j // 2, j % 2, c), device_id_type=MESH)

        def own(t, j):
            return pltpu.make_async_copy(_piece(t, ins[t], j), outs[t].at[j], lsem.at[t])

        for j in range(N_CHIPS):
            @pl.when(k != j)
            def _(j=j):
                for t in range(n):
                    to_chip(t, j).start()

            @pl.when(k == j)
            def _(j=j):
                for t in range(n):
                    own(t, j).start()

        for j in range(N_CHIPS):
            @pl.when(k != j)
            def _(j=j):
                for t in range(n):
                    from_chip(t, j).wait_recv()
                for t in range(n):
                    to_chip(t, j).wait_send()

            @pl.when(k == j)
            def _(j=j):
                for t in range(n):
                    own(t, j).wait()

    return pl.pallas_call(
        body, in_specs=[ANY] * n, out_specs=[ANY] * n,
        out_shape=[SDS((N_CHIPS,) + _piece_shape(t, halves[t].shape), F32) for t in range(n)],
        scratch_shapes=[pltpu.SemaphoreType.DMA((n, N_CHIPS))] * 2 + [pltpu.SemaphoreType.DMA((n,))],
        name="scatter_pieces")(*halves)


def _sum4(p, *, name):
    _, r, c = p.shape
    br = 256 if r % 256 == 0 else 128

    def body(p_ref, o_ref):
        o_ref[...] = ((p_ref[0] + p_ref[1]) + p_ref[2]) + p_ref[3]

    return pl.pallas_call(
        body, grid=(r // br,), in_specs=[pl.BlockSpec((N_CHIPS, br, c), lambda i: (0, i, 0))],
        out_specs=_rowblk(br, c), out_shape=SDS((r, c), F32), name=name, compiler_params=_params(("parallel",)))(p)


def _join_halves(sums):
    n = len(sums)

    def full_shape(t):
        s = list(sums[t].shape)
        s[HALF_AXIS[t]] *= 2
        return tuple(s)

    def body(*refs):
        ins, outs = refs[:n], refs[n:2 * n]
        send, recv, lsem = refs[2 * n:]
        x, y, c = _me()
        local = [pltpu.make_async_copy(ins[t], _half(outs[t], c, HALF_AXIS[t]), lsem.at[t]) for t in range(n)]
        cps = [pltpu.make_async_remote_copy(
            src_ref=ins[t], dst_ref=_half(outs[t], c, HALF_AXIS[t]), send_sem=send.at[t], recv_sem=recv.at[t],
            device_id=(x, y, 1 - c), device_id_type=MESH) for t in range(n)]
        for cp in local + cps:
            cp.start()
        for t in range(n):
            cps[t].wait_send()
            pltpu.make_async_remote_copy(
                src_ref=ins[t], dst_ref=_half(outs[t], 1 - c, HALF_AXIS[t]), send_sem=send.at[t], recv_sem=recv.at[t],
                device_id=(x, y, 1 - c), device_id_type=MESH).wait_recv()
        for cp in local:
            cp.wait()

    return pl.pallas_call(
        body, in_specs=[ANY] * n, out_specs=[ANY] * n,
        out_shape=[SDS(full_shape(t), F32) for t in range(n)],
        scratch_shapes=[pltpu.SemaphoreType.DMA((n,))] * 3, name="join_halves")(*sums)


SMALL_ROWS, SMALL_COLS = 8, 1024


def _pack_small(vs):
    flat = jnp.concatenate([v.reshape(-1) for v in vs])
    return jnp.pad(flat, (0, SMALL_ROWS * SMALL_COLS - flat.shape[0])).reshape(SMALL_ROWS, SMALL_COLS)


def _unpack_small(packed, sizes):
    flat = packed.reshape(-1)
    out, o = [], 0
    for n in sizes:
        out.append(flat[o:o + n].reshape(1, n))
        o += n
    return out


def _all_reduce_small(v):
    n_dev = 8

    def body(v_ref, o_ref, land, send, recv):
        x, y, c = _me()
        me = 4 * x + 2 * y + c
        land[me] = v_ref[...]
        cps = []
        for r in range(1, n_dev):
            fx, fy, fc = (r >> 2) & 1, (r >> 1) & 1, r & 1
            peer = (x ^ fx, y ^ fy, c ^ fc)
            cps.append(pltpu.make_async_remote_copy(
                src_ref=v_ref, dst_ref=land.at[me], send_sem=send.at[r - 1], recv_sem=recv.at[r - 1],
                device_id=peer, device_id_type=MESH))
        for cp in cps:
            cp.start()
        for r in range(1, n_dev):
            fx, fy, fc = (r >> 2) & 1, (r >> 1) & 1, r & 1
            src = 4 * (x ^ fx) + 2 * (y ^ fy) + (c ^ fc)
            pltpu.make_async_remote_copy(
                src_ref=v_ref, dst_ref=land.at[src], send_sem=send.at[r - 1], recv_sem=recv.at[r - 1],
                device_id=(x ^ fx, y ^ fy, c ^ fc), device_id_type=MESH).wait_recv()
        for cp in cps:
            cp.wait_send()
        acc = land[0]
        for r in range(1, n_dev):
            acc = acc + land[r]
        o_ref[...] = acc

    vm = pl.BlockSpec(memory_space=pltpu.VMEM)
    return pl.pallas_call(
        body, in_specs=[vm], out_specs=vm, out_shape=SDS(v.shape, F32),
        scratch_shapes=[pltpu.VMEM((n_dev,) + v.shape, F32), pltpu.SemaphoreType.DMA((n_dev - 1,)),
                        pltpu.SemaphoreType.DMA((n_dev - 1,))],
        name="all_reduce_small")(v)


def kernel(x, mem, norm_gain, mem_norm_gain, w_in, b_forget, q_gain_a, k_gain_a, sinks_a, q_gain_b, k_gain_b, q_gain_c, k_gain_c, w_mem_kv, w_branch_a, w_branch_b, w_branch_c, w_out, loss_target, m_norm_gain, m_mem_norm_gain, m_w_in, m_b_forget, m_q_gain_a, m_k_gain_a, m_sinks_a, m_q_gain_b, m_k_gain_b, m_q_gain_c, m_k_gain_c, m_w_mem_kv, m_w_branch_a, m_w_branch_b, m_w_branch_c, m_w_out, v_norm_gain, v_mem_norm_gain, v_w_in, v_b_forget, v_q_gain_a, v_k_gain_a, v_sinks_a, v_q_gain_b, v_k_gain_b, v_q_gain_c, v_k_gain_c, v_w_mem_kv, v_w_branch_a, v_w_branch_b, v_w_branch_c, v_w_out):
    xi, yi, ci = lax.axis_index("x"), lax.axis_index("y"), lax.axis_index("c")
    chip = jnp.reshape(2 * xi + yi, (1,)).astype(jnp.int32)
    core = jnp.reshape(ci, (1,)).astype(jnp.int32)

    slab = _pack_w_in(chip, w_in[0])
    mine = [slab, w_mem_kv[0].astype(BF16), w_branch_a[0].astype(BF16), w_branch_b[0].astype(BF16),
            w_branch_c[0].astype(BF16), w_out[0].astype(BF16)]
    g_slab, g_mk, g_wa, g_wb, g_wc, g_wo = _all_gather(mine)
    w_main, w_fb = _merge_slabs(g_slab)
    w_mk = g_mk.reshape(D_MODEL, 1024)
    wo = g_wo.reshape(D_MODEL, D_MODEL)

    r = _local_step(x[0], mem[0], loss_target[0], w_main, w_fb, w_mk, g_wa, g_wb, g_wc, wo, norm_gain, mem_norm_gain,
                    b_forget, q_gain_a, k_gain_a, sinks_a, q_gain_b, k_gain_b, q_gain_c, k_gain_c)

    grads = [r["d_wmain"], r["d_wfb"], r["d_wmk"], r["d_wa"], r["d_wb"], r["d_wc"], r["d_wo"]]
    got = _exchange_halves(grads)
    halves = [_add_half(g, h, core, ax, name=f"add_half_{t}") for t, (g, h, ax) in enumerate(zip(grads, got, HALF_AXIS))]
    parts = _scatter_pieces(halves)
    sums = [_sum4(p, name=f"sum4_{t}") for t, p in enumerate(parts)]
    r_main, r_fb, r_mk, r_wa, r_wb, r_wc, r_wo = _join_halves(sums)

    small_names = ["d_gain", "d_mem_gain", "d_bf", "d_qga", "d_kga", "d_sinks", "d_qgb", "d_kgb", "d_qgc", "d_kgc"]
    loss_part = (0.5 / D_MODEL) * jnp.sum(r["sq"], axis=1, keepdims=True)
    packed = _pack_small([r[n] for n in small_names] + [loss_part])
    red = _all_reduce_small(packed)
    small_w = [norm_gain, mem_norm_gain, b_forget, q_gain_a, k_gain_a, sinks_a, q_gain_b, k_gain_b, q_gain_c, k_gain_c]
    small_m = [m_norm_gain, m_mem_norm_gain, m_b_forget, m_q_gain_a, m_k_gain_a, m_sinks_a, m_q_gain_b, m_k_gain_b,
               m_q_gain_c, m_k_gain_c]
    small_v = [v_norm_gain, v_mem_norm_gain, v_b_forget, v_q_gain_a, v_k_gain_a, v_sinks_a, v_q_gain_b, v_k_gain_b,
               v_q_gain_c, v_k_gain_c]
    sizes = [w.shape[1] for w in small_w]
    s_d, s_m, s_v = _adamw(red, _pack_small(small_w), _pack_small(small_m), _pack_small(small_v), tr=8, name="adamw_small")
    g_small = _unpack_small(red, sizes + [1])
    loss = g_small[-1].reshape(())
    d_small, m_small, v_small = _unpack_small(s_d, sizes), _unpack_small(s_m, sizes), _unpack_small(s_v, sizes)

    gw_in, dw_in, mw_in, vw_in = _adamw_w_in(chip, r_main, r_fb, w_in[0], m_w_in[0], v_w_in[0])
    big = {}
    for nm, g, w, m, v in (("w_mem_kv", r_mk, w_mem_kv, m_w_mem_kv, v_w_mem_kv),
                           ("w_branch_a", r_wa, w_branch_a, m_w_branch_a, v_w_branch_a),
                           ("w_branch_b", r_wb, w_branch_b, m_w_branch_b, v_w_branch_b),
                           ("w_branch_c", r_wc, w_branch_c, m_w_branch_c, v_w_branch_c),
                           ("w_out", r_wo, w_out, m_w_out, v_w_out)):
        big[nm] = (g,) + tuple(_adamw(g, w[0], m[0], v[0], tr=256, name="adamw_" + nm))

    def collect(kind):
        sm = (g_small, d_small, m_small, v_small)[kind]
        win = (gw_in, dw_in, mw_in, vw_in)[kind]
        return ([sm[0], sm[1], win[None]] + [a for a in sm[2:10]]
                + [big[n][kind][None] for n in ("w_mem_kv", "w_branch_a", "w_branch_b", "w_branch_c", "w_out")])

    return (loss, r["grad_x"][None], *collect(0), *collect(1), *collect(2), *collect(3))
```

```python
import functools

import numpy as np
import jax
import jax.numpy as jnp
from jax import lax
from jax.experimental import pallas as pl
from jax.experimental.pallas import tpu as pltpu

F32 = jnp.float32
BF16 = jnp.bfloat16
HI = lax.Precision.HIGHEST
SDS = jax.ShapeDtypeStruct
MESH = pl.DeviceIdType.MESH

D_MODEL = 2048
HEAD_DIM = 64
A_HEADS = 12
A_GROUP = 3
B_HEADS = 12
C_HEADS = 4
C_HEAD_DIM = 128
WINDOW = 128
EPS = 1e-6
NEG = -1e30
LANE = 128

QA, KA, VA, ZA = 0, 768, 1024, 1280
QB, KB, VB, ZB = 2048, 2816, 3584, 4352
QC, ZC = 5120, 5632
GATE = 6144
P_MAIN = 12288
N_FORGET = 12
FORGET_COL = 5120
SHARD_COLS = 3075
SLAB = 3200
SLAB_START = (0, 3072, 6016, 9088)
SLAB_SHIFT = (0, 3, 122, 125)
N_CHIPS = 4

ADAM_LR = 0.001
ADAM_B1 = 0.9
ADAM_B2 = 0.999
ADAM_EPS = 1e-08
ADAM_WD = 0.01
ADAM_STEP = 10

VMEM_LIMIT = 56 * 1024 * 1024


def _params(sem, vmem=VMEM_LIMIT):
    return pltpu.CompilerParams(dimension_semantics=sem, vmem_limit_bytes=vmem)


def _win(tr, width, off):
    return pl.BlockSpec((pl.Element(tr), pl.Element(width)), lambda i, *_: (i * tr, off))


def _rowblk(tr, width):
    return pl.BlockSpec((tr, width), lambda i, *_: (i, 0))


def _const(shape):
    nd = len(shape)
    return pl.BlockSpec(shape, lambda *_: (0,) * nd)


def _rms(x, g):
    return x * lax.rsqrt(jnp.mean(x * x, axis=-1, keepdims=True) + EPS) * g


def _head_norm(x, g_tiled, bd):
    ms = jnp.dot(x * x, bd, precision=HI, preferred_element_type=F32)
    return x * lax.rsqrt(ms + EPS) * g_tiled


def _silu(z):
    return z * jax.nn.sigmoid(z)


def _dot_nt(a, b):
    return lax.dot_general(a, b, (((1,), (1,)), ((), ())), preferred_element_type=F32)


def _dot_tn(a, b):
    return lax.dot_general(a, b, (((0,), (0,)), ((), ())), preferred_element_type=F32)


def _dot(a, b):
    return jnp.dot(a, b, preferred_element_type=F32)


def _swa_fn(qk, vz, qkp, vzp, qg, kg, sinks, bd, first):
    q = _head_norm(qk[:, :768], qg, bd)
    k2 = jnp.concatenate([qkp[:, 768:], qk[:, 768:]], axis=0)
    k2 = _head_norm(k2, kg, bd[:256, :256])
    v2 = jnp.concatenate([vzp[:, :256], vz[:, :256]], axis=0)
    z = vz[:, 256:]
    qi = lax.broadcasted_iota(jnp.int32, (WINDOW, 2 * WINDOW), 0)
    kj = lax.broadcasted_iota(jnp.int32, (WINDOW, 2 * WINDOW), 1)
    rel = qi + WINDOW - kj
    valid = (rel >= 0) & (rel < WINDOW) & (kj >= WINDOW * first.astype(jnp.int32))
    relf = rel.astype(F32)
    qb = q.astype(BF16)
    kb = k2.astype(BF16)
    vb = v2.astype(BF16)
    outs = []
    for h in range(A_HEADS):
        kv = h // A_GROUP
        s = _dot_nt(qb[:, 64 * h:64 * h + 64], kb[:, 64 * kv:64 * kv + 64]) * (HEAD_DIM ** -0.5)
        slope = float(2.0 ** (-8.0 * (h + 1) / A_HEADS))
        s = jnp.where(valid, s - slope * relf, NEG)
        sink = sinks[:, h:h + 1]
        m = lax.stop_gradient(jnp.maximum(jnp.max(s, axis=-1, keepdims=True), sink))
        p = jnp.exp(s - m)
        den = jnp.sum(p, axis=-1, keepdims=True) + jnp.exp(sink - m)
        outs.append(_dot((p / den).astype(BF16), vb[:, 64 * kv:64 * kv + 64]))
    return jnp.concatenate(outs, axis=1) * _silu(z)


def _mem_fn(qz, mkv, qg, kg, bd):
    q = _head_norm(qz[:, :512], qg, bd).astype(BF16)
    k = _head_norm(mkv[:, :512], kg, bd).astype(BF16)
    v = mkv[:, 512:].astype(BF16)
    z = qz[:, 512:]
    outs = []
    for h in range(C_HEADS):
        sl = slice(128 * h, 128 * h + 128)
        s = _dot_nt(q[:, sl], k[:, sl]) * (C_HEAD_DIM ** -0.5)
        m = lax.stop_gradient(jnp.max(s, axis=-1, keepdims=True))
        p = jnp.exp(s - m)
        den = jnp.sum(p, axis=-1, keepdims=True)
        outs.append(_dot((p / den).astype(BF16), v[:, sl]))
    return jnp.concatenate(outs, axis=1) * _silu(z)


def _qn_fn(q, g, bd):
    return _head_norm(q, g, bd) * (HEAD_DIM ** -0.5)


def _kn_fn(k, g, bd):
    return _head_norm(k, g, bd)


def _block_diag(width, hd):
    i = np.arange(width) // hd
    return jnp.asarray((i[:, None] == i[None, :]).astype(np.float32) / hd)


def _head_sum(width, hd):
    i = np.arange(width) // hd
    return jnp.asarray((i[:, None] == np.arange(LANE)[None, :]).astype(np.float32))


def _rms_fwd(x, g, *, tr, name):
    rows, dm = x.shape

    def body(x_ref, g_ref, o_ref):
        o_ref[...] = _rms(x_ref[...], g_ref[...]).astype(BF16)

    return pl.pallas_call(
        body, grid=(rows // tr,),
        in_specs=[_rowblk(tr, dm), _const((1, dm))],
        out_specs=_rowblk(tr, dm),
        out_shape=SDS((rows, dm), BF16), name=name,
        compiler_params=_params(("parallel",)))(x, g)


def _rms_bwd(x, g, dy, resid, *, tr, name):
    rows, dm = x.shape
    want_dx = resid is not None

    def body(*refs):
        if want_dx:
            x_ref, g_ref, dy_ref, r_ref, dx_ref, dg_ref = refs
        else:
            x_ref, g_ref, dy_ref, dg_ref = refs
        _, vjp = jax.vjp(_rms, x_ref[...], g_ref[...])
        dx, dg = vjp(dy_ref[...])

        @pl.when(pl.program_id(0) == 0)
        def _():
            dg_ref[...] = jnp.zeros_like(dg_ref)

        dg_ref[...] += dg
        if want_dx:
            dx_ref[...] = r_ref[...] + dx

    ins = [x, g, dy] + ([resid] if want_dx else [])
    in_specs = [_rowblk(tr, dm), _const((1, dm)), _rowblk(tr, dm)] + ([_rowblk(tr, dm)] if want_dx else [])
    out_specs = ([_rowblk(tr, dm)] if want_dx else []) + [_const((1, dm))]
    out_shape = ([SDS((rows, dm), F32)] if want_dx else []) + [SDS((1, dm), F32)]
    return pl.pallas_call(
        body, grid=(rows // tr,), in_specs=in_specs, out_specs=out_specs, out_shape=out_shape, name=name,
        compiler_params=_params(("arbitrary",)))(*ins)


def _matmul(a, b, *, dims, out_dtype, tm, tn, tk, name, add=None):
    if dims == "tn":
        kdim, m = a.shape
    else:
        m, kdim = a.shape
    n = b.shape[0] if dims == "nt" else b.shape[1]
    tm, tn, tk = min(tm, m), min(tn, n), min(tk, kdim)
    assert m % tm == 0 and n % tn == 0 and kdim % tk == 0, (name, a.shape, b.shape)
    nk = kdim // tk
    has_add = add is not None

    def body(*refs):
        if has_add:
            a_ref, b_ref, add_ref, o_ref, acc = refs
        else:
            a_ref, b_ref, o_ref, acc = refs
        k = pl.program_id(2)
        if dims == "nn":
            part = _dot(a_ref[...], b_ref[...])
        elif dims == "nt":
            part = _dot_nt(a_ref[...], b_ref[...])
        else:
            part = _dot_tn(a_ref[...], b_ref[...])

        @pl.when(k == 0)
        def _():
            acc[...] = part + add_ref[...] if has_add else part

        @pl.when(k > 0)
        def _():
            acc[...] += part

        @pl.when(k == nk - 1)
        def _():
            o_ref[...] = acc[...].astype(out_dtype)

    a_spec = pl.BlockSpec((tk, tm), lambda i, j, k: (k, i)) if dims == "tn" else pl.BlockSpec((tm, tk), lambda i, j, k: (i, k))
    b_spec = pl.BlockSpec((tn, tk), lambda i, j, k: (j, k)) if dims == "nt" else pl.BlockSpec((tk, tn), lambda i, j, k: (k, j))
    o_spec = pl.BlockSpec((tm, tn), lambda i, j, k: (i, j))
    ins = [a, b] + ([add] if has_add else [])
    in_specs = [a_spec, b_spec] + ([o_spec] if has_add else [])
    return pl.pallas_call(
        body, grid=(m // tm, n // tn, nk), in_specs=in_specs, out_specs=o_spec,
        out_shape=SDS((m, n), out_dtype), scratch_shapes=[pltpu.VMEM((tm, tn), F32)], name=name,
        compiler_params=_params(("parallel", "parallel", "arbitrary")))(*ins)


def _swa_specs(nb):
    prev = lambda off: pl.BlockSpec((pl.Element(WINDOW), pl.Element(1024)),
                                    lambda n: (jnp.maximum(n - 1, 0) * WINDOW, off))
    return [_win(WINDOW, 1024, QA), _win(WINDOW, 1024, VA), prev(QA), prev(VA),
            _const((1, 768)), _const((1, 256)), _const((1, LANE)), _const((768, 768))]


def _swa_fwd(proj, qg, kg, sinks, bd):
    s = proj.shape[0]
    nb = s // WINDOW

    def body(qk_ref, vz_ref, qkp_ref, vzp_ref, qg_ref, kg_ref, sk_ref, bd_ref, o_ref):
        first = pl.program_id(0) == 0
        o_ref[...] = _swa_fn(qk_ref[...], vz_ref[...], qkp_ref[...], vzp_ref[...], qg_ref[...], kg_ref[...],
                             sk_ref[...], bd_ref[...], first).astype(BF16)

    return pl.pallas_call(
        body, grid=(nb,), in_specs=_swa_specs(nb), out_specs=_rowblk(WINDOW, 768),
        out_shape=SDS((s, 768), BF16), name="swa_fwd",
        compiler_params=_params(("parallel",)))(proj, proj, proj, proj, qg, kg, sinks, bd)


def _swa_bwd(proj, qg, kg, sinks, bd, dga):
    s = proj.shape[0]
    nb = s // WINDOW

    def body(qk_ref, vz_ref, qkp_ref, vzp_ref, qg_ref, kg_ref, sk_ref, bd_ref, dg_ref,
             dcur_ref, dprev_ref, dqg_ref, dkg_ref, dsk_ref):
        first = pl.program_id(0) == 0
        bd_v = bd_ref[...]
        fn = lambda qk, vz, qkp, vzp, qg_, kg_, sk: _swa_fn(qk, vz, qkp, vzp, qg_, kg_, sk, bd_v, first)
        _, vjp = jax.vjp(fn, qk_ref[...], vz_ref[...], qkp_ref[...], vzp_ref[...], qg_ref[...], kg_ref[...], sk_ref[...])
        dqk, dvz, dqkp, dvzp, dqg, dkg, dsk = vjp(dg_ref[...])

        @pl.when(first)
        def _():
            dqg_ref[...] = jnp.zeros_like(dqg_ref)
            dkg_ref[...] = jnp.zeros_like(dkg_ref)
            dsk_ref[...] = jnp.zeros_like(dsk_ref)

        dqg_ref[...] += dqg
        dkg_ref[...] += dkg
        dsk_ref[...] += dsk
        dcur_ref[...] = jnp.concatenate([dqk, dvz], axis=1)
        dprev_ref[...] = jnp.concatenate([dqkp[:, 768:], dvzp[:, :256]], axis=1)

    return pl.pallas_call(
        body, grid=(nb,), in_specs=_swa_specs(nb) + [_rowblk(WINDOW, 768)],
        out_specs=[_rowblk(WINDOW, 2048), pl.BlockSpec((None, WINDOW, 512), lambda n: (n, 0, 0)),
                   _const((1, 768)), _const((1, 256)), _const((1, LANE))],
        out_shape=[SDS((s, 2048), F32), SDS((nb, WINDOW, 512), F32), SDS((1, 768), F32), SDS((1, 256), F32),
                   SDS((1, LANE), F32)],
        name="swa_bwd", compiler_params=_params(("arbitrary",)))(proj, proj, proj, proj, qg, kg, sinks, bd, dga)


def _swa_combine(dcur, dprev):
    s = dcur.shape[0]
    nb = s // WINDOW

    def body(c_ref, p_ref, o_ref):
        c = c_ref[...]
        nxt = jnp.where(pl.program_id(0) == nb - 1, 0.0, p_ref[...])
        o_ref[...] = jnp.concatenate([c[:, :768], c[:, 768:1280] + nxt, c[:, 1280:]], axis=1).astype(BF16)

    return pl.pallas_call(
        body, grid=(nb,),
        in_specs=[_rowblk(WINDOW, 2048), pl.BlockSpec((None, WINDOW, 512), lambda n: (jnp.minimum(n + 1, nb - 1), 0, 0))],
        out_specs=_rowblk(WINDOW, 2048), out_shape=SDS((s, 2048), BF16), name="swa_combine",
        compiler_params=_params(("parallel",)))(dcur, dprev)


def _mem_fwd(proj, mkv, qg, kg, bd, *, tr):
    s = proj.shape[0]

    def body(qz_ref, mkv_ref, qg_ref, kg_ref, bd_ref, o_ref):
        o_ref[...] = _mem_fn(qz_ref[...], mkv_ref[...], qg_ref[...], kg_ref[...], bd_ref[...]).astype(BF16)

    return pl.pallas_call(
        body, grid=(s // tr,),
        in_specs=[_win(tr, 1024, QC), _const(mkv.shape), _const((1, 512)), _const((1, 512)), _const((512, 512))],
        out_specs=_rowblk(tr, 512), out_shape=SDS((s, 512), BF16), name="mem_fwd",
        compiler_params=_params(("parallel",)))(proj, mkv, qg, kg, bd)


def _mem_bwd(proj, mkv, qg, kg, bd, dgc, *, tr):
    s = proj.shape[0]

    def body(qz_ref, mkv_ref, qg_ref, kg_ref, bd_ref, dg_ref, dqz_ref, dmkv_ref, dqg_ref, dkg_ref):
        bd_v = bd_ref[...]
        fn = lambda qz, mkv_, qg_, kg_: _mem_fn(qz, mkv_, qg_, kg_, bd_v)
        _, vjp = jax.vjp(fn, qz_ref[...], mkv_ref[...], qg_ref[...], kg_ref[...])
        dqz, dmkv, dqg, dkg = vjp(dg_ref[...])

        @pl.when(pl.program_id(0) == 0)
        def _():
            dmkv_ref[...] = jnp.zeros_like(dmkv_ref)
            dqg_ref[...] = jnp.zeros_like(dqg_ref)
            dkg_ref[...] = jnp.zeros_like(dkg_ref)

        dmkv_ref[...] += dmkv
        dqg_ref[...] += dqg
        dkg_ref[...] += dkg
        dqz_ref[...] = dqz.astype(BF16)

    return pl.pallas_call(
        body, grid=(s // tr,),
        in_specs=[_win(tr, 1024, QC), _const(mkv.shape), _const((1, 512)), _const((1, 512)), _const((512, 512)),
                  _rowblk(tr, 512)],
        out_specs=[_rowblk(tr, 1024), _const(mkv.shape), _const((1, 512)), _const((1, 512))],
        out_shape=[SDS((s, 1024), BF16), SDS(mkv.shape, F32), SDS((1, 512), F32), SDS((1, 512), F32)],
        name="mem_bwd", compiler_params=_params(("arbitrary",)))(proj, mkv, qg, kg, bd, dgc)


def _log_sigmoid(x):
    return jnp.minimum(x, 0.0) - jnp.log1p(jnp.exp(-jnp.abs(x)))


def _fox_prep(proj, fbl, qg, kg, bfor, bd, *, tr):
    s = proj.shape[0]
    tri = jnp.asarray(np.tril(np.ones((tr, tr), np.float32)))

    def body(q_ref, k_ref, fb_ref, qg_ref, kg_ref, bf_ref, bd_ref, tri_ref, qn_ref, kn_ref, cq_ref, ck_ref, carry):
        @pl.when(pl.program_id(0) == 0)
        def _():
            carry[...] = jnp.zeros_like(carry)

        bd_v = bd_ref[...]
        qn_ref[...] = _qn_fn(q_ref[...], qg_ref[...], bd_v).astype(BF16)
        kn_ref[...] = _kn_fn(k_ref[...], kg_ref[...], bd_v).astype(BF16)
        lane = lax.broadcasted_iota(jnp.int32, (tr, LANE), 1)
        logf = jnp.where(lane < N_FORGET, _log_sigmoid(fb_ref[...] + bf_ref[...]), 0.0)
        c = jnp.dot(tri_ref[...], logf, precision=HI, preferred_element_type=F32) + carry[...]
        cq_ref[...] = c
        ck_ref[...] = jnp.transpose(c)
        carry[...] = c[tr - 1:tr, :]

    return pl.pallas_call(
        body, grid=(s // tr,),
        in_specs=[_win(tr, 768, QB), _win(tr, 768, KB), _rowblk(tr, LANE), _const((1, 768)), _const((1, 768)),
                  _const((1, LANE)), _const((768, 768)), _const((tr, tr))],
        out_specs=[_rowblk(tr, 768), _rowblk(tr, 768), _rowblk(tr, LANE), pl.BlockSpec((LANE, tr), lambda i: (0, i))],
        out_shape=[SDS((s, 768), BF16), SDS((s, 768), BF16), SDS((s, LANE), F32), SDS((LANE, s), F32)],
        scratch_shapes=[pltpu.VMEM((1, LANE), F32)], name="fox_prep",
        compiler_params=_params(("arbitrary",)))(proj, proj, fbl, qg, kg, bfor, bd, tri)


def _fox_tiles(s):
    tq = min(256, s)
    tk = min(512, s)
    return tq, tk


def _fox_fwd(proj, qn, kn, cq, ck):
    s = proj.shape[0]
    tq, tk = _fox_tiles(s)
    nq, nk = s // tq, s // tk

    def last_k(i):
        return (i * tq + tq - 1) // tk

    def body(q_ref, k_ref, v_ref, cq_ref, ck_ref, z_ref, gb_ref, yb_ref, lse_ref, acc, m_s, l_s):
        i, j = pl.program_id(0), pl.program_id(1)

        @pl.when(j == 0)
        def _():
            acc[...] = jnp.zeros_like(acc)
            m_s[...] = jnp.full_like(m_s, NEG)
            l_s[...] = jnp.ones_like(l_s)

        @pl.when(j <= last_k(i))
        def _():
            qpos = i * tq + lax.broadcasted_iota(jnp.int32, (tq, tk), 0)
            kpos = j * tk + lax.broadcasted_iota(jnp.int32, (tq, tk), 1)
            mask = kpos <= qpos
            q, k = q_ref[...], k_ref[...]
            v = v_ref[...].astype(BF16)
            cqv, ckv = cq_ref[...], ck_ref[...]
            m_all, l_all = m_s[...], l_s[...]
            lane = lax.broadcasted_iota(jnp.int32, (tq, LANE), 1)
            m_out, l_out = m_all, l_all
            for hp in range(B_HEADS // 2):
                acc_pair = acc[:, 128 * hp:128 * hp + 128]
                new = []
                for u in range(2):
                    h = 2 * hp + u
                    sl = slice(64 * h, 64 * h + 64)
                    sc = _dot_nt(q[:, sl], k[:, sl]) + cqv[:, h:h + 1] - ckv[h:h + 1, :]
                    sc = jnp.where(mask, sc, NEG)
                    m_prev = m_all[:, h:h + 1]
                    m_new = jnp.maximum(m_prev, jnp.max(sc, axis=-1, keepdims=True))
                    alpha = jnp.exp(m_prev - m_new)
                    p = jnp.exp(sc - m_new)
                    l_new = alpha * l_all[:, h:h + 1] + jnp.sum(p, axis=-1, keepdims=True)
                    new.append(alpha * acc_pair[:, 64 * u:64 * u + 64] + _dot(p.astype(BF16), v[:, sl]))
                    m_out = jnp.where(lane == h, m_new, m_out)
                    l_out = jnp.where(lane == h, l_new, l_out)
                acc[:, 128 * hp:128 * hp + 128] = jnp.concatenate(new, axis=1)
            m_s[...] = m_out
            l_s[...] = l_out

        @pl.when(j == nk - 1)
        def _():
            l_all = l_s[...]
            inv = 1.0 / l_all
            a = acc[...]
            y = jnp.concatenate([a[:, 64 * h:64 * h + 64] * inv[:, h:h + 1] for h in range(B_HEADS)], axis=1)
            yb_ref[...] = y
            gb_ref[...] = (y * _silu(z_ref[...])).astype(BF16)
            lse_ref[...] = m_s[...] + jnp.log(l_all)

    kmap = lambda i, j: (jnp.minimum(j, last_k(i)), 0)
    return pl.pallas_call(
        body, grid=(nq, nk),
        in_specs=[pl.BlockSpec((tq, 768), lambda i, j: (i, 0)),
                  pl.BlockSpec((tk, 768), kmap),
                  pl.BlockSpec((pl.Element(tk), pl.Element(768)), lambda i, j: (jnp.minimum(j, last_k(i)) * tk, VB)),
                  pl.BlockSpec((tq, LANE), lambda i, j: (i, 0)),
                  pl.BlockSpec((16, tk), lambda i, j: (0, jnp.minimum(j, last_k(i)))),
                  pl.BlockSpec((pl.Element(tq), pl.Element(768)), lambda i, j: (i * tq, ZB))],
        out_specs=[pl.BlockSpec((tq, 768), lambda i, j: (i, 0)), pl.BlockSpec((tq, 768), lambda i, j: (i, 0)),
                   pl.BlockSpec((tq, LANE), lambda i, j: (i, 0))],
        out_shape=[SDS((s, 768), BF16), SDS((s, 768), F32), SDS((s, LANE), F32)],
        scratch_shapes=[pltpu.VMEM((tq, 768), F32), pltpu.VMEM((tq, LANE), F32), pltpu.VMEM((tq, LANE), F32)],
        name="fox_fwd", compiler_params=_params(("parallel", "arbitrary")))(qn, kn, proj, cq, ck, proj)


def _fox_bwd_pre(proj, yb, dgb, hsum, *, tr):
    s = proj.shape[0]

    def body(z_ref, y_ref, dg_ref, hs_ref, dy_ref, dz_ref, dl_ref):
        z, y, dg = z_ref[...], y_ref[...], dg_ref[...]
        sg = jax.nn.sigmoid(z)
        dy = dg * (z * sg)
        dy_ref[...] = dy.astype(BF16)
        dz_ref[...] = (dg * y * (sg * (1.0 + z * (1.0 - sg)))).astype(BF16)
        dl_ref[...] = jnp.dot(dy * y, hs_ref[...], precision=HI, preferred_element_type=F32)

    return pl.pallas_call(
        body, grid=(s // tr,),
        in_specs=[_win(tr, 768, ZB), _rowblk(tr, 768), _rowblk(tr, 768), _const((768, LANE))],
        out_specs=[_rowblk(tr, 768), _rowblk(tr, 768), _rowblk(tr, LANE)],
        out_shape=[SDS((s, 768), BF16), SDS((s, 768), BF16), SDS((s, LANE), F32)], name="fox_bwd_pre",
        compiler_params=_params(("parallel",)))(proj, yb, dgb, hsum)


def _fox_bwd(proj, qn, kn, cq, ck, lse, delta, dyb):
    s = proj.shape[0]
    tq, tk = _fox_tiles(s)
    nq, nk = s // tq, s // tk

    def first_q(j):
        return (j * tk) // tq

    def body(q_ref, k_ref, v_ref, cq_ref, ck_ref, lse_ref, dl_ref, dy_ref,
             dq_ref, dcq_ref, dk_ref, dv_ref, dck_ref, dk_acc, dv_acc, dck_acc):
        j, i = pl.program_id(0), pl.program_id(1)

        @pl.when((j == 0) & (i == 0))
        def _():
            dq_ref[...] = jnp.zeros_like(dq_ref)
            dcq_ref[...] = jnp.zeros_like(dcq_ref)

        @pl.when(i == 0)
        def _():
            dk_acc[...] = jnp.zeros_like(dk_acc)
            dv_acc[...] = jnp.zeros_like(dv_acc)
            dck_acc[...] = jnp.zeros_like(dck_acc)

        @pl.when(i >= first_q(j))
        def _():
            qpos = i * tq + lax.broadcasted_iota(jnp.int32, (tq, tk), 0)
            kpos = j * tk + lax.broadcasted_iota(jnp.int32, (tq, tk), 1)
            mask = kpos <= qpos
            q, k = q_ref[...], k_ref[...]
            v = v_ref[...].astype(BF16)
            dy = dy_ref[...]
            cqv, ckv, lsev, dlv = cq_ref[...], ck_ref[...], lse_ref[...], dl_ref[...]
            lane = lax.broadcasted_iota(jnp.int32, (tq, LANE), 1)
            rows = pl.ds(pl.multiple_of(i * tq, tq), tq)
            dcq_t = jnp.zeros((tq, LANE), F32)
            for hp in range(B_HEADS // 2):
                dq_new, dk_new, dv_new = [], [], []
                for u in range(2):
                    h = 2 * hp + u
                    sl = slice(64 * h, 64 * h + 64)
                    sc = _dot_nt(q[:, sl], k[:, sl]) + cqv[:, h:h + 1] - ckv[h:h + 1, :]
                    sc = jnp.where(mask, sc, NEG)
                    p = jnp.exp(sc - lsev[:, h:h + 1])
                    dp = _dot_nt(dy[:, sl], v[:, sl])
                    ds = p * (dp - dlv[:, h:h + 1])
                    dsb = ds.astype(BF16)
                    dv_new.append(_dot_tn(p.astype(BF16), dy[:, sl]))
                    dk_new.append(_dot_tn(dsb, q[:, sl]))
                    dq_new.append(_dot(dsb, k[:, sl]))
                    dcq_t = jnp.where(lane == h, jnp.sum(ds, axis=-1, keepdims=True), dcq_t)
                    dck_acc[h:h + 1, :] -= jnp.sum(ds, axis=0, keepdims=True)
                cols = slice(128 * hp, 128 * hp + 128)
                dq_ref[rows, cols] += jnp.concatenate(dq_new, axis=1)
                dk_acc[:, cols] += jnp.concatenate(dk_new, axis=1)
                dv_acc[:, cols] += jnp.concatenate(dv_new, axis=1)
            dcq_ref[rows, :] += dcq_t

        @pl.when(i == nq - 1)
        def _():
            dk_ref[...] = dk_acc[...]
            dv_ref[...] = dv_acc[...].astype(BF16)
            dck_ref[...] = jnp.concatenate([dck_acc[...], jnp.zeros((LANE - 16, tk), F32)], axis=0)

    qmap = lambda j, i: (jnp.maximum(i, first_q(j)), 0)
    return pl.pallas_call(
        body, grid=(nk, nq),
        in_specs=[pl.BlockSpec((tq, 768), qmap),
                  pl.BlockSpec((tk, 768), lambda j, i: (j, 0)),
                  pl.BlockSpec((pl.Element(tk), pl.Element(768)), lambda j, i: (j * tk, VB)),
                  pl.BlockSpec((tq, LANE), qmap),
                  pl.BlockSpec((16, tk), lambda j, i: (0, j)),
                  pl.BlockSpec((tq, LANE), qmap),
                  pl.BlockSpec((tq, LANE), qmap),
                  pl.BlockSpec((tq, 768), qmap)],
        out_specs=[_const((s, 768)), _const((s, LANE)),
                   pl.BlockSpec((tk, 768), lambda j, i: (j, 0)), pl.BlockSpec((tk, 768), lambda j, i: (j, 0)),
                   pl.BlockSpec((LANE, tk), lambda j, i: (0, j))],
        out_shape=[SDS((s, 768), F32), SDS((s, LANE), F32), SDS((s, 768), F32), SDS((s, 768), BF16),
                   SDS((LANE, s), F32)],
        scratch_shapes=[pltpu.VMEM((tk, 768), F32), pltpu.VMEM((tk, 768), F32), pltpu.VMEM((16, tk), F32)],
        name="fox_bwd", compiler_params=_params(("arbitrary", "arbitrary")))(qn, kn, proj, cq, ck, lse, delta, dyb)


def _fox_bwd_post(proj, fbl, qg, kg, bfor, bd, dqn, dkn, dcq, dck, *, tr):
    s = proj.shape[0]
    nb = s // tr
    triu = jnp.asarray(np.triu(np.ones((tr, tr), np.float32)))
    rev = lambda i: nb - 1 - i

    def body(q_ref, k_ref, fb_ref, qg_ref, kg_ref, bf_ref, bd_ref, tri_ref, dqn_ref, dkn_ref, dcq_ref, dck_ref,
             dq_ref, dk_ref, dfb_ref, dqg_ref, dkg_ref, dbf_ref, carry):
        @pl.when(pl.program_id(0) == 0)
        def _():
            carry[...] = jnp.zeros_like(carry)
            dqg_ref[...] = jnp.zeros_like(dqg_ref)
            dkg_ref[...] = jnp.zeros_like(dkg_ref)
            dbf_ref[...] = jnp.zeros_like(dbf_ref)

        bd_v = bd_ref[...]
        _, vjp_q = jax.vjp(lambda q, g: _qn_fn(q, g, bd_v), q_ref[...], qg_ref[...])
        dq, dqg = vjp_q(dqn_ref[...])
        _, vjp_k = jax.vjp(lambda k, g: _kn_fn(k, g, bd_v), k_ref[...], kg_ref[...])
        dk, dkg = vjp_k(dkn_ref[...])
        dq_ref[...] = dq.astype(BF16)
        dk_ref[...] = dk.astype(BF16)
        dqg_ref[...] += dqg
        dkg_ref[...] += dkg

        dc = dcq_ref[...] + jnp.transpose(dck_ref[...])
        dlogf = jnp.dot(tri_ref[...], dc, precision=HI, preferred_element_type=F32) + carry[...]
        carry[...] = dlogf[0:1, :]
        lane = lax.broadcasted_iota(jnp.int32, (tr, LANE), 1)
        xf = fb_ref[...] + bf_ref[...]
        dfb = jnp.where(lane < N_FORGET, dlogf * jax.nn.sigmoid(-xf), 0.0)
        dfb_ref[...] = dfb.astype(BF16)
        dbf_ref[...] += jnp.sum(dfb, axis=0, keepdims=True)

    rb = lambda w: pl.BlockSpec((tr, w), lambda i: (rev(i), 0))
    wn = lambda w, off: pl.BlockSpec((pl.Element(tr), pl.Element(w)), lambda i: (rev(i) * tr, off))
    return pl.pallas_call(
        body, grid=(nb,),
        in_specs=[wn(768, QB), wn(768, KB), rb(LANE), _const((1, 768)), _const((1, 768)), _const((1, LANE)),
                  _const((768, 768)), _const((tr, tr)), rb(768), rb(768), rb(LANE),
                  pl.BlockSpec((LANE, tr), lambda i: (0, rev(i)))],
        out_specs=[rb(768), rb(768), rb(LANE), _const((1, 768)), _const((1, 768)), _const((1, LANE))],
        out_shape=[SDS((s, 768), BF16), SDS((s, 768), BF16), SDS((s, LANE), BF16), SDS((1, 768), F32),
                   SDS((1, 768), F32), SDS((1, LANE), F32)],
        scratch_shapes=[pltpu.VMEM((1, LANE), F32)], name="fox_bwd_post",
        compiler_params=_params(("arbitrary",)))(proj, proj, fbl, qg, kg, bfor, bd, triu, dqn, dkn, dcq, dck)


def _merge_specs(tr):
    row = lambda w: pl.BlockSpec((tr, w), lambda i, j: (i, 0))
    shard = lambda r: pl.BlockSpec((None, r, 512), lambda i, j: (j, 0, 0))
    gate = lambda b: pl.BlockSpec((tr, 512), lambda i, j: (i, (GATE + 2048 * b) // 512 + j))
    return [row(768), row(768), row(512), shard(768), shard(768), shard(512), gate(0), gate(1), gate(2)]


def _merge_fwd(proj, ga, gb, gc, wa, wb, wc, *, tr):
    s = proj.shape[0]

    def body(ga_ref, gb_ref, gc_ref, wa_ref, wb_ref, wc_ref, l0_ref, l1_ref, l2_ref, y_ref):
        ua = _dot(ga_ref[...], wa_ref[...])
        ub = _dot(gb_ref[...], wb_ref[...])
        uc = _dot(gc_ref[...], wc_ref[...])
        y = jax.nn.sigmoid(l0_ref[...]) * ua + jax.nn.sigmoid(l1_ref[...]) * ub + jax.nn.sigmoid(l2_ref[...]) * uc
        y_ref[...] = y.astype(BF16)

    return pl.pallas_call(
        body, grid=(s // tr, N_CHIPS), in_specs=_merge_specs(tr),
        out_specs=pl.BlockSpec((tr, 512), lambda i, j: (i, j)), out_shape=SDS((s, D_MODEL), BF16), name="merge_fwd",
        compiler_params=_params(("parallel", "arbitrary")))(ga, gb, gc, wa, wb, wc, proj, proj, proj)


def _merge_bwd(proj, ga, gb, gc, wa, wb, wc, dy, *, tr):
    s = proj.shape[0]

    def body(ga_ref, gb_ref, gc_ref, wa_ref, wb_ref, wc_ref, l0_ref, l1_ref, l2_ref, dy_ref,
             dl0_ref, dl1_ref, dl2_ref, dua_ref, dub_ref, duc_ref, dga_ref, dgb_ref, dgc_ref):
        j = pl.program_id(1)
        dyv = dy_ref[...]

        @pl.when(j == 0)
        def _():
            dga_ref[...] = jnp.zeros_like(dga_ref)
            dgb_ref[...] = jnp.zeros_like(dgb_ref)
            dgc_ref[...] = jnp.zeros_like(dgc_ref)

        for g_ref, w_ref, l_ref, dl_ref, du_ref, dg_ref in (
                (ga_ref, wa_ref, l0_ref, dl0_ref, dua_ref, dga_ref),
                (gb_ref, wb_ref, l1_ref, dl1_ref, dub_ref, dgb_ref),
                (gc_ref, wc_ref, l2_ref, dl2_ref, duc_ref, dgc_ref)):
            w = w_ref[...]
            u = _dot(g_ref[...], w)
            sg = jax.nn.sigmoid(l_ref[...])
            dl_ref[...] = (dyv * u * sg * (1.0 - sg)).astype(BF16)
            du = (dyv * sg).astype(BF16)
            du_ref[...] = du
            dg_ref[...] += _dot_nt(du, w)

    blk = pl.BlockSpec((tr, 512), lambda i, j: (i, j))
    row = lambda w: pl.BlockSpec((tr, w), lambda i, j: (i, 0))
    big = SDS((s, D_MODEL), BF16)
    return pl.pallas_call(
        body, grid=(s // tr, N_CHIPS), in_specs=_merge_specs(tr) + [blk],
        out_specs=[blk] * 6 + [row(768), row(768), row(512)],
        out_shape=[big] * 6 + [SDS((s, 768), F32), SDS((s, 768), F32), SDS((s, 512), F32)], name="merge_bwd",
        compiler_params=_params(("parallel", "arbitrary")))(ga, gb, gc, wa, wb, wc, proj, proj, proj, dy)


def _out_loss(y, wo, x, tgt, *, tr, tn):
    s = x.shape[0]

    def body(y_ref, w_ref, x_ref, t_ref, d_ref, db_ref, sq_ref):
        @pl.when((pl.program_id(0) == 0) & (pl.program_id(1) == 0))
        def _():
            sq_ref[...] = jnp.zeros_like(sq_ref)

        out = x_ref[...] + _dot(y_ref[...], w_ref[...])
        diff = out - t_ref[...]
        sq_ref[...] += jnp.sum(diff * diff, axis=0, keepdims=True)
        d = diff * (1.0 / D_MODEL)
        d_ref[...] = d
        db_ref[...] = d.astype(BF16)

    blk = pl.BlockSpec((tr, tn), lambda i, j: (i, j))
    return pl.pallas_call(
        body, grid=(s // tr, D_MODEL // tn),
        in_specs=[pl.BlockSpec((tr, D_MODEL), lambda i, j: (i, 0)), pl.BlockSpec((D_MODEL, tn), lambda i, j: (0, j)), blk, blk],
        out_specs=[blk, blk, _const((1, tn))],
        out_shape=[SDS((s, D_MODEL), F32), SDS((s, D_MODEL), BF16), SDS((1, tn), F32)], name="out_loss",
        compiler_params=_params(("arbitrary", "arbitrary")))(y, wo, x, tgt)


def _tile_gain(g, reps):
    return jnp.tile(g.reshape(1, -1), (1, reps))


def _pad_lane(v):
    v = v.reshape(1, -1)
    return jnp.pad(v, ((0, 0), (0, LANE - v.shape[1])))


def _local_step(x, mem, tgt, w_main, w_fb, w_mk, wa, wb, wc, wo, norm_gain, mem_norm_gain, b_forget,
                q_gain_a, k_gain_a, sinks_a, q_gain_b, k_gain_b, q_gain_c, k_gain_c):
    s = x.shape[0]
    tr = min(512, s)
    bd64 = _block_diag(768, HEAD_DIM)
    bd128 = _block_diag(512, C_HEAD_DIM)
    hsum = _head_sum(768, HEAD_DIM)
    qga, kga = _tile_gain(q_gain_a, 12), _tile_gain(k_gain_a, 4)
    qgb, kgb = _tile_gain(q_gain_b, 12), _tile_gain(k_gain_b, 12)
    qgc, kgc = _tile_gain(q_gain_c, 4), _tile_gain(k_gain_c, 4)
    sinks = _pad_lane(sinks_a)
    bfor = _pad_lane(b_forget)

    hn = _rms_fwd(x, norm_gain, tr=tr, name="rms_x")
    proj = _matmul(hn, w_main, dims="nn", out_dtype=F32, tm=1024, tn=512, tk=D_MODEL, name="proj_main")
    fbl = _matmul(hn, w_fb, dims="nn", out_dtype=F32, tm=1024, tn=LANE, tk=D_MODEL, name="proj_forget")
    memn = _rms_fwd(mem, mem_norm_gain, tr=mem.shape[0], name="rms_mem")
    mkv = _matmul(memn, w_mk, dims="nn", out_dtype=F32, tm=256, tn=512, tk=D_MODEL, name="mem_kv")

    ga = _swa_fwd(proj, qga, kga, sinks, bd64)
    qn, kn, cq, ck = _fox_prep(proj, fbl, qgb, kgb, bfor, bd64, tr=tr)
    gb, yb, lse = _fox_fwd(proj, qn, kn, cq, ck)
    gc = _mem_fwd(proj, mkv, qgc, kgc, bd128, tr=tr)
    y = _merge_fwd(proj, ga, gb, gc, wa, wb, wc, tr=tr)
    dout, dout_b, sq = _out_loss(y, wo, x, tgt, tr=tr, tn=512)

    d_wo = _matmul(y, dout_b, dims="tn", out_dtype=F32, tm=1024, tn=512, tk=2048, name="dw_out")
    dy = _matmul(dout_b, wo, dims="nt", out_dtype=F32, tm=1024, tn=512, tk=D_MODEL, name="dy")
    dl0, dl1, dl2, dua, dub, duc, dga, dgb, dgc = _merge_bwd(proj, ga, gb, gc, wa, wb, wc, dy, tr=tr)
    d_wa = _matmul(ga, dua, dims="tn", out_dtype=F32, tm=768, tn=512, tk=2048, name="dw_branch_a")
    d_wb = _matmul(gb, dub, dims="tn", out_dtype=F32, tm=768, tn=512, tk=2048, name="dw_branch_b")
    d_wc = _matmul(gc, duc, dims="tn", out_dtype=F32, tm=512, tn=512, tk=2048, name="dw_branch_c")

    dcur, dprev, d_qga, d_kga, d_sinks = _swa_bwd(proj, qga, kga, sinks, bd64, dga)
    dproj_a = _swa_combine(dcur, dprev)

    dyb, dzb, delta = _fox_bwd_pre(proj, yb, dgb, hsum, tr=tr)
    dqn, dcq, dkn, dvb, dck = _fox_bwd(proj, qn, kn, cq, ck, lse, delta, dyb)
    dqb, dkb, dfb, d_qgb, d_kgb, d_bf = _fox_bwd_post(proj, fbl, qgb, kgb, bfor, bd64, dqn, dkn, dcq, dck, tr=tr)

    dproj_c, dmkv, d_qgc, d_kgc = _mem_bwd(proj, mkv, qgc, kgc, bd128, dgc, tr=tr)
    dmkv_b = dmkv.astype(BF16)
    d_wmk = _matmul(memn, dmkv_b, dims="tn", out_dtype=F32, tm=1024, tn=512, tk=256, name="dw_mem_kv")
    dmemn = _matmul(dmkv_b, w_mk, dims="nt", out_dtype=F32, tm=256, tn=512, tk=1024, name="dmemn")
    (d_mem_gain,) = _rms_bwd(mem, mem_norm_gain, dmemn, None, tr=mem.shape[0], name="rms_mem_bwd")

    dproj = jnp.concatenate([dproj_a, dqb, dkb, dvb, dzb, dproj_c, dl0, dl1, dl2], axis=1)
    dhn_f = _matmul(dfb, w_fb, dims="nt", out_dtype=F32, tm=1024, tn=512, tk=LANE, name="dhn_forget")
    dhn = _matmul(dproj, w_main, dims="nt", out_dtype=F32, tm=1024, tn=512, tk=2048, name="dhn", add=dhn_f)
    d_wmain = _matmul(hn, dproj, dims="tn", out_dtype=F32, tm=1024, tn=512, tk=2048, name="dw_main")
    d_wfb = _matmul(hn, dfb, dims="tn", out_dtype=F32, tm=1024, tn=LANE, tk=512, name="dw_forget")
    grad_x, d_gain = _rms_bwd(x, norm_gain, dhn, dout, tr=tr, name="rms_x_bwd")

    fold = lambda g, reps: jnp.sum(g.reshape(reps, -1), axis=0, keepdims=True)
    return dict(
        sq=sq, grad_x=grad_x, d_wmain=d_wmain, d_wfb=d_wfb, d_wmk=d_wmk, d_wa=d_wa, d_wb=d_wb, d_wc=d_wc, d_wo=d_wo,
        d_gain=d_gain, d_mem_gain=d_mem_gain, d_bf=d_bf[:, :N_FORGET],
        d_qga=fold(d_qga, 12), d_kga=fold(d_kga, 4), d_sinks=d_sinks[:, :A_HEADS],
        d_qgb=fold(d_qgb, 12), d_kgb=fold(d_kgb, 12), d_qgc=fold(d_qgc, 4), d_kgc=fold(d_kgc, 4))


PACK_ROWS = 256
FORGET_IN_SHARD = FORGET_COL - SHARD_COLS
AFTER_FORGET = FORGET_COL - SLAB_START[1]
END_CHIP1 = 2 * SHARD_COLS - N_FORGET - SLAB_START[1]


def _pack_w_in(chip, w):
    rows = w.shape[0]
    tr = PACK_ROWS

    def body(k_ref, w_ref, o_ref, scr):
        scr[...] = jnp.zeros_like(scr)
        scr[:, pl.ds(0, SHARD_COLS)] = w_ref[...]
        v = scr[...]
        k = k_ref[0]
        col = lax.broadcasted_iota(jnp.int32, (tr, SLAB), 1)
        no_forget = jnp.zeros((tr, LANE), BF16)

        @pl.when(k == 0)
        def _():
            o_ref[:, 0:SLAB] = v.astype(BF16)
            o_ref[:, SLAB:] = no_forget

        @pl.when(k == 1)
        def _():
            before = pltpu.roll(v, SLAB_SHIFT[1], axis=1)
            after = pltpu.roll(v, SLAB - (N_FORGET - SLAB_SHIFT[1]), axis=1)
            slab = jnp.where(col < AFTER_FORGET, before, jnp.where(col < END_CHIP1, after, 0.0))
            o_ref[:, 0:SLAB] = slab.astype(BF16)
            f = pltpu.roll(v, SLAB - FORGET_IN_SHARD, axis=1)[:, :LANE]
            o_ref[:, SLAB:] = jnp.where(col[:, :LANE] < N_FORGET, f, 0.0).astype(BF16)

        for kk in (2, 3):
            @pl.when(k == kk)
            def _(kk=kk):
                o_ref[:, 0:SLAB] = pltpu.roll(v, SLAB_SHIFT[kk], axis=1).astype(BF16)
                o_ref[:, SLAB:] = no_forget

    return pl.pallas_call(
        body, grid_spec=pltpu.PrefetchScalarGridSpec(
            num_scalar_prefetch=1, grid=(rows // tr,),
            in_specs=[pl.BlockSpec((tr, SHARD_COLS), lambda i, k: (i, 0))],
            out_specs=pl.BlockSpec((None, tr, SLAB + LANE), lambda i, k: (k[0], i, 0)),
            scratch_shapes=[pltpu.VMEM((tr, SLAB), F32)]),
        out_shape=SDS((N_CHIPS, rows, SLAB + LANE), BF16), name="pack_w_in",
        compiler_params=_params(("arbitrary",)))(chip, w)


def _merge_slabs(g):
    rows = g.shape[1]
    tr = PACK_ROWS
    t = [s // LANE for s in SLAB_START]
    n_t = SLAB // LANE

    def body(g_ref, m_ref, f_ref):
        for k in range(N_CHIPS):
            lo = t[k] + (1 if k > 0 else 0)
            hi = t[k + 1] if k + 1 < N_CHIPS else t[k] + n_t
            m_ref[:, lo * LANE:hi * LANE] = g_ref[k, :, (lo - t[k]) * LANE:(hi - t[k]) * LANE]
            if k + 1 < N_CHIPS:
                a = g_ref[k, :, (hi - t[k]) * LANE:(hi - t[k] + 1) * LANE].astype(F32)
                b = g_ref[k + 1, :, 0:LANE].astype(F32)
                m_ref[:, hi * LANE:(hi + 1) * LANE] = (a + b).astype(BF16)
        f_ref[...] = g_ref[1, :, SLAB:]

    return pl.pallas_call(
        body, grid=(rows // tr,),
        in_specs=[pl.BlockSpec((N_CHIPS, tr, SLAB + LANE), lambda i: (0, i, 0))],
        out_specs=[_rowblk(tr, P_MAIN), _rowblk(tr, LANE)],
        out_shape=[SDS((rows, P_MAIN), BF16), SDS((rows, LANE), BF16)], name="merge_slabs",
        compiler_params=_params(("parallel",)))(g)


def _adamw_math(w, g, m, v):
    nm = ADAM_B1 * m + (1.0 - ADAM_B1) * g
    nv = ADAM_B2 * v + (1.0 - ADAM_B2) * (g * g)
    m_hat = nm / (1.0 - ADAM_B1 ** ADAM_STEP)
    v_hat = nv / (1.0 - ADAM_B2 ** ADAM_STEP)
    delta = -ADAM_LR * (m_hat / (jnp.sqrt(v_hat) + ADAM_EPS) + ADAM_WD * w)
    return delta, nm, nv


def _adamw(g, w, m, v, *, tr, name):
    rows, cols = w.shape
    tr = min(tr, rows)

    def body(g_ref, w_ref, m_ref, v_ref, d_ref, nm_ref, nv_ref):
        d, nm, nv = _adamw_math(w_ref[...], g_ref[...], m_ref[...], v_ref[...])
        d_ref[...] = d
        nm_ref[...] = nm
        nv_ref[...] = nv

    spec = _rowblk(tr, cols)
    return pl.pallas_call(
        body, grid=(rows // tr,), in_specs=[spec] * 4, out_specs=[spec] * 3,
        out_shape=[SDS((rows, cols), F32)] * 3, name=name, compiler_params=_params(("parallel",)))(g, w, m, v)


def _adamw_w_in(chip_core, slab_mine, slab_theirs, forget_mine, forget_theirs, w, m, v):
    rows = w.shape[0]
    tr = PACK_ROWS // 2
    nbh = rows // 2 // tr

    def body(k_ref, sa_ref, sb_ref, fa_ref, fb_ref, w_ref, m_ref, v_ref, g_ref, d_ref, nm_ref, nv_ref):
        use_mine = pl.program_id(0) // nbh == k_ref[1]
        sl = jnp.where(use_mine, sa_ref[...], sb_ref[...])
        f_tile = jnp.where(use_mine, fa_ref[...], fb_ref[...])
        k = k_ref[0]

        def emit(wide):
            g = wide[:, :SHARD_COLS]
            g_ref[...] = g
            d, nm, nv = _adamw_math(w_ref[...], g, m_ref[...], v_ref[...])
            d_ref[...] = d
            nm_ref[...] = nm
            nv_ref[...] = nv

        @pl.when(k == 0)
        def _():
            emit(sl)

        @pl.when(k == 1)
        def _():
            col = lax.broadcasted_iota(jnp.int32, (tr, SLAB), 1)
            before = pltpu.roll(sl, SLAB - SLAB_SHIFT[1], axis=1)
            after = pltpu.roll(sl, N_FORGET - SLAB_SHIFT[1], axis=1)
            wide_f = jnp.concatenate([f_tile, jnp.zeros((tr, SLAB - LANE), F32)], axis=1)
            forget = pltpu.roll(wide_f, FORGET_IN_SHARD, axis=1)
            emit(jnp.where(col < FORGET_IN_SHARD, before, jnp.where(col < FORGET_IN_SHARD + N_FORGET, forget, after)))

        for kk in (2, 3):
            @pl.when(k == kk)
            def _(kk=kk):
                emit(pltpu.roll(sl, SLAB - SLAB_SHIFT[kk], axis=1))

    nat = pl.BlockSpec((tr, SHARD_COLS), lambda i, k: (i, 0))
    half = lambda width: pl.BlockSpec((tr, width), lambda i, k: (i % nbh, 0))
    return pl.pallas_call(
        body, grid_spec=pltpu.PrefetchScalarGridSpec(
            num_scalar_prefetch=1, grid=(rows // tr,),
            in_specs=[half(SLAB), half(SLAB), half(LANE), half(LANE), nat, nat, nat],
            out_specs=[nat] * 4),
        out_shape=[SDS((rows, SHARD_COLS), F32)] * 4, name="adamw_w_in",
        compiler_params=_params(("arbitrary",)))(chip_core, slab_mine, slab_theirs, forget_mine, forget_theirs, w, m, v)


ANY = pl.BlockSpec(memory_space=pl.ANY)
HALF_AXIS = (0, 0, 1, 0, 0, 0, 1)


def _me():
    return lax.axis_index("x"), lax.axis_index("y"), lax.axis_index("c")


def _half(ref, which, axis):
    n = ref.shape[axis] // 2
    sl = pl.ds(which * n, n)
    return ref.at[sl] if axis == 0 else ref.at[:, sl]


def _piece(t, ref, j):
    if t == 0:
        return ref.at[:, pl.ds(SLAB_START[j], SLAB)]
    if t == 1:
        return ref
    if t in (2, 6):
        return ref.at[pl.ds(512 * j, 512)]
    return ref.at[:, pl.ds(512 * j, 512)]


def _piece_shape(t, shape):
    if t == 0:
        return (shape[0], SLAB)
    if t == 1:
        return shape
    if t in (2, 6):
        return (512, shape[1])
    return (shape[0], 512)


def _all_gather(slabs, parts):
    parts = [slabs] + list(parts)
    n = len(parts)

    def body(*refs):
        ins, outs = refs[:n], refs[n:2 * n]
        send, recv, fsend, frecv, lsem = refs[2 * n:]
        x, y, c = _me()
        k = 2 * x + y
        sib = (x, y, 1 - c)
        chips = [(1 - x, y), (x, 1 - y), (1 - x, 1 - y)]

        def rows(t, which):
            h = outs[t].shape[1] // 2
            return pl.ds(which * h, h)

        def mine(t):
            return ins[t].at[k, rows(t, c)] if t == 0 else ins[t].at[rows(t, c)]

        local = [pltpu.make_async_copy(ins[t], outs[t].at[k], lsem.at[t]) for t in range(1, n)]
        for cp in local:
            cp.start()

        def first(t, j, chip):
            return pltpu.make_async_remote_copy(
                src_ref=mine(t), dst_ref=outs[t].at[k, rows(t, c)],
                send_sem=send.at[t, j], recv_sem=recv.at[t, j], device_id=(chip[0], chip[1], c), device_id_type=MESH)

        def passed(t, j, chip, which):
            kj = 2 * chip[0] + chip[1]
            blk = outs[t].at[kj, rows(t, which)]
            return pltpu.make_async_remote_copy(
                src_ref=blk, dst_ref=blk, send_sem=fsend.at[t, j], recv_sem=frecv.at[t, j],
                device_id=sib, device_id_type=MESH)

        firsts = [first(t, j, chip) for j, chip in enumerate(chips) for t in range(n)]
        for cp in firsts:
            cp.start()
        passes = []
        for j, chip in enumerate(chips):
            for t in range(n):
                kj = 2 * chip[0] + chip[1]
                pltpu.make_async_remote_copy(
                    src_ref=mine(t), dst_ref=outs[t].at[kj, rows(t, c)],
                    send_sem=send.at[t, j], recv_sem=recv.at[t, j], device_id=(chip[0], chip[1], c),
                    device_id_type=MESH).wait_recv()
                cp = passed(t, j, chip, c)
                cp.start()
                passes.append(cp)
        for j, chip in enumerate(chips):
            for t in range(n):
                passed(t, j, chip, 1 - c).wait_recv()
        for cp in firsts + passes:
            cp.wait_send()
        for cp in local:
            cp.wait()

    return pl.pallas_call(
        body, in_specs=[ANY] * n, out_specs=[ANY] * n,
        out_shape=[SDS(slabs.shape, slabs.dtype)] + [SDS((N_CHIPS,) + p.shape, p.dtype) for p in parts[1:]],
        scratch_shapes=[pltpu.SemaphoreType.DMA((n, 3))] * 4 + [pltpu.SemaphoreType.DMA((n,))],
        input_output_aliases={0: 0}, name="all_gather_weights")(*parts)


def _exchange_halves(arrs):
    n = len(arrs)

    def body(*refs):
        ins, outs = refs[:n], refs[n:2 * n]
        send, recv = refs[2 * n:]
        x, y, c = _me()
        cps = [pltpu.make_async_remote_copy(
            src_ref=_half(ins[t], 1 - c, HALF_AXIS[t]), dst_ref=outs[t], send_sem=send.at[t], recv_sem=recv.at[t],
            device_id=(x, y, 1 - c), device_id_type=MESH) for t in range(n)]
        for cp in cps:
            cp.start()
        for cp in cps:
            cp.wait()

    def hshape(t):
        s = list(arrs[t].shape)
        s[HALF_AXIS[t]] //= 2
        return tuple(s)

    return pl.pallas_call(
        body, in_specs=[ANY] * n, out_specs=[ANY] * n,
        out_shape=[SDS(hshape(t), arrs[t].dtype) for t in range(n)],
        scratch_shapes=[pltpu.SemaphoreType.DMA((n,))] * 2, name="exchange_halves")(*arrs)


def _add_half(full, got, core, axis, *, name):
    r, c = got.shape
    br, bc = (256 if r % 256 == 0 else 128), min(2048, c)
    off_r = (r // br) if axis == 0 else 0
    off_c = (c // bc) if axis == 1 else 0

    def body(c_ref, a_ref, b_ref, o_ref):
        o_ref[...] = (a_ref[...] + b_ref[...]).astype(BF16)

    return pl.pallas_call(
        body, grid_spec=pltpu.PrefetchScalarGridSpec(
            num_scalar_prefetch=1, grid=(r // br, c // bc),
            in_specs=[pl.BlockSpec((br, bc), lambda i, j, cr: (i + cr[0] * off_r, j + cr[0] * off_c)),
                      pl.BlockSpec((br, bc), lambda i, j, cr: (i, j))],
            out_specs=pl.BlockSpec((br, bc), lambda i, j, cr: (i, j))),
        out_shape=SDS((r, c), BF16), name=name, compiler_params=_params(("parallel", "parallel")))(core, full, got)


def _scatter_pieces(halves):
    n = len(halves)

    def body(*refs):
        ins, outs = refs[:n], refs[n:2 * n]
        send, recv, lsem = refs[2 * n:]
        x, y, c = _me()
        k = 2 * x + y

        def to_chip(t, j):
            return pltpu.make_async_remote_copy(
                src_ref=_piece(t, ins[t], j), dst_ref=outs[t].at[k], send_sem=send.at[t, j], recv_sem=recv.at[t, k],
                device_id=(j // 2, j % 2, c), device_id_type=MESH)

        def from_chip(t, j):
            return pltpu.make_async_remote_copy(
                src_ref=_piece(t, ins[t], j), dst_ref=outs[t].at[j], send_sem=send.at[t, j], recv_sem=recv.at[t, j],
                device_id=(j // 2, j % 2, c), device_id_type=MESH)

        def own(t, j):
            return pltpu.make_async_copy(_piece(t, ins[t], j), outs[t].at[j], lsem.at[t])

        for j in range(N_CHIPS):
            @pl.when(k != j)
            def _(j=j):
                for t in range(n):
                    to_chip(t, j).start()

            @pl.when(k == j)
            def _(j=j):
                for t in range(n):
                    own(t, j).start()

        for j in range(N_CHIPS):
            @pl.when(k != j)
            def _(j=j):
                for t in range(n):
                    from_chip(t, j).wait_recv()
                for t in range(n):
                    to_chip(t, j).wait_send()

            @pl.when(k == j)
            def _(j=j):
                for t in range(n):
                    own(t, j).wait()

    return pl.pallas_call(
        body, in_specs=[ANY] * n, out_specs=[ANY] * n,
        out_shape=[SDS((N_CHIPS,) + _piece_shape(t, halves[t].shape), halves[t].dtype) for t in range(n)],
        scratch_shapes=[pltpu.SemaphoreType.DMA((n, N_CHIPS))] * 2 + [pltpu.SemaphoreType.DMA((n,))],
        name="scatter_pieces")(*halves)


def _sum4(p, *, name):
    _, r, c = p.shape
    br = 256 if r % 256 == 0 else 128

    def body(p_ref, o_ref):
        o_ref[...] = ((p_ref[0].astype(F32) + p_ref[1].astype(F32)) + p_ref[2].astype(F32)) + p_ref[3].astype(F32)

    return pl.pallas_call(
        body, grid=(r // br,), in_specs=[pl.BlockSpec((N_CHIPS, br, c), lambda i: (0, i, 0))],
        out_specs=_rowblk(br, c), out_shape=SDS((r, c), F32), name=name, compiler_params=_params(("parallel",)))(p)


def _swap_halves(sums):
    n = len(sums)

    def body(*refs):
        ins, outs = refs[:n], refs[n:2 * n]
        send, recv = refs[2 * n:]
        x, y, c = _me()
        cps = [pltpu.make_async_remote_copy(
            src_ref=ins[t], dst_ref=outs[t], send_sem=send.at[t], recv_sem=recv.at[t],
            device_id=(x, y, 1 - c), device_id_type=MESH) for t in range(n)]
        for cp in cps:
            cp.start()
        for cp in cps:
            cp.wait()

    return pl.pallas_call(
        body, in_specs=[ANY] * n, out_specs=[ANY] * n,
        out_shape=[SDS(s.shape, F32) for s in sums],
        scratch_shapes=[pltpu.SemaphoreType.DMA((n,))] * 2, name="swap_halves")(*sums)


def _adamw_halves(mine, theirs, core, w, m, v, *, axis, tr, name):
    rows, cols = w.shape

    if axis == 0:
        nbh = rows // 2 // tr
        g_spec = pl.BlockSpec((tr, cols), lambda i, cr: (i % nbh, 0))
    else:
        g_spec = pl.BlockSpec((tr, cols // 2), lambda i, cr: (i, 0))

    def body(c_ref, a_ref, b_ref, w_ref, m_ref, v_ref, g_ref, d_ref, nm_ref, nv_ref):
        a, b = a_ref[...], b_ref[...]
        if axis == 0:
            g = jnp.where(pl.program_id(0) // nbh == c_ref[0], a, b)
        else:
            low = c_ref[0] == 0
            g = jnp.concatenate([jnp.where(low, a, b), jnp.where(low, b, a)], axis=1)
        g_ref[...] = g
        d, nm, nv = _adamw_math(w_ref[...], g, m_ref[...], v_ref[...])
        d_ref[...] = d
        nm_ref[...] = nm
        nv_ref[...] = nv

    nat = pl.BlockSpec((tr, cols), lambda i, cr: (i, 0))
    return pl.pallas_call(
        body, grid_spec=pltpu.PrefetchScalarGridSpec(
            num_scalar_prefetch=1, grid=(rows // tr,), in_specs=[g_spec, g_spec, nat, nat, nat], out_specs=[nat] * 4),
        out_shape=[SDS((rows, cols), F32)] * 4, name=name, compiler_params=_params(("arbitrary",)))(
            core, mine, theirs, w, m, v)


SMALL_ROWS, SMALL_COLS = 8, 1024


def _pack_small(vs):
    flat = jnp.concatenate([v.reshape(-1) for v in vs])
    return jnp.pad(flat, (0, SMALL_ROWS * SMALL_COLS - flat.shape[0])).reshape(SMALL_ROWS, SMALL_COLS)


def _unpack_small(packed, sizes):
    flat = packed.reshape(-1)
    out, o = [], 0
    for n in sizes:
        out.append(flat[o:o + n].reshape(1, n))
        o += n
    return out


def _all_reduce_small(v):
    n_dev = 8

    def body(v_ref, o_ref, land, send, recv):
        x, y, c = _me()
        me = 4 * x + 2 * y + c
        land[me] = v_ref[...]
        cps = []
        for r in range(1, n_dev):
            fx, fy, fc = (r >> 2) & 1, (r >> 1) & 1, r & 1
            peer = (x ^ fx, y ^ fy, c ^ fc)
            cps.append(pltpu.make_async_remote_copy(
                src_ref=v_ref, dst_ref=land.at[me], send_sem=send.at[r - 1], recv_sem=recv.at[r - 1],
                device_id=peer, device_id_type=MESH))
        for cp in cps:
            cp.start()
        for r in range(1, n_dev):
            fx, fy, fc = (r >> 2) & 1, (r >> 1) & 1, r & 1
            src = 4 * (x ^ fx) + 2 * (y ^ fy) + (c ^ fc)
            pltpu.make_async_remote_copy(
                src_ref=v_ref, dst_ref=land.at[src], send_sem=send.at[r - 1], recv_sem=recv.at[r - 1],
                device_id=(x ^ fx, y ^ fy, c ^ fc), device_id_type=MESH).wait_recv()
        for cp in cps:
            cp.wait_send()
        acc = land[0]
        for r in range(1, n_dev):
            acc = acc + land[r]
        o_ref[...] = acc

    vm = pl.BlockSpec(memory_space=pltpu.VMEM)
    return pl.pallas_call(
        body, in_specs=[vm], out_specs=vm, out_shape=SDS(v.shape, F32),
        scratch_shapes=[pltpu.VMEM((n_dev,) + v.shape, F32), pltpu.SemaphoreType.DMA((n_dev - 1,)),
                        pltpu.SemaphoreType.DMA((n_dev - 1,))],
        name="all_reduce_small")(v)


def kernel(x, mem, norm_gain, mem_norm_gain, w_in, b_forget, q_gain_a, k_gain_a, sinks_a, q_gain_b, k_gain_b, q_gain_c, k_gain_c, w_mem_kv, w_branch_a, w_branch_b, w_branch_c, w_out, loss_target, m_norm_gain, m_mem_norm_gain, m_w_in, m_b_forget, m_q_gain_a, m_k_gain_a, m_sinks_a, m_q_gain_b, m_k_gain_b, m_q_gain_c, m_k_gain_c, m_w_mem_kv, m_w_branch_a, m_w_branch_b, m_w_branch_c, m_w_out, v_norm_gain, v_mem_norm_gain, v_w_in, v_b_forget, v_q_gain_a, v_k_gain_a, v_sinks_a, v_q_gain_b, v_k_gain_b, v_q_gain_c, v_k_gain_c, v_w_mem_kv, v_w_branch_a, v_w_branch_b, v_w_branch_c, v_w_out):
    xi, yi, ci = lax.axis_index("x"), lax.axis_index("y"), lax.axis_index("c")
    chip = jnp.reshape(2 * xi + yi, (1,)).astype(jnp.int32)
    core = jnp.reshape(ci, (1,)).astype(jnp.int32)

    slabs = _pack_w_in(chip, w_in[0])
    mine = [w_mem_kv[0].astype(BF16), w_branch_a[0].astype(BF16), w_branch_b[0].astype(BF16),
            w_branch_c[0].astype(BF16), w_out[0].astype(BF16)]
    g_slab, g_mk, g_wa, g_wb, g_wc, g_wo = _all_gather(slabs, mine)
    w_main, w_fb = _merge_slabs(g_slab)
    w_mk = g_mk.reshape(D_MODEL, 1024)
    wo = g_wo.reshape(D_MODEL, D_MODEL)

    r = _local_step(x[0], mem[0], loss_target[0], w_main, w_fb, w_mk, g_wa, g_wb, g_wc, wo, norm_gain, mem_norm_gain,
                    b_forget, q_gain_a, k_gain_a, sinks_a, q_gain_b, k_gain_b, q_gain_c, k_gain_c)

    grads = [r["d_wmain"], r["d_wfb"], r["d_wmk"], r["d_wa"], r["d_wb"], r["d_wc"], r["d_wo"]]
    got = _exchange_halves(grads)
    halves = [_add_half(g, h, core, ax, name=f"add_half_{t}") for t, (g, h, ax) in enumerate(zip(grads, got, HALF_AXIS))]
    parts = _scatter_pieces(halves)
    sums = [_sum4(p, name=f"sum4_{t}") for t, p in enumerate(parts)]
    theirs = _swap_halves(sums)

    small_names = ["d_gain", "d_mem_gain", "d_bf", "d_qga", "d_kga", "d_sinks", "d_qgb", "d_kgb", "d_qgc", "d_kgc"]
    loss_part = (0.5 / D_MODEL) * jnp.sum(r["sq"], axis=1, keepdims=True)
    packed = _pack_small([r[n] for n in small_names] + [loss_part])
    red = _all_reduce_small(packed)
    small_w = [norm_gain, mem_norm_gain, b_forget, q_gain_a, k_gain_a, sinks_a, q_gain_b, k_gain_b, q_gain_c, k_gain_c]
    small_m = [m_norm_gain, m_mem_norm_gain, m_b_forget, m_q_gain_a, m_k_gain_a, m_sinks_a, m_q_gain_b, m_k_gain_b,
               m_q_gain_c, m_k_gain_c]
    small_v = [v_norm_gain, v_mem_norm_gain, v_b_forget, v_q_gain_a, v_k_gain_a, v_sinks_a, v_q_gain_b, v_k_gain_b,
               v_q_gain_c, v_k_gain_c]
    sizes = [w.shape[1] for w in small_w]
    s_d, s_m, s_v = _adamw(red, _pack_small(small_w), _pack_small(small_m), _pack_small(small_v), tr=8, name="adamw_small")
    g_small = _unpack_small(red, sizes + [1])
    loss = g_small[-1].reshape(())
    d_small, m_small, v_small = _unpack_small(s_d, sizes), _unpack_small(s_m, sizes), _unpack_small(s_v, sizes)

    gw_in, dw_in, mw_in, vw_in = _adamw_w_in(jnp.concatenate([chip, core]), sums[0], theirs[0], sums[1], theirs[1],
                                             w_in[0], m_w_in[0], v_w_in[0])
    big = {}
    for t, nm, w, m, v in ((2, "w_mem_kv", w_mem_kv, m_w_mem_kv, v_w_mem_kv),
                           (3, "w_branch_a", w_branch_a, m_w_branch_a, v_w_branch_a),
                           (4, "w_branch_b", w_branch_b, m_w_branch_b, v_w_branch_b),
                           (5, "w_branch_c", w_branch_c, m_w_branch_c, v_w_branch_c),
                           (6, "w_out", w_out, m_w_out, v_w_out)):
        big[nm] = _adamw_halves(sums[t], theirs[t], core, w[0], m[0], v[0], axis=HALF_AXIS[t], tr=128,
                                name="adamw_" + nm)

    def collect(kind):
        sm = (g_small, d_small, m_small, v_small)[kind]
        win = (gw_in, dw_in, mw_in, vw_in)[kind]
        return ([sm[0], sm[1], win[None]] + [a for a in sm[2:10]]
                + [big[n][kind][None] for n in ("w_mem_kv", "w_branch_a", "w_branch_b", "w_branch_c", "w_out")])

    return (loss, r["grad_x"][None], *collect(0), *collect(1), *collect(2), *collect(3))
```

```python
import functools

import numpy as np
import jax
import jax.numpy as jnp
from jax import lax
from jax.experimental import pallas as pl
from jax.experimental.pallas import tpu as pltpu

F32 = jnp.float32
BF16 = jnp.bfloat16
HI = lax.Precision.HIGHEST
SDS = jax.ShapeDtypeStruct
MESH = pl.DeviceIdType.MESH

D_MODEL = 2048
HEAD_DIM = 64
A_HEADS = 12
A_GROUP = 3
B_HEADS = 12
C_HEADS = 4
C_HEAD_DIM = 128
WINDOW = 128
EPS = 1e-6
NEG = -1e30
LANE = 128

QA, KA, VA, ZA = 0, 768, 1024, 1280
QB, KB, VB, ZB = 2048, 2816, 3584, 4352
QC, ZC = 5120, 5632
GATE = 6144
P_MAIN = 12288
N_FORGET = 12
FORGET_COL = 5120
SHARD_COLS = 3075
SLAB = 3200
SLAB_START = (0, 3072, 6016, 9088)
SLAB_SHIFT = (0, 3, 122, 125)
N_CHIPS = 4

ADAM_LR = 0.001
ADAM_B1 = 0.9
ADAM_B2 = 0.999
ADAM_EPS = 1e-08
ADAM_WD = 0.01
ADAM_STEP = 10

VMEM_LIMIT = 56 * 1024 * 1024


def _params(sem, vmem=VMEM_LIMIT):
    return pltpu.CompilerParams(dimension_semantics=sem, vmem_limit_bytes=vmem)


def _win(tr, width, off):
    return pl.BlockSpec((pl.Element(tr), pl.Element(width)), lambda i, *_: (i * tr, off))


def _rowblk(tr, width):
    return pl.BlockSpec((tr, width), lambda i, *_: (i, 0))


def _const(shape):
    nd = len(shape)
    return pl.BlockSpec(shape, lambda *_: (0,) * nd)


def _rms(x, g):
    return x * lax.rsqrt(jnp.mean(x * x, axis=-1, keepdims=True) + EPS) * g


def _head_norm(x, g_tiled, bd):
    ms = jnp.dot(x * x, bd, precision=HI, preferred_element_type=F32)
    return x * lax.rsqrt(ms + EPS) * g_tiled


def _silu(z):
    return z * jax.nn.sigmoid(z)


def _dot_nt(a, b):
    return lax.dot_general(a, b, (((1,), (1,)), ((), ())), preferred_element_type=F32)


def _dot_tn(a, b):
    return lax.dot_general(a, b, (((0,), (0,)), ((), ())), preferred_element_type=F32)


def _dot(a, b):
    return jnp.dot(a, b, preferred_element_type=F32)


def _swa_fn(qk, vz, qkp, vzp, qg, kg, sinks, bd, first):
    q = _head_norm(qk[:, :768], qg, bd)
    k2 = jnp.concatenate([qkp[:, 768:], qk[:, 768:]], axis=0)
    k2 = _head_norm(k2, kg, bd[:256, :256])
    v2 = jnp.concatenate([vzp[:, :256], vz[:, :256]], axis=0)
    z = vz[:, 256:]
    qi = lax.broadcasted_iota(jnp.int32, (WINDOW, 2 * WINDOW), 0)
    kj = lax.broadcasted_iota(jnp.int32, (WINDOW, 2 * WINDOW), 1)
    rel = qi + WINDOW - kj
    valid = (rel >= 0) & (rel < WINDOW) & (kj >= WINDOW * first.astype(jnp.int32))
    relf = rel.astype(F32)
    qb = q.astype(BF16)
    kb = k2.astype(BF16)
    vb = v2.astype(BF16)
    outs = []
    for h in range(A_HEADS):
        kv = h // A_GROUP
        s = _dot_nt(qb[:, 64 * h:64 * h + 64], kb[:, 64 * kv:64 * kv + 64]) * (HEAD_DIM ** -0.5)
        slope = float(2.0 ** (-8.0 * (h + 1) / A_HEADS))
        s = jnp.where(valid, s - slope * relf, NEG)
        sink = sinks[:, h:h + 1]
        m = lax.stop_gradient(jnp.maximum(jnp.max(s, axis=-1, keepdims=True), sink))
        p = jnp.exp(s - m)
        den = jnp.sum(p, axis=-1, keepdims=True) + jnp.exp(sink - m)
        outs.append(_dot((p / den).astype(BF16), vb[:, 64 * kv:64 * kv + 64]))
    return jnp.concatenate(outs, axis=1) * _silu(z)


def _mem_fn(qz, mkv, qg, kg, bd):
    q = _head_norm(qz[:, :512], qg, bd).astype(BF16)
    k = _head_norm(mkv[:, :512], kg, bd).astype(BF16)
    v = mkv[:, 512:].astype(BF16)
    z = qz[:, 512:]
    outs = []
    for h in range(C_HEADS):
        sl = slice(128 * h, 128 * h + 128)
        s = _dot_nt(q[:, sl], k[:, sl]) * (C_HEAD_DIM ** -0.5)
        m = lax.stop_gradient(jnp.max(s, axis=-1, keepdims=True))
        p = jnp.exp(s - m)
        den = jnp.sum(p, axis=-1, keepdims=True)
        outs.append(_dot((p / den).astype(BF16), v[:, sl]))
    return jnp.concatenate(outs, axis=1) * _silu(z)


def _qn_fn(q, g, bd):
    return _head_norm(q, g, bd) * (HEAD_DIM ** -0.5)


def _kn_fn(k, g, bd):
    return _head_norm(k, g, bd)


def _block_diag(width, hd):
    i = np.arange(width) // hd
    return jnp.asarray((i[:, None] == i[None, :]).astype(np.float32) / hd)


def _head_sum(width, hd):
    i = np.arange(width) // hd
    return jnp.asarray((i[:, None] == np.arange(LANE)[None, :]).astype(np.float32))


def _rms_fwd(x, g, *, tr, name):
    rows, dm = x.shape

    def body(x_ref, g_ref, o_ref):
        o_ref[...] = _rms(x_ref[...], g_ref[...]).astype(BF16)

    return pl.pallas_call(
        body, grid=(rows // tr,),
        in_specs=[_rowblk(tr, dm), _const((1, dm))],
        out_specs=_rowblk(tr, dm),
        out_shape=SDS((rows, dm), BF16), name=name,
        compiler_params=_params(("parallel",)))(x, g)


def _rms_bwd(x, g, dy, resid, *, tr, name):
    rows, dm = x.shape
    want_dx = resid is not None

    def body(*refs):
        if want_dx:
            x_ref, g_ref, dy_ref, r_ref, dx_ref, dg_ref = refs
        else:
            x_ref, g_ref, dy_ref, dg_ref = refs
        _, vjp = jax.vjp(_rms, x_ref[...], g_ref[...])
        dx, dg = vjp(dy_ref[...])

        @pl.when(pl.program_id(0) == 0)
        def _():
            dg_ref[...] = jnp.zeros_like(dg_ref)

        dg_ref[...] += dg
        if want_dx:
            dx_ref[...] = r_ref[...] + dx

    ins = [x, g, dy] + ([resid] if want_dx else [])
    in_specs = [_rowblk(tr, dm), _const((1, dm)), _rowblk(tr, dm)] + ([_rowblk(tr, dm)] if want_dx else [])
    out_specs = ([_rowblk(tr, dm)] if want_dx else []) + [_const((1, dm))]
    out_shape = ([SDS((rows, dm), F32)] if want_dx else []) + [SDS((1, dm), F32)]
    return pl.pallas_call(
        body, grid=(rows // tr,), in_specs=in_specs, out_specs=out_specs, out_shape=out_shape, name=name,
        compiler_params=_params(("arbitrary",)))(*ins)


def _matmul(a, b, *, dims, out_dtype, tm, tn, tk, name, add=None):
    if dims == "tn":
        kdim, m = a.shape
    else:
        m, kdim = a.shape
    n = b.shape[0] if dims == "nt" else b.shape[1]
    tm, tn, tk = min(tm, m), min(tn, n), min(tk, kdim)
    assert m % tm == 0 and n % tn == 0 and kdim % tk == 0, (name, a.shape, b.shape)
    nk = kdim // tk
    has_add = add is not None

    def body(*refs):
        if has_add:
            a_ref, b_ref, add_ref, o_ref, acc = refs
        else:
            a_ref, b_ref, o_ref, acc = refs
        k = pl.program_id(2)
        if dims == "nn":
            part = _dot(a_ref[...], b_ref[...])
        elif dims == "nt":
            part = _dot_nt(a_ref[...], b_ref[...])
        else:
            part = _dot_tn(a_ref[...], b_ref[...])

        @pl.when(k == 0)
        def _():
            acc[...] = part + add_ref[...] if has_add else part

        @pl.when(k > 0)
        def _():
            acc[...] += part

        @pl.when(k == nk - 1)
        def _():
            o_ref[...] = acc[...].astype(out_dtype)

    a_spec = pl.BlockSpec((tk, tm), lambda i, j, k: (k, i)) if dims == "tn" else pl.BlockSpec((tm, tk), lambda i, j, k: (i, k))
    b_spec = pl.BlockSpec((tn, tk), lambda i, j, k: (j, k)) if dims == "nt" else pl.BlockSpec((tk, tn), lambda i, j, k: (k, j))
    o_spec = pl.BlockSpec((tm, tn), lambda i, j, k: (i, j))
    ins = [a, b] + ([add] if has_add else [])
    in_specs = [a_spec, b_spec] + ([o_spec] if has_add else [])
    return pl.pallas_call(
        body, grid=(m // tm, n // tn, nk), in_specs=in_specs, out_specs=o_spec,
        out_shape=SDS((m, n), out_dtype), scratch_shapes=[pltpu.VMEM((tm, tn), F32)], name=name,
        compiler_params=_params(("parallel", "parallel", "arbitrary")))(*ins)


def _swa_specs(nb):
    prev = lambda off: pl.BlockSpec((pl.Element(WINDOW), pl.Element(1024)),
                                    lambda n: (jnp.maximum(n - 1, 0) * WINDOW, off))
    return [_win(WINDOW, 1024, QA), _win(WINDOW, 1024, VA), prev(QA), prev(VA),
            _const((1, 768)), _const((1, 256)), _const((1, LANE)), _const((768, 768))]


def _swa_fwd(proj, qg, kg, sinks, bd):
    s = proj.shape[0]
    nb = s // WINDOW

    def body(qk_ref, vz_ref, qkp_ref, vzp_ref, qg_ref, kg_ref, sk_ref, bd_ref, o_ref):
        first = pl.program_id(0) == 0
        o_ref[...] = _swa_fn(qk_ref[...], vz_ref[...], qkp_ref[...], vzp_ref[...], qg_ref[...], kg_ref[...],
                             sk_ref[...], bd_ref[...], first).astype(BF16)

    return pl.pallas_call(
        body, grid=(nb,), in_specs=_swa_specs(nb), out_specs=_rowblk(WINDOW, 768),
        out_shape=SDS((s, 768), BF16), name="swa_fwd",
        compiler_params=_params(("parallel",)))(proj, proj, proj, proj, qg, kg, sinks, bd)


def _swa_bwd(proj, qg, kg, sinks, bd, dga):
    s = proj.shape[0]
    nb = s // WINDOW

    def body(qk_ref, vz_ref, qkp_ref, vzp_ref, qg_ref, kg_ref, sk_ref, bd_ref, dg_ref,
             dcur_ref, dprev_ref, dqg_ref, dkg_ref, dsk_ref):
        first = pl.program_id(0) == 0
        bd_v = bd_ref[...]
        fn = lambda qk, vz, qkp, vzp, qg_, kg_, sk: _swa_fn(qk, vz, qkp, vzp, qg_, kg_, sk, bd_v, first)
        _, vjp = jax.vjp(fn, qk_ref[...], vz_ref[...], qkp_ref[...], vzp_ref[...], qg_ref[...], kg_ref[...], sk_ref[...])
        dqk, dvz, dqkp, dvzp, dqg, dkg, dsk = vjp(dg_ref[...])

        @pl.when(first)
        def _():
            dqg_ref[...] = jnp.zeros_like(dqg_ref)
            dkg_ref[...] = jnp.zeros_like(dkg_ref)
            dsk_ref[...] = jnp.zeros_like(dsk_ref)

        dqg_ref[...] += dqg
        dkg_ref[...] += dkg
        dsk_ref[...] += dsk
        dcur_ref[...] = jnp.concatenate([dqk, dvz], axis=1)
        dprev_ref[...] = jnp.concatenate([dqkp[:, 768:], dvzp[:, :256]], axis=1)

    return pl.pallas_call(
        body, grid=(nb,), in_specs=_swa_specs(nb) + [_rowblk(WINDOW, 768)],
        out_specs=[_rowblk(WINDOW, 2048), pl.BlockSpec((None, WINDOW, 512), lambda n: (n, 0, 0)),
                   _const((1, 768)), _const((1, 256)), _const((1, LANE))],
        out_shape=[SDS((s, 2048), F32), SDS((nb, WINDOW, 512), F32), SDS((1, 768), F32), SDS((1, 256), F32),
                   SDS((1, LANE), F32)],
        name="swa_bwd", compiler_params=_params(("arbitrary",)))(proj, proj, proj, proj, qg, kg, sinks, bd, dga)


def _swa_combine(dcur, dprev):
    s = dcur.shape[0]
    nb = s // WINDOW

    def body(c_ref, p_ref, o_ref):
        c = c_ref[...]
        nxt = jnp.where(pl.program_id(0) == nb - 1, 0.0, p_ref[...])
        o_ref[...] = jnp.concatenate([c[:, :768], c[:, 768:1280] + nxt, c[:, 1280:]], axis=1).astype(BF16)

    return pl.pallas_call(
        body, grid=(nb,),
        in_specs=[_rowblk(WINDOW, 2048), pl.BlockSpec((None, WINDOW, 512), lambda n: (jnp.minimum(n + 1, nb - 1), 0, 0))],
        out_specs=_rowblk(WINDOW, 2048), out_shape=SDS((s, 2048), BF16), name="swa_combine",
        compiler_params=_params(("parallel",)))(dcur, dprev)


def _mem_fwd(proj, mkv, qg, kg, bd, *, tr):
    s = proj.shape[0]

    def body(qz_ref, mkv_ref, qg_ref, kg_ref, bd_ref, o_ref):
        o_ref[...] = _mem_fn(qz_ref[...], mkv_ref[...], qg_ref[...], kg_ref[...], bd_ref[...]).astype(BF16)

    return pl.pallas_call(
        body, grid=(s // tr,),
        in_specs=[_win(tr, 1024, QC), _const(mkv.shape), _const((1, 512)), _const((1, 512)), _const((512, 512))],
        out_specs=_rowblk(tr, 512), out_shape=SDS((s, 512), BF16), name="mem_fwd",
        compiler_params=_params(("parallel",)))(proj, mkv, qg, kg, bd)


def _mem_bwd(proj, mkv, qg, kg, bd, dgc, *, tr):
    s = proj.shape[0]

    def body(qz_ref, mkv_ref, qg_ref, kg_ref, bd_ref, dg_ref, dqz_ref, dmkv_ref, dqg_ref, dkg_ref):
        bd_v = bd_ref[...]
        fn = lambda qz, mkv_, qg_, kg_: _mem_fn(qz, mkv_, qg_, kg_, bd_v)
        _, vjp = jax.vjp(fn, qz_ref[...], mkv_ref[...], qg_ref[...], kg_ref[...])
        dqz, dmkv, dqg, dkg = vjp(dg_ref[...])

        @pl.when(pl.program_id(0) == 0)
        def _():
            dmkv_ref[...] = jnp.zeros_like(dmkv_ref)
            dqg_ref[...] = jnp.zeros_like(dqg_ref)
            dkg_ref[...] = jnp.zeros_like(dkg_ref)

        dmkv_ref[...] += dmkv
        dqg_ref[...] += dqg
        dkg_ref[...] += dkg
        dqz_ref[...] = dqz.astype(BF16)

    return pl.pallas_call(
        body, grid=(s // tr,),
        in_specs=[_win(tr, 1024, QC), _const(mkv.shape), _const((1, 512)), _const((1, 512)), _const((512, 512)),
                  _rowblk(tr, 512)],
        out_specs=[_rowblk(tr, 1024), _const(mkv.shape), _const((1, 512)), _const((1, 512))],
        out_shape=[SDS((s, 1024), BF16), SDS(mkv.shape, F32), SDS((1, 512), F32), SDS((1, 512), F32)],
        name="mem_bwd", compiler_params=_params(("arbitrary",)))(proj, mkv, qg, kg, bd, dgc)


def _log_sigmoid(x):
    return jnp.minimum(x, 0.0) - jnp.log1p(jnp.exp(-jnp.abs(x)))


def _fox_prep(proj, fbl, qg, kg, bfor, bd, *, tr):
    s = proj.shape[0]
    tri = jnp.asarray(np.tril(np.ones((tr, tr), np.float32)))

    def body(q_ref, k_ref, fb_ref, qg_ref, kg_ref, bf_ref, bd_ref, tri_ref, qn_ref, kn_ref, cq_ref, ck_ref, carry):
        @pl.when(pl.program_id(0) == 0)
        def _():
            carry[...] = jnp.zeros_like(carry)

        bd_v = bd_ref[...]
        qn_ref[...] = _qn_fn(q_ref[...], qg_ref[...], bd_v).astype(BF16)
        kn_ref[...] = _kn_fn(k_ref[...], kg_ref[...], bd_v).astype(BF16)
        lane = lax.broadcasted_iota(jnp.int32, (tr, LANE), 1)
        logf = jnp.where(lane < N_FORGET, _log_sigmoid(fb_ref[...] + bf_ref[...]), 0.0)
        c = jnp.dot(tri_ref[...], logf, precision=HI, preferred_element_type=F32) + carry[...]
        cq_ref[...] = c
        ck_ref[...] = jnp.transpose(c)
        carry[...] = c[tr - 1:tr, :]

    return pl.pallas_call(
        body, grid=(s // tr,),
        in_specs=[_win(tr, 768, QB), _win(tr, 768, KB), _rowblk(tr, LANE), _const((1, 768)), _const((1, 768)),
                  _const((1, LANE)), _const((768, 768)), _const((tr, tr))],
        out_specs=[_rowblk(tr, 768), _rowblk(tr, 768), _rowblk(tr, LANE), pl.BlockSpec((LANE, tr), lambda i: (0, i))],
        out_shape=[SDS((s, 768), BF16), SDS((s, 768), BF16), SDS((s, LANE), F32), SDS((LANE, s), F32)],
        scratch_shapes=[pltpu.VMEM((1, LANE), F32)], name="fox_prep",
        compiler_params=_params(("arbitrary",)))(proj, proj, fbl, qg, kg, bfor, bd, tri)


def _fox_tiles(s):
    tq = min(256, s)
    tk = min(512, s)
    return tq, tk


def _fox_fwd(proj, qn, kn, cq, ck):
    s = proj.shape[0]
    tq, tk = _fox_tiles(s)
    nq, nk = s // tq, s // tk

    def last_k(i):
        return (i * tq + tq - 1) // tk

    def body(q_ref, k_ref, v_ref, cq_ref, ck_ref, z_ref, gb_ref, yb_ref, lse_ref, acc, m_s, l_s):
        i, j = pl.program_id(0), pl.program_id(1)

        @pl.when(j == 0)
        def _():
            acc[...] = jnp.zeros_like(acc)
            m_s[...] = jnp.full_like(m_s, NEG)
            l_s[...] = jnp.ones_like(l_s)

        @pl.when(j <= last_k(i))
        def _():
            qpos = i * tq + lax.broadcasted_iota(jnp.int32, (tq, tk), 0)
            kpos = j * tk + lax.broadcasted_iota(jnp.int32, (tq, tk), 1)
            mask = kpos <= qpos
            q, k = q_ref[...], k_ref[...]
            v = v_ref[...].astype(BF16)
            cqv, ckv = cq_ref[...], ck_ref[...]
            m_all, l_all = m_s[...], l_s[...]
            lane = lax.broadcasted_iota(jnp.int32, (tq, LANE), 1)
            m_out, l_out = m_all, l_all
            for hp in range(B_HEADS // 2):
                acc_pair = acc[:, 128 * hp:128 * hp + 128]
                new = []
                for u in range(2):
                    h = 2 * hp + u
                    sl = slice(64 * h, 64 * h + 64)
                    sc = _dot_nt(q[:, sl], k[:, sl]) + cqv[:, h:h + 1] - ckv[h:h + 1, :]
                    sc = jnp.where(mask, sc, NEG)
                    m_prev = m_all[:, h:h + 1]
                    m_new = jnp.maximum(m_prev, jnp.max(sc, axis=-1, keepdims=True))
                    alpha = jnp.exp(m_prev - m_new)
                    p = jnp.exp(sc - m_new)
                    l_new = alpha * l_all[:, h:h + 1] + jnp.sum(p, axis=-1, keepdims=True)
                    new.append(alpha * acc_pair[:, 64 * u:64 * u + 64] + _dot(p.astype(BF16), v[:, sl]))
                    m_out = jnp.where(lane == h, m_new, m_out)
                    l_out = jnp.where(lane == h, l_new, l_out)
                acc[:, 128 * hp:128 * hp + 128] = jnp.concatenate(new, axis=1)
            m_s[...] = m_out
            l_s[...] = l_out

        @pl.when(j == nk - 1)
        def _():
            l_all = l_s[...]
            inv = 1.0 / l_all
            a = acc[...]
            y = jnp.concatenate([a[:, 64 * h:64 * h + 64] * inv[:, h:h + 1] for h in range(B_HEADS)], axis=1)
            yb_ref[...] = y
            gb_ref[...] = (y * _silu(z_ref[...])).astype(BF16)
            lse_ref[...] = m_s[...] + jnp.log(l_all)

    kmap = lambda i, j: (jnp.minimum(j, last_k(i)), 0)
    return pl.pallas_call(
        body, grid=(nq, nk),
        in_specs=[pl.BlockSpec((tq, 768), lambda i, j: (i, 0)),
                  pl.BlockSpec((tk, 768), kmap),
                  pl.BlockSpec((pl.Element(tk), pl.Element(768)), lambda i, j: (jnp.minimum(j, last_k(i)) * tk, VB)),
                  pl.BlockSpec((tq, LANE), lambda i, j: (i, 0)),
                  pl.BlockSpec((16, tk), lambda i, j: (0, jnp.minimum(j, last_k(i)))),
                  pl.BlockSpec((pl.Element(tq), pl.Element(768)), lambda i, j: (i * tq, ZB))],
        out_specs=[pl.BlockSpec((tq, 768), lambda i, j: (i, 0)), pl.BlockSpec((tq, 768), lambda i, j: (i, 0)),
                   pl.BlockSpec((tq, LANE), lambda i, j: (i, 0))],
        out_shape=[SDS((s, 768), BF16), SDS((s, 768), F32), SDS((s, LANE), F32)],
        scratch_shapes=[pltpu.VMEM((tq, 768), F32), pltpu.VMEM((tq, LANE), F32), pltpu.VMEM((tq, LANE), F32)],
        name="fox_fwd", compiler_params=_params(("parallel", "arbitrary")))(qn, kn, proj, cq, ck, proj)


def _fox_bwd_pre(proj, yb, dgb, hsum, *, tr):
    s = proj.shape[0]

    def body(z_ref, y_ref, dg_ref, hs_ref, dy_ref, dz_ref, dl_ref):
        z, y, dg = z_ref[...], y_ref[...], dg_ref[...]
        sg = jax.nn.sigmoid(z)
        dy = dg * (z * sg)
        dy_ref[...] = dy.astype(BF16)
        dz_ref[...] = (dg * y * (sg * (1.0 + z * (1.0 - sg)))).astype(BF16)
        dl_ref[...] = jnp.dot(dy * y, hs_ref[...], precision=HI, preferred_element_type=F32)

    return pl.pallas_call(
        body, grid=(s // tr,),
        in_specs=[_win(tr, 768, ZB), _rowblk(tr, 768), _rowblk(tr, 768), _const((768, LANE))],
        out_specs=[_rowblk(tr, 768), _rowblk(tr, 768), _rowblk(tr, LANE)],
        out_shape=[SDS((s, 768), BF16), SDS((s, 768), BF16), SDS((s, LANE), F32)], name="fox_bwd_pre",
        compiler_params=_params(("parallel",)))(proj, yb, dgb, hsum)


def _fox_bwd(proj, qn, kn, cq, ck, lse, delta, dyb):
    s = proj.shape[0]
    tq, tk = _fox_tiles(s)
    nq, nk = s // tq, s // tk

    def first_q(j):
        return (j * tk) // tq

    def body(q_ref, k_ref, v_ref, cq_ref, ck_ref, lse_ref, dl_ref, dy_ref,
             dq_ref, dcq_ref, dk_ref, dv_ref, dck_ref, dk_acc, dv_acc, dck_acc):
        j, i = pl.program_id(0), pl.program_id(1)

        @pl.when((j == 0) & (i == 0))
        def _():
            dq_ref[...] = jnp.zeros_like(dq_ref)
            dcq_ref[...] = jnp.zeros_like(dcq_ref)

        @pl.when(i == 0)
        def _():
            dk_acc[...] = jnp.zeros_like(dk_acc)
            dv_acc[...] = jnp.zeros_like(dv_acc)
            dck_acc[...] = jnp.zeros_like(dck_acc)

        @pl.when(i >= first_q(j))
        def _():
            qpos = i * tq + lax.broadcasted_iota(jnp.int32, (tq, tk), 0)
            kpos = j * tk + lax.broadcasted_iota(jnp.int32, (tq, tk), 1)
            mask = kpos <= qpos
            q, k = q_ref[...], k_ref[...]
            v = v_ref[...].astype(BF16)
            dy = dy_ref[...]
            cqv, ckv, lsev, dlv = cq_ref[...], ck_ref[...], lse_ref[...], dl_ref[...]
            lane = lax.broadcasted_iota(jnp.int32, (tq, LANE), 1)
            rows = pl.ds(pl.multiple_of(i * tq, tq), tq)
            dcq_t = jnp.zeros((tq, LANE), F32)
            for hp in range(B_HEADS // 2):
                dq_new, dk_new, dv_new = [], [], []
                for u in range(2):
                    h = 2 * hp + u
                    sl = slice(64 * h, 64 * h + 64)
                    sc = _dot_nt(q[:, sl], k[:, sl]) + cqv[:, h:h + 1] - ckv[h:h + 1, :]
                    sc = jnp.where(mask, sc, NEG)
                    p = jnp.exp(sc - lsev[:, h:h + 1])
                    dp = _dot_nt(dy[:, sl], v[:, sl])
                    ds = p * (dp - dlv[:, h:h + 1])
                    dsb = ds.astype(BF16)
                    dv_new.append(_dot_tn(p.astype(BF16), dy[:, sl]))
                    dk_new.append(_dot_tn(dsb, q[:, sl]))
                    dq_new.append(_dot(dsb, k[:, sl]))
                    dcq_t = jnp.where(lane == h, jnp.sum(ds, axis=-1, keepdims=True), dcq_t)
                    dck_acc[h:h + 1, :] -= jnp.sum(ds, axis=0, keepdims=True)
                cols = slice(128 * hp, 128 * hp + 128)
                dq_ref[rows, cols] += jnp.concatenate(dq_new, axis=1)
                dk_acc[:, cols] += jnp.concatenate(dk_new, axis=1)
                dv_acc[:, cols] += jnp.concatenate(dv_new, axis=1)
            dcq_ref[rows, :] += dcq_t

        @pl.when(i == nq - 1)
        def _():
            dk_ref[...] = dk_acc[...]
            dv_ref[...] = dv_acc[...].astype(BF16)
            dck_ref[...] = jnp.concatenate([dck_acc[...], jnp.zeros((LANE - 16, tk), F32)], axis=0)

    qmap = lambda j, i: (jnp.maximum(i, first_q(j)), 0)
    return pl.pallas_call(
        body, grid=(nk, nq),
        in_specs=[pl.BlockSpec((tq, 768), qmap),
                  pl.BlockSpec((tk, 768), lambda j, i: (j, 0)),
                  pl.BlockSpec((pl.Element(tk), pl.Element(768)), lambda j, i: (j * tk, VB)),
                  pl.BlockSpec((tq, LANE), qmap),
                  pl.BlockSpec((16, tk), lambda j, i: (0, j)),
                  pl.BlockSpec((tq, LANE), qmap),
                  pl.BlockSpec((tq, LANE), qmap),
                  pl.BlockSpec((tq, 768), qmap)],
        out_specs=[_const((s, 768)), _const((s, LANE)),
                   pl.BlockSpec((tk, 768), lambda j, i: (j, 0)), pl.BlockSpec((tk, 768), lambda j, i: (j, 0)),
                   pl.BlockSpec((LANE, tk), lambda j, i: (0, j))],
        out_shape=[SDS((s, 768), F32), SDS((s, LANE), F32), SDS((s, 768), F32), SDS((s, 768), BF16),
                   SDS((LANE, s), F32)],
        scratch_shapes=[pltpu.VMEM((tk, 768), F32), pltpu.VMEM((tk, 768), F32), pltpu.VMEM((16, tk), F32)],
        name="fox_bwd", compiler_params=_params(("arbitrary", "arbitrary")))(qn, kn, proj, cq, ck, lse, delta, dyb)


def _fox_bwd_post(proj, fbl, qg, kg, bfor, bd, dqn, dkn, dcq, dck, *, tr):
    s = proj.shape[0]
    nb = s // tr
    triu = jnp.asarray(np.triu(np.ones((tr, tr), np.float32)))
    rev = lambda i: nb - 1 - i

    def body(q_ref, k_ref, fb_ref, qg_ref, kg_ref, bf_ref, bd_ref, tri_ref, dqn_ref, dkn_ref, dcq_ref, dck_ref,
             dq_ref, dk_ref, dfb_ref, dqg_ref, dkg_ref, dbf_ref, carry):
        @pl.when(pl.program_id(0) == 0)
        def _():
            carry[...] = jnp.zeros_like(carry)
            dqg_ref[...] = jnp.zeros_like(dqg_ref)
            dkg_ref[...] = jnp.zeros_like(dkg_ref)
            dbf_ref[...] = jnp.zeros_like(dbf_ref)

        bd_v = bd_ref[...]
        _, vjp_q = jax.vjp(lambda q, g: _qn_fn(q, g, bd_v), q_ref[...], qg_ref[...])
        dq, dqg = vjp_q(dqn_ref[...])
        _, vjp_k = jax.vjp(lambda k, g: _kn_fn(k, g, bd_v), k_ref[...], kg_ref[...])
        dk, dkg = vjp_k(dkn_ref[...])
        dq_ref[...] = dq.astype(BF16)
        dk_ref[...] = dk.astype(BF16)
        dqg_ref[...] += dqg
        dkg_ref[...] += dkg

        dc = dcq_ref[...] + jnp.transpose(dck_ref[...])
        dlogf = jnp.dot(tri_ref[...], dc, precision=HI, preferred_element_type=F32) + carry[...]
        carry[...] = dlogf[0:1, :]
        lane = lax.broadcasted_iota(jnp.int32, (tr, LANE), 1)
        xf = fb_ref[...] + bf_ref[...]
        dfb = jnp.where(lane < N_FORGET, dlogf * jax.nn.sigmoid(-xf), 0.0)
        dfb_ref[...] = dfb.astype(BF16)
        dbf_ref[...] += jnp.sum(dfb, axis=0, keepdims=True)

    rb = lambda w: pl.BlockSpec((tr, w), lambda i: (rev(i), 0))
    wn = lambda w, off: pl.BlockSpec((pl.Element(tr), pl.Element(w)), lambda i: (rev(i) * tr, off))
    return pl.pallas_call(
        body, grid=(nb,),
        in_specs=[wn(768, QB), wn(768, KB), rb(LANE), _const((1, 768)), _const((1, 768)), _const((1, LANE)),
                  _const((768, 768)), _const((tr, tr)), rb(768), rb(768), rb(LANE),
                  pl.BlockSpec((LANE, tr), lambda i: (0, rev(i)))],
        out_specs=[rb(768), rb(768), rb(LANE), _const((1, 768)), _const((1, 768)), _const((1, LANE))],
        out_shape=[SDS((s, 768), BF16), SDS((s, 768), BF16), SDS((s, LANE), BF16), SDS((1, 768), F32),
                   SDS((1, 768), F32), SDS((1, LANE), F32)],
        scratch_shapes=[pltpu.VMEM((1, LANE), F32)], name="fox_bwd_post",
        compiler_params=_params(("arbitrary",)))(proj, proj, fbl, qg, kg, bfor, bd, triu, dqn, dkn, dcq, dck)


AUG = 128 * B_HEADS
COL_A, COL_B = 64, 67


def _split3(c):
    hi = c.astype(BF16)
    r1 = c - hi.astype(F32)
    mid = r1.astype(BF16)
    lo = (r1 - mid.astype(F32)).astype(BF16)
    return hi, mid, lo


def _expand_mats():
    def mat(col0):
        e = np.zeros((768 + 3 * LANE, AUG), np.float32)
        for h in range(B_HEADS):
            for d in range(HEAD_DIM):
                e[64 * h + d, 128 * h + d] = 1.0
            for part in range(3):
                e[768 + LANE * part + h, 128 * h + col0 + part] = 1.0
        return e

    def ones(col0):
        o = np.zeros((1, AUG), np.float32)
        for h in range(B_HEADS):
            o[0, 128 * h + col0:128 * h + col0 + 3] = 1.0
        return o

    return (jnp.asarray(mat(COL_A), BF16), jnp.asarray(mat(COL_B), BF16), jnp.asarray(ones(COL_A)), jnp.asarray(ones(COL_B)))


def _augment(data_bf16, triple, emat, ones_row):
    parts = [data_bf16] + (list(triple) if triple is not None else [jnp.zeros((data_bf16.shape[0], LANE), BF16)] * 3)
    wide = _dot(jnp.concatenate(parts, axis=1), emat)
    if ones_row is not None:
        wide = wide + ones_row
    return wide.astype(BF16)


def _compact(wide):
    return jnp.concatenate([wide[:, 128 * h:128 * h + 64] for h in range(B_HEADS)], axis=1)


def _lane_of_heads(wide, col):
    rows = wide.shape[0]
    lane = lax.broadcasted_iota(jnp.int32, (rows, LANE), 1)
    out = jnp.zeros((rows, LANE), F32)
    for h in range(B_HEADS):
        out = jnp.where(lane == h, wide[:, 128 * h + col:128 * h + col + 1], out)
    return out


def _fox2_prep(proj, fbl, qg, kg, bfor, bd, ea, eb, ones_a, ones_b, *, tr):
    s = proj.shape[0]
    tri = jnp.asarray(np.tril(np.ones((tr, tr), np.float32)))

    def body(q_ref, k_ref, v_ref, fb_ref, qg_ref, kg_ref, bf_ref, bd_ref, tri_ref, ea_ref, eb_ref, oa_ref, ob_ref,
             qa_ref, ka_ref, va_ref, qn_ref, c_ref, carry):
        @pl.when(pl.program_id(0) == 0)
        def _():
            carry[...] = jnp.zeros_like(carry)

        bd_v = bd_ref[...]
        lane = lax.broadcasted_iota(jnp.int32, (tr, LANE), 1)
        logf = jnp.where(lane < N_FORGET, _log_sigmoid(fb_ref[...] + bf_ref[...]), 0.0)
        c = jnp.dot(tri_ref[...], logf, precision=HI, preferred_element_type=F32) + carry[...]
        c_ref[...] = c
        carry[...] = c[tr - 1:tr, :]
        qn = _qn_fn(q_ref[...], qg_ref[...], bd_v).astype(BF16)
        kn = _kn_fn(k_ref[...], kg_ref[...], bd_v).astype(BF16)
        qn_ref[...] = qn
        qa_ref[...] = _augment(qn, _split3(c), ea_ref[...], ob_ref[...])
        ka_ref[...] = _augment(kn, _split3(-c), eb_ref[...], oa_ref[...])
        va_ref[...] = _augment(v_ref[...].astype(BF16), None, ea_ref[...], oa_ref[...])

    emat = _const((768 + 3 * LANE, AUG))
    return pl.pallas_call(
        body, grid=(s // tr,),
        in_specs=[_win(tr, 768, QB), _win(tr, 768, KB), _win(tr, 768, VB), _rowblk(tr, LANE), _const((1, 768)),
                  _const((1, 768)), _const((1, LANE)), _const((768, 768)), _const((tr, tr)), emat, emat,
                  _const((1, AUG)), _const((1, AUG))],
        out_specs=[_rowblk(tr, AUG), _rowblk(tr, AUG), _rowblk(tr, AUG), _rowblk(tr, 768), _rowblk(tr, LANE)],
        out_shape=[SDS((s, AUG), BF16)] * 3 + [SDS((s, 768), BF16), SDS((s, LANE), F32)],
        scratch_shapes=[pltpu.VMEM((1, LANE), F32)], name="fox_prep",
        compiler_params=_params(("arbitrary",)))(proj, proj, proj, fbl, qg, kg, bfor, bd, tri, ea, eb, ones_a, ones_b)


def _fox2_fwd(proj, qa, ka, va):
    s = proj.shape[0]
    tq, tk = _fox_tiles(s)
    nq, nk = s // tq, s // tk

    def last_k(i):
        return (i * tq + tq - 1) // tk

    def body(q_ref, k_ref, v_ref, z_ref, gb_ref, yb_ref, lse_ref, acc, m_s):
        i, j = pl.program_id(0), pl.program_id(1)

        @pl.when(j == 0)
        def _():
            acc[...] = jnp.zeros_like(acc)
            m_s[...] = jnp.full_like(m_s, NEG)

        def tile(masked):
            if masked:
                qpos = i * tq + lax.broadcasted_iota(jnp.int32, (tq, tk), 0)
                kpos = j * tk + lax.broadcasted_iota(jnp.int32, (tq, tk), 1)
                mask = kpos <= qpos
            m_all = m_s[...]
            lane = lax.broadcasted_iota(jnp.int32, (tq, LANE), 1)
            m_out = m_all
            for h in range(B_HEADS):
                sl = slice(128 * h, 128 * h + 128)
                sc = _dot_nt(q_ref[:, sl], k_ref[:, sl])
                if masked:
                    sc = jnp.where(mask, sc, NEG)
                m_prev = m_all[:, h:h + 1]
                m_new = jnp.maximum(m_prev, jnp.max(sc, axis=-1, keepdims=True))
                p = jnp.exp(sc - m_new).astype(BF16)
                acc[:, sl] = jnp.exp(m_prev - m_new) * acc[:, sl] + _dot(p, v_ref[:, sl])
                m_out = jnp.where(lane == h, m_new, m_out)
            m_s[...] = m_out

        full = j * tk + tk - 1 <= i * tq

        @pl.when(full)
        def _():
            tile(False)

        @pl.when(jnp.logical_and(jnp.logical_not(full), j <= last_k(i)))
        def _():
            tile(True)

        @pl.when(j == nk - 1)
        def _():
            a = acc[...]
            l_all = _lane_of_heads(a, COL_A)
            y = jnp.concatenate([a[:, 128 * h:128 * h + 64] / a[:, 128 * h + COL_A:128 * h + COL_A + 1]
                                 for h in range(B_HEADS)], axis=1)
            yb_ref[...] = y
            gb_ref[...] = (y * _silu(z_ref[...])).astype(BF16)
            lane = lax.broadcasted_iota(jnp.int32, (tq, LANE), 1)
            lse_ref[...] = jnp.where(lane < B_HEADS, m_s[...] + jnp.log(jnp.where(lane < B_HEADS, l_all, 1.0)), 0.0)

    kmap = lambda i, j: (jnp.minimum(j, last_k(i)), 0)
    return pl.pallas_call(
        body, grid=(nq, nk),
        in_specs=[pl.BlockSpec((tq, AUG), lambda i, j: (i, 0)), pl.BlockSpec((tk, AUG), kmap), pl.BlockSpec((tk, AUG), kmap),
                  pl.BlockSpec((pl.Element(tq), pl.Element(768)), lambda i, j: (i * tq, ZB))],
        out_specs=[pl.BlockSpec((tq, 768), lambda i, j: (i, 0)), pl.BlockSpec((tq, 768), lambda i, j: (i, 0)),
                   pl.BlockSpec((tq, LANE), lambda i, j: (i, 0))],
        out_shape=[SDS((s, 768), BF16), SDS((s, 768), F32), SDS((s, LANE), F32)],
        scratch_shapes=[pltpu.VMEM((tq, AUG), F32), pltpu.VMEM((tq, LANE), F32)],
        name="fox_fwd", compiler_params=_params(("parallel", "arbitrary")))(qa, ka, va, proj)


def _fox2_bwd_pre(proj, yb, dgb, qn, c, lse, hsum, ea, ones_b, *, tr):
    s = proj.shape[0]

    def body(z_ref, y_ref, dg_ref, qn_ref, c_ref, lse_ref, hs_ref, ea_ref, ob_ref, qa_ref, dya_ref, dz_ref):
        z, y, dg = z_ref[...], y_ref[...], dg_ref[...]
        sg = jax.nn.sigmoid(z)
        dy = dg * (z * sg)
        dz_ref[...] = (dg * y * (sg * (1.0 + z * (1.0 - sg)))).astype(BF16)
        delta = jnp.dot(dy * y, hs_ref[...], precision=HI, preferred_element_type=F32)
        e = ea_ref[...]
        dya_ref[...] = _augment(dy.astype(BF16), _split3(-delta), e, None)
        qa_ref[...] = _augment(qn_ref[...], _split3(c_ref[...] - lse_ref[...]), e, ob_ref[...])

    return pl.pallas_call(
        body, grid=(s // tr,),
        in_specs=[_win(tr, 768, ZB), _rowblk(tr, 768), _rowblk(tr, 768), _rowblk(tr, 768), _rowblk(tr, LANE),
                  _rowblk(tr, LANE), _const((768, LANE)), _const((768 + 3 * LANE, AUG)), _const((1, AUG))],
        out_specs=[_rowblk(tr, AUG), _rowblk(tr, AUG), _rowblk(tr, 768)],
        out_shape=[SDS((s, AUG), BF16), SDS((s, AUG), BF16), SDS((s, 768), BF16)], name="fox_bwd_pre",
        compiler_params=_params(("parallel",)))(proj, yb, dgb, qn, c, lse, hsum, ea, ones_b)


def _fox2_bwd(qa, ka, va, dya):
    s = qa.shape[0]
    tq, tk = _fox_tiles(s)
    nq, nk = s // tq, s // tk

    def first_q(j):
        return (j * tk) // tq

    def body(q_ref, k_ref, v_ref, dy_ref, dq_hbm, dk_ref, dv_ref, dck_ref, dq_acc, dk_acc, dv_acc, sem):
        j, i = pl.program_id(0), pl.program_id(1)

        @pl.when((j == 0) & (i == 0))
        def _():
            dq_acc[...] = jnp.zeros_like(dq_acc)

        @pl.when(i == 0)
        def _():
            dk_acc[...] = jnp.zeros_like(dk_acc)
            dv_acc[...] = jnp.zeros_like(dv_acc)

        def tile(masked):
            if masked:
                qpos = i * tq + lax.broadcasted_iota(jnp.int32, (tq, tk), 0)
                kpos = j * tk + lax.broadcasted_iota(jnp.int32, (tq, tk), 1)
                mask = kpos <= qpos
            rows = pl.ds(pl.multiple_of(i * tq, tq), tq)
            for h in range(B_HEADS):
                sl = slice(128 * h, 128 * h + 128)
                q, k, dy = q_ref[:, sl], k_ref[:, sl], dy_ref[:, sl]
                sc = _dot_nt(q, k)
                if masked:
                    sc = jnp.where(mask, sc, NEG)
                p = jnp.exp(sc)
                ds = (p * _dot_nt(dy, v_ref[:, sl])).astype(BF16)
                dv_acc[:, sl] += _dot_tn(p.astype(BF16), dy)
                dk_acc[:, sl] += _dot_tn(ds, q)
                dq_acc[rows, sl] += _dot(ds, k)

        full = j * tk + tk - 1 <= i * tq

        @pl.when(full)
        def _():
            tile(False)

        @pl.when(jnp.logical_and(jnp.logical_not(full), i >= first_q(j)))
        def _():
            tile(True)

        @pl.when(i == nq - 1)
        def _():
            dkw = dk_acc[...]
            dk_ref[...] = _compact(dkw)
            dv_ref[...] = _compact(dv_acc[...]).astype(BF16)
            dck_ref[...] = -_lane_of_heads(dkw, COL_B)

        @pl.when((j == nk - 1) & (i == nq - 1))
        def _():
            cp = pltpu.make_async_copy(dq_acc, dq_hbm, sem)
            cp.start()
            cp.wait()

    qmap = lambda j, i: (jnp.maximum(i, first_q(j)), 0)
    kblk = lambda w: pl.BlockSpec((tk, w), lambda j, i: (j, 0))
    return pl.pallas_call(
        body, grid=(nk, nq),
        in_specs=[pl.BlockSpec((tq, AUG), qmap), kblk(AUG), kblk(AUG), pl.BlockSpec((tq, AUG), qmap)],
        out_specs=[pl.BlockSpec(memory_space=pl.ANY), kblk(768), kblk(768), kblk(LANE)],
        out_shape=[SDS((s, AUG), F32), SDS((s, 768), F32), SDS((s, 768), BF16), SDS((s, LANE), F32)],
        scratch_shapes=[pltpu.VMEM((s, AUG), F32), pltpu.VMEM((tk, AUG), F32), pltpu.VMEM((tk, AUG), F32),
                        pltpu.SemaphoreType.DMA],
        name="fox_bwd", compiler_params=_params(("arbitrary", "arbitrary")))(qa, ka, va, dya)


def _fox2_bwd_post(proj, fbl, qg, kg, bfor, bd, dqa, dkn, dck, *, tr):
    s = proj.shape[0]
    nb = s // tr
    triu = jnp.asarray(np.triu(np.ones((tr, tr), np.float32)))
    rev = lambda i: nb - 1 - i

    def body(q_ref, k_ref, fb_ref, qg_ref, kg_ref, bf_ref, bd_ref, tri_ref, dqa_ref, dkn_ref, dck_ref,
             dq_ref, dk_ref, dfb_ref, dqg_ref, dkg_ref, dbf_ref, carry):
        @pl.when(pl.program_id(0) == 0)
        def _():
            carry[...] = jnp.zeros_like(carry)
            dqg_ref[...] = jnp.zeros_like(dqg_ref)
            dkg_ref[...] = jnp.zeros_like(dkg_ref)
            dbf_ref[...] = jnp.zeros_like(dbf_ref)

        bd_v = bd_ref[...]
        dqw = dqa_ref[...]
        _, vjp_q = jax.vjp(lambda q, g: _qn_fn(q, g, bd_v), q_ref[...], qg_ref[...])
        dq, dqg = vjp_q(_compact(dqw))
        _, vjp_k = jax.vjp(lambda k, g: _kn_fn(k, g, bd_v), k_ref[...], kg_ref[...])
        dk, dkg = vjp_k(dkn_ref[...])
        dq_ref[...] = dq.astype(BF16)
        dk_ref[...] = dk.astype(BF16)
        dqg_ref[...] += dqg
        dkg_ref[...] += dkg

        dc = _lane_of_heads(dqw, COL_A) + dck_ref[...]
        dlogf = jnp.dot(tri_ref[...], dc, precision=HI, preferred_element_type=F32) + carry[...]
        carry[...] = dlogf[0:1, :]
        lane = lax.broadcasted_iota(jnp.int32, (tr, LANE), 1)
        xf = fb_ref[...] + bf_ref[...]
        dfb = jnp.where(lane < N_FORGET, dlogf * jax.nn.sigmoid(-xf), 0.0)
        dfb_ref[...] = dfb.astype(BF16)
        dbf_ref[...] += jnp.sum(dfb, axis=0, keepdims=True)

    rb = lambda w: pl.BlockSpec((tr, w), lambda i: (rev(i), 0))
    wn = lambda w, off: pl.BlockSpec((pl.Element(tr), pl.Element(w)), lambda i: (rev(i) * tr, off))
    return pl.pallas_call(
        body, grid=(nb,),
        in_specs=[wn(768, QB), wn(768, KB), rb(LANE), _const((1, 768)), _const((1, 768)), _const((1, LANE)),
                  _const((768, 768)), _const((tr, tr)), rb(AUG), rb(768), rb(LANE)],
        out_specs=[rb(768), rb(768), rb(LANE), _const((1, 768)), _const((1, 768)), _const((1, LANE))],
        out_shape=[SDS((s, 768), BF16), SDS((s, 768), BF16), SDS((s, LANE), BF16), SDS((1, 768), F32),
                   SDS((1, 768), F32), SDS((1, LANE), F32)],
        scratch_shapes=[pltpu.VMEM((1, LANE), F32)], name="fox_bwd_post",
        compiler_params=_params(("arbitrary",)))(proj, proj, fbl, qg, kg, bfor, bd, triu, dqa, dkn, dck)


def _merge_specs(tr):
    row = lambda w: pl.BlockSpec((tr, w), lambda i, j: (i, 0))
    shard = lambda r: pl.BlockSpec((None, r, 512), lambda i, j: (j, 0, 0))
    gate = lambda b: pl.BlockSpec((tr, 512), lambda i, j: (i, (GATE + 2048 * b) // 512 + j))
    return [row(768), row(768), row(512), shard(768), shard(768), shard(512), gate(0), gate(1), gate(2)]


def _merge_fwd(proj, ga, gb, gc, wa, wb, wc, *, tr):
    s = proj.shape[0]

    def body(ga_ref, gb_ref, gc_ref, wa_ref, wb_ref, wc_ref, l0_ref, l1_ref, l2_ref, y_ref):
        ua = _dot(ga_ref[...], wa_ref[...])
        ub = _dot(gb_ref[...], wb_ref[...])
        uc = _dot(gc_ref[...], wc_ref[...])
        y = jax.nn.sigmoid(l0_ref[...]) * ua + jax.nn.sigmoid(l1_ref[...]) * ub + jax.nn.sigmoid(l2_ref[...]) * uc
        y_ref[...] = y.astype(BF16)

    return pl.pallas_call(
        body, grid=(s // tr, N_CHIPS), in_specs=_merge_specs(tr),
        out_specs=pl.BlockSpec((tr, 512), lambda i, j: (i, j)), out_shape=SDS((s, D_MODEL), BF16), name="merge_fwd",
        compiler_params=_params(("parallel", "arbitrary")))(ga, gb, gc, wa, wb, wc, proj, proj, proj)


def _merge_bwd(proj, ga, gb, gc, wa, wb, wc, dy, *, tr):
    s = proj.shape[0]

    def body(ga_ref, gb_ref, gc_ref, wa_ref, wb_ref, wc_ref, l0_ref, l1_ref, l2_ref, dy_ref,
             dl0_ref, dl1_ref, dl2_ref, dua_ref, dub_ref, duc_ref, dga_ref, dgb_ref, dgc_ref):
        j = pl.program_id(1)
        dyv = dy_ref[...]

        @pl.when(j == 0)
        def _():
            dga_ref[...] = jnp.zeros_like(dga_ref)
            dgb_ref[...] = jnp.zeros_like(dgb_ref)
            dgc_ref[...] = jnp.zeros_like(dgc_ref)

        for g_ref, w_ref, l_ref, dl_ref, du_ref, dg_ref in (
                (ga_ref, wa_ref, l0_ref, dl0_ref, dua_ref, dga_ref),
                (gb_ref, wb_ref, l1_ref, dl1_ref, dub_ref, dgb_ref),
                (gc_ref, wc_ref, l2_ref, dl2_ref, duc_ref, dgc_ref)):
            w = w_ref[...]
            u = _dot(g_ref[...], w)
            sg = jax.nn.sigmoid(l_ref[...])
            dl_ref[...] = (dyv * u * sg * (1.0 - sg)).astype(BF16)
            du = (dyv * sg).astype(BF16)
            du_ref[...] = du
            dg_ref[...] += _dot_nt(du, w)

    blk = pl.BlockSpec((tr, 512), lambda i, j: (i, j))
    row = lambda w: pl.BlockSpec((tr, w), lambda i, j: (i, 0))
    big = SDS((s, D_MODEL), BF16)
    return pl.pallas_call(
        body, grid=(s // tr, N_CHIPS), in_specs=_merge_specs(tr) + [blk],
        out_specs=[blk] * 6 + [row(768), row(768), row(512)],
        out_shape=[big] * 6 + [SDS((s, 768), F32), SDS((s, 768), F32), SDS((s, 512), F32)], name="merge_bwd",
        compiler_params=_params(("parallel", "arbitrary")))(ga, gb, gc, wa, wb, wc, proj, proj, proj, dy)


def _out_loss(y, wo, x, tgt, *, tr, tn):
    s = x.shape[0]

    def body(y_ref, w_ref, x_ref, t_ref, d_ref, db_ref, sq_ref):
        @pl.when((pl.program_id(0) == 0) & (pl.program_id(1) == 0))
        def _():
            sq_ref[...] = jnp.zeros_like(sq_ref)

        out = x_ref[...] + _dot(y_ref[...], w_ref[...])
        diff = out - t_ref[...]
        sq_ref[...] += jnp.sum(diff * diff, axis=0, keepdims=True)
        d = diff * (1.0 / D_MODEL)
        d_ref[...] = d
        db_ref[...] = d.astype(BF16)

    blk = pl.BlockSpec((tr, tn), lambda i, j: (i, j))
    return pl.pallas_call(
        body, grid=(s // tr, D_MODEL // tn),
        in_specs=[pl.BlockSpec((tr, D_MODEL), lambda i, j: (i, 0)), pl.BlockSpec((D_MODEL, tn), lambda i, j: (0, j)), blk, blk],
        out_specs=[blk, blk, _const((1, tn))],
        out_shape=[SDS((s, D_MODEL), F32), SDS((s, D_MODEL), BF16), SDS((1, tn), F32)], name="out_loss",
        compiler_params=_params(("arbitrary", "arbitrary")))(y, wo, x, tgt)


def _tile_gain(g, reps):
    return jnp.tile(g.reshape(1, -1), (1, reps))


def _pad_lane(v):
    v = v.reshape(1, -1)
    return jnp.pad(v, ((0, 0), (0, LANE - v.shape[1])))


def _local_step(x, mem, tgt, w_main, w_fb, w_mk, wa, wb, wc, wo, norm_gain, mem_norm_gain, b_forget,
                q_gain_a, k_gain_a, sinks_a, q_gain_b, k_gain_b, q_gain_c, k_gain_c):
    s = x.shape[0]
    tr = min(512, s)
    bd64 = _block_diag(768, HEAD_DIM)
    bd128 = _block_diag(512, C_HEAD_DIM)
    hsum = _head_sum(768, HEAD_DIM)
    qga, kga = _tile_gain(q_gain_a, 12), _tile_gain(k_gain_a, 4)
    qgb, kgb = _tile_gain(q_gain_b, 12), _tile_gain(k_gain_b, 12)
    qgc, kgc = _tile_gain(q_gain_c, 4), _tile_gain(k_gain_c, 4)
    sinks = _pad_lane(sinks_a)
    bfor = _pad_lane(b_forget)

    hn = _rms_fwd(x, norm_gain, tr=tr, name="rms_x")
    proj = _matmul(hn, w_main, dims="nn", out_dtype=F32, tm=1024, tn=512, tk=D_MODEL, name="proj_main")
    fbl = _matmul(hn, w_fb, dims="nn", out_dtype=F32, tm=1024, tn=LANE, tk=D_MODEL, name="proj_forget")
    memn = _rms_fwd(mem, mem_norm_gain, tr=mem.shape[0], name="rms_mem")
    mkv = _matmul(memn, w_mk, dims="nn", out_dtype=F32, tm=256, tn=512, tk=D_MODEL, name="mem_kv")

    ga = _swa_fwd(proj, qga, kga, sinks, bd64)
    ea, eb, ones_a, ones_b = _expand_mats()
    tf = min(256, s)
    qa, ka, va, qn, cfox = _fox2_prep(proj, fbl, qgb, kgb, bfor, bd64, ea, eb, ones_a, ones_b, tr=tf)
    gb, yb, lse = _fox2_fwd(proj, qa, ka, va)
    gc = _mem_fwd(proj, mkv, qgc, kgc, bd128, tr=tr)
    y = _merge_fwd(proj, ga, gb, gc, wa, wb, wc, tr=tr)
    dout, dout_b, sq = _out_loss(y, wo, x, tgt, tr=tr, tn=512)

    d_wo = _matmul(y, dout_b, dims="tn", out_dtype=F32, tm=1024, tn=512, tk=2048, name="dw_out")
    dy = _matmul(dout_b, wo, dims="nt", out_dtype=F32, tm=1024, tn=512, tk=D_MODEL, name="dy")
    dl0, dl1, dl2, dua, dub, duc, dga, dgb, dgc = _merge_bwd(proj, ga, gb, gc, wa, wb, wc, dy, tr=tr)
    d_wa = _matmul(ga, dua, dims="tn", out_dtype=F32, tm=768, tn=512, tk=2048, name="dw_branch_a")
    d_wb = _matmul(gb, dub, dims="tn", out_dtype=F32, tm=768, tn=512, tk=2048, name="dw_branch_b")
    d_wc = _matmul(gc, duc, dims="tn", out_dtype=F32, tm=512, tn=512, tk=2048, name="dw_branch_c")

    dcur, dprev, d_qga, d_kga, d_sinks = _swa_bwd(proj, qga, kga, sinks, bd64, dga)
    dproj_a = _swa_combine(dcur, dprev)

    qab, dya, dzb = _fox2_bwd_pre(proj, yb, dgb, qn, cfox, lse, hsum, ea, ones_b, tr=tf)
    dqa, dkn, dvb, dck = _fox2_bwd(qab, ka, va, dya)
    dqb, dkb, dfb, d_qgb, d_kgb, d_bf = _fox2_bwd_post(proj, fbl, qgb, kgb, bfor, bd64, dqa, dkn, dck, tr=tf)

    dproj_c, dmkv, d_qgc, d_kgc = _mem_bwd(proj, mkv, qgc, kgc, bd128, dgc, tr=tr)
    dmkv_b = dmkv.astype(BF16)
    d_wmk = _matmul(memn, dmkv_b, dims="tn", out_dtype=F32, tm=1024, tn=512, tk=256, name="dw_mem_kv")
    dmemn = _matmul(dmkv_b, w_mk, dims="nt", out_dtype=F32, tm=256, tn=512, tk=1024, name="dmemn")
    (d_mem_gain,) = _rms_bwd(mem, mem_norm_gain, dmemn, None, tr=mem.shape[0], name="rms_mem_bwd")

    dproj = jnp.concatenate([dproj_a, dqb, dkb, dvb, dzb, dproj_c, dl0, dl1, dl2], axis=1)
    dhn_f = _matmul(dfb, w_fb, dims="nt", out_dtype=F32, tm=1024, tn=512, tk=LANE, name="dhn_forget")
    dhn = _matmul(dproj, w_main, dims="nt", out_dtype=F32, tm=1024, tn=512, tk=2048, name="dhn", add=dhn_f)
    d_wmain = _matmul(hn, dproj, dims="tn", out_dtype=F32, tm=1024, tn=512, tk=2048, name="dw_main")
    d_wfb = _matmul(hn, dfb, dims="tn", out_dtype=F32, tm=1024, tn=LANE, tk=512, name="dw_forget")
    grad_x, d_gain = _rms_bwd(x, norm_gain, dhn, dout, tr=tr, name="rms_x_bwd")

    fold = lambda g, reps: jnp.sum(g.reshape(reps, -1), axis=0, keepdims=True)
    return dict(
        sq=sq, grad_x=grad_x, d_wmain=d_wmain, d_wfb=d_wfb, d_wmk=d_wmk, d_wa=d_wa, d_wb=d_wb, d_wc=d_wc, d_wo=d_wo,
        d_gain=d_gain, d_mem_gain=d_mem_gain, d_bf=d_bf[:, :N_FORGET],
        d_qga=fold(d_qga, 12), d_kga=fold(d_kga, 4), d_sinks=d_sinks[:, :A_HEADS],
        d_qgb=fold(d_qgb, 12), d_kgb=fold(d_kgb, 12), d_qgc=fold(d_qgc, 4), d_kgc=fold(d_kgc, 4))


PACK_ROWS = 256
FORGET_IN_SHARD = FORGET_COL - SHARD_COLS
AFTER_FORGET = FORGET_COL - SLAB_START[1]
END_CHIP1 = 2 * SHARD_COLS - N_FORGET - SLAB_START[1]


def _pack_w_in(chip, w):
    rows = w.shape[0]
    tr = PACK_ROWS

    def body(k_ref, w_ref, o_ref, scr):
        scr[...] = jnp.zeros_like(scr)
        scr[:, pl.ds(0, SHARD_COLS)] = w_ref[...]
        v = scr[...]
        k = k_ref[0]
        col = lax.broadcasted_iota(jnp.int32, (tr, SLAB), 1)
        no_forget = jnp.zeros((tr, LANE), BF16)

        @pl.when(k == 0)
        def _():
            o_ref[:, 0:SLAB] = v.astype(BF16)
            o_ref[:, SLAB:] = no_forget

        @pl.when(k == 1)
        def _():
            before = pltpu.roll(v, SLAB_SHIFT[1], axis=1)
            after = pltpu.roll(v, SLAB - (N_FORGET - SLAB_SHIFT[1]), axis=1)
            slab = jnp.where(col < AFTER_FORGET, before, jnp.where(col < END_CHIP1, after, 0.0))
            o_ref[:, 0:SLAB] = slab.astype(BF16)
            f = pltpu.roll(v, SLAB - FORGET_IN_SHARD, axis=1)[:, :LANE]
            o_ref[:, SLAB:] = jnp.where(col[:, :LANE] < N_FORGET, f, 0.0).astype(BF16)

        for kk in (2, 3):
            @pl.when(k == kk)
            def _(kk=kk):
                o_ref[:, 0:SLAB] = pltpu.roll(v, SLAB_SHIFT[kk], axis=1).astype(BF16)
                o_ref[:, SLAB:] = no_forget

    return pl.pallas_call(
        body, grid_spec=pltpu.PrefetchScalarGridSpec(
            num_scalar_prefetch=1, grid=(rows // tr,),
            in_specs=[pl.BlockSpec((tr, SHARD_COLS), lambda i, k: (i, 0))],
            out_specs=pl.BlockSpec((None, tr, SLAB + LANE), lambda i, k: (k[0], i, 0)),
            scratch_shapes=[pltpu.VMEM((tr, SLAB), F32)]),
        out_shape=SDS((N_CHIPS, rows, SLAB + LANE), BF16), name="pack_w_in",
        compiler_params=_params(("arbitrary",)))(chip, w)


def _merge_slabs(g):
    rows = g.shape[1]
    tr = PACK_ROWS
    t = [s // LANE for s in SLAB_START]
    n_t = SLAB // LANE

    def body(g_ref, m_ref, f_ref):
        for k in range(N_CHIPS):
            lo = t[k] + (1 if k > 0 else 0)
            hi = t[k + 1] if k + 1 < N_CHIPS else t[k] + n_t
            m_ref[:, lo * LANE:hi * LANE] = g_ref[k, :, (lo - t[k]) * LANE:(hi - t[k]) * LANE]
            if k + 1 < N_CHIPS:
                a = g_ref[k, :, (hi - t[k]) * LANE:(hi - t[k] + 1) * LANE].astype(F32)
                b = g_ref[k + 1, :, 0:LANE].astype(F32)
                m_ref[:, hi * LANE:(hi + 1) * LANE] = (a + b).astype(BF16)
        f_ref[...] = g_ref[1, :, SLAB:]

    return pl.pallas_call(
        body, grid=(rows // tr,),
        in_specs=[pl.BlockSpec((N_CHIPS, tr, SLAB + LANE), lambda i: (0, i, 0))],
        out_specs=[_rowblk(tr, P_MAIN), _rowblk(tr, LANE)],
        out_shape=[SDS((rows, P_MAIN), BF16), SDS((rows, LANE), BF16)], name="merge_slabs",
        compiler_params=_params(("parallel",)))(g)


def _adamw_math(w, g, m, v):
    nm = ADAM_B1 * m + (1.0 - ADAM_B1) * g
    nv = ADAM_B2 * v + (1.0 - ADAM_B2) * (g * g)
    m_hat = nm / (1.0 - ADAM_B1 ** ADAM_STEP)
    v_hat = nv / (1.0 - ADAM_B2 ** ADAM_STEP)
    delta = -ADAM_LR * (m_hat / (jnp.sqrt(v_hat) + ADAM_EPS) + ADAM_WD * w)
    return delta, nm, nv


def _adamw(g, w, m, v, *, tr, name):
    rows, cols = w.shape
    tr = min(tr, rows)

    def body(g_ref, w_ref, m_ref, v_ref, d_ref, nm_ref, nv_ref):
        d, nm, nv = _adamw_math(w_ref[...], g_ref[...], m_ref[...], v_ref[...])
        d_ref[...] = d
        nm_ref[...] = nm
        nv_ref[...] = nv

    spec = _rowblk(tr, cols)
    return pl.pallas_call(
        body, grid=(rows // tr,), in_specs=[spec] * 4, out_specs=[spec] * 3,
        out_shape=[SDS((rows, cols), F32)] * 3, name=name, compiler_params=_params(("parallel",)))(g, w, m, v)


def _adamw_w_in(chip_core, slab_mine, slab_theirs, forget_mine, forget_theirs, w, m, v):
    rows = w.shape[0]
    tr = PACK_ROWS // 2
    nbh = rows // 2 // tr

    def body(k_ref, sa_ref, sb_ref, fa_ref, fb_ref, w_ref, m_ref, v_ref, g_ref, d_ref, nm_ref, nv_ref):
        use_mine = pl.program_id(0) // nbh == k_ref[1]
        sl = jnp.where(use_mine, sa_ref[...], sb_ref[...])
        f_tile = jnp.where(use_mine, fa_ref[...], fb_ref[...])
        k = k_ref[0]

        def emit(wide):
            g = wide[:, :SHARD_COLS]
            g_ref[...] = g
            d, nm, nv = _adamw_math(w_ref[...], g, m_ref[...], v_ref[...])
            d_ref[...] = d
            nm_ref[...] = nm
            nv_ref[...] = nv

        @pl.when(k == 0)
        def _():
            emit(sl)

        @pl.when(k == 1)
        def _():
            col = lax.broadcasted_iota(jnp.int32, (tr, SLAB), 1)
            before = pltpu.roll(sl, SLAB - SLAB_SHIFT[1], axis=1)
            after = pltpu.roll(sl, N_FORGET - SLAB_SHIFT[1], axis=1)
            wide_f = jnp.concatenate([f_tile, jnp.zeros((tr, SLAB - LANE), F32)], axis=1)
            forget = pltpu.roll(wide_f, FORGET_IN_SHARD, axis=1)
            emit(jnp.where(col < FORGET_IN_SHARD, before, jnp.where(col < FORGET_IN_SHARD + N_FORGET, forget, after)))

        for kk in (2, 3):
            @pl.when(k == kk)
            def _(kk=kk):
                emit(pltpu.roll(sl, SLAB - SLAB_SHIFT[kk], axis=1))

    nat = pl.BlockSpec((tr, SHARD_COLS), lambda i, k: (i, 0))
    half = lambda width: pl.BlockSpec((tr, width), lambda i, k: (i % nbh, 0))
    return pl.pallas_call(
        body, grid_spec=pltpu.PrefetchScalarGridSpec(
            num_scalar_prefetch=1, grid=(rows // tr,),
            in_specs=[half(SLAB), half(SLAB), half(LANE), half(LANE), nat, nat, nat],
            out_specs=[nat] * 4),
        out_shape=[SDS((rows, SHARD_COLS), F32)] * 4, name="adamw_w_in",
        compiler_params=_params(("arbitrary",)))(chip_core, slab_mine, slab_theirs, forget_mine, forget_theirs, w, m, v)


ANY = pl.BlockSpec(memory_space=pl.ANY)
HALF_AXIS = (0, 0, 1, 0, 0, 0, 1)


def _me():
    return lax.axis_index("x"), lax.axis_index("y"), lax.axis_index("c")


def _half(ref, which, axis):
    n = ref.shape[axis] // 2
    sl = pl.ds(which * n, n)
    return ref.at[sl] if axis == 0 else ref.at[:, sl]


def _piece(t, ref, j):
    if t == 0:
        return ref.at[:, pl.ds(SLAB_START[j], SLAB)]
    if t == 1:
        return ref
    if t in (2, 6):
        return ref.at[pl.ds(512 * j, 512)]
    return ref.at[:, pl.ds(512 * j, 512)]


def _piece_shape(t, shape):
    if t == 0:
        return (shape[0], SLAB)
    if t == 1:
        return shape
    if t in (2, 6):
        return (512, shape[1])
    return (shape[0], 512)


def _all_gather(slabs, parts):
    parts = [slabs] + list(parts)
    n = len(parts)

    def body(*refs):
        ins, outs = refs[:n], refs[n:2 * n]
        send, recv, fsend, frecv, lsem = refs[2 * n:]
        x, y, c = _me()
        k = 2 * x + y
        sib = (x, y, 1 - c)
        chips = [(1 - x, y), (x, 1 - y), (1 - x, 1 - y)]

        def rows(t, which):
            h = outs[t].shape[1] // 2
            return pl.ds(which * h, h)

        def mine(t):
            return ins[t].at[k, rows(t, c)] if t == 0 else ins[t].at[rows(t, c)]

        local = [pltpu.make_async_copy(ins[t], outs[t].at[k], lsem.at[t]) for t in range(1, n)]
        for cp in local:
            cp.start()

        def first(t, j, chip):
            return pltpu.make_async_remote_copy(
                src_ref=mine(t), dst_ref=outs[t].at[k, rows(t, c)],
                send_sem=send.at[t, j], recv_sem=recv.at[t, j], device_id=(chip[0], chip[1], c), device_id_type=MESH)

        def passed(t, j, chip, which):
            kj = 2 * chip[0] + chip[1]
            blk = outs[t].at[kj, rows(t, which)]
            return pltpu.make_async_remote_copy(
                src_ref=blk, dst_ref=blk, send_sem=fsend.at[t, j], recv_sem=frecv.at[t, j],
                device_id=sib, device_id_type=MESH)

        firsts = [first(t, j, chip) for j, chip in enumerate(chips) for t in range(n)]
        for cp in firsts:
            cp.start()
        passes = []
        for j, chip in enumerate(chips):
            for t in range(n):
                kj = 2 * chip[0] + chip[1]
                pltpu.make_async_remote_copy(
                    src_ref=mine(t), dst_ref=outs[t].at[kj, rows(t, c)],
                    send_sem=send.at[t, j], recv_sem=recv.at[t, j], device_id=(chip[0], chip[1], c),
                    device_id_type=MESH).wait_recv()
                cp = passed(t, j, chip, c)
                cp.start()
                passes.append(cp)
        for j, chip in enumerate(chips):
            for t in range(n):
                passed(t, j, chip, 1 - c).wait_recv()
        for cp in firsts + passes:
            cp.wait_send()
        for cp in local:
            cp.wait()

    return pl.pallas_call(
        body, in_specs=[ANY] * n, out_specs=[ANY] * n,
        out_shape=[SDS(slabs.shape, slabs.dtype)] + [SDS((N_CHIPS,) + p.shape, p.dtype) for p in parts[1:]],
        scratch_shapes=[pltpu.SemaphoreType.DMA((n, 3))] * 4 + [pltpu.SemaphoreType.DMA((n,))],
        input_output_aliases={0: 0}, name="all_gather_weights")(*parts)


def _exchange_halves(arrs):
    n = len(arrs)

    def body(*refs):
        ins, outs = refs[:n], refs[n:2 * n]
        send, recv = refs[2 * n:]
        x, y, c = _me()
        cps = [pltpu.make_async_remote_copy(
            src_ref=_half(ins[t], 1 - c, HALF_AXIS[t]), dst_ref=outs[t], send_sem=send.at[t], recv_sem=recv.at[t],
            device_id=(x, y, 1 - c), device_id_type=MESH) for t in range(n)]
        for cp in cps:
            cp.start()
        for cp in cps:
            cp.wait()

    def hshape(t):
        s = list(arrs[t].shape)
        s[HALF_AXIS[t]] //= 2
        return tuple(s)

    return pl.pallas_call(
        body, in_specs=[ANY] * n, out_specs=[ANY] * n,
        out_shape=[SDS(hshape(t), arrs[t].dtype) for t in range(n)],
        scratch_shapes=[pltpu.SemaphoreType.DMA((n,))] * 2, name="exchange_halves")(*arrs)


def _add_half(full, got, core, axis, *, name):
    r, c = got.shape
    br, bc = (256 if r % 256 == 0 else 128), min(2048, c)
    off_r = (r // br) if axis == 0 else 0
    off_c = (c // bc) if axis == 1 else 0

    def body(c_ref, a_ref, b_ref, o_ref):
        o_ref[...] = (a_ref[...] + b_ref[...]).astype(BF16)

    return pl.pallas_call(
        body, grid_spec=pltpu.PrefetchScalarGridSpec(
            num_scalar_prefetch=1, grid=(r // br, c // bc),
            in_specs=[pl.BlockSpec((br, bc), lambda i, j, cr: (i + cr[0] * off_r, j + cr[0] * off_c)),
                      pl.BlockSpec((br, bc), lambda i, j, cr: (i, j))],
            out_specs=pl.BlockSpec((br, bc), lambda i, j, cr: (i, j))),
        out_shape=SDS((r, c), BF16), name=name, compiler_params=_params(("parallel", "parallel")))(core, full, got)


def _scatter_pieces(halves):
    n = len(halves)

    def body(*refs):
        ins, outs = refs[:n], refs[n:2 * n]
        send, recv, lsem = refs[2 * n:]
        x, y, c = _me()
        k = 2 * x + y

        def to_chip(t, j):
            return pltpu.make_async_remote_copy(
                src_ref=_piece(t, ins[t], j), dst_ref=outs[t].at[k], send_sem=send.at[t, j], recv_sem=recv.at[t, k],
                device_id=(j // 2, j % 2, c), device_id_type=MESH)

        def from_chip(t, j):
            return pltpu.make_async_remote_copy(
                src_ref=_piece(t, ins[t], j), dst_ref=outs[t].at[j], send_sem=send.at[t, j], recv_sem=recv.at[t, j],
                device_id=(j // 2, j % 2, c), device_id_type=MESH)

        def own(t, j):
            return pltpu.make_async_copy(_piece(t, ins[t], j), outs[t].at[j], lsem.at[t])

        for j in range(N_CHIPS):
            @pl.when(k != j)
            def _(j=j):
                for t in range(n):
                    to_chip(t, j).start()

            @pl.when(k == j)
            def _(j=j):
                for t in range(n):
                    own(t, j).start()

        for j in range(N_CHIPS):
            @pl.when(k != j)
            def _(j=j):
                for t in range(n):
                    from_chip(t, j).wait_recv()
                for t in range(n):
                    to_chip(t, j).wait_send()

            @pl.when(k == j)
            def _(j=j):
                for t in range(n):
                    own(t, j).wait()

    return pl.pallas_call(
        body, in_specs=[ANY] * n, out_specs=[ANY] * n,
        out_shape=[SDS((N_CHIPS,) + _piece_shape(t, halves[t].shape), halves[t].dtype) for t in range(n)],
        scratch_shapes=[pltpu.SemaphoreType.DMA((n, N_CHIPS))] * 2 + [pltpu.SemaphoreType.DMA((n,))],
        name="scatter_pieces")(*halves)


def _sum4(p, *, name):
    _, r, c = p.shape
    br = 256 if r % 256 == 0 else 128

    def body(p_ref, o_ref):
        o_ref[...] = ((p_ref[0].astype(F32) + p_ref[1].astype(F32)) + p_ref[2].astype(F32)) + p_ref[3].astype(F32)

    return pl.pallas_call(
        body, grid=(r // br,), in_specs=[pl.BlockSpec((N_CHIPS, br, c), lambda i: (0, i, 0))],
        out_specs=_rowblk(br, c), out_shape=SDS((r, c), F32), name=name, compiler_params=_params(("parallel",)))(p)


def _swap_halves(sums):
    n = len(sums)

    def body(*refs):
        ins, outs = refs[:n], refs[n:2 * n]
        send, recv = refs[2 * n:]
        x, y, c = _me()
        cps = [pltpu.make_async_remote_copy(
            src_ref=ins[t], dst_ref=outs[t], send_sem=send.at[t], recv_sem=recv.at[t],
            device_id=(x, y, 1 - c), device_id_type=MESH) for t in range(n)]
        for cp in cps:
            cp.start()
        for cp in cps:
            cp.wait()

    return pl.pallas_call(
        body, in_specs=[ANY] * n, out_specs=[ANY] * n,
        out_shape=[SDS(s.shape, F32) for s in sums],
        scratch_shapes=[pltpu.SemaphoreType.DMA((n,))] * 2, name="swap_halves")(*sums)


def _adamw_halves(mine, theirs, core, w, m, v, *, axis, tr, name):
    rows, cols = w.shape

    if axis == 0:
        nbh = rows // 2 // tr
        g_spec = pl.BlockSpec((tr, cols), lambda i, cr: (i % nbh, 0))
    else:
        g_spec = pl.BlockSpec((tr, cols // 2), lambda i, cr: (i, 0))

    def body(c_ref, a_ref, b_ref, w_ref, m_ref, v_ref, g_ref, d_ref, nm_ref, nv_ref):
        a, b = a_ref[...], b_ref[...]
        if axis == 0:
            g = jnp.where(pl.program_id(0) // nbh == c_ref[0], a, b)
        else:
            low = c_ref[0] == 0
            g = jnp.concatenate([jnp.where(low, a, b), jnp.where(low, b, a)], axis=1)
        g_ref[...] = g
        d, nm, nv = _adamw_math(w_ref[...], g, m_ref[...], v_ref[...])
        d_ref[...] = d
        nm_ref[...] = nm
        nv_ref[...] = nv

    nat = pl.BlockSpec((tr, cols), lambda i, cr: (i, 0))
    return pl.pallas_call(
        body, grid_spec=pltpu.PrefetchScalarGridSpec(
            num_scalar_prefetch=1, grid=(rows // tr,), in_specs=[g_spec, g_spec, nat, nat, nat], out_specs=[nat] * 4),
        out_shape=[SDS((rows, cols), F32)] * 4, name=name, compiler_params=_params(("arbitrary",)))(
            core, mine, theirs, w, m, v)


SMALL_ROWS, SMALL_COLS = 8, 1024


def _pack_small(vs):
    flat = jnp.concatenate([v.reshape(-1) for v in vs])
    return jnp.pad(flat, (0, SMALL_ROWS * SMALL_COLS - flat.shape[0])).reshape(SMALL_ROWS, SMALL_COLS)


def _unpack_small(packed, sizes):
    flat = packed.reshape(-1)
    out, o = [], 0
    for n in sizes:
        out.append(flat[o:o + n].reshape(1, n))
        o += n
    return out


def _all_reduce_small(v):
    n_dev = 8

    def body(v_ref, o_ref, land, send, recv):
        x, y, c = _me()
        me = 4 * x + 2 * y + c
        land[me] = v_ref[...]
        cps = []
        for r in range(1, n_dev):
            fx, fy, fc = (r >> 2) & 1, (r >> 1) & 1, r & 1
            peer = (x ^ fx, y ^ fy, c ^ fc)
            cps.append(pltpu.make_async_remote_copy(
                src_ref=v_ref, dst_ref=land.at[me], send_sem=send.at[r - 1], recv_sem=recv.at[r - 1],
                device_id=peer, device_id_type=MESH))
        for cp in cps:
            cp.start()
        for r in range(1, n_dev):
            fx, fy, fc = (r >> 2) & 1, (r >> 1) & 1, r & 1
            src = 4 * (x ^ fx) + 2 * (y ^ fy) + (c ^ fc)
            pltpu.make_async_remote_copy(
                src_ref=v_ref, dst_ref=land.at[src], send_sem=send.at[r - 1], recv_sem=recv.at[r - 1],
                device_id=(x ^ fx, y ^ fy, c ^ fc), device_id_type=MESH).wait_recv()
        for cp in cps:
            cp.wait_send()
        acc = land[0]
        for r in range(1, n_dev):
            acc = acc + land[r]
        o_ref[...] = acc

    vm = pl.BlockSpec(memory_space=pltpu.VMEM)
    return pl.pallas_call(
        body, in_specs=[vm], out_specs=vm, out_shape=SDS(v.shape, F32),
        scratch_shapes=[pltpu.VMEM((n_dev,) + v.shape, F32), pltpu.SemaphoreType.DMA((n_dev - 1,)),
                        pltpu.SemaphoreType.DMA((n_dev - 1,))],
        name="all_reduce_small")(v)


def kernel(x, mem, norm_gain, mem_norm_gain, w_in, b_forget, q_gain_a, k_gain_a, sinks_a, q_gain_b, k_gain_b, q_gain_c, k_gain_c, w_mem_kv, w_branch_a, w_branch_b, w_branch_c, w_out, loss_target, m_norm_gain, m_mem_norm_gain, m_w_in, m_b_forget, m_q_gain_a, m_k_gain_a, m_sinks_a, m_q_gain_b, m_k_gain_b, m_q_gain_c, m_k_gain_c, m_w_mem_kv, m_w_branch_a, m_w_branch_b, m_w_branch_c, m_w_out, v_norm_gain, v_mem_norm_gain, v_w_in, v_b_forget, v_q_gain_a, v_k_gain_a, v_sinks_a, v_q_gain_b, v_k_gain_b, v_q_gain_c, v_k_gain_c, v_w_mem_kv, v_w_branch_a, v_w_branch_b, v_w_branch_c, v_w_out):
    xi, yi, ci = lax.axis_index("x"), lax.axis_index("y"), lax.axis_index("c")
    chip = jnp.reshape(2 * xi + yi, (1,)).astype(jnp.int32)
    core = jnp.reshape(ci, (1,)).astype(jnp.int32)

    slabs = _pack_w_in(chip, w_in[0])
    mine = [w_mem_kv[0].astype(BF16), w_branch_a[0].astype(BF16), w_branch_b[0].astype(BF16),
            w_branch_c[0].astype(BF16), w_out[0].astype(BF16)]
    g_slab, g_mk, g_wa, g_wb, g_wc, g_wo = _all_gather(slabs, mine)
    w_main, w_fb = _merge_slabs(g_slab)
    w_mk = g_mk.reshape(D_MODEL, 1024)
    wo = g_wo.reshape(D_MODEL, D_MODEL)

    r = _local_step(x[0], mem[0], loss_target[0], w_main, w_fb, w_mk, g_wa, g_wb, g_wc, wo, norm_gain, mem_norm_gain,
                    b_forget, q_gain_a, k_gain_a, sinks_a, q_gain_b, k_gain_b, q_gain_c, k_gain_c)

    grads = [r["d_wmain"], r["d_wfb"], r["d_wmk"], r["d_wa"], r["d_wb"], r["d_wc"], r["d_wo"]]
    got = _exchange_halves(grads)
    halves = [_add_half(g, h, core, ax, name=f"add_half_{t}") for t, (g, h, ax) in enumerate(zip(grads, got, HALF_AXIS))]
    parts = _scatter_pieces(halves)
    sums = [_sum4(p, name=f"sum4_{t}") for t, p in enumerate(parts)]
    theirs = _swap_halves(sums)

    small_names = ["d_gain", "d_mem_gain", "d_bf", "d_qga", "d_kga", "d_sinks", "d_qgb", "d_kgb", "d_qgc", "d_kgc"]
    loss_part = (0.5 / D_MODEL) * jnp.sum(r["sq"], axis=1, keepdims=True)
    packed = _pack_small([r[n] for n in small_names] + [loss_part])
    red = _all_reduce_small(packed)
    small_w = [norm_gain, mem_norm_gain, b_forget, q_gain_a, k_gain_a, sinks_a, q_gain_b, k_gain_b, q_gain_c, k_gain_c]
    small_m = [m_norm_gain, m_mem_norm_gain, m_b_forget, m_q_gain_a, m_k_gain_a, m_sinks_a, m_q_gain_b, m_k_gain_b,
               m_q_gain_c, m_k_gain_c]
    small_v = [v_norm_gain, v_mem_norm_gain, v_b_forget, v_q_gain_a, v_k_gain_a, v_sinks_a, v_q_gain_b, v_k_gain_b,
               v_q_gain_c, v_k_gain_c]
    sizes = [w.shape[1] for w in small_w]
    s_d, s_m, s_v = _adamw(red, _pack_small(small_w), _pack_small(small_m), _pack_small(small_v), tr=8, name="adamw_small")
    g_small = _unpack_small(red, sizes + [1])
    loss = g_small[-1].reshape(())
    d_small, m_small, v_small = _unpack_small(s_d, sizes), _unpack_small(s_m, sizes), _unpack_small(s_v, sizes)

    gw_in, dw_in, mw_in, vw_in = _adamw_w_in(jnp.concatenate([chip, core]), sums[0], theirs[0], sums[1], theirs[1],
                                             w_in[0], m_w_in[0], v_w_in[0])
    big = {}
    for t, nm, w, m, v in ((2, "w_mem_kv", w_mem_kv, m_w_mem_kv, v_w_mem_kv),
                           (3, "w_branch_a", w_branch_a, m_w_branch_a, v_w_branch_a),
                           (4, "w_branch_b", w_branch_b, m_w_branch_b, v_w_branch_b),
                           (5, "w_branch_c", w_branch_c, m_w_branch_c, v_w_branch_c),
                           (6, "w_out", w_out, m_w_out, v_w_out)):
        big[nm] = _adamw_halves(sums[t], theirs[t], core, w[0], m[0], v[0], axis=HALF_AXIS[t], tr=128,
                                name="adamw_" + nm)

    def collect(kind):
        sm = (g_small, d_small, m_small, v_small)[kind]
        win = (gw_in, dw_in, mw_in, vw_in)[kind]
        return ([sm[0], sm[1], win[None]] + [a for a in sm[2:10]]
                + [big[n][kind][None] for n in ("w_mem_kv", "w_branch_a", "w_branch_b", "w_branch_c", "w_out")])

    return (loss, r["grad_x"][None], *collect(0), *collect(1), *collect(2), *collect(3))
```

```python
import functools

import numpy as np
import jax
import jax.numpy as jnp
from jax import lax
from jax.experimental import pallas as pl
from jax.experimental.pallas import tpu as pltpu

F32 = jnp.float32
BF16 = jnp.bfloat16
HI = lax.Precision.HIGHEST
SDS = jax.ShapeDtypeStruct
MESH = pl.DeviceIdType.MESH

D_MODEL = 2048
HEAD_DIM = 64
A_HEADS = 12
A_GROUP = 3
B_HEADS = 12
C_HEADS = 4
C_HEAD_DIM = 128
WINDOW = 128
EPS = 1e-6
NEG = -1e30
LANE = 128

QA, KA, VA, ZA = 0, 768, 1024, 1280
QB, KB, VB, ZB = 2048, 2816, 3584, 4352
QC, ZC = 5120, 5632
GATE = 6144
P_MAIN = 12288
N_FORGET = 12
FORGET_COL = 5120
SHARD_COLS = 3075
SLAB = 3200
SLAB_START = (0, 3072, 6016, 9088)
SLAB_SHIFT = (0, 3, 122, 125)
N_CHIPS = 4

ADAM_LR = 0.001
ADAM_B1 = 0.9
ADAM_B2 = 0.999
ADAM_EPS = 1e-08
ADAM_WD = 0.01
ADAM_STEP = 10

VMEM_LIMIT = 56 * 1024 * 1024


def _params(sem, vmem=VMEM_LIMIT):
    return pltpu.CompilerParams(dimension_semantics=sem, vmem_limit_bytes=vmem)


def _win(tr, width, off):
    return pl.BlockSpec((pl.Element(tr), pl.Element(width)), lambda i, *_: (i * tr, off))


def _rowblk(tr, width):
    return pl.BlockSpec((tr, width), lambda i, *_: (i, 0))


def _const(shape):
    nd = len(shape)
    return pl.BlockSpec(shape, lambda *_: (0,) * nd)


def _rms(x, g):
    return x * lax.rsqrt(jnp.mean(x * x, axis=-1, keepdims=True) + EPS) * g


def _head_mean_impl(x2, bd):
    hi = x2.astype(BF16)
    lo = (x2 - hi.astype(F32)).astype(BF16)
    return _dot(hi, bd) + _dot(lo, bd)


@jax.custom_vjp
def _head_mean(x2, bd):
    return _head_mean_impl(x2, bd)


_head_mean.defvjp(lambda x2, bd: (_head_mean_impl(x2, bd), bd),
                  lambda bd, g: (_head_mean_impl(g, bd), jnp.zeros_like(bd)))


def _head_norm(x, g_tiled, bd):
    return x * lax.rsqrt(_head_mean(x * x, bd) + EPS) * g_tiled


def _silu(z):
    return z * jax.nn.sigmoid(z)


def _dot_nt(a, b):
    return lax.dot_general(a, b, (((1,), (1,)), ((), ())), preferred_element_type=F32)


def _dot_tn(a, b):
    return lax.dot_general(a, b, (((0,), (0,)), ((), ())), preferred_element_type=F32)


def _dot(a, b):
    return jnp.dot(a, b, preferred_element_type=F32)


def _swa_fn(qk, vz, qkp, vzp, qg, kg, sinks, bd, bias, first):
    q = _head_norm(qk[:, :768], qg, bd)
    k2 = jnp.concatenate([qkp[:, 768:], qk[:, 768:]], axis=0)
    k2 = _head_norm(k2, kg, bd[:256, :256])
    v2 = jnp.concatenate([vzp[:, :256], vz[:, :256]], axis=0)
    z = vz[:, 256:]
    rows = A_GROUP * WINDOW
    kj = lax.broadcasted_iota(jnp.int32, (rows, 2 * WINDOW), 1)
    no_prev = kj < WINDOW * first.astype(jnp.int32)
    qb = q.astype(BF16)
    kb = k2.astype(BF16)
    vb = v2.astype(BF16)
    outs = [None] * A_HEADS
    for g in range(A_HEADS // A_GROUP):
        heads = [A_GROUP * g + u for u in range(A_GROUP)]
        qs = jnp.concatenate([qb[:, 64 * h:64 * h + 64] for h in heads], axis=0)
        s = _dot_nt(qs, kb[:, 64 * g:64 * g + 64]) * (HEAD_DIM ** -0.5) + bias[g]
        s = jnp.where(no_prev, NEG, s)
        sink = jnp.concatenate([jnp.broadcast_to(sinks[:, h:h + 1], (WINDOW, 1)) for h in heads], axis=0)
        m = lax.stop_gradient(jnp.maximum(jnp.max(s, axis=-1, keepdims=True), sink))
        p = jnp.exp(s - m)
        den = jnp.sum(p, axis=-1, keepdims=True) + jnp.exp(sink - m)
        o = _dot((p * (1.0 / den)).astype(BF16), vb[:, 64 * g:64 * g + 64])
        for u, h in enumerate(heads):
            outs[h] = o[WINDOW * u:WINDOW * u + WINDOW, :]
    return jnp.concatenate(outs, axis=1) * _silu(z)


def _swa_bias():
    qi = np.arange(WINDOW)[:, None]
    kj = np.arange(2 * WINDOW)[None, :]
    rel = qi + WINDOW - kj
    valid = (rel >= 0) & (rel < WINDOW)
    out = np.zeros((A_HEADS // A_GROUP, A_GROUP * WINDOW, 2 * WINDOW), np.float32)
    for h in range(A_HEADS):
        slope = np.float32(2.0 ** (-8.0 * (h + 1) / A_HEADS))
        blk = np.where(valid, -slope * rel.astype(np.float32), np.float32(NEG))
        g, u = divmod(h, A_GROUP)
        out[g, WINDOW * u:WINDOW * u + WINDOW, :] = blk
    return jnp.asarray(out)


def _mem_fn(qz, mkv, qg, kg, bd):
    q = _head_norm(qz[:, :512], qg, bd).astype(BF16)
    k = _head_norm(mkv[:, :512], kg, bd).astype(BF16)
    v = mkv[:, 512:].astype(BF16)
    z = qz[:, 512:]
    outs = []
    for h in range(C_HEADS):
        sl = slice(128 * h, 128 * h + 128)
        s = _dot_nt(q[:, sl], k[:, sl]) * (C_HEAD_DIM ** -0.5)
        m = lax.stop_gradient(jnp.max(s, axis=-1, keepdims=True))
        p = jnp.exp(s - m)
        den = jnp.sum(p, axis=-1, keepdims=True)
        outs.append(_dot((p * (1.0 / den)).astype(BF16), v[:, sl]))
    return jnp.concatenate(outs, axis=1) * _silu(z)


def _qn_fn(q, g, bd):
    return _head_norm(q, g, bd) * (HEAD_DIM ** -0.5)


def _kn_fn(k, g, bd):
    return _head_norm(k, g, bd)


def _block_diag(width, hd):
    i = np.arange(width) // hd
    return jnp.asarray((i[:, None] == i[None, :]).astype(np.float32) / hd, BF16)


def _head_sum(width, hd):
    i = np.arange(width) // hd
    return jnp.asarray((i[:, None] == np.arange(LANE)[None, :]).astype(np.float32))


def _rms_fwd(x, g, *, tr, name):
    rows, dm = x.shape

    def body(x_ref, g_ref, o_ref):
        o_ref[...] = _rms(x_ref[...], g_ref[...]).astype(BF16)

    return pl.pallas_call(
        body, grid=(rows // tr,),
        in_specs=[_rowblk(tr, dm), _const((1, dm))],
        out_specs=_rowblk(tr, dm),
        out_shape=SDS((rows, dm), BF16), name=name,
        compiler_params=_params(("parallel",)))(x, g)


def _rms_bwd(x, g, dy, resid, *, tr, name):
    rows, dm = x.shape
    want_dx = resid is not None

    def body(*refs):
        if want_dx:
            x_ref, g_ref, dy_ref, r_ref, dx_ref, dg_ref = refs
        else:
            x_ref, g_ref, dy_ref, dg_ref = refs
        _, vjp = jax.vjp(_rms, x_ref[...], g_ref[...])
        dx, dg = vjp(dy_ref[...])

        @pl.when(pl.program_id(0) == 0)
        def _():
            dg_ref[...] = jnp.zeros_like(dg_ref)

        dg_ref[...] += dg
        if want_dx:
            dx_ref[...] = r_ref[...] + dx

    ins = [x, g, dy] + ([resid] if want_dx else [])
    in_specs = [_rowblk(tr, dm), _const((1, dm)), _rowblk(tr, dm)] + ([_rowblk(tr, dm)] if want_dx else [])
    out_specs = ([_rowblk(tr, dm)] if want_dx else []) + [_const((1, dm))]
    out_shape = ([SDS((rows, dm), F32)] if want_dx else []) + [SDS((1, dm), F32)]
    return pl.pallas_call(
        body, grid=(rows // tr,), in_specs=in_specs, out_specs=out_specs, out_shape=out_shape, name=name,
        compiler_params=_params(("arbitrary",)))(*ins)


def _matmul(a, b, *, dims, out_dtype, tm, tn, tk, name, add=None):
    if dims == "tn":
        kdim, m = a.shape
    else:
        m, kdim = a.shape
    n = b.shape[0] if dims == "nt" else b.shape[1]
    tm, tn, tk = min(tm, m), min(tn, n), min(tk, kdim)
    assert m % tm == 0 and n % tn == 0 and kdim % tk == 0, (name, a.shape, b.shape)
    nk = kdim // tk
    has_add = add is not None

    def body(*refs):
        if has_add:
            a_ref, b_ref, add_ref, o_ref, acc = refs
        else:
            a_ref, b_ref, o_ref, acc = refs
        k = pl.program_id(2)
        if dims == "nn":
            part = _dot(a_ref[...], b_ref[...])
        elif dims == "nt":
            part = _dot_nt(a_ref[...], b_ref[...])
        else:
            part = _dot_tn(a_ref[...], b_ref[...])

        @pl.when(k == 0)
        def _():
            acc[...] = part + add_ref[...] if has_add else part

        @pl.when(k > 0)
        def _():
            acc[...] += part

        @pl.when(k == nk - 1)
        def _():
            o_ref[...] = acc[...].astype(out_dtype)

    a_spec = pl.BlockSpec((tk, tm), lambda i, j, k: (k, i)) if dims == "tn" else pl.BlockSpec((tm, tk), lambda i, j, k: (i, k))
    b_spec = pl.BlockSpec((tn, tk), lambda i, j, k: (j, k)) if dims == "nt" else pl.BlockSpec((tk, tn), lambda i, j, k: (k, j))
    o_spec = pl.BlockSpec((tm, tn), lambda i, j, k: (i, j))
    ins = [a, b] + ([add] if has_add else [])
    in_specs = [a_spec, b_spec] + ([o_spec] if has_add else [])
    return pl.pallas_call(
        body, grid=(m // tm, n // tn, nk), in_specs=in_specs, out_specs=o_spec,
        out_shape=SDS((m, n), out_dtype), scratch_shapes=[pltpu.VMEM((tm, tn), F32)], name=name,
        compiler_params=_params(("parallel", "parallel", "arbitrary")))(*ins)


def _swa_specs(nb):
    prev = lambda off: pl.BlockSpec((pl.Element(WINDOW), pl.Element(1024)),
                                    lambda n: (jnp.maximum(n - 1, 0) * WINDOW, off))
    return [_win(WINDOW, 1024, QA), _win(WINDOW, 1024, VA), prev(QA), prev(VA),
            _const((1, 768)), _const((1, 256)), _const((1, LANE)), _const((768, 768)),
            _const((A_HEADS // A_GROUP, A_GROUP * WINDOW, 2 * WINDOW))]


def _swa_fwd(proj, qg, kg, sinks, bd, bias):
    s = proj.shape[0]
    nb = s // WINDOW

    def body(qk_ref, vz_ref, qkp_ref, vzp_ref, qg_ref, kg_ref, sk_ref, bd_ref, bias_ref, o_ref):
        first = pl.program_id(0) == 0
        o_ref[...] = _swa_fn(qk_ref[...], vz_ref[...], qkp_ref[...], vzp_ref[...], qg_ref[...], kg_ref[...],
                             sk_ref[...], bd_ref[...], bias_ref[...], first).astype(BF16)

    return pl.pallas_call(
        body, grid=(nb,), in_specs=_swa_specs(nb), out_specs=_rowblk(WINDOW, 768),
        out_shape=SDS((s, 768), BF16), name="swa_fwd",
        compiler_params=_params(("parallel",)))(proj, proj, proj, proj, qg, kg, sinks, bd, bias)


def _swa_bwd(proj, qg, kg, sinks, bd, bias, dga):
    s = proj.shape[0]
    nb = s // WINDOW

    def body(qk_ref, vz_ref, qkp_ref, vzp_ref, qg_ref, kg_ref, sk_ref, bd_ref, bias_ref, dg_ref,
             dcur_ref, dprev_ref, dqg_ref, dkg_ref, dsk_ref):
        first = pl.program_id(0) == 0
        bd_v = bd_ref[...]
        bias_v = bias_ref[...]
        fn = lambda qk, vz, qkp, vzp, qg_, kg_, sk: _swa_fn(qk, vz, qkp, vzp, qg_, kg_, sk, bd_v, bias_v, first)
        _, vjp = jax.vjp(fn, qk_ref[...], vz_ref[...], qkp_ref[...], vzp_ref[...], qg_ref[...], kg_ref[...], sk_ref[...])
        dqk, dvz, dqkp, dvzp, dqg, dkg, dsk = vjp(dg_ref[...])

        @pl.when(first)
        def _():
            dqg_ref[...] = jnp.zeros_like(dqg_ref)
            dkg_ref[...] = jnp.zeros_like(dkg_ref)
            dsk_ref[...] = jnp.zeros_like(dsk_ref)

        dqg_ref[...] += dqg
        dkg_ref[...] += dkg
        dsk_ref[...] += dsk
        dcur_ref[...] = jnp.concatenate([dqk, dvz], axis=1)
        dprev_ref[...] = jnp.concatenate([dqkp[:, 768:], dvzp[:, :256]], axis=1)

    return pl.pallas_call(
        body, grid=(nb,), in_specs=_swa_specs(nb) + [_rowblk(WINDOW, 768)],
        out_specs=[_rowblk(WINDOW, 2048), pl.BlockSpec((None, WINDOW, 512), lambda n: (n, 0, 0)),
                   _const((1, 768)), _const((1, 256)), _const((1, LANE))],
        out_shape=[SDS((s, 2048), F32), SDS((nb, WINDOW, 512), F32), SDS((1, 768), F32), SDS((1, 256), F32),
                   SDS((1, LANE), F32)],
        name="swa_bwd", compiler_params=_params(("arbitrary",)))(proj, proj, proj, proj, qg, kg, sinks, bd, bias, dga)


def _swa_combine(dcur, dprev):
    s = dcur.shape[0]
    nb = s // WINDOW

    def body(c_ref, p_ref, o_ref):
        c = c_ref[...]
        nxt = jnp.where(pl.program_id(0) == nb - 1, 0.0, p_ref[...])
        o_ref[...] = jnp.concatenate([c[:, :768], c[:, 768:1280] + nxt, c[:, 1280:]], axis=1).astype(BF16)

    return pl.pallas_call(
        body, grid=(nb,),
        in_specs=[_rowblk(WINDOW, 2048), pl.BlockSpec((None, WINDOW, 512), lambda n: (jnp.minimum(n + 1, nb - 1), 0, 0))],
        out_specs=_rowblk(WINDOW, 2048), out_shape=SDS((s, 2048), BF16), name="swa_combine",
        compiler_params=_params(("parallel",)))(dcur, dprev)


def _mem_fwd(proj, mkv, qg, kg, bd, *, tr):
    s = proj.shape[0]

    def body(qz_ref, mkv_ref, qg_ref, kg_ref, bd_ref, o_ref):
        o_ref[...] = _mem_fn(qz_ref[...], mkv_ref[...], qg_ref[...], kg_ref[...], bd_ref[...]).astype(BF16)

    return pl.pallas_call(
        body, grid=(s // tr,),
        in_specs=[_win(tr, 1024, QC), _const(mkv.shape), _const((1, 512)), _const((1, 512)), _const((512, 512))],
        out_specs=_rowblk(tr, 512), out_shape=SDS((s, 512), BF16), name="mem_fwd",
        compiler_params=_params(("parallel",)))(proj, mkv, qg, kg, bd)


def _mem_bwd(proj, mkv, qg, kg, bd, dgc, *, tr):
    s = proj.shape[0]

    def body(qz_ref, mkv_ref, qg_ref, kg_ref, bd_ref, dg_ref, dqz_ref, dmkv_ref, dqg_ref, dkg_ref):
        bd_v = bd_ref[...]
        fn = lambda qz, mkv_, qg_, kg_: _mem_fn(qz, mkv_, qg_, kg_, bd_v)
        _, vjp = jax.vjp(fn, qz_ref[...], mkv_ref[...], qg_ref[...], kg_ref[...])
        dqz, dmkv, dqg, dkg = vjp(dg_ref[...])

        @pl.when(pl.program_id(0) == 0)
        def _():
            dmkv_ref[...] = jnp.zeros_like(dmkv_ref)
            dqg_ref[...] = jnp.zeros_like(dqg_ref)
            dkg_ref[...] = jnp.zeros_like(dkg_ref)

        dmkv_ref[...] += dmkv
        dqg_ref[...] += dqg
        dkg_ref[...] += dkg
        dqz_ref[...] = dqz.astype(BF16)

    return pl.pallas_call(
        body, grid=(s // tr,),
        in_specs=[_win(tr, 1024, QC), _const(mkv.shape), _const((1, 512)), _const((1, 512)), _const((512, 512)),
                  _rowblk(tr, 512)],
        out_specs=[_rowblk(tr, 1024), _const(mkv.shape), _const((1, 512)), _const((1, 512))],
        out_shape=[SDS((s, 1024), BF16), SDS(mkv.shape, F32), SDS((1, 512), F32), SDS((1, 512), F32)],
        name="mem_bwd", compiler_params=_params(("arbitrary",)))(proj, mkv, qg, kg, bd, dgc)


def _log_sigmoid(x):
    return jnp.minimum(x, 0.0) - jnp.log1p(jnp.exp(-jnp.abs(x)))


def _fox_prep(proj, fbl, qg, kg, bfor, bd, *, tr):
    s = proj.shape[0]
    tri = jnp.asarray(np.tril(np.ones((tr, tr), np.float32)))

    def body(q_ref, k_ref, fb_ref, qg_ref, kg_ref, bf_ref, bd_ref, tri_ref, qn_ref, kn_ref, cq_ref, ck_ref, carry):
        @pl.when(pl.program_id(0) == 0)
        def _():
            carry[...] = jnp.zeros_like(carry)

        bd_v = bd_ref[...]
        qn_ref[...] = _qn_fn(q_ref[...], qg_ref[...], bd_v).astype(BF16)
        kn_ref[...] = _kn_fn(k_ref[...], kg_ref[...], bd_v).astype(BF16)
        lane = lax.broadcasted_iota(jnp.int32, (tr, LANE), 1)
        logf = jnp.where(lane < N_FORGET, _log_sigmoid(fb_ref[...] + bf_ref[...]), 0.0)
        c = jnp.dot(tri_ref[...], logf, precision=HI, preferred_element_type=F32) + carry[...]
        cq_ref[...] = c
        ck_ref[...] = jnp.transpose(c)
        carry[...] = c[tr - 1:tr, :]

    return pl.pallas_call(
        body, grid=(s // tr,),
        in_specs=[_win(tr, 768, QB), _win(tr, 768, KB), _rowblk(tr, LANE), _const((1, 768)), _const((1, 768)),
                  _const((1, LANE)), _const((768, 768)), _const((tr, tr))],
        out_specs=[_rowblk(tr, 768), _rowblk(tr, 768), _rowblk(tr, LANE), pl.BlockSpec((LANE, tr), lambda i: (0, i))],
        out_shape=[SDS((s, 768), BF16), SDS((s, 768), BF16), SDS((s, LANE), F32), SDS((LANE, s), F32)],
        scratch_shapes=[pltpu.VMEM((1, LANE), F32)], name="fox_prep",
        compiler_params=_params(("arbitrary",)))(proj, proj, fbl, qg, kg, bfor, bd, tri)


def _fox_tiles(s):
    tq = min(256, s)
    tk = min(512, s)
    return tq, tk


def _fox_fwd(proj, qn, kn, cq, ck):
    s = proj.shape[0]
    tq, tk = _fox_tiles(s)
    nq, nk = s // tq, s // tk

    def last_k(i):
        return (i * tq + tq - 1) // tk

    def body(q_ref, k_ref, v_ref, cq_ref, ck_ref, z_ref, gb_ref, yb_ref, lse_ref, acc, m_s, l_s):
        i, j = pl.program_id(0), pl.program_id(1)

        @pl.when(j == 0)
        def _():
            acc[...] = jnp.zeros_like(acc)
            m_s[...] = jnp.full_like(m_s, NEG)
            l_s[...] = jnp.ones_like(l_s)

        @pl.when(j <= last_k(i))
        def _():
            qpos = i * tq + lax.broadcasted_iota(jnp.int32, (tq, tk), 0)
            kpos = j * tk + lax.broadcasted_iota(jnp.int32, (tq, tk), 1)
            mask = kpos <= qpos
            q, k = q_ref[...], k_ref[...]
            v = v_ref[...].astype(BF16)
            cqv, ckv = cq_ref[...], ck_ref[...]
            m_all, l_all = m_s[...], l_s[...]
            lane = lax.broadcasted_iota(jnp.int32, (tq, LANE), 1)
            m_out, l_out = m_all, l_all
            for hp in range(B_HEADS // 2):
                acc_pair = acc[:, 128 * hp:128 * hp + 128]
                new = []
                for u in range(2):
                    h = 2 * hp + u
                    sl = slice(64 * h, 64 * h + 64)
                    sc = _dot_nt(q[:, sl], k[:, sl]) + cqv[:, h:h + 1] - ckv[h:h + 1, :]
                    sc = jnp.where(mask, sc, NEG)
                    m_prev = m_all[:, h:h + 1]
                    m_new = jnp.maximum(m_prev, jnp.max(sc, axis=-1, keepdims=True))
                    alpha = jnp.exp(m_prev - m_new)
                    p = jnp.exp(sc - m_new)
                    l_new = alpha * l_all[:, h:h + 1] + jnp.sum(p, axis=-1, keepdims=True)
                    new.append(alpha * acc_pair[:, 64 * u:64 * u + 64] + _dot(p.astype(BF16), v[:, sl]))
                    m_out = jnp.where(lane == h, m_new, m_out)
                    l_out = jnp.where(lane == h, l_new, l_out)
                acc[:, 128 * hp:128 * hp + 128] = jnp.concatenate(new, axis=1)
            m_s[...] = m_out
            l_s[...] = l_out

        @pl.when(j == nk - 1)
        def _():
            l_all = l_s[...]
            inv = 1.0 / l_all
            a = acc[...]
            y = jnp.concatenate([a[:, 64 * h:64 * h + 64] * inv[:, h:h + 1] for h in range(B_HEADS)], axis=1)
            yb_ref[...] = y
            gb_ref[...] = (y * _silu(z_ref[...])).astype(BF16)
            lse_ref[...] = m_s[...] + jnp.log(l_all)

    kmap = lambda i, j: (jnp.minimum(j, last_k(i)), 0)
    return pl.pallas_call(
        body, grid=(nq, nk),
        in_specs=[pl.BlockSpec((tq, 768), lambda i, j: (i, 0)),
                  pl.BlockSpec((tk, 768), kmap),
                  pl.BlockSpec((pl.Element(tk), pl.Element(768)), lambda i, j: (jnp.minimum(j, last_k(i)) * tk, VB)),
                  pl.BlockSpec((tq, LANE), lambda i, j: (i, 0)),
                  pl.BlockSpec((16, tk), lambda i, j: (0, jnp.minimum(j, last_k(i)))),
                  pl.BlockSpec((pl.Element(tq), pl.Element(768)), lambda i, j: (i * tq, ZB))],
        out_specs=[pl.BlockSpec((tq, 768), lambda i, j: (i, 0)), pl.BlockSpec((tq, 768), lambda i, j: (i, 0)),
                   pl.BlockSpec((tq, LANE), lambda i, j: (i, 0))],
        out_shape=[SDS((s, 768), BF16), SDS((s, 768), F32), SDS((s, LANE), F32)],
        scratch_shapes=[pltpu.VMEM((tq, 768), F32), pltpu.VMEM((tq, LANE), F32), pltpu.VMEM((tq, LANE), F32)],
        name="fox_fwd", compiler_params=_params(("parallel", "arbitrary")))(qn, kn, proj, cq, ck, proj)


def _fox_bwd_pre(proj, yb, dgb, hsum, *, tr):
    s = proj.shape[0]

    def body(z_ref, y_ref, dg_ref, hs_ref, dy_ref, dz_ref, dl_ref):
        z, y, dg = z_ref[...], y_ref[...], dg_ref[...]
        sg = jax.nn.sigmoid(z)
        dy = dg * (z * sg)
        dy_ref[...] = dy.astype(BF16)
        dz_ref[...] = (dg * y * (sg * (1.0 + z * (1.0 - sg)))).astype(BF16)
        dl_ref[...] = jnp.dot(dy * y, hs_ref[...], precision=HI, preferred_element_type=F32)

    return pl.pallas_call(
        body, grid=(s // tr,),
        in_specs=[_win(tr, 768, ZB), _rowblk(tr, 768), _rowblk(tr, 768), _const((768, LANE))],
        out_specs=[_rowblk(tr, 768), _rowblk(tr, 768), _rowblk(tr, LANE)],
        out_shape=[SDS((s, 768), BF16), SDS((s, 768), BF16), SDS((s, LANE), F32)], name="fox_bwd_pre",
        compiler_params=_params(("parallel",)))(proj, yb, dgb, hsum)


def _fox_bwd(proj, qn, kn, cq, ck, lse, delta, dyb):
    s = proj.shape[0]
    tq, tk = _fox_tiles(s)
    nq, nk = s // tq, s // tk

    def first_q(j):
        return (j * tk) // tq

    def body(q_ref, k_ref, v_ref, cq_ref, ck_ref, lse_ref, dl_ref, dy_ref,
             dq_ref, dcq_ref, dk_ref, dv_ref, dck_ref, dk_acc, dv_acc, dck_acc):
        j, i = pl.program_id(0), pl.program_id(1)

        @pl.when((j == 0) & (i == 0))
        def _():
            dq_ref[...] = jnp.zeros_like(dq_ref)
            dcq_ref[...] = jnp.zeros_like(dcq_ref)

        @pl.when(i == 0)
        def _():
            dk_acc[...] = jnp.zeros_like(dk_acc)
            dv_acc[...] = jnp.zeros_like(dv_acc)
            dck_acc[...] = jnp.zeros_like(dck_acc)

        @pl.when(i >= first_q(j))
        def _():
            qpos = i * tq + lax.broadcasted_iota(jnp.int32, (tq, tk), 0)
            kpos = j * tk + lax.broadcasted_iota(jnp.int32, (tq, tk), 1)
            mask = kpos <= qpos
            q, k = q_ref[...], k_ref[...]
            v = v_ref[...].astype(BF16)
            dy = dy_ref[...]
            cqv, ckv, lsev, dlv = cq_ref[...], ck_ref[...], lse_ref[...], dl_ref[...]
            lane = lax.broadcasted_iota(jnp.int32, (tq, LANE), 1)
            rows = pl.ds(pl.multiple_of(i * tq, tq), tq)
            dcq_t = jnp.zeros((tq, LANE), F32)
            for hp in range(B_HEADS // 2):
                dq_new, dk_new, dv_new = [], [], []
                for u in range(2):
                    h = 2 * hp + u
                    sl = slice(64 * h, 64 * h + 64)
                    sc = _dot_nt(q[:, sl], k[:, sl]) + cqv[:, h:h + 1] - ckv[h:h + 1, :]
                    sc = jnp.where(mask, sc, NEG)
                    p = jnp.exp(sc - lsev[:, h:h + 1])
                    dp = _dot_nt(dy[:, sl], v[:, sl])
                    ds = p * (dp - dlv[:, h:h + 1])
                    dsb = ds.astype(BF16)
                    dv_new.append(_dot_tn(p.astype(BF16), dy[:, sl]))
                    dk_new.append(_dot_tn(dsb, q[:, sl]))
                    dq_new.append(_dot(dsb, k[:, sl]))
                    dcq_t = jnp.where(lane == h, jnp.sum(ds, axis=-1, keepdims=True), dcq_t)
                    dck_acc[h:h + 1, :] -= jnp.sum(ds, axis=0, keepdims=True)
                cols = slice(128 * hp, 128 * hp + 128)
                dq_ref[rows, cols] += jnp.concatenate(dq_new, axis=1)
                dk_acc[:, cols] += jnp.concatenate(dk_new, axis=1)
                dv_acc[:, cols] += jnp.concatenate(dv_new, axis=1)
            dcq_ref[rows, :] += dcq_t

        @pl.when(i == nq - 1)
        def _():
            dk_ref[...] = dk_acc[...]
            dv_ref[...] = dv_acc[...].astype(BF16)
            dck_ref[...] = jnp.concatenate([dck_acc[...], jnp.zeros((LANE - 16, tk), F32)], axis=0)

    qmap = lambda j, i: (jnp.maximum(i, first_q(j)), 0)
    return pl.pallas_call(
        body, grid=(nk, nq),
        in_specs=[pl.BlockSpec((tq, 768), qmap),
                  pl.BlockSpec((tk, 768), lambda j, i: (j, 0)),
                  pl.BlockSpec((pl.Element(tk), pl.Element(768)), lambda j, i: (j * tk, VB)),
                  pl.BlockSpec((tq, LANE), qmap),
                  pl.BlockSpec((16, tk), lambda j, i: (0, j)),
                  pl.BlockSpec((tq, LANE), qmap),
                  pl.BlockSpec((tq, LANE), qmap),
                  pl.BlockSpec((tq, 768), qmap)],
        out_specs=[_const((s, 768)), _const((s, LANE)),
                   pl.BlockSpec((tk, 768), lambda j, i: (j, 0)), pl.BlockSpec((tk, 768), lambda j, i: (j, 0)),
                   pl.BlockSpec((LANE, tk), lambda j, i: (0, j))],
        out_shape=[SDS((s, 768), F32), SDS((s, LANE), F32), SDS((s, 768), F32), SDS((s, 768), BF16),
                   SDS((LANE, s), F32)],
        scratch_shapes=[pltpu.VMEM((tk, 768), F32), pltpu.VMEM((tk, 768), F32), pltpu.VMEM((16, tk), F32)],
        name="fox_bwd", compiler_params=_params(("arbitrary", "arbitrary")))(qn, kn, proj, cq, ck, lse, delta, dyb)


def _fox_bwd_post(proj, fbl, qg, kg, bfor, bd, dqn, dkn, dcq, dck, *, tr):
    s = proj.shape[0]
    nb = s // tr
    triu = jnp.asarray(np.triu(np.ones((tr, tr), np.float32)))
    rev = lambda i: nb - 1 - i

    def body(q_ref, k_ref, fb_ref, qg_ref, kg_ref, bf_ref, bd_ref, tri_ref, dqn_ref, dkn_ref, dcq_ref, dck_ref,
             dq_ref, dk_ref, dfb_ref, dqg_ref, dkg_ref, dbf_ref, carry):
        @pl.when(pl.program_id(0) == 0)
        def _():
            carry[...] = jnp.zeros_like(carry)
            dqg_ref[...] = jnp.zeros_like(dqg_ref)
            dkg_ref[...] = jnp.zeros_like(dkg_ref)
            dbf_ref[...] = jnp.zeros_like(dbf_ref)

        bd_v = bd_ref[...]
        _, vjp_q = jax.vjp(lambda q, g: _qn_fn(q, g, bd_v), q_ref[...], qg_ref[...])
        dq, dqg = vjp_q(dqn_ref[...])
        _, vjp_k = jax.vjp(lambda k, g: _kn_fn(k, g, bd_v), k_ref[...], kg_ref[...])
        dk, dkg = vjp_k(dkn_ref[...])
        dq_ref[...] = dq.astype(BF16)
        dk_ref[...] = dk.astype(BF16)
        dqg_ref[...] += dqg
        dkg_ref[...] += dkg

        dc = dcq_ref[...] + jnp.transpose(dck_ref[...])
        dlogf = jnp.dot(tri_ref[...], dc, precision=HI, preferred_element_type=F32) + carry[...]
        carry[...] = dlogf[0:1, :]
        lane = lax.broadcasted_iota(jnp.int32, (tr, LANE), 1)
        xf = fb_ref[...] + bf_ref[...]
        dfb = jnp.where(lane < N_FORGET, dlogf * jax.nn.sigmoid(-xf), 0.0)
        dfb_ref[...] = dfb.astype(BF16)
        dbf_ref[...] += jnp.sum(dfb, axis=0, keepdims=True)

    rb = lambda w: pl.BlockSpec((tr, w), lambda i: (rev(i), 0))
    wn = lambda w, off: pl.BlockSpec((pl.Element(tr), pl.Element(w)), lambda i: (rev(i) * tr, off))
    return pl.pallas_call(
        body, grid=(nb,),
        in_specs=[wn(768, QB), wn(768, KB), rb(LANE), _const((1, 768)), _const((1, 768)), _const((1, LANE)),
                  _const((768, 768)), _const((tr, tr)), rb(768), rb(768), rb(LANE),
                  pl.BlockSpec((LANE, tr), lambda i: (0, rev(i)))],
        out_specs=[rb(768), rb(768), rb(LANE), _const((1, 768)), _const((1, 768)), _const((1, LANE))],
        out_shape=[SDS((s, 768), BF16), SDS((s, 768), BF16), SDS((s, LANE), BF16), SDS((1, 768), F32),
                   SDS((1, 768), F32), SDS((1, LANE), F32)],
        scratch_shapes=[pltpu.VMEM((1, LANE), F32)], name="fox_bwd_post",
        compiler_params=_params(("arbitrary",)))(proj, proj, fbl, qg, kg, bfor, bd, triu, dqn, dkn, dcq, dck)


AUG = 128 * B_HEADS
COL_A, COL_B = 64, 67


def _split3(c):
    hi = c.astype(BF16)
    r1 = c - hi.astype(F32)
    mid = r1.astype(BF16)
    lo = (r1 - mid.astype(F32)).astype(BF16)
    return hi, mid, lo


def _expand_mats():
    def mat(col0):
        e = np.zeros((768 + 3 * LANE, AUG), np.float32)
        for h in range(B_HEADS):
            for d in range(HEAD_DIM):
                e[64 * h + d, 128 * h + d] = 1.0
            for part in range(3):
                e[768 + LANE * part + h, 128 * h + col0 + part] = 1.0
        return e

    def ones(col0):
        o = np.zeros((1, AUG), np.float32)
        for h in range(B_HEADS):
            o[0, 128 * h + col0:128 * h + col0 + 3] = 1.0
        return o

    return (jnp.asarray(mat(COL_A), BF16), jnp.asarray(mat(COL_B), BF16), jnp.asarray(ones(COL_A)), jnp.asarray(ones(COL_B)))


def _augment(data_bf16, triple, emat, ones_row):
    parts = [data_bf16] + (list(triple) if triple is not None else [jnp.zeros((data_bf16.shape[0], LANE), BF16)] * 3)
    wide = _dot(jnp.concatenate(parts, axis=1), emat)
    if ones_row is not None:
        wide = wide + ones_row
    return wide.astype(BF16)


def _compact(wide):
    return jnp.concatenate([wide[:, 128 * h:128 * h + 64] for h in range(B_HEADS)], axis=1)


def _lane_of_heads(wide, col):
    rows = wide.shape[0]
    lane = lax.broadcasted_iota(jnp.int32, (rows, LANE), 1)
    out = jnp.zeros((rows, LANE), F32)
    for h in range(B_HEADS):
        out = jnp.where(lane == h, wide[:, 128 * h + col:128 * h + col + 1], out)
    return out


def _fox2_prep(proj, fbl, qg, kg, bfor, bd, ea, eb, ones_a, ones_b, *, tr):
    s = proj.shape[0]
    tri = jnp.asarray(np.tril(np.ones((tr, tr), np.float32)))

    def body(q_ref, k_ref, v_ref, fb_ref, qg_ref, kg_ref, bf_ref, bd_ref, tri_ref, ea_ref, eb_ref, oa_ref, ob_ref,
             qa_ref, ka_ref, va_ref, qn_ref, c_ref, carry):
        @pl.when(pl.program_id(0) == 0)
        def _():
            carry[...] = jnp.zeros_like(carry)

        bd_v = bd_ref[...]
        lane = lax.broadcasted_iota(jnp.int32, (tr, LANE), 1)
        logf = jnp.where(lane < N_FORGET, _log_sigmoid(fb_ref[...] + bf_ref[...]), 0.0)
        c = jnp.dot(tri_ref[...], logf, precision=HI, preferred_element_type=F32) + carry[...]
        c_ref[...] = c
        carry[...] = c[tr - 1:tr, :]
        qn = _qn_fn(q_ref[...], qg_ref[...], bd_v).astype(BF16)
        kn = _kn_fn(k_ref[...], kg_ref[...], bd_v).astype(BF16)
        qn_ref[...] = qn
        qa_ref[...] = _augment(qn, _split3(c), ea_ref[...], ob_ref[...])
        ka_ref[...] = _augment(kn, _split3(-c), eb_ref[...], oa_ref[...])
        va_ref[...] = _augment(v_ref[...].astype(BF16), None, ea_ref[...], oa_ref[...])

    emat = _const((768 + 3 * LANE, AUG))
    return pl.pallas_call(
        body, grid=(s // tr,),
        in_specs=[_win(tr, 768, QB), _win(tr, 768, KB), _win(tr, 768, VB), _rowblk(tr, LANE), _const((1, 768)),
                  _const((1, 768)), _const((1, LANE)), _const((768, 768)), _const((tr, tr)), emat, emat,
                  _const((1, AUG)), _const((1, AUG))],
        out_specs=[_rowblk(tr, AUG), _rowblk(tr, AUG), _rowblk(tr, AUG), _rowblk(tr, 768), _rowblk(tr, LANE)],
        out_shape=[SDS((s, AUG), BF16)] * 3 + [SDS((s, 768), BF16), SDS((s, LANE), F32)],
        scratch_shapes=[pltpu.VMEM((1, LANE), F32)], name="fox_prep",
        compiler_params=_params(("arbitrary",)))(proj, proj, proj, fbl, qg, kg, bfor, bd, tri, ea, eb, ones_a, ones_b)


def _fox2_fwd(proj, qa, ka, va):
    s = proj.shape[0]
    tq, tk = _fox_tiles(s)
    nq, nk = s // tq, s // tk

    def last_k(i):
        return (i * tq + tq - 1) // tk

    def body(q_ref, k_ref, v_ref, z_ref, gb_ref, yb_ref, lse_ref, acc, m_s):
        i, j = pl.program_id(0), pl.program_id(1)

        @pl.when(j == 0)
        def _():
            acc[...] = jnp.zeros_like(acc)
            m_s[...] = jnp.full_like(m_s, NEG)

        def tile(masked):
            if masked:
                qpos = i * tq + lax.broadcasted_iota(jnp.int32, (tq, tk), 0)
                kpos = j * tk + lax.broadcasted_iota(jnp.int32, (tq, tk), 1)
                mask = kpos <= qpos
            m_all = m_s[...]
            lane = lax.broadcasted_iota(jnp.int32, (tq, LANE), 1)
            m_out = m_all
            for h in range(B_HEADS):
                sl = slice(128 * h, 128 * h + 128)
                sc = _dot_nt(q_ref[:, sl], k_ref[:, sl])
                if masked:
                    sc = jnp.where(mask, sc, NEG)
                m_prev = m_all[:, h:h + 1]
                m_new = jnp.maximum(m_prev, jnp.max(sc, axis=-1, keepdims=True))
                p = jnp.exp(sc - m_new).astype(BF16)
                acc[:, sl] = jnp.exp(m_prev - m_new) * acc[:, sl] + _dot(p, v_ref[:, sl])
                m_out = jnp.where(lane == h, m_new, m_out)
            m_s[...] = m_out

        full = j * tk + tk - 1 <= i * tq

        @pl.when(full)
        def _():
            tile(False)

        @pl.when(jnp.logical_and(jnp.logical_not(full), j <= last_k(i)))
        def _():
            tile(True)

        @pl.when(j == nk - 1)
        def _():
            a = acc[...]
            l_all = _lane_of_heads(a, COL_A)
            y = jnp.concatenate([a[:, 128 * h:128 * h + 64] / a[:, 128 * h + COL_A:128 * h + COL_A + 1]
                                 for h in range(B_HEADS)], axis=1)
            yb_ref[...] = y
            gb_ref[...] = (y * _silu(z_ref[...])).astype(BF16)
            lane = lax.broadcasted_iota(jnp.int32, (tq, LANE), 1)
            lse_ref[...] = jnp.where(lane < B_HEADS, m_s[...] + jnp.log(jnp.where(lane < B_HEADS, l_all, 1.0)), 0.0)

    kmap = lambda i, j: (jnp.minimum(j, last_k(i)), 0)
    return pl.pallas_call(
        body, grid=(nq, nk),
        in_specs=[pl.BlockSpec((tq, AUG), lambda i, j: (i, 0)), pl.BlockSpec((tk, AUG), kmap), pl.BlockSpec((tk, AUG), kmap),
                  pl.BlockSpec((pl.Element(tq), pl.Element(768)), lambda i, j: (i * tq, ZB))],
        out_specs=[pl.BlockSpec((tq, 768), lambda i, j: (i, 0)), pl.BlockSpec((tq, 768), lambda i, j: (i, 0)),
                   pl.BlockSpec((tq, LANE), lambda i, j: (i, 0))],
        out_shape=[SDS((s, 768), BF16), SDS((s, 768), F32), SDS((s, LANE), F32)],
        scratch_shapes=[pltpu.VMEM((tq, AUG), F32), pltpu.VMEM((tq, LANE), F32)],
        name="fox_fwd", compiler_params=_params(("parallel", "arbitrary")))(qa, ka, va, proj)


def _fox2_bwd_pre(proj, yb, dgb, qn, c, lse, hsum, ea, ones_b, *, tr):
    s = proj.shape[0]

    def body(z_ref, y_ref, dg_ref, qn_ref, c_ref, lse_ref, hs_ref, ea_ref, ob_ref, qa_ref, dya_ref, dz_ref):
        z, y, dg = z_ref[...], y_ref[...], dg_ref[...]
        sg = jax.nn.sigmoid(z)
        dy = dg * (z * sg)
        dz_ref[...] = (dg * y * (sg * (1.0 + z * (1.0 - sg)))).astype(BF16)
        delta = jnp.dot(dy * y, hs_ref[...], precision=HI, preferred_element_type=F32)
        e = ea_ref[...]
        dya_ref[...] = _augment(dy.astype(BF16), _split3(-delta), e, None)
        qa_ref[...] = _augment(qn_ref[...], _split3(c_ref[...] - lse_ref[...]), e, ob_ref[...])

    return pl.pallas_call(
        body, grid=(s // tr,),
        in_specs=[_win(tr, 768, ZB), _rowblk(tr, 768), _rowblk(tr, 768), _rowblk(tr, 768), _rowblk(tr, LANE),
                  _rowblk(tr, LANE), _const((768, LANE)), _const((768 + 3 * LANE, AUG)), _const((1, AUG))],
        out_specs=[_rowblk(tr, AUG), _rowblk(tr, AUG), _rowblk(tr, 768)],
        out_shape=[SDS((s, AUG), BF16), SDS((s, AUG), BF16), SDS((s, 768), BF16)], name="fox_bwd_pre",
        compiler_params=_params(("parallel",)))(proj, yb, dgb, qn, c, lse, hsum, ea, ones_b)


def _fox2_bwd(qa, ka, va, dya):
    s = qa.shape[0]
    tq, tk = _fox_tiles(s)
    nq, nk = s // tq, s // tk

    def first_q(j):
        return (j * tk) // tq

    def body(q_ref, k_ref, v_ref, dy_ref, dq_hbm, dk_ref, dv_ref, dck_ref, dq_acc, dk_acc, dv_acc, sem):
        j, i = pl.program_id(0), pl.program_id(1)

        @pl.when((j == 0) & (i == 0))
        def _():
            dq_acc[...] = jnp.zeros_like(dq_acc)

        @pl.when(i == 0)
        def _():
            dk_acc[...] = jnp.zeros_like(dk_acc)
            dv_acc[...] = jnp.zeros_like(dv_acc)

        def tile(masked):
            if masked:
                qpos = i * tq + lax.broadcasted_iota(jnp.int32, (tq, tk), 0)
                kpos = j * tk + lax.broadcasted_iota(jnp.int32, (tq, tk), 1)
                mask = kpos <= qpos
            rows = pl.ds(pl.multiple_of(i * tq, tq), tq)
            for h in range(B_HEADS):
                sl = slice(128 * h, 128 * h + 128)
                q, k, dy = q_ref[:, sl], k_ref[:, sl], dy_ref[:, sl]
                sc = _dot_nt(q, k)
                if masked:
                    sc = jnp.where(mask, sc, NEG)
                p = jnp.exp(sc)
                ds = (p * _dot_nt(dy, v_ref[:, sl])).astype(BF16)
                dv_acc[:, sl] += _dot_tn(p.astype(BF16), dy)
                dk_acc[:, sl] += _dot_tn(ds, q)
                dq_acc[rows, sl] += _dot(ds, k)

        full = j * tk + tk - 1 <= i * tq

        @pl.when(full)
        def _():
            tile(False)

        @pl.when(jnp.logical_and(jnp.logical_not(full), i >= first_q(j)))
        def _():
            tile(True)

        @pl.when(i == nq - 1)
        def _():
            dkw = dk_acc[...]
            dk_ref[...] = _compact(dkw)
            dv_ref[...] = _compact(dv_acc[...]).astype(BF16)
            dck_ref[...] = -_lane_of_heads(dkw, COL_B)

        @pl.when((j == nk - 1) & (i == nq - 1))
        def _():
            cp = pltpu.make_async_copy(dq_acc, dq_hbm, sem)
            cp.start()
            cp.wait()

    qmap = lambda j, i: (jnp.maximum(i, first_q(j)), 0)
    kblk = lambda w: pl.BlockSpec((tk, w), lambda j, i: (j, 0))
    return pl.pallas_call(
        body, grid=(nk, nq),
        in_specs=[pl.BlockSpec((tq, AUG), qmap), kblk(AUG), kblk(AUG), pl.BlockSpec((tq, AUG), qmap)],
        out_specs=[pl.BlockSpec(memory_space=pl.ANY), kblk(768), kblk(768), kblk(LANE)],
        out_shape=[SDS((s, AUG), F32), SDS((s, 768), F32), SDS((s, 768), BF16), SDS((s, LANE), F32)],
        scratch_shapes=[pltpu.VMEM((s, AUG), F32), pltpu.VMEM((tk, AUG), F32), pltpu.VMEM((tk, AUG), F32),
                        pltpu.SemaphoreType.DMA],
        name="fox_bwd", compiler_params=_params(("arbitrary", "arbitrary")))(qa, ka, va, dya)


def _fox2_bwd_post(proj, fbl, qg, kg, bfor, bd, dqa, dkn, dck, *, tr):
    s = proj.shape[0]
    nb = s // tr
    triu = jnp.asarray(np.triu(np.ones((tr, tr), np.float32)))
    rev = lambda i: nb - 1 - i

    def body(q_ref, k_ref, fb_ref, qg_ref, kg_ref, bf_ref, bd_ref, tri_ref, dqa_ref, dkn_ref, dck_ref,
             dq_ref, dk_ref, dfb_ref, dqg_ref, dkg_ref, dbf_ref, carry):
        @pl.when(pl.program_id(0) == 0)
        def _():
            carry[...] = jnp.zeros_like(carry)
            dqg_ref[...] = jnp.zeros_like(dqg_ref)
            dkg_ref[...] = jnp.zeros_like(dkg_ref)
            dbf_ref[...] = jnp.zeros_like(dbf_ref)

        bd_v = bd_ref[...]
        dqw = dqa_ref[...]
        _, vjp_q = jax.vjp(lambda q, g: _qn_fn(q, g, bd_v), q_ref[...], qg_ref[...])
        dq, dqg = vjp_q(_compact(dqw))
        _, vjp_k = jax.vjp(lambda k, g: _kn_fn(k, g, bd_v), k_ref[...], kg_ref[...])
        dk, dkg = vjp_k(dkn_ref[...])
        dq_ref[...] = dq.astype(BF16)
        dk_ref[...] = dk.astype(BF16)
        dqg_ref[...] += dqg
        dkg_ref[...] += dkg

        dc = _lane_of_heads(dqw, COL_A) + dck_ref[...]
        dlogf = jnp.dot(tri_ref[...], dc, precision=HI, preferred_element_type=F32) + carry[...]
        carry[...] = dlogf[0:1, :]
        lane = lax.broadcasted_iota(jnp.int32, (tr, LANE), 1)
        xf = fb_ref[...] + bf_ref[...]
        dfb = jnp.where(lane < N_FORGET, dlogf * jax.nn.sigmoid(-xf), 0.0)
        dfb_ref[...] = dfb.astype(BF16)
        dbf_ref[...] += jnp.sum(dfb, axis=0, keepdims=True)

    rb = lambda w: pl.BlockSpec((tr, w), lambda i: (rev(i), 0))
    wn = lambda w, off: pl.BlockSpec((pl.Element(tr), pl.Element(w)), lambda i: (rev(i) * tr, off))
    return pl.pallas_call(
        body, grid=(nb,),
        in_specs=[wn(768, QB), wn(768, KB), rb(LANE), _const((1, 768)), _const((1, 768)), _const((1, LANE)),
                  _const((768, 768)), _const((tr, tr)), rb(AUG), rb(768), rb(LANE)],
        out_specs=[rb(768), rb(768), rb(LANE), _const((1, 768)), _const((1, 768)), _const((1, LANE))],
        out_shape=[SDS((s, 768), BF16), SDS((s, 768), BF16), SDS((s, LANE), BF16), SDS((1, 768), F32),
                   SDS((1, 768), F32), SDS((1, LANE), F32)],
        scratch_shapes=[pltpu.VMEM((1, LANE), F32)], name="fox_bwd_post",
        compiler_params=_params(("arbitrary",)))(proj, proj, fbl, qg, kg, bfor, bd, triu, dqa, dkn, dck)


def _merge_specs(tr):
    row = lambda w: pl.BlockSpec((tr, w), lambda i, j: (i, 0))
    shard = lambda r: pl.BlockSpec((None, r, 512), lambda i, j: (j, 0, 0))
    gate = lambda b: pl.BlockSpec((tr, 512), lambda i, j: (i, (GATE + 2048 * b) // 512 + j))
    return [row(768), row(768), row(512), shard(768), shard(768), shard(512), gate(0), gate(1), gate(2)]


def _merge_fwd(proj, ga, gb, gc, wa, wb, wc, *, tr):
    s = proj.shape[0]

    def body(ga_ref, gb_ref, gc_ref, wa_ref, wb_ref, wc_ref, l0_ref, l1_ref, l2_ref, y_ref):
        ua = _dot(ga_ref[...], wa_ref[...])
        ub = _dot(gb_ref[...], wb_ref[...])
        uc = _dot(gc_ref[...], wc_ref[...])
        y = jax.nn.sigmoid(l0_ref[...]) * ua + jax.nn.sigmoid(l1_ref[...]) * ub + jax.nn.sigmoid(l2_ref[...]) * uc
        y_ref[...] = y.astype(BF16)

    return pl.pallas_call(
        body, grid=(s // tr, N_CHIPS), in_specs=_merge_specs(tr),
        out_specs=pl.BlockSpec((tr, 512), lambda i, j: (i, j)), out_shape=SDS((s, D_MODEL), BF16), name="merge_fwd",
        compiler_params=_params(("parallel", "arbitrary")))(ga, gb, gc, wa, wb, wc, proj, proj, proj)


def _merge_bwd(proj, ga, gb, gc, wa, wb, wc, dy, *, tr):
    s = proj.shape[0]

    def body(ga_ref, gb_ref, gc_ref, wa_ref, wb_ref, wc_ref, l0_ref, l1_ref, l2_ref, dy_ref,
             dl0_ref, dl1_ref, dl2_ref, dua_ref, dub_ref, duc_ref, dga_ref, dgb_ref, dgc_ref):
        j = pl.program_id(1)
        dyv = dy_ref[...]

        @pl.when(j == 0)
        def _():
            dga_ref[...] = jnp.zeros_like(dga_ref)
            dgb_ref[...] = jnp.zeros_like(dgb_ref)
            dgc_ref[...] = jnp.zeros_like(dgc_ref)

        for g_ref, w_ref, l_ref, dl_ref, du_ref, dg_ref in (
                (ga_ref, wa_ref, l0_ref, dl0_ref, dua_ref, dga_ref),
                (gb_ref, wb_ref, l1_ref, dl1_ref, dub_ref, dgb_ref),
                (gc_ref, wc_ref, l2_ref, dl2_ref, duc_ref, dgc_ref)):
            w = w_ref[...]
            u = _dot(g_ref[...], w)
            sg = jax.nn.sigmoid(l_ref[...])
            dl_ref[...] = (dyv * u * sg * (1.0 - sg)).astype(BF16)
            du = (dyv * sg).astype(BF16)
            du_ref[...] = du
            dg_ref[...] += _dot_nt(du, w)

    blk = pl.BlockSpec((tr, 512), lambda i, j: (i, j))
    row = lambda w: pl.BlockSpec((tr, w), lambda i, j: (i, 0))
    big = SDS((s, D_MODEL), BF16)
    return pl.pallas_call(
        body, grid=(s // tr, N_CHIPS), in_specs=_merge_specs(tr) + [blk],
        out_specs=[blk] * 6 + [row(768), row(768), row(512)],
        out_shape=[big] * 6 + [SDS((s, 768), F32), SDS((s, 768), F32), SDS((s, 512), F32)], name="merge_bwd",
        compiler_params=_params(("parallel", "arbitrary")))(ga, gb, gc, wa, wb, wc, proj, proj, proj, dy)


def _out_loss(y, wo, x, tgt, *, tr, tn):
    s = x.shape[0]

    def body(y_ref, w_ref, x_ref, t_ref, d_ref, db_ref, sq_ref):
        @pl.when((pl.program_id(0) == 0) & (pl.program_id(1) == 0))
        def _():
            sq_ref[...] = jnp.zeros_like(sq_ref)

        out = x_ref[...] + _dot(y_ref[...], w_ref[...])
        diff = out - t_ref[...]
        sq_ref[...] += jnp.sum(diff * diff, axis=0, keepdims=True)
        d = diff * (1.0 / D_MODEL)
        d_ref[...] = d
        db_ref[...] = d.astype(BF16)

    blk = pl.BlockSpec((tr, tn), lambda i, j: (i, j))
    return pl.pallas_call(
        body, grid=(s // tr, D_MODEL // tn),
        in_specs=[pl.BlockSpec((tr, D_MODEL), lambda i, j: (i, 0)), pl.BlockSpec((D_MODEL, tn), lambda i, j: (0, j)), blk, blk],
        out_specs=[blk, blk, _const((1, tn))],
        out_shape=[SDS((s, D_MODEL), F32), SDS((s, D_MODEL), BF16), SDS((1, tn), F32)], name="out_loss",
        compiler_params=_params(("arbitrary", "arbitrary")))(y, wo, x, tgt)


def _tile_gain(g, reps):
    return jnp.tile(g.reshape(1, -1), (1, reps))


def _pad_lane(v):
    v = v.reshape(1, -1)
    return jnp.pad(v, ((0, 0), (0, LANE - v.shape[1])))


def _local_step(x, mem, tgt, w_main, w_fb, w_mk, wa, wb, wc, wo, norm_gain, mem_norm_gain, b_forget,
                q_gain_a, k_gain_a, sinks_a, q_gain_b, k_gain_b, q_gain_c, k_gain_c):
    s = x.shape[0]
    tr = min(512, s)
    bd64 = _block_diag(768, HEAD_DIM)
    bd128 = _block_diag(512, C_HEAD_DIM)
    hsum = _head_sum(768, HEAD_DIM)
    qga, kga = _tile_gain(q_gain_a, 12), _tile_gain(k_gain_a, 4)
    qgb, kgb = _tile_gain(q_gain_b, 12), _tile_gain(k_gain_b, 12)
    qgc, kgc = _tile_gain(q_gain_c, 4), _tile_gain(k_gain_c, 4)
    sinks = _pad_lane(sinks_a)
    bfor = _pad_lane(b_forget)

    hn = _rms_fwd(x, norm_gain, tr=tr, name="rms_x")
    proj = _matmul(hn, w_main, dims="nn", out_dtype=F32, tm=1024, tn=512, tk=D_MODEL, name="proj_main")
    fbl = _matmul(hn, w_fb, dims="nn", out_dtype=F32, tm=1024, tn=LANE, tk=D_MODEL, name="proj_forget")
    memn = _rms_fwd(mem, mem_norm_gain, tr=mem.shape[0], name="rms_mem")
    mkv = _matmul(memn, w_mk, dims="nn", out_dtype=F32, tm=256, tn=512, tk=D_MODEL, name="mem_kv")

    swa_bias = _swa_bias()
    ga = _swa_fwd(proj, qga, kga, sinks, bd64, swa_bias)
    ea, eb, ones_a, ones_b = _expand_mats()
    tf = min(256, s)
    qa, ka, va, qn, cfox = _fox2_prep(proj, fbl, qgb, kgb, bfor, bd64, ea, eb, ones_a, ones_b, tr=tf)
    gb, yb, lse = _fox2_fwd(proj, qa, ka, va)
    gc = _mem_fwd(proj, mkv, qgc, kgc, bd128, tr=tr)
    y = _merge_fwd(proj, ga, gb, gc, wa, wb, wc, tr=tr)
    dout, dout_b, sq = _out_loss(y, wo, x, tgt, tr=tr, tn=512)

    d_wo = _matmul(y, dout_b, dims="tn", out_dtype=F32, tm=1024, tn=512, tk=2048, name="dw_out")
    dy = _matmul(dout_b, wo, dims="nt", out_dtype=F32, tm=1024, tn=512, tk=D_MODEL, name="dy")
    dl0, dl1, dl2, dua, dub, duc, dga, dgb, dgc = _merge_bwd(proj, ga, gb, gc, wa, wb, wc, dy, tr=tr)
    d_wa = _matmul(ga, dua, dims="tn", out_dtype=F32, tm=768, tn=512, tk=2048, name="dw_branch_a")
    d_wb = _matmul(gb, dub, dims="tn", out_dtype=F32, tm=768, tn=512, tk=2048, name="dw_branch_b")
    d_wc = _matmul(gc, duc, dims="tn", out_dtype=F32, tm=512, tn=512, tk=2048, name="dw_branch_c")

    dcur, dprev, d_qga, d_kga, d_sinks = _swa_bwd(proj, qga, kga, sinks, bd64, swa_bias, dga)
    dproj_a = _swa_combine(dcur, dprev)

    qab, dya, dzb = _fox2_bwd_pre(proj, yb, dgb, qn, cfox, lse, hsum, ea, ones_b, tr=tf)
    dqa, dkn, dvb, dck = _fox2_bwd(qab, ka, va, dya)
    dqb, dkb, dfb, d_qgb, d_kgb, d_bf = _fox2_bwd_post(proj, fbl, qgb, kgb, bfor, bd64, dqa, dkn, dck, tr=tf)

    dproj_c, dmkv, d_qgc, d_kgc = _mem_bwd(proj, mkv, qgc, kgc, bd128, dgc, tr=tr)
    dmkv_b = dmkv.astype(BF16)
    d_wmk = _matmul(memn, dmkv_b, dims="tn", out_dtype=F32, tm=1024, tn=512, tk=256, name="dw_mem_kv")
    dmemn = _matmul(dmkv_b, w_mk, dims="nt", out_dtype=F32, tm=256, tn=512, tk=1024, name="dmemn")
    (d_mem_gain,) = _rms_bwd(mem, mem_norm_gain, dmemn, None, tr=mem.shape[0], name="rms_mem_bwd")

    dproj = jnp.concatenate([dproj_a, dqb, dkb, dvb, dzb, dproj_c, dl0, dl1, dl2], axis=1)
    dhn_f = _matmul(dfb, w_fb, dims="nt", out_dtype=F32, tm=1024, tn=512, tk=LANE, name="dhn_forget")
    dhn = _matmul(dproj, w_main, dims="nt", out_dtype=F32, tm=1024, tn=512, tk=2048, name="dhn", add=dhn_f)
    d_wmain = _matmul(hn, dproj, dims="tn", out_dtype=F32, tm=1024, tn=512, tk=2048, name="dw_main")
    d_wfb = _matmul(hn, dfb, dims="tn", out_dtype=F32, tm=1024, tn=LANE, tk=512, name="dw_forget")
    grad_x, d_gain = _rms_bwd(x, norm_gain, dhn, dout, tr=tr, name="rms_x_bwd")

    fold = lambda g, reps: jnp.sum(g.reshape(reps, -1), axis=0, keepdims=True)
    return dict(
        sq=sq, grad_x=grad_x, d_wmain=d_wmain, d_wfb=d_wfb, d_wmk=d_wmk, d_wa=d_wa, d_wb=d_wb, d_wc=d_wc, d_wo=d_wo,
        d_gain=d_gain, d_mem_gain=d_mem_gain, d_bf=d_bf[:, :N_FORGET],
        d_qga=fold(d_qga, 12), d_kga=fold(d_kga, 4), d_sinks=d_sinks[:, :A_HEADS],
        d_qgb=fold(d_qgb, 12), d_kgb=fold(d_kgb, 12), d_qgc=fold(d_qgc, 4), d_kgc=fold(d_kgc, 4))


PACK_ROWS = 256
FORGET_IN_SHARD = FORGET_COL - SHARD_COLS
AFTER_FORGET = FORGET_COL - SLAB_START[1]
END_CHIP1 = 2 * SHARD_COLS - N_FORGET - SLAB_START[1]


def _pack_w_in(chip, w):
    rows = w.shape[0]
    tr = PACK_ROWS

    def body(k_ref, w_ref, o_ref, scr):
        scr[...] = jnp.zeros_like(scr)
        scr[:, pl.ds(0, SHARD_COLS)] = w_ref[...]
        v = scr[...]
        k = k_ref[0]
        col = lax.broadcasted_iota(jnp.int32, (tr, SLAB), 1)
        no_forget = jnp.zeros((tr, LANE), BF16)

        @pl.when(k == 0)
        def _():
            o_ref[:, 0:SLAB] = v.astype(BF16)
            o_ref[:, SLAB:] = no_forget

        @pl.when(k == 1)
        def _():
            before = pltpu.roll(v, SLAB_SHIFT[1], axis=1)
            after = pltpu.roll(v, SLAB - (N_FORGET - SLAB_SHIFT[1]), axis=1)
            slab = jnp.where(col < AFTER_FORGET, before, jnp.where(col < END_CHIP1, after, 0.0))
            o_ref[:, 0:SLAB] = slab.astype(BF16)
            f = pltpu.roll(v, SLAB - FORGET_IN_SHARD, axis=1)[:, :LANE]
            o_ref[:, SLAB:] = jnp.where(col[:, :LANE] < N_FORGET, f, 0.0).astype(BF16)

        for kk in (2, 3):
            @pl.when(k == kk)
            def _(kk=kk):
                o_ref[:, 0:SLAB] = pltpu.roll(v, SLAB_SHIFT[kk], axis=1).astype(BF16)
                o_ref[:, SLAB:] = no_forget

    return pl.pallas_call(
        body, grid_spec=pltpu.PrefetchScalarGridSpec(
            num_scalar_prefetch=1, grid=(rows // tr,),
            in_specs=[pl.BlockSpec((tr, SHARD_COLS), lambda i, k: (i, 0))],
            out_specs=pl.BlockSpec((None, tr, SLAB + LANE), lambda i, k: (k[0], i, 0)),
            scratch_shapes=[pltpu.VMEM((tr, SLAB), F32)]),
        out_shape=SDS((N_CHIPS, rows, SLAB + LANE), BF16), name="pack_w_in",
        compiler_params=_params(("arbitrary",)))(chip, w)


def _merge_slabs(g):
    rows = g.shape[1]
    tr = PACK_ROWS
    t = [s // LANE for s in SLAB_START]
    n_t = SLAB // LANE

    def body(g_ref, m_ref, f_ref):
        for k in range(N_CHIPS):
            lo = t[k] + (1 if k > 0 else 0)
            hi = t[k + 1] if k + 1 < N_CHIPS else t[k] + n_t
            m_ref[:, lo * LANE:hi * LANE] = g_ref[k, :, (lo - t[k]) * LANE:(hi - t[k]) * LANE]
            if k + 1 < N_CHIPS:
                a = g_ref[k, :, (hi - t[k]) * LANE:(hi - t[k] + 1) * LANE].astype(F32)
                b = g_ref[k + 1, :, 0:LANE].astype(F32)
                m_ref[:, hi * LANE:(hi + 1) * LANE] = (a + b).astype(BF16)
        f_ref[...] = g_ref[1, :, SLAB:]

    return pl.pallas_call(
        body, grid=(rows // tr,),
        in_specs=[pl.BlockSpec((N_CHIPS, tr, SLAB + LANE), lambda i: (0, i, 0))],
        out_specs=[_rowblk(tr, P_MAIN), _rowblk(tr, LANE)],
        out_shape=[SDS((rows, P_MAIN), BF16), SDS((rows, LANE), BF16)], name="merge_slabs",
        compiler_params=_params(("parallel",)))(g)


def _adamw_math(w, g, m, v):
    nm = ADAM_B1 * m + (1.0 - ADAM_B1) * g
    nv = ADAM_B2 * v + (1.0 - ADAM_B2) * (g * g)
    m_hat = nm / (1.0 - ADAM_B1 ** ADAM_STEP)
    v_hat = nv / (1.0 - ADAM_B2 ** ADAM_STEP)
    delta = -ADAM_LR * (m_hat / (jnp.sqrt(v_hat) + ADAM_EPS) + ADAM_WD * w)
    return delta, nm, nv


def _adamw(g, w, m, v, *, tr, name):
    rows, cols = w.shape
    tr = min(tr, rows)

    def body(g_ref, w_ref, m_ref, v_ref, d_ref, nm_ref, nv_ref):
        d, nm, nv = _adamw_math(w_ref[...], g_ref[...], m_ref[...], v_ref[...])
        d_ref[...] = d
        nm_ref[...] = nm
        nv_ref[...] = nv

    spec = _rowblk(tr, cols)
    return pl.pallas_call(
        body, grid=(rows // tr,), in_specs=[spec] * 4, out_specs=[spec] * 3,
        out_shape=[SDS((rows, cols), F32)] * 3, name=name, compiler_params=_params(("parallel",)))(g, w, m, v)


def _adamw_w_in(chip_core, slab_mine, slab_theirs, forget_mine, forget_theirs, w, m, v):
    rows = w.shape[0]
    tr = PACK_ROWS // 2
    nbh = rows // 2 // tr

    def body(k_ref, sa_ref, sb_ref, fa_ref, fb_ref, w_ref, m_ref, v_ref, g_ref, d_ref, nm_ref, nv_ref):
        use_mine = pl.program_id(0) // nbh == k_ref[1]
        sl = jnp.where(use_mine, sa_ref[...], sb_ref[...])
        f_tile = jnp.where(use_mine, fa_ref[...], fb_ref[...])
        k = k_ref[0]

        def emit(wide):
            g = wide[:, :SHARD_COLS]
            g_ref[...] = g
            d, nm, nv = _adamw_math(w_ref[...], g, m_ref[...], v_ref[...])
            d_ref[...] = d
            nm_ref[...] = nm
            nv_ref[...] = nv

        @pl.when(k == 0)
        def _():
            emit(sl)

        @pl.when(k == 1)
        def _():
            col = lax.broadcasted_iota(jnp.int32, (tr, SLAB), 1)
            before = pltpu.roll(sl, SLAB - SLAB_SHIFT[1], axis=1)
            after = pltpu.roll(sl, N_FORGET - SLAB_SHIFT[1], axis=1)
            wide_f = jnp.concatenate([f_tile, jnp.zeros((tr, SLAB - LANE), F32)], axis=1)
            forget = pltpu.roll(wide_f, FORGET_IN_SHARD, axis=1)
            emit(jnp.where(col < FORGET_IN_SHARD, before, jnp.where(col < FORGET_IN_SHARD + N_FORGET, forget, after)))

        for kk in (2, 3):
            @pl.when(k == kk)
            def _(kk=kk):
                emit(pltpu.roll(sl, SLAB - SLAB_SHIFT[kk], axis=1))

    nat = pl.BlockSpec((tr, SHARD_COLS), lambda i, k: (i, 0))
    half = lambda width: pl.BlockSpec((tr, width), lambda i, k: (i % nbh, 0))
    return pl.pallas_call(
        body, grid_spec=pltpu.PrefetchScalarGridSpec(
            num_scalar_prefetch=1, grid=(rows // tr,),
            in_specs=[half(SLAB), half(SLAB), half(LANE), half(LANE), nat, nat, nat],
            out_specs=[nat] * 4),
        out_shape=[SDS((rows, SHARD_COLS), F32)] * 4, name="adamw_w_in",
        compiler_params=_params(("arbitrary",)))(chip_core, slab_mine, slab_theirs, forget_mine, forget_theirs, w, m, v)


ANY = pl.BlockSpec(memory_space=pl.ANY)
HALF_AXIS = (0, 0, 1, 0, 0, 0, 1)


def _me():
    return lax.axis_index("x"), lax.axis_index("y"), lax.axis_index("c")


def _half(ref, which, axis):
    n = ref.shape[axis] // 2
    sl = pl.ds(which * n, n)
    return ref.at[sl] if axis == 0 else ref.at[:, sl]


def _piece(t, ref, j):
    if t == 0:
        return ref.at[:, pl.ds(SLAB_START[j], SLAB)]
    if t == 1:
        return ref
    if t in (2, 6):
        return ref.at[pl.ds(512 * j, 512)]
    return ref.at[:, pl.ds(512 * j, 512)]


def _piece_shape(t, shape):
    if t == 0:
        return (shape[0], SLAB)
    if t == 1:
        return shape
    if t in (2, 6):
        return (512, shape[1])
    return (shape[0], 512)


def _all_gather(slabs, parts):
    parts = [slabs] + list(parts)
    n = len(parts)

    def body(*refs):
        ins, outs = refs[:n], refs[n:2 * n]
        send, recv, fsend, frecv, lsem = refs[2 * n:]
        x, y, c = _me()
        k = 2 * x + y
        sib = (x, y, 1 - c)
        chips = [(1 - x, y), (x, 1 - y), (1 - x, 1 - y)]

        def rows(t, which):
            h = outs[t].shape[1] // 2
            return pl.ds(which * h, h)

        def mine(t):
            return ins[t].at[k, rows(t, c)] if t == 0 else ins[t].at[rows(t, c)]

        local = [pltpu.make_async_copy(ins[t], outs[t].at[k], lsem.at[t]) for t in range(1, n)]
        for cp in local:
            cp.start()

        def first(t, j, chip):
            return pltpu.make_async_remote_copy(
                src_ref=mine(t), dst_ref=outs[t].at[k, rows(t, c)],
                send_sem=send.at[t, j], recv_sem=recv.at[t, j], device_id=(chip[0], chip[1], c), device_id_type=MESH)

        def passed(t, j, chip, which):
            kj = 2 * chip[0] + chip[1]
            blk = outs[t].at[kj, rows(t, which)]
            return pltpu.make_async_remote_copy(
                src_ref=blk, dst_ref=blk, send_sem=fsend.at[t, j], recv_sem=frecv.at[t, j],
                device_id=sib, device_id_type=MESH)

        firsts = [first(t, j, chip) for j, chip in enumerate(chips) for t in range(n)]
        for cp in firsts:
            cp.start()
        passes = []
        for j, chip in enumerate(chips):
            for t in range(n):
                kj = 2 * chip[0] + chip[1]
                pltpu.make_async_remote_copy(
                    src_ref=mine(t), dst_ref=outs[t].at[kj, rows(t, c)],
                    send_sem=send.at[t, j], recv_sem=recv.at[t, j], device_id=(chip[0], chip[1], c),
                    device_id_type=MESH).wait_recv()
                cp = passed(t, j, chip, c)
                cp.start()
                passes.append(cp)
        for j, chip in enumerate(chips):
            for t in range(n):
                passed(t, j, chip, 1 - c).wait_recv()
        for cp in firsts + passes:
            cp.wait_send()
        for cp in local:
            cp.wait()

    return pl.pallas_call(
        body, in_specs=[ANY] * n, out_specs=[ANY] * n,
        out_shape=[SDS(slabs.shape, slabs.dtype)] + [SDS((N_CHIPS,) + p.shape, p.dtype) for p in parts[1:]],
        scratch_shapes=[pltpu.SemaphoreType.DMA((n, 3))] * 4 + [pltpu.SemaphoreType.DMA((n,))],
        input_output_aliases={0: 0}, name="all_gather_weights")(*parts)


def _exchange_halves(arrs):
    n = len(arrs)

    def body(*refs):
        ins, outs = refs[:n], refs[n:2 * n]
        send, recv = refs[2 * n:]
        x, y, c = _me()
        cps = [pltpu.make_async_remote_copy(
            src_ref=_half(ins[t], 1 - c, HALF_AXIS[t]), dst_ref=outs[t], send_sem=send.at[t], recv_sem=recv.at[t],
            device_id=(x, y, 1 - c), device_id_type=MESH) for t in range(n)]
        for cp in cps:
            cp.start()
        for cp in cps:
            cp.wait()

    def hshape(t):
        s = list(arrs[t].shape)
        s[HALF_AXIS[t]] //= 2
        return tuple(s)

    return pl.pallas_call(
        body, in_specs=[ANY] * n, out_specs=[ANY] * n,
        out_shape=[SDS(hshape(t), arrs[t].dtype) for t in range(n)],
        scratch_shapes=[pltpu.SemaphoreType.DMA((n,))] * 2, name="exchange_halves")(*arrs)


def _add_half(full, got, core, axis, *, name):
    r, c = got.shape
    br, bc = (256 if r % 256 == 0 else 128), min(2048, c)
    off_r = (r // br) if axis == 0 else 0
    off_c = (c // bc) if axis == 1 else 0

    def body(c_ref, a_ref, b_ref, o_ref):
        o_ref[...] = (a_ref[...] + b_ref[...]).astype(BF16)

    return pl.pallas_call(
        body, grid_spec=pltpu.PrefetchScalarGridSpec(
            num_scalar_prefetch=1, grid=(r // br, c // bc),
            in_specs=[pl.BlockSpec((br, bc), lambda i, j, cr: (i + cr[0] * off_r, j + cr[0] * off_c)),
                      pl.BlockSpec((br, bc), lambda i, j, cr: (i, j))],
            out_specs=pl.BlockSpec((br, bc), lambda i, j, cr: (i, j))),
        out_shape=SDS((r, c), BF16), name=name, compiler_params=_params(("parallel", "parallel")))(core, full, got)


def _scatter_pieces(halves):
    n = len(halves)

    def body(*refs):
        ins, outs = refs[:n], refs[n:2 * n]
        send, recv, lsem = refs[2 * n:]
        x, y, c = _me()
        k = 2 * x + y

        def to_chip(t, j):
            return pltpu.make_async_remote_copy(
                src_ref=_piece(t, ins[t], j), dst_ref=outs[t].at[k], send_sem=send.at[t, j], recv_sem=recv.at[t, k],
                device_id=(j // 2, j % 2, c), device_id_type=MESH)

        def from_chip(t, j):
            return pltpu.make_async_remote_copy(
                src_ref=_piece(t, ins[t], j), dst_ref=outs[t].at[j], send_sem=send.at[t, j], recv_sem=recv.at[t, j],
                device_id=(j // 2, j % 2, c), device_id_type=MESH)

        def own(t, j):
            return pltpu.make_async_copy(_piece(t, ins[t], j), outs[t].at[j], lsem.at[t])

        for j in range(N_CHIPS):
            @pl.when(k != j)
            def _(j=j):
                for t in range(n):
                    to_chip(t, j).start()

            @pl.when(k == j)
            def _(j=j):
                for t in range(n):
                    own(t, j).start()

        for j in range(N_CHIPS):
            @pl.when(k != j)
            def _(j=j):
                for t in range(n):
                    from_chip(t, j).wait_recv()
                for t in range(n):
                    to_chip(t, j).wait_send()

            @pl.when(k == j)
            def _(j=j):
                for t in range(n):
                    own(t, j).wait()

    return pl.pallas_call(
        body, in_specs=[ANY] * n, out_specs=[ANY] * n,
        out_shape=[SDS((N_CHIPS,) + _piece_shape(t, halves[t].shape), halves[t].dtype) for t in range(n)],
        scratch_shapes=[pltpu.SemaphoreType.DMA((n, N_CHIPS))] * 2 + [pltpu.SemaphoreType.DMA((n,))],
        name="scatter_pieces")(*halves)


def _sum4(p, *, name):
    _, r, c = p.shape
    br = 256 if r % 256 == 0 else 128

    def body(p_ref, o_ref):
        o_ref[...] = ((p_ref[0].astype(F32) + p_ref[1].astype(F32)) + p_ref[2].astype(F32)) + p_ref[3].astype(F32)

    return pl.pallas_call(
        body, grid=(r // br,), in_specs=[pl.BlockSpec((N_CHIPS, br, c), lambda i: (0, i, 0))],
        out_specs=_rowblk(br, c), out_shape=SDS((r, c), F32), name=name, compiler_params=_params(("parallel",)))(p)


def _swap_halves(sums):
    n = len(sums)

    def body(*refs):
        ins, outs = refs[:n], refs[n:2 * n]
        send, recv = refs[2 * n:]
        x, y, c = _me()
        cps = [pltpu.make_async_remote_copy(
            src_ref=ins[t], dst_ref=outs[t], send_sem=send.at[t], recv_sem=recv.at[t],
            device_id=(x, y, 1 - c), device_id_type=MESH) for t in range(n)]
        for cp in cps:
            cp.start()
        for cp in cps:
            cp.wait()

    return pl.pallas_call(
        body, in_specs=[ANY] * n, out_specs=[ANY] * n,
        out_shape=[SDS(s.shape, F32) for s in sums],
        scratch_shapes=[pltpu.SemaphoreType.DMA((n,))] * 2, name="swap_halves")(*sums)


def _adamw_halves(mine, theirs, core, w, m, v, *, axis, tr, name):
    rows, cols = w.shape

    if axis == 0:
        nbh = rows // 2 // tr
        g_spec = pl.BlockSpec((tr, cols), lambda i, cr: (i % nbh, 0))
    else:
        g_spec = pl.BlockSpec((tr, cols // 2), lambda i, cr: (i, 0))

    def body(c_ref, a_ref, b_ref, w_ref, m_ref, v_ref, g_ref, d_ref, nm_ref, nv_ref):
        a, b = a_ref[...], b_ref[...]
        if axis == 0:
            g = jnp.where(pl.program_id(0) // nbh == c_ref[0], a, b)
        else:
            low = c_ref[0] == 0
            g = jnp.concatenate([jnp.where(low, a, b), jnp.where(low, b, a)], axis=1)
        g_ref[...] = g
        d, nm, nv = _adamw_math(w_ref[...], g, m_ref[...], v_ref[...])
        d_ref[...] = d
        nm_ref[...] = nm
        nv_ref[...] = nv

    nat = pl.BlockSpec((tr, cols), lambda i, cr: (i, 0))
    return pl.pallas_call(
        body, grid_spec=pltpu.PrefetchScalarGridSpec(
            num_scalar_prefetch=1, grid=(rows // tr,), in_specs=[g_spec, g_spec, nat, nat, nat], out_specs=[nat] * 4),
        out_shape=[SDS((rows, cols), F32)] * 4, name=name, compiler_params=_params(("arbitrary",)))(
            core, mine, theirs, w, m, v)


SMALL_ROWS, SMALL_COLS = 8, 1024


def _pack_small(vs):
    flat = jnp.concatenate([v.reshape(-1) for v in vs])
    return jnp.pad(flat, (0, SMALL_ROWS * SMALL_COLS - flat.shape[0])).reshape(SMALL_ROWS, SMALL_COLS)


def _unpack_small(packed, sizes):
    flat = packed.reshape(-1)
    out, o = [], 0
    for n in sizes:
        out.append(flat[o:o + n].reshape(1, n))
        o += n
    return out


def _all_reduce_small(v):
    n_dev = 8

    def body(v_ref, o_ref, land, send, recv):
        x, y, c = _me()
        me = 4 * x + 2 * y + c
        land[me] = v_ref[...]
        cps = []
        for r in range(1, n_dev):
            fx, fy, fc = (r >> 2) & 1, (r >> 1) & 1, r & 1
            peer = (x ^ fx, y ^ fy, c ^ fc)
            cps.append(pltpu.make_async_remote_copy(
                src_ref=v_ref, dst_ref=land.at[me], send_sem=send.at[r - 1], recv_sem=recv.at[r - 1],
                device_id=peer, device_id_type=MESH))
        for cp in cps:
            cp.start()
        for r in range(1, n_dev):
            fx, fy, fc = (r >> 2) & 1, (r >> 1) & 1, r & 1
            src = 4 * (x ^ fx) + 2 * (y ^ fy) + (c ^ fc)
            pltpu.make_async_remote_copy(
                src_ref=v_ref, dst_ref=land.at[src], send_sem=send.at[r - 1], recv_sem=recv.at[r - 1],
                device_id=(x ^ fx, y ^ fy, c ^ fc), device_id_type=MESH).wait_recv()
        for cp in cps:
            cp.wait_send()
        acc = land[0]
        for r in range(1, n_dev):
            acc = acc + land[r]
        o_ref[...] = acc

    vm = pl.BlockSpec(memory_space=pltpu.VMEM)
    return pl.pallas_call(
        body, in_specs=[vm], out_specs=vm, out_shape=SDS(v.shape, F32),
        scratch_shapes=[pltpu.VMEM((n_dev,) + v.shape, F32), pltpu.SemaphoreType.DMA((n_dev - 1,)),
                        pltpu.SemaphoreType.DMA((n_dev - 1,))],
        name="all_reduce_small")(v)


def kernel(x, mem, norm_gain, mem_norm_gain, w_in, b_forget, q_gain_a, k_gain_a, sinks_a, q_gain_b, k_gain_b, q_gain_c, k_gain_c, w_mem_kv, w_branch_a, w_branch_b, w_branch_c, w_out, loss_target, m_norm_gain, m_mem_norm_gain, m_w_in, m_b_forget, m_q_gain_a, m_k_gain_a, m_sinks_a, m_q_gain_b, m_k_gain_b, m_q_gain_c, m_k_gain_c, m_w_mem_kv, m_w_branch_a, m_w_branch_b, m_w_branch_c, m_w_out, v_norm_gain, v_mem_norm_gain, v_w_in, v_b_forget, v_q_gain_a, v_k_gain_a, v_sinks_a, v_q_gain_b, v_k_gain_b, v_q_gain_c, v_k_gain_c, v_w_mem_kv, v_w_branch_a, v_w_branch_b, v_w_branch_c, v_w_out):
    xi, yi, ci = lax.axis_index("x"), lax.axis_index("y"), lax.axis_index("c")
    chip = jnp.reshape(2 * xi + yi, (1,)).astype(jnp.int32)
    core = jnp.reshape(ci, (1,)).astype(jnp.int32)

    slabs = _pack_w_in(chip, w_in[0])
    mine = [w_mem_kv[0].astype(BF16), w_branch_a[0].astype(BF16), w_branch_b[0].astype(BF16),
            w_branch_c[0].astype(BF16), w_out[0].astype(BF16)]
    g_slab, g_mk, g_wa, g_wb, g_wc, g_wo = _all_gather(slabs, mine)
    w_main, w_fb = _merge_slabs(g_slab)
    w_mk = g_mk.reshape(D_MODEL, 1024)
    wo = g_wo.reshape(D_MODEL, D_MODEL)

    r = _local_step(x[0], mem[0], loss_target[0], w_main, w_fb, w_mk, g_wa, g_wb, g_wc, wo, norm_gain, mem_norm_gain,
                    b_forget, q_gain_a, k_gain_a, sinks_a, q_gain_b, k_gain_b, q_gain_c, k_gain_c)

    grads = [r["d_wmain"], r["d_wfb"], r["d_wmk"], r["d_wa"], r["d_wb"], r["d_wc"], r["d_wo"]]
    got = _exchange_halves(grads)
    halves = [_add_half(g, h, core, ax, name=f"add_half_{t}") for t, (g, h, ax) in enumerate(zip(grads, got, HALF_AXIS))]
    parts = _scatter_pieces(halves)
    sums = [_sum4(p, name=f"sum4_{t}") for t, p in enumerate(parts)]
    theirs = _swap_halves(sums)

    small_names = ["d_gain", "d_mem_gain", "d_bf", "d_qga", "d_kga", "d_sinks", "d_qgb", "d_kgb", "d_qgc", "d_kgc"]
    loss_part = (0.5 / D_MODEL) * jnp.sum(r["sq"], axis=1, keepdims=True)
    packed = _pack_small([r[n] for n in small_names] + [loss_part])
    red = _all_reduce_small(packed)
    small_w = [norm_gain, mem_norm_gain, b_forget, q_gain_a, k_gain_a, sinks_a, q_gain_b, k_gain_b, q_gain_c, k_gain_c]
    small_m = [m_norm_gain, m_mem_norm_gain, m_b_forget, m_q_gain_a, m_k_gain_a, m_sinks_a, m_q_gain_b, m_k_gain_b,
               m_q_gain_c, m_k_gain_c]
    small_v = [v_norm_gain, v_mem_norm_gain, v_b_forget, v_q_gain_a, v_k_gain_a, v_sinks_a, v_q_gain_b, v_k_gain_b,
               v_q_gain_c, v_k_gain_c]
    sizes = [w.shape[1] for w in small_w]
    s_d, s_m, s_v = _adamw(red, _pack_small(small_w), _pack_small(small_m), _pack_small(small_v), tr=8, name="adamw_small")
    g_small = _unpack_small(red, sizes + [1])
    loss = g_small[-1].reshape(())
    d_small, m_small, v_small = _unpack_small(s_d, sizes), _unpack_small(s_m, sizes), _unpack_small(s_v, sizes)

    gw_in, dw_in, mw_in, vw_in = _adamw_w_in(jnp.concatenate([chip, core]), sums[0], theirs[0], sums[1], theirs[1],
                                             w_in[0], m_w_in[0], v_w_in[0])
    big = {}
    for t, nm, w, m, v in ((2, "w_mem_kv", w_mem_kv, m_w_mem_kv, v_w_mem_kv),
                           (3, "w_branch_a", w_branch_a, m_w_branch_a, v_w_branch_a),
                           (4, "w_branch_b", w_branch_b, m_w_branch_b, v_w_branch_b),
                           (5, "w_branch_c", w_branch_c, m_w_branch_c, v_w_branch_c),
                           (6, "w_out", w_out, m_w_out, v_w_out)):
        big[nm] = _adamw_halves(sums[t], theirs[t], core, w[0], m[0], v[0], axis=HALF_AXIS[t], tr=128,
                                name="adamw_" + nm)

    def collect(kind):
        sm = (g_small, d_small, m_small, v_small)[kind]
        win = (gw_in, dw_in, mw_in, vw_in)[kind]
        return ([sm[0], sm[1], win[None]] + [a for a in sm[2:10]]
                + [big[n][kind][None] for n in ("w_mem_kv", "w_branch_a", "w_branch_b", "w_branch_c", "w_out")])

    return (loss, r["grad_x"][None], *collect(0), *collect(1), *collect(2), *collect(3))
```

```python
import functools

import numpy as np
import jax
import jax.numpy as jnp
from jax import lax
from jax.experimental import pallas as pl
from jax.experimental.pallas import tpu as pltpu

F32 = jnp.float32
BF16 = jnp.bfloat16
HI = lax.Precision.HIGHEST
SDS = jax.ShapeDtypeStruct
MESH = pl.DeviceIdType.MESH

D_MODEL = 2048
HEAD_DIM = 64
A_HEADS = 12
A_GROUP = 3
B_HEADS = 12
C_HEADS = 4
C_HEAD_DIM = 128
WINDOW = 128
EPS = 1e-6
NEG = -1e30
LANE = 128

QA, KA, VA, ZA = 0, 768, 1024, 1280
QB, KB, VB, ZB = 2048, 2816, 3584, 4352
QC, ZC = 5120, 5632
GATE = 6144
P_MAIN = 12288
N_FORGET = 12
FORGET_COL = 5120
SHARD_COLS = 3075
SLAB = 3200
SLAB_START = (0, 3072, 6016, 9088)
SLAB_SHIFT = (0, 3, 122, 125)
N_CHIPS = 4

ADAM_LR = 0.001
ADAM_B1 = 0.9
ADAM_B2 = 0.999
ADAM_EPS = 1e-08
ADAM_WD = 0.01
ADAM_STEP = 10

VMEM_LIMIT = 56 * 1024 * 1024


def _params(sem, vmem=VMEM_LIMIT):
    return pltpu.CompilerParams(dimension_semantics=sem, vmem_limit_bytes=vmem)


def _win(tr, width, off):
    return pl.BlockSpec((pl.Element(tr), pl.Element(width)), lambda i, *_: (i * tr, off))


def _rowblk(tr, width):
    return pl.BlockSpec((tr, width), lambda i, *_: (i, 0))


def _const(shape):
    nd = len(shape)
    return pl.BlockSpec(shape, lambda *_: (0,) * nd)


def _rms(x, g):
    return x * lax.rsqrt(jnp.mean(x * x, axis=-1, keepdims=True) + EPS) * g


def _head_mean_impl(x2, bd):
    hi = x2.astype(BF16)
    lo = (x2 - hi.astype(F32)).astype(BF16)
    return _dot(hi, bd) + _dot(lo, bd)


@jax.custom_vjp
def _head_mean(x2, bd):
    return _head_mean_impl(x2, bd)


_head_mean.defvjp(lambda x2, bd: (_head_mean_impl(x2, bd), bd),
                  lambda bd, g: (_head_mean_impl(g, bd), jnp.zeros_like(bd)))


def _head_norm(x, g_tiled, bd):
    return x * lax.rsqrt(_head_mean(x * x, bd) + EPS) * g_tiled


def _silu(z):
    return z * jax.nn.sigmoid(z)


def _dot_nt(a, b):
    return lax.dot_general(a, b, (((1,), (1,)), ((), ())), preferred_element_type=F32)


def _dot_tn(a, b):
    return lax.dot_general(a, b, (((0,), (0,)), ((), ())), preferred_element_type=F32)


def _dot(a, b):
    return jnp.dot(a, b, preferred_element_type=F32)


def _swa_fn(qk, vz, qkp, vzp, qg, kg, sinks, bd, bias, first):
    q = _head_norm(qk[:, :768], qg, bd)
    k2 = jnp.concatenate([qkp[:, 768:], qk[:, 768:]], axis=0)
    k2 = _head_norm(k2, kg, bd[:256, :256])
    v2 = jnp.concatenate([vzp[:, :256], vz[:, :256]], axis=0)
    z = vz[:, 256:]
    rows = A_GROUP * WINDOW
    kj = lax.broadcasted_iota(jnp.int32, (rows, 2 * WINDOW), 1)
    no_prev = kj < WINDOW * first.astype(jnp.int32)
    qb = q.astype(BF16)
    kb = k2.astype(BF16)
    vb = v2.astype(BF16)
    outs = [None] * A_HEADS
    for g in range(A_HEADS // A_GROUP):
        heads = [A_GROUP * g + u for u in range(A_GROUP)]
        qs = jnp.concatenate([qb[:, 64 * h:64 * h + 64] for h in heads], axis=0)
        s = _dot_nt(qs, kb[:, 64 * g:64 * g + 64]) * (HEAD_DIM ** -0.5) + bias[g]
        s = jnp.where(no_prev, NEG, s)
        sink = jnp.concatenate([jnp.broadcast_to(sinks[:, h:h + 1], (WINDOW, 1)) for h in heads], axis=0)
        m = lax.stop_gradient(jnp.maximum(jnp.max(s, axis=-1, keepdims=True), sink))
        p = jnp.exp(s - m)
        den = jnp.sum(p, axis=-1, keepdims=True) + jnp.exp(sink - m)
        o = _dot((p * (1.0 / den)).astype(BF16), vb[:, 64 * g:64 * g + 64])
        for u, h in enumerate(heads):
            outs[h] = o[WINDOW * u:WINDOW * u + WINDOW, :]
    return jnp.concatenate(outs, axis=1) * _silu(z)


def _swa_bias():
    qi = np.arange(WINDOW)[:, None]
    kj = np.arange(2 * WINDOW)[None, :]
    rel = qi + WINDOW - kj
    valid = (rel >= 0) & (rel < WINDOW)
    out = np.zeros((A_HEADS // A_GROUP, A_GROUP * WINDOW, 2 * WINDOW), np.float32)
    for h in range(A_HEADS):
        slope = np.float32(2.0 ** (-8.0 * (h + 1) / A_HEADS))
        blk = np.where(valid, -slope * rel.astype(np.float32), np.float32(NEG))
        g, u = divmod(h, A_GROUP)
        out[g, WINDOW * u:WINDOW * u + WINDOW, :] = blk
    return jnp.asarray(out)


def _mem_fn(qz, mkv, qg, kg, bd):
    q = _head_norm(qz[:, :512], qg, bd).astype(BF16)
    k = _head_norm(mkv[:, :512], kg, bd).astype(BF16)
    v = mkv[:, 512:].astype(BF16)
    z = qz[:, 512:]
    outs = []
    for h in range(C_HEADS):
        sl = slice(128 * h, 128 * h + 128)
        s = _dot_nt(q[:, sl], k[:, sl]) * (C_HEAD_DIM ** -0.5)
        m = lax.stop_gradient(jnp.max(s, axis=-1, keepdims=True))
        p = jnp.exp(s - m)
        den = jnp.sum(p, axis=-1, keepdims=True)
        outs.append(_dot((p * (1.0 / den)).astype(BF16), v[:, sl]))
    return jnp.concatenate(outs, axis=1) * _silu(z)


def _qn_fn(q, g, bd):
    return _head_norm(q, g, bd) * (HEAD_DIM ** -0.5)


def _kn_fn(k, g, bd):
    return _head_norm(k, g, bd)


def _block_diag(width, hd):
    i = np.arange(width) // hd
    return jnp.asarray((i[:, None] == i[None, :]).astype(np.float32) / hd, BF16)


def _head_sum(width, hd):
    i = np.arange(width) // hd
    return jnp.asarray((i[:, None] == np.arange(LANE)[None, :]).astype(np.float32))


def _rms_fwd(x, g, *, tr, name):
    rows, dm = x.shape

    def body(x_ref, g_ref, o_ref):
        o_ref[...] = _rms(x_ref[...], g_ref[...]).astype(BF16)

    return pl.pallas_call(
        body, grid=(rows // tr,),
        in_specs=[_rowblk(tr, dm), _const((1, dm))],
        out_specs=_rowblk(tr, dm),
        out_shape=SDS((rows, dm), BF16), name=name,
        compiler_params=_params(("parallel",)))(x, g)


def _rms_bwd(x, g, dy, resid, *, tr, name):
    rows, dm = x.shape
    want_dx = resid is not None

    def body(*refs):
        if want_dx:
            x_ref, g_ref, dy_ref, r_ref, dx_ref, dg_ref = refs
        else:
            x_ref, g_ref, dy_ref, dg_ref = refs
        _, vjp = jax.vjp(_rms, x_ref[...], g_ref[...])
        dx, dg = vjp(dy_ref[...])

        @pl.when(pl.program_id(0) == 0)
        def _():
            dg_ref[...] = jnp.zeros_like(dg_ref)

        dg_ref[...] += dg
        if want_dx:
            dx_ref[...] = r_ref[...] + dx

    ins = [x, g, dy] + ([resid] if want_dx else [])
    in_specs = [_rowblk(tr, dm), _const((1, dm)), _rowblk(tr, dm)] + ([_rowblk(tr, dm)] if want_dx else [])
    out_specs = ([_rowblk(tr, dm)] if want_dx else []) + [_const((1, dm))]
    out_shape = ([SDS((rows, dm), F32)] if want_dx else []) + [SDS((1, dm), F32)]
    return pl.pallas_call(
        body, grid=(rows // tr,), in_specs=in_specs, out_specs=out_specs, out_shape=out_shape, name=name,
        compiler_params=_params(("arbitrary",)))(*ins)


class _Comm:
    def __init__(self, ins, out_shapes, sems, start, finish):
        self.ins, self.out_shapes, self.sems, self.start, self.finish = list(ins), list(out_shapes), list(sems), start, finish


def _run_comm(comm, name):
    n_in, n_out = len(comm.ins), len(comm.out_shapes)

    def body(*refs):
        ins, outs, sems = refs[:n_in], refs[n_in:n_in + n_out], refs[n_in + n_out:]
        comm.start(ins, outs, sems)
        comm.finish(ins, outs, sems)

    hbm = pl.BlockSpec(memory_space=pl.ANY)
    return pl.pallas_call(body, in_specs=[hbm] * n_in, out_specs=[hbm] * n_out, out_shape=comm.out_shapes,
                          scratch_shapes=comm.sems, name=name)(*comm.ins)


def _matmul(a, b, *, dims, out_dtype, tm, tn, tk, name, add=None, comms=()):
    if dims == "tn":
        kdim, m = a.shape
    else:
        m, kdim = a.shape
    n = b.shape[0] if dims == "nt" else b.shape[1]
    tm, tn, tk = min(tm, m), min(tn, n), min(tk, kdim)
    assert m % tm == 0 and n % tn == 0 and kdim % tk == 0, (name, a.shape, b.shape)
    ni, nj, nk = m // tm, n // tn, kdim // tk
    has_add = add is not None
    n_mm_in = 3 if has_add else 2
    c_in = [len(c.ins) for c in comms]
    c_out = [len(c.out_shapes) for c in comms]
    c_sem = [len(c.sems) for c in comms]

    def body(*refs):
        a_ref, b_ref = refs[0], refs[1]
        add_ref = refs[2] if has_add else None
        pos = n_mm_in
        cin = []
        for cnt in c_in:
            cin.append(refs[pos:pos + cnt])
            pos += cnt
        o_ref = refs[pos]
        pos += 1
        cout = []
        for cnt in c_out:
            cout.append(refs[pos:pos + cnt])
            pos += cnt
        acc = refs[pos]
        pos += 1
        csem = []
        for cnt in c_sem:
            csem.append(refs[pos:pos + cnt])
            pos += cnt
        i, j, k = pl.program_id(0), pl.program_id(1), pl.program_id(2)

        if comms:
            @pl.when((i == 0) & (j == 0) & (k == 0))
            def _():
                for c, ci, co, cs in zip(comms, cin, cout, csem):
                    c.start(ci, co, cs)

        if dims == "nn":
            part = _dot(a_ref[...], b_ref[...])
        elif dims == "nt":
            part = _dot_nt(a_ref[...], b_ref[...])
        else:
            part = _dot_tn(a_ref[...], b_ref[...])

        @pl.when(k == 0)
        def _():
            acc[...] = part + add_ref[...] if has_add else part

        @pl.when(k > 0)
        def _():
            acc[...] += part

        @pl.when(k == nk - 1)
        def _():
            o_ref[...] = acc[...].astype(out_dtype)

        if comms:
            @pl.when((i == ni - 1) & (j == nj - 1) & (k == nk - 1))
            def _():
                for c, ci, co, cs in zip(comms, cin, cout, csem):
                    c.finish(ci, co, cs)

    a_spec = pl.BlockSpec((tk, tm), lambda i, j, k: (k, i)) if dims == "tn" else pl.BlockSpec((tm, tk), lambda i, j, k: (i, k))
    b_spec = pl.BlockSpec((tn, tk), lambda i, j, k: (j, k)) if dims == "nt" else pl.BlockSpec((tk, tn), lambda i, j, k: (k, j))
    o_spec = pl.BlockSpec((tm, tn), lambda i, j, k: (i, j))
    hbm = pl.BlockSpec(memory_space=pl.ANY)
    ins = [a, b] + ([add] if has_add else []) + [x for c in comms for x in c.ins]
    in_specs = [a_spec, b_spec] + ([o_spec] if has_add else []) + [hbm] * sum(c_in)
    out_specs = [o_spec] + [hbm] * sum(c_out)
    out_shape = [SDS((m, n), out_dtype)] + [s for c in comms for s in c.out_shapes]
    scratch = [pltpu.VMEM((tm, tn), F32)] + [s for c in comms for s in c.sems]
    sem = ("arbitrary",) * 3 if comms else ("parallel", "parallel", "arbitrary")
    res = pl.pallas_call(
        body, grid=(ni, nj, nk), in_specs=in_specs, out_specs=out_specs, out_shape=out_shape, scratch_shapes=scratch,
        name=name, compiler_params=_params(sem))(*ins)
    if not comms:
        return res[0]
    outs, pos = [], 1
    for cnt in c_out:
        outs.append(list(res[pos:pos + cnt]))
        pos += cnt
    return res[0], outs


def _swa_specs(nb):
    prev = lambda off: pl.BlockSpec((pl.Element(WINDOW), pl.Element(1024)),
                                    lambda n: (jnp.maximum(n - 1, 0) * WINDOW, off))
    return [_win(WINDOW, 1024, QA), _win(WINDOW, 1024, VA), prev(QA), prev(VA),
            _const((1, 768)), _const((1, 256)), _const((1, LANE)), _const((768, 768)),
            _const((A_HEADS // A_GROUP, A_GROUP * WINDOW, 2 * WINDOW))]


def _swa_fwd(proj, qg, kg, sinks, bd, bias):
    s = proj.shape[0]
    nb = s // WINDOW

    def body(qk_ref, vz_ref, qkp_ref, vzp_ref, qg_ref, kg_ref, sk_ref, bd_ref, bias_ref, o_ref):
        first = pl.program_id(0) == 0
        o_ref[...] = _swa_fn(qk_ref[...], vz_ref[...], qkp_ref[...], vzp_ref[...], qg_ref[...], kg_ref[...],
                             sk_ref[...], bd_ref[...], bias_ref[...], first).astype(BF16)

    return pl.pallas_call(
        body, grid=(nb,), in_specs=_swa_specs(nb), out_specs=_rowblk(WINDOW, 768),
        out_shape=SDS((s, 768), BF16), name="swa_fwd",
        compiler_params=_params(("parallel",)))(proj, proj, proj, proj, qg, kg, sinks, bd, bias)


def _swa_bwd(proj, qg, kg, sinks, bd, bias, dga):
    s = proj.shape[0]
    nb = s // WINDOW

    def body(qk_ref, vz_ref, qkp_ref, vzp_ref, qg_ref, kg_ref, sk_ref, bd_ref, bias_ref, dg_ref,
             dcur_ref, dprev_ref, dqg_ref, dkg_ref, dsk_ref):
        first = pl.program_id(0) == 0
        bd_v = bd_ref[...]
        bias_v = bias_ref[...]
        fn = lambda qk, vz, qkp, vzp, qg_, kg_, sk: _swa_fn(qk, vz, qkp, vzp, qg_, kg_, sk, bd_v, bias_v, first)
        _, vjp = jax.vjp(fn, qk_ref[...], vz_ref[...], qkp_ref[...], vzp_ref[...], qg_ref[...], kg_ref[...], sk_ref[...])
        dqk, dvz, dqkp, dvzp, dqg, dkg, dsk = vjp(dg_ref[...])

        @pl.when(first)
        def _():
            dqg_ref[...] = jnp.zeros_like(dqg_ref)
            dkg_ref[...] = jnp.zeros_like(dkg_ref)
            dsk_ref[...] = jnp.zeros_like(dsk_ref)

        dqg_ref[...] += dqg
        dkg_ref[...] += dkg
        dsk_ref[...] += dsk
        dcur_ref[...] = jnp.concatenate([dqk, dvz], axis=1)
        dprev_ref[...] = jnp.concatenate([dqkp[:, 768:], dvzp[:, :256]], axis=1)

    return pl.pallas_call(
        body, grid=(nb,), in_specs=_swa_specs(nb) + [_rowblk(WINDOW, 768)],
        out_specs=[_rowblk(WINDOW, 2048), pl.BlockSpec((None, WINDOW, 512), lambda n: (n, 0, 0)),
                   _const((1, 768)), _const((1, 256)), _const((1, LANE))],
        out_shape=[SDS((s, 2048), F32), SDS((nb, WINDOW, 512), F32), SDS((1, 768), F32), SDS((1, 256), F32),
                   SDS((1, LANE), F32)],
        name="swa_bwd", compiler_params=_params(("arbitrary",)))(proj, proj, proj, proj, qg, kg, sinks, bd, bias, dga)


def _swa_combine(dcur, dprev):
    s = dcur.shape[0]
    nb = s // WINDOW

    def body(c_ref, p_ref, o_ref):
        c = c_ref[...]
        nxt = jnp.where(pl.program_id(0) == nb - 1, 0.0, p_ref[...])
        o_ref[...] = jnp.concatenate([c[:, :768], c[:, 768:1280] + nxt, c[:, 1280:]], axis=1).astype(BF16)

    return pl.pallas_call(
        body, grid=(nb,),
        in_specs=[_rowblk(WINDOW, 2048), pl.BlockSpec((None, WINDOW, 512), lambda n: (jnp.minimum(n + 1, nb - 1), 0, 0))],
        out_specs=_rowblk(WINDOW, 2048), out_shape=SDS((s, 2048), BF16), name="swa_combine",
        compiler_params=_params(("parallel",)))(dcur, dprev)


def _mem_fwd(proj, mkv, qg, kg, bd, *, tr):
    s = proj.shape[0]

    def body(qz_ref, mkv_ref, qg_ref, kg_ref, bd_ref, o_ref):
        o_ref[...] = _mem_fn(qz_ref[...], mkv_ref[...], qg_ref[...], kg_ref[...], bd_ref[...]).astype(BF16)

    return pl.pallas_call(
        body, grid=(s // tr,),
        in_specs=[_win(tr, 1024, QC), _const(mkv.shape), _const((1, 512)), _const((1, 512)), _const((512, 512))],
        out_specs=_rowblk(tr, 512), out_shape=SDS((s, 512), BF16), name="mem_fwd",
        compiler_params=_params(("parallel",)))(proj, mkv, qg, kg, bd)


def _mem_bwd(proj, mkv, qg, kg, bd, dgc, *, tr):
    s = proj.shape[0]

    def body(qz_ref, mkv_ref, qg_ref, kg_ref, bd_ref, dg_ref, dqz_ref, dmkv_ref, dqg_ref, dkg_ref):
        bd_v = bd_ref[...]
        fn = lambda qz, mkv_, qg_, kg_: _mem_fn(qz, mkv_, qg_, kg_, bd_v)
        _, vjp = jax.vjp(fn, qz_ref[...], mkv_ref[...], qg_ref[...], kg_ref[...])
        dqz, dmkv, dqg, dkg = vjp(dg_ref[...])

        @pl.when(pl.program_id(0) == 0)
        def _():
            dmkv_ref[...] = jnp.zeros_like(dmkv_ref)
            dqg_ref[...] = jnp.zeros_like(dqg_ref)
            dkg_ref[...] = jnp.zeros_like(dkg_ref)

        dmkv_ref[...] += dmkv
        dqg_ref[...] += dqg
        dkg_ref[...] += dkg
        dqz_ref[...] = dqz.astype(BF16)

    return pl.pallas_call(
        body, grid=(s // tr,),
        in_specs=[_win(tr, 1024, QC), _const(mkv.shape), _const((1, 512)), _const((1, 512)), _const((512, 512)),
                  _rowblk(tr, 512)],
        out_specs=[_rowblk(tr, 1024), _const(mkv.shape), _const((1, 512)), _const((1, 512))],
        out_shape=[SDS((s, 1024), BF16), SDS(mkv.shape, F32), SDS((1, 512), F32), SDS((1, 512), F32)],
        name="mem_bwd", compiler_params=_params(("arbitrary",)))(proj, mkv, qg, kg, bd, dgc)


def _log_sigmoid(x):
    return jnp.minimum(x, 0.0) - jnp.log1p(jnp.exp(-jnp.abs(x)))


def _fox_prep(proj, fbl, qg, kg, bfor, bd, *, tr):
    s = proj.shape[0]
    tri = jnp.asarray(np.tril(np.ones((tr, tr), np.float32)))

    def body(q_ref, k_ref, fb_ref, qg_ref, kg_ref, bf_ref, bd_ref, tri_ref, qn_ref, kn_ref, cq_ref, ck_ref, carry):
        @pl.when(pl.program_id(0) == 0)
        def _():
            carry[...] = jnp.zeros_like(carry)

        bd_v = bd_ref[...]
        qn_ref[...] = _qn_fn(q_ref[...], qg_ref[...], bd_v).astype(BF16)
        kn_ref[...] = _kn_fn(k_ref[...], kg_ref[...], bd_v).astype(BF16)
        lane = lax.broadcasted_iota(jnp.int32, (tr, LANE), 1)
        logf = jnp.where(lane < N_FORGET, _log_sigmoid(fb_ref[...] + bf_ref[...]), 0.0)
        c = jnp.dot(tri_ref[...], logf, precision=HI, preferred_element_type=F32) + carry[...]
        cq_ref[...] = c
        ck_ref[...] = jnp.transpose(c)
        carry[...] = c[tr - 1:tr, :]

    return pl.pallas_call(
        body, grid=(s // tr,),
        in_specs=[_win(tr, 768, QB), _win(tr, 768, KB), _rowblk(tr, LANE), _const((1, 768)), _const((1, 768)),
                  _const((1, LANE)), _const((768, 768)), _const((tr, tr))],
        out_specs=[_rowblk(tr, 768), _rowblk(tr, 768), _rowblk(tr, LANE), pl.BlockSpec((LANE, tr), lambda i: (0, i))],
        out_shape=[SDS((s, 768), BF16), SDS((s, 768), BF16), SDS((s, LANE), F32), SDS((LANE, s), F32)],
        scratch_shapes=[pltpu.VMEM((1, LANE), F32)], name="fox_prep",
        compiler_params=_params(("arbitrary",)))(proj, proj, fbl, qg, kg, bfor, bd, tri)


def _fox_tiles(s):
    tq = min(256, s)
    tk = min(512, s)
    return tq, tk


def _fox_fwd(proj, qn, kn, cq, ck):
    s = proj.shape[0]
    tq, tk = _fox_tiles(s)
    nq, nk = s // tq, s // tk

    def last_k(i):
        return (i * tq + tq - 1) // tk

    def body(q_ref, k_ref, v_ref, cq_ref, ck_ref, z_ref, gb_ref, yb_ref, lse_ref, acc, m_s, l_s):
        i, j = pl.program_id(0), pl.program_id(1)

        @pl.when(j == 0)
        def _():
            acc[...] = jnp.zeros_like(acc)
            m_s[...] = jnp.full_like(m_s, NEG)
            l_s[...] = jnp.ones_like(l_s)

        @pl.when(j <= last_k(i))
        def _():
            qpos = i * tq + lax.broadcasted_iota(jnp.int32, (tq, tk), 0)
            kpos = j * tk + lax.broadcasted_iota(jnp.int32, (tq, tk), 1)
            mask = kpos <= qpos
            q, k = q_ref[...], k_ref[...]
            v = v_ref[...].astype(BF16)
            cqv, ckv = cq_ref[...], ck_ref[...]
            m_all, l_all = m_s[...], l_s[...]
            lane = lax.broadcasted_iota(jnp.int32, (tq, LANE), 1)
            m_out, l_out = m_all, l_all
            for hp in range(B_HEADS // 2):
                acc_pair = acc[:, 128 * hp:128 * hp + 128]
                new = []
                for u in range(2):
                    h = 2 * hp + u
                    sl = slice(64 * h, 64 * h + 64)
                    sc = _dot_nt(q[:, sl], k[:, sl]) + cqv[:, h:h + 1] - ckv[h:h + 1, :]
                    sc = jnp.where(mask, sc, NEG)
                    m_prev = m_all[:, h:h + 1]
                    m_new = jnp.maximum(m_prev, jnp.max(sc, axis=-1, keepdims=True))
                    alpha = jnp.exp(m_prev - m_new)
                    p = jnp.exp(sc - m_new)
                    l_new = alpha * l_all[:, h:h + 1] + jnp.sum(p, axis=-1, keepdims=True)
                    new.append(alpha * acc_pair[:, 64 * u:64 * u + 64] + _dot(p.astype(BF16), v[:, sl]))
                    m_out = jnp.where(lane == h, m_new, m_out)
                    l_out = jnp.where(lane == h, l_new, l_out)
                acc[:, 128 * hp:128 * hp + 128] = jnp.concatenate(new, axis=1)
            m_s[...] = m_out
            l_s[...] = l_out

        @pl.when(j == nk - 1)
        def _():
            l_all = l_s[...]
            inv = 1.0 / l_all
            a = acc[...]
            y = jnp.concatenate([a[:, 64 * h:64 * h + 64] * inv[:, h:h + 1] for h in range(B_HEADS)], axis=1)
            yb_ref[...] = y
            gb_ref[...] = (y * _silu(z_ref[...])).astype(BF16)
            lse_ref[...] = m_s[...] + jnp.log(l_all)

    kmap = lambda i, j: (jnp.minimum(j, last_k(i)), 0)
    return pl.pallas_call(
        body, grid=(nq, nk),
        in_specs=[pl.BlockSpec((tq, 768), lambda i, j: (i, 0)),
                  pl.BlockSpec((tk, 768), kmap),
                  pl.BlockSpec((pl.Element(tk), pl.Element(768)), lambda i, j: (jnp.minimum(j, last_k(i)) * tk, VB)),
                  pl.BlockSpec((tq, LANE), lambda i, j: (i, 0)),
                  pl.BlockSpec((16, tk), lambda i, j: (0, jnp.minimum(j, last_k(i)))),
                  pl.BlockSpec((pl.Element(tq), pl.Element(768)), lambda i, j: (i * tq, ZB))],
        out_specs=[pl.BlockSpec((tq, 768), lambda i, j: (i, 0)), pl.BlockSpec((tq, 768), lambda i, j: (i, 0)),
                   pl.BlockSpec((tq, LANE), lambda i, j: (i, 0))],
        out_shape=[SDS((s, 768), BF16), SDS((s, 768), F32), SDS((s, LANE), F32)],
        scratch_shapes=[pltpu.VMEM((tq, 768), F32), pltpu.VMEM((tq, LANE), F32), pltpu.VMEM((tq, LANE), F32)],
        name="fox_fwd", compiler_params=_params(("parallel", "arbitrary")))(qn, kn, proj, cq, ck, proj)


def _fox_bwd_pre(proj, yb, dgb, hsum, *, tr):
    s = proj.shape[0]

    def body(z_ref, y_ref, dg_ref, hs_ref, dy_ref, dz_ref, dl_ref):
        z, y, dg = z_ref[...], y_ref[...], dg_ref[...]
        sg = jax.nn.sigmoid(z)
        dy = dg * (z * sg)
        dy_ref[...] = dy.astype(BF16)
        dz_ref[...] = (dg * y * (sg * (1.0 + z * (1.0 - sg)))).astype(BF16)
        dl_ref[...] = jnp.dot(dy * y, hs_ref[...], precision=HI, preferred_element_type=F32)

    return pl.pallas_call(
        body, grid=(s // tr,),
        in_specs=[_win(tr, 768, ZB), _rowblk(tr, 768), _rowblk(tr, 768), _const((768, LANE))],
        out_specs=[_rowblk(tr, 768), _rowblk(tr, 768), _rowblk(tr, LANE)],
        out_shape=[SDS((s, 768), BF16), SDS((s, 768), BF16), SDS((s, LANE), F32)], name="fox_bwd_pre",
        compiler_params=_params(("parallel",)))(proj, yb, dgb, hsum)


def _fox_bwd(proj, qn, kn, cq, ck, lse, delta, dyb):
    s = proj.shape[0]
    tq, tk = _fox_tiles(s)
    nq, nk = s // tq, s // tk

    def first_q(j):
        return (j * tk) // tq

    def body(q_ref, k_ref, v_ref, cq_ref, ck_ref, lse_ref, dl_ref, dy_ref,
             dq_ref, dcq_ref, dk_ref, dv_ref, dck_ref, dk_acc, dv_acc, dck_acc):
        j, i = pl.program_id(0), pl.program_id(1)

        @pl.when((j == 0) & (i == 0))
        def _():
            dq_ref[...] = jnp.zeros_like(dq_ref)
            dcq_ref[...] = jnp.zeros_like(dcq_ref)

        @pl.when(i == 0)
        def _():
            dk_acc[...] = jnp.zeros_like(dk_acc)
            dv_acc[...] = jnp.zeros_like(dv_acc)
            dck_acc[...] = jnp.zeros_like(dck_acc)

        @pl.when(i >= first_q(j))
        def _():
            qpos = i * tq + lax.broadcasted_iota(jnp.int32, (tq, tk), 0)
            kpos = j * tk + lax.broadcasted_iota(jnp.int32, (tq, tk), 1)
            mask = kpos <= qpos
            q, k = q_ref[...], k_ref[...]
            v = v_ref[...].astype(BF16)
            dy = dy_ref[...]
            cqv, ckv, lsev, dlv = cq_ref[...], ck_ref[...], lse_ref[...], dl_ref[...]
            lane = lax.broadcasted_iota(jnp.int32, (tq, LANE), 1)
            rows = pl.ds(pl.multiple_of(i * tq, tq), tq)
            dcq_t = jnp.zeros((tq, LANE), F32)
            for hp in range(B_HEADS // 2):
                dq_new, dk_new, dv_new = [], [], []
                for u in range(2):
                    h = 2 * hp + u
                    sl = slice(64 * h, 64 * h + 64)
                    sc = _dot_nt(q[:, sl], k[:, sl]) + cqv[:, h:h + 1] - ckv[h:h + 1, :]
                    sc = jnp.where(mask, sc, NEG)
                    p = jnp.exp(sc - lsev[:, h:h + 1])
                    dp = _dot_nt(dy[:, sl], v[:, sl])
                    ds = p * (dp - dlv[:, h:h + 1])
                    dsb = ds.astype(BF16)
                    dv_new.append(_dot_tn(p.astype(BF16), dy[:, sl]))
                    dk_new.append(_dot_tn(dsb, q[:, sl]))
                    dq_new.append(_dot(dsb, k[:, sl]))
                    dcq_t = jnp.where(lane == h, jnp.sum(ds, axis=-1, keepdims=True), dcq_t)
                    dck_acc[h:h + 1, :] -= jnp.sum(ds, axis=0, keepdims=True)
                cols = slice(128 * hp, 128 * hp + 128)
                dq_ref[rows, cols] += jnp.concatenate(dq_new, axis=1)
                dk_acc[:, cols] += jnp.concatenate(dk_new, axis=1)
                dv_acc[:, cols] += jnp.concatenate(dv_new, axis=1)
            dcq_ref[rows, :] += dcq_t

        @pl.when(i == nq - 1)
        def _():
            dk_ref[...] = dk_acc[...]
            dv_ref[...] = dv_acc[...].astype(BF16)
            dck_ref[...] = jnp.concatenate([dck_acc[...], jnp.zeros((LANE - 16, tk), F32)], axis=0)

    qmap = lambda j, i: (jnp.maximum(i, first_q(j)), 0)
    return pl.pallas_call(
        body, grid=(nk, nq),
        in_specs=[pl.BlockSpec((tq, 768), qmap),
                  pl.BlockSpec((tk, 768), lambda j, i: (j, 0)),
                  pl.BlockSpec((pl.Element(tk), pl.Element(768)), lambda j, i: (j * tk, VB)),
                  pl.BlockSpec((tq, LANE), qmap),
                  pl.BlockSpec((16, tk), lambda j, i: (0, j)),
                  pl.BlockSpec((tq, LANE), qmap),
                  pl.BlockSpec((tq, LANE), qmap),
                  pl.BlockSpec((tq, 768), qmap)],
        out_specs=[_const((s, 768)), _const((s, LANE)),
                   pl.BlockSpec((tk, 768), lambda j, i: (j, 0)), pl.BlockSpec((tk, 768), lambda j, i: (j, 0)),
                   pl.BlockSpec((LANE, tk), lambda j, i: (0, j))],
        out_shape=[SDS((s, 768), F32), SDS((s, LANE), F32), SDS((s, 768), F32), SDS((s, 768), BF16),
                   SDS((LANE, s), F32)],
        scratch_shapes=[pltpu.VMEM((tk, 768), F32), pltpu.VMEM((tk, 768), F32), pltpu.VMEM((16, tk), F32)],
        name="fox_bwd", compiler_params=_params(("arbitrary", "arbitrary")))(qn, kn, proj, cq, ck, lse, delta, dyb)


def _fox_bwd_post(proj, fbl, qg, kg, bfor, bd, dqn, dkn, dcq, dck, *, tr):
    s = proj.shape[0]
    nb = s // tr
    triu = jnp.asarray(np.triu(np.ones((tr, tr), np.float32)))
    rev = lambda i: nb - 1 - i

    def body(q_ref, k_ref, fb_ref, qg_ref, kg_ref, bf_ref, bd_ref, tri_ref, dqn_ref, dkn_ref, dcq_ref, dck_ref,
             dq_ref, dk_ref, dfb_ref, dqg_ref, dkg_ref, dbf_ref, carry):
        @pl.when(pl.program_id(0) == 0)
        def _():
            carry[...] = jnp.zeros_like(carry)
            dqg_ref[...] = jnp.zeros_like(dqg_ref)
            dkg_ref[...] = jnp.zeros_like(dkg_ref)
            dbf_ref[...] = jnp.zeros_like(dbf_ref)

        bd_v = bd_ref[...]
        _, vjp_q = jax.vjp(lambda q, g: _qn_fn(q, g, bd_v), q_ref[...], qg_ref[...])
        dq, dqg = vjp_q(dqn_ref[...])
        _, vjp_k = jax.vjp(lambda k, g: _kn_fn(k, g, bd_v), k_ref[...], kg_ref[...])
        dk, dkg = vjp_k(dkn_ref[...])
        dq_ref[...] = dq.astype(BF16)
        dk_ref[...] = dk.astype(BF16)
        dqg_ref[...] += dqg
        dkg_ref[...] += dkg

        dc = dcq_ref[...] + jnp.transpose(dck_ref[...])
        dlogf = jnp.dot(tri_ref[...], dc, precision=HI, preferred_element_type=F32) + carry[...]
        carry[...] = dlogf[0:1, :]
        lane = lax.broadcasted_iota(jnp.int32, (tr, LANE), 1)
        xf = fb_ref[...] + bf_ref[...]
        dfb = jnp.where(lane < N_FORGET, dlogf * jax.nn.sigmoid(-xf), 0.0)
        dfb_ref[...] = dfb.astype(BF16)
        dbf_ref[...] += jnp.sum(dfb, axis=0, keepdims=True)

    rb = lambda w: pl.BlockSpec((tr, w), lambda i: (rev(i), 0))
    wn = lambda w, off: pl.BlockSpec((pl.Element(tr), pl.Element(w)), lambda i: (rev(i) * tr, off))
    return pl.pallas_call(
        body, grid=(nb,),
        in_specs=[wn(768, QB), wn(768, KB), rb(LANE), _const((1, 768)), _const((1, 768)), _const((1, LANE)),
                  _const((768, 768)), _const((tr, tr)), rb(768), rb(768), rb(LANE),
                  pl.BlockSpec((LANE, tr), lambda i: (0, rev(i)))],
        out_specs=[rb(768), rb(768), rb(LANE), _const((1, 768)), _const((1, 768)), _const((1, LANE))],
        out_shape=[SDS((s, 768), BF16), SDS((s, 768), BF16), SDS((s, LANE), BF16), SDS((1, 768), F32),
                   SDS((1, 768), F32), SDS((1, LANE), F32)],
        scratch_shapes=[pltpu.VMEM((1, LANE), F32)], name="fox_bwd_post",
        compiler_params=_params(("arbitrary",)))(proj, proj, fbl, qg, kg, bfor, bd, triu, dqn, dkn, dcq, dck)


AUG = 128 * B_HEADS
COL_A, COL_B = 64, 67


def _split3(c):
    hi = c.astype(BF16)
    r1 = c - hi.astype(F32)
    mid = r1.astype(BF16)
    lo = (r1 - mid.astype(F32)).astype(BF16)
    return hi, mid, lo


def _expand_mats():
    def mat(col0):
        e = np.zeros((768 + 3 * LANE, AUG), np.float32)
        for h in range(B_HEADS):
            for d in range(HEAD_DIM):
                e[64 * h + d, 128 * h + d] = 1.0
            for part in range(3):
                e[768 + LANE * part + h, 128 * h + col0 + part] = 1.0
        return e

    def ones(col0):
        o = np.zeros((1, AUG), np.float32)
        for h in range(B_HEADS):
            o[0, 128 * h + col0:128 * h + col0 + 3] = 1.0
        return o

    return (jnp.asarray(mat(COL_A), BF16), jnp.asarray(mat(COL_B), BF16), jnp.asarray(ones(COL_A)), jnp.asarray(ones(COL_B)))


def _augment(data_bf16, triple, emat, ones_row):
    parts = [data_bf16] + (list(triple) if triple is not None else [jnp.zeros((data_bf16.shape[0], LANE), BF16)] * 3)
    wide = _dot(jnp.concatenate(parts, axis=1), emat)
    if ones_row is not None:
        wide = wide + ones_row
    return wide.astype(BF16)


def _compact(wide):
    return jnp.concatenate([wide[:, 128 * h:128 * h + 64] for h in range(B_HEADS)], axis=1)


def _lane_of_heads(wide, col):
    rows = wide.shape[0]
    lane = lax.broadcasted_iota(jnp.int32, (rows, LANE), 1)
    out = jnp.zeros((rows, LANE), F32)
    for h in range(B_HEADS):
        out = jnp.where(lane == h, wide[:, 128 * h + col:128 * h + col + 1], out)
    return out


def _fox2_prep(proj, fbl, qg, kg, bfor, bd, ea, eb, ones_a, ones_b, *, tr):
    s = proj.shape[0]
    tri = jnp.asarray(np.tril(np.ones((tr, tr), np.float32)))

    def body(q_ref, k_ref, v_ref, fb_ref, qg_ref, kg_ref, bf_ref, bd_ref, tri_ref, ea_ref, eb_ref, oa_ref, ob_ref,
             qa_ref, ka_ref, va_ref, qn_ref, c_ref, carry):
        @pl.when(pl.program_id(0) == 0)
        def _():
            carry[...] = jnp.zeros_like(carry)

        bd_v = bd_ref[...]
        lane = lax.broadcasted_iota(jnp.int32, (tr, LANE), 1)
        logf = jnp.where(lane < N_FORGET, _log_sigmoid(fb_ref[...] + bf_ref[...]), 0.0)
        c = jnp.dot(tri_ref[...], logf, precision=HI, preferred_element_type=F32) + carry[...]
        c_ref[...] = c
        carry[...] = c[tr - 1:tr, :]
        qn = _qn_fn(q_ref[...], qg_ref[...], bd_v).astype(BF16)
        kn = _kn_fn(k_ref[...], kg_ref[...], bd_v).astype(BF16)
        qn_ref[...] = qn
        qa_ref[...] = _augment(qn, _split3(c), ea_ref[...], ob_ref[...])
        ka_ref[...] = _augment(kn, _split3(-c), eb_ref[...], oa_ref[...])
        va_ref[...] = _augment(v_ref[...].astype(BF16), None, ea_ref[...], oa_ref[...])

    emat = _const((768 + 3 * LANE, AUG))
    return pl.pallas_call(
        body, grid=(s // tr,),
        in_specs=[_win(tr, 768, QB), _win(tr, 768, KB), _win(tr, 768, VB), _rowblk(tr, LANE), _const((1, 768)),
                  _const((1, 768)), _const((1, LANE)), _const((768, 768)), _const((tr, tr)), emat, emat,
                  _const((1, AUG)), _const((1, AUG))],
        out_specs=[_rowblk(tr, AUG), _rowblk(tr, AUG), _rowblk(tr, AUG), _rowblk(tr, 768), _rowblk(tr, LANE)],
        out_shape=[SDS((s, AUG), BF16)] * 3 + [SDS((s, 768), BF16), SDS((s, LANE), F32)],
        scratch_shapes=[pltpu.VMEM((1, LANE), F32)], name="fox_prep",
        compiler_params=_params(("arbitrary",)))(proj, proj, proj, fbl, qg, kg, bfor, bd, tri, ea, eb, ones_a, ones_b)


def _fox2_fwd(proj, qa, ka, va):
    s = proj.shape[0]
    tq, tk = _fox_tiles(s)
    nq, nk = s // tq, s // tk

    def last_k(i):
        return (i * tq + tq - 1) // tk

    def body(q_ref, k_ref, v_ref, z_ref, gb_ref, yb_ref, lse_ref, acc, m_s):
        i, j = pl.program_id(0), pl.program_id(1)

        @pl.when(j == 0)
        def _():
            acc[...] = jnp.zeros_like(acc)
            m_s[...] = jnp.full_like(m_s, NEG)

        def tile(masked):
            if masked:
                qpos = i * tq + lax.broadcasted_iota(jnp.int32, (tq, tk), 0)
                kpos = j * tk + lax.broadcasted_iota(jnp.int32, (tq, tk), 1)
                mask = kpos <= qpos
            m_all = m_s[...]
            lane = lax.broadcasted_iota(jnp.int32, (tq, LANE), 1)
            m_out = m_all
            for h in range(B_HEADS):
                sl = slice(128 * h, 128 * h + 128)
                sc = _dot_nt(q_ref[:, sl], k_ref[:, sl])
                if masked:
                    sc = jnp.where(mask, sc, NEG)
                m_prev = m_all[:, h:h + 1]
                m_new = jnp.maximum(m_prev, jnp.max(sc, axis=-1, keepdims=True))
                p = jnp.exp(sc - m_new).astype(BF16)
                acc[:, sl] = jnp.exp(m_prev - m_new) * acc[:, sl] + _dot(p, v_ref[:, sl])
                m_out = jnp.where(lane == h, m_new, m_out)
            m_s[...] = m_out

        full = j * tk + tk - 1 <= i * tq

        @pl.when(full)
        def _():
            tile(False)

        @pl.when(jnp.logical_and(jnp.logical_not(full), j <= last_k(i)))
        def _():
            tile(True)

        @pl.when(j == nk - 1)
        def _():
            a = acc[...]
            l_all = _lane_of_heads(a, COL_A)
            y = jnp.concatenate([a[:, 128 * h:128 * h + 64] / a[:, 128 * h + COL_A:128 * h + COL_A + 1]
                                 for h in range(B_HEADS)], axis=1)
            yb_ref[...] = y
            gb_ref[...] = (y * _silu(z_ref[...])).astype(BF16)
            lane = lax.broadcasted_iota(jnp.int32, (tq, LANE), 1)
            lse_ref[...] = jnp.where(lane < B_HEADS, m_s[...] + jnp.log(jnp.where(lane < B_HEADS, l_all, 1.0)), 0.0)

    kmap = lambda i, j: (jnp.minimum(j, last_k(i)), 0)
    return pl.pallas_call(
        body, grid=(nq, nk),
        in_specs=[pl.BlockSpec((tq, AUG), lambda i, j: (i, 0)), pl.BlockSpec((tk, AUG), kmap), pl.BlockSpec((tk, AUG), kmap),
                  pl.BlockSpec((pl.Element(tq), pl.Element(768)), lambda i, j: (i * tq, ZB))],
        out_specs=[pl.BlockSpec((tq, 768), lambda i, j: (i, 0)), pl.BlockSpec((tq, 768), lambda i, j: (i, 0)),
                   pl.BlockSpec((tq, LANE), lambda i, j: (i, 0))],
        out_shape=[SDS((s, 768), BF16), SDS((s, 768), F32), SDS((s, LANE), F32)],
        scratch_shapes=[pltpu.VMEM((tq, AUG), F32), pltpu.VMEM((tq, LANE), F32)],
        name="fox_fwd", compiler_params=_params(("parallel", "arbitrary")))(qa, ka, va, proj)


def _fox2_bwd_pre(proj, yb, dgb, qn, c, lse, hsum, ea, ones_b, *, tr):
    s = proj.shape[0]

    def body(z_ref, y_ref, dg_ref, qn_ref, c_ref, lse_ref, hs_ref, ea_ref, ob_ref, qa_ref, dya_ref, dz_ref):
        z, y, dg = z_ref[...], y_ref[...], dg_ref[...]
        sg = jax.nn.sigmoid(z)
        dy = dg * (z * sg)
        dz_ref[...] = (dg * y * (sg * (1.0 + z * (1.0 - sg)))).astype(BF16)
        delta = jnp.dot(dy * y, hs_ref[...], precision=HI, preferred_element_type=F32)
        e = ea_ref[...]
        dya_ref[...] = _augment(dy.astype(BF16), _split3(-delta), e, None)
        qa_ref[...] = _augment(qn_ref[...], _split3(c_ref[...] - lse_ref[...]), e, ob_ref[...])

    return pl.pallas_call(
        body, grid=(s // tr,),
        in_specs=[_win(tr, 768, ZB), _rowblk(tr, 768), _rowblk(tr, 768), _rowblk(tr, 768), _rowblk(tr, LANE),
                  _rowblk(tr, LANE), _const((768, LANE)), _const((768 + 3 * LANE, AUG)), _const((1, AUG))],
        out_specs=[_rowblk(tr, AUG), _rowblk(tr, AUG), _rowblk(tr, 768)],
        out_shape=[SDS((s, AUG), BF16), SDS((s, AUG), BF16), SDS((s, 768), BF16)], name="fox_bwd_pre",
        compiler_params=_params(("parallel",)))(proj, yb, dgb, qn, c, lse, hsum, ea, ones_b)


def _fox2_bwd(qa, ka, va, dya):
    s = qa.shape[0]
    tq, tk = _fox_tiles(s)
    nq, nk = s // tq, s // tk

    def first_q(j):
        return (j * tk) // tq

    def body(q_ref, k_ref, v_ref, dy_ref, dq_hbm, dk_ref, dv_ref, dck_ref, dq_acc, dk_acc, dv_acc, sem):
        j, i = pl.program_id(0), pl.program_id(1)

        @pl.when((j == 0) & (i == 0))
        def _():
            dq_acc[...] = jnp.zeros_like(dq_acc)

        @pl.when(i == 0)
        def _():
            dk_acc[...] = jnp.zeros_like(dk_acc)
            dv_acc[...] = jnp.zeros_like(dv_acc)

        def tile(masked):
            if masked:
                qpos = i * tq + lax.broadcasted_iota(jnp.int32, (tq, tk), 0)
                kpos = j * tk + lax.broadcasted_iota(jnp.int32, (tq, tk), 1)
                mask = kpos <= qpos
            rows = pl.ds(pl.multiple_of(i * tq, tq), tq)
            for h in range(B_HEADS):
                sl = slice(128 * h, 128 * h + 128)
                q, k, dy = q_ref[:, sl], k_ref[:, sl], dy_ref[:, sl]
                sc = _dot_nt(q, k)
                if masked:
                    sc = jnp.where(mask, sc, NEG)
                p = jnp.exp(sc)
                ds = (p * _dot_nt(dy, v_ref[:, sl])).astype(BF16)
                dv_acc[:, sl] += _dot_tn(p.astype(BF16), dy)
                dk_acc[:, sl] += _dot_tn(ds, q)
                dq_acc[rows, sl] += _dot(ds, k)

        full = j * tk + tk - 1 <= i * tq

        @pl.when(full)
        def _():
            tile(False)

        @pl.when(jnp.logical_and(jnp.logical_not(full), i >= first_q(j)))
        def _():
            tile(True)

        @pl.when(i == nq - 1)
        def _():
            dkw = dk_acc[...]
            dk_ref[...] = _compact(dkw)
            dv_ref[...] = _compact(dv_acc[...]).astype(BF16)
            dck_ref[...] = -_lane_of_heads(dkw, COL_B)

        @pl.when((j == nk - 1) & (i == nq - 1))
        def _():
            cp = pltpu.make_async_copy(dq_acc, dq_hbm, sem)
            cp.start()
            cp.wait()

    qmap = lambda j, i: (jnp.maximum(i, first_q(j)), 0)
    kblk = lambda w: pl.BlockSpec((tk, w), lambda j, i: (j, 0))
    return pl.pallas_call(
        body, grid=(nk, nq),
        in_specs=[pl.BlockSpec((tq, AUG), qmap), kblk(AUG), kblk(AUG), pl.BlockSpec((tq, AUG), qmap)],
        out_specs=[pl.BlockSpec(memory_space=pl.ANY), kblk(768), kblk(768), kblk(LANE)],
        out_shape=[SDS((s, AUG), F32), SDS((s, 768), F32), SDS((s, 768), BF16), SDS((s, LANE), F32)],
        scratch_shapes=[pltpu.VMEM((s, AUG), F32), pltpu.VMEM((tk, AUG), F32), pltpu.VMEM((tk, AUG), F32),
                        pltpu.SemaphoreType.DMA],
        name="fox_bwd", compiler_params=_params(("arbitrary", "arbitrary")))(qa, ka, va, dya)


def _fox2_bwd_post(proj, fbl, qg, kg, bfor, bd, dqa, dkn, dck, *, tr):
    s = proj.shape[0]
    nb = s // tr
    triu = jnp.asarray(np.triu(np.ones((tr, tr), np.float32)))
    rev = lambda i: nb - 1 - i

    def body(q_ref, k_ref, fb_ref, qg_ref, kg_ref, bf_ref, bd_ref, tri_ref, dqa_ref, dkn_ref, dck_ref,
             dq_ref, dk_ref, dfb_ref, dqg_ref, dkg_ref, dbf_ref, carry):
        @pl.when(pl.program_id(0) == 0)
        def _():
            carry[...] = jnp.zeros_like(carry)
            dqg_ref[...] = jnp.zeros_like(dqg_ref)
            dkg_ref[...] = jnp.zeros_like(dkg_ref)
            dbf_ref[...] = jnp.zeros_like(dbf_ref)

        bd_v = bd_ref[...]
        dqw = dqa_ref[...]
        _, vjp_q = jax.vjp(lambda q, g: _qn_fn(q, g, bd_v), q_ref[...], qg_ref[...])
        dq, dqg = vjp_q(_compact(dqw))
        _, vjp_k = jax.vjp(lambda k, g: _kn_fn(k, g, bd_v), k_ref[...], kg_ref[...])
        dk, dkg = vjp_k(dkn_ref[...])
        dq_ref[...] = dq.astype(BF16)
        dk_ref[...] = dk.astype(BF16)
        dqg_ref[...] += dqg
        dkg_ref[...] += dkg

        dc = _lane_of_heads(dqw, COL_A) + dck_ref[...]
        dlogf = jnp.dot(tri_ref[...], dc, precision=HI, preferred_element_type=F32) + carry[...]
        carry[...] = dlogf[0:1, :]
        lane = lax.broadcasted_iota(jnp.int32, (tr, LANE), 1)
        xf = fb_ref[...] + bf_ref[...]
        dfb = jnp.where(lane < N_FORGET, dlogf * jax.nn.sigmoid(-xf), 0.0)
        dfb_ref[...] = dfb.astype(BF16)
        dbf_ref[...] += jnp.sum(dfb, axis=0, keepdims=True)

    rb = lambda w: pl.BlockSpec((tr, w), lambda i: (rev(i), 0))
    wn = lambda w, off: pl.BlockSpec((pl.Element(tr), pl.Element(w)), lambda i: (rev(i) * tr, off))
    return pl.pallas_call(
        body, grid=(nb,),
        in_specs=[wn(768, QB), wn(768, KB), rb(LANE), _const((1, 768)), _const((1, 768)), _const((1, LANE)),
                  _const((768, 768)), _const((tr, tr)), rb(AUG), rb(768), rb(LANE)],
        out_specs=[rb(768), rb(768), rb(LANE), _const((1, 768)), _const((1, 768)), _const((1, LANE))],
        out_shape=[SDS((s, 768), BF16), SDS((s, 768), BF16), SDS((s, LANE), BF16), SDS((1, 768), F32),
                   SDS((1, 768), F32), SDS((1, LANE), F32)],
        scratch_shapes=[pltpu.VMEM((1, LANE), F32)], name="fox_bwd_post",
        compiler_params=_params(("arbitrary",)))(proj, proj, fbl, qg, kg, bfor, bd, triu, dqa, dkn, dck)


def _merge_specs(tr):
    row = lambda w: pl.BlockSpec((tr, w), lambda i, j: (i, 0))
    shard = lambda r: pl.BlockSpec((None, r, 512), lambda i, j: (j, 0, 0))
    gate = lambda b: pl.BlockSpec((tr, 512), lambda i, j: (i, (GATE + 2048 * b) // 512 + j))
    return [row(768), row(768), row(512), shard(768), shard(768), shard(512), gate(0), gate(1), gate(2)]


def _merge_fwd(proj, ga, gb, gc, wa, wb, wc, *, tr):
    s = proj.shape[0]

    def body(ga_ref, gb_ref, gc_ref, wa_ref, wb_ref, wc_ref, l0_ref, l1_ref, l2_ref, y_ref):
        ua = _dot(ga_ref[...], wa_ref[...])
        ub = _dot(gb_ref[...], wb_ref[...])
        uc = _dot(gc_ref[...], wc_ref[...])
        y = jax.nn.sigmoid(l0_ref[...]) * ua + jax.nn.sigmoid(l1_ref[...]) * ub + jax.nn.sigmoid(l2_ref[...]) * uc
        y_ref[...] = y.astype(BF16)

    return pl.pallas_call(
        body, grid=(s // tr, N_CHIPS), in_specs=_merge_specs(tr),
        out_specs=pl.BlockSpec((tr, 512), lambda i, j: (i, j)), out_shape=SDS((s, D_MODEL), BF16), name="merge_fwd",
        compiler_params=_params(("parallel", "arbitrary")))(ga, gb, gc, wa, wb, wc, proj, proj, proj)


def _merge_bwd(proj, ga, gb, gc, wa, wb, wc, dy, *, tr):
    s = proj.shape[0]

    def body(ga_ref, gb_ref, gc_ref, wa_ref, wb_ref, wc_ref, l0_ref, l1_ref, l2_ref, dy_ref,
             dl0_ref, dl1_ref, dl2_ref, dua_ref, dub_ref, duc_ref, dga_ref, dgb_ref, dgc_ref):
        j = pl.program_id(1)
        dyv = dy_ref[...]

        @pl.when(j == 0)
        def _():
            dga_ref[...] = jnp.zeros_like(dga_ref)
            dgb_ref[...] = jnp.zeros_like(dgb_ref)
            dgc_ref[...] = jnp.zeros_like(dgc_ref)

        for g_ref, w_ref, l_ref, dl_ref, du_ref, dg_ref in (
                (ga_ref, wa_ref, l0_ref, dl0_ref, dua_ref, dga_ref),
                (gb_ref, wb_ref, l1_ref, dl1_ref, dub_ref, dgb_ref),
                (gc_ref, wc_ref, l2_ref, dl2_ref, duc_ref, dgc_ref)):
            w = w_ref[...]
            u = _dot(g_ref[...], w)
            sg = jax.nn.sigmoid(l_ref[...])
            dl_ref[...] = (dyv * u * sg * (1.0 - sg)).astype(BF16)
            du = (dyv * sg).astype(BF16)
            du_ref[...] = du
            dg_ref[...] += _dot_nt(du, w)

    blk = pl.BlockSpec((tr, 512), lambda i, j: (i, j))
    row = lambda w: pl.BlockSpec((tr, w), lambda i, j: (i, 0))
    big = SDS((s, D_MODEL), BF16)
    return pl.pallas_call(
        body, grid=(s // tr, N_CHIPS), in_specs=_merge_specs(tr) + [blk],
        out_specs=[blk] * 6 + [row(768), row(768), row(512)],
        out_shape=[big] * 6 + [SDS((s, 768), F32), SDS((s, 768), F32), SDS((s, 512), F32)], name="merge_bwd",
        compiler_params=_params(("parallel", "arbitrary")))(ga, gb, gc, wa, wb, wc, proj, proj, proj, dy)


def _out_loss(y, wo, x, tgt, *, tr, tn):
    s = x.shape[0]

    def body(y_ref, w_ref, x_ref, t_ref, d_ref, db_ref, sq_ref):
        @pl.when((pl.program_id(0) == 0) & (pl.program_id(1) == 0))
        def _():
            sq_ref[...] = jnp.zeros_like(sq_ref)

        out = x_ref[...] + _dot(y_ref[...], w_ref[...])
        diff = out - t_ref[...]
        sq_ref[...] += jnp.sum(diff * diff, axis=0, keepdims=True)
        d = diff * (1.0 / D_MODEL)
        d_ref[...] = d
        db_ref[...] = d.astype(BF16)

    blk = pl.BlockSpec((tr, tn), lambda i, j: (i, j))
    return pl.pallas_call(
        body, grid=(s // tr, D_MODEL // tn),
        in_specs=[pl.BlockSpec((tr, D_MODEL), lambda i, j: (i, 0)), pl.BlockSpec((D_MODEL, tn), lambda i, j: (0, j)), blk, blk],
        out_specs=[blk, blk, _const((1, tn))],
        out_shape=[SDS((s, D_MODEL), F32), SDS((s, D_MODEL), BF16), SDS((1, tn), F32)], name="out_loss",
        compiler_params=_params(("arbitrary", "arbitrary")))(y, wo, x, tgt)


def _tile_gain(g, reps):
    return jnp.tile(g.reshape(1, -1), (1, reps))


def _pad_lane(v):
    v = v.reshape(1, -1)
    return jnp.pad(v, ((0, 0), (0, LANE - v.shape[1])))


def _local_step(x, mem, tgt, w_main, w_fb, w_small, norm_gain, mem_norm_gain, b_forget,
                q_gain_a, k_gain_a, sinks_a, q_gain_b, k_gain_b, q_gain_c, k_gain_c, core=None):
    s = x.shape[0]
    tr = min(512, s)
    bd64 = _block_diag(768, HEAD_DIM)
    bd128 = _block_diag(512, C_HEAD_DIM)
    hsum = _head_sum(768, HEAD_DIM)
    qga, kga = _tile_gain(q_gain_a, 12), _tile_gain(k_gain_a, 4)
    qgb, kgb = _tile_gain(q_gain_b, 12), _tile_gain(k_gain_b, 12)
    qgc, kgc = _tile_gain(q_gain_c, 4), _tile_gain(k_gain_c, 4)
    sinks = _pad_lane(sinks_a)
    bfor = _pad_lane(b_forget)

    hn = _rms_fwd(x, norm_gain, tr=tr, name="rms_x")
    on_mesh = core is not None
    if on_mesh:
        proj, (gathered,) = _matmul(hn, w_main, dims="nn", out_dtype=F32, tm=1024, tn=512, tk=D_MODEL, name="proj_main",
                                    comms=[_gather_comm(list(w_small))])
        w_mk, wa, wb, wc, wo = gathered
        w_mk, wo = w_mk.reshape(D_MODEL, 1024), wo.reshape(D_MODEL, D_MODEL)
    else:
        proj = _matmul(hn, w_main, dims="nn", out_dtype=F32, tm=1024, tn=512, tk=D_MODEL, name="proj_main")
        w_mk, wa, wb, wc, wo = w_small
    fbl = _matmul(hn, w_fb, dims="nn", out_dtype=F32, tm=1024, tn=LANE, tk=D_MODEL, name="proj_forget")
    memn = _rms_fwd(mem, mem_norm_gain, tr=mem.shape[0], name="rms_mem")
    mkv = _matmul(memn, w_mk, dims="nn", out_dtype=F32, tm=256, tn=512, tk=D_MODEL, name="mem_kv")

    swa_bias = _swa_bias()
    ga = _swa_fwd(proj, qga, kga, sinks, bd64, swa_bias)
    ea, eb, ones_a, ones_b = _expand_mats()
    tf = min(256, s)
    qa, ka, va, qn, cfox = _fox2_prep(proj, fbl, qgb, kgb, bfor, bd64, ea, eb, ones_a, ones_b, tr=tf)
    gb, yb, lse = _fox2_fwd(proj, qa, ka, va)
    gc = _mem_fwd(proj, mkv, qgc, kgc, bd128, tr=tr)
    y = _merge_fwd(proj, ga, gb, gc, wa, wb, wc, tr=tr)
    dout, dout_b, sq = _out_loss(y, wo, x, tgt, tr=tr, tn=512)

    d_wo = _matmul(y, dout_b, dims="tn", out_dtype=F32, tm=1024, tn=512, tk=2048, name="dw_out")
    dy = _matmul(dout_b, wo, dims="nt", out_dtype=F32, tm=1024, tn=512, tk=D_MODEL, name="dy")
    dl0, dl1, dl2, dua, dub, duc, dga, dgb, dgc = _merge_bwd(proj, ga, gb, gc, wa, wb, wc, dy, tr=tr)
    d_wa = _matmul(ga, dua, dims="tn", out_dtype=F32, tm=768, tn=512, tk=2048, name="dw_branch_a")
    d_wb = _matmul(gb, dub, dims="tn", out_dtype=F32, tm=768, tn=512, tk=2048, name="dw_branch_b")
    d_wc = _matmul(gc, duc, dims="tn", out_dtype=F32, tm=512, tn=512, tk=2048, name="dw_branch_c")

    dcur, dprev, d_qga, d_kga, d_sinks = _swa_bwd(proj, qga, kga, sinks, bd64, swa_bias, dga)
    dproj_a = _swa_combine(dcur, dprev)

    qab, dya, dzb = _fox2_bwd_pre(proj, yb, dgb, qn, cfox, lse, hsum, ea, ones_b, tr=tf)
    dqa, dkn, dvb, dck = _fox2_bwd(qab, ka, va, dya)
    dqb, dkb, dfb, d_qgb, d_kgb, d_bf = _fox2_bwd_post(proj, fbl, qgb, kgb, bfor, bd64, dqa, dkn, dck, tr=tf)

    dproj_c, dmkv, d_qgc, d_kgc = _mem_bwd(proj, mkv, qgc, kgc, bd128, dgc, tr=tr)
    dmkv_b = dmkv.astype(BF16)
    d_wmk = _matmul(memn, dmkv_b, dims="tn", out_dtype=F32, tm=1024, tn=512, tk=256, name="dw_mem_kv")
    dmemn = _matmul(dmkv_b, w_mk, dims="nt", out_dtype=F32, tm=256, tn=512, tk=1024, name="dmemn")
    (d_mem_gain,) = _rms_bwd(mem, mem_norm_gain, dmemn, None, tr=mem.shape[0], name="rms_mem_bwd")

    dproj = jnp.concatenate([dproj_a, dqb, dkb, dvb, dzb, dproj_c, dl0, dl1, dl2], axis=1)
    dhn_f = _matmul(dfb, w_fb, dims="nt", out_dtype=F32, tm=1024, tn=512, tk=LANE, name="dhn_forget")
    d_wfb = _matmul(hn, dfb, dims="tn", out_dtype=F32, tm=1024, tn=LANE, tk=512, name="dw_forget")
    big = {}
    if on_mesh:
        half = D_MODEL // 2
        c0 = core[0]
        hn_other = lax.dynamic_slice(hn, (0, (1 - c0) * half), (s, half))
        hn_own = lax.dynamic_slice(hn, (0, c0 * half), (s, half))
        g1, k1 = [d_wmk, d_wa, d_wb, d_wc, d_wo], [2, 3, 4, 5, 6]
        d_other, (got1,) = _matmul(hn_other, dproj, dims="tn", out_dtype=F32, tm=1024, tn=512, tk=2048,
                                   name="dw_main_other", comms=[_exchange_comm(g1, k1)])
        h1 = [_add_half(g, got, core, HALF_AXIS[k], name=f"add_half_{k}") for g, got, k in zip(g1, got1, k1)]
        d_own, (got0, parts1) = _matmul(
            hn_own, dproj, dims="tn", out_dtype=F32, tm=1024, tn=512, tk=2048, name="dw_main_own",
            comms=[_exchange_comm([d_other, d_wfb], [0, 1], whole=(0,)), _scatter_comm(h1, k1)])
        h0 = [_add_pair(d_own, got0[0], name="add_pair_main"), _add_half(d_wfb, got0[1], core, 0, name="add_half_1")]
        sums1 = [_sum4(p, name=f"sum4_{k}") for p, k in zip(parts1, k1)]
        dhn, (parts0, theirs1) = _matmul(dproj, w_main, dims="nt", out_dtype=F32, tm=1024, tn=512, tk=2048, name="dhn",
                                         add=dhn_f, comms=[_scatter_comm(h0, [0, 1]), _swap_comm(sums1)])
        sums0 = [_sum4(p, name=f"sum4_{k}") for p, k in zip(parts0, (0, 1))]
        theirs0 = _run_comm(_swap_comm(sums0), "swap_halves")
        big = dict(sums=sums0 + sums1, theirs=list(theirs0) + list(theirs1))
    else:
        dhn = _matmul(dproj, w_main, dims="nt", out_dtype=F32, tm=1024, tn=512, tk=2048, name="dhn", add=dhn_f)
        d_wmain = _matmul(hn, dproj, dims="tn", out_dtype=F32, tm=1024, tn=512, tk=2048, name="dw_main")
        big = dict(d_wmain=d_wmain, d_wfb=d_wfb, d_wmk=d_wmk, d_wa=d_wa, d_wb=d_wb, d_wc=d_wc, d_wo=d_wo)
    grad_x, d_gain = _rms_bwd(x, norm_gain, dhn, dout, tr=tr, name="rms_x_bwd")

    fold = lambda g, reps: jnp.sum(g.reshape(reps, -1), axis=0, keepdims=True)
    return dict(
        sq=sq, grad_x=grad_x, **big,
        d_gain=d_gain, d_mem_gain=d_mem_gain, d_bf=d_bf[:, :N_FORGET],
        d_qga=fold(d_qga, 12), d_kga=fold(d_kga, 4), d_sinks=d_sinks[:, :A_HEADS],
        d_qgb=fold(d_qgb, 12), d_kgb=fold(d_kgb, 12), d_qgc=fold(d_qgc, 4), d_kgc=fold(d_kgc, 4))


PACK_ROWS = 256
FORGET_IN_SHARD = FORGET_COL - SHARD_COLS
AFTER_FORGET = FORGET_COL - SLAB_START[1]
END_CHIP1 = 2 * SHARD_COLS - N_FORGET - SLAB_START[1]


def _pack_w_in(chip, w):
    rows = w.shape[0]
    tr = PACK_ROWS

    def body(k_ref, w_ref, o_ref, scr):
        scr[...] = jnp.zeros_like(scr)
        scr[:, pl.ds(0, SHARD_COLS)] = w_ref[...]
        v = scr[...]
        k = k_ref[0]
        col = lax.broadcasted_iota(jnp.int32, (tr, SLAB), 1)
        no_forget = jnp.zeros((tr, LANE), BF16)

        @pl.when(k == 0)
        def _():
            o_ref[:, 0:SLAB] = v.astype(BF16)
            o_ref[:, SLAB:] = no_forget

        @pl.when(k == 1)
        def _():
            before = pltpu.roll(v, SLAB_SHIFT[1], axis=1)
            after = pltpu.roll(v, SLAB - (N_FORGET - SLAB_SHIFT[1]), axis=1)
            slab = jnp.where(col < AFTER_FORGET, before, jnp.where(col < END_CHIP1, after, 0.0))
            o_ref[:, 0:SLAB] = slab.astype(BF16)
            f = pltpu.roll(v, SLAB - FORGET_IN_SHARD, axis=1)[:, :LANE]
            o_ref[:, SLAB:] = jnp.where(col[:, :LANE] < N_FORGET, f, 0.0).astype(BF16)

        for kk in (2, 3):
            @pl.when(k == kk)
            def _(kk=kk):
                o_ref[:, 0:SLAB] = pltpu.roll(v, SLAB_SHIFT[kk], axis=1).astype(BF16)
                o_ref[:, SLAB:] = no_forget

    return pl.pallas_call(
        body, grid_spec=pltpu.PrefetchScalarGridSpec(
            num_scalar_prefetch=1, grid=(rows // tr,),
            in_specs=[pl.BlockSpec((tr, SHARD_COLS), lambda i, k: (i, 0))],
            out_specs=pl.BlockSpec((None, tr, SLAB + LANE), lambda i, k: (k[0], i, 0)),
            scratch_shapes=[pltpu.VMEM((tr, SLAB), F32)]),
        out_shape=SDS((N_CHIPS, rows, SLAB + LANE), BF16), name="pack_w_in",
        compiler_params=_params(("arbitrary",)))(chip, w)


def _merge_slabs(g):
    rows = g.shape[1]
    tr = PACK_ROWS
    t = [s // LANE for s in SLAB_START]
    n_t = SLAB // LANE

    def body(g_ref, m_ref, f_ref):
        for k in range(N_CHIPS):
            lo = t[k] + (1 if k > 0 else 0)
            hi = t[k + 1] if k + 1 < N_CHIPS else t[k] + n_t
            m_ref[:, lo * LANE:hi * LANE] = g_ref[k, :, (lo - t[k]) * LANE:(hi - t[k]) * LANE]
            if k + 1 < N_CHIPS:
                a = g_ref[k, :, (hi - t[k]) * LANE:(hi - t[k] + 1) * LANE].astype(F32)
                b = g_ref[k + 1, :, 0:LANE].astype(F32)
                m_ref[:, hi * LANE:(hi + 1) * LANE] = (a + b).astype(BF16)
        f_ref[...] = g_ref[1, :, SLAB:]

    return pl.pallas_call(
        body, grid=(rows // tr,),
        in_specs=[pl.BlockSpec((N_CHIPS, tr, SLAB + LANE), lambda i: (0, i, 0))],
        out_specs=[_rowblk(tr, P_MAIN), _rowblk(tr, LANE)],
        out_shape=[SDS((rows, P_MAIN), BF16), SDS((rows, LANE), BF16)], name="merge_slabs",
        compiler_params=_params(("parallel",)))(g)


def _adamw_math(w, g, m, v):
    nm = ADAM_B1 * m + (1.0 - ADAM_B1) * g
    nv = ADAM_B2 * v + (1.0 - ADAM_B2) * (g * g)
    m_hat = nm / (1.0 - ADAM_B1 ** ADAM_STEP)
    v_hat = nv / (1.0 - ADAM_B2 ** ADAM_STEP)
    delta = -ADAM_LR * (m_hat / (jnp.sqrt(v_hat) + ADAM_EPS) + ADAM_WD * w)
    return delta, nm, nv


def _adamw(g, w, m, v, *, tr, name):
    rows, cols = w.shape
    tr = min(tr, rows)

    def body(g_ref, w_ref, m_ref, v_ref, d_ref, nm_ref, nv_ref):
        d, nm, nv = _adamw_math(w_ref[...], g_ref[...], m_ref[...], v_ref[...])
        d_ref[...] = d
        nm_ref[...] = nm
        nv_ref[...] = nv

    spec = _rowblk(tr, cols)
    return pl.pallas_call(
        body, grid=(rows // tr,), in_specs=[spec] * 4, out_specs=[spec] * 3,
        out_shape=[SDS((rows, cols), F32)] * 3, name=name, compiler_params=_params(("parallel",)))(g, w, m, v)


def _adamw_w_in(chip_core, slab_mine, slab_theirs, forget_mine, forget_theirs, w, m, v):
    rows = w.shape[0]
    tr = PACK_ROWS // 2
    nbh = rows // 2 // tr

    def body(k_ref, sa_ref, sb_ref, fa_ref, fb_ref, w_ref, m_ref, v_ref, g_ref, d_ref, nm_ref, nv_ref):
        use_mine = pl.program_id(0) // nbh == k_ref[1]
        sl = jnp.where(use_mine, sa_ref[...], sb_ref[...])
        f_tile = jnp.where(use_mine, fa_ref[...], fb_ref[...])
        k = k_ref[0]

        def emit(wide):
            g = wide[:, :SHARD_COLS]
            g_ref[...] = g
            d, nm, nv = _adamw_math(w_ref[...], g, m_ref[...], v_ref[...])
            d_ref[...] = d
            nm_ref[...] = nm
            nv_ref[...] = nv

        @pl.when(k == 0)
        def _():
            emit(sl)

        @pl.when(k == 1)
        def _():
            col = lax.broadcasted_iota(jnp.int32, (tr, SLAB), 1)
            before = pltpu.roll(sl, SLAB - SLAB_SHIFT[1], axis=1)
            after = pltpu.roll(sl, N_FORGET - SLAB_SHIFT[1], axis=1)
            wide_f = jnp.concatenate([f_tile, jnp.zeros((tr, SLAB - LANE), F32)], axis=1)
            forget = pltpu.roll(wide_f, FORGET_IN_SHARD, axis=1)
            emit(jnp.where(col < FORGET_IN_SHARD, before, jnp.where(col < FORGET_IN_SHARD + N_FORGET, forget, after)))

        for kk in (2, 3):
            @pl.when(k == kk)
            def _(kk=kk):
                emit(pltpu.roll(sl, SLAB - SLAB_SHIFT[kk], axis=1))

    nat = pl.BlockSpec((tr, SHARD_COLS), lambda i, k: (i, 0))
    half = lambda width: pl.BlockSpec((tr, width), lambda i, k: (i % nbh, 0))
    return pl.pallas_call(
        body, grid_spec=pltpu.PrefetchScalarGridSpec(
            num_scalar_prefetch=1, grid=(rows // tr,),
            in_specs=[half(SLAB), half(SLAB), half(LANE), half(LANE), nat, nat, nat],
            out_specs=[nat] * 4),
        out_shape=[SDS((rows, SHARD_COLS), F32)] * 4, name="adamw_w_in",
        compiler_params=_params(("arbitrary",)))(chip_core, slab_mine, slab_theirs, forget_mine, forget_theirs, w, m, v)


ANY = pl.BlockSpec(memory_space=pl.ANY)
HALF_AXIS = (0, 0, 1, 0, 0, 0, 1)


def _me():
    return lax.axis_index("x"), lax.axis_index("y"), lax.axis_index("c")


def _half(ref, which, axis):
    n = ref.shape[axis] // 2
    sl = pl.ds(which * n, n)
    return ref.at[sl] if axis == 0 else ref.at[:, sl]


def _piece(t, ref, j):
    if t == 0:
        return ref.at[:, pl.ds(SLAB_START[j], SLAB)]
    if t == 1:
        return ref
    if t in (2, 6):
        return ref.at[pl.ds(512 * j, 512)]
    return ref.at[:, pl.ds(512 * j, 512)]


def _piece_shape(t, shape):
    if t == 0:
        return (shape[0], SLAB)
    if t == 1:
        return shape
    if t in (2, 6):
        return (512, shape[1])
    return (shape[0], 512)


def _gather_plan(ins, outs, own_slot_in_src):
    x, y, c = _me()
    k = 2 * x + y
    sib = (x, y, 1 - c)
    chips = [(1 - x, y), (x, 1 - y), (1 - x, 1 - y)]
    n = len(outs)

    def rows(t, which):
        h = outs[t].shape[1] // 2
        return pl.ds(which * h, h)

    def mine(t):
        return ins[t].at[k, rows(t, c)] if own_slot_in_src else ins[t].at[rows(t, c)]

    def first(t, j, sems):
        chip = chips[j]
        return pltpu.make_async_remote_copy(
            src_ref=mine(t), dst_ref=outs[t].at[k, rows(t, c)], send_sem=sems[0].at[t, j], recv_sem=sems[1].at[t, j],
            device_id=(chip[0], chip[1], c), device_id_type=MESH)

    def landed(t, j, sems):
        chip = chips[j]
        return pltpu.make_async_remote_copy(
            src_ref=mine(t), dst_ref=outs[t].at[2 * chip[0] + chip[1], rows(t, c)], send_sem=sems[0].at[t, j],
            recv_sem=sems[1].at[t, j], device_id=(chip[0], chip[1], c), device_id_type=MESH)

    def passed(t, j, which, sems):
        chip = chips[j]
        blk = outs[t].at[2 * chip[0] + chip[1], rows(t, which)]
        return pltpu.make_async_remote_copy(
            src_ref=blk, dst_ref=blk, send_sem=sems[2].at[t, j], recv_sem=sems[3].at[t, j], device_id=sib,
            device_id_type=MESH)

    def start(sems):
        for j in range(3):
            for t in range(n):
                first(t, j, sems).start()

    def finish(sems):
        for j in range(3):
            for t in range(n):
                landed(t, j, sems).wait_recv()
                passed(t, j, c, sems).start()
        for j in range(3):
            for t in range(n):
                passed(t, j, 1 - c, sems).wait_recv()
        for j in range(3):
            for t in range(n):
                first(t, j, sems).wait_send()
                passed(t, j, c, sems).wait_send()

    return k, start, finish


def _all_gather_slabs(slabs):
    def body(in_ref, out_ref, *sems):
        _, start, finish = _gather_plan([in_ref], [out_ref], True)
        start(sems)
        finish(sems)

    return pl.pallas_call(
        body, in_specs=[ANY], out_specs=ANY, out_shape=SDS(slabs.shape, slabs.dtype),
        scratch_shapes=[pltpu.SemaphoreType.DMA((1, 3))] * 4, input_output_aliases={0: 0},
        name="all_gather_slabs")(slabs)


def _gather_comm(parts):
    n = len(parts)

    def start(ins, outs, sems):
        k, go, _ = _gather_plan(ins, outs, False)
        for t in range(n):
            pltpu.make_async_copy(ins[t], outs[t].at[k], sems[4].at[t]).start()
        go(sems)

    def finish(ins, outs, sems):
        k, _, done = _gather_plan(ins, outs, False)
        done(sems)
        for t in range(n):
            pltpu.make_async_copy(ins[t], outs[t].at[k], sems[4].at[t]).wait()

    return _Comm(parts, [SDS((N_CHIPS,) + p.shape, p.dtype) for p in parts],
                 [pltpu.SemaphoreType.DMA((n, 3))] * 4 + [pltpu.SemaphoreType.DMA((n,))], start, finish)


def _exchange_comm(arrs, kinds, whole=()):
    n = len(arrs)

    def copies(ins, outs, sems):
        x, y, c = _me()
        return [pltpu.make_async_remote_copy(
            src_ref=ins[t] if t in whole else _half(ins[t], 1 - c, HALF_AXIS[kinds[t]]), dst_ref=outs[t],
            send_sem=sems[0].at[t], recv_sem=sems[1].at[t], device_id=(x, y, 1 - c), device_id_type=MESH)
            for t in range(n)]

    def start(ins, outs, sems):
        for cp in copies(ins, outs, sems):
            cp.start()

    def finish(ins, outs, sems):
        for cp in copies(ins, outs, sems):
            cp.wait()

    def hshape(t):
        s = list(arrs[t].shape)
        if t not in whole:
            s[HALF_AXIS[kinds[t]]] //= 2
        return SDS(tuple(s), arrs[t].dtype)

    return _Comm(arrs, [hshape(t) for t in range(n)], [pltpu.SemaphoreType.DMA((n,))] * 2, start, finish)


def _add_half(full, got, core, axis, *, name):
    r, c = got.shape
    br, bc = (256 if r % 256 == 0 else 128), min(2048, c)
    off_r = (r // br) if axis == 0 else 0
    off_c = (c // bc) if axis == 1 else 0

    def body(c_ref, a_ref, b_ref, o_ref):
        o_ref[...] = (a_ref[...] + b_ref[...]).astype(BF16)

    return pl.pallas_call(
        body, grid_spec=pltpu.PrefetchScalarGridSpec(
            num_scalar_prefetch=1, grid=(r // br, c // bc),
            in_specs=[pl.BlockSpec((br, bc), lambda i, j, cr: (i + cr[0] * off_r, j + cr[0] * off_c)),
                      pl.BlockSpec((br, bc), lambda i, j, cr: (i, j))],
            out_specs=pl.BlockSpec((br, bc), lambda i, j, cr: (i, j))),
        out_shape=SDS((r, c), BF16), name=name, compiler_params=_params(("parallel", "parallel")))(core, full, got)


def _add_pair(a, b, *, name):
    r, c = a.shape
    br, bc = 256, min(2048, c)

    def body(a_ref, b_ref, o_ref):
        o_ref[...] = (a_ref[...] + b_ref[...]).astype(BF16)

    spec = pl.BlockSpec((br, bc), lambda i, j: (i, j))
    return pl.pallas_call(body, grid=(r // br, c // bc), in_specs=[spec, spec], out_specs=spec,
                          out_shape=SDS((r, c), BF16), name=name, compiler_params=_params(("parallel", "parallel")))(a, b)


def _scatter_comm(halves, kinds):
    n = len(halves)

    def plan(ins, outs, sems):
        send, recv, lsem = sems
        x, y, c = _me()
        k = 2 * x + y

        def to_chip(t, j):
            return pltpu.make_async_remote_copy(
                src_ref=_piece(kinds[t], ins[t], j), dst_ref=outs[t].at[k], send_sem=send.at[t, j],
                recv_sem=recv.at[t, k], device_id=(j // 2, j % 2, c), device_id_type=MESH)

        def from_chip(t, j):
            return pltpu.make_async_remote_copy(
                src_ref=_piece(kinds[t], ins[t], j), dst_ref=outs[t].at[j], send_sem=send.at[t, j],
                recv_sem=recv.at[t, j], device_id=(j // 2, j % 2, c), device_id_type=MESH)

        def own(t, j):
            return pltpu.make_async_copy(_piece(kinds[t], ins[t], j), outs[t].at[j], lsem.at[t])

        return k, to_chip, from_chip, own

    def start(ins, outs, sems):
        k, to_chip, _, own = plan(ins, outs, sems)
        for j in range(N_CHIPS):
            @pl.when(k != j)
            def _(j=j):
                for t in range(n):
                    to_chip(t, j).start()

            @pl.when(k == j)
            def _(j=j):
                for t in range(n):
                    own(t, j).start()

    def finish(ins, outs, sems):
        k, to_chip, from_chip, own = plan(ins, outs, sems)
        for j in range(N_CHIPS):
            @pl.when(k != j)
            def _(j=j):
                for t in range(n):
                    from_chip(t, j).wait_recv()
                for t in range(n):
                    to_chip(t, j).wait_send()

            @pl.when(k == j)
            def _(j=j):
                for t in range(n):
                    own(t, j).wait()

    return _Comm(halves, [SDS((N_CHIPS,) + _piece_shape(kinds[t], halves[t].shape), halves[t].dtype) for t in range(n)],
                 [pltpu.SemaphoreType.DMA((n, N_CHIPS))] * 2 + [pltpu.SemaphoreType.DMA((n,))], start, finish)


def _sum4(p, *, name):
    _, r, c = p.shape
    br = 256 if r % 256 == 0 else 128

    def body(p_ref, o_ref):
        o_ref[...] = ((p_ref[0].astype(F32) + p_ref[1].astype(F32)) + p_ref[2].astype(F32)) + p_ref[3].astype(F32)

    return pl.pallas_call(
        body, grid=(r // br,), in_specs=[pl.BlockSpec((N_CHIPS, br, c), lambda i: (0, i, 0))],
        out_specs=_rowblk(br, c), out_shape=SDS((r, c), F32), name=name, compiler_params=_params(("parallel",)))(p)


def _swap_comm(sums):
    return _exchange_comm(sums, [None] * len(sums), whole=tuple(range(len(sums))))


def _adamw_halves(mine, theirs, core, w, m, v, *, axis, tr, name):
    rows, cols = w.shape

    if axis == 0:
        nbh = rows // 2 // tr
        g_spec = pl.BlockSpec((tr, cols), lambda i, cr: (i % nbh, 0))
    else:
        g_spec = pl.BlockSpec((tr, cols // 2), lambda i, cr: (i, 0))

    def body(c_ref, a_ref, b_ref, w_ref, m_ref, v_ref, g_ref, d_ref, nm_ref, nv_ref):
        a, b = a_ref[...], b_ref[...]
        if axis == 0:
            g = jnp.where(pl.program_id(0) // nbh == c_ref[0], a, b)
        else:
            low = c_ref[0] == 0
            g = jnp.concatenate([jnp.where(low, a, b), jnp.where(low, b, a)], axis=1)
        g_ref[...] = g
        d, nm, nv = _adamw_math(w_ref[...], g, m_ref[...], v_ref[...])
        d_ref[...] = d
        nm_ref[...] = nm
        nv_ref[...] = nv

    nat = pl.BlockSpec((tr, cols), lambda i, cr: (i, 0))
    return pl.pallas_call(
        body, grid_spec=pltpu.PrefetchScalarGridSpec(
            num_scalar_prefetch=1, grid=(rows // tr,), in_specs=[g_spec, g_spec, nat, nat, nat], out_specs=[nat] * 4),
        out_shape=[SDS((rows, cols), F32)] * 4, name=name, compiler_params=_params(("arbitrary",)))(
            core, mine, theirs, w, m, v)


SMALL_ROWS, SMALL_COLS = 8, 1024


def _pack_small(vs):
    flat = jnp.concatenate([v.reshape(-1) for v in vs])
    return jnp.pad(flat, (0, SMALL_ROWS * SMALL_COLS - flat.shape[0])).reshape(SMALL_ROWS, SMALL_COLS)


def _unpack_small(packed, sizes):
    flat = packed.reshape(-1)
    out, o = [], 0
    for n in sizes:
        out.append(flat[o:o + n].reshape(1, n))
        o += n
    return out


def _all_reduce_small(v):
    n_dev = 8

    def body(v_ref, o_ref, land, send, recv):
        x, y, c = _me()
        me = 4 * x + 2 * y + c
        land[me] = v_ref[...]
        cps = []
        for r in range(1, n_dev):
            fx, fy, fc = (r >> 2) & 1, (r >> 1) & 1, r & 1
            peer = (x ^ fx, y ^ fy, c ^ fc)
            cps.append(pltpu.make_async_remote_copy(
                src_ref=v_ref, dst_ref=land.at[me], send_sem=send.at[r - 1], recv_sem=recv.at[r - 1],
                device_id=peer, device_id_type=MESH))
        for cp in cps:
            cp.start()
        for r in range(1, n_dev):
            fx, fy, fc = (r >> 2) & 1, (r >> 1) & 1, r & 1
            src = 4 * (x ^ fx) + 2 * (y ^ fy) + (c ^ fc)
            pltpu.make_async_remote_copy(
                src_ref=v_ref, dst_ref=land.at[src], send_sem=send.at[r - 1], recv_sem=recv.at[r - 1],
                device_id=(x ^ fx, y ^ fy, c ^ fc), device_id_type=MESH).wait_recv()
        for cp in cps:
            cp.wait_send()
        acc = land[0]
        for r in range(1, n_dev):
            acc = acc + land[r]
        o_ref[...] = acc

    vm = pl.BlockSpec(memory_space=pltpu.VMEM)
    return pl.pallas_call(
        body, in_specs=[vm], out_specs=vm, out_shape=SDS(v.shape, F32),
        scratch_shapes=[pltpu.VMEM((n_dev,) + v.shape, F32), pltpu.SemaphoreType.DMA((n_dev - 1,)),
                        pltpu.SemaphoreType.DMA((n_dev - 1,))],
        name="all_reduce_small")(v)


def kernel(x, mem, norm_gain, mem_norm_gain, w_in, b_forget, q_gain_a, k_gain_a, sinks_a, q_gain_b, k_gain_b, q_gain_c, k_gain_c, w_mem_kv, w_branch_a, w_branch_b, w_branch_c, w_out, loss_target, m_norm_gain, m_mem_norm_gain, m_w_in, m_b_forget, m_q_gain_a, m_k_gain_a, m_sinks_a, m_q_gain_b, m_k_gain_b, m_q_gain_c, m_k_gain_c, m_w_mem_kv, m_w_branch_a, m_w_branch_b, m_w_branch_c, m_w_out, v_norm_gain, v_mem_norm_gain, v_w_in, v_b_forget, v_q_gain_a, v_k_gain_a, v_sinks_a, v_q_gain_b, v_k_gain_b, v_q_gain_c, v_k_gain_c, v_w_mem_kv, v_w_branch_a, v_w_branch_b, v_w_branch_c, v_w_out):
    xi, yi, ci = lax.axis_index("x"), lax.axis_index("y"), lax.axis_index("c")
    chip = jnp.reshape(2 * xi + yi, (1,)).astype(jnp.int32)
    core = jnp.reshape(ci, (1,)).astype(jnp.int32)

    slabs = _pack_w_in(chip, w_in[0])
    mine = [w_mem_kv[0].astype(BF16), w_branch_a[0].astype(BF16), w_branch_b[0].astype(BF16),
            w_branch_c[0].astype(BF16), w_out[0].astype(BF16)]
    w_main, w_fb = _merge_slabs(_all_gather_slabs(slabs))

    r = _local_step(x[0], mem[0], loss_target[0], w_main, w_fb, mine, norm_gain, mem_norm_gain,
                    b_forget, q_gain_a, k_gain_a, sinks_a, q_gain_b, k_gain_b, q_gain_c, k_gain_c, core=core)
    sums, theirs = r["sums"], r["theirs"]

    small_names = ["d_gain", "d_mem_gain", "d_bf", "d_qga", "d_kga", "d_sinks", "d_qgb", "d_kgb", "d_qgc", "d_kgc"]
    loss_part = (0.5 / D_MODEL) * jnp.sum(r["sq"], axis=1, keepdims=True)
    packed = _pack_small([r[n] for n in small_names] + [loss_part])
    red = _all_reduce_small(packed)
    small_w = [norm_gain, mem_norm_gain, b_forget, q_gain_a, k_gain_a, sinks_a, q_gain_b, k_gain_b, q_gain_c, k_gain_c]
    small_m = [m_norm_gain, m_mem_norm_gain, m_b_forget, m_q_gain_a, m_k_gain_a, m_sinks_a, m_q_gain_b, m_k_gain_b,
               m_q_gain_c, m_k_gain_c]
    small_v = [v_norm_gain, v_mem_norm_gain, v_b_forget, v_q_gain_a, v_k_gain_a, v_sinks_a, v_q_gain_b, v_k_gain_b,
               v_q_gain_c, v_k_gain_c]
    sizes = [w.shape[1] for w in small_w]
    s_d, s_m, s_v = _adamw(red, _pack_small(small_w), _pack_small(small_m), _pack_small(small_v), tr=8, name="adamw_small")
    g_small = _unpack_small(red, sizes + [1])
    loss = g_small[-1].reshape(())
    d_small, m_small, v_small = _unpack_small(s_d, sizes), _unpack_small(s_m, sizes), _unpack_small(s_v, sizes)

    gw_in, dw_in, mw_in, vw_in = _adamw_w_in(jnp.concatenate([chip, core]), sums[0], theirs[0], sums[1], theirs[1],
                                             w_in[0], m_w_in[0], v_w_in[0])
    big = {}
    for t, nm, w, m, v in ((2, "w_mem_kv", w_mem_kv, m_w_mem_kv, v_w_mem_kv),
                           (3, "w_branch_a", w_branch_a, m_w_branch_a, v_w_branch_a),
                           (4, "w_branch_b", w_branch_b, m_w_branch_b, v_w_branch_b),
                           (5, "w_branch_c", w_branch_c, m_w_branch_c, v_w_branch_c),
                           (6, "w_out", w_out, m_w_out, v_w_out)):
        big[nm] = _adamw_halves(sums[t], theirs[t], core, w[0], m[0], v[0], axis=HALF_AXIS[t], tr=128,
                                name="adamw_" + nm)

    def collect(kind):
        sm = (g_small, d_small, m_small, v_small)[kind]
        win = (gw_in, dw_in, mw_in, vw_in)[kind]
        return ([sm[0], sm[1], win[None]] + [a for a in sm[2:10]]
                + [big[n][kind][None] for n in ("w_mem_kv", "w_branch_a", "w_branch_b", "w_branch_c", "w_out")])

    return (loss, r["grad_x"][None], *collect(0), *collect(1), *collect(2), *collect(3))
```

```python
import functools

import numpy as np
import jax
import jax.numpy as jnp
from jax import lax
from jax.experimental import pallas as pl
from jax.experimental.pallas import tpu as pltpu

F32 = jnp.float32
BF16 = jnp.bfloat16
HI = lax.Precision.HIGHEST
SDS = jax.ShapeDtypeStruct
MESH = pl.DeviceIdType.MESH

D_MODEL = 2048
HEAD_DIM = 64
A_HEADS = 12
A_GROUP = 3
B_HEADS = 12
C_HEADS = 4
C_HEAD_DIM = 128
WINDOW = 128
EPS = 1e-6
NEG = -1e30
LANE = 128

QA, KA, VA, ZA = 0, 768, 1024, 1280
QB, KB, VB, ZB = 2048, 2816, 3584, 4352
QC, ZC = 5120, 5632
GATE = 6144
P_MAIN = 12288
N_FORGET = 12
FORGET_COL = 5120
SHARD_COLS = 3075
SLAB = 3200
SLAB_START = (0, 3072, 6016, 9088)
SLAB_SHIFT = (0, 3, 122, 125)
N_CHIPS = 4

ADAM_LR = 0.001
ADAM_B1 = 0.9
ADAM_B2 = 0.999
ADAM_EPS = 1e-08
ADAM_WD = 0.01
ADAM_STEP = 10

VMEM_LIMIT = 56 * 1024 * 1024


def _params(sem, vmem=VMEM_LIMIT):
    return pltpu.CompilerParams(dimension_semantics=sem, vmem_limit_bytes=vmem)


def _win(tr, width, off):
    return pl.BlockSpec((pl.Element(tr), pl.Element(width)), lambda i, *_: (i * tr, off))


def _rowblk(tr, width):
    return pl.BlockSpec((tr, width), lambda i, *_: (i, 0))


def _const(shape):
    nd = len(shape)
    return pl.BlockSpec(shape, lambda *_: (0,) * nd)


def _rms(x, g):
    return x * lax.rsqrt(jnp.mean(x * x, axis=-1, keepdims=True) + EPS) * g


def _head_mean_impl(x2, bd):
    hi = x2.astype(BF16)
    lo = (x2 - hi.astype(F32)).astype(BF16)
    return _dot(hi, bd) + _dot(lo, bd)


@jax.custom_vjp
def _head_mean(x2, bd):
    return _head_mean_impl(x2, bd)


_head_mean.defvjp(lambda x2, bd: (_head_mean_impl(x2, bd), bd),
                  lambda bd, g: (_head_mean_impl(g, bd), jnp.zeros_like(bd)))


def _head_norm(x, g_tiled, bd):
    return x * lax.rsqrt(_head_mean(x * x, bd) + EPS) * g_tiled


def _silu(z):
    return z * jax.nn.sigmoid(z)


def _dot_nt(a, b):
    return lax.dot_general(a, b, (((1,), (1,)), ((), ())), preferred_element_type=F32)


def _dot_tn(a, b):
    return lax.dot_general(a, b, (((0,), (0,)), ((), ())), preferred_element_type=F32)


def _dot(a, b):
    return jnp.dot(a, b, preferred_element_type=F32)


def _swa_fn(qk, vz, qkp, vzp, qg, kg, sinks, bd, bias, first):
    q = _head_norm(qk[:, :768], qg, bd)
    k2 = jnp.concatenate([qkp[:, 768:], qk[:, 768:]], axis=0)
    k2 = _head_norm(k2, kg, bd[:256, :256])
    v2 = jnp.concatenate([vzp[:, :256], vz[:, :256]], axis=0)
    z = vz[:, 256:]
    rows = A_GROUP * WINDOW
    kj = lax.broadcasted_iota(jnp.int32, (rows, 2 * WINDOW), 1)
    no_prev = kj < WINDOW * first.astype(jnp.int32)
    qb = q.astype(BF16)
    kb = k2.astype(BF16)
    vb = v2.astype(BF16)
    outs = [None] * A_HEADS
    for g in range(A_HEADS // A_GROUP):
        heads = [A_GROUP * g + u for u in range(A_GROUP)]
        qs = jnp.concatenate([qb[:, 64 * h:64 * h + 64] for h in heads], axis=0)
        s = _dot_nt(qs, kb[:, 64 * g:64 * g + 64]) * (HEAD_DIM ** -0.5) + bias[g]
        s = jnp.where(no_prev, NEG, s)
        sink = jnp.concatenate([jnp.broadcast_to(sinks[:, h:h + 1], (WINDOW, 1)) for h in heads], axis=0)
        m = lax.stop_gradient(jnp.maximum(jnp.max(s, axis=-1, keepdims=True), sink))
        p = jnp.exp(s - m)
        den = jnp.sum(p, axis=-1, keepdims=True) + jnp.exp(sink - m)
        o = _dot((p * (1.0 / den)).astype(BF16), vb[:, 64 * g:64 * g + 64])
        for u, h in enumerate(heads):
            outs[h] = o[WINDOW * u:WINDOW * u + WINDOW, :]
    return jnp.concatenate(outs, axis=1) * _silu(z)


def _swa_bias():
    qi = np.arange(WINDOW)[:, None]
    kj = np.arange(2 * WINDOW)[None, :]
    rel = qi + WINDOW - kj
    valid = (rel >= 0) & (rel < WINDOW)
    out = np.zeros((A_HEADS // A_GROUP, A_GROUP * WINDOW, 2 * WINDOW), np.float32)
    for h in range(A_HEADS):
        slope = np.float32(2.0 ** (-8.0 * (h + 1) / A_HEADS))
        blk = np.where(valid, -slope * rel.astype(np.float32), np.float32(NEG))
        g, u = divmod(h, A_GROUP)
        out[g, WINDOW * u:WINDOW * u + WINDOW, :] = blk
    return jnp.asarray(out)


def _mem_fn(qz, mkv, qg, kg, bd):
    q = _head_norm(qz[:, :512], qg, bd).astype(BF16)
    k = _head_norm(mkv[:, :512], kg, bd).astype(BF16)
    v = mkv[:, 512:].astype(BF16)
    z = qz[:, 512:]
    outs = []
    for h in range(C_HEADS):
        sl = slice(128 * h, 128 * h + 128)
        s = _dot_nt(q[:, sl], k[:, sl]) * (C_HEAD_DIM ** -0.5)
        m = lax.stop_gradient(jnp.max(s, axis=-1, keepdims=True))
        p = jnp.exp(s - m)
        den = jnp.sum(p, axis=-1, keepdims=True)
        outs.append(_dot((p * (1.0 / den)).astype(BF16), v[:, sl]))
    return jnp.concatenate(outs, axis=1) * _silu(z)


def _qn_fn(q, g, bd):
    return _head_norm(q, g, bd) * (HEAD_DIM ** -0.5)


def _kn_fn(k, g, bd):
    return _head_norm(k, g, bd)


def _block_diag(width, hd):
    i = np.arange(width) // hd
    return jnp.asarray((i[:, None] == i[None, :]).astype(np.float32) / hd, BF16)


def _head_sum(width, hd):
    i = np.arange(width) // hd
    return jnp.asarray((i[:, None] == np.arange(LANE)[None, :]).astype(np.float32))


def _rms_fwd(x, g, *, tr, name):
    rows, dm = x.shape

    def body(x_ref, g_ref, o_ref):
        o_ref[...] = _rms(x_ref[...], g_ref[...]).astype(BF16)

    return pl.pallas_call(
        body, grid=(rows // tr,),
        in_specs=[_rowblk(tr, dm), _const((1, dm))],
        out_specs=_rowblk(tr, dm),
        out_shape=SDS((rows, dm), BF16), name=name,
        compiler_params=_params(("parallel",)))(x, g)


def _rms_bwd(x, g, dy, resid, *, tr, name):
    rows, dm = x.shape
    want_dx = resid is not None

    def body(*refs):
        if want_dx:
            x_ref, g_ref, dy_ref, r_ref, dx_ref, dg_ref = refs
        else:
            x_ref, g_ref, dy_ref, dg_ref = refs
        _, vjp = jax.vjp(_rms, x_ref[...], g_ref[...])
        dx, dg = vjp(dy_ref[...])

        @pl.when(pl.program_id(0) == 0)
        def _():
            dg_ref[...] = jnp.zeros_like(dg_ref)

        dg_ref[...] += dg
        if want_dx:
            dx_ref[...] = r_ref[...] + dx

    ins = [x, g, dy] + ([resid] if want_dx else [])
    in_specs = [_rowblk(tr, dm), _const((1, dm)), _rowblk(tr, dm)] + ([_rowblk(tr, dm)] if want_dx else [])
    out_specs = ([_rowblk(tr, dm)] if want_dx else []) + [_const((1, dm))]
    out_shape = ([SDS((rows, dm), F32)] if want_dx else []) + [SDS((1, dm), F32)]
    return pl.pallas_call(
        body, grid=(rows // tr,), in_specs=in_specs, out_specs=out_specs, out_shape=out_shape, name=name,
        compiler_params=_params(("arbitrary",)))(*ins)


class _Comm:
    def __init__(self, ins, out_shapes, sems, start, finish):
        self.ins, self.out_shapes, self.sems, self.start, self.finish = list(ins), list(out_shapes), list(sems), start, finish


def _run_comm(comm, name):
    n_in, n_out = len(comm.ins), len(comm.out_shapes)

    def body(*refs):
        ins, outs, sems = refs[:n_in], refs[n_in:n_in + n_out], refs[n_in + n_out:]
        comm.start(ins, outs, sems)
        comm.finish(ins, outs, sems)

    hbm = pl.BlockSpec(memory_space=pl.ANY)
    return pl.pallas_call(body, in_specs=[hbm] * n_in, out_specs=[hbm] * n_out, out_shape=comm.out_shapes,
                          scratch_shapes=comm.sems, name=name)(*comm.ins)


def _matmul(a, b, *, dims, out_dtype, tm, tn, tk, name, add=None, comms=()):
    if dims == "tn":
        kdim, m = a.shape
    else:
        m, kdim = a.shape
    n = b.shape[0] if dims == "nt" else b.shape[1]
    tm, tn, tk = min(tm, m), min(tn, n), min(tk, kdim)
    assert m % tm == 0 and n % tn == 0 and kdim % tk == 0, (name, a.shape, b.shape)
    ni, nj, nk = m // tm, n // tn, kdim // tk
    has_add = add is not None
    n_mm_in = 3 if has_add else 2
    c_in = [len(c.ins) for c in comms]
    c_out = [len(c.out_shapes) for c in comms]
    c_sem = [len(c.sems) for c in comms]

    def body(*refs):
        a_ref, b_ref = refs[0], refs[1]
        add_ref = refs[2] if has_add else None
        pos = n_mm_in
        cin = []
        for cnt in c_in:
            cin.append(refs[pos:pos + cnt])
            pos += cnt
        o_ref = refs[pos]
        pos += 1
        cout = []
        for cnt in c_out:
            cout.append(refs[pos:pos + cnt])
            pos += cnt
        acc = refs[pos]
        pos += 1
        csem = []
        for cnt in c_sem:
            csem.append(refs[pos:pos + cnt])
            pos += cnt
        i, j, k = pl.program_id(0), pl.program_id(1), pl.program_id(2)

        if comms:
            @pl.when((i == 0) & (j == 0) & (k == 0))
            def _():
                for c, ci, co, cs in zip(comms, cin, cout, csem):
                    c.start(ci, co, cs)

        if dims == "nn":
            part = _dot(a_ref[...], b_ref[...])
        elif dims == "nt":
            part = _dot_nt(a_ref[...], b_ref[...])
        else:
            part = _dot_tn(a_ref[...], b_ref[...])

        @pl.when(k == 0)
        def _():
            acc[...] = part + add_ref[...] if has_add else part

        @pl.when(k > 0)
        def _():
            acc[...] += part

        @pl.when(k == nk - 1)
        def _():
            o_ref[...] = acc[...].astype(out_dtype)

        if comms:
            @pl.when((i == ni - 1) & (j == nj - 1) & (k == nk - 1))
            def _():
                for c, ci, co, cs in zip(comms, cin, cout, csem):
                    c.finish(ci, co, cs)

    a_spec = pl.BlockSpec((tk, tm), lambda i, j, k: (k, i)) if dims == "tn" else pl.BlockSpec((tm, tk), lambda i, j, k: (i, k))
    b_spec = pl.BlockSpec((tn, tk), lambda i, j, k: (j, k)) if dims == "nt" else pl.BlockSpec((tk, tn), lambda i, j, k: (k, j))
    o_spec = pl.BlockSpec((tm, tn), lambda i, j, k: (i, j))
    hbm = pl.BlockSpec(memory_space=pl.ANY)
    ins = [a, b] + ([add] if has_add else []) + [x for c in comms for x in c.ins]
    in_specs = [a_spec, b_spec] + ([o_spec] if has_add else []) + [hbm] * sum(c_in)
    out_specs = [o_spec] + [hbm] * sum(c_out)
    out_shape = [SDS((m, n), out_dtype)] + [s for c in comms for s in c.out_shapes]
    scratch = [pltpu.VMEM((tm, tn), F32)] + [s for c in comms for s in c.sems]
    sem = ("arbitrary",) * 3 if comms else ("parallel", "parallel", "arbitrary")
    res = pl.pallas_call(
        body, grid=(ni, nj, nk), in_specs=in_specs, out_specs=out_specs, out_shape=out_shape, scratch_shapes=scratch,
        name=name, compiler_params=_params(sem))(*ins)
    if not comms:
        return res[0]
    outs, pos = [], 1
    for cnt in c_out:
        outs.append(list(res[pos:pos + cnt]))
        pos += cnt
    return res[0], outs


def _swa_specs(nb):
    prev = lambda off: pl.BlockSpec((pl.Element(WINDOW), pl.Element(1024)),
                                    lambda n: (jnp.maximum(n - 1, 0) * WINDOW, off))
    return [_win(WINDOW, 1024, QA), _win(WINDOW, 1024, VA), prev(QA), prev(VA),
            _const((1, 768)), _const((1, 256)), _const((1, LANE)), _const((768, 768)),
            _const((A_HEADS // A_GROUP, A_GROUP * WINDOW, 2 * WINDOW))]


def _swa_fwd(proj, qg, kg, sinks, bd, bias):
    s = proj.shape[0]
    nb = s // WINDOW

    def body(qk_ref, vz_ref, qkp_ref, vzp_ref, qg_ref, kg_ref, sk_ref, bd_ref, bias_ref, o_ref):
        first = pl.program_id(0) == 0
        o_ref[...] = _swa_fn(qk_ref[...], vz_ref[...], qkp_ref[...], vzp_ref[...], qg_ref[...], kg_ref[...],
                             sk_ref[...], bd_ref[...], bias_ref[...], first).astype(BF16)

    return pl.pallas_call(
        body, grid=(nb,), in_specs=_swa_specs(nb), out_specs=_rowblk(WINDOW, 768),
        out_shape=SDS((s, 768), BF16), name="swa_fwd",
        compiler_params=_params(("parallel",)))(proj, proj, proj, proj, qg, kg, sinks, bd, bias)


def _swa_bwd(proj, qg, kg, sinks, bd, bias, dga):
    s = proj.shape[0]
    nb = s // WINDOW

    def body(qk_ref, vz_ref, qkp_ref, vzp_ref, qg_ref, kg_ref, sk_ref, bd_ref, bias_ref, dg_ref,
             dcur_ref, dprev_ref, dqg_ref, dkg_ref, dsk_ref):
        first = pl.program_id(0) == 0
        bd_v = bd_ref[...]
        bias_v = bias_ref[...]
        fn = lambda qk, vz, qkp, vzp, qg_, kg_, sk: _swa_fn(qk, vz, qkp, vzp, qg_, kg_, sk, bd_v, bias_v, first)
        _, vjp = jax.vjp(fn, qk_ref[...], vz_ref[...], qkp_ref[...], vzp_ref[...], qg_ref[...], kg_ref[...], sk_ref[...])
        dqk, dvz, dqkp, dvzp, dqg, dkg, dsk = vjp(dg_ref[...])

        @pl.when(first)
        def _():
            dqg_ref[...] = jnp.zeros_like(dqg_ref)
            dkg_ref[...] = jnp.zeros_like(dkg_ref)
            dsk_ref[...] = jnp.zeros_like(dsk_ref)

        dqg_ref[...] += dqg
        dkg_ref[...] += dkg
        dsk_ref[...] += dsk
        dcur_ref[...] = jnp.concatenate([dqk, dvz], axis=1)
        dprev_ref[...] = jnp.concatenate([dqkp[:, 768:], dvzp[:, :256]], axis=1)

    return pl.pallas_call(
        body, grid=(nb,), in_specs=_swa_specs(nb) + [_rowblk(WINDOW, 768)],
        out_specs=[_rowblk(WINDOW, 2048), pl.BlockSpec((None, WINDOW, 512), lambda n: (n, 0, 0)),
                   _const((1, 768)), _const((1, 256)), _const((1, LANE))],
        out_shape=[SDS((s, 2048), F32), SDS((nb, WINDOW, 512), F32), SDS((1, 768), F32), SDS((1, 256), F32),
                   SDS((1, LANE), F32)],
        name="swa_bwd", compiler_params=_params(("arbitrary",)))(proj, proj, proj, proj, qg, kg, sinks, bd, bias, dga)


def _swa_combine(dcur, dprev):
    s = dcur.shape[0]
    nb = s // WINDOW

    def body(c_ref, p_ref, o_ref):
        c = c_ref[...]
        nxt = jnp.where(pl.program_id(0) == nb - 1, 0.0, p_ref[...])
        o_ref[...] = jnp.concatenate([c[:, :768], c[:, 768:1280] + nxt, c[:, 1280:]], axis=1).astype(BF16)

    return pl.pallas_call(
        body, grid=(nb,),
        in_specs=[_rowblk(WINDOW, 2048), pl.BlockSpec((None, WINDOW, 512), lambda n: (jnp.minimum(n + 1, nb - 1), 0, 0))],
        out_specs=_rowblk(WINDOW, 2048), out_shape=SDS((s, 2048), BF16), name="swa_combine",
        compiler_params=_params(("parallel",)))(dcur, dprev)


def _mem_fwd(proj, mkv, qg, kg, bd, *, tr):
    s = proj.shape[0]

    def body(qz_ref, mkv_ref, qg_ref, kg_ref, bd_ref, o_ref):
        o_ref[...] = _mem_fn(qz_ref[...], mkv_ref[...], qg_ref[...], kg_ref[...], bd_ref[...]).astype(BF16)

    return pl.pallas_call(
        body, grid=(s // tr,),
        in_specs=[_win(tr, 1024, QC), _const(mkv.shape), _const((1, 512)), _const((1, 512)), _const((512, 512))],
        out_specs=_rowblk(tr, 512), out_shape=SDS((s, 512), BF16), name="mem_fwd",
        compiler_params=_params(("parallel",)))(proj, mkv, qg, kg, bd)


def _mem_bwd(proj, mkv, qg, kg, bd, dgc, *, tr):
    s = proj.shape[0]

    def body(qz_ref, mkv_ref, qg_ref, kg_ref, bd_ref, dg_ref, dqz_ref, dmkv_ref, dqg_ref, dkg_ref):
        bd_v = bd_ref[...]
        fn = lambda qz, mkv_, qg_, kg_: _mem_fn(qz, mkv_, qg_, kg_, bd_v)
        _, vjp = jax.vjp(fn, qz_ref[...], mkv_ref[...], qg_ref[...], kg_ref[...])
        dqz, dmkv, dqg, dkg = vjp(dg_ref[...])

        @pl.when(pl.program_id(0) == 0)
        def _():
            dmkv_ref[...] = jnp.zeros_like(dmkv_ref)
            dqg_ref[...] = jnp.zeros_like(dqg_ref)
            dkg_ref[...] = jnp.zeros_like(dkg_ref)

        dmkv_ref[...] += dmkv
        dqg_ref[...] += dqg
        dkg_ref[...] += dkg
        dqz_ref[...] = dqz.astype(BF16)

    return pl.pallas_call(
        body, grid=(s // tr,),
        in_specs=[_win(tr, 1024, QC), _const(mkv.shape), _const((1, 512)), _const((1, 512)), _const((512, 512)),
                  _rowblk(tr, 512)],
        out_specs=[_rowblk(tr, 1024), _const(mkv.shape), _const((1, 512)), _const((1, 512))],
        out_shape=[SDS((s, 1024), BF16), SDS(mkv.shape, F32), SDS((1, 512), F32), SDS((1, 512), F32)],
        name="mem_bwd", compiler_params=_params(("arbitrary",)))(proj, mkv, qg, kg, bd, dgc)


def _log_sigmoid(x):
    return jnp.minimum(x, 0.0) - jnp.log1p(jnp.exp(-jnp.abs(x)))


def _fox_prep(proj, fbl, qg, kg, bfor, bd, *, tr):
    s = proj.shape[0]
    tri = jnp.asarray(np.tril(np.ones((tr, tr), np.float32)))

    def body(q_ref, k_ref, fb_ref, qg_ref, kg_ref, bf_ref, bd_ref, tri_ref, qn_ref, kn_ref, cq_ref, ck_ref, carry):
        @pl.when(pl.program_id(0) == 0)
        def _():
            carry[...] = jnp.zeros_like(carry)

        bd_v = bd_ref[...]
        qn_ref[...] = _qn_fn(q_ref[...], qg_ref[...], bd_v).astype(BF16)
        kn_ref[...] = _kn_fn(k_ref[...], kg_ref[...], bd_v).astype(BF16)
        lane = lax.broadcasted_iota(jnp.int32, (tr, LANE), 1)
        logf = jnp.where(lane < N_FORGET, _log_sigmoid(fb_ref[...] + bf_ref[...]), 0.0)
        c = jnp.dot(tri_ref[...], logf, precision=HI, preferred_element_type=F32) + carry[...]
        cq_ref[...] = c
        ck_ref[...] = jnp.transpose(c)
        carry[...] = c[tr - 1:tr, :]

    return pl.pallas_call(
        body, grid=(s // tr,),
        in_specs=[_win(tr, 768, QB), _win(tr, 768, KB), _rowblk(tr, LANE), _const((1, 768)), _const((1, 768)),
                  _const((1, LANE)), _const((768, 768)), _const((tr, tr))],
        out_specs=[_rowblk(tr, 768), _rowblk(tr, 768), _rowblk(tr, LANE), pl.BlockSpec((LANE, tr), lambda i: (0, i))],
        out_shape=[SDS((s, 768), BF16), SDS((s, 768), BF16), SDS((s, LANE), F32), SDS((LANE, s), F32)],
        scratch_shapes=[pltpu.VMEM((1, LANE), F32)], name="fox_prep",
        compiler_params=_params(("arbitrary",)))(proj, proj, fbl, qg, kg, bfor, bd, tri)


FOX_TQ, FOX_TK = 256, 512
FOX_FWD_TQ, FOX_FWD_TK = 512, 1024


def _fox_tiles(s):
    return min(FOX_TQ, s), min(FOX_TK, s)


def _fox_fwd(proj, qn, kn, cq, ck):
    s = proj.shape[0]
    tq, tk = _fox_tiles(s)
    nq, nk = s // tq, s // tk

    def last_k(i):
        return (i * tq + tq - 1) // tk

    def body(q_ref, k_ref, v_ref, cq_ref, ck_ref, z_ref, gb_ref, yb_ref, lse_ref, acc, m_s, l_s):
        i, j = pl.program_id(0), pl.program_id(1)

        @pl.when(j == 0)
        def _():
            acc[...] = jnp.zeros_like(acc)
            m_s[...] = jnp.full_like(m_s, NEG)
            l_s[...] = jnp.ones_like(l_s)

        @pl.when(j <= last_k(i))
        def _():
            qpos = i * tq + lax.broadcasted_iota(jnp.int32, (tq, tk), 0)
            kpos = j * tk + lax.broadcasted_iota(jnp.int32, (tq, tk), 1)
            mask = kpos <= qpos
            q, k = q_ref[...], k_ref[...]
            v = v_ref[...].astype(BF16)
            cqv, ckv = cq_ref[...], ck_ref[...]
            m_all, l_all = m_s[...], l_s[...]
            lane = lax.broadcasted_iota(jnp.int32, (tq, LANE), 1)
            m_out, l_out = m_all, l_all
            for hp in range(B_HEADS // 2):
                acc_pair = acc[:, 128 * hp:128 * hp + 128]
                new = []
                for u in range(2):
                    h = 2 * hp + u
                    sl = slice(64 * h, 64 * h + 64)
                    sc = _dot_nt(q[:, sl], k[:, sl]) + cqv[:, h:h + 1] - ckv[h:h + 1, :]
                    sc = jnp.where(mask, sc, NEG)
                    m_prev = m_all[:, h:h + 1]
                    m_new = jnp.maximum(m_prev, jnp.max(sc, axis=-1, keepdims=True))
                    alpha = jnp.exp(m_prev - m_new)
                    p = jnp.exp(sc - m_new)
                    l_new = alpha * l_all[:, h:h + 1] + jnp.sum(p, axis=-1, keepdims=True)
                    new.append(alpha * acc_pair[:, 64 * u:64 * u + 64] + _dot(p.astype(BF16), v[:, sl]))
                    m_out = jnp.where(lane == h, m_new, m_out)
                    l_out = jnp.where(lane == h, l_new, l_out)
                acc[:, 128 * hp:128 * hp + 128] = jnp.concatenate(new, axis=1)
            m_s[...] = m_out
            l_s[...] = l_out

        @pl.when(j == nk - 1)
        def _():
            l_all = l_s[...]
            inv = 1.0 / l_all
            a = acc[...]
            y = jnp.concatenate([a[:, 64 * h:64 * h + 64] * inv[:, h:h + 1] for h in range(B_HEADS)], axis=1)
            yb_ref[...] = y
            gb_ref[...] = (y * _silu(z_ref[...])).astype(BF16)
            lse_ref[...] = m_s[...] + jnp.log(l_all)

    kmap = lambda i, j: (jnp.minimum(j, last_k(i)), 0)
    return pl.pallas_call(
        body, grid=(nq, nk),
        in_specs=[pl.BlockSpec((tq, 768), lambda i, j: (i, 0)),
                  pl.BlockSpec((tk, 768), kmap),
                  pl.BlockSpec((pl.Element(tk), pl.Element(768)), lambda i, j: (jnp.minimum(j, last_k(i)) * tk, VB)),
                  pl.BlockSpec((tq, LANE), lambda i, j: (i, 0)),
                  pl.BlockSpec((16, tk), lambda i, j: (0, jnp.minimum(j, last_k(i)))),
                  pl.BlockSpec((pl.Element(tq), pl.Element(768)), lambda i, j: (i * tq, ZB))],
        out_specs=[pl.BlockSpec((tq, 768), lambda i, j: (i, 0)), pl.BlockSpec((tq, 768), lambda i, j: (i, 0)),
                   pl.BlockSpec((tq, LANE), lambda i, j: (i, 0))],
        out_shape=[SDS((s, 768), BF16), SDS((s, 768), F32), SDS((s, LANE), F32)],
        scratch_shapes=[pltpu.VMEM((tq, 768), F32), pltpu.VMEM((tq, LANE), F32), pltpu.VMEM((tq, LANE), F32)],
        name="fox_fwd", compiler_params=_params(("parallel", "arbitrary")))(qn, kn, proj, cq, ck, proj)


def _fox_bwd_pre(proj, yb, dgb, hsum, *, tr):
    s = proj.shape[0]

    def body(z_ref, y_ref, dg_ref, hs_ref, dy_ref, dz_ref, dl_ref):
        z, y, dg = z_ref[...], y_ref[...], dg_ref[...]
        sg = jax.nn.sigmoid(z)
        dy = dg * (z * sg)
        dy_ref[...] = dy.astype(BF16)
        dz_ref[...] = (dg * y * (sg * (1.0 + z * (1.0 - sg)))).astype(BF16)
        dl_ref[...] = jnp.dot(dy * y, hs_ref[...], precision=HI, preferred_element_type=F32)

    return pl.pallas_call(
        body, grid=(s // tr,),
        in_specs=[_win(tr, 768, ZB), _rowblk(tr, 768), _rowblk(tr, 768), _const((768, LANE))],
        out_specs=[_rowblk(tr, 768), _rowblk(tr, 768), _rowblk(tr, LANE)],
        out_shape=[SDS((s, 768), BF16), SDS((s, 768), BF16), SDS((s, LANE), F32)], name="fox_bwd_pre",
        compiler_params=_params(("parallel",)))(proj, yb, dgb, hsum)


def _fox_bwd(proj, qn, kn, cq, ck, lse, delta, dyb):
    s = proj.shape[0]
    tq, tk = _fox_tiles(s)
    nq, nk = s // tq, s // tk

    def first_q(j):
        return (j * tk) // tq

    def body(q_ref, k_ref, v_ref, cq_ref, ck_ref, lse_ref, dl_ref, dy_ref,
             dq_ref, dcq_ref, dk_ref, dv_ref, dck_ref, dk_acc, dv_acc, dck_acc):
        j, i = pl.program_id(0), pl.program_id(1)

        @pl.when((j == 0) & (i == 0))
        def _():
            dq_ref[...] = jnp.zeros_like(dq_ref)
            dcq_ref[...] = jnp.zeros_like(dcq_ref)

        @pl.when(i == 0)
        def _():
            dk_acc[...] = jnp.zeros_like(dk_acc)
            dv_acc[...] = jnp.zeros_like(dv_acc)
            dck_acc[...] = jnp.zeros_like(dck_acc)

        @pl.when(i >= first_q(j))
        def _():
            qpos = i * tq + lax.broadcasted_iota(jnp.int32, (tq, tk), 0)
            kpos = j * tk + lax.broadcasted_iota(jnp.int32, (tq, tk), 1)
            mask = kpos <= qpos
            q, k = q_ref[...], k_ref[...]
            v = v_ref[...].astype(BF16)
            dy = dy_ref[...]
            cqv, ckv, lsev, dlv = cq_ref[...], ck_ref[...], lse_ref[...], dl_ref[...]
            lane = lax.broadcasted_iota(jnp.int32, (tq, LANE), 1)
            rows = pl.ds(pl.multiple_of(i * tq, tq), tq)
            dcq_t = jnp.zeros((tq, LANE), F32)
            for hp in range(B_HEADS // 2):
                dq_new, dk_new, dv_new = [], [], []
                for u in range(2):
                    h = 2 * hp + u
                    sl = slice(64 * h, 64 * h + 64)
                    sc = _dot_nt(q[:, sl], k[:, sl]) + cqv[:, h:h + 1] - ckv[h:h + 1, :]
                    sc = jnp.where(mask, sc, NEG)
                    p = jnp.exp(sc - lsev[:, h:h + 1])
                    dp = _dot_nt(dy[:, sl], v[:, sl])
                    ds = p * (dp - dlv[:, h:h + 1])
                    dsb = ds.astype(BF16)
                    dv_new.append(_dot_tn(p.astype(BF16), dy[:, sl]))
                    dk_new.append(_dot_tn(dsb, q[:, sl]))
                    dq_new.append(_dot(dsb, k[:, sl]))
                    dcq_t = jnp.where(lane == h, jnp.sum(ds, axis=-1, keepdims=True), dcq_t)
                    dck_acc[h:h + 1, :] -= jnp.sum(ds, axis=0, keepdims=True)
                cols = slice(128 * hp, 128 * hp + 128)
                dq_ref[rows, cols] += jnp.concatenate(dq_new, axis=1)
                dk_acc[:, cols] += jnp.concatenate(dk_new, axis=1)
                dv_acc[:, cols] += jnp.concatenate(dv_new, axis=1)
            dcq_ref[rows, :] += dcq_t

        @pl.when(i == nq - 1)
        def _():
            dk_ref[...] = dk_acc[...]
            dv_ref[...] = dv_acc[...].astype(BF16)
            dck_ref[...] = jnp.concatenate([dck_acc[...], jnp.zeros((LANE - 16, tk), F32)], axis=0)

    qmap = lambda j, i: (jnp.maximum(i, first_q(j)), 0)
    return pl.pallas_call(
        body, grid=(nk, nq),
        in_specs=[pl.BlockSpec((tq, 768), qmap),
                  pl.BlockSpec((tk, 768), lambda j, i: (j, 0)),
                  pl.BlockSpec((pl.Element(tk), pl.Element(768)), lambda j, i: (j * tk, VB)),
                  pl.BlockSpec((tq, LANE), qmap),
                  pl.BlockSpec((16, tk), lambda j, i: (0, j)),
                  pl.BlockSpec((tq, LANE), qmap),
                  pl.BlockSpec((tq, LANE), qmap),
                  pl.BlockSpec((tq, 768), qmap)],
        out_specs=[_const((s, 768)), _const((s, LANE)),
                   pl.BlockSpec((tk, 768), lambda j, i: (j, 0)), pl.BlockSpec((tk, 768), lambda j, i: (j, 0)),
                   pl.BlockSpec((LANE, tk), lambda j, i: (0, j))],
        out_shape=[SDS((s, 768), F32), SDS((s, LANE), F32), SDS((s, 768), F32), SDS((s, 768), BF16),
                   SDS((LANE, s), F32)],
        scratch_shapes=[pltpu.VMEM((tk, 768), F32), pltpu.VMEM((tk, 768), F32), pltpu.VMEM((16, tk), F32)],
        name="fox_bwd", compiler_params=_params(("arbitrary", "arbitrary")))(qn, kn, proj, cq, ck, lse, delta, dyb)


def _fox_bwd_post(proj, fbl, qg, kg, bfor, bd, dqn, dkn, dcq, dck, *, tr):
    s = proj.shape[0]
    nb = s // tr
    triu = jnp.asarray(np.triu(np.ones((tr, tr), np.float32)))
    rev = lambda i: nb - 1 - i

    def body(q_ref, k_ref, fb_ref, qg_ref, kg_ref, bf_ref, bd_ref, tri_ref, dqn_ref, dkn_ref, dcq_ref, dck_ref,
             dq_ref, dk_ref, dfb_ref, dqg_ref, dkg_ref, dbf_ref, carry):
        @pl.when(pl.program_id(0) == 0)
        def _():
            carry[...] = jnp.zeros_like(carry)
            dqg_ref[...] = jnp.zeros_like(dqg_ref)
            dkg_ref[...] = jnp.zeros_like(dkg_ref)
            dbf_ref[...] = jnp.zeros_like(dbf_ref)

        bd_v = bd_ref[...]
        _, vjp_q = jax.vjp(lambda q, g: _qn_fn(q, g, bd_v), q_ref[...], qg_ref[...])
        dq, dqg = vjp_q(dqn_ref[...])
        _, vjp_k = jax.vjp(lambda k, g: _kn_fn(k, g, bd_v), k_ref[...], kg_ref[...])
        dk, dkg = vjp_k(dkn_ref[...])
        dq_ref[...] = dq.astype(BF16)
        dk_ref[...] = dk.astype(BF16)
        dqg_ref[...] += dqg
        dkg_ref[...] += dkg

        dc = dcq_ref[...] + jnp.transpose(dck_ref[...])
        dlogf = jnp.dot(tri_ref[...], dc, precision=HI, preferred_element_type=F32) + carry[...]
        carry[...] = dlogf[0:1, :]
        lane = lax.broadcasted_iota(jnp.int32, (tr, LANE), 1)
        xf = fb_ref[...] + bf_ref[...]
        dfb = jnp.where(lane < N_FORGET, dlogf * jax.nn.sigmoid(-xf), 0.0)
        dfb_ref[...] = dfb.astype(BF16)
        dbf_ref[...] += jnp.sum(dfb, axis=0, keepdims=True)

    rb = lambda w: pl.BlockSpec((tr, w), lambda i: (rev(i), 0))
    wn = lambda w, off: pl.BlockSpec((pl.Element(tr), pl.Element(w)), lambda i: (rev(i) * tr, off))
    return pl.pallas_call(
        body, grid=(nb,),
        in_specs=[wn(768, QB), wn(768, KB), rb(LANE), _const((1, 768)), _const((1, 768)), _const((1, LANE)),
                  _const((768, 768)), _const((tr, tr)), rb(768), rb(768), rb(LANE),
                  pl.BlockSpec((LANE, tr), lambda i: (0, rev(i)))],
        out_specs=[rb(768), rb(768), rb(LANE), _const((1, 768)), _const((1, 768)), _const((1, LANE))],
        out_shape=[SDS((s, 768), BF16), SDS((s, 768), BF16), SDS((s, LANE), BF16), SDS((1, 768), F32),
                   SDS((1, 768), F32), SDS((1, LANE), F32)],
        scratch_shapes=[pltpu.VMEM((1, LANE), F32)], name="fox_bwd_post",
        compiler_params=_params(("arbitrary",)))(proj, proj, fbl, qg, kg, bfor, bd, triu, dqn, dkn, dcq, dck)


AUG = 128 * B_HEADS
COL_A, COL_B = 64, 67


def _split3(c):
    hi = c.astype(BF16)
    r1 = c - hi.astype(F32)
    mid = r1.astype(BF16)
    lo = (r1 - mid.astype(F32)).astype(BF16)
    return hi, mid, lo


def _expand_mats():
    def mat(col0):
        e = np.zeros((768 + 3 * LANE, AUG), np.float32)
        for h in range(B_HEADS):
            for d in range(HEAD_DIM):
                e[64 * h + d, 128 * h + d] = 1.0
            for part in range(3):
                e[768 + LANE * part + h, 128 * h + col0 + part] = 1.0
        return e

    def ones(col0):
        o = np.zeros((1, AUG), np.float32)
        for h in range(B_HEADS):
            o[0, 128 * h + col0:128 * h + col0 + 3] = 1.0
        return o

    return (jnp.asarray(mat(COL_A), BF16), jnp.asarray(mat(COL_B), BF16), jnp.asarray(ones(COL_A)), jnp.asarray(ones(COL_B)))


def _augment(data_bf16, triple, emat, ones_row):
    parts = [data_bf16] + (list(triple) if triple is not None else [jnp.zeros((data_bf16.shape[0], LANE), BF16)] * 3)
    wide = _dot(jnp.concatenate(parts, axis=1), emat)
    if ones_row is not None:
        wide = wide + ones_row
    return wide


def _compact(wide):
    return jnp.concatenate([wide[:, 128 * h:128 * h + 64] for h in range(B_HEADS)], axis=1)


def _lane_of_heads(wide, col):
    rows = wide.shape[0]
    lane = lax.broadcasted_iota(jnp.int32, (rows, LANE), 1)
    out = jnp.zeros((rows, LANE), F32)
    for h in range(B_HEADS):
        out = jnp.where(lane == h, wide[:, 128 * h + col:128 * h + col + 1], out)
    return out


def _fox2_prep(proj, fbl, qg, kg, bfor, bd, ea, eb, ones_a, ones_b, *, tr):
    s = proj.shape[0]
    tri = jnp.asarray(np.tril(np.ones((tr, tr), np.float32)))

    def body(q_ref, k_ref, v_ref, fb_ref, qg_ref, kg_ref, bf_ref, bd_ref, tri_ref, ea_ref, eb_ref, oa_ref, ob_ref,
             qat_ref, ka_ref, kat_ref, va_ref, vat_ref, qn_ref, c_ref, carry):
        @pl.when(pl.program_id(0) == 0)
        def _():
            carry[...] = jnp.zeros_like(carry)

        bd_v = bd_ref[...]
        lane = lax.broadcasted_iota(jnp.int32, (tr, LANE), 1)
        logf = jnp.where(lane < N_FORGET, _log_sigmoid(fb_ref[...] + bf_ref[...]), 0.0)
        c = jnp.dot(tri_ref[...], logf, precision=HI, preferred_element_type=F32) + carry[...]
        c_ref[...] = c
        carry[...] = c[tr - 1:tr, :]
        qn = _qn_fn(q_ref[...], qg_ref[...], bd_v).astype(BF16)
        kn = _kn_fn(k_ref[...], kg_ref[...], bd_v).astype(BF16)
        qn_ref[...] = qn
        qat_ref[...] = jnp.transpose(_augment(qn, _split3(c), ea_ref[...], ob_ref[...])).astype(BF16)
        ka = _augment(kn, _split3(-c), eb_ref[...], oa_ref[...])
        ka_ref[...] = ka.astype(BF16)
        kat_ref[...] = jnp.transpose(ka).astype(BF16)
        va = _augment(v_ref[...].astype(BF16), None, ea_ref[...], oa_ref[...])
        va_ref[...] = va.astype(BF16)
        vat_ref[...] = jnp.transpose(va).astype(BF16)

    emat = _const((768 + 3 * LANE, AUG))
    return pl.pallas_call(
        body, grid=(s // tr,),
        in_specs=[_win(tr, 768, QB), _win(tr, 768, KB), _win(tr, 768, VB), _rowblk(tr, LANE), _const((1, 768)),
                  _const((1, 768)), _const((1, LANE)), _const((768, 768)), _const((tr, tr)), emat, emat,
                  _const((1, AUG)), _const((1, AUG))],
        out_specs=[pl.BlockSpec((AUG, tr), lambda i: (0, i)), _rowblk(tr, AUG), pl.BlockSpec((AUG, tr), lambda i: (0, i)),
                   _rowblk(tr, AUG), pl.BlockSpec((AUG, tr), lambda i: (0, i)), _rowblk(tr, 768), _rowblk(tr, LANE)],
        out_shape=[SDS((AUG, s), BF16), SDS((s, AUG), BF16), SDS((AUG, s), BF16), SDS((s, AUG), BF16),
                   SDS((AUG, s), BF16), SDS((s, 768), BF16), SDS((s, LANE), F32)],
        scratch_shapes=[pltpu.VMEM((1, LANE), F32)], name="fox_prep",
        compiler_params=_params(("arbitrary",)))(proj, proj, proj, fbl, qg, kg, bfor, bd, tri, ea, eb, ones_a, ones_b)


def _fox2_fwd(proj, qat, ka, vat):
    s = proj.shape[0]
    tq, tk = min(FOX_FWD_TQ, s), min(FOX_FWD_TK, s)
    nq, nk = s // tq, s // tk

    def last_k(i):
        return (i * tq + tq - 1) // tk

    def body(qt_ref, k_ref, vt_ref, z_ref, gb_ref, yb_ref, lse_ref, acc, m_s):
        i, j = pl.program_id(0), pl.program_id(1)

        @pl.when(j == 0)
        def _():
            acc[...] = jnp.zeros_like(acc)
            m_s[...] = jnp.full_like(m_s, NEG)

        def tile(masked):
            if masked:
                kpos = j * tk + lax.broadcasted_iota(jnp.int32, (tk, tq), 0)
                qpos = i * tq + lax.broadcasted_iota(jnp.int32, (tk, tq), 1)
                mask = kpos <= qpos
            for h in range(B_HEADS):
                sl = slice(128 * h, 128 * h + 128)
                sc = _dot(k_ref[:, sl], qt_ref[sl, :])
                if masked:
                    sc = jnp.where(mask, sc, NEG)
                m_prev = m_s[h:h + 1, :]
                m_new = jnp.maximum(m_prev, jnp.max(sc, axis=0, keepdims=True))
                p = jnp.exp(sc - m_new).astype(BF16)
                acc[sl, :] = jnp.exp(m_prev - m_new) * acc[sl, :] + _dot(vt_ref[sl, :], p)
                m_s[h:h + 1, :] = m_new

        full = j * tk + tk - 1 <= i * tq

        @pl.when(full)
        def _():
            tile(False)

        @pl.when(jnp.logical_and(jnp.logical_not(full), j <= last_k(i)))
        def _():
            tile(True)

        @pl.when(j == nk - 1)
        def _():
            outs = []
            row = lax.broadcasted_iota(jnp.int32, (LANE, tq), 0)
            lse_t = jnp.zeros((LANE, tq), F32)
            for h in range(B_HEADS):
                l_row = acc[128 * h + COL_A:128 * h + COL_A + 1, :]
                outs.append(acc[128 * h:128 * h + 64, :] * (1.0 / l_row))
                lse_t = jnp.where(row == h, m_s[h:h + 1, :] + jnp.log(l_row), lse_t)
            y = jnp.transpose(jnp.concatenate(outs, axis=0))
            yb_ref[...] = y
            gb_ref[...] = (y * _silu(z_ref[...])).astype(BF16)
            lse_ref[...] = jnp.transpose(lse_t)

    kcol = lambda i, j: (0, jnp.minimum(j, last_k(i)))
    return pl.pallas_call(
        body, grid=(nq, nk),
        in_specs=[pl.BlockSpec((AUG, tq), lambda i, j: (0, i)),
                  pl.BlockSpec((tk, AUG), lambda i, j: (jnp.minimum(j, last_k(i)), 0)),
                  pl.BlockSpec((AUG, tk), kcol),
                  pl.BlockSpec((pl.Element(tq), pl.Element(768)), lambda i, j: (i * tq, ZB))],
        out_specs=[pl.BlockSpec((tq, 768), lambda i, j: (i, 0)), pl.BlockSpec((tq, 768), lambda i, j: (i, 0)),
                   pl.BlockSpec((tq, LANE), lambda i, j: (i, 0))],
        out_shape=[SDS((s, 768), BF16), SDS((s, 768), F32), SDS((s, LANE), F32)],
        scratch_shapes=[pltpu.VMEM((AUG, tq), F32), pltpu.VMEM((16, tq), F32)],
        name="fox_fwd", compiler_params=_params(("parallel", "arbitrary")))(qat, ka, vat, proj)


def _fox2_bwd_pre(proj, yb, dgb, qn, c, lse, hsum, ea, ones_b, *, tr):
    s = proj.shape[0]

    def body(z_ref, y_ref, dg_ref, qn_ref, c_ref, lse_ref, hs_ref, ea_ref, ob_ref,
             qa_ref, qat_ref, dya_ref, dyat_ref, dz_ref):
        z, y, dg = z_ref[...], y_ref[...], dg_ref[...]
        sg = jax.nn.sigmoid(z)
        dy = dg * (z * sg)
        dz_ref[...] = (dg * y * (sg * (1.0 + z * (1.0 - sg)))).astype(BF16)
        delta = jnp.dot(dy * y, hs_ref[...], precision=HI, preferred_element_type=F32)
        e = ea_ref[...]
        dya = _augment(dy.astype(BF16), _split3(-delta), e, None)
        dya_ref[...] = dya.astype(BF16)
        dyat_ref[...] = jnp.transpose(dya).astype(BF16)
        qa = _augment(qn_ref[...], _split3(c_ref[...] - lse_ref[...]), e, ob_ref[...])
        qa_ref[...] = qa.astype(BF16)
        qat_ref[...] = jnp.transpose(qa).astype(BF16)

    return pl.pallas_call(
        body, grid=(s // tr,),
        in_specs=[_win(tr, 768, ZB), _rowblk(tr, 768), _rowblk(tr, 768), _rowblk(tr, 768), _rowblk(tr, LANE),
                  _rowblk(tr, LANE), _const((768, LANE)), _const((768 + 3 * LANE, AUG)), _const((1, AUG))],
        out_specs=[_rowblk(tr, AUG), pl.BlockSpec((AUG, tr), lambda i: (0, i)), _rowblk(tr, AUG),
                   pl.BlockSpec((AUG, tr), lambda i: (0, i)), _rowblk(tr, 768)],
        out_shape=[SDS((s, AUG), BF16), SDS((AUG, s), BF16), SDS((s, AUG), BF16), SDS((AUG, s), BF16),
                   SDS((s, 768), BF16)], name="fox_bwd_pre",
        compiler_params=_params(("parallel",)))(proj, yb, dgb, qn, c, lse, hsum, ea, ones_b)


def _fox2_bwd(qb, qbt, ka, kat, va, dya, dyat):
    s = qb.shape[0]
    tq, tk = _fox_tiles(s)
    nq, nk = s // tq, s // tk

    def first_q(j):
        return (j * tk) // tq

    def body(q_ref, qt_ref, k_ref, kt_ref, v_ref, dy_ref, dyt_ref, dq_hbm, dk_ref, dv_ref, dck_ref,
             dq_acc, dk_acc, dv_acc, sem):
        j, i = pl.program_id(0), pl.program_id(1)

        @pl.when((j == 0) & (i == 0))
        def _():
            dq_acc[...] = jnp.zeros_like(dq_acc)

        @pl.when(i == 0)
        def _():
            dk_acc[...] = jnp.zeros_like(dk_acc)
            dv_acc[...] = jnp.zeros_like(dv_acc)

        def tile(masked):
            if masked:
                kpos = j * tk + lax.broadcasted_iota(jnp.int32, (tk, tq), 0)
                qpos = i * tq + lax.broadcasted_iota(jnp.int32, (tk, tq), 1)
                mask = kpos <= qpos
            cols = pl.ds(pl.multiple_of(i * tq, tq), tq)
            for h in range(B_HEADS):
                sl = slice(128 * h, 128 * h + 128)
                sc = _dot(k_ref[:, sl], qt_ref[sl, :])
                if masked:
                    sc = jnp.where(mask, sc, NEG)
                p = jnp.exp(sc)
                ds = (p * _dot(v_ref[:, sl], dyt_ref[sl, :])).astype(BF16)
                dv_acc[:, sl] += _dot(p.astype(BF16), dy_ref[:, sl])
                dk_acc[:, sl] += _dot(ds, q_ref[:, sl])
                dq_acc[sl, cols] += _dot(kt_ref[sl, :], ds)

        full = j * tk + tk - 1 <= i * tq

        @pl.when(full)
        def _():
            tile(False)

        @pl.when(jnp.logical_and(jnp.logical_not(full), i >= first_q(j)))
        def _():
            tile(True)

        @pl.when(i == nq - 1)
        def _():
            dkw = dk_acc[...]
            dk_ref[...] = _compact(dkw)
            dv_ref[...] = _compact(dv_acc[...]).astype(BF16)
            dck_ref[...] = -_lane_of_heads(dkw, COL_B)

        @pl.when((j == nk - 1) & (i == nq - 1))
        def _():
            cp = pltpu.make_async_copy(dq_acc, dq_hbm, sem)
            cp.start()
            cp.wait()

    qrow = pl.BlockSpec((tq, AUG), lambda j, i: (jnp.maximum(i, first_q(j)), 0))
    qcol = pl.BlockSpec((AUG, tq), lambda j, i: (0, jnp.maximum(i, first_q(j))))
    krow = pl.BlockSpec((tk, AUG), lambda j, i: (j, 0))
    kcol = pl.BlockSpec((AUG, tk), lambda j, i: (0, j))
    kblk = lambda w: pl.BlockSpec((tk, w), lambda j, i: (j, 0))
    return pl.pallas_call(
        body, grid=(nk, nq),
        in_specs=[qrow, qcol, krow, kcol, krow, qrow, qcol],
        out_specs=[pl.BlockSpec(memory_space=pl.ANY), kblk(768), kblk(768), kblk(LANE)],
        out_shape=[SDS((AUG, s), F32), SDS((s, 768), F32), SDS((s, 768), BF16), SDS((s, LANE), F32)],
        scratch_shapes=[pltpu.VMEM((AUG, s), F32), pltpu.VMEM((tk, AUG), F32), pltpu.VMEM((tk, AUG), F32),
                        pltpu.SemaphoreType.DMA],
        name="fox_bwd", compiler_params=_params(("arbitrary", "arbitrary")))(qb, qbt, ka, kat, va, dya, dyat)


def _fox2_bwd_post(proj, fbl, qg, kg, bfor, bd, dqa, dkn, dck, *, tr):
    s = proj.shape[0]
    nb = s // tr
    triu = jnp.asarray(np.triu(np.ones((tr, tr), np.float32)))
    rev = lambda i: nb - 1 - i

    def body(q_ref, k_ref, fb_ref, qg_ref, kg_ref, bf_ref, bd_ref, tri_ref, dqa_ref, dkn_ref, dck_ref,
             dq_ref, dk_ref, dfb_ref, dqg_ref, dkg_ref, dbf_ref, carry):
        @pl.when(pl.program_id(0) == 0)
        def _():
            carry[...] = jnp.zeros_like(carry)
            dqg_ref[...] = jnp.zeros_like(dqg_ref)
            dkg_ref[...] = jnp.zeros_like(dkg_ref)
            dbf_ref[...] = jnp.zeros_like(dbf_ref)

        bd_v = bd_ref[...]
        dqw = jnp.transpose(dqa_ref[...])
        _, vjp_q = jax.vjp(lambda q, g: _qn_fn(q, g, bd_v), q_ref[...], qg_ref[...])
        dq, dqg = vjp_q(_compact(dqw))
        _, vjp_k = jax.vjp(lambda k, g: _kn_fn(k, g, bd_v), k_ref[...], kg_ref[...])
        dk, dkg = vjp_k(dkn_ref[...])
        dq_ref[...] = dq.astype(BF16)
        dk_ref[...] = dk.astype(BF16)
        dqg_ref[...] += dqg
        dkg_ref[...] += dkg

        dc = _lane_of_heads(dqw, COL_A) + dck_ref[...]
        dlogf = jnp.dot(tri_ref[...], dc, precision=HI, preferred_element_type=F32) + carry[...]
        carry[...] = dlogf[0:1, :]
        lane = lax.broadcasted_iota(jnp.int32, (tr, LANE), 1)
        xf = fb_ref[...] + bf_ref[...]
        dfb = jnp.where(lane < N_FORGET, dlogf * jax.nn.sigmoid(-xf), 0.0)
        dfb_ref[...] = dfb.astype(BF16)
        dbf_ref[...] += jnp.sum(dfb, axis=0, keepdims=True)

    rb = lambda w: pl.BlockSpec((tr, w), lambda i: (rev(i), 0))
    wn = lambda w, off: pl.BlockSpec((pl.Element(tr), pl.Element(w)), lambda i: (rev(i) * tr, off))
    return pl.pallas_call(
        body, grid=(nb,),
        in_specs=[wn(768, QB), wn(768, KB), rb(LANE), _const((1, 768)), _const((1, 768)), _const((1, LANE)),
                  _const((768, 768)), _const((tr, tr)), pl.BlockSpec((AUG, tr), lambda i: (0, rev(i))), rb(768), rb(LANE)],
        out_specs=[rb(768), rb(768), rb(LANE), _const((1, 768)), _const((1, 768)), _const((1, LANE))],
        out_shape=[SDS((s, 768), BF16), SDS((s, 768), BF16), SDS((s, LANE), BF16), SDS((1, 768), F32),
                   SDS((1, 768), F32), SDS((1, LANE), F32)],
        scratch_shapes=[pltpu.VMEM((1, LANE), F32)], name="fox_bwd_post",
        compiler_params=_params(("arbitrary",)))(proj, proj, fbl, qg, kg, bfor, bd, triu, dqa, dkn, dck)


def _merge_specs(tr):
    row = lambda w: pl.BlockSpec((tr, w), lambda i, j: (i, 0))
    shard = lambda r: pl.BlockSpec((None, r, 512), lambda i, j: (j, 0, 0))
    gate = lambda b: pl.BlockSpec((tr, 512), lambda i, j: (i, (GATE + 2048 * b) // 512 + j))
    return [row(768), row(768), row(512), shard(768), shard(768), shard(512), gate(0), gate(1), gate(2)]


def _merge_fwd(proj, ga, gb, gc, wa, wb, wc, *, tr):
    s = proj.shape[0]

    def body(ga_ref, gb_ref, gc_ref, wa_ref, wb_ref, wc_ref, l0_ref, l1_ref, l2_ref, y_ref):
        ua = _dot(ga_ref[...], wa_ref[...])
        ub = _dot(gb_ref[...], wb_ref[...])
        uc = _dot(gc_ref[...], wc_ref[...])
        y = jax.nn.sigmoid(l0_ref[...]) * ua + jax.nn.sigmoid(l1_ref[...]) * ub + jax.nn.sigmoid(l2_ref[...]) * uc
        y_ref[...] = y.astype(BF16)

    return pl.pallas_call(
        body, grid=(s // tr, N_CHIPS), in_specs=_merge_specs(tr),
        out_specs=pl.BlockSpec((tr, 512), lambda i, j: (i, j)), out_shape=SDS((s, D_MODEL), BF16), name="merge_fwd",
        compiler_params=_params(("parallel", "arbitrary")))(ga, gb, gc, wa, wb, wc, proj, proj, proj)


def _merge_bwd(proj, ga, gb, gc, wa, wb, wc, dy, *, tr):
    s = proj.shape[0]

    def body(ga_ref, gb_ref, gc_ref, wa_ref, wb_ref, wc_ref, l0_ref, l1_ref, l2_ref, dy_ref,
             dl0_ref, dl1_ref, dl2_ref, dua_ref, dub_ref, duc_ref, dga_ref, dgb_ref, dgc_ref):
        j = pl.program_id(1)
        dyv = dy_ref[...]

        @pl.when(j == 0)
        def _():
            dga_ref[...] = jnp.zeros_like(dga_ref)
            dgb_ref[...] = jnp.zeros_like(dgb_ref)
            dgc_ref[...] = jnp.zeros_like(dgc_ref)

        for g_ref, w_ref, l_ref, dl_ref, du_ref, dg_ref in (
                (ga_ref, wa_ref, l0_ref, dl0_ref, dua_ref, dga_ref),
                (gb_ref, wb_ref, l1_ref, dl1_ref, dub_ref, dgb_ref),
                (gc_ref, wc_ref, l2_ref, dl2_ref, duc_ref, dgc_ref)):
            w = w_ref[...]
            u = _dot(g_ref[...], w)
            sg = jax.nn.sigmoid(l_ref[...])
            dl_ref[...] = (dyv * u * sg * (1.0 - sg)).astype(BF16)
            du = (dyv * sg).astype(BF16)
            du_ref[...] = du
            dg_ref[...] += _dot_nt(du, w)

    blk = pl.BlockSpec((tr, 512), lambda i, j: (i, j))
    row = lambda w: pl.BlockSpec((tr, w), lambda i, j: (i, 0))
    big = SDS((s, D_MODEL), BF16)
    return pl.pallas_call(
        body, grid=(s // tr, N_CHIPS), in_specs=_merge_specs(tr) + [blk],
        out_specs=[blk] * 6 + [row(768), row(768), row(512)],
        out_shape=[big] * 6 + [SDS((s, 768), F32), SDS((s, 768), F32), SDS((s, 512), F32)], name="merge_bwd",
        compiler_params=_params(("parallel", "arbitrary")))(ga, gb, gc, wa, wb, wc, proj, proj, proj, dy)


def _out_loss(y, wo, x, tgt, *, tr, tn):
    s = x.shape[0]

    def body(y_ref, w_ref, x_ref, t_ref, d_ref, db_ref, sq_ref):
        @pl.when((pl.program_id(0) == 0) & (pl.program_id(1) == 0))
        def _():
            sq_ref[...] = jnp.zeros_like(sq_ref)

        out = x_ref[...] + _dot(y_ref[...], w_ref[...])
        diff = out - t_ref[...]
        sq_ref[...] += jnp.sum(diff * diff, axis=0, keepdims=True)
        d = diff * (1.0 / D_MODEL)
        d_ref[...] = d
        db_ref[...] = d.astype(BF16)

    blk = pl.BlockSpec((tr, tn), lambda i, j: (i, j))
    return pl.pallas_call(
        body, grid=(s // tr, D_MODEL // tn),
        in_specs=[pl.BlockSpec((tr, D_MODEL), lambda i, j: (i, 0)), pl.BlockSpec((D_MODEL, tn), lambda i, j: (0, j)), blk, blk],
        out_specs=[blk, blk, _const((1, tn))],
        out_shape=[SDS((s, D_MODEL), F32), SDS((s, D_MODEL), BF16), SDS((1, tn), F32)], name="out_loss",
        compiler_params=_params(("arbitrary", "arbitrary")))(y, wo, x, tgt)


def _tile_gain(g, reps):
    return jnp.tile(g.reshape(1, -1), (1, reps))


def _pad_lane(v):
    v = v.reshape(1, -1)
    return jnp.pad(v, ((0, 0), (0, LANE - v.shape[1])))


def _local_step(x, mem, tgt, w_main, w_fb, w_small, norm_gain, mem_norm_gain, b_forget,
                q_gain_a, k_gain_a, sinks_a, q_gain_b, k_gain_b, q_gain_c, k_gain_c, core=None):
    s = x.shape[0]
    tr = min(512, s)
    bd64 = _block_diag(768, HEAD_DIM)
    bd128 = _block_diag(512, C_HEAD_DIM)
    hsum = _head_sum(768, HEAD_DIM)
    qga, kga = _tile_gain(q_gain_a, 12), _tile_gain(k_gain_a, 4)
    qgb, kgb = _tile_gain(q_gain_b, 12), _tile_gain(k_gain_b, 12)
    qgc, kgc = _tile_gain(q_gain_c, 4), _tile_gain(k_gain_c, 4)
    sinks = _pad_lane(sinks_a)
    bfor = _pad_lane(b_forget)

    hn = _rms_fwd(x, norm_gain, tr=tr, name="rms_x")
    on_mesh = core is not None
    if on_mesh:
        proj, (gathered,) = _matmul(hn, w_main, dims="nn", out_dtype=F32, tm=1024, tn=512, tk=D_MODEL, name="proj_main",
                                    comms=[_gather_comm(list(w_small))])
        w_mk, wa, wb, wc, wo = gathered
        w_mk, wo = w_mk.reshape(D_MODEL, 1024), wo.reshape(D_MODEL, D_MODEL)
    else:
        proj = _matmul(hn, w_main, dims="nn", out_dtype=F32, tm=1024, tn=512, tk=D_MODEL, name="proj_main")
        w_mk, wa, wb, wc, wo = w_small
    fbl = _matmul(hn, w_fb, dims="nn", out_dtype=F32, tm=1024, tn=LANE, tk=D_MODEL, name="proj_forget")
    memn = _rms_fwd(mem, mem_norm_gain, tr=mem.shape[0], name="rms_mem")
    mkv = _matmul(memn, w_mk, dims="nn", out_dtype=F32, tm=256, tn=512, tk=D_MODEL, name="mem_kv")

    swa_bias = _swa_bias()
    ga = _swa_fwd(proj, qga, kga, sinks, bd64, swa_bias)
    ea, eb, ones_a, ones_b = _expand_mats()
    tf = min(256, s)
    qat, ka, kat, va, vat, qn, cfox = _fox2_prep(proj, fbl, qgb, kgb, bfor, bd64, ea, eb, ones_a, ones_b, tr=tf)
    gb, yb, lse = _fox2_fwd(proj, qat, ka, vat)
    gc = _mem_fwd(proj, mkv, qgc, kgc, bd128, tr=tr)
    y = _merge_fwd(proj, ga, gb, gc, wa, wb, wc, tr=tr)
    dout, dout_b, sq = _out_loss(y, wo, x, tgt, tr=tr, tn=512)

    d_wo = _matmul(y, dout_b, dims="tn", out_dtype=F32, tm=1024, tn=512, tk=2048, name="dw_out")
    dy = _matmul(dout_b, wo, dims="nt", out_dtype=F32, tm=1024, tn=512, tk=D_MODEL, name="dy")
    dl0, dl1, dl2, dua, dub, duc, dga, dgb, dgc = _merge_bwd(proj, ga, gb, gc, wa, wb, wc, dy, tr=tr)
    d_wa = _matmul(ga, dua, dims="tn", out_dtype=F32, tm=768, tn=512, tk=2048, name="dw_branch_a")
    d_wb = _matmul(gb, dub, dims="tn", out_dtype=F32, tm=768, tn=512, tk=2048, name="dw_branch_b")
    d_wc = _matmul(gc, duc, dims="tn", out_dtype=F32, tm=512, tn=512, tk=2048, name="dw_branch_c")

    dcur, dprev, d_qga, d_kga, d_sinks = _swa_bwd(proj, qga, kga, sinks, bd64, swa_bias, dga)
    dproj_a = _swa_combine(dcur, dprev)

    qab, qabt, dya, dyat, dzb = _fox2_bwd_pre(proj, yb, dgb, qn, cfox, lse, hsum, ea, ones_b, tr=tf)
    dqa, dkn, dvb, dck = _fox2_bwd(qab, qabt, ka, kat, va, dya, dyat)
    dqb, dkb, dfb, d_qgb, d_kgb, d_bf = _fox2_bwd_post(proj, fbl, qgb, kgb, bfor, bd64, dqa, dkn, dck, tr=tf)

    dproj_c, dmkv, d_qgc, d_kgc = _mem_bwd(proj, mkv, qgc, kgc, bd128, dgc, tr=tr)
    dmkv_b = dmkv.astype(BF16)
    d_wmk = _matmul(memn, dmkv_b, dims="tn", out_dtype=F32, tm=1024, tn=512, tk=256, name="dw_mem_kv")
    dmemn = _matmul(dmkv_b, w_mk, dims="nt", out_dtype=F32, tm=256, tn=512, tk=1024, name="dmemn")
    (d_mem_gain,) = _rms_bwd(mem, mem_norm_gain, dmemn, None, tr=mem.shape[0], name="rms_mem_bwd")

    dproj = jnp.concatenate([dproj_a, dqb, dkb, dvb, dzb, dproj_c, dl0, dl1, dl2], axis=1)
    dhn_f = _matmul(dfb, w_fb, dims="nt", out_dtype=F32, tm=1024, tn=512, tk=LANE, name="dhn_forget")
    d_wfb = _matmul(hn, dfb, dims="tn", out_dtype=F32, tm=1024, tn=LANE, tk=512, name="dw_forget")
    big = {}
    if on_mesh:
        half = D_MODEL // 2
        c0 = core[0]
        hn_other = lax.dynamic_slice(hn, (0, (1 - c0) * half), (s, half))
        hn_own = lax.dynamic_slice(hn, (0, c0 * half), (s, half))
        g1, k1 = [d_wmk, d_wa, d_wb, d_wc, d_wo], [2, 3, 4, 5, 6]
        d_other, (got1,) = _matmul(hn_other, dproj, dims="tn", out_dtype=F32, tm=1024, tn=512, tk=2048,
                                   name="dw_main_other", comms=[_exchange_comm(g1, k1)])
        h1 = [_add_half(g, got, core, HALF_AXIS[k], name=f"add_half_{k}") for g, got, k in zip(g1, got1, k1)]
        d_own, (got0, parts1) = _matmul(
            hn_own, dproj, dims="tn", out_dtype=F32, tm=1024, tn=512, tk=2048, name="dw_main_own",
            comms=[_exchange_comm([d_other, d_wfb], [0, 1], whole=(0,)), _scatter_comm(h1, k1)])
        h0 = [_add_pair(d_own, got0[0], name="add_pair_main"), _add_half(d_wfb, got0[1], core, 0, name="add_half_1")]
        sums1 = [_sum4(p, name=f"sum4_{k}") for p, k in zip(parts1, k1)]
        dhn, (parts0, theirs1) = _matmul(dproj, w_main, dims="nt", out_dtype=F32, tm=1024, tn=512, tk=2048, name="dhn",
                                         add=dhn_f, comms=[_scatter_comm(h0, [0, 1]), _swap_comm(sums1)])
        sums0 = [_sum4(p, name=f"sum4_{k}") for p, k in zip(parts0, (0, 1))]
        theirs0 = _run_comm(_swap_comm(sums0), "swap_halves")
        big = dict(sums=sums0 + sums1, theirs=list(theirs0) + list(theirs1))
    else:
        dhn = _matmul(dproj, w_main, dims="nt", out_dtype=F32, tm=1024, tn=512, tk=2048, name="dhn", add=dhn_f)
        d_wmain = _matmul(hn, dproj, dims="tn", out_dtype=F32, tm=1024, tn=512, tk=2048, name="dw_main")
        big = dict(d_wmain=d_wmain, d_wfb=d_wfb, d_wmk=d_wmk, d_wa=d_wa, d_wb=d_wb, d_wc=d_wc, d_wo=d_wo)
    grad_x, d_gain = _rms_bwd(x, norm_gain, dhn, dout, tr=tr, name="rms_x_bwd")

    fold = lambda g, reps: jnp.sum(g.reshape(reps, -1), axis=0, keepdims=True)
    return dict(
        sq=sq, grad_x=grad_x, **big,
        d_gain=d_gain, d_mem_gain=d_mem_gain, d_bf=d_bf[:, :N_FORGET],
        d_qga=fold(d_qga, 12), d_kga=fold(d_kga, 4), d_sinks=d_sinks[:, :A_HEADS],
        d_qgb=fold(d_qgb, 12), d_kgb=fold(d_kgb, 12), d_qgc=fold(d_qgc, 4), d_kgc=fold(d_kgc, 4))


PACK_ROWS = 256
FORGET_IN_SHARD = FORGET_COL - SHARD_COLS
AFTER_FORGET = FORGET_COL - SLAB_START[1]
END_CHIP1 = 2 * SHARD_COLS - N_FORGET - SLAB_START[1]


def _pack_w_in(chip, w):
    rows = w.shape[0]
    tr = PACK_ROWS

    def body(k_ref, w_ref, o_ref, scr):
        scr[...] = jnp.zeros_like(scr)
        scr[:, pl.ds(0, SHARD_COLS)] = w_ref[...]
        v = scr[...]
        k = k_ref[0]
        col = lax.broadcasted_iota(jnp.int32, (tr, SLAB), 1)
        no_forget = jnp.zeros((tr, LANE), BF16)

        @pl.when(k == 0)
        def _():
            o_ref[:, 0:SLAB] = v.astype(BF16)
            o_ref[:, SLAB:] = no_forget

        @pl.when(k == 1)
        def _():
            before = pltpu.roll(v, SLAB_SHIFT[1], axis=1)
            after = pltpu.roll(v, SLAB - (N_FORGET - SLAB_SHIFT[1]), axis=1)
            slab = jnp.where(col < AFTER_FORGET, before, jnp.where(col < END_CHIP1, after, 0.0))
            o_ref[:, 0:SLAB] = slab.astype(BF16)
            f = pltpu.roll(v, SLAB - FORGET_IN_SHARD, axis=1)[:, :LANE]
            o_ref[:, SLAB:] = jnp.where(col[:, :LANE] < N_FORGET, f, 0.0).astype(BF16)

        for kk in (2, 3):
            @pl.when(k == kk)
            def _(kk=kk):
                o_ref[:, 0:SLAB] = pltpu.roll(v, SLAB_SHIFT[kk], axis=1).astype(BF16)
                o_ref[:, SLAB:] = no_forget

    return pl.pallas_call(
        body, grid_spec=pltpu.PrefetchScalarGridSpec(
            num_scalar_prefetch=1, grid=(rows // tr,),
            in_specs=[pl.BlockSpec((tr, SHARD_COLS), lambda i, k: (i, 0))],
            out_specs=pl.BlockSpec((None, tr, SLAB + LANE), lambda i, k: (k[0], i, 0)),
            scratch_shapes=[pltpu.VMEM((tr, SLAB), F32)]),
        out_shape=SDS((N_CHIPS, rows, SLAB + LANE), BF16), name="pack_w_in",
        compiler_params=_params(("arbitrary",)))(chip, w)


def _merge_slabs(g):
    rows = g.shape[1]
    tr = PACK_ROWS
    t = [s // LANE for s in SLAB_START]
    n_t = SLAB // LANE

    def body(g_ref, m_ref, f_ref):
        for k in range(N_CHIPS):
            lo = t[k] + (1 if k > 0 else 0)
            hi = t[k + 1] if k + 1 < N_CHIPS else t[k] + n_t
            m_ref[:, lo * LANE:hi * LANE] = g_ref[k, :, (lo - t[k]) * LANE:(hi - t[k]) * LANE]
            if k + 1 < N_CHIPS:
                a = g_ref[k, :, (hi - t[k]) * LANE:(hi - t[k] + 1) * LANE].astype(F32)
                b = g_ref[k + 1, :, 0:LANE].astype(F32)
                m_ref[:, hi * LANE:(hi + 1) * LANE] = (a + b).astype(BF16)
        f_ref[...] = g_ref[1, :, SLAB:]

    return pl.pallas_call(
        body, grid=(rows // tr,),
        in_specs=[pl.BlockSpec((N_CHIPS, tr, SLAB + LANE), lambda i: (0, i, 0))],
        out_specs=[_rowblk(tr, P_MAIN), _rowblk(tr, LANE)],
        out_shape=[SDS((rows, P_MAIN), BF16), SDS((rows, LANE), BF16)], name="merge_slabs",
        compiler_params=_params(("parallel",)))(g)


def _adamw_math(w, g, m, v):
    nm = ADAM_B1 * m + (1.0 - ADAM_B1) * g
    nv = ADAM_B2 * v + (1.0 - ADAM_B2) * (g * g)
    m_hat = nm / (1.0 - ADAM_B1 ** ADAM_STEP)
    v_hat = nv / (1.0 - ADAM_B2 ** ADAM_STEP)
    delta = -ADAM_LR * (m_hat / (jnp.sqrt(v_hat) + ADAM_EPS) + ADAM_WD * w)
    return delta, nm, nv


def _adamw(g, w, m, v, *, tr, name):
    rows, cols = w.shape
    tr = min(tr, rows)

    def body(g_ref, w_ref, m_ref, v_ref, d_ref, nm_ref, nv_ref):
        d, nm, nv = _adamw_math(w_ref[...], g_ref[...], m_ref[...], v_ref[...])
        d_ref[...] = d
        nm_ref[...] = nm
        nv_ref[...] = nv

    spec = _rowblk(tr, cols)
    return pl.pallas_call(
        body, grid=(rows // tr,), in_specs=[spec] * 4, out_specs=[spec] * 3,
        out_shape=[SDS((rows, cols), F32)] * 3, name=name, compiler_params=_params(("parallel",)))(g, w, m, v)


def _adamw_w_in(chip_core, slab_mine, slab_theirs, forget_mine, forget_theirs, w, m, v):
    rows = w.shape[0]
    tr = PACK_ROWS // 2
    nbh = rows // 2 // tr

    def body(k_ref, sa_ref, sb_ref, fa_ref, fb_ref, w_ref, m_ref, v_ref, g_ref, d_ref, nm_ref, nv_ref):
        use_mine = pl.program_id(0) // nbh == k_ref[1]
        sl = jnp.where(use_mine, sa_ref[...], sb_ref[...])
        f_tile = jnp.where(use_mine, fa_ref[...], fb_ref[...])
        k = k_ref[0]

        def emit(wide):
            g = wide[:, :SHARD_COLS]
            g_ref[...] = g
            d, nm, nv = _adamw_math(w_ref[...], g, m_ref[...], v_ref[...])
            d_ref[...] = d
            nm_ref[...] = nm
            nv_ref[...] = nv

        @pl.when(k == 0)
        def _():
            emit(sl)

        @pl.when(k == 1)
        def _():
            col = lax.broadcasted_iota(jnp.int32, (tr, SLAB), 1)
            before = pltpu.roll(sl, SLAB - SLAB_SHIFT[1], axis=1)
            after = pltpu.roll(sl, N_FORGET - SLAB_SHIFT[1], axis=1)
            wide_f = jnp.concatenate([f_tile, jnp.zeros((tr, SLAB - LANE), F32)], axis=1)
            forget = pltpu.roll(wide_f, FORGET_IN_SHARD, axis=1)
            emit(jnp.where(col < FORGET_IN_SHARD, before, jnp.where(col < FORGET_IN_SHARD + N_FORGET, forget, after)))

        for kk in (2, 3):
            @pl.when(k == kk)
            def _(kk=kk):
                emit(pltpu.roll(sl, SLAB - SLAB_SHIFT[kk], axis=1))

    nat = pl.BlockSpec((tr, SHARD_COLS), lambda i, k: (i, 0))
    half = lambda width: pl.BlockSpec((tr, width), lambda i, k: (i % nbh, 0))
    return pl.pallas_call(
        body, grid_spec=pltpu.PrefetchScalarGridSpec(
            num_scalar_prefetch=1, grid=(rows // tr,),
            in_specs=[half(SLAB), half(SLAB), half(LANE), half(LANE), nat, nat, nat],
            out_specs=[nat] * 4),
        out_shape=[SDS((rows, SHARD_COLS), F32)] * 4, name="adamw_w_in",
        compiler_params=_params(("arbitrary",)))(chip_core, slab_mine, slab_theirs, forget_mine, forget_theirs, w, m, v)


ANY = pl.BlockSpec(memory_space=pl.ANY)
HALF_AXIS = (0, 0, 1, 0, 0, 0, 1)


def _me():
    return lax.axis_index("x"), lax.axis_index("y"), lax.axis_index("c")


def _half(ref, which, axis):
    n = ref.shape[axis] // 2
    sl = pl.ds(which * n, n)
    return ref.at[sl] if axis == 0 else ref.at[:, sl]


def _piece(t, ref, j):
    if t == 0:
        return ref.at[:, pl.ds(SLAB_START[j], SLAB)]
    if t == 1:
        return ref
    if t in (2, 6):
        return ref.at[pl.ds(512 * j, 512)]
    return ref.at[:, pl.ds(512 * j, 512)]


def _piece_shape(t, shape):
    if t == 0:
        return (shape[0], SLAB)
    if t == 1:
        return shape
    if t in (2, 6):
        return (512, shape[1])
    return (shape[0], 512)


def _gather_plan(ins, outs, own_slot_in_src):
    x, y, c = _me()
    k = 2 * x + y
    sib = (x, y, 1 - c)
    chips = [(1 - x, y), (x, 1 - y), (1 - x, 1 - y)]
    n = len(outs)

    def rows(t, which):
        h = outs[t].shape[1] // 2
        return pl.ds(which * h, h)

    def mine(t):
        return ins[t].at[k, rows(t, c)] if own_slot_in_src else ins[t].at[rows(t, c)]

    def first(t, j, sems):
        chip = chips[j]
        return pltpu.make_async_remote_copy(
            src_ref=mine(t), dst_ref=outs[t].at[k, rows(t, c)], send_sem=sems[0].at[t, j], recv_sem=sems[1].at[t, j],
            device_id=(chip[0], chip[1], c), device_id_type=MESH)

    def landed(t, j, sems):
        chip = chips[j]
        return pltpu.make_async_remote_copy(
            src_ref=mine(t), dst_ref=outs[t].at[2 * chip[0] + chip[1], rows(t, c)], send_sem=sems[0].at[t, j],
            recv_sem=sems[1].at[t, j], device_id=(chip[0], chip[1], c), device_id_type=MESH)

    def passed(t, j, which, sems):
        chip = chips[j]
        blk = outs[t].at[2 * chip[0] + chip[1], rows(t, which)]
        return pltpu.make_async_remote_copy(
            src_ref=blk, dst_ref=blk, send_sem=sems[2].at[t, j], recv_sem=sems[3].at[t, j], device_id=sib,
            device_id_type=MESH)

    def start(sems):
        for j in range(3):
            for t in range(n):
                first(t, j, sems).start()

    def finish(sems):
        for j in range(3):
            for t in range(n):
                landed(t, j, sems).wait_recv()
                passed(t, j, c, sems).start()
        for j in range(3):
            for t in range(n):
                passed(t, j, 1 - c, sems).wait_recv()
        for j in range(3):
            for t in range(n):
                first(t, j, sems).wait_send()
                passed(t, j, c, sems).wait_send()

    return k, start, finish


def _all_gather_slabs(slabs):
    def body(in_ref, out_ref, *sems):
        _, start, finish = _gather_plan([in_ref], [out_ref], True)
        start(sems)
        finish(sems)

    return pl.pallas_call(
        body, in_specs=[ANY], out_specs=ANY, out_shape=SDS(slabs.shape, slabs.dtype),
        scratch_shapes=[pltpu.SemaphoreType.DMA((1, 3))] * 4, input_output_aliases={0: 0},
        name="all_gather_slabs")(slabs)


def _gather_comm(parts):
    n = len(parts)

    def start(ins, outs, sems):
        k, go, _ = _gather_plan(ins, outs, False)
        for t in range(n):
            pltpu.make_async_copy(ins[t], outs[t].at[k], sems[4].at[t]).start()
        go(sems)

    def finish(ins, outs, sems):
        k, _, done = _gather_plan(ins, outs, False)
        done(sems)
        for t in range(n):
            pltpu.make_async_copy(ins[t], outs[t].at[k], sems[4].at[t]).wait()

    return _Comm(parts, [SDS((N_CHIPS,) + p.shape, p.dtype) for p in parts],
                 [pltpu.SemaphoreType.DMA((n, 3))] * 4 + [pltpu.SemaphoreType.DMA((n,))], start, finish)


def _exchange_comm(arrs, kinds, whole=()):
    n = len(arrs)

    def copies(ins, outs, sems):
        x, y, c = _me()
        return [pltpu.make_async_remote_copy(
            src_ref=ins[t] if t in whole else _half(ins[t], 1 - c, HALF_AXIS[kinds[t]]), dst_ref=outs[t],
            send_sem=sems[0].at[t], recv_sem=sems[1].at[t], device_id=(x, y, 1 - c), device_id_type=MESH)
            for t in range(n)]

    def start(ins, outs, sems):
        for cp in copies(ins, outs, sems):
            cp.start()

    def finish(ins, outs, sems):
        for cp in copies(ins, outs, sems):
            cp.wait()

    def hshape(t):
        s = list(arrs[t].shape)
        if t not in whole:
            s[HALF_AXIS[kinds[t]]] //= 2
        return SDS(tuple(s), arrs[t].dtype)

    return _Comm(arrs, [hshape(t) for t in range(n)], [pltpu.SemaphoreType.DMA((n,))] * 2, start, finish)


def _add_half(full, got, core, axis, *, name):
    r, c = got.shape
    br, bc = (256 if r % 256 == 0 else 128), min(2048, c)
    off_r = (r // br) if axis == 0 else 0
    off_c = (c // bc) if axis == 1 else 0

    def body(c_ref, a_ref, b_ref, o_ref):
        o_ref[...] = (a_ref[...] + b_ref[...]).astype(BF16)

    return pl.pallas_call(
        body, grid_spec=pltpu.PrefetchScalarGridSpec(
            num_scalar_prefetch=1, grid=(r // br, c // bc),
            in_specs=[pl.BlockSpec((br, bc), lambda i, j, cr: (i + cr[0] * off_r, j + cr[0] * off_c)),
                      pl.BlockSpec((br, bc), lambda i, j, cr: (i, j))],
            out_specs=pl.BlockSpec((br, bc), lambda i, j, cr: (i, j))),
        out_shape=SDS((r, c), BF16), name=name, compiler_params=_params(("parallel", "parallel")))(core, full, got)


def _add_pair(a, b, *, name):
    r, c = a.shape
    br, bc = 256, min(2048, c)

    def body(a_ref, b_ref, o_ref):
        o_ref[...] = (a_ref[...] + b_ref[...]).astype(BF16)

    spec = pl.BlockSpec((br, bc), lambda i, j: (i, j))
    return pl.pallas_call(body, grid=(r // br, c // bc), in_specs=[spec, spec], out_specs=spec,
                          out_shape=SDS((r, c), BF16), name=name, compiler_params=_params(("parallel", "parallel")))(a, b)


def _scatter_comm(halves, kinds):
    n = len(halves)

    def plan(ins, outs, sems):
        send, recv, lsem = sems
        x, y, c = _me()
        k = 2 * x + y

        def to_chip(t, j):
            return pltpu.make_async_remote_copy(
                src_ref=_piece(kinds[t], ins[t], j), dst_ref=outs[t].at[k], send_sem=send.at[t, j],
                recv_sem=recv.at[t, k], device_id=(j // 2, j % 2, c), device_id_type=MESH)

        def from_chip(t, j):
            return pltpu.make_async_remote_copy(
                src_ref=_piece(kinds[t], ins[t], j), dst_ref=outs[t].at[j], send_sem=send.at[t, j],
                recv_sem=recv.at[t, j], device_id=(j // 2, j % 2, c), device_id_type=MESH)

        def own(t, j):
            return pltpu.make_async_copy(_piece(kinds[t], ins[t], j), outs[t].at[j], lsem.at[t])

        return k, to_chip, from_chip, own

    def start(ins, outs, sems):
        k, to_chip, _, own = plan(ins, outs, sems)
        for j in range(N_CHIPS):
            @pl.when(k != j)
            def _(j=j):
                for t in range(n):
                    to_chip(t, j).start()

            @pl.when(k == j)
            def _(j=j):
                for t in range(n):
                    own(t, j).start()

    def finish(ins, outs, sems):
        k, to_chip, from_chip, own = plan(ins, outs, sems)
        for j in range(N_CHIPS):
            @pl.when(k != j)
            def _(j=j):
                for t in range(n):
                    from_chip(t, j).wait_recv()
                for t in range(n):
                    to_chip(t, j).wait_send()

            @pl.when(k == j)
            def _(j=j):
                for t in range(n):
                    own(t, j).wait()

    return _Comm(halves, [SDS((N_CHIPS,) + _piece_shape(kinds[t], halves[t].shape), halves[t].dtype) for t in range(n)],
                 [pltpu.SemaphoreType.DMA((n, N_CHIPS))] * 2 + [pltpu.SemaphoreType.DMA((n,))], start, finish)


def _sum4(p, *, name):
    _, r, c = p.shape
    br = 256 if r % 256 == 0 else 128

    def body(p_ref, o_ref):
        o_ref[...] = ((p_ref[0].astype(F32) + p_ref[1].astype(F32)) + p_ref[2].astype(F32)) + p_ref[3].astype(F32)

    return pl.pallas_call(
        body, grid=(r // br,), in_specs=[pl.BlockSpec((N_CHIPS, br, c), lambda i: (0, i, 0))],
        out_specs=_rowblk(br, c), out_shape=SDS((r, c), F32), name=name, compiler_params=_params(("parallel",)))(p)


def _swap_comm(sums):
    return _exchange_comm(sums, [None] * len(sums), whole=tuple(range(len(sums))))


def _adamw_halves(mine, theirs, core, w, m, v, *, axis, tr, name):
    rows, cols = w.shape

    if axis == 0:
        nbh = rows // 2 // tr
        g_spec = pl.BlockSpec((tr, cols), lambda i, cr: (i % nbh, 0))
    else:
        g_spec = pl.BlockSpec((tr, cols // 2), lambda i, cr: (i, 0))

    def body(c_ref, a_ref, b_ref, w_ref, m_ref, v_ref, g_ref, d_ref, nm_ref, nv_ref):
        a, b = a_ref[...], b_ref[...]
        if axis == 0:
            g = jnp.where(pl.program_id(0) // nbh == c_ref[0], a, b)
        else:
            low = c_ref[0] == 0
            g = jnp.concatenate([jnp.where(low, a, b), jnp.where(low, b, a)], axis=1)
        g_ref[...] = g
        d, nm, nv = _adamw_math(w_ref[...], g, m_ref[...], v_ref[...])
        d_ref[...] = d
        nm_ref[...] = nm
        nv_ref[...] = nv

    nat = pl.BlockSpec((tr, cols), lambda i, cr: (i, 0))
    return pl.pallas_call(
        body, grid_spec=pltpu.PrefetchScalarGridSpec(
            num_scalar_prefetch=1, grid=(rows // tr,), in_specs=[g_spec, g_spec, nat, nat, nat], out_specs=[nat] * 4),
        out_shape=[SDS((rows, cols), F32)] * 4, name=name, compiler_params=_params(("arbitrary",)))(
            core, mine, theirs, w, m, v)


SMALL_ROWS, SMALL_COLS = 8, 1024


def _pack_small(vs):
    flat = jnp.concatenate([v.reshape(-1) for v in vs])
    return jnp.pad(flat, (0, SMALL_ROWS * SMALL_COLS - flat.shape[0])).reshape(SMALL_ROWS, SMALL_COLS)


def _unpack_small(packed, sizes):
    flat = packed.reshape(-1)
    out, o = [], 0
    for n in sizes:
        out.append(flat[o:o + n].reshape(1, n))
        o += n
    return out


def _all_reduce_small(v):
    n_dev = 8

    def body(v_ref, o_ref, land, send, recv):
        x, y, c = _me()
        me = 4 * x + 2 * y + c
        land[me] = v_ref[...]
        cps = []
        for r in range(1, n_dev):
            fx, fy, fc = (r >> 2) & 1, (r >> 1) & 1, r & 1
            peer = (x ^ fx, y ^ fy, c ^ fc)
            cps.append(pltpu.make_async_remote_copy(
                src_ref=v_ref, dst_ref=land.at[me], send_sem=send.at[r - 1], recv_sem=recv.at[r - 1],
                device_id=peer, device_id_type=MESH))
        for cp in cps:
            cp.start()
        for r in range(1, n_dev):
            fx, fy, fc = (r >> 2) & 1, (r >> 1) & 1, r & 1
            src = 4 * (x ^ fx) + 2 * (y ^ fy) + (c ^ fc)
            pltpu.make_async_remote_copy(
                src_ref=v_ref, dst_ref=land.at[src], send_sem=send.at[r - 1], recv_sem=recv.at[r - 1],
                device_id=(x ^ fx, y ^ fy, c ^ fc), device_id_type=MESH).wait_recv()
        for cp in cps:
            cp.wait_send()
        acc = land[0]
        for r in range(1, n_dev):
            acc = acc + land[r]
        o_ref[...] = acc

    vm = pl.BlockSpec(memory_space=pltpu.VMEM)
    return pl.pallas_call(
        body, in_specs=[vm], out_specs=vm, out_shape=SDS(v.shape, F32),
        scratch_shapes=[pltpu.VMEM((n_dev,) + v.shape, F32), pltpu.SemaphoreType.DMA((n_dev - 1,)),
                        pltpu.SemaphoreType.DMA((n_dev - 1,))],
        name="all_reduce_small")(v)


def kernel(x, mem, norm_gain, mem_norm_gain, w_in, b_forget, q_gain_a, k_gain_a, sinks_a, q_gain_b, k_gain_b, q_gain_c, k_gain_c, w_mem_kv, w_branch_a, w_branch_b, w_branch_c, w_out, loss_target, m_norm_gain, m_mem_norm_gain, m_w_in, m_b_forget, m_q_gain_a, m_k_gain_a, m_sinks_a, m_q_gain_b, m_k_gain_b, m_q_gain_c, m_k_gain_c, m_w_mem_kv, m_w_branch_a, m_w_branch_b, m_w_branch_c, m_w_out, v_norm_gain, v_mem_norm_gain, v_w_in, v_b_forget, v_q_gain_a, v_k_gain_a, v_sinks_a, v_q_gain_b, v_k_gain_b, v_q_gain_c, v_k_gain_c, v_w_mem_kv, v_w_branch_a, v_w_branch_b, v_w_branch_c, v_w_out):
    xi, yi, ci = lax.axis_index("x"), lax.axis_index("y"), lax.axis_index("c")
    chip = jnp.reshape(2 * xi + yi, (1,)).astype(jnp.int32)
    core = jnp.reshape(ci, (1,)).astype(jnp.int32)

    slabs = _pack_w_in(chip, w_in[0])
    mine = [w_mem_kv[0].astype(BF16), w_branch_a[0].astype(BF16), w_branch_b[0].astype(BF16),
            w_branch_c[0].astype(BF16), w_out[0].astype(BF16)]
    w_main, w_fb = _merge_slabs(_all_gather_slabs(slabs))

    r = _local_step(x[0], mem[0], loss_target[0], w_main, w_fb, mine, norm_gain, mem_norm_gain,
                    b_forget, q_gain_a, k_gain_a, sinks_a, q_gain_b, k_gain_b, q_gain_c, k_gain_c, core=core)
    sums, theirs = r["sums"], r["theirs"]

    small_names = ["d_gain", "d_mem_gain", "d_bf", "d_qga", "d_kga", "d_sinks", "d_qgb", "d_kgb", "d_qgc", "d_kgc"]
    loss_part = (0.5 / D_MODEL) * jnp.sum(r["sq"], axis=1, keepdims=True)
    packed = _pack_small([r[n] for n in small_names] + [loss_part])
    red = _all_reduce_small(packed)
    small_w = [norm_gain, mem_norm_gain, b_forget, q_gain_a, k_gain_a, sinks_a, q_gain_b, k_gain_b, q_gain_c, k_gain_c]
    small_m = [m_norm_gain, m_mem_norm_gain, m_b_forget, m_q_gain_a, m_k_gain_a, m_sinks_a, m_q_gain_b, m_k_gain_b,
               m_q_gain_c, m_k_gain_c]
    small_v = [v_norm_gain, v_mem_norm_gain, v_b_forget, v_q_gain_a, v_k_gain_a, v_sinks_a, v_q_gain_b, v_k_gain_b,
               v_q_gain_c, v_k_gain_c]
    sizes = [w.shape[1] for w in small_w]
    s_d, s_m, s_v = _adamw(red, _pack_small(small_w), _pack_small(small_m), _pack_small(small_v), tr=8, name="adamw_small")
    g_small = _unpack_small(red, sizes + [1])
    loss = g_small[-1].reshape(())
    d_small, m_small, v_small = _unpack_small(s_d, sizes), _unpack_small(s_m, sizes), _unpack_small(s_v, sizes)

    gw_in, dw_in, mw_in, vw_in = _adamw_w_in(jnp.concatenate([chip, core]), sums[0], theirs[0], sums[1], theirs[1],
                                             w_in[0], m_w_in[0], v_w_in[0])
    big = {}
    for t, nm, w, m, v in ((2, "w_mem_kv", w_mem_kv, m_w_mem_kv, v_w_mem_kv),
                           (3, "w_branch_a", w_branch_a, m_w_branch_a, v_w_branch_a),
                           (4, "w_branch_b", w_branch_b, m_w_branch_b, v_w_branch_b),
                           (5, "w_branch_c", w_branch_c, m_w_branch_c, v_w_branch_c),
                           (6, "w_out", w_out, m_w_out, v_w_out)):
        big[nm] = _adamw_halves(sums[t], theirs[t], core, w[0], m[0], v[0], axis=HALF_AXIS[t], tr=128,
                                name="adamw_" + nm)

    def collect(kind):
        sm = (g_small, d_small, m_small, v_small)[kind]
        win = (gw_in, dw_in, mw_in, vw_in)[kind]
        return ([sm[0], sm[1], win[None]] + [a for a in sm[2:10]]
                + [big[n][kind][None] for n in ("w_mem_kv", "w_branch_a", "w_branch_b", "w_branch_c", "w_out")])

    return (loss, r["grad_x"][None], *collect(0), *collect(1), *collect(2), *collect(3))
```

```python
import functools

import numpy as np
import jax
import jax.numpy as jnp
from jax import lax
from jax.experimental import pallas as pl
from jax.experimental.pallas import tpu as pltpu

F32 = jnp.float32
BF16 = jnp.bfloat16
HI = lax.Precision.HIGHEST
SDS = jax.ShapeDtypeStruct
MESH = pl.DeviceIdType.MESH

D_MODEL = 2048
HEAD_DIM = 64
A_HEADS = 12
A_GROUP = 3
B_HEADS = 12
C_HEADS = 4
C_HEAD_DIM = 128
WINDOW = 128
EPS = 1e-6
NEG = -1e30
LANE = 128

QA, KA, VA, ZA = 0, 768, 1024, 1280
QB, KB, VB, ZB = 2048, 2816, 3584, 4352
QC, ZC = 5120, 5632
GATE = 6144
P_MAIN = 12288
N_FORGET = 12
FORGET_COL = 5120
SHARD_COLS = 3075
SLAB = 3200
SLAB_START = (0, 3072, 6016, 9088)
SLAB_SHIFT = (0, 3, 122, 125)
N_CHIPS = 4

ADAM_LR = 0.001
ADAM_B1 = 0.9
ADAM_B2 = 0.999
ADAM_EPS = 1e-08
ADAM_WD = 0.01
ADAM_STEP = 10

VMEM_LIMIT = 56 * 1024 * 1024


def _params(sem, vmem=VMEM_LIMIT):
    return pltpu.CompilerParams(dimension_semantics=sem, vmem_limit_bytes=vmem)


def _win(tr, width, off):
    return pl.BlockSpec((pl.Element(tr), pl.Element(width)), lambda i, *_: (i * tr, off))


def _rowblk(tr, width):
    return pl.BlockSpec((tr, width), lambda i, *_: (i, 0))


def _const(shape):
    nd = len(shape)
    return pl.BlockSpec(shape, lambda *_: (0,) * nd)


def _rms(x, g):
    return x * lax.rsqrt(jnp.mean(x * x, axis=-1, keepdims=True) + EPS) * g


def _head_mean_impl(x2, bd):
    hi = x2.astype(BF16)
    lo = (x2 - hi.astype(F32)).astype(BF16)
    return _dot(hi, bd) + _dot(lo, bd)


@jax.custom_vjp
def _head_mean(x2, bd):
    return _head_mean_impl(x2, bd)


_head_mean.defvjp(lambda x2, bd: (_head_mean_impl(x2, bd), bd),
                  lambda bd, g: (_head_mean_impl(g, bd), jnp.zeros_like(bd)))


def _head_norm(x, g_tiled, bd):
    return x * lax.rsqrt(_head_mean(x * x, bd) + EPS) * g_tiled


def _silu(z):
    return z * jax.nn.sigmoid(z)


def _dot_nt(a, b):
    return lax.dot_general(a, b, (((1,), (1,)), ((), ())), preferred_element_type=F32)


def _dot_tn(a, b):
    return lax.dot_general(a, b, (((0,), (0,)), ((), ())), preferred_element_type=F32)


def _dot(a, b):
    return jnp.dot(a, b, preferred_element_type=F32)


def _swa_fn(qk, vz, qkp, vzp, qg, kg, sinks, bd, bias, first):
    q = _head_norm(qk[:, :768], qg, bd)
    k2 = jnp.concatenate([qkp[:, 768:], qk[:, 768:]], axis=0)
    k2 = _head_norm(k2, kg, bd[:256, :256])
    v2 = jnp.concatenate([vzp[:, :256], vz[:, :256]], axis=0)
    z = vz[:, 256:]
    cols = A_GROUP * WINDOW
    kj = lax.broadcasted_iota(jnp.int32, (2 * WINDOW, cols), 0)
    no_prev = kj < WINDOW * first.astype(jnp.int32)
    qtb = jnp.transpose(q).astype(BF16)
    kb = k2.astype(BF16)
    vtb = jnp.transpose(v2).astype(BF16)
    outs = [None] * A_HEADS
    for g in range(A_HEADS // A_GROUP):
        heads = [A_GROUP * g + u for u in range(A_GROUP)]
        qs = jnp.concatenate([qtb[64 * h:64 * h + 64, :] for h in heads], axis=1)
        s = _dot(kb[:, 64 * g:64 * g + 64], qs) * (HEAD_DIM ** -0.5) + bias[g]
        s = jnp.where(no_prev, NEG, s)
        sink = jnp.concatenate([jnp.broadcast_to(sinks[:, h:h + 1], (1, WINDOW)) for h in heads], axis=1)
        m = lax.stop_gradient(jnp.maximum(jnp.max(s, axis=0, keepdims=True), sink))
        p = jnp.exp(s - m)
        den = jnp.sum(p, axis=0, keepdims=True) + jnp.exp(sink - m)
        o = _dot(vtb[64 * g:64 * g + 64, :], (p * (1.0 / den)).astype(BF16))
        for u, h in enumerate(heads):
            outs[h] = o[:, WINDOW * u:WINDOW * u + WINDOW]
    return jnp.transpose(jnp.concatenate(outs, axis=0)) * _silu(z)


def _swa_bias():
    qi = np.arange(WINDOW)[None, :]
    kj = np.arange(2 * WINDOW)[:, None]
    rel = qi + WINDOW - kj
    valid = (rel >= 0) & (rel < WINDOW)
    out = np.zeros((A_HEADS // A_GROUP, 2 * WINDOW, A_GROUP * WINDOW), np.float32)
    for h in range(A_HEADS):
        slope = np.float32(2.0 ** (-8.0 * (h + 1) / A_HEADS))
        blk = np.where(valid, -slope * rel.astype(np.float32), np.float32(NEG))
        g, u = divmod(h, A_GROUP)
        out[g, :, WINDOW * u:WINDOW * u + WINDOW] = blk
    return jnp.asarray(out)


def _mem_fn(qz, mkv, qg, kg, bd):
    q = _head_norm(qz[:, :512], qg, bd).astype(BF16)
    k = _head_norm(mkv[:, :512], kg, bd).astype(BF16)
    v = mkv[:, 512:].astype(BF16)
    z = qz[:, 512:]
    outs = []
    for h in range(C_HEADS):
        sl = slice(128 * h, 128 * h + 128)
        s = _dot_nt(q[:, sl], k[:, sl]) * (C_HEAD_DIM ** -0.5)
        m = lax.stop_gradient(jnp.max(s, axis=-1, keepdims=True))
        p = jnp.exp(s - m)
        den = jnp.sum(p, axis=-1, keepdims=True)
        outs.append(_dot((p * (1.0 / den)).astype(BF16), v[:, sl]))
    return jnp.concatenate(outs, axis=1) * _silu(z)


def _qn_fn(q, g, bd):
    return _head_norm(q, g, bd) * (HEAD_DIM ** -0.5)


def _kn_fn(k, g, bd):
    return _head_norm(k, g, bd)


def _block_diag(width, hd):
    i = np.arange(width) // hd
    return jnp.asarray((i[:, None] == i[None, :]).astype(np.float32) / hd, BF16)


def _head_sum(width, hd):
    i = np.arange(width) // hd
    return jnp.asarray((i[:, None] == np.arange(LANE)[None, :]).astype(np.float32))


def _rms_fwd(x, g, *, tr, name):
    rows, dm = x.shape

    def body(x_ref, g_ref, o_ref):
        o_ref[...] = _rms(x_ref[...], g_ref[...]).astype(BF16)

    return pl.pallas_call(
        body, grid=(rows // tr,),
        in_specs=[_rowblk(tr, dm), _const((1, dm))],
        out_specs=_rowblk(tr, dm),
        out_shape=SDS((rows, dm), BF16), name=name,
        compiler_params=_params(("parallel",)))(x, g)


def _rms_bwd(x, g, dy, resid, *, tr, name):
    rows, dm = x.shape
    want_dx = resid is not None

    def body(*refs):
        if want_dx:
            x_ref, g_ref, dy_ref, r_ref, dx_ref, dg_ref = refs
        else:
            x_ref, g_ref, dy_ref, dg_ref = refs
        _, vjp = jax.vjp(_rms, x_ref[...], g_ref[...])
        dx, dg = vjp(dy_ref[...])

        @pl.when(pl.program_id(0) == 0)
        def _():
            dg_ref[...] = jnp.zeros_like(dg_ref)

        dg_ref[...] += dg
        if want_dx:
            dx_ref[...] = r_ref[...] + dx

    ins = [x, g, dy] + ([resid] if want_dx else [])
    in_specs = [_rowblk(tr, dm), _const((1, dm)), _rowblk(tr, dm)] + ([_rowblk(tr, dm)] if want_dx else [])
    out_specs = ([_rowblk(tr, dm)] if want_dx else []) + [_const((1, dm))]
    out_shape = ([SDS((rows, dm), F32)] if want_dx else []) + [SDS((1, dm), F32)]
    return pl.pallas_call(
        body, grid=(rows // tr,), in_specs=in_specs, out_specs=out_specs, out_shape=out_shape, name=name,
        compiler_params=_params(("arbitrary",)))(*ins)


class _Comm:
    def __init__(self, ins, out_shapes, sems, start, finish):
        self.ins, self.out_shapes, self.sems, self.start, self.finish = list(ins), list(out_shapes), list(sems), start, finish


def _run_comm(comm, name):
    n_in, n_out = len(comm.ins), len(comm.out_shapes)

    def body(*refs):
        ins, outs, sems = refs[:n_in], refs[n_in:n_in + n_out], refs[n_in + n_out:]
        comm.start(ins, outs, sems)
        comm.finish(ins, outs, sems)

    hbm = pl.BlockSpec(memory_space=pl.ANY)
    return pl.pallas_call(body, in_specs=[hbm] * n_in, out_specs=[hbm] * n_out, out_shape=comm.out_shapes,
                          scratch_shapes=comm.sems, name=name)(*comm.ins)


def _matmul(a, b, *, dims, out_dtype, tm, tn, tk, name, add=None, comms=()):
    a_list = list(a) if isinstance(a, (list, tuple)) else [a]
    b_list = list(b) if isinstance(b, (list, tuple)) else [b]
    assert len(a_list) == 1 or dims == "nt"
    assert len(b_list) == 1 or dims == "tn"
    if dims == "tn":
        kdim, m = a_list[0].shape
    else:
        m, kdim = a_list[0].shape[0], sum(p.shape[1] for p in a_list)
    n = b_list[0].shape[0] if dims == "nt" else sum(p.shape[1] for p in b_list)
    tm, tn, tk = min(tm, m), min(tn, n), min(tk, kdim)
    assert m % tm == 0 and n % tn == 0 and kdim % tk == 0, (name, m, n, kdim)
    ni, nj, nk = m // tm, n // tn, kdim // tk
    a_rng, b_rng, pos = [], [], 0
    for p in a_list:
        assert len(a_list) == 1 or p.shape[1] % tk == 0
        a_rng.append((pos, p.shape[1] // tk if len(a_list) > 1 else nk))
        pos += a_rng[-1][1]
    pos = 0
    for p in b_list:
        assert len(b_list) == 1 or p.shape[1] % tn == 0
        b_rng.append((pos, p.shape[1] // tn if len(b_list) > 1 else nj))
        pos += b_rng[-1][1]
    has_add = add is not None
    n_mm_in = len(a_list) + len(b_list) + (1 if has_add else 0)
    c_in = [len(c.ins) for c in comms]
    c_out = [len(c.out_shapes) for c in comms]
    c_sem = [len(c.sems) for c in comms]

    def body(*refs):
        a_refs, b_refs = refs[:len(a_list)], refs[len(a_list):len(a_list) + len(b_list)]
        add_ref = refs[n_mm_in - 1] if has_add else None
        pos = n_mm_in
        cin = []
        for cnt in c_in:
            cin.append(refs[pos:pos + cnt])
            pos += cnt
        o_ref = refs[pos]
        pos += 1
        cout = []
        for cnt in c_out:
            cout.append(refs[pos:pos + cnt])
            pos += cnt
        acc = refs[pos]
        pos += 1
        csem = []
        for cnt in c_sem:
            csem.append(refs[pos:pos + cnt])
            pos += cnt
        i, j, k = pl.program_id(0), pl.program_id(1), pl.program_id(2)

        if comms:
            @pl.when((i == 0) & (j == 0) & (k == 0))
            def _():
                for c, ci, co, cs in zip(comms, cin, cout, csem):
                    c.start(ci, co, cs)

        def accumulate(a_ref, b_ref, first_k, later_k):
            if dims == "nn":
                part = _dot(a_ref[...], b_ref[...])
            elif dims == "nt":
                part = _dot_nt(a_ref[...], b_ref[...])
            else:
                part = _dot_tn(a_ref[...], b_ref[...])

            if first_k:
                @pl.when(k == 0)
                def _():
                    acc[...] = part + add_ref[...] if has_add else part

            if later_k:
                @pl.when(k > 0)
                def _():
                    acc[...] += part

        if len(a_list) > 1:
            for a_ref, (k0, cnt) in zip(a_refs, a_rng):
                @pl.when((k >= k0) & (k < k0 + cnt))
                def _(a_ref=a_ref, k0=k0, cnt=cnt):
                    accumulate(a_ref, b_refs[0], k0 == 0, k0 + cnt > 1)
        elif len(b_list) > 1:
            for b_ref, (j0, cnt) in zip(b_refs, b_rng):
                @pl.when((j >= j0) & (j < j0 + cnt))
                def _(b_ref=b_ref):
                    accumulate(a_refs[0], b_ref, True, nk > 1)
        else:
            accumulate(a_refs[0], b_refs[0], True, nk > 1)

        @pl.when(k == nk - 1)
        def _():
            o_ref[...] = acc[...].astype(out_dtype)

        if comms:
            @pl.when((i == ni - 1) & (j == nj - 1) & (k == nk - 1))
            def _():
                for c, ci, co, cs in zip(comms, cin, cout, csem):
                    c.finish(ci, co, cs)

    def a_spec(k0, cnt):
        if dims == "tn":
            return pl.BlockSpec((tk, tm), lambda i, j, k: (k, i))
        return pl.BlockSpec((tm, tk), lambda i, j, k: (i, jnp.clip(k - k0, 0, cnt - 1)))

    def b_spec(j0, cnt):
        if dims == "nt":
            return pl.BlockSpec((tn, tk), lambda i, j, k: (j, k))
        return pl.BlockSpec((tk, tn), lambda i, j, k: (k, jnp.clip(j - j0, 0, cnt - 1)))

    o_spec = pl.BlockSpec((tm, tn), lambda i, j, k: (i, j))
    hbm = pl.BlockSpec(memory_space=pl.ANY)
    ins = a_list + b_list + ([add] if has_add else []) + [x for c in comms for x in c.ins]
    in_specs = ([a_spec(*r) for r in a_rng] + [b_spec(*r) for r in b_rng] + ([o_spec] if has_add else [])
                + [hbm] * sum(c_in))
    out_specs = [o_spec] + [hbm] * sum(c_out)
    out_shape = [SDS((m, n), out_dtype)] + [s for c in comms for s in c.out_shapes]
    scratch = [pltpu.VMEM((tm, tn), F32)] + [s for c in comms for s in c.sems]
    sem = ("arbitrary",) * 3 if comms else ("parallel", "parallel", "arbitrary")
    res = pl.pallas_call(
        body, grid=(ni, nj, nk), in_specs=in_specs, out_specs=out_specs, out_shape=out_shape, scratch_shapes=scratch,
        name=name, compiler_params=_params(sem))(*ins)
    if not comms:
        return res[0]
    outs, pos = [], 1
    for cnt in c_out:
        outs.append(list(res[pos:pos + cnt]))
        pos += cnt
    return res[0], outs


def _swa_specs(nb):
    prev = lambda off: pl.BlockSpec((pl.Element(WINDOW), pl.Element(1024)),
                                    lambda n: (jnp.maximum(n - 1, 0) * WINDOW, off))
    return [_win(WINDOW, 1024, QA), _win(WINDOW, 1024, VA), prev(QA), prev(VA),
            _const((1, 768)), _const((1, 256)), _const((1, LANE)), _const((768, 768)),
            _const((A_HEADS // A_GROUP, 2 * WINDOW, A_GROUP * WINDOW))]


def _swa_fwd(proj, qg, kg, sinks, bd, bias):
    s = proj.shape[0]
    nb = s // WINDOW

    def body(qk_ref, vz_ref, qkp_ref, vzp_ref, qg_ref, kg_ref, sk_ref, bd_ref, bias_ref, o_ref):
        first = pl.program_id(0) == 0
        o_ref[...] = _swa_fn(qk_ref[...], vz_ref[...], qkp_ref[...], vzp_ref[...], qg_ref[...], kg_ref[...],
                             sk_ref[...], bd_ref[...], bias_ref[...], first).astype(BF16)

    return pl.pallas_call(
        body, grid=(nb,), in_specs=_swa_specs(nb), out_specs=_rowblk(WINDOW, 768),
        out_shape=SDS((s, 768), BF16), name="swa_fwd",
        compiler_params=_params(("parallel",)))(proj, proj, proj, proj, qg, kg, sinks, bd, bias)


def _swa_bwd(proj, qg, kg, sinks, bd, bias, dga):
    s = proj.shape[0]
    nb = s // WINDOW

    def body(qk_ref, vz_ref, qkp_ref, vzp_ref, qg_ref, kg_ref, sk_ref, bd_ref, bias_ref, dg_ref,
             dcur_ref, dprev_ref, dqg_ref, dkg_ref, dsk_ref):
        first = pl.program_id(0) == 0
        bd_v = bd_ref[...]
        bias_v = bias_ref[...]
        fn = lambda qk, vz, qkp, vzp, qg_, kg_, sk: _swa_fn(qk, vz, qkp, vzp, qg_, kg_, sk, bd_v, bias_v, first)
        _, vjp = jax.vjp(fn, qk_ref[...], vz_ref[...], qkp_ref[...], vzp_ref[...], qg_ref[...], kg_ref[...], sk_ref[...])
        dqk, dvz, dqkp, dvzp, dqg, dkg, dsk = vjp(dg_ref[...])

        @pl.when(first)
        def _():
            dqg_ref[...] = jnp.zeros_like(dqg_ref)
            dkg_ref[...] = jnp.zeros_like(dkg_ref)
            dsk_ref[...] = jnp.zeros_like(dsk_ref)

        dqg_ref[...] += dqg
        dkg_ref[...] += dkg
        dsk_ref[...] += dsk
        dcur_ref[...] = jnp.concatenate([dqk, dvz], axis=1)
        dprev_ref[...] = jnp.concatenate([dqkp[:, 768:], dvzp[:, :256]], axis=1)

    return pl.pallas_call(
        body, grid=(nb,), in_specs=_swa_specs(nb) + [_rowblk(WINDOW, 768)],
        out_specs=[_rowblk(WINDOW, 2048), pl.BlockSpec((None, WINDOW, 512), lambda n: (n, 0, 0)),
                   _const((1, 768)), _const((1, 256)), _const((1, LANE))],
        out_shape=[SDS((s, 2048), F32), SDS((nb, WINDOW, 512), F32), SDS((1, 768), F32), SDS((1, 256), F32),
                   SDS((1, LANE), F32)],
        name="swa_bwd", compiler_params=_params(("arbitrary",)))(proj, proj, proj, proj, qg, kg, sinks, bd, bias, dga)


def _swa_combine(dcur, dprev):
    s = dcur.shape[0]
    nb = s // WINDOW

    def body(c_ref, p_ref, o_ref):
        c = c_ref[...]
        nxt = jnp.where(pl.program_id(0) == nb - 1, 0.0, p_ref[...])
        o_ref[...] = jnp.concatenate([c[:, :768], c[:, 768:1280] + nxt, c[:, 1280:]], axis=1).astype(BF16)

    return pl.pallas_call(
        body, grid=(nb,),
        in_specs=[_rowblk(WINDOW, 2048), pl.BlockSpec((None, WINDOW, 512), lambda n: (jnp.minimum(n + 1, nb - 1), 0, 0))],
        out_specs=_rowblk(WINDOW, 2048), out_shape=SDS((s, 2048), BF16), name="swa_combine",
        compiler_params=_params(("parallel",)))(dcur, dprev)


def _mem_fwd(proj, mkv, qg, kg, bd, *, tr):
    s = proj.shape[0]

    def body(qz_ref, mkv_ref, qg_ref, kg_ref, bd_ref, o_ref):
        o_ref[...] = _mem_fn(qz_ref[...], mkv_ref[...], qg_ref[...], kg_ref[...], bd_ref[...]).astype(BF16)

    return pl.pallas_call(
        body, grid=(s // tr,),
        in_specs=[_win(tr, 1024, QC), _const(mkv.shape), _const((1, 512)), _const((1, 512)), _const((512, 512))],
        out_specs=_rowblk(tr, 512), out_shape=SDS((s, 512), BF16), name="mem_fwd",
        compiler_params=_params(("parallel",)))(proj, mkv, qg, kg, bd)


def _mem_bwd(proj, mkv, qg, kg, bd, dgc, *, tr):
    s = proj.shape[0]

    def body(qz_ref, mkv_ref, qg_ref, kg_ref, bd_ref, dg_ref, dqz_ref, dmkv_ref, dqg_ref, dkg_ref):
        bd_v = bd_ref[...]
        fn = lambda qz, mkv_, qg_, kg_: _mem_fn(qz, mkv_, qg_, kg_, bd_v)
        _, vjp = jax.vjp(fn, qz_ref[...], mkv_ref[...], qg_ref[...], kg_ref[...])
        dqz, dmkv, dqg, dkg = vjp(dg_ref[...])

        @pl.when(pl.program_id(0) == 0)
        def _():
            dmkv_ref[...] = jnp.zeros_like(dmkv_ref)
            dqg_ref[...] = jnp.zeros_like(dqg_ref)
            dkg_ref[...] = jnp.zeros_like(dkg_ref)

        dmkv_ref[...] += dmkv
        dqg_ref[...] += dqg
        dkg_ref[...] += dkg
        dqz_ref[...] = dqz.astype(BF16)

    return pl.pallas_call(
        body, grid=(s // tr,),
        in_specs=[_win(tr, 1024, QC), _const(mkv.shape), _const((1, 512)), _const((1, 512)), _const((512, 512)),
                  _rowblk(tr, 512)],
        out_specs=[_rowblk(tr, 1024), _const(mkv.shape), _const((1, 512)), _const((1, 512))],
        out_shape=[SDS((s, 1024), BF16), SDS(mkv.shape, F32), SDS((1, 512), F32), SDS((1, 512), F32)],
        name="mem_bwd", compiler_params=_params(("arbitrary",)))(proj, mkv, qg, kg, bd, dgc)


def _log_sigmoid(x):
    return jnp.minimum(x, 0.0) - jnp.log1p(jnp.exp(-jnp.abs(x)))


def _fox_prep(proj, fbl, qg, kg, bfor, bd, *, tr):
    s = proj.shape[0]
    tri = jnp.asarray(np.tril(np.ones((tr, tr), np.float32)))

    def body(q_ref, k_ref, fb_ref, qg_ref, kg_ref, bf_ref, bd_ref, tri_ref, qn_ref, kn_ref, cq_ref, ck_ref, carry):
        @pl.when(pl.program_id(0) == 0)
        def _():
            carry[...] = jnp.zeros_like(carry)

        bd_v = bd_ref[...]
        qn_ref[...] = _qn_fn(q_ref[...], qg_ref[...], bd_v).astype(BF16)
        kn_ref[...] = _kn_fn(k_ref[...], kg_ref[...], bd_v).astype(BF16)
        lane = lax.broadcasted_iota(jnp.int32, (tr, LANE), 1)
        logf = jnp.where(lane < N_FORGET, _log_sigmoid(fb_ref[...] + bf_ref[...]), 0.0)
        c = jnp.dot(tri_ref[...], logf, precision=HI, preferred_element_type=F32) + carry[...]
        cq_ref[...] = c
        ck_ref[...] = jnp.transpose(c)
        carry[...] = c[tr - 1:tr, :]

    return pl.pallas_call(
        body, grid=(s // tr,),
        in_specs=[_win(tr, 768, QB), _win(tr, 768, KB), _rowblk(tr, LANE), _const((1, 768)), _const((1, 768)),
                  _const((1, LANE)), _const((768, 768)), _const((tr, tr))],
        out_specs=[_rowblk(tr, 768), _rowblk(tr, 768), _rowblk(tr, LANE), pl.BlockSpec((LANE, tr), lambda i: (0, i))],
        out_shape=[SDS((s, 768), BF16), SDS((s, 768), BF16), SDS((s, LANE), F32), SDS((LANE, s), F32)],
        scratch_shapes=[pltpu.VMEM((1, LANE), F32)], name="fox_prep",
        compiler_params=_params(("arbitrary",)))(proj, proj, fbl, qg, kg, bfor, bd, tri)


FOX_TQ, FOX_TK = 256, 512
FOX_FWD_TQ, FOX_FWD_TK = 512, 1024


def _fox_tiles(s):
    return min(FOX_TQ, s), min(FOX_TK, s)


def _fox_fwd(proj, qn, kn, cq, ck):
    s = proj.shape[0]
    tq, tk = _fox_tiles(s)
    nq, nk = s // tq, s // tk

    def last_k(i):
        return (i * tq + tq - 1) // tk

    def body(q_ref, k_ref, v_ref, cq_ref, ck_ref, z_ref, gb_ref, yb_ref, lse_ref, acc, m_s, l_s):
        i, j = pl.program_id(0), pl.program_id(1)

        @pl.when(j == 0)
        def _():
            acc[...] = jnp.zeros_like(acc)
            m_s[...] = jnp.full_like(m_s, NEG)
            l_s[...] = jnp.ones_like(l_s)

        @pl.when(j <= last_k(i))
        def _():
            qpos = i * tq + lax.broadcasted_iota(jnp.int32, (tq, tk), 0)
            kpos = j * tk + lax.broadcasted_iota(jnp.int32, (tq, tk), 1)
            mask = kpos <= qpos
            q, k = q_ref[...], k_ref[...]
            v = v_ref[...].astype(BF16)
            cqv, ckv = cq_ref[...], ck_ref[...]
            m_all, l_all = m_s[...], l_s[...]
            lane = lax.broadcasted_iota(jnp.int32, (tq, LANE), 1)
            m_out, l_out = m_all, l_all
            for hp in range(B_HEADS // 2):
                acc_pair = acc[:, 128 * hp:128 * hp + 128]
                new = []
                for u in range(2):
                    h = 2 * hp + u
                    sl = slice(64 * h, 64 * h + 64)
                    sc = _dot_nt(q[:, sl], k[:, sl]) + cqv[:, h:h + 1] - ckv[h:h + 1, :]
                    sc = jnp.where(mask, sc, NEG)
                    m_prev = m_all[:, h:h + 1]
                    m_new = jnp.maximum(m_prev, jnp.max(sc, axis=-1, keepdims=True))
                    alpha = jnp.exp(m_prev - m_new)
                    p = jnp.exp(sc - m_new)
                    l_new = alpha * l_all[:, h:h + 1] + jnp.sum(p, axis=-1, keepdims=True)
                    new.append(alpha * acc_pair[:, 64 * u:64 * u + 64] + _dot(p.astype(BF16), v[:, sl]))
                    m_out = jnp.where(lane == h, m_new, m_out)
                    l_out = jnp.where(lane == h, l_new, l_out)
                acc[:, 128 * hp:128 * hp + 128] = jnp.concatenate(new, axis=1)
            m_s[...] = m_out
            l_s[...] = l_out

        @pl.when(j == nk - 1)
        def _():
            l_all = l_s[...]
            inv = 1.0 / l_all
            a = acc[...]
            y = jnp.concatenate([a[:, 64 * h:64 * h + 64] * inv[:, h:h + 1] for h in range(B_HEADS)], axis=1)
            yb_ref[...] = y
            gb_ref[...] = (y * _silu(z_ref[...])).astype(BF16)
            lse_ref[...] = m_s[...] + jnp.log(l_all)

    kmap = lambda i, j: (jnp.minimum(j, last_k(i)), 0)
    return pl.pallas_call(
        body, grid=(nq, nk),
        in_specs=[pl.BlockSpec((tq, 768), lambda i, j: (i, 0)),
                  pl.BlockSpec((tk, 768), kmap),
                  pl.BlockSpec((pl.Element(tk), pl.Element(768)), lambda i, j: (jnp.minimum(j, last_k(i)) * tk, VB)),
                  pl.BlockSpec((tq, LANE), lambda i, j: (i, 0)),
                  pl.BlockSpec((16, tk), lambda i, j: (0, jnp.minimum(j, last_k(i)))),
                  pl.BlockSpec((pl.Element(tq), pl.Element(768)), lambda i, j: (i * tq, ZB))],
        out_specs=[pl.BlockSpec((tq, 768), lambda i, j: (i, 0)), pl.BlockSpec((tq, 768), lambda i, j: (i, 0)),
                   pl.BlockSpec((tq, LANE), lambda i, j: (i, 0))],
        out_shape=[SDS((s, 768), BF16), SDS((s, 768), F32), SDS((s, LANE), F32)],
        scratch_shapes=[pltpu.VMEM((tq, 768), F32), pltpu.VMEM((tq, LANE), F32), pltpu.VMEM((tq, LANE), F32)],
        name="fox_fwd", compiler_params=_params(("parallel", "arbitrary")))(qn, kn, proj, cq, ck, proj)


def _fox_bwd_pre(proj, yb, dgb, hsum, *, tr):
    s = proj.shape[0]

    def body(z_ref, y_ref, dg_ref, hs_ref, dy_ref, dz_ref, dl_ref):
        z, y, dg = z_ref[...], y_ref[...], dg_ref[...]
        sg = jax.nn.sigmoid(z)
        dy = dg * (z * sg)
        dy_ref[...] = dy.astype(BF16)
        dz_ref[...] = (dg * y * (sg * (1.0 + z * (1.0 - sg)))).astype(BF16)
        dl_ref[...] = jnp.dot(dy * y, hs_ref[...], precision=HI, preferred_element_type=F32)

    return pl.pallas_call(
        body, grid=(s // tr,),
        in_specs=[_win(tr, 768, ZB), _rowblk(tr, 768), _rowblk(tr, 768), _const((768, LANE))],
        out_specs=[_rowblk(tr, 768), _rowblk(tr, 768), _rowblk(tr, LANE)],
        out_shape=[SDS((s, 768), BF16), SDS((s, 768), BF16), SDS((s, LANE), F32)], name="fox_bwd_pre",
        compiler_params=_params(("parallel",)))(proj, yb, dgb, hsum)


def _fox_bwd(proj, qn, kn, cq, ck, lse, delta, dyb):
    s = proj.shape[0]
    tq, tk = _fox_tiles(s)
    nq, nk = s // tq, s // tk

    def first_q(j):
        return (j * tk) // tq

    def body(q_ref, k_ref, v_ref, cq_ref, ck_ref, lse_ref, dl_ref, dy_ref,
             dq_ref, dcq_ref, dk_ref, dv_ref, dck_ref, dk_acc, dv_acc, dck_acc):
        j, i = pl.program_id(0), pl.program_id(1)

        @pl.when((j == 0) & (i == 0))
        def _():
            dq_ref[...] = jnp.zeros_like(dq_ref)
            dcq_ref[...] = jnp.zeros_like(dcq_ref)

        @pl.when(i == 0)
        def _():
            dk_acc[...] = jnp.zeros_like(dk_acc)
            dv_acc[...] = jnp.zeros_like(dv_acc)
            dck_acc[...] = jnp.zeros_like(dck_acc)

        @pl.when(i >= first_q(j))
        def _():
            qpos = i * tq + lax.broadcasted_iota(jnp.int32, (tq, tk), 0)
            kpos = j * tk + lax.broadcasted_iota(jnp.int32, (tq, tk), 1)
            mask = kpos <= qpos
            q, k = q_ref[...], k_ref[...]
            v = v_ref[...].astype(BF16)
            dy = dy_ref[...]
            cqv, ckv, lsev, dlv = cq_ref[...], ck_ref[...], lse_ref[...], dl_ref[...]
            lane = lax.broadcasted_iota(jnp.int32, (tq, LANE), 1)
            rows = pl.ds(pl.multiple_of(i * tq, tq), tq)
            dcq_t = jnp.zeros((tq, LANE), F32)
            for hp in range(B_HEADS // 2):
                dq_new, dk_new, dv_new = [], [], []
                for u in range(2):
                    h = 2 * hp + u
                    sl = slice(64 * h, 64 * h + 64)
                    sc = _dot_nt(q[:, sl], k[:, sl]) + cqv[:, h:h + 1] - ckv[h:h + 1, :]
                    sc = jnp.where(mask, sc, NEG)
                    p = jnp.exp(sc - lsev[:, h:h + 1])
                    dp = _dot_nt(dy[:, sl], v[:, sl])
                    ds = p * (dp - dlv[:, h:h + 1])
                    dsb = ds.astype(BF16)
                    dv_new.append(_dot_tn(p.astype(BF16), dy[:, sl]))
                    dk_new.append(_dot_tn(dsb, q[:, sl]))
                    dq_new.append(_dot(dsb, k[:, sl]))
                    dcq_t = jnp.where(lane == h, jnp.sum(ds, axis=-1, keepdims=True), dcq_t)
                    dck_acc[h:h + 1, :] -= jnp.sum(ds, axis=0, keepdims=True)
                cols = slice(128 * hp, 128 * hp + 128)
                dq_ref[rows, cols] += jnp.concatenate(dq_new, axis=1)
                dk_acc[:, cols] += jnp.concatenate(dk_new, axis=1)
                dv_acc[:, cols] += jnp.concatenate(dv_new, axis=1)
            dcq_ref[rows, :] += dcq_t

        @pl.when(i == nq - 1)
        def _():
            dk_ref[...] = dk_acc[...]
            dv_ref[...] = dv_acc[...].astype(BF16)
            dck_ref[...] = jnp.concatenate([dck_acc[...], jnp.zeros((LANE - 16, tk), F32)], axis=0)

    qmap = lambda j, i: (jnp.maximum(i, first_q(j)), 0)
    return pl.pallas_call(
        body, grid=(nk, nq),
        in_specs=[pl.BlockSpec((tq, 768), qmap),
                  pl.BlockSpec((tk, 768), lambda j, i: (j, 0)),
                  pl.BlockSpec((pl.Element(tk), pl.Element(768)), lambda j, i: (j * tk, VB)),
                  pl.BlockSpec((tq, LANE), qmap),
                  pl.BlockSpec((16, tk), lambda j, i: (0, j)),
                  pl.BlockSpec((tq, LANE), qmap),
                  pl.BlockSpec((tq, LANE), qmap),
                  pl.BlockSpec((tq, 768), qmap)],
        out_specs=[_const((s, 768)), _const((s, LANE)),
                   pl.BlockSpec((tk, 768), lambda j, i: (j, 0)), pl.BlockSpec((tk, 768), lambda j, i: (j, 0)),
                   pl.BlockSpec((LANE, tk), lambda j, i: (0, j))],
        out_shape=[SDS((s, 768), F32), SDS((s, LANE), F32), SDS((s, 768), F32), SDS((s, 768), BF16),
                   SDS((LANE, s), F32)],
        scratch_shapes=[pltpu.VMEM((tk, 768), F32), pltpu.VMEM((tk, 768), F32), pltpu.VMEM((16, tk), F32)],
        name="fox_bwd", compiler_params=_params(("arbitrary", "arbitrary")))(qn, kn, proj, cq, ck, lse, delta, dyb)


def _fox_bwd_post(proj, fbl, qg, kg, bfor, bd, dqn, dkn, dcq, dck, *, tr):
    s = proj.shape[0]
    nb = s // tr
    triu = jnp.asarray(np.triu(np.ones((tr, tr), np.float32)))
    rev = lambda i: nb - 1 - i

    def body(q_ref, k_ref, fb_ref, qg_ref, kg_ref, bf_ref, bd_ref, tri_ref, dqn_ref, dkn_ref, dcq_ref, dck_ref,
             dq_ref, dk_ref, dfb_ref, dqg_ref, dkg_ref, dbf_ref, carry):
        @pl.when(pl.program_id(0) == 0)
        def _():
            carry[...] = jnp.zeros_like(carry)
            dqg_ref[...] = jnp.zeros_like(dqg_ref)
            dkg_ref[...] = jnp.zeros_like(dkg_ref)
            dbf_ref[...] = jnp.zeros_like(dbf_ref)

        bd_v = bd_ref[...]
        _, vjp_q = jax.vjp(lambda q, g: _qn_fn(q, g, bd_v), q_ref[...], qg_ref[...])
        dq, dqg = vjp_q(dqn_ref[...])
        _, vjp_k = jax.vjp(lambda k, g: _kn_fn(k, g, bd_v), k_ref[...], kg_ref[...])
        dk, dkg = vjp_k(dkn_ref[...])
        dq_ref[...] = dq.astype(BF16)
        dk_ref[...] = dk.astype(BF16)
        dqg_ref[...] += dqg
        dkg_ref[...] += dkg

        dc = dcq_ref[...] + jnp.transpose(dck_ref[...])
        dlogf = jnp.dot(tri_ref[...], dc, precision=HI, preferred_element_type=F32) + carry[...]
        carry[...] = dlogf[0:1, :]
        lane = lax.broadcasted_iota(jnp.int32, (tr, LANE), 1)
        xf = fb_ref[...] + bf_ref[...]
        dfb = jnp.where(lane < N_FORGET, dlogf * jax.nn.sigmoid(-xf), 0.0)
        dfb_ref[...] = dfb.astype(BF16)
        dbf_ref[...] += jnp.sum(dfb, axis=0, keepdims=True)

    rb = lambda w: pl.BlockSpec((tr, w), lambda i: (rev(i), 0))
    wn = lambda w, off: pl.BlockSpec((pl.Element(tr), pl.Element(w)), lambda i: (rev(i) * tr, off))
    return pl.pallas_call(
        body, grid=(nb,),
        in_specs=[wn(768, QB), wn(768, KB), rb(LANE), _const((1, 768)), _const((1, 768)), _const((1, LANE)),
                  _const((768, 768)), _const((tr, tr)), rb(768), rb(768), rb(LANE),
                  pl.BlockSpec((LANE, tr), lambda i: (0, rev(i)))],
        out_specs=[rb(768), rb(768), rb(LANE), _const((1, 768)), _const((1, 768)), _const((1, LANE))],
        out_shape=[SDS((s, 768), BF16), SDS((s, 768), BF16), SDS((s, LANE), BF16), SDS((1, 768), F32),
                   SDS((1, 768), F32), SDS((1, LANE), F32)],
        scratch_shapes=[pltpu.VMEM((1, LANE), F32)], name="fox_bwd_post",
        compiler_params=_params(("arbitrary",)))(proj, proj, fbl, qg, kg, bfor, bd, triu, dqn, dkn, dcq, dck)


AUG = 128 * B_HEADS
COL_A, COL_B = 64, 67


def _split3(c):
    hi = c.astype(BF16)
    r1 = c - hi.astype(F32)
    mid = r1.astype(BF16)
    lo = (r1 - mid.astype(F32)).astype(BF16)
    return hi, mid, lo


def _expand_mats():
    def mat(col0):
        e = np.zeros((768 + 3 * LANE, AUG), np.float32)
        for h in range(B_HEADS):
            for d in range(HEAD_DIM):
                e[64 * h + d, 128 * h + d] = 1.0
            for part in range(3):
                e[768 + LANE * part + h, 128 * h + col0 + part] = 1.0
        return e

    def ones(col0):
        o = np.zeros((1, AUG), np.float32)
        for h in range(B_HEADS):
            o[0, 128 * h + col0:128 * h + col0 + 3] = 1.0
        return o

    return (jnp.asarray(mat(COL_A), BF16), jnp.asarray(mat(COL_B), BF16), jnp.asarray(ones(COL_A)), jnp.asarray(ones(COL_B)))


def _augment(data_bf16, triple, emat, ones_row):
    parts = [data_bf16] + (list(triple) if triple is not None else [jnp.zeros((data_bf16.shape[0], LANE), BF16)] * 3)
    wide = _dot(jnp.concatenate(parts, axis=1), emat)
    if ones_row is not None:
        wide = wide + ones_row
    return wide


def _compact(wide):
    return jnp.concatenate([wide[:, 128 * h:128 * h + 64] for h in range(B_HEADS)], axis=1)


def _lane_of_heads(wide, col):
    rows = wide.shape[0]
    lane = lax.broadcasted_iota(jnp.int32, (rows, LANE), 1)
    out = jnp.zeros((rows, LANE), F32)
    for h in range(B_HEADS):
        out = jnp.where(lane == h, wide[:, 128 * h + col:128 * h + col + 1], out)
    return out


def _fox2_prep(proj, fbl, qg, kg, bfor, bd, ea, eb, ones_a, ones_b, *, tr):
    s = proj.shape[0]
    tri = jnp.asarray(np.tril(np.ones((tr, tr), np.float32)))

    def body(q_ref, k_ref, v_ref, fb_ref, qg_ref, kg_ref, bf_ref, bd_ref, tri_ref, ea_ref, eb_ref, oa_ref, ob_ref,
             qat_ref, ka_ref, kat_ref, va_ref, vat_ref, qn_ref, c_ref, carry):
        @pl.when(pl.program_id(0) == 0)
        def _():
            carry[...] = jnp.zeros_like(carry)

        bd_v = bd_ref[...]
        lane = lax.broadcasted_iota(jnp.int32, (tr, LANE), 1)
        logf = jnp.where(lane < N_FORGET, _log_sigmoid(fb_ref[...] + bf_ref[...]), 0.0)
        c = jnp.dot(tri_ref[...], logf, precision=HI, preferred_element_type=F32) + carry[...]
        c_ref[...] = c
        carry[...] = c[tr - 1:tr, :]
        qn = _qn_fn(q_ref[...], qg_ref[...], bd_v).astype(BF16)
        kn = _kn_fn(k_ref[...], kg_ref[...], bd_v).astype(BF16)
        qn_ref[...] = qn
        qat_ref[...] = jnp.transpose(_augment(qn, _split3(c), ea_ref[...], ob_ref[...])).astype(BF16)
        ka = _augment(kn, _split3(-c), eb_ref[...], oa_ref[...])
        ka_ref[...] = ka.astype(BF16)
        kat_ref[...] = jnp.transpose(ka).astype(BF16)
        va = _augment(v_ref[...].astype(BF16), None, ea_ref[...], oa_ref[...])
        va_ref[...] = va.astype(BF16)
        vat_ref[...] = jnp.transpose(va).astype(BF16)

    emat = _const((768 + 3 * LANE, AUG))
    return pl.pallas_call(
        body, grid=(s // tr,),
        in_specs=[_win(tr, 768, QB), _win(tr, 768, KB), _win(tr, 768, VB), _rowblk(tr, LANE), _const((1, 768)),
                  _const((1, 768)), _const((1, LANE)), _const((768, 768)), _const((tr, tr)), emat, emat,
                  _const((1, AUG)), _const((1, AUG))],
        out_specs=[pl.BlockSpec((AUG, tr), lambda i: (0, i)), _rowblk(tr, AUG), pl.BlockSpec((AUG, tr), lambda i: (0, i)),
                   _rowblk(tr, AUG), pl.BlockSpec((AUG, tr), lambda i: (0, i)), _rowblk(tr, 768), _rowblk(tr, LANE)],
        out_shape=[SDS((AUG, s), BF16), SDS((s, AUG), BF16), SDS((AUG, s), BF16), SDS((s, AUG), BF16),
                   SDS((AUG, s), BF16), SDS((s, 768), BF16), SDS((s, LANE), F32)],
        scratch_shapes=[pltpu.VMEM((1, LANE), F32)], name="fox_prep",
        compiler_params=_params(("arbitrary",)))(proj, proj, proj, fbl, qg, kg, bfor, bd, tri, ea, eb, ones_a, ones_b)


def _fox2_fwd(proj, qat, ka, vat):
    s = proj.shape[0]
    tq, tk = min(FOX_FWD_TQ, s), min(FOX_FWD_TK, s)
    nq, nk = s // tq, s // tk

    def last_k(i):
        return (i * tq + tq - 1) // tk

    def body(qt_ref, k_ref, vt_ref, z_ref, gb_ref, yb_ref, lse_ref, acc, m_s):
        i, j = pl.program_id(0), pl.program_id(1)

        @pl.when(j == 0)
        def _():
            acc[...] = jnp.zeros_like(acc)
            m_s[...] = jnp.full_like(m_s, NEG)

        def tile(masked):
            if masked:
                kpos = j * tk + lax.broadcasted_iota(jnp.int32, (tk, tq), 0)
                qpos = i * tq + lax.broadcasted_iota(jnp.int32, (tk, tq), 1)
                mask = kpos <= qpos
            for h in range(B_HEADS):
                sl = slice(128 * h, 128 * h + 128)
                sc = _dot(k_ref[:, sl], qt_ref[sl, :])
                if masked:
                    sc = jnp.where(mask, sc, NEG)
                m_prev = m_s[h:h + 1, :]
                m_new = jnp.maximum(m_prev, jnp.max(sc, axis=0, keepdims=True))
                p = jnp.exp(sc - m_new).astype(BF16)
                acc[sl, :] = jnp.exp(m_prev - m_new) * acc[sl, :] + _dot(vt_ref[sl, :], p)
                m_s[h:h + 1, :] = m_new

        full = j * tk + tk - 1 <= i * tq

        @pl.when(full)
        def _():
            tile(False)

        @pl.when(jnp.logical_and(jnp.logical_not(full), j <= last_k(i)))
        def _():
            tile(True)

        @pl.when(j == nk - 1)
        def _():
            outs = []
            row = lax.broadcasted_iota(jnp.int32, (LANE, tq), 0)
            lse_t = jnp.zeros((LANE, tq), F32)
            for h in range(B_HEADS):
                l_row = acc[128 * h + COL_A:128 * h + COL_A + 1, :]
                outs.append(acc[128 * h:128 * h + 64, :] * (1.0 / l_row))
                lse_t = jnp.where(row == h, m_s[h:h + 1, :] + jnp.log(l_row), lse_t)
            y = jnp.transpose(jnp.concatenate(outs, axis=0))
            yb_ref[...] = y
            gb_ref[...] = (y * _silu(z_ref[...])).astype(BF16)
            lse_ref[...] = jnp.transpose(lse_t)

    kcol = lambda i, j: (0, jnp.minimum(j, last_k(i)))
    return pl.pallas_call(
        body, grid=(nq, nk),
        in_specs=[pl.BlockSpec((AUG, tq), lambda i, j: (0, i)),
                  pl.BlockSpec((tk, AUG), lambda i, j: (jnp.minimum(j, last_k(i)), 0)),
                  pl.BlockSpec((AUG, tk), kcol),
                  pl.BlockSpec((pl.Element(tq), pl.Element(768)), lambda i, j: (i * tq, ZB))],
        out_specs=[pl.BlockSpec((tq, 768), lambda i, j: (i, 0)), pl.BlockSpec((tq, 768), lambda i, j: (i, 0)),
                   pl.BlockSpec((tq, LANE), lambda i, j: (i, 0))],
        out_shape=[SDS((s, 768), BF16), SDS((s, 768), F32), SDS((s, LANE), F32)],
        scratch_shapes=[pltpu.VMEM((AUG, tq), F32), pltpu.VMEM((16, tq), F32)],
        name="fox_fwd", compiler_params=_params(("parallel", "arbitrary")))(qat, ka, vat, proj)


def _fox2_bwd_pre(proj, yb, dgb, qn, c, lse, hsum, ea, ones_b, *, tr):
    s = proj.shape[0]

    def body(z_ref, y_ref, dg_ref, qn_ref, c_ref, lse_ref, hs_ref, ea_ref, ob_ref,
             qa_ref, qat_ref, dya_ref, dyat_ref, dz_ref):
        z, y, dg = z_ref[...], y_ref[...], dg_ref[...]
        sg = jax.nn.sigmoid(z)
        dy = dg * (z * sg)
        dz_ref[...] = (dg * y * (sg * (1.0 + z * (1.0 - sg)))).astype(BF16)
        delta = jnp.dot(dy * y, hs_ref[...], precision=HI, preferred_element_type=F32)
        e = ea_ref[...]
        dya = _augment(dy.astype(BF16), _split3(-delta), e, None)
        dya_ref[...] = dya.astype(BF16)
        dyat_ref[...] = jnp.transpose(dya).astype(BF16)
        qa = _augment(qn_ref[...], _split3(c_ref[...] - lse_ref[...]), e, ob_ref[...])
        qa_ref[...] = qa.astype(BF16)
        qat_ref[...] = jnp.transpose(qa).astype(BF16)

    return pl.pallas_call(
        body, grid=(s // tr,),
        in_specs=[_win(tr, 768, ZB), _rowblk(tr, 768), _rowblk(tr, 768), _rowblk(tr, 768), _rowblk(tr, LANE),
                  _rowblk(tr, LANE), _const((768, LANE)), _const((768 + 3 * LANE, AUG)), _const((1, AUG))],
        out_specs=[_rowblk(tr, AUG), pl.BlockSpec((AUG, tr), lambda i: (0, i)), _rowblk(tr, AUG),
                   pl.BlockSpec((AUG, tr), lambda i: (0, i)), _rowblk(tr, 768)],
        out_shape=[SDS((s, AUG), BF16), SDS((AUG, s), BF16), SDS((s, AUG), BF16), SDS((AUG, s), BF16),
                   SDS((s, 768), BF16)], name="fox_bwd_pre",
        compiler_params=_params(("parallel",)))(proj, yb, dgb, qn, c, lse, hsum, ea, ones_b)


def _fox2_bwd(qb, qbt, ka, kat, va, dya, dyat):
    s = qb.shape[0]
    tq, tk = _fox_tiles(s)
    nq, nk = s // tq, s // tk

    def first_q(j):
        return (j * tk) // tq

    def body(q_ref, qt_ref, k_ref, kt_ref, v_ref, dy_ref, dyt_ref, dq_hbm, dk_ref, dv_ref, dck_ref,
             dq_acc, dk_acc, dv_acc, sem):
        j, i = pl.program_id(0), pl.program_id(1)

        @pl.when((j == 0) & (i == 0))
        def _():
            dq_acc[...] = jnp.zeros_like(dq_acc)

        @pl.when(i == 0)
        def _():
            dk_acc[...] = jnp.zeros_like(dk_acc)
            dv_acc[...] = jnp.zeros_like(dv_acc)

        def tile(masked):
            if masked:
                kpos = j * tk + lax.broadcasted_iota(jnp.int32, (tk, tq), 0)
                qpos = i * tq + lax.broadcasted_iota(jnp.int32, (tk, tq), 1)
                mask = kpos <= qpos
            cols = pl.ds(pl.multiple_of(i * tq, tq), tq)
            for h in range(B_HEADS):
                sl = slice(128 * h, 128 * h + 128)
                sc = _dot(k_ref[:, sl], qt_ref[sl, :])
                if masked:
                    sc = jnp.where(mask, sc, NEG)
                p = jnp.exp(sc)
                ds = (p * _dot(v_ref[:, sl], dyt_ref[sl, :])).astype(BF16)
                dv_acc[:, sl] += _dot(p.astype(BF16), dy_ref[:, sl])
                dk_acc[:, sl] += _dot(ds, q_ref[:, sl])
                dq_acc[sl, cols] += _dot(kt_ref[sl, :], ds)

        full = j * tk + tk - 1 <= i * tq

        @pl.when(full)
        def _():
            tile(False)

        @pl.when(jnp.logical_and(jnp.logical_not(full), i >= first_q(j)))
        def _():
            tile(True)

        @pl.when(i == nq - 1)
        def _():
            dkw = dk_acc[...]
            dk_ref[...] = _compact(dkw)
            dv_ref[...] = _compact(dv_acc[...]).astype(BF16)
            dck_ref[...] = -_lane_of_heads(dkw, COL_B)

        @pl.when((j == nk - 1) & (i == nq - 1))
        def _():
            cp = pltpu.make_async_copy(dq_acc, dq_hbm, sem)
            cp.start()
            cp.wait()

    qrow = pl.BlockSpec((tq, AUG), lambda j, i: (jnp.maximum(i, first_q(j)), 0))
    qcol = pl.BlockSpec((AUG, tq), lambda j, i: (0, jnp.maximum(i, first_q(j))))
    krow = pl.BlockSpec((tk, AUG), lambda j, i: (j, 0))
    kcol = pl.BlockSpec((AUG, tk), lambda j, i: (0, j))
    kblk = lambda w: pl.BlockSpec((tk, w), lambda j, i: (j, 0))
    return pl.pallas_call(
        body, grid=(nk, nq),
        in_specs=[qrow, qcol, krow, kcol, krow, qrow, qcol],
        out_specs=[pl.BlockSpec(memory_space=pl.ANY), kblk(768), kblk(768), kblk(LANE)],
        out_shape=[SDS((AUG, s), F32), SDS((s, 768), F32), SDS((s, 768), BF16), SDS((s, LANE), F32)],
        scratch_shapes=[pltpu.VMEM((AUG, s), F32), pltpu.VMEM((tk, AUG), F32), pltpu.VMEM((tk, AUG), F32),
                        pltpu.SemaphoreType.DMA],
        name="fox_bwd", compiler_params=_params(("arbitrary", "arbitrary")))(qb, qbt, ka, kat, va, dya, dyat)


def _fox2_bwd_post(proj, fbl, qg, kg, bfor, bd, dqa, dkn, dck, *, tr):
    s = proj.shape[0]
    nb = s // tr
    triu = jnp.asarray(np.triu(np.ones((tr, tr), np.float32)))
    rev = lambda i: nb - 1 - i

    def body(q_ref, k_ref, fb_ref, qg_ref, kg_ref, bf_ref, bd_ref, tri_ref, dqa_ref, dkn_ref, dck_ref,
             dq_ref, dk_ref, dfb_ref, dqg_ref, dkg_ref, dbf_ref, carry):
        @pl.when(pl.program_id(0) == 0)
        def _():
            carry[...] = jnp.zeros_like(carry)
            dqg_ref[...] = jnp.zeros_like(dqg_ref)
            dkg_ref[...] = jnp.zeros_like(dkg_ref)
            dbf_ref[...] = jnp.zeros_like(dbf_ref)

        bd_v = bd_ref[...]
        dqw = jnp.transpose(dqa_ref[...])
        _, vjp_q = jax.vjp(lambda q, g: _qn_fn(q, g, bd_v), q_ref[...], qg_ref[...])
        dq, dqg = vjp_q(_compact(dqw))
        _, vjp_k = jax.vjp(lambda k, g: _kn_fn(k, g, bd_v), k_ref[...], kg_ref[...])
        dk, dkg = vjp_k(dkn_ref[...])
        dq_ref[...] = dq.astype(BF16)
        dk_ref[...] = dk.astype(BF16)
        dqg_ref[...] += dqg
        dkg_ref[...] += dkg

        dc = _lane_of_heads(dqw, COL_A) + dck_ref[...]
        dlogf = jnp.dot(tri_ref[...], dc, precision=HI, preferred_element_type=F32) + carry[...]
        carry[...] = dlogf[0:1, :]
        lane = lax.broadcasted_iota(jnp.int32, (tr, LANE), 1)
        xf = fb_ref[...] + bf_ref[...]
        dfb = jnp.where(lane < N_FORGET, dlogf * jax.nn.sigmoid(-xf), 0.0)
        dfb_ref[...] = dfb.astype(BF16)
        dbf_ref[...] += jnp.sum(dfb, axis=0, keepdims=True)

    rb = lambda w: pl.BlockSpec((tr, w), lambda i: (rev(i), 0))
    wn = lambda w, off: pl.BlockSpec((pl.Element(tr), pl.Element(w)), lambda i: (rev(i) * tr, off))
    return pl.pallas_call(
        body, grid=(nb,),
        in_specs=[wn(768, QB), wn(768, KB), rb(LANE), _const((1, 768)), _const((1, 768)), _const((1, LANE)),
                  _const((768, 768)), _const((tr, tr)), pl.BlockSpec((AUG, tr), lambda i: (0, rev(i))), rb(768), rb(LANE)],
        out_specs=[rb(768), rb(768), rb(LANE), _const((1, 768)), _const((1, 768)), _const((1, LANE))],
        out_shape=[SDS((s, 768), BF16), SDS((s, 768), BF16), SDS((s, LANE), BF16), SDS((1, 768), F32),
                   SDS((1, 768), F32), SDS((1, LANE), F32)],
        scratch_shapes=[pltpu.VMEM((1, LANE), F32)], name="fox_bwd_post",
        compiler_params=_params(("arbitrary",)))(proj, proj, fbl, qg, kg, bfor, bd, triu, dqa, dkn, dck)


def _merge_specs(tr):
    row = lambda w: pl.BlockSpec((tr, w), lambda i, j: (i, 0))
    shard = lambda r: pl.BlockSpec((None, r, 512), lambda i, j: (j, 0, 0))
    gate = lambda b: pl.BlockSpec((tr, 512), lambda i, j: (i, (GATE + 2048 * b) // 512 + j))
    return [row(768), row(768), row(512), shard(768), shard(768), shard(512), gate(0), gate(1), gate(2)]


def _merge_fwd(proj, ga, gb, gc, wa, wb, wc, *, tr):
    s = proj.shape[0]

    def body(ga_ref, gb_ref, gc_ref, wa_ref, wb_ref, wc_ref, l0_ref, l1_ref, l2_ref, y_ref):
        ua = _dot(ga_ref[...], wa_ref[...])
        ub = _dot(gb_ref[...], wb_ref[...])
        uc = _dot(gc_ref[...], wc_ref[...])
        y = jax.nn.sigmoid(l0_ref[...]) * ua + jax.nn.sigmoid(l1_ref[...]) * ub + jax.nn.sigmoid(l2_ref[...]) * uc
        y_ref[...] = y.astype(BF16)

    return pl.pallas_call(
        body, grid=(s // tr, N_CHIPS), in_specs=_merge_specs(tr),
        out_specs=pl.BlockSpec((tr, 512), lambda i, j: (i, j)), out_shape=SDS((s, D_MODEL), BF16), name="merge_fwd",
        compiler_params=_params(("parallel", "arbitrary")))(ga, gb, gc, wa, wb, wc, proj, proj, proj)


def _merge_bwd(proj, ga, gb, gc, wa, wb, wc, dy, *, tr):
    s = proj.shape[0]

    def body(ga_ref, gb_ref, gc_ref, wa_ref, wb_ref, wc_ref, l0_ref, l1_ref, l2_ref, dy_ref,
             dl0_ref, dl1_ref, dl2_ref, dua_ref, dub_ref, duc_ref, dga_ref, dgb_ref, dgc_ref):
        j = pl.program_id(1)
        dyv = dy_ref[...]

        @pl.when(j == 0)
        def _():
            dga_ref[...] = jnp.zeros_like(dga_ref)
            dgb_ref[...] = jnp.zeros_like(dgb_ref)
            dgc_ref[...] = jnp.zeros_like(dgc_ref)

        for g_ref, w_ref, l_ref, dl_ref, du_ref, dg_ref in (
                (ga_ref, wa_ref, l0_ref, dl0_ref, dua_ref, dga_ref),
                (gb_ref, wb_ref, l1_ref, dl1_ref, dub_ref, dgb_ref),
                (gc_ref, wc_ref, l2_ref, dl2_ref, duc_ref, dgc_ref)):
            w = w_ref[...]
            u = _dot(g_ref[...], w)
            sg = jax.nn.sigmoid(l_ref[...])
            dl_ref[...] = (dyv * u * sg * (1.0 - sg)).astype(BF16)
            du = (dyv * sg).astype(BF16)
            du_ref[...] = du
            dg_ref[...] += _dot_nt(du, w)

    blk = pl.BlockSpec((tr, 512), lambda i, j: (i, j))
    row = lambda w: pl.BlockSpec((tr, w), lambda i, j: (i, 0))
    big = SDS((s, D_MODEL), BF16)
    return pl.pallas_call(
        body, grid=(s // tr, N_CHIPS), in_specs=_merge_specs(tr) + [blk],
        out_specs=[blk] * 6 + [row(768), row(768), row(512)],
        out_shape=[big] * 6 + [SDS((s, 768), F32), SDS((s, 768), F32), SDS((s, 512), F32)], name="merge_bwd",
        compiler_params=_params(("parallel", "arbitrary")))(ga, gb, gc, wa, wb, wc, proj, proj, proj, dy)


def _out_loss(y, wo, x, tgt, *, tr, tn):
    s = x.shape[0]

    def body(y_ref, w_ref, x_ref, t_ref, d_ref, db_ref, sq_ref):
        @pl.when((pl.program_id(0) == 0) & (pl.program_id(1) == 0))
        def _():
            sq_ref[...] = jnp.zeros_like(sq_ref)

        out = x_ref[...] + _dot(y_ref[...], w_ref[...])
        diff = out - t_ref[...]
        sq_ref[...] += jnp.sum(diff * diff, axis=0, keepdims=True)
        d = diff * (1.0 / D_MODEL)
        d_ref[...] = d
        db_ref[...] = d.astype(BF16)

    blk = pl.BlockSpec((tr, tn), lambda i, j: (i, j))
    return pl.pallas_call(
        body, grid=(s // tr, D_MODEL // tn),
        in_specs=[pl.BlockSpec((tr, D_MODEL), lambda i, j: (i, 0)), pl.BlockSpec((D_MODEL, tn), lambda i, j: (0, j)), blk, blk],
        out_specs=[blk, blk, _const((1, tn))],
        out_shape=[SDS((s, D_MODEL), F32), SDS((s, D_MODEL), BF16), SDS((1, tn), F32)], name="out_loss",
        compiler_params=_params(("arbitrary", "arbitrary")))(y, wo, x, tgt)


def _tile_gain(g, reps):
    return jnp.tile(g.reshape(1, -1), (1, reps))


def _pad_lane(v):
    v = v.reshape(1, -1)
    return jnp.pad(v, ((0, 0), (0, LANE - v.shape[1])))


def _local_step(x, mem, tgt, w_main, w_fb, w_small, norm_gain, mem_norm_gain, b_forget,
                q_gain_a, k_gain_a, sinks_a, q_gain_b, k_gain_b, q_gain_c, k_gain_c, core=None):
    s = x.shape[0]
    tr = min(512, s)
    bd64 = _block_diag(768, HEAD_DIM)
    bd128 = _block_diag(512, C_HEAD_DIM)
    hsum = _head_sum(768, HEAD_DIM)
    qga, kga = _tile_gain(q_gain_a, 12), _tile_gain(k_gain_a, 4)
    qgb, kgb = _tile_gain(q_gain_b, 12), _tile_gain(k_gain_b, 12)
    qgc, kgc = _tile_gain(q_gain_c, 4), _tile_gain(k_gain_c, 4)
    sinks = _pad_lane(sinks_a)
    bfor = _pad_lane(b_forget)

    hn = _rms_fwd(x, norm_gain, tr=tr, name="rms_x")
    on_mesh = core is not None
    if on_mesh:
        proj, (gathered,) = _matmul(hn, w_main, dims="nn", out_dtype=F32, tm=1024, tn=512, tk=D_MODEL, name="proj_main",
                                    comms=[_gather_comm(list(w_small))])
        w_mk, wa, wb, wc, wo = gathered
        w_mk, wo = w_mk.reshape(D_MODEL, 1024), wo.reshape(D_MODEL, D_MODEL)
    else:
        proj = _matmul(hn, w_main, dims="nn", out_dtype=F32, tm=1024, tn=512, tk=D_MODEL, name="proj_main")
        w_mk, wa, wb, wc, wo = w_small
    fbl = _matmul(hn, w_fb, dims="nn", out_dtype=F32, tm=1024, tn=LANE, tk=D_MODEL, name="proj_forget")
    memn = _rms_fwd(mem, mem_norm_gain, tr=mem.shape[0], name="rms_mem")
    mkv = _matmul(memn, w_mk, dims="nn", out_dtype=F32, tm=256, tn=512, tk=D_MODEL, name="mem_kv")

    swa_bias = _swa_bias()
    ga = _swa_fwd(proj, qga, kga, sinks, bd64, swa_bias)
    ea, eb, ones_a, ones_b = _expand_mats()
    tf = min(256, s)
    qat, ka, kat, va, vat, qn, cfox = _fox2_prep(proj, fbl, qgb, kgb, bfor, bd64, ea, eb, ones_a, ones_b, tr=tf)
    gb, yb, lse = _fox2_fwd(proj, qat, ka, vat)
    gc = _mem_fwd(proj, mkv, qgc, kgc, bd128, tr=tr)
    y = _merge_fwd(proj, ga, gb, gc, wa, wb, wc, tr=tr)
    dout, dout_b, sq = _out_loss(y, wo, x, tgt, tr=tr, tn=512)

    d_wo = _matmul(y, dout_b, dims="tn", out_dtype=F32, tm=1024, tn=512, tk=4096, name="dw_out")
    dy = _matmul(dout_b, wo, dims="nt", out_dtype=F32, tm=1024, tn=512, tk=D_MODEL, name="dy")
    dl0, dl1, dl2, dua, dub, duc, dga, dgb, dgc = _merge_bwd(proj, ga, gb, gc, wa, wb, wc, dy, tr=tr)
    d_wa = _matmul(ga, dua, dims="tn", out_dtype=F32, tm=768, tn=512, tk=4096, name="dw_branch_a")
    d_wb = _matmul(gb, dub, dims="tn", out_dtype=F32, tm=768, tn=512, tk=4096, name="dw_branch_b")
    d_wc = _matmul(gc, duc, dims="tn", out_dtype=F32, tm=512, tn=512, tk=4096, name="dw_branch_c")

    dcur, dprev, d_qga, d_kga, d_sinks = _swa_bwd(proj, qga, kga, sinks, bd64, swa_bias, dga)
    dproj_a = _swa_combine(dcur, dprev)

    qab, qabt, dya, dyat, dzb = _fox2_bwd_pre(proj, yb, dgb, qn, cfox, lse, hsum, ea, ones_b, tr=tf)
    dqa, dkn, dvb, dck = _fox2_bwd(qab, qabt, ka, kat, va, dya, dyat)
    dqb, dkb, dfb, d_qgb, d_kgb, d_bf = _fox2_bwd_post(proj, fbl, qgb, kgb, bfor, bd64, dqa, dkn, dck, tr=tf)

    dproj_c, dmkv, d_qgc, d_kgc = _mem_bwd(proj, mkv, qgc, kgc, bd128, dgc, tr=tr)
    dmkv_b = dmkv.astype(BF16)
    d_wmk = _matmul(memn, dmkv_b, dims="tn", out_dtype=F32, tm=1024, tn=512, tk=256, name="dw_mem_kv")
    dmemn = _matmul(dmkv_b, w_mk, dims="nt", out_dtype=F32, tm=256, tn=512, tk=1024, name="dmemn")
    (d_mem_gain,) = _rms_bwd(mem, mem_norm_gain, dmemn, None, tr=mem.shape[0], name="rms_mem_bwd")

    dproj = [dproj_a, jnp.concatenate([dqb, dkb, dvb, dzb, dproj_c], axis=1), dl0, dl1, dl2]
    dhn_f = _matmul(dfb, w_fb, dims="nt", out_dtype=F32, tm=1024, tn=512, tk=LANE, name="dhn_forget")
    d_wfb = _matmul(hn, dfb, dims="tn", out_dtype=F32, tm=1024, tn=LANE, tk=512, name="dw_forget")
    big = {}
    if on_mesh:
        half = D_MODEL // 2
        c0 = core[0]
        hn_other = lax.dynamic_slice(hn, (0, (1 - c0) * half), (s, half))
        hn_own = lax.dynamic_slice(hn, (0, c0 * half), (s, half))
        g1, k1 = [d_wmk, d_wa, d_wb, d_wc, d_wo], [2, 3, 4, 5, 6]
        d_other, (got1,) = _matmul(hn_other, dproj, dims="tn", out_dtype=F32, tm=1024, tn=512, tk=4096,
                                   name="dw_main_other", comms=[_exchange_comm(g1, k1)])
        h1 = [_add_half(g, got, core, HALF_AXIS[k], name=f"add_half_{k}") for g, got, k in zip(g1, got1, k1)]
        d_own, (got0, parts1) = _matmul(
            hn_own, dproj, dims="tn", out_dtype=F32, tm=1024, tn=512, tk=4096, name="dw_main_own",
            comms=[_exchange_comm([d_other, d_wfb], [0, 1], whole=(0,)), _scatter_comm(h1, k1)])
        h0 = [_add_pair(d_own, got0[0], name="add_pair_main"), _add_half(d_wfb, got0[1], core, 0, name="add_half_1")]
        sums1 = [_sum4(p, name=f"sum4_{k}") for p, k in zip(parts1, k1)]
        dhn, (parts0, theirs1) = _matmul(dproj, w_main, dims="nt", out_dtype=F32, tm=512, tn=1024, tk=2048, name="dhn",
                                         add=dhn_f, comms=[_scatter_comm(h0, [0, 1]), _swap_comm(sums1)])
        sums0 = [_sum4(p, name=f"sum4_{k}") for p, k in zip(parts0, (0, 1))]
        theirs0 = _run_comm(_swap_comm(sums0), "swap_halves")
        big = dict(sums=sums0 + sums1, theirs=list(theirs0) + list(theirs1))
    else:
        dhn = _matmul(dproj, w_main, dims="nt", out_dtype=F32, tm=512, tn=1024, tk=2048, name="dhn", add=dhn_f)
        d_wmain = _matmul(hn, dproj, dims="tn", out_dtype=F32, tm=1024, tn=512, tk=4096, name="dw_main")
        big = dict(d_wmain=d_wmain, d_wfb=d_wfb, d_wmk=d_wmk, d_wa=d_wa, d_wb=d_wb, d_wc=d_wc, d_wo=d_wo)
    grad_x, d_gain = _rms_bwd(x, norm_gain, dhn, dout, tr=tr, name="rms_x_bwd")

    fold = lambda g, reps: jnp.sum(g.reshape(reps, -1), axis=0, keepdims=True)
    return dict(
        sq=sq, grad_x=grad_x, **big,
        d_gain=d_gain, d_mem_gain=d_mem_gain, d_bf=d_bf[:, :N_FORGET],
        d_qga=fold(d_qga, 12), d_kga=fold(d_kga, 4), d_sinks=d_sinks[:, :A_HEADS],
        d_qgb=fold(d_qgb, 12), d_kgb=fold(d_kgb, 12), d_qgc=fold(d_qgc, 4), d_kgc=fold(d_kgc, 4))


PACK_ROWS = 256
FORGET_IN_SHARD = FORGET_COL - SHARD_COLS
AFTER_FORGET = FORGET_COL - SLAB_START[1]
END_CHIP1 = 2 * SHARD_COLS - N_FORGET - SLAB_START[1]


def _pack_w_in(chip, w):
    rows = w.shape[0]
    tr = PACK_ROWS

    def body(k_ref, w_ref, o_ref, scr):
        scr[...] = jnp.zeros_like(scr)
        scr[:, pl.ds(0, SHARD_COLS)] = w_ref[...]
        v = scr[...]
        k = k_ref[0]
        col = lax.broadcasted_iota(jnp.int32, (tr, SLAB), 1)
        no_forget = jnp.zeros((tr, LANE), BF16)

        @pl.when(k == 0)
        def _():
            o_ref[:, 0:SLAB] = v.astype(BF16)
            o_ref[:, SLAB:] = no_forget

        @pl.when(k == 1)
        def _():
            before = pltpu.roll(v, SLAB_SHIFT[1], axis=1)
            after = pltpu.roll(v, SLAB - (N_FORGET - SLAB_SHIFT[1]), axis=1)
            slab = jnp.where(col < AFTER_FORGET, before, jnp.where(col < END_CHIP1, after, 0.0))
            o_ref[:, 0:SLAB] = slab.astype(BF16)
            f = pltpu.roll(v, SLAB - FORGET_IN_SHARD, axis=1)[:, :LANE]
            o_ref[:, SLAB:] = jnp.where(col[:, :LANE] < N_FORGET, f, 0.0).astype(BF16)

        for kk in (2, 3):
            @pl.when(k == kk)
            def _(kk=kk):
                o_ref[:, 0:SLAB] = pltpu.roll(v, SLAB_SHIFT[kk], axis=1).astype(BF16)
                o_ref[:, SLAB:] = no_forget

    return pl.pallas_call(
        body, grid_spec=pltpu.PrefetchScalarGridSpec(
            num_scalar_prefetch=1, grid=(rows // tr,),
            in_specs=[pl.BlockSpec((tr, SHARD_COLS), lambda i, k: (i, 0))],
            out_specs=pl.BlockSpec((None, tr, SLAB + LANE), lambda i, k: (k[0], i, 0)),
            scratch_shapes=[pltpu.VMEM((tr, SLAB), F32)]),
        out_shape=SDS((N_CHIPS, rows, SLAB + LANE), BF16), name="pack_w_in",
        compiler_params=_params(("arbitrary",)))(chip, w)


def _merge_slabs(g):
    rows = g.shape[1]
    tr = PACK_ROWS
    t = [s // LANE for s in SLAB_START]
    n_t = SLAB // LANE

    def body(g_ref, m_ref, f_ref):
        for k in range(N_CHIPS):
            lo = t[k] + (1 if k > 0 else 0)
            hi = t[k + 1] if k + 1 < N_CHIPS else t[k] + n_t
            m_ref[:, lo * LANE:hi * LANE] = g_ref[k, :, (lo - t[k]) * LANE:(hi - t[k]) * LANE]
            if k + 1 < N_CHIPS:
                a = g_ref[k, :, (hi - t[k]) * LANE:(hi - t[k] + 1) * LANE].astype(F32)
                b = g_ref[k + 1, :, 0:LANE].astype(F32)
                m_ref[:, hi * LANE:(hi + 1) * LANE] = (a + b).astype(BF16)
        f_ref[...] = g_ref[1, :, SLAB:]

    return pl.pallas_call(
        body, grid=(rows // tr,),
        in_specs=[pl.BlockSpec((N_CHIPS, tr, SLAB + LANE), lambda i: (0, i, 0))],
        out_specs=[_rowblk(tr, P_MAIN), _rowblk(tr, LANE)],
        out_shape=[SDS((rows, P_MAIN), BF16), SDS((rows, LANE), BF16)], name="merge_slabs",
        compiler_params=_params(("parallel",)))(g)


def _adamw_math(w, g, m, v):
    nm = ADAM_B1 * m + (1.0 - ADAM_B1) * g
    nv = ADAM_B2 * v + (1.0 - ADAM_B2) * (g * g)
    m_hat = nm / (1.0 - ADAM_B1 ** ADAM_STEP)
    v_hat = nv / (1.0 - ADAM_B2 ** ADAM_STEP)
    delta = -ADAM_LR * (m_hat / (jnp.sqrt(v_hat) + ADAM_EPS) + ADAM_WD * w)
    return delta, nm, nv


def _adamw(g, w, m, v, *, tr, name):
    rows, cols = w.shape
    tr = min(tr, rows)

    def body(g_ref, w_ref, m_ref, v_ref, d_ref, nm_ref, nv_ref):
        d, nm, nv = _adamw_math(w_ref[...], g_ref[...], m_ref[...], v_ref[...])
        d_ref[...] = d
        nm_ref[...] = nm
        nv_ref[...] = nv

    spec = _rowblk(tr, cols)
    return pl.pallas_call(
        body, grid=(rows // tr,), in_specs=[spec] * 4, out_specs=[spec] * 3,
        out_shape=[SDS((rows, cols), F32)] * 3, name=name, compiler_params=_params(("parallel",)))(g, w, m, v)


def _adamw_w_in(chip_core, slab_mine, slab_theirs, forget_mine, forget_theirs, w, m, v):
    rows = w.shape[0]
    tr = PACK_ROWS // 2
    nbh = rows // 2 // tr

    def body(k_ref, sa_ref, sb_ref, fa_ref, fb_ref, w_ref, m_ref, v_ref, g_ref, d_ref, nm_ref, nv_ref):
        use_mine = pl.program_id(0) // nbh == k_ref[1]
        sl = jnp.where(use_mine, sa_ref[...], sb_ref[...])
        f_tile = jnp.where(use_mine, fa_ref[...], fb_ref[...])
        k = k_ref[0]

        def emit(wide):
            g = wide[:, :SHARD_COLS]
            g_ref[...] = g
            d, nm, nv = _adamw_math(w_ref[...], g, m_ref[...], v_ref[...])
            d_ref[...] = d
            nm_ref[...] = nm
            nv_ref[...] = nv

        @pl.when(k == 0)
        def _():
            emit(sl)

        @pl.when(k == 1)
        def _():
            col = lax.broadcasted_iota(jnp.int32, (tr, SLAB), 1)
            before = pltpu.roll(sl, SLAB - SLAB_SHIFT[1], axis=1)
            after = pltpu.roll(sl, N_FORGET - SLAB_SHIFT[1], axis=1)
            wide_f = jnp.concatenate([f_tile, jnp.zeros((tr, SLAB - LANE), F32)], axis=1)
            forget = pltpu.roll(wide_f, FORGET_IN_SHARD, axis=1)
            emit(jnp.where(col < FORGET_IN_SHARD, before, jnp.where(col < FORGET_IN_SHARD + N_FORGET, forget, after)))

        for kk in (2, 3):
            @pl.when(k == kk)
            def _(kk=kk):
                emit(pltpu.roll(sl, SLAB - SLAB_SHIFT[kk], axis=1))

    nat = pl.BlockSpec((tr, SHARD_COLS), lambda i, k: (i, 0))
    half = lambda width: pl.BlockSpec((tr, width), lambda i, k: (i % nbh, 0))
    return pl.pallas_call(
        body, grid_spec=pltpu.PrefetchScalarGridSpec(
            num_scalar_prefetch=1, grid=(rows // tr,),
            in_specs=[half(SLAB), half(SLAB), half(LANE), half(LANE), nat, nat, nat],
            out_specs=[nat] * 4),
        out_shape=[SDS((rows, SHARD_COLS), F32)] * 4, name="adamw_w_in",
        compiler_params=_params(("arbitrary",)))(chip_core, slab_mine, slab_theirs, forget_mine, forget_theirs, w, m, v)


ANY = pl.BlockSpec(memory_space=pl.ANY)
HALF_AXIS = (0, 0, 1, 0, 0, 0, 1)


def _me():
    return lax.axis_index("x"), lax.axis_index("y"), lax.axis_index("c")


def _half(ref, which, axis):
    n = ref.shape[axis] // 2
    sl = pl.ds(which * n, n)
    return ref.at[sl] if axis == 0 else ref.at[:, sl]


def _piece(t, ref, j):
    if t == 0:
        return ref.at[:, pl.ds(SLAB_START[j], SLAB)]
    if t == 1:
        return ref
    if t in (2, 6):
        return ref.at[pl.ds(512 * j, 512)]
    return ref.at[:, pl.ds(512 * j, 512)]


def _piece_shape(t, shape):
    if t == 0:
        return (shape[0], SLAB)
    if t == 1:
        return shape
    if t in (2, 6):
        return (512, shape[1])
    return (shape[0], 512)


def _gather_plan(ins, outs, own_slot_in_src):
    x, y, c = _me()
    k = 2 * x + y
    sib = (x, y, 1 - c)
    chips = [(1 - x, y), (x, 1 - y), (1 - x, 1 - y)]
    n = len(outs)

    def rows(t, which):
        h = outs[t].shape[1] // 2
        return pl.ds(which * h, h)

    def mine(t):
        return ins[t].at[k, rows(t, c)] if own_slot_in_src else ins[t].at[rows(t, c)]

    def first(t, j, sems):
        chip = chips[j]
        return pltpu.make_async_remote_copy(
            src_ref=mine(t), dst_ref=outs[t].at[k, rows(t, c)], send_sem=sems[0].at[t, j], recv_sem=sems[1].at[t, j],
            device_id=(chip[0], chip[1], c), device_id_type=MESH)

    def landed(t, j, sems):
        chip = chips[j]
        return pltpu.make_async_remote_copy(
            src_ref=mine(t), dst_ref=outs[t].at[2 * chip[0] + chip[1], rows(t, c)], send_sem=sems[0].at[t, j],
            recv_sem=sems[1].at[t, j], device_id=(chip[0], chip[1], c), device_id_type=MESH)

    def passed(t, j, which, sems):
        chip = chips[j]
        blk = outs[t].at[2 * chip[0] + chip[1], rows(t, which)]
        return pltpu.make_async_remote_copy(
            src_ref=blk, dst_ref=blk, send_sem=sems[2].at[t, j], recv_sem=sems[3].at[t, j], device_id=sib,
            device_id_type=MESH)

    def start(sems):
        for j in range(3):
            for t in range(n):
                first(t, j, sems).start()

    def finish(sems):
        for j in range(3):
            for t in range(n):
                landed(t, j, sems).wait_recv()
                passed(t, j, c, sems).start()
        for j in range(3):
            for t in range(n):
                passed(t, j, 1 - c, sems).wait_recv()
        for j in range(3):
            for t in range(n):
                first(t, j, sems).wait_send()
                passed(t, j, c, sems).wait_send()

    return k, start, finish


def _all_gather_slabs(slabs):
    def body(in_ref, out_ref, *sems):
        _, start, finish = _gather_plan([in_ref], [out_ref], True)
        start(sems)
        finish(sems)

    return pl.pallas_call(
        body, in_specs=[ANY], out_specs=ANY, out_shape=SDS(slabs.shape, slabs.dtype),
        scratch_shapes=[pltpu.SemaphoreType.DMA((1, 3))] * 4, input_output_aliases={0: 0},
        name="all_gather_slabs")(slabs)


def _gather_comm(parts):
    n = len(parts)

    def start(ins, outs, sems):
        k, go, _ = _gather_plan(ins, outs, False)
        for t in range(n):
            pltpu.make_async_copy(ins[t], outs[t].at[k], sems[4].at[t]).start()
        go(sems)

    def finish(ins, outs, sems):
        k, _, done = _gather_plan(ins, outs, False)
        done(sems)
        for t in range(n):
            pltpu.make_async_copy(ins[t], outs[t].at[k], sems[4].at[t]).wait()

    return _Comm(parts, [SDS((N_CHIPS,) + p.shape, p.dtype) for p in parts],
                 [pltpu.SemaphoreType.DMA((n, 3))] * 4 + [pltpu.SemaphoreType.DMA((n,))], start, finish)


def _exchange_comm(arrs, kinds, whole=()):
    n = len(arrs)

    def copies(ins, outs, sems):
        x, y, c = _me()
        return [pltpu.make_async_remote_copy(
            src_ref=ins[t] if t in whole else _half(ins[t], 1 - c, HALF_AXIS[kinds[t]]), dst_ref=outs[t],
            send_sem=sems[0].at[t], recv_sem=sems[1].at[t], device_id=(x, y, 1 - c), device_id_type=MESH)
            for t in range(n)]

    def start(ins, outs, sems):
        for cp in copies(ins, outs, sems):
            cp.start()

    def finish(ins, outs, sems):
        for cp in copies(ins, outs, sems):
            cp.wait()

    def hshape(t):
        s = list(arrs[t].shape)
        if t not in whole:
            s[HALF_AXIS[kinds[t]]] //= 2
        return SDS(tuple(s), arrs[t].dtype)

    return _Comm(arrs, [hshape(t) for t in range(n)], [pltpu.SemaphoreType.DMA((n,))] * 2, start, finish)


def _add_half(full, got, core, axis, *, name):
    r, c = got.shape
    br, bc = (256 if r % 256 == 0 else 128), min(2048, c)
    off_r = (r // br) if axis == 0 else 0
    off_c = (c // bc) if axis == 1 else 0

    def body(c_ref, a_ref, b_ref, o_ref):
        o_ref[...] = (a_ref[...] + b_ref[...]).astype(BF16)

    return pl.pallas_call(
        body, grid_spec=pltpu.PrefetchScalarGridSpec(
            num_scalar_prefetch=1, grid=(r // br, c // bc),
            in_specs=[pl.BlockSpec((br, bc), lambda i, j, cr: (i + cr[0] * off_r, j + cr[0] * off_c)),
                      pl.BlockSpec((br, bc), lambda i, j, cr: (i, j))],
            out_specs=pl.BlockSpec((br, bc), lambda i, j, cr: (i, j))),
        out_shape=SDS((r, c), BF16), name=name, compiler_params=_params(("parallel", "parallel")))(core, full, got)


def _add_pair(a, b, *, name):
    r, c = a.shape
    br, bc = 256, min(2048, c)

    def body(a_ref, b_ref, o_ref):
        o_ref[...] = (a_ref[...] + b_ref[...]).astype(BF16)

    spec = pl.BlockSpec((br, bc), lambda i, j: (i, j))
    return pl.pallas_call(body, grid=(r // br, c // bc), in_specs=[spec, spec], out_specs=spec,
                          out_shape=SDS((r, c), BF16), name=name, compiler_params=_params(("parallel", "parallel")))(a, b)


def _scatter_comm(halves, kinds):
    n = len(halves)

    def plan(ins, outs, sems):
        send, recv, lsem = sems
        x, y, c = _me()
        k = 2 * x + y

        def to_chip(t, j):
            return pltpu.make_async_remote_copy(
                src_ref=_piece(kinds[t], ins[t], j), dst_ref=outs[t].at[k], send_sem=send.at[t, j],
                recv_sem=recv.at[t, k], device_id=(j // 2, j % 2, c), device_id_type=MESH)

        def from_chip(t, j):
            return pltpu.make_async_remote_copy(
                src_ref=_piece(kinds[t], ins[t], j), dst_ref=outs[t].at[j], send_sem=send.at[t, j],
                recv_sem=recv.at[t, j], device_id=(j // 2, j % 2, c), device_id_type=MESH)

        def own(t, j):
            return pltpu.make_async_copy(_piece(kinds[t], ins[t], j), outs[t].at[j], lsem.at[t])

        return k, to_chip, from_chip, own

    def start(ins, outs, sems):
        k, to_chip, _, own = plan(ins, outs, sems)
        for j in range(N_CHIPS):
            @pl.when(k != j)
            def _(j=j):
                for t in range(n):
                    to_chip(t, j).start()

            @pl.when(k == j)
            def _(j=j):
                for t in range(n):
                    own(t, j).start()

    def finish(ins, outs, sems):
        k, to_chip, from_chip, own = plan(ins, outs, sems)
        for j in range(N_CHIPS):
            @pl.when(k != j)
            def _(j=j):
                for t in range(n):
                    from_chip(t, j).wait_recv()
                for t in range(n):
                    to_chip(t, j).wait_send()

            @pl.when(k == j)
            def _(j=j):
                for t in range(n):
                    own(t, j).wait()

    return _Comm(halves, [SDS((N_CHIPS,) + _piece_shape(kinds[t], halves[t].shape), halves[t].dtype) for t in range(n)],
                 [pltpu.SemaphoreType.DMA((n, N_CHIPS))] * 2 + [pltpu.SemaphoreType.DMA((n,))], start, finish)


def _sum4(p, *, name):
    _, r, c = p.shape
    br = 256 if r % 256 == 0 else 128

    def body(p_ref, o_ref):
        o_ref[...] = ((p_ref[0].astype(F32) + p_ref[1].astype(F32)) + p_ref[2].astype(F32)) + p_ref[3].astype(F32)

    return pl.pallas_call(
        body, grid=(r // br,), in_specs=[pl.BlockSpec((N_CHIPS, br, c), lambda i: (0, i, 0))],
        out_specs=_rowblk(br, c), out_shape=SDS((r, c), F32), name=name, compiler_params=_params(("parallel",)))(p)


def _swap_comm(sums):
    return _exchange_comm(sums, [None] * len(sums), whole=tuple(range(len(sums))))


def _adamw_halves(mine, theirs, core, w, m, v, *, axis, tr, name):
    rows, cols = w.shape

    if axis == 0:
        nbh = rows // 2 // tr
        g_spec = pl.BlockSpec((tr, cols), lambda i, cr: (i % nbh, 0))
    else:
        g_spec = pl.BlockSpec((tr, cols // 2), lambda i, cr: (i, 0))

    def body(c_ref, a_ref, b_ref, w_ref, m_ref, v_ref, g_ref, d_ref, nm_ref, nv_ref):
        a, b = a_ref[...], b_ref[...]
        if axis == 0:
            g = jnp.where(pl.program_id(0) // nbh == c_ref[0], a, b)
        else:
            low = c_ref[0] == 0
            g = jnp.concatenate([jnp.where(low, a, b), jnp.where(low, b, a)], axis=1)
        g_ref[...] = g
        d, nm, nv = _adamw_math(w_ref[...], g, m_ref[...], v_ref[...])
        d_ref[...] = d
        nm_ref[...] = nm
        nv_ref[...] = nv

    nat = pl.BlockSpec((tr, cols), lambda i, cr: (i, 0))
    return pl.pallas_call(
        body, grid_spec=pltpu.PrefetchScalarGridSpec(
            num_scalar_prefetch=1, grid=(rows // tr,), in_specs=[g_spec, g_spec, nat, nat, nat], out_specs=[nat] * 4),
        out_shape=[SDS((rows, cols), F32)] * 4, name=name, compiler_params=_params(("arbitrary",)))(
            core, mine, theirs, w, m, v)


SMALL_ROWS, SMALL_COLS = 8, 1024


def _pack_small(vs):
    flat = jnp.concatenate([v.reshape(-1) for v in vs])
    return jnp.pad(flat, (0, SMALL_ROWS * SMALL_COLS - flat.shape[0])).reshape(SMALL_ROWS, SMALL_COLS)


def _unpack_small(packed, sizes):
    flat = packed.reshape(-1)
    out, o = [], 0
    for n in sizes:
        out.append(flat[o:o + n].reshape(1, n))
        o += n
    return out


def _all_reduce_small(v):
    n_dev = 8

    def body(v_ref, o_ref, land, send, recv):
        x, y, c = _me()
        me = 4 * x + 2 * y + c
        land[me] = v_ref[...]
        cps = []
        for r in range(1, n_dev):
            fx, fy, fc = (r >> 2) & 1, (r >> 1) & 1, r & 1
            peer = (x ^ fx, y ^ fy, c ^ fc)
            cps.append(pltpu.make_async_remote_copy(
                src_ref=v_ref, dst_ref=land.at[me], send_sem=send.at[r - 1], recv_sem=recv.at[r - 1],
                device_id=peer, device_id_type=MESH))
        for cp in cps:
            cp.start()
        for r in range(1, n_dev):
            fx, fy, fc = (r >> 2) & 1, (r >> 1) & 1, r & 1
            src = 4 * (x ^ fx) + 2 * (y ^ fy) + (c ^ fc)
            pltpu.make_async_remote_copy(
                src_ref=v_ref, dst_ref=land.at[src], send_sem=send.at[r - 1], recv_sem=recv.at[r - 1],
                device_id=(x ^ fx, y ^ fy, c ^ fc), device_id_type=MESH).wait_recv()
        for cp in cps:
            cp.wait_send()
        acc = land[0]
        for r in range(1, n_dev):
            acc = acc + land[r]
        o_ref[...] = acc

    vm = pl.BlockSpec(memory_space=pltpu.VMEM)
    return pl.pallas_call(
        body, in_specs=[vm], out_specs=vm, out_shape=SDS(v.shape, F32),
        scratch_shapes=[pltpu.VMEM((n_dev,) + v.shape, F32), pltpu.SemaphoreType.DMA((n_dev - 1,)),
                        pltpu.SemaphoreType.DMA((n_dev - 1,))],
        name="all_reduce_small")(v)


def kernel(x, mem, norm_gain, mem_norm_gain, w_in, b_forget, q_gain_a, k_gain_a, sinks_a, q_gain_b, k_gain_b, q_gain_c, k_gain_c, w_mem_kv, w_branch_a, w_branch_b, w_branch_c, w_out, loss_target, m_norm_gain, m_mem_norm_gain, m_w_in, m_b_forget, m_q_gain_a, m_k_gain_a, m_sinks_a, m_q_gain_b, m_k_gain_b, m_q_gain_c, m_k_gain_c, m_w_mem_kv, m_w_branch_a, m_w_branch_b, m_w_branch_c, m_w_out, v_norm_gain, v_mem_norm_gain, v_w_in, v_b_forget, v_q_gain_a, v_k_gain_a, v_sinks_a, v_q_gain_b, v_k_gain_b, v_q_gain_c, v_k_gain_c, v_w_mem_kv, v_w_branch_a, v_w_branch_b, v_w_branch_c, v_w_out):
    xi, yi, ci = lax.axis_index("x"), lax.axis_index("y"), lax.axis_index("c")
    chip = jnp.reshape(2 * xi + yi, (1,)).astype(jnp.int32)
    core = jnp.reshape(ci, (1,)).astype(jnp.int32)

    slabs = _pack_w_in(chip, w_in[0])
    mine = [w_mem_kv[0].astype(BF16), w_branch_a[0].astype(BF16), w_branch_b[0].astype(BF16),
            w_branch_c[0].astype(BF16), w_out[0].astype(BF16)]
    w_main, w_fb = _merge_slabs(_all_gather_slabs(slabs))

    r = _local_step(x[0], mem[0], loss_target[0], w_main, w_fb, mine, norm_gain, mem_norm_gain,
                    b_forget, q_gain_a, k_gain_a, sinks_a, q_gain_b, k_gain_b, q_gain_c, k_gain_c, core=core)
    sums, theirs = r["sums"], r["theirs"]

    small_names = ["d_gain", "d_mem_gain", "d_bf", "d_qga", "d_kga", "d_sinks", "d_qgb", "d_kgb", "d_qgc", "d_kgc"]
    loss_part = (0.5 / D_MODEL) * jnp.sum(r["sq"], axis=1, keepdims=True)
    packed = _pack_small([r[n] for n in small_names] + [loss_part])
    red = _all_reduce_small(packed)
    small_w = [norm_gain, mem_norm_gain, b_forget, q_gain_a, k_gain_a, sinks_a, q_gain_b, k_gain_b, q_gain_c, k_gain_c]
    small_m = [m_norm_gain, m_mem_norm_gain, m_b_forget, m_q_gain_a, m_k_gain_a, m_sinks_a, m_q_gain_b, m_k_gain_b,
               m_q_gain_c, m_k_gain_c]
    small_v = [v_norm_gain, v_mem_norm_gain, v_b_forget, v_q_gain_a, v_k_gain_a, v_sinks_a, v_q_gain_b, v_k_gain_b,
               v_q_gain_c, v_k_gain_c]
    sizes = [w.shape[1] for w in small_w]
    s_d, s_m, s_v = _adamw(red, _pack_small(small_w), _pack_small(small_m), _pack_small(small_v), tr=8, name="adamw_small")
    g_small = _unpack_small(red, sizes + [1])
    loss = g_small[-1].reshape(())
    d_small, m_small, v_small = _unpack_small(s_d, sizes), _unpack_small(s_m, sizes), _unpack_small(s_v, sizes)

    gw_in, dw_in, mw_in, vw_in = _adamw_w_in(jnp.concatenate([chip, core]), sums[0], theirs[0], sums[1], theirs[1],
                                             w_in[0], m_w_in[0], v_w_in[0])
    big = {}
    for t, nm, w, m, v in ((2, "w_mem_kv", w_mem_kv, m_w_mem_kv, v_w_mem_kv),
                           (3, "w_branch_a", w_branch_a, m_w_branch_a, v_w_branch_a),
                           (4, "w_branch_b", w_branch_b, m_w_branch_b, v_w_branch_b),
                           (5, "w_branch_c", w_branch_c, m_w_branch_c, v_w_branch_c),
                           (6, "w_out", w_out, m_w_out, v_w_out)):
        big[nm] = _adamw_halves(sums[t], theirs[t], core, w[0], m[0], v[0], axis=HALF_AXIS[t], tr=128,
                                name="adamw_" + nm)

    def collect(kind):
        sm = (g_small, d_small, m_small, v_small)[kind]
        win = (gw_in, dw_in, mw_in, vw_in)[kind]
        return ([sm[0], sm[1], win[None]] + [a for a in sm[2:10]]
                + [big[n][kind][None] for n in ("w_mem_kv", "w_branch_a", "w_branch_b", "w_branch_c", "w_out")])

    return (loss, r["grad_x"][None], *collect(0), *collect(1), *collect(2), *collect(3))
```

```python
import functools

import numpy as np
import jax
import jax.numpy as jnp
from jax import lax
from jax.experimental import pallas as pl
from jax.experimental.pallas import tpu as pltpu

F32 = jnp.float32
BF16 = jnp.bfloat16
HI = lax.Precision.HIGHEST
SDS = jax.ShapeDtypeStruct
MESH = pl.DeviceIdType.MESH

D_MODEL = 2048
HEAD_DIM = 64
A_HEADS = 12
A_GROUP = 3
B_HEADS = 12
C_HEADS = 4
C_HEAD_DIM = 128
WINDOW = 128
EPS = 1e-6
NEG = -1e30
LANE = 128

QA, KA, VA, ZA = 0, 768, 1024, 1280
QB, KB, VB, ZB = 2048, 2816, 3584, 4352
QC, ZC = 5120, 5632
GATE = 6144
P_MAIN = 12288
N_FORGET = 12
FORGET_COL = 5120
SHARD_COLS = 3075
SLAB = 3200
SLAB_START = (0, 3072, 6016, 9088)
SLAB_SHIFT = (0, 3, 122, 125)
N_CHIPS = 4

ADAM_LR = 0.001
ADAM_B1 = 0.9
ADAM_B2 = 0.999
ADAM_EPS = 1e-08
ADAM_WD = 0.01
ADAM_STEP = 10

VMEM_LIMIT = 56 * 1024 * 1024
VMEM_WIDE = 62 * 1024 * 1024


def _params(sem, vmem=VMEM_LIMIT):
    return pltpu.CompilerParams(dimension_semantics=sem, vmem_limit_bytes=vmem)


def _win(tr, width, off):
    return pl.BlockSpec((pl.Element(tr), pl.Element(width)), lambda i, *_: (i * tr, off))


def _rowblk(tr, width):
    return pl.BlockSpec((tr, width), lambda i, *_: (i, 0))


def _const(shape):
    nd = len(shape)
    return pl.BlockSpec(shape, lambda *_: (0,) * nd)


def _rms(x, g):
    return x * lax.rsqrt(jnp.mean(x * x, axis=-1, keepdims=True) + EPS) * g


def _head_mean_impl(x2, bd):
    hi = x2.astype(BF16)
    lo = (x2 - hi.astype(F32)).astype(BF16)
    return _dot(hi, bd) + _dot(lo, bd)


@jax.custom_vjp
def _head_mean(x2, bd):
    return _head_mean_impl(x2, bd)


_head_mean.defvjp(lambda x2, bd: (_head_mean_impl(x2, bd), bd),
                  lambda bd, g: (_head_mean_impl(g, bd), jnp.zeros_like(bd)))


def _head_norm(x, g_tiled, bd):
    return x * lax.rsqrt(_head_mean(x * x, bd) + EPS) * g_tiled


def _silu(z):
    return z * jax.nn.sigmoid(z)


def _dot_nt(a, b):
    return lax.dot_general(a, b, (((1,), (1,)), ((), ())), preferred_element_type=F32)


def _dot_tn(a, b):
    return lax.dot_general(a, b, (((0,), (0,)), ((), ())), preferred_element_type=F32)


def _dot(a, b):
    return jnp.dot(a, b, preferred_element_type=F32)


def _swa_fn(qk, vz, qkp, vzp, qg, kg, sinks, bd, bias, first):
    q = _head_norm(qk[:, :768], qg, bd)
    k2 = jnp.concatenate([qkp[:, 768:], qk[:, 768:]], axis=0)
    k2 = _head_norm(k2, kg, bd[:256, :256])
    v2 = jnp.concatenate([vzp[:, :256], vz[:, :256]], axis=0)
    z = vz[:, 256:]
    cols = A_GROUP * WINDOW
    kj = lax.broadcasted_iota(jnp.int32, (2 * WINDOW, cols), 0)
    no_prev = kj < WINDOW * first.astype(jnp.int32)
    qtb = jnp.transpose(q).astype(BF16)
    kb = k2.astype(BF16)
    vtb = jnp.transpose(v2).astype(BF16)
    outs = [None] * A_HEADS
    for g in range(A_HEADS // A_GROUP):
        heads = [A_GROUP * g + u for u in range(A_GROUP)]
        qs = jnp.concatenate([qtb[64 * h:64 * h + 64, :] for h in heads], axis=1)
        s = _dot(kb[:, 64 * g:64 * g + 64], qs) * (HEAD_DIM ** -0.5) + bias[g]
        s = jnp.where(no_prev, NEG, s)
        sink = jnp.concatenate([jnp.broadcast_to(sinks[:, h:h + 1], (1, WINDOW)) for h in heads], axis=1)
        m = lax.stop_gradient(jnp.maximum(jnp.max(s, axis=0, keepdims=True), sink))
        p = jnp.exp(s - m)
        den = jnp.sum(p, axis=0, keepdims=True) + jnp.exp(sink - m)
        o = _dot(vtb[64 * g:64 * g + 64, :], (p * (1.0 / den)).astype(BF16))
        for u, h in enumerate(heads):
            outs[h] = o[:, WINDOW * u:WINDOW * u + WINDOW]
    return jnp.transpose(jnp.concatenate(outs, axis=0)) * _silu(z)


def _swa_bias():
    qi = np.arange(WINDOW)[None, :]
    kj = np.arange(2 * WINDOW)[:, None]
    rel = qi + WINDOW - kj
    valid = (rel >= 0) & (rel < WINDOW)
    out = np.zeros((A_HEADS // A_GROUP, 2 * WINDOW, A_GROUP * WINDOW), np.float32)
    for h in range(A_HEADS):
        slope = np.float32(2.0 ** (-8.0 * (h + 1) / A_HEADS))
        blk = np.where(valid, -slope * rel.astype(np.float32), np.float32(NEG))
        g, u = divmod(h, A_GROUP)
        out[g, :, WINDOW * u:WINDOW * u + WINDOW] = blk
    return jnp.asarray(out)


def _mem_fn(qz, mkv, qg, kg, bd):
    q = _head_norm(qz[:, :512], qg, bd).astype(BF16)
    k = _head_norm(mkv[:, :512], kg, bd).astype(BF16)
    v = mkv[:, 512:].astype(BF16)
    z = qz[:, 512:]
    outs = []
    for h in range(C_HEADS):
        sl = slice(128 * h, 128 * h + 128)
        s = _dot_nt(q[:, sl], k[:, sl]) * (C_HEAD_DIM ** -0.5)
        m = lax.stop_gradient(jnp.max(s, axis=-1, keepdims=True))
        p = jnp.exp(s - m)
        den = jnp.sum(p, axis=-1, keepdims=True)
        outs.append(_dot((p * (1.0 / den)).astype(BF16), v[:, sl]))
    return jnp.concatenate(outs, axis=1) * _silu(z)


def _qn_fn(q, g, bd):
    return _head_norm(q, g, bd) * (HEAD_DIM ** -0.5)


def _kn_fn(k, g, bd):
    return _head_norm(k, g, bd)


def _block_diag(width, hd):
    i = np.arange(width) // hd
    return jnp.asarray((i[:, None] == i[None, :]).astype(np.float32) / hd, BF16)


def _head_sum(width, hd):
    i = np.arange(width) // hd
    return jnp.asarray((i[:, None] == np.arange(LANE)[None, :]).astype(np.float32))


def _rms_fwd(x, g, *, tr, name):
    rows, dm = x.shape

    def body(x_ref, g_ref, o_ref):
        o_ref[...] = _rms(x_ref[...], g_ref[...]).astype(BF16)

    return pl.pallas_call(
        body, grid=(rows // tr,),
        in_specs=[_rowblk(tr, dm), _const((1, dm))],
        out_specs=_rowblk(tr, dm),
        out_shape=SDS((rows, dm), BF16), name=name,
        compiler_params=_params(("parallel",)))(x, g)


def _rms_bwd(x, g, dy, resid, *, tr, name):
    rows, dm = x.shape
    want_dx = resid is not None

    def body(*refs):
        if want_dx:
            x_ref, g_ref, dy_ref, r_ref, dx_ref, dg_ref = refs
        else:
            x_ref, g_ref, dy_ref, dg_ref = refs
        _, vjp = jax.vjp(_rms, x_ref[...], g_ref[...])
        dx, dg = vjp(dy_ref[...])

        @pl.when(pl.program_id(0) == 0)
        def _():
            dg_ref[...] = jnp.zeros_like(dg_ref)

        dg_ref[...] += dg
        if want_dx:
            dx_ref[...] = r_ref[...] + dx

    ins = [x, g, dy] + ([resid] if want_dx else [])
    in_specs = [_rowblk(tr, dm), _const((1, dm)), _rowblk(tr, dm)] + ([_rowblk(tr, dm)] if want_dx else [])
    out_specs = ([_rowblk(tr, dm)] if want_dx else []) + [_const((1, dm))]
    out_shape = ([SDS((rows, dm), F32)] if want_dx else []) + [SDS((1, dm), F32)]
    return pl.pallas_call(
        body, grid=(rows // tr,), in_specs=in_specs, out_specs=out_specs, out_shape=out_shape, name=name,
        compiler_params=_params(("arbitrary",)))(*ins)


class _Comm:
    def __init__(self, ins, out_shapes, sems, start, finish):
        self.ins, self.out_shapes, self.sems, self.start, self.finish = list(ins), list(out_shapes), list(sems), start, finish


def _run_comm(comm, name):
    n_in, n_out = len(comm.ins), len(comm.out_shapes)

    def body(*refs):
        ins, outs, sems = refs[:n_in], refs[n_in:n_in + n_out], refs[n_in + n_out:]
        comm.start(ins, outs, sems)
        comm.finish(ins, outs, sems)

    hbm = pl.BlockSpec(memory_space=pl.ANY)
    return pl.pallas_call(body, in_specs=[hbm] * n_in, out_specs=[hbm] * n_out, out_shape=comm.out_shapes,
                          scratch_shapes=comm.sems, name=name)(*comm.ins)


def _matmul(a, b, *, dims, out_dtype, tm, tn, tk, name, add=None, comms=(), vmem=VMEM_LIMIT):
    a_list = list(a) if isinstance(a, (list, tuple)) else [a]
    b_list = list(b) if isinstance(b, (list, tuple)) else [b]
    assert len(a_list) == 1 or dims == "nt"
    assert len(b_list) == 1 or dims == "tn"
    if dims == "tn":
        kdim, m = a_list[0].shape
    else:
        m, kdim = a_list[0].shape[0], sum(p.shape[1] for p in a_list)
    n = b_list[0].shape[0] if dims == "nt" else sum(p.shape[1] for p in b_list)
    tm, tn, tk = min(tm, m), min(tn, n), min(tk, kdim)
    assert m % tm == 0 and n % tn == 0 and kdim % tk == 0, (name, m, n, kdim)
    ni, nj, nk = m // tm, n // tn, kdim // tk
    a_rng, b_rng, pos = [], [], 0
    for p in a_list:
        assert len(a_list) == 1 or p.shape[1] % tk == 0
        a_rng.append((pos, p.shape[1] // tk if len(a_list) > 1 else nk))
        pos += a_rng[-1][1]
    pos = 0
    for p in b_list:
        assert len(b_list) == 1 or p.shape[1] % tn == 0
        b_rng.append((pos, p.shape[1] // tn if len(b_list) > 1 else nj))
        pos += b_rng[-1][1]
    has_add = add is not None
    n_mm_in = len(a_list) + len(b_list) + (1 if has_add else 0)
    c_in = [len(c.ins) for c in comms]
    c_out = [len(c.out_shapes) for c in comms]
    c_sem = [len(c.sems) for c in comms]

    def body(*refs):
        a_refs, b_refs = refs[:len(a_list)], refs[len(a_list):len(a_list) + len(b_list)]
        add_ref = refs[n_mm_in - 1] if has_add else None
        pos = n_mm_in
        cin = []
        for cnt in c_in:
            cin.append(refs[pos:pos + cnt])
            pos += cnt
        o_ref = refs[pos]
        pos += 1
        cout = []
        for cnt in c_out:
            cout.append(refs[pos:pos + cnt])
            pos += cnt
        acc = refs[pos]
        pos += 1
        csem = []
        for cnt in c_sem:
            csem.append(refs[pos:pos + cnt])
            pos += cnt
        i, j, k = pl.program_id(0), pl.program_id(1), pl.program_id(2)

        if comms:
            @pl.when((i == 0) & (j == 0) & (k == 0))
            def _():
                for c, ci, co, cs in zip(comms, cin, cout, csem):
                    c.start(ci, co, cs)

        def accumulate(a_ref, b_ref, first_k, later_k):
            if dims == "nn":
                part = _dot(a_ref[...], b_ref[...])
            elif dims == "nt":
                part = _dot_nt(a_ref[...], b_ref[...])
            else:
                part = _dot_tn(a_ref[...], b_ref[...])

            if first_k:
                @pl.when(k == 0)
                def _():
                    acc[...] = part + add_ref[...] if has_add else part

            if later_k:
                @pl.when(k > 0)
                def _():
                    acc[...] += part

        if len(a_list) > 1:
            for a_ref, (k0, cnt) in zip(a_refs, a_rng):
                @pl.when((k >= k0) & (k < k0 + cnt))
                def _(a_ref=a_ref, k0=k0, cnt=cnt):
                    accumulate(a_ref, b_refs[0], k0 == 0, k0 + cnt > 1)
        elif len(b_list) > 1:
            for b_ref, (j0, cnt) in zip(b_refs, b_rng):
                @pl.when((j >= j0) & (j < j0 + cnt))
                def _(b_ref=b_ref):
                    accumulate(a_refs[0], b_ref, True, nk > 1)
        else:
            accumulate(a_refs[0], b_refs[0], True, nk > 1)

        @pl.when(k == nk - 1)
        def _():
            o_ref[...] = acc[...].astype(out_dtype)

        if comms:
            @pl.when((i == ni - 1) & (j == nj - 1) & (k == nk - 1))
            def _():
                for c, ci, co, cs in zip(comms, cin, cout, csem):
                    c.finish(ci, co, cs)

    def a_spec(k0, cnt):
        if dims == "tn":
            return pl.BlockSpec((tk, tm), lambda i, j, k: (k, i))
        return pl.BlockSpec((tm, tk), lambda i, j, k: (i, jnp.clip(k - k0, 0, cnt - 1)))

    def b_spec(j0, cnt):
        if dims == "nt":
            return pl.BlockSpec((tn, tk), lambda i, j, k: (j, k))
        return pl.BlockSpec((tk, tn), lambda i, j, k: (k, jnp.clip(j - j0, 0, cnt - 1)))

    o_spec = pl.BlockSpec((tm, tn), lambda i, j, k: (i, j))
    hbm = pl.BlockSpec(memory_space=pl.ANY)
    ins = a_list + b_list + ([add] if has_add else []) + [x for c in comms for x in c.ins]
    in_specs = ([a_spec(*r) for r in a_rng] + [b_spec(*r) for r in b_rng] + ([o_spec] if has_add else [])
                + [hbm] * sum(c_in))
    out_specs = [o_spec] + [hbm] * sum(c_out)
    out_shape = [SDS((m, n), out_dtype)] + [s for c in comms for s in c.out_shapes]
    scratch = [pltpu.VMEM((tm, tn), F32)] + [s for c in comms for s in c.sems]
    sem = ("arbitrary",) * 3 if comms else ("parallel", "parallel", "arbitrary")
    res = pl.pallas_call(
        body, grid=(ni, nj, nk), in_specs=in_specs, out_specs=out_specs, out_shape=out_shape, scratch_shapes=scratch,
        name=name, compiler_params=_params(sem, vmem))(*ins)
    if not comms:
        return res[0]
    outs, pos = [], 1
    for cnt in c_out:
        outs.append(list(res[pos:pos + cnt]))
        pos += cnt
    return res[0], outs


def _swa_specs(nb):
    prev = lambda off: pl.BlockSpec((pl.Element(WINDOW), pl.Element(1024)),
                                    lambda n: (jnp.maximum(n - 1, 0) * WINDOW, off))
    return [_win(WINDOW, 1024, QA), _win(WINDOW, 1024, VA), prev(QA), prev(VA),
            _const((1, 768)), _const((1, 256)), _const((1, LANE)), _const((768, 768)),
            _const((A_HEADS // A_GROUP, 2 * WINDOW, A_GROUP * WINDOW))]


def _swa_fwd(proj, qg, kg, sinks, bd, bias):
    s = proj.shape[0]
    nb = s // WINDOW

    def body(qk_ref, vz_ref, qkp_ref, vzp_ref, qg_ref, kg_ref, sk_ref, bd_ref, bias_ref, o_ref):
        first = pl.program_id(0) == 0
        o_ref[...] = _swa_fn(qk_ref[...], vz_ref[...], qkp_ref[...], vzp_ref[...], qg_ref[...], kg_ref[...],
                             sk_ref[...], bd_ref[...], bias_ref[...], first).astype(BF16)

    return pl.pallas_call(
        body, grid=(nb,), in_specs=_swa_specs(nb), out_specs=_rowblk(WINDOW, 768),
        out_shape=SDS((s, 768), BF16), name="swa_fwd",
        compiler_params=_params(("parallel",)))(proj, proj, proj, proj, qg, kg, sinks, bd, bias)


def _swa_bwd(proj, qg, kg, sinks, bd, bias, dga):
    s = proj.shape[0]
    nb = s // WINDOW

    def body(qk_ref, vz_ref, qkp_ref, vzp_ref, qg_ref, kg_ref, sk_ref, bd_ref, bias_ref, dg_ref,
             dcur_ref, dprev_ref, dqg_ref, dkg_ref, dsk_ref):
        first = pl.program_id(0) == 0
        bd_v = bd_ref[...]
        bias_v = bias_ref[...]
        fn = lambda qk, vz, qkp, vzp, qg_, kg_, sk: _swa_fn(qk, vz, qkp, vzp, qg_, kg_, sk, bd_v, bias_v, first)
        _, vjp = jax.vjp(fn, qk_ref[...], vz_ref[...], qkp_ref[...], vzp_ref[...], qg_ref[...], kg_ref[...], sk_ref[...])
        dqk, dvz, dqkp, dvzp, dqg, dkg, dsk = vjp(dg_ref[...])

        @pl.when(first)
        def _():
            dqg_ref[...] = jnp.zeros_like(dqg_ref)
            dkg_ref[...] = jnp.zeros_like(dkg_ref)
            dsk_ref[...] = jnp.zeros_like(dsk_ref)

        dqg_ref[...] += dqg
        dkg_ref[...] += dkg
        dsk_ref[...] += dsk
        dcur_ref[...] = jnp.concatenate([dqk, dvz], axis=1)
        dprev_ref[...] = jnp.concatenate([dqkp[:, 768:], dvzp[:, :256]], axis=1)

    return pl.pallas_call(
        body, grid=(nb,), in_specs=_swa_specs(nb) + [_rowblk(WINDOW, 768)],
        out_specs=[_rowblk(WINDOW, 2048), pl.BlockSpec((None, WINDOW, 512), lambda n: (n, 0, 0)),
                   _const((1, 768)), _const((1, 256)), _const((1, LANE))],
        out_shape=[SDS((s, 2048), F32), SDS((nb, WINDOW, 512), F32), SDS((1, 768), F32), SDS((1, 256), F32),
                   SDS((1, LANE), F32)],
        name="swa_bwd", compiler_params=_params(("arbitrary",)))(proj, proj, proj, proj, qg, kg, sinks, bd, bias, dga)


def _swa_combine(dcur, dprev):
    s = dcur.shape[0]
    nb = s // WINDOW

    def body(c_ref, p_ref, o_ref):
        c = c_ref[...]
        nxt = jnp.where(pl.program_id(0) == nb - 1, 0.0, p_ref[...])
        o_ref[...] = jnp.concatenate([c[:, :768], c[:, 768:1280] + nxt, c[:, 1280:]], axis=1).astype(BF16)

    return pl.pallas_call(
        body, grid=(nb,),
        in_specs=[_rowblk(WINDOW, 2048), pl.BlockSpec((None, WINDOW, 512), lambda n: (jnp.minimum(n + 1, nb - 1), 0, 0))],
        out_specs=_rowblk(WINDOW, 2048), out_shape=SDS((s, 2048), BF16), name="swa_combine",
        compiler_params=_params(("parallel",)))(dcur, dprev)


def _mem_fwd(proj, mkv, qg, kg, bd, *, tr):
    s = proj.shape[0]

    def body(qz_ref, mkv_ref, qg_ref, kg_ref, bd_ref, o_ref):
        o_ref[...] = _mem_fn(qz_ref[...], mkv_ref[...], qg_ref[...], kg_ref[...], bd_ref[...]).astype(BF16)

    return pl.pallas_call(
        body, grid=(s // tr,),
        in_specs=[_win(tr, 1024, QC), _const(mkv.shape), _const((1, 512)), _const((1, 512)), _const((512, 512))],
        out_specs=_rowblk(tr, 512), out_shape=SDS((s, 512), BF16), name="mem_fwd",
        compiler_params=_params(("parallel",)))(proj, mkv, qg, kg, bd)


def _mem_bwd(proj, mkv, qg, kg, bd, dgc, *, tr):
    s = proj.shape[0]

    def body(qz_ref, mkv_ref, qg_ref, kg_ref, bd_ref, dg_ref, dqz_ref, dmkv_ref, dqg_ref, dkg_ref):
        bd_v = bd_ref[...]
        fn = lambda qz, mkv_, qg_, kg_: _mem_fn(qz, mkv_, qg_, kg_, bd_v)
        _, vjp = jax.vjp(fn, qz_ref[...], mkv_ref[...], qg_ref[...], kg_ref[...])
        dqz, dmkv, dqg, dkg = vjp(dg_ref[...])

        @pl.when(pl.program_id(0) == 0)
        def _():
            dmkv_ref[...] = jnp.zeros_like(dmkv_ref)
            dqg_ref[...] = jnp.zeros_like(dqg_ref)
            dkg_ref[...] = jnp.zeros_like(dkg_ref)

        dmkv_ref[...] += dmkv
        dqg_ref[...] += dqg
        dkg_ref[...] += dkg
        dqz_ref[...] = dqz.astype(BF16)

    return pl.pallas_call(
        body, grid=(s // tr,),
        in_specs=[_win(tr, 1024, QC), _const(mkv.shape), _const((1, 512)), _const((1, 512)), _const((512, 512)),
                  _rowblk(tr, 512)],
        out_specs=[_rowblk(tr, 1024), _const(mkv.shape), _const((1, 512)), _const((1, 512))],
        out_shape=[SDS((s, 1024), BF16), SDS(mkv.shape, F32), SDS((1, 512), F32), SDS((1, 512), F32)],
        name="mem_bwd", compiler_params=_params(("arbitrary",)))(proj, mkv, qg, kg, bd, dgc)


def _log_sigmoid(x):
    return jnp.minimum(x, 0.0) - jnp.log1p(jnp.exp(-jnp.abs(x)))


def _fox_prep(proj, fbl, qg, kg, bfor, bd, *, tr):
    s = proj.shape[0]
    tri = jnp.asarray(np.tril(np.ones((tr, tr), np.float32)))

    def body(q_ref, k_ref, fb_ref, qg_ref, kg_ref, bf_ref, bd_ref, tri_ref, qn_ref, kn_ref, cq_ref, ck_ref, carry):
        @pl.when(pl.program_id(0) == 0)
        def _():
            carry[...] = jnp.zeros_like(carry)

        bd_v = bd_ref[...]
        qn_ref[...] = _qn_fn(q_ref[...], qg_ref[...], bd_v).astype(BF16)
        kn_ref[...] = _kn_fn(k_ref[...], kg_ref[...], bd_v).astype(BF16)
        lane = lax.broadcasted_iota(jnp.int32, (tr, LANE), 1)
        logf = jnp.where(lane < N_FORGET, _log_sigmoid(fb_ref[...] + bf_ref[...]), 0.0)
        c = jnp.dot(tri_ref[...], logf, precision=HI, preferred_element_type=F32) + carry[...]
        cq_ref[...] = c
        ck_ref[...] = jnp.transpose(c)
        carry[...] = c[tr - 1:tr, :]

    return pl.pallas_call(
        body, grid=(s // tr,),
        in_specs=[_win(tr, 768, QB), _win(tr, 768, KB), _rowblk(tr, LANE), _const((1, 768)), _const((1, 768)),
                  _const((1, LANE)), _const((768, 768)), _const((tr, tr))],
        out_specs=[_rowblk(tr, 768), _rowblk(tr, 768), _rowblk(tr, LANE), pl.BlockSpec((LANE, tr), lambda i: (0, i))],
        out_shape=[SDS((s, 768), BF16), SDS((s, 768), BF16), SDS((s, LANE), F32), SDS((LANE, s), F32)],
        scratch_shapes=[pltpu.VMEM((1, LANE), F32)], name="fox_prep",
        compiler_params=_params(("arbitrary",)))(proj, proj, fbl, qg, kg, bfor, bd, tri)


FOX_TQ, FOX_TK = 512, 512
FOX_FWD_TQ, FOX_FWD_TK = 512, 1024


def _fox_tiles(s):
    return min(FOX_TQ, s), min(FOX_TK, s)


def _fox_fwd(proj, qn, kn, cq, ck):
    s = proj.shape[0]
    tq, tk = _fox_tiles(s)
    nq, nk = s // tq, s // tk

    def last_k(i):
        return (i * tq + tq - 1) // tk

    def body(q_ref, k_ref, v_ref, cq_ref, ck_ref, z_ref, gb_ref, yb_ref, lse_ref, acc, m_s, l_s):
        i, j = pl.program_id(0), pl.program_id(1)

        @pl.when(j == 0)
        def _():
            acc[...] = jnp.zeros_like(acc)
            m_s[...] = jnp.full_like(m_s, NEG)
            l_s[...] = jnp.ones_like(l_s)

        @pl.when(j <= last_k(i))
        def _():
            qpos = i * tq + lax.broadcasted_iota(jnp.int32, (tq, tk), 0)
            kpos = j * tk + lax.broadcasted_iota(jnp.int32, (tq, tk), 1)
            mask = kpos <= qpos
            q, k = q_ref[...], k_ref[...]
            v = v_ref[...].astype(BF16)
            cqv, ckv = cq_ref[...], ck_ref[...]
            m_all, l_all = m_s[...], l_s[...]
            lane = lax.broadcasted_iota(jnp.int32, (tq, LANE), 1)
            m_out, l_out = m_all, l_all
            for hp in range(B_HEADS // 2):
                acc_pair = acc[:, 128 * hp:128 * hp + 128]
                new = []
                for u in range(2):
                    h = 2 * hp + u
                    sl = slice(64 * h, 64 * h + 64)
                    sc = _dot_nt(q[:, sl], k[:, sl]) + cqv[:, h:h + 1] - ckv[h:h + 1, :]
                    sc = jnp.where(mask, sc, NEG)
                    m_prev = m_all[:, h:h + 1]
                    m_new = jnp.maximum(m_prev, jnp.max(sc, axis=-1, keepdims=True))
                    alpha = jnp.exp(m_prev - m_new)
                    p = jnp.exp(sc - m_new)
                    l_new = alpha * l_all[:, h:h + 1] + jnp.sum(p, axis=-1, keepdims=True)
                    new.append(alpha * acc_pair[:, 64 * u:64 * u + 64] + _dot(p.astype(BF16), v[:, sl]))
                    m_out = jnp.where(lane == h, m_new, m_out)
                    l_out = jnp.where(lane == h, l_new, l_out)
                acc[:, 128 * hp:128 * hp + 128] = jnp.concatenate(new, axis=1)
            m_s[...] = m_out
            l_s[...] = l_out

        @pl.when(j == nk - 1)
        def _():
            l_all = l_s[...]
            inv = 1.0 / l_all
            a = acc[...]
            y = jnp.concatenate([a[:, 64 * h:64 * h + 64] * inv[:, h:h + 1] for h in range(B_HEADS)], axis=1)
            yb_ref[...] = y
            gb_ref[...] = (y * _silu(z_ref[...])).astype(BF16)
            lse_ref[...] = m_s[...] + jnp.log(l_all)

    kmap = lambda i, j: (jnp.minimum(j, last_k(i)), 0)
    return pl.pallas_call(
        body, grid=(nq, nk),
        in_specs=[pl.BlockSpec((tq, 768), lambda i, j: (i, 0)),
                  pl.BlockSpec((tk, 768), kmap),
                  pl.BlockSpec((pl.Element(tk), pl.Element(768)), lambda i, j: (jnp.minimum(j, last_k(i)) * tk, VB)),
                  pl.BlockSpec((tq, LANE), lambda i, j: (i, 0)),
                  pl.BlockSpec((16, tk), lambda i, j: (0, jnp.minimum(j, last_k(i)))),
                  pl.BlockSpec((pl.Element(tq), pl.Element(768)), lambda i, j: (i * tq, ZB))],
        out_specs=[pl.BlockSpec((tq, 768), lambda i, j: (i, 0)), pl.BlockSpec((tq, 768), lambda i, j: (i, 0)),
                   pl.BlockSpec((tq, LANE), lambda i, j: (i, 0))],
        out_shape=[SDS((s, 768), BF16), SDS((s, 768), F32), SDS((s, LANE), F32)],
        scratch_shapes=[pltpu.VMEM((tq, 768), F32), pltpu.VMEM((tq, LANE), F32), pltpu.VMEM((tq, LANE), F32)],
        name="fox_fwd", compiler_params=_params(("parallel", "arbitrary")))(qn, kn, proj, cq, ck, proj)


def _fox_bwd_pre(proj, yb, dgb, hsum, *, tr):
    s = proj.shape[0]

    def body(z_ref, y_ref, dg_ref, hs_ref, dy_ref, dz_ref, dl_ref):
        z, y, dg = z_ref[...], y_ref[...], dg_ref[...]
        sg = jax.nn.sigmoid(z)
        dy = dg * (z * sg)
        dy_ref[...] = dy.astype(BF16)
        dz_ref[...] = (dg * y * (sg * (1.0 + z * (1.0 - sg)))).astype(BF16)
        dl_ref[...] = jnp.dot(dy * y, hs_ref[...], precision=HI, preferred_element_type=F32)

    return pl.pallas_call(
        body, grid=(s // tr,),
        in_specs=[_win(tr, 768, ZB), _rowblk(tr, 768), _rowblk(tr, 768), _const((768, LANE))],
        out_specs=[_rowblk(tr, 768), _rowblk(tr, 768), _rowblk(tr, LANE)],
        out_shape=[SDS((s, 768), BF16), SDS((s, 768), BF16), SDS((s, LANE), F32)], name="fox_bwd_pre",
        compiler_params=_params(("parallel",)))(proj, yb, dgb, hsum)


def _fox_bwd(proj, qn, kn, cq, ck, lse, delta, dyb):
    s = proj.shape[0]
    tq, tk = _fox_tiles(s)
    nq, nk = s // tq, s // tk

    def first_q(j):
        return (j * tk) // tq

    def body(q_ref, k_ref, v_ref, cq_ref, ck_ref, lse_ref, dl_ref, dy_ref,
             dq_ref, dcq_ref, dk_ref, dv_ref, dck_ref, dk_acc, dv_acc, dck_acc):
        j, i = pl.program_id(0), pl.program_id(1)

        @pl.when((j == 0) & (i == 0))
        def _():
            dq_ref[...] = jnp.zeros_like(dq_ref)
            dcq_ref[...] = jnp.zeros_like(dcq_ref)

        @pl.when(i == 0)
        def _():
            dk_acc[...] = jnp.zeros_like(dk_acc)
            dv_acc[...] = jnp.zeros_like(dv_acc)
            dck_acc[...] = jnp.zeros_like(dck_acc)

        @pl.when(i >= first_q(j))
        def _():
            qpos = i * tq + lax.broadcasted_iota(jnp.int32, (tq, tk), 0)
            kpos = j * tk + lax.broadcasted_iota(jnp.int32, (tq, tk), 1)
            mask = kpos <= qpos
            q, k = q_ref[...], k_ref[...]
            v = v_ref[...].astype(BF16)
            dy = dy_ref[...]
            cqv, ckv, lsev, dlv = cq_ref[...], ck_ref[...], lse_ref[...], dl_ref[...]
            lane = lax.broadcasted_iota(jnp.int32, (tq, LANE), 1)
            rows = pl.ds(pl.multiple_of(i * tq, tq), tq)
            dcq_t = jnp.zeros((tq, LANE), F32)
            for hp in range(B_HEADS // 2):
                dq_new, dk_new, dv_new = [], [], []
                for u in range(2):
                    h = 2 * hp + u
                    sl = slice(64 * h, 64 * h + 64)
                    sc = _dot_nt(q[:, sl], k[:, sl]) + cqv[:, h:h + 1] - ckv[h:h + 1, :]
                    sc = jnp.where(mask, sc, NEG)
                    p = jnp.exp(sc - lsev[:, h:h + 1])
                    dp = _dot_nt(dy[:, sl], v[:, sl])
                    ds = p * (dp - dlv[:, h:h + 1])
                    dsb = ds.astype(BF16)
                    dv_new.append(_dot_tn(p.astype(BF16), dy[:, sl]))
                    dk_new.append(_dot_tn(dsb, q[:, sl]))
                    dq_new.append(_dot(dsb, k[:, sl]))
                    dcq_t = jnp.where(lane == h, jnp.sum(ds, axis=-1, keepdims=True), dcq_t)
                    dck_acc[h:h + 1, :] -= jnp.sum(ds, axis=0, keepdims=True)
                cols = slice(128 * hp, 128 * hp + 128)
                dq_ref[rows, cols] += jnp.concatenate(dq_new, axis=1)
                dk_acc[:, cols] += jnp.concatenate(dk_new, axis=1)
                dv_acc[:, cols] += jnp.concatenate(dv_new, axis=1)
            dcq_ref[rows, :] += dcq_t

        @pl.when(i == nq - 1)
        def _():
            dk_ref[...] = dk_acc[...]
            dv_ref[...] = dv_acc[...].astype(BF16)
            dck_ref[...] = jnp.concatenate([dck_acc[...], jnp.zeros((LANE - 16, tk), F32)], axis=0)

    qmap = lambda j, i: (jnp.maximum(i, first_q(j)), 0)
    return pl.pallas_call(
        body, grid=(nk, nq),
        in_specs=[pl.BlockSpec((tq, 768), qmap),
                  pl.BlockSpec((tk, 768), lambda j, i: (j, 0)),
                  pl.BlockSpec((pl.Element(tk), pl.Element(768)), lambda j, i: (j * tk, VB)),
                  pl.BlockSpec((tq, LANE), qmap),
                  pl.BlockSpec((16, tk), lambda j, i: (0, j)),
                  pl.BlockSpec((tq, LANE), qmap),
                  pl.BlockSpec((tq, LANE), qmap),
                  pl.BlockSpec((tq, 768), qmap)],
        out_specs=[_const((s, 768)), _const((s, LANE)),
                   pl.BlockSpec((tk, 768), lambda j, i: (j, 0)), pl.BlockSpec((tk, 768), lambda j, i: (j, 0)),
                   pl.BlockSpec((LANE, tk), lambda j, i: (0, j))],
        out_shape=[SDS((s, 768), F32), SDS((s, LANE), F32), SDS((s, 768), F32), SDS((s, 768), BF16),
                   SDS((LANE, s), F32)],
        scratch_shapes=[pltpu.VMEM((tk, 768), F32), pltpu.VMEM((tk, 768), F32), pltpu.VMEM((16, tk), F32)],
        name="fox_bwd", compiler_params=_params(("arbitrary", "arbitrary")))(qn, kn, proj, cq, ck, lse, delta, dyb)


def _fox_bwd_post(proj, fbl, qg, kg, bfor, bd, dqn, dkn, dcq, dck, *, tr):
    s = proj.shape[0]
    nb = s // tr
    triu = jnp.asarray(np.triu(np.ones((tr, tr), np.float32)))
    rev = lambda i: nb - 1 - i

    def body(q_ref, k_ref, fb_ref, qg_ref, kg_ref, bf_ref, bd_ref, tri_ref, dqn_ref, dkn_ref, dcq_ref, dck_ref,
             dq_ref, dk_ref, dfb_ref, dqg_ref, dkg_ref, dbf_ref, carry):
        @pl.when(pl.program_id(0) == 0)
        def _():
            carry[...] = jnp.zeros_like(carry)
            dqg_ref[...] = jnp.zeros_like(dqg_ref)
            dkg_ref[...] = jnp.zeros_like(dkg_ref)
            dbf_ref[...] = jnp.zeros_like(dbf_ref)

        bd_v = bd_ref[...]
        _, vjp_q = jax.vjp(lambda q, g: _qn_fn(q, g, bd_v), q_ref[...], qg_ref[...])
        dq, dqg = vjp_q(dqn_ref[...])
        _, vjp_k = jax.vjp(lambda k, g: _kn_fn(k, g, bd_v), k_ref[...], kg_ref[...])
        dk, dkg = vjp_k(dkn_ref[...])
        dq_ref[...] = dq.astype(BF16)
        dk_ref[...] = dk.astype(BF16)
        dqg_ref[...] += dqg
        dkg_ref[...] += dkg

        dc = dcq_ref[...] + jnp.transpose(dck_ref[...])
        dlogf = jnp.dot(tri_ref[...], dc, precision=HI, preferred_element_type=F32) + carry[...]
        carry[...] = dlogf[0:1, :]
        lane = lax.broadcasted_iota(jnp.int32, (tr, LANE), 1)
        xf = fb_ref[...] + bf_ref[...]
        dfb = jnp.where(lane < N_FORGET, dlogf * jax.nn.sigmoid(-xf), 0.0)
        dfb_ref[...] = dfb.astype(BF16)
        dbf_ref[...] += jnp.sum(dfb, axis=0, keepdims=True)

    rb = lambda w: pl.BlockSpec((tr, w), lambda i: (rev(i), 0))
    wn = lambda w, off: pl.BlockSpec((pl.Element(tr), pl.Element(w)), lambda i: (rev(i) * tr, off))
    return pl.pallas_call(
        body, grid=(nb,),
        in_specs=[wn(768, QB), wn(768, KB), rb(LANE), _const((1, 768)), _const((1, 768)), _const((1, LANE)),
                  _const((768, 768)), _const((tr, tr)), rb(768), rb(768), rb(LANE),
                  pl.BlockSpec((LANE, tr), lambda i: (0, rev(i)))],
        out_specs=[rb(768), rb(768), rb(LANE), _const((1, 768)), _const((1, 768)), _const((1, LANE))],
        out_shape=[SDS((s, 768), BF16), SDS((s, 768), BF16), SDS((s, LANE), BF16), SDS((1, 768), F32),
                   SDS((1, 768), F32), SDS((1, LANE), F32)],
        scratch_shapes=[pltpu.VMEM((1, LANE), F32)], name="fox_bwd_post",
        compiler_params=_params(("arbitrary",)))(proj, proj, fbl, qg, kg, bfor, bd, triu, dqn, dkn, dcq, dck)


AUG = 128 * B_HEADS
COL_A, COL_B = 64, 67


def _split3(c):
    hi = c.astype(BF16)
    r1 = c - hi.astype(F32)
    mid = r1.astype(BF16)
    lo = (r1 - mid.astype(F32)).astype(BF16)
    return hi, mid, lo


def _expand_mats():
    def mat(col0):
        e = np.zeros((768 + 3 * LANE, AUG), np.float32)
        for h in range(B_HEADS):
            for d in range(HEAD_DIM):
                e[64 * h + d, 128 * h + d] = 1.0
            for part in range(3):
                e[768 + LANE * part + h, 128 * h + col0 + part] = 1.0
        return e

    def ones(col0):
        o = np.zeros((1, AUG), np.float32)
        for h in range(B_HEADS):
            o[0, 128 * h + col0:128 * h + col0 + 3] = 1.0
        return o

    return (jnp.asarray(mat(COL_A), BF16), jnp.asarray(mat(COL_B), BF16), jnp.asarray(ones(COL_A)), jnp.asarray(ones(COL_B)))


def _augment(data_bf16, triple, emat, ones_row):
    parts = [data_bf16] + (list(triple) if triple is not None else [jnp.zeros((data_bf16.shape[0], LANE), BF16)] * 3)
    wide = _dot(jnp.concatenate(parts, axis=1), emat)
    if ones_row is not None:
        wide = wide + ones_row
    return wide


def _compact(wide):
    return jnp.concatenate([wide[:, 128 * h:128 * h + 64] for h in range(wide.shape[1] // 128)], axis=1)


def _lane_of_heads(wide, col, first=0):
    rows = wide.shape[0]
    lane = lax.broadcasted_iota(jnp.int32, (rows, LANE), 1)
    out = jnp.zeros((rows, LANE), F32)
    for h in range(wide.shape[1] // 128):
        out = jnp.where(lane == first + h, wide[:, 128 * h + col:128 * h + col + 1], out)
    return out


def _fox2_prep(proj, fbl, qg, kg, bfor, bd, ea, eb, ones_a, ones_b, *, tr):
    s = proj.shape[0]
    tri = jnp.asarray(np.tril(np.ones((tr, tr), np.float32)))

    def body(q_ref, k_ref, v_ref, fb_ref, qg_ref, kg_ref, bf_ref, bd_ref, tri_ref, ea_ref, eb_ref, oa_ref, ob_ref,
             qat_ref, ka_ref, kat_ref, va_ref, vat_ref, qn_ref, c_ref, carry):
        @pl.when(pl.program_id(0) == 0)
        def _():
            carry[...] = jnp.zeros_like(carry)

        bd_v = bd_ref[...]
        lane = lax.broadcasted_iota(jnp.int32, (tr, LANE), 1)
        logf = jnp.where(lane < N_FORGET, _log_sigmoid(fb_ref[...] + bf_ref[...]), 0.0)
        c = jnp.dot(tri_ref[...], logf, precision=HI, preferred_element_type=F32) + carry[...]
        c_ref[...] = c
        carry[...] = c[tr - 1:tr, :]
        qn = _qn_fn(q_ref[...], qg_ref[...], bd_v).astype(BF16)
        kn = _kn_fn(k_ref[...], kg_ref[...], bd_v).astype(BF16)
        qn_ref[...] = qn
        qat_ref[...] = jnp.transpose(_augment(qn, _split3(c), ea_ref[...], ob_ref[...])).astype(BF16)
        ka = _augment(kn, _split3(-c), eb_ref[...], oa_ref[...])
        ka_ref[...] = ka.astype(BF16)
        kat_ref[...] = jnp.transpose(ka).astype(BF16)
        va = _augment(v_ref[...].astype(BF16), None, ea_ref[...], oa_ref[...])
        va_ref[...] = va.astype(BF16)
        vat_ref[...] = jnp.transpose(va).astype(BF16)

    emat = _const((768 + 3 * LANE, AUG))
    return pl.pallas_call(
        body, grid=(s // tr,),
        in_specs=[_win(tr, 768, QB), _win(tr, 768, KB), _win(tr, 768, VB), _rowblk(tr, LANE), _const((1, 768)),
                  _const((1, 768)), _const((1, LANE)), _const((768, 768)), _const((tr, tr)), emat, emat,
                  _const((1, AUG)), _const((1, AUG))],
        out_specs=[pl.BlockSpec((AUG, tr), lambda i: (0, i)), _rowblk(tr, AUG), pl.BlockSpec((AUG, tr), lambda i: (0, i)),
                   _rowblk(tr, AUG), pl.BlockSpec((AUG, tr), lambda i: (0, i)), _rowblk(tr, 768), _rowblk(tr, LANE)],
        out_shape=[SDS((AUG, s), BF16), SDS((s, AUG), BF16), SDS((AUG, s), BF16), SDS((s, AUG), BF16),
                   SDS((AUG, s), BF16), SDS((s, 768), BF16), SDS((s, LANE), F32)],
        scratch_shapes=[pltpu.VMEM((1, LANE), F32)], name="fox_prep",
        compiler_params=_params(("arbitrary",)))(proj, proj, proj, fbl, qg, kg, bfor, bd, tri, ea, eb, ones_a, ones_b)


def _fox2_fwd(proj, qat, ka, vat):
    s = proj.shape[0]
    tq, tk = min(FOX_FWD_TQ, s), min(FOX_FWD_TK, s)
    nq, nk = s // tq, s // tk

    def last_k(i):
        return (i * tq + tq - 1) // tk

    def body(qt_ref, k_ref, vt_ref, z_ref, gb_ref, yb_ref, lse_ref, acc, m_s):
        i, j = pl.program_id(0), pl.program_id(1)

        @pl.when(j == 0)
        def _():
            acc[...] = jnp.zeros_like(acc)
            m_s[...] = jnp.full_like(m_s, NEG)

        def tile(masked):
            if masked:
                kpos = j * tk + lax.broadcasted_iota(jnp.int32, (tk, tq), 0)
                qpos = i * tq + lax.broadcasted_iota(jnp.int32, (tk, tq), 1)
                mask = kpos <= qpos
            for h in range(B_HEADS):
                sl = slice(128 * h, 128 * h + 128)
                sc = _dot(k_ref[:, sl], qt_ref[sl, :])
                if masked:
                    sc = jnp.where(mask, sc, NEG)
                m_prev = m_s[h:h + 1, :]
                m_new = jnp.maximum(m_prev, jnp.max(sc, axis=0, keepdims=True))
                p = jnp.exp(sc - m_new).astype(BF16)
                acc[sl, :] = jnp.exp(m_prev - m_new) * acc[sl, :] + _dot(vt_ref[sl, :], p)
                m_s[h:h + 1, :] = m_new

        full = j * tk + tk - 1 <= i * tq

        @pl.when(full)
        def _():
            tile(False)

        @pl.when(jnp.logical_and(jnp.logical_not(full), j <= last_k(i)))
        def _():
            tile(True)

        @pl.when(j == nk - 1)
        def _():
            outs = []
            row = lax.broadcasted_iota(jnp.int32, (LANE, tq), 0)
            lse_t = jnp.zeros((LANE, tq), F32)
            for h in range(B_HEADS):
                l_row = acc[128 * h + COL_A:128 * h + COL_A + 1, :]
                outs.append(acc[128 * h:128 * h + 64, :] * (1.0 / l_row))
                lse_t = jnp.where(row == h, m_s[h:h + 1, :] + jnp.log(l_row), lse_t)
            y = jnp.transpose(jnp.concatenate(outs, axis=0))
            yb_ref[...] = y
            gb_ref[...] = (y * _silu(z_ref[...])).astype(BF16)
            lse_ref[...] = jnp.transpose(lse_t)

    kcol = lambda i, j: (0, jnp.minimum(j, last_k(i)))
    return pl.pallas_call(
        body, grid=(nq, nk),
        in_specs=[pl.BlockSpec((AUG, tq), lambda i, j: (0, i)),
                  pl.BlockSpec((tk, AUG), lambda i, j: (jnp.minimum(j, last_k(i)), 0)),
                  pl.BlockSpec((AUG, tk), kcol),
                  pl.BlockSpec((pl.Element(tq), pl.Element(768)), lambda i, j: (i * tq, ZB))],
        out_specs=[pl.BlockSpec((tq, 768), lambda i, j: (i, 0)), pl.BlockSpec((tq, 768), lambda i, j: (i, 0)),
                   pl.BlockSpec((tq, LANE), lambda i, j: (i, 0))],
        out_shape=[SDS((s, 768), BF16), SDS((s, 768), F32), SDS((s, LANE), F32)],
        scratch_shapes=[pltpu.VMEM((AUG, tq), F32), pltpu.VMEM((16, tq), F32)],
        name="fox_fwd", compiler_params=_params(("parallel", "arbitrary")))(qat, ka, vat, proj)


def _fox2_bwd_pre(proj, yb, dgb, qn, c, lse, hsum, ea, ones_b, *, tr):
    s = proj.shape[0]

    def body(z_ref, y_ref, dg_ref, qn_ref, c_ref, lse_ref, hs_ref, ea_ref, ob_ref,
             qa_ref, qat_ref, dya_ref, dyat_ref, dz_ref):
        z, y, dg = z_ref[...], y_ref[...], dg_ref[...]
        sg = jax.nn.sigmoid(z)
        dy = dg * (z * sg)
        dz_ref[...] = (dg * y * (sg * (1.0 + z * (1.0 - sg)))).astype(BF16)
        delta = jnp.dot(dy * y, hs_ref[...], precision=HI, preferred_element_type=F32)
        e = ea_ref[...]
        dya = _augment(dy.astype(BF16), _split3(-delta), e, None)
        dya_ref[...] = dya.astype(BF16)
        dyat_ref[...] = jnp.transpose(dya).astype(BF16)
        qa = _augment(qn_ref[...], _split3(c_ref[...] - lse_ref[...]), e, ob_ref[...])
        qa_ref[...] = qa.astype(BF16)
        qat_ref[...] = jnp.transpose(qa).astype(BF16)

    return pl.pallas_call(
        body, grid=(s // tr,),
        in_specs=[_win(tr, 768, ZB), _rowblk(tr, 768), _rowblk(tr, 768), _rowblk(tr, 768), _rowblk(tr, LANE),
                  _rowblk(tr, LANE), _const((768, LANE)), _const((768 + 3 * LANE, AUG)), _const((1, AUG))],
        out_specs=[_rowblk(tr, AUG), pl.BlockSpec((AUG, tr), lambda i: (0, i)), _rowblk(tr, AUG),
                   pl.BlockSpec((AUG, tr), lambda i: (0, i)), _rowblk(tr, 768)],
        out_shape=[SDS((s, AUG), BF16), SDS((AUG, s), BF16), SDS((s, AUG), BF16), SDS((AUG, s), BF16),
                   SDS((s, 768), BF16)], name="fox_bwd_pre",
        compiler_params=_params(("parallel",)))(proj, yb, dgb, qn, c, lse, hsum, ea, ones_b)


def _fox2_bwd(qb, qbt, ka, kat, va, dya, dyat):
    s = qb.shape[0]
    tq, tk = _fox_tiles(s)
    nq, nk = s // tq, s // tk
    ng = 2
    gh = B_HEADS // ng
    gw = 128 * gh

    def first_q(j):
        return (j * tk) // tq

    def body(q_ref, qt_ref, k_ref, kt_ref, v_ref, dy_ref, dyt_ref, dq_hbm, dk_ref, dv_ref, dck_ref,
             dq_acc, dk_acc, dv_acc, sem):
        g, j, i = pl.program_id(0), pl.program_id(1), pl.program_id(2)

        @pl.when((j == 0) & (i == 0))
        def _():
            dq_acc[...] = jnp.zeros_like(dq_acc)

        @pl.when(i == 0)
        def _():
            dk_acc[...] = jnp.zeros_like(dk_acc)
            dv_acc[...] = jnp.zeros_like(dv_acc)

        def tile(masked):
            if masked:
                kpos = j * tk + lax.broadcasted_iota(jnp.int32, (tk, tq), 0)
                qpos = i * tq + lax.broadcasted_iota(jnp.int32, (tk, tq), 1)
                mask = kpos <= qpos
            cols = pl.ds(pl.multiple_of(i * tq, tq), tq)
            for h in range(gh):
                sl = slice(128 * h, 128 * h + 128)
                sc = _dot(k_ref[:, sl], qt_ref[sl, :])
                if masked:
                    sc = jnp.where(mask, sc, NEG)
                p = jnp.exp(sc)
                ds = (p * _dot(v_ref[:, sl], dyt_ref[sl, :])).astype(BF16)
                dv_acc[:, sl] += _dot(p.astype(BF16), dy_ref[:, sl])
                dk_acc[:, sl] += _dot(ds, q_ref[:, sl])
                dq_acc[sl, cols] += _dot(kt_ref[sl, :], ds)

        full = j * tk + tk - 1 <= i * tq

        @pl.when(full)
        def _():
            tile(False)

        @pl.when(jnp.logical_and(jnp.logical_not(full), i >= first_q(j)))
        def _():
            tile(True)

        @pl.when(i == nq - 1)
        def _():
            dkw = dk_acc[...]
            dk_ref[...] = _compact(dkw)
            dv_ref[...] = _compact(dv_acc[...]).astype(BF16)
            dck_ref[...] = -_lane_of_heads(dkw, COL_B, gh * g)

        @pl.when((j == nk - 1) & (i == nq - 1))
        def _():
            cp = pltpu.make_async_copy(dq_acc, dq_hbm.at[pl.ds(pl.multiple_of(g * gw, gw), gw)], sem)
            cp.start()
            cp.wait()

    qrow = pl.BlockSpec((tq, gw), lambda g, j, i: (jnp.maximum(i, first_q(j)), g))
    qcol = pl.BlockSpec((gw, tq), lambda g, j, i: (g, jnp.maximum(i, first_q(j))))
    krow = pl.BlockSpec((tk, gw), lambda g, j, i: (j, g))
    kcol = pl.BlockSpec((gw, tk), lambda g, j, i: (g, j))
    kout = pl.BlockSpec((tk, gw // 2), lambda g, j, i: (j, g))
    return pl.pallas_call(
        body, grid=(ng, nk, nq),
        in_specs=[qrow, qcol, krow, kcol, krow, qrow, qcol],
        out_specs=[pl.BlockSpec(memory_space=pl.ANY), kout, kout,
                   pl.BlockSpec((None, tk, LANE), lambda g, j, i: (g, j, 0))],
        out_shape=[SDS((AUG, s), F32), SDS((s, 768), F32), SDS((s, 768), BF16), SDS((ng, s, LANE), F32)],
        scratch_shapes=[pltpu.VMEM((gw, s), F32), pltpu.VMEM((tk, gw), F32), pltpu.VMEM((tk, gw), F32),
                        pltpu.SemaphoreType.DMA],
        name="fox_bwd", compiler_params=_params(("arbitrary",) * 3))(qb, qbt, ka, kat, va, dya, dyat)


def _fox2_bwd_post(proj, fbl, qg, kg, bfor, bd, dqa, dkn, dck, *, tr):
    s = proj.shape[0]
    nb = s // tr
    triu = jnp.asarray(np.triu(np.ones((tr, tr), np.float32)))
    rev = lambda i: nb - 1 - i

    def body(q_ref, k_ref, fb_ref, qg_ref, kg_ref, bf_ref, bd_ref, tri_ref, dqa_ref, dkn_ref, dck_ref,
             dq_ref, dk_ref, dfb_ref, dqg_ref, dkg_ref, dbf_ref, carry):
        @pl.when(pl.program_id(0) == 0)
        def _():
            carry[...] = jnp.zeros_like(carry)
            dqg_ref[...] = jnp.zeros_like(dqg_ref)
            dkg_ref[...] = jnp.zeros_like(dkg_ref)
            dbf_ref[...] = jnp.zeros_like(dbf_ref)

        bd_v = bd_ref[...]
        dqw = jnp.transpose(dqa_ref[...])
        _, vjp_q = jax.vjp(lambda q, g: _qn_fn(q, g, bd_v), q_ref[...], qg_ref[...])
        dq, dqg = vjp_q(_compact(dqw))
        _, vjp_k = jax.vjp(lambda k, g: _kn_fn(k, g, bd_v), k_ref[...], kg_ref[...])
        dk, dkg = vjp_k(dkn_ref[...])
        dq_ref[...] = dq.astype(BF16)
        dk_ref[...] = dk.astype(BF16)
        dqg_ref[...] += dqg
        dkg_ref[...] += dkg

        dc = _lane_of_heads(dqw, COL_A) + (dck_ref[0] + dck_ref[1])
        dlogf = jnp.dot(tri_ref[...], dc, precision=HI, preferred_element_type=F32) + carry[...]
        carry[...] = dlogf[0:1, :]
        lane = lax.broadcasted_iota(jnp.int32, (tr, LANE), 1)
        xf = fb_ref[...] + bf_ref[...]
        dfb = jnp.where(lane < N_FORGET, dlogf * jax.nn.sigmoid(-xf), 0.0)
        dfb_ref[...] = dfb.astype(BF16)
        dbf_ref[...] += jnp.sum(dfb, axis=0, keepdims=True)

    rb = lambda w: pl.BlockSpec((tr, w), lambda i: (rev(i), 0))
    wn = lambda w, off: pl.BlockSpec((pl.Element(tr), pl.Element(w)), lambda i: (rev(i) * tr, off))
    return pl.pallas_call(
        body, grid=(nb,),
        in_specs=[wn(768, QB), wn(768, KB), rb(LANE), _const((1, 768)), _const((1, 768)), _const((1, LANE)),
                  _const((768, 768)), _const((tr, tr)), pl.BlockSpec((AUG, tr), lambda i: (0, rev(i))), rb(768),
                  pl.BlockSpec((2, tr, LANE), lambda i: (0, rev(i), 0))],
        out_specs=[rb(768), rb(768), rb(LANE), _const((1, 768)), _const((1, 768)), _const((1, LANE))],
        out_shape=[SDS((s, 768), BF16), SDS((s, 768), BF16), SDS((s, LANE), BF16), SDS((1, 768), F32),
                   SDS((1, 768), F32), SDS((1, LANE), F32)],
        scratch_shapes=[pltpu.VMEM((1, LANE), F32)], name="fox_bwd_post",
        compiler_params=_params(("arbitrary",)))(proj, proj, fbl, qg, kg, bfor, bd, triu, dqa, dkn, dck)


def _merge_specs(tr):
    row = lambda w: pl.BlockSpec((tr, w), lambda i, j: (i, 0))
    shard = lambda r: pl.BlockSpec((None, r, 512), lambda i, j: (j, 0, 0))
    gate = lambda b: pl.BlockSpec((tr, 512), lambda i, j: (i, (GATE + 2048 * b) // 512 + j))
    return [row(768), row(768), row(512), shard(768), shard(768), shard(512), gate(0), gate(1), gate(2)]


def _merge_fwd(proj, ga, gb, gc, wa, wb, wc, *, tr):
    s = proj.shape[0]

    def body(ga_ref, gb_ref, gc_ref, wa_ref, wb_ref, wc_ref, l0_ref, l1_ref, l2_ref, y_ref):
        ua = _dot(ga_ref[...], wa_ref[...])
        ub = _dot(gb_ref[...], wb_ref[...])
        uc = _dot(gc_ref[...], wc_ref[...])
        y = jax.nn.sigmoid(l0_ref[...]) * ua + jax.nn.sigmoid(l1_ref[...]) * ub + jax.nn.sigmoid(l2_ref[...]) * uc
        y_ref[...] = y.astype(BF16)

    return pl.pallas_call(
        body, grid=(s // tr, N_CHIPS), in_specs=_merge_specs(tr),
        out_specs=pl.BlockSpec((tr, 512), lambda i, j: (i, j)), out_shape=SDS((s, D_MODEL), BF16), name="merge_fwd",
        compiler_params=_params(("parallel", "arbitrary")))(ga, gb, gc, wa, wb, wc, proj, proj, proj)


def _merge_bwd(proj, ga, gb, gc, wa, wb, wc, dy, *, tr):
    s = proj.shape[0]

    def body(ga_ref, gb_ref, gc_ref, wa_ref, wb_ref, wc_ref, l0_ref, l1_ref, l2_ref, dy_ref,
             dl0_ref, dl1_ref, dl2_ref, dua_ref, dub_ref, duc_ref, dga_ref, dgb_ref, dgc_ref):
        j = pl.program_id(1)
        dyv = dy_ref[...]

        @pl.when(j == 0)
        def _():
            dga_ref[...] = jnp.zeros_like(dga_ref)
            dgb_ref[...] = jnp.zeros_like(dgb_ref)
            dgc_ref[...] = jnp.zeros_like(dgc_ref)

        for g_ref, w_ref, l_ref, dl_ref, du_ref, dg_ref in (
                (ga_ref, wa_ref, l0_ref, dl0_ref, dua_ref, dga_ref),
                (gb_ref, wb_ref, l1_ref, dl1_ref, dub_ref, dgb_ref),
                (gc_ref, wc_ref, l2_ref, dl2_ref, duc_ref, dgc_ref)):
            w = w_ref[...]
            u = _dot(g_ref[...], w)
            sg = jax.nn.sigmoid(l_ref[...])
            dl_ref[...] = (dyv * u * sg * (1.0 - sg)).astype(BF16)
            du = (dyv * sg).astype(BF16)
            du_ref[...] = du
            dg_ref[...] += _dot_nt(du, w)

    blk = pl.BlockSpec((tr, 512), lambda i, j: (i, j))
    row = lambda w: pl.BlockSpec((tr, w), lambda i, j: (i, 0))
    big = SDS((s, D_MODEL), BF16)
    return pl.pallas_call(
        body, grid=(s // tr, N_CHIPS), in_specs=_merge_specs(tr) + [blk],
        out_specs=[blk] * 6 + [row(768), row(768), row(512)],
        out_shape=[big] * 6 + [SDS((s, 768), F32), SDS((s, 768), F32), SDS((s, 512), F32)], name="merge_bwd",
        compiler_params=_params(("parallel", "arbitrary")))(ga, gb, gc, wa, wb, wc, proj, proj, proj, dy)


def _out_loss(y, wo, x, tgt, *, tr, tn):
    s = x.shape[0]

    def body(y_ref, w_ref, x_ref, t_ref, d_ref, db_ref, sq_ref):
        @pl.when((pl.program_id(0) == 0) & (pl.program_id(1) == 0))
        def _():
            sq_ref[...] = jnp.zeros_like(sq_ref)

        out = x_ref[...] + _dot(y_ref[...], w_ref[...])
        diff = out - t_ref[...]
        sq_ref[...] += jnp.sum(diff * diff, axis=0, keepdims=True)
        d = diff * (1.0 / D_MODEL)
        d_ref[...] = d
        db_ref[...] = d.astype(BF16)

    blk = pl.BlockSpec((tr, tn), lambda i, j: (i, j))
    return pl.pallas_call(
        body, grid=(s // tr, D_MODEL // tn),
        in_specs=[pl.BlockSpec((tr, D_MODEL), lambda i, j: (i, 0)), pl.BlockSpec((D_MODEL, tn), lambda i, j: (0, j)), blk, blk],
        out_specs=[blk, blk, _const((1, tn))],
        out_shape=[SDS((s, D_MODEL), F32), SDS((s, D_MODEL), BF16), SDS((1, tn), F32)], name="out_loss",
        compiler_params=_params(("arbitrary", "arbitrary")))(y, wo, x, tgt)


def _tile_gain(g, reps):
    return jnp.tile(g.reshape(1, -1), (1, reps))


def _pad_lane(v):
    v = v.reshape(1, -1)
    return jnp.pad(v, ((0, 0), (0, LANE - v.shape[1])))


def _local_step(x, mem, tgt, w_main, w_fb, w_small, norm_gain, mem_norm_gain, b_forget,
                q_gain_a, k_gain_a, sinks_a, q_gain_b, k_gain_b, q_gain_c, k_gain_c, core=None):
    s = x.shape[0]
    tr = min(512, s)
    bd64 = _block_diag(768, HEAD_DIM)
    bd128 = _block_diag(512, C_HEAD_DIM)
    hsum = _head_sum(768, HEAD_DIM)
    qga, kga = _tile_gain(q_gain_a, 12), _tile_gain(k_gain_a, 4)
    qgb, kgb = _tile_gain(q_gain_b, 12), _tile_gain(k_gain_b, 12)
    qgc, kgc = _tile_gain(q_gain_c, 4), _tile_gain(k_gain_c, 4)
    sinks = _pad_lane(sinks_a)
    bfor = _pad_lane(b_forget)

    hn = _rms_fwd(x, norm_gain, tr=tr, name="rms_x")
    on_mesh = core is not None
    if on_mesh:
        proj, (gathered,) = _matmul(hn, w_main, dims="nn", out_dtype=F32, tm=1024, tn=512, tk=D_MODEL, name="proj_main",
                                    comms=[_gather_comm(list(w_small))])
        w_mk, wa, wb, wc, wo = gathered
        w_mk, wo = w_mk.reshape(D_MODEL, 1024), wo.reshape(D_MODEL, D_MODEL)
    else:
        proj = _matmul(hn, w_main, dims="nn", out_dtype=F32, tm=1024, tn=512, tk=D_MODEL, name="proj_main")
        w_mk, wa, wb, wc, wo = w_small
    fbl = _matmul(hn, w_fb, dims="nn", out_dtype=F32, tm=1024, tn=LANE, tk=D_MODEL, name="proj_forget")
    memn = _rms_fwd(mem, mem_norm_gain, tr=mem.shape[0], name="rms_mem")
    mkv = _matmul(memn, w_mk, dims="nn", out_dtype=F32, tm=256, tn=512, tk=D_MODEL, name="mem_kv")

    swa_bias = _swa_bias()
    ga = _swa_fwd(proj, qga, kga, sinks, bd64, swa_bias)
    ea, eb, ones_a, ones_b = _expand_mats()
    tf = min(256, s)
    qat, ka, kat, va, vat, qn, cfox = _fox2_prep(proj, fbl, qgb, kgb, bfor, bd64, ea, eb, ones_a, ones_b, tr=tf)
    gb, yb, lse = _fox2_fwd(proj, qat, ka, vat)
    gc = _mem_fwd(proj, mkv, qgc, kgc, bd128, tr=tr)
    y = _merge_fwd(proj, ga, gb, gc, wa, wb, wc, tr=tr)
    dout, dout_b, sq = _out_loss(y, wo, x, tgt, tr=tr, tn=512)

    d_wo = _matmul(y, dout_b, dims="tn", out_dtype=F32, tm=1024, tn=512, tk=4096, name="dw_out")
    dy = _matmul(dout_b, wo, dims="nt", out_dtype=F32, tm=1024, tn=512, tk=D_MODEL, name="dy")
    dl0, dl1, dl2, dua, dub, duc, dga, dgb, dgc = _merge_bwd(proj, ga, gb, gc, wa, wb, wc, dy, tr=tr)
    d_wa = _matmul(ga, dua, dims="tn", out_dtype=F32, tm=768, tn=512, tk=4096, name="dw_branch_a")
    d_wb = _matmul(gb, dub, dims="tn", out_dtype=F32, tm=768, tn=512, tk=4096, name="dw_branch_b")
    d_wc = _matmul(gc, duc, dims="tn", out_dtype=F32, tm=512, tn=512, tk=4096, name="dw_branch_c")

    dcur, dprev, d_qga, d_kga, d_sinks = _swa_bwd(proj, qga, kga, sinks, bd64, swa_bias, dga)
    dproj_a = _swa_combine(dcur, dprev)

    qab, qabt, dya, dyat, dzb = _fox2_bwd_pre(proj, yb, dgb, qn, cfox, lse, hsum, ea, ones_b, tr=tf)
    dqa, dkn, dvb, dck = _fox2_bwd(qab, qabt, ka, kat, va, dya, dyat)
    dqb, dkb, dfb, d_qgb, d_kgb, d_bf = _fox2_bwd_post(proj, fbl, qgb, kgb, bfor, bd64, dqa, dkn, dck, tr=tf)

    dproj_c, dmkv, d_qgc, d_kgc = _mem_bwd(proj, mkv, qgc, kgc, bd128, dgc, tr=tr)
    dmkv_b = dmkv.astype(BF16)
    d_wmk = _matmul(memn, dmkv_b, dims="tn", out_dtype=F32, tm=1024, tn=512, tk=256, name="dw_mem_kv")
    dmemn = _matmul(dmkv_b, w_mk, dims="nt", out_dtype=F32, tm=256, tn=512, tk=1024, name="dmemn")
    (d_mem_gain,) = _rms_bwd(mem, mem_norm_gain, dmemn, None, tr=mem.shape[0], name="rms_mem_bwd")

    dproj = [dproj_a, jnp.concatenate([dqb, dkb, dvb, dzb, dproj_c], axis=1), dl0, dl1, dl2]
    dhn_f = _matmul(dfb, w_fb, dims="nt", out_dtype=F32, tm=1024, tn=512, tk=LANE, name="dhn_forget")
    d_wfb = _matmul(hn, dfb, dims="tn", out_dtype=F32, tm=1024, tn=LANE, tk=512, name="dw_forget")
    big = {}
    if on_mesh:
        half = D_MODEL // 2
        c0 = core[0]
        hn_other = lax.dynamic_slice(hn, (0, (1 - c0) * half), (s, half))
        hn_own = lax.dynamic_slice(hn, (0, c0 * half), (s, half))
        g1, k1 = [d_wmk, d_wa, d_wb, d_wc, d_wo], [2, 3, 4, 5, 6]
        d_other, (got1,) = _matmul(hn_other, dproj, dims="tn", out_dtype=F32, tm=1024, tn=512, tk=4096,
                                   name="dw_main_other", comms=[_exchange_comm(g1, k1)])
        h1 = [_add_half(g, got, core, HALF_AXIS[k], name=f"add_half_{k}") for g, got, k in zip(g1, got1, k1)]
        d_own, (got0, parts1) = _matmul(
            hn_own, dproj, dims="tn", out_dtype=F32, tm=1024, tn=512, tk=4096, name="dw_main_own",
            comms=[_exchange_comm([d_other, d_wfb], [0, 1], whole=(0,)), _scatter_comm(h1, k1)])
        h0 = [_add_pair(d_own, got0[0], name="add_pair_main"), _add_half(d_wfb, got0[1], core, 0, name="add_half_1")]
        sums1 = [_sum4(p, name=f"sum4_{k}") for p, k in zip(parts1, k1)]
        dhn, (parts0, theirs1) = _matmul(dproj, w_main, dims="nt", out_dtype=F32, tm=1024, tn=512, tk=2048, vmem=VMEM_WIDE, name="dhn",
                                         add=dhn_f, comms=[_scatter_comm(h0, [0, 1]), _swap_comm(sums1)])
        sums0 = [_sum4(p, name=f"sum4_{k}") for p, k in zip(parts0, (0, 1))]
        theirs0 = _run_comm(_swap_comm(sums0), "swap_halves")
        big = dict(sums=sums0 + sums1, theirs=list(theirs0) + list(theirs1))
    else:
        dhn = _matmul(dproj, w_main, dims="nt", out_dtype=F32, tm=1024, tn=512, tk=2048, vmem=VMEM_WIDE, name="dhn", add=dhn_f)
        d_wmain = _matmul(hn, dproj, dims="tn", out_dtype=F32, tm=1024, tn=512, tk=4096, name="dw_main")
        big = dict(d_wmain=d_wmain, d_wfb=d_wfb, d_wmk=d_wmk, d_wa=d_wa, d_wb=d_wb, d_wc=d_wc, d_wo=d_wo)
    grad_x, d_gain = _rms_bwd(x, norm_gain, dhn, dout, tr=tr, name="rms_x_bwd")

    fold = lambda g, reps: jnp.sum(g.reshape(reps, -1), axis=0, keepdims=True)
    return dict(
        sq=sq, grad_x=grad_x, **big,
        d_gain=d_gain, d_mem_gain=d_mem_gain, d_bf=d_bf[:, :N_FORGET],
        d_qga=fold(d_qga, 12), d_kga=fold(d_kga, 4), d_sinks=d_sinks[:, :A_HEADS],
        d_qgb=fold(d_qgb, 12), d_kgb=fold(d_kgb, 12), d_qgc=fold(d_qgc, 4), d_kgc=fold(d_kgc, 4))


PACK_ROWS = 256
FORGET_IN_SHARD = FORGET_COL - SHARD_COLS
AFTER_FORGET = FORGET_COL - SLAB_START[1]
END_CHIP1 = 2 * SHARD_COLS - N_FORGET - SLAB_START[1]


def _pack_w_in(chip, w):
    rows = w.shape[0]
    tr = PACK_ROWS

    def body(k_ref, w_ref, o_ref, scr):
        scr[...] = jnp.zeros_like(scr)
        scr[:, pl.ds(0, SHARD_COLS)] = w_ref[...]
        v = scr[...]
        k = k_ref[0]
        col = lax.broadcasted_iota(jnp.int32, (tr, SLAB), 1)
        no_forget = jnp.zeros((tr, LANE), BF16)

        @pl.when(k == 0)
        def _():
            o_ref[:, 0:SLAB] = v.astype(BF16)
            o_ref[:, SLAB:] = no_forget

        @pl.when(k == 1)
        def _():
            before = pltpu.roll(v, SLAB_SHIFT[1], axis=1)
            after = pltpu.roll(v, SLAB - (N_FORGET - SLAB_SHIFT[1]), axis=1)
            slab = jnp.where(col < AFTER_FORGET, before, jnp.where(col < END_CHIP1, after, 0.0))
            o_ref[:, 0:SLAB] = slab.astype(BF16)
            f = pltpu.roll(v, SLAB - FORGET_IN_SHARD, axis=1)[:, :LANE]
            o_ref[:, SLAB:] = jnp.where(col[:, :LANE] < N_FORGET, f, 0.0).astype(BF16)

        for kk in (2, 3):
            @pl.when(k == kk)
            def _(kk=kk):
                o_ref[:, 0:SLAB] = pltpu.roll(v, SLAB_SHIFT[kk], axis=1).astype(BF16)
                o_ref[:, SLAB:] = no_forget

    return pl.pallas_call(
        body, grid_spec=pltpu.PrefetchScalarGridSpec(
            num_scalar_prefetch=1, grid=(rows // tr,),
            in_specs=[pl.BlockSpec((tr, SHARD_COLS), lambda i, k: (i, 0))],
            out_specs=pl.BlockSpec((None, tr, SLAB + LANE), lambda i, k: (k[0], i, 0)),
            scratch_shapes=[pltpu.VMEM((tr, SLAB), F32)]),
        out_shape=SDS((N_CHIPS, rows, SLAB + LANE), BF16), name="pack_w_in",
        compiler_params=_params(("arbitrary",)))(chip, w)


def _merge_slabs(g):
    rows = g.shape[1]
    tr = PACK_ROWS
    t = [s // LANE for s in SLAB_START]
    n_t = SLAB // LANE

    def body(g_ref, m_ref, f_ref):
        for k in range(N_CHIPS):
            lo = t[k] + (1 if k > 0 else 0)
            hi = t[k + 1] if k + 1 < N_CHIPS else t[k] + n_t
            m_ref[:, lo * LANE:hi * LANE] = g_ref[k, :, (lo - t[k]) * LANE:(hi - t[k]) * LANE]
            if k + 1 < N_CHIPS:
                a = g_ref[k, :, (hi - t[k]) * LANE:(hi - t[k] + 1) * LANE].astype(F32)
                b = g_ref[k + 1, :, 0:LANE].astype(F32)
                m_ref[:, hi * LANE:(hi + 1) * LANE] = (a + b).astype(BF16)
        f_ref[...] = g_ref[1, :, SLAB:]

    return pl.pallas_call(
        body, grid=(rows // tr,),
        in_specs=[pl.BlockSpec((N_CHIPS, tr, SLAB + LANE), lambda i: (0, i, 0))],
        out_specs=[_rowblk(tr, P_MAIN), _rowblk(tr, LANE)],
        out_shape=[SDS((rows, P_MAIN), BF16), SDS((rows, LANE), BF16)], name="merge_slabs",
        compiler_params=_params(("parallel",)))(g)


def _adamw_math(w, g, m, v):
    nm = ADAM_B1 * m + (1.0 - ADAM_B1) * g
    nv = ADAM_B2 * v + (1.0 - ADAM_B2) * (g * g)
    m_hat = nm / (1.0 - ADAM_B1 ** ADAM_STEP)
    v_hat = nv / (1.0 - ADAM_B2 ** ADAM_STEP)
    delta = -ADAM_LR * (m_hat / (jnp.sqrt(v_hat) + ADAM_EPS) + ADAM_WD * w)
    return delta, nm, nv


def _adamw(g, w, m, v, *, tr, name):
    rows, cols = w.shape
    tr = min(tr, rows)

    def body(g_ref, w_ref, m_ref, v_ref, d_ref, nm_ref, nv_ref):
        d, nm, nv = _adamw_math(w_ref[...], g_ref[...], m_ref[...], v_ref[...])
        d_ref[...] = d
        nm_ref[...] = nm
        nv_ref[...] = nv

    spec = _rowblk(tr, cols)
    return pl.pallas_call(
        body, grid=(rows // tr,), in_specs=[spec] * 4, out_specs=[spec] * 3,
        out_shape=[SDS((rows, cols), F32)] * 3, name=name, compiler_params=_params(("parallel",)))(g, w, m, v)


def _adamw_w_in(chip_core, slab_mine, slab_theirs, forget_mine, forget_theirs, w, m, v):
    rows = w.shape[0]
    tr = PACK_ROWS // 2
    nbh = rows // 2 // tr

    def body(k_ref, sa_ref, sb_ref, fa_ref, fb_ref, w_ref, m_ref, v_ref, g_ref, d_ref, nm_ref, nv_ref):
        use_mine = pl.program_id(0) // nbh == k_ref[1]
        sl = jnp.where(use_mine, sa_ref[...], sb_ref[...])
        f_tile = jnp.where(use_mine, fa_ref[...], fb_ref[...])
        k = k_ref[0]

        def emit(wide):
            g = wide[:, :SHARD_COLS]
            g_ref[...] = g
            d, nm, nv = _adamw_math(w_ref[...], g, m_ref[...], v_ref[...])
            d_ref[...] = d
            nm_ref[...] = nm
            nv_ref[...] = nv

        @pl.when(k == 0)
        def _():
            emit(sl)

        @pl.when(k == 1)
        def _():
            col = lax.broadcasted_iota(jnp.int32, (tr, SLAB), 1)
            before = pltpu.roll(sl, SLAB - SLAB_SHIFT[1], axis=1)
            after = pltpu.roll(sl, N_FORGET - SLAB_SHIFT[1], axis=1)
            wide_f = jnp.concatenate([f_tile, jnp.zeros((tr, SLAB - LANE), F32)], axis=1)
            forget = pltpu.roll(wide_f, FORGET_IN_SHARD, axis=1)
            emit(jnp.where(col < FORGET_IN_SHARD, before, jnp.where(col < FORGET_IN_SHARD + N_FORGET, forget, after)))

        for kk in (2, 3):
            @pl.when(k == kk)
            def _(kk=kk):
                emit(pltpu.roll(sl, SLAB - SLAB_SHIFT[kk], axis=1))

    nat = pl.BlockSpec((tr, SHARD_COLS), lambda i, k: (i, 0))
    half = lambda width: pl.BlockSpec((tr, width), lambda i, k: (i % nbh, 0))
    return pl.pallas_call(
        body, grid_spec=pltpu.PrefetchScalarGridSpec(
            num_scalar_prefetch=1, grid=(rows // tr,),
            in_specs=[half(SLAB), half(SLAB), half(LANE), half(LANE), nat, nat, nat],
            out_specs=[nat] * 4),
        out_shape=[SDS((rows, SHARD_COLS), F32)] * 4, name="adamw_w_in",
        compiler_params=_params(("arbitrary",)))(chip_core, slab_mine, slab_theirs, forget_mine, forget_theirs, w, m, v)


ANY = pl.BlockSpec(memory_space=pl.ANY)
HALF_AXIS = (0, 0, 1, 0, 0, 0, 1)


def _me():
    return lax.axis_index("x"), lax.axis_index("y"), lax.axis_index("c")


def _half(ref, which, axis):
    n = ref.shape[axis] // 2
    sl = pl.ds(which * n, n)
    return ref.at[sl] if axis == 0 else ref.at[:, sl]


def _piece(t, ref, j):
    if t == 0:
        return ref.at[:, pl.ds(SLAB_START[j], SLAB)]
    if t == 1:
        return ref
    if t in (2, 6):
        return ref.at[pl.ds(512 * j, 512)]
    return ref.at[:, pl.ds(512 * j, 512)]


def _piece_shape(t, shape):
    if t == 0:
        return (shape[0], SLAB)
    if t == 1:
        return shape
    if t in (2, 6):
        return (512, shape[1])
    return (shape[0], 512)


def _gather_plan(ins, outs, own_slot_in_src):
    x, y, c = _me()
    k = 2 * x + y
    sib = (x, y, 1 - c)
    chips = [(1 - x, y), (x, 1 - y), (1 - x, 1 - y)]
    n = len(outs)

    def rows(t, which):
        h = outs[t].shape[1] // 2
        return pl.ds(which * h, h)

    def mine(t):
        return ins[t].at[k, rows(t, c)] if own_slot_in_src else ins[t].at[rows(t, c)]

    def first(t, j, sems):
        chip = chips[j]
        return pltpu.make_async_remote_copy(
            src_ref=mine(t), dst_ref=outs[t].at[k, rows(t, c)], send_sem=sems[0].at[t, j], recv_sem=sems[1].at[t, j],
            device_id=(chip[0], chip[1], c), device_id_type=MESH)

    def landed(t, j, sems):
        chip = chips[j]
        return pltpu.make_async_remote_copy(
            src_ref=mine(t), dst_ref=outs[t].at[2 * chip[0] + chip[1], rows(t, c)], send_sem=sems[0].at[t, j],
            recv_sem=sems[1].at[t, j], device_id=(chip[0], chip[1], c), device_id_type=MESH)

    def passed(t, j, which, sems):
        chip = chips[j]
        blk = outs[t].at[2 * chip[0] + chip[1], rows(t, which)]
        return pltpu.make_async_remote_copy(
            src_ref=blk, dst_ref=blk, send_sem=sems[2].at[t, j], recv_sem=sems[3].at[t, j], device_id=sib,
            device_id_type=MESH)

    def start(sems):
        for j in range(3):
            for t in range(n):
                first(t, j, sems).start()

    def finish(sems):
        for j in range(3):
            for t in range(n):
                landed(t, j, sems).wait_recv()
                passed(t, j, c, sems).start()
        for j in range(3):
            for t in range(n):
                passed(t, j, 1 - c, sems).wait_recv()
        for j in range(3):
            for t in range(n):
                first(t, j, sems).wait_send()
                passed(t, j, c, sems).wait_send()

    return k, start, finish


def _all_gather_slabs(slabs):
    def body(in_ref, out_ref, *sems):
        _, start, finish = _gather_plan([in_ref], [out_ref], True)
        start(sems)
        finish(sems)

    return pl.pallas_call(
        body, in_specs=[ANY], out_specs=ANY, out_shape=SDS(slabs.shape, slabs.dtype),
        scratch_shapes=[pltpu.SemaphoreType.DMA((1, 3))] * 4, input_output_aliases={0: 0},
        name="all_gather_slabs")(slabs)


def _gather_comm(parts):
    n = len(parts)

    def start(ins, outs, sems):
        k, go, _ = _gather_plan(ins, outs, False)
        for t in range(n):
            pltpu.make_async_copy(ins[t], outs[t].at[k], sems[4].at[t]).start()
        go(sems)

    def finish(ins, outs, sems):
        k, _, done = _gather_plan(ins, outs, False)
        done(sems)
        for t in range(n):
            pltpu.make_async_copy(ins[t], outs[t].at[k], sems[4].at[t]).wait()

    return _Comm(parts, [SDS((N_CHIPS,) + p.shape, p.dtype) for p in parts],
                 [pltpu.SemaphoreType.DMA((n, 3))] * 4 + [pltpu.SemaphoreType.DMA((n,))], start, finish)


def _exchange_comm(arrs, kinds, whole=()):
    n = len(arrs)

    def copies(ins, outs, sems):
        x, y, c = _me()
        return [pltpu.make_async_remote_copy(
            src_ref=ins[t] if t in whole else _half(ins[t], 1 - c, HALF_AXIS[kinds[t]]), dst_ref=outs[t],
            send_sem=sems[0].at[t], recv_sem=sems[1].at[t], device_id=(x, y, 1 - c), device_id_type=MESH)
            for t in range(n)]

    def start(ins, outs, sems):
        for cp in copies(ins, outs, sems):
            cp.start()

    def finish(ins, outs, sems):
        for cp in copies(ins, outs, sems):
            cp.wait()

    def hshape(t):
        s = list(arrs[t].shape)
        if t not in whole:
            s[HALF_AXIS[kinds[t]]] //= 2
        return SDS(tuple(s), arrs[t].dtype)

    return _Comm(arrs, [hshape(t) for t in range(n)], [pltpu.SemaphoreType.DMA((n,))] * 2, start, finish)


def _add_half(full, got, core, axis, *, name):
    r, c = got.shape
    br, bc = (256 if r % 256 == 0 else 128), min(2048, c)
    off_r = (r // br) if axis == 0 else 0
    off_c = (c // bc) if axis == 1 else 0

    def body(c_ref, a_ref, b_ref, o_ref):
        o_ref[...] = (a_ref[...] + b_ref[...]).astype(BF16)

    return pl.pallas_call(
        body, grid_spec=pltpu.PrefetchScalarGridSpec(
            num_scalar_prefetch=1, grid=(r // br, c // bc),
            in_specs=[pl.BlockSpec((br, bc), lambda i, j, cr: (i + cr[0] * off_r, j + cr[0] * off_c)),
                      pl.BlockSpec((br, bc), lambda i, j, cr: (i, j))],
            out_specs=pl.BlockSpec((br, bc), lambda i, j, cr: (i, j))),
        out_shape=SDS((r, c), BF16), name=name, compiler_params=_params(("parallel", "parallel")))(core, full, got)


def _add_pair(a, b, *, name):
    r, c = a.shape
    br, bc = 256, min(2048, c)

    def body(a_ref, b_ref, o_ref):
        o_ref[...] = (a_ref[...] + b_ref[...]).astype(BF16)

    spec = pl.BlockSpec((br, bc), lambda i, j: (i, j))
    return pl.pallas_call(body, grid=(r // br, c // bc), in_specs=[spec, spec], out_specs=spec,
                          out_shape=SDS((r, c), BF16), name=name, compiler_params=_params(("parallel", "parallel")))(a, b)


def _scatter_comm(halves, kinds):
    n = len(halves)

    def plan(ins, outs, sems):
        send, recv, lsem = sems
        x, y, c = _me()
        k = 2 * x + y

        def to_chip(t, j):
            return pltpu.make_async_remote_copy(
                src_ref=_piece(kinds[t], ins[t], j), dst_ref=outs[t].at[k], send_sem=send.at[t, j],
                recv_sem=recv.at[t, k], device_id=(j // 2, j % 2, c), device_id_type=MESH)

        def from_chip(t, j):
            return pltpu.make_async_remote_copy(
                src_ref=_piece(kinds[t], ins[t], j), dst_ref=outs[t].at[j], send_sem=send.at[t, j],
                recv_sem=recv.at[t, j], device_id=(j // 2, j % 2, c), device_id_type=MESH)

        def own(t, j):
            return pltpu.make_async_copy(_piece(kinds[t], ins[t], j), outs[t].at[j], lsem.at[t])

        return k, to_chip, from_chip, own

    def start(ins, outs, sems):
        k, to_chip, _, own = plan(ins, outs, sems)
        for j in range(N_CHIPS):
            @pl.when(k != j)
            def _(j=j):
                for t in range(n):
                    to_chip(t, j).start()

            @pl.when(k == j)
            def _(j=j):
                for t in range(n):
                    own(t, j).start()

    def finish(ins, outs, sems):
        k, to_chip, from_chip, own = plan(ins, outs, sems)
        for j in range(N_CHIPS):
            @pl.when(k != j)
            def _(j=j):
                for t in range(n):
                    from_chip(t, j).wait_recv()
                for t in range(n):
                    to_chip(t, j).wait_send()

            @pl.when(k == j)
            def _(j=j):
                for t in range(n):
                    own(t, j).wait()

    return _Comm(halves, [SDS((N_CHIPS,) + _piece_shape(kinds[t], halves[t].shape), halves[t].dtype) for t in range(n)],
                 [pltpu.SemaphoreType.DMA((n, N_CHIPS))] * 2 + [pltpu.SemaphoreType.DMA((n,))], start, finish)


def _sum4(p, *, name):
    _, r, c = p.shape
    br = 256 if r % 256 == 0 else 128

    def body(p_ref, o_ref):
        o_ref[...] = ((p_ref[0].astype(F32) + p_ref[1].astype(F32)) + p_ref[2].astype(F32)) + p_ref[3].astype(F32)

    return pl.pallas_call(
        body, grid=(r // br,), in_specs=[pl.BlockSpec((N_CHIPS, br, c), lambda i: (0, i, 0))],
        out_specs=_rowblk(br, c), out_shape=SDS((r, c), F32), name=name, compiler_params=_params(("parallel",)))(p)


def _swap_comm(sums):
    return _exchange_comm(sums, [None] * len(sums), whole=tuple(range(len(sums))))


def _adamw_halves(mine, theirs, core, w, m, v, *, axis, tr, name):
    rows, cols = w.shape

    if axis == 0:
        nbh = rows // 2 // tr
        g_spec = pl.BlockSpec((tr, cols), lambda i, cr: (i % nbh, 0))
    else:
        g_spec = pl.BlockSpec((tr, cols // 2), lambda i, cr: (i, 0))

    def body(c_ref, a_ref, b_ref, w_ref, m_ref, v_ref, g_ref, d_ref, nm_ref, nv_ref):
        a, b = a_ref[...], b_ref[...]
        if axis == 0:
            g = jnp.where(pl.program_id(0) // nbh == c_ref[0], a, b)
        else:
            low = c_ref[0] == 0
            g = jnp.concatenate([jnp.where(low, a, b), jnp.where(low, b, a)], axis=1)
        g_ref[...] = g
        d, nm, nv = _adamw_math(w_ref[...], g, m_ref[...], v_ref[...])
        d_ref[...] = d
        nm_ref[...] = nm
        nv_ref[...] = nv

    nat = pl.BlockSpec((tr, cols), lambda i, cr: (i, 0))
    return pl.pallas_call(
        body, grid_spec=pltpu.PrefetchScalarGridSpec(
            num_scalar_prefetch=1, grid=(rows // tr,), in_specs=[g_spec, g_spec, nat, nat, nat], out_specs=[nat] * 4),
        out_shape=[SDS((rows, cols), F32)] * 4, name=name, compiler_params=_params(("arbitrary",)))(
            core, mine, theirs, w, m, v)


SMALL_ROWS, SMALL_COLS = 8, 1024


def _pack_small(vs):
    flat = jnp.concatenate([v.reshape(-1) for v in vs])
    return jnp.pad(flat, (0, SMALL_ROWS * SMALL_COLS - flat.shape[0])).reshape(SMALL_ROWS, SMALL_COLS)


def _unpack_small(packed, sizes):
    flat = packed.reshape(-1)
    out, o = [], 0
    for n in sizes:
        out.append(flat[o:o + n].reshape(1, n))
        o += n
    return out


def _all_reduce_small(v):
    n_dev = 8

    def body(v_ref, o_ref, land, send, recv):
        x, y, c = _me()
        me = 4 * x + 2 * y + c
        land[me] = v_ref[...]
        cps = []
        for r in range(1, n_dev):
            fx, fy, fc = (r >> 2) & 1, (r >> 1) & 1, r & 1
            peer = (x ^ fx, y ^ fy, c ^ fc)
            cps.append(pltpu.make_async_remote_copy(
                src_ref=v_ref, dst_ref=land.at[me], send_sem=send.at[r - 1], recv_sem=recv.at[r - 1],
                device_id=peer, device_id_type=MESH))
        for cp in cps:
            cp.start()
        for r in range(1, n_dev):
            fx, fy, fc = (r >> 2) & 1, (r >> 1) & 1, r & 1
            src = 4 * (x ^ fx) + 2 * (y ^ fy) + (c ^ fc)
            pltpu.make_async_remote_copy(
                src_ref=v_ref, dst_ref=land.at[src], send_sem=send.at[r - 1], recv_sem=recv.at[r - 1],
                device_id=(x ^ fx, y ^ fy, c ^ fc), device_id_type=MESH).wait_recv()
        for cp in cps:
            cp.wait_send()
        acc = land[0]
        for r in range(1, n_dev):
            acc = acc + land[r]
        o_ref[...] = acc

    vm = pl.BlockSpec(memory_space=pltpu.VMEM)
    return pl.pallas_call(
        body, in_specs=[vm], out_specs=vm, out_shape=SDS(v.shape, F32),
        scratch_shapes=[pltpu.VMEM((n_dev,) + v.shape, F32), pltpu.SemaphoreType.DMA((n_dev - 1,)),
                        pltpu.SemaphoreType.DMA((n_dev - 1,))],
        name="all_reduce_small")(v)


def kernel(x, mem, norm_gain, mem_norm_gain, w_in, b_forget, q_gain_a, k_gain_a, sinks_a, q_gain_b, k_gain_b, q_gain_c, k_gain_c, w_mem_kv, w_branch_a, w_branch_b, w_branch_c, w_out, loss_target, m_norm_gain, m_mem_norm_gain, m_w_in, m_b_forget, m_q_gain_a, m_k_gain_a, m_sinks_a, m_q_gain_b, m_k_gain_b, m_q_gain_c, m_k_gain_c, m_w_mem_kv, m_w_branch_a, m_w_branch_b, m_w_branch_c, m_w_out, v_norm_gain, v_mem_norm_gain, v_w_in, v_b_forget, v_q_gain_a, v_k_gain_a, v_sinks_a, v_q_gain_b, v_k_gain_b, v_q_gain_c, v_k_gain_c, v_w_mem_kv, v_w_branch_a, v_w_branch_b, v_w_branch_c, v_w_out):
    xi, yi, ci = lax.axis_index("x"), lax.axis_index("y"), lax.axis_index("c")
    chip = jnp.reshape(2 * xi + yi, (1,)).astype(jnp.int32)
    core = jnp.reshape(ci, (1,)).astype(jnp.int32)

    slabs = _pack_w_in(chip, w_in[0])
    mine = [w_mem_kv[0].astype(BF16), w_branch_a[0].astype(BF16), w_branch_b[0].astype(BF16),
            w_branch_c[0].astype(BF16), w_out[0].astype(BF16)]
    w_main, w_fb = _merge_slabs(_all_gather_slabs(slabs))

    r = _local_step(x[0], mem[0], loss_target[0], w_main, w_fb, mine, norm_gain, mem_norm_gain,
                    b_forget, q_gain_a, k_gain_a, sinks_a, q_gain_b, k_gain_b, q_gain_c, k_gain_c, core=core)
    sums, theirs = r["sums"], r["theirs"]

    small_names = ["d_gain", "d_mem_gain", "d_bf", "d_qga", "d_kga", "d_sinks", "d_qgb", "d_kgb", "d_qgc", "d_kgc"]
    loss_part = (0.5 / D_MODEL) * jnp.sum(r["sq"], axis=1, keepdims=True)
    packed = _pack_small([r[n] for n in small_names] + [loss_part])
    red = _all_reduce_small(packed)
    small_w = [norm_gain, mem_norm_gain, b_forget, q_gain_a, k_gain_a, sinks_a, q_gain_b, k_gain_b, q_gain_c, k_gain_c]
    small_m = [m_norm_gain, m_mem_norm_gain, m_b_forget, m_q_gain_a, m_k_gain_a, m_sinks_a, m_q_gain_b, m_k_gain_b,
               m_q_gain_c, m_k_gain_c]
    small_v = [v_norm_gain, v_mem_norm_gain, v_b_forget, v_q_gain_a, v_k_gain_a, v_sinks_a, v_q_gain_b, v_k_gain_b,
               v_q_gain_c, v_k_gain_c]
    sizes = [w.shape[1] for w in small_w]
    s_d, s_m, s_v = _adamw(red, _pack_small(small_w), _pack_small(small_m), _pack_small(small_v), tr=8, name="adamw_small")
    g_small = _unpack_small(red, sizes + [1])
    loss = g_small[-1].reshape(())
    d_small, m_small, v_small = _unpack_small(s_d, sizes), _unpack_small(s_m, sizes), _unpack_small(s_v, sizes)

    gw_in, dw_in, mw_in, vw_in = _adamw_w_in(jnp.concatenate([chip, core]), sums[0], theirs[0], sums[1], theirs[1],
                                             w_in[0], m_w_in[0], v_w_in[0])
    big = {}
    for t, nm, w, m, v in ((2, "w_mem_kv", w_mem_kv, m_w_mem_kv, v_w_mem_kv),
                           (3, "w_branch_a", w_branch_a, m_w_branch_a, v_w_branch_a),
                           (4, "w_branch_b", w_branch_b, m_w_branch_b, v_w_branch_b),
                           (5, "w_branch_c", w_branch_c, m_w_branch_c, v_w_branch_c),
                           (6, "w_out", w_out, m_w_out, v_w_out)):
        big[nm] = _adamw_halves(sums[t], theirs[t], core, w[0], m[0], v[0], axis=HALF_AXIS[t], tr=128,
                                name="adamw_" + nm)

    def collect(kind):
        sm = (g_small, d_small, m_small, v_small)[kind]
        win = (gw_in, dw_in, mw_in, vw_in)[kind]
        return ([sm[0], sm[1], win[None]] + [a for a in sm[2:10]]
                + [big[n][kind][None] for n in ("w_mem_kv", "w_branch_a", "w_branch_b", "w_branch_c", "w_out")])

    return (loss, r["grad_x"][None], *collect(0), *collect(1), *collect(2), *collect(3))
```

```python
import functools

import numpy as np
import jax
import jax.numpy as jnp
from jax import lax
from jax.experimental import pallas as pl
from jax.experimental.pallas import tpu as pltpu

F32 = jnp.float32
BF16 = jnp.bfloat16
HI = lax.Precision.HIGHEST
SDS = jax.ShapeDtypeStruct
MESH = pl.DeviceIdType.MESH

D_MODEL = 2048
HEAD_DIM = 64
A_HEADS = 12
A_GROUP = 3
B_HEADS = 12
C_HEADS = 4
C_HEAD_DIM = 128
WINDOW = 128
EPS = 1e-6
NEG = -1e30
LANE = 128

QA, KA, VA, ZA = 0, 768, 1024, 1280
QB, KB, VB, ZB = 2048, 2816, 3584, 4352
QC, ZC = 5120, 5632
GATE = 6144
P_MAIN = 12288
N_FORGET = 12
FORGET_COL = 5120
SHARD_COLS = 3075
SLAB = 3200
SLAB_START = (0, 3072, 6016, 9088)
SLAB_SHIFT = (0, 3, 122, 125)
N_CHIPS = 4

ADAM_LR = 0.001
ADAM_B1 = 0.9
ADAM_B2 = 0.999
ADAM_EPS = 1e-08
ADAM_WD = 0.01
ADAM_STEP = 10

VMEM_LIMIT = 56 * 1024 * 1024
VMEM_WIDE = 62 * 1024 * 1024


def _params(sem, vmem=VMEM_LIMIT):
    return pltpu.CompilerParams(dimension_semantics=sem, vmem_limit_bytes=vmem)


def _win(tr, width, off):
    return pl.BlockSpec((pl.Element(tr), pl.Element(width)), lambda i, *_: (i * tr, off))


def _rowblk(tr, width):
    return pl.BlockSpec((tr, width), lambda i, *_: (i, 0))


def _const(shape):
    nd = len(shape)
    return pl.BlockSpec(shape, lambda *_: (0,) * nd)


def _rms(x, g):
    return x * lax.rsqrt(jnp.mean(x * x, axis=-1, keepdims=True) + EPS) * g


def _head_mean_impl(x2, bd):
    hi = x2.astype(BF16)
    lo = (x2 - hi.astype(F32)).astype(BF16)
    return _dot(hi, bd) + _dot(lo, bd)


@jax.custom_vjp
def _head_mean(x2, bd):
    return _head_mean_impl(x2, bd)


_head_mean.defvjp(lambda x2, bd: (_head_mean_impl(x2, bd), bd),
                  lambda bd, g: (_head_mean_impl(g, bd), jnp.zeros_like(bd)))


def _head_norm(x, g_tiled, bd):
    return x * lax.rsqrt(_head_mean(x * x, bd) + EPS) * g_tiled


def _silu(z):
    return z * jax.nn.sigmoid(z)


def _dot_nt(a, b):
    return lax.dot_general(a, b, (((1,), (1,)), ((), ())), preferred_element_type=F32)


def _dot_tn(a, b):
    return lax.dot_general(a, b, (((0,), (0,)), ((), ())), preferred_element_type=F32)


def _dot(a, b):
    return jnp.dot(a, b, preferred_element_type=F32)


def _swa_fn(qk, vz, qkp, vzp, qg, kg, sinks, bd, bias, first):
    q = _head_norm(qk[:, :768], qg, bd)
    k2 = jnp.concatenate([qkp[:, 768:], qk[:, 768:]], axis=0)
    k2 = _head_norm(k2, kg, bd[:256, :256])
    v2 = jnp.concatenate([vzp[:, :256], vz[:, :256]], axis=0)
    z = vz[:, 256:]
    cols = A_GROUP * WINDOW
    kj = lax.broadcasted_iota(jnp.int32, (2 * WINDOW, cols), 0)
    no_prev = kj < WINDOW * first.astype(jnp.int32)
    qtb = jnp.transpose(q).astype(BF16)
    kb = k2.astype(BF16)
    vtb = jnp.transpose(v2).astype(BF16)
    outs = [None] * A_HEADS
    for g in range(A_HEADS // A_GROUP):
        heads = [A_GROUP * g + u for u in range(A_GROUP)]
        qs = jnp.concatenate([qtb[64 * h:64 * h + 64, :] for h in heads], axis=1)
        s = _dot(kb[:, 64 * g:64 * g + 64], qs) * (HEAD_DIM ** -0.5) + bias[g]
        s = jnp.where(no_prev, NEG, s)
        sink = jnp.concatenate([jnp.broadcast_to(sinks[:, h:h + 1], (1, WINDOW)) for h in heads], axis=1)
        m = lax.stop_gradient(jnp.maximum(jnp.max(s, axis=0, keepdims=True), sink))
        p = jnp.exp(s - m)
        den = jnp.sum(p, axis=0, keepdims=True) + jnp.exp(sink - m)
        o = _dot(vtb[64 * g:64 * g + 64, :], (p * (1.0 / den)).astype(BF16))
        for u, h in enumerate(heads):
            outs[h] = o[:, WINDOW * u:WINDOW * u + WINDOW]
    return jnp.transpose(jnp.concatenate(outs, axis=0)) * _silu(z)


def _swa_bias():
    qi = np.arange(WINDOW)[None, :]
    kj = np.arange(2 * WINDOW)[:, None]
    rel = qi + WINDOW - kj
    valid = (rel >= 0) & (rel < WINDOW)
    out = np.zeros((A_HEADS // A_GROUP, 2 * WINDOW, A_GROUP * WINDOW), np.float32)
    for h in range(A_HEADS):
        slope = np.float32(2.0 ** (-8.0 * (h + 1) / A_HEADS))
        blk = np.where(valid, -slope * rel.astype(np.float32), np.float32(NEG))
        g, u = divmod(h, A_GROUP)
        out[g, :, WINDOW * u:WINDOW * u + WINDOW] = blk
    return jnp.asarray(out)


def _mem_fn(qz, mkv, qg, kg, bd):
    q = _head_norm(qz[:, :512], qg, bd).astype(BF16)
    k = _head_norm(mkv[:, :512], kg, bd).astype(BF16)
    v = mkv[:, 512:].astype(BF16)
    z = qz[:, 512:]
    outs = []
    for h in range(C_HEADS):
        sl = slice(128 * h, 128 * h + 128)
        s = _dot_nt(q[:, sl], k[:, sl]) * (C_HEAD_DIM ** -0.5)
        m = lax.stop_gradient(jnp.max(s, axis=-1, keepdims=True))
        p = jnp.exp(s - m)
        den = jnp.sum(p, axis=-1, keepdims=True)
        outs.append(_dot((p * (1.0 / den)).astype(BF16), v[:, sl]))
    return jnp.concatenate(outs, axis=1) * _silu(z)


def _qn_fn(q, g, bd):
    return _head_norm(q, g, bd) * (HEAD_DIM ** -0.5)


def _kn_fn(k, g, bd):
    return _head_norm(k, g, bd)


def _block_diag(width, hd):
    i = np.arange(width) // hd
    return jnp.asarray((i[:, None] == i[None, :]).astype(np.float32) / hd, BF16)


def _head_sum(width, hd):
    i = np.arange(width) // hd
    return jnp.asarray((i[:, None] == np.arange(LANE)[None, :]).astype(np.float32))


def _rms_fwd(x, g, *, tr, name):
    rows, dm = x.shape

    def body(x_ref, g_ref, o_ref):
        o_ref[...] = _rms(x_ref[...], g_ref[...]).astype(BF16)

    return pl.pallas_call(
        body, grid=(rows // tr,),
        in_specs=[_rowblk(tr, dm), _const((1, dm))],
        out_specs=_rowblk(tr, dm),
        out_shape=SDS((rows, dm), BF16), name=name,
        compiler_params=_params(("parallel",)))(x, g)


def _rms_bwd(x, g, dy, resid, *, tr, name):
    rows, dm = x.shape
    want_dx = resid is not None

    def body(*refs):
        if want_dx:
            x_ref, g_ref, dy_ref, r_ref, dx_ref, dg_ref = refs
        else:
            x_ref, g_ref, dy_ref, dg_ref = refs
        _, vjp = jax.vjp(_rms, x_ref[...], g_ref[...])
        dx, dg = vjp(dy_ref[...])

        @pl.when(pl.program_id(0) == 0)
        def _():
            dg_ref[...] = jnp.zeros_like(dg_ref)

        dg_ref[...] += dg
        if want_dx:
            dx_ref[...] = r_ref[...] + dx

    ins = [x, g, dy] + ([resid] if want_dx else [])
    in_specs = [_rowblk(tr, dm), _const((1, dm)), _rowblk(tr, dm)] + ([_rowblk(tr, dm)] if want_dx else [])
    out_specs = ([_rowblk(tr, dm)] if want_dx else []) + [_const((1, dm))]
    out_shape = ([SDS((rows, dm), F32)] if want_dx else []) + [SDS((1, dm), F32)]
    return pl.pallas_call(
        body, grid=(rows // tr,), in_specs=in_specs, out_specs=out_specs, out_shape=out_shape, name=name,
        compiler_params=_params(("arbitrary",)))(*ins)


class _Comm:
    def __init__(self, ins, out_shapes, sems, start, finish):
        self.ins, self.out_shapes, self.sems, self.start, self.finish = list(ins), list(out_shapes), list(sems), start, finish


def _run_comm(comm, name):
    n_in, n_out = len(comm.ins), len(comm.out_shapes)

    def body(*refs):
        ins, outs, sems = refs[:n_in], refs[n_in:n_in + n_out], refs[n_in + n_out:]
        comm.start(ins, outs, sems)
        comm.finish(ins, outs, sems)

    hbm = pl.BlockSpec(memory_space=pl.ANY)
    return pl.pallas_call(body, in_specs=[hbm] * n_in, out_specs=[hbm] * n_out, out_shape=comm.out_shapes,
                          scratch_shapes=comm.sems, name=name)(*comm.ins)


def _matmul(a, b, *, dims, out_dtype, tm, tn, tk, name, add=None, comms=(), vmem=VMEM_LIMIT):
    a_list = list(a) if isinstance(a, (list, tuple)) else [a]
    b_list = list(b) if isinstance(b, (list, tuple)) else [b]
    assert len(a_list) == 1 or dims == "nt"
    assert len(b_list) == 1 or dims == "tn"
    if dims == "tn":
        kdim, m = a_list[0].shape
    else:
        m, kdim = a_list[0].shape[0], sum(p.shape[1] for p in a_list)
    n = b_list[0].shape[0] if dims == "nt" else sum(p.shape[1] for p in b_list)
    tm, tn, tk = min(tm, m), min(tn, n), min(tk, kdim)
    assert m % tm == 0 and n % tn == 0 and kdim % tk == 0, (name, m, n, kdim)
    ni, nj, nk = m // tm, n // tn, kdim // tk
    a_rng, b_rng, pos = [], [], 0
    for p in a_list:
        assert len(a_list) == 1 or p.shape[1] % tk == 0
        a_rng.append((pos, p.shape[1] // tk if len(a_list) > 1 else nk))
        pos += a_rng[-1][1]
    pos = 0
    for p in b_list:
        assert len(b_list) == 1 or p.shape[1] % tn == 0
        b_rng.append((pos, p.shape[1] // tn if len(b_list) > 1 else nj))
        pos += b_rng[-1][1]
    has_add = add is not None
    n_mm_in = len(a_list) + len(b_list) + (1 if has_add else 0)
    c_in = [len(c.ins) for c in comms]
    c_out = [len(c.out_shapes) for c in comms]
    c_sem = [len(c.sems) for c in comms]

    def body(*refs):
        a_refs, b_refs = refs[:len(a_list)], refs[len(a_list):len(a_list) + len(b_list)]
        add_ref = refs[n_mm_in - 1] if has_add else None
        pos = n_mm_in
        cin = []
        for cnt in c_in:
            cin.append(refs[pos:pos + cnt])
            pos += cnt
        o_ref = refs[pos]
        pos += 1
        cout = []
        for cnt in c_out:
            cout.append(refs[pos:pos + cnt])
            pos += cnt
        acc = refs[pos]
        pos += 1
        csem = []
        for cnt in c_sem:
            csem.append(refs[pos:pos + cnt])
            pos += cnt
        i, j, k = pl.program_id(0), pl.program_id(1), pl.program_id(2)

        if comms:
            @pl.when((i == 0) & (j == 0) & (k == 0))
            def _():
                for c, ci, co, cs in zip(comms, cin, cout, csem):
                    c.start(ci, co, cs)

        def accumulate(a_ref, b_ref, first_k, later_k):
            if dims == "nn":
                part = _dot(a_ref[...], b_ref[...])
            elif dims == "nt":
                part = _dot_nt(a_ref[...], b_ref[...])
            else:
                part = _dot_tn(a_ref[...], b_ref[...])

            if first_k:
                @pl.when(k == 0)
                def _():
                    acc[...] = part + add_ref[...] if has_add else part

            if later_k:
                @pl.when(k > 0)
                def _():
                    acc[...] += part

        if len(a_list) > 1:
            for a_ref, (k0, cnt) in zip(a_refs, a_rng):
                @pl.when((k >= k0) & (k < k0 + cnt))
                def _(a_ref=a_ref, k0=k0, cnt=cnt):
                    accumulate(a_ref, b_refs[0], k0 == 0, k0 + cnt > 1)
        elif len(b_list) > 1:
            for b_ref, (j0, cnt) in zip(b_refs, b_rng):
                @pl.when((j >= j0) & (j < j0 + cnt))
                def _(b_ref=b_ref):
                    accumulate(a_refs[0], b_ref, True, nk > 1)
        else:
            accumulate(a_refs[0], b_refs[0], True, nk > 1)

        @pl.when(k == nk - 1)
        def _():
            o_ref[...] = acc[...].astype(out_dtype)

        if comms:
            @pl.when((i == ni - 1) & (j == nj - 1) & (k == nk - 1))
            def _():
                for c, ci, co, cs in zip(comms, cin, cout, csem):
                    c.finish(ci, co, cs)

    def a_spec(k0, cnt):
        if dims == "tn":
            return pl.BlockSpec((tk, tm), lambda i, j, k: (k, i))
        return pl.BlockSpec((tm, tk), lambda i, j, k: (i, jnp.clip(k - k0, 0, cnt - 1)))

    def b_spec(j0, cnt):
        if dims == "nt":
            return pl.BlockSpec((tn, tk), lambda i, j, k: (j, k))
        return pl.BlockSpec((tk, tn), lambda i, j, k: (k, jnp.clip(j - j0, 0, cnt - 1)))

    o_spec = pl.BlockSpec((tm, tn), lambda i, j, k: (i, j))
    hbm = pl.BlockSpec(memory_space=pl.ANY)
    ins = a_list + b_list + ([add] if has_add else []) + [x for c in comms for x in c.ins]
    in_specs = ([a_spec(*r) for r in a_rng] + [b_spec(*r) for r in b_rng] + ([o_spec] if has_add else [])
                + [hbm] * sum(c_in))
    out_specs = [o_spec] + [hbm] * sum(c_out)
    out_shape = [SDS((m, n), out_dtype)] + [s for c in comms for s in c.out_shapes]
    scratch = [pltpu.VMEM((tm, tn), F32)] + [s for c in comms for s in c.sems]
    sem = ("arbitrary",) * 3 if comms else ("parallel", "parallel", "arbitrary")
    res = pl.pallas_call(
        body, grid=(ni, nj, nk), in_specs=in_specs, out_specs=out_specs, out_shape=out_shape, scratch_shapes=scratch,
        name=name, compiler_params=_params(sem, vmem))(*ins)
    if not comms:
        return res[0]
    outs, pos = [], 1
    for cnt in c_out:
        outs.append(list(res[pos:pos + cnt]))
        pos += cnt
    return res[0], outs


def _swa_specs(nb):
    prev = lambda off: pl.BlockSpec((pl.Element(WINDOW), pl.Element(1024)),
                                    lambda n: (jnp.maximum(n - 1, 0) * WINDOW, off))
    return [_win(WINDOW, 1024, QA), _win(WINDOW, 1024, VA), prev(QA), prev(VA),
            _const((1, 768)), _const((1, 256)), _const((1, LANE)), _const((768, 768)),
            _const((A_HEADS // A_GROUP, 2 * WINDOW, A_GROUP * WINDOW))]


def _swa_fwd(proj, qg, kg, sinks, bd, bias):
    s = proj.shape[0]
    nb = s // WINDOW

    def body(qk_ref, vz_ref, qkp_ref, vzp_ref, qg_ref, kg_ref, sk_ref, bd_ref, bias_ref, o_ref):
        first = pl.program_id(0) == 0
        o_ref[...] = _swa_fn(qk_ref[...], vz_ref[...], qkp_ref[...], vzp_ref[...], qg_ref[...], kg_ref[...],
                             sk_ref[...], bd_ref[...], bias_ref[...], first).astype(BF16)

    return pl.pallas_call(
        body, grid=(nb,), in_specs=_swa_specs(nb), out_specs=_rowblk(WINDOW, 768),
        out_shape=SDS((s, 768), BF16), name="swa_fwd",
        compiler_params=_params(("parallel",)))(proj, proj, proj, proj, qg, kg, sinks, bd, bias)


def _swa_bwd(proj, qg, kg, sinks, bd, bias, dga):
    s = proj.shape[0]
    nb = s // WINDOW

    def body(qk_ref, vz_ref, qkp_ref, vzp_ref, qg_ref, kg_ref, sk_ref, bd_ref, bias_ref, dg_ref,
             dcur_ref, dprev_ref, dqg_ref, dkg_ref, dsk_ref):
        first = pl.program_id(0) == 0
        bd_v = bd_ref[...]
        bias_v = bias_ref[...]
        fn = lambda qk, vz, qkp, vzp, qg_, kg_, sk: _swa_fn(qk, vz, qkp, vzp, qg_, kg_, sk, bd_v, bias_v, first)
        _, vjp = jax.vjp(fn, qk_ref[...], vz_ref[...], qkp_ref[...], vzp_ref[...], qg_ref[...], kg_ref[...], sk_ref[...])
        dqk, dvz, dqkp, dvzp, dqg, dkg, dsk = vjp(dg_ref[...])

        @pl.when(first)
        def _():
            dqg_ref[...] = jnp.zeros_like(dqg_ref)
            dkg_ref[...] = jnp.zeros_like(dkg_ref)
            dsk_ref[...] = jnp.zeros_like(dsk_ref)

        dqg_ref[...] += dqg
        dkg_ref[...] += dkg
        dsk_ref[...] += dsk
        dcur_ref[...] = jnp.concatenate([dqk, dvz], axis=1)
        dprev_ref[...] = jnp.concatenate([dqkp[:, 768:], dvzp[:, :256]], axis=1)

    return pl.pallas_call(
        body, grid=(nb,), in_specs=_swa_specs(nb) + [_rowblk(WINDOW, 768)],
        out_specs=[_rowblk(WINDOW, 2048), pl.BlockSpec((None, WINDOW, 512), lambda n: (n, 0, 0)),
                   _const((1, 768)), _const((1, 256)), _const((1, LANE))],
        out_shape=[SDS((s, 2048), F32), SDS((nb, WINDOW, 512), F32), SDS((1, 768), F32), SDS((1, 256), F32),
                   SDS((1, LANE), F32)],
        name="swa_bwd", compiler_params=_params(("arbitrary",)))(proj, proj, proj, proj, qg, kg, sinks, bd, bias, dga)


def _swa_combine(dcur, dprev):
    s = dcur.shape[0]
    nb = s // WINDOW

    def body(c_ref, p_ref, o_ref):
        c = c_ref[...]
        nxt = jnp.where(pl.program_id(0) == nb - 1, 0.0, p_ref[...])
        o_ref[...] = jnp.concatenate([c[:, :768], c[:, 768:1280] + nxt, c[:, 1280:]], axis=1).astype(BF16)

    return pl.pallas_call(
        body, grid=(nb,),
        in_specs=[_rowblk(WINDOW, 2048), pl.BlockSpec((None, WINDOW, 512), lambda n: (jnp.minimum(n + 1, nb - 1), 0, 0))],
        out_specs=_rowblk(WINDOW, 2048), out_shape=SDS((s, 2048), BF16), name="swa_combine",
        compiler_params=_params(("parallel",)))(dcur, dprev)


def _mem_fwd(proj, mkv, qg, kg, bd, *, tr):
    s = proj.shape[0]

    def body(qz_ref, mkv_ref, qg_ref, kg_ref, bd_ref, o_ref):
        o_ref[...] = _mem_fn(qz_ref[...], mkv_ref[...], qg_ref[...], kg_ref[...], bd_ref[...]).astype(BF16)

    return pl.pallas_call(
        body, grid=(s // tr,),
        in_specs=[_win(tr, 1024, QC), _const(mkv.shape), _const((1, 512)), _const((1, 512)), _const((512, 512))],
        out_specs=_rowblk(tr, 512), out_shape=SDS((s, 512), BF16), name="mem_fwd",
        compiler_params=_params(("parallel",)))(proj, mkv, qg, kg, bd)


def _mem_bwd(proj, mkv, qg, kg, bd, dgc, *, tr):
    s = proj.shape[0]

    def body(qz_ref, mkv_ref, qg_ref, kg_ref, bd_ref, dg_ref, dqz_ref, dmkv_ref, dqg_ref, dkg_ref):
        bd_v = bd_ref[...]
        fn = lambda qz, mkv_, qg_, kg_: _mem_fn(qz, mkv_, qg_, kg_, bd_v)
        _, vjp = jax.vjp(fn, qz_ref[...], mkv_ref[...], qg_ref[...], kg_ref[...])
        dqz, dmkv, dqg, dkg = vjp(dg_ref[...])

        @pl.when(pl.program_id(0) == 0)
        def _():
            dmkv_ref[...] = jnp.zeros_like(dmkv_ref)
            dqg_ref[...] = jnp.zeros_like(dqg_ref)
            dkg_ref[...] = jnp.zeros_like(dkg_ref)

        dmkv_ref[...] += dmkv
        dqg_ref[...] += dqg
        dkg_ref[...] += dkg
        dqz_ref[...] = dqz.astype(BF16)

    return pl.pallas_call(
        body, grid=(s // tr,),
        in_specs=[_win(tr, 1024, QC), _const(mkv.shape), _const((1, 512)), _const((1, 512)), _const((512, 512)),
                  _rowblk(tr, 512)],
        out_specs=[_rowblk(tr, 1024), _const(mkv.shape), _const((1, 512)), _const((1, 512))],
        out_shape=[SDS((s, 1024), BF16), SDS(mkv.shape, F32), SDS((1, 512), F32), SDS((1, 512), F32)],
        name="mem_bwd", compiler_params=_params(("arbitrary",)))(proj, mkv, qg, kg, bd, dgc)


def _log_sigmoid(x):
    return jnp.minimum(x, 0.0) - jnp.log1p(jnp.exp(-jnp.abs(x)))


def _fox_prep(proj, fbl, qg, kg, bfor, bd, *, tr):
    s = proj.shape[0]
    tri = jnp.asarray(np.tril(np.ones((tr, tr), np.float32)))

    def body(q_ref, k_ref, fb_ref, qg_ref, kg_ref, bf_ref, bd_ref, tri_ref, qn_ref, kn_ref, cq_ref, ck_ref, carry):
        @pl.when(pl.program_id(0) == 0)
        def _():
            carry[...] = jnp.zeros_like(carry)

        bd_v = bd_ref[...]
        qn_ref[...] = _qn_fn(q_ref[...], qg_ref[...], bd_v).astype(BF16)
        kn_ref[...] = _kn_fn(k_ref[...], kg_ref[...], bd_v).astype(BF16)
        lane = lax.broadcasted_iota(jnp.int32, (tr, LANE), 1)
        logf = jnp.where(lane < N_FORGET, _log_sigmoid(fb_ref[...] + bf_ref[...]), 0.0)
        c = jnp.dot(tri_ref[...], logf, precision=HI, preferred_element_type=F32) + carry[...]
        cq_ref[...] = c
        ck_ref[...] = jnp.transpose(c)
        carry[...] = c[tr - 1:tr, :]

    return pl.pallas_call(
        body, grid=(s // tr,),
        in_specs=[_win(tr, 768, QB), _win(tr, 768, KB), _rowblk(tr, LANE), _const((1, 768)), _const((1, 768)),
                  _const((1, LANE)), _const((768, 768)), _const((tr, tr))],
        out_specs=[_rowblk(tr, 768), _rowblk(tr, 768), _rowblk(tr, LANE), pl.BlockSpec((LANE, tr), lambda i: (0, i))],
        out_shape=[SDS((s, 768), BF16), SDS((s, 768), BF16), SDS((s, LANE), F32), SDS((LANE, s), F32)],
        scratch_shapes=[pltpu.VMEM((1, LANE), F32)], name="fox_prep",
        compiler_params=_params(("arbitrary",)))(proj, proj, fbl, qg, kg, bfor, bd, tri)


FOX_TQ, FOX_TK = 512, 512
FOX_FWD_TQ, FOX_FWD_TK = 512, 1024


def _fox_tiles(s):
    return min(FOX_TQ, s), min(FOX_TK, s)


def _fox_fwd(proj, qn, kn, cq, ck):
    s = proj.shape[0]
    tq, tk = _fox_tiles(s)
    nq, nk = s // tq, s // tk

    def last_k(i):
        return (i * tq + tq - 1) // tk

    def body(q_ref, k_ref, v_ref, cq_ref, ck_ref, z_ref, gb_ref, yb_ref, lse_ref, acc, m_s, l_s):
        i, j = pl.program_id(0), pl.program_id(1)

        @pl.when(j == 0)
        def _():
            acc[...] = jnp.zeros_like(acc)
            m_s[...] = jnp.full_like(m_s, NEG)
            l_s[...] = jnp.ones_like(l_s)

        @pl.when(j <= last_k(i))
        def _():
            qpos = i * tq + lax.broadcasted_iota(jnp.int32, (tq, tk), 0)
            kpos = j * tk + lax.broadcasted_iota(jnp.int32, (tq, tk), 1)
            mask = kpos <= qpos
            q, k = q_ref[...], k_ref[...]
            v = v_ref[...].astype(BF16)
            cqv, ckv = cq_ref[...], ck_ref[...]
            m_all, l_all = m_s[...], l_s[...]
            lane = lax.broadcasted_iota(jnp.int32, (tq, LANE), 1)
            m_out, l_out = m_all, l_all
            for hp in range(B_HEADS // 2):
                acc_pair = acc[:, 128 * hp:128 * hp + 128]
                new = []
                for u in range(2):
                    h = 2 * hp + u
                    sl = slice(64 * h, 64 * h + 64)
                    sc = _dot_nt(q[:, sl], k[:, sl]) + cqv[:, h:h + 1] - ckv[h:h + 1, :]
                    sc = jnp.where(mask, sc, NEG)
                    m_prev = m_all[:, h:h + 1]
                    m_new = jnp.maximum(m_prev, jnp.max(sc, axis=-1, keepdims=True))
                    alpha = jnp.exp(m_prev - m_new)
                    p = jnp.exp(sc - m_new)
                    l_new = alpha * l_all[:, h:h + 1] + jnp.sum(p, axis=-1, keepdims=True)
                    new.append(alpha * acc_pair[:, 64 * u:64 * u + 64] + _dot(p.astype(BF16), v[:, sl]))
                    m_out = jnp.where(lane == h, m_new, m_out)
                    l_out = jnp.where(lane == h, l_new, l_out)
                acc[:, 128 * hp:128 * hp + 128] = jnp.concatenate(new, axis=1)
            m_s[...] = m_out
            l_s[...] = l_out

        @pl.when(j == nk - 1)
        def _():
            l_all = l_s[...]
            inv = 1.0 / l_all
            a = acc[...]
            y = jnp.concatenate([a[:, 64 * h:64 * h + 64] * inv[:, h:h + 1] for h in range(B_HEADS)], axis=1)
            yb_ref[...] = y
            gb_ref[...] = (y * _silu(z_ref[...])).astype(BF16)
            lse_ref[...] = m_s[...] + jnp.log(l_all)

    kmap = lambda i, j: (jnp.minimum(j, last_k(i)), 0)
    return pl.pallas_call(
        body, grid=(nq, nk),
        in_specs=[pl.BlockSpec((tq, 768), lambda i, j: (i, 0)),
                  pl.BlockSpec((tk, 768), kmap),
                  pl.BlockSpec((pl.Element(tk), pl.Element(768)), lambda i, j: (jnp.minimum(j, last_k(i)) * tk, VB)),
                  pl.BlockSpec((tq, LANE), lambda i, j: (i, 0)),
                  pl.BlockSpec((16, tk), lambda i, j: (0, jnp.minimum(j, last_k(i)))),
                  pl.BlockSpec((pl.Element(tq), pl.Element(768)), lambda i, j: (i * tq, ZB))],
        out_specs=[pl.BlockSpec((tq, 768), lambda i, j: (i, 0)), pl.BlockSpec((tq, 768), lambda i, j: (i, 0)),
                   pl.BlockSpec((tq, LANE), lambda i, j: (i, 0))],
        out_shape=[SDS((s, 768), BF16), SDS((s, 768), F32), SDS((s, LANE), F32)],
        scratch_shapes=[pltpu.VMEM((tq, 768), F32), pltpu.VMEM((tq, LANE), F32), pltpu.VMEM((tq, LANE), F32)],
        name="fox_fwd", compiler_params=_params(("parallel", "arbitrary")))(qn, kn, proj, cq, ck, proj)


def _fox_bwd_pre(proj, yb, dgb, hsum, *, tr):
    s = proj.shape[0]

    def body(z_ref, y_ref, dg_ref, hs_ref, dy_ref, dz_ref, dl_ref):
        z, y, dg = z_ref[...], y_ref[...], dg_ref[...]
        sg = jax.nn.sigmoid(z)
        dy = dg * (z * sg)
        dy_ref[...] = dy.astype(BF16)
        dz_ref[...] = (dg * y * (sg * (1.0 + z * (1.0 - sg)))).astype(BF16)
        dl_ref[...] = jnp.dot(dy * y, hs_ref[...], precision=HI, preferred_element_type=F32)

    return pl.pallas_call(
        body, grid=(s // tr,),
        in_specs=[_win(tr, 768, ZB), _rowblk(tr, 768), _rowblk(tr, 768), _const((768, LANE))],
        out_specs=[_rowblk(tr, 768), _rowblk(tr, 768), _rowblk(tr, LANE)],
        out_shape=[SDS((s, 768), BF16), SDS((s, 768), BF16), SDS((s, LANE), F32)], name="fox_bwd_pre",
        compiler_params=_params(("parallel",)))(proj, yb, dgb, hsum)


def _fox_bwd(proj, qn, kn, cq, ck, lse, delta, dyb):
    s = proj.shape[0]
    tq, tk = _fox_tiles(s)
    nq, nk = s // tq, s // tk

    def first_q(j):
        return (j * tk) // tq

    def body(q_ref, k_ref, v_ref, cq_ref, ck_ref, lse_ref, dl_ref, dy_ref,
             dq_ref, dcq_ref, dk_ref, dv_ref, dck_ref, dk_acc, dv_acc, dck_acc):
        j, i = pl.program_id(0), pl.program_id(1)

        @pl.when((j == 0) & (i == 0))
        def _():
            dq_ref[...] = jnp.zeros_like(dq_ref)
            dcq_ref[...] = jnp.zeros_like(dcq_ref)

        @pl.when(i == 0)
        def _():
            dk_acc[...] = jnp.zeros_like(dk_acc)
            dv_acc[...] = jnp.zeros_like(dv_acc)
            dck_acc[...] = jnp.zeros_like(dck_acc)

        @pl.when(i >= first_q(j))
        def _():
            qpos = i * tq + lax.broadcasted_iota(jnp.int32, (tq, tk), 0)
            kpos = j * tk + lax.broadcasted_iota(jnp.int32, (tq, tk), 1)
            mask = kpos <= qpos
            q, k = q_ref[...], k_ref[...]
            v = v_ref[...].astype(BF16)
            dy = dy_ref[...]
            cqv, ckv, lsev, dlv = cq_ref[...], ck_ref[...], lse_ref[...], dl_ref[...]
            lane = lax.broadcasted_iota(jnp.int32, (tq, LANE), 1)
            rows = pl.ds(pl.multiple_of(i * tq, tq), tq)
            dcq_t = jnp.zeros((tq, LANE), F32)
            for hp in range(B_HEADS // 2):
                dq_new, dk_new, dv_new = [], [], []
                for u in range(2):
                    h = 2 * hp + u
                    sl = slice(64 * h, 64 * h + 64)
                    sc = _dot_nt(q[:, sl], k[:, sl]) + cqv[:, h:h + 1] - ckv[h:h + 1, :]
                    sc = jnp.where(mask, sc, NEG)
                    p = jnp.exp(sc - lsev[:, h:h + 1])
                    dp = _dot_nt(dy[:, sl], v[:, sl])
                    ds = p * (dp - dlv[:, h:h + 1])
                    dsb = ds.astype(BF16)
                    dv_new.append(_dot_tn(p.astype(BF16), dy[:, sl]))
                    dk_new.append(_dot_tn(dsb, q[:, sl]))
                    dq_new.append(_dot(dsb, k[:, sl]))
                    dcq_t = jnp.where(lane == h, jnp.sum(ds, axis=-1, keepdims=True), dcq_t)
                    dck_acc[h:h + 1, :] -= jnp.sum(ds, axis=0, keepdims=True)
                cols = slice(128 * hp, 128 * hp + 128)
                dq_ref[rows, cols] += jnp.concatenate(dq_new, axis=1)
                dk_acc[:, cols] += jnp.concatenate(dk_new, axis=1)
                dv_acc[:, cols] += jnp.concatenate(dv_new, axis=1)
            dcq_ref[rows, :] += dcq_t

        @pl.when(i == nq - 1)
        def _():
            dk_ref[...] = dk_acc[...]
            dv_ref[...] = dv_acc[...].astype(BF16)
            dck_ref[...] = jnp.concatenate([dck_acc[...], jnp.zeros((LANE - 16, tk), F32)], axis=0)

    qmap = lambda j, i: (jnp.maximum(i, first_q(j)), 0)
    return pl.pallas_call(
        body, grid=(nk, nq),
        in_specs=[pl.BlockSpec((tq, 768), qmap),
                  pl.BlockSpec((tk, 768), lambda j, i: (j, 0)),
                  pl.BlockSpec((pl.Element(tk), pl.Element(768)), lambda j, i: (j * tk, VB)),
                  pl.BlockSpec((tq, LANE), qmap),
                  pl.BlockSpec((16, tk), lambda j, i: (0, j)),
                  pl.BlockSpec((tq, LANE), qmap),
                  pl.BlockSpec((tq, LANE), qmap),
                  pl.BlockSpec((tq, 768), qmap)],
        out_specs=[_const((s, 768)), _const((s, LANE)),
                   pl.BlockSpec((tk, 768), lambda j, i: (j, 0)), pl.BlockSpec((tk, 768), lambda j, i: (j, 0)),
                   pl.BlockSpec((LANE, tk), lambda j, i: (0, j))],
        out_shape=[SDS((s, 768), F32), SDS((s, LANE), F32), SDS((s, 768), F32), SDS((s, 768), BF16),
                   SDS((LANE, s), F32)],
        scratch_shapes=[pltpu.VMEM((tk, 768), F32), pltpu.VMEM((tk, 768), F32), pltpu.VMEM((16, tk), F32)],
        name="fox_bwd", compiler_params=_params(("arbitrary", "arbitrary")))(qn, kn, proj, cq, ck, lse, delta, dyb)


def _fox_bwd_post(proj, fbl, qg, kg, bfor, bd, dqn, dkn, dcq, dck, *, tr):
    s = proj.shape[0]
    nb = s // tr
    triu = jnp.asarray(np.triu(np.ones((tr, tr), np.float32)))
    rev = lambda i: nb - 1 - i

    def body(q_ref, k_ref, fb_ref, qg_ref, kg_ref, bf_ref, bd_ref, tri_ref, dqn_ref, dkn_ref, dcq_ref, dck_ref,
             dq_ref, dk_ref, dfb_ref, dqg_ref, dkg_ref, dbf_ref, carry):
        @pl.when(pl.program_id(0) == 0)
        def _():
            carry[...] = jnp.zeros_like(carry)
            dqg_ref[...] = jnp.zeros_like(dqg_ref)
            dkg_ref[...] = jnp.zeros_like(dkg_ref)
            dbf_ref[...] = jnp.zeros_like(dbf_ref)

        bd_v = bd_ref[...]
        _, vjp_q = jax.vjp(lambda q, g: _qn_fn(q, g, bd_v), q_ref[...], qg_ref[...])
        dq, dqg = vjp_q(dqn_ref[...])
        _, vjp_k = jax.vjp(lambda k, g: _kn_fn(k, g, bd_v), k_ref[...], kg_ref[...])
        dk, dkg = vjp_k(dkn_ref[...])
        dq_ref[...] = dq.astype(BF16)
        dk_ref[...] = dk.astype(BF16)
        dqg_ref[...] += dqg
        dkg_ref[...] += dkg

        dc = dcq_ref[...] + jnp.transpose(dck_ref[...])
        dlogf = jnp.dot(tri_ref[...], dc, precision=HI, preferred_element_type=F32) + carry[...]
        carry[...] = dlogf[0:1, :]
        lane = lax.broadcasted_iota(jnp.int32, (tr, LANE), 1)
        xf = fb_ref[...] + bf_ref[...]
        dfb = jnp.where(lane < N_FORGET, dlogf * jax.nn.sigmoid(-xf), 0.0)
        dfb_ref[...] = dfb.astype(BF16)
        dbf_ref[...] += jnp.sum(dfb, axis=0, keepdims=True)

    rb = lambda w: pl.BlockSpec((tr, w), lambda i: (rev(i), 0))
    wn = lambda w, off: pl.BlockSpec((pl.Element(tr), pl.Element(w)), lambda i: (rev(i) * tr, off))
    return pl.pallas_call(
        body, grid=(nb,),
        in_specs=[wn(768, QB), wn(768, KB), rb(LANE), _const((1, 768)), _const((1, 768)), _const((1, LANE)),
                  _const((768, 768)), _const((tr, tr)), rb(768), rb(768), rb(LANE),
                  pl.BlockSpec((LANE, tr), lambda i: (0, rev(i)))],
        out_specs=[rb(768), rb(768), rb(LANE), _const((1, 768)), _const((1, 768)), _const((1, LANE))],
        out_shape=[SDS((s, 768), BF16), SDS((s, 768), BF16), SDS((s, LANE), BF16), SDS((1, 768), F32),
                   SDS((1, 768), F32), SDS((1, LANE), F32)],
        scratch_shapes=[pltpu.VMEM((1, LANE), F32)], name="fox_bwd_post",
        compiler_params=_params(("arbitrary",)))(proj, proj, fbl, qg, kg, bfor, bd, triu, dqn, dkn, dcq, dck)


AUG = 128 * B_HEADS
COL_A, COL_B = 64, 67


def _split3(c):
    hi = c.astype(BF16)
    r1 = c - hi.astype(F32)
    mid = r1.astype(BF16)
    lo = (r1 - mid.astype(F32)).astype(BF16)
    return hi, mid, lo


def _expand_mats():
    def mat(col0):
        e = np.zeros((768 + 3 * LANE, AUG), np.float32)
        for h in range(B_HEADS):
            for d in range(HEAD_DIM):
                e[64 * h + d, 128 * h + d] = 1.0
            for part in range(3):
                e[768 + LANE * part + h, 128 * h + col0 + part] = 1.0
        return e

    def ones(col0):
        o = np.zeros((1, AUG), np.float32)
        for h in range(B_HEADS):
            o[0, 128 * h + col0:128 * h + col0 + 3] = 1.0
        return o

    return (jnp.asarray(mat(COL_A), BF16), jnp.asarray(mat(COL_B), BF16), jnp.asarray(ones(COL_A)), jnp.asarray(ones(COL_B)))


def _augment(data_bf16, triple, emat, ones_row):
    parts = [data_bf16] + (list(triple) if triple is not None else [jnp.zeros((data_bf16.shape[0], LANE), BF16)] * 3)
    wide = _dot(jnp.concatenate(parts, axis=1), emat)
    if ones_row is not None:
        wide = wide + ones_row
    return wide


def _compact(wide):
    return jnp.concatenate([wide[:, 128 * h:128 * h + 64] for h in range(wide.shape[1] // 128)], axis=1)


def _lane_of_heads(wide, col, first=0):
    rows = wide.shape[0]
    lane = lax.broadcasted_iota(jnp.int32, (rows, LANE), 1)
    out = jnp.zeros((rows, LANE), F32)
    for h in range(wide.shape[1] // 128):
        out = jnp.where(lane == first + h, wide[:, 128 * h + col:128 * h + col + 1], out)
    return out


def _fox2_prep(proj, fbl, qg, kg, bfor, bd, ea, eb, ones_a, ones_b, *, tr):
    s = proj.shape[0]
    tri = jnp.asarray(np.tril(np.ones((tr, tr), np.float32)))

    def body(q_ref, k_ref, v_ref, fb_ref, qg_ref, kg_ref, bf_ref, bd_ref, tri_ref, ea_ref, eb_ref, oa_ref, ob_ref,
             qat_ref, ka_ref, kat_ref, va_ref, vat_ref, qn_ref, c_ref, carry):
        @pl.when(pl.program_id(0) == 0)
        def _():
            carry[...] = jnp.zeros_like(carry)

        bd_v = bd_ref[...]
        lane = lax.broadcasted_iota(jnp.int32, (tr, LANE), 1)
        logf = jnp.where(lane < N_FORGET, _log_sigmoid(fb_ref[...] + bf_ref[...]), 0.0)
        c = jnp.dot(tri_ref[...], logf, precision=HI, preferred_element_type=F32) + carry[...]
        c_ref[...] = c
        carry[...] = c[tr - 1:tr, :]
        qn = _qn_fn(q_ref[...], qg_ref[...], bd_v).astype(BF16)
        kn = _kn_fn(k_ref[...], kg_ref[...], bd_v).astype(BF16)
        qn_ref[...] = qn
        qat_ref[...] = jnp.transpose(_augment(qn, _split3(c), ea_ref[...], ob_ref[...])).astype(BF16)
        ka = _augment(kn, _split3(-c), eb_ref[...], oa_ref[...])
        ka_ref[...] = ka.astype(BF16)
        kat_ref[...] = jnp.transpose(ka).astype(BF16)
        va = _augment(v_ref[...].astype(BF16), None, ea_ref[...], oa_ref[...])
        va_ref[...] = va.astype(BF16)
        vat_ref[...] = jnp.transpose(va).astype(BF16)

    emat = _const((768 + 3 * LANE, AUG))
    return pl.pallas_call(
        body, grid=(s // tr,),
        in_specs=[_win(tr, 768, QB), _win(tr, 768, KB), _win(tr, 768, VB), _rowblk(tr, LANE), _const((1, 768)),
                  _const((1, 768)), _const((1, LANE)), _const((768, 768)), _const((tr, tr)), emat, emat,
                  _const((1, AUG)), _const((1, AUG))],
        out_specs=[pl.BlockSpec((AUG, tr), lambda i: (0, i)), _rowblk(tr, AUG), pl.BlockSpec((AUG, tr), lambda i: (0, i)),
                   _rowblk(tr, AUG), pl.BlockSpec((AUG, tr), lambda i: (0, i)), _rowblk(tr, 768), _rowblk(tr, LANE)],
        out_shape=[SDS((AUG, s), BF16), SDS((s, AUG), BF16), SDS((AUG, s), BF16), SDS((s, AUG), BF16),
                   SDS((AUG, s), BF16), SDS((s, 768), BF16), SDS((s, LANE), F32)],
        scratch_shapes=[pltpu.VMEM((1, LANE), F32)], name="fox_prep",
        compiler_params=_params(("arbitrary",)))(proj, proj, proj, fbl, qg, kg, bfor, bd, tri, ea, eb, ones_a, ones_b)


def _fox2_fwd(proj, qat, ka, vat):
    s = proj.shape[0]
    tq, tk = min(FOX_FWD_TQ, s), min(FOX_FWD_TK, s)
    nq, nk = s // tq, s // tk

    def last_k(i):
        return (i * tq + tq - 1) // tk

    def body(qt_ref, k_ref, vt_ref, z_ref, gb_ref, yb_ref, lse_ref, acc, m_s):
        i, j = pl.program_id(0), pl.program_id(1)

        @pl.when(j == 0)
        def _():
            acc[...] = jnp.zeros_like(acc)
            m_s[...] = jnp.full_like(m_s, NEG)

        def tile(masked):
            if masked:
                kpos = j * tk + lax.broadcasted_iota(jnp.int32, (tk, tq), 0)
                qpos = i * tq + lax.broadcasted_iota(jnp.int32, (tk, tq), 1)
                mask = kpos <= qpos
            for h in range(B_HEADS):
                sl = slice(128 * h, 128 * h + 128)
                sc = _dot(k_ref[:, sl], qt_ref[sl, :])
                if masked:
                    sc = jnp.where(mask, sc, NEG)
                m_prev = m_s[h:h + 1, :]
                m_new = jnp.maximum(m_prev, jnp.max(sc, axis=0, keepdims=True))
                p = jnp.exp(sc - m_new).astype(BF16)
                acc[sl, :] = jnp.exp(m_prev - m_new) * acc[sl, :] + _dot(vt_ref[sl, :], p)
                m_s[h:h + 1, :] = m_new

        full = j * tk + tk - 1 <= i * tq

        @pl.when(full)
        def _():
            tile(False)

        @pl.when(jnp.logical_and(jnp.logical_not(full), j <= last_k(i)))
        def _():
            tile(True)

        @pl.when(j == nk - 1)
        def _():
            outs = []
            row = lax.broadcasted_iota(jnp.int32, (LANE, tq), 0)
            lse_t = jnp.zeros((LANE, tq), F32)
            for h in range(B_HEADS):
                l_row = acc[128 * h + COL_A:128 * h + COL_A + 1, :]
                outs.append(acc[128 * h:128 * h + 64, :] * (1.0 / l_row))
                lse_t = jnp.where(row == h, m_s[h:h + 1, :] + jnp.log(l_row), lse_t)
            y = jnp.transpose(jnp.concatenate(outs, axis=0))
            yb_ref[...] = y
            gb_ref[...] = (y * _silu(z_ref[...])).astype(BF16)
            lse_ref[...] = jnp.transpose(lse_t)

    kcol = lambda i, j: (0, jnp.minimum(j, last_k(i)))
    return pl.pallas_call(
        body, grid=(nq, nk),
        in_specs=[pl.BlockSpec((AUG, tq), lambda i, j: (0, i)),
                  pl.BlockSpec((tk, AUG), lambda i, j: (jnp.minimum(j, last_k(i)), 0)),
                  pl.BlockSpec((AUG, tk), kcol),
                  pl.BlockSpec((pl.Element(tq), pl.Element(768)), lambda i, j: (i * tq, ZB))],
        out_specs=[pl.BlockSpec((tq, 768), lambda i, j: (i, 0)), pl.BlockSpec((tq, 768), lambda i, j: (i, 0)),
                   pl.BlockSpec((tq, LANE), lambda i, j: (i, 0))],
        out_shape=[SDS((s, 768), BF16), SDS((s, 768), F32), SDS((s, LANE), F32)],
        scratch_shapes=[pltpu.VMEM((AUG, tq), F32), pltpu.VMEM((16, tq), F32)],
        name="fox_fwd", compiler_params=_params(("parallel", "arbitrary")))(qat, ka, vat, proj)


def _fox2_bwd_pre(proj, yb, dgb, qn, c, lse, hsum, ea, ones_b, *, tr):
    s = proj.shape[0]

    def body(z_ref, y_ref, dg_ref, qn_ref, c_ref, lse_ref, hs_ref, ea_ref, ob_ref,
             qa_ref, qat_ref, dya_ref, dyat_ref, dz_ref):
        z, y, dg = z_ref[...], y_ref[...], dg_ref[...]
        sg = jax.nn.sigmoid(z)
        dy = dg * (z * sg)
        dz_ref[...] = (dg * y * (sg * (1.0 + z * (1.0 - sg)))).astype(BF16)
        delta = jnp.dot(dy * y, hs_ref[...], precision=HI, preferred_element_type=F32)
        e = ea_ref[...]
        dya = _augment(dy.astype(BF16), _split3(-delta), e, None)
        dya_ref[...] = dya.astype(BF16)
        dyat_ref[...] = jnp.transpose(dya).astype(BF16)
        qa = _augment(qn_ref[...], _split3(c_ref[...] - lse_ref[...]), e, ob_ref[...])
        qa_ref[...] = qa.astype(BF16)
        qat_ref[...] = jnp.transpose(qa).astype(BF16)

    return pl.pallas_call(
        body, grid=(s // tr,),
        in_specs=[_win(tr, 768, ZB), _rowblk(tr, 768), _rowblk(tr, 768), _rowblk(tr, 768), _rowblk(tr, LANE),
                  _rowblk(tr, LANE), _const((768, LANE)), _const((768 + 3 * LANE, AUG)), _const((1, AUG))],
        out_specs=[_rowblk(tr, AUG), pl.BlockSpec((AUG, tr), lambda i: (0, i)), _rowblk(tr, AUG),
                   pl.BlockSpec((AUG, tr), lambda i: (0, i)), _rowblk(tr, 768)],
        out_shape=[SDS((s, AUG), BF16), SDS((AUG, s), BF16), SDS((s, AUG), BF16), SDS((AUG, s), BF16),
                   SDS((s, 768), BF16)], name="fox_bwd_pre",
        compiler_params=_params(("parallel",)))(proj, yb, dgb, qn, c, lse, hsum, ea, ones_b)


def _fox2_bwd(qb, qbt, ka, kat, va, dya, dyat):
    s = qb.shape[0]
    tq, tk = _fox_tiles(s)
    nq, nk = s // tq, s // tk
    ng = 2
    gh = B_HEADS // ng
    gw = 128 * gh

    def first_q(j):
        return (j * tk) // tq

    def body(q_ref, qt_ref, k_ref, kt_ref, v_ref, dy_ref, dyt_ref, dq_hbm, dk_ref, dv_ref, dck_ref,
             dq_acc, dk_acc, dv_acc, sem):
        g, j, i = pl.program_id(0), pl.program_id(1), pl.program_id(2)

        @pl.when((j == 0) & (i == 0))
        def _():
            dq_acc[...] = jnp.zeros_like(dq_acc)

        @pl.when(i == 0)
        def _():
            dk_acc[...] = jnp.zeros_like(dk_acc)
            dv_acc[...] = jnp.zeros_like(dv_acc)

        def tile(masked):
            if masked:
                kpos = j * tk + lax.broadcasted_iota(jnp.int32, (tk, tq), 0)
                qpos = i * tq + lax.broadcasted_iota(jnp.int32, (tk, tq), 1)
                mask = kpos <= qpos
            cols = pl.ds(pl.multiple_of(i * tq, tq), tq)
            for h in range(gh):
                sl = slice(128 * h, 128 * h + 128)
                sc = _dot(k_ref[:, sl], qt_ref[sl, :])
                if masked:
                    sc = jnp.where(mask, sc, NEG)
                p = jnp.exp(sc)
                ds = (p * _dot(v_ref[:, sl], dyt_ref[sl, :])).astype(BF16)
                dv_acc[:, sl] += _dot(p.astype(BF16), dy_ref[:, sl])
                dk_acc[:, sl] += _dot(ds, q_ref[:, sl])
                dq_acc[sl, cols] += _dot(kt_ref[sl, :], ds)

        full = j * tk + tk - 1 <= i * tq

        @pl.when(full)
        def _():
            tile(False)

        @pl.when(jnp.logical_and(jnp.logical_not(full), i >= first_q(j)))
        def _():
            tile(True)

        @pl.when(i == nq - 1)
        def _():
            dkw = dk_acc[...]
            dk_ref[...] = _compact(dkw)
            dv_ref[...] = _compact(dv_acc[...]).astype(BF16)
            dck_ref[...] = -_lane_of_heads(dkw, COL_B, gh * g)

        @pl.when((j == nk - 1) & (i == nq - 1))
        def _():
            cp = pltpu.make_async_copy(dq_acc, dq_hbm.at[pl.ds(pl.multiple_of(g * gw, gw), gw)], sem)
            cp.start()
            cp.wait()

    qrow = pl.BlockSpec((tq, gw), lambda g, j, i: (jnp.maximum(i, first_q(j)), g))
    qcol = pl.BlockSpec((gw, tq), lambda g, j, i: (g, jnp.maximum(i, first_q(j))))
    krow = pl.BlockSpec((tk, gw), lambda g, j, i: (j, g))
    kcol = pl.BlockSpec((gw, tk), lambda g, j, i: (g, j))
    kout = pl.BlockSpec((tk, gw // 2), lambda g, j, i: (j, g))
    return pl.pallas_call(
        body, grid=(ng, nk, nq),
        in_specs=[qrow, qcol, krow, kcol, krow, qrow, qcol],
        out_specs=[pl.BlockSpec(memory_space=pl.ANY), kout, kout,
                   pl.BlockSpec((None, tk, LANE), lambda g, j, i: (g, j, 0))],
        out_shape=[SDS((AUG, s), F32), SDS((s, 768), F32), SDS((s, 768), BF16), SDS((ng, s, LANE), F32)],
        scratch_shapes=[pltpu.VMEM((gw, s), F32), pltpu.VMEM((tk, gw), F32), pltpu.VMEM((tk, gw), F32),
                        pltpu.SemaphoreType.DMA],
        name="fox_bwd", compiler_params=_params(("arbitrary",) * 3))(qb, qbt, ka, kat, va, dya, dyat)


def _fox2_bwd_post(proj, fbl, qg, kg, bfor, bd, dqa, dkn, dck, *, tr):
    s = proj.shape[0]
    nb = s // tr
    triu = jnp.asarray(np.triu(np.ones((tr, tr), np.float32)))
    rev = lambda i: nb - 1 - i

    def body(q_ref, k_ref, fb_ref, qg_ref, kg_ref, bf_ref, bd_ref, tri_ref, dqa_ref, dkn_ref, dck_ref,
             dq_ref, dk_ref, dfb_ref, dqg_ref, dkg_ref, dbf_ref, carry):
        @pl.when(pl.program_id(0) == 0)
        def _():
            carry[...] = jnp.zeros_like(carry)
            dqg_ref[...] = jnp.zeros_like(dqg_ref)
            dkg_ref[...] = jnp.zeros_like(dkg_ref)
            dbf_ref[...] = jnp.zeros_like(dbf_ref)

        bd_v = bd_ref[...]
        dqw = jnp.transpose(dqa_ref[...])
        _, vjp_q = jax.vjp(lambda q, g: _qn_fn(q, g, bd_v), q_ref[...], qg_ref[...])
        dq, dqg = vjp_q(_compact(dqw))
        _, vjp_k = jax.vjp(lambda k, g: _kn_fn(k, g, bd_v), k_ref[...], kg_ref[...])
        dk, dkg = vjp_k(dkn_ref[...])
        dq_ref[...] = dq.astype(BF16)
        dk_ref[...] = dk.astype(BF16)
        dqg_ref[...] += dqg
        dkg_ref[...] += dkg

        dc = _lane_of_heads(dqw, COL_A) + (dck_ref[0] + dck_ref[1])
        dlogf = jnp.dot(tri_ref[...], dc, precision=HI, preferred_element_type=F32) + carry[...]
        carry[...] = dlogf[0:1, :]
        lane = lax.broadcasted_iota(jnp.int32, (tr, LANE), 1)
        xf = fb_ref[...] + bf_ref[...]
        dfb = jnp.where(lane < N_FORGET, dlogf * jax.nn.sigmoid(-xf), 0.0)
        dfb_ref[...] = dfb.astype(BF16)
        dbf_ref[...] += jnp.sum(dfb, axis=0, keepdims=True)

    rb = lambda w: pl.BlockSpec((tr, w), lambda i: (rev(i), 0))
    wn = lambda w, off: pl.BlockSpec((pl.Element(tr), pl.Element(w)), lambda i: (rev(i) * tr, off))
    return pl.pallas_call(
        body, grid=(nb,),
        in_specs=[wn(768, QB), wn(768, KB), rb(LANE), _const((1, 768)), _const((1, 768)), _const((1, LANE)),
                  _const((768, 768)), _const((tr, tr)), pl.BlockSpec((AUG, tr), lambda i: (0, rev(i))), rb(768),
                  pl.BlockSpec((2, tr, LANE), lambda i: (0, rev(i), 0))],
        out_specs=[rb(768), rb(768), rb(LANE), _const((1, 768)), _const((1, 768)), _const((1, LANE))],
        out_shape=[SDS((s, 768), BF16), SDS((s, 768), BF16), SDS((s, LANE), BF16), SDS((1, 768), F32),
                   SDS((1, 768), F32), SDS((1, LANE), F32)],
        scratch_shapes=[pltpu.VMEM((1, LANE), F32)], name="fox_bwd_post",
        compiler_params=_params(("arbitrary",)))(proj, proj, fbl, qg, kg, bfor, bd, triu, dqa, dkn, dck)


def _merge_specs(tr):
    row = lambda w: pl.BlockSpec((tr, w), lambda i, j: (i, 0))
    shard = lambda r: pl.BlockSpec((None, r, 512), lambda i, j: (j, 0, 0))
    gate = lambda b: pl.BlockSpec((tr, 512), lambda i, j: (i, (GATE + 2048 * b) // 512 + j))
    return [row(768), row(768), row(512), shard(768), shard(768), shard(512), gate(0), gate(1), gate(2)]


def _merge_fwd(proj, ga, gb, gc, wa, wb, wc, *, tr):
    s = proj.shape[0]

    def body(ga_ref, gb_ref, gc_ref, wa_ref, wb_ref, wc_ref, l0_ref, l1_ref, l2_ref, y_ref):
        ua = _dot(ga_ref[...], wa_ref[...])
        ub = _dot(gb_ref[...], wb_ref[...])
        uc = _dot(gc_ref[...], wc_ref[...])
        y = jax.nn.sigmoid(l0_ref[...]) * ua + jax.nn.sigmoid(l1_ref[...]) * ub + jax.nn.sigmoid(l2_ref[...]) * uc
        y_ref[...] = y.astype(BF16)

    return pl.pallas_call(
        body, grid=(s // tr, N_CHIPS), in_specs=_merge_specs(tr),
        out_specs=pl.BlockSpec((tr, 512), lambda i, j: (i, j)), out_shape=SDS((s, D_MODEL), BF16), name="merge_fwd",
        compiler_params=_params(("parallel", "arbitrary")))(ga, gb, gc, wa, wb, wc, proj, proj, proj)


def _merge_bwd(proj, ga, gb, gc, wa, wb, wc, dy, *, tr):
    s = proj.shape[0]

    def body(ga_ref, gb_ref, gc_ref, wa_ref, wb_ref, wc_ref, l0_ref, l1_ref, l2_ref, dy_ref,
             dl0_ref, dl1_ref, dl2_ref, dua_ref, dub_ref, duc_ref, dga_ref, dgb_ref, dgc_ref):
        j = pl.program_id(1)
        dyv = dy_ref[...]

        @pl.when(j == 0)
        def _():
            dga_ref[...] = jnp.zeros_like(dga_ref)
            dgb_ref[...] = jnp.zeros_like(dgb_ref)
            dgc_ref[...] = jnp.zeros_like(dgc_ref)

        for g_ref, w_ref, l_ref, dl_ref, du_ref, dg_ref in (
                (ga_ref, wa_ref, l0_ref, dl0_ref, dua_ref, dga_ref),
                (gb_ref, wb_ref, l1_ref, dl1_ref, dub_ref, dgb_ref),
                (gc_ref, wc_ref, l2_ref, dl2_ref, duc_ref, dgc_ref)):
            w = w_ref[...]
            u = _dot(g_ref[...], w)
            sg = jax.nn.sigmoid(l_ref[...])
            dl_ref[...] = (dyv * u * sg * (1.0 - sg)).astype(BF16)
            du = (dyv * sg).astype(BF16)
            du_ref[...] = du
            dg_ref[...] += _dot_nt(du, w)

    blk = pl.BlockSpec((tr, 512), lambda i, j: (i, j))
    row = lambda w: pl.BlockSpec((tr, w), lambda i, j: (i, 0))
    big = SDS((s, D_MODEL), BF16)
    return pl.pallas_call(
        body, grid=(s // tr, N_CHIPS), in_specs=_merge_specs(tr) + [blk],
        out_specs=[blk] * 6 + [row(768), row(768), row(512)],
        out_shape=[big] * 6 + [SDS((s, 768), F32), SDS((s, 768), F32), SDS((s, 512), F32)], name="merge_bwd",
        compiler_params=_params(("parallel", "arbitrary")))(ga, gb, gc, wa, wb, wc, proj, proj, proj, dy)


def _out_loss(y, wo, x, tgt, *, tr, tn):
    s = x.shape[0]

    def body(y_ref, w_ref, x_ref, t_ref, d_ref, db_ref, sq_ref):
        @pl.when((pl.program_id(0) == 0) & (pl.program_id(1) == 0))
        def _():
            sq_ref[...] = jnp.zeros_like(sq_ref)

        out = x_ref[...] + _dot(y_ref[...], w_ref[...])
        diff = out - t_ref[...]
        sq_ref[...] += jnp.sum(diff * diff, axis=0, keepdims=True)
        d = diff * (1.0 / D_MODEL)
        d_ref[...] = d
        db_ref[...] = d.astype(BF16)

    blk = pl.BlockSpec((tr, tn), lambda i, j: (i, j))
    return pl.pallas_call(
        body, grid=(s // tr, D_MODEL // tn),
        in_specs=[pl.BlockSpec((tr, D_MODEL), lambda i, j: (i, 0)), pl.BlockSpec((D_MODEL, tn), lambda i, j: (0, j)), blk, blk],
        out_specs=[blk, blk, _const((1, tn))],
        out_shape=[SDS((s, D_MODEL), F32), SDS((s, D_MODEL), BF16), SDS((1, tn), F32)], name="out_loss",
        compiler_params=_params(("arbitrary", "arbitrary")))(y, wo, x, tgt)


def _tile_gain(g, reps):
    return jnp.tile(g.reshape(1, -1), (1, reps))


def _pad_lane(v):
    v = v.reshape(1, -1)
    return jnp.pad(v, ((0, 0), (0, LANE - v.shape[1])))


def _local_step(x, mem, tgt, w_main, w_fb, w_small, norm_gain, mem_norm_gain, b_forget,
                q_gain_a, k_gain_a, sinks_a, q_gain_b, k_gain_b, q_gain_c, k_gain_c, core=None):
    s = x.shape[0]
    tr = min(512, s)
    bd64 = _block_diag(768, HEAD_DIM)
    bd128 = _block_diag(512, C_HEAD_DIM)
    hsum = _head_sum(768, HEAD_DIM)
    qga, kga = _tile_gain(q_gain_a, 12), _tile_gain(k_gain_a, 4)
    qgb, kgb = _tile_gain(q_gain_b, 12), _tile_gain(k_gain_b, 12)
    qgc, kgc = _tile_gain(q_gain_c, 4), _tile_gain(k_gain_c, 4)
    sinks = _pad_lane(sinks_a)
    bfor = _pad_lane(b_forget)

    hn = _rms_fwd(x, norm_gain, tr=tr, name="rms_x")
    on_mesh = core is not None
    if on_mesh:
        proj, (gathered,) = _matmul(hn, w_main, dims="nn", out_dtype=F32, tm=1024, tn=512, tk=D_MODEL, name="proj_main",
                                    comms=[_gather_comm(list(w_small))])
        w_mk, wa, wb, wc, wo = gathered
        w_mk, wo = w_mk.reshape(D_MODEL, 1024), wo.reshape(D_MODEL, D_MODEL)
    else:
        proj = _matmul(hn, w_main, dims="nn", out_dtype=F32, tm=1024, tn=512, tk=D_MODEL, name="proj_main")
        w_mk, wa, wb, wc, wo = w_small
    fbl = _matmul(hn, w_fb, dims="nn", out_dtype=F32, tm=1024, tn=LANE, tk=D_MODEL, name="proj_forget")
    memn = _rms_fwd(mem, mem_norm_gain, tr=mem.shape[0], name="rms_mem")
    mkv = _matmul(memn, w_mk, dims="nn", out_dtype=F32, tm=256, tn=512, tk=D_MODEL, name="mem_kv")

    swa_bias = _swa_bias()
    ga = _swa_fwd(proj, qga, kga, sinks, bd64, swa_bias)
    ea, eb, ones_a, ones_b = _expand_mats()
    tf = min(256, s)
    qat, ka, kat, va, vat, qn, cfox = _fox2_prep(proj, fbl, qgb, kgb, bfor, bd64, ea, eb, ones_a, ones_b, tr=tf)
    gb, yb, lse = _fox2_fwd(proj, qat, ka, vat)
    gc = _mem_fwd(proj, mkv, qgc, kgc, bd128, tr=tr)
    y = _merge_fwd(proj, ga, gb, gc, wa, wb, wc, tr=tr)
    dout, dout_b, sq = _out_loss(y, wo, x, tgt, tr=tr, tn=512)

    d_wo = _matmul(y, dout_b, dims="tn", out_dtype=F32, tm=1024, tn=512, tk=4096, name="dw_out")
    dy = _matmul(dout_b, wo, dims="nt", out_dtype=F32, tm=1024, tn=512, tk=D_MODEL, name="dy")
    dl0, dl1, dl2, dua, dub, duc, dga, dgb, dgc = _merge_bwd(proj, ga, gb, gc, wa, wb, wc, dy, tr=tr)
    d_wa = _matmul(ga, dua, dims="tn", out_dtype=F32, tm=768, tn=512, tk=4096, name="dw_branch_a")
    d_wb = _matmul(gb, dub, dims="tn", out_dtype=F32, tm=768, tn=512, tk=4096, name="dw_branch_b")
    d_wc = _matmul(gc, duc, dims="tn", out_dtype=F32, tm=512, tn=512, tk=4096, name="dw_branch_c")

    dcur, dprev, d_qga, d_kga, d_sinks = _swa_bwd(proj, qga, kga, sinks, bd64, swa_bias, dga)
    dproj_a = _swa_combine(dcur, dprev)

    qab, qabt, dya, dyat, dzb = _fox2_bwd_pre(proj, yb, dgb, qn, cfox, lse, hsum, ea, ones_b, tr=tf)
    dqa, dkn, dvb, dck = _fox2_bwd(qab, qabt, ka, kat, va, dya, dyat)
    dqb, dkb, dfb, d_qgb, d_kgb, d_bf = _fox2_bwd_post(proj, fbl, qgb, kgb, bfor, bd64, dqa, dkn, dck, tr=tf)

    dproj_c, dmkv, d_qgc, d_kgc = _mem_bwd(proj, mkv, qgc, kgc, bd128, dgc, tr=tr)
    dmkv_b = dmkv.astype(BF16)
    d_wmk = _matmul(memn, dmkv_b, dims="tn", out_dtype=F32, tm=1024, tn=512, tk=256, name="dw_mem_kv")
    dmemn = _matmul(dmkv_b, w_mk, dims="nt", out_dtype=F32, tm=256, tn=512, tk=1024, name="dmemn")
    (d_mem_gain,) = _rms_bwd(mem, mem_norm_gain, dmemn, None, tr=mem.shape[0], name="rms_mem_bwd")

    dproj = [dproj_a, jnp.concatenate([dqb, dkb, dvb, dzb, dproj_c], axis=1), dl0, dl1, dl2]
    dhn_f = _matmul(dfb, w_fb, dims="nt", out_dtype=F32, tm=1024, tn=512, tk=LANE, name="dhn_forget")
    d_wfb = _matmul(hn, dfb, dims="tn", out_dtype=F32, tm=1024, tn=LANE, tk=512, name="dw_forget")
    big = {}
    if on_mesh:
        half = D_MODEL // 2
        c0 = core[0]
        hn_other = lax.dynamic_slice(hn, (0, (1 - c0) * half), (s, half))
        hn_own = lax.dynamic_slice(hn, (0, c0 * half), (s, half))
        g1, k1 = [d_wmk, d_wa, d_wb, d_wc, d_wo], [2, 3, 4, 5, 6]
        d_other, (got1,) = _matmul(hn_other, dproj, dims="tn", out_dtype=F32, tm=1024, tn=512, tk=4096,
                                   name="dw_main_other", comms=[_exchange_comm(g1, k1)])
        h1 = [_add_half(g, got, core, HALF_AXIS[k], name=f"add_half_{k}") for g, got, k in zip(g1, got1, k1)]
        d_own, (got0, parts1) = _matmul(
            hn_own, dproj, dims="tn", out_dtype=F32, tm=1024, tn=512, tk=4096, name="dw_main_own",
            comms=[_exchange_comm([d_other, d_wfb], [0, 1], whole=(0,)), _scatter_comm(h1, k1)])
        h0 = [_add_pair(d_own, got0[0], name="add_pair_main"), _add_half(d_wfb, got0[1], core, 0, name="add_half_1")]
        sums1 = [_sum4(p, name=f"sum4_{k}") for p, k in zip(parts1, k1)]
        dhn, (parts0, theirs1) = _matmul(dproj, w_main, dims="nt", out_dtype=F32, tm=1024, tn=512, tk=2048, vmem=VMEM_WIDE, name="dhn",
                                         add=dhn_f, comms=[_scatter_comm(h0, [0, 1]), _swap_comm(sums1)])
        sums0 = [_sum4(p, name=f"sum4_{k}") for p, k in zip(parts0, (0, 1))]
        theirs0 = _run_comm(_swap_comm(sums0), "swap_halves")
        big = dict(sums=sums0 + sums1, theirs=list(theirs0) + list(theirs1))
    else:
        dhn = _matmul(dproj, w_main, dims="nt", out_dtype=F32, tm=1024, tn=512, tk=2048, vmem=VMEM_WIDE, name="dhn", add=dhn_f)
        d_wmain = _matmul(hn, dproj, dims="tn", out_dtype=F32, tm=1024, tn=512, tk=4096, name="dw_main")
        big = dict(d_wmain=d_wmain, d_wfb=d_wfb, d_wmk=d_wmk, d_wa=d_wa, d_wb=d_wb, d_wc=d_wc, d_wo=d_wo)
    grad_x, d_gain = _rms_bwd(x, norm_gain, dhn, dout, tr=tr, name="rms_x_bwd")

    fold = lambda g, reps: jnp.sum(g.reshape(reps, -1), axis=0, keepdims=True)
    return dict(
        sq=sq, grad_x=grad_x, **big,
        d_gain=d_gain, d_mem_gain=d_mem_gain, d_bf=d_bf[:, :N_FORGET],
        d_qga=fold(d_qga, 12), d_kga=fold(d_kga, 4), d_sinks=d_sinks[:, :A_HEADS],
        d_qgb=fold(d_qgb, 12), d_kgb=fold(d_kgb, 12), d_qgc=fold(d_qgc, 4), d_kgc=fold(d_kgc, 4))


PACK_ROWS = 256
FORGET_IN_SHARD = FORGET_COL - SHARD_COLS
AFTER_FORGET = FORGET_COL - SLAB_START[1]
END_CHIP1 = 2 * SHARD_COLS - N_FORGET - SLAB_START[1]


def _pack_w_in(chip, w):
    rows = w.shape[1]
    tr = PACK_ROWS

    def body(k_ref, w_ref, o_ref, scr):
        scr[...] = jnp.zeros_like(scr)
        scr[pl.ds(0, SHARD_COLS), :] = w_ref[...]
        v = jnp.transpose(scr[...])
        k = k_ref[0]
        col = lax.broadcasted_iota(jnp.int32, (tr, SLAB), 1)
        no_forget = jnp.zeros((tr, LANE), BF16)

        @pl.when(k == 0)
        def _():
            o_ref[:, 0:SLAB] = v.astype(BF16)
            o_ref[:, SLAB:] = no_forget

        @pl.when(k == 1)
        def _():
            before = pltpu.roll(v, SLAB_SHIFT[1], axis=1)
            after = pltpu.roll(v, SLAB - (N_FORGET - SLAB_SHIFT[1]), axis=1)
            slab = jnp.where(col < AFTER_FORGET, before, jnp.where(col < END_CHIP1, after, 0.0))
            o_ref[:, 0:SLAB] = slab.astype(BF16)
            f = pltpu.roll(v, SLAB - FORGET_IN_SHARD, axis=1)[:, :LANE]
            o_ref[:, SLAB:] = jnp.where(col[:, :LANE] < N_FORGET, f, 0.0).astype(BF16)

        for kk in (2, 3):
            @pl.when(k == kk)
            def _(kk=kk):
                o_ref[:, 0:SLAB] = pltpu.roll(v, SLAB_SHIFT[kk], axis=1).astype(BF16)
                o_ref[:, SLAB:] = no_forget

    return pl.pallas_call(
        body, grid_spec=pltpu.PrefetchScalarGridSpec(
            num_scalar_prefetch=1, grid=(rows // tr,),
            in_specs=[pl.BlockSpec((SHARD_COLS, tr), lambda i, k: (0, i))],
            out_specs=pl.BlockSpec((None, tr, SLAB + LANE), lambda i, k: (k[0], i, 0)),
            scratch_shapes=[pltpu.VMEM((SLAB, tr), F32)]),
        out_shape=SDS((N_CHIPS, rows, SLAB + LANE), BF16), name="pack_w_in",
        compiler_params=_params(("arbitrary",)))(chip, w)


def _merge_slabs(g):
    rows = g.shape[1]
    tr = PACK_ROWS
    t = [s // LANE for s in SLAB_START]
    n_t = SLAB // LANE

    def body(g_ref, m_ref, f_ref):
        for k in range(N_CHIPS):
            lo = t[k] + (1 if k > 0 else 0)
            hi = t[k + 1] if k + 1 < N_CHIPS else t[k] + n_t
            m_ref[:, lo * LANE:hi * LANE] = g_ref[k, :, (lo - t[k]) * LANE:(hi - t[k]) * LANE]
            if k + 1 < N_CHIPS:
                a = g_ref[k, :, (hi - t[k]) * LANE:(hi - t[k] + 1) * LANE].astype(F32)
                b = g_ref[k + 1, :, 0:LANE].astype(F32)
                m_ref[:, hi * LANE:(hi + 1) * LANE] = (a + b).astype(BF16)
        f_ref[...] = g_ref[1, :, SLAB:]

    return pl.pallas_call(
        body, grid=(rows // tr,),
        in_specs=[pl.BlockSpec((N_CHIPS, tr, SLAB + LANE), lambda i: (0, i, 0))],
        out_specs=[_rowblk(tr, P_MAIN), _rowblk(tr, LANE)],
        out_shape=[SDS((rows, P_MAIN), BF16), SDS((rows, LANE), BF16)], name="merge_slabs",
        compiler_params=_params(("parallel",)))(g)


def _adamw_math(w, g, m, v):
    nm = ADAM_B1 * m + (1.0 - ADAM_B1) * g
    nv = ADAM_B2 * v + (1.0 - ADAM_B2) * (g * g)
    m_hat = nm / (1.0 - ADAM_B1 ** ADAM_STEP)
    v_hat = nv / (1.0 - ADAM_B2 ** ADAM_STEP)
    delta = -ADAM_LR * (m_hat / (jnp.sqrt(v_hat) + ADAM_EPS) + ADAM_WD * w)
    return delta, nm, nv


def _adamw(g, w, m, v, *, tr, name):
    rows, cols = w.shape
    tr = min(tr, rows)

    def body(g_ref, w_ref, m_ref, v_ref, d_ref, nm_ref, nv_ref):
        d, nm, nv = _adamw_math(w_ref[...], g_ref[...], m_ref[...], v_ref[...])
        d_ref[...] = d
        nm_ref[...] = nm
        nv_ref[...] = nv

    spec = _rowblk(tr, cols)
    return pl.pallas_call(
        body, grid=(rows // tr,), in_specs=[spec] * 4, out_specs=[spec] * 3,
        out_shape=[SDS((rows, cols), F32)] * 3, name=name, compiler_params=_params(("parallel",)))(g, w, m, v)


def _adamw_w_in(chip_core, slab_mine, slab_theirs, forget_mine, forget_theirs, w, m, v):
    rows = w.shape[1]
    tr = PACK_ROWS // 2
    nbh = rows // 2 // tr

    def body(k_ref, sa_ref, sb_ref, fa_ref, fb_ref, w_ref, m_ref, v_ref, g_ref, d_ref, nm_ref, nv_ref):
        use_mine = pl.program_id(0) // nbh == k_ref[1]
        sl = jnp.where(use_mine, sa_ref[...], sb_ref[...])
        f_tile = jnp.where(use_mine, fa_ref[...], fb_ref[...])
        k = k_ref[0]

        def emit(wide):
            g = jnp.transpose(wide)[:SHARD_COLS, :]
            g_ref[...] = g
            d, nm, nv = _adamw_math(w_ref[...], g, m_ref[...], v_ref[...])
            d_ref[...] = d
            nm_ref[...] = nm
            nv_ref[...] = nv

        @pl.when(k == 0)
        def _():
            emit(sl)

        @pl.when(k == 1)
        def _():
            col = lax.broadcasted_iota(jnp.int32, (tr, SLAB), 1)
            before = pltpu.roll(sl, SLAB - SLAB_SHIFT[1], axis=1)
            after = pltpu.roll(sl, N_FORGET - SLAB_SHIFT[1], axis=1)
            wide_f = jnp.concatenate([f_tile, jnp.zeros((tr, SLAB - LANE), F32)], axis=1)
            forget = pltpu.roll(wide_f, FORGET_IN_SHARD, axis=1)
            emit(jnp.where(col < FORGET_IN_SHARD, before, jnp.where(col < FORGET_IN_SHARD + N_FORGET, forget, after)))

        for kk in (2, 3):
            @pl.when(k == kk)
            def _(kk=kk):
                emit(pltpu.roll(sl, SLAB - SLAB_SHIFT[kk], axis=1))

    nat = pl.BlockSpec((SHARD_COLS, tr), lambda i, k: (0, i))
    half = lambda width: pl.BlockSpec((tr, width), lambda i, k: (i % nbh, 0))
    return pl.pallas_call(
        body, grid_spec=pltpu.PrefetchScalarGridSpec(
            num_scalar_prefetch=1, grid=(rows // tr,),
            in_specs=[half(SLAB), half(SLAB), half(LANE), half(LANE), nat, nat, nat],
            out_specs=[nat] * 4),
        out_shape=[SDS((SHARD_COLS, rows), F32)] * 4, name="adamw_w_in",
        compiler_params=_params(("arbitrary",)))(chip_core, slab_mine, slab_theirs, forget_mine, forget_theirs, w, m, v)


ANY = pl.BlockSpec(memory_space=pl.ANY)
HALF_AXIS = (0, 0, 1, 0, 0, 0, 1)


def _me():
    return lax.axis_index("x"), lax.axis_index("y"), lax.axis_index("c")


def _half(ref, which, axis):
    n = ref.shape[axis] // 2
    sl = pl.ds(which * n, n)
    return ref.at[sl] if axis == 0 else ref.at[:, sl]


def _piece(t, ref, j):
    if t == 0:
        return ref.at[:, pl.ds(SLAB_START[j], SLAB)]
    if t == 1:
        return ref
    if t in (2, 6):
        return ref.at[pl.ds(512 * j, 512)]
    return ref.at[:, pl.ds(512 * j, 512)]


def _piece_shape(t, shape):
    if t == 0:
        return (shape[0], SLAB)
    if t == 1:
        return shape
    if t in (2, 6):
        return (512, shape[1])
    return (shape[0], 512)


def _gather_plan(ins, outs, own_slot_in_src):
    x, y, c = _me()
    k = 2 * x + y
    sib = (x, y, 1 - c)
    chips = [(1 - x, y), (x, 1 - y), (1 - x, 1 - y)]
    n = len(outs)

    def rows(t, which):
        h = outs[t].shape[1] // 2
        return pl.ds(which * h, h)

    def mine(t):
        return ins[t].at[k, rows(t, c)] if own_slot_in_src else ins[t].at[rows(t, c)]

    def first(t, j, sems):
        chip = chips[j]
        return pltpu.make_async_remote_copy(
            src_ref=mine(t), dst_ref=outs[t].at[k, rows(t, c)], send_sem=sems[0].at[t, j], recv_sem=sems[1].at[t, j],
            device_id=(chip[0], chip[1], c), device_id_type=MESH)

    def landed(t, j, sems):
        chip = chips[j]
        return pltpu.make_async_remote_copy(
            src_ref=mine(t), dst_ref=outs[t].at[2 * chip[0] + chip[1], rows(t, c)], send_sem=sems[0].at[t, j],
            recv_sem=sems[1].at[t, j], device_id=(chip[0], chip[1], c), device_id_type=MESH)

    def passed(t, j, which, sems):
        chip = chips[j]
        blk = outs[t].at[2 * chip[0] + chip[1], rows(t, which)]
        return pltpu.make_async_remote_copy(
            src_ref=blk, dst_ref=blk, send_sem=sems[2].at[t, j], recv_sem=sems[3].at[t, j], device_id=sib,
            device_id_type=MESH)

    def start(sems):
        for j in range(3):
            for t in range(n):
                first(t, j, sems).start()

    def finish(sems):
        for j in range(3):
            for t in range(n):
                landed(t, j, sems).wait_recv()
                passed(t, j, c, sems).start()
        for j in range(3):
            for t in range(n):
                passed(t, j, 1 - c, sems).wait_recv()
        for j in range(3):
            for t in range(n):
                first(t, j, sems).wait_send()
                passed(t, j, c, sems).wait_send()

    return k, start, finish


def _all_gather_slabs(slabs):
    def body(in_ref, out_ref, *sems):
        _, start, finish = _gather_plan([in_ref], [out_ref], True)
        start(sems)
        finish(sems)

    return pl.pallas_call(
        body, in_specs=[ANY], out_specs=ANY, out_shape=SDS(slabs.shape, slabs.dtype),
        scratch_shapes=[pltpu.SemaphoreType.DMA((1, 3))] * 4, input_output_aliases={0: 0},
        name="all_gather_slabs")(slabs)


def _gather_comm(parts):
    n = len(parts)

    def start(ins, outs, sems):
        k, go, _ = _gather_plan(ins, outs, False)
        for t in range(n):
            pltpu.make_async_copy(ins[t], outs[t].at[k], sems[4].at[t]).start()
        go(sems)

    def finish(ins, outs, sems):
        k, _, done = _gather_plan(ins, outs, False)
        done(sems)
        for t in range(n):
            pltpu.make_async_copy(ins[t], outs[t].at[k], sems[4].at[t]).wait()

    return _Comm(parts, [SDS((N_CHIPS,) + p.shape, p.dtype) for p in parts],
                 [pltpu.SemaphoreType.DMA((n, 3))] * 4 + [pltpu.SemaphoreType.DMA((n,))], start, finish)


def _exchange_comm(arrs, kinds, whole=()):
    n = len(arrs)

    def copies(ins, outs, sems):
        x, y, c = _me()
        return [pltpu.make_async_remote_copy(
            src_ref=ins[t] if t in whole else _half(ins[t], 1 - c, HALF_AXIS[kinds[t]]), dst_ref=outs[t],
            send_sem=sems[0].at[t], recv_sem=sems[1].at[t], device_id=(x, y, 1 - c), device_id_type=MESH)
            for t in range(n)]

    def start(ins, outs, sems):
        for cp in copies(ins, outs, sems):
            cp.start()

    def finish(ins, outs, sems):
        for cp in copies(ins, outs, sems):
            cp.wait()

    def hshape(t):
        s = list(arrs[t].shape)
        if t not in whole:
            s[HALF_AXIS[kinds[t]]] //= 2
        return SDS(tuple(s), arrs[t].dtype)

    return _Comm(arrs, [hshape(t) for t in range(n)], [pltpu.SemaphoreType.DMA((n,))] * 2, start, finish)


def _add_half(full, got, core, axis, *, name):
    r, c = got.shape
    br, bc = (256 if r % 256 == 0 else 128), min(2048, c)
    off_r = (r // br) if axis == 0 else 0
    off_c = (c // bc) if axis == 1 else 0

    def body(c_ref, a_ref, b_ref, o_ref):
        o_ref[...] = (a_ref[...] + b_ref[...]).astype(BF16)

    return pl.pallas_call(
        body, grid_spec=pltpu.PrefetchScalarGridSpec(
            num_scalar_prefetch=1, grid=(r // br, c // bc),
            in_specs=[pl.BlockSpec((br, bc), lambda i, j, cr: (i + cr[0] * off_r, j + cr[0] * off_c)),
                      pl.BlockSpec((br, bc), lambda i, j, cr: (i, j))],
            out_specs=pl.BlockSpec((br, bc), lambda i, j, cr: (i, j))),
        out_shape=SDS((r, c), BF16), name=name, compiler_params=_params(("parallel", "parallel")))(core, full, got)


def _add_pair(a, b, *, name):
    r, c = a.shape
    br, bc = 256, min(2048, c)

    def body(a_ref, b_ref, o_ref):
        o_ref[...] = (a_ref[...] + b_ref[...]).astype(BF16)

    spec = pl.BlockSpec((br, bc), lambda i, j: (i, j))
    return pl.pallas_call(body, grid=(r // br, c // bc), in_specs=[spec, spec], out_specs=spec,
                          out_shape=SDS((r, c), BF16), name=name, compiler_params=_params(("parallel", "parallel")))(a, b)


def _scatter_comm(halves, kinds):
    n = len(halves)

    def plan(ins, outs, sems):
        send, recv, lsem = sems
        x, y, c = _me()
        k = 2 * x + y

        def to_chip(t, j):
            return pltpu.make_async_remote_copy(
                src_ref=_piece(kinds[t], ins[t], j), dst_ref=outs[t].at[k], send_sem=send.at[t, j],
                recv_sem=recv.at[t, k], device_id=(j // 2, j % 2, c), device_id_type=MESH)

        def from_chip(t, j):
            return pltpu.make_async_remote_copy(
                src_ref=_piece(kinds[t], ins[t], j), dst_ref=outs[t].at[j], send_sem=send.at[t, j],
                recv_sem=recv.at[t, j], device_id=(j // 2, j % 2, c), device_id_type=MESH)

        def own(t, j):
            return pltpu.make_async_copy(_piece(kinds[t], ins[t], j), outs[t].at[j], lsem.at[t])

        return k, to_chip, from_chip, own

    def start(ins, outs, sems):
        k, to_chip, _, own = plan(ins, outs, sems)
        for j in range(N_CHIPS):
            @pl.when(k != j)
            def _(j=j):
                for t in range(n):
                    to_chip(t, j).start()

            @pl.when(k == j)
            def _(j=j):
                for t in range(n):
                    own(t, j).start()

    def finish(ins, outs, sems):
        k, to_chip, from_chip, own = plan(ins, outs, sems)
        for j in range(N_CHIPS):
            @pl.when(k != j)
            def _(j=j):
                for t in range(n):
                    from_chip(t, j).wait_recv()
                for t in range(n):
                    to_chip(t, j).wait_send()

            @pl.when(k == j)
            def _(j=j):
                for t in range(n):
                    own(t, j).wait()

    return _Comm(halves, [SDS((N_CHIPS,) + _piece_shape(kinds[t], halves[t].shape), halves[t].dtype) for t in range(n)],
                 [pltpu.SemaphoreType.DMA((n, N_CHIPS))] * 2 + [pltpu.SemaphoreType.DMA((n,))], start, finish)


def _sum4(p, *, name):
    _, r, c = p.shape
    br = 256 if r % 256 == 0 else 128

    def body(p_ref, o_ref):
        o_ref[...] = ((p_ref[0].astype(F32) + p_ref[1].astype(F32)) + p_ref[2].astype(F32)) + p_ref[3].astype(F32)

    return pl.pallas_call(
        body, grid=(r // br,), in_specs=[pl.BlockSpec((N_CHIPS, br, c), lambda i: (0, i, 0))],
        out_specs=_rowblk(br, c), out_shape=SDS((r, c), F32), name=name, compiler_params=_params(("parallel",)))(p)


def _swap_comm(sums):
    return _exchange_comm(sums, [None] * len(sums), whole=tuple(range(len(sums))))


def _adamw_halves(mine, theirs, core, w, m, v, *, axis, tr, name):
    rows, cols = w.shape

    if axis == 0:
        nbh = rows // 2 // tr
        g_spec = pl.BlockSpec((tr, cols), lambda i, cr: (i % nbh, 0))
    else:
        g_spec = pl.BlockSpec((tr, cols // 2), lambda i, cr: (i, 0))

    def body(c_ref, a_ref, b_ref, w_ref, m_ref, v_ref, g_ref, d_ref, nm_ref, nv_ref):
        a, b = a_ref[...], b_ref[...]
        if axis == 0:
            g = jnp.where(pl.program_id(0) // nbh == c_ref[0], a, b)
        else:
            low = c_ref[0] == 0
            g = jnp.concatenate([jnp.where(low, a, b), jnp.where(low, b, a)], axis=1)
        g_ref[...] = g
        d, nm, nv = _adamw_math(w_ref[...], g, m_ref[...], v_ref[...])
        d_ref[...] = d
        nm_ref[...] = nm
        nv_ref[...] = nv

    nat = pl.BlockSpec((tr, cols), lambda i, cr: (i, 0))
    return pl.pallas_call(
        body, grid_spec=pltpu.PrefetchScalarGridSpec(
            num_scalar_prefetch=1, grid=(rows // tr,), in_specs=[g_spec, g_spec, nat, nat, nat], out_specs=[nat] * 4),
        out_shape=[SDS((rows, cols), F32)] * 4, name=name, compiler_params=_params(("arbitrary",)))(
            core, mine, theirs, w, m, v)


SMALL_ROWS, SMALL_COLS = 8, 1024


def _pack_small(vs):
    flat = jnp.concatenate([v.reshape(-1) for v in vs])
    return jnp.pad(flat, (0, SMALL_ROWS * SMALL_COLS - flat.shape[0])).reshape(SMALL_ROWS, SMALL_COLS)


def _unpack_small(packed, sizes):
    flat = packed.reshape(-1)
    out, o = [], 0
    for n in sizes:
        out.append(flat[o:o + n].reshape(1, n))
        o += n
    return out


def _all_reduce_small(v):
    n_dev = 8

    def body(v_ref, o_ref, land, send, recv):
        x, y, c = _me()
        me = 4 * x + 2 * y + c
        land[me] = v_ref[...]
        cps = []
        for r in range(1, n_dev):
            fx, fy, fc = (r >> 2) & 1, (r >> 1) & 1, r & 1
            peer = (x ^ fx, y ^ fy, c ^ fc)
            cps.append(pltpu.make_async_remote_copy(
                src_ref=v_ref, dst_ref=land.at[me], send_sem=send.at[r - 1], recv_sem=recv.at[r - 1],
                device_id=peer, device_id_type=MESH))
        for cp in cps:
            cp.start()
        for r in range(1, n_dev):
            fx, fy, fc = (r >> 2) & 1, (r >> 1) & 1, r & 1
            src = 4 * (x ^ fx) + 2 * (y ^ fy) + (c ^ fc)
            pltpu.make_async_remote_copy(
                src_ref=v_ref, dst_ref=land.at[src], send_sem=send.at[r - 1], recv_sem=recv.at[r - 1],
                device_id=(x ^ fx, y ^ fy, c ^ fc), device_id_type=MESH).wait_recv()
        for cp in cps:
            cp.wait_send()
        acc = land[0]
        for r in range(1, n_dev):
            acc = acc + land[r]
        o_ref[...] = acc

    vm = pl.BlockSpec(memory_space=pltpu.VMEM)
    return pl.pallas_call(
        body, in_specs=[vm], out_specs=vm, out_shape=SDS(v.shape, F32),
        scratch_shapes=[pltpu.VMEM((n_dev,) + v.shape, F32), pltpu.SemaphoreType.DMA((n_dev - 1,)),
                        pltpu.SemaphoreType.DMA((n_dev - 1,))],
        name="all_reduce_small")(v)


def kernel(x, mem, norm_gain, mem_norm_gain, w_in, b_forget, q_gain_a, k_gain_a, sinks_a, q_gain_b, k_gain_b, q_gain_c, k_gain_c, w_mem_kv, w_branch_a, w_branch_b, w_branch_c, w_out, loss_target, m_norm_gain, m_mem_norm_gain, m_w_in, m_b_forget, m_q_gain_a, m_k_gain_a, m_sinks_a, m_q_gain_b, m_k_gain_b, m_q_gain_c, m_k_gain_c, m_w_mem_kv, m_w_branch_a, m_w_branch_b, m_w_branch_c, m_w_out, v_norm_gain, v_mem_norm_gain, v_w_in, v_b_forget, v_q_gain_a, v_k_gain_a, v_sinks_a, v_q_gain_b, v_k_gain_b, v_q_gain_c, v_k_gain_c, v_w_mem_kv, v_w_branch_a, v_w_branch_b, v_w_branch_c, v_w_out):
    xi, yi, ci = lax.axis_index("x"), lax.axis_index("y"), lax.axis_index("c")
    chip = jnp.reshape(2 * xi + yi, (1,)).astype(jnp.int32)
    core = jnp.reshape(ci, (1,)).astype(jnp.int32)

    slabs = _pack_w_in(chip, jnp.transpose(w_in[0]))
    mine = [w_mem_kv[0].astype(BF16), w_branch_a[0].astype(BF16), w_branch_b[0].astype(BF16),
            w_branch_c[0].astype(BF16), w_out[0].astype(BF16)]
    w_main, w_fb = _merge_slabs(_all_gather_slabs(slabs))

    r = _local_step(x[0], mem[0], loss_target[0], w_main, w_fb, mine, norm_gain, mem_norm_gain,
                    b_forget, q_gain_a, k_gain_a, sinks_a, q_gain_b, k_gain_b, q_gain_c, k_gain_c, core=core)
    sums, theirs = r["sums"], r["theirs"]

    small_names = ["d_gain", "d_mem_gain", "d_bf", "d_qga", "d_kga", "d_sinks", "d_qgb", "d_kgb", "d_qgc", "d_kgc"]
    loss_part = (0.5 / D_MODEL) * jnp.sum(r["sq"], axis=1, keepdims=True)
    packed = _pack_small([r[n] for n in small_names] + [loss_part])
    red = _all_reduce_small(packed)
    small_w = [norm_gain, mem_norm_gain, b_forget, q_gain_a, k_gain_a, sinks_a, q_gain_b, k_gain_b, q_gain_c, k_gain_c]
    small_m = [m_norm_gain, m_mem_norm_gain, m_b_forget, m_q_gain_a, m_k_gain_a, m_sinks_a, m_q_gain_b, m_k_gain_b,
               m_q_gain_c, m_k_gain_c]
    small_v = [v_norm_gain, v_mem_norm_gain, v_b_forget, v_q_gain_a, v_k_gain_a, v_sinks_a, v_q_gain_b, v_k_gain_b,
               v_q_gain_c, v_k_gain_c]
    sizes = [w.shape[1] for w in small_w]
    s_d, s_m, s_v = _adamw(red, _pack_small(small_w), _pack_small(small_m), _pack_small(small_v), tr=8, name="adamw_small")
    g_small = _unpack_small(red, sizes + [1])
    loss = g_small[-1].reshape(())
    d_small, m_small, v_small = _unpack_small(s_d, sizes), _unpack_small(s_m, sizes), _unpack_small(s_v, sizes)

    gw_in, dw_in, mw_in, vw_in = _adamw_w_in(jnp.concatenate([chip, core]), sums[0], theirs[0], sums[1], theirs[1],
                                             jnp.transpose(w_in[0]), jnp.transpose(m_w_in[0]), jnp.transpose(v_w_in[0]))
    big = {}
    for t, nm, w, m, v in ((2, "w_mem_kv", w_mem_kv, m_w_mem_kv, v_w_mem_kv),
                           (3, "w_branch_a", w_branch_a, m_w_branch_a, v_w_branch_a),
                           (4, "w_branch_b", w_branch_b, m_w_branch_b, v_w_branch_b),
                           (5, "w_branch_c", w_branch_c, m_w_branch_c, v_w_branch_c),
                           (6, "w_out", w_out, m_w_out, v_w_out)):
        big[nm] = _adamw_halves(sums[t], theirs[t], core, w[0], m[0], v[0], axis=HALF_AXIS[t], tr=128,
                                name="adamw_" + nm)

    def collect(kind):
        sm = (g_small, d_small, m_small, v_small)[kind]
        win = (gw_in, dw_in, mw_in, vw_in)[kind]
        return ([sm[0], sm[1], jnp.transpose(win)[None]] + [a for a in sm[2:10]]
                + [big[n][kind][None] for n in ("w_mem_kv", "w_branch_a", "w_branch_b", "w_branch_c", "w_out")])

    return (loss, r["grad_x"][None], *collect(0), *collect(1), *collect(2), *collect(3))
```

```python
import functools

import numpy as np
import jax
import jax.numpy as jnp
from jax import lax
from jax.experimental import pallas as pl
from jax.experimental.pallas import tpu as pltpu

F32 = jnp.float32
BF16 = jnp.bfloat16
HI = lax.Precision.HIGHEST
SDS = jax.ShapeDtypeStruct
MESH = pl.DeviceIdType.MESH

D_MODEL = 2048
HEAD_DIM = 64
A_HEADS = 12
A_GROUP = 3
B_HEADS = 12
C_HEADS = 4
C_HEAD_DIM = 128
WINDOW = 128
EPS = 1e-6
NEG = -1e30
LANE = 128

QA, KA, VA, ZA = 0, 768, 1024, 1280
QB, KB, VB, ZB = 2048, 2816, 3584, 4352
QC, ZC = 5120, 5632
GATE = 6144
P_MAIN = 12288
N_FORGET = 12
FORGET_COL = 5120
SHARD_COLS = 3075
SLAB = 3200
SLAB_START = (0, 3072, 6016, 9088)
SLAB_SHIFT = (0, 3, 122, 125)
N_CHIPS = 4

ADAM_LR = 0.001
ADAM_B1 = 0.9
ADAM_B2 = 0.999
ADAM_EPS = 1e-08
ADAM_WD = 0.01
ADAM_STEP = 10

VMEM_LIMIT = 56 * 1024 * 1024
VMEM_WIDE = 62 * 1024 * 1024


def _params(sem, vmem=VMEM_LIMIT):
    return pltpu.CompilerParams(dimension_semantics=sem, vmem_limit_bytes=vmem)


def _win(tr, width, off):
    return pl.BlockSpec((pl.Element(tr), pl.Element(width)), lambda i, *_: (i * tr, off))


def _rowblk(tr, width):
    return pl.BlockSpec((tr, width), lambda i, *_: (i, 0))


def _const(shape):
    nd = len(shape)
    return pl.BlockSpec(shape, lambda *_: (0,) * nd)


def _rms(x, g):
    return x * lax.rsqrt(jnp.mean(x * x, axis=-1, keepdims=True) + EPS) * g


def _head_mean_impl(x2, bd):
    hi = x2.astype(BF16)
    lo = (x2 - hi.astype(F32)).astype(BF16)
    return _dot(hi, bd) + _dot(lo, bd)


@jax.custom_vjp
def _head_mean(x2, bd):
    return _head_mean_impl(x2, bd)


_head_mean.defvjp(lambda x2, bd: (_head_mean_impl(x2, bd), bd),
                  lambda bd, g: (_head_mean_impl(g, bd), jnp.zeros_like(bd)))


def _head_norm(x, g_tiled, bd):
    return x * lax.rsqrt(_head_mean(x * x, bd) + EPS) * g_tiled


def _silu(z):
    return z * jax.nn.sigmoid(z)


def _dot_nt(a, b):
    return lax.dot_general(a, b, (((1,), (1,)), ((), ())), preferred_element_type=F32)


def _dot_tn(a, b):
    return lax.dot_general(a, b, (((0,), (0,)), ((), ())), preferred_element_type=F32)


def _dot(a, b):
    return jnp.dot(a, b, preferred_element_type=F32)


def _swa_fn(qk, vz, qkp, vzp, qg, kg, sinks, bd, bias, first):
    q = _head_norm(qk[:, :768], qg, bd)
    k2 = jnp.concatenate([qkp[:, 768:], qk[:, 768:]], axis=0)
    k2 = _head_norm(k2, kg, bd[:256, :256])
    v2 = jnp.concatenate([vzp[:, :256], vz[:, :256]], axis=0)
    z = vz[:, 256:]
    cols = A_GROUP * WINDOW
    kj = lax.broadcasted_iota(jnp.int32, (2 * WINDOW, cols), 0)
    no_prev = kj < WINDOW * first.astype(jnp.int32)
    qtb = jnp.transpose(q).astype(BF16)
    kb = k2.astype(BF16)
    vtb = jnp.transpose(v2).astype(BF16)
    outs = [None] * A_HEADS
    for g in range(A_HEADS // A_GROUP):
        heads = [A_GROUP * g + u for u in range(A_GROUP)]
        qs = jnp.concatenate([qtb[64 * h:64 * h + 64, :] for h in heads], axis=1)
        s = _dot(kb[:, 64 * g:64 * g + 64], qs) * (HEAD_DIM ** -0.5) + bias[g]
        s = jnp.where(no_prev, NEG, s)
        sink = jnp.concatenate([jnp.broadcast_to(sinks[:, h:h + 1], (1, WINDOW)) for h in heads], axis=1)
        m = lax.stop_gradient(jnp.maximum(jnp.max(s, axis=0, keepdims=True), sink))
        p = jnp.exp(s - m)
        den = jnp.sum(p, axis=0, keepdims=True) + jnp.exp(sink - m)
        o = _dot(vtb[64 * g:64 * g + 64, :], (p * (1.0 / den)).astype(BF16))
        for u, h in enumerate(heads):
            outs[h] = o[:, WINDOW * u:WINDOW * u + WINDOW]
    return jnp.transpose(jnp.concatenate(outs, axis=0)) * _silu(z)


def _swa_bias():
    qi = np.arange(WINDOW)[None, :]
    kj = np.arange(2 * WINDOW)[:, None]
    rel = qi + WINDOW - kj
    valid = (rel >= 0) & (rel < WINDOW)
    out = np.zeros((A_HEADS // A_GROUP, 2 * WINDOW, A_GROUP * WINDOW), np.float32)
    for h in range(A_HEADS):
        slope = np.float32(2.0 ** (-8.0 * (h + 1) / A_HEADS))
        blk = np.where(valid, -slope * rel.astype(np.float32), np.float32(NEG))
        g, u = divmod(h, A_GROUP)
        out[g, :, WINDOW * u:WINDOW * u + WINDOW] = blk
    return jnp.asarray(out)


def _mem_fn(qz, mkv, qg, kg, bd):
    q = _head_norm(qz[:, :512], qg, bd).astype(BF16)
    k = _head_norm(mkv[:, :512], kg, bd).astype(BF16)
    v = mkv[:, 512:].astype(BF16)
    z = qz[:, 512:]
    outs = []
    for h in range(C_HEADS):
        sl = slice(128 * h, 128 * h + 128)
        s = _dot_nt(q[:, sl], k[:, sl]) * (C_HEAD_DIM ** -0.5)
        m = lax.stop_gradient(jnp.max(s, axis=-1, keepdims=True))
        p = jnp.exp(s - m)
        den = jnp.sum(p, axis=-1, keepdims=True)
        outs.append(_dot((p * (1.0 / den)).astype(BF16), v[:, sl]))
    return jnp.concatenate(outs, axis=1) * _silu(z)


def _qn_fn(q, g, bd):
    return _head_norm(q, g, bd) * (HEAD_DIM ** -0.5)


def _kn_fn(k, g, bd):
    return _head_norm(k, g, bd)


def _block_diag(width, hd):
    i = np.arange(width) // hd
    return jnp.asarray((i[:, None] == i[None, :]).astype(np.float32) / hd, BF16)


def _head_sum(width, hd):
    i = np.arange(width) // hd
    return jnp.asarray((i[:, None] == np.arange(LANE)[None, :]).astype(np.float32))


def _rms_fwd(x, g, *, tr, name):
    rows, dm = x.shape

    def body(x_ref, g_ref, o_ref):
        o_ref[...] = _rms(x_ref[...], g_ref[...]).astype(BF16)

    return pl.pallas_call(
        body, grid=(rows // tr,),
        in_specs=[_rowblk(tr, dm), _const((1, dm))],
        out_specs=_rowblk(tr, dm),
        out_shape=SDS((rows, dm), BF16), name=name,
        compiler_params=_params(("parallel",)))(x, g)


def _rms_bwd(x, g, dy, resid, *, tr, name):
    rows, dm = x.shape
    want_dx = resid is not None

    def body(*refs):
        if want_dx:
            x_ref, g_ref, dy_ref, r_ref, dx_ref, dg_ref = refs
        else:
            x_ref, g_ref, dy_ref, dg_ref = refs
        _, vjp = jax.vjp(_rms, x_ref[...], g_ref[...])
        dx, dg = vjp(dy_ref[...])

        @pl.when(pl.program_id(0) == 0)
        def _():
            dg_ref[...] = jnp.zeros_like(dg_ref)

        dg_ref[...] += dg
        if want_dx:
            dx_ref[...] = r_ref[...] + dx

    ins = [x, g, dy] + ([resid] if want_dx else [])
    in_specs = [_rowblk(tr, dm), _const((1, dm)), _rowblk(tr, dm)] + ([_rowblk(tr, dm)] if want_dx else [])
    out_specs = ([_rowblk(tr, dm)] if want_dx else []) + [_const((1, dm))]
    out_shape = ([SDS((rows, dm), F32)] if want_dx else []) + [SDS((1, dm), F32)]
    return pl.pallas_call(
        body, grid=(rows // tr,), in_specs=in_specs, out_specs=out_specs, out_shape=out_shape, name=name,
        compiler_params=_params(("arbitrary",)))(*ins)


class _Comm:
    def __init__(self, ins, out_shapes, sems, start, finish, alias=()):
        self.ins, self.out_shapes, self.sems, self.start, self.finish = list(ins), list(out_shapes), list(sems), start, finish
        self.alias = tuple(alias)


def _run_comm(comm, name):
    n_in, n_out = len(comm.ins), len(comm.out_shapes)

    def body(*refs):
        ins, outs, sems = refs[:n_in], refs[n_in:n_in + n_out], refs[n_in + n_out:]
        comm.start(ins, outs, sems)
        comm.finish(ins, outs, sems)

    hbm = pl.BlockSpec(memory_space=pl.ANY)
    return pl.pallas_call(body, in_specs=[hbm] * n_in, out_specs=[hbm] * n_out, out_shape=comm.out_shapes,
                          scratch_shapes=comm.sems, name=name)(*comm.ins)


def _matmul(a, b, *, dims, out_dtype, tm, tn, tk, name, add=None, comms=(), vmem=VMEM_LIMIT, k_window=None):
    a_list = list(a) if isinstance(a, (list, tuple)) else [a]
    b_list = list(b) if isinstance(b, (list, tuple)) else [b]
    assert len(a_list) == 1 or dims == "nt"
    assert len(b_list) == 1 or dims == "tn"
    if dims == "tn":
        kdim, m = a_list[0].shape
    else:
        m, kdim = a_list[0].shape[0], sum(p.shape[1] for p in a_list)
    n = b_list[0].shape[0] if dims == "nt" else sum(p.shape[1] for p in b_list)
    tm, tn, tk = min(tm, m), min(tn, n), min(tk, kdim)
    assert m % tm == 0 and n % tn == 0 and kdim % tk == 0, (name, m, n, kdim)
    ni, nj, nk = m // tm, n // tn, kdim // tk
    kw0 = 0
    if k_window is not None:
        assert dims == "nn" and len(a_list) == 1 and len(b_list) == 1
        kw0, nk = k_window
    a_rng, b_rng, pos = [], [], 0
    for p in a_list:
        assert len(a_list) == 1 or p.shape[1] % tk == 0
        a_rng.append((pos, p.shape[1] // tk if len(a_list) > 1 else nk))
        pos += a_rng[-1][1]
    pos = 0
    for p in b_list:
        assert len(b_list) == 1 or p.shape[1] % tn == 0
        b_rng.append((pos, p.shape[1] // tn if len(b_list) > 1 else nj))
        pos += b_rng[-1][1]
    has_add = add is not None
    n_mm_in = len(a_list) + len(b_list) + (1 if has_add else 0)
    c_in = [len(c.ins) for c in comms]
    c_out = [len(c.out_shapes) for c in comms]
    c_sem = [len(c.sems) for c in comms]

    def body(*refs):
        a_refs, b_refs = refs[:len(a_list)], refs[len(a_list):len(a_list) + len(b_list)]
        add_ref = refs[n_mm_in - 1] if has_add else None
        pos = n_mm_in
        cin = []
        for cnt in c_in:
            cin.append(refs[pos:pos + cnt])
            pos += cnt
        o_ref = refs[pos]
        pos += 1
        cout = []
        for cnt in c_out:
            cout.append(refs[pos:pos + cnt])
            pos += cnt
        acc = refs[pos]
        pos += 1
        csem = []
        for cnt in c_sem:
            csem.append(refs[pos:pos + cnt])
            pos += cnt
        i, j, k = pl.program_id(0), pl.program_id(1), pl.program_id(2)

        if comms:
            @pl.when((i == 0) & (j == 0) & (k == 0))
            def _():
                for c, ci, co, cs in zip(comms, cin, cout, csem):
                    c.start(ci, co, cs)

        def accumulate(a_ref, b_ref, first_k, later_k):
            if dims == "nn":
                part = _dot(a_ref[...], b_ref[...])
            elif dims == "nt":
                part = _dot_nt(a_ref[...], b_ref[...])
            else:
                part = _dot_tn(a_ref[...], b_ref[...])

            if first_k:
                @pl.when(k == 0)
                def _():
                    acc[...] = part + add_ref[...] if has_add else part

            if later_k:
                @pl.when(k > 0)
                def _():
                    acc[...] += part

        if len(a_list) > 1:
            for a_ref, (k0, cnt) in zip(a_refs, a_rng):
                @pl.when((k >= k0) & (k < k0 + cnt))
                def _(a_ref=a_ref, k0=k0, cnt=cnt):
                    accumulate(a_ref, b_refs[0], k0 == 0, k0 + cnt > 1)
        elif len(b_list) > 1:
            for b_ref, (j0, cnt) in zip(b_refs, b_rng):
                @pl.when((j >= j0) & (j < j0 + cnt))
                def _(b_ref=b_ref):
                    accumulate(a_refs[0], b_ref, True, nk > 1)
        else:
            accumulate(a_refs[0], b_refs[0], True, nk > 1)

        @pl.when(k == nk - 1)
        def _():
            o_ref[...] = acc[...].astype(out_dtype)

        if comms:
            @pl.when((i == ni - 1) & (j == nj - 1) & (k == nk - 1))
            def _():
                for c, ci, co, cs in zip(comms, cin, cout, csem):
                    c.finish(ci, co, cs)

    def a_spec(k0, cnt):
        if dims == "tn":
            return pl.BlockSpec((tk, tm), lambda i, j, k: (k, i))
        if k_window is not None:
            return pl.BlockSpec((tm, tk), lambda i, j, k: (i, kw0 + k))
        return pl.BlockSpec((tm, tk), lambda i, j, k: (i, jnp.clip(k - k0, 0, cnt - 1)))

    def b_spec(j0, cnt):
        if dims == "nt":
            return pl.BlockSpec((tn, tk), lambda i, j, k: (j, k))
        if k_window is not None:
            return pl.BlockSpec((tk, tn), lambda i, j, k: (kw0 + k, j))
        return pl.BlockSpec((tk, tn), lambda i, j, k: (k, jnp.clip(j - j0, 0, cnt - 1)))

    o_spec = pl.BlockSpec((tm, tn), lambda i, j, k: (i, j))
    hbm = pl.BlockSpec(memory_space=pl.ANY)
    ins = a_list + b_list + ([add] if has_add else []) + [x for c in comms for x in c.ins]
    in_specs = ([a_spec(*r) for r in a_rng] + [b_spec(*r) for r in b_rng] + ([o_spec] if has_add else [])
                + [hbm] * sum(c_in))
    out_specs = [o_spec] + [hbm] * sum(c_out)
    out_shape = [SDS((m, n), out_dtype)] + [s for c in comms for s in c.out_shapes]
    scratch = [pltpu.VMEM((tm, tn), F32)] + [s for c in comms for s in c.sems]
    sem = ("arbitrary",) * 3 if comms else ("parallel", "parallel", "arbitrary")
    aliases, in_pos, out_pos = {}, n_mm_in, 1
    for c, ci, co in zip(comms, c_in, c_out):
        for src, dst in c.alias:
            aliases[in_pos + src] = out_pos + dst
        in_pos, out_pos = in_pos + ci, out_pos + co
    res = pl.pallas_call(
        body, grid=(ni, nj, nk), in_specs=in_specs, out_specs=out_specs, out_shape=out_shape, scratch_shapes=scratch,
        input_output_aliases=aliases, name=name, compiler_params=_params(sem, vmem))(*ins)
    if not comms:
        return res[0]
    outs, pos = [], 1
    for cnt in c_out:
        outs.append(list(res[pos:pos + cnt]))
        pos += cnt
    return res[0], outs


def _swa_specs(nb):
    prev = lambda off: pl.BlockSpec((pl.Element(WINDOW), pl.Element(1024)),
                                    lambda n: (jnp.maximum(n - 1, 0) * WINDOW, off))
    return [_win(WINDOW, 1024, QA), _win(WINDOW, 1024, VA), prev(QA), prev(VA),
            _const((1, 768)), _const((1, 256)), _const((1, LANE)), _const((768, 768)),
            _const((A_HEADS // A_GROUP, 2 * WINDOW, A_GROUP * WINDOW))]


def _swa_fwd(proj, qg, kg, sinks, bd, bias):
    s = proj.shape[0]
    nb = s // WINDOW

    def body(qk_ref, vz_ref, qkp_ref, vzp_ref, qg_ref, kg_ref, sk_ref, bd_ref, bias_ref, o_ref):
        first = pl.program_id(0) == 0
        o_ref[...] = _swa_fn(qk_ref[...], vz_ref[...], qkp_ref[...], vzp_ref[...], qg_ref[...], kg_ref[...],
                             sk_ref[...], bd_ref[...], bias_ref[...], first).astype(BF16)

    return pl.pallas_call(
        body, grid=(nb,), in_specs=_swa_specs(nb), out_specs=_rowblk(WINDOW, 768),
        out_shape=SDS((s, 768), BF16), name="swa_fwd",
        compiler_params=_params(("parallel",)))(proj, proj, proj, proj, qg, kg, sinks, bd, bias)


def _swa_bwd(proj, qg, kg, sinks, bd, bias, dga):
    s = proj.shape[0]
    nb = s // WINDOW

    def body(qk_ref, vz_ref, qkp_ref, vzp_ref, qg_ref, kg_ref, sk_ref, bd_ref, bias_ref, dg_ref,
             dcur_ref, dprev_ref, dqg_ref, dkg_ref, dsk_ref):
        first = pl.program_id(0) == 0
        bd_v = bd_ref[...]
        bias_v = bias_ref[...]
        fn = lambda qk, vz, qkp, vzp, qg_, kg_, sk: _swa_fn(qk, vz, qkp, vzp, qg_, kg_, sk, bd_v, bias_v, first)
        _, vjp = jax.vjp(fn, qk_ref[...], vz_ref[...], qkp_ref[...], vzp_ref[...], qg_ref[...], kg_ref[...], sk_ref[...])
        dqk, dvz, dqkp, dvzp, dqg, dkg, dsk = vjp(dg_ref[...])

        @pl.when(first)
        def _():
            dqg_ref[...] = jnp.zeros_like(dqg_ref)
            dkg_ref[...] = jnp.zeros_like(dkg_ref)
            dsk_ref[...] = jnp.zeros_like(dsk_ref)

        dqg_ref[...] += dqg
        dkg_ref[...] += dkg
        dsk_ref[...] += dsk
        dcur_ref[...] = jnp.concatenate([dqk, dvz], axis=1)
        dprev_ref[...] = jnp.concatenate([dqkp[:, 768:], dvzp[:, :256]], axis=1)

    return pl.pallas_call(
        body, grid=(nb,), in_specs=_swa_specs(nb) + [_rowblk(WINDOW, 768)],
        out_specs=[_rowblk(WINDOW, 2048), pl.BlockSpec((None, WINDOW, 512), lambda n: (n, 0, 0)),
                   _const((1, 768)), _const((1, 256)), _const((1, LANE))],
        out_shape=[SDS((s, 2048), F32), SDS((nb, WINDOW, 512), F32), SDS((1, 768), F32), SDS((1, 256), F32),
                   SDS((1, LANE), F32)],
        name="swa_bwd", compiler_params=_params(("arbitrary",)))(proj, proj, proj, proj, qg, kg, sinks, bd, bias, dga)


def _swa_combine(dcur, dprev):
    s = dcur.shape[0]
    nb = s // WINDOW

    def body(c_ref, p_ref, o_ref):
        c = c_ref[...]
        nxt = jnp.where(pl.program_id(0) == nb - 1, 0.0, p_ref[...])
        o_ref[...] = jnp.concatenate([c[:, :768], c[:, 768:1280] + nxt, c[:, 1280:]], axis=1).astype(BF16)

    return pl.pallas_call(
        body, grid=(nb,),
        in_specs=[_rowblk(WINDOW, 2048), pl.BlockSpec((None, WINDOW, 512), lambda n: (jnp.minimum(n + 1, nb - 1), 0, 0))],
        out_specs=_rowblk(WINDOW, 2048), out_shape=SDS((s, 2048), BF16), name="swa_combine",
        compiler_params=_params(("parallel",)))(dcur, dprev)


def _mem_fwd(proj, mkv, qg, kg, bd, *, tr):
    s = proj.shape[0]

    def body(qz_ref, mkv_ref, qg_ref, kg_ref, bd_ref, o_ref):
        o_ref[...] = _mem_fn(qz_ref[...], mkv_ref[...], qg_ref[...], kg_ref[...], bd_ref[...]).astype(BF16)

    return pl.pallas_call(
        body, grid=(s // tr,),
        in_specs=[_win(tr, 1024, QC), _const(mkv.shape), _const((1, 512)), _const((1, 512)), _const((512, 512))],
        out_specs=_rowblk(tr, 512), out_shape=SDS((s, 512), BF16), name="mem_fwd",
        compiler_params=_params(("parallel",)))(proj, mkv, qg, kg, bd)


def _mem_bwd(proj, mkv, qg, kg, bd, dgc, *, tr):
    s = proj.shape[0]

    def body(qz_ref, mkv_ref, qg_ref, kg_ref, bd_ref, dg_ref, dqz_ref, dmkv_ref, dqg_ref, dkg_ref):
        bd_v = bd_ref[...]
        fn = lambda qz, mkv_, qg_, kg_: _mem_fn(qz, mkv_, qg_, kg_, bd_v)
        _, vjp = jax.vjp(fn, qz_ref[...], mkv_ref[...], qg_ref[...], kg_ref[...])
        dqz, dmkv, dqg, dkg = vjp(dg_ref[...])

        @pl.when(pl.program_id(0) == 0)
        def _():
            dmkv_ref[...] = jnp.zeros_like(dmkv_ref)
            dqg_ref[...] = jnp.zeros_like(dqg_ref)
            dkg_ref[...] = jnp.zeros_like(dkg_ref)

        dmkv_ref[...] += dmkv
        dqg_ref[...] += dqg
        dkg_ref[...] += dkg
        dqz_ref[...] = dqz.astype(BF16)

    return pl.pallas_call(
        body, grid=(s // tr,),
        in_specs=[_win(tr, 1024, QC), _const(mkv.shape), _const((1, 512)), _const((1, 512)), _const((512, 512)),
                  _rowblk(tr, 512)],
        out_specs=[_rowblk(tr, 1024), _const(mkv.shape), _const((1, 512)), _const((1, 512))],
        out_shape=[SDS((s, 1024), BF16), SDS(mkv.shape, F32), SDS((1, 512), F32), SDS((1, 512), F32)],
        name="mem_bwd", compiler_params=_params(("arbitrary",)))(proj, mkv, qg, kg, bd, dgc)


def _log_sigmoid(x):
    return jnp.minimum(x, 0.0) - jnp.log1p(jnp.exp(-jnp.abs(x)))


FOX_TQ, FOX_TK = 512, 512
FOX_FWD_TQ, FOX_FWD_TK = 512, 1024


def _fox_tiles(s):
    return min(FOX_TQ, s), min(FOX_TK, s)


AUG = 128 * B_HEADS
COL_A, COL_B = 64, 67


def _split3(c):
    hi = c.astype(BF16)
    r1 = c - hi.astype(F32)
    mid = r1.astype(BF16)
    lo = (r1 - mid.astype(F32)).astype(BF16)
    return hi, mid, lo


def _expand_mats():
    def mat(col0):
        e = np.zeros((768 + 3 * LANE, AUG), np.float32)
        for h in range(B_HEADS):
            for d in range(HEAD_DIM):
                e[64 * h + d, 128 * h + d] = 1.0
            for part in range(3):
                e[768 + LANE * part + h, 128 * h + col0 + part] = 1.0
        return e

    def ones(col0):
        o = np.zeros((1, AUG), np.float32)
        for h in range(B_HEADS):
            o[0, 128 * h + col0:128 * h + col0 + 3] = 1.0
        return o

    return (jnp.asarray(mat(COL_A), BF16), jnp.asarray(mat(COL_B), BF16), jnp.asarray(ones(COL_A)), jnp.asarray(ones(COL_B)))


def _augment(data_bf16, triple, emat, ones_row):
    parts = [data_bf16] + (list(triple) if triple is not None else [jnp.zeros((data_bf16.shape[0], LANE), BF16)] * 3)
    wide = _dot(jnp.concatenate(parts, axis=1), emat)
    if ones_row is not None:
        wide = wide + ones_row
    return wide


def _compact(wide):
    return jnp.concatenate([wide[:, 128 * h:128 * h + 64] for h in range(wide.shape[1] // 128)], axis=1)


def _lane_of_heads(wide, col, first=0):
    rows = wide.shape[0]
    lane = lax.broadcasted_iota(jnp.int32, (rows, LANE), 1)
    out = jnp.zeros((rows, LANE), F32)
    for h in range(wide.shape[1] // 128):
        out = jnp.where(lane == first + h, wide[:, 128 * h + col:128 * h + col + 1], out)
    return out


def _fox2_prep(proj, fbl, qg, kg, bfor, bd, ea, eb, ones_a, ones_b, *, tr):
    s = proj.shape[0]
    tri = jnp.asarray(np.tril(np.ones((tr, tr), np.float32)))

    def body(q_ref, k_ref, v_ref, fb_ref, qg_ref, kg_ref, bf_ref, bd_ref, tri_ref, ea_ref, eb_ref, oa_ref, ob_ref,
             qat_ref, ka_ref, kat_ref, va_ref, vat_ref, qn_ref, c_ref, carry):
        @pl.when(pl.program_id(0) == 0)
        def _():
            carry[...] = jnp.zeros_like(carry)

        bd_v = bd_ref[...]
        lane = lax.broadcasted_iota(jnp.int32, (tr, LANE), 1)
        logf = jnp.where(lane < N_FORGET, _log_sigmoid(fb_ref[...] + bf_ref[...]), 0.0)
        c = jnp.dot(tri_ref[...], logf, precision=HI, preferred_element_type=F32) + carry[...]
        c_ref[...] = c
        carry[...] = c[tr - 1:tr, :]
        qn = _qn_fn(q_ref[...], qg_ref[...], bd_v).astype(BF16)
        kn = _kn_fn(k_ref[...], kg_ref[...], bd_v).astype(BF16)
        qn_ref[...] = qn
        qat_ref[...] = jnp.transpose(_augment(qn, _split3(c), ea_ref[...], ob_ref[...])).astype(BF16)
        ka = _augment(kn, _split3(-c), eb_ref[...], oa_ref[...])
        ka_ref[...] = ka.astype(BF16)
        kat_ref[...] = jnp.transpose(ka).astype(BF16)
        va = _augment(v_ref[...].astype(BF16), None, ea_ref[...], oa_ref[...])
        va_ref[...] = va.astype(BF16)
        vat_ref[...] = jnp.transpose(va).astype(BF16)

    emat = _const((768 + 3 * LANE, AUG))
    return pl.pallas_call(
        body, grid=(s // tr,),
        in_specs=[_win(tr, 768, QB), _win(tr, 768, KB), _win(tr, 768, VB), _rowblk(tr, LANE), _const((1, 768)),
                  _const((1, 768)), _const((1, LANE)), _const((768, 768)), _const((tr, tr)), emat, emat,
                  _const((1, AUG)), _const((1, AUG))],
        out_specs=[pl.BlockSpec((AUG, tr), lambda i: (0, i)), _rowblk(tr, AUG), pl.BlockSpec((AUG, tr), lambda i: (0, i)),
                   _rowblk(tr, AUG), pl.BlockSpec((AUG, tr), lambda i: (0, i)), _rowblk(tr, 768), _rowblk(tr, LANE)],
        out_shape=[SDS((AUG, s), BF16), SDS((s, AUG), BF16), SDS((AUG, s), BF16), SDS((s, AUG), BF16),
                   SDS((AUG, s), BF16), SDS((s, 768), BF16), SDS((s, LANE), F32)],
        scratch_shapes=[pltpu.VMEM((1, LANE), F32)], name="fox_prep",
        compiler_params=_params(("arbitrary",)))(proj, proj, proj, fbl, qg, kg, bfor, bd, tri, ea, eb, ones_a, ones_b)


def _fox2_fwd(proj, qat, ka, vat):
    s = proj.shape[0]
    tq, tk = min(FOX_FWD_TQ, s), min(FOX_FWD_TK, s)
    nq, nk = s // tq, s // tk

    def last_k(i):
        return (i * tq + tq - 1) // tk

    def body(qt_ref, k_ref, vt_ref, z_ref, gb_ref, yb_ref, lse_ref, acc, m_s):
        i, j = pl.program_id(0), pl.program_id(1)

        @pl.when(j == 0)
        def _():
            acc[...] = jnp.zeros_like(acc)
            m_s[...] = jnp.full_like(m_s, NEG)

        def tile(masked):
            if masked:
                kpos = j * tk + lax.broadcasted_iota(jnp.int32, (tk, tq), 0)
                qpos = i * tq + lax.broadcasted_iota(jnp.int32, (tk, tq), 1)
                mask = kpos <= qpos
            for h in range(B_HEADS):
                sl = slice(128 * h, 128 * h + 128)
                sc = _dot(k_ref[:, sl], qt_ref[sl, :])
                if masked:
                    sc = jnp.where(mask, sc, NEG)
                m_prev = m_s[h:h + 1, :]
                m_new = jnp.maximum(m_prev, jnp.max(sc, axis=0, keepdims=True))
                p = jnp.exp(sc - m_new).astype(BF16)
                acc[sl, :] = jnp.exp(m_prev - m_new) * acc[sl, :] + _dot(vt_ref[sl, :], p)
                m_s[h:h + 1, :] = m_new

        full = j * tk + tk - 1 <= i * tq

        @pl.when(full)
        def _():
            tile(False)

        @pl.when(jnp.logical_and(jnp.logical_not(full), j <= last_k(i)))
        def _():
            tile(True)

        @pl.when(j == nk - 1)
        def _():
            outs = []
            row = lax.broadcasted_iota(jnp.int32, (LANE, tq), 0)
            lse_t = jnp.zeros((LANE, tq), F32)
            for h in range(B_HEADS):
                l_row = acc[128 * h + COL_A:128 * h + COL_A + 1, :]
                outs.append(acc[128 * h:128 * h + 64, :] * (1.0 / l_row))
                lse_t = jnp.where(row == h, m_s[h:h + 1, :] + jnp.log(l_row), lse_t)
            y = jnp.transpose(jnp.concatenate(outs, axis=0))
            yb_ref[...] = y
            gb_ref[...] = (y * _silu(z_ref[...])).astype(BF16)
            lse_ref[...] = jnp.transpose(lse_t)

    kcol = lambda i, j: (0, jnp.minimum(j, last_k(i)))
    return pl.pallas_call(
        body, grid=(nq, nk),
        in_specs=[pl.BlockSpec((AUG, tq), lambda i, j: (0, i)),
                  pl.BlockSpec((tk, AUG), lambda i, j: (jnp.minimum(j, last_k(i)), 0)),
                  pl.BlockSpec((AUG, tk), kcol),
                  pl.BlockSpec((pl.Element(tq), pl.Element(768)), lambda i, j: (i * tq, ZB))],
        out_specs=[pl.BlockSpec((tq, 768), lambda i, j: (i, 0)), pl.BlockSpec((tq, 768), lambda i, j: (i, 0)),
                   pl.BlockSpec((tq, LANE), lambda i, j: (i, 0))],
        out_shape=[SDS((s, 768), BF16), SDS((s, 768), F32), SDS((s, LANE), F32)],
        scratch_shapes=[pltpu.VMEM((AUG, tq), F32), pltpu.VMEM((16, tq), F32)],
        name="fox_fwd", compiler_params=_params(("parallel", "arbitrary")))(qat, ka, vat, proj)


def _fox2_bwd_pre(proj, yb, dgb, qn, c, lse, hsum, ea, ones_b, *, tr):
    s = proj.shape[0]

    def body(z_ref, y_ref, dg_ref, qn_ref, c_ref, lse_ref, hs_ref, ea_ref, ob_ref,
             qa_ref, qat_ref, dya_ref, dyat_ref, dz_ref):
        z, y, dg = z_ref[...], y_ref[...], dg_ref[...]
        sg = jax.nn.sigmoid(z)
        dy = dg * (z * sg)
        dz_ref[...] = (dg * y * (sg * (1.0 + z * (1.0 - sg)))).astype(BF16)
        delta = jnp.dot(dy * y, hs_ref[...], precision=HI, preferred_element_type=F32)
        e = ea_ref[...]
        dya = _augment(dy.astype(BF16), _split3(-delta), e, None)
        dya_ref[...] = dya.astype(BF16)
        dyat_ref[...] = jnp.transpose(dya).astype(BF16)
        qa = _augment(qn_ref[...], _split3(c_ref[...] - lse_ref[...]), e, ob_ref[...])
        qa_ref[...] = qa.astype(BF16)
        qat_ref[...] = jnp.transpose(qa).astype(BF16)

    return pl.pallas_call(
        body, grid=(s // tr,),
        in_specs=[_win(tr, 768, ZB), _rowblk(tr, 768), _rowblk(tr, 768), _rowblk(tr, 768), _rowblk(tr, LANE),
                  _rowblk(tr, LANE), _const((768, LANE)), _const((768 + 3 * LANE, AUG)), _const((1, AUG))],
        out_specs=[_rowblk(tr, AUG), pl.BlockSpec((AUG, tr), lambda i: (0, i)), _rowblk(tr, AUG),
                   pl.BlockSpec((AUG, tr), lambda i: (0, i)), _rowblk(tr, 768)],
        out_shape=[SDS((s, AUG), BF16), SDS((AUG, s), BF16), SDS((s, AUG), BF16), SDS((AUG, s), BF16),
                   SDS((s, 768), BF16)], name="fox_bwd_pre",
        compiler_params=_params(("parallel",)))(proj, yb, dgb, qn, c, lse, hsum, ea, ones_b)


def _fox2_bwd(qb, qbt, ka, kat, va, dya, dyat):
    s = qb.shape[0]
    tq, tk = _fox_tiles(s)
    nq, nk = s // tq, s // tk
    ng = 2
    gh = B_HEADS // ng
    gw = 128 * gh

    def first_q(j):
        return (j * tk) // tq

    def body(q_ref, qt_ref, k_ref, kt_ref, v_ref, dy_ref, dyt_ref, dq_hbm, dk_ref, dv_ref, dck_ref,
             dq_acc, dk_acc, dv_acc, sem):
        g, j, i = pl.program_id(0), pl.program_id(1), pl.program_id(2)

        @pl.when((j == 0) & (i == 0))
        def _():
            dq_acc[...] = jnp.zeros_like(dq_acc)

        @pl.when(i == 0)
        def _():
            dk_acc[...] = jnp.zeros_like(dk_acc)
            dv_acc[...] = jnp.zeros_like(dv_acc)

        def tile(masked):
            if masked:
                kpos = j * tk + lax.broadcasted_iota(jnp.int32, (tk, tq), 0)
                qpos = i * tq + lax.broadcasted_iota(jnp.int32, (tk, tq), 1)
                mask = kpos <= qpos
            cols = pl.ds(pl.multiple_of(i * tq, tq), tq)
            for h in range(gh):
                sl = slice(128 * h, 128 * h + 128)
                sc = _dot(k_ref[:, sl], qt_ref[sl, :])
                if masked:
                    sc = jnp.where(mask, sc, NEG)
                p = jnp.exp(sc)
                ds = (p * _dot(v_ref[:, sl], dyt_ref[sl, :])).astype(BF16)
                dv_acc[:, sl] += _dot(p.astype(BF16), dy_ref[:, sl])
                dk_acc[:, sl] += _dot(ds, q_ref[:, sl])
                dq_acc[sl, cols] += _dot(kt_ref[sl, :], ds)

        full = j * tk + tk - 1 <= i * tq

        @pl.when(full)
        def _():
            tile(False)

        @pl.when(jnp.logical_and(jnp.logical_not(full), i >= first_q(j)))
        def _():
            tile(True)

        @pl.when(i == nq - 1)
        def _():
            dkw = dk_acc[...]
            dk_ref[...] = _compact(dkw)
            dv_ref[...] = _compact(dv_acc[...]).astype(BF16)
            dck_ref[...] = -_lane_of_heads(dkw, COL_B, gh * g)

        @pl.when((j == nk - 1) & (i == nq - 1))
        def _():
            cp = pltpu.make_async_copy(dq_acc, dq_hbm.at[pl.ds(pl.multiple_of(g * gw, gw), gw)], sem)
            cp.start()
            cp.wait()

    qrow = pl.BlockSpec((tq, gw), lambda g, j, i: (jnp.maximum(i, first_q(j)), g))
    qcol = pl.BlockSpec((gw, tq), lambda g, j, i: (g, jnp.maximum(i, first_q(j))))
    krow = pl.BlockSpec((tk, gw), lambda g, j, i: (j, g))
    kcol = pl.BlockSpec((gw, tk), lambda g, j, i: (g, j))
    kout = pl.BlockSpec((tk, gw // 2), lambda g, j, i: (j, g))
    return pl.pallas_call(
        body, grid=(ng, nk, nq),
        in_specs=[qrow, qcol, krow, kcol, krow, qrow, qcol],
        out_specs=[pl.BlockSpec(memory_space=pl.ANY), kout, kout,
                   pl.BlockSpec((None, tk, LANE), lambda g, j, i: (g, j, 0))],
        out_shape=[SDS((AUG, s), F32), SDS((s, 768), F32), SDS((s, 768), BF16), SDS((ng, s, LANE), F32)],
        scratch_shapes=[pltpu.VMEM((gw, s), F32), pltpu.VMEM((tk, gw), F32), pltpu.VMEM((tk, gw), F32),
                        pltpu.SemaphoreType.DMA],
        name="fox_bwd", compiler_params=_params(("arbitrary",) * 3))(qb, qbt, ka, kat, va, dya, dyat)


def _fox2_bwd_post(proj, fbl, qg, kg, bfor, bd, dqa, dkn, dck, *, tr):
    s = proj.shape[0]
    nb = s // tr
    triu = jnp.asarray(np.triu(np.ones((tr, tr), np.float32)))
    rev = lambda i: nb - 1 - i

    def body(q_ref, k_ref, fb_ref, qg_ref, kg_ref, bf_ref, bd_ref, tri_ref, dqa_ref, dkn_ref, dck_ref,
             dq_ref, dk_ref, dfb_ref, dqg_ref, dkg_ref, dbf_ref, carry):
        @pl.when(pl.program_id(0) == 0)
        def _():
            carry[...] = jnp.zeros_like(carry)
            dqg_ref[...] = jnp.zeros_like(dqg_ref)
            dkg_ref[...] = jnp.zeros_like(dkg_ref)
            dbf_ref[...] = jnp.zeros_like(dbf_ref)

        bd_v = bd_ref[...]
        dqw = jnp.transpose(dqa_ref[...])
        _, vjp_q = jax.vjp(lambda q, g: _qn_fn(q, g, bd_v), q_ref[...], qg_ref[...])
        dq, dqg = vjp_q(_compact(dqw))
        _, vjp_k = jax.vjp(lambda k, g: _kn_fn(k, g, bd_v), k_ref[...], kg_ref[...])
        dk, dkg = vjp_k(dkn_ref[...])
        dq_ref[...] = dq.astype(BF16)
        dk_ref[...] = dk.astype(BF16)
        dqg_ref[...] += dqg
        dkg_ref[...] += dkg

        dc = _lane_of_heads(dqw, COL_A) + (dck_ref[0] + dck_ref[1])
        dlogf = jnp.dot(tri_ref[...], dc, precision=HI, preferred_element_type=F32) + carry[...]
        carry[...] = dlogf[0:1, :]
        lane = lax.broadcasted_iota(jnp.int32, (tr, LANE), 1)
        xf = fb_ref[...] + bf_ref[...]
        dfb = jnp.where(lane < N_FORGET, dlogf * jax.nn.sigmoid(-xf), 0.0)
        dfb_ref[...] = dfb.astype(BF16)
        dbf_ref[...] += jnp.sum(dfb, axis=0, keepdims=True)

    rb = lambda w: pl.BlockSpec((tr, w), lambda i: (rev(i), 0))
    wn = lambda w, off: pl.BlockSpec((pl.Element(tr), pl.Element(w)), lambda i: (rev(i) * tr, off))
    return pl.pallas_call(
        body, grid=(nb,),
        in_specs=[wn(768, QB), wn(768, KB), rb(LANE), _const((1, 768)), _const((1, 768)), _const((1, LANE)),
                  _const((768, 768)), _const((tr, tr)), pl.BlockSpec((AUG, tr), lambda i: (0, rev(i))), rb(768),
                  pl.BlockSpec((2, tr, LANE), lambda i: (0, rev(i), 0))],
        out_specs=[rb(768), rb(768), rb(LANE), _const((1, 768)), _const((1, 768)), _const((1, LANE))],
        out_shape=[SDS((s, 768), BF16), SDS((s, 768), BF16), SDS((s, LANE), BF16), SDS((1, 768), F32),
                   SDS((1, 768), F32), SDS((1, LANE), F32)],
        scratch_shapes=[pltpu.VMEM((1, LANE), F32)], name="fox_bwd_post",
        compiler_params=_params(("arbitrary",)))(proj, proj, fbl, qg, kg, bfor, bd, triu, dqa, dkn, dck)


def _merge_specs(tr):
    row = lambda w: pl.BlockSpec((tr, w), lambda i, j: (i, 0))
    shard = lambda r: pl.BlockSpec((None, r, 512), lambda i, j: (j, 0, 0))
    gate = lambda b: pl.BlockSpec((tr, 512), lambda i, j: (i, (GATE + 2048 * b) // 512 + j))
    return [row(768), row(768), row(512), shard(768), shard(768), shard(512), gate(0), gate(1), gate(2)]


def _merge_fwd(proj, ga, gb, gc, wa, wb, wc, *, tr):
    s = proj.shape[0]

    def body(ga_ref, gb_ref, gc_ref, wa_ref, wb_ref, wc_ref, l0_ref, l1_ref, l2_ref, y_ref):
        ua = _dot(ga_ref[...], wa_ref[...])
        ub = _dot(gb_ref[...], wb_ref[...])
        uc = _dot(gc_ref[...], wc_ref[...])
        y = jax.nn.sigmoid(l0_ref[...]) * ua + jax.nn.sigmoid(l1_ref[...]) * ub + jax.nn.sigmoid(l2_ref[...]) * uc
        y_ref[...] = y.astype(BF16)

    return pl.pallas_call(
        body, grid=(s // tr, N_CHIPS), in_specs=_merge_specs(tr),
        out_specs=pl.BlockSpec((tr, 512), lambda i, j: (i, j)), out_shape=SDS((s, D_MODEL), BF16), name="merge_fwd",
        compiler_params=_params(("parallel", "arbitrary")))(ga, gb, gc, wa, wb, wc, proj, proj, proj)


def _merge_bwd(proj, ga, gb, gc, wa, wb, wc, dy, *, tr):
    s = proj.shape[0]

    def body(ga_ref, gb_ref, gc_ref, wa_ref, wb_ref, wc_ref, l0_ref, l1_ref, l2_ref, dy_ref,
             dl0_ref, dl1_ref, dl2_ref, dua_ref, dub_ref, duc_ref, dga_ref, dgb_ref, dgc_ref):
        j = pl.program_id(1)
        dyv = dy_ref[...]

        @pl.when(j == 0)
        def _():
            dga_ref[...] = jnp.zeros_like(dga_ref)
            dgb_ref[...] = jnp.zeros_like(dgb_ref)
            dgc_ref[...] = jnp.zeros_like(dgc_ref)

        for g_ref, w_ref, l_ref, dl_ref, du_ref, dg_ref in (
                (ga_ref, wa_ref, l0_ref, dl0_ref, dua_ref, dga_ref),
                (gb_ref, wb_ref, l1_ref, dl1_ref, dub_ref, dgb_ref),
                (gc_ref, wc_ref, l2_ref, dl2_ref, duc_ref, dgc_ref)):
            w = w_ref[...]
            u = _dot(g_ref[...], w)
            sg = jax.nn.sigmoid(l_ref[...])
            dl_ref[...] = (dyv * u * sg * (1.0 - sg)).astype(BF16)
            du = (dyv * sg).astype(BF16)
            du_ref[...] = du
            dg_ref[...] += _dot_nt(du, w)

    blk = pl.BlockSpec((tr, 512), lambda i, j: (i, j))
    row = lambda w: pl.BlockSpec((tr, w), lambda i, j: (i, 0))
    big = SDS((s, D_MODEL), BF16)
    return pl.pallas_call(
        body, grid=(s // tr, N_CHIPS), in_specs=_merge_specs(tr) + [blk],
        out_specs=[blk] * 6 + [row(768), row(768), row(512)],
        out_shape=[big] * 6 + [SDS((s, 768), F32), SDS((s, 768), F32), SDS((s, 512), F32)], name="merge_bwd",
        compiler_params=_params(("parallel", "arbitrary")))(ga, gb, gc, wa, wb, wc, proj, proj, proj, dy)


def _out_loss(y, wo, x, tgt, *, tr, tn):
    s = x.shape[0]

    def body(y_ref, w_ref, x_ref, t_ref, d_ref, db_ref, sq_ref):
        @pl.when((pl.program_id(0) == 0) & (pl.program_id(1) == 0))
        def _():
            sq_ref[...] = jnp.zeros_like(sq_ref)

        out = x_ref[...] + _dot(y_ref[...], w_ref[...])
        diff = out - t_ref[...]
        sq_ref[...] += jnp.sum(diff * diff, axis=0, keepdims=True)
        d = diff * (1.0 / D_MODEL)
        d_ref[...] = d
        db_ref[...] = d.astype(BF16)

    blk = pl.BlockSpec((tr, tn), lambda i, j: (i, j))
    return pl.pallas_call(
        body, grid=(s // tr, D_MODEL // tn),
        in_specs=[pl.BlockSpec((tr, D_MODEL), lambda i, j: (i, 0)), pl.BlockSpec((D_MODEL, tn), lambda i, j: (0, j)), blk, blk],
        out_specs=[blk, blk, _const((1, tn))],
        out_shape=[SDS((s, D_MODEL), F32), SDS((s, D_MODEL), BF16), SDS((1, tn), F32)], name="out_loss",
        compiler_params=_params(("arbitrary", "arbitrary")))(y, wo, x, tgt)


def _tile_gain(g, reps):
    return jnp.tile(g.reshape(1, -1), (1, reps))


def _pad_lane(v):
    v = v.reshape(1, -1)
    return jnp.pad(v, ((0, 0), (0, LANE - v.shape[1])))


def _local_step(x, mem, tgt, w_main, w_fb, w_small, norm_gain, mem_norm_gain, b_forget,
                q_gain_a, k_gain_a, sinks_a, q_gain_b, k_gain_b, q_gain_c, k_gain_c, core=None):
    s = x.shape[0]
    tr = min(512, s)
    bd64 = _block_diag(768, HEAD_DIM)
    bd128 = _block_diag(512, C_HEAD_DIM)
    hsum = _head_sum(768, HEAD_DIM)
    qga, kga = _tile_gain(q_gain_a, 12), _tile_gain(k_gain_a, 4)
    qgb, kgb = _tile_gain(q_gain_b, 12), _tile_gain(k_gain_b, 12)
    qgc, kgc = _tile_gain(q_gain_c, 4), _tile_gain(k_gain_c, 4)
    sinks = _pad_lane(sinks_a)
    bfor = _pad_lane(b_forget)

    hn = _rms_fwd(x, norm_gain, tr=tr, name="rms_x")
    on_mesh = core is not None
    if on_mesh:
        half = D_MODEL // 2
        slabs = w_main
        top = _merge_slabs(slabs, (0, half))
        part, (slabs,) = _matmul(hn, top[0], dims="nn", out_dtype=F32, tm=1024, tn=512, tk=half, k_window=(0, 1),
                                 name="proj_main_top", comms=[_slab_gather_comm(slabs, (half, half))])
        w_main, w_fb = _merge_slabs(slabs[0], (half, half), into=top)
        proj, (gathered,) = _matmul(hn, w_main, dims="nn", out_dtype=F32, tm=1024, tn=512, tk=half, k_window=(1, 1),
                                    add=part, name="proj_main", comms=[_gather_comm(list(w_small))])
        w_mk, wa, wb, wc, wo = gathered
        w_mk, wo = w_mk.reshape(D_MODEL, 1024), wo.reshape(D_MODEL, D_MODEL)
    else:
        proj = _matmul(hn, w_main, dims="nn", out_dtype=F32, tm=1024, tn=512, tk=D_MODEL, name="proj_main")
        w_mk, wa, wb, wc, wo = w_small
    fbl = _matmul(hn, w_fb, dims="nn", out_dtype=F32, tm=1024, tn=LANE, tk=D_MODEL, name="proj_forget")
    memn = _rms_fwd(mem, mem_norm_gain, tr=mem.shape[0], name="rms_mem")
    mkv = _matmul(memn, w_mk, dims="nn", out_dtype=F32, tm=256, tn=512, tk=D_MODEL, name="mem_kv")

    swa_bias = _swa_bias()
    ga = _swa_fwd(proj, qga, kga, sinks, bd64, swa_bias)
    ea, eb, ones_a, ones_b = _expand_mats()
    tf = min(256, s)
    qat, ka, kat, va, vat, qn, cfox = _fox2_prep(proj, fbl, qgb, kgb, bfor, bd64, ea, eb, ones_a, ones_b, tr=tf)
    gb, yb, lse = _fox2_fwd(proj, qat, ka, vat)
    gc = _mem_fwd(proj, mkv, qgc, kgc, bd128, tr=tr)
    y = _merge_fwd(proj, ga, gb, gc, wa, wb, wc, tr=tr)
    dout, dout_b, sq = _out_loss(y, wo, x, tgt, tr=tr, tn=512)

    d_wo = _matmul(y, dout_b, dims="tn", out_dtype=F32, tm=1024, tn=512, tk=4096, name="dw_out")
    dy = _matmul(dout_b, wo, dims="nt", out_dtype=F32, tm=1024, tn=512, tk=D_MODEL, name="dy")
    dl0, dl1, dl2, dua, dub, duc, dga, dgb, dgc = _merge_bwd(proj, ga, gb, gc, wa, wb, wc, dy, tr=tr)
    d_wa = _matmul(ga, dua, dims="tn", out_dtype=F32, tm=768, tn=512, tk=4096, name="dw_branch_a")
    d_wb = _matmul(gb, dub, dims="tn", out_dtype=F32, tm=768, tn=512, tk=4096, name="dw_branch_b")
    d_wc = _matmul(gc, duc, dims="tn", out_dtype=F32, tm=512, tn=512, tk=4096, name="dw_branch_c")

    dcur, dprev, d_qga, d_kga, d_sinks = _swa_bwd(proj, qga, kga, sinks, bd64, swa_bias, dga)
    dproj_a = _swa_combine(dcur, dprev)

    qab, qabt, dya, dyat, dzb = _fox2_bwd_pre(proj, yb, dgb, qn, cfox, lse, hsum, ea, ones_b, tr=tf)
    dqa, dkn, dvb, dck = _fox2_bwd(qab, qabt, ka, kat, va, dya, dyat)
    dqb, dkb, dfb, d_qgb, d_kgb, d_bf = _fox2_bwd_post(proj, fbl, qgb, kgb, bfor, bd64, dqa, dkn, dck, tr=tf)

    dproj_c, dmkv, d_qgc, d_kgc = _mem_bwd(proj, mkv, qgc, kgc, bd128, dgc, tr=tr)
    dmkv_b = dmkv.astype(BF16)
    d_wmk = _matmul(memn, dmkv_b, dims="tn", out_dtype=F32, tm=1024, tn=512, tk=256, name="dw_mem_kv")
    dmemn = _matmul(dmkv_b, w_mk, dims="nt", out_dtype=F32, tm=256, tn=512, tk=1024, name="dmemn")
    (d_mem_gain,) = _rms_bwd(mem, mem_norm_gain, dmemn, None, tr=mem.shape[0], name="rms_mem_bwd")

    dproj = [dproj_a, jnp.concatenate([dqb, dkb, dvb, dzb, dproj_c], axis=1), dl0, dl1, dl2]
    dhn_f = _matmul(dfb, w_fb, dims="nt", out_dtype=F32, tm=1024, tn=512, tk=LANE, name="dhn_forget")
    d_wfb = _matmul(hn, dfb, dims="tn", out_dtype=F32, tm=1024, tn=LANE, tk=512, name="dw_forget")
    big = {}
    if on_mesh:
        half = D_MODEL // 2
        c0 = core[0]
        hn_other = lax.dynamic_slice(hn, (0, (1 - c0) * half), (s, half))
        hn_own = lax.dynamic_slice(hn, (0, c0 * half), (s, half))
        g1, k1 = [d_wmk, d_wa, d_wb, d_wc, d_wo], [2, 3, 4, 5, 6]
        d_other, (got1,) = _matmul(hn_other, dproj, dims="tn", out_dtype=F32, tm=1024, tn=512, tk=4096,
                                   name="dw_main_other", comms=[_exchange_comm(g1, k1)])
        h1 = [_add_half(g, got, core, HALF_AXIS[k], name=f"add_half_{k}") for g, got, k in zip(g1, got1, k1)]
        d_own, (got0, parts1) = _matmul(
            hn_own, dproj, dims="tn", out_dtype=F32, tm=1024, tn=512, tk=4096, name="dw_main_own",
            comms=[_exchange_comm([d_other, d_wfb], [0, 1], whole=(0,)), _scatter_comm(h1, k1)])
        h0 = [_add_pair(d_own, got0[0], name="add_pair_main"), _add_half(d_wfb, got0[1], core, 0, name="add_half_1")]
        sums1 = [_sum4(p, name=f"sum4_{k}") for p, k in zip(parts1, k1)]
        dhn, (parts0, theirs1) = _matmul(dproj, w_main, dims="nt", out_dtype=F32, tm=1024, tn=512, tk=2048, vmem=VMEM_WIDE, name="dhn",
                                         add=dhn_f, comms=[_scatter_comm(h0, [0, 1]), _swap_comm(sums1)])
        sums0 = [_sum4(p, name=f"sum4_{k}") for p, k in zip(parts0, (0, 1))]
        theirs0 = _run_comm(_swap_comm(sums0), "swap_halves")
        big = dict(sums=sums0 + sums1, theirs=list(theirs0) + list(theirs1))
    else:
        dhn = _matmul(dproj, w_main, dims="nt", out_dtype=F32, tm=1024, tn=512, tk=2048, vmem=VMEM_WIDE, name="dhn", add=dhn_f)
        d_wmain = _matmul(hn, dproj, dims="tn", out_dtype=F32, tm=1024, tn=512, tk=4096, name="dw_main")
        big = dict(d_wmain=d_wmain, d_wfb=d_wfb, d_wmk=d_wmk, d_wa=d_wa, d_wb=d_wb, d_wc=d_wc, d_wo=d_wo)
    grad_x, d_gain = _rms_bwd(x, norm_gain, dhn, dout, tr=tr, name="rms_x_bwd")

    fold = lambda g, reps: jnp.sum(g.reshape(reps, -1), axis=0, keepdims=True)
    return dict(
        sq=sq, grad_x=grad_x, **big,
        d_gain=d_gain, d_mem_gain=d_mem_gain, d_bf=d_bf[:, :N_FORGET],
        d_qga=fold(d_qga, 12), d_kga=fold(d_kga, 4), d_sinks=d_sinks[:, :A_HEADS],
        d_qgb=fold(d_qgb, 12), d_kgb=fold(d_kgb, 12), d_qgc=fold(d_qgc, 4), d_kgc=fold(d_kgc, 4))


PACK_ROWS = 256
FORGET_IN_SHARD = FORGET_COL - SHARD_COLS
AFTER_FORGET = FORGET_COL - SLAB_START[1]
END_CHIP1 = 2 * SHARD_COLS - N_FORGET - SLAB_START[1]


def _pack_w_in(chip, w):
    rows = w.shape[1]
    tr = PACK_ROWS

    def body(k_ref, w_ref, o_ref, scr):
        scr[...] = jnp.zeros_like(scr)
        scr[pl.ds(0, SHARD_COLS), :] = w_ref[...]
        v = jnp.transpose(scr[...])
        k = k_ref[0]
        col = lax.broadcasted_iota(jnp.int32, (tr, SLAB), 1)
        no_forget = jnp.zeros((tr, LANE), BF16)

        @pl.when(k == 0)
        def _():
            o_ref[:, 0:SLAB] = v.astype(BF16)
            o_ref[:, SLAB:] = no_forget

        @pl.when(k == 1)
        def _():
            before = pltpu.roll(v, SLAB_SHIFT[1], axis=1)
            after = pltpu.roll(v, SLAB - (N_FORGET - SLAB_SHIFT[1]), axis=1)
            slab = jnp.where(col < AFTER_FORGET, before, jnp.where(col < END_CHIP1, after, 0.0))
            o_ref[:, 0:SLAB] = slab.astype(BF16)
            f = pltpu.roll(v, SLAB - FORGET_IN_SHARD, axis=1)[:, :LANE]
            o_ref[:, SLAB:] = jnp.where(col[:, :LANE] < N_FORGET, f, 0.0).astype(BF16)

        for kk in (2, 3):
            @pl.when(k == kk)
            def _(kk=kk):
                o_ref[:, 0:SLAB] = pltpu.roll(v, SLAB_SHIFT[kk], axis=1).astype(BF16)
                o_ref[:, SLAB:] = no_forget

    return pl.pallas_call(
        body, grid_spec=pltpu.PrefetchScalarGridSpec(
            num_scalar_prefetch=1, grid=(rows // tr,),
            in_specs=[pl.BlockSpec((SHARD_COLS, tr), lambda i, k: (0, i))],
            out_specs=pl.BlockSpec((None, tr, SLAB + LANE), lambda i, k: (k[0], i, 0)),
            scratch_shapes=[pltpu.VMEM((SLAB, tr), F32)]),
        out_shape=SDS((N_CHIPS, rows, SLAB + LANE), BF16), name="pack_w_in",
        compiler_params=_params(("arbitrary",)))(chip, w)


def _merge_slabs(g, row_window=None, into=None):
    rows = g.shape[1]
    tr = PACK_ROWS
    t = [s // LANE for s in SLAB_START]
    n_t = SLAB // LANE
    r0, nr = (0, rows) if row_window is None else row_window
    b0 = r0 // tr

    def body(g_ref, *refs):
        m_ref, f_ref = refs[-2:]
        for k in range(N_CHIPS):
            lo = t[k] + (1 if k > 0 else 0)
            hi = t[k + 1] if k + 1 < N_CHIPS else t[k] + n_t
            m_ref[:, lo * LANE:hi * LANE] = g_ref[k, :, (lo - t[k]) * LANE:(hi - t[k]) * LANE]
            if k + 1 < N_CHIPS:
                a = g_ref[k, :, (hi - t[k]) * LANE:(hi - t[k] + 1) * LANE].astype(F32)
                b = g_ref[k + 1, :, 0:LANE].astype(F32)
                m_ref[:, hi * LANE:(hi + 1) * LANE] = (a + b).astype(BF16)
        f_ref[...] = g_ref[1, :, SLAB:]

    hbm = pl.BlockSpec(memory_space=pl.ANY)
    extra = list(into) if into is not None else []
    return pl.pallas_call(
        body, grid=(nr // tr,),
        in_specs=[pl.BlockSpec((N_CHIPS, tr, SLAB + LANE), lambda i: (0, b0 + i, 0))] + [hbm] * len(extra),
        out_specs=[pl.BlockSpec((tr, P_MAIN), lambda i: (b0 + i, 0)), pl.BlockSpec((tr, LANE), lambda i: (b0 + i, 0))],
        out_shape=[SDS((rows, P_MAIN), BF16), SDS((rows, LANE), BF16)],
        input_output_aliases={1: 0, 2: 1} if extra else {}, name="merge_slabs",
        compiler_params=_params(("parallel",)))(g, *extra)


def _adamw_math(w, g, m, v):
    nm = ADAM_B1 * m + (1.0 - ADAM_B1) * g
    nv = ADAM_B2 * v + (1.0 - ADAM_B2) * (g * g)
    m_hat = nm / (1.0 - ADAM_B1 ** ADAM_STEP)
    v_hat = nv / (1.0 - ADAM_B2 ** ADAM_STEP)
    delta = -ADAM_LR * (m_hat / (jnp.sqrt(v_hat) + ADAM_EPS) + ADAM_WD * w)
    return delta, nm, nv


def _adamw(g, w, m, v, *, tr, name):
    rows, cols = w.shape
    tr = min(tr, rows)

    def body(g_ref, w_ref, m_ref, v_ref, d_ref, nm_ref, nv_ref):
        d, nm, nv = _adamw_math(w_ref[...], g_ref[...], m_ref[...], v_ref[...])
        d_ref[...] = d
        nm_ref[...] = nm
        nv_ref[...] = nv

    spec = _rowblk(tr, cols)
    return pl.pallas_call(
        body, grid=(rows // tr,), in_specs=[spec] * 4, out_specs=[spec] * 3,
        out_shape=[SDS((rows, cols), F32)] * 3, name=name, compiler_params=_params(("parallel",)))(g, w, m, v)


def _adamw_w_in(chip_core, slab_mine, slab_theirs, forget_mine, forget_theirs, w, m, v):
    rows = w.shape[1]
    tr = PACK_ROWS // 2
    nbh = rows // 2 // tr

    def body(k_ref, sa_ref, sb_ref, fa_ref, fb_ref, w_ref, m_ref, v_ref, g_ref, d_ref, nm_ref, nv_ref):
        use_mine = pl.program_id(0) // nbh == k_ref[1]
        sl = jnp.where(use_mine, sa_ref[...], sb_ref[...])
        f_tile = jnp.where(use_mine, fa_ref[...], fb_ref[...])
        k = k_ref[0]

        def emit(wide):
            g = jnp.transpose(wide)[:SHARD_COLS, :]
            g_ref[...] = g
            d, nm, nv = _adamw_math(w_ref[...], g, m_ref[...], v_ref[...])
            d_ref[...] = d
            nm_ref[...] = nm
            nv_ref[...] = nv

        @pl.when(k == 0)
        def _():
            emit(sl)

        @pl.when(k == 1)
        def _():
            col = lax.broadcasted_iota(jnp.int32, (tr, SLAB), 1)
            before = pltpu.roll(sl, SLAB - SLAB_SHIFT[1], axis=1)
            after = pltpu.roll(sl, N_FORGET - SLAB_SHIFT[1], axis=1)
            wide_f = jnp.concatenate([f_tile, jnp.zeros((tr, SLAB - LANE), F32)], axis=1)
            forget = pltpu.roll(wide_f, FORGET_IN_SHARD, axis=1)
            emit(jnp.where(col < FORGET_IN_SHARD, before, jnp.where(col < FORGET_IN_SHARD + N_FORGET, forget, after)))

        for kk in (2, 3):
            @pl.when(k == kk)
            def _(kk=kk):
                emit(pltpu.roll(sl, SLAB - SLAB_SHIFT[kk], axis=1))

    nat = pl.BlockSpec((SHARD_COLS, tr), lambda i, k: (0, i))
    half = lambda width: pl.BlockSpec((tr, width), lambda i, k: (i % nbh, 0))
    return pl.pallas_call(
        body, grid_spec=pltpu.PrefetchScalarGridSpec(
            num_scalar_prefetch=1, grid=(rows // tr,),
            in_specs=[half(SLAB), half(SLAB), half(LANE), half(LANE), nat, nat, nat],
            out_specs=[nat] * 4),
        out_shape=[SDS((SHARD_COLS, rows), F32)] * 4, name="adamw_w_in",
        compiler_params=_params(("arbitrary",)))(chip_core, slab_mine, slab_theirs, forget_mine, forget_theirs, w, m, v)


ANY = pl.BlockSpec(memory_space=pl.ANY)
HALF_AXIS = (0, 0, 1, 0, 0, 0, 1)


def _me():
    return lax.axis_index("x"), lax.axis_index("y"), lax.axis_index("c")


def _half(ref, which, axis):
    n = ref.shape[axis] // 2
    sl = pl.ds(which * n, n)
    return ref.at[sl] if axis == 0 else ref.at[:, sl]


def _piece(t, ref, j):
    if t == 0:
        return ref.at[:, pl.ds(SLAB_START[j], SLAB)]
    if t == 1:
        return ref
    if t in (2, 6):
        return ref.at[pl.ds(512 * j, 512)]
    return ref.at[:, pl.ds(512 * j, 512)]


def _piece_shape(t, shape):
    if t == 0:
        return (shape[0], SLAB)
    if t == 1:
        return shape
    if t in (2, 6):
        return (512, shape[1])
    return (shape[0], 512)


def _gather_plan(ins, outs, own_slot_in_src, row_window=None):
    x, y, c = _me()
    k = 2 * x + y
    sib = (x, y, 1 - c)
    chips = [(1 - x, y), (x, 1 - y), (1 - x, 1 - y)]
    n = len(outs)

    def rows(t, which):
        r0, nr = (0, outs[t].shape[1]) if row_window is None else row_window
        return pl.ds(r0 + which * (nr // 2), nr // 2)

    def mine(t):
        return ins[t].at[k, rows(t, c)] if own_slot_in_src else ins[t].at[rows(t, c)]

    def first(t, j, sems):
        chip = chips[j]
        return pltpu.make_async_remote_copy(
            src_ref=mine(t), dst_ref=outs[t].at[k, rows(t, c)], send_sem=sems[0].at[t, j], recv_sem=sems[1].at[t, j],
            device_id=(chip[0], chip[1], c), device_id_type=MESH)

    def landed(t, j, sems):
        chip = chips[j]
        return pltpu.make_async_remote_copy(
            src_ref=mine(t), dst_ref=outs[t].at[2 * chip[0] + chip[1], rows(t, c)], send_sem=sems[0].at[t, j],
            recv_sem=sems[1].at[t, j], device_id=(chip[0], chip[1], c), device_id_type=MESH)

    def passed(t, j, which, sems):
        chip = chips[j]
        blk = outs[t].at[2 * chip[0] + chip[1], rows(t, which)]
        return pltpu.make_async_remote_copy(
            src_ref=blk, dst_ref=blk, send_sem=sems[2].at[t, j], recv_sem=sems[3].at[t, j], device_id=sib,
            device_id_type=MESH)

    def start(sems):
        for j in range(3):
            for t in range(n):
                first(t, j, sems).start()

    def finish(sems):
        for j in range(3):
            for t in range(n):
                landed(t, j, sems).wait_recv()
                passed(t, j, c, sems).start()
        for j in range(3):
            for t in range(n):
                passed(t, j, 1 - c, sems).wait_recv()
        for j in range(3):
            for t in range(n):
                first(t, j, sems).wait_send()
                passed(t, j, c, sems).wait_send()

    return k, start, finish


def _slab_gather_comm(slabs, row_window):
    def start(ins, outs, sems):
        _gather_plan(ins, outs, True, row_window)[1](sems)

    def finish(ins, outs, sems):
        _gather_plan(ins, outs, True, row_window)[2](sems)

    return _Comm([slabs], [SDS(slabs.shape, slabs.dtype)], [pltpu.SemaphoreType.DMA((1, 3))] * 4, start, finish,
                 alias=((0, 0),))


def _all_gather_slabs(slabs, row_window):
    comm = _slab_gather_comm(slabs, row_window)

    def body(in_ref, out_ref, *sems):
        comm.start([in_ref], [out_ref], sems)
        comm.finish([in_ref], [out_ref], sems)

    return pl.pallas_call(
        body, in_specs=[ANY], out_specs=ANY, out_shape=SDS(slabs.shape, slabs.dtype),
        scratch_shapes=comm.sems, input_output_aliases={0: 0}, name="all_gather_slabs")(slabs)


def _gather_comm(parts):
    n = len(parts)

    def start(ins, outs, sems):
        k, go, _ = _gather_plan(ins, outs, False)
        for t in range(n):
            pltpu.make_async_copy(ins[t], outs[t].at[k], sems[4].at[t]).start()
        go(sems)

    def finish(ins, outs, sems):
        k, _, done = _gather_plan(ins, outs, False)
        done(sems)
        for t in range(n):
            pltpu.make_async_copy(ins[t], outs[t].at[k], sems[4].at[t]).wait()

    return _Comm(parts, [SDS((N_CHIPS,) + p.shape, p.dtype) for p in parts],
                 [pltpu.SemaphoreType.DMA((n, 3))] * 4 + [pltpu.SemaphoreType.DMA((n,))], start, finish)


def _exchange_comm(arrs, kinds, whole=()):
    n = len(arrs)

    def copies(ins, outs, sems):
        x, y, c = _me()
        return [pltpu.make_async_remote_copy(
            src_ref=ins[t] if t in whole else _half(ins[t], 1 - c, HALF_AXIS[kinds[t]]), dst_ref=outs[t],
            send_sem=sems[0].at[t], recv_sem=sems[1].at[t], device_id=(x, y, 1 - c), device_id_type=MESH)
            for t in range(n)]

    def start(ins, outs, sems):
        for cp in copies(ins, outs, sems):
            cp.start()

    def finish(ins, outs, sems):
        for cp in copies(ins, outs, sems):
            cp.wait()

    def hshape(t):
        s = list(arrs[t].shape)
        if t not in whole:
            s[HALF_AXIS[kinds[t]]] //= 2
        return SDS(tuple(s), arrs[t].dtype)

    return _Comm(arrs, [hshape(t) for t in range(n)], [pltpu.SemaphoreType.DMA((n,))] * 2, start, finish)


def _add_half(full, got, core, axis, *, name):
    r, c = got.shape
    br, bc = (256 if r % 256 == 0 else 128), min(2048, c)
    off_r = (r // br) if axis == 0 else 0
    off_c = (c // bc) if axis == 1 else 0

    def body(c_ref, a_ref, b_ref, o_ref):
        o_ref[...] = (a_ref[...] + b_ref[...]).astype(BF16)

    return pl.pallas_call(
        body, grid_spec=pltpu.PrefetchScalarGridSpec(
            num_scalar_prefetch=1, grid=(r // br, c // bc),
            in_specs=[pl.BlockSpec((br, bc), lambda i, j, cr: (i + cr[0] * off_r, j + cr[0] * off_c)),
                      pl.BlockSpec((br, bc), lambda i, j, cr: (i, j))],
            out_specs=pl.BlockSpec((br, bc), lambda i, j, cr: (i, j))),
        out_shape=SDS((r, c), BF16), name=name, compiler_params=_params(("parallel", "parallel")))(core, full, got)


def _add_pair(a, b, *, name):
    r, c = a.shape
    br, bc = 256, min(2048, c)

    def body(a_ref, b_ref, o_ref):
        o_ref[...] = (a_ref[...] + b_ref[...]).astype(BF16)

    spec = pl.BlockSpec((br, bc), lambda i, j: (i, j))
    return pl.pallas_call(body, grid=(r // br, c // bc), in_specs=[spec, spec], out_specs=spec,
                          out_shape=SDS((r, c), BF16), name=name, compiler_params=_params(("parallel", "parallel")))(a, b)


def _scatter_comm(halves, kinds):
    n = len(halves)

    def plan(ins, outs, sems):
        send, recv, lsem = sems
        x, y, c = _me()
        k = 2 * x + y

        def to_chip(t, j):
            return pltpu.make_async_remote_copy(
                src_ref=_piece(kinds[t], ins[t], j), dst_ref=outs[t].at[k], send_sem=send.at[t, j],
                recv_sem=recv.at[t, k], device_id=(j // 2, j % 2, c), device_id_type=MESH)

        def from_chip(t, j):
            return pltpu.make_async_remote_copy(
                src_ref=_piece(kinds[t], ins[t], j), dst_ref=outs[t].at[j], send_sem=send.at[t, j],
                recv_sem=recv.at[t, j], device_id=(j // 2, j % 2, c), device_id_type=MESH)

        def own(t, j):
            return pltpu.make_async_copy(_piece(kinds[t], ins[t], j), outs[t].at[j], lsem.at[t])

        return k, to_chip, from_chip, own

    def start(ins, outs, sems):
        k, to_chip, _, own = plan(ins, outs, sems)
        for j in range(N_CHIPS):
            @pl.when(k != j)
            def _(j=j):
                for t in range(n):
                    to_chip(t, j).start()

            @pl.when(k == j)
            def _(j=j):
                for t in range(n):
                    own(t, j).start()

    def finish(ins, outs, sems):
        k, to_chip, from_chip, own = plan(ins, outs, sems)
        for j in range(N_CHIPS):
            @pl.when(k != j)
            def _(j=j):
                for t in range(n):
                    from_chip(t, j).wait_recv()
                for t in range(n):
                    to_chip(t, j).wait_send()

            @pl.when(k == j)
            def _(j=j):
                for t in range(n):
                    own(t, j).wait()

    return _Comm(halves, [SDS((N_CHIPS,) + _piece_shape(kinds[t], halves[t].shape), halves[t].dtype) for t in range(n)],
                 [pltpu.SemaphoreType.DMA((n, N_CHIPS))] * 2 + [pltpu.SemaphoreType.DMA((n,))], start, finish)


def _sum4(p, *, name):
    _, r, c = p.shape
    br = 256 if r % 256 == 0 else 128

    def body(p_ref, o_ref):
        o_ref[...] = ((p_ref[0].astype(F32) + p_ref[1].astype(F32)) + p_ref[2].astype(F32)) + p_ref[3].astype(F32)

    return pl.pallas_call(
        body, grid=(r // br,), in_specs=[pl.BlockSpec((N_CHIPS, br, c), lambda i: (0, i, 0))],
        out_specs=_rowblk(br, c), out_shape=SDS((r, c), F32), name=name, compiler_params=_params(("parallel",)))(p)


def _swap_comm(sums):
    return _exchange_comm(sums, [None] * len(sums), whole=tuple(range(len(sums))))


def _adamw_halves(mine, theirs, core, w, m, v, *, axis, tr, name):
    rows, cols = w.shape

    if axis == 0:
        nbh = rows // 2 // tr
        g_spec = pl.BlockSpec((tr, cols), lambda i, cr: (i % nbh, 0))
    else:
        g_spec = pl.BlockSpec((tr, cols // 2), lambda i, cr: (i, 0))

    def body(c_ref, a_ref, b_ref, w_ref, m_ref, v_ref, g_ref, d_ref, nm_ref, nv_ref):
        a, b = a_ref[...], b_ref[...]
        if axis == 0:
            g = jnp.where(pl.program_id(0) // nbh == c_ref[0], a, b)
        else:
            low = c_ref[0] == 0
            g = jnp.concatenate([jnp.where(low, a, b), jnp.where(low, b, a)], axis=1)
        g_ref[...] = g
        d, nm, nv = _adamw_math(w_ref[...], g, m_ref[...], v_ref[...])
        d_ref[...] = d
        nm_ref[...] = nm
        nv_ref[...] = nv

    nat = pl.BlockSpec((tr, cols), lambda i, cr: (i, 0))
    return pl.pallas_call(
        body, grid_spec=pltpu.PrefetchScalarGridSpec(
            num_scalar_prefetch=1, grid=(rows // tr,), in_specs=[g_spec, g_spec, nat, nat, nat], out_specs=[nat] * 4),
        out_shape=[SDS((rows, cols), F32)] * 4, name=name, compiler_params=_params(("arbitrary",)))(
            core, mine, theirs, w, m, v)


SMALL_ROWS, SMALL_COLS = 8, 1024


def _pack_small(vs):
    flat = jnp.concatenate([v.reshape(-1) for v in vs])
    return jnp.pad(flat, (0, SMALL_ROWS * SMALL_COLS - flat.shape[0])).reshape(SMALL_ROWS, SMALL_COLS)


def _unpack_small(packed, sizes):
    flat = packed.reshape(-1)
    out, o = [], 0
    for n in sizes:
        out.append(flat[o:o + n].reshape(1, n))
        o += n
    return out


def _all_reduce_small(v):
    n_dev = 8

    def body(v_ref, o_ref, land, send, recv):
        x, y, c = _me()
        me = 4 * x + 2 * y + c
        land[me] = v_ref[...]
        cps = []
        for r in range(1, n_dev):
            fx, fy, fc = (r >> 2) & 1, (r >> 1) & 1, r & 1
            peer = (x ^ fx, y ^ fy, c ^ fc)
            cps.append(pltpu.make_async_remote_copy(
                src_ref=v_ref, dst_ref=land.at[me], send_sem=send.at[r - 1], recv_sem=recv.at[r - 1],
                device_id=peer, device_id_type=MESH))
        for cp in cps:
            cp.start()
        for r in range(1, n_dev):
            fx, fy, fc = (r >> 2) & 1, (r >> 1) & 1, r & 1
            src = 4 * (x ^ fx) + 2 * (y ^ fy) + (c ^ fc)
            pltpu.make_async_remote_copy(
                src_ref=v_ref, dst_ref=land.at[src], send_sem=send.at[r - 1], recv_sem=recv.at[r - 1],
                device_id=(x ^ fx, y ^ fy, c ^ fc), device_id_type=MESH).wait_recv()
        for cp in cps:
            cp.wait_send()
        acc = land[0]
        for r in range(1, n_dev):
            acc = acc + land[r]
        o_ref[...] = acc

    vm = pl.BlockSpec(memory_space=pltpu.VMEM)
    return pl.pallas_call(
        body, in_specs=[vm], out_specs=vm, out_shape=SDS(v.shape, F32),
        scratch_shapes=[pltpu.VMEM((n_dev,) + v.shape, F32), pltpu.SemaphoreType.DMA((n_dev - 1,)),
                        pltpu.SemaphoreType.DMA((n_dev - 1,))],
        name="all_reduce_small")(v)


def kernel(x, mem, norm_gain, mem_norm_gain, w_in, b_forget, q_gain_a, k_gain_a, sinks_a, q_gain_b, k_gain_b, q_gain_c, k_gain_c, w_mem_kv, w_branch_a, w_branch_b, w_branch_c, w_out, loss_target, m_norm_gain, m_mem_norm_gain, m_w_in, m_b_forget, m_q_gain_a, m_k_gain_a, m_sinks_a, m_q_gain_b, m_k_gain_b, m_q_gain_c, m_k_gain_c, m_w_mem_kv, m_w_branch_a, m_w_branch_b, m_w_branch_c, m_w_out, v_norm_gain, v_mem_norm_gain, v_w_in, v_b_forget, v_q_gain_a, v_k_gain_a, v_sinks_a, v_q_gain_b, v_k_gain_b, v_q_gain_c, v_k_gain_c, v_w_mem_kv, v_w_branch_a, v_w_branch_b, v_w_branch_c, v_w_out):
    xi, yi, ci = lax.axis_index("x"), lax.axis_index("y"), lax.axis_index("c")
    chip = jnp.reshape(2 * xi + yi, (1,)).astype(jnp.int32)
    core = jnp.reshape(ci, (1,)).astype(jnp.int32)

    slabs = _pack_w_in(chip, jnp.transpose(w_in[0]))
    mine = [w_mem_kv[0].astype(BF16), w_branch_a[0].astype(BF16), w_branch_b[0].astype(BF16),
            w_branch_c[0].astype(BF16), w_out[0].astype(BF16)]
    slabs = _all_gather_slabs(slabs, (0, D_MODEL // 2))

    r = _local_step(x[0], mem[0], loss_target[0], slabs, None, mine, norm_gain, mem_norm_gain,
                    b_forget, q_gain_a, k_gain_a, sinks_a, q_gain_b, k_gain_b, q_gain_c, k_gain_c, core=core)
    sums, theirs = r["sums"], r["theirs"]

    small_names = ["d_gain", "d_mem_gain", "d_bf", "d_qga", "d_kga", "d_sinks", "d_qgb", "d_kgb", "d_qgc", "d_kgc"]
    loss_part = (0.5 / D_MODEL) * jnp.sum(r["sq"], axis=1, keepdims=True)
    packed = _pack_small([r[n] for n in small_names] + [loss_part])
    red = _all_reduce_small(packed)
    small_w = [norm_gain, mem_norm_gain, b_forget, q_gain_a, k_gain_a, sinks_a, q_gain_b, k_gain_b, q_gain_c, k_gain_c]
    small_m = [m_norm_gain, m_mem_norm_gain, m_b_forget, m_q_gain_a, m_k_gain_a, m_sinks_a, m_q_gain_b, m_k_gain_b,
               m_q_gain_c, m_k_gain_c]
    small_v = [v_norm_gain, v_mem_norm_gain, v_b_forget, v_q_gain_a, v_k_gain_a, v_sinks_a, v_q_gain_b, v_k_gain_b,
               v_q_gain_c, v_k_gain_c]
    sizes = [w.shape[1] for w in small_w]
    s_d, s_m, s_v = _adamw(red, _pack_small(small_w), _pack_small(small_m), _pack_small(small_v), tr=8, name="adamw_small")
    g_small = _unpack_small(red, sizes + [1])
    loss = g_small[-1].reshape(())
    d_small, m_small, v_small = _unpack_small(s_d, sizes), _unpack_small(s_m, sizes), _unpack_small(s_v, sizes)

    gw_in, dw_in, mw_in, vw_in = _adamw_w_in(jnp.concatenate([chip, core]), sums[0], theirs[0], sums[1], theirs[1],
                                             jnp.transpose(w_in[0]), jnp.transpose(m_w_in[0]), jnp.transpose(v_w_in[0]))
    big = {}
    for t, nm, w, m, v in ((2, "w_mem_kv", w_mem_kv, m_w_mem_kv, v_w_mem_kv),
                           (3, "w_branch_a", w_branch_a, m_w_branch_a, v_w_branch_a),
                           (4, "w_branch_b", w_branch_b, m_w_branch_b, v_w_branch_b),
                           (5, "w_branch_c", w_branch_c, m_w_branch_c, v_w_branch_c),
                           (6, "w_out", w_out, m_w_out, v_w_out)):
        big[nm] = _adamw_halves(sums[t], theirs[t], core, w[0], m[0], v[0], axis=HALF_AXIS[t], tr=128,
                                name="adamw_" + nm)

    def collect(kind):
        sm = (g_small, d_small, m_small, v_small)[kind]
        win = (gw_in, dw_in, mw_in, vw_in)[kind]
        return ([sm[0], sm[1], jnp.transpose(win)[None]] + [a for a in sm[2:10]]
                + [big[n][kind][None] for n in ("w_mem_kv", "w_branch_a", "w_branch_b", "w_branch_c", "w_out")])

    return (loss, r["grad_x"][None], *collect(0), *collect(1), *collect(2), *collect(3))
```

```python
import functools

import numpy as np
import jax
import jax.numpy as jnp
from jax import lax
from jax.experimental import pallas as pl
from jax.experimental.pallas import tpu as pltpu

F32 = jnp.float32
BF16 = jnp.bfloat16
HI = lax.Precision.HIGHEST
SDS = jax.ShapeDtypeStruct
MESH = pl.DeviceIdType.MESH

D_MODEL = 2048
HEAD_DIM = 64
A_HEADS = 12
A_GROUP = 3
B_HEADS = 12
C_HEADS = 4
C_HEAD_DIM = 128
WINDOW = 128
EPS = 1e-6
NEG = -1e30
LANE = 128

QA, KA, VA, ZA = 0, 768, 1024, 1280
QB, KB, VB, ZB = 2048, 2816, 3584, 4352
QC, ZC = 5120, 5632
GATE = 6144
P_MAIN = 12288
N_FORGET = 12
FORGET_COL = 5120
SHARD_COLS = 3075
SLAB = 3200
SLAB_START = (0, 3072, 6016, 9088)
SLAB_SHIFT = (0, 3, 122, 125)
N_CHIPS = 4

ADAM_LR = 0.001
ADAM_B1 = 0.9
ADAM_B2 = 0.999
ADAM_EPS = 1e-08
ADAM_WD = 0.01
ADAM_STEP = 10

VMEM_LIMIT = 56 * 1024 * 1024
VMEM_WIDE = 62 * 1024 * 1024


def _params(sem, vmem=VMEM_LIMIT):
    return pltpu.CompilerParams(dimension_semantics=sem, vmem_limit_bytes=vmem)


def _win(tr, width, off):
    return pl.BlockSpec((pl.Element(tr), pl.Element(width)), lambda i, *_: (i * tr, off))


def _rowblk(tr, width):
    return pl.BlockSpec((tr, width), lambda i, *_: (i, 0))


def _const(shape):
    nd = len(shape)
    return pl.BlockSpec(shape, lambda *_: (0,) * nd)


def _rms(x, g):
    return x * lax.rsqrt(jnp.mean(x * x, axis=-1, keepdims=True) + EPS) * g


def _head_mean_impl(x2, bd):
    hi = x2.astype(BF16)
    lo = (x2 - hi.astype(F32)).astype(BF16)
    return _dot(hi, bd) + _dot(lo, bd)


@jax.custom_vjp
def _head_mean(x2, bd):
    return _head_mean_impl(x2, bd)


_head_mean.defvjp(lambda x2, bd: (_head_mean_impl(x2, bd), bd),
                  lambda bd, g: (_head_mean_impl(g, bd), jnp.zeros_like(bd)))


def _head_norm(x, g_tiled, bd):
    return x * lax.rsqrt(_head_mean(x * x, bd) + EPS) * g_tiled


def _silu(z):
    return z * jax.nn.sigmoid(z)


def _dot_nt(a, b):
    return lax.dot_general(a, b, (((1,), (1,)), ((), ())), preferred_element_type=F32)


def _dot_tn(a, b):
    return lax.dot_general(a, b, (((0,), (0,)), ((), ())), preferred_element_type=F32)


def _dot(a, b):
    return jnp.dot(a, b, preferred_element_type=F32)


def _swa_fn(qk, vz, qkp, vzp, qg, kg, sinks, bd, bias, first):
    q = _head_norm(qk[:, :768], qg, bd)
    k2 = jnp.concatenate([qkp[:, 768:], qk[:, 768:]], axis=0)
    k2 = _head_norm(k2, kg, bd[:256, :256])
    v2 = jnp.concatenate([vzp[:, :256], vz[:, :256]], axis=0)
    z = vz[:, 256:]
    cols = A_GROUP * WINDOW
    kj = lax.broadcasted_iota(jnp.int32, (2 * WINDOW, cols), 0)
    no_prev = kj < WINDOW * first.astype(jnp.int32)
    qtb = jnp.transpose(q).astype(BF16)
    kb = k2.astype(BF16)
    vtb = jnp.transpose(v2).astype(BF16)
    outs = [None] * A_HEADS
    for g in range(A_HEADS // A_GROUP):
        heads = [A_GROUP * g + u for u in range(A_GROUP)]
        qs = jnp.concatenate([qtb[64 * h:64 * h + 64, :] for h in heads], axis=1)
        s = _dot(kb[:, 64 * g:64 * g + 64], qs) * (HEAD_DIM ** -0.5) + bias[g]
        s = jnp.where(no_prev, NEG, s)
        sink = jnp.concatenate([jnp.broadcast_to(sinks[:, h:h + 1], (1, WINDOW)) for h in heads], axis=1)
        m = lax.stop_gradient(jnp.maximum(jnp.max(s, axis=0, keepdims=True), sink))
        p = jnp.exp(s - m)
        den = jnp.sum(p, axis=0, keepdims=True) + jnp.exp(sink - m)
        o = _dot(vtb[64 * g:64 * g + 64, :], (p * (1.0 / den)).astype(BF16))
        for u, h in enumerate(heads):
            outs[h] = o[:, WINDOW * u:WINDOW * u + WINDOW]
    return jnp.transpose(jnp.concatenate(outs, axis=0)) * _silu(z)


def _swa_bias():
    qi = np.arange(WINDOW)[None, :]
    kj = np.arange(2 * WINDOW)[:, None]
    rel = qi + WINDOW - kj
    valid = (rel >= 0) & (rel < WINDOW)
    out = np.zeros((A_HEADS // A_GROUP, 2 * WINDOW, A_GROUP * WINDOW), np.float32)
    for h in range(A_HEADS):
        slope = np.float32(2.0 ** (-8.0 * (h + 1) / A_HEADS))
        blk = np.where(valid, -slope * rel.astype(np.float32), np.float32(NEG))
        g, u = divmod(h, A_GROUP)
        out[g, :, WINDOW * u:WINDOW * u + WINDOW] = blk
    return jnp.asarray(out)


def _mem_fn(qz, mkv, qg, kg, bd):
    q = _head_norm(qz[:, :512], qg, bd).astype(BF16)
    k = _head_norm(mkv[:, :512], kg, bd).astype(BF16)
    v = mkv[:, 512:].astype(BF16)
    z = qz[:, 512:]
    outs = []
    for h in range(C_HEADS):
        sl = slice(128 * h, 128 * h + 128)
        s = _dot_nt(q[:, sl], k[:, sl]) * (C_HEAD_DIM ** -0.5)
        m = lax.stop_gradient(jnp.max(s, axis=-1, keepdims=True))
        p = jnp.exp(s - m)
        den = jnp.sum(p, axis=-1, keepdims=True)
        outs.append(_dot((p * (1.0 / den)).astype(BF16), v[:, sl]))
    return jnp.concatenate(outs, axis=1) * _silu(z)


def _qn_fn(q, g, bd):
    return _head_norm(q, g, bd) * (HEAD_DIM ** -0.5)


def _kn_fn(k, g, bd):
    return _head_norm(k, g, bd)


def _block_diag(width, hd):
    i = np.arange(width) // hd
    return jnp.asarray((i[:, None] == i[None, :]).astype(np.float32) / hd, BF16)


def _head_sum(width, hd):
    i = np.arange(width) // hd
    return jnp.asarray((i[:, None] == np.arange(LANE)[None, :]).astype(np.float32))


def _rms_fwd(x, g, *, tr, name):
    rows, dm = x.shape

    def body(x_ref, g_ref, o_ref):
        o_ref[...] = _rms(x_ref[...], g_ref[...]).astype(BF16)

    return pl.pallas_call(
        body, grid=(rows // tr,),
        in_specs=[_rowblk(tr, dm), _const((1, dm))],
        out_specs=_rowblk(tr, dm),
        out_shape=SDS((rows, dm), BF16), name=name,
        compiler_params=_params(("parallel",)))(x, g)


def _rms_bwd(x, g, dy, resid, *, tr, name):
    rows, dm = x.shape
    want_dx = resid is not None

    def body(*refs):
        if want_dx:
            x_ref, g_ref, dy_ref, r_ref, dx_ref, dg_ref = refs
        else:
            x_ref, g_ref, dy_ref, dg_ref = refs
        _, vjp = jax.vjp(_rms, x_ref[...], g_ref[...])
        dx, dg = vjp(dy_ref[...])

        @pl.when(pl.program_id(0) == 0)
        def _():
            dg_ref[...] = jnp.zeros_like(dg_ref)

        dg_ref[...] += dg
        if want_dx:
            dx_ref[...] = r_ref[...] + dx

    ins = [x, g, dy] + ([resid] if want_dx else [])
    in_specs = [_rowblk(tr, dm), _const((1, dm)), _rowblk(tr, dm)] + ([_rowblk(tr, dm)] if want_dx else [])
    out_specs = ([_rowblk(tr, dm)] if want_dx else []) + [_const((1, dm))]
    out_shape = ([SDS((rows, dm), F32)] if want_dx else []) + [SDS((1, dm), F32)]
    return pl.pallas_call(
        body, grid=(rows // tr,), in_specs=in_specs, out_specs=out_specs, out_shape=out_shape, name=name,
        compiler_params=_params(("arbitrary",)))(*ins)


class _Comm:
    def __init__(self, ins, out_shapes, sems, start, finish):
        self.ins, self.out_shapes, self.sems, self.start, self.finish = list(ins), list(out_shapes), list(sems), start, finish


def _run_comm(comm, name):
    n_in, n_out = len(comm.ins), len(comm.out_shapes)

    def body(*refs):
        ins, outs, sems = refs[:n_in], refs[n_in:n_in + n_out], refs[n_in + n_out:]
        comm.start(ins, outs, sems)
        comm.finish(ins, outs, sems)

    hbm = pl.BlockSpec(memory_space=pl.ANY)
    return pl.pallas_call(body, in_specs=[hbm] * n_in, out_specs=[hbm] * n_out, out_shape=comm.out_shapes,
                          scratch_shapes=comm.sems, name=name)(*comm.ins)


def _matmul(a, b, *, dims, out_dtype, tm, tn, tk, name, add=None, comms=(), vmem=VMEM_LIMIT):
    a_list = list(a) if isinstance(a, (list, tuple)) else [a]
    b_list = list(b) if isinstance(b, (list, tuple)) else [b]
    assert len(a_list) == 1 or dims == "nt"
    assert len(b_list) == 1 or dims == "tn"
    if dims == "tn":
        kdim, m = a_list[0].shape
    else:
        m, kdim = a_list[0].shape[0], sum(p.shape[1] for p in a_list)
    n = b_list[0].shape[0] if dims == "nt" else sum(p.shape[1] for p in b_list)
    tm, tn, tk = min(tm, m), min(tn, n), min(tk, kdim)
    assert m % tm == 0 and n % tn == 0 and kdim % tk == 0, (name, m, n, kdim)
    ni, nj, nk = m // tm, n // tn, kdim // tk
    a_rng, b_rng, pos = [], [], 0
    for p in a_list:
        assert len(a_list) == 1 or p.shape[1] % tk == 0
        a_rng.append((pos, p.shape[1] // tk if len(a_list) > 1 else nk))
        pos += a_rng[-1][1]
    pos = 0
    for p in b_list:
        assert len(b_list) == 1 or p.shape[1] % tn == 0
        b_rng.append((pos, p.shape[1] // tn if len(b_list) > 1 else nj))
        pos += b_rng[-1][1]
    has_add = add is not None
    n_mm_in = len(a_list) + len(b_list) + (1 if has_add else 0)
    c_in = [len(c.ins) for c in comms]
    c_out = [len(c.out_shapes) for c in comms]
    c_sem = [len(c.sems) for c in comms]

    def body(*refs):
        a_refs, b_refs = refs[:len(a_list)], refs[len(a_list):len(a_list) + len(b_list)]
        add_ref = refs[n_mm_in - 1] if has_add else None
        pos = n_mm_in
        cin = []
        for cnt in c_in:
            cin.append(refs[pos:pos + cnt])
            pos += cnt
        o_ref = refs[pos]
        pos += 1
        cout = []
        for cnt in c_out:
            cout.append(refs[pos:pos + cnt])
            pos += cnt
        acc = refs[pos]
        pos += 1
        csem = []
        for cnt in c_sem:
            csem.append(refs[pos:pos + cnt])
            pos += cnt
        i, j, k = pl.program_id(0), pl.program_id(1), pl.program_id(2)

        if comms:
            @pl.when((i == 0) & (j == 0) & (k == 0))
            def _():
                for c, ci, co, cs in zip(comms, cin, cout, csem):
                    c.start(ci, co, cs)

        def accumulate(a_ref, b_ref, first_k, later_k):
            if dims == "nn":
                part = _dot(a_ref[...], b_ref[...])
            elif dims == "nt":
                part = _dot_nt(a_ref[...], b_ref[...])
            else:
                part = _dot_tn(a_ref[...], b_ref[...])

            if first_k:
                @pl.when(k == 0)
                def _():
                    acc[...] = part + add_ref[...] if has_add else part

            if later_k:
                @pl.when(k > 0)
                def _():
                    acc[...] += part

        if len(a_list) > 1:
            for a_ref, (k0, cnt) in zip(a_refs, a_rng):
                @pl.when((k >= k0) & (k < k0 + cnt))
                def _(a_ref=a_ref, k0=k0, cnt=cnt):
                    accumulate(a_ref, b_refs[0], k0 == 0, k0 + cnt > 1)
        elif len(b_list) > 1:
            for b_ref, (j0, cnt) in zip(b_refs, b_rng):
                @pl.when((j >= j0) & (j < j0 + cnt))
                def _(b_ref=b_ref):
                    accumulate(a_refs[0], b_ref, True, nk > 1)
        else:
            accumulate(a_refs[0], b_refs[0], True, nk > 1)

        @pl.when(k == nk - 1)
        def _():
            o_ref[...] = acc[...].astype(out_dtype)

        if comms:
            @pl.when((i == ni - 1) & (j == nj - 1) & (k == nk - 1))
            def _():
                for c, ci, co, cs in zip(comms, cin, cout, csem):
                    c.finish(ci, co, cs)

    def a_spec(k0, cnt):
        if dims == "tn":
            return pl.BlockSpec((tk, tm), lambda i, j, k: (k, i))
        return pl.BlockSpec((tm, tk), lambda i, j, k: (i, jnp.clip(k - k0, 0, cnt - 1)))

    def b_spec(j0, cnt):
        if dims == "nt":
            return pl.BlockSpec((tn, tk), lambda i, j, k: (j, k))
        return pl.BlockSpec((tk, tn), lambda i, j, k: (k, jnp.clip(j - j0, 0, cnt - 1)))

    o_spec = pl.BlockSpec((tm, tn), lambda i, j, k: (i, j))
    hbm = pl.BlockSpec(memory_space=pl.ANY)
    ins = a_list + b_list + ([add] if has_add else []) + [x for c in comms for x in c.ins]
    in_specs = ([a_spec(*r) for r in a_rng] + [b_spec(*r) for r in b_rng] + ([o_spec] if has_add else [])
                + [hbm] * sum(c_in))
    out_specs = [o_spec] + [hbm] * sum(c_out)
    out_shape = [SDS((m, n), out_dtype)] + [s for c in comms for s in c.out_shapes]
    scratch = [pltpu.VMEM((tm, tn), F32)] + [s for c in comms for s in c.sems]
    sem = ("arbitrary",) * 3 if comms else ("parallel", "parallel", "arbitrary")
    res = pl.pallas_call(
        body, grid=(ni, nj, nk), in_specs=in_specs, out_specs=out_specs, out_shape=out_shape, scratch_shapes=scratch,
        name=name, compiler_params=_params(sem, vmem))(*ins)
    if not comms:
        return res[0]
    outs, pos = [], 1
    for cnt in c_out:
        outs.append(list(res[pos:pos + cnt]))
        pos += cnt
    return res[0], outs


def _swa_specs(nb):
    prev = lambda off: pl.BlockSpec((pl.Element(WINDOW), pl.Element(1024)),
                                    lambda n: (jnp.maximum(n - 1, 0) * WINDOW, off))
    return [_win(WINDOW, 1024, QA), _win(WINDOW, 1024, VA), prev(QA), prev(VA),
            _const((1, 768)), _const((1, 256)), _const((1, LANE)), _const((768, 768)),
            _const((A_HEADS // A_GROUP, 2 * WINDOW, A_GROUP * WINDOW))]


def _swa_fwd(proj, qg, kg, sinks, bd, bias):
    s = proj.shape[0]
    nb = s // WINDOW

    def body(qk_ref, vz_ref, qkp_ref, vzp_ref, qg_ref, kg_ref, sk_ref, bd_ref, bias_ref, o_ref):
        first = pl.program_id(0) == 0
        o_ref[...] = _swa_fn(qk_ref[...], vz_ref[...], qkp_ref[...], vzp_ref[...], qg_ref[...], kg_ref[...],
                             sk_ref[...], bd_ref[...], bias_ref[...], first).astype(BF16)

    return pl.pallas_call(
        body, grid=(nb,), in_specs=_swa_specs(nb), out_specs=_rowblk(WINDOW, 768),
        out_shape=SDS((s, 768), BF16), name="swa_fwd",
        compiler_params=_params(("parallel",)))(proj, proj, proj, proj, qg, kg, sinks, bd, bias)


def _swa_bwd(proj, qg, kg, sinks, bd, bias, dga):
    s = proj.shape[0]
    nb = s // WINDOW

    def body(qk_ref, vz_ref, qkp_ref, vzp_ref, qg_ref, kg_ref, sk_ref, bd_ref, bias_ref, dg_ref,
             dcur_ref, dprev_ref, dqg_ref, dkg_ref, dsk_ref):
        first = pl.program_id(0) == 0
        bd_v = bd_ref[...]
        bias_v = bias_ref[...]
        fn = lambda qk, vz, qkp, vzp, qg_, kg_, sk: _swa_fn(qk, vz, qkp, vzp, qg_, kg_, sk, bd_v, bias_v, first)
        _, vjp = jax.vjp(fn, qk_ref[...], vz_ref[...], qkp_ref[...], vzp_ref[...], qg_ref[...], kg_ref[...], sk_ref[...])
        dqk, dvz, dqkp, dvzp, dqg, dkg, dsk = vjp(dg_ref[...])

        @pl.when(first)
        def _():
            dqg_ref[...] = jnp.zeros_like(dqg_ref)
            dkg_ref[...] = jnp.zeros_like(dkg_ref)
            dsk_ref[...] = jnp.zeros_like(dsk_ref)

        dqg_ref[...] += dqg
        dkg_ref[...] += dkg
        dsk_ref[...] += dsk
        dcur_ref[...] = jnp.concatenate([dqk, dvz], axis=1)
        dprev_ref[...] = jnp.concatenate([dqkp[:, 768:], dvzp[:, :256]], axis=1)

    return pl.pallas_call(
        body, grid=(nb,), in_specs=_swa_specs(nb) + [_rowblk(WINDOW, 768)],
        out_specs=[_rowblk(WINDOW, 2048), pl.BlockSpec((None, WINDOW, 512), lambda n: (n, 0, 0)),
                   _const((1, 768)), _const((1, 256)), _const((1, LANE))],
        out_shape=[SDS((s, 2048), F32), SDS((nb, WINDOW, 512), F32), SDS((1, 768), F32), SDS((1, 256), F32),
                   SDS((1, LANE), F32)],
        name="swa_bwd", compiler_params=_params(("arbitrary",)))(proj, proj, proj, proj, qg, kg, sinks, bd, bias, dga)


def _swa_combine(dcur, dprev):
    s = dcur.shape[0]
    nb = s // WINDOW

    def body(c_ref, p_ref, o_ref):
        c = c_ref[...]
        nxt = jnp.where(pl.program_id(0) == nb - 1, 0.0, p_ref[...])
        o_ref[...] = jnp.concatenate([c[:, :768], c[:, 768:1280] + nxt, c[:, 1280:]], axis=1).astype(BF16)

    return pl.pallas_call(
        body, grid=(nb,),
        in_specs=[_rowblk(WINDOW, 2048), pl.BlockSpec((None, WINDOW, 512), lambda n: (jnp.minimum(n + 1, nb - 1), 0, 0))],
        out_specs=_rowblk(WINDOW, 2048), out_shape=SDS((s, 2048), BF16), name="swa_combine",
        compiler_params=_params(("parallel",)))(dcur, dprev)


def _mem_fwd(proj, mkv, qg, kg, bd, *, tr):
    s = proj.shape[0]

    def body(qz_ref, mkv_ref, qg_ref, kg_ref, bd_ref, o_ref):
        o_ref[...] = _mem_fn(qz_ref[...], mkv_ref[...], qg_ref[...], kg_ref[...], bd_ref[...]).astype(BF16)

    return pl.pallas_call(
        body, grid=(s // tr,),
        in_specs=[_win(tr, 1024, QC), _const(mkv.shape), _const((1, 512)), _const((1, 512)), _const((512, 512))],
        out_specs=_rowblk(tr, 512), out_shape=SDS((s, 512), BF16), name="mem_fwd",
        compiler_params=_params(("parallel",)))(proj, mkv, qg, kg, bd)


def _mem_bwd(proj, mkv, qg, kg, bd, dgc, *, tr):
    s = proj.shape[0]

    def body(qz_ref, mkv_ref, qg_ref, kg_ref, bd_ref, dg_ref, dqz_ref, dmkv_ref, dqg_ref, dkg_ref):
        bd_v = bd_ref[...]
        fn = lambda qz, mkv_, qg_, kg_: _mem_fn(qz, mkv_, qg_, kg_, bd_v)
        _, vjp = jax.vjp(fn, qz_ref[...], mkv_ref[...], qg_ref[...], kg_ref[...])
        dqz, dmkv, dqg, dkg = vjp(dg_ref[...])

        @pl.when(pl.program_id(0) == 0)
        def _():
            dmkv_ref[...] = jnp.zeros_like(dmkv_ref)
            dqg_ref[...] = jnp.zeros_like(dqg_ref)
            dkg_ref[...] = jnp.zeros_like(dkg_ref)

        dmkv_ref[...] += dmkv
        dqg_ref[...] += dqg
        dkg_ref[...] += dkg
        dqz_ref[...] = dqz.astype(BF16)

    return pl.pallas_call(
        body, grid=(s // tr,),
        in_specs=[_win(tr, 1024, QC), _const(mkv.shape), _const((1, 512)), _const((1, 512)), _const((512, 512)),
                  _rowblk(tr, 512)],
        out_specs=[_rowblk(tr, 1024), _const(mkv.shape), _const((1, 512)), _const((1, 512))],
        out_shape=[SDS((s, 1024), BF16), SDS(mkv.shape, F32), SDS((1, 512), F32), SDS((1, 512), F32)],
        name="mem_bwd", compiler_params=_params(("arbitrary",)))(proj, mkv, qg, kg, bd, dgc)


def _log_sigmoid(x):
    return jnp.minimum(x, 0.0) - jnp.log1p(jnp.exp(-jnp.abs(x)))


FOX_TQ, FOX_TK = 512, 512
FOX_FWD_TQ, FOX_FWD_TK = 512, 1024


def _fox_tiles(s):
    return min(FOX_TQ, s), min(FOX_TK, s)


AUG = 128 * B_HEADS
COL_A, COL_B = 64, 67


def _split3(c):
    hi = c.astype(BF16)
    r1 = c - hi.astype(F32)
    mid = r1.astype(BF16)
    lo = (r1 - mid.astype(F32)).astype(BF16)
    return hi, mid, lo


def _expand_mats():
    def mat(col0):
        e = np.zeros((768 + 3 * LANE, AUG), np.float32)
        for h in range(B_HEADS):
            for d in range(HEAD_DIM):
                e[64 * h + d, 128 * h + d] = 1.0
            for part in range(3):
                e[768 + LANE * part + h, 128 * h + col0 + part] = 1.0
        return e

    def ones(col0):
        o = np.zeros((1, AUG), np.float32)
        for h in range(B_HEADS):
            o[0, 128 * h + col0:128 * h + col0 + 3] = 1.0
        return o

    return (jnp.asarray(mat(COL_A), BF16), jnp.asarray(mat(COL_B), BF16), jnp.asarray(ones(COL_A)), jnp.asarray(ones(COL_B)))


def _augment(data_bf16, triple, emat, ones_row):
    parts = [data_bf16] + (list(triple) if triple is not None else [jnp.zeros((data_bf16.shape[0], LANE), BF16)] * 3)
    wide = _dot(jnp.concatenate(parts, axis=1), emat)
    if ones_row is not None:
        wide = wide + ones_row
    return wide


def _compact(wide):
    return jnp.concatenate([wide[:, 128 * h:128 * h + 64] for h in range(wide.shape[1] // 128)], axis=1)


def _lane_of_heads(wide, col, first=0):
    rows = wide.shape[0]
    lane = lax.broadcasted_iota(jnp.int32, (rows, LANE), 1)
    out = jnp.zeros((rows, LANE), F32)
    for h in range(wide.shape[1] // 128):
        out = jnp.where(lane == first + h, wide[:, 128 * h + col:128 * h + col + 1], out)
    return out


def _fox2_prep(proj, fbl, qg, kg, bfor, bd, ea, eb, ones_a, ones_b, *, tr):
    s = proj.shape[0]
    tri = jnp.asarray(np.tril(np.ones((tr, tr), np.float32)))

    def body(q_ref, k_ref, v_ref, fb_ref, qg_ref, kg_ref, bf_ref, bd_ref, tri_ref, ea_ref, eb_ref, oa_ref, ob_ref,
             qat_ref, ka_ref, kat_ref, va_ref, vat_ref, qn_ref, c_ref, carry):
        @pl.when(pl.program_id(0) == 0)
        def _():
            carry[...] = jnp.zeros_like(carry)

        bd_v = bd_ref[...]
        lane = lax.broadcasted_iota(jnp.int32, (tr, LANE), 1)
        logf = jnp.where(lane < N_FORGET, _log_sigmoid(fb_ref[...] + bf_ref[...]), 0.0)
        c = jnp.dot(tri_ref[...], logf, precision=HI, preferred_element_type=F32) + carry[...]
        c_ref[...] = c
        carry[...] = c[tr - 1:tr, :]
        qn = _qn_fn(q_ref[...], qg_ref[...], bd_v).astype(BF16)
        kn = _kn_fn(k_ref[...], kg_ref[...], bd_v).astype(BF16)
        qn_ref[...] = qn
        qat_ref[...] = jnp.transpose(_augment(qn, _split3(c), ea_ref[...], ob_ref[...])).astype(BF16)
        ka = _augment(kn, _split3(-c), eb_ref[...], oa_ref[...])
        ka_ref[...] = ka.astype(BF16)
        kat_ref[...] = jnp.transpose(ka).astype(BF16)
        va = _augment(v_ref[...].astype(BF16), None, ea_ref[...], oa_ref[...])
        va_ref[...] = va.astype(BF16)
        vat_ref[...] = jnp.transpose(va).astype(BF16)

    emat = _const((768 + 3 * LANE, AUG))
    return pl.pallas_call(
        body, grid=(s // tr,),
        in_specs=[_win(tr, 768, QB), _win(tr, 768, KB), _win(tr, 768, VB), _rowblk(tr, LANE), _const((1, 768)),
                  _const((1, 768)), _const((1, LANE)), _const((768, 768)), _const((tr, tr)), emat, emat,
                  _const((1, AUG)), _const((1, AUG))],
        out_specs=[pl.BlockSpec((AUG, tr), lambda i: (0, i)), _rowblk(tr, AUG), pl.BlockSpec((AUG, tr), lambda i: (0, i)),
                   _rowblk(tr, AUG), pl.BlockSpec((AUG, tr), lambda i: (0, i)), _rowblk(tr, 768), _rowblk(tr, LANE)],
        out_shape=[SDS((AUG, s), BF16), SDS((s, AUG), BF16), SDS((AUG, s), BF16), SDS((s, AUG), BF16),
                   SDS((AUG, s), BF16), SDS((s, 768), BF16), SDS((s, LANE), F32)],
        scratch_shapes=[pltpu.VMEM((1, LANE), F32)], name="fox_prep",
        compiler_params=_params(("arbitrary",)))(proj, proj, proj, fbl, qg, kg, bfor, bd, tri, ea, eb, ones_a, ones_b)


def _fox2_fwd(proj, qat, ka, vat):
    s = proj.shape[0]
    tq, tk = min(FOX_FWD_TQ, s), min(FOX_FWD_TK, s)
    nq, nk = s // tq, s // tk

    def last_k(i):
        return (i * tq + tq - 1) // tk

    def body(qt_ref, k_ref, vt_ref, z_ref, gb_ref, yb_ref, lse_ref, acc, m_s):
        i, j = pl.program_id(0), pl.program_id(1)

        @pl.when(j == 0)
        def _():
            acc[...] = jnp.zeros_like(acc)
            m_s[...] = jnp.full_like(m_s, NEG)

        def tile(masked):
            if masked:
                kpos = j * tk + lax.broadcasted_iota(jnp.int32, (tk, tq), 0)
                qpos = i * tq + lax.broadcasted_iota(jnp.int32, (tk, tq), 1)
                mask = kpos <= qpos
            for h in range(B_HEADS):
                sl = slice(128 * h, 128 * h + 128)
                sc = _dot(k_ref[:, sl], qt_ref[sl, :])
                if masked:
                    sc = jnp.where(mask, sc, NEG)
                m_prev = m_s[h:h + 1, :]
                m_new = jnp.maximum(m_prev, jnp.max(sc, axis=0, keepdims=True))
                p = jnp.exp(sc - m_new).astype(BF16)
                acc[sl, :] = jnp.exp(m_prev - m_new) * acc[sl, :] + _dot(vt_ref[sl, :], p)
                m_s[h:h + 1, :] = m_new

        full = j * tk + tk - 1 <= i * tq

        @pl.when(full)
        def _():
            tile(False)

        @pl.when(jnp.logical_and(jnp.logical_not(full), j <= last_k(i)))
        def _():
            tile(True)

        @pl.when(j == nk - 1)
        def _():
            outs = []
            row = lax.broadcasted_iota(jnp.int32, (LANE, tq), 0)
            lse_t = jnp.zeros((LANE, tq), F32)
            for h in range(B_HEADS):
                l_row = acc[128 * h + COL_A:128 * h + COL_A + 1, :]
                outs.append(acc[128 * h:128 * h + 64, :] * (1.0 / l_row))
                lse_t = jnp.where(row == h, m_s[h:h + 1, :] + jnp.log(l_row), lse_t)
            y = jnp.transpose(jnp.concatenate(outs, axis=0))
            yb_ref[...] = y
            gb_ref[...] = (y * _silu(z_ref[...])).astype(BF16)
            lse_ref[...] = jnp.transpose(lse_t)

    kcol = lambda i, j: (0, jnp.minimum(j, last_k(i)))
    return pl.pallas_call(
        body, grid=(nq, nk),
        in_specs=[pl.BlockSpec((AUG, tq), lambda i, j: (0, i)),
                  pl.BlockSpec((tk, AUG), lambda i, j: (jnp.minimum(j, last_k(i)), 0)),
                  pl.BlockSpec((AUG, tk), kcol),
                  pl.BlockSpec((pl.Element(tq), pl.Element(768)), lambda i, j: (i * tq, ZB))],
        out_specs=[pl.BlockSpec((tq, 768), lambda i, j: (i, 0)), pl.BlockSpec((tq, 768), lambda i, j: (i, 0)),
                   pl.BlockSpec((tq, LANE), lambda i, j: (i, 0))],
        out_shape=[SDS((s, 768), BF16), SDS((s, 768), F32), SDS((s, LANE), F32)],
        scratch_shapes=[pltpu.VMEM((AUG, tq), F32), pltpu.VMEM((16, tq), F32)],
        name="fox_fwd", compiler_params=_params(("parallel", "arbitrary")))(qat, ka, vat, proj)


def _fox2_bwd_pre(proj, yb, dgb, qn, c, lse, hsum, ea, ones_b, *, tr):
    s = proj.shape[0]

    def body(z_ref, y_ref, dg_ref, qn_ref, c_ref, lse_ref, hs_ref, ea_ref, ob_ref,
             qa_ref, qat_ref, dya_ref, dyat_ref, dz_ref):
        z, y, dg = z_ref[...], y_ref[...], dg_ref[...]
        sg = jax.nn.sigmoid(z)
        dy = dg * (z * sg)
        dz_ref[...] = (dg * y * (sg * (1.0 + z * (1.0 - sg)))).astype(BF16)
        delta = jnp.dot(dy * y, hs_ref[...], precision=HI, preferred_element_type=F32)
        e = ea_ref[...]
        dya = _augment(dy.astype(BF16), _split3(-delta), e, None)
        dya_ref[...] = dya.astype(BF16)
        dyat_ref[...] = jnp.transpose(dya).astype(BF16)
        qa = _augment(qn_ref[...], _split3(c_ref[...] - lse_ref[...]), e, ob_ref[...])
        qa_ref[...] = qa.astype(BF16)
        qat_ref[...] = jnp.transpose(qa).astype(BF16)

    return pl.pallas_call(
        body, grid=(s // tr,),
        in_specs=[_win(tr, 768, ZB), _rowblk(tr, 768), _rowblk(tr, 768), _rowblk(tr, 768), _rowblk(tr, LANE),
                  _rowblk(tr, LANE), _const((768, LANE)), _const((768 + 3 * LANE, AUG)), _const((1, AUG))],
        out_specs=[_rowblk(tr, AUG), pl.BlockSpec((AUG, tr), lambda i: (0, i)), _rowblk(tr, AUG),
                   pl.BlockSpec((AUG, tr), lambda i: (0, i)), _rowblk(tr, 768)],
        out_shape=[SDS((s, AUG), BF16), SDS((AUG, s), BF16), SDS((s, AUG), BF16), SDS((AUG, s), BF16),
                   SDS((s, 768), BF16)], name="fox_bwd_pre",
        compiler_params=_params(("parallel",)))(proj, yb, dgb, qn, c, lse, hsum, ea, ones_b)


def _fox2_bwd(qb, qbt, ka, kat, va, dya, dyat):
    s = qb.shape[0]
    tq, tk = _fox_tiles(s)
    nq, nk = s // tq, s // tk
    ng = 2
    gh = B_HEADS // ng
    gw = 128 * gh

    def first_q(j):
        return (j * tk) // tq

    def body(q_ref, qt_ref, k_ref, kt_ref, v_ref, dy_ref, dyt_ref, dq_hbm, dk_ref, dv_ref, dck_ref,
             dq_acc, dk_acc, dv_acc, sem):
        g, j, i = pl.program_id(0), pl.program_id(1), pl.program_id(2)

        @pl.when((j == 0) & (i == 0))
        def _():
            dq_acc[...] = jnp.zeros_like(dq_acc)

        @pl.when(i == 0)
        def _():
            dk_acc[...] = jnp.zeros_like(dk_acc)
            dv_acc[...] = jnp.zeros_like(dv_acc)

        def tile(masked):
            if masked:
                kpos = j * tk + lax.broadcasted_iota(jnp.int32, (tk, tq), 0)
                qpos = i * tq + lax.broadcasted_iota(jnp.int32, (tk, tq), 1)
                mask = kpos <= qpos
            cols = pl.ds(pl.multiple_of(i * tq, tq), tq)
            for h in range(gh):
                sl = slice(128 * h, 128 * h + 128)
                sc = _dot(k_ref[:, sl], qt_ref[sl, :])
                if masked:
                    sc = jnp.where(mask, sc, NEG)
                p = jnp.exp(sc)
                ds = (p * _dot(v_ref[:, sl], dyt_ref[sl, :])).astype(BF16)
                dv_acc[:, sl] += _dot(p.astype(BF16), dy_ref[:, sl])
                dk_acc[:, sl] += _dot(ds, q_ref[:, sl])
                dq_acc[sl, cols] += _dot(kt_ref[sl, :], ds)

        full = j * tk + tk - 1 <= i * tq

        @pl.when(full)
        def _():
            tile(False)

        @pl.when(jnp.logical_and(jnp.logical_not(full), i >= first_q(j)))
        def _():
            tile(True)

        @pl.when(i == nq - 1)
        def _():
            dkw = dk_acc[...]
            dk_ref[...] = _compact(dkw)
            dv_ref[...] = _compact(dv_acc[...]).astype(BF16)
            dck_ref[...] = -_lane_of_heads(dkw, COL_B, gh * g)

        @pl.when((j == nk - 1) & (i == nq - 1))
        def _():
            cp = pltpu.make_async_copy(dq_acc, dq_hbm.at[pl.ds(pl.multiple_of(g * gw, gw), gw)], sem)
            cp.start()
            cp.wait()

    qrow = pl.BlockSpec((tq, gw), lambda g, j, i: (jnp.maximum(i, first_q(j)), g))
    qcol = pl.BlockSpec((gw, tq), lambda g, j, i: (g, jnp.maximum(i, first_q(j))))
    krow = pl.BlockSpec((tk, gw), lambda g, j, i: (j, g))
    kcol = pl.BlockSpec((gw, tk), lambda g, j, i: (g, j))
    kout = pl.BlockSpec((tk, gw // 2), lambda g, j, i: (j, g))
    return pl.pallas_call(
        body, grid=(ng, nk, nq),
        in_specs=[qrow, qcol, krow, kcol, krow, qrow, qcol],
        out_specs=[pl.BlockSpec(memory_space=pl.ANY), kout, kout,
                   pl.BlockSpec((None, tk, LANE), lambda g, j, i: (g, j, 0))],
        out_shape=[SDS((AUG, s), F32), SDS((s, 768), F32), SDS((s, 768), BF16), SDS((ng, s, LANE), F32)],
        scratch_shapes=[pltpu.VMEM((gw, s), F32), pltpu.VMEM((tk, gw), F32), pltpu.VMEM((tk, gw), F32),
                        pltpu.SemaphoreType.DMA],
        name="fox_bwd", compiler_params=_params(("arbitrary",) * 3))(qb, qbt, ka, kat, va, dya, dyat)


def _fox2_bwd_post(proj, fbl, qg, kg, bfor, bd, dqa, dkn, dck, *, tr):
    s = proj.shape[0]
    nb = s // tr
    triu = jnp.asarray(np.triu(np.ones((tr, tr), np.float32)))
    rev = lambda i: nb - 1 - i

    def body(q_ref, k_ref, fb_ref, qg_ref, kg_ref, bf_ref, bd_ref, tri_ref, dqa_ref, dkn_ref, dck_ref,
             dq_ref, dk_ref, dfb_ref, dqg_ref, dkg_ref, dbf_ref, carry):
        @pl.when(pl.program_id(0) == 0)
        def _():
            carry[...] = jnp.zeros_like(carry)
            dqg_ref[...] = jnp.zeros_like(dqg_ref)
            dkg_ref[...] = jnp.zeros_like(dkg_ref)
            dbf_ref[...] = jnp.zeros_like(dbf_ref)

        bd_v = bd_ref[...]
        dqw = jnp.transpose(dqa_ref[...])
        _, vjp_q = jax.vjp(lambda q, g: _qn_fn(q, g, bd_v), q_ref[...], qg_ref[...])
        dq, dqg = vjp_q(_compact(dqw))
        _, vjp_k = jax.vjp(lambda k, g: _kn_fn(k, g, bd_v), k_ref[...], kg_ref[...])
        dk, dkg = vjp_k(dkn_ref[...])
        dq_ref[...] = dq.astype(BF16)
        dk_ref[...] = dk.astype(BF16)
        dqg_ref[...] += dqg
        dkg_ref[...] += dkg

        dc = _lane_of_heads(dqw, COL_A) + (dck_ref[0] + dck_ref[1])
        dlogf = jnp.dot(tri_ref[...], dc, precision=HI, preferred_element_type=F32) + carry[...]
        carry[...] = dlogf[0:1, :]
        lane = lax.broadcasted_iota(jnp.int32, (tr, LANE), 1)
        xf = fb_ref[...] + bf_ref[...]
        dfb = jnp.where(lane < N_FORGET, dlogf * jax.nn.sigmoid(-xf), 0.0)
        dfb_ref[...] = dfb.astype(BF16)
        dbf_ref[...] += jnp.sum(dfb, axis=0, keepdims=True)

    rb = lambda w: pl.BlockSpec((tr, w), lambda i: (rev(i), 0))
    wn = lambda w, off: pl.BlockSpec((pl.Element(tr), pl.Element(w)), lambda i: (rev(i) * tr, off))
    return pl.pallas_call(
        body, grid=(nb,),
        in_specs=[wn(768, QB), wn(768, KB), rb(LANE), _const((1, 768)), _const((1, 768)), _const((1, LANE)),
                  _const((768, 768)), _const((tr, tr)), pl.BlockSpec((AUG, tr), lambda i: (0, rev(i))), rb(768),
                  pl.BlockSpec((2, tr, LANE), lambda i: (0, rev(i), 0))],
        out_specs=[rb(768), rb(768), rb(LANE), _const((1, 768)), _const((1, 768)), _const((1, LANE))],
        out_shape=[SDS((s, 768), BF16), SDS((s, 768), BF16), SDS((s, LANE), BF16), SDS((1, 768), F32),
                   SDS((1, 768), F32), SDS((1, LANE), F32)],
        scratch_shapes=[pltpu.VMEM((1, LANE), F32)], name="fox_bwd_post",
        compiler_params=_params(("arbitrary",)))(proj, proj, fbl, qg, kg, bfor, bd, triu, dqa, dkn, dck)


def _merge_specs(tr):
    row = lambda w: pl.BlockSpec((tr, w), lambda i, j: (i, 0))
    shard = lambda r: pl.BlockSpec((None, r, 512), lambda i, j: (j, 0, 0))
    gate = lambda b: pl.BlockSpec((tr, 512), lambda i, j: (i, (GATE + 2048 * b) // 512 + j))
    return [row(768), row(768), row(512), shard(768), shard(768), shard(512), gate(0), gate(1), gate(2)]


def _merge_fwd(proj, ga, gb, gc, wa, wb, wc, *, tr):
    s = proj.shape[0]

    def body(ga_ref, gb_ref, gc_ref, wa_ref, wb_ref, wc_ref, l0_ref, l1_ref, l2_ref, y_ref):
        ua = _dot(ga_ref[...], wa_ref[...])
        ub = _dot(gb_ref[...], wb_ref[...])
        uc = _dot(gc_ref[...], wc_ref[...])
        y = jax.nn.sigmoid(l0_ref[...]) * ua + jax.nn.sigmoid(l1_ref[...]) * ub + jax.nn.sigmoid(l2_ref[...]) * uc
        y_ref[...] = y.astype(BF16)

    return pl.pallas_call(
        body, grid=(s // tr, N_CHIPS), in_specs=_merge_specs(tr),
        out_specs=pl.BlockSpec((tr, 512), lambda i, j: (i, j)), out_shape=SDS((s, D_MODEL), BF16), name="merge_fwd",
        compiler_params=_params(("parallel", "arbitrary")))(ga, gb, gc, wa, wb, wc, proj, proj, proj)


def _merge_bwd(proj, ga, gb, gc, wa, wb, wc, dy, *, tr):
    s = proj.shape[0]

    def body(ga_ref, gb_ref, gc_ref, wa_ref, wb_ref, wc_ref, l0_ref, l1_ref, l2_ref, dy_ref,
             dl0_ref, dl1_ref, dl2_ref, dua_ref, dub_ref, duc_ref, dga_ref, dgb_ref, dgc_ref):
        j = pl.program_id(1)
        dyv = dy_ref[...]

        @pl.when(j == 0)
        def _():
            dga_ref[...] = jnp.zeros_like(dga_ref)
            dgb_ref[...] = jnp.zeros_like(dgb_ref)
            dgc_ref[...] = jnp.zeros_like(dgc_ref)

        for g_ref, w_ref, l_ref, dl_ref, du_ref, dg_ref in (
                (ga_ref, wa_ref, l0_ref, dl0_ref, dua_ref, dga_ref),
                (gb_ref, wb_ref, l1_ref, dl1_ref, dub_ref, dgb_ref),
                (gc_ref, wc_ref, l2_ref, dl2_ref, duc_ref, dgc_ref)):
            w = w_ref[...]
            u = _dot(g_ref[...], w)
            sg = jax.nn.sigmoid(l_ref[...])
            dl_ref[...] = (dyv * u * sg * (1.0 - sg)).astype(BF16)
            du = (dyv * sg).astype(BF16)
            du_ref[...] = du
            dg_ref[...] += _dot_nt(du, w)

    blk = pl.BlockSpec((tr, 512), lambda i, j: (i, j))
    row = lambda w: pl.BlockSpec((tr, w), lambda i, j: (i, 0))
    big = SDS((s, D_MODEL), BF16)
    return pl.pallas_call(
        body, grid=(s // tr, N_CHIPS), in_specs=_merge_specs(tr) + [blk],
        out_specs=[blk] * 6 + [row(768), row(768), row(512)],
        out_shape=[big] * 6 + [SDS((s, 768), F32), SDS((s, 768), F32), SDS((s, 512), F32)], name="merge_bwd",
        compiler_params=_params(("parallel", "arbitrary")))(ga, gb, gc, wa, wb, wc, proj, proj, proj, dy)


def _out_loss(y, wo, x, tgt, *, tr, tn):
    s = x.shape[0]

    def body(y_ref, w_ref, x_ref, t_ref, d_ref, db_ref, sq_ref):
        @pl.when((pl.program_id(0) == 0) & (pl.program_id(1) == 0))
        def _():
            sq_ref[...] = jnp.zeros_like(sq_ref)

        out = x_ref[...] + _dot(y_ref[...], w_ref[...])
        diff = out - t_ref[...]
        sq_ref[...] += jnp.sum(diff * diff, axis=0, keepdims=True)
        d = diff * (1.0 / D_MODEL)
        d_ref[...] = d
        db_ref[...] = d.astype(BF16)

    blk = pl.BlockSpec((tr, tn), lambda i, j: (i, j))
    return pl.pallas_call(
        body, grid=(s // tr, D_MODEL // tn),
        in_specs=[pl.BlockSpec((tr, D_MODEL), lambda i, j: (i, 0)), pl.BlockSpec((D_MODEL, tn), lambda i, j: (0, j)), blk, blk],
        out_specs=[blk, blk, _const((1, tn))],
        out_shape=[SDS((s, D_MODEL), F32), SDS((s, D_MODEL), BF16), SDS((1, tn), F32)], name="out_loss",
        compiler_params=_params(("arbitrary", "arbitrary")))(y, wo, x, tgt)


def _tile_gain(g, reps):
    return jnp.tile(g.reshape(1, -1), (1, reps))


def _pad_lane(v):
    v = v.reshape(1, -1)
    return jnp.pad(v, ((0, 0), (0, LANE - v.shape[1])))


def _local_step(x, mem, tgt, w_main, w_fb, w_small, norm_gain, mem_norm_gain, b_forget,
                q_gain_a, k_gain_a, sinks_a, q_gain_b, k_gain_b, q_gain_c, k_gain_c, core=None):
    s = x.shape[0]
    tr = min(512, s)
    bd64 = _block_diag(768, HEAD_DIM)
    bd128 = _block_diag(512, C_HEAD_DIM)
    hsum = _head_sum(768, HEAD_DIM)
    qga, kga = _tile_gain(q_gain_a, 12), _tile_gain(k_gain_a, 4)
    qgb, kgb = _tile_gain(q_gain_b, 12), _tile_gain(k_gain_b, 12)
    qgc, kgc = _tile_gain(q_gain_c, 4), _tile_gain(k_gain_c, 4)
    sinks = _pad_lane(sinks_a)
    bfor = _pad_lane(b_forget)

    hn = _rms_fwd(x, norm_gain, tr=tr, name="rms_x")
    on_mesh = core is not None
    if on_mesh:
        proj, (gathered,) = _matmul(hn, w_main, dims="nn", out_dtype=F32, tm=1024, tn=512, tk=D_MODEL, name="proj_main",
                                    comms=[_gather_comm(list(w_small))])
        w_mk, wa, wb, wc, wo = gathered
        w_mk, wo = w_mk.reshape(D_MODEL, 1024), wo.reshape(D_MODEL, D_MODEL)
    else:
        proj = _matmul(hn, w_main, dims="nn", out_dtype=F32, tm=1024, tn=512, tk=D_MODEL, name="proj_main")
        w_mk, wa, wb, wc, wo = w_small
    fbl = _matmul(hn, w_fb, dims="nn", out_dtype=F32, tm=1024, tn=LANE, tk=D_MODEL, name="proj_forget")
    memn = _rms_fwd(mem, mem_norm_gain, tr=mem.shape[0], name="rms_mem")
    mkv = _matmul(memn, w_mk, dims="nn", out_dtype=F32, tm=256, tn=512, tk=D_MODEL, name="mem_kv")

    swa_bias = _swa_bias()
    ga = _swa_fwd(proj, qga, kga, sinks, bd64, swa_bias)
    ea, eb, ones_a, ones_b = _expand_mats()
    tf = min(256, s)
    qat, ka, kat, va, vat, qn, cfox = _fox2_prep(proj, fbl, qgb, kgb, bfor, bd64, ea, eb, ones_a, ones_b, tr=tf)
    gb, yb, lse = _fox2_fwd(proj, qat, ka, vat)
    gc = _mem_fwd(proj, mkv, qgc, kgc, bd128, tr=tr)
    y = _merge_fwd(proj, ga, gb, gc, wa, wb, wc, tr=tr)
    dout, dout_b, sq = _out_loss(y, wo, x, tgt, tr=tr, tn=512)

    d_wo = _matmul(y, dout_b, dims="tn", out_dtype=F32, tm=1024, tn=512, tk=4096, name="dw_out")
    dy = _matmul(dout_b, wo, dims="nt", out_dtype=F32, tm=1024, tn=512, tk=D_MODEL, name="dy")
    dl0, dl1, dl2, dua, dub, duc, dga, dgb, dgc = _merge_bwd(proj, ga, gb, gc, wa, wb, wc, dy, tr=tr)
    d_wa = _matmul(ga, dua, dims="tn", out_dtype=F32, tm=768, tn=512, tk=4096, name="dw_branch_a")
    d_wb = _matmul(gb, dub, dims="tn", out_dtype=F32, tm=768, tn=512, tk=4096, name="dw_branch_b")
    d_wc = _matmul(gc, duc, dims="tn", out_dtype=F32, tm=512, tn=512, tk=4096, name="dw_branch_c")

    dcur, dprev, d_qga, d_kga, d_sinks = _swa_bwd(proj, qga, kga, sinks, bd64, swa_bias, dga)
    dproj_a = _swa_combine(dcur, dprev)

    qab, qabt, dya, dyat, dzb = _fox2_bwd_pre(proj, yb, dgb, qn, cfox, lse, hsum, ea, ones_b, tr=tf)
    dqa, dkn, dvb, dck = _fox2_bwd(qab, qabt, ka, kat, va, dya, dyat)
    dqb, dkb, dfb, d_qgb, d_kgb, d_bf = _fox2_bwd_post(proj, fbl, qgb, kgb, bfor, bd64, dqa, dkn, dck, tr=tf)

    dproj_c, dmkv, d_qgc, d_kgc = _mem_bwd(proj, mkv, qgc, kgc, bd128, dgc, tr=tr)
    dmkv_b = dmkv.astype(BF16)
    d_wmk = _matmul(memn, dmkv_b, dims="tn", out_dtype=F32, tm=1024, tn=512, tk=256, name="dw_mem_kv")
    dmemn = _matmul(dmkv_b, w_mk, dims="nt", out_dtype=F32, tm=256, tn=512, tk=1024, name="dmemn")
    (d_mem_gain,) = _rms_bwd(mem, mem_norm_gain, dmemn, None, tr=mem.shape[0], name="rms_mem_bwd")

    dproj = [dproj_a, jnp.concatenate([dqb, dkb, dvb, dzb, dproj_c], axis=1), dl0, dl1, dl2]
    dhn_f = _matmul(dfb, w_fb, dims="nt", out_dtype=F32, tm=1024, tn=512, tk=LANE, name="dhn_forget")
    d_wfb = _matmul(hn, dfb, dims="tn", out_dtype=F32, tm=1024, tn=LANE, tk=512, name="dw_forget")
    big = {}
    if on_mesh:
        half = D_MODEL // 2
        c0 = core[0]
        hn_other = lax.dynamic_slice(hn, (0, (1 - c0) * half), (s, half))
        hn_own = lax.dynamic_slice(hn, (0, c0 * half), (s, half))
        g1, k1 = [d_wmk, d_wa, d_wb, d_wc, d_wo], [2, 3, 4, 5, 6]
        d_other, (got1,) = _matmul(hn_other, dproj, dims="tn", out_dtype=F32, tm=1024, tn=512, tk=4096,
                                   name="dw_main_other", comms=[_exchange_comm(g1, k1)])
        h1 = [_add_half(g, got, core, HALF_AXIS[k], name=f"add_half_{k}") for g, got, k in zip(g1, got1, k1)]
        d_own, (got0, parts1) = _matmul(
            hn_own, dproj, dims="tn", out_dtype=F32, tm=1024, tn=512, tk=4096, name="dw_main_own",
            comms=[_exchange_comm([d_other, d_wfb], [0, 1], whole=(0,)), _scatter_comm(h1, k1)])
        h0 = [_add_pair(d_own, got0[0], name="add_pair_main"), _add_half(d_wfb, got0[1], core, 0, name="add_half_1")]
        sums1 = [_sum4(p, name=f"sum4_{k}") for p, k in zip(parts1, k1)]
        dhn, (parts0, theirs1) = _matmul(dproj, w_main, dims="nt", out_dtype=F32, tm=1024, tn=512, tk=2048, vmem=VMEM_WIDE, name="dhn",
                                         add=dhn_f, comms=[_scatter_comm(h0, [0, 1]), _swap_comm(sums1)])
        sums0 = [_sum4(p, name=f"sum4_{k}") for p, k in zip(parts0, (0, 1))]
        theirs0 = _run_comm(_swap_comm(sums0), "swap_halves")
        big = dict(sums=sums0 + sums1, theirs=list(theirs0) + list(theirs1))
    else:
        dhn = _matmul(dproj, w_main, dims="nt", out_dtype=F32, tm=1024, tn=512, tk=2048, vmem=VMEM_WIDE, name="dhn", add=dhn_f)
        d_wmain = _matmul(hn, dproj, dims="tn", out_dtype=F32, tm=1024, tn=512, tk=4096, name="dw_main")
        big = dict(d_wmain=d_wmain, d_wfb=d_wfb, d_wmk=d_wmk, d_wa=d_wa, d_wb=d_wb, d_wc=d_wc, d_wo=d_wo)
    grad_x, d_gain = _rms_bwd(x, norm_gain, dhn, dout, tr=tr, name="rms_x_bwd")

    fold = lambda g, reps: jnp.sum(g.reshape(reps, -1), axis=0, keepdims=True)
    return dict(
        sq=sq, grad_x=grad_x, **big,
        d_gain=d_gain, d_mem_gain=d_mem_gain, d_bf=d_bf[:, :N_FORGET],
        d_qga=fold(d_qga, 12), d_kga=fold(d_kga, 4), d_sinks=d_sinks[:, :A_HEADS],
        d_qgb=fold(d_qgb, 12), d_kgb=fold(d_kgb, 12), d_qgc=fold(d_qgc, 4), d_kgc=fold(d_kgc, 4))


PACK_ROWS = 256
FORGET_IN_SHARD = FORGET_COL - SHARD_COLS
AFTER_FORGET = FORGET_COL - SLAB_START[1]
END_CHIP1 = 2 * SHARD_COLS - N_FORGET - SLAB_START[1]


def _pack_w_in(chip, w):
    rows = w.shape[1]
    tr = PACK_ROWS

    def body(k_ref, w_ref, o_ref, scr):
        scr[...] = jnp.zeros_like(scr)
        scr[pl.ds(0, SHARD_COLS), :] = w_ref[...]
        v = jnp.transpose(scr[...])
        k = k_ref[0]
        col = lax.broadcasted_iota(jnp.int32, (tr, SLAB), 1)
        no_forget = jnp.zeros((tr, LANE), BF16)

        @pl.when(k == 0)
        def _():
            o_ref[:, 0:SLAB] = v.astype(BF16)
            o_ref[:, SLAB:] = no_forget

        @pl.when(k == 1)
        def _():
            before = pltpu.roll(v, SLAB_SHIFT[1], axis=1)
            after = pltpu.roll(v, SLAB - (N_FORGET - SLAB_SHIFT[1]), axis=1)
            slab = jnp.where(col < AFTER_FORGET, before, jnp.where(col < END_CHIP1, after, 0.0))
            o_ref[:, 0:SLAB] = slab.astype(BF16)
            f = pltpu.roll(v, SLAB - FORGET_IN_SHARD, axis=1)[:, :LANE]
            o_ref[:, SLAB:] = jnp.where(col[:, :LANE] < N_FORGET, f, 0.0).astype(BF16)

        for kk in (2, 3):
            @pl.when(k == kk)
            def _(kk=kk):
                o_ref[:, 0:SLAB] = pltpu.roll(v, SLAB_SHIFT[kk], axis=1).astype(BF16)
                o_ref[:, SLAB:] = no_forget

    return pl.pallas_call(
        body, grid_spec=pltpu.PrefetchScalarGridSpec(
            num_scalar_prefetch=1, grid=(rows // tr,),
            in_specs=[pl.BlockSpec((SHARD_COLS, tr), lambda i, k: (0, i))],
            out_specs=pl.BlockSpec((None, tr, SLAB + LANE), lambda i, k: (k[0], i, 0)),
            scratch_shapes=[pltpu.VMEM((SLAB, tr), F32)]),
        out_shape=SDS((N_CHIPS, rows, SLAB + LANE), BF16), name="pack_w_in",
        compiler_params=_params(("arbitrary",)))(chip, w)


def _merge_slabs(g):
    rows = g.shape[1]
    tr = PACK_ROWS
    t = [s // LANE for s in SLAB_START]
    n_t = SLAB // LANE

    def body(g_ref, m_ref, f_ref):
        for k in range(N_CHIPS):
            lo = t[k] + (1 if k > 0 else 0)
            hi = t[k + 1] if k + 1 < N_CHIPS else t[k] + n_t
            m_ref[:, lo * LANE:hi * LANE] = g_ref[k, :, (lo - t[k]) * LANE:(hi - t[k]) * LANE]
            if k + 1 < N_CHIPS:
                a = g_ref[k, :, (hi - t[k]) * LANE:(hi - t[k] + 1) * LANE].astype(F32)
                b = g_ref[k + 1, :, 0:LANE].astype(F32)
                m_ref[:, hi * LANE:(hi + 1) * LANE] = (a + b).astype(BF16)
        f_ref[...] = g_ref[1, :, SLAB:]

    return pl.pallas_call(
        body, grid=(rows // tr,),
        in_specs=[pl.BlockSpec((N_CHIPS, tr, SLAB + LANE), lambda i: (0, i, 0))],
        out_specs=[_rowblk(tr, P_MAIN), _rowblk(tr, LANE)],
        out_shape=[SDS((rows, P_MAIN), BF16), SDS((rows, LANE), BF16)], name="merge_slabs",
        compiler_params=_params(("parallel",)))(g)


def _adamw_math(w, g, m, v):
    nm = ADAM_B1 * m + (1.0 - ADAM_B1) * g
    nv = ADAM_B2 * v + (1.0 - ADAM_B2) * (g * g)
    m_hat = nm / (1.0 - ADAM_B1 ** ADAM_STEP)
    v_hat = nv / (1.0 - ADAM_B2 ** ADAM_STEP)
    delta = -ADAM_LR * (m_hat / (jnp.sqrt(v_hat) + ADAM_EPS) + ADAM_WD * w)
    return delta, nm, nv


def _adamw(g, w, m, v, *, tr, name):
    rows, cols = w.shape
    tr = min(tr, rows)

    def body(g_ref, w_ref, m_ref, v_ref, d_ref, nm_ref, nv_ref):
        d, nm, nv = _adamw_math(w_ref[...], g_ref[...], m_ref[...], v_ref[...])
        d_ref[...] = d
        nm_ref[...] = nm
        nv_ref[...] = nv

    spec = _rowblk(tr, cols)
    return pl.pallas_call(
        body, grid=(rows // tr,), in_specs=[spec] * 4, out_specs=[spec] * 3,
        out_shape=[SDS((rows, cols), F32)] * 3, name=name, compiler_params=_params(("parallel",)))(g, w, m, v)


def _adamw_w_in(chip_core, slab_mine, slab_theirs, forget_mine, forget_theirs, w, m, v):
    rows = w.shape[1]
    tr = PACK_ROWS // 2
    nbh = rows // 2 // tr

    def body(k_ref, sa_ref, sb_ref, fa_ref, fb_ref, w_ref, m_ref, v_ref, g_ref, d_ref, nm_ref, nv_ref):
        use_mine = pl.program_id(0) // nbh == k_ref[1]
        sl = jnp.where(use_mine, sa_ref[...], sb_ref[...])
        f_tile = jnp.where(use_mine, fa_ref[...], fb_ref[...])
        k = k_ref[0]

        def emit(wide):
            g = jnp.transpose(wide)[:SHARD_COLS, :]
            g_ref[...] = g
            d, nm, nv = _adamw_math(w_ref[...], g, m_ref[...], v_ref[...])
            d_ref[...] = d
            nm_ref[...] = nm
            nv_ref[...] = nv

        @pl.when(k == 0)
        def _():
            emit(sl)

        @pl.when(k == 1)
        def _():
            col = lax.broadcasted_iota(jnp.int32, (tr, SLAB), 1)
            before = pltpu.roll(sl, SLAB - SLAB_SHIFT[1], axis=1)
            after = pltpu.roll(sl, N_FORGET - SLAB_SHIFT[1], axis=1)
            wide_f = jnp.concatenate([f_tile, jnp.zeros((tr, SLAB - LANE), F32)], axis=1)
            forget = pltpu.roll(wide_f, FORGET_IN_SHARD, axis=1)
            emit(jnp.where(col < FORGET_IN_SHARD, before, jnp.where(col < FORGET_IN_SHARD + N_FORGET, forget, after)))

        for kk in (2, 3):
            @pl.when(k == kk)
            def _(kk=kk):
                emit(pltpu.roll(sl, SLAB - SLAB_SHIFT[kk], axis=1))

    nat = pl.BlockSpec((SHARD_COLS, tr), lambda i, k: (0, i))
    half = lambda width: pl.BlockSpec((tr, width), lambda i, k: (i % nbh, 0))
    return pl.pallas_call(
        body, grid_spec=pltpu.PrefetchScalarGridSpec(
            num_scalar_prefetch=1, grid=(rows // tr,),
            in_specs=[half(SLAB), half(SLAB), half(LANE), half(LANE), nat, nat, nat],
            out_specs=[nat] * 4),
        out_shape=[SDS((SHARD_COLS, rows), F32)] * 4, name="adamw_w_in",
        compiler_params=_params(("arbitrary",)))(chip_core, slab_mine, slab_theirs, forget_mine, forget_theirs, w, m, v)


ANY = pl.BlockSpec(memory_space=pl.ANY)
HALF_AXIS = (0, 0, 1, 0, 0, 0, 1)


def _me():
    return lax.axis_index("x"), lax.axis_index("y"), lax.axis_index("c")


def _half(ref, which, axis):
    n = ref.shape[axis] // 2
    sl = pl.ds(which * n, n)
    return ref.at[sl] if axis == 0 else ref.at[:, sl]


def _piece(t, ref, j):
    if t == 0:
        return ref.at[:, pl.ds(SLAB_START[j], SLAB)]
    if t == 1:
        return ref
    if t in (2, 6):
        return ref.at[pl.ds(512 * j, 512)]
    return ref.at[:, pl.ds(512 * j, 512)]


def _piece_shape(t, shape):
    if t == 0:
        return (shape[0], SLAB)
    if t == 1:
        return shape
    if t in (2, 6):
        return (512, shape[1])
    return (shape[0], 512)


def _gather_plan(ins, outs, own_slot_in_src):
    x, y, c = _me()
    k = 2 * x + y
    sib = (x, y, 1 - c)
    chips = [(1 - x, y), (x, 1 - y), (1 - x, 1 - y)]
    n = len(outs)

    def rows(t, which):
        h = outs[t].shape[1] // 2
        return pl.ds(which * h, h)

    def mine(t):
        return ins[t].at[k, rows(t, c)] if own_slot_in_src else ins[t].at[rows(t, c)]

    def first(t, j, sems):
        chip = chips[j]
        return pltpu.make_async_remote_copy(
            src_ref=mine(t), dst_ref=outs[t].at[k, rows(t, c)], send_sem=sems[0].at[t, j], recv_sem=sems[1].at[t, j],
            device_id=(chip[0], chip[1], c), device_id_type=MESH)

    def landed(t, j, sems):
        chip = chips[j]
        return pltpu.make_async_remote_copy(
            src_ref=mine(t), dst_ref=outs[t].at[2 * chip[0] + chip[1], rows(t, c)], send_sem=sems[0].at[t, j],
            recv_sem=sems[1].at[t, j], device_id=(chip[0], chip[1], c), device_id_type=MESH)

    def passed(t, j, which, sems):
        chip = chips[j]
        blk = outs[t].at[2 * chip[0] + chip[1], rows(t, which)]
        return pltpu.make_async_remote_copy(
            src_ref=blk, dst_ref=blk, send_sem=sems[2].at[t, j], recv_sem=sems[3].at[t, j], device_id=sib,
            device_id_type=MESH)

    def start(sems):
        for j in range(3):
            for t in range(n):
                first(t, j, sems).start()

    def finish(sems):
        for j in range(3):
            for t in range(n):
                landed(t, j, sems).wait_recv()
                passed(t, j, c, sems).start()
        for j in range(3):
            for t in range(n):
                passed(t, j, 1 - c, sems).wait_recv()
        for j in range(3):
            for t in range(n):
                first(t, j, sems).wait_send()
                passed(t, j, c, sems).wait_send()

    return k, start, finish


def _all_gather_slabs(slabs):
    def body(in_ref, out_ref, nbr_sem, quarter_sem, pass_sem):
        x, y, c = _me()
        k = 2 * x + y
        rows = out_ref.shape[1]
        h, q = rows // 2, rows // 4
        nbrs = [(1 - x, y), (x, 1 - y)]
        slot = lambda chip: 2 * chip[0] + chip[1]
        diag = 2 * (1 - x) + (1 - y)
        half = pl.ds(c * h, h)
        quarter = lambda a: pl.ds(c * h + a * q, q)

        def first(a):
            return pltpu.make_async_remote_copy(
                src_ref=in_ref.at[k, half], dst_ref=out_ref.at[k, half], send_sem=nbr_sem.at[0, a],
                recv_sem=nbr_sem.at[1, a], device_id=(nbrs[a][0], nbrs[a][1], c), device_id_type=MESH)

        def landed(a):
            blk = out_ref.at[slot(nbrs[a]), half]
            return pltpu.make_async_remote_copy(
                src_ref=blk, dst_ref=blk, send_sem=nbr_sem.at[0, a], recv_sem=nbr_sem.at[1, a],
                device_id=(nbrs[a][0], nbrs[a][1], c), device_id_type=MESH)

        def relay(a):
            blk = out_ref.at[slot(nbrs[a]), quarter(a)]
            to = nbrs[1 - a]
            return pltpu.make_async_remote_copy(
                src_ref=blk, dst_ref=blk, send_sem=quarter_sem.at[0, a], recv_sem=quarter_sem.at[1, a],
                device_id=(to[0], to[1], c), device_id_type=MESH)

        def relayed(a):
            blk = out_ref.at[diag, quarter(a)]
            frm = nbrs[1 - a]
            return pltpu.make_async_remote_copy(
                src_ref=blk, dst_ref=blk, send_sem=quarter_sem.at[0, a], recv_sem=quarter_sem.at[1, a],
                device_id=(frm[0], frm[1], c), device_id_type=MESH)

        def passed(j, which):
            sl = diag if j == 2 else slot(nbrs[j])
            blk = out_ref.at[sl, pl.ds(which * h, h)]
            return pltpu.make_async_remote_copy(
                src_ref=blk, dst_ref=blk, send_sem=pass_sem.at[0, j], recv_sem=pass_sem.at[1, j],
                device_id=(x, y, 1 - c), device_id_type=MESH)

        for a in range(2):
            first(a).start()
        for a in range(2):
            landed(a).wait_recv()
            relay(a).start()
            passed(a, c).start()
        for a in range(2):
            relayed(a).wait_recv()
        passed(2, c).start()
        for j in range(3):
            passed(j, 1 - c).wait_recv()
        for a in range(2):
            first(a).wait_send()
            relay(a).wait_send()
        for j in range(3):
            passed(j, c).wait_send()

    return pl.pallas_call(
        body, in_specs=[ANY], out_specs=ANY, out_shape=SDS(slabs.shape, slabs.dtype),
        scratch_shapes=[pltpu.SemaphoreType.DMA((2, 2)), pltpu.SemaphoreType.DMA((2, 2)), pltpu.SemaphoreType.DMA((2, 3))],
        input_output_aliases={0: 0}, name="all_gather_slabs")(slabs)


def _gather_comm(parts):
    n = len(parts)

    def start(ins, outs, sems):
        k, go, _ = _gather_plan(ins, outs, False)
        for t in range(n):
            pltpu.make_async_copy(ins[t], outs[t].at[k], sems[4].at[t]).start()
        go(sems)

    def finish(ins, outs, sems):
        k, _, done = _gather_plan(ins, outs, False)
        done(sems)
        for t in range(n):
            pltpu.make_async_copy(ins[t], outs[t].at[k], sems[4].at[t]).wait()

    return _Comm(parts, [SDS((N_CHIPS,) + p.shape, p.dtype) for p in parts],
                 [pltpu.SemaphoreType.DMA((n, 3))] * 4 + [pltpu.SemaphoreType.DMA((n,))], start, finish)


def _exchange_comm(arrs, kinds, whole=()):
    n = len(arrs)

    def copies(ins, outs, sems):
        x, y, c = _me()
        return [pltpu.make_async_remote_copy(
            src_ref=ins[t] if t in whole else _half(ins[t], 1 - c, HALF_AXIS[kinds[t]]), dst_ref=outs[t],
            send_sem=sems[0].at[t], recv_sem=sems[1].at[t], device_id=(x, y, 1 - c), device_id_type=MESH)
            for t in range(n)]

    def start(ins, outs, sems):
        for cp in copies(ins, outs, sems):
            cp.start()

    def finish(ins, outs, sems):
        for cp in copies(ins, outs, sems):
            cp.wait()

    def hshape(t):
        s = list(arrs[t].shape)
        if t not in whole:
            s[HALF_AXIS[kinds[t]]] //= 2
        return SDS(tuple(s), arrs[t].dtype)

    return _Comm(arrs, [hshape(t) for t in range(n)], [pltpu.SemaphoreType.DMA((n,))] * 2, start, finish)


def _add_half(full, got, core, axis, *, name):
    r, c = got.shape
    br, bc = (256 if r % 256 == 0 else 128), min(2048, c)
    off_r = (r // br) if axis == 0 else 0
    off_c = (c // bc) if axis == 1 else 0

    def body(c_ref, a_ref, b_ref, o_ref):
        o_ref[...] = (a_ref[...] + b_ref[...]).astype(BF16)

    return pl.pallas_call(
        body, grid_spec=pltpu.PrefetchScalarGridSpec(
            num_scalar_prefetch=1, grid=(r // br, c // bc),
            in_specs=[pl.BlockSpec((br, bc), lambda i, j, cr: (i + cr[0] * off_r, j + cr[0] * off_c)),
                      pl.BlockSpec((br, bc), lambda i, j, cr: (i, j))],
            out_specs=pl.BlockSpec((br, bc), lambda i, j, cr: (i, j))),
        out_shape=SDS((r, c), BF16), name=name, compiler_params=_params(("parallel", "parallel")))(core, full, got)


def _add_pair(a, b, *, name):
    r, c = a.shape
    br, bc = 256, min(2048, c)

    def body(a_ref, b_ref, o_ref):
        o_ref[...] = (a_ref[...] + b_ref[...]).astype(BF16)

    spec = pl.BlockSpec((br, bc), lambda i, j: (i, j))
    return pl.pallas_call(body, grid=(r // br, c // bc), in_specs=[spec, spec], out_specs=spec,
                          out_shape=SDS((r, c), BF16), name=name, compiler_params=_params(("parallel", "parallel")))(a, b)


def _scatter_comm(halves, kinds):
    n = len(halves)

    def plan(ins, outs, sems):
        send, recv, lsem = sems
        x, y, c = _me()
        k = 2 * x + y

        def to_chip(t, j):
            return pltpu.make_async_remote_copy(
                src_ref=_piece(kinds[t], ins[t], j), dst_ref=outs[t].at[k], send_sem=send.at[t, j],
                recv_sem=recv.at[t, k], device_id=(j // 2, j % 2, c), device_id_type=MESH)

        def from_chip(t, j):
            return pltpu.make_async_remote_copy(
                src_ref=_piece(kinds[t], ins[t], j), dst_ref=outs[t].at[j], send_sem=send.at[t, j],
                recv_sem=recv.at[t, j], device_id=(j // 2, j % 2, c), device_id_type=MESH)

        def own(t, j):
            return pltpu.make_async_copy(_piece(kinds[t], ins[t], j), outs[t].at[j], lsem.at[t])

        return k, to_chip, from_chip, own

    def start(ins, outs, sems):
        k, to_chip, _, own = plan(ins, outs, sems)
        for j in range(N_CHIPS):
            @pl.when(k != j)
            def _(j=j):
                for t in range(n):
                    to_chip(t, j).start()

            @pl.when(k == j)
            def _(j=j):
                for t in range(n):
                    own(t, j).start()

    def finish(ins, outs, sems):
        k, to_chip, from_chip, own = plan(ins, outs, sems)
        for j in range(N_CHIPS):
            @pl.when(k != j)
            def _(j=j):
                for t in range(n):
                    from_chip(t, j).wait_recv()
                for t in range(n):
                    to_chip(t, j).wait_send()

            @pl.when(k == j)
            def _(j=j):
                for t in range(n):
                    own(t, j).wait()

    return _Comm(halves, [SDS((N_CHIPS,) + _piece_shape(kinds[t], halves[t].shape), halves[t].dtype) for t in range(n)],
                 [pltpu.SemaphoreType.DMA((n, N_CHIPS))] * 2 + [pltpu.SemaphoreType.DMA((n,))], start, finish)


def _sum4(p, *, name):
    _, r, c = p.shape
    br = 256 if r % 256 == 0 else 128

    def body(p_ref, o_ref):
        o_ref[...] = ((p_ref[0].astype(F32) + p_ref[1].astype(F32)) + p_ref[2].astype(F32)) + p_ref[3].astype(F32)

    return pl.pallas_call(
        body, grid=(r // br,), in_specs=[pl.BlockSpec((N_CHIPS, br, c), lambda i: (0, i, 0))],
        out_specs=_rowblk(br, c), out_shape=SDS((r, c), F32), name=name, compiler_params=_params(("parallel",)))(p)


def _swap_comm(sums):
    return _exchange_comm(sums, [None] * len(sums), whole=tuple(range(len(sums))))


def _adamw_halves(mine, theirs, core, w, m, v, *, axis, tr, name):
    rows, cols = w.shape

    if axis == 0:
        nbh = rows // 2 // tr
        g_spec = pl.BlockSpec((tr, cols), lambda i, cr: (i % nbh, 0))
    else:
        g_spec = pl.BlockSpec((tr, cols // 2), lambda i, cr: (i, 0))

    def body(c_ref, a_ref, b_ref, w_ref, m_ref, v_ref, g_ref, d_ref, nm_ref, nv_ref):
        a, b = a_ref[...], b_ref[...]
        if axis == 0:
            g = jnp.where(pl.program_id(0) // nbh == c_ref[0], a, b)
        else:
            low = c_ref[0] == 0
            g = jnp.concatenate([jnp.where(low, a, b), jnp.where(low, b, a)], axis=1)
        g_ref[...] = g
        d, nm, nv = _adamw_math(w_ref[...], g, m_ref[...], v_ref[...])
        d_ref[...] = d
        nm_ref[...] = nm
        nv_ref[...] = nv

    nat = pl.BlockSpec((tr, cols), lambda i, cr: (i, 0))
    return pl.pallas_call(
        body, grid_spec=pltpu.PrefetchScalarGridSpec(
            num_scalar_prefetch=1, grid=(rows // tr,), in_specs=[g_spec, g_spec, nat, nat, nat], out_specs=[nat] * 4),
        out_shape=[SDS((rows, cols), F32)] * 4, name=name, compiler_params=_params(("arbitrary",)))(
            core, mine, theirs, w, m, v)


SMALL_ROWS, SMALL_COLS = 8, 1024


def _pack_small(vs):
    flat = jnp.concatenate([v.reshape(-1) for v in vs])
    return jnp.pad(flat, (0, SMALL_ROWS * SMALL_COLS - flat.shape[0])).reshape(SMALL_ROWS, SMALL_COLS)


def _unpack_small(packed, sizes):
    flat = packed.reshape(-1)
    out, o = [], 0
    for n in sizes:
        out.append(flat[o:o + n].reshape(1, n))
        o += n
    return out


def _all_reduce_small(v):
    n_dev = 8

    def body(v_ref, o_ref, land, send, recv):
        x, y, c = _me()
        me = 4 * x + 2 * y + c
        land[me] = v_ref[...]
        cps = []
        for r in range(1, n_dev):
            fx, fy, fc = (r >> 2) & 1, (r >> 1) & 1, r & 1
            peer = (x ^ fx, y ^ fy, c ^ fc)
            cps.append(pltpu.make_async_remote_copy(
                src_ref=v_ref, dst_ref=land.at[me], send_sem=send.at[r - 1], recv_sem=recv.at[r - 1],
                device_id=peer, device_id_type=MESH))
        for cp in cps:
            cp.start()
        for r in range(1, n_dev):
            fx, fy, fc = (r >> 2) & 1, (r >> 1) & 1, r & 1
            src = 4 * (x ^ fx) + 2 * (y ^ fy) + (c ^ fc)
            pltpu.make_async_remote_copy(
                src_ref=v_ref, dst_ref=land.at[src], send_sem=send.at[r - 1], recv_sem=recv.at[r - 1],
                device_id=(x ^ fx, y ^ fy, c ^ fc), device_id_type=MESH).wait_recv()
        for cp in cps:
            cp.wait_send()
        acc = land[0]
        for r in range(1, n_dev):
            acc = acc + land[r]
        o_ref[...] = acc

    vm = pl.BlockSpec(memory_space=pltpu.VMEM)
    return pl.pallas_call(
        body, in_specs=[vm], out_specs=vm, out_shape=SDS(v.shape, F32),
        scratch_shapes=[pltpu.VMEM((n_dev,) + v.shape, F32), pltpu.SemaphoreType.DMA((n_dev - 1,)),
                        pltpu.SemaphoreType.DMA((n_dev - 1,))],
        name="all_reduce_small")(v)


def kernel(x, mem, norm_gain, mem_norm_gain, w_in, b_forget, q_gain_a, k_gain_a, sinks_a, q_gain_b, k_gain_b, q_gain_c, k_gain_c, w_mem_kv, w_branch_a, w_branch_b, w_branch_c, w_out, loss_target, m_norm_gain, m_mem_norm_gain, m_w_in, m_b_forget, m_q_gain_a, m_k_gain_a, m_sinks_a, m_q_gain_b, m_k_gain_b, m_q_gain_c, m_k_gain_c, m_w_mem_kv, m_w_branch_a, m_w_branch_b, m_w_branch_c, m_w_out, v_norm_gain, v_mem_norm_gain, v_w_in, v_b_forget, v_q_gain_a, v_k_gain_a, v_sinks_a, v_q_gain_b, v_k_gain_b, v_q_gain_c, v_k_gain_c, v_w_mem_kv, v_w_branch_a, v_w_branch_b, v_w_branch_c, v_w_out):
    xi, yi, ci = lax.axis_index("x"), lax.axis_index("y"), lax.axis_index("c")
    chip = jnp.reshape(2 * xi + yi, (1,)).astype(jnp.int32)
    core = jnp.reshape(ci, (1,)).astype(jnp.int32)

    slabs = _pack_w_in(chip, jnp.transpose(w_in[0]))
    mine = [w_mem_kv[0].astype(BF16), w_branch_a[0].astype(BF16), w_branch_b[0].astype(BF16),
            w_branch_c[0].astype(BF16), w_out[0].astype(BF16)]
    w_main, w_fb = _merge_slabs(_all_gather_slabs(slabs))

    r = _local_step(x[0], mem[0], loss_target[0], w_main, w_fb, mine, norm_gain, mem_norm_gain,
                    b_forget, q_gain_a, k_gain_a, sinks_a, q_gain_b, k_gain_b, q_gain_c, k_gain_c, core=core)
    sums, theirs = r["sums"], r["theirs"]

    small_names = ["d_gain", "d_mem_gain", "d_bf", "d_qga", "d_kga", "d_sinks", "d_qgb", "d_kgb", "d_qgc", "d_kgc"]
    loss_part = (0.5 / D_MODEL) * jnp.sum(r["sq"], axis=1, keepdims=True)
    packed = _pack_small([r[n] for n in small_names] + [loss_part])
    red = _all_reduce_small(packed)
    small_w = [norm_gain, mem_norm_gain, b_forget, q_gain_a, k_gain_a, sinks_a, q_gain_b, k_gain_b, q_gain_c, k_gain_c]
    small_m = [m_norm_gain, m_mem_norm_gain, m_b_forget, m_q_gain_a, m_k_gain_a, m_sinks_a, m_q_gain_b, m_k_gain_b,
               m_q_gain_c, m_k_gain_c]
    small_v = [v_norm_gain, v_mem_norm_gain, v_b_forget, v_q_gain_a, v_k_gain_a, v_sinks_a, v_q_gain_b, v_k_gain_b,
               v_q_gain_c, v_k_gain_c]
    sizes = [w.shape[1] for w in small_w]
    s_d, s_m, s_v = _adamw(red, _pack_small(small_w), _pack_small(small_m), _pack_small(small_v), tr=8, name="adamw_small")
    g_small = _unpack_small(red, sizes + [1])
    loss = g_small[-1].reshape(())
    d_small, m_small, v_small = _unpack_small(s_d, sizes), _unpack_small(s_m, sizes), _unpack_small(s_v, sizes)

    gw_in, dw_in, mw_in, vw_in = _adamw_w_in(jnp.concatenate([chip, core]), sums[0], theirs[0], sums[1], theirs[1],
                                             jnp.transpose(w_in[0]), jnp.transpose(m_w_in[0]), jnp.transpose(v_w_in[0]))
    big = {}
    for t, nm, w, m, v in ((2, "w_mem_kv", w_mem_kv, m_w_mem_kv, v_w_mem_kv),
                           (3, "w_branch_a", w_branch_a, m_w_branch_a, v_w_branch_a),
                           (4, "w_branch_b", w_branch_b, m_w_branch_b, v_w_branch_b),
                           (5, "w_branch_c", w_branch_c, m_w_branch_c, v_w_branch_c),
                           (6, "w_out", w_out, m_w_out, v_w_out)):
        big[nm] = _adamw_halves(sums[t], theirs[t], core, w[0], m[0], v[0], axis=HALF_AXIS[t], tr=128,
                                name="adamw_" + nm)

    def collect(kind):
        sm = (g_small, d_small, m_small, v_small)[kind]
        win = (gw_in, dw_in, mw_in, vw_in)[kind]
        return ([sm[0], sm[1], jnp.transpose(win)[None]] + [a for a in sm[2:10]]
                + [big[n][kind][None] for n in ("w_mem_kv", "w_branch_a", "w_branch_b", "w_branch_c", "w_out")])

    return (loss, r["grad_x"][None], *collect(0), *collect(1), *collect(2), *collect(3))
```

```python
import functools

import numpy as np
import jax
import jax.numpy as jnp
from jax import lax
from jax.experimental import pallas as pl
from jax.experimental.pallas import tpu as pltpu

F32 = jnp.float32
BF16 = jnp.bfloat16
HI = lax.Precision.HIGHEST
SDS = jax.ShapeDtypeStruct
MESH = pl.DeviceIdType.MESH

D_MODEL = 2048
HEAD_DIM = 64
A_HEADS = 12
A_GROUP = 3
B_HEADS = 12
C_HEADS = 4
C_HEAD_DIM = 128
WINDOW = 128
EPS = 1e-6
NEG = -1e30
LANE = 128

QA, KA, VA, ZA = 0, 768, 1024, 1280
QB, KB, VB, ZB = 2048, 2816, 3584, 4352
QC, ZC = 5120, 5632
GATE = 6144
P_MAIN = 12288
N_FORGET = 12
FORGET_COL = 5120
SHARD_COLS = 3075
SLAB = 3200
SLAB_START = (0, 3072, 6016, 9088)
SLAB_SHIFT = (0, 3, 122, 125)
N_CHIPS = 4

ADAM_LR = 0.001
ADAM_B1 = 0.9
ADAM_B2 = 0.999
ADAM_EPS = 1e-08
ADAM_WD = 0.01
ADAM_STEP = 10

VMEM_LIMIT = 56 * 1024 * 1024
VMEM_WIDE = 62 * 1024 * 1024


def _params(sem, vmem=VMEM_LIMIT):
    return pltpu.CompilerParams(dimension_semantics=sem, vmem_limit_bytes=vmem)


def _win(tr, width, off):
    return pl.BlockSpec((pl.Element(tr), pl.Element(width)), lambda i, *_: (i * tr, off))


def _rowblk(tr, width):
    return pl.BlockSpec((tr, width), lambda i, *_: (i, 0))


def _const(shape):
    nd = len(shape)
    return pl.BlockSpec(shape, lambda *_: (0,) * nd)


def _rms(x, g):
    return x * lax.rsqrt(jnp.mean(x * x, axis=-1, keepdims=True) + EPS) * g


def _head_mean_impl(x2, bd):
    hi = x2.astype(BF16)
    lo = (x2 - hi.astype(F32)).astype(BF16)
    return _dot(hi, bd) + _dot(lo, bd)


@jax.custom_vjp
def _head_mean(x2, bd):
    return _head_mean_impl(x2, bd)


_head_mean.defvjp(lambda x2, bd: (_head_mean_impl(x2, bd), bd),
                  lambda bd, g: (_head_mean_impl(g, bd), jnp.zeros_like(bd)))


def _head_norm(x, g_tiled, bd):
    return x * lax.rsqrt(_head_mean(x * x, bd) + EPS) * g_tiled


def _silu(z):
    return z * jax.nn.sigmoid(z)


def _dot_nt(a, b):
    return lax.dot_general(a, b, (((1,), (1,)), ((), ())), preferred_element_type=F32)


def _dot_tn(a, b):
    return lax.dot_general(a, b, (((0,), (0,)), ((), ())), preferred_element_type=F32)


def _dot(a, b):
    return jnp.dot(a, b, preferred_element_type=F32)


def _swa_fn(qk, vz, qkp, vzp, qg, kg, sinks, bd, bias, first):
    q = _head_norm(qk[:, :768], qg, bd)
    k2 = jnp.concatenate([qkp[:, 768:], qk[:, 768:]], axis=0)
    k2 = _head_norm(k2, kg, bd[:256, :256])
    v2 = jnp.concatenate([vzp[:, :256], vz[:, :256]], axis=0)
    z = vz[:, 256:]
    cols = A_GROUP * WINDOW
    kj = lax.broadcasted_iota(jnp.int32, (2 * WINDOW, cols), 0)
    no_prev = kj < WINDOW * first.astype(jnp.int32)
    qtb = jnp.transpose(q).astype(BF16)
    kb = k2.astype(BF16)
    vtb = jnp.transpose(v2).astype(BF16)
    outs = [None] * A_HEADS
    for g in range(A_HEADS // A_GROUP):
        heads = [A_GROUP * g + u for u in range(A_GROUP)]
        qs = jnp.concatenate([qtb[64 * h:64 * h + 64, :] for h in heads], axis=1)
        s = _dot(kb[:, 64 * g:64 * g + 64], qs) * (HEAD_DIM ** -0.5) + bias[g]
        s = jnp.where(no_prev, NEG, s)
        sink = jnp.concatenate([jnp.broadcast_to(sinks[:, h:h + 1], (1, WINDOW)) for h in heads], axis=1)
        m = lax.stop_gradient(jnp.maximum(jnp.max(s, axis=0, keepdims=True), sink))
        p = jnp.exp(s - m)
        den = jnp.sum(p, axis=0, keepdims=True) + jnp.exp(sink - m)
        o = _dot(vtb[64 * g:64 * g + 64, :], (p * (1.0 / den)).astype(BF16))
        for u, h in enumerate(heads):
            outs[h] = o[:, WINDOW * u:WINDOW * u + WINDOW]
    return jnp.transpose(jnp.concatenate(outs, axis=0)) * _silu(z)


def _swa_bias():
    qi = np.arange(WINDOW)[None, :]
    kj = np.arange(2 * WINDOW)[:, None]
    rel = qi + WINDOW - kj
    valid = (rel >= 0) & (rel < WINDOW)
    out = np.zeros((A_HEADS // A_GROUP, 2 * WINDOW, A_GROUP * WINDOW), np.float32)
    for h in range(A_HEADS):
        slope = np.float32(2.0 ** (-8.0 * (h + 1) / A_HEADS))
        blk = np.where(valid, -slope * rel.astype(np.float32), np.float32(NEG))
        g, u = divmod(h, A_GROUP)
        out[g, :, WINDOW * u:WINDOW * u + WINDOW] = blk
    return jnp.asarray(out)


def _mem_fn(qz, mkv, qg, kg, bd):
    q = _head_norm(qz[:, :512], qg, bd).astype(BF16)
    k = _head_norm(mkv[:, :512], kg, bd).astype(BF16)
    v = mkv[:, 512:].astype(BF16)
    z = qz[:, 512:]
    outs = []
    for h in range(C_HEADS):
        sl = slice(128 * h, 128 * h + 128)
        s = _dot_nt(q[:, sl], k[:, sl]) * (C_HEAD_DIM ** -0.5)
        m = lax.stop_gradient(jnp.max(s, axis=-1, keepdims=True))
        p = jnp.exp(s - m)
        den = jnp.sum(p, axis=-1, keepdims=True)
        outs.append(_dot((p * (1.0 / den)).astype(BF16), v[:, sl]))
    return jnp.concatenate(outs, axis=1) * _silu(z)


def _qn_fn(q, g, bd):
    return _head_norm(q, g, bd) * (HEAD_DIM ** -0.5)


def _kn_fn(k, g, bd):
    return _head_norm(k, g, bd)


def _block_diag(width, hd):
    i = np.arange(width) // hd
    return jnp.asarray((i[:, None] == i[None, :]).astype(np.float32) / hd, BF16)


def _head_sum(width, hd):
    i = np.arange(width) // hd
    return jnp.asarray((i[:, None] == np.arange(LANE)[None, :]).astype(np.float32))


def _rms_fwd(x, g, *, tr, name):
    rows, dm = x.shape

    def body(x_ref, g_ref, o_ref):
        o_ref[...] = _rms(x_ref[...], g_ref[...]).astype(BF16)

    return pl.pallas_call(
        body, grid=(rows // tr,),
        in_specs=[_rowblk(tr, dm), _const((1, dm))],
        out_specs=_rowblk(tr, dm),
        out_shape=SDS((rows, dm), BF16), name=name,
        compiler_params=_params(("parallel",)))(x, g)


def _rms_bwd(x, g, dy, resid, *, tr, name, comm=None):
    rows, dm = x.shape
    want_dx = resid is not None
    n_in = 4 if want_dx else 3
    n_out = 2 if want_dx else 1
    c_in = len(comm.ins) if comm else 0
    c_out = len(comm.out_shapes) if comm else 0
    nb = rows // tr

    def body(*refs):
        x_ref, g_ref, dy_ref = refs[:3]
        r_ref = refs[3] if want_dx else None
        cin = refs[n_in:n_in + c_in]
        outs = refs[n_in + c_in:n_in + c_in + n_out]
        dg_ref = outs[-1]
        cout = refs[n_in + c_in + n_out:n_in + c_in + n_out + c_out]
        csem = refs[n_in + c_in + n_out + c_out:]

        if comm:
            @pl.when(pl.program_id(0) == 0)
            def _():
                comm.start(cin, cout, csem)

        _, vjp = jax.vjp(_rms, x_ref[...], g_ref[...])
        dx, dg = vjp(dy_ref[...])

        @pl.when(pl.program_id(0) == 0)
        def _():
            dg_ref[...] = jnp.zeros_like(dg_ref)

        dg_ref[...] += dg
        if want_dx:
            outs[0][...] = r_ref[...] + dx

        if comm:
            @pl.when(pl.program_id(0) == nb - 1)
            def _():
                comm.finish(cin, cout, csem)

    hbm = pl.BlockSpec(memory_space=pl.ANY)
    ins = [x, g, dy] + ([resid] if want_dx else []) + (list(comm.ins) if comm else [])
    in_specs = ([_rowblk(tr, dm), _const((1, dm)), _rowblk(tr, dm)] + ([_rowblk(tr, dm)] if want_dx else [])
                + [hbm] * c_in)
    out_specs = ([_rowblk(tr, dm)] if want_dx else []) + [_const((1, dm))] + [hbm] * c_out
    out_shape = (([SDS((rows, dm), F32)] if want_dx else []) + [SDS((1, dm), F32)]
                 + (list(comm.out_shapes) if comm else []))
    res = pl.pallas_call(
        body, grid=(nb,), in_specs=in_specs, out_specs=out_specs, out_shape=out_shape,
        scratch_shapes=list(comm.sems) if comm else [], name=name, compiler_params=_params(("arbitrary",)))(*ins)
    return (list(res[:n_out]), list(res[n_out:])) if comm else res


class _Comm:
    def __init__(self, ins, out_shapes, sems, start, finish):
        self.ins, self.out_shapes, self.sems, self.start, self.finish = list(ins), list(out_shapes), list(sems), start, finish


def _matmul(a, b, *, dims, out_dtype, tm, tn, tk, name, add=None, comms=(), vmem=VMEM_LIMIT):
    a_list = list(a) if isinstance(a, (list, tuple)) else [a]
    b_list = list(b) if isinstance(b, (list, tuple)) else [b]
    assert len(a_list) == 1 or dims == "nt"
    assert len(b_list) == 1 or dims == "tn"
    if dims == "tn":
        kdim, m = a_list[0].shape
    else:
        m, kdim = a_list[0].shape[0], sum(p.shape[1] for p in a_list)
    n = b_list[0].shape[0] if dims == "nt" else sum(p.shape[1] for p in b_list)
    tm, tn, tk = min(tm, m), min(tn, n), min(tk, kdim)
    assert m % tm == 0 and n % tn == 0 and kdim % tk == 0, (name, m, n, kdim)
    ni, nj, nk = m // tm, n // tn, kdim // tk
    a_rng, b_rng, pos = [], [], 0
    for p in a_list:
        assert len(a_list) == 1 or p.shape[1] % tk == 0
        a_rng.append((pos, p.shape[1] // tk if len(a_list) > 1 else nk))
        pos += a_rng[-1][1]
    pos = 0
    for p in b_list:
        assert len(b_list) == 1 or p.shape[1] % tn == 0
        b_rng.append((pos, p.shape[1] // tn if len(b_list) > 1 else nj))
        pos += b_rng[-1][1]
    has_add = add is not None
    n_mm_in = len(a_list) + len(b_list) + (1 if has_add else 0)
    c_in = [len(c.ins) for c in comms]
    c_out = [len(c.out_shapes) for c in comms]
    c_sem = [len(c.sems) for c in comms]

    def body(*refs):
        a_refs, b_refs = refs[:len(a_list)], refs[len(a_list):len(a_list) + len(b_list)]
        add_ref = refs[n_mm_in - 1] if has_add else None
        pos = n_mm_in
        cin = []
        for cnt in c_in:
            cin.append(refs[pos:pos + cnt])
            pos += cnt
        o_ref = refs[pos]
        pos += 1
        cout = []
        for cnt in c_out:
            cout.append(refs[pos:pos + cnt])
            pos += cnt
        acc = refs[pos]
        pos += 1
        csem = []
        for cnt in c_sem:
            csem.append(refs[pos:pos + cnt])
            pos += cnt
        i, j, k = pl.program_id(0), pl.program_id(1), pl.program_id(2)

        if comms:
            @pl.when((i == 0) & (j == 0) & (k == 0))
            def _():
                for c, ci, co, cs in zip(comms, cin, cout, csem):
                    c.start(ci, co, cs)

        def accumulate(a_ref, b_ref, first_k, later_k):
            if dims == "nn":
                part = _dot(a_ref[...], b_ref[...])
            elif dims == "nt":
                part = _dot_nt(a_ref[...], b_ref[...])
            else:
                part = _dot_tn(a_ref[...], b_ref[...])

            if first_k:
                @pl.when(k == 0)
                def _():
                    acc[...] = part + add_ref[...] if has_add else part

            if later_k:
                @pl.when(k > 0)
                def _():
                    acc[...] += part

        if len(a_list) > 1:
            for a_ref, (k0, cnt) in zip(a_refs, a_rng):
                @pl.when((k >= k0) & (k < k0 + cnt))
                def _(a_ref=a_ref, k0=k0, cnt=cnt):
                    accumulate(a_ref, b_refs[0], k0 == 0, k0 + cnt > 1)
        elif len(b_list) > 1:
            for b_ref, (j0, cnt) in zip(b_refs, b_rng):
                @pl.when((j >= j0) & (j < j0 + cnt))
                def _(b_ref=b_ref):
                    accumulate(a_refs[0], b_ref, True, nk > 1)
        else:
            accumulate(a_refs[0], b_refs[0], True, nk > 1)

        @pl.when(k == nk - 1)
        def _():
            o_ref[...] = acc[...].astype(out_dtype)

        if comms:
            @pl.when((i == ni - 1) & (j == nj - 1) & (k == nk - 1))
            def _():
                for c, ci, co, cs in zip(comms, cin, cout, csem):
                    c.finish(ci, co, cs)

    def a_spec(k0, cnt):
        if dims == "tn":
            return pl.BlockSpec((tk, tm), lambda i, j, k: (k, i))
        return pl.BlockSpec((tm, tk), lambda i, j, k: (i, jnp.clip(k - k0, 0, cnt - 1)))

    def b_spec(j0, cnt):
        if dims == "nt":
            return pl.BlockSpec((tn, tk), lambda i, j, k: (j, k))
        return pl.BlockSpec((tk, tn), lambda i, j, k: (k, jnp.clip(j - j0, 0, cnt - 1)))

    o_spec = pl.BlockSpec((tm, tn), lambda i, j, k: (i, j))
    hbm = pl.BlockSpec(memory_space=pl.ANY)
    ins = a_list + b_list + ([add] if has_add else []) + [x for c in comms for x in c.ins]
    in_specs = ([a_spec(*r) for r in a_rng] + [b_spec(*r) for r in b_rng] + ([o_spec] if has_add else [])
                + [hbm] * sum(c_in))
    out_specs = [o_spec] + [hbm] * sum(c_out)
    out_shape = [SDS((m, n), out_dtype)] + [s for c in comms for s in c.out_shapes]
    scratch = [pltpu.VMEM((tm, tn), F32)] + [s for c in comms for s in c.sems]
    sem = ("arbitrary",) * 3 if comms else ("parallel", "parallel", "arbitrary")
    res = pl.pallas_call(
        body, grid=(ni, nj, nk), in_specs=in_specs, out_specs=out_specs, out_shape=out_shape, scratch_shapes=scratch,
        name=name, compiler_params=_params(sem, vmem))(*ins)
    if not comms:
        return res[0]
    outs, pos = [], 1
    for cnt in c_out:
        outs.append(list(res[pos:pos + cnt]))
        pos += cnt
    return res[0], outs


def _swa_specs(nb):
    prev = lambda off: pl.BlockSpec((pl.Element(WINDOW), pl.Element(1024)),
                                    lambda n: (jnp.maximum(n - 1, 0) * WINDOW, off))
    return [_win(WINDOW, 1024, QA), _win(WINDOW, 1024, VA), prev(QA), prev(VA),
            _const((1, 768)), _const((1, 256)), _const((1, LANE)), _const((768, 768)),
            _const((A_HEADS // A_GROUP, 2 * WINDOW, A_GROUP * WINDOW))]


def _swa_fwd(proj, qg, kg, sinks, bd, bias):
    s = proj.shape[0]
    nb = s // WINDOW

    def body(qk_ref, vz_ref, qkp_ref, vzp_ref, qg_ref, kg_ref, sk_ref, bd_ref, bias_ref, o_ref):
        first = pl.program_id(0) == 0
        o_ref[...] = _swa_fn(qk_ref[...], vz_ref[...], qkp_ref[...], vzp_ref[...], qg_ref[...], kg_ref[...],
                             sk_ref[...], bd_ref[...], bias_ref[...], first).astype(BF16)

    return pl.pallas_call(
        body, grid=(nb,), in_specs=_swa_specs(nb), out_specs=_rowblk(WINDOW, 768),
        out_shape=SDS((s, 768), BF16), name="swa_fwd",
        compiler_params=_params(("parallel",)))(proj, proj, proj, proj, qg, kg, sinks, bd, bias)


def _swa_bwd(proj, qg, kg, sinks, bd, bias, dga):
    s = proj.shape[0]
    nb = s // WINDOW

    def body(qk_ref, vz_ref, qkp_ref, vzp_ref, qg_ref, kg_ref, sk_ref, bd_ref, bias_ref, dg_ref,
             dcur_ref, dprev_ref, dqg_ref, dkg_ref, dsk_ref):
        first = pl.program_id(0) == 0
        bd_v = bd_ref[...]
        bias_v = bias_ref[...]
        fn = lambda qk, vz, qkp, vzp, qg_, kg_, sk: _swa_fn(qk, vz, qkp, vzp, qg_, kg_, sk, bd_v, bias_v, first)
        _, vjp = jax.vjp(fn, qk_ref[...], vz_ref[...], qkp_ref[...], vzp_ref[...], qg_ref[...], kg_ref[...], sk_ref[...])
        dqk, dvz, dqkp, dvzp, dqg, dkg, dsk = vjp(dg_ref[...])

        @pl.when(first)
        def _():
            dqg_ref[...] = jnp.zeros_like(dqg_ref)
            dkg_ref[...] = jnp.zeros_like(dkg_ref)
            dsk_ref[...] = jnp.zeros_like(dsk_ref)

        dqg_ref[...] += dqg
        dkg_ref[...] += dkg
        dsk_ref[...] += dsk
        dcur_ref[...] = jnp.concatenate([dqk, dvz], axis=1)
        dprev_ref[...] = jnp.concatenate([dqkp[:, 768:], dvzp[:, :256]], axis=1)

    return pl.pallas_call(
        body, grid=(nb,), in_specs=_swa_specs(nb) + [_rowblk(WINDOW, 768)],
        out_specs=[_rowblk(WINDOW, 2048), pl.BlockSpec((None, WINDOW, 512), lambda n: (n, 0, 0)),
                   _const((1, 768)), _const((1, 256)), _const((1, LANE))],
        out_shape=[SDS((s, 2048), F32), SDS((nb, WINDOW, 512), F32), SDS((1, 768), F32), SDS((1, 256), F32),
                   SDS((1, LANE), F32)],
        name="swa_bwd", compiler_params=_params(("arbitrary",)))(proj, proj, proj, proj, qg, kg, sinks, bd, bias, dga)


def _swa_combine(dcur, dprev):
    s = dcur.shape[0]
    nb = s // WINDOW

    def body(c_ref, p_ref, o_ref):
        c = c_ref[...]
        nxt = jnp.where(pl.program_id(0) == nb - 1, 0.0, p_ref[...])
        o_ref[...] = jnp.concatenate([c[:, :768], c[:, 768:1280] + nxt, c[:, 1280:]], axis=1).astype(BF16)

    return pl.pallas_call(
        body, grid=(nb,),
        in_specs=[_rowblk(WINDOW, 2048), pl.BlockSpec((None, WINDOW, 512), lambda n: (jnp.minimum(n + 1, nb - 1), 0, 0))],
        out_specs=_rowblk(WINDOW, 2048), out_shape=SDS((s, 2048), BF16), name="swa_combine",
        compiler_params=_params(("parallel",)))(dcur, dprev)


def _mem_fwd(proj, mkv, qg, kg, bd, *, tr):
    s = proj.shape[0]

    def body(qz_ref, mkv_ref, qg_ref, kg_ref, bd_ref, o_ref):
        o_ref[...] = _mem_fn(qz_ref[...], mkv_ref[...], qg_ref[...], kg_ref[...], bd_ref[...]).astype(BF16)

    return pl.pallas_call(
        body, grid=(s // tr,),
        in_specs=[_win(tr, 1024, QC), _const(mkv.shape), _const((1, 512)), _const((1, 512)), _const((512, 512))],
        out_specs=_rowblk(tr, 512), out_shape=SDS((s, 512), BF16), name="mem_fwd",
        compiler_params=_params(("parallel",)))(proj, mkv, qg, kg, bd)


def _mem_bwd(proj, mkv, qg, kg, bd, dgc, *, tr):
    s = proj.shape[0]

    def body(qz_ref, mkv_ref, qg_ref, kg_ref, bd_ref, dg_ref, dqz_ref, dmkv_ref, dqg_ref, dkg_ref):
        bd_v = bd_ref[...]
        fn = lambda qz, mkv_, qg_, kg_: _mem_fn(qz, mkv_, qg_, kg_, bd_v)
        _, vjp = jax.vjp(fn, qz_ref[...], mkv_ref[...], qg_ref[...], kg_ref[...])
        dqz, dmkv, dqg, dkg = vjp(dg_ref[...])

        @pl.when(pl.program_id(0) == 0)
        def _():
            dmkv_ref[...] = jnp.zeros_like(dmkv_ref)
            dqg_ref[...] = jnp.zeros_like(dqg_ref)
            dkg_ref[...] = jnp.zeros_like(dkg_ref)

        dmkv_ref[...] += dmkv
        dqg_ref[...] += dqg
        dkg_ref[...] += dkg
        dqz_ref[...] = dqz.astype(BF16)

    return pl.pallas_call(
        body, grid=(s // tr,),
        in_specs=[_win(tr, 1024, QC), _const(mkv.shape), _const((1, 512)), _const((1, 512)), _const((512, 512)),
                  _rowblk(tr, 512)],
        out_specs=[_rowblk(tr, 1024), _const(mkv.shape), _const((1, 512)), _const((1, 512))],
        out_shape=[SDS((s, 1024), BF16), SDS(mkv.shape, F32), SDS((1, 512), F32), SDS((1, 512), F32)],
        name="mem_bwd", compiler_params=_params(("arbitrary",)))(proj, mkv, qg, kg, bd, dgc)


def _log_sigmoid(x):
    return jnp.minimum(x, 0.0) - jnp.log1p(jnp.exp(-jnp.abs(x)))


FOX_TQ, FOX_TK = 512, 512
FOX_FWD_TQ, FOX_FWD_TK = 512, 1024


def _fox_tiles(s):
    return min(FOX_TQ, s), min(FOX_TK, s)


AUG = 128 * B_HEADS
COL_A, COL_B = 64, 67


def _split3(c):
    hi = c.astype(BF16)
    r1 = c - hi.astype(F32)
    mid = r1.astype(BF16)
    lo = (r1 - mid.astype(F32)).astype(BF16)
    return hi, mid, lo


def _expand_mats():
    def mat(col0):
        e = np.zeros((768 + 3 * LANE, AUG), np.float32)
        for h in range(B_HEADS):
            for d in range(HEAD_DIM):
                e[64 * h + d, 128 * h + d] = 1.0
            for part in range(3):
                e[768 + LANE * part + h, 128 * h + col0 + part] = 1.0
        return e

    def ones(col0):
        o = np.zeros((1, AUG), np.float32)
        for h in range(B_HEADS):
            o[0, 128 * h + col0:128 * h + col0 + 3] = 1.0
        return o

    return (jnp.asarray(mat(COL_A), BF16), jnp.asarray(mat(COL_B), BF16), jnp.asarray(ones(COL_A)), jnp.asarray(ones(COL_B)))


def _augment(data_bf16, triple, emat, ones_row):
    parts = [data_bf16] + (list(triple) if triple is not None else [jnp.zeros((data_bf16.shape[0], LANE), BF16)] * 3)
    wide = _dot(jnp.concatenate(parts, axis=1), emat)
    if ones_row is not None:
        wide = wide + ones_row
    return wide


def _compact(wide):
    return jnp.concatenate([wide[:, 128 * h:128 * h + 64] for h in range(wide.shape[1] // 128)], axis=1)


def _lane_of_heads(wide, col, first=0):
    rows = wide.shape[0]
    lane = lax.broadcasted_iota(jnp.int32, (rows, LANE), 1)
    out = jnp.zeros((rows, LANE), F32)
    for h in range(wide.shape[1] // 128):
        out = jnp.where(lane == first + h, wide[:, 128 * h + col:128 * h + col + 1], out)
    return out


def _fox2_prep(proj, fbl, qg, kg, bfor, bd, ea, eb, ones_a, ones_b, *, tr):
    s = proj.shape[0]
    tri = jnp.asarray(np.tril(np.ones((tr, tr), np.float32)))

    def body(q_ref, k_ref, v_ref, fb_ref, qg_ref, kg_ref, bf_ref, bd_ref, tri_ref, ea_ref, eb_ref, oa_ref, ob_ref,
             qat_ref, ka_ref, kat_ref, va_ref, vat_ref, qn_ref, c_ref, carry):
        @pl.when(pl.program_id(0) == 0)
        def _():
            carry[...] = jnp.zeros_like(carry)

        bd_v = bd_ref[...]
        lane = lax.broadcasted_iota(jnp.int32, (tr, LANE), 1)
        logf = jnp.where(lane < N_FORGET, _log_sigmoid(fb_ref[...] + bf_ref[...]), 0.0)
        c = jnp.dot(tri_ref[...], logf, precision=HI, preferred_element_type=F32) + carry[...]
        c_ref[...] = c
        carry[...] = c[tr - 1:tr, :]
        qn = _qn_fn(q_ref[...], qg_ref[...], bd_v).astype(BF16)
        kn = _kn_fn(k_ref[...], kg_ref[...], bd_v).astype(BF16)
        qn_ref[...] = qn
        qat_ref[...] = jnp.transpose(_augment(qn, _split3(c), ea_ref[...], ob_ref[...])).astype(BF16)
        ka = _augment(kn, _split3(-c), eb_ref[...], oa_ref[...])
        ka_ref[...] = ka.astype(BF16)
        kat_ref[...] = jnp.transpose(ka).astype(BF16)
        va = _augment(v_ref[...].astype(BF16), None, ea_ref[...], oa_ref[...])
        va_ref[...] = va.astype(BF16)
        vat_ref[...] = jnp.transpose(va).astype(BF16)

    emat = _const((768 + 3 * LANE, AUG))
    return pl.pallas_call(
        body, grid=(s // tr,),
        in_specs=[_win(tr, 768, QB), _win(tr, 768, KB), _win(tr, 768, VB), _rowblk(tr, LANE), _const((1, 768)),
                  _const((1, 768)), _const((1, LANE)), _const((768, 768)), _const((tr, tr)), emat, emat,
                  _const((1, AUG)), _const((1, AUG))],
        out_specs=[pl.BlockSpec((AUG, tr), lambda i: (0, i)), _rowblk(tr, AUG), pl.BlockSpec((AUG, tr), lambda i: (0, i)),
                   _rowblk(tr, AUG), pl.BlockSpec((AUG, tr), lambda i: (0, i)), _rowblk(tr, 768), _rowblk(tr, LANE)],
        out_shape=[SDS((AUG, s), BF16), SDS((s, AUG), BF16), SDS((AUG, s), BF16), SDS((s, AUG), BF16),
                   SDS((AUG, s), BF16), SDS((s, 768), BF16), SDS((s, LANE), F32)],
        scratch_shapes=[pltpu.VMEM((1, LANE), F32)], name="fox_prep",
        compiler_params=_params(("arbitrary",)))(proj, proj, proj, fbl, qg, kg, bfor, bd, tri, ea, eb, ones_a, ones_b)


def _fox2_fwd(proj, qat, ka, vat):
    s = proj.shape[0]
    tq, tk = min(FOX_FWD_TQ, s), min(FOX_FWD_TK, s)
    nq, nk = s // tq, s // tk

    def last_k(i):
        return (i * tq + tq - 1) // tk

    def body(qt_ref, k_ref, vt_ref, z_ref, gb_ref, yb_ref, lse_ref, acc, m_s):
        i, j = pl.program_id(0), pl.program_id(1)

        @pl.when(j == 0)
        def _():
            acc[...] = jnp.zeros_like(acc)
            m_s[...] = jnp.full_like(m_s, NEG)

        def tile(masked):
            if masked:
                kpos = j * tk + lax.broadcasted_iota(jnp.int32, (tk, tq), 0)
                qpos = i * tq + lax.broadcasted_iota(jnp.int32, (tk, tq), 1)
                mask = kpos <= qpos
            for h in range(B_HEADS):
                sl = slice(128 * h, 128 * h + 128)
                sc = _dot(k_ref[:, sl], qt_ref[sl, :])
                if masked:
                    sc = jnp.where(mask, sc, NEG)
                m_prev = m_s[h:h + 1, :]
                m_new = jnp.maximum(m_prev, jnp.max(sc, axis=0, keepdims=True))
                p = jnp.exp(sc - m_new).astype(BF16)
                acc[sl, :] = jnp.exp(m_prev - m_new) * acc[sl, :] + _dot(vt_ref[sl, :], p)
                m_s[h:h + 1, :] = m_new

        full = j * tk + tk - 1 <= i * tq

        @pl.when(full)
        def _():
            tile(False)

        @pl.when(jnp.logical_and(jnp.logical_not(full), j <= last_k(i)))
        def _():
            tile(True)

        @pl.when(j == nk - 1)
        def _():
            outs = []
            row = lax.broadcasted_iota(jnp.int32, (LANE, tq), 0)
            lse_t = jnp.zeros((LANE, tq), F32)
            for h in range(B_HEADS):
                l_row = acc[128 * h + COL_A:128 * h + COL_A + 1, :]
                outs.append(acc[128 * h:128 * h + 64, :] * (1.0 / l_row))
                lse_t = jnp.where(row == h, m_s[h:h + 1, :] + jnp.log(l_row), lse_t)
            y = jnp.transpose(jnp.concatenate(outs, axis=0))
            yb_ref[...] = y
            gb_ref[...] = (y * _silu(z_ref[...])).astype(BF16)
            lse_ref[...] = jnp.transpose(lse_t)

    kcol = lambda i, j: (0, jnp.minimum(j, last_k(i)))
    return pl.pallas_call(
        body, grid=(nq, nk),
        in_specs=[pl.BlockSpec((AUG, tq), lambda i, j: (0, i)),
                  pl.BlockSpec((tk, AUG), lambda i, j: (jnp.minimum(j, last_k(i)), 0)),
                  pl.BlockSpec((AUG, tk), kcol),
                  pl.BlockSpec((pl.Element(tq), pl.Element(768)), lambda i, j: (i * tq, ZB))],
        out_specs=[pl.BlockSpec((tq, 768), lambda i, j: (i, 0)), pl.BlockSpec((tq, 768), lambda i, j: (i, 0)),
                   pl.BlockSpec((tq, LANE), lambda i, j: (i, 0))],
        out_shape=[SDS((s, 768), BF16), SDS((s, 768), F32), SDS((s, LANE), F32)],
        scratch_shapes=[pltpu.VMEM((AUG, tq), F32), pltpu.VMEM((16, tq), F32)],
        name="fox_fwd", compiler_params=_params(("parallel", "arbitrary")))(qat, ka, vat, proj)


def _fox2_bwd_pre(proj, yb, dgb, qn, c, lse, hsum, ea, ones_b, *, tr):
    s = proj.shape[0]

    def body(z_ref, y_ref, dg_ref, qn_ref, c_ref, lse_ref, hs_ref, ea_ref, ob_ref,
             qa_ref, qat_ref, dya_ref, dyat_ref, dz_ref):
        z, y, dg = z_ref[...], y_ref[...], dg_ref[...]
        sg = jax.nn.sigmoid(z)
        dy = dg * (z * sg)
        dz_ref[...] = (dg * y * (sg * (1.0 + z * (1.0 - sg)))).astype(BF16)
        delta = jnp.dot(dy * y, hs_ref[...], precision=HI, preferred_element_type=F32)
        e = ea_ref[...]
        dya = _augment(dy.astype(BF16), _split3(-delta), e, None)
        dya_ref[...] = dya.astype(BF16)
        dyat_ref[...] = jnp.transpose(dya).astype(BF16)
        qa = _augment(qn_ref[...], _split3(c_ref[...] - lse_ref[...]), e, ob_ref[...])
        qa_ref[...] = qa.astype(BF16)
        qat_ref[...] = jnp.transpose(qa).astype(BF16)

    return pl.pallas_call(
        body, grid=(s // tr,),
        in_specs=[_win(tr, 768, ZB), _rowblk(tr, 768), _rowblk(tr, 768), _rowblk(tr, 768), _rowblk(tr, LANE),
                  _rowblk(tr, LANE), _const((768, LANE)), _const((768 + 3 * LANE, AUG)), _const((1, AUG))],
        out_specs=[_rowblk(tr, AUG), pl.BlockSpec((AUG, tr), lambda i: (0, i)), _rowblk(tr, AUG),
                   pl.BlockSpec((AUG, tr), lambda i: (0, i)), _rowblk(tr, 768)],
        out_shape=[SDS((s, AUG), BF16), SDS((AUG, s), BF16), SDS((s, AUG), BF16), SDS((AUG, s), BF16),
                   SDS((s, 768), BF16)], name="fox_bwd_pre",
        compiler_params=_params(("parallel",)))(proj, yb, dgb, qn, c, lse, hsum, ea, ones_b)


def _fox2_bwd(qb, qbt, ka, kat, va, dya, dyat):
    s = qb.shape[0]
    tq, tk = _fox_tiles(s)
    nq, nk = s // tq, s // tk
    ng = 2
    gh = B_HEADS // ng
    gw = 128 * gh

    def first_q(j):
        return (j * tk) // tq

    def body(q_ref, qt_ref, k_ref, kt_ref, v_ref, dy_ref, dyt_ref, dq_hbm, dk_ref, dv_ref, dck_ref,
             dq_acc, dk_acc, dv_acc, sem):
        g, j, i = pl.program_id(0), pl.program_id(1), pl.program_id(2)

        @pl.when((j == 0) & (i == 0))
        def _():
            dq_acc[...] = jnp.zeros_like(dq_acc)

        @pl.when(i == 0)
        def _():
            dk_acc[...] = jnp.zeros_like(dk_acc)
            dv_acc[...] = jnp.zeros_like(dv_acc)

        def tile(masked):
            if masked:
                kpos = j * tk + lax.broadcasted_iota(jnp.int32, (tk, tq), 0)
                qpos = i * tq + lax.broadcasted_iota(jnp.int32, (tk, tq), 1)
                mask = kpos <= qpos
            cols = pl.ds(pl.multiple_of(i * tq, tq), tq)
            for h in range(gh):
                sl = slice(128 * h, 128 * h + 128)
                sc = _dot(k_ref[:, sl], qt_ref[sl, :])
                if masked:
                    sc = jnp.where(mask, sc, NEG)
                p = jnp.exp(sc)
                ds = (p * _dot(v_ref[:, sl], dyt_ref[sl, :])).astype(BF16)
                dv_acc[:, sl] += _dot(p.astype(BF16), dy_ref[:, sl])
                dk_acc[:, sl] += _dot(ds, q_ref[:, sl])
                dq_acc[sl, cols] += _dot(kt_ref[sl, :], ds)

        full = j * tk + tk - 1 <= i * tq

        @pl.when(full)
        def _():
            tile(False)

        @pl.when(jnp.logical_and(jnp.logical_not(full), i >= first_q(j)))
        def _():
            tile(True)

        @pl.when(i == nq - 1)
        def _():
            dkw = dk_acc[...]
            dk_ref[...] = _compact(dkw)
            dv_ref[...] = _compact(dv_acc[...]).astype(BF16)
            dck_ref[...] = -_lane_of_heads(dkw, COL_B, gh * g)

        @pl.when((j == nk - 1) & (i == nq - 1))
        def _():
            cp = pltpu.make_async_copy(dq_acc, dq_hbm.at[pl.ds(pl.multiple_of(g * gw, gw), gw)], sem)
            cp.start()
            cp.wait()

    qrow = pl.BlockSpec((tq, gw), lambda g, j, i: (jnp.maximum(i, first_q(j)), g))
    qcol = pl.BlockSpec((gw, tq), lambda g, j, i: (g, jnp.maximum(i, first_q(j))))
    krow = pl.BlockSpec((tk, gw), lambda g, j, i: (j, g))
    kcol = pl.BlockSpec((gw, tk), lambda g, j, i: (g, j))
    kout = pl.BlockSpec((tk, gw // 2), lambda g, j, i: (j, g))
    return pl.pallas_call(
        body, grid=(ng, nk, nq),
        in_specs=[qrow, qcol, krow, kcol, krow, qrow, qcol],
        out_specs=[pl.BlockSpec(memory_space=pl.ANY), kout, kout,
                   pl.BlockSpec((None, tk, LANE), lambda g, j, i: (g, j, 0))],
        out_shape=[SDS((AUG, s), F32), SDS((s, 768), F32), SDS((s, 768), BF16), SDS((ng, s, LANE), F32)],
        scratch_shapes=[pltpu.VMEM((gw, s), F32), pltpu.VMEM((tk, gw), F32), pltpu.VMEM((tk, gw), F32),
                        pltpu.SemaphoreType.DMA],
        name="fox_bwd", compiler_params=_params(("arbitrary",) * 3))(qb, qbt, ka, kat, va, dya, dyat)


def _fox2_bwd_post(proj, fbl, qg, kg, bfor, bd, dqa, dkn, dck, *, tr):
    s = proj.shape[0]
    nb = s // tr
    triu = jnp.asarray(np.triu(np.ones((tr, tr), np.float32)))
    rev = lambda i: nb - 1 - i

    def body(q_ref, k_ref, fb_ref, qg_ref, kg_ref, bf_ref, bd_ref, tri_ref, dqa_ref, dkn_ref, dck_ref,
             dq_ref, dk_ref, dfb_ref, dqg_ref, dkg_ref, dbf_ref, carry):
        @pl.when(pl.program_id(0) == 0)
        def _():
            carry[...] = jnp.zeros_like(carry)
            dqg_ref[...] = jnp.zeros_like(dqg_ref)
            dkg_ref[...] = jnp.zeros_like(dkg_ref)
            dbf_ref[...] = jnp.zeros_like(dbf_ref)

        bd_v = bd_ref[...]
        dqw = jnp.transpose(dqa_ref[...])
        _, vjp_q = jax.vjp(lambda q, g: _qn_fn(q, g, bd_v), q_ref[...], qg_ref[...])
        dq, dqg = vjp_q(_compact(dqw))
        _, vjp_k = jax.vjp(lambda k, g: _kn_fn(k, g, bd_v), k_ref[...], kg_ref[...])
        dk, dkg = vjp_k(dkn_ref[...])
        dq_ref[...] = dq.astype(BF16)
        dk_ref[...] = dk.astype(BF16)
        dqg_ref[...] += dqg
        dkg_ref[...] += dkg

        dc = _lane_of_heads(dqw, COL_A) + (dck_ref[0] + dck_ref[1])
        dlogf = jnp.dot(tri_ref[...], dc, precision=HI, preferred_element_type=F32) + carry[...]
        carry[...] = dlogf[0:1, :]
        lane = lax.broadcasted_iota(jnp.int32, (tr, LANE), 1)
        xf = fb_ref[...] + bf_ref[...]
        dfb = jnp.where(lane < N_FORGET, dlogf * jax.nn.sigmoid(-xf), 0.0)
        dfb_ref[...] = dfb.astype(BF16)
        dbf_ref[...] += jnp.sum(dfb, axis=0, keepdims=True)

    rb = lambda w: pl.BlockSpec((tr, w), lambda i: (rev(i), 0))
    wn = lambda w, off: pl.BlockSpec((pl.Element(tr), pl.Element(w)), lambda i: (rev(i) * tr, off))
    return pl.pallas_call(
        body, grid=(nb,),
        in_specs=[wn(768, QB), wn(768, KB), rb(LANE), _const((1, 768)), _const((1, 768)), _const((1, LANE)),
                  _const((768, 768)), _const((tr, tr)), pl.BlockSpec((AUG, tr), lambda i: (0, rev(i))), rb(768),
                  pl.BlockSpec((2, tr, LANE), lambda i: (0, rev(i), 0))],
        out_specs=[rb(768), rb(768), rb(LANE), _const((1, 768)), _const((1, 768)), _const((1, LANE))],
        out_shape=[SDS((s, 768), BF16), SDS((s, 768), BF16), SDS((s, LANE), BF16), SDS((1, 768), F32),
                   SDS((1, 768), F32), SDS((1, LANE), F32)],
        scratch_shapes=[pltpu.VMEM((1, LANE), F32)], name="fox_bwd_post",
        compiler_params=_params(("arbitrary",)))(proj, proj, fbl, qg, kg, bfor, bd, triu, dqa, dkn, dck)


def _merge_specs(tr):
    row = lambda w: pl.BlockSpec((tr, w), lambda i, j: (i, 0))
    shard = lambda r: pl.BlockSpec((None, r, 512), lambda i, j: (j, 0, 0))
    gate = lambda b: pl.BlockSpec((tr, 512), lambda i, j: (i, (GATE + 2048 * b) // 512 + j))
    return [row(768), row(768), row(512), shard(768), shard(768), shard(512), gate(0), gate(1), gate(2)]


def _merge_fwd(proj, ga, gb, gc, wa, wb, wc, *, tr):
    s = proj.shape[0]

    def body(ga_ref, gb_ref, gc_ref, wa_ref, wb_ref, wc_ref, l0_ref, l1_ref, l2_ref, y_ref):
        ua = _dot(ga_ref[...], wa_ref[...])
        ub = _dot(gb_ref[...], wb_ref[...])
        uc = _dot(gc_ref[...], wc_ref[...])
        y = jax.nn.sigmoid(l0_ref[...]) * ua + jax.nn.sigmoid(l1_ref[...]) * ub + jax.nn.sigmoid(l2_ref[...]) * uc
        y_ref[...] = y.astype(BF16)

    return pl.pallas_call(
        body, grid=(s // tr, N_CHIPS), in_specs=_merge_specs(tr),
        out_specs=pl.BlockSpec((tr, 512), lambda i, j: (i, j)), out_shape=SDS((s, D_MODEL), BF16), name="merge_fwd",
        compiler_params=_params(("parallel", "arbitrary")))(ga, gb, gc, wa, wb, wc, proj, proj, proj)


def _merge_bwd(proj, ga, gb, gc, wa, wb, wc, dy, *, tr):
    s = proj.shape[0]

    def body(ga_ref, gb_ref, gc_ref, wa_ref, wb_ref, wc_ref, l0_ref, l1_ref, l2_ref, dy_ref,
             dl0_ref, dl1_ref, dl2_ref, dua_ref, dub_ref, duc_ref, dga_ref, dgb_ref, dgc_ref):
        j = pl.program_id(1)
        dyv = dy_ref[...]

        @pl.when(j == 0)
        def _():
            dga_ref[...] = jnp.zeros_like(dga_ref)
            dgb_ref[...] = jnp.zeros_like(dgb_ref)
            dgc_ref[...] = jnp.zeros_like(dgc_ref)

        for g_ref, w_ref, l_ref, dl_ref, du_ref, dg_ref in (
                (ga_ref, wa_ref, l0_ref, dl0_ref, dua_ref, dga_ref),
                (gb_ref, wb_ref, l1_ref, dl1_ref, dub_ref, dgb_ref),
                (gc_ref, wc_ref, l2_ref, dl2_ref, duc_ref, dgc_ref)):
            w = w_ref[...]
            u = _dot(g_ref[...], w)
            sg = jax.nn.sigmoid(l_ref[...])
            dl_ref[...] = (dyv * u * sg * (1.0 - sg)).astype(BF16)
            du = (dyv * sg).astype(BF16)
            du_ref[...] = du
            dg_ref[...] += _dot_nt(du, w)

    blk = pl.BlockSpec((tr, 512), lambda i, j: (i, j))
    row = lambda w: pl.BlockSpec((tr, w), lambda i, j: (i, 0))
    big = SDS((s, D_MODEL), BF16)
    return pl.pallas_call(
        body, grid=(s // tr, N_CHIPS), in_specs=_merge_specs(tr) + [blk],
        out_specs=[blk] * 6 + [row(768), row(768), row(512)],
        out_shape=[big] * 6 + [SDS((s, 768), F32), SDS((s, 768), F32), SDS((s, 512), F32)], name="merge_bwd",
        compiler_params=_params(("parallel", "arbitrary")))(ga, gb, gc, wa, wb, wc, proj, proj, proj, dy)


def _out_loss(y, wo, x, tgt, *, tr, tn):
    s = x.shape[0]

    def body(y_ref, w_ref, x_ref, t_ref, d_ref, db_ref, sq_ref):
        @pl.when((pl.program_id(0) == 0) & (pl.program_id(1) == 0))
        def _():
            sq_ref[...] = jnp.zeros_like(sq_ref)

        out = x_ref[...] + _dot(y_ref[...], w_ref[...])
        diff = out - t_ref[...]
        sq_ref[...] += jnp.sum(diff * diff, axis=0, keepdims=True)
        d = diff * (1.0 / D_MODEL)
        d_ref[...] = d
        db_ref[...] = d.astype(BF16)

    blk = pl.BlockSpec((tr, tn), lambda i, j: (i, j))
    return pl.pallas_call(
        body, grid=(s // tr, D_MODEL // tn),
        in_specs=[pl.BlockSpec((tr, D_MODEL), lambda i, j: (i, 0)), pl.BlockSpec((D_MODEL, tn), lambda i, j: (0, j)), blk, blk],
        out_specs=[blk, blk, _const((1, tn))],
        out_shape=[SDS((s, D_MODEL), F32), SDS((s, D_MODEL), BF16), SDS((1, tn), F32)], name="out_loss",
        compiler_params=_params(("arbitrary", "arbitrary")))(y, wo, x, tgt)


def _tile_gain(g, reps):
    return jnp.tile(g.reshape(1, -1), (1, reps))


def _pad_lane(v):
    v = v.reshape(1, -1)
    return jnp.pad(v, ((0, 0), (0, LANE - v.shape[1])))


def _local_step(x, mem, tgt, w_main, w_fb, w_small, norm_gain, mem_norm_gain, b_forget,
                q_gain_a, k_gain_a, sinks_a, q_gain_b, k_gain_b, q_gain_c, k_gain_c, core=None):
    s = x.shape[0]
    tr = min(512, s)
    bd64 = _block_diag(768, HEAD_DIM)
    bd128 = _block_diag(512, C_HEAD_DIM)
    hsum = _head_sum(768, HEAD_DIM)
    qga, kga = _tile_gain(q_gain_a, 12), _tile_gain(k_gain_a, 4)
    qgb, kgb = _tile_gain(q_gain_b, 12), _tile_gain(k_gain_b, 12)
    qgc, kgc = _tile_gain(q_gain_c, 4), _tile_gain(k_gain_c, 4)
    sinks = _pad_lane(sinks_a)
    bfor = _pad_lane(b_forget)

    hn = _rms_fwd(x, norm_gain, tr=tr, name="rms_x")
    on_mesh = core is not None
    if on_mesh:
        proj, (gathered,) = _matmul(hn, w_main, dims="nn", out_dtype=F32, tm=1024, tn=1024, tk=D_MODEL, name="proj_main",
                                    comms=[_gather_comm(list(w_small))])
        w_mk, wa, wb, wc, wo = gathered
        w_mk, wo = w_mk.reshape(D_MODEL, 1024), wo.reshape(D_MODEL, D_MODEL)
    else:
        proj = _matmul(hn, w_main, dims="nn", out_dtype=F32, tm=1024, tn=1024, tk=D_MODEL, name="proj_main")
        w_mk, wa, wb, wc, wo = w_small
    fbl = _matmul(hn, w_fb, dims="nn", out_dtype=F32, tm=1024, tn=LANE, tk=D_MODEL, name="proj_forget")
    memn = _rms_fwd(mem, mem_norm_gain, tr=mem.shape[0], name="rms_mem")
    mkv = _matmul(memn, w_mk, dims="nn", out_dtype=F32, tm=256, tn=512, tk=D_MODEL, name="mem_kv")

    swa_bias = _swa_bias()
    ga = _swa_fwd(proj, qga, kga, sinks, bd64, swa_bias)
    ea, eb, ones_a, ones_b = _expand_mats()
    tf = min(256, s)
    qat, ka, kat, va, vat, qn, cfox = _fox2_prep(proj, fbl, qgb, kgb, bfor, bd64, ea, eb, ones_a, ones_b, tr=tf)
    gb, yb, lse = _fox2_fwd(proj, qat, ka, vat)
    gc = _mem_fwd(proj, mkv, qgc, kgc, bd128, tr=tr)
    y = _merge_fwd(proj, ga, gb, gc, wa, wb, wc, tr=tr)
    dout, dout_b, sq = _out_loss(y, wo, x, tgt, tr=tr, tn=512)

    d_wo = _matmul(y, dout_b, dims="tn", out_dtype=F32, tm=1024, tn=512, tk=4096, name="dw_out")
    dy = _matmul(dout_b, wo, dims="nt", out_dtype=F32, tm=1024, tn=512, tk=D_MODEL, name="dy")
    dl0, dl1, dl2, dua, dub, duc, dga, dgb, dgc = _merge_bwd(proj, ga, gb, gc, wa, wb, wc, dy, tr=tr)
    d_wa = _matmul(ga, dua, dims="tn", out_dtype=F32, tm=768, tn=512, tk=4096, name="dw_branch_a")
    d_wb = _matmul(gb, dub, dims="tn", out_dtype=F32, tm=768, tn=512, tk=4096, name="dw_branch_b")
    d_wc = _matmul(gc, duc, dims="tn", out_dtype=F32, tm=512, tn=512, tk=4096, name="dw_branch_c")

    dcur, dprev, d_qga, d_kga, d_sinks = _swa_bwd(proj, qga, kga, sinks, bd64, swa_bias, dga)
    dproj_a = _swa_combine(dcur, dprev)

    qab, qabt, dya, dyat, dzb = _fox2_bwd_pre(proj, yb, dgb, qn, cfox, lse, hsum, ea, ones_b, tr=tf)
    dqa, dkn, dvb, dck = _fox2_bwd(qab, qabt, ka, kat, va, dya, dyat)
    dqb, dkb, dfb, d_qgb, d_kgb, d_bf = _fox2_bwd_post(proj, fbl, qgb, kgb, bfor, bd64, dqa, dkn, dck, tr=tf)

    dproj_c, dmkv, d_qgc, d_kgc = _mem_bwd(proj, mkv, qgc, kgc, bd128, dgc, tr=tr)
    dmkv_b = dmkv.astype(BF16)
    d_wmk = _matmul(memn, dmkv_b, dims="tn", out_dtype=F32, tm=1024, tn=512, tk=256, name="dw_mem_kv")
    dmemn = _matmul(dmkv_b, w_mk, dims="nt", out_dtype=F32, tm=256, tn=512, tk=1024, name="dmemn")
    (d_mem_gain,) = _rms_bwd(mem, mem_norm_gain, dmemn, None, tr=mem.shape[0], name="rms_mem_bwd")

    dproj = [dproj_a, jnp.concatenate([dqb, dkb, dvb, dzb, dproj_c], axis=1), dl0, dl1, dl2]
    dhn_f = _matmul(dfb, w_fb, dims="nt", out_dtype=F32, tm=1024, tn=512, tk=LANE, name="dhn_forget")
    d_wfb = _matmul(hn, dfb, dims="tn", out_dtype=F32, tm=1024, tn=LANE, tk=512, name="dw_forget")
    big = {}
    if on_mesh:
        half = D_MODEL // 2
        c0 = core[0]
        hn_other = lax.dynamic_slice(hn, (0, (1 - c0) * half), (s, half))
        hn_own = lax.dynamic_slice(hn, (0, c0 * half), (s, half))
        g1, k1 = [d_wmk, d_wa, d_wb, d_wc, d_wo], [2, 3, 4, 5, 6]
        d_other, (got1,) = _matmul(hn_other, dproj, dims="tn", out_dtype=F32, tm=1024, tn=512, tk=4096,
                                   name="dw_main_other", comms=[_exchange_comm(g1, k1)])
        h1 = [_add_half(g, got, core, HALF_AXIS[k], name=f"add_half_{k}") for g, got, k in zip(g1, got1, k1)]
        d_own, (got0, parts1) = _matmul(
            hn_own, dproj, dims="tn", out_dtype=F32, tm=1024, tn=512, tk=4096, name="dw_main_own",
            comms=[_exchange_comm([d_other, d_wfb], [0, 1], whole=(0,)), _scatter_comm(h1, k1)])
        h0 = [_add_pair(d_own, got0[0], name="add_pair_main"), _add_half(d_wfb, got0[1], core, 0, name="add_half_1")]
        sums1 = [_sum4(p, name=f"sum4_{k}") for p, k in zip(parts1, k1)]
        dhn, (parts0, theirs1) = _matmul(dproj, w_main, dims="nt", out_dtype=F32, tm=1024, tn=512, tk=2048, vmem=VMEM_WIDE, name="dhn",
                                         add=dhn_f, comms=[_scatter_comm(h0, [0, 1]), _swap_comm(sums1)])
        sums0 = [_sum4(p, name=f"sum4_{k}") for p, k in zip(parts0, (0, 1))]
        (grad_x, d_gain), theirs0 = _rms_bwd(x, norm_gain, dhn, dout, tr=tr, name="rms_x_bwd", comm=_swap_comm(sums0))
        big = dict(sums=sums0 + sums1, theirs=list(theirs0) + list(theirs1))
    else:
        dhn = _matmul(dproj, w_main, dims="nt", out_dtype=F32, tm=1024, tn=512, tk=2048, vmem=VMEM_WIDE, name="dhn", add=dhn_f)
        d_wmain = _matmul(hn, dproj, dims="tn", out_dtype=F32, tm=1024, tn=512, tk=4096, name="dw_main")
        big = dict(d_wmain=d_wmain, d_wfb=d_wfb, d_wmk=d_wmk, d_wa=d_wa, d_wb=d_wb, d_wc=d_wc, d_wo=d_wo)
        grad_x, d_gain = _rms_bwd(x, norm_gain, dhn, dout, tr=tr, name="rms_x_bwd")

    fold = lambda g, reps: jnp.sum(g.reshape(reps, -1), axis=0, keepdims=True)
    return dict(
        sq=sq, grad_x=grad_x, **big,
        d_gain=d_gain, d_mem_gain=d_mem_gain, d_bf=d_bf[:, :N_FORGET],
        d_qga=fold(d_qga, 12), d_kga=fold(d_kga, 4), d_sinks=d_sinks[:, :A_HEADS],
        d_qgb=fold(d_qgb, 12), d_kgb=fold(d_kgb, 12), d_qgc=fold(d_qgc, 4), d_kgc=fold(d_kgc, 4))


PACK_ROWS = 256
FORGET_IN_SHARD = FORGET_COL - SHARD_COLS
AFTER_FORGET = FORGET_COL - SLAB_START[1]
END_CHIP1 = 2 * SHARD_COLS - N_FORGET - SLAB_START[1]


def _pack_w_in(chip, w):
    rows = w.shape[1]
    tr = PACK_ROWS

    def body(k_ref, w_ref, o_ref, scr):
        scr[...] = jnp.zeros_like(scr)
        scr[pl.ds(0, SHARD_COLS), :] = w_ref[...]
        v = jnp.transpose(scr[...])
        k = k_ref[0]
        col = lax.broadcasted_iota(jnp.int32, (tr, SLAB), 1)
        no_forget = jnp.zeros((tr, LANE), BF16)

        @pl.when(k == 0)
        def _():
            o_ref[:, 0:SLAB] = v.astype(BF16)
            o_ref[:, SLAB:] = no_forget

        @pl.when(k == 1)
        def _():
            before = pltpu.roll(v, SLAB_SHIFT[1], axis=1)
            after = pltpu.roll(v, SLAB - (N_FORGET - SLAB_SHIFT[1]), axis=1)
            slab = jnp.where(col < AFTER_FORGET, before, jnp.where(col < END_CHIP1, after, 0.0))
            o_ref[:, 0:SLAB] = slab.astype(BF16)
            f = pltpu.roll(v, SLAB - FORGET_IN_SHARD, axis=1)[:, :LANE]
            o_ref[:, SLAB:] = jnp.where(col[:, :LANE] < N_FORGET, f, 0.0).astype(BF16)

        for kk in (2, 3):
            @pl.when(k == kk)
            def _(kk=kk):
                o_ref[:, 0:SLAB] = pltpu.roll(v, SLAB_SHIFT[kk], axis=1).astype(BF16)
                o_ref[:, SLAB:] = no_forget

    return pl.pallas_call(
        body, grid_spec=pltpu.PrefetchScalarGridSpec(
            num_scalar_prefetch=1, grid=(rows // tr,),
            in_specs=[pl.BlockSpec((SHARD_COLS, tr), lambda i, k: (0, i))],
            out_specs=pl.BlockSpec((None, tr, SLAB + LANE), lambda i, k: (k[0], i, 0)),
            scratch_shapes=[pltpu.VMEM((SLAB, tr), F32)]),
        out_shape=SDS((N_CHIPS, rows, SLAB + LANE), BF16), name="pack_w_in",
        compiler_params=_params(("arbitrary",)))(chip, w)


def _merge_slabs(g):
    rows = g.shape[1]
    tr = PACK_ROWS
    t = [s // LANE for s in SLAB_START]
    n_t = SLAB // LANE

    def body(g_ref, m_ref, f_ref):
        for k in range(N_CHIPS):
            lo = t[k] + (1 if k > 0 else 0)
            hi = t[k + 1] if k + 1 < N_CHIPS else t[k] + n_t
            m_ref[:, lo * LANE:hi * LANE] = g_ref[k, :, (lo - t[k]) * LANE:(hi - t[k]) * LANE]
            if k + 1 < N_CHIPS:
                a = g_ref[k, :, (hi - t[k]) * LANE:(hi - t[k] + 1) * LANE].astype(F32)
                b = g_ref[k + 1, :, 0:LANE].astype(F32)
                m_ref[:, hi * LANE:(hi + 1) * LANE] = (a + b).astype(BF16)
        f_ref[...] = g_ref[1, :, SLAB:]

    return pl.pallas_call(
        body, grid=(rows // tr,),
        in_specs=[pl.BlockSpec((N_CHIPS, tr, SLAB + LANE), lambda i: (0, i, 0))],
        out_specs=[_rowblk(tr, P_MAIN), _rowblk(tr, LANE)],
        out_shape=[SDS((rows, P_MAIN), BF16), SDS((rows, LANE), BF16)], name="merge_slabs",
        compiler_params=_params(("parallel",)))(g)


def _adamw_math(w, g, m, v):
    nm = ADAM_B1 * m + (1.0 - ADAM_B1) * g
    nv = ADAM_B2 * v + (1.0 - ADAM_B2) * (g * g)
    m_hat = nm / (1.0 - ADAM_B1 ** ADAM_STEP)
    v_hat = nv / (1.0 - ADAM_B2 ** ADAM_STEP)
    delta = -ADAM_LR * (m_hat / (jnp.sqrt(v_hat) + ADAM_EPS) + ADAM_WD * w)
    return delta, nm, nv


def _adamw(g, w, m, v, *, tr, name):
    rows, cols = w.shape
    tr = min(tr, rows)

    def body(g_ref, w_ref, m_ref, v_ref, d_ref, nm_ref, nv_ref):
        d, nm, nv = _adamw_math(w_ref[...], g_ref[...], m_ref[...], v_ref[...])
        d_ref[...] = d
        nm_ref[...] = nm
        nv_ref[...] = nv

    spec = _rowblk(tr, cols)
    return pl.pallas_call(
        body, grid=(rows // tr,), in_specs=[spec] * 4, out_specs=[spec] * 3,
        out_shape=[SDS((rows, cols), F32)] * 3, name=name, compiler_params=_params(("parallel",)))(g, w, m, v)


def _adamw_w_in(chip_core, slab_mine, slab_theirs, forget_mine, forget_theirs, w, m, v):
    rows = w.shape[1]
    tr = PACK_ROWS // 2
    nbh = rows // 2 // tr

    def body(k_ref, sa_ref, sb_ref, fa_ref, fb_ref, w_ref, m_ref, v_ref, g_ref, d_ref, nm_ref, nv_ref):
        use_mine = pl.program_id(0) // nbh == k_ref[1]
        sl = jnp.where(use_mine, sa_ref[...], sb_ref[...])
        f_tile = jnp.where(use_mine, fa_ref[...], fb_ref[...])
        k = k_ref[0]

        def emit(wide):
            g = jnp.transpose(wide)[:SHARD_COLS, :]
            g_ref[...] = g
            d, nm, nv = _adamw_math(w_ref[...], g, m_ref[...], v_ref[...])
            d_ref[...] = d
            nm_ref[...] = nm
            nv_ref[...] = nv

        @pl.when(k == 0)
        def _():
            emit(sl)

        @pl.when(k == 1)
        def _():
            col = lax.broadcasted_iota(jnp.int32, (tr, SLAB), 1)
            before = pltpu.roll(sl, SLAB - SLAB_SHIFT[1], axis=1)
            after = pltpu.roll(sl, N_FORGET - SLAB_SHIFT[1], axis=1)
            wide_f = jnp.concatenate([f_tile, jnp.zeros((tr, SLAB - LANE), F32)], axis=1)
            forget = pltpu.roll(wide_f, FORGET_IN_SHARD, axis=1)
            emit(jnp.where(col < FORGET_IN_SHARD, before, jnp.where(col < FORGET_IN_SHARD + N_FORGET, forget, after)))

        for kk in (2, 3):
            @pl.when(k == kk)
            def _(kk=kk):
                emit(pltpu.roll(sl, SLAB - SLAB_SHIFT[kk], axis=1))

    nat = pl.BlockSpec((SHARD_COLS, tr), lambda i, k: (0, i))
    half = lambda width: pl.BlockSpec((tr, width), lambda i, k: (i % nbh, 0))
    return pl.pallas_call(
        body, grid_spec=pltpu.PrefetchScalarGridSpec(
            num_scalar_prefetch=1, grid=(rows // tr,),
            in_specs=[half(SLAB), half(SLAB), half(LANE), half(LANE), nat, nat, nat],
            out_specs=[nat] * 4),
        out_shape=[SDS((SHARD_COLS, rows), F32)] * 4, name="adamw_w_in",
        compiler_params=_params(("arbitrary",)))(chip_core, slab_mine, slab_theirs, forget_mine, forget_theirs, w, m, v)


ANY = pl.BlockSpec(memory_space=pl.ANY)
HALF_AXIS = (0, 0, 1, 0, 0, 0, 1)


def _me():
    return lax.axis_index("x"), lax.axis_index("y"), lax.axis_index("c")


def _half(ref, which, axis):
    n = ref.shape[axis] // 2
    sl = pl.ds(which * n, n)
    return ref.at[sl] if axis == 0 else ref.at[:, sl]


def _piece(t, ref, j):
    if t == 0:
        return ref.at[:, pl.ds(SLAB_START[j], SLAB)]
    if t == 1:
        return ref
    if t in (2, 6):
        return ref.at[pl.ds(512 * j, 512)]
    return ref.at[:, pl.ds(512 * j, 512)]


def _piece_shape(t, shape):
    if t == 0:
        return (shape[0], SLAB)
    if t == 1:
        return shape
    if t in (2, 6):
        return (512, shape[1])
    return (shape[0], 512)


def _gather_plan(ins, outs, own_slot_in_src):
    x, y, c = _me()
    k = 2 * x + y
    sib = (x, y, 1 - c)
    chips = [(1 - x, y), (x, 1 - y), (1 - x, 1 - y)]
    n = len(outs)

    def rows(t, which):
        h = outs[t].shape[1] // 2
        return pl.ds(which * h, h)

    def mine(t):
        return ins[t].at[k, rows(t, c)] if own_slot_in_src else ins[t].at[rows(t, c)]

    def first(t, j, sems):
        chip = chips[j]
        return pltpu.make_async_remote_copy(
            src_ref=mine(t), dst_ref=outs[t].at[k, rows(t, c)], send_sem=sems[0].at[t, j], recv_sem=sems[1].at[t, j],
            device_id=(chip[0], chip[1], c), device_id_type=MESH)

    def landed(t, j, sems):
        chip = chips[j]
        return pltpu.make_async_remote_copy(
            src_ref=mine(t), dst_ref=outs[t].at[2 * chip[0] + chip[1], rows(t, c)], send_sem=sems[0].at[t, j],
            recv_sem=sems[1].at[t, j], device_id=(chip[0], chip[1], c), device_id_type=MESH)

    def passed(t, j, which, sems):
        chip = chips[j]
        blk = outs[t].at[2 * chip[0] + chip[1], rows(t, which)]
        return pltpu.make_async_remote_copy(
            src_ref=blk, dst_ref=blk, send_sem=sems[2].at[t, j], recv_sem=sems[3].at[t, j], device_id=sib,
            device_id_type=MESH)

    def start(sems):
        for j in range(3):
            for t in range(n):
                first(t, j, sems).start()

    def finish(sems):
        for j in range(3):
            for t in range(n):
                landed(t, j, sems).wait_recv()
                passed(t, j, c, sems).start()
        for j in range(3):
            for t in range(n):
                passed(t, j, 1 - c, sems).wait_recv()
        for j in range(3):
            for t in range(n):
                first(t, j, sems).wait_send()
                passed(t, j, c, sems).wait_send()

    return k, start, finish


def _all_gather_slabs(slabs):
    def body(in_ref, out_ref, nbr_sem, quarter_sem, pass_sem):
        x, y, c = _me()
        k = 2 * x + y
        rows = out_ref.shape[1]
        h, q = rows // 2, rows // 4
        nbrs = [(1 - x, y), (x, 1 - y)]
        slot = lambda chip: 2 * chip[0] + chip[1]
        diag = 2 * (1 - x) + (1 - y)
        half = pl.ds(c * h, h)
        quarter = lambda a: pl.ds(c * h + a * q, q)

        def first(a):
            return pltpu.make_async_remote_copy(
                src_ref=in_ref.at[k, half], dst_ref=out_ref.at[k, half], send_sem=nbr_sem.at[0, a],
                recv_sem=nbr_sem.at[1, a], device_id=(nbrs[a][0], nbrs[a][1], c), device_id_type=MESH)

        def landed(a):
            blk = out_ref.at[slot(nbrs[a]), half]
            return pltpu.make_async_remote_copy(
                src_ref=blk, dst_ref=blk, send_sem=nbr_sem.at[0, a], recv_sem=nbr_sem.at[1, a],
                device_id=(nbrs[a][0], nbrs[a][1], c), device_id_type=MESH)

        def relay(a):
            blk = out_ref.at[slot(nbrs[a]), quarter(a)]
            to = nbrs[1 - a]
            return pltpu.make_async_remote_copy(
                src_ref=blk, dst_ref=blk, send_sem=quarter_sem.at[0, a], recv_sem=quarter_sem.at[1, a],
                device_id=(to[0], to[1], c), device_id_type=MESH)

        def relayed(a):
            blk = out_ref.at[diag, quarter(a)]
            frm = nbrs[1 - a]
            return pltpu.make_async_remote_copy(
                src_ref=blk, dst_ref=blk, send_sem=quarter_sem.at[0, a], recv_sem=quarter_sem.at[1, a],
                device_id=(frm[0], frm[1], c), device_id_type=MESH)

        def passed(j, which):
            sl = diag if j == 2 else slot(nbrs[j])
            blk = out_ref.at[sl, pl.ds(which * h, h)]
            return pltpu.make_async_remote_copy(
                src_ref=blk, dst_ref=blk, send_sem=pass_sem.at[0, j], recv_sem=pass_sem.at[1, j],
                device_id=(x, y, 1 - c), device_id_type=MESH)

        for a in range(2):
            first(a).start()
        for a in range(2):
            landed(a).wait_recv()
            relay(a).start()
            passed(a, c).start()
        for a in range(2):
            relayed(a).wait_recv()
        passed(2, c).start()
        for j in range(3):
            passed(j, 1 - c).wait_recv()
        for a in range(2):
            first(a).wait_send()
            relay(a).wait_send()
        for j in range(3):
            passed(j, c).wait_send()

    return pl.pallas_call(
        body, in_specs=[ANY], out_specs=ANY, out_shape=SDS(slabs.shape, slabs.dtype),
        scratch_shapes=[pltpu.SemaphoreType.DMA((2, 2)), pltpu.SemaphoreType.DMA((2, 2)), pltpu.SemaphoreType.DMA((2, 3))],
        input_output_aliases={0: 0}, name="all_gather_slabs")(slabs)


def _gather_comm(parts):
    n = len(parts)

    def start(ins, outs, sems):
        k, go, _ = _gather_plan(ins, outs, False)
        for t in range(n):
            pltpu.make_async_copy(ins[t], outs[t].at[k], sems[4].at[t]).start()
        go(sems)

    def finish(ins, outs, sems):
        k, _, done = _gather_plan(ins, outs, False)
        done(sems)
        for t in range(n):
            pltpu.make_async_copy(ins[t], outs[t].at[k], sems[4].at[t]).wait()

    return _Comm(parts, [SDS((N_CHIPS,) + p.shape, p.dtype) for p in parts],
                 [pltpu.SemaphoreType.DMA((n, 3))] * 4 + [pltpu.SemaphoreType.DMA((n,))], start, finish)


def _exchange_comm(arrs, kinds, whole=()):
    n = len(arrs)

    def copies(ins, outs, sems):
        x, y, c = _me()
        return [pltpu.make_async_remote_copy(
            src_ref=ins[t] if t in whole else _half(ins[t], 1 - c, HALF_AXIS[kinds[t]]), dst_ref=outs[t],
            send_sem=sems[0].at[t], recv_sem=sems[1].at[t], device_id=(x, y, 1 - c), device_id_type=MESH)
            for t in range(n)]

    def start(ins, outs, sems):
        for cp in copies(ins, outs, sems):
            cp.start()

    def finish(ins, outs, sems):
        for cp in copies(ins, outs, sems):
            cp.wait()

    def hshape(t):
        s = list(arrs[t].shape)
        if t not in whole:
            s[HALF_AXIS[kinds[t]]] //= 2
        return SDS(tuple(s), arrs[t].dtype)

    return _Comm(arrs, [hshape(t) for t in range(n)], [pltpu.SemaphoreType.DMA((n,))] * 2, start, finish)


def _add_half(full, got, core, axis, *, name):
    r, c = got.shape
    br, bc = (256 if r % 256 == 0 else 128), min(2048, c)
    off_r = (r // br) if axis == 0 else 0
    off_c = (c // bc) if axis == 1 else 0

    def body(c_ref, a_ref, b_ref, o_ref):
        o_ref[...] = (a_ref[...] + b_ref[...]).astype(BF16)

    return pl.pallas_call(
        body, grid_spec=pltpu.PrefetchScalarGridSpec(
            num_scalar_prefetch=1, grid=(r // br, c // bc),
            in_specs=[pl.BlockSpec((br, bc), lambda i, j, cr: (i + cr[0] * off_r, j + cr[0] * off_c)),
                      pl.BlockSpec((br, bc), lambda i, j, cr: (i, j))],
            out_specs=pl.BlockSpec((br, bc), lambda i, j, cr: (i, j))),
        out_shape=SDS((r, c), BF16), name=name, compiler_params=_params(("parallel", "parallel")))(core, full, got)


def _add_pair(a, b, *, name):
    r, c = a.shape
    br, bc = 256, min(2048, c)

    def body(a_ref, b_ref, o_ref):
        o_ref[...] = (a_ref[...] + b_ref[...]).astype(BF16)

    spec = pl.BlockSpec((br, bc), lambda i, j: (i, j))
    return pl.pallas_call(body, grid=(r // br, c // bc), in_specs=[spec, spec], out_specs=spec,
                          out_shape=SDS((r, c), BF16), name=name, compiler_params=_params(("parallel", "parallel")))(a, b)


def _scatter_comm(halves, kinds):
    n = len(halves)

    def plan(ins, outs, sems):
        send, recv, lsem = sems
        x, y, c = _me()
        k = 2 * x + y

        def to_chip(t, j):
            return pltpu.make_async_remote_copy(
                src_ref=_piece(kinds[t], ins[t], j), dst_ref=outs[t].at[k], send_sem=send.at[t, j],
                recv_sem=recv.at[t, k], device_id=(j // 2, j % 2, c), device_id_type=MESH)

        def from_chip(t, j):
            return pltpu.make_async_remote_copy(
                src_ref=_piece(kinds[t], ins[t], j), dst_ref=outs[t].at[j], send_sem=send.at[t, j],
                recv_sem=recv.at[t, j], device_id=(j // 2, j % 2, c), device_id_type=MESH)

        def own(t, j):
            return pltpu.make_async_copy(_piece(kinds[t], ins[t], j), outs[t].at[j], lsem.at[t])

        return k, to_chip, from_chip, own

    def start(ins, outs, sems):
        k, to_chip, _, own = plan(ins, outs, sems)
        for j in range(N_CHIPS):
            @pl.when(k != j)
            def _(j=j):
                for t in range(n):
                    to_chip(t, j).start()

            @pl.when(k == j)
            def _(j=j):
                for t in range(n):
                    own(t, j).start()

    def finish(ins, outs, sems):
        k, to_chip, from_chip, own = plan(ins, outs, sems)
        for j in range(N_CHIPS):
            @pl.when(k != j)
            def _(j=j):
                for t in range(n):
                    from_chip(t, j).wait_recv()
                for t in range(n):
                    to_chip(t, j).wait_send()

            @pl.when(k == j)
            def _(j=j):
                for t in range(n):
                    own(t, j).wait()

    return _Comm(halves, [SDS((N_CHIPS,) + _piece_shape(kinds[t], halves[t].shape), halves[t].dtype) for t in range(n)],
                 [pltpu.SemaphoreType.DMA((n, N_CHIPS))] * 2 + [pltpu.SemaphoreType.DMA((n,))], start, finish)


def _sum4(p, *, name):
    _, r, c = p.shape
    br = 256 if r % 256 == 0 else 128

    def body(p_ref, o_ref):
        o_ref[...] = ((p_ref[0].astype(F32) + p_ref[1].astype(F32)) + p_ref[2].astype(F32)) + p_ref[3].astype(F32)

    return pl.pallas_call(
        body, grid=(r // br,), in_specs=[pl.BlockSpec((N_CHIPS, br, c), lambda i: (0, i, 0))],
        out_specs=_rowblk(br, c), out_shape=SDS((r, c), F32), name=name, compiler_params=_params(("parallel",)))(p)


def _swap_comm(sums):
    return _exchange_comm(sums, [None] * len(sums), whole=tuple(range(len(sums))))


def _adamw_halves(mine, theirs, core, w, m, v, *, axis, tr, name):
    rows, cols = w.shape

    if axis == 0:
        nbh = rows // 2 // tr
        g_spec = pl.BlockSpec((tr, cols), lambda i, cr: (i % nbh, 0))
    else:
        g_spec = pl.BlockSpec((tr, cols // 2), lambda i, cr: (i, 0))

    def body(c_ref, a_ref, b_ref, w_ref, m_ref, v_ref, g_ref, d_ref, nm_ref, nv_ref):
        a, b = a_ref[...], b_ref[...]
        if axis == 0:
            g = jnp.where(pl.program_id(0) // nbh == c_ref[0], a, b)
        else:
            low = c_ref[0] == 0
            g = jnp.concatenate([jnp.where(low, a, b), jnp.where(low, b, a)], axis=1)
        g_ref[...] = g
        d, nm, nv = _adamw_math(w_ref[...], g, m_ref[...], v_ref[...])
        d_ref[...] = d
        nm_ref[...] = nm
        nv_ref[...] = nv

    nat = pl.BlockSpec((tr, cols), lambda i, cr: (i, 0))
    return pl.pallas_call(
        body, grid_spec=pltpu.PrefetchScalarGridSpec(
            num_scalar_prefetch=1, grid=(rows // tr,), in_specs=[g_spec, g_spec, nat, nat, nat], out_specs=[nat] * 4),
        out_shape=[SDS((rows, cols), F32)] * 4, name=name, compiler_params=_params(("arbitrary",)))(
            core, mine, theirs, w, m, v)


SMALL_ROWS, SMALL_COLS = 8, 1024


def _pack_small(vs):
    flat = jnp.concatenate([v.reshape(-1) for v in vs])
    return jnp.pad(flat, (0, SMALL_ROWS * SMALL_COLS - flat.shape[0])).reshape(SMALL_ROWS, SMALL_COLS)


def _unpack_small(packed, sizes):
    flat = packed.reshape(-1)
    out, o = [], 0
    for n in sizes:
        out.append(flat[o:o + n].reshape(1, n))
        o += n
    return out


def _all_reduce_small(v):
    n_dev = 8

    def body(v_ref, o_ref, land, send, recv):
        x, y, c = _me()
        me = 4 * x + 2 * y + c
        land[me] = v_ref[...]
        cps = []
        for r in range(1, n_dev):
            fx, fy, fc = (r >> 2) & 1, (r >> 1) & 1, r & 1
            peer = (x ^ fx, y ^ fy, c ^ fc)
            cps.append(pltpu.make_async_remote_copy(
                src_ref=v_ref, dst_ref=land.at[me], send_sem=send.at[r - 1], recv_sem=recv.at[r - 1],
                device_id=peer, device_id_type=MESH))
        for cp in cps:
            cp.start()
        for r in range(1, n_dev):
            fx, fy, fc = (r >> 2) & 1, (r >> 1) & 1, r & 1
            src = 4 * (x ^ fx) + 2 * (y ^ fy) + (c ^ fc)
            pltpu.make_async_remote_copy(
                src_ref=v_ref, dst_ref=land.at[src], send_sem=send.at[r - 1], recv_sem=recv.at[r - 1],
                device_id=(x ^ fx, y ^ fy, c ^ fc), device_id_type=MESH).wait_recv()
        for cp in cps:
            cp.wait_send()
        acc = land[0]
        for r in range(1, n_dev):
            acc = acc + land[r]
        o_ref[...] = acc

    vm = pl.BlockSpec(memory_space=pltpu.VMEM)
    return pl.pallas_call(
        body, in_specs=[vm], out_specs=vm, out_shape=SDS(v.shape, F32),
        scratch_shapes=[pltpu.VMEM((n_dev,) + v.shape, F32), pltpu.SemaphoreType.DMA((n_dev - 1,)),
                        pltpu.SemaphoreType.DMA((n_dev - 1,))],
        name="all_reduce_small")(v)


def kernel(x, mem, norm_gain, mem_norm_gain, w_in, b_forget, q_gain_a, k_gain_a, sinks_a, q_gain_b, k_gain_b, q_gain_c, k_gain_c, w_mem_kv, w_branch_a, w_branch_b, w_branch_c, w_out, loss_target, m_norm_gain, m_mem_norm_gain, m_w_in, m_b_forget, m_q_gain_a, m_k_gain_a, m_sinks_a, m_q_gain_b, m_k_gain_b, m_q_gain_c, m_k_gain_c, m_w_mem_kv, m_w_branch_a, m_w_branch_b, m_w_branch_c, m_w_out, v_norm_gain, v_mem_norm_gain, v_w_in, v_b_forget, v_q_gain_a, v_k_gain_a, v_sinks_a, v_q_gain_b, v_k_gain_b, v_q_gain_c, v_k_gain_c, v_w_mem_kv, v_w_branch_a, v_w_branch_b, v_w_branch_c, v_w_out):
    xi, yi, ci = lax.axis_index("x"), lax.axis_index("y"), lax.axis_index("c")
    chip = jnp.reshape(2 * xi + yi, (1,)).astype(jnp.int32)
    core = jnp.reshape(ci, (1,)).astype(jnp.int32)

    slabs = _pack_w_in(chip, jnp.transpose(w_in[0]))
    mine = [w_mem_kv[0].astype(BF16), w_branch_a[0].astype(BF16), w_branch_b[0].astype(BF16),
            w_branch_c[0].astype(BF16), w_out[0].astype(BF16)]
    w_main, w_fb = _merge_slabs(_all_gather_slabs(slabs))

    r = _local_step(x[0], mem[0], loss_target[0], w_main, w_fb, mine, norm_gain, mem_norm_gain,
                    b_forget, q_gain_a, k_gain_a, sinks_a, q_gain_b, k_gain_b, q_gain_c, k_gain_c, core=core)
    sums, theirs = r["sums"], r["theirs"]

    small_names = ["d_gain", "d_mem_gain", "d_bf", "d_qga", "d_kga", "d_sinks", "d_qgb", "d_kgb", "d_qgc", "d_kgc"]
    loss_part = (0.5 / D_MODEL) * jnp.sum(r["sq"], axis=1, keepdims=True)
    packed = _pack_small([r[n] for n in small_names] + [loss_part])
    red = _all_reduce_small(packed)
    small_w = [norm_gain, mem_norm_gain, b_forget, q_gain_a, k_gain_a, sinks_a, q_gain_b, k_gain_b, q_gain_c, k_gain_c]
    small_m = [m_norm_gain, m_mem_norm_gain, m_b_forget, m_q_gain_a, m_k_gain_a, m_sinks_a, m_q_gain_b, m_k_gain_b,
               m_q_gain_c, m_k_gain_c]
    small_v = [v_norm_gain, v_mem_norm_gain, v_b_forget, v_q_gain_a, v_k_gain_a, v_sinks_a, v_q_gain_b, v_k_gain_b,
               v_q_gain_c, v_k_gain_c]
    sizes = [w.shape[1] for w in small_w]
    s_d, s_m, s_v = _adamw(red, _pack_small(small_w), _pack_small(small_m), _pack_small(small_v), tr=8, name="adamw_small")
    g_small = _unpack_small(red, sizes + [1])
    loss = g_small[-1].reshape(())
    d_small, m_small, v_small = _unpack_small(s_d, sizes), _unpack_small(s_m, sizes), _unpack_small(s_v, sizes)

    gw_in, dw_in, mw_in, vw_in = _adamw_w_in(jnp.concatenate([chip, core]), sums[0], theirs[0], sums[1], theirs[1],
                                             jnp.transpose(w_in[0]), jnp.transpose(m_w_in[0]), jnp.transpose(v_w_in[0]))
    big = {}
    for t, nm, w, m, v in ((2, "w_mem_kv", w_mem_kv, m_w_mem_kv, v_w_mem_kv),
                           (3, "w_branch_a", w_branch_a, m_w_branch_a, v_w_branch_a),
                           (4, "w_branch_b", w_branch_b, m_w_branch_b, v_w_branch_b),
                           (5, "w_branch_c", w_branch_c, m_w_branch_c, v_w_branch_c),
                           (6, "w_out", w_out, m_w_out, v_w_out)):
        big[nm] = _adamw_halves(sums[t], theirs[t], core, w[0], m[0], v[0], axis=HALF_AXIS[t], tr=128,
                                name="adamw_" + nm)

    def collect(kind):
        sm = (g_small, d_small, m_small, v_small)[kind]
        win = (gw_in, dw_in, mw_in, vw_in)[kind]
        return ([sm[0], sm[1], jnp.transpose(win)[None]] + [a for a in sm[2:10]]
                + [big[n][kind][None] for n in ("w_mem_kv", "w_branch_a", "w_branch_b", "w_branch_c", "w_out")])

    return (loss, r["grad_x"][None], *collect(0), *collect(1), *collect(2), *collect(3))
```

```python
import functools

import numpy as np
import jax
import jax.numpy as jnp
from jax import lax
from jax.experimental import pallas as pl
from jax.experimental.pallas import tpu as pltpu

F32 = jnp.float32
BF16 = jnp.bfloat16
HI = lax.Precision.HIGHEST
SDS = jax.ShapeDtypeStruct
MESH = pl.DeviceIdType.MESH

D_MODEL = 2048
HEAD_DIM = 64
A_HEADS = 12
A_GROUP = 3
B_HEADS = 12
C_HEADS = 4
C_HEAD_DIM = 128
WINDOW = 128
EPS = 1e-6
NEG = -1e30
LANE = 128

QA, KA, VA, ZA = 0, 768, 1024, 1280
QB, KB, VB, ZB = 2048, 2816, 3584, 4352
QC, ZC = 5120, 5632
GATE = 6144
P_MAIN = 12288
N_FORGET = 12
FORGET_COL = 5120
SHARD_COLS = 3075
SLAB = 3200
SLAB_START = (0, 3072, 6016, 9088)
SLAB_SHIFT = (0, 3, 122, 125)
N_CHIPS = 4

ADAM_LR = 0.001
ADAM_B1 = 0.9
ADAM_B2 = 0.999
ADAM_EPS = 1e-08
ADAM_WD = 0.01
ADAM_STEP = 10

VMEM_LIMIT = 56 * 1024 * 1024
VMEM_WIDE = 62 * 1024 * 1024


def _params(sem, vmem=VMEM_LIMIT):
    return pltpu.CompilerParams(dimension_semantics=sem, vmem_limit_bytes=vmem)


def _win(tr, width, off):
    return pl.BlockSpec((pl.Element(tr), pl.Element(width)), lambda i, *_: (i * tr, off))


def _rowblk(tr, width):
    return pl.BlockSpec((tr, width), lambda i, *_: (i, 0))


def _const(shape):
    nd = len(shape)
    return pl.BlockSpec(shape, lambda *_: (0,) * nd)


def _rms(x, g):
    return x * lax.rsqrt(jnp.mean(x * x, axis=-1, keepdims=True) + EPS) * g


def _head_mean_impl(x2, bd):
    hi = x2.astype(BF16)
    lo = (x2 - hi.astype(F32)).astype(BF16)
    return _dot(hi, bd) + _dot(lo, bd)


@jax.custom_vjp
def _head_mean(x2, bd):
    return _head_mean_impl(x2, bd)


_head_mean.defvjp(lambda x2, bd: (_head_mean_impl(x2, bd), bd),
                  lambda bd, g: (_head_mean_impl(g, bd), jnp.zeros_like(bd)))


def _head_norm(x, g_tiled, bd):
    return x * lax.rsqrt(_head_mean(x * x, bd) + EPS) * g_tiled


def _silu(z):
    return z * jax.nn.sigmoid(z)


def _dot_nt(a, b):
    return lax.dot_general(a, b, (((1,), (1,)), ((), ())), preferred_element_type=F32)


def _dot_tn(a, b):
    return lax.dot_general(a, b, (((0,), (0,)), ((), ())), preferred_element_type=F32)


def _dot(a, b):
    return jnp.dot(a, b, preferred_element_type=F32)


def _swa_fn(qk, vz, qkp, vzp, qg, kg, sinks, bd, bias, first):
    q = _head_norm(qk[:, :768], qg, bd)
    k2 = jnp.concatenate([qkp[:, 768:], qk[:, 768:]], axis=0)
    k2 = _head_norm(k2, kg, bd[:256, :256])
    v2 = jnp.concatenate([vzp[:, :256], vz[:, :256]], axis=0)
    z = vz[:, 256:]
    cols = A_GROUP * WINDOW
    kj = lax.broadcasted_iota(jnp.int32, (2 * WINDOW, cols), 0)
    no_prev = kj < WINDOW * first.astype(jnp.int32)
    qtb = jnp.transpose(q).astype(BF16)
    kb = k2.astype(BF16)
    vtb = jnp.transpose(v2).astype(BF16)
    outs = [None] * A_HEADS
    for g in range(A_HEADS // A_GROUP):
        heads = [A_GROUP * g + u for u in range(A_GROUP)]
        qs = jnp.concatenate([qtb[64 * h:64 * h + 64, :] for h in heads], axis=1)
        s = _dot(kb[:, 64 * g:64 * g + 64], qs) * (HEAD_DIM ** -0.5) + bias[g]
        s = jnp.where(no_prev, NEG, s)
        sink = jnp.concatenate([jnp.broadcast_to(sinks[:, h:h + 1], (1, WINDOW)) for h in heads], axis=1)
        m = lax.stop_gradient(jnp.maximum(jnp.max(s, axis=0, keepdims=True), sink))
        p = jnp.exp(s - m)
        den = jnp.sum(p, axis=0, keepdims=True) + jnp.exp(sink - m)
        o = _dot(vtb[64 * g:64 * g + 64, :], (p * (1.0 / den)).astype(BF16))
        for u, h in enumerate(heads):
            outs[h] = o[:, WINDOW * u:WINDOW * u + WINDOW]
    return jnp.transpose(jnp.concatenate(outs, axis=0)) * _silu(z)


def _swa_bias():
    qi = np.arange(WINDOW)[None, :]
    kj = np.arange(2 * WINDOW)[:, None]
    rel = qi + WINDOW - kj
    valid = (rel >= 0) & (rel < WINDOW)
    out = np.zeros((A_HEADS // A_GROUP, 2 * WINDOW, A_GROUP * WINDOW), np.float32)
    for h in range(A_HEADS):
        slope = np.float32(2.0 ** (-8.0 * (h + 1) / A_HEADS))
        blk = np.where(valid, -slope * rel.astype(np.float32), np.float32(NEG))
        g, u = divmod(h, A_GROUP)
        out[g, :, WINDOW * u:WINDOW * u + WINDOW] = blk
    return jnp.asarray(out)


def _mem_fn(qz, mkv, qg, kg, bd):
    q = _head_norm(qz[:, :512], qg, bd).astype(BF16)
    k = _head_norm(mkv[:, :512], kg, bd).astype(BF16)
    v = mkv[:, 512:].astype(BF16)
    z = qz[:, 512:]
    outs = []
    for h in range(C_HEADS):
        sl = slice(128 * h, 128 * h + 128)
        s = _dot_nt(q[:, sl], k[:, sl]) * (C_HEAD_DIM ** -0.5)
        m = lax.stop_gradient(jnp.max(s, axis=-1, keepdims=True))
        p = jnp.exp(s - m)
        den = jnp.sum(p, axis=-1, keepdims=True)
        outs.append(_dot((p * (1.0 / den)).astype(BF16), v[:, sl]))
    return jnp.concatenate(outs, axis=1) * _silu(z)


def _qn_fn(q, g, bd):
    return _head_norm(q, g, bd) * (HEAD_DIM ** -0.5)


def _kn_fn(k, g, bd):
    return _head_norm(k, g, bd)


def _block_diag(width, hd):
    i = np.arange(width) // hd
    return jnp.asarray((i[:, None] == i[None, :]).astype(np.float32) / hd, BF16)


def _head_sum(width, hd):
    i = np.arange(width) // hd
    return jnp.asarray((i[:, None] == np.arange(LANE)[None, :]).astype(np.float32))


def _rms_fwd(x, g, *, tr, name):
    rows, dm = x.shape

    def body(x_ref, g_ref, o_ref):
        o_ref[...] = _rms(x_ref[...], g_ref[...]).astype(BF16)

    return pl.pallas_call(
        body, grid=(rows // tr,),
        in_specs=[_rowblk(tr, dm), _const((1, dm))],
        out_specs=_rowblk(tr, dm),
        out_shape=SDS((rows, dm), BF16), name=name,
        compiler_params=_params(("parallel",)))(x, g)


def _rms_bwd(x, g, dy, resid, *, tr, name, comm=None):
    rows, dm = x.shape
    want_dx = resid is not None
    n_in = 4 if want_dx else 3
    n_out = 2 if want_dx else 1
    c_in = len(comm.ins) if comm else 0
    c_out = len(comm.out_shapes) if comm else 0
    nb = rows // tr

    def body(*refs):
        x_ref, g_ref, dy_ref = refs[:3]
        r_ref = refs[3] if want_dx else None
        cin = refs[n_in:n_in + c_in]
        outs = refs[n_in + c_in:n_in + c_in + n_out]
        dg_ref = outs[-1]
        cout = refs[n_in + c_in + n_out:n_in + c_in + n_out + c_out]
        csem = refs[n_in + c_in + n_out + c_out:]

        if comm:
            @pl.when(pl.program_id(0) == 0)
            def _():
                comm.start(cin, cout, csem)

        _, vjp = jax.vjp(_rms, x_ref[...], g_ref[...])
        dx, dg = vjp(dy_ref[...])

        @pl.when(pl.program_id(0) == 0)
        def _():
            dg_ref[...] = jnp.zeros_like(dg_ref)

        dg_ref[...] += dg
        if want_dx:
            outs[0][...] = r_ref[...] + dx

        if comm:
            @pl.when(pl.program_id(0) == nb - 1)
            def _():
                comm.finish(cin, cout, csem)

    hbm = pl.BlockSpec(memory_space=pl.ANY)
    ins = [x, g, dy] + ([resid] if want_dx else []) + (list(comm.ins) if comm else [])
    in_specs = ([_rowblk(tr, dm), _const((1, dm)), _rowblk(tr, dm)] + ([_rowblk(tr, dm)] if want_dx else [])
                + [hbm] * c_in)
    out_specs = ([_rowblk(tr, dm)] if want_dx else []) + [_const((1, dm))] + [hbm] * c_out
    out_shape = (([SDS((rows, dm), F32)] if want_dx else []) + [SDS((1, dm), F32)]
                 + (list(comm.out_shapes) if comm else []))
    res = pl.pallas_call(
        body, grid=(nb,), in_specs=in_specs, out_specs=out_specs, out_shape=out_shape,
        scratch_shapes=list(comm.sems) if comm else [], name=name, compiler_params=_params(("arbitrary",)))(*ins)
    return (list(res[:n_out]), list(res[n_out:])) if comm else res


class _Comm:
    def __init__(self, ins, out_shapes, sems, start, finish):
        self.ins, self.out_shapes, self.sems, self.start, self.finish = list(ins), list(out_shapes), list(sems), start, finish


def _matmul(a, b, *, dims, out_dtype, tm, tn, tk, name, add=None, comms=(), vmem=VMEM_LIMIT):
    a_list = list(a) if isinstance(a, (list, tuple)) else [a]
    b_list = list(b) if isinstance(b, (list, tuple)) else [b]
    assert len(a_list) == 1 or dims == "nt"
    assert len(b_list) == 1 or dims == "tn"
    if dims == "tn":
        kdim, m = a_list[0].shape
    else:
        m, kdim = a_list[0].shape[0], sum(p.shape[1] for p in a_list)
    n = b_list[0].shape[0] if dims == "nt" else sum(p.shape[1] for p in b_list)
    tm, tn, tk = min(tm, m), min(tn, n), min(tk, kdim)
    assert m % tm == 0 and n % tn == 0 and kdim % tk == 0, (name, m, n, kdim)
    ni, nj, nk = m // tm, n // tn, kdim // tk
    a_rng, b_rng, pos = [], [], 0
    for p in a_list:
        assert len(a_list) == 1 or p.shape[1] % tk == 0
        a_rng.append((pos, p.shape[1] // tk if len(a_list) > 1 else nk))
        pos += a_rng[-1][1]
    pos = 0
    for p in b_list:
        assert len(b_list) == 1 or p.shape[1] % tn == 0
        b_rng.append((pos, p.shape[1] // tn if len(b_list) > 1 else nj))
        pos += b_rng[-1][1]
    has_add = add is not None
    n_mm_in = len(a_list) + len(b_list) + (1 if has_add else 0)
    c_in = [len(c.ins) for c in comms]
    c_out = [len(c.out_shapes) for c in comms]
    c_sem = [len(c.sems) for c in comms]

    def body(*refs):
        a_refs, b_refs = refs[:len(a_list)], refs[len(a_list):len(a_list) + len(b_list)]
        add_ref = refs[n_mm_in - 1] if has_add else None
        pos = n_mm_in
        cin = []
        for cnt in c_in:
            cin.append(refs[pos:pos + cnt])
            pos += cnt
        o_ref = refs[pos]
        pos += 1
        cout = []
        for cnt in c_out:
            cout.append(refs[pos:pos + cnt])
            pos += cnt
        acc = refs[pos]
        pos += 1
        csem = []
        for cnt in c_sem:
            csem.append(refs[pos:pos + cnt])
            pos += cnt
        i, j, k = pl.program_id(0), pl.program_id(1), pl.program_id(2)

        if comms:
            @pl.when((i == 0) & (j == 0) & (k == 0))
            def _():
                for c, ci, co, cs in zip(comms, cin, cout, csem):
                    c.start(ci, co, cs)

        def accumulate(a_ref, b_ref, first_k, later_k):
            if dims == "nn":
                part = _dot(a_ref[...], b_ref[...])
            elif dims == "nt":
                part = _dot_nt(a_ref[...], b_ref[...])
            else:
                part = _dot_tn(a_ref[...], b_ref[...])

            if first_k:
                @pl.when(k == 0)
                def _():
                    acc[...] = part + add_ref[...] if has_add else part

            if later_k:
                @pl.when(k > 0)
                def _():
                    acc[...] += part

        if len(a_list) > 1:
            for a_ref, (k0, cnt) in zip(a_refs, a_rng):
                @pl.when((k >= k0) & (k < k0 + cnt))
                def _(a_ref=a_ref, k0=k0, cnt=cnt):
                    accumulate(a_ref, b_refs[0], k0 == 0, k0 + cnt > 1)
        elif len(b_list) > 1:
            for b_ref, (j0, cnt) in zip(b_refs, b_rng):
                @pl.when((j >= j0) & (j < j0 + cnt))
                def _(b_ref=b_ref):
                    accumulate(a_refs[0], b_ref, True, nk > 1)
        else:
            accumulate(a_refs[0], b_refs[0], True, nk > 1)

        @pl.when(k == nk - 1)
        def _():
            o_ref[...] = acc[...].astype(out_dtype)

        if comms:
            @pl.when((i == ni - 1) & (j == nj - 1) & (k == nk - 1))
            def _():
                for c, ci, co, cs in zip(comms, cin, cout, csem):
                    c.finish(ci, co, cs)

    def a_spec(k0, cnt):
        if dims == "tn":
            return pl.BlockSpec((tk, tm), lambda i, j, k: (k, i))
        return pl.BlockSpec((tm, tk), lambda i, j, k: (i, jnp.clip(k - k0, 0, cnt - 1)))

    def b_spec(j0, cnt):
        if dims == "nt":
            return pl.BlockSpec((tn, tk), lambda i, j, k: (j, k))
        return pl.BlockSpec((tk, tn), lambda i, j, k: (k, jnp.clip(j - j0, 0, cnt - 1)))

    o_spec = pl.BlockSpec((tm, tn), lambda i, j, k: (i, j))
    hbm = pl.BlockSpec(memory_space=pl.ANY)
    ins = a_list + b_list + ([add] if has_add else []) + [x for c in comms for x in c.ins]
    in_specs = ([a_spec(*r) for r in a_rng] + [b_spec(*r) for r in b_rng] + ([o_spec] if has_add else [])
                + [hbm] * sum(c_in))
    out_specs = [o_spec] + [hbm] * sum(c_out)
    out_shape = [SDS((m, n), out_dtype)] + [s for c in comms for s in c.out_shapes]
    scratch = [pltpu.VMEM((tm, tn), F32)] + [s for c in comms for s in c.sems]
    sem = ("arbitrary",) * 3 if comms else ("parallel", "parallel", "arbitrary")
    res = pl.pallas_call(
        body, grid=(ni, nj, nk), in_specs=in_specs, out_specs=out_specs, out_shape=out_shape, scratch_shapes=scratch,
        name=name, compiler_params=_params(sem, vmem))(*ins)
    if not comms:
        return res[0]
    outs, pos = [], 1
    for cnt in c_out:
        outs.append(list(res[pos:pos + cnt]))
        pos += cnt
    return res[0], outs


def _swa_specs(nb):
    prev = lambda off: pl.BlockSpec((pl.Element(WINDOW), pl.Element(1024)),
                                    lambda n: (jnp.maximum(n - 1, 0) * WINDOW, off))
    return [_win(WINDOW, 1024, QA), _win(WINDOW, 1024, VA), prev(QA), prev(VA),
            _const((1, 768)), _const((1, 256)), _const((1, LANE)), _const((768, 768)),
            _const((A_HEADS // A_GROUP, 2 * WINDOW, A_GROUP * WINDOW))]


def _swa_fwd(proj, qg, kg, sinks, bd, bias):
    s = proj.shape[0]
    nb = s // WINDOW

    def body(qk_ref, vz_ref, qkp_ref, vzp_ref, qg_ref, kg_ref, sk_ref, bd_ref, bias_ref, o_ref):
        first = pl.program_id(0) == 0
        o_ref[...] = _swa_fn(qk_ref[...], vz_ref[...], qkp_ref[...], vzp_ref[...], qg_ref[...], kg_ref[...],
                             sk_ref[...], bd_ref[...], bias_ref[...], first).astype(BF16)

    return pl.pallas_call(
        body, grid=(nb,), in_specs=_swa_specs(nb), out_specs=_rowblk(WINDOW, 768),
        out_shape=SDS((s, 768), BF16), name="swa_fwd",
        compiler_params=_params(("parallel",)))(proj, proj, proj, proj, qg, kg, sinks, bd, bias)


def _swa_bwd(proj, qg, kg, sinks, bd, bias, dga):
    s = proj.shape[0]
    nb = s // WINDOW

    def body(qk_ref, vz_ref, qkp_ref, vzp_ref, qg_ref, kg_ref, sk_ref, bd_ref, bias_ref, dg_ref,
             dcur_ref, dprev_ref, dqg_ref, dkg_ref, dsk_ref):
        first = pl.program_id(0) == 0
        bd_v = bd_ref[...]
        bias_v = bias_ref[...]
        fn = lambda qk, vz, qkp, vzp, qg_, kg_, sk: _swa_fn(qk, vz, qkp, vzp, qg_, kg_, sk, bd_v, bias_v, first)
        _, vjp = jax.vjp(fn, qk_ref[...], vz_ref[...], qkp_ref[...], vzp_ref[...], qg_ref[...], kg_ref[...], sk_ref[...])
        dqk, dvz, dqkp, dvzp, dqg, dkg, dsk = vjp(dg_ref[...])

        @pl.when(first)
        def _():
            dqg_ref[...] = jnp.zeros_like(dqg_ref)
            dkg_ref[...] = jnp.zeros_like(dkg_ref)
            dsk_ref[...] = jnp.zeros_like(dsk_ref)

        dqg_ref[...] += dqg
        dkg_ref[...] += dkg
        dsk_ref[...] += dsk
        dcur_ref[...] = jnp.concatenate([dqk, dvz], axis=1)
        dprev_ref[...] = jnp.concatenate([dqkp[:, 768:], dvzp[:, :256]], axis=1)

    return pl.pallas_call(
        body, grid=(nb,), in_specs=_swa_specs(nb) + [_rowblk(WINDOW, 768)],
        out_specs=[_rowblk(WINDOW, 2048), pl.BlockSpec((None, WINDOW, 512), lambda n: (n, 0, 0)),
                   _const((1, 768)), _const((1, 256)), _const((1, LANE))],
        out_shape=[SDS((s, 2048), F32), SDS((nb, WINDOW, 512), F32), SDS((1, 768), F32), SDS((1, 256), F32),
                   SDS((1, LANE), F32)],
        name="swa_bwd", compiler_params=_params(("arbitrary",)))(proj, proj, proj, proj, qg, kg, sinks, bd, bias, dga)


def _swa_combine(dcur, dprev):
    s = dcur.shape[0]
    nb = s // WINDOW

    def body(c_ref, p_ref, o_ref):
        c = c_ref[...]
        nxt = jnp.where(pl.program_id(0) == nb - 1, 0.0, p_ref[...])
        o_ref[...] = jnp.concatenate([c[:, :768], c[:, 768:1280] + nxt, c[:, 1280:]], axis=1).astype(BF16)

    return pl.pallas_call(
        body, grid=(nb,),
        in_specs=[_rowblk(WINDOW, 2048), pl.BlockSpec((None, WINDOW, 512), lambda n: (jnp.minimum(n + 1, nb - 1), 0, 0))],
        out_specs=_rowblk(WINDOW, 2048), out_shape=SDS((s, 2048), BF16), name="swa_combine",
        compiler_params=_params(("parallel",)))(dcur, dprev)


def _mem_fwd(proj, mkv, qg, kg, bd, *, tr):
    s = proj.shape[0]

    def body(qz_ref, mkv_ref, qg_ref, kg_ref, bd_ref, o_ref):
        o_ref[...] = _mem_fn(qz_ref[...], mkv_ref[...], qg_ref[...], kg_ref[...], bd_ref[...]).astype(BF16)

    return pl.pallas_call(
        body, grid=(s // tr,),
        in_specs=[_win(tr, 1024, QC), _const(mkv.shape), _const((1, 512)), _const((1, 512)), _const((512, 512))],
        out_specs=_rowblk(tr, 512), out_shape=SDS((s, 512), BF16), name="mem_fwd",
        compiler_params=_params(("parallel",)))(proj, mkv, qg, kg, bd)


def _mem_bwd(proj, mkv, qg, kg, bd, dgc, *, tr):
    s = proj.shape[0]

    def body(qz_ref, mkv_ref, qg_ref, kg_ref, bd_ref, dg_ref, dqz_ref, dmkv_ref, dqg_ref, dkg_ref):
        bd_v = bd_ref[...]
        fn = lambda qz, mkv_, qg_, kg_: _mem_fn(qz, mkv_, qg_, kg_, bd_v)
        _, vjp = jax.vjp(fn, qz_ref[...], mkv_ref[...], qg_ref[...], kg_ref[...])
        dqz, dmkv, dqg, dkg = vjp(dg_ref[...])

        @pl.when(pl.program_id(0) == 0)
        def _():
            dmkv_ref[...] = jnp.zeros_like(dmkv_ref)
            dqg_ref[...] = jnp.zeros_like(dqg_ref)
            dkg_ref[...] = jnp.zeros_like(dkg_ref)

        dmkv_ref[...] += dmkv
        dqg_ref[...] += dqg
        dkg_ref[...] += dkg
        dqz_ref[...] = dqz.astype(BF16)

    return pl.pallas_call(
        body, grid=(s // tr,),
        in_specs=[_win(tr, 1024, QC), _const(mkv.shape), _const((1, 512)), _const((1, 512)), _const((512, 512)),
                  _rowblk(tr, 512)],
        out_specs=[_rowblk(tr, 1024), _const(mkv.shape), _const((1, 512)), _const((1, 512))],
        out_shape=[SDS((s, 1024), BF16), SDS(mkv.shape, F32), SDS((1, 512), F32), SDS((1, 512), F32)],
        name="mem_bwd", compiler_params=_params(("arbitrary",)))(proj, mkv, qg, kg, bd, dgc)


def _log_sigmoid(x):
    return jnp.minimum(x, 0.0) - jnp.log1p(jnp.exp(-jnp.abs(x)))


FOX_TQ, FOX_TK = 512, 512
FOX_FWD_TQ, FOX_FWD_TK = 512, 1024


def _fox_tiles(s):
    return min(FOX_TQ, s), min(FOX_TK, s)


AUG = 128 * B_HEADS
COL_A, COL_B = 64, 67


def _split3(c):
    hi = c.astype(BF16)
    r1 = c - hi.astype(F32)
    mid = r1.astype(BF16)
    lo = (r1 - mid.astype(F32)).astype(BF16)
    return hi, mid, lo


def _expand_mats():
    def mat(col0):
        e = np.zeros((768 + 3 * LANE, AUG), np.float32)
        for h in range(B_HEADS):
            for d in range(HEAD_DIM):
                e[64 * h + d, 128 * h + d] = 1.0
            for part in range(3):
                e[768 + LANE * part + h, 128 * h + col0 + part] = 1.0
        return e

    def ones(col0):
        o = np.zeros((1, AUG), np.float32)
        for h in range(B_HEADS):
            o[0, 128 * h + col0:128 * h + col0 + 3] = 1.0
        return o

    ea, eb, oa, ob = mat(COL_A), mat(COL_B), ones(COL_A), ones(COL_B)
    bf = lambda m: jnp.asarray(m, BF16)
    return dict(ea=bf(ea), eb=bf(eb), eat=bf(ea.T), ebt=bf(eb.T), oa=jnp.asarray(oa), ob=jnp.asarray(ob),
                oat=jnp.asarray(oa.T), obt=jnp.asarray(ob.T))


def _augment(data_bf16, triple, emat, ones_row):
    parts = [data_bf16] + (list(triple) if triple is not None else [jnp.zeros((data_bf16.shape[0], LANE), BF16)] * 3)
    cat = jnp.concatenate(parts, axis=1)
    wide = _dot(cat, emat) if emat.shape[1] == AUG else _dot_nt(emat, cat)
    if ones_row is not None:
        wide = wide + ones_row
    return wide


def _compact(wide):
    return jnp.concatenate([wide[:, 128 * h:128 * h + 64] for h in range(wide.shape[1] // 128)], axis=1)


def _lane_of_heads(wide, col, first=0):
    rows = wide.shape[0]
    lane = lax.broadcasted_iota(jnp.int32, (rows, LANE), 1)
    out = jnp.zeros((rows, LANE), F32)
    for h in range(wide.shape[1] // 128):
        out = jnp.where(lane == first + h, wide[:, 128 * h + col:128 * h + col + 1], out)
    return out


def _fox2_prep(proj, fbl, qg, kg, bfor, bd, em, *, tr):
    s = proj.shape[0]
    tri = jnp.asarray(np.tril(np.ones((tr, tr), np.float32)))

    def body(q_ref, k_ref, v_ref, fb_ref, qg_ref, kg_ref, bf_ref, bd_ref, tri_ref, ea_ref, eb_ref, eat_ref, ebt_ref,
             oa_ref, oat_ref, obt_ref, qat_ref, ka_ref, kat_ref, va_ref, vat_ref, qn_ref, c_ref, carry):
        @pl.when(pl.program_id(0) == 0)
        def _():
            carry[...] = jnp.zeros_like(carry)

        bd_v = bd_ref[...]
        lane = lax.broadcasted_iota(jnp.int32, (tr, LANE), 1)
        logf = jnp.where(lane < N_FORGET, _log_sigmoid(fb_ref[...] + bf_ref[...]), 0.0)
        c = jnp.dot(tri_ref[...], logf, precision=HI, preferred_element_type=F32) + carry[...]
        c_ref[...] = c
        carry[...] = c[tr - 1:tr, :]
        qn = _qn_fn(q_ref[...], qg_ref[...], bd_v).astype(BF16)
        kn = _kn_fn(k_ref[...], kg_ref[...], bd_v).astype(BF16)
        qn_ref[...] = qn
        qat_ref[...] = _augment(qn, _split3(c), eat_ref[...], obt_ref[...]).astype(BF16)
        ck = _split3(-c)
        ka_ref[...] = _augment(kn, ck, eb_ref[...], oa_ref[...]).astype(BF16)
        kat_ref[...] = _augment(kn, ck, ebt_ref[...], oat_ref[...]).astype(BF16)
        vb = v_ref[...].astype(BF16)
        va_ref[...] = _augment(vb, None, ea_ref[...], oa_ref[...]).astype(BF16)
        vat_ref[...] = _augment(vb, None, eat_ref[...], oat_ref[...]).astype(BF16)

    emat, emat_t = _const((768 + 3 * LANE, AUG)), _const((AUG, 768 + 3 * LANE))
    return pl.pallas_call(
        body, grid=(s // tr,),
        in_specs=[_win(tr, 768, QB), _win(tr, 768, KB), _win(tr, 768, VB), _rowblk(tr, LANE), _const((1, 768)),
                  _const((1, 768)), _const((1, LANE)), _const((768, 768)), _const((tr, tr)), emat, emat, emat_t, emat_t,
                  _const((1, AUG)), _const((AUG, 1)), _const((AUG, 1))],
        out_specs=[pl.BlockSpec((AUG, tr), lambda i: (0, i)), _rowblk(tr, AUG), pl.BlockSpec((AUG, tr), lambda i: (0, i)),
                   _rowblk(tr, AUG), pl.BlockSpec((AUG, tr), lambda i: (0, i)), _rowblk(tr, 768), _rowblk(tr, LANE)],
        out_shape=[SDS((AUG, s), BF16), SDS((s, AUG), BF16), SDS((AUG, s), BF16), SDS((s, AUG), BF16),
                   SDS((AUG, s), BF16), SDS((s, 768), BF16), SDS((s, LANE), F32)],
        scratch_shapes=[pltpu.VMEM((1, LANE), F32)], name="fox_prep",
        compiler_params=_params(("arbitrary",)))(proj, proj, proj, fbl, qg, kg, bfor, bd, tri, em["ea"], em["eb"], em["eat"],
                                                 em["ebt"], em["oa"], em["oat"], em["obt"])


def _fox2_fwd(proj, qat, ka, vat):
    s = proj.shape[0]
    tq, tk = min(FOX_FWD_TQ, s), min(FOX_FWD_TK, s)
    nq, nk = s // tq, s // tk

    def last_k(i):
        return (i * tq + tq - 1) // tk

    def body(qt_ref, k_ref, vt_ref, z_ref, gb_ref, yb_ref, lse_ref, acc, m_s):
        i, j = pl.program_id(0), pl.program_id(1)

        @pl.when(j == 0)
        def _():
            acc[...] = jnp.zeros_like(acc)
            m_s[...] = jnp.full_like(m_s, NEG)

        def tile(masked):
            if masked:
                kpos = j * tk + lax.broadcasted_iota(jnp.int32, (tk, tq), 0)
                qpos = i * tq + lax.broadcasted_iota(jnp.int32, (tk, tq), 1)
                mask = kpos <= qpos
            for h in range(B_HEADS):
                sl = slice(128 * h, 128 * h + 128)
                sc = _dot(k_ref[:, sl], qt_ref[sl, :])
                if masked:
                    sc = jnp.where(mask, sc, NEG)
                m_prev = m_s[h:h + 1, :]
                m_new = jnp.maximum(m_prev, jnp.max(sc, axis=0, keepdims=True))
                p = jnp.exp(sc - m_new).astype(BF16)
                acc[sl, :] = jnp.exp(m_prev - m_new) * acc[sl, :] + _dot(vt_ref[sl, :], p)
                m_s[h:h + 1, :] = m_new

        full = j * tk + tk - 1 <= i * tq

        @pl.when(full)
        def _():
            tile(False)

        @pl.when(jnp.logical_and(jnp.logical_not(full), j <= last_k(i)))
        def _():
            tile(True)

        @pl.when(j == nk - 1)
        def _():
            outs = []
            row = lax.broadcasted_iota(jnp.int32, (LANE, tq), 0)
            lse_t = jnp.zeros((LANE, tq), F32)
            for h in range(B_HEADS):
                l_row = acc[128 * h + COL_A:128 * h + COL_A + 1, :]
                outs.append(acc[128 * h:128 * h + 64, :] * (1.0 / l_row))
                lse_t = jnp.where(row == h, m_s[h:h + 1, :] + jnp.log(l_row), lse_t)
            y = jnp.transpose(jnp.concatenate(outs, axis=0))
            yb_ref[...] = y
            gb_ref[...] = (y * _silu(z_ref[...])).astype(BF16)
            lse_ref[...] = jnp.transpose(lse_t)

    kcol = lambda i, j: (0, jnp.minimum(j, last_k(i)))
    return pl.pallas_call(
        body, grid=(nq, nk),
        in_specs=[pl.BlockSpec((AUG, tq), lambda i, j: (0, i)),
                  pl.BlockSpec((tk, AUG), lambda i, j: (jnp.minimum(j, last_k(i)), 0)),
                  pl.BlockSpec((AUG, tk), kcol),
                  pl.BlockSpec((pl.Element(tq), pl.Element(768)), lambda i, j: (i * tq, ZB))],
        out_specs=[pl.BlockSpec((tq, 768), lambda i, j: (i, 0)), pl.BlockSpec((tq, 768), lambda i, j: (i, 0)),
                   pl.BlockSpec((tq, LANE), lambda i, j: (i, 0))],
        out_shape=[SDS((s, 768), BF16), SDS((s, 768), F32), SDS((s, LANE), F32)],
        scratch_shapes=[pltpu.VMEM((AUG, tq), F32), pltpu.VMEM((16, tq), F32)],
        name="fox_fwd", compiler_params=_params(("parallel", "arbitrary")))(qat, ka, vat, proj)


def _fox2_bwd_pre(proj, yb, dgb, qn, c, lse, hsum, em, *, tr):
    s = proj.shape[0]

    def body(z_ref, y_ref, dg_ref, qn_ref, c_ref, lse_ref, hs_ref, ea_ref, eat_ref, ob_ref, obt_ref,
             qa_ref, qat_ref, dya_ref, dyat_ref, dz_ref):
        z, y, dg = z_ref[...], y_ref[...], dg_ref[...]
        sg = jax.nn.sigmoid(z)
        dy = dg * (z * sg)
        dz_ref[...] = (dg * y * (sg * (1.0 + z * (1.0 - sg)))).astype(BF16)
        delta = jnp.dot(dy * y, hs_ref[...], precision=HI, preferred_element_type=F32)
        e, et = ea_ref[...], eat_ref[...]
        dyb, dl3 = dy.astype(BF16), _split3(-delta)
        dya_ref[...] = _augment(dyb, dl3, e, None).astype(BF16)
        dyat_ref[...] = _augment(dyb, dl3, et, None).astype(BF16)
        qn, cl3 = qn_ref[...], _split3(c_ref[...] - lse_ref[...])
        qa_ref[...] = _augment(qn, cl3, e, ob_ref[...]).astype(BF16)
        qat_ref[...] = _augment(qn, cl3, et, obt_ref[...]).astype(BF16)

    return pl.pallas_call(
        body, grid=(s // tr,),
        in_specs=[_win(tr, 768, ZB), _rowblk(tr, 768), _rowblk(tr, 768), _rowblk(tr, 768), _rowblk(tr, LANE),
                  _rowblk(tr, LANE), _const((768, LANE)), _const((768 + 3 * LANE, AUG)), _const((AUG, 768 + 3 * LANE)),
                  _const((1, AUG)), _const((AUG, 1))],
        out_specs=[_rowblk(tr, AUG), pl.BlockSpec((AUG, tr), lambda i: (0, i)), _rowblk(tr, AUG),
                   pl.BlockSpec((AUG, tr), lambda i: (0, i)), _rowblk(tr, 768)],
        out_shape=[SDS((s, AUG), BF16), SDS((AUG, s), BF16), SDS((s, AUG), BF16), SDS((AUG, s), BF16),
                   SDS((s, 768), BF16)], name="fox_bwd_pre",
        compiler_params=_params(("parallel",)))(proj, yb, dgb, qn, c, lse, hsum, em["ea"], em["eat"], em["ob"], em["obt"])


def _fox2_bwd(qb, qbt, ka, kat, va, dya, dyat):
    s = qb.shape[0]
    tq, tk = _fox_tiles(s)
    nq, nk = s // tq, s // tk
    ng = 2
    gh = B_HEADS // ng
    gw = 128 * gh

    def first_q(j):
        return (j * tk) // tq

    def body(q_ref, qt_ref, k_ref, kt_ref, v_ref, dy_ref, dyt_ref, dq_hbm, dk_ref, dv_ref, dck_ref,
             dq_acc, dk_acc, dv_acc, sem):
        g, j, i = pl.program_id(0), pl.program_id(1), pl.program_id(2)

        @pl.when((j == 0) & (i == 0))
        def _():
            dq_acc[...] = jnp.zeros_like(dq_acc)

        @pl.when(i == 0)
        def _():
            dk_acc[...] = jnp.zeros_like(dk_acc)
            dv_acc[...] = jnp.zeros_like(dv_acc)

        def tile(masked):
            if masked:
                kpos = j * tk + lax.broadcasted_iota(jnp.int32, (tk, tq), 0)
                qpos = i * tq + lax.broadcasted_iota(jnp.int32, (tk, tq), 1)
                mask = kpos <= qpos
            cols = pl.ds(pl.multiple_of(i * tq, tq), tq)
            for h in range(gh):
                sl = slice(128 * h, 128 * h + 128)
                sc = _dot(k_ref[:, sl], qt_ref[sl, :])
                if masked:
                    sc = jnp.where(mask, sc, NEG)
                p = jnp.exp(sc)
                ds = (p * _dot(v_ref[:, sl], dyt_ref[sl, :])).astype(BF16)
                dv_acc[:, sl] += _dot(p.astype(BF16), dy_ref[:, sl])
                dk_acc[:, sl] += _dot(ds, q_ref[:, sl])
                dq_acc[sl, cols] += _dot(kt_ref[sl, :], ds)

        full = j * tk + tk - 1 <= i * tq

        @pl.when(full)
        def _():
            tile(False)

        @pl.when(jnp.logical_and(jnp.logical_not(full), i >= first_q(j)))
        def _():
            tile(True)

        @pl.when(i == nq - 1)
        def _():
            dkw = dk_acc[...]
            dk_ref[...] = _compact(dkw)
            dv_ref[...] = _compact(dv_acc[...]).astype(BF16)
            dck_ref[...] = -_lane_of_heads(dkw, COL_B, gh * g)

        @pl.when((j == nk - 1) & (i == nq - 1))
        def _():
            cp = pltpu.make_async_copy(dq_acc, dq_hbm.at[pl.ds(pl.multiple_of(g * gw, gw), gw)], sem)
            cp.start()
            cp.wait()

    qrow = pl.BlockSpec((tq, gw), lambda g, j, i: (jnp.maximum(i, first_q(j)), g))
    qcol = pl.BlockSpec((gw, tq), lambda g, j, i: (g, jnp.maximum(i, first_q(j))))
    krow = pl.BlockSpec((tk, gw), lambda g, j, i: (j, g))
    kcol = pl.BlockSpec((gw, tk), lambda g, j, i: (g, j))
    kout = pl.BlockSpec((tk, gw // 2), lambda g, j, i: (j, g))
    return pl.pallas_call(
        body, grid=(ng, nk, nq),
        in_specs=[qrow, qcol, krow, kcol, krow, qrow, qcol],
        out_specs=[pl.BlockSpec(memory_space=pl.ANY), kout, kout,
                   pl.BlockSpec((None, tk, LANE), lambda g, j, i: (g, j, 0))],
        out_shape=[SDS((AUG, s), F32), SDS((s, 768), F32), SDS((s, 768), BF16), SDS((ng, s, LANE), F32)],
        scratch_shapes=[pltpu.VMEM((gw, s), F32), pltpu.VMEM((tk, gw), F32), pltpu.VMEM((tk, gw), F32),
                        pltpu.SemaphoreType.DMA],
        name="fox_bwd", compiler_params=_params(("arbitrary",) * 3))(qb, qbt, ka, kat, va, dya, dyat)


def _fox2_bwd_post(proj, fbl, qg, kg, bfor, bd, dqa, dkn, dck, *, tr):
    s = proj.shape[0]
    nb = s // tr
    triu = jnp.asarray(np.triu(np.ones((tr, tr), np.float32)))
    rev = lambda i: nb - 1 - i

    def body(q_ref, k_ref, fb_ref, qg_ref, kg_ref, bf_ref, bd_ref, tri_ref, dqa_ref, dkn_ref, dck_ref,
             dq_ref, dk_ref, dfb_ref, dqg_ref, dkg_ref, dbf_ref, carry):
        @pl.when(pl.program_id(0) == 0)
        def _():
            carry[...] = jnp.zeros_like(carry)
            dqg_ref[...] = jnp.zeros_like(dqg_ref)
            dkg_ref[...] = jnp.zeros_like(dkg_ref)
            dbf_ref[...] = jnp.zeros_like(dbf_ref)

        bd_v = bd_ref[...]
        dqw = jnp.transpose(dqa_ref[...])
        _, vjp_q = jax.vjp(lambda q, g: _qn_fn(q, g, bd_v), q_ref[...], qg_ref[...])
        dq, dqg = vjp_q(_compact(dqw))
        _, vjp_k = jax.vjp(lambda k, g: _kn_fn(k, g, bd_v), k_ref[...], kg_ref[...])
        dk, dkg = vjp_k(dkn_ref[...])
        dq_ref[...] = dq.astype(BF16)
        dk_ref[...] = dk.astype(BF16)
        dqg_ref[...] += dqg
        dkg_ref[...] += dkg

        dc = _lane_of_heads(dqw, COL_A) + (dck_ref[0] + dck_ref[1])
        dlogf = jnp.dot(tri_ref[...], dc, precision=HI, preferred_element_type=F32) + carry[...]
        carry[...] = dlogf[0:1, :]
        lane = lax.broadcasted_iota(jnp.int32, (tr, LANE), 1)
        xf = fb_ref[...] + bf_ref[...]
        dfb = jnp.where(lane < N_FORGET, dlogf * jax.nn.sigmoid(-xf), 0.0)
        dfb_ref[...] = dfb.astype(BF16)
        dbf_ref[...] += jnp.sum(dfb, axis=0, keepdims=True)

    rb = lambda w: pl.BlockSpec((tr, w), lambda i: (rev(i), 0))
    wn = lambda w, off: pl.BlockSpec((pl.Element(tr), pl.Element(w)), lambda i: (rev(i) * tr, off))
    return pl.pallas_call(
        body, grid=(nb,),
        in_specs=[wn(768, QB), wn(768, KB), rb(LANE), _const((1, 768)), _const((1, 768)), _const((1, LANE)),
                  _const((768, 768)), _const((tr, tr)), pl.BlockSpec((AUG, tr), lambda i: (0, rev(i))), rb(768),
                  pl.BlockSpec((2, tr, LANE), lambda i: (0, rev(i), 0))],
        out_specs=[rb(768), rb(768), rb(LANE), _const((1, 768)), _const((1, 768)), _const((1, LANE))],
        out_shape=[SDS((s, 768), BF16), SDS((s, 768), BF16), SDS((s, LANE), BF16), SDS((1, 768), F32),
                   SDS((1, 768), F32), SDS((1, LANE), F32)],
        scratch_shapes=[pltpu.VMEM((1, LANE), F32)], name="fox_bwd_post",
        compiler_params=_params(("arbitrary",)))(proj, proj, fbl, qg, kg, bfor, bd, triu, dqa, dkn, dck)


def _merge_specs(tr):
    row = lambda w: pl.BlockSpec((tr, w), lambda i, j: (i, 0))
    shard = lambda r: pl.BlockSpec((None, r, 512), lambda i, j: (j, 0, 0))
    gate = lambda b: pl.BlockSpec((tr, 512), lambda i, j: (i, (GATE + 2048 * b) // 512 + j))
    return [row(768), row(768), row(512), shard(768), shard(768), shard(512), gate(0), gate(1), gate(2)]


def _merge_fwd(proj, ga, gb, gc, wa, wb, wc, *, tr):
    s = proj.shape[0]

    def body(ga_ref, gb_ref, gc_ref, wa_ref, wb_ref, wc_ref, l0_ref, l1_ref, l2_ref, y_ref):
        ua = _dot(ga_ref[...], wa_ref[...])
        ub = _dot(gb_ref[...], wb_ref[...])
        uc = _dot(gc_ref[...], wc_ref[...])
        y = jax.nn.sigmoid(l0_ref[...]) * ua + jax.nn.sigmoid(l1_ref[...]) * ub + jax.nn.sigmoid(l2_ref[...]) * uc
        y_ref[...] = y.astype(BF16)

    return pl.pallas_call(
        body, grid=(s // tr, N_CHIPS), in_specs=_merge_specs(tr),
        out_specs=pl.BlockSpec((tr, 512), lambda i, j: (i, j)), out_shape=SDS((s, D_MODEL), BF16), name="merge_fwd",
        compiler_params=_params(("parallel", "arbitrary")))(ga, gb, gc, wa, wb, wc, proj, proj, proj)


def _merge_bwd(proj, ga, gb, gc, wa, wb, wc, dy, *, tr):
    s = proj.shape[0]

    def body(ga_ref, gb_ref, gc_ref, wa_ref, wb_ref, wc_ref, l0_ref, l1_ref, l2_ref, dy_ref,
             dl0_ref, dl1_ref, dl2_ref, dua_ref, dub_ref, duc_ref, dga_ref, dgb_ref, dgc_ref):
        j = pl.program_id(1)
        dyv = dy_ref[...]

        @pl.when(j == 0)
        def _():
            dga_ref[...] = jnp.zeros_like(dga_ref)
            dgb_ref[...] = jnp.zeros_like(dgb_ref)
            dgc_ref[...] = jnp.zeros_like(dgc_ref)

        for g_ref, w_ref, l_ref, dl_ref, du_ref, dg_ref in (
                (ga_ref, wa_ref, l0_ref, dl0_ref, dua_ref, dga_ref),
                (gb_ref, wb_ref, l1_ref, dl1_ref, dub_ref, dgb_ref),
                (gc_ref, wc_ref, l2_ref, dl2_ref, duc_ref, dgc_ref)):
            w = w_ref[...]
            u = _dot(g_ref[...], w)
            sg = jax.nn.sigmoid(l_ref[...])
            dl_ref[...] = (dyv * u * sg * (1.0 - sg)).astype(BF16)
            du = (dyv * sg).astype(BF16)
            du_ref[...] = du
            dg_ref[...] += _dot_nt(du, w)

    blk = pl.BlockSpec((tr, 512), lambda i, j: (i, j))
    row = lambda w: pl.BlockSpec((tr, w), lambda i, j: (i, 0))
    big = SDS((s, D_MODEL), BF16)
    return pl.pallas_call(
        body, grid=(s // tr, N_CHIPS), in_specs=_merge_specs(tr) + [blk],
        out_specs=[blk] * 6 + [row(768), row(768), row(512)],
        out_shape=[big] * 6 + [SDS((s, 768), F32), SDS((s, 768), F32), SDS((s, 512), F32)], name="merge_bwd",
        compiler_params=_params(("parallel", "arbitrary")))(ga, gb, gc, wa, wb, wc, proj, proj, proj, dy)


def _out_loss(y, wo, x, tgt, *, tr, tn):
    s = x.shape[0]

    def body(y_ref, w_ref, x_ref, t_ref, d_ref, db_ref, sq_ref):
        @pl.when((pl.program_id(0) == 0) & (pl.program_id(1) == 0))
        def _():
            sq_ref[...] = jnp.zeros_like(sq_ref)

        out = x_ref[...] + _dot(y_ref[...], w_ref[...])
        diff = out - t_ref[...]
        sq_ref[...] += jnp.sum(diff * diff, axis=0, keepdims=True)
        d = diff * (1.0 / D_MODEL)
        d_ref[...] = d
        db_ref[...] = d.astype(BF16)

    blk = pl.BlockSpec((tr, tn), lambda i, j: (i, j))
    return pl.pallas_call(
        body, grid=(s // tr, D_MODEL // tn),
        in_specs=[pl.BlockSpec((tr, D_MODEL), lambda i, j: (i, 0)), pl.BlockSpec((D_MODEL, tn), lambda i, j: (0, j)), blk, blk],
        out_specs=[blk, blk, _const((1, tn))],
        out_shape=[SDS((s, D_MODEL), F32), SDS((s, D_MODEL), BF16), SDS((1, tn), F32)], name="out_loss",
        compiler_params=_params(("arbitrary", "arbitrary")))(y, wo, x, tgt)


def _tile_gain(g, reps):
    return jnp.tile(g.reshape(1, -1), (1, reps))


def _pad_lane(v):
    v = v.reshape(1, -1)
    return jnp.pad(v, ((0, 0), (0, LANE - v.shape[1])))


def _local_step(x, mem, tgt, w_main, w_fb, w_small, norm_gain, mem_norm_gain, b_forget,
                q_gain_a, k_gain_a, sinks_a, q_gain_b, k_gain_b, q_gain_c, k_gain_c, core=None):
    s = x.shape[0]
    tr = min(512, s)
    bd64 = _block_diag(768, HEAD_DIM)
    bd128 = _block_diag(512, C_HEAD_DIM)
    hsum = _head_sum(768, HEAD_DIM)
    qga, kga = _tile_gain(q_gain_a, 12), _tile_gain(k_gain_a, 4)
    qgb, kgb = _tile_gain(q_gain_b, 12), _tile_gain(k_gain_b, 12)
    qgc, kgc = _tile_gain(q_gain_c, 4), _tile_gain(k_gain_c, 4)
    sinks = _pad_lane(sinks_a)
    bfor = _pad_lane(b_forget)

    hn = _rms_fwd(x, norm_gain, tr=tr, name="rms_x")
    on_mesh = core is not None
    if on_mesh:
        proj, (gathered,) = _matmul(hn, w_main, dims="nn", out_dtype=F32, tm=1024, tn=1024, tk=D_MODEL, name="proj_main",
                                    comms=[_gather_comm(list(w_small))])
        w_mk, wa, wb, wc, wo = gathered
        w_mk, wo = w_mk.reshape(D_MODEL, 1024), wo.reshape(D_MODEL, D_MODEL)
    else:
        proj = _matmul(hn, w_main, dims="nn", out_dtype=F32, tm=1024, tn=1024, tk=D_MODEL, name="proj_main")
        w_mk, wa, wb, wc, wo = w_small
    fbl = _matmul(hn, w_fb, dims="nn", out_dtype=F32, tm=1024, tn=LANE, tk=D_MODEL, name="proj_forget")
    memn = _rms_fwd(mem, mem_norm_gain, tr=mem.shape[0], name="rms_mem")
    mkv = _matmul(memn, w_mk, dims="nn", out_dtype=F32, tm=256, tn=512, tk=D_MODEL, name="mem_kv")

    swa_bias = _swa_bias()
    ga = _swa_fwd(proj, qga, kga, sinks, bd64, swa_bias)
    em = _expand_mats()
    tf = min(256, s)
    qat, ka, kat, va, vat, qn, cfox = _fox2_prep(proj, fbl, qgb, kgb, bfor, bd64, em, tr=tf)
    gb, yb, lse = _fox2_fwd(proj, qat, ka, vat)
    gc = _mem_fwd(proj, mkv, qgc, kgc, bd128, tr=tr)
    y = _merge_fwd(proj, ga, gb, gc, wa, wb, wc, tr=tr)
    dout, dout_b, sq = _out_loss(y, wo, x, tgt, tr=tr, tn=512)

    d_wo = _matmul(y, dout_b, dims="tn", out_dtype=F32, tm=1024, tn=512, tk=4096, name="dw_out")
    dy = _matmul(dout_b, wo, dims="nt", out_dtype=F32, tm=1024, tn=512, tk=D_MODEL, name="dy")
    dl0, dl1, dl2, dua, dub, duc, dga, dgb, dgc = _merge_bwd(proj, ga, gb, gc, wa, wb, wc, dy, tr=tr)
    d_wa = _matmul(ga, dua, dims="tn", out_dtype=F32, tm=768, tn=512, tk=4096, name="dw_branch_a")
    d_wb = _matmul(gb, dub, dims="tn", out_dtype=F32, tm=768, tn=512, tk=4096, name="dw_branch_b")
    d_wc = _matmul(gc, duc, dims="tn", out_dtype=F32, tm=512, tn=512, tk=4096, name="dw_branch_c")

    dcur, dprev, d_qga, d_kga, d_sinks = _swa_bwd(proj, qga, kga, sinks, bd64, swa_bias, dga)
    dproj_a = _swa_combine(dcur, dprev)

    qab, qabt, dya, dyat, dzb = _fox2_bwd_pre(proj, yb, dgb, qn, cfox, lse, hsum, em, tr=tf)
    dqa, dkn, dvb, dck = _fox2_bwd(qab, qabt, ka, kat, va, dya, dyat)
    dqb, dkb, dfb, d_qgb, d_kgb, d_bf = _fox2_bwd_post(proj, fbl, qgb, kgb, bfor, bd64, dqa, dkn, dck, tr=tf)

    dproj_c, dmkv, d_qgc, d_kgc = _mem_bwd(proj, mkv, qgc, kgc, bd128, dgc, tr=tr)
    dmkv_b = dmkv.astype(BF16)
    d_wmk = _matmul(memn, dmkv_b, dims="tn", out_dtype=F32, tm=1024, tn=512, tk=256, name="dw_mem_kv")
    dmemn = _matmul(dmkv_b, w_mk, dims="nt", out_dtype=F32, tm=256, tn=512, tk=1024, name="dmemn")
    (d_mem_gain,) = _rms_bwd(mem, mem_norm_gain, dmemn, None, tr=mem.shape[0], name="rms_mem_bwd")

    dproj = [dproj_a, jnp.concatenate([dqb, dkb, dvb, dzb, dproj_c], axis=1), dl0, dl1, dl2]
    dhn_f = _matmul(dfb, w_fb, dims="nt", out_dtype=F32, tm=1024, tn=512, tk=LANE, name="dhn_forget")
    d_wfb = _matmul(hn, dfb, dims="tn", out_dtype=F32, tm=1024, tn=LANE, tk=512, name="dw_forget")
    big = {}
    if on_mesh:
        half = D_MODEL // 2
        c0 = core[0]
        hn_other = lax.dynamic_slice(hn, (0, (1 - c0) * half), (s, half))
        hn_own = lax.dynamic_slice(hn, (0, c0 * half), (s, half))
        g1, k1 = [d_wmk, d_wa, d_wb, d_wc, d_wo], [2, 3, 4, 5, 6]
        d_other, (got1,) = _matmul(hn_other, dproj, dims="tn", out_dtype=F32, tm=1024, tn=512, tk=4096,
                                   name="dw_main_other", comms=[_exchange_comm(g1, k1)])
        h1 = [_add_half(g, got, core, HALF_AXIS[k], name=f"add_half_{k}") for g, got, k in zip(g1, got1, k1)]
        d_own, (got0, parts1) = _matmul(
            hn_own, dproj, dims="tn", out_dtype=F32, tm=1024, tn=512, tk=4096, name="dw_main_own",
            comms=[_exchange_comm([d_other, d_wfb], [0, 1], whole=(0,)), _scatter_comm(h1, k1)])
        h0 = [_add_pair(d_own, got0[0], name="add_pair_main"), _add_half(d_wfb, got0[1], core, 0, name="add_half_1")]
        sums1 = [_sum4(p, name=f"sum4_{k}") for p, k in zip(parts1, k1)]
        dhn, (parts0, theirs1) = _matmul(dproj, w_main, dims="nt", out_dtype=F32, tm=1024, tn=512, tk=2048, vmem=VMEM_WIDE, name="dhn",
                                         add=dhn_f, comms=[_scatter_comm(h0, [0, 1]), _swap_comm(sums1)])
        sums0 = [_sum4(p, name=f"sum4_{k}") for p, k in zip(parts0, (0, 1))]
        (grad_x, d_gain), theirs0 = _rms_bwd(x, norm_gain, dhn, dout, tr=tr, name="rms_x_bwd", comm=_swap_comm(sums0))
        big = dict(sums=sums0 + sums1, theirs=list(theirs0) + list(theirs1))
    else:
        dhn = _matmul(dproj, w_main, dims="nt", out_dtype=F32, tm=1024, tn=512, tk=2048, vmem=VMEM_WIDE, name="dhn", add=dhn_f)
        d_wmain = _matmul(hn, dproj, dims="tn", out_dtype=F32, tm=1024, tn=512, tk=4096, name="dw_main")
        big = dict(d_wmain=d_wmain, d_wfb=d_wfb, d_wmk=d_wmk, d_wa=d_wa, d_wb=d_wb, d_wc=d_wc, d_wo=d_wo)
        grad_x, d_gain = _rms_bwd(x, norm_gain, dhn, dout, tr=tr, name="rms_x_bwd")

    fold = lambda g, reps: jnp.sum(g.reshape(reps, -1), axis=0, keepdims=True)
    return dict(
        sq=sq, grad_x=grad_x, **big,
        d_gain=d_gain, d_mem_gain=d_mem_gain, d_bf=d_bf[:, :N_FORGET],
        d_qga=fold(d_qga, 12), d_kga=fold(d_kga, 4), d_sinks=d_sinks[:, :A_HEADS],
        d_qgb=fold(d_qgb, 12), d_kgb=fold(d_kgb, 12), d_qgc=fold(d_qgc, 4), d_kgc=fold(d_kgc, 4))


PACK_ROWS = 256
FORGET_IN_SHARD = FORGET_COL - SHARD_COLS
AFTER_FORGET = FORGET_COL - SLAB_START[1]
END_CHIP1 = 2 * SHARD_COLS - N_FORGET - SLAB_START[1]


def _pack_w_in(chip, w):
    rows = w.shape[1]
    tr = PACK_ROWS

    def body(k_ref, w_ref, o_ref, scr):
        scr[...] = jnp.zeros_like(scr)
        scr[pl.ds(0, SHARD_COLS), :] = w_ref[...]
        v = jnp.transpose(scr[...])
        k = k_ref[0]
        col = lax.broadcasted_iota(jnp.int32, (tr, SLAB), 1)
        no_forget = jnp.zeros((tr, LANE), BF16)

        @pl.when(k == 0)
        def _():
            o_ref[:, 0:SLAB] = v.astype(BF16)
            o_ref[:, SLAB:] = no_forget

        @pl.when(k == 1)
        def _():
            before = pltpu.roll(v, SLAB_SHIFT[1], axis=1)
            after = pltpu.roll(v, SLAB - (N_FORGET - SLAB_SHIFT[1]), axis=1)
            slab = jnp.where(col < AFTER_FORGET, before, jnp.where(col < END_CHIP1, after, 0.0))
            o_ref[:, 0:SLAB] = slab.astype(BF16)
            f = pltpu.roll(v, SLAB - FORGET_IN_SHARD, axis=1)[:, :LANE]
            o_ref[:, SLAB:] = jnp.where(col[:, :LANE] < N_FORGET, f, 0.0).astype(BF16)

        for kk in (2, 3):
            @pl.when(k == kk)
            def _(kk=kk):
                o_ref[:, 0:SLAB] = pltpu.roll(v, SLAB_SHIFT[kk], axis=1).astype(BF16)
                o_ref[:, SLAB:] = no_forget

    return pl.pallas_call(
        body, grid_spec=pltpu.PrefetchScalarGridSpec(
            num_scalar_prefetch=1, grid=(rows // tr,),
            in_specs=[pl.BlockSpec((SHARD_COLS, tr), lambda i, k: (0, i))],
            out_specs=pl.BlockSpec((None, tr, SLAB + LANE), lambda i, k: (k[0], i, 0)),
            scratch_shapes=[pltpu.VMEM((SLAB, tr), F32)]),
        out_shape=SDS((N_CHIPS, rows, SLAB + LANE), BF16), name="pack_w_in",
        compiler_params=_params(("arbitrary",)))(chip, w)


def _merge_slabs(g):
    rows = g.shape[1]
    tr = PACK_ROWS
    t = [s // LANE for s in SLAB_START]
    n_t = SLAB // LANE

    def body(g_ref, m_ref, f_ref):
        for k in range(N_CHIPS):
            lo = t[k] + (1 if k > 0 else 0)
            hi = t[k + 1] if k + 1 < N_CHIPS else t[k] + n_t
            m_ref[:, lo * LANE:hi * LANE] = g_ref[k, :, (lo - t[k]) * LANE:(hi - t[k]) * LANE]
            if k + 1 < N_CHIPS:
                a = g_ref[k, :, (hi - t[k]) * LANE:(hi - t[k] + 1) * LANE].astype(F32)
                b = g_ref[k + 1, :, 0:LANE].astype(F32)
                m_ref[:, hi * LANE:(hi + 1) * LANE] = (a + b).astype(BF16)
        f_ref[...] = g_ref[1, :, SLAB:]

    return pl.pallas_call(
        body, grid=(rows // tr,),
        in_specs=[pl.BlockSpec((N_CHIPS, tr, SLAB + LANE), lambda i: (0, i, 0))],
        out_specs=[_rowblk(tr, P_MAIN), _rowblk(tr, LANE)],
        out_shape=[SDS((rows, P_MAIN), BF16), SDS((rows, LANE), BF16)], name="merge_slabs",
        compiler_params=_params(("parallel",)))(g)


def _adamw_math(w, g, m, v):
    nm = ADAM_B1 * m + (1.0 - ADAM_B1) * g
    nv = ADAM_B2 * v + (1.0 - ADAM_B2) * (g * g)
    m_hat = nm / (1.0 - ADAM_B1 ** ADAM_STEP)
    v_hat = nv / (1.0 - ADAM_B2 ** ADAM_STEP)
    delta = -ADAM_LR * (m_hat / (jnp.sqrt(v_hat) + ADAM_EPS) + ADAM_WD * w)
    return delta, nm, nv


def _adamw(g, w, m, v, *, tr, name):
    rows, cols = w.shape
    tr = min(tr, rows)

    def body(g_ref, w_ref, m_ref, v_ref, d_ref, nm_ref, nv_ref):
        d, nm, nv = _adamw_math(w_ref[...], g_ref[...], m_ref[...], v_ref[...])
        d_ref[...] = d
        nm_ref[...] = nm
        nv_ref[...] = nv

    spec = _rowblk(tr, cols)
    return pl.pallas_call(
        body, grid=(rows // tr,), in_specs=[spec] * 4, out_specs=[spec] * 3,
        out_shape=[SDS((rows, cols), F32)] * 3, name=name, compiler_params=_params(("parallel",)))(g, w, m, v)


def _adamw_w_in(chip_core, slab_mine, slab_theirs, forget_mine, forget_theirs, w, m, v):
    rows = w.shape[1]
    tr = PACK_ROWS // 2
    nbh = rows // 2 // tr

    def body(k_ref, sa_ref, sb_ref, fa_ref, fb_ref, w_ref, m_ref, v_ref, g_ref, d_ref, nm_ref, nv_ref):
        use_mine = pl.program_id(0) // nbh == k_ref[1]
        sl = jnp.where(use_mine, sa_ref[...], sb_ref[...])
        f_tile = jnp.where(use_mine, fa_ref[...], fb_ref[...])
        k = k_ref[0]

        def emit(wide):
            g = jnp.transpose(wide)[:SHARD_COLS, :]
            g_ref[...] = g
            d, nm, nv = _adamw_math(w_ref[...], g, m_ref[...], v_ref[...])
            d_ref[...] = d
            nm_ref[...] = nm
            nv_ref[...] = nv

        @pl.when(k == 0)
        def _():
            emit(sl)

        @pl.when(k == 1)
        def _():
            col = lax.broadcasted_iota(jnp.int32, (tr, SLAB), 1)
            before = pltpu.roll(sl, SLAB - SLAB_SHIFT[1], axis=1)
            after = pltpu.roll(sl, N_FORGET - SLAB_SHIFT[1], axis=1)
            wide_f = jnp.concatenate([f_tile, jnp.zeros((tr, SLAB - LANE), F32)], axis=1)
            forget = pltpu.roll(wide_f, FORGET_IN_SHARD, axis=1)
            emit(jnp.where(col < FORGET_IN_SHARD, before, jnp.where(col < FORGET_IN_SHARD + N_FORGET, forget, after)))

        for kk in (2, 3):
            @pl.when(k == kk)
            def _(kk=kk):
                emit(pltpu.roll(sl, SLAB - SLAB_SHIFT[kk], axis=1))

    nat = pl.BlockSpec((SHARD_COLS, tr), lambda i, k: (0, i))
    half = lambda width: pl.BlockSpec((tr, width), lambda i, k: (i % nbh, 0))
    return pl.pallas_call(
        body, grid_spec=pltpu.PrefetchScalarGridSpec(
            num_scalar_prefetch=1, grid=(rows // tr,),
            in_specs=[half(SLAB), half(SLAB), half(LANE), half(LANE), nat, nat, nat],
            out_specs=[nat] * 4),
        out_shape=[SDS((SHARD_COLS, rows), F32)] * 4, name="adamw_w_in",
        compiler_params=_params(("arbitrary",)))(chip_core, slab_mine, slab_theirs, forget_mine, forget_theirs, w, m, v)


ANY = pl.BlockSpec(memory_space=pl.ANY)
HALF_AXIS = (0, 0, 1, 0, 0, 0, 1)


def _me():
    return lax.axis_index("x"), lax.axis_index("y"), lax.axis_index("c")


def _half(ref, which, axis):
    n = ref.shape[axis] // 2
    sl = pl.ds(which * n, n)
    return ref.at[sl] if axis == 0 else ref.at[:, sl]


def _piece(t, ref, j):
    if t == 0:
        return ref.at[:, pl.ds(SLAB_START[j], SLAB)]
    if t == 1:
        return ref
    if t in (2, 6):
        return ref.at[pl.ds(512 * j, 512)]
    return ref.at[:, pl.ds(512 * j, 512)]


def _piece_shape(t, shape):
    if t == 0:
        return (shape[0], SLAB)
    if t == 1:
        return shape
    if t in (2, 6):
        return (512, shape[1])
    return (shape[0], 512)


def _gather_plan(ins, outs, own_slot_in_src):
    x, y, c = _me()
    k = 2 * x + y
    sib = (x, y, 1 - c)
    chips = [(1 - x, y), (x, 1 - y), (1 - x, 1 - y)]
    n = len(outs)

    def rows(t, which):
        h = outs[t].shape[1] // 2
        return pl.ds(which * h, h)

    def mine(t):
        return ins[t].at[k, rows(t, c)] if own_slot_in_src else ins[t].at[rows(t, c)]

    def first(t, j, sems):
        chip = chips[j]
        return pltpu.make_async_remote_copy(
            src_ref=mine(t), dst_ref=outs[t].at[k, rows(t, c)], send_sem=sems[0].at[t, j], recv_sem=sems[1].at[t, j],
            device_id=(chip[0], chip[1], c), device_id_type=MESH)

    def landed(t, j, sems):
        chip = chips[j]
        return pltpu.make_async_remote_copy(
            src_ref=mine(t), dst_ref=outs[t].at[2 * chip[0] + chip[1], rows(t, c)], send_sem=sems[0].at[t, j],
            recv_sem=sems[1].at[t, j], device_id=(chip[0], chip[1], c), device_id_type=MESH)

    def passed(t, j, which, sems):
        chip = chips[j]
        blk = outs[t].at[2 * chip[0] + chip[1], rows(t, which)]
        return pltpu.make_async_remote_copy(
            src_ref=blk, dst_ref=blk, send_sem=sems[2].at[t, j], recv_sem=sems[3].at[t, j], device_id=sib,
            device_id_type=MESH)

    def start(sems):
        for j in range(3):
            for t in range(n):
                first(t, j, sems).start()

    def finish(sems):
        for j in range(3):
            for t in range(n):
                landed(t, j, sems).wait_recv()
                passed(t, j, c, sems).start()
        for j in range(3):
            for t in range(n):
                passed(t, j, 1 - c, sems).wait_recv()
        for j in range(3):
            for t in range(n):
                first(t, j, sems).wait_send()
                passed(t, j, c, sems).wait_send()

    return k, start, finish


def _all_gather_slabs(slabs):
    def body(in_ref, out_ref, nbr_sem, quarter_sem, pass_sem):
        x, y, c = _me()
        k = 2 * x + y
        rows = out_ref.shape[1]
        h, q = rows // 2, rows // 4
        nbrs = [(1 - x, y), (x, 1 - y)]
        slot = lambda chip: 2 * chip[0] + chip[1]
        diag = 2 * (1 - x) + (1 - y)
        half = pl.ds(c * h, h)
        quarter = lambda a: pl.ds(c * h + a * q, q)

        def first(a):
            return pltpu.make_async_remote_copy(
                src_ref=in_ref.at[k, half], dst_ref=out_ref.at[k, half], send_sem=nbr_sem.at[0, a],
                recv_sem=nbr_sem.at[1, a], device_id=(nbrs[a][0], nbrs[a][1], c), device_id_type=MESH)

        def landed(a):
            blk = out_ref.at[slot(nbrs[a]), half]
            return pltpu.make_async_remote_copy(
                src_ref=blk, dst_ref=blk, send_sem=nbr_sem.at[0, a], recv_sem=nbr_sem.at[1, a],
                device_id=(nbrs[a][0], nbrs[a][1], c), device_id_type=MESH)

        def relay(a):
            blk = out_ref.at[slot(nbrs[a]), quarter(a)]
            to = nbrs[1 - a]
            return pltpu.make_async_remote_copy(
                src_ref=blk, dst_ref=blk, send_sem=quarter_sem.at[0, a], recv_sem=quarter_sem.at[1, a],
                device_id=(to[0], to[1], c), device_id_type=MESH)

        def relayed(a):
            blk = out_ref.at[diag, quarter(a)]
            frm = nbrs[1 - a]
            return pltpu.make_async_remote_copy(
                src_ref=blk, dst_ref=blk, send_sem=quarter_sem.at[0, a], recv_sem=quarter_sem.at[1, a],
                device_id=(frm[0], frm[1], c), device_id_type=MESH)

        def passed(j, which):
            sl = diag if j == 2 else slot(nbrs[j])
            blk = out_ref.at[sl, pl.ds(which * h, h)]
            return pltpu.make_async_remote_copy(
                src_ref=blk, dst_ref=blk, send_sem=pass_sem.at[0, j], recv_sem=pass_sem.at[1, j],
                device_id=(x, y, 1 - c), device_id_type=MESH)

        for a in range(2):
            first(a).start()
        for a in range(2):
            landed(a).wait_recv()
            relay(a).start()
            passed(a, c).start()
        for a in range(2):
            relayed(a).wait_recv()
        passed(2, c).start()
        for j in range(3):
            passed(j, 1 - c).wait_recv()
        for a in range(2):
            first(a).wait_send()
            relay(a).wait_send()
        for j in range(3):
            passed(j, c).wait_send()

    return pl.pallas_call(
        body, in_specs=[ANY], out_specs=ANY, out_shape=SDS(slabs.shape, slabs.dtype),
        scratch_shapes=[pltpu.SemaphoreType.DMA((2, 2)), pltpu.SemaphoreType.DMA((2, 2)), pltpu.SemaphoreType.DMA((2, 3))],
        input_output_aliases={0: 0}, name="all_gather_slabs")(slabs)


def _gather_comm(parts):
    n = len(parts)

    def start(ins, outs, sems):
        k, go, _ = _gather_plan(ins, outs, False)
        for t in range(n):
            pltpu.make_async_copy(ins[t], outs[t].at[k], sems[4].at[t]).start()
        go(sems)

    def finish(ins, outs, sems):
        k, _, done = _gather_plan(ins, outs, False)
        done(sems)
        for t in range(n):
            pltpu.make_async_copy(ins[t], outs[t].at[k], sems[4].at[t]).wait()

    return _Comm(parts, [SDS((N_CHIPS,) + p.shape, p.dtype) for p in parts],
                 [pltpu.SemaphoreType.DMA((n, 3))] * 4 + [pltpu.SemaphoreType.DMA((n,))], start, finish)


def _exchange_comm(arrs, kinds, whole=()):
    n = len(arrs)

    def copies(ins, outs, sems):
        x, y, c = _me()
        return [pltpu.make_async_remote_copy(
            src_ref=ins[t] if t in whole else _half(ins[t], 1 - c, HALF_AXIS[kinds[t]]), dst_ref=outs[t],
            send_sem=sems[0].at[t], recv_sem=sems[1].at[t], device_id=(x, y, 1 - c), device_id_type=MESH)
            for t in range(n)]

    def start(ins, outs, sems):
        for cp in copies(ins, outs, sems):
            cp.start()

    def finish(ins, outs, sems):
        for cp in copies(ins, outs, sems):
            cp.wait()

    def hshape(t):
        s = list(arrs[t].shape)
        if t not in whole:
            s[HALF_AXIS[kinds[t]]] //= 2
        return SDS(tuple(s), arrs[t].dtype)

    return _Comm(arrs, [hshape(t) for t in range(n)], [pltpu.SemaphoreType.DMA((n,))] * 2, start, finish)


def _add_half(full, got, core, axis, *, name):
    r, c = got.shape
    br, bc = (256 if r % 256 == 0 else 128), min(2048, c)
    off_r = (r // br) if axis == 0 else 0
    off_c = (c // bc) if axis == 1 else 0

    def body(c_ref, a_ref, b_ref, o_ref):
        o_ref[...] = (a_ref[...] + b_ref[...]).astype(BF16)

    return pl.pallas_call(
        body, grid_spec=pltpu.PrefetchScalarGridSpec(
            num_scalar_prefetch=1, grid=(r // br, c // bc),
            in_specs=[pl.BlockSpec((br, bc), lambda i, j, cr: (i + cr[0] * off_r, j + cr[0] * off_c)),
                      pl.BlockSpec((br, bc), lambda i, j, cr: (i, j))],
            out_specs=pl.BlockSpec((br, bc), lambda i, j, cr: (i, j))),
        out_shape=SDS((r, c), BF16), name=name, compiler_params=_params(("parallel", "parallel")))(core, full, got)


def _add_pair(a, b, *, name):
    r, c = a.shape
    br, bc = 256, min(2048, c)

    def body(a_ref, b_ref, o_ref):
        o_ref[...] = (a_ref[...] + b_ref[...]).astype(BF16)

    spec = pl.BlockSpec((br, bc), lambda i, j: (i, j))
    return pl.pallas_call(body, grid=(r // br, c // bc), in_specs=[spec, spec], out_specs=spec,
                          out_shape=SDS((r, c), BF16), name=name, compiler_params=_params(("parallel", "parallel")))(a, b)


def _scatter_comm(halves, kinds):
    n = len(halves)

    def plan(ins, outs, sems):
        send, recv, lsem = sems
        x, y, c = _me()
        k = 2 * x + y

        def to_chip(t, j):
            return pltpu.make_async_remote_copy(
                src_ref=_piece(kinds[t], ins[t], j), dst_ref=outs[t].at[k], send_sem=send.at[t, j],
                recv_sem=recv.at[t, k], device_id=(j // 2, j % 2, c), device_id_type=MESH)

        def from_chip(t, j):
            return pltpu.make_async_remote_copy(
                src_ref=_piece(kinds[t], ins[t], j), dst_ref=outs[t].at[j], send_sem=send.at[t, j],
                recv_sem=recv.at[t, j], device_id=(j // 2, j % 2, c), device_id_type=MESH)

        def own(t, j):
            return pltpu.make_async_copy(_piece(kinds[t], ins[t], j), outs[t].at[j], lsem.at[t])

        return k, to_chip, from_chip, own

    def start(ins, outs, sems):
        k, to_chip, _, own = plan(ins, outs, sems)
        for j in range(N_CHIPS):
            @pl.when(k != j)
            def _(j=j):
                for t in range(n):
                    to_chip(t, j).start()

            @pl.when(k == j)
            def _(j=j):
                for t in range(n):
                    own(t, j).start()

    def finish(ins, outs, sems):
        k, to_chip, from_chip, own = plan(ins, outs, sems)
        for j in range(N_CHIPS):
            @pl.when(k != j)
            def _(j=j):
                for t in range(n):
                    from_chip(t, j).wait_recv()
                for t in range(n):
                    to_chip(t, j).wait_send()

            @pl.when(k == j)
            def _(j=j):
                for t in range(n):
                    own(t, j).wait()

    return _Comm(halves, [SDS((N_CHIPS,) + _piece_shape(kinds[t], halves[t].shape), halves[t].dtype) for t in range(n)],
                 [pltpu.SemaphoreType.DMA((n, N_CHIPS))] * 2 + [pltpu.SemaphoreType.DMA((n,))], start, finish)


def _sum4(p, *, name):
    _, r, c = p.shape
    br = 256 if r % 256 == 0 else 128

    def body(p_ref, o_ref):
        o_ref[...] = ((p_ref[0].astype(F32) + p_ref[1].astype(F32)) + p_ref[2].astype(F32)) + p_ref[3].astype(F32)

    return pl.pallas_call(
        body, grid=(r // br,), in_specs=[pl.BlockSpec((N_CHIPS, br, c), lambda i: (0, i, 0))],
        out_specs=_rowblk(br, c), out_shape=SDS((r, c), F32), name=name, compiler_params=_params(("parallel",)))(p)


def _swap_comm(sums):
    return _exchange_comm(sums, [None] * len(sums), whole=tuple(range(len(sums))))


def _adamw_halves(mine, theirs, core, w, m, v, *, axis, tr, name):
    rows, cols = w.shape

    if axis == 0:
        nbh = rows // 2 // tr
        g_spec = pl.BlockSpec((tr, cols), lambda i, cr: (i % nbh, 0))
    else:
        g_spec = pl.BlockSpec((tr, cols // 2), lambda i, cr: (i, 0))

    def body(c_ref, a_ref, b_ref, w_ref, m_ref, v_ref, g_ref, d_ref, nm_ref, nv_ref):
        a, b = a_ref[...], b_ref[...]
        if axis == 0:
            g = jnp.where(pl.program_id(0) // nbh == c_ref[0], a, b)
        else:
            low = c_ref[0] == 0
            g = jnp.concatenate([jnp.where(low, a, b), jnp.where(low, b, a)], axis=1)
        g_ref[...] = g
        d, nm, nv = _adamw_math(w_ref[...], g, m_ref[...], v_ref[...])
        d_ref[...] = d
        nm_ref[...] = nm
        nv_ref[...] = nv

    nat = pl.BlockSpec((tr, cols), lambda i, cr: (i, 0))
    return pl.pallas_call(
        body, grid_spec=pltpu.PrefetchScalarGridSpec(
            num_scalar_prefetch=1, grid=(rows // tr,), in_specs=[g_spec, g_spec, nat, nat, nat], out_specs=[nat] * 4),
        out_shape=[SDS((rows, cols), F32)] * 4, name=name, compiler_params=_params(("arbitrary",)))(
            core, mine, theirs, w, m, v)


SMALL_ROWS, SMALL_COLS = 8, 1024


def _pack_small(vs):
    flat = jnp.concatenate([v.reshape(-1) for v in vs])
    return jnp.pad(flat, (0, SMALL_ROWS * SMALL_COLS - flat.shape[0])).reshape(SMALL_ROWS, SMALL_COLS)


def _unpack_small(packed, sizes):
    flat = packed.reshape(-1)
    out, o = [], 0
    for n in sizes:
        out.append(flat[o:o + n].reshape(1, n))
        o += n
    return out


def _all_reduce_small(v):
    n_dev = 8

    def body(v_ref, o_ref, land, send, recv):
        x, y, c = _me()
        me = 4 * x + 2 * y + c
        land[me] = v_ref[...]
        cps = []
        for r in range(1, n_dev):
            fx, fy, fc = (r >> 2) & 1, (r >> 1) & 1, r & 1
            peer = (x ^ fx, y ^ fy, c ^ fc)
            cps.append(pltpu.make_async_remote_copy(
                src_ref=v_ref, dst_ref=land.at[me], send_sem=send.at[r - 1], recv_sem=recv.at[r - 1],
                device_id=peer, device_id_type=MESH))
        for cp in cps:
            cp.start()
        for r in range(1, n_dev):
            fx, fy, fc = (r >> 2) & 1, (r >> 1) & 1, r & 1
            src = 4 * (x ^ fx) + 2 * (y ^ fy) + (c ^ fc)
            pltpu.make_async_remote_copy(
                src_ref=v_ref, dst_ref=land.at[src], send_sem=send.at[r - 1], recv_sem=recv.at[r - 1],
                device_id=(x ^ fx, y ^ fy, c ^ fc), device_id_type=MESH).wait_recv()
        for cp in cps:
            cp.wait_send()
        acc = land[0]
        for r in range(1, n_dev):
            acc = acc + land[r]
        o_ref[...] = acc

    vm = pl.BlockSpec(memory_space=pltpu.VMEM)
    return pl.pallas_call(
        body, in_specs=[vm], out_specs=vm, out_shape=SDS(v.shape, F32),
        scratch_shapes=[pltpu.VMEM((n_dev,) + v.shape, F32), pltpu.SemaphoreType.DMA((n_dev - 1,)),
                        pltpu.SemaphoreType.DMA((n_dev - 1,))],
        name="all_reduce_small")(v)


def kernel(x, mem, norm_gain, mem_norm_gain, w_in, b_forget, q_gain_a, k_gain_a, sinks_a, q_gain_b, k_gain_b, q_gain_c, k_gain_c, w_mem_kv, w_branch_a, w_branch_b, w_branch_c, w_out, loss_target, m_norm_gain, m_mem_norm_gain, m_w_in, m_b_forget, m_q_gain_a, m_k_gain_a, m_sinks_a, m_q_gain_b, m_k_gain_b, m_q_gain_c, m_k_gain_c, m_w_mem_kv, m_w_branch_a, m_w_branch_b, m_w_branch_c, m_w_out, v_norm_gain, v_mem_norm_gain, v_w_in, v_b_forget, v_q_gain_a, v_k_gain_a, v_sinks_a, v_q_gain_b, v_k_gain_b, v_q_gain_c, v_k_gain_c, v_w_mem_kv, v_w_branch_a, v_w_branch_b, v_w_branch_c, v_w_out):
    xi, yi, ci = lax.axis_index("x"), lax.axis_index("y"), lax.axis_index("c")
    chip = jnp.reshape(2 * xi + yi, (1,)).astype(jnp.int32)
    core = jnp.reshape(ci, (1,)).astype(jnp.int32)

    slabs = _pack_w_in(chip, jnp.transpose(w_in[0]))
    mine = [w_mem_kv[0].astype(BF16), w_branch_a[0].astype(BF16), w_branch_b[0].astype(BF16),
            w_branch_c[0].astype(BF16), w_out[0].astype(BF16)]
    w_main, w_fb = _merge_slabs(_all_gather_slabs(slabs))

    r = _local_step(x[0], mem[0], loss_target[0], w_main, w_fb, mine, norm_gain, mem_norm_gain,
                    b_forget, q_gain_a, k_gain_a, sinks_a, q_gain_b, k_gain_b, q_gain_c, k_gain_c, core=core)
    sums, theirs = r["sums"], r["theirs"]

    small_names = ["d_gain", "d_mem_gain", "d_bf", "d_qga", "d_kga", "d_sinks", "d_qgb", "d_kgb", "d_qgc", "d_kgc"]
    loss_part = (0.5 / D_MODEL) * jnp.sum(r["sq"], axis=1, keepdims=True)
    packed = _pack_small([r[n] for n in small_names] + [loss_part])
    red = _all_reduce_small(packed)
    small_w = [norm_gain, mem_norm_gain, b_forget, q_gain_a, k_gain_a, sinks_a, q_gain_b, k_gain_b, q_gain_c, k_gain_c]
    small_m = [m_norm_gain, m_mem_norm_gain, m_b_forget, m_q_gain_a, m_k_gain_a, m_sinks_a, m_q_gain_b, m_k_gain_b,
               m_q_gain_c, m_k_gain_c]
    small_v = [v_norm_gain, v_mem_norm_gain, v_b_forget, v_q_gain_a, v_k_gain_a, v_sinks_a, v_q_gain_b, v_k_gain_b,
               v_q_gain_c, v_k_gain_c]
    sizes = [w.shape[1] for w in small_w]
    s_d, s_m, s_v = _adamw(red, _pack_small(small_w), _pack_small(small_m), _pack_small(small_v), tr=8, name="adamw_small")
    g_small = _unpack_small(red, sizes + [1])
    loss = g_small[-1].reshape(())
    d_small, m_small, v_small = _unpack_small(s_d, sizes), _unpack_small(s_m, sizes), _unpack_small(s_v, sizes)

    gw_in, dw_in, mw_in, vw_in = _adamw_w_in(jnp.concatenate([chip, core]), sums[0], theirs[0], sums[1], theirs[1],
                                             jnp.transpose(w_in[0]), jnp.transpose(m_w_in[0]), jnp.transpose(v_w_in[0]))
    big = {}
    for t, nm, w, m, v in ((2, "w_mem_kv", w_mem_kv, m_w_mem_kv, v_w_mem_kv),
                           (3, "w_branch_a", w_branch_a, m_w_branch_a, v_w_branch_a),
                           (4, "w_branch_b", w_branch_b, m_w_branch_b, v_w_branch_b),
                           (5, "w_branch_c", w_branch_c, m_w_branch_c, v_w_branch_c),
                           (6, "w_out", w_out, m_w_out, v_w_out)):
        big[nm] = _adamw_halves(sums[t], theirs[t], core, w[0], m[0], v[0], axis=HALF_AXIS[t], tr=128,
                                name="adamw_" + nm)

    def collect(kind):
        sm = (g_small, d_small, m_small, v_small)[kind]
        win = (gw_in, dw_in, mw_in, vw_in)[kind]
        return ([sm[0], sm[1], jnp.transpose(win)[None]] + [a for a in sm[2:10]]
                + [big[n][kind][None] for n in ("w_mem_kv", "w_branch_a", "w_branch_b", "w_branch_c", "w_out")])

    return (loss, r["grad_x"][None], *collect(0), *collect(1), *collect(2), *collect(3))
```

```python
import functools

import numpy as np
import jax
import jax.numpy as jnp
from jax import lax
from jax.experimental import pallas as pl
from jax.experimental.pallas import tpu as pltpu

F32 = jnp.float32
BF16 = jnp.bfloat16
HI = lax.Precision.HIGHEST
SDS = jax.ShapeDtypeStruct
MESH = pl.DeviceIdType.MESH

D_MODEL = 2048
HEAD_DIM = 64
A_HEADS = 12
A_GROUP = 3
B_HEADS = 12
C_HEADS = 4
C_HEAD_DIM = 128
WINDOW = 128
EPS = 1e-6
NEG = -1e30
LANE = 128

QA, KA, VA, ZA = 0, 768, 1024, 1280
QB, KB, VB, ZB = 2048, 2816, 3584, 4352
QC, ZC = 5120, 5632
GATE = 6144
P_MAIN = 12288
N_FORGET = 12
FORGET_COL = 5120
SHARD_COLS = 3075
SLAB = 3200
SLAB_START = (0, 3072, 6016, 9088)
SLAB_SHIFT = (0, 3, 122, 125)
N_CHIPS = 4

ADAM_LR = 0.001
ADAM_B1 = 0.9
ADAM_B2 = 0.999
ADAM_EPS = 1e-08
ADAM_WD = 0.01
ADAM_STEP = 10

VMEM_LIMIT = 56 * 1024 * 1024
VMEM_WIDE = 62 * 1024 * 1024


def _params(sem, vmem=VMEM_LIMIT):
    return pltpu.CompilerParams(dimension_semantics=sem, vmem_limit_bytes=vmem)


def _win(tr, width, off):
    return pl.BlockSpec((pl.Element(tr), pl.Element(width)), lambda i, *_: (i * tr, off))


def _rowblk(tr, width):
    return pl.BlockSpec((tr, width), lambda i, *_: (i, 0))


def _const(shape):
    nd = len(shape)
    return pl.BlockSpec(shape, lambda *_: (0,) * nd)


def _rms(x, g):
    return x * lax.rsqrt(jnp.mean(x * x, axis=-1, keepdims=True) + EPS) * g


def _head_mean_impl(x2, bd):
    hi = x2.astype(BF16)
    lo = (x2 - hi.astype(F32)).astype(BF16)
    return _dot(hi, bd) + _dot(lo, bd)


@jax.custom_vjp
def _head_mean(x2, bd):
    return _head_mean_impl(x2, bd)


_head_mean.defvjp(lambda x2, bd: (_head_mean_impl(x2, bd), bd),
                  lambda bd, g: (_head_mean_impl(g, bd), jnp.zeros_like(bd)))


def _head_norm(x, g_tiled, bd):
    return x * lax.rsqrt(_head_mean(x * x, bd) + EPS) * g_tiled


def _silu(z):
    return z * jax.nn.sigmoid(z)


def _dot_nt(a, b):
    return lax.dot_general(a, b, (((1,), (1,)), ((), ())), preferred_element_type=F32)


def _dot_tn(a, b):
    return lax.dot_general(a, b, (((0,), (0,)), ((), ())), preferred_element_type=F32)


def _dot(a, b):
    return jnp.dot(a, b, preferred_element_type=F32)


def _swa_fn(qk, vz, qkp, vzp, qg, kg, sinks, bd, bias, first):
    q = _head_norm(qk[:, :768], qg, bd)
    k2 = jnp.concatenate([qkp[:, 768:], qk[:, 768:]], axis=0)
    k2 = _head_norm(k2, kg, bd[:256, :256])
    v2 = jnp.concatenate([vzp[:, :256], vz[:, :256]], axis=0)
    z = vz[:, 256:]
    cols = A_GROUP * WINDOW
    kj = lax.broadcasted_iota(jnp.int32, (2 * WINDOW, cols), 0)
    no_prev = kj < WINDOW * first.astype(jnp.int32)
    qtb = jnp.transpose(q).astype(BF16)
    kb = k2.astype(BF16)
    vtb = jnp.transpose(v2).astype(BF16)
    outs = [None] * A_HEADS
    for g in range(A_HEADS // A_GROUP):
        heads = [A_GROUP * g + u for u in range(A_GROUP)]
        qs = jnp.concatenate([qtb[64 * h:64 * h + 64, :] for h in heads], axis=1)
        s = _dot(kb[:, 64 * g:64 * g + 64], qs) * (HEAD_DIM ** -0.5) + bias[g]
        s = jnp.where(no_prev, NEG, s)
        sink = jnp.concatenate([jnp.broadcast_to(sinks[:, h:h + 1], (1, WINDOW)) for h in heads], axis=1)
        m = lax.stop_gradient(jnp.maximum(jnp.max(s, axis=0, keepdims=True), sink))
        p = jnp.exp(s - m)
        den = jnp.sum(p, axis=0, keepdims=True) + jnp.exp(sink - m)
        o = _dot(vtb[64 * g:64 * g + 64, :], (p * (1.0 / den)).astype(BF16))
        for u, h in enumerate(heads):
            outs[h] = o[:, WINDOW * u:WINDOW * u + WINDOW]
    return jnp.transpose(jnp.concatenate(outs, axis=0)) * _silu(z)


def _swa_bias():
    qi = np.arange(WINDOW)[None, :]
    kj = np.arange(2 * WINDOW)[:, None]
    rel = qi + WINDOW - kj
    valid = (rel >= 0) & (rel < WINDOW)
    out = np.zeros((A_HEADS // A_GROUP, 2 * WINDOW, A_GROUP * WINDOW), np.float32)
    for h in range(A_HEADS):
        slope = np.float32(2.0 ** (-8.0 * (h + 1) / A_HEADS))
        blk = np.where(valid, -slope * rel.astype(np.float32), np.float32(NEG))
        g, u = divmod(h, A_GROUP)
        out[g, :, WINDOW * u:WINDOW * u + WINDOW] = blk
    return jnp.asarray(out)


def _mem_fn(qz, mkv, qg, kg, bd):
    q = _head_norm(qz[:, :512], qg, bd).astype(BF16)
    k = _head_norm(mkv[:, :512], kg, bd).astype(BF16)
    v = mkv[:, 512:].astype(BF16)
    z = qz[:, 512:]
    outs = []
    for h in range(C_HEADS):
        sl = slice(128 * h, 128 * h + 128)
        s = _dot_nt(q[:, sl], k[:, sl]) * (C_HEAD_DIM ** -0.5)
        m = lax.stop_gradient(jnp.max(s, axis=-1, keepdims=True))
        p = jnp.exp(s - m)
        den = jnp.sum(p, axis=-1, keepdims=True)
        outs.append(_dot((p * (1.0 / den)).astype(BF16), v[:, sl]))
    return jnp.concatenate(outs, axis=1) * _silu(z)


def _qn_fn(q, g, bd):
    return _head_norm(q, g, bd) * (HEAD_DIM ** -0.5)


def _kn_fn(k, g, bd):
    return _head_norm(k, g, bd)


def _block_diag(width, hd):
    i = np.arange(width) // hd
    return jnp.asarray((i[:, None] == i[None, :]).astype(np.float32) / hd, BF16)


def _head_sum(width, hd):
    i = np.arange(width) // hd
    return jnp.asarray((i[:, None] == np.arange(LANE)[None, :]).astype(np.float32))


def _rms_fwd(x, g, *, tr, name, inplace_comm=None):
    rows, dm = x.shape
    nb = rows // tr
    comm = inplace_comm

    def body(x_ref, g_ref, *refs):
        if comm:
            buf_in, o_ref, buf_out = refs[:3]

            @pl.when(pl.program_id(0) == 0)
            def _():
                comm.start([buf_in], [buf_out], refs[3:])
        else:
            o_ref = refs[0]

        o_ref[...] = _rms(x_ref[...], g_ref[...]).astype(BF16)

        if comm:
            @pl.when(pl.program_id(0) == nb - 1)
            def _():
                comm.finish([buf_in], [buf_out], refs[3:])

    hbm = pl.BlockSpec(memory_space=pl.ANY)
    res = pl.pallas_call(
        body, grid=(nb,),
        in_specs=[_rowblk(tr, dm), _const((1, dm))] + ([hbm] if comm else []),
        out_specs=[_rowblk(tr, dm)] + ([hbm] if comm else []),
        out_shape=[SDS((rows, dm), BF16)] + (list(comm.out_shapes) if comm else []),
        scratch_shapes=list(comm.sems) if comm else [], input_output_aliases={2: 1} if comm else {}, name=name,
        compiler_params=_params(("arbitrary",) if comm else ("parallel",)))(x, g, *(comm.ins if comm else []))
    return (res[0], res[1]) if comm else res[0]


def _rms_bwd(x, g, dy, resid, *, tr, name, comm=None):
    rows, dm = x.shape
    want_dx = resid is not None
    n_in = 4 if want_dx else 3
    n_out = 2 if want_dx else 1
    c_in = len(comm.ins) if comm else 0
    c_out = len(comm.out_shapes) if comm else 0
    nb = rows // tr

    def body(*refs):
        x_ref, g_ref, dy_ref = refs[:3]
        r_ref = refs[3] if want_dx else None
        cin = refs[n_in:n_in + c_in]
        outs = refs[n_in + c_in:n_in + c_in + n_out]
        dg_ref = outs[-1]
        cout = refs[n_in + c_in + n_out:n_in + c_in + n_out + c_out]
        csem = refs[n_in + c_in + n_out + c_out:]

        if comm:
            @pl.when(pl.program_id(0) == 0)
            def _():
                comm.start(cin, cout, csem)

        _, vjp = jax.vjp(_rms, x_ref[...], g_ref[...])
        dx, dg = vjp(dy_ref[...])

        @pl.when(pl.program_id(0) == 0)
        def _():
            dg_ref[...] = jnp.zeros_like(dg_ref)

        dg_ref[...] += dg
        if want_dx:
            outs[0][...] = r_ref[...] + dx

        if comm:
            @pl.when(pl.program_id(0) == nb - 1)
            def _():
                comm.finish(cin, cout, csem)

    hbm = pl.BlockSpec(memory_space=pl.ANY)
    ins = [x, g, dy] + ([resid] if want_dx else []) + (list(comm.ins) if comm else [])
    in_specs = ([_rowblk(tr, dm), _const((1, dm)), _rowblk(tr, dm)] + ([_rowblk(tr, dm)] if want_dx else [])
                + [hbm] * c_in)
    out_specs = ([_rowblk(tr, dm)] if want_dx else []) + [_const((1, dm))] + [hbm] * c_out
    out_shape = (([SDS((rows, dm), F32)] if want_dx else []) + [SDS((1, dm), F32)]
                 + (list(comm.out_shapes) if comm else []))
    res = pl.pallas_call(
        body, grid=(nb,), in_specs=in_specs, out_specs=out_specs, out_shape=out_shape,
        scratch_shapes=list(comm.sems) if comm else [], name=name, compiler_params=_params(("arbitrary",)))(*ins)
    return (list(res[:n_out]), list(res[n_out:])) if comm else res


class _Comm:
    def __init__(self, ins, out_shapes, sems, start, finish):
        self.ins, self.out_shapes, self.sems, self.start, self.finish = list(ins), list(out_shapes), list(sems), start, finish


def _matmul(a, b, *, dims, out_dtype, tm, tn, tk, name, add=None, comms=(), vmem=VMEM_LIMIT):
    a_list = list(a) if isinstance(a, (list, tuple)) else [a]
    b_list = list(b) if isinstance(b, (list, tuple)) else [b]
    assert len(a_list) == 1 or dims == "nt"
    assert len(b_list) == 1 or dims == "tn"
    if dims == "tn":
        kdim, m = a_list[0].shape
    else:
        m, kdim = a_list[0].shape[0], sum(p.shape[1] for p in a_list)
    n = b_list[0].shape[0] if dims == "nt" else sum(p.shape[1] for p in b_list)
    tm, tn, tk = min(tm, m), min(tn, n), min(tk, kdim)
    assert m % tm == 0 and n % tn == 0 and kdim % tk == 0, (name, m, n, kdim)
    ni, nj, nk = m // tm, n // tn, kdim // tk
    a_rng, b_rng, pos = [], [], 0
    for p in a_list:
        assert len(a_list) == 1 or p.shape[1] % tk == 0
        a_rng.append((pos, p.shape[1] // tk if len(a_list) > 1 else nk))
        pos += a_rng[-1][1]
    pos = 0
    for p in b_list:
        assert len(b_list) == 1 or p.shape[1] % tn == 0
        b_rng.append((pos, p.shape[1] // tn if len(b_list) > 1 else nj))
        pos += b_rng[-1][1]
    has_add = add is not None
    n_mm_in = len(a_list) + len(b_list) + (1 if has_add else 0)
    c_in = [len(c.ins) for c in comms]
    c_out = [len(c.out_shapes) for c in comms]
    c_sem = [len(c.sems) for c in comms]

    def body(*refs):
        a_refs, b_refs = refs[:len(a_list)], refs[len(a_list):len(a_list) + len(b_list)]
        add_ref = refs[n_mm_in - 1] if has_add else None
        pos = n_mm_in
        cin = []
        for cnt in c_in:
            cin.append(refs[pos:pos + cnt])
            pos += cnt
        o_ref = refs[pos]
        pos += 1
        cout = []
        for cnt in c_out:
            cout.append(refs[pos:pos + cnt])
            pos += cnt
        acc = refs[pos]
        pos += 1
        csem = []
        for cnt in c_sem:
            csem.append(refs[pos:pos + cnt])
            pos += cnt
        i, j, k = pl.program_id(0), pl.program_id(1), pl.program_id(2)

        if comms:
            @pl.when((i == 0) & (j == 0) & (k == 0))
            def _():
                for c, ci, co, cs in zip(comms, cin, cout, csem):
                    c.start(ci, co, cs)

        def accumulate(a_ref, b_ref, first_k, later_k):
            if dims == "nn":
                part = _dot(a_ref[...], b_ref[...])
            elif dims == "nt":
                part = _dot_nt(a_ref[...], b_ref[...])
            else:
                part = _dot_tn(a_ref[...], b_ref[...])

            if first_k:
                @pl.when(k == 0)
                def _():
                    acc[...] = part + add_ref[...] if has_add else part

            if later_k:
                @pl.when(k > 0)
                def _():
                    acc[...] += part

        if len(a_list) > 1:
            for a_ref, (k0, cnt) in zip(a_refs, a_rng):
                @pl.when((k >= k0) & (k < k0 + cnt))
                def _(a_ref=a_ref, k0=k0, cnt=cnt):
                    accumulate(a_ref, b_refs[0], k0 == 0, k0 + cnt > 1)
        elif len(b_list) > 1:
            for b_ref, (j0, cnt) in zip(b_refs, b_rng):
                @pl.when((j >= j0) & (j < j0 + cnt))
                def _(b_ref=b_ref):
                    accumulate(a_refs[0], b_ref, True, nk > 1)
        else:
            accumulate(a_refs[0], b_refs[0], True, nk > 1)

        @pl.when(k == nk - 1)
        def _():
            o_ref[...] = acc[...].astype(out_dtype)

        if comms:
            @pl.when((i == ni - 1) & (j == nj - 1) & (k == nk - 1))
            def _():
                for c, ci, co, cs in zip(comms, cin, cout, csem):
                    c.finish(ci, co, cs)

    def a_spec(k0, cnt):
        if dims == "tn":
            return pl.BlockSpec((tk, tm), lambda i, j, k: (k, i))
        return pl.BlockSpec((tm, tk), lambda i, j, k: (i, jnp.clip(k - k0, 0, cnt - 1)))

    def b_spec(j0, cnt):
        if dims == "nt":
            return pl.BlockSpec((tn, tk), lambda i, j, k: (j, k))
        return pl.BlockSpec((tk, tn), lambda i, j, k: (k, jnp.clip(j - j0, 0, cnt - 1)))

    o_spec = pl.BlockSpec((tm, tn), lambda i, j, k: (i, j))
    hbm = pl.BlockSpec(memory_space=pl.ANY)
    ins = a_list + b_list + ([add] if has_add else []) + [x for c in comms for x in c.ins]
    in_specs = ([a_spec(*r) for r in a_rng] + [b_spec(*r) for r in b_rng] + ([o_spec] if has_add else [])
                + [hbm] * sum(c_in))
    out_specs = [o_spec] + [hbm] * sum(c_out)
    out_shape = [SDS((m, n), out_dtype)] + [s for c in comms for s in c.out_shapes]
    scratch = [pltpu.VMEM((tm, tn), F32)] + [s for c in comms for s in c.sems]
    sem = ("arbitrary",) * 3 if comms else ("parallel", "parallel", "arbitrary")
    res = pl.pallas_call(
        body, grid=(ni, nj, nk), in_specs=in_specs, out_specs=out_specs, out_shape=out_shape, scratch_shapes=scratch,
        name=name, compiler_params=_params(sem, vmem))(*ins)
    if not comms:
        return res[0]
    outs, pos = [], 1
    for cnt in c_out:
        outs.append(list(res[pos:pos + cnt]))
        pos += cnt
    return res[0], outs


def _swa_specs(nb):
    prev = lambda off: pl.BlockSpec((pl.Element(WINDOW), pl.Element(1024)),
                                    lambda n: (jnp.maximum(n - 1, 0) * WINDOW, off))
    return [_win(WINDOW, 1024, QA), _win(WINDOW, 1024, VA), prev(QA), prev(VA),
            _const((1, 768)), _const((1, 256)), _const((1, LANE)), _const((768, 768)),
            _const((A_HEADS // A_GROUP, 2 * WINDOW, A_GROUP * WINDOW))]


def _swa_fwd(proj, qg, kg, sinks, bd, bias):
    s = proj.shape[0]
    nb = s // WINDOW

    def body(qk_ref, vz_ref, qkp_ref, vzp_ref, qg_ref, kg_ref, sk_ref, bd_ref, bias_ref, o_ref):
        first = pl.program_id(0) == 0
        o_ref[...] = _swa_fn(qk_ref[...], vz_ref[...], qkp_ref[...], vzp_ref[...], qg_ref[...], kg_ref[...],
                             sk_ref[...], bd_ref[...], bias_ref[...], first).astype(BF16)

    return pl.pallas_call(
        body, grid=(nb,), in_specs=_swa_specs(nb), out_specs=_rowblk(WINDOW, 768),
        out_shape=SDS((s, 768), BF16), name="swa_fwd",
        compiler_params=_params(("parallel",)))(proj, proj, proj, proj, qg, kg, sinks, bd, bias)


def _swa_bwd(proj, qg, kg, sinks, bd, bias, dga):
    s = proj.shape[0]
    nb = s // WINDOW

    def body(qk_ref, vz_ref, qkp_ref, vzp_ref, qg_ref, kg_ref, sk_ref, bd_ref, bias_ref, dg_ref,
             dcur_ref, dprev_ref, dqg_ref, dkg_ref, dsk_ref):
        first = pl.program_id(0) == 0
        bd_v = bd_ref[...]
        bias_v = bias_ref[...]
        fn = lambda qk, vz, qkp, vzp, qg_, kg_, sk: _swa_fn(qk, vz, qkp, vzp, qg_, kg_, sk, bd_v, bias_v, first)
        _, vjp = jax.vjp(fn, qk_ref[...], vz_ref[...], qkp_ref[...], vzp_ref[...], qg_ref[...], kg_ref[...], sk_ref[...])
        dqk, dvz, dqkp, dvzp, dqg, dkg, dsk = vjp(dg_ref[...])

        @pl.when(first)
        def _():
            dqg_ref[...] = jnp.zeros_like(dqg_ref)
            dkg_ref[...] = jnp.zeros_like(dkg_ref)
            dsk_ref[...] = jnp.zeros_like(dsk_ref)

        dqg_ref[...] += dqg
        dkg_ref[...] += dkg
        dsk_ref[...] += dsk
        dcur_ref[...] = jnp.concatenate([dqk, dvz], axis=1)
        dprev_ref[...] = jnp.concatenate([dqkp[:, 768:], dvzp[:, :256]], axis=1)

    return pl.pallas_call(
        body, grid=(nb,), in_specs=_swa_specs(nb) + [_rowblk(WINDOW, 768)],
        out_specs=[_rowblk(WINDOW, 2048), pl.BlockSpec((None, WINDOW, 512), lambda n: (n, 0, 0)),
                   _const((1, 768)), _const((1, 256)), _const((1, LANE))],
        out_shape=[SDS((s, 2048), F32), SDS((nb, WINDOW, 512), F32), SDS((1, 768), F32), SDS((1, 256), F32),
                   SDS((1, LANE), F32)],
        name="swa_bwd", compiler_params=_params(("arbitrary",)))(proj, proj, proj, proj, qg, kg, sinks, bd, bias, dga)


def _swa_combine(dcur, dprev):
    s = dcur.shape[0]
    nb = s // WINDOW

    def body(c_ref, p_ref, o_ref):
        c = c_ref[...]
        nxt = jnp.where(pl.program_id(0) == nb - 1, 0.0, p_ref[...])
        o_ref[...] = jnp.concatenate([c[:, :768], c[:, 768:1280] + nxt, c[:, 1280:]], axis=1).astype(BF16)

    return pl.pallas_call(
        body, grid=(nb,),
        in_specs=[_rowblk(WINDOW, 2048), pl.BlockSpec((None, WINDOW, 512), lambda n: (jnp.minimum(n + 1, nb - 1), 0, 0))],
        out_specs=_rowblk(WINDOW, 2048), out_shape=SDS((s, 2048), BF16), name="swa_combine",
        compiler_params=_params(("parallel",)))(dcur, dprev)


def _mem_fwd(proj, mkv, qg, kg, bd, *, tr):
    s = proj.shape[0]

    def body(qz_ref, mkv_ref, qg_ref, kg_ref, bd_ref, o_ref):
        o_ref[...] = _mem_fn(qz_ref[...], mkv_ref[...], qg_ref[...], kg_ref[...], bd_ref[...]).astype(BF16)

    return pl.pallas_call(
        body, grid=(s // tr,),
        in_specs=[_win(tr, 1024, QC), _const(mkv.shape), _const((1, 512)), _const((1, 512)), _const((512, 512))],
        out_specs=_rowblk(tr, 512), out_shape=SDS((s, 512), BF16), name="mem_fwd",
        compiler_params=_params(("parallel",)))(proj, mkv, qg, kg, bd)


def _mem_bwd(proj, mkv, qg, kg, bd, dgc, *, tr):
    s = proj.shape[0]

    def body(qz_ref, mkv_ref, qg_ref, kg_ref, bd_ref, dg_ref, dqz_ref, dmkv_ref, dqg_ref, dkg_ref):
        bd_v = bd_ref[...]
        fn = lambda qz, mkv_, qg_, kg_: _mem_fn(qz, mkv_, qg_, kg_, bd_v)
        _, vjp = jax.vjp(fn, qz_ref[...], mkv_ref[...], qg_ref[...], kg_ref[...])
        dqz, dmkv, dqg, dkg = vjp(dg_ref[...])

        @pl.when(pl.program_id(0) == 0)
        def _():
            dmkv_ref[...] = jnp.zeros_like(dmkv_ref)
            dqg_ref[...] = jnp.zeros_like(dqg_ref)
            dkg_ref[...] = jnp.zeros_like(dkg_ref)

        dmkv_ref[...] += dmkv
        dqg_ref[...] += dqg
        dkg_ref[...] += dkg
        dqz_ref[...] = dqz.astype(BF16)

    return pl.pallas_call(
        body, grid=(s // tr,),
        in_specs=[_win(tr, 1024, QC), _const(mkv.shape), _const((1, 512)), _const((1, 512)), _const((512, 512)),
                  _rowblk(tr, 512)],
        out_specs=[_rowblk(tr, 1024), _const(mkv.shape), _const((1, 512)), _const((1, 512))],
        out_shape=[SDS((s, 1024), BF16), SDS(mkv.shape, F32), SDS((1, 512), F32), SDS((1, 512), F32)],
        name="mem_bwd", compiler_params=_params(("arbitrary",)))(proj, mkv, qg, kg, bd, dgc)


def _log_sigmoid(x):
    return jnp.minimum(x, 0.0) - jnp.log1p(jnp.exp(-jnp.abs(x)))


FOX_TQ, FOX_TK = 512, 512
FOX_FWD_TQ, FOX_FWD_TK = 512, 1024


def _fox_tiles(s):
    return min(FOX_TQ, s), min(FOX_TK, s)


AUG = 128 * B_HEADS
COL_A, COL_B = 64, 67


def _split3(c):
    hi = c.astype(BF16)
    r1 = c - hi.astype(F32)
    mid = r1.astype(BF16)
    lo = (r1 - mid.astype(F32)).astype(BF16)
    return hi, mid, lo


def _expand_mats():
    def mat(col0):
        e = np.zeros((768 + 3 * LANE, AUG), np.float32)
        for h in range(B_HEADS):
            for d in range(HEAD_DIM):
                e[64 * h + d, 128 * h + d] = 1.0
            for part in range(3):
                e[768 + LANE * part + h, 128 * h + col0 + part] = 1.0
        return e

    def ones(col0):
        o = np.zeros((1, AUG), np.float32)
        for h in range(B_HEADS):
            o[0, 128 * h + col0:128 * h + col0 + 3] = 1.0
        return o

    return (jnp.asarray(mat(COL_A), BF16), jnp.asarray(mat(COL_B), BF16), jnp.asarray(ones(COL_A)), jnp.asarray(ones(COL_B)))


def _augment(data_bf16, triple, emat, ones_row):
    parts = [data_bf16] + (list(triple) if triple is not None else [jnp.zeros((data_bf16.shape[0], LANE), BF16)] * 3)
    wide = _dot(jnp.concatenate(parts, axis=1), emat)
    if ones_row is not None:
        wide = wide + ones_row
    return wide


def _compact(wide):
    return jnp.concatenate([wide[:, 128 * h:128 * h + 64] for h in range(wide.shape[1] // 128)], axis=1)


def _lane_of_heads(wide, col, first=0):
    rows = wide.shape[0]
    lane = lax.broadcasted_iota(jnp.int32, (rows, LANE), 1)
    out = jnp.zeros((rows, LANE), F32)
    for h in range(wide.shape[1] // 128):
        out = jnp.where(lane == first + h, wide[:, 128 * h + col:128 * h + col + 1], out)
    return out


def _fox2_prep(proj, fbl, qg, kg, bfor, bd, ea, eb, ones_a, ones_b, *, tr):
    s = proj.shape[0]
    tri = jnp.asarray(np.tril(np.ones((tr, tr), np.float32)))

    def body(q_ref, k_ref, v_ref, fb_ref, qg_ref, kg_ref, bf_ref, bd_ref, tri_ref, ea_ref, eb_ref, oa_ref, ob_ref,
             qat_ref, ka_ref, kat_ref, va_ref, vat_ref, qn_ref, c_ref, carry):
        @pl.when(pl.program_id(0) == 0)
        def _():
            carry[...] = jnp.zeros_like(carry)

        bd_v = bd_ref[...]
        lane = lax.broadcasted_iota(jnp.int32, (tr, LANE), 1)
        logf = jnp.where(lane < N_FORGET, _log_sigmoid(fb_ref[...] + bf_ref[...]), 0.0)
        c = jnp.dot(tri_ref[...], logf, precision=HI, preferred_element_type=F32) + carry[...]
        c_ref[...] = c
        carry[...] = c[tr - 1:tr, :]
        qn = _qn_fn(q_ref[...], qg_ref[...], bd_v).astype(BF16)
        kn = _kn_fn(k_ref[...], kg_ref[...], bd_v).astype(BF16)
        qn_ref[...] = qn
        qat_ref[...] = jnp.transpose(_augment(qn, _split3(c), ea_ref[...], ob_ref[...])).astype(BF16)
        ka = _augment(kn, _split3(-c), eb_ref[...], oa_ref[...])
        ka_ref[...] = ka.astype(BF16)
        kat_ref[...] = jnp.transpose(ka).astype(BF16)
        va = _augment(v_ref[...].astype(BF16), None, ea_ref[...], oa_ref[...])
        va_ref[...] = va.astype(BF16)
        vat_ref[...] = jnp.transpose(va).astype(BF16)

    emat = _const((768 + 3 * LANE, AUG))
    return pl.pallas_call(
        body, grid=(s // tr,),
        in_specs=[_win(tr, 768, QB), _win(tr, 768, KB), _win(tr, 768, VB), _rowblk(tr, LANE), _const((1, 768)),
                  _const((1, 768)), _const((1, LANE)), _const((768, 768)), _const((tr, tr)), emat, emat,
                  _const((1, AUG)), _const((1, AUG))],
        out_specs=[pl.BlockSpec((AUG, tr), lambda i: (0, i)), _rowblk(tr, AUG), pl.BlockSpec((AUG, tr), lambda i: (0, i)),
                   _rowblk(tr, AUG), pl.BlockSpec((AUG, tr), lambda i: (0, i)), _rowblk(tr, 768), _rowblk(tr, LANE)],
        out_shape=[SDS((AUG, s), BF16), SDS((s, AUG), BF16), SDS((AUG, s), BF16), SDS((s, AUG), BF16),
                   SDS((AUG, s), BF16), SDS((s, 768), BF16), SDS((s, LANE), F32)],
        scratch_shapes=[pltpu.VMEM((1, LANE), F32)], name="fox_prep",
        compiler_params=_params(("arbitrary",)))(proj, proj, proj, fbl, qg, kg, bfor, bd, tri, ea, eb, ones_a, ones_b)


def _fox2_fwd(proj, qat, ka, vat):
    s = proj.shape[0]
    tq, tk = min(FOX_FWD_TQ, s), min(FOX_FWD_TK, s)
    nq, nk = s // tq, s // tk

    def last_k(i):
        return (i * tq + tq - 1) // tk

    def body(qt_ref, k_ref, vt_ref, z_ref, gb_ref, yb_ref, lse_ref, acc, m_s):
        i, j = pl.program_id(0), pl.program_id(1)

        @pl.when(j == 0)
        def _():
            acc[...] = jnp.zeros_like(acc)
            m_s[...] = jnp.full_like(m_s, NEG)

        def tile(masked):
            if masked:
                kpos = j * tk + lax.broadcasted_iota(jnp.int32, (tk, tq), 0)
                qpos = i * tq + lax.broadcasted_iota(jnp.int32, (tk, tq), 1)
                mask = kpos <= qpos
            for h in range(B_HEADS):
                sl = slice(128 * h, 128 * h + 128)
                sc = _dot(k_ref[:, sl], qt_ref[sl, :])
                if masked:
                    sc = jnp.where(mask, sc, NEG)
                m_prev = m_s[h:h + 1, :]
                m_new = jnp.maximum(m_prev, jnp.max(sc, axis=0, keepdims=True))
                p = jnp.exp(sc - m_new).astype(BF16)
                acc[sl, :] = jnp.exp(m_prev - m_new) * acc[sl, :] + _dot(vt_ref[sl, :], p)
                m_s[h:h + 1, :] = m_new

        full = j * tk + tk - 1 <= i * tq

        @pl.when(full)
        def _():
            tile(False)

        @pl.when(jnp.logical_and(jnp.logical_not(full), j <= last_k(i)))
        def _():
            tile(True)

        @pl.when(j == nk - 1)
        def _():
            outs = []
            row = lax.broadcasted_iota(jnp.int32, (LANE, tq), 0)
            lse_t = jnp.zeros((LANE, tq), F32)
            for h in range(B_HEADS):
                l_row = acc[128 * h + COL_A:128 * h + COL_A + 1, :]
                outs.append(acc[128 * h:128 * h + 64, :] * (1.0 / l_row))
                lse_t = jnp.where(row == h, m_s[h:h + 1, :] + jnp.log(l_row), lse_t)
            y = jnp.transpose(jnp.concatenate(outs, axis=0))
            yb_ref[...] = y
            gb_ref[...] = (y * _silu(z_ref[...])).astype(BF16)
            lse_ref[...] = jnp.transpose(lse_t)

    kcol = lambda i, j: (0, jnp.minimum(j, last_k(i)))
    return pl.pallas_call(
        body, grid=(nq, nk),
        in_specs=[pl.BlockSpec((AUG, tq), lambda i, j: (0, i)),
                  pl.BlockSpec((tk, AUG), lambda i, j: (jnp.minimum(j, last_k(i)), 0)),
                  pl.BlockSpec((AUG, tk), kcol),
                  pl.BlockSpec((pl.Element(tq), pl.Element(768)), lambda i, j: (i * tq, ZB))],
        out_specs=[pl.BlockSpec((tq, 768), lambda i, j: (i, 0)), pl.BlockSpec((tq, 768), lambda i, j: (i, 0)),
                   pl.BlockSpec((tq, LANE), lambda i, j: (i, 0))],
        out_shape=[SDS((s, 768), BF16), SDS((s, 768), F32), SDS((s, LANE), F32)],
        scratch_shapes=[pltpu.VMEM((AUG, tq), F32), pltpu.VMEM((16, tq), F32)],
        name="fox_fwd", compiler_params=_params(("parallel", "arbitrary")))(qat, ka, vat, proj)


def _fox2_bwd_pre(proj, yb, dgb, qn, c, lse, hsum, ea, ones_b, *, tr):
    s = proj.shape[0]

    def body(z_ref, y_ref, dg_ref, qn_ref, c_ref, lse_ref, hs_ref, ea_ref, ob_ref,
             qa_ref, qat_ref, dya_ref, dyat_ref, dz_ref):
        z, y, dg = z_ref[...], y_ref[...], dg_ref[...]
        sg = jax.nn.sigmoid(z)
        dy = dg * (z * sg)
        dz_ref[...] = (dg * y * (sg * (1.0 + z * (1.0 - sg)))).astype(BF16)
        delta = jnp.dot(dy * y, hs_ref[...], precision=HI, preferred_element_type=F32)
        e = ea_ref[...]
        dya = _augment(dy.astype(BF16), _split3(-delta), e, None)
        dya_ref[...] = dya.astype(BF16)
        dyat_ref[...] = jnp.transpose(dya).astype(BF16)
        qa = _augment(qn_ref[...], _split3(c_ref[...] - lse_ref[...]), e, ob_ref[...])
        qa_ref[...] = qa.astype(BF16)
        qat_ref[...] = jnp.transpose(qa).astype(BF16)

    return pl.pallas_call(
        body, grid=(s // tr,),
        in_specs=[_win(tr, 768, ZB), _rowblk(tr, 768), _rowblk(tr, 768), _rowblk(tr, 768), _rowblk(tr, LANE),
                  _rowblk(tr, LANE), _const((768, LANE)), _const((768 + 3 * LANE, AUG)), _const((1, AUG))],
        out_specs=[_rowblk(tr, AUG), pl.BlockSpec((AUG, tr), lambda i: (0, i)), _rowblk(tr, AUG),
                   pl.BlockSpec((AUG, tr), lambda i: (0, i)), _rowblk(tr, 768)],
        out_shape=[SDS((s, AUG), BF16), SDS((AUG, s), BF16), SDS((s, AUG), BF16), SDS((AUG, s), BF16),
                   SDS((s, 768), BF16)], name="fox_bwd_pre",
        compiler_params=_params(("parallel",)))(proj, yb, dgb, qn, c, lse, hsum, ea, ones_b)


def _fox2_bwd(qb, qbt, ka, kat, va, dya, dyat):
    s = qb.shape[0]
    tq, tk = _fox_tiles(s)
    nq, nk = s // tq, s // tk
    ng = 2
    gh = B_HEADS // ng
    gw = 128 * gh

    def first_q(j):
        return (j * tk) // tq

    def body(q_ref, qt_ref, k_ref, kt_ref, v_ref, dy_ref, dyt_ref, dq_hbm, dk_ref, dv_ref, dck_ref,
             dq_acc, dk_acc, dv_acc, sem):
        g, j, i = pl.program_id(0), pl.program_id(1), pl.program_id(2)

        @pl.when((j == 0) & (i == 0))
        def _():
            dq_acc[...] = jnp.zeros_like(dq_acc)

        @pl.when(i == 0)
        def _():
            dk_acc[...] = jnp.zeros_like(dk_acc)
            dv_acc[...] = jnp.zeros_like(dv_acc)

        def tile(masked):
            if masked:
                kpos = j * tk + lax.broadcasted_iota(jnp.int32, (tk, tq), 0)
                qpos = i * tq + lax.broadcasted_iota(jnp.int32, (tk, tq), 1)
                mask = kpos <= qpos
            cols = pl.ds(pl.multiple_of(i * tq, tq), tq)
            for h in range(gh):
                sl = slice(128 * h, 128 * h + 128)
                sc = _dot(k_ref[:, sl], qt_ref[sl, :])
                if masked:
                    sc = jnp.where(mask, sc, NEG)
                p = jnp.exp(sc)
                ds = (p * _dot(v_ref[:, sl], dyt_ref[sl, :])).astype(BF16)
                dv_acc[:, sl] += _dot(p.astype(BF16), dy_ref[:, sl])
                dk_acc[:, sl] += _dot(ds, q_ref[:, sl])
                dq_acc[sl, cols] += _dot(kt_ref[sl, :], ds)

        full = j * tk + tk - 1 <= i * tq

        @pl.when(full)
        def _():
            tile(False)

        @pl.when(jnp.logical_and(jnp.logical_not(full), i >= first_q(j)))
        def _():
            tile(True)

        @pl.when(i == nq - 1)
        def _():
            dkw = dk_acc[...]
            dk_ref[...] = _compact(dkw)
            dv_ref[...] = _compact(dv_acc[...]).astype(BF16)
            dck_ref[...] = -_lane_of_heads(dkw, COL_B, gh * g)

        @pl.when((j == nk - 1) & (i == nq - 1))
        def _():
            cp = pltpu.make_async_copy(dq_acc, dq_hbm.at[pl.ds(pl.multiple_of(g * gw, gw), gw)], sem)
            cp.start()
            cp.wait()

    qrow = pl.BlockSpec((tq, gw), lambda g, j, i: (jnp.maximum(i, first_q(j)), g))
    qcol = pl.BlockSpec((gw, tq), lambda g, j, i: (g, jnp.maximum(i, first_q(j))))
    krow = pl.BlockSpec((tk, gw), lambda g, j, i: (j, g))
    kcol = pl.BlockSpec((gw, tk), lambda g, j, i: (g, j))
    kout = pl.BlockSpec((tk, gw // 2), lambda g, j, i: (j, g))
    return pl.pallas_call(
        body, grid=(ng, nk, nq),
        in_specs=[qrow, qcol, krow, kcol, krow, qrow, qcol],
        out_specs=[pl.BlockSpec(memory_space=pl.ANY), kout, kout,
                   pl.BlockSpec((None, tk, LANE), lambda g, j, i: (g, j, 0))],
        out_shape=[SDS((AUG, s), F32), SDS((s, 768), F32), SDS((s, 768), BF16), SDS((ng, s, LANE), F32)],
        scratch_shapes=[pltpu.VMEM((gw, s), F32), pltpu.VMEM((tk, gw), F32), pltpu.VMEM((tk, gw), F32),
                        pltpu.SemaphoreType.DMA],
        name="fox_bwd", compiler_params=_params(("arbitrary",) * 3))(qb, qbt, ka, kat, va, dya, dyat)


def _fox2_bwd_post(proj, fbl, qg, kg, bfor, bd, dqa, dkn, dck, *, tr):
    s = proj.shape[0]
    nb = s // tr
    triu = jnp.asarray(np.triu(np.ones((tr, tr), np.float32)))
    rev = lambda i: nb - 1 - i

    def body(q_ref, k_ref, fb_ref, qg_ref, kg_ref, bf_ref, bd_ref, tri_ref, dqa_ref, dkn_ref, dck_ref,
             dq_ref, dk_ref, dfb_ref, dqg_ref, dkg_ref, dbf_ref, carry):
        @pl.when(pl.program_id(0) == 0)
        def _():
            carry[...] = jnp.zeros_like(carry)
            dqg_ref[...] = jnp.zeros_like(dqg_ref)
            dkg_ref[...] = jnp.zeros_like(dkg_ref)
            dbf_ref[...] = jnp.zeros_like(dbf_ref)

        bd_v = bd_ref[...]
        dqw = jnp.transpose(dqa_ref[...])
        _, vjp_q = jax.vjp(lambda q, g: _qn_fn(q, g, bd_v), q_ref[...], qg_ref[...])
        dq, dqg = vjp_q(_compact(dqw))
        _, vjp_k = jax.vjp(lambda k, g: _kn_fn(k, g, bd_v), k_ref[...], kg_ref[...])
        dk, dkg = vjp_k(dkn_ref[...])
        dq_ref[...] = dq.astype(BF16)
        dk_ref[...] = dk.astype(BF16)
        dqg_ref[...] += dqg
        dkg_ref[...] += dkg

        dc = _lane_of_heads(dqw, COL_A) + (dck_ref[0] + dck_ref[1])
        dlogf = jnp.dot(tri_ref[...], dc, precision=HI, preferred_element_type=F32) + carry[...]
        carry[...] = dlogf[0:1, :]
        lane = lax.broadcasted_iota(jnp.int32, (tr, LANE), 1)
        xf = fb_ref[...] + bf_ref[...]
        dfb = jnp.where(lane < N_FORGET, dlogf * jax.nn.sigmoid(-xf), 0.0)
        dfb_ref[...] = dfb.astype(BF16)
        dbf_ref[...] += jnp.sum(dfb, axis=0, keepdims=True)

    rb = lambda w: pl.BlockSpec((tr, w), lambda i: (rev(i), 0))
    wn = lambda w, off: pl.BlockSpec((pl.Element(tr), pl.Element(w)), lambda i: (rev(i) * tr, off))
    return pl.pallas_call(
        body, grid=(nb,),
        in_specs=[wn(768, QB), wn(768, KB), rb(LANE), _const((1, 768)), _const((1, 768)), _const((1, LANE)),
                  _const((768, 768)), _const((tr, tr)), pl.BlockSpec((AUG, tr), lambda i: (0, rev(i))), rb(768),
                  pl.BlockSpec((2, tr, LANE), lambda i: (0, rev(i), 0))],
        out_specs=[rb(768), rb(768), rb(LANE), _const((1, 768)), _const((1, 768)), _const((1, LANE))],
        out_shape=[SDS((s, 768), BF16), SDS((s, 768), BF16), SDS((s, LANE), BF16), SDS((1, 768), F32),
                   SDS((1, 768), F32), SDS((1, LANE), F32)],
        scratch_shapes=[pltpu.VMEM((1, LANE), F32)], name="fox_bwd_post",
        compiler_params=_params(("arbitrary",)))(proj, proj, fbl, qg, kg, bfor, bd, triu, dqa, dkn, dck)


def _merge_specs(tr):
    row = lambda w: pl.BlockSpec((tr, w), lambda i, j: (i, 0))
    shard = lambda r: pl.BlockSpec((None, r, 512), lambda i, j: (j, 0, 0))
    gate = lambda b: pl.BlockSpec((tr, 512), lambda i, j: (i, (GATE + 2048 * b) // 512 + j))
    return [row(768), row(768), row(512), shard(768), shard(768), shard(512), gate(0), gate(1), gate(2)]


def _merge_fwd(proj, ga, gb, gc, wa, wb, wc, *, tr):
    s = proj.shape[0]

    def body(ga_ref, gb_ref, gc_ref, wa_ref, wb_ref, wc_ref, l0_ref, l1_ref, l2_ref, y_ref):
        ua = _dot(ga_ref[...], wa_ref[...])
        ub = _dot(gb_ref[...], wb_ref[...])
        uc = _dot(gc_ref[...], wc_ref[...])
        y = jax.nn.sigmoid(l0_ref[...]) * ua + jax.nn.sigmoid(l1_ref[...]) * ub + jax.nn.sigmoid(l2_ref[...]) * uc
        y_ref[...] = y.astype(BF16)

    return pl.pallas_call(
        body, grid=(s // tr, N_CHIPS), in_specs=_merge_specs(tr),
        out_specs=pl.BlockSpec((tr, 512), lambda i, j: (i, j)), out_shape=SDS((s, D_MODEL), BF16), name="merge_fwd",
        compiler_params=_params(("parallel", "arbitrary")))(ga, gb, gc, wa, wb, wc, proj, proj, proj)


def _merge_bwd(proj, ga, gb, gc, wa, wb, wc, dy, *, tr):
    s = proj.shape[0]

    def body(ga_ref, gb_ref, gc_ref, wa_ref, wb_ref, wc_ref, l0_ref, l1_ref, l2_ref, dy_ref,
             dl0_ref, dl1_ref, dl2_ref, dua_ref, dub_ref, duc_ref, dga_ref, dgb_ref, dgc_ref):
        j = pl.program_id(1)
        dyv = dy_ref[...]

        @pl.when(j == 0)
        def _():
            dga_ref[...] = jnp.zeros_like(dga_ref)
            dgb_ref[...] = jnp.zeros_like(dgb_ref)
            dgc_ref[...] = jnp.zeros_like(dgc_ref)

        for g_ref, w_ref, l_ref, dl_ref, du_ref, dg_ref in (
                (ga_ref, wa_ref, l0_ref, dl0_ref, dua_ref, dga_ref),
                (gb_ref, wb_ref, l1_ref, dl1_ref, dub_ref, dgb_ref),
                (gc_ref, wc_ref, l2_ref, dl2_ref, duc_ref, dgc_ref)):
            w = w_ref[...]
            u = _dot(g_ref[...], w)
            sg = jax.nn.sigmoid(l_ref[...])
            dl_ref[...] = (dyv * u * sg * (1.0 - sg)).astype(BF16)
            du = (dyv * sg).astype(BF16)
            du_ref[...] = du
            dg_ref[...] += _dot_nt(du, w)

    blk = pl.BlockSpec((tr, 512), lambda i, j: (i, j))
    row = lambda w: pl.BlockSpec((tr, w), lambda i, j: (i, 0))
    big = SDS((s, D_MODEL), BF16)
    return pl.pallas_call(
        body, grid=(s // tr, N_CHIPS), in_specs=_merge_specs(tr) + [blk],
        out_specs=[blk] * 6 + [row(768), row(768), row(512)],
        out_shape=[big] * 6 + [SDS((s, 768), F32), SDS((s, 768), F32), SDS((s, 512), F32)], name="merge_bwd",
        compiler_params=_params(("parallel", "arbitrary")))(ga, gb, gc, wa, wb, wc, proj, proj, proj, dy)


def _out_loss(y, wo, x, tgt, *, tr, tn):
    s = x.shape[0]

    def body(y_ref, w_ref, x_ref, t_ref, d_ref, db_ref, sq_ref):
        @pl.when((pl.program_id(0) == 0) & (pl.program_id(1) == 0))
        def _():
            sq_ref[...] = jnp.zeros_like(sq_ref)

        out = x_ref[...] + _dot(y_ref[...], w_ref[...])
        diff = out - t_ref[...]
        sq_ref[...] += jnp.sum(diff * diff, axis=0, keepdims=True)
        d = diff * (1.0 / D_MODEL)
        d_ref[...] = d
        db_ref[...] = d.astype(BF16)

    blk = pl.BlockSpec((tr, tn), lambda i, j: (i, j))
    return pl.pallas_call(
        body, grid=(s // tr, D_MODEL // tn),
        in_specs=[pl.BlockSpec((tr, D_MODEL), lambda i, j: (i, 0)), pl.BlockSpec((D_MODEL, tn), lambda i, j: (0, j)), blk, blk],
        out_specs=[blk, blk, _const((1, tn))],
        out_shape=[SDS((s, D_MODEL), F32), SDS((s, D_MODEL), BF16), SDS((1, tn), F32)], name="out_loss",
        compiler_params=_params(("arbitrary", "arbitrary")))(y, wo, x, tgt)


def _tile_gain(g, reps):
    return jnp.tile(g.reshape(1, -1), (1, reps))


def _pad_lane(v):
    v = v.reshape(1, -1)
    return jnp.pad(v, ((0, 0), (0, LANE - v.shape[1])))


def _local_step(x, mem, tgt, w_main, w_fb, w_small, norm_gain, mem_norm_gain, b_forget,
                q_gain_a, k_gain_a, sinks_a, q_gain_b, k_gain_b, q_gain_c, k_gain_c, core=None, hn=None):
    s = x.shape[0]
    tr = min(512, s)
    bd64 = _block_diag(768, HEAD_DIM)
    bd128 = _block_diag(512, C_HEAD_DIM)
    hsum = _head_sum(768, HEAD_DIM)
    qga, kga = _tile_gain(q_gain_a, 12), _tile_gain(k_gain_a, 4)
    qgb, kgb = _tile_gain(q_gain_b, 12), _tile_gain(k_gain_b, 12)
    qgc, kgc = _tile_gain(q_gain_c, 4), _tile_gain(k_gain_c, 4)
    sinks = _pad_lane(sinks_a)
    bfor = _pad_lane(b_forget)

    if hn is None:
        hn = _rms_fwd(x, norm_gain, tr=tr, name="rms_x")
    on_mesh = core is not None
    if on_mesh:
        proj, (gathered,) = _matmul(hn, w_main, dims="nn", out_dtype=F32, tm=1024, tn=1024, tk=D_MODEL, name="proj_main",
                                    comms=[_gather_comm(list(w_small))])
        w_mk, wa, wb, wc, wo = gathered
        w_mk, wo = w_mk.reshape(D_MODEL, 1024), wo.reshape(D_MODEL, D_MODEL)
    else:
        proj = _matmul(hn, w_main, dims="nn", out_dtype=F32, tm=1024, tn=1024, tk=D_MODEL, name="proj_main")
        w_mk, wa, wb, wc, wo = w_small
    fbl = _matmul(hn, w_fb, dims="nn", out_dtype=F32, tm=1024, tn=LANE, tk=D_MODEL, name="proj_forget")
    memn = _rms_fwd(mem, mem_norm_gain, tr=mem.shape[0], name="rms_mem")
    mkv = _matmul(memn, w_mk, dims="nn", out_dtype=F32, tm=256, tn=512, tk=D_MODEL, name="mem_kv")

    swa_bias = _swa_bias()
    ga = _swa_fwd(proj, qga, kga, sinks, bd64, swa_bias)
    ea, eb, ones_a, ones_b = _expand_mats()
    tf = min(256, s)
    qat, ka, kat, va, vat, qn, cfox = _fox2_prep(proj, fbl, qgb, kgb, bfor, bd64, ea, eb, ones_a, ones_b, tr=tf)
    gb, yb, lse = _fox2_fwd(proj, qat, ka, vat)
    gc = _mem_fwd(proj, mkv, qgc, kgc, bd128, tr=tr)
    y = _merge_fwd(proj, ga, gb, gc, wa, wb, wc, tr=tr)
    dout, dout_b, sq = _out_loss(y, wo, x, tgt, tr=tr, tn=512)

    d_wo = _matmul(y, dout_b, dims="tn", out_dtype=F32, tm=1024, tn=512, tk=4096, name="dw_out")
    dy = _matmul(dout_b, wo, dims="nt", out_dtype=F32, tm=1024, tn=512, tk=D_MODEL, name="dy")
    dl0, dl1, dl2, dua, dub, duc, dga, dgb, dgc = _merge_bwd(proj, ga, gb, gc, wa, wb, wc, dy, tr=tr)
    d_wa = _matmul(ga, dua, dims="tn", out_dtype=F32, tm=768, tn=512, tk=4096, name="dw_branch_a")
    d_wb = _matmul(gb, dub, dims="tn", out_dtype=F32, tm=768, tn=512, tk=4096, name="dw_branch_b")
    d_wc = _matmul(gc, duc, dims="tn", out_dtype=F32, tm=512, tn=512, tk=4096, name="dw_branch_c")

    dcur, dprev, d_qga, d_kga, d_sinks = _swa_bwd(proj, qga, kga, sinks, bd64, swa_bias, dga)
    dproj_a = _swa_combine(dcur, dprev)

    qab, qabt, dya, dyat, dzb = _fox2_bwd_pre(proj, yb, dgb, qn, cfox, lse, hsum, ea, ones_b, tr=tf)
    dqa, dkn, dvb, dck = _fox2_bwd(qab, qabt, ka, kat, va, dya, dyat)
    dqb, dkb, dfb, d_qgb, d_kgb, d_bf = _fox2_bwd_post(proj, fbl, qgb, kgb, bfor, bd64, dqa, dkn, dck, tr=tf)

    dproj_c, dmkv, d_qgc, d_kgc = _mem_bwd(proj, mkv, qgc, kgc, bd128, dgc, tr=tr)
    dmkv_b = dmkv.astype(BF16)
    d_wmk = _matmul(memn, dmkv_b, dims="tn", out_dtype=F32, tm=1024, tn=512, tk=256, name="dw_mem_kv")
    dmemn = _matmul(dmkv_b, w_mk, dims="nt", out_dtype=F32, tm=256, tn=512, tk=1024, name="dmemn")
    (d_mem_gain,) = _rms_bwd(mem, mem_norm_gain, dmemn, None, tr=mem.shape[0], name="rms_mem_bwd")

    dproj = [dproj_a, jnp.concatenate([dqb, dkb, dvb, dzb, dproj_c], axis=1), dl0, dl1, dl2]
    dhn_f = _matmul(dfb, w_fb, dims="nt", out_dtype=F32, tm=1024, tn=512, tk=LANE, name="dhn_forget")
    d_wfb = _matmul(hn, dfb, dims="tn", out_dtype=F32, tm=1024, tn=LANE, tk=512, name="dw_forget")
    big = {}
    if on_mesh:
        half = D_MODEL // 2
        c0 = core[0]
        hn_other = lax.dynamic_slice(hn, (0, (1 - c0) * half), (s, half))
        hn_own = lax.dynamic_slice(hn, (0, c0 * half), (s, half))
        g1, k1 = [d_wmk, d_wa, d_wb, d_wc, d_wo], [2, 3, 4, 5, 6]
        d_other, (got1,) = _matmul(hn_other, dproj, dims="tn", out_dtype=F32, tm=1024, tn=512, tk=4096,
                                   name="dw_main_other", comms=[_exchange_comm(g1, k1)])
        h1 = [_add_half(g, got, core, HALF_AXIS[k], name=f"add_half_{k}") for g, got, k in zip(g1, got1, k1)]
        d_own, (got0, parts1) = _matmul(
            hn_own, dproj, dims="tn", out_dtype=F32, tm=1024, tn=512, tk=4096, name="dw_main_own",
            comms=[_exchange_comm([d_other, d_wfb], [0, 1], whole=(0,)), _scatter_comm(h1, k1)])
        h0 = [_add_pair(d_own, got0[0], name="add_pair_main"), _add_half(d_wfb, got0[1], core, 0, name="add_half_1")]
        sums1 = [_sum4(p, name=f"sum4_{k}") for p, k in zip(parts1, k1)]
        dhn, (parts0, theirs1) = _matmul(dproj, w_main, dims="nt", out_dtype=F32, tm=1024, tn=512, tk=2048, vmem=VMEM_WIDE, name="dhn",
                                         add=dhn_f, comms=[_scatter_comm(h0, [0, 1]), _swap_comm(sums1)])
        sums0 = [_sum4(p, name=f"sum4_{k}") for p, k in zip(parts0, (0, 1))]
        (grad_x, d_gain), theirs0 = _rms_bwd(x, norm_gain, dhn, dout, tr=tr, name="rms_x_bwd", comm=_swap_comm(sums0))
        big = dict(sums=sums0 + sums1, theirs=list(theirs0) + list(theirs1))
    else:
        dhn = _matmul(dproj, w_main, dims="nt", out_dtype=F32, tm=1024, tn=512, tk=2048, vmem=VMEM_WIDE, name="dhn", add=dhn_f)
        d_wmain = _matmul(hn, dproj, dims="tn", out_dtype=F32, tm=1024, tn=512, tk=4096, name="dw_main")
        big = dict(d_wmain=d_wmain, d_wfb=d_wfb, d_wmk=d_wmk, d_wa=d_wa, d_wb=d_wb, d_wc=d_wc, d_wo=d_wo)
        grad_x, d_gain = _rms_bwd(x, norm_gain, dhn, dout, tr=tr, name="rms_x_bwd")

    fold = lambda g, reps: jnp.sum(g.reshape(reps, -1), axis=0, keepdims=True)
    return dict(
        sq=sq, grad_x=grad_x, **big,
        d_gain=d_gain, d_mem_gain=d_mem_gain, d_bf=d_bf[:, :N_FORGET],
        d_qga=fold(d_qga, 12), d_kga=fold(d_kga, 4), d_sinks=d_sinks[:, :A_HEADS],
        d_qgb=fold(d_qgb, 12), d_kgb=fold(d_kgb, 12), d_qgc=fold(d_qgc, 4), d_kgc=fold(d_kgc, 4))


PACK_ROWS = 256
FORGET_IN_SHARD = FORGET_COL - SHARD_COLS
AFTER_FORGET = FORGET_COL - SLAB_START[1]
END_CHIP1 = 2 * SHARD_COLS - N_FORGET - SLAB_START[1]


def _pack_w_in(chip, w):
    rows = w.shape[1]
    tr = PACK_ROWS

    def body(k_ref, w_ref, o_ref, scr):
        scr[...] = jnp.zeros_like(scr)
        scr[pl.ds(0, SHARD_COLS), :] = w_ref[...]
        v = jnp.transpose(scr[...])
        k = k_ref[0]
        col = lax.broadcasted_iota(jnp.int32, (tr, SLAB), 1)
        no_forget = jnp.zeros((tr, LANE), BF16)

        @pl.when(k == 0)
        def _():
            o_ref[:, 0:SLAB] = v.astype(BF16)
            o_ref[:, SLAB:] = no_forget

        @pl.when(k == 1)
        def _():
            before = pltpu.roll(v, SLAB_SHIFT[1], axis=1)
            after = pltpu.roll(v, SLAB - (N_FORGET - SLAB_SHIFT[1]), axis=1)
            slab = jnp.where(col < AFTER_FORGET, before, jnp.where(col < END_CHIP1, after, 0.0))
            o_ref[:, 0:SLAB] = slab.astype(BF16)
            f = pltpu.roll(v, SLAB - FORGET_IN_SHARD, axis=1)[:, :LANE]
            o_ref[:, SLAB:] = jnp.where(col[:, :LANE] < N_FORGET, f, 0.0).astype(BF16)

        for kk in (2, 3):
            @pl.when(k == kk)
            def _(kk=kk):
                o_ref[:, 0:SLAB] = pltpu.roll(v, SLAB_SHIFT[kk], axis=1).astype(BF16)
                o_ref[:, SLAB:] = no_forget

    return pl.pallas_call(
        body, grid_spec=pltpu.PrefetchScalarGridSpec(
            num_scalar_prefetch=1, grid=(rows // tr,),
            in_specs=[pl.BlockSpec((SHARD_COLS, tr), lambda i, k: (0, i))],
            out_specs=pl.BlockSpec((None, tr, SLAB + LANE), lambda i, k: (k[0], i, 0)),
            scratch_shapes=[pltpu.VMEM((SLAB, tr), F32)]),
        out_shape=SDS((N_CHIPS, rows, SLAB + LANE), BF16), name="pack_w_in",
        compiler_params=_params(("arbitrary",)))(chip, w)


def _merge_slabs(g):
    rows = g.shape[1]
    tr = PACK_ROWS
    t = [s // LANE for s in SLAB_START]
    n_t = SLAB // LANE

    def body(g_ref, m_ref, f_ref):
        for k in range(N_CHIPS):
            lo = t[k] + (1 if k > 0 else 0)
            hi = t[k + 1] if k + 1 < N_CHIPS else t[k] + n_t
            m_ref[:, lo * LANE:hi * LANE] = g_ref[k, :, (lo - t[k]) * LANE:(hi - t[k]) * LANE]
            if k + 1 < N_CHIPS:
                a = g_ref[k, :, (hi - t[k]) * LANE:(hi - t[k] + 1) * LANE].astype(F32)
                b = g_ref[k + 1, :, 0:LANE].astype(F32)
                m_ref[:, hi * LANE:(hi + 1) * LANE] = (a + b).astype(BF16)
        f_ref[...] = g_ref[1, :, SLAB:]

    return pl.pallas_call(
        body, grid=(rows // tr,),
        in_specs=[pl.BlockSpec((N_CHIPS, tr, SLAB + LANE), lambda i: (0, i, 0))],
        out_specs=[_rowblk(tr, P_MAIN), _rowblk(tr, LANE)],
        out_shape=[SDS((rows, P_MAIN), BF16), SDS((rows, LANE), BF16)], name="merge_slabs",
        compiler_params=_params(("parallel",)))(g)


def _adamw_math(w, g, m, v):
    nm = ADAM_B1 * m + (1.0 - ADAM_B1) * g
    nv = ADAM_B2 * v + (1.0 - ADAM_B2) * (g * g)
    m_hat = nm / (1.0 - ADAM_B1 ** ADAM_STEP)
    v_hat = nv / (1.0 - ADAM_B2 ** ADAM_STEP)
    delta = -ADAM_LR * (m_hat / (jnp.sqrt(v_hat) + ADAM_EPS) + ADAM_WD * w)
    return delta, nm, nv


def _adamw(g, w, m, v, *, tr, name):
    rows, cols = w.shape
    tr = min(tr, rows)

    def body(g_ref, w_ref, m_ref, v_ref, d_ref, nm_ref, nv_ref):
        d, nm, nv = _adamw_math(w_ref[...], g_ref[...], m_ref[...], v_ref[...])
        d_ref[...] = d
        nm_ref[...] = nm
        nv_ref[...] = nv

    spec = _rowblk(tr, cols)
    return pl.pallas_call(
        body, grid=(rows // tr,), in_specs=[spec] * 4, out_specs=[spec] * 3,
        out_shape=[SDS((rows, cols), F32)] * 3, name=name, compiler_params=_params(("parallel",)))(g, w, m, v)


def _adamw_w_in(chip_core, slab_mine, slab_theirs, forget_mine, forget_theirs, w, m, v):
    rows = w.shape[1]
    tr = PACK_ROWS // 2
    nbh = rows // 2 // tr

    def body(k_ref, sa_ref, sb_ref, fa_ref, fb_ref, w_ref, m_ref, v_ref, g_ref, d_ref, nm_ref, nv_ref):
        use_mine = pl.program_id(0) // nbh == k_ref[1]
        sl = jnp.where(use_mine, sa_ref[...], sb_ref[...])
        f_tile = jnp.where(use_mine, fa_ref[...], fb_ref[...])
        k = k_ref[0]

        def emit(wide):
            g = jnp.transpose(wide)[:SHARD_COLS, :]
            g_ref[...] = g
            d, nm, nv = _adamw_math(w_ref[...], g, m_ref[...], v_ref[...])
            d_ref[...] = d
            nm_ref[...] = nm
            nv_ref[...] = nv

        @pl.when(k == 0)
        def _():
            emit(sl)

        @pl.when(k == 1)
        def _():
            col = lax.broadcasted_iota(jnp.int32, (tr, SLAB), 1)
            before = pltpu.roll(sl, SLAB - SLAB_SHIFT[1], axis=1)
            after = pltpu.roll(sl, N_FORGET - SLAB_SHIFT[1], axis=1)
            wide_f = jnp.concatenate([f_tile, jnp.zeros((tr, SLAB - LANE), F32)], axis=1)
            forget = pltpu.roll(wide_f, FORGET_IN_SHARD, axis=1)
            emit(jnp.where(col < FORGET_IN_SHARD, before, jnp.where(col < FORGET_IN_SHARD + N_FORGET, forget, after)))

        for kk in (2, 3):
            @pl.when(k == kk)
            def _(kk=kk):
                emit(pltpu.roll(sl, SLAB - SLAB_SHIFT[kk], axis=1))

    nat = pl.BlockSpec((SHARD_COLS, tr), lambda i, k: (0, i))
    half = lambda width: pl.BlockSpec((tr, width), lambda i, k: (i % nbh, 0))
    return pl.pallas_call(
        body, grid_spec=pltpu.PrefetchScalarGridSpec(
            num_scalar_prefetch=1, grid=(rows // tr,),
            in_specs=[half(SLAB), half(SLAB), half(LANE), half(LANE), nat, nat, nat],
            out_specs=[nat] * 4),
        out_shape=[SDS((SHARD_COLS, rows), F32)] * 4, name="adamw_w_in",
        compiler_params=_params(("arbitrary",)))(chip_core, slab_mine, slab_theirs, forget_mine, forget_theirs, w, m, v)


ANY = pl.BlockSpec(memory_space=pl.ANY)
HALF_AXIS = (0, 0, 1, 0, 0, 0, 1)


def _me():
    return lax.axis_index("x"), lax.axis_index("y"), lax.axis_index("c")


def _half(ref, which, axis):
    n = ref.shape[axis] // 2
    sl = pl.ds(which * n, n)
    return ref.at[sl] if axis == 0 else ref.at[:, sl]


def _piece(t, ref, j):
    if t == 0:
        return ref.at[:, pl.ds(SLAB_START[j], SLAB)]
    if t == 1:
        return ref
    if t in (2, 6):
        return ref.at[pl.ds(512 * j, 512)]
    return ref.at[:, pl.ds(512 * j, 512)]


def _piece_shape(t, shape):
    if t == 0:
        return (shape[0], SLAB)
    if t == 1:
        return shape
    if t in (2, 6):
        return (512, shape[1])
    return (shape[0], 512)


def _gather_plan(ins, outs, own_slot_in_src):
    x, y, c = _me()
    k = 2 * x + y
    sib = (x, y, 1 - c)
    chips = [(1 - x, y), (x, 1 - y), (1 - x, 1 - y)]
    n = len(outs)

    def rows(t, which):
        h = outs[t].shape[1] // 2
        return pl.ds(which * h, h)

    def mine(t):
        return ins[t].at[k, rows(t, c)] if own_slot_in_src else ins[t].at[rows(t, c)]

    def first(t, j, sems):
        chip = chips[j]
        return pltpu.make_async_remote_copy(
            src_ref=mine(t), dst_ref=outs[t].at[k, rows(t, c)], send_sem=sems[0].at[t, j], recv_sem=sems[1].at[t, j],
            device_id=(chip[0], chip[1], c), device_id_type=MESH)

    def landed(t, j, sems):
        chip = chips[j]
        return pltpu.make_async_remote_copy(
            src_ref=mine(t), dst_ref=outs[t].at[2 * chip[0] + chip[1], rows(t, c)], send_sem=sems[0].at[t, j],
            recv_sem=sems[1].at[t, j], device_id=(chip[0], chip[1], c), device_id_type=MESH)

    def passed(t, j, which, sems):
        chip = chips[j]
        blk = outs[t].at[2 * chip[0] + chip[1], rows(t, which)]
        return pltpu.make_async_remote_copy(
            src_ref=blk, dst_ref=blk, send_sem=sems[2].at[t, j], recv_sem=sems[3].at[t, j], device_id=sib,
            device_id_type=MESH)

    def start(sems):
        for j in range(3):
            for t in range(n):
                first(t, j, sems).start()

    def finish(sems):
        for j in range(3):
            for t in range(n):
                landed(t, j, sems).wait_recv()
                passed(t, j, c, sems).start()
        for j in range(3):
            for t in range(n):
                passed(t, j, 1 - c, sems).wait_recv()
        for j in range(3):
            for t in range(n):
                first(t, j, sems).wait_send()
                passed(t, j, c, sems).wait_send()

    return k, start, finish


def _slab_gather_comm(slabs):
    def plan(in_ref, out_ref, nbr_sem, quarter_sem, pass_sem):
        x, y, c = _me()
        k = 2 * x + y
        rows = out_ref.shape[1]
        h, q = rows // 2, rows // 4
        nbrs = [(1 - x, y), (x, 1 - y)]
        slot = lambda chip: 2 * chip[0] + chip[1]
        diag = 2 * (1 - x) + (1 - y)
        half = pl.ds(c * h, h)
        quarter = lambda a: pl.ds(c * h + a * q, q)

        def first(a):
            return pltpu.make_async_remote_copy(
                src_ref=in_ref.at[k, half], dst_ref=out_ref.at[k, half], send_sem=nbr_sem.at[0, a],
                recv_sem=nbr_sem.at[1, a], device_id=(nbrs[a][0], nbrs[a][1], c), device_id_type=MESH)

        def landed(a):
            blk = out_ref.at[slot(nbrs[a]), half]
            return pltpu.make_async_remote_copy(
                src_ref=blk, dst_ref=blk, send_sem=nbr_sem.at[0, a], recv_sem=nbr_sem.at[1, a],
                device_id=(nbrs[a][0], nbrs[a][1], c), device_id_type=MESH)

        def relay(a):
            blk = out_ref.at[slot(nbrs[a]), quarter(a)]
            to = nbrs[1 - a]
            return pltpu.make_async_remote_copy(
                src_ref=blk, dst_ref=blk, send_sem=quarter_sem.at[0, a], recv_sem=quarter_sem.at[1, a],
                device_id=(to[0], to[1], c), device_id_type=MESH)

        def relayed(a):
            blk = out_ref.at[diag, quarter(a)]
            frm = nbrs[1 - a]
            return pltpu.make_async_remote_copy(
                src_ref=blk, dst_ref=blk, send_sem=quarter_sem.at[0, a], recv_sem=quarter_sem.at[1, a],
                device_id=(frm[0], frm[1], c), device_id_type=MESH)

        def passed(j, which):
            sl = diag if j == 2 else slot(nbrs[j])
            blk = out_ref.at[sl, pl.ds(which * h, h)]
            return pltpu.make_async_remote_copy(
                src_ref=blk, dst_ref=blk, send_sem=pass_sem.at[0, j], recv_sem=pass_sem.at[1, j],
                device_id=(x, y, 1 - c), device_id_type=MESH)

        def start():
            for a in range(2):
                first(a).start()

        def finish():
            for a in range(2):
                landed(a).wait_recv()
                relay(a).start()
                passed(a, c).start()
            for a in range(2):
                relayed(a).wait_recv()
            passed(2, c).start()
            for j in range(3):
                passed(j, 1 - c).wait_recv()
            for a in range(2):
                first(a).wait_send()
                relay(a).wait_send()
            for j in range(3):
                passed(j, c).wait_send()

        return start, finish

    return _Comm([slabs], [SDS(slabs.shape, slabs.dtype)],
                 [pltpu.SemaphoreType.DMA((2, 2)), pltpu.SemaphoreType.DMA((2, 2)), pltpu.SemaphoreType.DMA((2, 3))],
                 lambda ins, outs, sems: plan(ins[0], outs[0], *sems)[0](),
                 lambda ins, outs, sems: plan(ins[0], outs[0], *sems)[1]())


def _gather_comm(parts):
    n = len(parts)

    def start(ins, outs, sems):
        k, go, _ = _gather_plan(ins, outs, False)
        for t in range(n):
            pltpu.make_async_copy(ins[t], outs[t].at[k], sems[4].at[t]).start()
        go(sems)

    def finish(ins, outs, sems):
        k, _, done = _gather_plan(ins, outs, False)
        done(sems)
        for t in range(n):
            pltpu.make_async_copy(ins[t], outs[t].at[k], sems[4].at[t]).wait()

    return _Comm(parts, [SDS((N_CHIPS,) + p.shape, p.dtype) for p in parts],
                 [pltpu.SemaphoreType.DMA((n, 3))] * 4 + [pltpu.SemaphoreType.DMA((n,))], start, finish)


def _exchange_comm(arrs, kinds, whole=()):
    n = len(arrs)

    def copies(ins, outs, sems):
        x, y, c = _me()
        return [pltpu.make_async_remote_copy(
            src_ref=ins[t] if t in whole else _half(ins[t], 1 - c, HALF_AXIS[kinds[t]]), dst_ref=outs[t],
            send_sem=sems[0].at[t], recv_sem=sems[1].at[t], device_id=(x, y, 1 - c), device_id_type=MESH)
            for t in range(n)]

    def start(ins, outs, sems):
        for cp in copies(ins, outs, sems):
            cp.start()

    def finish(ins, outs, sems):
        for cp in copies(ins, outs, sems):
            cp.wait()

    def hshape(t):
        s = list(arrs[t].shape)
        if t not in whole:
            s[HALF_AXIS[kinds[t]]] //= 2
        return SDS(tuple(s), arrs[t].dtype)

    return _Comm(arrs, [hshape(t) for t in range(n)], [pltpu.SemaphoreType.DMA((n,))] * 2, start, finish)


def _add_half(full, got, core, axis, *, name):
    r, c = got.shape
    br, bc = (256 if r % 256 == 0 else 128), min(2048, c)
    off_r = (r // br) if axis == 0 else 0
    off_c = (c // bc) if axis == 1 else 0

    def body(c_ref, a_ref, b_ref, o_ref):
        o_ref[...] = (a_ref[...] + b_ref[...]).astype(BF16)

    return pl.pallas_call(
        body, grid_spec=pltpu.PrefetchScalarGridSpec(
            num_scalar_prefetch=1, grid=(r // br, c // bc),
            in_specs=[pl.BlockSpec((br, bc), lambda i, j, cr: (i + cr[0] * off_r, j + cr[0] * off_c)),
                      pl.BlockSpec((br, bc), lambda i, j, cr: (i, j))],
            out_specs=pl.BlockSpec((br, bc), lambda i, j, cr: (i, j))),
        out_shape=SDS((r, c), BF16), name=name, compiler_params=_params(("parallel", "parallel")))(core, full, got)


def _add_pair(a, b, *, name):
    r, c = a.shape
    br, bc = 256, min(2048, c)

    def body(a_ref, b_ref, o_ref):
        o_ref[...] = (a_ref[...] + b_ref[...]).astype(BF16)

    spec = pl.BlockSpec((br, bc), lambda i, j: (i, j))
    return pl.pallas_call(body, grid=(r // br, c // bc), in_specs=[spec, spec], out_specs=spec,
                          out_shape=SDS((r, c), BF16), name=name, compiler_params=_params(("parallel", "parallel")))(a, b)


def _scatter_comm(halves, kinds):
    n = len(halves)

    def plan(ins, outs, sems):
        send, recv, lsem = sems
        x, y, c = _me()
        k = 2 * x + y

        def to_chip(t, j):
            return pltpu.make_async_remote_copy(
                src_ref=_piece(kinds[t], ins[t], j), dst_ref=outs[t].at[k], send_sem=send.at[t, j],
                recv_sem=recv.at[t, k], device_id=(j // 2, j % 2, c), device_id_type=MESH)

        def from_chip(t, j):
            return pltpu.make_async_remote_copy(
                src_ref=_piece(kinds[t], ins[t], j), dst_ref=outs[t].at[j], send_sem=send.at[t, j],
                recv_sem=recv.at[t, j], device_id=(j // 2, j % 2, c), device_id_type=MESH)

        def own(t, j):
            return pltpu.make_async_copy(_piece(kinds[t], ins[t], j), outs[t].at[j], lsem.at[t])

        return k, to_chip, from_chip, own

    def start(ins, outs, sems):
        k, to_chip, _, own = plan(ins, outs, sems)
        for j in range(N_CHIPS):
            @pl.when(k != j)
            def _(j=j):
                for t in range(n):
                    to_chip(t, j).start()

            @pl.when(k == j)
            def _(j=j):
                for t in range(n):
                    own(t, j).start()

    def finish(ins, outs, sems):
        k, to_chip, from_chip, own = plan(ins, outs, sems)
        for j in range(N_CHIPS):
            @pl.when(k != j)
            def _(j=j):
                for t in range(n):
                    from_chip(t, j).wait_recv()
                for t in range(n):
                    to_chip(t, j).wait_send()

            @pl.when(k == j)
            def _(j=j):
                for t in range(n):
                    own(t, j).wait()

    return _Comm(halves, [SDS((N_CHIPS,) + _piece_shape(kinds[t], halves[t].shape), halves[t].dtype) for t in range(n)],
                 [pltpu.SemaphoreType.DMA((n, N_CHIPS))] * 2 + [pltpu.SemaphoreType.DMA((n,))], start, finish)


def _sum4(p, *, name):
    _, r, c = p.shape
    br = 256 if r % 256 == 0 else 128

    def body(p_ref, o_ref):
        o_ref[...] = ((p_ref[0].astype(F32) + p_ref[1].astype(F32)) + p_ref[2].astype(F32)) + p_ref[3].astype(F32)

    return pl.pallas_call(
        body, grid=(r // br,), in_specs=[pl.BlockSpec((N_CHIPS, br, c), lambda i: (0, i, 0))],
        out_specs=_rowblk(br, c), out_shape=SDS((r, c), F32), name=name, compiler_params=_params(("parallel",)))(p)


def _swap_comm(sums):
    return _exchange_comm(sums, [None] * len(sums), whole=tuple(range(len(sums))))


def _adamw_halves(mine, theirs, core, w, m, v, *, axis, tr, name):
    rows, cols = w.shape

    if axis == 0:
        nbh = rows // 2 // tr
        g_spec = pl.BlockSpec((tr, cols), lambda i, cr: (i % nbh, 0))
    else:
        g_spec = pl.BlockSpec((tr, cols // 2), lambda i, cr: (i, 0))

    def body(c_ref, a_ref, b_ref, w_ref, m_ref, v_ref, g_ref, d_ref, nm_ref, nv_ref):
        a, b = a_ref[...], b_ref[...]
        if axis == 0:
            g = jnp.where(pl.program_id(0) // nbh == c_ref[0], a, b)
        else:
            low = c_ref[0] == 0
            g = jnp.concatenate([jnp.where(low, a, b), jnp.where(low, b, a)], axis=1)
        g_ref[...] = g
        d, nm, nv = _adamw_math(w_ref[...], g, m_ref[...], v_ref[...])
        d_ref[...] = d
        nm_ref[...] = nm
        nv_ref[...] = nv

    nat = pl.BlockSpec((tr, cols), lambda i, cr: (i, 0))
    return pl.pallas_call(
        body, grid_spec=pltpu.PrefetchScalarGridSpec(
            num_scalar_prefetch=1, grid=(rows // tr,), in_specs=[g_spec, g_spec, nat, nat, nat], out_specs=[nat] * 4),
        out_shape=[SDS((rows, cols), F32)] * 4, name=name, compiler_params=_params(("arbitrary",)))(
            core, mine, theirs, w, m, v)


SMALL_ROWS, SMALL_COLS = 8, 1024


def _pack_small(vs):
    flat = jnp.concatenate([v.reshape(-1) for v in vs])
    return jnp.pad(flat, (0, SMALL_ROWS * SMALL_COLS - flat.shape[0])).reshape(SMALL_ROWS, SMALL_COLS)


def _unpack_small(packed, sizes):
    flat = packed.reshape(-1)
    out, o = [], 0
    for n in sizes:
        out.append(flat[o:o + n].reshape(1, n))
        o += n
    return out


def _all_reduce_small(v):
    n_dev = 8

    def body(v_ref, o_ref, land, send, recv):
        x, y, c = _me()
        me = 4 * x + 2 * y + c
        land[me] = v_ref[...]
        cps = []
        for r in range(1, n_dev):
            fx, fy, fc = (r >> 2) & 1, (r >> 1) & 1, r & 1
            peer = (x ^ fx, y ^ fy, c ^ fc)
            cps.append(pltpu.make_async_remote_copy(
                src_ref=v_ref, dst_ref=land.at[me], send_sem=send.at[r - 1], recv_sem=recv.at[r - 1],
                device_id=peer, device_id_type=MESH))
        for cp in cps:
            cp.start()
        for r in range(1, n_dev):
            fx, fy, fc = (r >> 2) & 1, (r >> 1) & 1, r & 1
            src = 4 * (x ^ fx) + 2 * (y ^ fy) + (c ^ fc)
            pltpu.make_async_remote_copy(
                src_ref=v_ref, dst_ref=land.at[src], send_sem=send.at[r - 1], recv_sem=recv.at[r - 1],
                device_id=(x ^ fx, y ^ fy, c ^ fc), device_id_type=MESH).wait_recv()
        for cp in cps:
            cp.wait_send()
        acc = land[0]
        for r in range(1, n_dev):
            acc = acc + land[r]
        o_ref[...] = acc

    vm = pl.BlockSpec(memory_space=pltpu.VMEM)
    return pl.pallas_call(
        body, in_specs=[vm], out_specs=vm, out_shape=SDS(v.shape, F32),
        scratch_shapes=[pltpu.VMEM((n_dev,) + v.shape, F32), pltpu.SemaphoreType.DMA((n_dev - 1,)),
                        pltpu.SemaphoreType.DMA((n_dev - 1,))],
        name="all_reduce_small")(v)


def kernel(x, mem, norm_gain, mem_norm_gain, w_in, b_forget, q_gain_a, k_gain_a, sinks_a, q_gain_b, k_gain_b, q_gain_c, k_gain_c, w_mem_kv, w_branch_a, w_branch_b, w_branch_c, w_out, loss_target, m_norm_gain, m_mem_norm_gain, m_w_in, m_b_forget, m_q_gain_a, m_k_gain_a, m_sinks_a, m_q_gain_b, m_k_gain_b, m_q_gain_c, m_k_gain_c, m_w_mem_kv, m_w_branch_a, m_w_branch_b, m_w_branch_c, m_w_out, v_norm_gain, v_mem_norm_gain, v_w_in, v_b_forget, v_q_gain_a, v_k_gain_a, v_sinks_a, v_q_gain_b, v_k_gain_b, v_q_gain_c, v_k_gain_c, v_w_mem_kv, v_w_branch_a, v_w_branch_b, v_w_branch_c, v_w_out):
    xi, yi, ci = lax.axis_index("x"), lax.axis_index("y"), lax.axis_index("c")
    chip = jnp.reshape(2 * xi + yi, (1,)).astype(jnp.int32)
    core = jnp.reshape(ci, (1,)).astype(jnp.int32)

    slabs = _pack_w_in(chip, jnp.transpose(w_in[0]))
    mine = [w_mem_kv[0].astype(BF16), w_branch_a[0].astype(BF16), w_branch_b[0].astype(BF16),
            w_branch_c[0].astype(BF16), w_out[0].astype(BF16)]
    hn, slabs = _rms_fwd(x[0], norm_gain, tr=512, name="rms_x", inplace_comm=_slab_gather_comm(slabs))
    w_main, w_fb = _merge_slabs(slabs)

    r = _local_step(x[0], mem[0], loss_target[0], w_main, w_fb, mine, norm_gain, mem_norm_gain,
                    b_forget, q_gain_a, k_gain_a, sinks_a, q_gain_b, k_gain_b, q_gain_c, k_gain_c, core=core, hn=hn)
    sums, theirs = r["sums"], r["theirs"]

    small_names = ["d_gain", "d_mem_gain", "d_bf", "d_qga", "d_kga", "d_sinks", "d_qgb", "d_kgb", "d_qgc", "d_kgc"]
    loss_part = (0.5 / D_MODEL) * jnp.sum(r["sq"], axis=1, keepdims=True)
    packed = _pack_small([r[n] for n in small_names] + [loss_part])
    red = _all_reduce_small(packed)
    small_w = [norm_gain, mem_norm_gain, b_forget, q_gain_a, k_gain_a, sinks_a, q_gain_b, k_gain_b, q_gain_c, k_gain_c]
    small_m = [m_norm_gain, m_mem_norm_gain, m_b_forget, m_q_gain_a, m_k_gain_a, m_sinks_a, m_q_gain_b, m_k_gain_b,
               m_q_gain_c, m_k_gain_c]
    small_v = [v_norm_gain, v_mem_norm_gain, v_b_forget, v_q_gain_a, v_k_gain_a, v_sinks_a, v_q_gain_b, v_k_gain_b,
               v_q_gain_c, v_k_gain_c]
    sizes = [w.shape[1] for w in small_w]
    s_d, s_m, s_v = _adamw(red, _pack_small(small_w), _pack_small(small_m), _pack_small(small_v), tr=8, name="adamw_small")
    g_small = _unpack_small(red, sizes + [1])
    loss = g_small[-1].reshape(())
    d_small, m_small, v_small = _unpack_small(s_d, sizes), _unpack_small(s_m, sizes), _unpack_small(s_v, sizes)

    gw_in, dw_in, mw_in, vw_in = _adamw_w_in(jnp.concatenate([chip, core]), sums[0], theirs[0], sums[1], theirs[1],
                                             jnp.transpose(w_in[0]), jnp.transpose(m_w_in[0]), jnp.transpose(v_w_in[0]))
    big = {}
    for t, nm, w, m, v in ((2, "w_mem_kv", w_mem_kv, m_w_mem_kv, v_w_mem_kv),
                           (3, "w_branch_a", w_branch_a, m_w_branch_a, v_w_branch_a),
                           (4, "w_branch_b", w_branch_b, m_w_branch_b, v_w_branch_b),
                           (5, "w_branch_c", w_branch_c, m_w_branch_c, v_w_branch_c),
                           (6, "w_out", w_out, m_w_out, v_w_out)):
        big[nm] = _adamw_halves(sums[t], theirs[t], core, w[0], m[0], v[0], axis=HALF_AXIS[t], tr=128,
                                name="adamw_" + nm)

    def collect(kind):
        sm = (g_small, d_small, m_small, v_small)[kind]
        win = (gw_in, dw_in, mw_in, vw_in)[kind]
        return ([sm[0], sm[1], jnp.transpose(win)[None]] + [a for a in sm[2:10]]
                + [big[n][kind][None] for n in ("w_mem_kv", "w_branch_a", "w_branch_b", "w_branch_c", "w_out")])

    return (loss, r["grad_x"][None], *collect(0), *collect(1), *collect(2), *collect(3))
```

```python
import functools

import numpy as np
import jax
import jax.numpy as jnp
from jax import lax
from jax.experimental import pallas as pl
from jax.experimental.pallas import tpu as pltpu

F32 = jnp.float32
BF16 = jnp.bfloat16
HI = lax.Precision.HIGHEST
SDS = jax.ShapeDtypeStruct
MESH = pl.DeviceIdType.MESH

D_MODEL = 2048
HEAD_DIM = 64
A_HEADS = 12
A_GROUP = 3
B_HEADS = 12
C_HEADS = 4
C_HEAD_DIM = 128
WINDOW = 128
EPS = 1e-6
NEG = -1e30
LANE = 128

QA, KA, VA, ZA = 0, 768, 1024, 1280
QB, KB, VB, ZB = 2048, 2816, 3584, 4352
QC, ZC = 5120, 5632
GATE = 6144
P_MAIN = 12288
N_FORGET = 12
FORGET_COL = 5120
SHARD_COLS = 3075
SLAB = 3200
SLAB_START = (0, 3072, 6016, 9088)
SLAB_SHIFT = (0, 3, 122, 125)
N_CHIPS = 4

ADAM_LR = 0.001
ADAM_B1 = 0.9
ADAM_B2 = 0.999
ADAM_EPS = 1e-08
ADAM_WD = 0.01
ADAM_STEP = 10

VMEM_LIMIT = 56 * 1024 * 1024
VMEM_WIDE = 62 * 1024 * 1024


def _params(sem, vmem=VMEM_LIMIT):
    return pltpu.CompilerParams(dimension_semantics=sem, vmem_limit_bytes=vmem)


def _win(tr, width, off):
    return pl.BlockSpec((pl.Element(tr), pl.Element(width)), lambda i, *_: (i * tr, off))


def _rowblk(tr, width):
    return pl.BlockSpec((tr, width), lambda i, *_: (i, 0))


def _const(shape):
    nd = len(shape)
    return pl.BlockSpec(shape, lambda *_: (0,) * nd)


def _rms(x, g):
    return x * lax.rsqrt(jnp.mean(x * x, axis=-1, keepdims=True) + EPS) * g


def _head_mean_impl(x2, bd):
    hi = x2.astype(BF16)
    lo = (x2 - hi.astype(F32)).astype(BF16)
    return _dot(hi, bd) + _dot(lo, bd)


@jax.custom_vjp
def _head_mean(x2, bd):
    return _head_mean_impl(x2, bd)


_head_mean.defvjp(lambda x2, bd: (_head_mean_impl(x2, bd), bd),
                  lambda bd, g: (_head_mean_impl(g, bd), jnp.zeros_like(bd)))


def _head_norm(x, g_tiled, bd):
    return x * lax.rsqrt(_head_mean(x * x, bd) + EPS) * g_tiled


def _silu(z):
    return z * jax.nn.sigmoid(z)


def _dot_nt(a, b):
    return lax.dot_general(a, b, (((1,), (1,)), ((), ())), preferred_element_type=F32)


def _dot_tn(a, b):
    return lax.dot_general(a, b, (((0,), (0,)), ((), ())), preferred_element_type=F32)


def _dot(a, b):
    return jnp.dot(a, b, preferred_element_type=F32)


def _swa_fn(qk, vz, qkp, vzp, qg, kg, sinks, bd, bias, first):
    q = _head_norm(qk[:, :768], qg, bd)
    k2 = jnp.concatenate([qkp[:, 768:], qk[:, 768:]], axis=0)
    k2 = _head_norm(k2, kg, bd[:256, :256])
    v2 = jnp.concatenate([vzp[:, :256], vz[:, :256]], axis=0)
    z = vz[:, 256:]
    cols = A_GROUP * WINDOW
    kj = lax.broadcasted_iota(jnp.int32, (2 * WINDOW, cols), 0)
    no_prev = kj < WINDOW * first.astype(jnp.int32)
    qtb = jnp.transpose(q).astype(BF16)
    kb = k2.astype(BF16)
    vtb = jnp.transpose(v2).astype(BF16)
    outs = [None] * A_HEADS
    for g in range(A_HEADS // A_GROUP):
        heads = [A_GROUP * g + u for u in range(A_GROUP)]
        qs = jnp.concatenate([qtb[64 * h:64 * h + 64, :] for h in heads], axis=1)
        s = _dot(kb[:, 64 * g:64 * g + 64], qs) * (HEAD_DIM ** -0.5) + bias[g]
        s = jnp.where(no_prev, NEG, s)
        sink = jnp.concatenate([jnp.broadcast_to(sinks[:, h:h + 1], (1, WINDOW)) for h in heads], axis=1)
        m = lax.stop_gradient(jnp.maximum(jnp.max(s, axis=0, keepdims=True), sink))
        p = jnp.exp(s - m)
        den = jnp.sum(p, axis=0, keepdims=True) + jnp.exp(sink - m)
        o = _dot(vtb[64 * g:64 * g + 64, :], (p * (1.0 / den)).astype(BF16))
        for u, h in enumerate(heads):
            outs[h] = o[:, WINDOW * u:WINDOW * u + WINDOW]
    return jnp.transpose(jnp.concatenate(outs, axis=0)) * _silu(z)


def _swa_bias():
    qi = np.arange(WINDOW)[None, :]
    kj = np.arange(2 * WINDOW)[:, None]
    rel = qi + WINDOW - kj
    valid = (rel >= 0) & (rel < WINDOW)
    out = np.zeros((A_HEADS // A_GROUP, 2 * WINDOW, A_GROUP * WINDOW), np.float32)
    for h in range(A_HEADS):
        slope = np.float32(2.0 ** (-8.0 * (h + 1) / A_HEADS))
        blk = np.where(valid, -slope * rel.astype(np.float32), np.float32(NEG))
        g, u = divmod(h, A_GROUP)
        out[g, :, WINDOW * u:WINDOW * u + WINDOW] = blk
    return jnp.asarray(out)


def _mem_fn(qz, mkv, qg, kg, bd):
    q = _head_norm(qz[:, :512], qg, bd).astype(BF16)
    k = _head_norm(mkv[:, :512], kg, bd).astype(BF16)
    v = mkv[:, 512:].astype(BF16)
    z = qz[:, 512:]
    outs = []
    for h in range(C_HEADS):
        sl = slice(128 * h, 128 * h + 128)
        s = _dot_nt(q[:, sl], k[:, sl]) * (C_HEAD_DIM ** -0.5)
        m = lax.stop_gradient(jnp.max(s, axis=-1, keepdims=True))
        p = jnp.exp(s - m)
        den = jnp.sum(p, axis=-1, keepdims=True)
        outs.append(_dot((p * (1.0 / den)).astype(BF16), v[:, sl]))
    return jnp.concatenate(outs, axis=1) * _silu(z)


def _qn_fn(q, g, bd):
    return _head_norm(q, g, bd) * (HEAD_DIM ** -0.5)


def _kn_fn(k, g, bd):
    return _head_norm(k, g, bd)


def _block_diag(width, hd):
    i = np.arange(width) // hd
    return jnp.asarray((i[:, None] == i[None, :]).astype(np.float32) / hd, BF16)


def _head_sum(width, hd):
    i = np.arange(width) // hd
    return jnp.asarray((i[:, None] == np.arange(LANE)[None, :]).astype(np.float32))


def _rms_fwd(x, g, *, tr, name):
    rows, dm = x.shape

    def body(x_ref, g_ref, o_ref):
        o_ref[...] = _rms(x_ref[...], g_ref[...]).astype(BF16)

    return pl.pallas_call(
        body, grid=(rows // tr,),
        in_specs=[_rowblk(tr, dm), _const((1, dm))],
        out_specs=_rowblk(tr, dm),
        out_shape=SDS((rows, dm), BF16), name=name,
        compiler_params=_params(("parallel",)))(x, g)


def _rms_bwd(x, g, dy, resid, *, tr, name, comm=None):
    rows, dm = x.shape
    want_dx = resid is not None
    n_in = 4 if want_dx else 3
    n_out = 2 if want_dx else 1
    c_in = len(comm.ins) if comm else 0
    c_out = len(comm.out_shapes) if comm else 0
    nb = rows // tr

    def body(*refs):
        x_ref, g_ref, dy_ref = refs[:3]
        r_ref = refs[3] if want_dx else None
        cin = refs[n_in:n_in + c_in]
        outs = refs[n_in + c_in:n_in + c_in + n_out]
        dg_ref = outs[-1]
        cout = refs[n_in + c_in + n_out:n_in + c_in + n_out + c_out]
        csem = refs[n_in + c_in + n_out + c_out:]

        if comm:
            @pl.when(pl.program_id(0) == 0)
            def _():
                comm.start(cin, cout, csem)

        _, vjp = jax.vjp(_rms, x_ref[...], g_ref[...])
        dx, dg = vjp(dy_ref[...])

        @pl.when(pl.program_id(0) == 0)
        def _():
            dg_ref[...] = jnp.zeros_like(dg_ref)

        dg_ref[...] += dg
        if want_dx:
            outs[0][...] = r_ref[...] + dx

        if comm:
            @pl.when(pl.program_id(0) == nb - 1)
            def _():
                comm.finish(cin, cout, csem)

    hbm = pl.BlockSpec(memory_space=pl.ANY)
    ins = [x, g, dy] + ([resid] if want_dx else []) + (list(comm.ins) if comm else [])
    in_specs = ([_rowblk(tr, dm), _const((1, dm)), _rowblk(tr, dm)] + ([_rowblk(tr, dm)] if want_dx else [])
                + [hbm] * c_in)
    out_specs = ([_rowblk(tr, dm)] if want_dx else []) + [_const((1, dm))] + [hbm] * c_out
    out_shape = (([SDS((rows, dm), F32)] if want_dx else []) + [SDS((1, dm), F32)]
                 + (list(comm.out_shapes) if comm else []))
    res = pl.pallas_call(
        body, grid=(nb,), in_specs=in_specs, out_specs=out_specs, out_shape=out_shape,
        scratch_shapes=list(comm.sems) if comm else [], name=name, compiler_params=_params(("arbitrary",)))(*ins)
    return (list(res[:n_out]), list(res[n_out:])) if comm else res


class _Comm:
    def __init__(self, ins, out_shapes, sems, start, finish):
        self.ins, self.out_shapes, self.sems, self.start, self.finish = list(ins), list(out_shapes), list(sems), start, finish


def _matmul(a, b, *, dims, out_dtype, tm, tn, tk, name, add=None, comms=(), vmem=VMEM_LIMIT):
    a_list = list(a) if isinstance(a, (list, tuple)) else [a]
    b_list = list(b) if isinstance(b, (list, tuple)) else [b]
    assert len(a_list) == 1 or dims == "nt"
    assert len(b_list) == 1 or dims == "tn"
    if dims == "tn":
        kdim, m = a_list[0].shape
    else:
        m, kdim = a_list[0].shape[0], sum(p.shape[1] for p in a_list)
    n = b_list[0].shape[0] if dims == "nt" else sum(p.shape[1] for p in b_list)
    tm, tn, tk = min(tm, m), min(tn, n), min(tk, kdim)
    assert m % tm == 0 and n % tn == 0 and kdim % tk == 0, (name, m, n, kdim)
    ni, nj, nk = m // tm, n // tn, kdim // tk
    a_rng, b_rng, pos = [], [], 0
    for p in a_list:
        assert len(a_list) == 1 or p.shape[1] % tk == 0
        a_rng.append((pos, p.shape[1] // tk if len(a_list) > 1 else nk))
        pos += a_rng[-1][1]
    pos = 0
    for p in b_list:
        assert len(b_list) == 1 or p.shape[1] % tn == 0
        b_rng.append((pos, p.shape[1] // tn if len(b_list) > 1 else nj))
        pos += b_rng[-1][1]
    has_add = add is not None
    n_mm_in = len(a_list) + len(b_list) + (1 if has_add else 0)
    c_in = [len(c.ins) for c in comms]
    c_out = [len(c.out_shapes) for c in comms]
    c_sem = [len(c.sems) for c in comms]

    def body(*refs):
        a_refs, b_refs = refs[:len(a_list)], refs[len(a_list):len(a_list) + len(b_list)]
        add_ref = refs[n_mm_in - 1] if has_add else None
        pos = n_mm_in
        cin = []
        for cnt in c_in:
            cin.append(refs[pos:pos + cnt])
            pos += cnt
        o_ref = refs[pos]
        pos += 1
        cout = []
        for cnt in c_out:
            cout.append(refs[pos:pos + cnt])
            pos += cnt
        acc = refs[pos]
        pos += 1
        csem = []
        for cnt in c_sem:
            csem.append(refs[pos:pos + cnt])
            pos += cnt
        i, j, k = pl.program_id(0), pl.program_id(1), pl.program_id(2)

        if comms:
            @pl.when((i == 0) & (j == 0) & (k == 0))
            def _():
                for c, ci, co, cs in zip(comms, cin, cout, csem):
                    c.start(ci, co, cs)

        def accumulate(a_ref, b_ref, first_k, later_k):
            if dims == "nn":
                part = _dot(a_ref[...], b_ref[...])
            elif dims == "nt":
                part = _dot_nt(a_ref[...], b_ref[...])
            else:
                part = _dot_tn(a_ref[...], b_ref[...])

            if first_k:
                @pl.when(k == 0)
                def _():
                    acc[...] = part + add_ref[...] if has_add else part

            if later_k:
                @pl.when(k > 0)
                def _():
                    acc[...] += part

        if len(a_list) > 1:
            for a_ref, (k0, cnt) in zip(a_refs, a_rng):
                @pl.when((k >= k0) & (k < k0 + cnt))
                def _(a_ref=a_ref, k0=k0, cnt=cnt):
                    accumulate(a_ref, b_refs[0], k0 == 0, k0 + cnt > 1)
        elif len(b_list) > 1:
            for b_ref, (j0, cnt) in zip(b_refs, b_rng):
                @pl.when((j >= j0) & (j < j0 + cnt))
                def _(b_ref=b_ref):
                    accumulate(a_refs[0], b_ref, True, nk > 1)
        else:
            accumulate(a_refs[0], b_refs[0], True, nk > 1)

        @pl.when(k == nk - 1)
        def _():
            o_ref[...] = acc[...].astype(out_dtype)

        if comms:
            @pl.when((i == ni - 1) & (j == nj - 1) & (k == nk - 1))
            def _():
                for c, ci, co, cs in zip(comms, cin, cout, csem):
                    c.finish(ci, co, cs)

    def a_spec(k0, cnt):
        if dims == "tn":
            return pl.BlockSpec((tk, tm), lambda i, j, k: (k, i))
        return pl.BlockSpec((tm, tk), lambda i, j, k: (i, jnp.clip(k - k0, 0, cnt - 1)))

    def b_spec(j0, cnt):
        if dims == "nt":
            return pl.BlockSpec((tn, tk), lambda i, j, k: (j, k))
        return pl.BlockSpec((tk, tn), lambda i, j, k: (k, jnp.clip(j - j0, 0, cnt - 1)))

    o_spec = pl.BlockSpec((tm, tn), lambda i, j, k: (i, j))
    hbm = pl.BlockSpec(memory_space=pl.ANY)
    ins = a_list + b_list + ([add] if has_add else []) + [x for c in comms for x in c.ins]
    in_specs = ([a_spec(*r) for r in a_rng] + [b_spec(*r) for r in b_rng] + ([o_spec] if has_add else [])
                + [hbm] * sum(c_in))
    out_specs = [o_spec] + [hbm] * sum(c_out)
    out_shape = [SDS((m, n), out_dtype)] + [s for c in comms for s in c.out_shapes]
    scratch = [pltpu.VMEM((tm, tn), F32)] + [s for c in comms for s in c.sems]
    sem = ("arbitrary",) * 3 if comms else ("parallel", "parallel", "arbitrary")
    res = pl.pallas_call(
        body, grid=(ni, nj, nk), in_specs=in_specs, out_specs=out_specs, out_shape=out_shape, scratch_shapes=scratch,
        name=name, compiler_params=_params(sem, vmem))(*ins)
    if not comms:
        return res[0]
    outs, pos = [], 1
    for cnt in c_out:
        outs.append(list(res[pos:pos + cnt]))
        pos += cnt
    return res[0], outs


def _swa_specs(nb):
    prev = lambda off: pl.BlockSpec((pl.Element(WINDOW), pl.Element(1024)),
                                    lambda n: (jnp.maximum(n - 1, 0) * WINDOW, off))
    return [_win(WINDOW, 1024, QA), _win(WINDOW, 1024, VA), prev(QA), prev(VA),
            _const((1, 768)), _const((1, 256)), _const((1, LANE)), _const((768, 768)),
            _const((A_HEADS // A_GROUP, 2 * WINDOW, A_GROUP * WINDOW))]


def _swa_fwd(proj, qg, kg, sinks, bd, bias):
    s = proj.shape[0]
    nb = s // WINDOW

    def body(qk_ref, vz_ref, qkp_ref, vzp_ref, qg_ref, kg_ref, sk_ref, bd_ref, bias_ref, o_ref):
        first = pl.program_id(0) == 0
        o_ref[...] = _swa_fn(qk_ref[...], vz_ref[...], qkp_ref[...], vzp_ref[...], qg_ref[...], kg_ref[...],
                             sk_ref[...], bd_ref[...], bias_ref[...], first).astype(BF16)

    return pl.pallas_call(
        body, grid=(nb,), in_specs=_swa_specs(nb), out_specs=_rowblk(WINDOW, 768),
        out_shape=SDS((s, 768), BF16), name="swa_fwd",
        compiler_params=_params(("parallel",)))(proj, proj, proj, proj, qg, kg, sinks, bd, bias)


def _swa_bwd(proj, qg, kg, sinks, bd, bias, dga):
    s = proj.shape[0]
    nb = s // WINDOW

    def body(qk_ref, vz_ref, qkp_ref, vzp_ref, qg_ref, kg_ref, sk_ref, bd_ref, bias_ref, dg_ref,
             dcur_ref, dprev_ref, dqg_ref, dkg_ref, dsk_ref):
        first = pl.program_id(0) == 0
        bd_v = bd_ref[...]
        bias_v = bias_ref[...]
        fn = lambda qk, vz, qkp, vzp, qg_, kg_, sk: _swa_fn(qk, vz, qkp, vzp, qg_, kg_, sk, bd_v, bias_v, first)
        _, vjp = jax.vjp(fn, qk_ref[...], vz_ref[...], qkp_ref[...], vzp_ref[...], qg_ref[...], kg_ref[...], sk_ref[...])
        dqk, dvz, dqkp, dvzp, dqg, dkg, dsk = vjp(dg_ref[...])

        @pl.when(first)
        def _():
            dqg_ref[...] = jnp.zeros_like(dqg_ref)
            dkg_ref[...] = jnp.zeros_like(dkg_ref)
            dsk_ref[...] = jnp.zeros_like(dsk_ref)

        dqg_ref[...] += dqg
        dkg_ref[...] += dkg
        dsk_ref[...] += dsk
        dcur_ref[...] = jnp.concatenate([dqk, dvz], axis=1)
        dprev_ref[...] = jnp.concatenate([dqkp[:, 768:], dvzp[:, :256]], axis=1)

    return pl.pallas_call(
        body, grid=(nb,), in_specs=_swa_specs(nb) + [_rowblk(WINDOW, 768)],
        out_specs=[_rowblk(WINDOW, 2048), pl.BlockSpec((None, WINDOW, 512), lambda n: (n, 0, 0)),
                   _const((1, 768)), _const((1, 256)), _const((1, LANE))],
        out_shape=[SDS((s, 2048), F32), SDS((nb, WINDOW, 512), F32), SDS((1, 768), F32), SDS((1, 256), F32),
                   SDS((1, LANE), F32)],
        name="swa_bwd", compiler_params=_params(("arbitrary",)))(proj, proj, proj, proj, qg, kg, sinks, bd, bias, dga)


def _swa_combine(dcur, dprev):
    s = dcur.shape[0]
    nb = s // WINDOW

    def body(c_ref, p_ref, o_ref):
        c = c_ref[...]
        nxt = jnp.where(pl.program_id(0) == nb - 1, 0.0, p_ref[...])
        o_ref[...] = jnp.concatenate([c[:, :768], c[:, 768:1280] + nxt, c[:, 1280:]], axis=1).astype(BF16)

    return pl.pallas_call(
        body, grid=(nb,),
        in_specs=[_rowblk(WINDOW, 2048), pl.BlockSpec((None, WINDOW, 512), lambda n: (jnp.minimum(n + 1, nb - 1), 0, 0))],
        out_specs=_rowblk(WINDOW, 2048), out_shape=SDS((s, 2048), BF16), name="swa_combine",
        compiler_params=_params(("parallel",)))(dcur, dprev)


def _mem_fwd(proj, mkv, qg, kg, bd, *, tr):
    s = proj.shape[0]

    def body(qz_ref, mkv_ref, qg_ref, kg_ref, bd_ref, o_ref):
        o_ref[...] = _mem_fn(qz_ref[...], mkv_ref[...], qg_ref[...], kg_ref[...], bd_ref[...]).astype(BF16)

    return pl.pallas_call(
        body, grid=(s // tr,),
        in_specs=[_win(tr, 1024, QC), _const(mkv.shape), _const((1, 512)), _const((1, 512)), _const((512, 512))],
        out_specs=_rowblk(tr, 512), out_shape=SDS((s, 512), BF16), name="mem_fwd",
        compiler_params=_params(("parallel",)))(proj, mkv, qg, kg, bd)


def _mem_bwd(proj, mkv, qg, kg, bd, dgc, *, tr):
    s = proj.shape[0]

    def body(qz_ref, mkv_ref, qg_ref, kg_ref, bd_ref, dg_ref, dqz_ref, dmkv_ref, dqg_ref, dkg_ref):
        bd_v = bd_ref[...]
        fn = lambda qz, mkv_, qg_, kg_: _mem_fn(qz, mkv_, qg_, kg_, bd_v)
        _, vjp = jax.vjp(fn, qz_ref[...], mkv_ref[...], qg_ref[...], kg_ref[...])
        dqz, dmkv, dqg, dkg = vjp(dg_ref[...])

        @pl.when(pl.program_id(0) == 0)
        def _():
            dmkv_ref[...] = jnp.zeros_like(dmkv_ref)
            dqg_ref[...] = jnp.zeros_like(dqg_ref)
            dkg_ref[...] = jnp.zeros_like(dkg_ref)

        dmkv_ref[...] += dmkv
        dqg_ref[...] += dqg
        dkg_ref[...] += dkg
        dqz_ref[...] = dqz.astype(BF16)

    return pl.pallas_call(
        body, grid=(s // tr,),
        in_specs=[_win(tr, 1024, QC), _const(mkv.shape), _const((1, 512)), _const((1, 512)), _const((512, 512)),
                  _rowblk(tr, 512)],
        out_specs=[_rowblk(tr, 1024), _const(mkv.shape), _const((1, 512)), _const((1, 512))],
        out_shape=[SDS((s, 1024), BF16), SDS(mkv.shape, F32), SDS((1, 512), F32), SDS((1, 512), F32)],
        name="mem_bwd", compiler_params=_params(("arbitrary",)))(proj, mkv, qg, kg, bd, dgc)


def _log_sigmoid(x):
    return jnp.minimum(x, 0.0) - jnp.log1p(jnp.exp(-jnp.abs(x)))


FOX_TQ, FOX_TK = 512, 512
FOX_FWD_TQ, FOX_FWD_TK = 512, 1024


def _fox_tiles(s):
    return min(FOX_TQ, s), min(FOX_TK, s)


AUG = 128 * B_HEADS
COL_A, COL_B = 64, 67


def _split3(c):
    hi = c.astype(BF16)
    r1 = c - hi.astype(F32)
    mid = r1.astype(BF16)
    lo = (r1 - mid.astype(F32)).astype(BF16)
    return hi, mid, lo


def _expand_mats():
    def mat(col0):
        e = np.zeros((768 + 3 * LANE, AUG), np.float32)
        for h in range(B_HEADS):
            for d in range(HEAD_DIM):
                e[64 * h + d, 128 * h + d] = 1.0
            for part in range(3):
                e[768 + LANE * part + h, 128 * h + col0 + part] = 1.0
        return e

    def ones(col0):
        o = np.zeros((1, AUG), np.float32)
        for h in range(B_HEADS):
            o[0, 128 * h + col0:128 * h + col0 + 3] = 1.0
        return o

    return (jnp.asarray(mat(COL_A), BF16), jnp.asarray(mat(COL_B), BF16), jnp.asarray(ones(COL_A)), jnp.asarray(ones(COL_B)))


def _augment(data_bf16, triple, emat, ones_row):
    parts = [data_bf16] + (list(triple) if triple is not None else [jnp.zeros((data_bf16.shape[0], LANE), BF16)] * 3)
    wide = _dot(jnp.concatenate(parts, axis=1), emat)
    if ones_row is not None:
        wide = wide + ones_row
    return wide


def _compact(wide):
    return jnp.concatenate([wide[:, 128 * h:128 * h + 64] for h in range(wide.shape[1] // 128)], axis=1)


def _lane_of_heads(wide, col, first=0):
    rows = wide.shape[0]
    lane = lax.broadcasted_iota(jnp.int32, (rows, LANE), 1)
    out = jnp.zeros((rows, LANE), F32)
    for h in range(wide.shape[1] // 128):
        out = jnp.where(lane == first + h, wide[:, 128 * h + col:128 * h + col + 1], out)
    return out


def _fox2_prep(proj, fbl, qg, kg, bfor, bd, ea, eb, ones_a, ones_b, *, tr):
    s = proj.shape[0]
    tri = jnp.asarray(np.tril(np.ones((tr, tr), np.float32)))

    def body(q_ref, k_ref, v_ref, fb_ref, qg_ref, kg_ref, bf_ref, bd_ref, tri_ref, ea_ref, eb_ref, oa_ref, ob_ref,
             qat_ref, ka_ref, kat_ref, va_ref, vat_ref, qn_ref, c_ref, carry):
        @pl.when(pl.program_id(0) == 0)
        def _():
            carry[...] = jnp.zeros_like(carry)

        bd_v = bd_ref[...]
        lane = lax.broadcasted_iota(jnp.int32, (tr, LANE), 1)
        logf = jnp.where(lane < N_FORGET, _log_sigmoid(fb_ref[...] + bf_ref[...]), 0.0)
        c = jnp.dot(tri_ref[...], logf, precision=HI, preferred_element_type=F32) + carry[...]
        c_ref[...] = c
        carry[...] = c[tr - 1:tr, :]
        qn = _qn_fn(q_ref[...], qg_ref[...], bd_v).astype(BF16)
        kn = _kn_fn(k_ref[...], kg_ref[...], bd_v).astype(BF16)
        qn_ref[...] = qn
        qat_ref[...] = jnp.transpose(_augment(qn, _split3(c), ea_ref[...], ob_ref[...])).astype(BF16)
        ka = _augment(kn, _split3(-c), eb_ref[...], oa_ref[...])
        ka_ref[...] = ka.astype(BF16)
        kat_ref[...] = jnp.transpose(ka).astype(BF16)
        va = _augment(v_ref[...].astype(BF16), None, ea_ref[...], oa_ref[...])
        va_ref[...] = va.astype(BF16)
        vat_ref[...] = jnp.transpose(va).astype(BF16)

    emat = _const((768 + 3 * LANE, AUG))
    return pl.pallas_call(
        body, grid=(s // tr,),
        in_specs=[_win(tr, 768, QB), _win(tr, 768, KB), _win(tr, 768, VB), _rowblk(tr, LANE), _const((1, 768)),
                  _const((1, 768)), _const((1, LANE)), _const((768, 768)), _const((tr, tr)), emat, emat,
                  _const((1, AUG)), _const((1, AUG))],
        out_specs=[pl.BlockSpec((AUG, tr), lambda i: (0, i)), _rowblk(tr, AUG), pl.BlockSpec((AUG, tr), lambda i: (0, i)),
                   _rowblk(tr, AUG), pl.BlockSpec((AUG, tr), lambda i: (0, i)), _rowblk(tr, 768), _rowblk(tr, LANE)],
        out_shape=[SDS((AUG, s), BF16), SDS((s, AUG), BF16), SDS((AUG, s), BF16), SDS((s, AUG), BF16),
                   SDS((AUG, s), BF16), SDS((s, 768), BF16), SDS((s, LANE), F32)],
        scratch_shapes=[pltpu.VMEM((1, LANE), F32)], name="fox_prep",
        compiler_params=_params(("arbitrary",)))(proj, proj, proj, fbl, qg, kg, bfor, bd, tri, ea, eb, ones_a, ones_b)


def _fox2_fwd(proj, qat, ka, vat):
    s = proj.shape[0]
    tq, tk = min(FOX_FWD_TQ, s), min(FOX_FWD_TK, s)
    nq, nk = s // tq, s // tk

    def last_k(i):
        return (i * tq + tq - 1) // tk

    def body(qt_ref, k_ref, vt_ref, z_ref, gb_ref, yb_ref, lse_ref, acc, m_s):
        i, j = pl.program_id(0), pl.program_id(1)

        @pl.when(j == 0)
        def _():
            acc[...] = jnp.zeros_like(acc)
            m_s[...] = jnp.full_like(m_s, NEG)

        def tile(masked):
            if masked:
                kpos = j * tk + lax.broadcasted_iota(jnp.int32, (tk, tq), 0)
                qpos = i * tq + lax.broadcasted_iota(jnp.int32, (tk, tq), 1)
                mask = kpos <= qpos
            for h in range(B_HEADS):
                sl = slice(128 * h, 128 * h + 128)
                sc = _dot(k_ref[:, sl], qt_ref[sl, :])
                if masked:
                    sc = jnp.where(mask, sc, NEG)
                m_prev = m_s[h:h + 1, :]
                m_new = jnp.maximum(m_prev, jnp.max(sc, axis=0, keepdims=True))
                p = jnp.exp(sc - m_new).astype(BF16)
                acc[sl, :] = jnp.exp(m_prev - m_new) * acc[sl, :] + _dot(vt_ref[sl, :], p)
                m_s[h:h + 1, :] = m_new

        full = j * tk + tk - 1 <= i * tq

        @pl.when(full)
        def _():
            tile(False)

        @pl.when(jnp.logical_and(jnp.logical_not(full), j <= last_k(i)))
        def _():
            tile(True)

        @pl.when(j == nk - 1)
        def _():
            outs = []
            row = lax.broadcasted_iota(jnp.int32, (LANE, tq), 0)
            lse_t = jnp.zeros((LANE, tq), F32)
            for h in range(B_HEADS):
                l_row = acc[128 * h + COL_A:128 * h + COL_A + 1, :]
                outs.append(acc[128 * h:128 * h + 64, :] * (1.0 / l_row))
                lse_t = jnp.where(row == h, m_s[h:h + 1, :] + jnp.log(l_row), lse_t)
            y = jnp.transpose(jnp.concatenate(outs, axis=0))
            yb_ref[...] = y
            gb_ref[...] = (y * _silu(z_ref[...])).astype(BF16)
            lse_ref[...] = jnp.transpose(lse_t)

    kcol = lambda i, j: (0, jnp.minimum(j, last_k(i)))
    return pl.pallas_call(
        body, grid=(nq, nk),
        in_specs=[pl.BlockSpec((AUG, tq), lambda i, j: (0, i)),
                  pl.BlockSpec((tk, AUG), lambda i, j: (jnp.minimum(j, last_k(i)), 0)),
                  pl.BlockSpec((AUG, tk), kcol),
                  pl.BlockSpec((pl.Element(tq), pl.Element(768)), lambda i, j: (i * tq, ZB))],
        out_specs=[pl.BlockSpec((tq, 768), lambda i, j: (i, 0)), pl.BlockSpec((tq, 768), lambda i, j: (i, 0)),
                   pl.BlockSpec((tq, LANE), lambda i, j: (i, 0))],
        out_shape=[SDS((s, 768), BF16), SDS((s, 768), F32), SDS((s, LANE), F32)],
        scratch_shapes=[pltpu.VMEM((AUG, tq), F32), pltpu.VMEM((16, tq), F32)],
        name="fox_fwd", compiler_params=_params(("parallel", "arbitrary")))(qat, ka, vat, proj)


def _fox2_bwd_pre(proj, yb, dgb, qn, c, lse, hsum, ea, ones_b, *, tr):
    s = proj.shape[0]

    def body(z_ref, y_ref, dg_ref, qn_ref, c_ref, lse_ref, hs_ref, ea_ref, ob_ref,
             qa_ref, qat_ref, dya_ref, dyat_ref, dz_ref):
        z, y, dg = z_ref[...], y_ref[...], dg_ref[...]
        sg = jax.nn.sigmoid(z)
        dy = dg * (z * sg)
        dz_ref[...] = (dg * y * (sg * (1.0 + z * (1.0 - sg)))).astype(BF16)
        delta = jnp.dot(dy * y, hs_ref[...], precision=HI, preferred_element_type=F32)
        e = ea_ref[...]
        dya = _augment(dy.astype(BF16), _split3(-delta), e, None)
        dya_ref[...] = dya.astype(BF16)
        dyat_ref[...] = jnp.transpose(dya).astype(BF16)
        qa = _augment(qn_ref[...], _split3(c_ref[...] - lse_ref[...]), e, ob_ref[...])
        qa_ref[...] = qa.astype(BF16)
        qat_ref[...] = jnp.transpose(qa).astype(BF16)

    return pl.pallas_call(
        body, grid=(s // tr,),
        in_specs=[_win(tr, 768, ZB), _rowblk(tr, 768), _rowblk(tr, 768), _rowblk(tr, 768), _rowblk(tr, LANE),
                  _rowblk(tr, LANE), _const((768, LANE)), _const((768 + 3 * LANE, AUG)), _const((1, AUG))],
        out_specs=[_rowblk(tr, AUG), pl.BlockSpec((AUG, tr), lambda i: (0, i)), _rowblk(tr, AUG),
                   pl.BlockSpec((AUG, tr), lambda i: (0, i)), _rowblk(tr, 768)],
        out_shape=[SDS((s, AUG), BF16), SDS((AUG, s), BF16), SDS((s, AUG), BF16), SDS((AUG, s), BF16),
                   SDS((s, 768), BF16)], name="fox_bwd_pre",
        compiler_params=_params(("parallel",)))(proj, yb, dgb, qn, c, lse, hsum, ea, ones_b)


def _fox2_bwd(qb, qbt, ka, kat, va, dya, dyat):
    s = qb.shape[0]
    tq, tk = _fox_tiles(s)
    nq, nk = s // tq, s // tk
    ng = 2
    gh = B_HEADS // ng
    gw = 128 * gh

    def first_q(j):
        return (j * tk) // tq

    def body(q_ref, qt_ref, k_ref, kt_ref, v_ref, dy_ref, dyt_ref, dq_hbm, dk_ref, dv_ref, dck_ref,
             dq_acc, dk_acc, dv_acc, sem):
        g, j, i = pl.program_id(0), pl.program_id(1), pl.program_id(2)

        @pl.when((j == 0) & (i == 0))
        def _():
            dq_acc[...] = jnp.zeros_like(dq_acc)

        @pl.when(i == 0)
        def _():
            dk_acc[...] = jnp.zeros_like(dk_acc)
            dv_acc[...] = jnp.zeros_like(dv_acc)

        def tile(masked):
            if masked:
                kpos = j * tk + lax.broadcasted_iota(jnp.int32, (tk, tq), 0)
                qpos = i * tq + lax.broadcasted_iota(jnp.int32, (tk, tq), 1)
                mask = kpos <= qpos
            cols = pl.ds(pl.multiple_of(i * tq, tq), tq)
            for h in range(gh):
                sl = slice(128 * h, 128 * h + 128)
                sc = _dot(k_ref[:, sl], qt_ref[sl, :])
                if masked:
                    sc = jnp.where(mask, sc, NEG)
                p = jnp.exp(sc)
                ds = (p * _dot(v_ref[:, sl], dyt_ref[sl, :])).astype(BF16)
                dv_acc[:, sl] += _dot(p.astype(BF16), dy_ref[:, sl])
                dk_acc[:, sl] += _dot(ds, q_ref[:, sl])
                dq_acc[sl, cols] += _dot(kt_ref[sl, :], ds)

        full = j * tk + tk - 1 <= i * tq

        @pl.when(full)
        def _():
            tile(False)

        @pl.when(jnp.logical_and(jnp.logical_not(full), i >= first_q(j)))
        def _():
            tile(True)

        @pl.when(i == nq - 1)
        def _():
            dkw = dk_acc[...]
            dk_ref[...] = _compact(dkw)
            dv_ref[...] = _compact(dv_acc[...]).astype(BF16)
            dck_ref[...] = -_lane_of_heads(dkw, COL_B, gh * g)

        @pl.when((j == nk - 1) & (i == nq - 1))
        def _():
            cp = pltpu.make_async_copy(dq_acc, dq_hbm.at[pl.ds(pl.multiple_of(g * gw, gw), gw)], sem)
            cp.start()
            cp.wait()

    qrow = pl.BlockSpec((tq, gw), lambda g, j, i: (jnp.maximum(i, first_q(j)), g))
    qcol = pl.BlockSpec((gw, tq), lambda g, j, i: (g, jnp.maximum(i, first_q(j))))
    krow = pl.BlockSpec((tk, gw), lambda g, j, i: (j, g))
    kcol = pl.BlockSpec((gw, tk), lambda g, j, i: (g, j))
    kout = pl.BlockSpec((tk, gw // 2), lambda g, j, i: (j, g))
    return pl.pallas_call(
        body, grid=(ng, nk, nq),
        in_specs=[qrow, qcol, krow, kcol, krow, qrow, qcol],
        out_specs=[pl.BlockSpec(memory_space=pl.ANY), kout, kout,
                   pl.BlockSpec((None, tk, LANE), lambda g, j, i: (g, j, 0))],
        out_shape=[SDS((AUG, s), F32), SDS((s, 768), F32), SDS((s, 768), BF16), SDS((ng, s, LANE), F32)],
        scratch_shapes=[pltpu.VMEM((gw, s), F32), pltpu.VMEM((tk, gw), F32), pltpu.VMEM((tk, gw), F32),
                        pltpu.SemaphoreType.DMA],
        name="fox_bwd", compiler_params=_params(("arbitrary",) * 3))(qb, qbt, ka, kat, va, dya, dyat)


def _fox2_bwd_post(proj, fbl, qg, kg, bfor, bd, dqa, dkn, dck, *, tr):
    s = proj.shape[0]
    nb = s // tr
    triu = jnp.asarray(np.triu(np.ones((tr, tr), np.float32)))
    rev = lambda i: nb - 1 - i

    def body(q_ref, k_ref, fb_ref, qg_ref, kg_ref, bf_ref, bd_ref, tri_ref, dqa_ref, dkn_ref, dck_ref,
             dq_ref, dk_ref, dfb_ref, dqg_ref, dkg_ref, dbf_ref, carry):
        @pl.when(pl.program_id(0) == 0)
        def _():
            carry[...] = jnp.zeros_like(carry)
            dqg_ref[...] = jnp.zeros_like(dqg_ref)
            dkg_ref[...] = jnp.zeros_like(dkg_ref)
            dbf_ref[...] = jnp.zeros_like(dbf_ref)

        bd_v = bd_ref[...]
        dqw = jnp.transpose(dqa_ref[...])
        _, vjp_q = jax.vjp(lambda q, g: _qn_fn(q, g, bd_v), q_ref[...], qg_ref[...])
        dq, dqg = vjp_q(_compact(dqw))
        _, vjp_k = jax.vjp(lambda k, g: _kn_fn(k, g, bd_v), k_ref[...], kg_ref[...])
        dk, dkg = vjp_k(dkn_ref[...])
        dq_ref[...] = dq.astype(BF16)
        dk_ref[...] = dk.astype(BF16)
        dqg_ref[...] += dqg
        dkg_ref[...] += dkg

        dc = _lane_of_heads(dqw, COL_A) + (dck_ref[0] + dck_ref[1])
        dlogf = jnp.dot(tri_ref[...], dc, precision=HI, preferred_element_type=F32) + carry[...]
        carry[...] = dlogf[0:1, :]
        lane = lax.broadcasted_iota(jnp.int32, (tr, LANE), 1)
        xf = fb_ref[...] + bf_ref[...]
        dfb = jnp.where(lane < N_FORGET, dlogf * jax.nn.sigmoid(-xf), 0.0)
        dfb_ref[...] = dfb.astype(BF16)
        dbf_ref[...] += jnp.sum(dfb, axis=0, keepdims=True)

    rb = lambda w: pl.BlockSpec((tr, w), lambda i: (rev(i), 0))
    wn = lambda w, off: pl.BlockSpec((pl.Element(tr), pl.Element(w)), lambda i: (rev(i) * tr, off))
    return pl.pallas_call(
        body, grid=(nb,),
        in_specs=[wn(768, QB), wn(768, KB), rb(LANE), _const((1, 768)), _const((1, 768)), _const((1, LANE)),
                  _const((768, 768)), _const((tr, tr)), pl.BlockSpec((AUG, tr), lambda i: (0, rev(i))), rb(768),
                  pl.BlockSpec((2, tr, LANE), lambda i: (0, rev(i), 0))],
        out_specs=[rb(768), rb(768), rb(LANE), _const((1, 768)), _const((1, 768)), _const((1, LANE))],
        out_shape=[SDS((s, 768), BF16), SDS((s, 768), BF16), SDS((s, LANE), BF16), SDS((1, 768), F32),
                   SDS((1, 768), F32), SDS((1, LANE), F32)],
        scratch_shapes=[pltpu.VMEM((1, LANE), F32)], name="fox_bwd_post",
        compiler_params=_params(("arbitrary",)))(proj, proj, fbl, qg, kg, bfor, bd, triu, dqa, dkn, dck)


def _merge_specs(tr):
    row = lambda w: pl.BlockSpec((tr, w), lambda i, j: (i, 0))
    shard = lambda r: pl.BlockSpec((None, r, 512), lambda i, j: (j, 0, 0))
    gate = lambda b: pl.BlockSpec((tr, 512), lambda i, j: (i, (GATE + 2048 * b) // 512 + j))
    return [row(768), row(768), row(512), shard(768), shard(768), shard(512), gate(0), gate(1), gate(2)]


def _merge_fwd(proj, ga, gb, gc, wa, wb, wc, *, tr):
    s = proj.shape[0]

    def body(ga_ref, gb_ref, gc_ref, wa_ref, wb_ref, wc_ref, l0_ref, l1_ref, l2_ref, y_ref):
        ua = _dot(ga_ref[...], wa_ref[...])
        ub = _dot(gb_ref[...], wb_ref[...])
        uc = _dot(gc_ref[...], wc_ref[...])
        y = jax.nn.sigmoid(l0_ref[...]) * ua + jax.nn.sigmoid(l1_ref[...]) * ub + jax.nn.sigmoid(l2_ref[...]) * uc
        y_ref[...] = y.astype(BF16)

    return pl.pallas_call(
        body, grid=(s // tr, N_CHIPS), in_specs=_merge_specs(tr),
        out_specs=pl.BlockSpec((tr, 512), lambda i, j: (i, j)), out_shape=SDS((s, D_MODEL), BF16), name="merge_fwd",
        compiler_params=_params(("parallel", "arbitrary")))(ga, gb, gc, wa, wb, wc, proj, proj, proj)


def _merge_bwd(proj, ga, gb, gc, wa, wb, wc, dy, *, tr):
    s = proj.shape[0]

    def body(ga_ref, gb_ref, gc_ref, wa_ref, wb_ref, wc_ref, l0_ref, l1_ref, l2_ref, dy_ref,
             dl0_ref, dl1_ref, dl2_ref, dua_ref, dub_ref, duc_ref, dga_ref, dgb_ref, dgc_ref):
        j = pl.program_id(1)
        dyv = dy_ref[...]

        @pl.when(j == 0)
        def _():
            dga_ref[...] = jnp.zeros_like(dga_ref)
            dgb_ref[...] = jnp.zeros_like(dgb_ref)
            dgc_ref[...] = jnp.zeros_like(dgc_ref)

        for g_ref, w_ref, l_ref, dl_ref, du_ref, dg_ref in (
                (ga_ref, wa_ref, l0_ref, dl0_ref, dua_ref, dga_ref),
                (gb_ref, wb_ref, l1_ref, dl1_ref, dub_ref, dgb_ref),
                (gc_ref, wc_ref, l2_ref, dl2_ref, duc_ref, dgc_ref)):
            w = w_ref[...]
            u = _dot(g_ref[...], w)
            sg = jax.nn.sigmoid(l_ref[...])
            dl_ref[...] = (dyv * u * sg * (1.0 - sg)).astype(BF16)
            du = (dyv * sg).astype(BF16)
            du_ref[...] = du
            dg_ref[...] += _dot_nt(du, w)

    blk = pl.BlockSpec((tr, 512), lambda i, j: (i, j))
    row = lambda w: pl.BlockSpec((tr, w), lambda i, j: (i, 0))
    big = SDS((s, D_MODEL), BF16)
    return pl.pallas_call(
        body, grid=(s // tr, N_CHIPS), in_specs=_merge_specs(tr) + [blk],
        out_specs=[blk] * 6 + [row(768), row(768), row(512)],
        out_shape=[big] * 6 + [SDS((s, 768), F32), SDS((s, 768), F32), SDS((s, 512), F32)], name="merge_bwd",
        compiler_params=_params(("parallel", "arbitrary")))(ga, gb, gc, wa, wb, wc, proj, proj, proj, dy)


def _out_loss(y, wo, x, tgt, *, tr, tn):
    s = x.shape[0]

    def body(y_ref, w_ref, x_ref, t_ref, d_ref, db_ref, sq_ref, dy_ref):
        @pl.when((pl.program_id(0) == 0) & (pl.program_id(1) == 0))
        def _():
            sq_ref[...] = jnp.zeros_like(sq_ref)

        @pl.when(pl.program_id(1) == 0)
        def _():
            dy_ref[...] = jnp.zeros_like(dy_ref)

        w = w_ref[...]
        out = x_ref[...] + _dot(y_ref[...], w)
        diff = out - t_ref[...]
        sq_ref[...] += jnp.sum(diff * diff, axis=0, keepdims=True)
        d = diff * (1.0 / D_MODEL)
        d_ref[...] = d
        db = d.astype(BF16)
        db_ref[...] = db
        dy_ref[...] += _dot_nt(db, w)

    blk = pl.BlockSpec((tr, tn), lambda i, j: (i, j))
    row = pl.BlockSpec((tr, D_MODEL), lambda i, j: (i, 0))
    return pl.pallas_call(
        body, grid=(s // tr, D_MODEL // tn),
        in_specs=[row, pl.BlockSpec((D_MODEL, tn), lambda i, j: (0, j)), blk, blk],
        out_specs=[blk, blk, _const((1, tn)), row],
        out_shape=[SDS((s, D_MODEL), F32), SDS((s, D_MODEL), BF16), SDS((1, tn), F32), SDS((s, D_MODEL), F32)],
        name="out_loss", compiler_params=_params(("arbitrary", "arbitrary")))(y, wo, x, tgt)


def _tile_gain(g, reps):
    return jnp.tile(g.reshape(1, -1), (1, reps))


def _pad_lane(v):
    v = v.reshape(1, -1)
    return jnp.pad(v, ((0, 0), (0, LANE - v.shape[1])))


def _local_step(x, mem, tgt, w_main, w_fb, w_small, norm_gain, mem_norm_gain, b_forget,
                q_gain_a, k_gain_a, sinks_a, q_gain_b, k_gain_b, q_gain_c, k_gain_c, core=None):
    s = x.shape[0]
    tr = min(512, s)
    bd64 = _block_diag(768, HEAD_DIM)
    bd128 = _block_diag(512, C_HEAD_DIM)
    hsum = _head_sum(768, HEAD_DIM)
    qga, kga = _tile_gain(q_gain_a, 12), _tile_gain(k_gain_a, 4)
    qgb, kgb = _tile_gain(q_gain_b, 12), _tile_gain(k_gain_b, 12)
    qgc, kgc = _tile_gain(q_gain_c, 4), _tile_gain(k_gain_c, 4)
    sinks = _pad_lane(sinks_a)
    bfor = _pad_lane(b_forget)

    hn = _rms_fwd(x, norm_gain, tr=tr, name="rms_x")
    on_mesh = core is not None
    if on_mesh:
        proj, (gathered,) = _matmul(hn, w_main, dims="nn", out_dtype=F32, tm=1024, tn=1024, tk=D_MODEL, name="proj_main",
                                    comms=[_gather_comm(list(w_small))])
        w_mk, wa, wb, wc, wo = gathered
        w_mk, wo = w_mk.reshape(D_MODEL, 1024), wo.reshape(D_MODEL, D_MODEL)
    else:
        proj = _matmul(hn, w_main, dims="nn", out_dtype=F32, tm=1024, tn=1024, tk=D_MODEL, name="proj_main")
        w_mk, wa, wb, wc, wo = w_small
    fbl = _matmul(hn, w_fb, dims="nn", out_dtype=F32, tm=1024, tn=LANE, tk=D_MODEL, name="proj_forget")
    memn = _rms_fwd(mem, mem_norm_gain, tr=mem.shape[0], name="rms_mem")
    mkv = _matmul(memn, w_mk, dims="nn", out_dtype=F32, tm=256, tn=512, tk=D_MODEL, name="mem_kv")

    swa_bias = _swa_bias()
    ga = _swa_fwd(proj, qga, kga, sinks, bd64, swa_bias)
    ea, eb, ones_a, ones_b = _expand_mats()
    tf = min(256, s)
    qat, ka, kat, va, vat, qn, cfox = _fox2_prep(proj, fbl, qgb, kgb, bfor, bd64, ea, eb, ones_a, ones_b, tr=tf)
    gb, yb, lse = _fox2_fwd(proj, qat, ka, vat)
    gc = _mem_fwd(proj, mkv, qgc, kgc, bd128, tr=tr)
    y = _merge_fwd(proj, ga, gb, gc, wa, wb, wc, tr=tr)
    dout, dout_b, sq, dy = _out_loss(y, wo, x, tgt, tr=tr, tn=512)

    d_wo = _matmul(y, dout_b, dims="tn", out_dtype=F32, tm=1024, tn=512, tk=4096, name="dw_out")
    dl0, dl1, dl2, dua, dub, duc, dga, dgb, dgc = _merge_bwd(proj, ga, gb, gc, wa, wb, wc, dy, tr=tr)
    d_wa = _matmul(ga, dua, dims="tn", out_dtype=F32, tm=768, tn=512, tk=4096, name="dw_branch_a")
    d_wb = _matmul(gb, dub, dims="tn", out_dtype=F32, tm=768, tn=512, tk=4096, name="dw_branch_b")
    d_wc = _matmul(gc, duc, dims="tn", out_dtype=F32, tm=512, tn=512, tk=4096, name="dw_branch_c")

    dcur, dprev, d_qga, d_kga, d_sinks = _swa_bwd(proj, qga, kga, sinks, bd64, swa_bias, dga)
    dproj_a = _swa_combine(dcur, dprev)

    qab, qabt, dya, dyat, dzb = _fox2_bwd_pre(proj, yb, dgb, qn, cfox, lse, hsum, ea, ones_b, tr=tf)
    dqa, dkn, dvb, dck = _fox2_bwd(qab, qabt, ka, kat, va, dya, dyat)
    dqb, dkb, dfb, d_qgb, d_kgb, d_bf = _fox2_bwd_post(proj, fbl, qgb, kgb, bfor, bd64, dqa, dkn, dck, tr=tf)

    dproj_c, dmkv, d_qgc, d_kgc = _mem_bwd(proj, mkv, qgc, kgc, bd128, dgc, tr=tr)
    dmkv_b = dmkv.astype(BF16)
    d_wmk = _matmul(memn, dmkv_b, dims="tn", out_dtype=F32, tm=1024, tn=512, tk=256, name="dw_mem_kv")
    dmemn = _matmul(dmkv_b, w_mk, dims="nt", out_dtype=F32, tm=256, tn=512, tk=1024, name="dmemn")
    (d_mem_gain,) = _rms_bwd(mem, mem_norm_gain, dmemn, None, tr=mem.shape[0], name="rms_mem_bwd")

    dproj = [dproj_a, jnp.concatenate([dqb, dkb, dvb, dzb, dproj_c], axis=1), dl0, dl1, dl2]
    dhn_f = _matmul(dfb, w_fb, dims="nt", out_dtype=F32, tm=1024, tn=512, tk=LANE, name="dhn_forget")
    d_wfb = _matmul(hn, dfb, dims="tn", out_dtype=F32, tm=1024, tn=LANE, tk=512, name="dw_forget")
    big = {}
    if on_mesh:
        half = D_MODEL // 2
        c0 = core[0]
        hn_other = lax.dynamic_slice(hn, (0, (1 - c0) * half), (s, half))
        hn_own = lax.dynamic_slice(hn, (0, c0 * half), (s, half))
        g1, k1 = [d_wmk, d_wa, d_wb, d_wc, d_wo], [2, 3, 4, 5, 6]
        d_other, (got1,) = _matmul(hn_other, dproj, dims="tn", out_dtype=F32, tm=1024, tn=512, tk=4096,
                                   name="dw_main_other", comms=[_exchange_comm(g1, k1)])
        h1 = [_add_half(g, got, core, HALF_AXIS[k], name=f"add_half_{k}") for g, got, k in zip(g1, got1, k1)]
        d_own, (got0, parts1) = _matmul(
            hn_own, dproj, dims="tn", out_dtype=F32, tm=1024, tn=512, tk=4096, name="dw_main_own",
            comms=[_exchange_comm([d_other, d_wfb], [0, 1], whole=(0,)), _scatter_comm(h1, k1)])
        h0 = [_add_pair(d_own, got0[0], name="add_pair_main"), _add_half(d_wfb, got0[1], core, 0, name="add_half_1")]
        sums1 = [_sum4(p, name=f"sum4_{k}") for p, k in zip(parts1, k1)]
        dhn, (parts0, theirs1) = _matmul(dproj, w_main, dims="nt", out_dtype=F32, tm=1024, tn=512, tk=2048, vmem=VMEM_WIDE, name="dhn",
                                         add=dhn_f, comms=[_scatter_comm(h0, [0, 1]), _swap_comm(sums1)])
        sums0 = [_sum4(p, name=f"sum4_{k}") for p, k in zip(parts0, (0, 1))]
        (grad_x, d_gain), theirs0 = _rms_bwd(x, norm_gain, dhn, dout, tr=tr, name="rms_x_bwd", comm=_swap_comm(sums0))
        big = dict(sums=sums0 + sums1, theirs=list(theirs0) + list(theirs1))
    else:
        dhn = _matmul(dproj, w_main, dims="nt", out_dtype=F32, tm=1024, tn=512, tk=2048, vmem=VMEM_WIDE, name="dhn", add=dhn_f)
        d_wmain = _matmul(hn, dproj, dims="tn", out_dtype=F32, tm=1024, tn=512, tk=4096, name="dw_main")
        big = dict(d_wmain=d_wmain, d_wfb=d_wfb, d_wmk=d_wmk, d_wa=d_wa, d_wb=d_wb, d_wc=d_wc, d_wo=d_wo)
        grad_x, d_gain = _rms_bwd(x, norm_gain, dhn, dout, tr=tr, name="rms_x_bwd")

    fold = lambda g, reps: jnp.sum(g.reshape(reps, -1), axis=0, keepdims=True)
    return dict(
        sq=sq, grad_x=grad_x, **big,
        d_gain=d_gain, d_mem_gain=d_mem_gain, d_bf=d_bf[:, :N_FORGET],
        d_qga=fold(d_qga, 12), d_kga=fold(d_kga, 4), d_sinks=d_sinks[:, :A_HEADS],
        d_qgb=fold(d_qgb, 12), d_kgb=fold(d_kgb, 12), d_qgc=fold(d_qgc, 4), d_kgc=fold(d_kgc, 4))


PACK_ROWS = 256
FORGET_IN_SHARD = FORGET_COL - SHARD_COLS
AFTER_FORGET = FORGET_COL - SLAB_START[1]
END_CHIP1 = 2 * SHARD_COLS - N_FORGET - SLAB_START[1]


def _pack_w_in(chip, w):
    rows = w.shape[1]
    tr = PACK_ROWS

    def body(k_ref, w_ref, o_ref, scr):
        scr[...] = jnp.zeros_like(scr)
        scr[pl.ds(0, SHARD_COLS), :] = w_ref[...]
        v = jnp.transpose(scr[...])
        k = k_ref[0]
        col = lax.broadcasted_iota(jnp.int32, (tr, SLAB), 1)
        no_forget = jnp.zeros((tr, LANE), BF16)

        @pl.when(k == 0)
        def _():
            o_ref[:, 0:SLAB] = v.astype(BF16)
            o_ref[:, SLAB:] = no_forget

        @pl.when(k == 1)
        def _():
            before = pltpu.roll(v, SLAB_SHIFT[1], axis=1)
            after = pltpu.roll(v, SLAB - (N_FORGET - SLAB_SHIFT[1]), axis=1)
            slab = jnp.where(col < AFTER_FORGET, before, jnp.where(col < END_CHIP1, after, 0.0))
            o_ref[:, 0:SLAB] = slab.astype(BF16)
            f = pltpu.roll(v, SLAB - FORGET_IN_SHARD, axis=1)[:, :LANE]
            o_ref[:, SLAB:] = jnp.where(col[:, :LANE] < N_FORGET, f, 0.0).astype(BF16)

        for kk in (2, 3):
            @pl.when(k == kk)
            def _(kk=kk):
                o_ref[:, 0:SLAB] = pltpu.roll(v, SLAB_SHIFT[kk], axis=1).astype(BF16)
                o_ref[:, SLAB:] = no_forget

    return pl.pallas_call(
        body, grid_spec=pltpu.PrefetchScalarGridSpec(
            num_scalar_prefetch=1, grid=(rows // tr,),
            in_specs=[pl.BlockSpec((SHARD_COLS, tr), lambda i, k: (0, i))],
            out_specs=pl.BlockSpec((None, tr, SLAB + LANE), lambda i, k: (k[0], i, 0)),
            scratch_shapes=[pltpu.VMEM((SLAB, tr), F32)]),
        out_shape=SDS((N_CHIPS, rows, SLAB + LANE), BF16), name="pack_w_in",
        compiler_params=_params(("arbitrary",)))(chip, w)


def _merge_slabs(g):
    rows = g.shape[1]
    tr = PACK_ROWS
    t = [s // LANE for s in SLAB_START]
    n_t = SLAB // LANE

    def body(g_ref, m_ref, f_ref):
        for k in range(N_CHIPS):
            lo = t[k] + (1 if k > 0 else 0)
            hi = t[k + 1] if k + 1 < N_CHIPS else t[k] + n_t
            m_ref[:, lo * LANE:hi * LANE] = g_ref[k, :, (lo - t[k]) * LANE:(hi - t[k]) * LANE]
            if k + 1 < N_CHIPS:
                a = g_ref[k, :, (hi - t[k]) * LANE:(hi - t[k] + 1) * LANE].astype(F32)
                b = g_ref[k + 1, :, 0:LANE].astype(F32)
                m_ref[:, hi * LANE:(hi + 1) * LANE] = (a + b).astype(BF16)
        f_ref[...] = g_ref[1, :, SLAB:]

    return pl.pallas_call(
        body, grid=(rows // tr,),
        in_specs=[pl.BlockSpec((N_CHIPS, tr, SLAB + LANE), lambda i: (0, i, 0))],
        out_specs=[_rowblk(tr, P_MAIN), _rowblk(tr, LANE)],
        out_shape=[SDS((rows, P_MAIN), BF16), SDS((rows, LANE), BF16)], name="merge_slabs",
        compiler_params=_params(("parallel",)))(g)


def _adamw_math(w, g, m, v):
    nm = ADAM_B1 * m + (1.0 - ADAM_B1) * g
    nv = ADAM_B2 * v + (1.0 - ADAM_B2) * (g * g)
    m_hat = nm / (1.0 - ADAM_B1 ** ADAM_STEP)
    v_hat = nv / (1.0 - ADAM_B2 ** ADAM_STEP)
    delta = -ADAM_LR * (m_hat / (jnp.sqrt(v_hat) + ADAM_EPS) + ADAM_WD * w)
    return delta, nm, nv


def _adamw(g, w, m, v, *, tr, name):
    rows, cols = w.shape
    tr = min(tr, rows)

    def body(g_ref, w_ref, m_ref, v_ref, d_ref, nm_ref, nv_ref):
        d, nm, nv = _adamw_math(w_ref[...], g_ref[...], m_ref[...], v_ref[...])
        d_ref[...] = d
        nm_ref[...] = nm
        nv_ref[...] = nv

    spec = _rowblk(tr, cols)
    return pl.pallas_call(
        body, grid=(rows // tr,), in_specs=[spec] * 4, out_specs=[spec] * 3,
        out_shape=[SDS((rows, cols), F32)] * 3, name=name, compiler_params=_params(("parallel",)))(g, w, m, v)


def _adamw_w_in(chip_core, slab_mine, slab_theirs, forget_mine, forget_theirs, w, m, v):
    rows = w.shape[1]
    tr = PACK_ROWS // 2
    nbh = rows // 2 // tr

    def body(k_ref, sa_ref, sb_ref, fa_ref, fb_ref, w_ref, m_ref, v_ref, g_ref, d_ref, nm_ref, nv_ref):
        use_mine = pl.program_id(0) // nbh == k_ref[1]
        sl = jnp.where(use_mine, sa_ref[...], sb_ref[...])
        f_tile = jnp.where(use_mine, fa_ref[...], fb_ref[...])
        k = k_ref[0]

        def emit(wide):
            g = jnp.transpose(wide)[:SHARD_COLS, :]
            g_ref[...] = g
            d, nm, nv = _adamw_math(w_ref[...], g, m_ref[...], v_ref[...])
            d_ref[...] = d
            nm_ref[...] = nm
            nv_ref[...] = nv

        @pl.when(k == 0)
        def _():
            emit(sl)

        @pl.when(k == 1)
        def _():
            col = lax.broadcasted_iota(jnp.int32, (tr, SLAB), 1)
            before = pltpu.roll(sl, SLAB - SLAB_SHIFT[1], axis=1)
            after = pltpu.roll(sl, N_FORGET - SLAB_SHIFT[1], axis=1)
            wide_f = jnp.concatenate([f_tile, jnp.zeros((tr, SLAB - LANE), F32)], axis=1)
            forget = pltpu.roll(wide_f, FORGET_IN_SHARD, axis=1)
            emit(jnp.where(col < FORGET_IN_SHARD, before, jnp.where(col < FORGET_IN_SHARD + N_FORGET, forget, after)))

        for kk in (2, 3):
            @pl.when(k == kk)
            def _(kk=kk):
                emit(pltpu.roll(sl, SLAB - SLAB_SHIFT[kk], axis=1))

    nat = pl.BlockSpec((SHARD_COLS, tr), lambda i, k: (0, i))
    half = lambda width: pl.BlockSpec((tr, width), lambda i, k: (i % nbh, 0))
    return pl.pallas_call(
        body, grid_spec=pltpu.PrefetchScalarGridSpec(
            num_scalar_prefetch=1, grid=(rows // tr,),
            in_specs=[half(SLAB), half(SLAB), half(LANE), half(LANE), nat, nat, nat],
            out_specs=[nat] * 4),
        out_shape=[SDS((SHARD_COLS, rows), F32)] * 4, name="adamw_w_in",
        compiler_params=_params(("arbitrary",)))(chip_core, slab_mine, slab_theirs, forget_mine, forget_theirs, w, m, v)


ANY = pl.BlockSpec(memory_space=pl.ANY)
HALF_AXIS = (0, 0, 1, 0, 0, 0, 1)


def _me():
    return lax.axis_index("x"), lax.axis_index("y"), lax.axis_index("c")


def _half(ref, which, axis):
    n = ref.shape[axis] // 2
    sl = pl.ds(which * n, n)
    return ref.at[sl] if axis == 0 else ref.at[:, sl]


def _piece(t, ref, j):
    if t == 0:
        return ref.at[:, pl.ds(SLAB_START[j], SLAB)]
    if t == 1:
        return ref
    if t in (2, 6):
        return ref.at[pl.ds(512 * j, 512)]
    return ref.at[:, pl.ds(512 * j, 512)]


def _piece_shape(t, shape):
    if t == 0:
        return (shape[0], SLAB)
    if t == 1:
        return shape
    if t in (2, 6):
        return (512, shape[1])
    return (shape[0], 512)


def _gather_plan(ins, outs, own_slot_in_src):
    x, y, c = _me()
    k = 2 * x + y
    sib = (x, y, 1 - c)
    chips = [(1 - x, y), (x, 1 - y), (1 - x, 1 - y)]
    n = len(outs)

    def rows(t, which):
        h = outs[t].shape[1] // 2
        return pl.ds(which * h, h)

    def mine(t):
        return ins[t].at[k, rows(t, c)] if own_slot_in_src else ins[t].at[rows(t, c)]

    def first(t, j, sems):
        chip = chips[j]
        return pltpu.make_async_remote_copy(
            src_ref=mine(t), dst_ref=outs[t].at[k, rows(t, c)], send_sem=sems[0].at[t, j], recv_sem=sems[1].at[t, j],
            device_id=(chip[0], chip[1], c), device_id_type=MESH)

    def landed(t, j, sems):
        chip = chips[j]
        return pltpu.make_async_remote_copy(
            src_ref=mine(t), dst_ref=outs[t].at[2 * chip[0] + chip[1], rows(t, c)], send_sem=sems[0].at[t, j],
            recv_sem=sems[1].at[t, j], device_id=(chip[0], chip[1], c), device_id_type=MESH)

    def passed(t, j, which, sems):
        chip = chips[j]
        blk = outs[t].at[2 * chip[0] + chip[1], rows(t, which)]
        return pltpu.make_async_remote_copy(
            src_ref=blk, dst_ref=blk, send_sem=sems[2].at[t, j], recv_sem=sems[3].at[t, j], device_id=sib,
            device_id_type=MESH)

    def start(sems):
        for j in range(3):
            for t in range(n):
                first(t, j, sems).start()

    def finish(sems):
        for j in range(3):
            for t in range(n):
                landed(t, j, sems).wait_recv()
                passed(t, j, c, sems).start()
        for j in range(3):
            for t in range(n):
                passed(t, j, 1 - c, sems).wait_recv()
        for j in range(3):
            for t in range(n):
                first(t, j, sems).wait_send()
                passed(t, j, c, sems).wait_send()

    return k, start, finish


def _all_gather_slabs(slabs):
    def body(in_ref, out_ref, nbr_sem, quarter_sem, pass_sem):
        x, y, c = _me()
        k = 2 * x + y
        rows = out_ref.shape[1]
        h, q = rows // 2, rows // 4
        nbrs = [(1 - x, y), (x, 1 - y)]
        slot = lambda chip: 2 * chip[0] + chip[1]
        diag = 2 * (1 - x) + (1 - y)
        half = pl.ds(c * h, h)
        quarter = lambda a: pl.ds(c * h + a * q, q)

        def first(a):
            return pltpu.make_async_remote_copy(
                src_ref=in_ref.at[k, half], dst_ref=out_ref.at[k, half], send_sem=nbr_sem.at[0, a],
                recv_sem=nbr_sem.at[1, a], device_id=(nbrs[a][0], nbrs[a][1], c), device_id_type=MESH)

        def landed(a):
            blk = out_ref.at[slot(nbrs[a]), half]
            return pltpu.make_async_remote_copy(
                src_ref=blk, dst_ref=blk, send_sem=nbr_sem.at[0, a], recv_sem=nbr_sem.at[1, a],
                device_id=(nbrs[a][0], nbrs[a][1], c), device_id_type=MESH)

        def relay(a):
            blk = out_ref.at[slot(nbrs[a]), quarter(a)]
            to = nbrs[1 - a]
            return pltpu.make_async_remote_copy(
                src_ref=blk, dst_ref=blk, send_sem=quarter_sem.at[0, a], recv_sem=quarter_sem.at[1, a],
                device_id=(to[0], to[1], c), device_id_type=MESH)

        def relayed(a):
            blk = out_ref.at[diag, quarter(a)]
            frm = nbrs[1 - a]
            return pltpu.make_async_remote_copy(
                src_ref=blk, dst_ref=blk, send_sem=quarter_sem.at[0, a], recv_sem=quarter_sem.at[1, a],
                device_id=(frm[0], frm[1], c), device_id_type=MESH)

        def passed(j, which):
            sl = diag if j == 2 else slot(nbrs[j])
            blk = out_ref.at[sl, pl.ds(which * h, h)]
            return pltpu.make_async_remote_copy(
                src_ref=blk, dst_ref=blk, send_sem=pass_sem.at[0, j], recv_sem=pass_sem.at[1, j],
                device_id=(x, y, 1 - c), device_id_type=MESH)

        for a in range(2):
            first(a).start()
        for a in range(2):
            landed(a).wait_recv()
            relay(a).start()
            passed(a, c).start()
        for a in range(2):
            relayed(a).wait_recv()
        passed(2, c).start()
        for j in range(3):
            passed(j, 1 - c).wait_recv()
        for a in range(2):
            first(a).wait_send()
            relay(a).wait_send()
        for j in range(3):
            passed(j, c).wait_send()

    return pl.pallas_call(
        body, in_specs=[ANY], out_specs=ANY, out_shape=SDS(slabs.shape, slabs.dtype),
        scratch_shapes=[pltpu.SemaphoreType.DMA((2, 2)), pltpu.SemaphoreType.DMA((2, 2)), pltpu.SemaphoreType.DMA((2, 3))],
        input_output_aliases={0: 0}, name="all_gather_slabs")(slabs)


def _gather_comm(parts):
    n = len(parts)

    def start(ins, outs, sems):
        k, go, _ = _gather_plan(ins, outs, False)
        for t in range(n):
            pltpu.make_async_copy(ins[t], outs[t].at[k], sems[4].at[t]).start()
        go(sems)

    def finish(ins, outs, sems):
        k, _, done = _gather_plan(ins, outs, False)
        done(sems)
        for t in range(n):
            pltpu.make_async_copy(ins[t], outs[t].at[k], sems[4].at[t]).wait()

    return _Comm(parts, [SDS((N_CHIPS,) + p.shape, p.dtype) for p in parts],
                 [pltpu.SemaphoreType.DMA((n, 3))] * 4 + [pltpu.SemaphoreType.DMA((n,))], start, finish)


def _exchange_comm(arrs, kinds, whole=()):
    n = len(arrs)

    def copies(ins, outs, sems):
        x, y, c = _me()
        return [pltpu.make_async_remote_copy(
            src_ref=ins[t] if t in whole else _half(ins[t], 1 - c, HALF_AXIS[kinds[t]]), dst_ref=outs[t],
            send_sem=sems[0].at[t], recv_sem=sems[1].at[t], device_id=(x, y, 1 - c), device_id_type=MESH)
            for t in range(n)]

    def start(ins, outs, sems):
        for cp in copies(ins, outs, sems):
            cp.start()

    def finish(ins, outs, sems):
        for cp in copies(ins, outs, sems):
            cp.wait()

    def hshape(t):
        s = list(arrs[t].shape)
        if t not in whole:
            s[HALF_AXIS[kinds[t]]] //= 2
        return SDS(tuple(s), arrs[t].dtype)

    return _Comm(arrs, [hshape(t) for t in range(n)], [pltpu.SemaphoreType.DMA((n,))] * 2, start, finish)


def _add_half(full, got, core, axis, *, name):
    r, c = got.shape
    br, bc = (256 if r % 256 == 0 else 128), min(2048, c)
    off_r = (r // br) if axis == 0 else 0
    off_c = (c // bc) if axis == 1 else 0

    def body(c_ref, a_ref, b_ref, o_ref):
        o_ref[...] = (a_ref[...] + b_ref[...]).astype(BF16)

    return pl.pallas_call(
        body, grid_spec=pltpu.PrefetchScalarGridSpec(
            num_scalar_prefetch=1, grid=(r // br, c // bc),
            in_specs=[pl.BlockSpec((br, bc), lambda i, j, cr: (i + cr[0] * off_r, j + cr[0] * off_c)),
                      pl.BlockSpec((br, bc), lambda i, j, cr: (i, j))],
            out_specs=pl.BlockSpec((br, bc), lambda i, j, cr: (i, j))),
        out_shape=SDS((r, c), BF16), name=name, compiler_params=_params(("parallel", "parallel")))(core, full, got)


def _add_pair(a, b, *, name):
    r, c = a.shape
    br, bc = 256, min(2048, c)

    def body(a_ref, b_ref, o_ref):
        o_ref[...] = (a_ref[...] + b_ref[...]).astype(BF16)

    spec = pl.BlockSpec((br, bc), lambda i, j: (i, j))
    return pl.pallas_call(body, grid=(r // br, c // bc), in_specs=[spec, spec], out_specs=spec,
                          out_shape=SDS((r, c), BF16), name=name, compiler_params=_params(("parallel", "parallel")))(a, b)


def _scatter_comm(halves, kinds):
    n = len(halves)

    def plan(ins, outs, sems):
        send, recv, lsem = sems
        x, y, c = _me()
        k = 2 * x + y

        def to_chip(t, j):
            return pltpu.make_async_remote_copy(
                src_ref=_piece(kinds[t], ins[t], j), dst_ref=outs[t].at[k], send_sem=send.at[t, j],
                recv_sem=recv.at[t, k], device_id=(j // 2, j % 2, c), device_id_type=MESH)

        def from_chip(t, j):
            return pltpu.make_async_remote_copy(
                src_ref=_piece(kinds[t], ins[t], j), dst_ref=outs[t].at[j], send_sem=send.at[t, j],
                recv_sem=recv.at[t, j], device_id=(j // 2, j % 2, c), device_id_type=MESH)

        def own(t, j):
            return pltpu.make_async_copy(_piece(kinds[t], ins[t], j), outs[t].at[j], lsem.at[t])

        return k, to_chip, from_chip, own

    def start(ins, outs, sems):
        k, to_chip, _, own = plan(ins, outs, sems)
        for j in range(N_CHIPS):
            @pl.when(k != j)
            def _(j=j):
                for t in range(n):
                    to_chip(t, j).start()

            @pl.when(k == j)
            def _(j=j):
                for t in range(n):
                    own(t, j).start()

    def finish(ins, outs, sems):
        k, to_chip, from_chip, own = plan(ins, outs, sems)
        for j in range(N_CHIPS):
            @pl.when(k != j)
            def _(j=j):
                for t in range(n):
                    from_chip(t, j).wait_recv()
                for t in range(n):
                    to_chip(t, j).wait_send()

            @pl.when(k == j)
            def _(j=j):
                for t in range(n):
                    own(t, j).wait()

    return _Comm(halves, [SDS((N_CHIPS,) + _piece_shape(kinds[t], halves[t].shape), halves[t].dtype) for t in range(n)],
                 [pltpu.SemaphoreType.DMA((n, N_CHIPS))] * 2 + [pltpu.SemaphoreType.DMA((n,))], start, finish)


def _sum4(p, *, name):
    _, r, c = p.shape
    br = 256 if r % 256 == 0 else 128

    def body(p_ref, o_ref):
        o_ref[...] = ((p_ref[0].astype(F32) + p_ref[1].astype(F32)) + p_ref[2].astype(F32)) + p_ref[3].astype(F32)

    return pl.pallas_call(
        body, grid=(r // br,), in_specs=[pl.BlockSpec((N_CHIPS, br, c), lambda i: (0, i, 0))],
        out_specs=_rowblk(br, c), out_shape=SDS((r, c), F32), name=name, compiler_params=_params(("parallel",)))(p)


def _swap_comm(sums):
    return _exchange_comm(sums, [None] * len(sums), whole=tuple(range(len(sums))))


def _adamw_halves(mine, theirs, core, w, m, v, *, axis, tr, name):
    rows, cols = w.shape

    if axis == 0:
        nbh = rows // 2 // tr
        g_spec = pl.BlockSpec((tr, cols), lambda i, cr: (i % nbh, 0))
    else:
        g_spec = pl.BlockSpec((tr, cols // 2), lambda i, cr: (i, 0))

    def body(c_ref, a_ref, b_ref, w_ref, m_ref, v_ref, g_ref, d_ref, nm_ref, nv_ref):
        a, b = a_ref[...], b_ref[...]
        if axis == 0:
            g = jnp.where(pl.program_id(0) // nbh == c_ref[0], a, b)
        else:
            low = c_ref[0] == 0
            g = jnp.concatenate([jnp.where(low, a, b), jnp.where(low, b, a)], axis=1)
        g_ref[...] = g
        d, nm, nv = _adamw_math(w_ref[...], g, m_ref[...], v_ref[...])
        d_ref[...] = d
        nm_ref[...] = nm
        nv_ref[...] = nv

    nat = pl.BlockSpec((tr, cols), lambda i, cr: (i, 0))
    return pl.pallas_call(
        body, grid_spec=pltpu.PrefetchScalarGridSpec(
            num_scalar_prefetch=1, grid=(rows // tr,), in_specs=[g_spec, g_spec, nat, nat, nat], out_specs=[nat] * 4),
        out_shape=[SDS((rows, cols), F32)] * 4, name=name, compiler_params=_params(("arbitrary",)))(
            core, mine, theirs, w, m, v)


SMALL_ROWS, SMALL_COLS = 8, 1024


def _pack_small(vs):
    flat = jnp.concatenate([v.reshape(-1) for v in vs])
    return jnp.pad(flat, (0, SMALL_ROWS * SMALL_COLS - flat.shape[0])).reshape(SMALL_ROWS, SMALL_COLS)


def _unpack_small(packed, sizes):
    flat = packed.reshape(-1)
    out, o = [], 0
    for n in sizes:
        out.append(flat[o:o + n].reshape(1, n))
        o += n
    return out


def _all_reduce_small(v):
    n_dev = 8

    def body(v_ref, o_ref, land, send, recv):
        x, y, c = _me()
        me = 4 * x + 2 * y + c
        land[me] = v_ref[...]
        cps = []
        for r in range(1, n_dev):
            fx, fy, fc = (r >> 2) & 1, (r >> 1) & 1, r & 1
            peer = (x ^ fx, y ^ fy, c ^ fc)
            cps.append(pltpu.make_async_remote_copy(
                src_ref=v_ref, dst_ref=land.at[me], send_sem=send.at[r - 1], recv_sem=recv.at[r - 1],
                device_id=peer, device_id_type=MESH))
        for cp in cps:
            cp.start()
        for r in range(1, n_dev):
            fx, fy, fc = (r >> 2) & 1, (r >> 1) & 1, r & 1
            src = 4 * (x ^ fx) + 2 * (y ^ fy) + (c ^ fc)
            pltpu.make_async_remote_copy(
                src_ref=v_ref, dst_ref=land.at[src], send_sem=send.at[r - 1], recv_sem=recv.at[r - 1],
                device_id=(x ^ fx, y ^ fy, c ^ fc), device_id_type=MESH).wait_recv()
        for cp in cps:
            cp.wait_send()
        acc = land[0]
        for r in range(1, n_dev):
            acc = acc + land[r]
        o_ref[...] = acc

    vm = pl.BlockSpec(memory_space=pltpu.VMEM)
    return pl.pallas_call(
        body, in_specs=[vm], out_specs=vm, out_shape=SDS(v.shape, F32),
        scratch_shapes=[pltpu.VMEM((n_dev,) + v.shape, F32), pltpu.SemaphoreType.DMA((n_dev - 1,)),
                        pltpu.SemaphoreType.DMA((n_dev - 1,))],
        name="all_reduce_small")(v)


def kernel(x, mem, norm_gain, mem_norm_gain, w_in, b_forget, q_gain_a, k_gain_a, sinks_a, q_gain_b, k_gain_b, q_gain_c, k_gain_c, w_mem_kv, w_branch_a, w_branch_b, w_branch_c, w_out, loss_target, m_norm_gain, m_mem_norm_gain, m_w_in, m_b_forget, m_q_gain_a, m_k_gain_a, m_sinks_a, m_q_gain_b, m_k_gain_b, m_q_gain_c, m_k_gain_c, m_w_mem_kv, m_w_branch_a, m_w_branch_b, m_w_branch_c, m_w_out, v_norm_gain, v_mem_norm_gain, v_w_in, v_b_forget, v_q_gain_a, v_k_gain_a, v_sinks_a, v_q_gain_b, v_k_gain_b, v_q_gain_c, v_k_gain_c, v_w_mem_kv, v_w_branch_a, v_w_branch_b, v_w_branch_c, v_w_out):
    xi, yi, ci = lax.axis_index("x"), lax.axis_index("y"), lax.axis_index("c")
    chip = jnp.reshape(2 * xi + yi, (1,)).astype(jnp.int32)
    core = jnp.reshape(ci, (1,)).astype(jnp.int32)

    slabs = _pack_w_in(chip, jnp.transpose(w_in[0]))
    mine = [w_mem_kv[0].astype(BF16), w_branch_a[0].astype(BF16), w_branch_b[0].astype(BF16),
            w_branch_c[0].astype(BF16), w_out[0].astype(BF16)]
    w_main, w_fb = _merge_slabs(_all_gather_slabs(slabs))

    r = _local_step(x[0], mem[0], loss_target[0], w_main, w_fb, mine, norm_gain, mem_norm_gain,
                    b_forget, q_gain_a, k_gain_a, sinks_a, q_gain_b, k_gain_b, q_gain_c, k_gain_c, core=core)
    sums, theirs = r["sums"], r["theirs"]

    small_names = ["d_gain", "d_mem_gain", "d_bf", "d_qga", "d_kga", "d_sinks", "d_qgb", "d_kgb", "d_qgc", "d_kgc"]
    loss_part = (0.5 / D_MODEL) * jnp.sum(r["sq"], axis=1, keepdims=True)
    packed = _pack_small([r[n] for n in small_names] + [loss_part])
    red = _all_reduce_small(packed)
    small_w = [norm_gain, mem_norm_gain, b_forget, q_gain_a, k_gain_a, sinks_a, q_gain_b, k_gain_b, q_gain_c, k_gain_c]
    small_m = [m_norm_gain, m_mem_norm_gain, m_b_forget, m_q_gain_a, m_k_gain_a, m_sinks_a, m_q_gain_b, m_k_gain_b,
               m_q_gain_c, m_k_gain_c]
    small_v = [v_norm_gain, v_mem_norm_gain, v_b_forget, v_q_gain_a, v_k_gain_a, v_sinks_a, v_q_gain_b, v_k_gain_b,
               v_q_gain_c, v_k_gain_c]
    sizes = [w.shape[1] for w in small_w]
    s_d, s_m, s_v = _adamw(red, _pack_small(small_w), _pack_small(small_m), _pack_small(small_v), tr=8, name="adamw_small")
    g_small = _unpack_small(red, sizes + [1])
    loss = g_small[-1].reshape(())
    d_small, m_small, v_small = _unpack_small(s_d, sizes), _unpack_small(s_m, sizes), _unpack_small(s_v, sizes)

    gw_in, dw_in, mw_in, vw_in = _adamw_w_in(jnp.concatenate([chip, core]), sums[0], theirs[0], sums[1], theirs[1],
                                             jnp.transpose(w_in[0]), jnp.transpose(m_w_in[0]), jnp.transpose(v_w_in[0]))
    big = {}
    for t, nm, w, m, v in ((2, "w_mem_kv", w_mem_kv, m_w_mem_kv, v_w_mem_kv),
                           (3, "w_branch_a", w_branch_a, m_w_branch_a, v_w_branch_a),
                           (4, "w_branch_b", w_branch_b, m_w_branch_b, v_w_branch_b),
                           (5, "w_branch_c", w_branch_c, m_w_branch_c, v_w_branch_c),
                           (6, "w_out", w_out, m_w_out, v_w_out)):
        big[nm] = _adamw_halves(sums[t], theirs[t], core, w[0], m[0], v[0], axis=HALF_AXIS[t], tr=128,
                                name="adamw_" + nm)

    def collect(kind):
        sm = (g_small, d_small, m_small, v_small)[kind]
        win = (gw_in, dw_in, mw_in, vw_in)[kind]
        return ([sm[0], sm[1], jnp.transpose(win)[None]] + [a for a in sm[2:10]]
                + [big[n][kind][None] for n in ("w_mem_kv", "w_branch_a", "w_branch_b", "w_branch_c", "w_out")])

    return (loss, r["grad_x"][None], *collect(0), *collect(1), *collect(2), *collect(3))
```

```python
import numpy as np
import jax
import jax.numpy as jnp
from jax import lax
from jax.experimental import pallas as pl
from jax.experimental.pallas import tpu as pltpu

F32 = jnp.float32
BF16 = jnp.bfloat16
HI = lax.Precision.HIGHEST
SDS = jax.ShapeDtypeStruct
MESH = pl.DeviceIdType.MESH

D_MODEL = 2048
HEAD_DIM = 64
A_HEADS = 12
A_GROUP = 3
B_HEADS = 12
C_HEADS = 4
C_HEAD_DIM = 128
WINDOW = 128
EPS = 1e-6
NEG = -1e30
LANE = 128

QA, KA, VA, ZA = 0, 768, 1024, 1280
QB, KB, VB, ZB = 2048, 2816, 3584, 4352
QC, ZC = 5120, 5632
GATE = 6144
P_MAIN = 12288
N_FORGET = 12
FORGET_COL = 5120
SHARD_COLS = 3075
SLAB = 3200
SLAB_START = (0, 3072, 6016, 9088)
SLAB_SHIFT = (0, 3, 122, 125)
N_CHIPS = 4

ADAM_LR = 0.001
ADAM_B1 = 0.9
ADAM_B2 = 0.999
ADAM_EPS = 1e-08
ADAM_WD = 0.01
ADAM_STEP = 10

VMEM_LIMIT = 56 * 1024 * 1024
VMEM_WIDE = 62 * 1024 * 1024


def _params(sem, vmem=VMEM_LIMIT):
    return pltpu.CompilerParams(dimension_semantics=sem, vmem_limit_bytes=vmem)


def _win(tr, width, off):
    return pl.BlockSpec((pl.Element(tr), pl.Element(width)), lambda i, *_: (i * tr, off))


def _rowblk(tr, width):
    return pl.BlockSpec((tr, width), lambda i, *_: (i, 0))


def _const(shape):
    nd = len(shape)
    return pl.BlockSpec(shape, lambda *_: (0,) * nd)


def _rms(x, g):
    return x * lax.rsqrt(jnp.mean(x * x, axis=-1, keepdims=True) + EPS) * g


def _head_mean_impl(x2, bd):
    hi = x2.astype(BF16)
    lo = (x2 - hi.astype(F32)).astype(BF16)
    return _dot(hi, bd) + _dot(lo, bd)


@jax.custom_vjp
def _head_mean(x2, bd):
    return _head_mean_impl(x2, bd)


_head_mean.defvjp(lambda x2, bd: (_head_mean_impl(x2, bd), bd),
                  lambda bd, g: (_head_mean_impl(g, bd), jnp.zeros_like(bd)))


def _head_norm(x, g_tiled, bd):
    return x * lax.rsqrt(_head_mean(x * x, bd) + EPS) * g_tiled


def _silu(z):
    return z * jax.nn.sigmoid(z)


def _dot_nt(a, b):
    return lax.dot_general(a, b, (((1,), (1,)), ((), ())), preferred_element_type=F32)


def _dot_tn(a, b):
    return lax.dot_general(a, b, (((0,), (0,)), ((), ())), preferred_element_type=F32)


def _dot(a, b):
    return jnp.dot(a, b, preferred_element_type=F32)


def _swa_fn(qk, vz, qkp, vzp, qg, kg, sinks, bd, bias, first):
    q = _head_norm(qk[:, :768], qg, bd)
    k2 = jnp.concatenate([qkp[:, 768:], qk[:, 768:]], axis=0)
    k2 = _head_norm(k2, kg, bd[:256, :256])
    v2 = jnp.concatenate([vzp[:, :256], vz[:, :256]], axis=0)
    z = vz[:, 256:]
    cols = A_GROUP * WINDOW
    kj = lax.broadcasted_iota(jnp.int32, (2 * WINDOW, cols), 0)
    no_prev = kj < WINDOW * first.astype(jnp.int32)
    qtb = jnp.transpose(q).astype(BF16)
    kb = k2.astype(BF16)
    vtb = jnp.transpose(v2).astype(BF16)
    outs = [None] * A_HEADS
    for g in range(A_HEADS // A_GROUP):
        heads = [A_GROUP * g + u for u in range(A_GROUP)]
        qs = jnp.concatenate([qtb[64 * h:64 * h + 64, :] for h in heads], axis=1)
        s = _dot(kb[:, 64 * g:64 * g + 64], qs) * (HEAD_DIM ** -0.5) + bias[g]
        s = jnp.where(no_prev, NEG, s)
        sink = jnp.concatenate([jnp.broadcast_to(sinks[:, h:h + 1], (1, WINDOW)) for h in heads], axis=1)
        m = lax.stop_gradient(jnp.maximum(jnp.max(s, axis=0, keepdims=True), sink))
        p = jnp.exp(s - m)
        den = jnp.sum(p, axis=0, keepdims=True) + jnp.exp(sink - m)
        o = _dot(vtb[64 * g:64 * g + 64, :], (p * (1.0 / den)).astype(BF16))
        for u, h in enumerate(heads):
            outs[h] = o[:, WINDOW * u:WINDOW * u + WINDOW]
    return jnp.transpose(jnp.concatenate(outs, axis=0)) * _silu(z)


def _swa_bias():
    qi = np.arange(WINDOW)[None, :]
    kj = np.arange(2 * WINDOW)[:, None]
    rel = qi + WINDOW - kj
    valid = (rel >= 0) & (rel < WINDOW)
    out = np.zeros((A_HEADS // A_GROUP, 2 * WINDOW, A_GROUP * WINDOW), np.float32)
    for h in range(A_HEADS):
        slope = np.float32(2.0 ** (-8.0 * (h + 1) / A_HEADS))
        blk = np.where(valid, -slope * rel.astype(np.float32), np.float32(NEG))
        g, u = divmod(h, A_GROUP)
        out[g, :, WINDOW * u:WINDOW * u + WINDOW] = blk
    return jnp.asarray(out)


def _mem_fn(qz, mkv, qg, kg, bd):
    q = _head_norm(qz[:, :512], qg, bd).astype(BF16)
    k = _head_norm(mkv[:, :512], kg, bd).astype(BF16)
    v = mkv[:, 512:].astype(BF16)
    z = qz[:, 512:]
    outs = []
    for h in range(C_HEADS):
        sl = slice(128 * h, 128 * h + 128)
        s = _dot_nt(q[:, sl], k[:, sl]) * (C_HEAD_DIM ** -0.5)
        m = lax.stop_gradient(jnp.max(s, axis=-1, keepdims=True))
        p = jnp.exp(s - m)
        den = jnp.sum(p, axis=-1, keepdims=True)
        outs.append(_dot((p * (1.0 / den)).astype(BF16), v[:, sl]))
    return jnp.concatenate(outs, axis=1) * _silu(z)


def _qn_fn(q, g, bd):
    return _head_norm(q, g, bd) * (HEAD_DIM ** -0.5)


def _kn_fn(k, g, bd):
    return _head_norm(k, g, bd)


def _block_diag(width, hd):
    i = np.arange(width) // hd
    return jnp.asarray((i[:, None] == i[None, :]).astype(np.float32) / hd, BF16)


def _head_sum(width, hd):
    i = np.arange(width) // hd
    return jnp.asarray((i[:, None] == np.arange(LANE)[None, :]).astype(np.float32))


def _rms_fwd(x, g, *, tr, name):
    rows, dm = x.shape

    def body(x_ref, g_ref, o_ref):
        o_ref[...] = _rms(x_ref[...], g_ref[...]).astype(BF16)

    return pl.pallas_call(
        body, grid=(rows // tr,),
        in_specs=[_rowblk(tr, dm), _const((1, dm))],
        out_specs=_rowblk(tr, dm),
        out_shape=SDS((rows, dm), BF16), name=name,
        compiler_params=_params(("parallel",)))(x, g)


def _rms_bwd(x, g, dy, resid, *, tr, name, comm=None):
    rows, dm = x.shape
    want_dx = resid is not None
    n_in = 4 if want_dx else 3
    n_out = 2 if want_dx else 1
    c_in = len(comm.ins) if comm else 0
    c_out = len(comm.out_shapes) if comm else 0
    nb = rows // tr

    def body(*refs):
        x_ref, g_ref, dy_ref = refs[:3]
        r_ref = refs[3] if want_dx else None
        cin = refs[n_in:n_in + c_in]
        outs = refs[n_in + c_in:n_in + c_in + n_out]
        dg_ref = outs[-1]
        cout = refs[n_in + c_in + n_out:n_in + c_in + n_out + c_out]
        csem = refs[n_in + c_in + n_out + c_out:]

        if comm:
            @pl.when(pl.program_id(0) == 0)
            def _():
                comm.start(cin, cout, csem)

        _, vjp = jax.vjp(_rms, x_ref[...], g_ref[...])
        dx, dg = vjp(dy_ref[...])

        @pl.when(pl.program_id(0) == 0)
        def _():
            dg_ref[...] = jnp.zeros_like(dg_ref)

        dg_ref[...] += dg
        if want_dx:
            outs[0][...] = r_ref[...] + dx

        if comm:
            @pl.when(pl.program_id(0) == nb - 1)
            def _():
                comm.finish(cin, cout, csem)

    hbm = pl.BlockSpec(memory_space=pl.ANY)
    ins = [x, g, dy] + ([resid] if want_dx else []) + (list(comm.ins) if comm else [])
    in_specs = ([_rowblk(tr, dm), _const((1, dm)), _rowblk(tr, dm)] + ([_rowblk(tr, dm)] if want_dx else [])
                + [hbm] * c_in)
    out_specs = ([_rowblk(tr, dm)] if want_dx else []) + [_const((1, dm))] + [hbm] * c_out
    out_shape = (([SDS((rows, dm), F32)] if want_dx else []) + [SDS((1, dm), F32)]
                 + (list(comm.out_shapes) if comm else []))
    res = pl.pallas_call(
        body, grid=(nb,), in_specs=in_specs, out_specs=out_specs, out_shape=out_shape,
        scratch_shapes=list(comm.sems) if comm else [], name=name, compiler_params=_params(("arbitrary",)))(*ins)
    return (list(res[:n_out]), list(res[n_out:])) if comm else res


class _Comm:
    def __init__(self, ins, out_shapes, sems, start, finish):
        self.ins, self.out_shapes, self.sems, self.start, self.finish = list(ins), list(out_shapes), list(sems), start, finish


def _matmul(a, b, *, dims, out_dtype, tm, tn, tk, name, add=None, comms=(), vmem=VMEM_LIMIT):
    a_list = list(a) if isinstance(a, (list, tuple)) else [a]
    b_list = list(b) if isinstance(b, (list, tuple)) else [b]
    assert len(a_list) == 1 or dims == "nt"
    assert len(b_list) == 1 or dims == "tn"
    if dims == "tn":
        kdim, m = a_list[0].shape
    else:
        m, kdim = a_list[0].shape[0], sum(p.shape[1] for p in a_list)
    n = b_list[0].shape[0] if dims == "nt" else sum(p.shape[1] for p in b_list)
    tm, tn, tk = min(tm, m), min(tn, n), min(tk, kdim)
    assert m % tm == 0 and n % tn == 0 and kdim % tk == 0, (name, m, n, kdim)
    ni, nj, nk = m // tm, n // tn, kdim // tk
    a_rng, b_rng, pos = [], [], 0
    for p in a_list:
        assert len(a_list) == 1 or p.shape[1] % tk == 0
        a_rng.append((pos, p.shape[1] // tk if len(a_list) > 1 else nk))
        pos += a_rng[-1][1]
    pos = 0
    for p in b_list:
        assert len(b_list) == 1 or p.shape[1] % tn == 0
        b_rng.append((pos, p.shape[1] // tn if len(b_list) > 1 else nj))
        pos += b_rng[-1][1]
    has_add = add is not None
    n_mm_in = len(a_list) + len(b_list) + (1 if has_add else 0)
    c_in = [len(c.ins) for c in comms]
    c_out = [len(c.out_shapes) for c in comms]
    c_sem = [len(c.sems) for c in comms]

    def body(*refs):
        a_refs, b_refs = refs[:len(a_list)], refs[len(a_list):len(a_list) + len(b_list)]
        add_ref = refs[n_mm_in - 1] if has_add else None
        pos = n_mm_in
        cin = []
        for cnt in c_in:
            cin.append(refs[pos:pos + cnt])
            pos += cnt
        o_ref = refs[pos]
        pos += 1
        cout = []
        for cnt in c_out:
            cout.append(refs[pos:pos + cnt])
            pos += cnt
        acc = refs[pos]
        pos += 1
        csem = []
        for cnt in c_sem:
            csem.append(refs[pos:pos + cnt])
            pos += cnt
        i, j, k = pl.program_id(0), pl.program_id(1), pl.program_id(2)

        if comms:
            @pl.when((i == 0) & (j == 0) & (k == 0))
            def _():
                for c, ci, co, cs in zip(comms, cin, cout, csem):
                    c.start(ci, co, cs)

        def accumulate(a_ref, b_ref, first_k, later_k):
            if dims == "nn":
                part = _dot(a_ref[...], b_ref[...])
            elif dims == "nt":
                part = _dot_nt(a_ref[...], b_ref[...])
            else:
                part = _dot_tn(a_ref[...], b_ref[...])

            if first_k:
                @pl.when(k == 0)
                def _():
                    acc[...] = part + add_ref[...] if has_add else part

            if later_k:
                @pl.when(k > 0)
                def _():
                    acc[...] += part

        if len(a_list) > 1:
            for a_ref, (k0, cnt) in zip(a_refs, a_rng):
                @pl.when((k >= k0) & (k < k0 + cnt))
                def _(a_ref=a_ref, k0=k0, cnt=cnt):
                    accumulate(a_ref, b_refs[0], k0 == 0, k0 + cnt > 1)
        elif len(b_list) > 1:
            for b_ref, (j0, cnt) in zip(b_refs, b_rng):
                @pl.when((j >= j0) & (j < j0 + cnt))
                def _(b_ref=b_ref):
                    accumulate(a_refs[0], b_ref, True, nk > 1)
        else:
            accumulate(a_refs[0], b_refs[0], True, nk > 1)

        @pl.when(k == nk - 1)
        def _():
            o_ref[...] = acc[...].astype(out_dtype)

        if comms:
            @pl.when((i == ni - 1) & (j == nj - 1) & (k == nk - 1))
            def _():
                for c, ci, co, cs in zip(comms, cin, cout, csem):
                    c.finish(ci, co, cs)

    def a_spec(k0, cnt):
        if dims == "tn":
            return pl.BlockSpec((tk, tm), lambda i, j, k: (k, i))
        return pl.BlockSpec((tm, tk), lambda i, j, k: (i, jnp.clip(k - k0, 0, cnt - 1)))

    def b_spec(j0, cnt):
        if dims == "nt":
            return pl.BlockSpec((tn, tk), lambda i, j, k: (j, k))
        return pl.BlockSpec((tk, tn), lambda i, j, k: (k, jnp.clip(j - j0, 0, cnt - 1)))

    o_spec = pl.BlockSpec((tm, tn), lambda i, j, k: (i, j))
    hbm = pl.BlockSpec(memory_space=pl.ANY)
    ins = a_list + b_list + ([add] if has_add else []) + [x for c in comms for x in c.ins]
    in_specs = ([a_spec(*r) for r in a_rng] + [b_spec(*r) for r in b_rng] + ([o_spec] if has_add else [])
                + [hbm] * sum(c_in))
    out_specs = [o_spec] + [hbm] * sum(c_out)
    out_shape = [SDS((m, n), out_dtype)] + [s for c in comms for s in c.out_shapes]
    scratch = [pltpu.VMEM((tm, tn), F32)] + [s for c in comms for s in c.sems]
    sem = ("arbitrary",) * 3 if comms else ("parallel", "parallel", "arbitrary")
    res = pl.pallas_call(
        body, grid=(ni, nj, nk), in_specs=in_specs, out_specs=out_specs, out_shape=out_shape, scratch_shapes=scratch,
        name=name, compiler_params=_params(sem, vmem))(*ins)
    if not comms:
        return res[0]
    outs, pos = [], 1
    for cnt in c_out:
        outs.append(list(res[pos:pos + cnt]))
        pos += cnt
    return res[0], outs


def _swa_specs(nb, blk=lambda n: n):
    cur = lambda off: pl.BlockSpec((pl.Element(WINDOW), pl.Element(1024)), lambda n: (blk(n) * WINDOW, off))
    prev = lambda off: pl.BlockSpec((pl.Element(WINDOW), pl.Element(1024)),
                                    lambda n: (jnp.maximum(blk(n) - 1, 0) * WINDOW, off))
    return [cur(QA), cur(VA), prev(QA), prev(VA),
            _const((1, 768)), _const((1, 256)), _const((1, LANE)), _const((768, 768)),
            _const((A_HEADS // A_GROUP, 2 * WINDOW, A_GROUP * WINDOW))]


def _swa_fwd(proj, qg, kg, sinks, bd, bias):
    s = proj.shape[0]
    nb = s // WINDOW

    def body(qk_ref, vz_ref, qkp_ref, vzp_ref, qg_ref, kg_ref, sk_ref, bd_ref, bias_ref, o_ref):
        first = pl.program_id(0) == 0
        o_ref[...] = _swa_fn(qk_ref[...], vz_ref[...], qkp_ref[...], vzp_ref[...], qg_ref[...], kg_ref[...],
                             sk_ref[...], bd_ref[...], bias_ref[...], first).astype(BF16)

    return pl.pallas_call(
        body, grid=(nb,), in_specs=_swa_specs(nb), out_specs=_rowblk(WINDOW, 768),
        out_shape=SDS((s, 768), BF16), name="swa_fwd",
        compiler_params=_params(("parallel",)))(proj, proj, proj, proj, qg, kg, sinks, bd, bias)


def _swa_bwd(proj, qg, kg, sinks, bd, bias, dga):
    s = proj.shape[0]
    nb = s // WINDOW
    blk = lambda n: nb - 1 - n

    def body(qk_ref, vz_ref, qkp_ref, vzp_ref, qg_ref, kg_ref, sk_ref, bd_ref, bias_ref, dg_ref,
             d_ref, dqg_ref, dkg_ref, dsk_ref, carry):
        first = blk(pl.program_id(0)) == 0
        bd_v = bd_ref[...]
        bias_v = bias_ref[...]
        fn = lambda qk, vz, qkp, vzp, qg_, kg_, sk: _swa_fn(qk, vz, qkp, vzp, qg_, kg_, sk, bd_v, bias_v, first)
        _, vjp = jax.vjp(fn, qk_ref[...], vz_ref[...], qkp_ref[...], vzp_ref[...], qg_ref[...], kg_ref[...], sk_ref[...])
        dqk, dvz, dqkp, dvzp, dqg, dkg, dsk = vjp(dg_ref[...])

        @pl.when(pl.program_id(0) == 0)
        def _():
            dqg_ref[...] = jnp.zeros_like(dqg_ref)
            dkg_ref[...] = jnp.zeros_like(dkg_ref)
            dsk_ref[...] = jnp.zeros_like(dsk_ref)
            carry[...] = jnp.zeros_like(carry)

        dqg_ref[...] += dqg
        dkg_ref[...] += dkg
        dsk_ref[...] += dsk
        kv = jnp.concatenate([dqk[:, 768:], dvz[:, :256]], axis=1) + carry[...]
        d_ref[...] = jnp.concatenate([dqk[:, :768], kv, dvz[:, 256:]], axis=1).astype(BF16)
        carry[...] = jnp.concatenate([dqkp[:, 768:], dvzp[:, :256]], axis=1)

    rev = lambda w: pl.BlockSpec((WINDOW, w), lambda n: (blk(n), 0))
    return pl.pallas_call(
        body, grid=(nb,), in_specs=_swa_specs(nb, blk) + [rev(768)],
        out_specs=[rev(2048), _const((1, 768)), _const((1, 256)), _const((1, LANE))],
        out_shape=[SDS((s, 2048), BF16), SDS((1, 768), F32), SDS((1, 256), F32), SDS((1, LANE), F32)],
        scratch_shapes=[pltpu.VMEM((WINDOW, 512), F32)],
        name="swa_bwd", compiler_params=_params(("arbitrary",)))(proj, proj, proj, proj, qg, kg, sinks, bd, bias, dga)


def _mem_fwd(proj, mkv, qg, kg, bd, *, tr):
    s = proj.shape[0]

    def body(qz_ref, mkv_ref, qg_ref, kg_ref, bd_ref, o_ref):
        o_ref[...] = _mem_fn(qz_ref[...], mkv_ref[...], qg_ref[...], kg_ref[...], bd_ref[...]).astype(BF16)

    return pl.pallas_call(
        body, grid=(s // tr,),
        in_specs=[_win(tr, 1024, QC), _const(mkv.shape), _const((1, 512)), _const((1, 512)), _const((512, 512))],
        out_specs=_rowblk(tr, 512), out_shape=SDS((s, 512), BF16), name="mem_fwd",
        compiler_params=_params(("parallel",)))(proj, mkv, qg, kg, bd)


def _mem_bwd(proj, mkv, qg, kg, bd, dgc, *, tr):
    s = proj.shape[0]

    def body(qz_ref, mkv_ref, qg_ref, kg_ref, bd_ref, dg_ref, dqz_ref, dmkv_ref, dqg_ref, dkg_ref):
        bd_v = bd_ref[...]
        fn = lambda qz, mkv_, qg_, kg_: _mem_fn(qz, mkv_, qg_, kg_, bd_v)
        _, vjp = jax.vjp(fn, qz_ref[...], mkv_ref[...], qg_ref[...], kg_ref[...])
        dqz, dmkv, dqg, dkg = vjp(dg_ref[...])

        @pl.when(pl.program_id(0) == 0)
        def _():
            dmkv_ref[...] = jnp.zeros_like(dmkv_ref)
            dqg_ref[...] = jnp.zeros_like(dqg_ref)
            dkg_ref[...] = jnp.zeros_like(dkg_ref)

        dmkv_ref[...] += dmkv
        dqg_ref[...] += dqg
        dkg_ref[...] += dkg
        dqz_ref[...] = dqz.astype(BF16)

    return pl.pallas_call(
        body, grid=(s // tr,),
        in_specs=[_win(tr, 1024, QC), _const(mkv.shape), _const((1, 512)), _const((1, 512)), _const((512, 512)),
                  _rowblk(tr, 512)],
        out_specs=[_rowblk(tr, 1024), _const(mkv.shape), _const((1, 512)), _const((1, 512))],
        out_shape=[SDS((s, 1024), BF16), SDS(mkv.shape, F32), SDS((1, 512), F32), SDS((1, 512), F32)],
        name="mem_bwd", compiler_params=_params(("arbitrary",)))(proj, mkv, qg, kg, bd, dgc)


def _log_sigmoid(x):
    return jnp.minimum(x, 0.0) - jnp.log1p(jnp.exp(-jnp.abs(x)))


FOX_TQ, FOX_TK = 512, 512
FOX_FWD_TQ, FOX_FWD_TK = 512, 1024


def _fox_tiles(s):
    return min(FOX_TQ, s), min(FOX_TK, s)


AUG = 128 * B_HEADS
COL_A, COL_B = 64, 67


def _split3(c):
    hi = c.astype(BF16)
    r1 = c - hi.astype(F32)
    mid = r1.astype(BF16)
    lo = (r1 - mid.astype(F32)).astype(BF16)
    return hi, mid, lo


def _expand_mats():
    def mat(col0):
        e = np.zeros((768 + 3 * LANE, AUG), np.float32)
        for h in range(B_HEADS):
            for d in range(HEAD_DIM):
                e[64 * h + d, 128 * h + d] = 1.0
            for part in range(3):
                e[768 + LANE * part + h, 128 * h + col0 + part] = 1.0
        return e

    def ones(col0):
        o = np.zeros((1, AUG), np.float32)
        for h in range(B_HEADS):
            o[0, 128 * h + col0:128 * h + col0 + 3] = 1.0
        return o

    return (jnp.asarray(mat(COL_A), BF16), jnp.asarray(mat(COL_B), BF16), jnp.asarray(ones(COL_A)), jnp.asarray(ones(COL_B)))


def _augment(data_bf16, triple, emat, ones_row):
    parts = [data_bf16] + (list(triple) if triple is not None else [jnp.zeros((data_bf16.shape[0], LANE), BF16)] * 3)
    wide = _dot(jnp.concatenate(parts, axis=1), emat)
    if ones_row is not None:
        wide = wide + ones_row
    return wide


def _compact(wide):
    return jnp.concatenate([wide[:, 128 * h:128 * h + 64] for h in range(wide.shape[1] // 128)], axis=1)


def _lane_of_heads(wide, col, first=0):
    rows = wide.shape[0]
    lane = lax.broadcasted_iota(jnp.int32, (rows, LANE), 1)
    out = jnp.zeros((rows, LANE), F32)
    for h in range(wide.shape[1] // 128):
        out = jnp.where(lane == first + h, wide[:, 128 * h + col:128 * h + col + 1], out)
    return out


def _fox2_prep(proj, fbl, qg, kg, bfor, bd, ea, eb, ones_a, ones_b, *, tr):
    s = proj.shape[0]
    tri = jnp.asarray(np.tril(np.ones((tr, tr), np.float32)))

    def body(q_ref, k_ref, v_ref, fb_ref, qg_ref, kg_ref, bf_ref, bd_ref, tri_ref, ea_ref, eb_ref, oa_ref, ob_ref,
             qat_ref, ka_ref, kat_ref, va_ref, vat_ref, qn_ref, c_ref, carry):
        @pl.when(pl.program_id(0) == 0)
        def _():
            carry[...] = jnp.zeros_like(carry)

        bd_v = bd_ref[...]
        lane = lax.broadcasted_iota(jnp.int32, (tr, LANE), 1)
        logf = jnp.where(lane < N_FORGET, _log_sigmoid(fb_ref[...] + bf_ref[...]), 0.0)
        c = jnp.dot(tri_ref[...], logf, precision=HI, preferred_element_type=F32) + carry[...]
        c_ref[...] = c
        carry[...] = c[tr - 1:tr, :]
        qn = _qn_fn(q_ref[...], qg_ref[...], bd_v).astype(BF16)
        kn = _kn_fn(k_ref[...], kg_ref[...], bd_v).astype(BF16)
        qn_ref[...] = qn
        qat_ref[...] = jnp.transpose(_augment(qn, _split3(c), ea_ref[...], ob_ref[...])).astype(BF16)
        ka = _augment(kn, _split3(-c), eb_ref[...], oa_ref[...])
        ka_ref[...] = ka.astype(BF16)
        kat_ref[...] = jnp.transpose(ka).astype(BF16)
        va = _augment(v_ref[...].astype(BF16), None, ea_ref[...], oa_ref[...])
        va_ref[...] = va.astype(BF16)
        vat_ref[...] = jnp.transpose(va).astype(BF16)

    emat = _const((768 + 3 * LANE, AUG))
    return pl.pallas_call(
        body, grid=(s // tr,),
        in_specs=[_win(tr, 768, QB), _win(tr, 768, KB), _win(tr, 768, VB), _rowblk(tr, LANE), _const((1, 768)),
                  _const((1, 768)), _const((1, LANE)), _const((768, 768)), _const((tr, tr)), emat, emat,
                  _const((1, AUG)), _const((1, AUG))],
        out_specs=[pl.BlockSpec((AUG, tr), lambda i: (0, i)), _rowblk(tr, AUG), pl.BlockSpec((AUG, tr), lambda i: (0, i)),
                   _rowblk(tr, AUG), pl.BlockSpec((AUG, tr), lambda i: (0, i)), _rowblk(tr, 768), _rowblk(tr, LANE)],
        out_shape=[SDS((AUG, s), BF16), SDS((s, AUG), BF16), SDS((AUG, s), BF16), SDS((s, AUG), BF16),
                   SDS((AUG, s), BF16), SDS((s, 768), BF16), SDS((s, LANE), F32)],
        scratch_shapes=[pltpu.VMEM((1, LANE), F32)], name="fox_prep",
        compiler_params=_params(("arbitrary",)))(proj, proj, proj, fbl, qg, kg, bfor, bd, tri, ea, eb, ones_a, ones_b)


def _fox2_fwd(proj, qat, ka, vat):
    s = proj.shape[0]
    tq, tk = min(FOX_FWD_TQ, s), min(FOX_FWD_TK, s)
    nq, nk = s // tq, s // tk

    def last_k(i):
        return (i * tq + tq - 1) // tk

    def body(qt_ref, k_ref, vt_ref, z_ref, gb_ref, yb_ref, lse_ref, acc, m_s):
        i, j = pl.program_id(0), pl.program_id(1)

        @pl.when(j == 0)
        def _():
            acc[...] = jnp.zeros_like(acc)
            m_s[...] = jnp.full_like(m_s, NEG)

        def tile(masked):
            if masked:
                kpos = j * tk + lax.broadcasted_iota(jnp.int32, (tk, tq), 0)
                qpos = i * tq + lax.broadcasted_iota(jnp.int32, (tk, tq), 1)
                mask = kpos <= qpos
            for h in range(B_HEADS):
                sl = slice(128 * h, 128 * h + 128)
                sc = _dot(k_ref[:, sl], qt_ref[sl, :])
                if masked:
                    sc = jnp.where(mask, sc, NEG)
                m_prev = m_s[h:h + 1, :]
                m_new = jnp.maximum(m_prev, jnp.max(sc, axis=0, keepdims=True))
                p = jnp.exp(sc - m_new).astype(BF16)
                acc[sl, :] = jnp.exp(m_prev - m_new) * acc[sl, :] + _dot(vt_ref[sl, :], p)
                m_s[h:h + 1, :] = m_new

        full = j * tk + tk - 1 <= i * tq

        @pl.when(full)
        def _():
            tile(False)

        @pl.when(jnp.logical_and(jnp.logical_not(full), j <= last_k(i)))
        def _():
            tile(True)

        @pl.when(j == nk - 1)
        def _():
            outs = []
            row = lax.broadcasted_iota(jnp.int32, (LANE, tq), 0)
            lse_t = jnp.zeros((LANE, tq), F32)
            for h in range(B_HEADS):
                l_row = acc[128 * h + COL_A:128 * h + COL_A + 1, :]
                outs.append(acc[128 * h:128 * h + 64, :] * (1.0 / l_row))
                lse_t = jnp.where(row == h, m_s[h:h + 1, :] + jnp.log(l_row), lse_t)
            y = jnp.transpose(jnp.concatenate(outs, axis=0))
            yb_ref[...] = y
            gb_ref[...] = (y * _silu(z_ref[...])).astype(BF16)
            lse_ref[...] = jnp.transpose(lse_t)

    kcol = lambda i, j: (0, jnp.minimum(j, last_k(i)))
    return pl.pallas_call(
        body, grid=(nq, nk),
        in_specs=[pl.BlockSpec((AUG, tq), lambda i, j: (0, i)),
                  pl.BlockSpec((tk, AUG), lambda i, j: (jnp.minimum(j, last_k(i)), 0)),
                  pl.BlockSpec((AUG, tk), kcol),
                  pl.BlockSpec((pl.Element(tq), pl.Element(768)), lambda i, j: (i * tq, ZB))],
        out_specs=[pl.BlockSpec((tq, 768), lambda i, j: (i, 0)), pl.BlockSpec((tq, 768), lambda i, j: (i, 0)),
                   pl.BlockSpec((tq, LANE), lambda i, j: (i, 0))],
        out_shape=[SDS((s, 768), BF16), SDS((s, 768), F32), SDS((s, LANE), F32)],
        scratch_shapes=[pltpu.VMEM((AUG, tq), F32), pltpu.VMEM((16, tq), F32)],
        name="fox_fwd", compiler_params=_params(("parallel", "arbitrary")))(qat, ka, vat, proj)


def _fox2_bwd_pre(proj, yb, dgb, qn, c, lse, hsum, ea, ones_b, *, tr):
    s = proj.shape[0]

    def body(z_ref, y_ref, dg_ref, qn_ref, c_ref, lse_ref, hs_ref, ea_ref, ob_ref,
             qa_ref, qat_ref, dya_ref, dyat_ref, dz_ref):
        z, y, dg = z_ref[...], y_ref[...], dg_ref[...]
        sg = jax.nn.sigmoid(z)
        dy = dg * (z * sg)
        dz_ref[...] = (dg * y * (sg * (1.0 + z * (1.0 - sg)))).astype(BF16)
        delta = jnp.dot(dy * y, hs_ref[...], precision=HI, preferred_element_type=F32)
        e = ea_ref[...]
        dya = _augment(dy.astype(BF16), _split3(-delta), e, None)
        dya_ref[...] = dya.astype(BF16)
        dyat_ref[...] = jnp.transpose(dya).astype(BF16)
        qa = _augment(qn_ref[...], _split3(c_ref[...] - lse_ref[...]), e, ob_ref[...])
        qa_ref[...] = qa.astype(BF16)
        qat_ref[...] = jnp.transpose(qa).astype(BF16)

    return pl.pallas_call(
        body, grid=(s // tr,),
        in_specs=[_win(tr, 768, ZB), _rowblk(tr, 768), _rowblk(tr, 768), _rowblk(tr, 768), _rowblk(tr, LANE),
                  _rowblk(tr, LANE), _const((768, LANE)), _const((768 + 3 * LANE, AUG)), _const((1, AUG))],
        out_specs=[_rowblk(tr, AUG), pl.BlockSpec((AUG, tr), lambda i: (0, i)), _rowblk(tr, AUG),
                   pl.BlockSpec((AUG, tr), lambda i: (0, i)), _rowblk(tr, 768)],
        out_shape=[SDS((s, AUG), BF16), SDS((AUG, s), BF16), SDS((s, AUG), BF16), SDS((AUG, s), BF16),
                   SDS((s, 768), BF16)], name="fox_bwd_pre",
        compiler_params=_params(("parallel",)))(proj, yb, dgb, qn, c, lse, hsum, ea, ones_b)


def _fox2_bwd(qb, qbt, ka, kat, va, dya, dyat):
    s = qb.shape[0]
    tq, tk = _fox_tiles(s)
    nq, nk = s // tq, s // tk
    ng = 2
    gh = B_HEADS // ng
    gw = 128 * gh

    def first_q(j):
        return (j * tk) // tq

    def body(q_ref, qt_ref, k_ref, kt_ref, v_ref, dy_ref, dyt_ref, dq_hbm, dk_ref, dv_ref, dck_ref,
             dq_acc, dk_acc, dv_acc, sem):
        g, j, i = pl.program_id(0), pl.program_id(1), pl.program_id(2)

        @pl.when((j == 0) & (i == 0))
        def _():
            dq_acc[...] = jnp.zeros_like(dq_acc)

        @pl.when(i == 0)
        def _():
            dk_acc[...] = jnp.zeros_like(dk_acc)
            dv_acc[...] = jnp.zeros_like(dv_acc)

        def tile(masked):
            if masked:
                kpos = j * tk + lax.broadcasted_iota(jnp.int32, (tk, tq), 0)
                qpos = i * tq + lax.broadcasted_iota(jnp.int32, (tk, tq), 1)
                mask = kpos <= qpos
            cols = pl.ds(pl.multiple_of(i * tq, tq), tq)
            for h in range(gh):
                sl = slice(128 * h, 128 * h + 128)
                sc = _dot(k_ref[:, sl], qt_ref[sl, :])
                if masked:
                    sc = jnp.where(mask, sc, NEG)
                p = jnp.exp(sc)
                ds = (p * _dot(v_ref[:, sl], dyt_ref[sl, :])).astype(BF16)
                dv_acc[:, sl] += _dot(p.astype(BF16), dy_ref[:, sl])
                dk_acc[:, sl] += _dot(ds, q_ref[:, sl])
                dq_acc[sl, cols] += _dot(kt_ref[sl, :], ds)

        full = j * tk + tk - 1 <= i * tq

        @pl.when(full)
        def _():
            tile(False)

        @pl.when(jnp.logical_and(jnp.logical_not(full), i >= first_q(j)))
        def _():
            tile(True)

        @pl.when(i == nq - 1)
        def _():
            dkw = dk_acc[...]
            dk_ref[...] = _compact(dkw)
            dv_ref[...] = _compact(dv_acc[...]).astype(BF16)
            dck_ref[...] = -_lane_of_heads(dkw, COL_B, gh * g)

        @pl.when((j == nk - 1) & (i == nq - 1))
        def _():
            cp = pltpu.make_async_copy(dq_acc, dq_hbm.at[pl.ds(pl.multiple_of(g * gw, gw), gw)], sem)
            cp.start()
            cp.wait()

    qrow = pl.BlockSpec((tq, gw), lambda g, j, i: (jnp.maximum(i, first_q(j)), g))
    qcol = pl.BlockSpec((gw, tq), lambda g, j, i: (g, jnp.maximum(i, first_q(j))))
    krow = pl.BlockSpec((tk, gw), lambda g, j, i: (j, g))
    kcol = pl.BlockSpec((gw, tk), lambda g, j, i: (g, j))
    kout = pl.BlockSpec((tk, gw // 2), lambda g, j, i: (j, g))
    return pl.pallas_call(
        body, grid=(ng, nk, nq),
        in_specs=[qrow, qcol, krow, kcol, krow, qrow, qcol],
        out_specs=[pl.BlockSpec(memory_space=pl.ANY), kout, kout,
                   pl.BlockSpec((None, tk, LANE), lambda g, j, i: (g, j, 0))],
        out_shape=[SDS((AUG, s), F32), SDS((s, 768), F32), SDS((s, 768), BF16), SDS((ng, s, LANE), F32)],
        scratch_shapes=[pltpu.VMEM((gw, s), F32), pltpu.VMEM((tk, gw), F32), pltpu.VMEM((tk, gw), F32),
                        pltpu.SemaphoreType.DMA],
        name="fox_bwd", compiler_params=_params(("arbitrary",) * 3))(qb, qbt, ka, kat, va, dya, dyat)


def _fox2_bwd_post(proj, fbl, qg, kg, bfor, bd, dqa, dkn, dck, *, tr):
    s = proj.shape[0]
    nb = s // tr
    triu = jnp.asarray(np.triu(np.ones((tr, tr), np.float32)))
    rev = lambda i: nb - 1 - i

    def body(q_ref, k_ref, fb_ref, qg_ref, kg_ref, bf_ref, bd_ref, tri_ref, dqa_ref, dkn_ref, dck_ref,
             dq_ref, dk_ref, dfb_ref, dqg_ref, dkg_ref, dbf_ref, carry):
        @pl.when(pl.program_id(0) == 0)
        def _():
            carry[...] = jnp.zeros_like(carry)
            dqg_ref[...] = jnp.zeros_like(dqg_ref)
            dkg_ref[...] = jnp.zeros_like(dkg_ref)
            dbf_ref[...] = jnp.zeros_like(dbf_ref)

        bd_v = bd_ref[...]
        dqw = jnp.transpose(dqa_ref[...])
        _, vjp_q = jax.vjp(lambda q, g: _qn_fn(q, g, bd_v), q_ref[...], qg_ref[...])
        dq, dqg = vjp_q(_compact(dqw))
        _, vjp_k = jax.vjp(lambda k, g: _kn_fn(k, g, bd_v), k_ref[...], kg_ref[...])
        dk, dkg = vjp_k(dkn_ref[...])
        dq_ref[...] = dq.astype(BF16)
        dk_ref[...] = dk.astype(BF16)
        dqg_ref[...] += dqg
        dkg_ref[...] += dkg

        dc = _lane_of_heads(dqw, COL_A) + (dck_ref[0] + dck_ref[1])
        dlogf = jnp.dot(tri_ref[...], dc, precision=HI, preferred_element_type=F32) + carry[...]
        carry[...] = dlogf[0:1, :]
        lane = lax.broadcasted_iota(jnp.int32, (tr, LANE), 1)
        xf = fb_ref[...] + bf_ref[...]
        dfb = jnp.where(lane < N_FORGET, dlogf * jax.nn.sigmoid(-xf), 0.0)
        dfb_ref[...] = dfb.astype(BF16)
        dbf_ref[...] += jnp.sum(dfb, axis=0, keepdims=True)

    rb = lambda w: pl.BlockSpec((tr, w), lambda i: (rev(i), 0))
    wn = lambda w, off: pl.BlockSpec((pl.Element(tr), pl.Element(w)), lambda i: (rev(i) * tr, off))
    return pl.pallas_call(
        body, grid=(nb,),
        in_specs=[wn(768, QB), wn(768, KB), rb(LANE), _const((1, 768)), _const((1, 768)), _const((1, LANE)),
                  _const((768, 768)), _const((tr, tr)), pl.BlockSpec((AUG, tr), lambda i: (0, rev(i))), rb(768),
                  pl.BlockSpec((2, tr, LANE), lambda i: (0, rev(i), 0))],
        out_specs=[rb(768), rb(768), rb(LANE), _const((1, 768)), _const((1, 768)), _const((1, LANE))],
        out_shape=[SDS((s, 768), BF16), SDS((s, 768), BF16), SDS((s, LANE), BF16), SDS((1, 768), F32),
                   SDS((1, 768), F32), SDS((1, LANE), F32)],
        scratch_shapes=[pltpu.VMEM((1, LANE), F32)], name="fox_bwd_post",
        compiler_params=_params(("arbitrary",)))(proj, proj, fbl, qg, kg, bfor, bd, triu, dqa, dkn, dck)


def _merge_specs(tr):
    row = lambda w: pl.BlockSpec((tr, w), lambda i, j: (i, 0))
    shard = lambda r: pl.BlockSpec((None, r, 512), lambda i, j: (j, 0, 0))
    gate = lambda b: pl.BlockSpec((tr, 512), lambda i, j: (i, (GATE + 2048 * b) // 512 + j))
    return [row(768), row(768), row(512), shard(768), shard(768), shard(512), gate(0), gate(1), gate(2)]


def _merge_fwd(proj, ga, gb, gc, wa, wb, wc, *, tr):
    s = proj.shape[0]

    def body(ga_ref, gb_ref, gc_ref, wa_ref, wb_ref, wc_ref, l0_ref, l1_ref, l2_ref, y_ref):
        ua = _dot(ga_ref[...], wa_ref[...])
        ub = _dot(gb_ref[...], wb_ref[...])
        uc = _dot(gc_ref[...], wc_ref[...])
        y = jax.nn.sigmoid(l0_ref[...]) * ua + jax.nn.sigmoid(l1_ref[...]) * ub + jax.nn.sigmoid(l2_ref[...]) * uc
        y_ref[...] = y.astype(BF16)

    return pl.pallas_call(
        body, grid=(s // tr, N_CHIPS), in_specs=_merge_specs(tr),
        out_specs=pl.BlockSpec((tr, 512), lambda i, j: (i, j)), out_shape=SDS((s, D_MODEL), BF16), name="merge_fwd",
        compiler_params=_params(("parallel", "arbitrary")))(ga, gb, gc, wa, wb, wc, proj, proj, proj)


def _merge_bwd(proj, ga, gb, gc, wa, wb, wc, dy, *, tr):
    s = proj.shape[0]

    def body(ga_ref, gb_ref, gc_ref, wa_ref, wb_ref, wc_ref, l0_ref, l1_ref, l2_ref, dy_ref,
             dl0_ref, dl1_ref, dl2_ref, dua_ref, dub_ref, duc_ref, dga_ref, dgb_ref, dgc_ref):
        j = pl.program_id(1)
        dyv = dy_ref[...]

        @pl.when(j == 0)
        def _():
            dga_ref[...] = jnp.zeros_like(dga_ref)
            dgb_ref[...] = jnp.zeros_like(dgb_ref)
            dgc_ref[...] = jnp.zeros_like(dgc_ref)

        for g_ref, w_ref, l_ref, dl_ref, du_ref, dg_ref in (
                (ga_ref, wa_ref, l0_ref, dl0_ref, dua_ref, dga_ref),
                (gb_ref, wb_ref, l1_ref, dl1_ref, dub_ref, dgb_ref),
                (gc_ref, wc_ref, l2_ref, dl2_ref, duc_ref, dgc_ref)):
            w = w_ref[...]
            u = _dot(g_ref[...], w)
            sg = jax.nn.sigmoid(l_ref[...])
            dl_ref[...] = (dyv * u * sg * (1.0 - sg)).astype(BF16)
            du = (dyv * sg).astype(BF16)
            du_ref[...] = du
            dg_ref[...] += _dot_nt(du, w)

    blk = pl.BlockSpec((tr, 512), lambda i, j: (i, j))
    row = lambda w: pl.BlockSpec((tr, w), lambda i, j: (i, 0))
    big = SDS((s, D_MODEL), BF16)
    return pl.pallas_call(
        body, grid=(s // tr, N_CHIPS), in_specs=_merge_specs(tr) + [blk],
        out_specs=[blk] * 6 + [row(768), row(768), row(512)],
        out_shape=[big] * 6 + [SDS((s, 768), F32), SDS((s, 768), F32), SDS((s, 512), F32)], name="merge_bwd",
        compiler_params=_params(("parallel", "arbitrary")))(ga, gb, gc, wa, wb, wc, proj, proj, proj, dy)


def _out_loss(y, wo, x, tgt, *, tr, tn):
    s = x.shape[0]

    def body(y_ref, w_ref, x_ref, t_ref, d_ref, db_ref, sq_ref, dy_ref):
        @pl.when((pl.program_id(0) == 0) & (pl.program_id(1) == 0))
        def _():
            sq_ref[...] = jnp.zeros_like(sq_ref)

        @pl.when(pl.program_id(1) == 0)
        def _():
            dy_ref[...] = jnp.zeros_like(dy_ref)

        w = w_ref[...]
        out = x_ref[...] + _dot(y_ref[...], w)
        diff = out - t_ref[...]
        sq_ref[...] += jnp.sum(diff * diff, axis=0, keepdims=True)
        d = diff * (1.0 / D_MODEL)
        d_ref[...] = d
        db = d.astype(BF16)
        db_ref[...] = db
        dy_ref[...] += _dot_nt(db, w)

    blk = pl.BlockSpec((tr, tn), lambda i, j: (i, j))
    row = pl.BlockSpec((tr, D_MODEL), lambda i, j: (i, 0))
    return pl.pallas_call(
        body, grid=(s // tr, D_MODEL // tn),
        in_specs=[row, pl.BlockSpec((D_MODEL, tn), lambda i, j: (0, j)), blk, blk],
        out_specs=[blk, blk, _const((1, tn)), row],
        out_shape=[SDS((s, D_MODEL), F32), SDS((s, D_MODEL), BF16), SDS((1, tn), F32), SDS((s, D_MODEL), F32)],
        name="out_loss", compiler_params=_params(("arbitrary", "arbitrary")))(y, wo, x, tgt)


def _tile_gain(g, reps):
    return jnp.tile(g.reshape(1, -1), (1, reps))


def _pad_lane(v):
    v = v.reshape(1, -1)
    return jnp.pad(v, ((0, 0), (0, LANE - v.shape[1])))


def _local_step(x, mem, tgt, w_main, w_fb, w_small, norm_gain, mem_norm_gain, b_forget,
                q_gain_a, k_gain_a, sinks_a, q_gain_b, k_gain_b, q_gain_c, k_gain_c, core=None):
    s = x.shape[0]
    tr = min(512, s)
    bd64 = _block_diag(768, HEAD_DIM)
    bd128 = _block_diag(512, C_HEAD_DIM)
    hsum = _head_sum(768, HEAD_DIM)
    qga, kga = _tile_gain(q_gain_a, 12), _tile_gain(k_gain_a, 4)
    qgb, kgb = _tile_gain(q_gain_b, 12), _tile_gain(k_gain_b, 12)
    qgc, kgc = _tile_gain(q_gain_c, 4), _tile_gain(k_gain_c, 4)
    sinks = _pad_lane(sinks_a)
    bfor = _pad_lane(b_forget)

    hn = _rms_fwd(x, norm_gain, tr=tr, name="rms_x")
    on_mesh = core is not None
    if on_mesh:
        proj, (gathered,) = _matmul(hn, w_main, dims="nn", out_dtype=F32, tm=1024, tn=1024, tk=D_MODEL, name="proj_main",
                                    comms=[_gather_comm(list(w_small))])
        w_mk, wa, wb, wc, wo = gathered
        w_mk, wo = w_mk.reshape(D_MODEL, 1024), wo.reshape(D_MODEL, D_MODEL)
    else:
        proj = _matmul(hn, w_main, dims="nn", out_dtype=F32, tm=1024, tn=1024, tk=D_MODEL, name="proj_main")
        w_mk, wa, wb, wc, wo = w_small
    fbl = _matmul(hn, w_fb, dims="nn", out_dtype=F32, tm=1024, tn=LANE, tk=D_MODEL, name="proj_forget")
    memn = _rms_fwd(mem, mem_norm_gain, tr=mem.shape[0], name="rms_mem")
    mkv = _matmul(memn, w_mk, dims="nn", out_dtype=F32, tm=256, tn=512, tk=D_MODEL, name="mem_kv")

    swa_bias = _swa_bias()
    ga = _swa_fwd(proj, qga, kga, sinks, bd64, swa_bias)
    ea, eb, ones_a, ones_b = _expand_mats()
    tf = min(256, s)
    qat, ka, kat, va, vat, qn, cfox = _fox2_prep(proj, fbl, qgb, kgb, bfor, bd64, ea, eb, ones_a, ones_b, tr=tf)
    gb, yb, lse = _fox2_fwd(proj, qat, ka, vat)
    gc = _mem_fwd(proj, mkv, qgc, kgc, bd128, tr=tr)
    y = _merge_fwd(proj, ga, gb, gc, wa, wb, wc, tr=tr)
    dout, dout_b, sq, dy = _out_loss(y, wo, x, tgt, tr=tr, tn=512)

    d_wo = _matmul(y, dout_b, dims="tn", out_dtype=F32, tm=1024, tn=512, tk=4096, name="dw_out")
    dl0, dl1, dl2, dua, dub, duc, dga, dgb, dgc = _merge_bwd(proj, ga, gb, gc, wa, wb, wc, dy, tr=tr)
    d_wa = _matmul(ga, dua, dims="tn", out_dtype=F32, tm=768, tn=512, tk=4096, name="dw_branch_a")
    d_wb = _matmul(gb, dub, dims="tn", out_dtype=F32, tm=768, tn=512, tk=4096, name="dw_branch_b")
    d_wc = _matmul(gc, duc, dims="tn", out_dtype=F32, tm=512, tn=512, tk=4096, name="dw_branch_c")

    dproj_a, d_qga, d_kga, d_sinks = _swa_bwd(proj, qga, kga, sinks, bd64, swa_bias, dga)

    qab, qabt, dya, dyat, dzb = _fox2_bwd_pre(proj, yb, dgb, qn, cfox, lse, hsum, ea, ones_b, tr=tf)
    dqa, dkn, dvb, dck = _fox2_bwd(qab, qabt, ka, kat, va, dya, dyat)
    dqb, dkb, dfb, d_qgb, d_kgb, d_bf = _fox2_bwd_post(proj, fbl, qgb, kgb, bfor, bd64, dqa, dkn, dck, tr=tf)

    dproj_c, dmkv, d_qgc, d_kgc = _mem_bwd(proj, mkv, qgc, kgc, bd128, dgc, tr=tr)
    dmkv_b = dmkv.astype(BF16)
    d_wmk = _matmul(memn, dmkv_b, dims="tn", out_dtype=F32, tm=1024, tn=512, tk=256, name="dw_mem_kv")
    dmemn = _matmul(dmkv_b, w_mk, dims="nt", out_dtype=F32, tm=256, tn=512, tk=1024, name="dmemn")
    (d_mem_gain,) = _rms_bwd(mem, mem_norm_gain, dmemn, None, tr=mem.shape[0], name="rms_mem_bwd")

    dproj = [dproj_a, jnp.concatenate([dqb, dkb, dvb, dzb, dproj_c], axis=1), dl0, dl1, dl2]
    dhn_f = _matmul(dfb, w_fb, dims="nt", out_dtype=F32, tm=1024, tn=512, tk=LANE, name="dhn_forget")
    d_wfb = _matmul(hn, dfb, dims="tn", out_dtype=F32, tm=1024, tn=LANE, tk=512, name="dw_forget")
    big = {}
    if on_mesh:
        half = D_MODEL // 2
        c0 = core[0]
        hn_other = lax.dynamic_slice(hn, (0, (1 - c0) * half), (s, half))
        hn_own = lax.dynamic_slice(hn, (0, c0 * half), (s, half))
        g1, k1 = [d_wmk, d_wa, d_wb, d_wc, d_wo], [2, 3, 4, 5, 6]
        d_other, (got1,) = _matmul(hn_other, dproj, dims="tn", out_dtype=F32, tm=1024, tn=512, tk=4096,
                                   name="dw_main_other", comms=[_exchange_comm(g1, k1)])
        h1 = [_add_half(g, got, core, HALF_AXIS[k], name=f"add_half_{k}") for g, got, k in zip(g1, got1, k1)]
        d_own, (got0, parts1) = _matmul(
            hn_own, dproj, dims="tn", out_dtype=F32, tm=1024, tn=512, tk=4096, name="dw_main_own",
            comms=[_exchange_comm([d_other, d_wfb], [0, 1], whole=(0,)), _scatter_comm(h1, k1)])
        h0 = [_add_pair(d_own, got0[0], name="add_pair_main"), _add_half(d_wfb, got0[1], core, 0, name="add_half_1")]
        sums1 = [_sum4(p, name=f"sum4_{k}") for p, k in zip(parts1, k1)]
        dhn, (parts0, theirs1) = _matmul(dproj, w_main, dims="nt", out_dtype=F32, tm=1024, tn=512, tk=2048, vmem=VMEM_WIDE, name="dhn",
                                         add=dhn_f, comms=[_scatter_comm(h0, [0, 1]), _swap_comm(sums1)])
        sums0 = [_sum4(p, name=f"sum4_{k}") for p, k in zip(parts0, (0, 1))]
        (grad_x, d_gain), theirs0 = _rms_bwd(x, norm_gain, dhn, dout, tr=tr, name="rms_x_bwd", comm=_swap_comm(sums0))
        big = dict(sums=sums0 + sums1, theirs=list(theirs0) + list(theirs1))
    else:
        dhn = _matmul(dproj, w_main, dims="nt", out_dtype=F32, tm=1024, tn=512, tk=2048, vmem=VMEM_WIDE, name="dhn", add=dhn_f)
        d_wmain = _matmul(hn, dproj, dims="tn", out_dtype=F32, tm=1024, tn=512, tk=4096, name="dw_main")
        big = dict(d_wmain=d_wmain, d_wfb=d_wfb, d_wmk=d_wmk, d_wa=d_wa, d_wb=d_wb, d_wc=d_wc, d_wo=d_wo)
        grad_x, d_gain = _rms_bwd(x, norm_gain, dhn, dout, tr=tr, name="rms_x_bwd")

    fold = lambda g, reps: jnp.sum(g.reshape(reps, -1), axis=0, keepdims=True)
    return dict(
        sq=sq, grad_x=grad_x, **big,
        d_gain=d_gain, d_mem_gain=d_mem_gain, d_bf=d_bf[:, :N_FORGET],
        d_qga=fold(d_qga, 12), d_kga=fold(d_kga, 4), d_sinks=d_sinks[:, :A_HEADS],
        d_qgb=fold(d_qgb, 12), d_kgb=fold(d_kgb, 12), d_qgc=fold(d_qgc, 4), d_kgc=fold(d_kgc, 4))


PACK_ROWS = 256
FORGET_IN_SHARD = FORGET_COL - SHARD_COLS
AFTER_FORGET = FORGET_COL - SLAB_START[1]
END_CHIP1 = 2 * SHARD_COLS - N_FORGET - SLAB_START[1]


def _pack_w_in(chip, w):
    rows = w.shape[1]
    tr = PACK_ROWS

    def body(k_ref, w_ref, o_ref, scr):
        scr[...] = jnp.zeros_like(scr)
        scr[pl.ds(0, SHARD_COLS), :] = w_ref[...]
        v = jnp.transpose(scr[...])
        k = k_ref[0]
        col = lax.broadcasted_iota(jnp.int32, (tr, SLAB), 1)
        no_forget = jnp.zeros((tr, LANE), BF16)

        @pl.when(k == 0)
        def _():
            o_ref[:, 0:SLAB] = v.astype(BF16)
            o_ref[:, SLAB:] = no_forget

        @pl.when(k == 1)
        def _():
            before = pltpu.roll(v, SLAB_SHIFT[1], axis=1)
            after = pltpu.roll(v, SLAB - (N_FORGET - SLAB_SHIFT[1]), axis=1)
            slab = jnp.where(col < AFTER_FORGET, before, jnp.where(col < END_CHIP1, after, 0.0))
            o_ref[:, 0:SLAB] = slab.astype(BF16)
            f = pltpu.roll(v, SLAB - FORGET_IN_SHARD, axis=1)[:, :LANE]
            o_ref[:, SLAB:] = jnp.where(col[:, :LANE] < N_FORGET, f, 0.0).astype(BF16)

        for kk in (2, 3):
            @pl.when(k == kk)
            def _(kk=kk):
                o_ref[:, 0:SLAB] = pltpu.roll(v, SLAB_SHIFT[kk], axis=1).astype(BF16)
                o_ref[:, SLAB:] = no_forget

    return pl.pallas_call(
        body, grid_spec=pltpu.PrefetchScalarGridSpec(
            num_scalar_prefetch=1, grid=(rows // tr,),
            in_specs=[pl.BlockSpec((SHARD_COLS, tr), lambda i, k: (0, i))],
            out_specs=pl.BlockSpec((None, tr, SLAB + LANE), lambda i, k: (k[0], i, 0)),
            scratch_shapes=[pltpu.VMEM((SLAB, tr), F32)]),
        out_shape=SDS((N_CHIPS, rows, SLAB + LANE), BF16), name="pack_w_in",
        compiler_params=_params(("arbitrary",)))(chip, w)


def _merge_slabs(g):
    rows = g.shape[1]
    tr = PACK_ROWS
    t = [s // LANE for s in SLAB_START]
    n_t = SLAB // LANE

    def body(g_ref, m_ref, f_ref):
        for k in range(N_CHIPS):
            lo = t[k] + (1 if k > 0 else 0)
            hi = t[k + 1] if k + 1 < N_CHIPS else t[k] + n_t
            m_ref[:, lo * LANE:hi * LANE] = g_ref[k, :, (lo - t[k]) * LANE:(hi - t[k]) * LANE]
            if k + 1 < N_CHIPS:
                a = g_ref[k, :, (hi - t[k]) * LANE:(hi - t[k] + 1) * LANE].astype(F32)
                b = g_ref[k + 1, :, 0:LANE].astype(F32)
                m_ref[:, hi * LANE:(hi + 1) * LANE] = (a + b).astype(BF16)
        f_ref[...] = g_ref[1, :, SLAB:]

    return pl.pallas_call(
        body, grid=(rows // tr,),
        in_specs=[pl.BlockSpec((N_CHIPS, tr, SLAB + LANE), lambda i: (0, i, 0))],
        out_specs=[_rowblk(tr, P_MAIN), _rowblk(tr, LANE)],
        out_shape=[SDS((rows, P_MAIN), BF16), SDS((rows, LANE), BF16)], name="merge_slabs",
        compiler_params=_params(("parallel",)))(g)


def _adamw_math(w, g, m, v):
    nm = ADAM_B1 * m + (1.0 - ADAM_B1) * g
    nv = ADAM_B2 * v + (1.0 - ADAM_B2) * (g * g)
    m_hat = nm / (1.0 - ADAM_B1 ** ADAM_STEP)
    v_hat = nv / (1.0 - ADAM_B2 ** ADAM_STEP)
    delta = -ADAM_LR * (m_hat / (jnp.sqrt(v_hat) + ADAM_EPS) + ADAM_WD * w)
    return delta, nm, nv


def _adamw(g, w, m, v, *, tr, name):
    rows, cols = w.shape
    tr = min(tr, rows)

    def body(g_ref, w_ref, m_ref, v_ref, d_ref, nm_ref, nv_ref):
        d, nm, nv = _adamw_math(w_ref[...], g_ref[...], m_ref[...], v_ref[...])
        d_ref[...] = d
        nm_ref[...] = nm
        nv_ref[...] = nv

    spec = _rowblk(tr, cols)
    return pl.pallas_call(
        body, grid=(rows // tr,), in_specs=[spec] * 4, out_specs=[spec] * 3,
        out_shape=[SDS((rows, cols), F32)] * 3, name=name, compiler_params=_params(("parallel",)))(g, w, m, v)


def _adamw_w_in(chip_core, slab_mine, slab_theirs, forget_mine, forget_theirs, w, m, v):
    rows = w.shape[1]
    tr = PACK_ROWS // 2
    nbh = rows // 2 // tr

    def body(k_ref, sa_ref, sb_ref, fa_ref, fb_ref, w_ref, m_ref, v_ref, g_ref, d_ref, nm_ref, nv_ref):
        use_mine = pl.program_id(0) // nbh == k_ref[1]
        sl = jnp.where(use_mine, sa_ref[...], sb_ref[...])
        f_tile = jnp.where(use_mine, fa_ref[...], fb_ref[...])
        k = k_ref[0]

        def emit(wide):
            g = jnp.transpose(wide)[:SHARD_COLS, :]
            g_ref[...] = g
            d, nm, nv = _adamw_math(w_ref[...], g, m_ref[...], v_ref[...])
            d_ref[...] = d
            nm_ref[...] = nm
            nv_ref[...] = nv

        @pl.when(k == 0)
        def _():
            emit(sl)

        @pl.when(k == 1)
        def _():
            col = lax.broadcasted_iota(jnp.int32, (tr, SLAB), 1)
            before = pltpu.roll(sl, SLAB - SLAB_SHIFT[1], axis=1)
            after = pltpu.roll(sl, N_FORGET - SLAB_SHIFT[1], axis=1)
            wide_f = jnp.concatenate([f_tile, jnp.zeros((tr, SLAB - LANE), F32)], axis=1)
            forget = pltpu.roll(wide_f, FORGET_IN_SHARD, axis=1)
            emit(jnp.where(col < FORGET_IN_SHARD, before, jnp.where(col < FORGET_IN_SHARD + N_FORGET, forget, after)))

        for kk in (2, 3):
            @pl.when(k == kk)
            def _(kk=kk):
                emit(pltpu.roll(sl, SLAB - SLAB_SHIFT[kk], axis=1))

    nat = pl.BlockSpec((SHARD_COLS, tr), lambda i, k: (0, i))
    half = lambda width: pl.BlockSpec((tr, width), lambda i, k: (i % nbh, 0))
    return pl.pallas_call(
        body, grid_spec=pltpu.PrefetchScalarGridSpec(
            num_scalar_prefetch=1, grid=(rows // tr,),
            in_specs=[half(SLAB), half(SLAB), half(LANE), half(LANE), nat, nat, nat],
            out_specs=[nat] * 4),
        out_shape=[SDS((SHARD_COLS, rows), F32)] * 4, name="adamw_w_in",
        compiler_params=_params(("arbitrary",)))(chip_core, slab_mine, slab_theirs, forget_mine, forget_theirs, w, m, v)


ANY = pl.BlockSpec(memory_space=pl.ANY)
HALF_AXIS = (0, 0, 1, 0, 0, 0, 1)


def _me():
    return lax.axis_index("x"), lax.axis_index("y"), lax.axis_index("c")


def _half(ref, which, axis):
    n = ref.shape[axis] // 2
    sl = pl.ds(which * n, n)
    return ref.at[sl] if axis == 0 else ref.at[:, sl]


def _piece(t, ref, j):
    if t == 0:
        return ref.at[:, pl.ds(SLAB_START[j], SLAB)]
    if t == 1:
        return ref
    if t in (2, 6):
        return ref.at[pl.ds(512 * j, 512)]
    return ref.at[:, pl.ds(512 * j, 512)]


def _piece_shape(t, shape):
    if t == 0:
        return (shape[0], SLAB)
    if t == 1:
        return shape
    if t in (2, 6):
        return (512, shape[1])
    return (shape[0], 512)


def _gather_plan(ins, outs, own_slot_in_src):
    x, y, c = _me()
    k = 2 * x + y
    sib = (x, y, 1 - c)
    chips = [(1 - x, y), (x, 1 - y), (1 - x, 1 - y)]
    n = len(outs)

    def rows(t, which):
        h = outs[t].shape[1] // 2
        return pl.ds(which * h, h)

    def mine(t):
        return ins[t].at[k, rows(t, c)] if own_slot_in_src else ins[t].at[rows(t, c)]

    def first(t, j, sems):
        chip = chips[j]
        return pltpu.make_async_remote_copy(
            src_ref=mine(t), dst_ref=outs[t].at[k, rows(t, c)], send_sem=sems[0].at[t, j], recv_sem=sems[1].at[t, j],
            device_id=(chip[0], chip[1], c), device_id_type=MESH)

    def landed(t, j, sems):
        chip = chips[j]
        return pltpu.make_async_remote_copy(
            src_ref=mine(t), dst_ref=outs[t].at[2 * chip[0] + chip[1], rows(t, c)], send_sem=sems[0].at[t, j],
            recv_sem=sems[1].at[t, j], device_id=(chip[0], chip[1], c), device_id_type=MESH)

    def passed(t, j, which, sems):
        chip = chips[j]
        blk = outs[t].at[2 * chip[0] + chip[1], rows(t, which)]
        return pltpu.make_async_remote_copy(
            src_ref=blk, dst_ref=blk, send_sem=sems[2].at[t, j], recv_sem=sems[3].at[t, j], device_id=sib,
            device_id_type=MESH)

    def start(sems):
        for j in range(3):
            for t in range(n):
                first(t, j, sems).start()

    def finish(sems):
        for j in range(3):
            for t in range(n):
                landed(t, j, sems).wait_recv()
                passed(t, j, c, sems).start()
        for j in range(3):
            for t in range(n):
                passed(t, j, 1 - c, sems).wait_recv()
        for j in range(3):
            for t in range(n):
                first(t, j, sems).wait_send()
                passed(t, j, c, sems).wait_send()

    return k, start, finish


def _all_gather_slabs(slabs):
    def body(in_ref, out_ref, nbr_sem, quarter_sem, pass_sem):
        x, y, c = _me()
        k = 2 * x + y
        rows = out_ref.shape[1]
        h, q = rows // 2, rows // 4
        nbrs = [(1 - x, y), (x, 1 - y)]
        slot = lambda chip: 2 * chip[0] + chip[1]
        diag = 2 * (1 - x) + (1 - y)
        half = pl.ds(c * h, h)
        quarter = lambda a: pl.ds(c * h + a * q, q)

        def first(a):
            return pltpu.make_async_remote_copy(
                src_ref=in_ref.at[k, half], dst_ref=out_ref.at[k, half], send_sem=nbr_sem.at[0, a],
                recv_sem=nbr_sem.at[1, a], device_id=(nbrs[a][0], nbrs[a][1], c), device_id_type=MESH)

        def landed(a):
            blk = out_ref.at[slot(nbrs[a]), half]
            return pltpu.make_async_remote_copy(
                src_ref=blk, dst_ref=blk, send_sem=nbr_sem.at[0, a], recv_sem=nbr_sem.at[1, a],
                device_id=(nbrs[a][0], nbrs[a][1], c), device_id_type=MESH)

        def relay(a):
            blk = out_ref.at[slot(nbrs[a]), quarter(a)]
            to = nbrs[1 - a]
            return pltpu.make_async_remote_copy(
                src_ref=blk, dst_ref=blk, send_sem=quarter_sem.at[0, a], recv_sem=quarter_sem.at[1, a],
                device_id=(to[0], to[1], c), device_id_type=MESH)

        def relayed(a):
            blk = out_ref.at[diag, quarter(a)]
            frm = nbrs[1 - a]
            return pltpu.make_async_remote_copy(
                src_ref=blk, dst_ref=blk, send_sem=quarter_sem.at[0, a], recv_sem=quarter_sem.at[1, a],
                device_id=(frm[0], frm[1], c), device_id_type=MESH)

        def passed(j, which):
            sl = diag if j == 2 else slot(nbrs[j])
            blk = out_ref.at[sl, pl.ds(which * h, h)]
            return pltpu.make_async_remote_copy(
                src_ref=blk, dst_ref=blk, send_sem=pass_sem.at[0, j], recv_sem=pass_sem.at[1, j],
                device_id=(x, y, 1 - c), device_id_type=MESH)

        for a in range(2):
            first(a).start()
        for a in range(2):
            landed(a).wait_recv()
            relay(a).start()
            passed(a, c).start()
        for a in range(2):
            relayed(a).wait_recv()
        passed(2, c).start()
        for j in range(3):
            passed(j, 1 - c).wait_recv()
        for a in range(2):
            first(a).wait_send()
            relay(a).wait_send()
        for j in range(3):
            passed(j, c).wait_send()

    return pl.pallas_call(
        body, in_specs=[ANY], out_specs=ANY, out_shape=SDS(slabs.shape, slabs.dtype),
        scratch_shapes=[pltpu.SemaphoreType.DMA((2, 2)), pltpu.SemaphoreType.DMA((2, 2)), pltpu.SemaphoreType.DMA((2, 3))],
        input_output_aliases={0: 0}, name="all_gather_slabs")(slabs)


def _gather_comm(parts):
    n = len(parts)

    def start(ins, outs, sems):
        k, go, _ = _gather_plan(ins, outs, False)
        for t in range(n):
            pltpu.make_async_copy(ins[t], outs[t].at[k], sems[4].at[t]).start()
        go(sems)

    def finish(ins, outs, sems):
        k, _, done = _gather_plan(ins, outs, False)
        done(sems)
        for t in range(n):
            pltpu.make_async_copy(ins[t], outs[t].at[k], sems[4].at[t]).wait()

    return _Comm(parts, [SDS((N_CHIPS,) + p.shape, p.dtype) for p in parts],
                 [pltpu.SemaphoreType.DMA((n, 3))] * 4 + [pltpu.SemaphoreType.DMA((n,))], start, finish)


def _exchange_comm(arrs, kinds, whole=()):
    n = len(arrs)

    def copies(ins, outs, sems):
        x, y, c = _me()
        return [pltpu.make_async_remote_copy(
            src_ref=ins[t] if t in whole else _half(ins[t], 1 - c, HALF_AXIS[kinds[t]]), dst_ref=outs[t],
            send_sem=sems[0].at[t], recv_sem=sems[1].at[t], device_id=(x, y, 1 - c), device_id_type=MESH)
            for t in range(n)]

    def start(ins, outs, sems):
        for cp in copies(ins, outs, sems):
            cp.start()

    def finish(ins, outs, sems):
        for cp in copies(ins, outs, sems):
            cp.wait()

    def hshape(t):
        s = list(arrs[t].shape)
        if t not in whole:
            s[HALF_AXIS[kinds[t]]] //= 2
        return SDS(tuple(s), arrs[t].dtype)

    return _Comm(arrs, [hshape(t) for t in range(n)], [pltpu.SemaphoreType.DMA((n,))] * 2, start, finish)


def _add_half(full, got, core, axis, *, name):
    r, c = got.shape
    br, bc = (256 if r % 256 == 0 else 128), min(2048, c)
    off_r = (r // br) if axis == 0 else 0
    off_c = (c // bc) if axis == 1 else 0

    def body(c_ref, a_ref, b_ref, o_ref):
        o_ref[...] = (a_ref[...] + b_ref[...]).astype(BF16)

    return pl.pallas_call(
        body, grid_spec=pltpu.PrefetchScalarGridSpec(
            num_scalar_prefetch=1, grid=(r // br, c // bc),
            in_specs=[pl.BlockSpec((br, bc), lambda i, j, cr: (i + cr[0] * off_r, j + cr[0] * off_c)),
                      pl.BlockSpec((br, bc), lambda i, j, cr: (i, j))],
            out_specs=pl.BlockSpec((br, bc), lambda i, j, cr: (i, j))),
        out_shape=SDS((r, c), BF16), name=name, compiler_params=_params(("parallel", "parallel")))(core, full, got)


def _add_pair(a, b, *, name):
    r, c = a.shape
    br, bc = 256, min(2048, c)

    def body(a_ref, b_ref, o_ref):
        o_ref[...] = (a_ref[...] + b_ref[...]).astype(BF16)

    spec = pl.BlockSpec((br, bc), lambda i, j: (i, j))
    return pl.pallas_call(body, grid=(r // br, c // bc), in_specs=[spec, spec], out_specs=spec,
                          out_shape=SDS((r, c), BF16), name=name, compiler_params=_params(("parallel", "parallel")))(a, b)


def _scatter_comm(halves, kinds):
    n = len(halves)

    def plan(ins, outs, sems):
        send, recv, lsem = sems
        x, y, c = _me()
        k = 2 * x + y

        def to_chip(t, j):
            return pltpu.make_async_remote_copy(
                src_ref=_piece(kinds[t], ins[t], j), dst_ref=outs[t].at[k], send_sem=send.at[t, j],
                recv_sem=recv.at[t, k], device_id=(j // 2, j % 2, c), device_id_type=MESH)

        def from_chip(t, j):
            return pltpu.make_async_remote_copy(
                src_ref=_piece(kinds[t], ins[t], j), dst_ref=outs[t].at[j], send_sem=send.at[t, j],
                recv_sem=recv.at[t, j], device_id=(j // 2, j % 2, c), device_id_type=MESH)

        def own(t, j):
            return pltpu.make_async_copy(_piece(kinds[t], ins[t], j), outs[t].at[j], lsem.at[t])

        return k, to_chip, from_chip, own

    def start(ins, outs, sems):
        k, to_chip, _, own = plan(ins, outs, sems)
        for j in range(N_CHIPS):
            @pl.when(k != j)
            def _(j=j):
                for t in range(n):
                    to_chip(t, j).start()

            @pl.when(k == j)
            def _(j=j):
                for t in range(n):
                    own(t, j).start()

    def finish(ins, outs, sems):
        k, to_chip, from_chip, own = plan(ins, outs, sems)
        for j in range(N_CHIPS):
            @pl.when(k != j)
            def _(j=j):
                for t in range(n):
                    from_chip(t, j).wait_recv()
                for t in range(n):
                    to_chip(t, j).wait_send()

            @pl.when(k == j)
            def _(j=j):
                for t in range(n):
                    own(t, j).wait()

    return _Comm(halves, [SDS((N_CHIPS,) + _piece_shape(kinds[t], halves[t].shape), halves[t].dtype) for t in range(n)],
                 [pltpu.SemaphoreType.DMA((n, N_CHIPS))] * 2 + [pltpu.SemaphoreType.DMA((n,))], start, finish)


def _sum4(p, *, name):
    _, r, c = p.shape
    br = 256 if r % 256 == 0 else 128

    def body(p_ref, o_ref):
        o_ref[...] = ((p_ref[0].astype(F32) + p_ref[1].astype(F32)) + p_ref[2].astype(F32)) + p_ref[3].astype(F32)

    return pl.pallas_call(
        body, grid=(r // br,), in_specs=[pl.BlockSpec((N_CHIPS, br, c), lambda i: (0, i, 0))],
        out_specs=_rowblk(br, c), out_shape=SDS((r, c), F32), name=name, compiler_params=_params(("parallel",)))(p)


def _swap_comm(sums):
    return _exchange_comm(sums, [None] * len(sums), whole=tuple(range(len(sums))))


def _adamw_halves(mine, theirs, core, w, m, v, *, axis, tr, name):
    rows, cols = w.shape

    if axis == 0:
        nbh = rows // 2 // tr
        g_spec = pl.BlockSpec((tr, cols), lambda i, cr: (i % nbh, 0))
    else:
        g_spec = pl.BlockSpec((tr, cols // 2), lambda i, cr: (i, 0))

    def body(c_ref, a_ref, b_ref, w_ref, m_ref, v_ref, g_ref, d_ref, nm_ref, nv_ref):
        a, b = a_ref[...], b_ref[...]
        if axis == 0:
            g = jnp.where(pl.program_id(0) // nbh == c_ref[0], a, b)
        else:
            low = c_ref[0] == 0
            g = jnp.concatenate([jnp.where(low, a, b), jnp.where(low, b, a)], axis=1)
        g_ref[...] = g
        d, nm, nv = _adamw_math(w_ref[...], g, m_ref[...], v_ref[...])
        d_ref[...] = d
        nm_ref[...] = nm
        nv_ref[...] = nv

    nat = pl.BlockSpec((tr, cols), lambda i, cr: (i, 0))
    return pl.pallas_call(
        body, grid_spec=pltpu.PrefetchScalarGridSpec(
            num_scalar_prefetch=1, grid=(rows // tr,), in_specs=[g_spec, g_spec, nat, nat, nat], out_specs=[nat] * 4),
        out_shape=[SDS((rows, cols), F32)] * 4, name=name, compiler_params=_params(("arbitrary",)))(
            core, mine, theirs, w, m, v)


SMALL_ROWS, SMALL_COLS = 8, 1024


def _pack_small(vs):
    flat = jnp.concatenate([v.reshape(-1) for v in vs])
    return jnp.pad(flat, (0, SMALL_ROWS * SMALL_COLS - flat.shape[0])).reshape(SMALL_ROWS, SMALL_COLS)


def _unpack_small(packed, sizes):
    flat = packed.reshape(-1)
    out, o = [], 0
    for n in sizes:
        out.append(flat[o:o + n].reshape(1, n))
        o += n
    return out


def _all_reduce_small(v):
    n_dev = 8

    def body(v_ref, o_ref, land, send, recv):
        x, y, c = _me()
        me = 4 * x + 2 * y + c
        land[me] = v_ref[...]
        cps = []
        for r in range(1, n_dev):
            fx, fy, fc = (r >> 2) & 1, (r >> 1) & 1, r & 1
            peer = (x ^ fx, y ^ fy, c ^ fc)
            cps.append(pltpu.make_async_remote_copy(
                src_ref=v_ref, dst_ref=land.at[me], send_sem=send.at[r - 1], recv_sem=recv.at[r - 1],
                device_id=peer, device_id_type=MESH))
        for cp in cps:
            cp.start()
        for r in range(1, n_dev):
            fx, fy, fc = (r >> 2) & 1, (r >> 1) & 1, r & 1
            src = 4 * (x ^ fx) + 2 * (y ^ fy) + (c ^ fc)
            pltpu.make_async_remote_copy(
                src_ref=v_ref, dst_ref=land.at[src], send_sem=send.at[r - 1], recv_sem=recv.at[r - 1],
                device_id=(x ^ fx, y ^ fy, c ^ fc), device_id_type=MESH).wait_recv()
        for cp in cps:
            cp.wait_send()
        acc = land[0]
        for r in range(1, n_dev):
            acc = acc + land[r]
        o_ref[...] = acc

    vm = pl.BlockSpec(memory_space=pltpu.VMEM)
    return pl.pallas_call(
        body, in_specs=[vm], out_specs=vm, out_shape=SDS(v.shape, F32),
        scratch_shapes=[pltpu.VMEM((n_dev,) + v.shape, F32), pltpu.SemaphoreType.DMA((n_dev - 1,)),
                        pltpu.SemaphoreType.DMA((n_dev - 1,))],
        name="all_reduce_small")(v)


def kernel(x, mem, norm_gain, mem_norm_gain, w_in, b_forget, q_gain_a, k_gain_a, sinks_a, q_gain_b, k_gain_b, q_gain_c, k_gain_c, w_mem_kv, w_branch_a, w_branch_b, w_branch_c, w_out, loss_target, m_norm_gain, m_mem_norm_gain, m_w_in, m_b_forget, m_q_gain_a, m_k_gain_a, m_sinks_a, m_q_gain_b, m_k_gain_b, m_q_gain_c, m_k_gain_c, m_w_mem_kv, m_w_branch_a, m_w_branch_b, m_w_branch_c, m_w_out, v_norm_gain, v_mem_norm_gain, v_w_in, v_b_forget, v_q_gain_a, v_k_gain_a, v_sinks_a, v_q_gain_b, v_k_gain_b, v_q_gain_c, v_k_gain_c, v_w_mem_kv, v_w_branch_a, v_w_branch_b, v_w_branch_c, v_w_out):
    xi, yi, ci = lax.axis_index("x"), lax.axis_index("y"), lax.axis_index("c")
    chip = jnp.reshape(2 * xi + yi, (1,)).astype(jnp.int32)
    core = jnp.reshape(ci, (1,)).astype(jnp.int32)

    slabs = _pack_w_in(chip, jnp.transpose(w_in[0]))
    mine = [w_mem_kv[0].astype(BF16), w_branch_a[0].astype(BF16), w_branch_b[0].astype(BF16),
            w_branch_c[0].astype(BF16), w_out[0].astype(BF16)]
    w_main, w_fb = _merge_slabs(_all_gather_slabs(slabs))

    r = _local_step(x[0], mem[0], loss_target[0], w_main, w_fb, mine, norm_gain, mem_norm_gain,
                    b_forget, q_gain_a, k_gain_a, sinks_a, q_gain_b, k_gain_b, q_gain_c, k_gain_c, core=core)
    sums, theirs = r["sums"], r["theirs"]

    small_names = ["d_gain", "d_mem_gain", "d_bf", "d_qga", "d_kga", "d_sinks", "d_qgb", "d_kgb", "d_qgc", "d_kgc"]
    loss_part = (0.5 / D_MODEL) * jnp.sum(r["sq"], axis=1, keepdims=True)
    packed = _pack_small([r[n] for n in small_names] + [loss_part])
    red = _all_reduce_small(packed)
    small_w = [norm_gain, mem_norm_gain, b_forget, q_gain_a, k_gain_a, sinks_a, q_gain_b, k_gain_b, q_gain_c, k_gain_c]
    small_m = [m_norm_gain, m_mem_norm_gain, m_b_forget, m_q_gain_a, m_k_gain_a, m_sinks_a, m_q_gain_b, m_k_gain_b,
               m_q_gain_c, m_k_gain_c]
    small_v = [v_norm_gain, v_mem_norm_gain, v_b_forget, v_q_gain_a, v_k_gain_a, v_sinks_a, v_q_gain_b, v_k_gain_b,
               v_q_gain_c, v_k_gain_c]
    sizes = [w.shape[1] for w in small_w]
    s_d, s_m, s_v = _adamw(red, _pack_small(small_w), _pack_small(small_m), _pack_small(small_v), tr=8, name="adamw_small")
    g_small = _unpack_small(red, sizes + [1])
    loss = g_small[-1].reshape(())
    d_small, m_small, v_small = _unpack_small(s_d, sizes), _unpack_small(s_m, sizes), _unpack_small(s_v, sizes)

    gw_in, dw_in, mw_in, vw_in = _adamw_w_in(jnp.concatenate([chip, core]), sums[0], theirs[0], sums[1], theirs[1],
                                             jnp.transpose(w_in[0]), jnp.transpose(m_w_in[0]), jnp.transpose(v_w_in[0]))
    big = {}
    for t, nm, w, m, v in ((2, "w_mem_kv", w_mem_kv, m_w_mem_kv, v_w_mem_kv),
                           (3, "w_branch_a", w_branch_a, m_w_branch_a, v_w_branch_a),
                           (4, "w_branch_b", w_branch_b, m_w_branch_b, v_w_branch_b),
                           (5, "w_branch_c", w_branch_c, m_w_branch_c, v_w_branch_c),
                           (6, "w_out", w_out, m_w_out, v_w_out)):
        big[nm] = _adamw_halves(sums[t], theirs[t], core, w[0], m[0], v[0], axis=HALF_AXIS[t], tr=128,
                                name="adamw_" + nm)

    def collect(kind):
        sm = (g_small, d_small, m_small, v_small)[kind]
        win = (gw_in, dw_in, mw_in, vw_in)[kind]
        return ([sm[0], sm[1], jnp.transpose(win)[None]] + [a for a in sm[2:10]]
                + [big[n][kind][None] for n in ("w_mem_kv", "w_branch_a", "w_branch_b", "w_branch_c", "w_out")])

    return (loss, r["grad_x"][None], *collect(0), *collect(1), *collect(2), *collect(3))
```

```python
import numpy as np
import jax
import jax.numpy as jnp
from jax import lax
from jax.experimental import pallas as pl
from jax.experimental.pallas import tpu as pltpu

F32 = jnp.float32
BF16 = jnp.bfloat16
HI = lax.Precision.HIGHEST
SDS = jax.ShapeDtypeStruct
MESH = pl.DeviceIdType.MESH

D_MODEL = 2048
HEAD_DIM = 64
A_HEADS = 12
A_GROUP = 3
B_HEADS = 12
C_HEADS = 4
C_HEAD_DIM = 128
WINDOW = 128
EPS = 1e-6
NEG = -1e30
LANE = 128

QA, KA, VA, ZA = 0, 768, 1024, 1280
QB, KB, VB, ZB = 2048, 2816, 3584, 4352
QC, ZC = 5120, 5632
GATE = 6144
P_MAIN = 12288
N_FORGET = 12
FORGET_COL = 5120
SHARD_COLS = 3075
SLAB = 3200
SLAB_START = (0, 3072, 6016, 9088)
SLAB_SHIFT = (0, 3, 122, 125)
N_CHIPS = 4

ADAM_LR = 0.001
ADAM_B1 = 0.9
ADAM_B2 = 0.999
ADAM_EPS = 1e-08
ADAM_WD = 0.01
ADAM_STEP = 10

VMEM_LIMIT = 56 * 1024 * 1024
VMEM_WIDE = 62 * 1024 * 1024


def _params(sem, vmem=VMEM_LIMIT):
    return pltpu.CompilerParams(dimension_semantics=sem, vmem_limit_bytes=vmem)


def _win(tr, width, off):
    return pl.BlockSpec((pl.Element(tr), pl.Element(width)), lambda i, *_: (i * tr, off))


def _rowblk(tr, width):
    return pl.BlockSpec((tr, width), lambda i, *_: (i, 0))


def _const(shape):
    nd = len(shape)
    return pl.BlockSpec(shape, lambda *_: (0,) * nd)


def _rms(x, g):
    return x * lax.rsqrt(jnp.mean(x * x, axis=-1, keepdims=True) + EPS) * g


def _head_mean_impl(x2, bd):
    hi = x2.astype(BF16)
    lo = (x2 - hi.astype(F32)).astype(BF16)
    return _dot(hi, bd) + _dot(lo, bd)


@jax.custom_vjp
def _head_mean(x2, bd):
    return _head_mean_impl(x2, bd)


_head_mean.defvjp(lambda x2, bd: (_head_mean_impl(x2, bd), bd),
                  lambda bd, g: (_head_mean_impl(g, bd), jnp.zeros_like(bd)))


def _head_norm(x, g_tiled, bd):
    return x * lax.rsqrt(_head_mean(x * x, bd) + EPS) * g_tiled


def _silu(z):
    return z * jax.nn.sigmoid(z)


def _dot_nt(a, b):
    return lax.dot_general(a, b, (((1,), (1,)), ((), ())), preferred_element_type=F32)


def _dot_tn(a, b):
    return lax.dot_general(a, b, (((0,), (0,)), ((), ())), preferred_element_type=F32)


def _dot(a, b):
    return jnp.dot(a, b, preferred_element_type=F32)


def _swa_fn(qk, vz, qkp, vzp, qg, kg, sinks, bd, bias, first):
    q = _head_norm(qk[:, :768], qg, bd)
    k2 = jnp.concatenate([qkp[:, 768:], qk[:, 768:]], axis=0)
    k2 = _head_norm(k2, kg, bd[:256, :256])
    v2 = jnp.concatenate([vzp[:, :256], vz[:, :256]], axis=0)
    z = vz[:, 256:]
    cols = A_GROUP * WINDOW
    kj = lax.broadcasted_iota(jnp.int32, (2 * WINDOW, cols), 0)
    no_prev = kj < WINDOW * first.astype(jnp.int32)
    qtb = jnp.transpose(q).astype(BF16)
    kb = k2.astype(BF16)
    vtb = jnp.transpose(v2).astype(BF16)
    outs = [None] * A_HEADS
    for g in range(A_HEADS // A_GROUP):
        heads = [A_GROUP * g + u for u in range(A_GROUP)]
        qs = jnp.concatenate([qtb[64 * h:64 * h + 64, :] for h in heads], axis=1)
        s = _dot(kb[:, 64 * g:64 * g + 64], qs) * (HEAD_DIM ** -0.5) + bias[g]
        s = jnp.where(no_prev, NEG, s)
        sink = jnp.concatenate([jnp.broadcast_to(sinks[:, h:h + 1], (1, WINDOW)) for h in heads], axis=1)
        m = lax.stop_gradient(jnp.maximum(jnp.max(s, axis=0, keepdims=True), sink))
        p = jnp.exp(s - m)
        den = jnp.sum(p, axis=0, keepdims=True) + jnp.exp(sink - m)
        o = _dot(vtb[64 * g:64 * g + 64, :], (p * (1.0 / den)).astype(BF16))
        for u, h in enumerate(heads):
            outs[h] = o[:, WINDOW * u:WINDOW * u + WINDOW]
    return jnp.transpose(jnp.concatenate(outs, axis=0)) * _silu(z)


def _swa_bias():
    qi = np.arange(WINDOW)[None, :]
    kj = np.arange(2 * WINDOW)[:, None]
    rel = qi + WINDOW - kj
    valid = (rel >= 0) & (rel < WINDOW)
    out = np.zeros((A_HEADS // A_GROUP, 2 * WINDOW, A_GROUP * WINDOW), np.float32)
    for h in range(A_HEADS):
        slope = np.float32(2.0 ** (-8.0 * (h + 1) / A_HEADS))
        blk = np.where(valid, -slope * rel.astype(np.float32), np.float32(NEG))
        g, u = divmod(h, A_GROUP)
        out[g, :, WINDOW * u:WINDOW * u + WINDOW] = blk
    return jnp.asarray(out)


def _mem_fn(qz, mkv, qg, kg, bd):
    q = _head_norm(qz[:, :512], qg, bd).astype(BF16)
    k = _head_norm(mkv[:, :512], kg, bd).astype(BF16)
    v = mkv[:, 512:].astype(BF16)
    z = qz[:, 512:]
    outs = []
    for h in range(C_HEADS):
        sl = slice(128 * h, 128 * h + 128)
        s = _dot_nt(q[:, sl], k[:, sl]) * (C_HEAD_DIM ** -0.5)
        m = lax.stop_gradient(jnp.max(s, axis=-1, keepdims=True))
        p = jnp.exp(s - m)
        den = jnp.sum(p, axis=-1, keepdims=True)
        outs.append(_dot((p * (1.0 / den)).astype(BF16), v[:, sl]))
    return jnp.concatenate(outs, axis=1) * _silu(z)


def _qn_fn(q, g, bd):
    return _head_norm(q, g, bd) * (HEAD_DIM ** -0.5)


def _kn_fn(k, g, bd):
    return _head_norm(k, g, bd)


def _block_diag(width, hd):
    i = np.arange(width) // hd
    return jnp.asarray((i[:, None] == i[None, :]).astype(np.float32) / hd, BF16)


def _head_sum(width, hd):
    i = np.arange(width) // hd
    return jnp.asarray((i[:, None] == np.arange(LANE)[None, :]).astype(np.float32))


def _rms_fwd(x, g, *, tr, name):
    rows, dm = x.shape

    def body(x_ref, g_ref, o_ref):
        o_ref[...] = _rms(x_ref[...], g_ref[...]).astype(BF16)

    return pl.pallas_call(
        body, grid=(rows // tr,),
        in_specs=[_rowblk(tr, dm), _const((1, dm))],
        out_specs=_rowblk(tr, dm),
        out_shape=SDS((rows, dm), BF16), name=name,
        compiler_params=_params(("parallel",)))(x, g)


def _rms_bwd(x, g, dy, resid, *, tr, name, comm=None):
    rows, dm = x.shape
    want_dx = resid is not None
    n_in = 4 if want_dx else 3
    n_out = 2 if want_dx else 1
    c_in = len(comm.ins) if comm else 0
    c_out = len(comm.out_shapes) if comm else 0
    nb = rows // tr

    def body(*refs):
        x_ref, g_ref, dy_ref = refs[:3]
        r_ref = refs[3] if want_dx else None
        cin = refs[n_in:n_in + c_in]
        outs = refs[n_in + c_in:n_in + c_in + n_out]
        dg_ref = outs[-1]
        cout = refs[n_in + c_in + n_out:n_in + c_in + n_out + c_out]
        csem = refs[n_in + c_in + n_out + c_out:]

        if comm:
            @pl.when(pl.program_id(0) == 0)
            def _():
                comm.start(cin, cout, csem)

        _, vjp = jax.vjp(_rms, x_ref[...], g_ref[...])
        dx, dg = vjp(dy_ref[...])

        @pl.when(pl.program_id(0) == 0)
        def _():
            dg_ref[...] = jnp.zeros_like(dg_ref)

        dg_ref[...] += dg
        if want_dx:
            outs[0][...] = r_ref[...] + dx

        if comm:
            @pl.when(pl.program_id(0) == nb - 1)
            def _():
                comm.finish(cin, cout, csem)

    hbm = pl.BlockSpec(memory_space=pl.ANY)
    ins = [x, g, dy] + ([resid] if want_dx else []) + (list(comm.ins) if comm else [])
    in_specs = ([_rowblk(tr, dm), _const((1, dm)), _rowblk(tr, dm)] + ([_rowblk(tr, dm)] if want_dx else [])
                + [hbm] * c_in)
    out_specs = ([_rowblk(tr, dm)] if want_dx else []) + [_const((1, dm))] + [hbm] * c_out
    out_shape = (([SDS((rows, dm), F32)] if want_dx else []) + [SDS((1, dm), F32)]
                 + (list(comm.out_shapes) if comm else []))
    res = pl.pallas_call(
        body, grid=(nb,), in_specs=in_specs, out_specs=out_specs, out_shape=out_shape,
        scratch_shapes=list(comm.sems) if comm else [], name=name, compiler_params=_params(("arbitrary",)))(*ins)
    return (list(res[:n_out]), list(res[n_out:])) if comm else res


class _Comm:
    def __init__(self, ins, out_shapes, sems, start, finish):
        self.ins, self.out_shapes, self.sems, self.start, self.finish = list(ins), list(out_shapes), list(sems), start, finish


def _matmul(a, b, *, dims, out_dtype, tm, tn, tk, name, add=None, comms=(), vmem=VMEM_LIMIT):
    a_list = list(a) if isinstance(a, (list, tuple)) else [a]
    b_list = list(b) if isinstance(b, (list, tuple)) else [b]
    assert len(a_list) == 1 or dims == "nt"
    assert len(b_list) == 1 or dims == "tn"
    if dims == "tn":
        kdim, m = a_list[0].shape
    else:
        m, kdim = a_list[0].shape[0], sum(p.shape[1] for p in a_list)
    n = b_list[0].shape[0] if dims == "nt" else sum(p.shape[1] for p in b_list)
    tm, tn, tk = min(tm, m), min(tn, n), min(tk, kdim)
    assert m % tm == 0 and n % tn == 0 and kdim % tk == 0, (name, m, n, kdim)
    ni, nj, nk = m // tm, n // tn, kdim // tk
    a_rng, b_rng, pos = [], [], 0
    for p in a_list:
        assert len(a_list) == 1 or p.shape[1] % tk == 0
        a_rng.append((pos, p.shape[1] // tk if len(a_list) > 1 else nk))
        pos += a_rng[-1][1]
    pos = 0
    for p in b_list:
        assert len(b_list) == 1 or p.shape[1] % tn == 0
        b_rng.append((pos, p.shape[1] // tn if len(b_list) > 1 else nj))
        pos += b_rng[-1][1]
    has_add = add is not None
    n_mm_in = len(a_list) + len(b_list) + (1 if has_add else 0)
    c_in = [len(c.ins) for c in comms]
    c_out = [len(c.out_shapes) for c in comms]
    c_sem = [len(c.sems) for c in comms]

    def body(*refs):
        a_refs, b_refs = refs[:len(a_list)], refs[len(a_list):len(a_list) + len(b_list)]
        add_ref = refs[n_mm_in - 1] if has_add else None
        pos = n_mm_in
        cin = []
        for cnt in c_in:
            cin.append(refs[pos:pos + cnt])
            pos += cnt
        o_ref = refs[pos]
        pos += 1
        cout = []
        for cnt in c_out:
            cout.append(refs[pos:pos + cnt])
            pos += cnt
        acc = refs[pos]
        pos += 1
        csem = []
        for cnt in c_sem:
            csem.append(refs[pos:pos + cnt])
            pos += cnt
        i, j, k = pl.program_id(0), pl.program_id(1), pl.program_id(2)

        if comms:
            @pl.when((i == 0) & (j == 0) & (k == 0))
            def _():
                for c, ci, co, cs in zip(comms, cin, cout, csem):
                    c.start(ci, co, cs)

        def accumulate(a_ref, b_ref, first_k, later_k):
            if dims == "nn":
                part = _dot(a_ref[...], b_ref[...])
            elif dims == "nt":
                part = _dot_nt(a_ref[...], b_ref[...])
            else:
                part = _dot_tn(a_ref[...], b_ref[...])

            if first_k:
                @pl.when(k == 0)
                def _():
                    acc[...] = part + add_ref[...] if has_add else part

            if later_k:
                @pl.when(k > 0)
                def _():
                    acc[...] += part

        if len(a_list) > 1:
            for a_ref, (k0, cnt) in zip(a_refs, a_rng):
                @pl.when((k >= k0) & (k < k0 + cnt))
                def _(a_ref=a_ref, k0=k0, cnt=cnt):
                    accumulate(a_ref, b_refs[0], k0 == 0, k0 + cnt > 1)
        elif len(b_list) > 1:
            for b_ref, (j0, cnt) in zip(b_refs, b_rng):
                @pl.when((j >= j0) & (j < j0 + cnt))
                def _(b_ref=b_ref):
                    accumulate(a_refs[0], b_ref, True, nk > 1)
        else:
            accumulate(a_refs[0], b_refs[0], True, nk > 1)

        @pl.when(k == nk - 1)
        def _():
            o_ref[...] = acc[...].astype(out_dtype)

        if comms:
            @pl.when((i == ni - 1) & (j == nj - 1) & (k == nk - 1))
            def _():
                for c, ci, co, cs in zip(comms, cin, cout, csem):
                    c.finish(ci, co, cs)

    def a_spec(k0, cnt):
        if dims == "tn":
            return pl.BlockSpec((tk, tm), lambda i, j, k: (k, i))
        return pl.BlockSpec((tm, tk), lambda i, j, k: (i, jnp.clip(k - k0, 0, cnt - 1)))

    def b_spec(j0, cnt):
        if dims == "nt":
            return pl.BlockSpec((tn, tk), lambda i, j, k: (j, k))
        return pl.BlockSpec((tk, tn), lambda i, j, k: (k, jnp.clip(j - j0, 0, cnt - 1)))

    o_spec = pl.BlockSpec((tm, tn), lambda i, j, k: (i, j))
    hbm = pl.BlockSpec(memory_space=pl.ANY)
    ins = a_list + b_list + ([add] if has_add else []) + [x for c in comms for x in c.ins]
    in_specs = ([a_spec(*r) for r in a_rng] + [b_spec(*r) for r in b_rng] + ([o_spec] if has_add else [])
                + [hbm] * sum(c_in))
    out_specs = [o_spec] + [hbm] * sum(c_out)
    out_shape = [SDS((m, n), out_dtype)] + [s for c in comms for s in c.out_shapes]
    scratch = [pltpu.VMEM((tm, tn), F32)] + [s for c in comms for s in c.sems]
    sem = ("arbitrary",) * 3 if comms else ("parallel", "parallel", "arbitrary")
    res = pl.pallas_call(
        body, grid=(ni, nj, nk), in_specs=in_specs, out_specs=out_specs, out_shape=out_shape, scratch_shapes=scratch,
        name=name, compiler_params=_params(sem, vmem))(*ins)
    if not comms:
        return res[0]
    outs, pos = [], 1
    for cnt in c_out:
        outs.append(list(res[pos:pos + cnt]))
        pos += cnt
    return res[0], outs


def _swa_specs(nb, blk=lambda n: n):
    cur = lambda off: pl.BlockSpec((pl.Element(WINDOW), pl.Element(1024)), lambda n: (blk(n) * WINDOW, off))
    prev = lambda off: pl.BlockSpec((pl.Element(WINDOW), pl.Element(1024)),
                                    lambda n: (jnp.maximum(blk(n) - 1, 0) * WINDOW, off))
    return [cur(QA), cur(VA), prev(QA), prev(VA),
            _const((1, 768)), _const((1, 256)), _const((1, LANE)), _const((768, 768)),
            _const((A_HEADS // A_GROUP, 2 * WINDOW, A_GROUP * WINDOW))]


def _swa_fwd(proj, qg, kg, sinks, bd, bias):
    s = proj.shape[0]
    nb = s // WINDOW

    def body(qk_ref, vz_ref, qkp_ref, vzp_ref, qg_ref, kg_ref, sk_ref, bd_ref, bias_ref, o_ref):
        first = pl.program_id(0) == 0
        o_ref[...] = _swa_fn(qk_ref[...], vz_ref[...], qkp_ref[...], vzp_ref[...], qg_ref[...], kg_ref[...],
                             sk_ref[...], bd_ref[...], bias_ref[...], first).astype(BF16)

    return pl.pallas_call(
        body, grid=(nb,), in_specs=_swa_specs(nb), out_specs=_rowblk(WINDOW, 768),
        out_shape=SDS((s, 768), BF16), name="swa_fwd",
        compiler_params=_params(("parallel",)))(proj, proj, proj, proj, qg, kg, sinks, bd, bias)


def _swa_bwd(proj, qg, kg, sinks, bd, bias, dga):
    s = proj.shape[0]
    nb = s // WINDOW
    blk = lambda n: nb - 1 - n

    def body(qk_ref, vz_ref, qkp_ref, vzp_ref, qg_ref, kg_ref, sk_ref, bd_ref, bias_ref, dg_ref,
             d_ref, dqg_ref, dkg_ref, dsk_ref, carry):
        first = blk(pl.program_id(0)) == 0
        bd_v = bd_ref[...]
        bias_v = bias_ref[...]
        fn = lambda qk, vz, qkp, vzp, qg_, kg_, sk: _swa_fn(qk, vz, qkp, vzp, qg_, kg_, sk, bd_v, bias_v, first)
        _, vjp = jax.vjp(fn, qk_ref[...], vz_ref[...], qkp_ref[...], vzp_ref[...], qg_ref[...], kg_ref[...], sk_ref[...])
        dqk, dvz, dqkp, dvzp, dqg, dkg, dsk = vjp(dg_ref[...])

        @pl.when(pl.program_id(0) == 0)
        def _():
            dqg_ref[...] = jnp.zeros_like(dqg_ref)
            dkg_ref[...] = jnp.zeros_like(dkg_ref)
            dsk_ref[...] = jnp.zeros_like(dsk_ref)
            carry[...] = jnp.zeros_like(carry)

        dqg_ref[...] += dqg
        dkg_ref[...] += dkg
        dsk_ref[...] += dsk
        kv = jnp.concatenate([dqk[:, 768:], dvz[:, :256]], axis=1) + carry[...]
        d_ref[...] = jnp.concatenate([dqk[:, :768], kv, dvz[:, 256:]], axis=1).astype(BF16)
        carry[...] = jnp.concatenate([dqkp[:, 768:], dvzp[:, :256]], axis=1)

    rev = lambda w: pl.BlockSpec((WINDOW, w), lambda n: (blk(n), 0))
    return pl.pallas_call(
        body, grid=(nb,), in_specs=_swa_specs(nb, blk) + [rev(768)],
        out_specs=[rev(2048), _const((1, 768)), _const((1, 256)), _const((1, LANE))],
        out_shape=[SDS((s, 2048), BF16), SDS((1, 768), F32), SDS((1, 256), F32), SDS((1, LANE), F32)],
        scratch_shapes=[pltpu.VMEM((WINDOW, 512), F32)],
        name="swa_bwd", compiler_params=_params(("arbitrary",)))(proj, proj, proj, proj, qg, kg, sinks, bd, bias, dga)


def _mem_fwd(proj, mkv, qg, kg, bd, *, tr):
    s = proj.shape[0]

    def body(qz_ref, mkv_ref, qg_ref, kg_ref, bd_ref, o_ref):
        o_ref[...] = _mem_fn(qz_ref[...], mkv_ref[...], qg_ref[...], kg_ref[...], bd_ref[...]).astype(BF16)

    return pl.pallas_call(
        body, grid=(s // tr,),
        in_specs=[_win(tr, 1024, QC), _const(mkv.shape), _const((1, 512)), _const((1, 512)), _const((512, 512))],
        out_specs=_rowblk(tr, 512), out_shape=SDS((s, 512), BF16), name="mem_fwd",
        compiler_params=_params(("parallel",)))(proj, mkv, qg, kg, bd)


def _mem_bwd(proj, mkv, qg, kg, bd, dgc, *, tr):
    s = proj.shape[0]

    def body(qz_ref, mkv_ref, qg_ref, kg_ref, bd_ref, dg_ref, dqz_ref, dmkv_ref, dqg_ref, dkg_ref):
        bd_v = bd_ref[...]
        fn = lambda qz, mkv_, qg_, kg_: _mem_fn(qz, mkv_, qg_, kg_, bd_v)
        _, vjp = jax.vjp(fn, qz_ref[...], mkv_ref[...], qg_ref[...], kg_ref[...])
        dqz, dmkv, dqg, dkg = vjp(dg_ref[...])

        @pl.when(pl.program_id(0) == 0)
        def _():
            dmkv_ref[...] = jnp.zeros_like(dmkv_ref)
            dqg_ref[...] = jnp.zeros_like(dqg_ref)
            dkg_ref[...] = jnp.zeros_like(dkg_ref)

        dmkv_ref[...] += dmkv
        dqg_ref[...] += dqg
        dkg_ref[...] += dkg
        dqz_ref[...] = dqz.astype(BF16)

    return pl.pallas_call(
        body, grid=(s // tr,),
        in_specs=[_win(tr, 1024, QC), _const(mkv.shape), _const((1, 512)), _const((1, 512)), _const((512, 512)),
                  _rowblk(tr, 512)],
        out_specs=[_rowblk(tr, 1024), _const(mkv.shape), _const((1, 512)), _const((1, 512))],
        out_shape=[SDS((s, 1024), BF16), SDS(mkv.shape, F32), SDS((1, 512), F32), SDS((1, 512), F32)],
        name="mem_bwd", compiler_params=_params(("arbitrary",)))(proj, mkv, qg, kg, bd, dgc)


def _log_sigmoid(x):
    return jnp.minimum(x, 0.0) - jnp.log1p(jnp.exp(-jnp.abs(x)))


FOX_TQ, FOX_TK = 512, 512
FOX_FWD_TQ, FOX_FWD_TK = 512, 1024


def _fox_tiles(s):
    return min(FOX_TQ, s), min(FOX_TK, s)


AUG = 128 * B_HEADS
COL_A, COL_B = 64, 67


def _split3(c):
    hi = c.astype(BF16)
    r1 = c - hi.astype(F32)
    mid = r1.astype(BF16)
    lo = (r1 - mid.astype(F32)).astype(BF16)
    return hi, mid, lo


def _expand_mats():
    def mat(col0):
        e = np.zeros((768 + 3 * LANE, AUG), np.float32)
        for h in range(B_HEADS):
            for d in range(HEAD_DIM):
                e[64 * h + d, 128 * h + d] = 1.0
            for part in range(3):
                e[768 + LANE * part + h, 128 * h + col0 + part] = 1.0
        return e

    def ones(col0):
        o = np.zeros((1, AUG), np.float32)
        for h in range(B_HEADS):
            o[0, 128 * h + col0:128 * h + col0 + 3] = 1.0
        return o

    return (jnp.asarray(mat(COL_A), BF16), jnp.asarray(mat(COL_B), BF16), jnp.asarray(ones(COL_A)), jnp.asarray(ones(COL_B)))


def _augment(data_bf16, triple, emat, ones_row):
    parts = [data_bf16] + (list(triple) if triple is not None else [jnp.zeros((data_bf16.shape[0], LANE), BF16)] * 3)
    wide = _dot(jnp.concatenate(parts, axis=1), emat)
    if ones_row is not None:
        wide = wide + ones_row
    return wide


def _compact(wide):
    return jnp.concatenate([wide[:, 128 * h:128 * h + 64] for h in range(wide.shape[1] // 128)], axis=1)


def _lane_of_heads(wide, col, first=0):
    rows = wide.shape[0]
    lane = lax.broadcasted_iota(jnp.int32, (rows, LANE), 1)
    out = jnp.zeros((rows, LANE), F32)
    for h in range(wide.shape[1] // 128):
        out = jnp.where(lane == first + h, wide[:, 128 * h + col:128 * h + col + 1], out)
    return out


def _fox2_prep(proj, fbl, qg, kg, bfor, bd, ea, eb, ones_a, ones_b, *, tr):
    s = proj.shape[0]
    tri = jnp.asarray(np.tril(np.ones((tr, tr), np.float32)))

    def body(q_ref, k_ref, v_ref, fb_ref, qg_ref, kg_ref, bf_ref, bd_ref, tri_ref, ea_ref, eb_ref, oa_ref, ob_ref,
             qat_ref, ka_ref, kat_ref, va_ref, vat_ref, qn_ref, c_ref, carry):
        @pl.when(pl.program_id(0) == 0)
        def _():
            carry[...] = jnp.zeros_like(carry)

        bd_v = bd_ref[...]
        lane = lax.broadcasted_iota(jnp.int32, (tr, LANE), 1)
        logf = jnp.where(lane < N_FORGET, _log_sigmoid(fb_ref[...] + bf_ref[...]), 0.0)
        c = jnp.dot(tri_ref[...], logf, precision=HI, preferred_element_type=F32) + carry[...]
        c_ref[...] = c
        carry[...] = c[tr - 1:tr, :]
        qn = _qn_fn(q_ref[...], qg_ref[...], bd_v).astype(BF16)
        kn = _kn_fn(k_ref[...], kg_ref[...], bd_v).astype(BF16)
        qn_ref[...] = qn
        qat_ref[...] = jnp.transpose(_augment(qn, _split3(c), ea_ref[...], ob_ref[...])).astype(BF16)
        ka = _augment(kn, _split3(-c), eb_ref[...], oa_ref[...])
        ka_ref[...] = ka.astype(BF16)
        kat_ref[...] = jnp.transpose(ka).astype(BF16)
        va = _augment(v_ref[...].astype(BF16), None, ea_ref[...], oa_ref[...])
        va_ref[...] = va.astype(BF16)
        vat_ref[...] = jnp.transpose(va).astype(BF16)

    emat = _const((768 + 3 * LANE, AUG))
    return pl.pallas_call(
        body, grid=(s // tr,),
        in_specs=[_win(tr, 768, QB), _win(tr, 768, KB), _win(tr, 768, VB), _rowblk(tr, LANE), _const((1, 768)),
                  _const((1, 768)), _const((1, LANE)), _const((768, 768)), _const((tr, tr)), emat, emat,
                  _const((1, AUG)), _const((1, AUG))],
        out_specs=[pl.BlockSpec((AUG, tr), lambda i: (0, i)), _rowblk(tr, AUG), pl.BlockSpec((AUG, tr), lambda i: (0, i)),
                   _rowblk(tr, AUG), pl.BlockSpec((AUG, tr), lambda i: (0, i)), _rowblk(tr, 768), _rowblk(tr, LANE)],
        out_shape=[SDS((AUG, s), BF16), SDS((s, AUG), BF16), SDS((AUG, s), BF16), SDS((s, AUG), BF16),
                   SDS((AUG, s), BF16), SDS((s, 768), BF16), SDS((s, LANE), F32)],
        scratch_shapes=[pltpu.VMEM((1, LANE), F32)], name="fox_prep",
        compiler_params=_params(("arbitrary",)))(proj, proj, proj, fbl, qg, kg, bfor, bd, tri, ea, eb, ones_a, ones_b)


def _fox2_fwd(proj, qat, ka, vat):
    s = proj.shape[0]
    tq, tk = min(FOX_FWD_TQ, s), min(FOX_FWD_TK, s)
    nq, nk = s // tq, s // tk

    def last_k(i):
        return (i * tq + tq - 1) // tk

    def body(qt_ref, k_ref, vt_ref, z_ref, gb_ref, yb_ref, lse_ref, acc, m_s):
        i, j = pl.program_id(0), pl.program_id(1)

        @pl.when(j == 0)
        def _():
            acc[...] = jnp.zeros_like(acc)
            m_s[...] = jnp.full_like(m_s, NEG)

        def tile(masked):
            if masked:
                kpos = j * tk + lax.broadcasted_iota(jnp.int32, (tk, tq), 0)
                qpos = i * tq + lax.broadcasted_iota(jnp.int32, (tk, tq), 1)
                mask = kpos <= qpos
            for h in range(B_HEADS):
                sl = slice(128 * h, 128 * h + 128)
                sc = _dot(k_ref[:, sl], qt_ref[sl, :])
                if masked:
                    sc = jnp.where(mask, sc, NEG)
                m_prev = m_s[h:h + 1, :]
                m_new = jnp.maximum(m_prev, jnp.max(sc, axis=0, keepdims=True))
                p = jnp.exp(sc - m_new).astype(BF16)
                acc[sl, :] = jnp.exp(m_prev - m_new) * acc[sl, :] + _dot(vt_ref[sl, :], p)
                m_s[h:h + 1, :] = m_new

        full = j * tk + tk - 1 <= i * tq

        @pl.when(full)
        def _():
            tile(False)

        @pl.when(jnp.logical_and(jnp.logical_not(full), j <= last_k(i)))
        def _():
            tile(True)

        @pl.when(j == nk - 1)
        def _():
            outs = []
            row = lax.broadcasted_iota(jnp.int32, (LANE, tq), 0)
            lse_t = jnp.zeros((LANE, tq), F32)
            for h in range(B_HEADS):
                l_row = acc[128 * h + COL_A:128 * h + COL_A + 1, :]
                outs.append(acc[128 * h:128 * h + 64, :] * (1.0 / l_row))
                lse_t = jnp.where(row == h, m_s[h:h + 1, :] + jnp.log(l_row), lse_t)
            y = jnp.transpose(jnp.concatenate(outs, axis=0))
            yb_ref[...] = y
            gb_ref[...] = (y * _silu(z_ref[...])).astype(BF16)
            lse_ref[...] = jnp.transpose(lse_t)

    kcol = lambda i, j: (0, jnp.minimum(j, last_k(i)))
    return pl.pallas_call(
        body, grid=(nq, nk),
        in_specs=[pl.BlockSpec((AUG, tq), lambda i, j: (0, i)),
                  pl.BlockSpec((tk, AUG), lambda i, j: (jnp.minimum(j, last_k(i)), 0)),
                  pl.BlockSpec((AUG, tk), kcol),
                  pl.BlockSpec((pl.Element(tq), pl.Element(768)), lambda i, j: (i * tq, ZB))],
        out_specs=[pl.BlockSpec((tq, 768), lambda i, j: (i, 0)), pl.BlockSpec((tq, 768), lambda i, j: (i, 0)),
                   pl.BlockSpec((tq, LANE), lambda i, j: (i, 0))],
        out_shape=[SDS((s, 768), BF16), SDS((s, 768), F32), SDS((s, LANE), F32)],
        scratch_shapes=[pltpu.VMEM((AUG, tq), F32), pltpu.VMEM((16, tq), F32)],
        name="fox_fwd", compiler_params=_params(("parallel", "arbitrary")))(qat, ka, vat, proj)


def _fox2_bwd_pre(proj, yb, dgb, qn, c, lse, hsum, ea, ones_b, *, tr):
    s = proj.shape[0]

    def body(z_ref, y_ref, dg_ref, qn_ref, c_ref, lse_ref, hs_ref, ea_ref, ob_ref,
             qa_ref, qat_ref, dya_ref, dyat_ref, dz_ref):
        z, y, dg = z_ref[...], y_ref[...], dg_ref[...]
        sg = jax.nn.sigmoid(z)
        dy = dg * (z * sg)
        dz_ref[...] = (dg * y * (sg * (1.0 + z * (1.0 - sg)))).astype(BF16)
        delta = jnp.dot(dy * y, hs_ref[...], precision=HI, preferred_element_type=F32)
        e = ea_ref[...]
        dya = _augment(dy.astype(BF16), _split3(-delta), e, None)
        dya_ref[...] = dya.astype(BF16)
        dyat_ref[...] = jnp.transpose(dya).astype(BF16)
        qa = _augment(qn_ref[...], _split3(c_ref[...] - lse_ref[...]), e, ob_ref[...])
        qa_ref[...] = qa.astype(BF16)
        qat_ref[...] = jnp.transpose(qa).astype(BF16)

    return pl.pallas_call(
        body, grid=(s // tr,),
        in_specs=[_win(tr, 768, ZB), _rowblk(tr, 768), _rowblk(tr, 768), _rowblk(tr, 768), _rowblk(tr, LANE),
                  _rowblk(tr, LANE), _const((768, LANE)), _const((768 + 3 * LANE, AUG)), _const((1, AUG))],
        out_specs=[_rowblk(tr, AUG), pl.BlockSpec((AUG, tr), lambda i: (0, i)), _rowblk(tr, AUG),
                   pl.BlockSpec((AUG, tr), lambda i: (0, i)), _rowblk(tr, 768)],
        out_shape=[SDS((s, AUG), BF16), SDS((AUG, s), BF16), SDS((s, AUG), BF16), SDS((AUG, s), BF16),
                   SDS((s, 768), BF16)], name="fox_bwd_pre",
        compiler_params=_params(("parallel",)))(proj, yb, dgb, qn, c, lse, hsum, ea, ones_b)


def _fox2_bwd(qb, qbt, ka, kat, va, dya, dyat):
    s = qb.shape[0]
    tq, tk = _fox_tiles(s)
    nq, nk = s // tq, s // tk
    ng = 2
    gh = B_HEADS // ng
    gw = 128 * gh

    def first_q(j):
        return (j * tk) // tq

    def body(q_ref, qt_ref, k_ref, kt_ref, v_ref, dy_ref, dyt_ref, dq_hbm, dk_ref, dv_ref, dck_ref,
             dq_acc, dk_acc, dv_acc, sem):
        g, j, i = pl.program_id(0), pl.program_id(1), pl.program_id(2)

        @pl.when((j == 0) & (i == 0))
        def _():
            dq_acc[...] = jnp.zeros_like(dq_acc)

        @pl.when(i == 0)
        def _():
            dk_acc[...] = jnp.zeros_like(dk_acc)
            dv_acc[...] = jnp.zeros_like(dv_acc)

        def tile(masked):
            if masked:
                kpos = j * tk + lax.broadcasted_iota(jnp.int32, (tk, tq), 0)
                qpos = i * tq + lax.broadcasted_iota(jnp.int32, (tk, tq), 1)
                mask = kpos <= qpos
            cols = pl.ds(pl.multiple_of(i * tq, tq), tq)
            for h in range(gh):
                sl = slice(128 * h, 128 * h + 128)
                sc = _dot(k_ref[:, sl], qt_ref[sl, :])
                if masked:
                    sc = jnp.where(mask, sc, NEG)
                p = jnp.exp(sc)
                ds = (p * _dot(v_ref[:, sl], dyt_ref[sl, :])).astype(BF16)
                dv_acc[:, sl] += _dot(p.astype(BF16), dy_ref[:, sl])
                dk_acc[:, sl] += _dot(ds, q_ref[:, sl])
                dq_acc[sl, cols] += _dot(kt_ref[sl, :], ds)

        full = j * tk + tk - 1 <= i * tq

        @pl.when(full)
        def _():
            tile(False)

        @pl.when(jnp.logical_and(jnp.logical_not(full), i >= first_q(j)))
        def _():
            tile(True)

        @pl.when(i == nq - 1)
        def _():
            dkw = dk_acc[...]
            dk_ref[...] = _compact(dkw)
            dv_ref[...] = _compact(dv_acc[...]).astype(BF16)
            dck_ref[...] = -_lane_of_heads(dkw, COL_B, gh * g)

        @pl.when((j == nk - 1) & (i == nq - 1))
        def _():
            cp = pltpu.make_async_copy(dq_acc, dq_hbm.at[pl.ds(pl.multiple_of(g * gw, gw), gw)], sem)
            cp.start()
            cp.wait()

    qrow = pl.BlockSpec((tq, gw), lambda g, j, i: (jnp.maximum(i, first_q(j)), g))
    qcol = pl.BlockSpec((gw, tq), lambda g, j, i: (g, jnp.maximum(i, first_q(j))))
    krow = pl.BlockSpec((tk, gw), lambda g, j, i: (j, g))
    kcol = pl.BlockSpec((gw, tk), lambda g, j, i: (g, j))
    kout = pl.BlockSpec((tk, gw // 2), lambda g, j, i: (j, g))
    return pl.pallas_call(
        body, grid=(ng, nk, nq),
        in_specs=[qrow, qcol, krow, kcol, krow, qrow, qcol],
        out_specs=[pl.BlockSpec(memory_space=pl.ANY), kout, kout,
                   pl.BlockSpec((None, tk, LANE), lambda g, j, i: (g, j, 0))],
        out_shape=[SDS((AUG, s), F32), SDS((s, 768), F32), SDS((s, 768), BF16), SDS((ng, s, LANE), F32)],
        scratch_shapes=[pltpu.VMEM((gw, s), F32), pltpu.VMEM((tk, gw), F32), pltpu.VMEM((tk, gw), F32),
                        pltpu.SemaphoreType.DMA],
        name="fox_bwd", compiler_params=_params(("arbitrary",) * 3))(qb, qbt, ka, kat, va, dya, dyat)


def _fox2_bwd_post(proj, fbl, qg, kg, bfor, bd, dqa, dkn, dck, *, tr):
    s = proj.shape[0]
    nb = s // tr
    triu = jnp.asarray(np.triu(np.ones((tr, tr), np.float32)))
    rev = lambda i: nb - 1 - i

    def body(q_ref, k_ref, fb_ref, qg_ref, kg_ref, bf_ref, bd_ref, tri_ref, dqa_ref, dkn_ref, dck_ref,
             dq_ref, dk_ref, dfb_ref, dqg_ref, dkg_ref, dbf_ref, carry):
        @pl.when(pl.program_id(0) == 0)
        def _():
            carry[...] = jnp.zeros_like(carry)
            dqg_ref[...] = jnp.zeros_like(dqg_ref)
            dkg_ref[...] = jnp.zeros_like(dkg_ref)
            dbf_ref[...] = jnp.zeros_like(dbf_ref)

        bd_v = bd_ref[...]
        dqw = jnp.transpose(dqa_ref[...])
        _, vjp_q = jax.vjp(lambda q, g: _qn_fn(q, g, bd_v), q_ref[...], qg_ref[...])
        dq, dqg = vjp_q(_compact(dqw))
        _, vjp_k = jax.vjp(lambda k, g: _kn_fn(k, g, bd_v), k_ref[...], kg_ref[...])
        dk, dkg = vjp_k(dkn_ref[...])
        dq_ref[...] = dq.astype(BF16)
        dk_ref[...] = dk.astype(BF16)
        dqg_ref[...] += dqg
        dkg_ref[...] += dkg

        dc = _lane_of_heads(dqw, COL_A) + (dck_ref[0] + dck_ref[1])
        dlogf = jnp.dot(tri_ref[...], dc, precision=HI, preferred_element_type=F32) + carry[...]
        carry[...] = dlogf[0:1, :]
        lane = lax.broadcasted_iota(jnp.int32, (tr, LANE), 1)
        xf = fb_ref[...] + bf_ref[...]
        dfb = jnp.where(lane < N_FORGET, dlogf * jax.nn.sigmoid(-xf), 0.0)
        dfb_ref[...] = dfb.astype(BF16)
        dbf_ref[...] += jnp.sum(dfb, axis=0, keepdims=True)

    rb = lambda w: pl.BlockSpec((tr, w), lambda i: (rev(i), 0))
    wn = lambda w, off: pl.BlockSpec((pl.Element(tr), pl.Element(w)), lambda i: (rev(i) * tr, off))
    return pl.pallas_call(
        body, grid=(nb,),
        in_specs=[wn(768, QB), wn(768, KB), rb(LANE), _const((1, 768)), _const((1, 768)), _const((1, LANE)),
                  _const((768, 768)), _const((tr, tr)), pl.BlockSpec((AUG, tr), lambda i: (0, rev(i))), rb(768),
                  pl.BlockSpec((2, tr, LANE), lambda i: (0, rev(i), 0))],
        out_specs=[rb(768), rb(768), rb(LANE), _const((1, 768)), _const((1, 768)), _const((1, LANE))],
        out_shape=[SDS((s, 768), BF16), SDS((s, 768), BF16), SDS((s, LANE), BF16), SDS((1, 768), F32),
                   SDS((1, 768), F32), SDS((1, LANE), F32)],
        scratch_shapes=[pltpu.VMEM((1, LANE), F32)], name="fox_bwd_post",
        compiler_params=_params(("arbitrary",)))(proj, proj, fbl, qg, kg, bfor, bd, triu, dqa, dkn, dck)


def _merge_specs(tr):
    row = lambda w: pl.BlockSpec((tr, w), lambda i, j: (i, 0))
    shard = lambda r: pl.BlockSpec((None, r, 512), lambda i, j: (j, 0, 0))
    gate = lambda b: pl.BlockSpec((tr, 512), lambda i, j: (i, (GATE + 2048 * b) // 512 + j))
    return [row(768), row(768), row(512), shard(768), shard(768), shard(512), gate(0), gate(1), gate(2)]


def _merge_fwd(proj, ga, gb, gc, wa, wb, wc, *, tr):
    s = proj.shape[0]

    def body(ga_ref, gb_ref, gc_ref, wa_ref, wb_ref, wc_ref, l0_ref, l1_ref, l2_ref, y_ref):
        ua = _dot(ga_ref[...], wa_ref[...])
        ub = _dot(gb_ref[...], wb_ref[...])
        uc = _dot(gc_ref[...], wc_ref[...])
        y = jax.nn.sigmoid(l0_ref[...]) * ua + jax.nn.sigmoid(l1_ref[...]) * ub + jax.nn.sigmoid(l2_ref[...]) * uc
        y_ref[...] = y.astype(BF16)

    return pl.pallas_call(
        body, grid=(s // tr, N_CHIPS), in_specs=_merge_specs(tr),
        out_specs=pl.BlockSpec((tr, 512), lambda i, j: (i, j)), out_shape=SDS((s, D_MODEL), BF16), name="merge_fwd",
        compiler_params=_params(("parallel", "arbitrary")))(ga, gb, gc, wa, wb, wc, proj, proj, proj)


def _merge_bwd(proj, ga, gb, gc, wa, wb, wc, dy, *, tr):
    s = proj.shape[0]

    def body(ga_ref, gb_ref, gc_ref, wa_ref, wb_ref, wc_ref, l0_ref, l1_ref, l2_ref, dy_ref,
             dl0_ref, dl1_ref, dl2_ref, dua_ref, dub_ref, duc_ref, dga_ref, dgb_ref, dgc_ref):
        j = pl.program_id(1)
        dyv = dy_ref[...]

        @pl.when(j == 0)
        def _():
            dga_ref[...] = jnp.zeros_like(dga_ref)
            dgb_ref[...] = jnp.zeros_like(dgb_ref)
            dgc_ref[...] = jnp.zeros_like(dgc_ref)

        for g_ref, w_ref, l_ref, dl_ref, du_ref, dg_ref in (
                (ga_ref, wa_ref, l0_ref, dl0_ref, dua_ref, dga_ref),
                (gb_ref, wb_ref, l1_ref, dl1_ref, dub_ref, dgb_ref),
                (gc_ref, wc_ref, l2_ref, dl2_ref, duc_ref, dgc_ref)):
            w = w_ref[...]
            u = _dot(g_ref[...], w)
            sg = jax.nn.sigmoid(l_ref[...])
            dl_ref[...] = (dyv * u * sg * (1.0 - sg)).astype(BF16)
            du = (dyv * sg).astype(BF16)
            du_ref[...] = du
            dg_ref[...] += _dot_nt(du, w)

    blk = pl.BlockSpec((tr, 512), lambda i, j: (i, j))
    row = lambda w: pl.BlockSpec((tr, w), lambda i, j: (i, 0))
    big = SDS((s, D_MODEL), BF16)
    return pl.pallas_call(
        body, grid=(s // tr, N_CHIPS), in_specs=_merge_specs(tr) + [blk],
        out_specs=[blk] * 6 + [row(768), row(768), row(512)],
        out_shape=[big] * 6 + [SDS((s, 768), F32), SDS((s, 768), F32), SDS((s, 512), F32)], name="merge_bwd",
        compiler_params=_params(("parallel", "arbitrary")))(ga, gb, gc, wa, wb, wc, proj, proj, proj, dy)


def _out_loss(y, wo, x, tgt, *, tr, tn):
    s = x.shape[0]

    def body(y_ref, w_ref, x_ref, t_ref, d_ref, db_ref, sq_ref, dy_ref):
        @pl.when((pl.program_id(0) == 0) & (pl.program_id(1) == 0))
        def _():
            sq_ref[...] = jnp.zeros_like(sq_ref)

        @pl.when(pl.program_id(1) == 0)
        def _():
            dy_ref[...] = jnp.zeros_like(dy_ref)

        w = w_ref[...]
        out = x_ref[...] + _dot(y_ref[...], w)
        diff = out - t_ref[...]
        sq_ref[...] += jnp.sum(diff * diff, axis=0, keepdims=True)
        d = diff * (1.0 / D_MODEL)
        d_ref[...] = d
        db = d.astype(BF16)
        db_ref[...] = db
        dy_ref[...] += _dot_nt(db, w)

    blk = pl.BlockSpec((tr, tn), lambda i, j: (i, j))
    row = pl.BlockSpec((tr, D_MODEL), lambda i, j: (i, 0))
    return pl.pallas_call(
        body, grid=(s // tr, D_MODEL // tn),
        in_specs=[row, pl.BlockSpec((D_MODEL, tn), lambda i, j: (0, j)), blk, blk],
        out_specs=[blk, blk, _const((1, tn)), row],
        out_shape=[SDS((s, D_MODEL), F32), SDS((s, D_MODEL), BF16), SDS((1, tn), F32), SDS((s, D_MODEL), F32)],
        name="out_loss", compiler_params=_params(("arbitrary", "arbitrary")))(y, wo, x, tgt)


def _tile_gain(g, reps):
    return jnp.tile(g.reshape(1, -1), (1, reps))


def _pad_lane(v):
    v = v.reshape(1, -1)
    return jnp.pad(v, ((0, 0), (0, LANE - v.shape[1])))


def _local_step(x, mem, tgt, w_main, w_fb, w_small, norm_gain, mem_norm_gain, b_forget,
                q_gain_a, k_gain_a, sinks_a, q_gain_b, k_gain_b, q_gain_c, k_gain_c, core=None):
    s = x.shape[0]
    tr = min(512, s)
    bd64 = _block_diag(768, HEAD_DIM)
    bd128 = _block_diag(512, C_HEAD_DIM)
    hsum = _head_sum(768, HEAD_DIM)
    qga, kga = _tile_gain(q_gain_a, 12), _tile_gain(k_gain_a, 4)
    qgb, kgb = _tile_gain(q_gain_b, 12), _tile_gain(k_gain_b, 12)
    qgc, kgc = _tile_gain(q_gain_c, 4), _tile_gain(k_gain_c, 4)
    sinks = _pad_lane(sinks_a)
    bfor = _pad_lane(b_forget)

    hn = _rms_fwd(x, norm_gain, tr=tr, name="rms_x")
    on_mesh = core is not None
    if on_mesh:
        proj, (gathered,) = _matmul(hn, w_main, dims="nn", out_dtype=F32, tm=1024, tn=1024, tk=D_MODEL, name="proj_main",
                                    comms=[_gather_comm(list(w_small))])
        w_mk, wa, wb, wc, wo = gathered
        w_mk, wo = w_mk.reshape(D_MODEL, 1024), wo.reshape(D_MODEL, D_MODEL)
    else:
        proj = _matmul(hn, w_main, dims="nn", out_dtype=F32, tm=1024, tn=1024, tk=D_MODEL, name="proj_main")
        w_mk, wa, wb, wc, wo = w_small
    fbl = _matmul(hn, w_fb, dims="nn", out_dtype=F32, tm=1024, tn=LANE, tk=D_MODEL, name="proj_forget")
    memn = _rms_fwd(mem, mem_norm_gain, tr=mem.shape[0], name="rms_mem")
    mkv = _matmul(memn, w_mk, dims="nn", out_dtype=F32, tm=256, tn=512, tk=D_MODEL, name="mem_kv")

    swa_bias = _swa_bias()
    ga = _swa_fwd(proj, qga, kga, sinks, bd64, swa_bias)
    ea, eb, ones_a, ones_b = _expand_mats()
    tf = min(256, s)
    qat, ka, kat, va, vat, qn, cfox = _fox2_prep(proj, fbl, qgb, kgb, bfor, bd64, ea, eb, ones_a, ones_b, tr=tf)
    gb, yb, lse = _fox2_fwd(proj, qat, ka, vat)
    gc = _mem_fwd(proj, mkv, qgc, kgc, bd128, tr=tr)
    y = _merge_fwd(proj, ga, gb, gc, wa, wb, wc, tr=tr)
    dout, dout_b, sq, dy = _out_loss(y, wo, x, tgt, tr=tr, tn=512)

    d_wo = _matmul(y, dout_b, dims="tn", out_dtype=F32, tm=1024, tn=512, tk=4096, name="dw_out")
    dl0, dl1, dl2, dua, dub, duc, dga, dgb, dgc = _merge_bwd(proj, ga, gb, gc, wa, wb, wc, dy, tr=tr)
    d_wa = _matmul(ga, dua, dims="tn", out_dtype=F32, tm=768, tn=512, tk=4096, name="dw_branch_a")
    d_wb = _matmul(gb, dub, dims="tn", out_dtype=F32, tm=768, tn=512, tk=4096, name="dw_branch_b")
    d_wc = _matmul(gc, duc, dims="tn", out_dtype=F32, tm=512, tn=512, tk=4096, name="dw_branch_c")

    dproj_a, d_qga, d_kga, d_sinks = _swa_bwd(proj, qga, kga, sinks, bd64, swa_bias, dga)

    qab, qabt, dya, dyat, dzb = _fox2_bwd_pre(proj, yb, dgb, qn, cfox, lse, hsum, ea, ones_b, tr=tf)
    dqa, dkn, dvb, dck = _fox2_bwd(qab, qabt, ka, kat, va, dya, dyat)
    dqb, dkb, dfb, d_qgb, d_kgb, d_bf = _fox2_bwd_post(proj, fbl, qgb, kgb, bfor, bd64, dqa, dkn, dck, tr=tf)

    dproj_c, dmkv, d_qgc, d_kgc = _mem_bwd(proj, mkv, qgc, kgc, bd128, dgc, tr=tr)
    dmkv_b = dmkv.astype(BF16)
    d_wmk = _matmul(memn, dmkv_b, dims="tn", out_dtype=F32, tm=1024, tn=512, tk=256, name="dw_mem_kv")
    dmemn = _matmul(dmkv_b, w_mk, dims="nt", out_dtype=F32, tm=256, tn=512, tk=1024, name="dmemn")
    (d_mem_gain,) = _rms_bwd(mem, mem_norm_gain, dmemn, None, tr=mem.shape[0], name="rms_mem_bwd")

    dproj = [dproj_a, jnp.concatenate([dqb, dkb, dvb, dzb, dproj_c], axis=1), dl0, dl1, dl2]
    dhn_f = _matmul(dfb, w_fb, dims="nt", out_dtype=F32, tm=1024, tn=512, tk=LANE, name="dhn_forget")
    d_wfb = _matmul(hn, dfb, dims="tn", out_dtype=F32, tm=1024, tn=LANE, tk=512, name="dw_forget")
    big = {}
    if on_mesh:
        half = D_MODEL // 2
        c0 = core[0]
        hn_other = lax.dynamic_slice(hn, (0, (1 - c0) * half), (s, half))
        hn_own = lax.dynamic_slice(hn, (0, c0 * half), (s, half))
        g1, k1 = [d_wmk, d_wa, d_wb, d_wc, d_wo], [2, 3, 4, 5, 6]
        d_other, (got1,) = _matmul(hn_other, dproj, dims="tn", out_dtype=BF16, tm=1024, tn=512, tk=4096,
                                   name="dw_main_other", comms=[_exchange_comm(g1, k1)])
        h1 = [_add_half(g, got, core, HALF_AXIS[k], name=f"add_half_{k}") for g, got, k in zip(g1, got1, k1)]
        d_own, (got0, parts1) = _matmul(
            hn_own, dproj, dims="tn", out_dtype=F32, tm=1024, tn=512, tk=4096, name="dw_main_own",
            comms=[_exchange_comm([d_other, d_wfb], [0, 1], whole=(0,)), _scatter_comm(h1, k1)])
        h0 = [_add_pair(d_own, got0[0], name="add_pair_main"), _add_half(d_wfb, got0[1], core, 0, name="add_half_1")]
        sums1 = [_sum4(p, name=f"sum4_{k}") for p, k in zip(parts1, k1)]
        dhn, (parts0, theirs1) = _matmul(dproj, w_main, dims="nt", out_dtype=F32, tm=1024, tn=512, tk=2048, vmem=VMEM_WIDE, name="dhn",
                                         add=dhn_f, comms=[_scatter_comm(h0, [0, 1]), _swap_comm(sums1)])
        sums0 = [_sum4(p, name=f"sum4_{k}") for p, k in zip(parts0, (0, 1))]
        (grad_x, d_gain), theirs0 = _rms_bwd(x, norm_gain, dhn, dout, tr=tr, name="rms_x_bwd", comm=_swap_comm(sums0))
        big = dict(sums=sums0 + sums1, theirs=list(theirs0) + list(theirs1))
    else:
        dhn = _matmul(dproj, w_main, dims="nt", out_dtype=F32, tm=1024, tn=512, tk=2048, vmem=VMEM_WIDE, name="dhn", add=dhn_f)
        d_wmain = _matmul(hn, dproj, dims="tn", out_dtype=F32, tm=1024, tn=512, tk=4096, name="dw_main")
        big = dict(d_wmain=d_wmain, d_wfb=d_wfb, d_wmk=d_wmk, d_wa=d_wa, d_wb=d_wb, d_wc=d_wc, d_wo=d_wo)
        grad_x, d_gain = _rms_bwd(x, norm_gain, dhn, dout, tr=tr, name="rms_x_bwd")

    fold = lambda g, reps: jnp.sum(g.reshape(reps, -1), axis=0, keepdims=True)
    return dict(
        sq=sq, grad_x=grad_x, **big,
        d_gain=d_gain, d_mem_gain=d_mem_gain, d_bf=d_bf[:, :N_FORGET],
        d_qga=fold(d_qga, 12), d_kga=fold(d_kga, 4), d_sinks=d_sinks[:, :A_HEADS],
        d_qgb=fold(d_qgb, 12), d_kgb=fold(d_kgb, 12), d_qgc=fold(d_qgc, 4), d_kgc=fold(d_kgc, 4))


PACK_ROWS = 256
FORGET_IN_SHARD = FORGET_COL - SHARD_COLS
AFTER_FORGET = FORGET_COL - SLAB_START[1]
END_CHIP1 = 2 * SHARD_COLS - N_FORGET - SLAB_START[1]


def _pack_w_in(chip, w):
    rows = w.shape[1]
    tr = PACK_ROWS

    def body(k_ref, w_ref, o_ref, scr):
        scr[...] = jnp.zeros_like(scr)
        scr[pl.ds(0, SHARD_COLS), :] = w_ref[...]
        v = jnp.transpose(scr[...])
        k = k_ref[0]
        col = lax.broadcasted_iota(jnp.int32, (tr, SLAB), 1)
        no_forget = jnp.zeros((tr, LANE), BF16)

        @pl.when(k == 0)
        def _():
            o_ref[:, 0:SLAB] = v.astype(BF16)
            o_ref[:, SLAB:] = no_forget

        @pl.when(k == 1)
        def _():
            before = pltpu.roll(v, SLAB_SHIFT[1], axis=1)
            after = pltpu.roll(v, SLAB - (N_FORGET - SLAB_SHIFT[1]), axis=1)
            slab = jnp.where(col < AFTER_FORGET, before, jnp.where(col < END_CHIP1, after, 0.0))
            o_ref[:, 0:SLAB] = slab.astype(BF16)
            f = pltpu.roll(v, SLAB - FORGET_IN_SHARD, axis=1)[:, :LANE]
            o_ref[:, SLAB:] = jnp.where(col[:, :LANE] < N_FORGET, f, 0.0).astype(BF16)

        for kk in (2, 3):
            @pl.when(k == kk)
            def _(kk=kk):
                o_ref[:, 0:SLAB] = pltpu.roll(v, SLAB_SHIFT[kk], axis=1).astype(BF16)
                o_ref[:, SLAB:] = no_forget

    return pl.pallas_call(
        body, grid_spec=pltpu.PrefetchScalarGridSpec(
            num_scalar_prefetch=1, grid=(rows // tr,),
            in_specs=[pl.BlockSpec((SHARD_COLS, tr), lambda i, k: (0, i))],
            out_specs=pl.BlockSpec((None, tr, SLAB + LANE), lambda i, k: (k[0], i, 0)),
            scratch_shapes=[pltpu.VMEM((SLAB, tr), F32)]),
        out_shape=SDS((N_CHIPS, rows, SLAB + LANE), BF16), name="pack_w_in",
        compiler_params=_params(("arbitrary",)))(chip, w)


def _merge_slabs(g):
    rows = g.shape[1]
    tr = PACK_ROWS
    t = [s // LANE for s in SLAB_START]
    n_t = SLAB // LANE

    def body(g_ref, m_ref, f_ref):
        for k in range(N_CHIPS):
            lo = t[k] + (1 if k > 0 else 0)
            hi = t[k + 1] if k + 1 < N_CHIPS else t[k] + n_t
            m_ref[:, lo * LANE:hi * LANE] = g_ref[k, :, (lo - t[k]) * LANE:(hi - t[k]) * LANE]
            if k + 1 < N_CHIPS:
                a = g_ref[k, :, (hi - t[k]) * LANE:(hi - t[k] + 1) * LANE].astype(F32)
                b = g_ref[k + 1, :, 0:LANE].astype(F32)
                m_ref[:, hi * LANE:(hi + 1) * LANE] = (a + b).astype(BF16)
        f_ref[...] = g_ref[1, :, SLAB:]

    return pl.pallas_call(
        body, grid=(rows // tr,),
        in_specs=[pl.BlockSpec((N_CHIPS, tr, SLAB + LANE), lambda i: (0, i, 0))],
        out_specs=[_rowblk(tr, P_MAIN), _rowblk(tr, LANE)],
        out_shape=[SDS((rows, P_MAIN), BF16), SDS((rows, LANE), BF16)], name="merge_slabs",
        compiler_params=_params(("parallel",)))(g)


def _adamw_math(w, g, m, v):
    nm = ADAM_B1 * m + (1.0 - ADAM_B1) * g
    nv = ADAM_B2 * v + (1.0 - ADAM_B2) * (g * g)
    m_hat = nm / (1.0 - ADAM_B1 ** ADAM_STEP)
    v_hat = nv / (1.0 - ADAM_B2 ** ADAM_STEP)
    delta = -ADAM_LR * (m_hat / (jnp.sqrt(v_hat) + ADAM_EPS) + ADAM_WD * w)
    return delta, nm, nv


def _adamw(g, w, m, v, *, tr, name):
    rows, cols = w.shape
    tr = min(tr, rows)

    def body(g_ref, w_ref, m_ref, v_ref, d_ref, nm_ref, nv_ref):
        d, nm, nv = _adamw_math(w_ref[...], g_ref[...], m_ref[...], v_ref[...])
        d_ref[...] = d
        nm_ref[...] = nm
        nv_ref[...] = nv

    spec = _rowblk(tr, cols)
    return pl.pallas_call(
        body, grid=(rows // tr,), in_specs=[spec] * 4, out_specs=[spec] * 3,
        out_shape=[SDS((rows, cols), F32)] * 3, name=name, compiler_params=_params(("parallel",)))(g, w, m, v)


def _adamw_w_in(chip_core, slab_mine, slab_theirs, forget_mine, forget_theirs, w, m, v):
    rows = w.shape[1]
    tr = PACK_ROWS // 2
    nbh = rows // 2 // tr

    def body(k_ref, sa_ref, sb_ref, fa_ref, fb_ref, w_ref, m_ref, v_ref, g_ref, d_ref, nm_ref, nv_ref):
        use_mine = pl.program_id(0) // nbh == k_ref[1]
        sl = jnp.where(use_mine, sa_ref[...], sb_ref[...])
        f_tile = jnp.where(use_mine, fa_ref[...], fb_ref[...])
        k = k_ref[0]

        def emit(wide):
            g = jnp.transpose(wide)[:SHARD_COLS, :]
            g_ref[...] = g
            d, nm, nv = _adamw_math(w_ref[...], g, m_ref[...], v_ref[...])
            d_ref[...] = d
            nm_ref[...] = nm
            nv_ref[...] = nv

        @pl.when(k == 0)
        def _():
            emit(sl)

        @pl.when(k == 1)
        def _():
            col = lax.broadcasted_iota(jnp.int32, (tr, SLAB), 1)
            before = pltpu.roll(sl, SLAB - SLAB_SHIFT[1], axis=1)
            after = pltpu.roll(sl, N_FORGET - SLAB_SHIFT[1], axis=1)
            wide_f = jnp.concatenate([f_tile, jnp.zeros((tr, SLAB - LANE), F32)], axis=1)
            forget = pltpu.roll(wide_f, FORGET_IN_SHARD, axis=1)
            emit(jnp.where(col < FORGET_IN_SHARD, before, jnp.where(col < FORGET_IN_SHARD + N_FORGET, forget, after)))

        for kk in (2, 3):
            @pl.when(k == kk)
            def _(kk=kk):
                emit(pltpu.roll(sl, SLAB - SLAB_SHIFT[kk], axis=1))

    nat = pl.BlockSpec((SHARD_COLS, tr), lambda i, k: (0, i))
    half = lambda width: pl.BlockSpec((tr, width), lambda i, k: (i % nbh, 0))
    return pl.pallas_call(
        body, grid_spec=pltpu.PrefetchScalarGridSpec(
            num_scalar_prefetch=1, grid=(rows // tr,),
            in_specs=[half(SLAB), half(SLAB), half(LANE), half(LANE), nat, nat, nat],
            out_specs=[nat] * 4),
        out_shape=[SDS((SHARD_COLS, rows), F32)] * 4, name="adamw_w_in",
        compiler_params=_params(("arbitrary",)))(chip_core, slab_mine, slab_theirs, forget_mine, forget_theirs, w, m, v)


ANY = pl.BlockSpec(memory_space=pl.ANY)
HALF_AXIS = (0, 0, 1, 0, 0, 0, 1)


def _me():
    return lax.axis_index("x"), lax.axis_index("y"), lax.axis_index("c")


def _half(ref, which, axis):
    n = ref.shape[axis] // 2
    sl = pl.ds(which * n, n)
    return ref.at[sl] if axis == 0 else ref.at[:, sl]


def _piece(t, ref, j):
    if t == 0:
        return ref.at[:, pl.ds(SLAB_START[j], SLAB)]
    if t == 1:
        return ref
    if t in (2, 6):
        return ref.at[pl.ds(512 * j, 512)]
    return ref.at[:, pl.ds(512 * j, 512)]


def _piece_shape(t, shape):
    if t == 0:
        return (shape[0], SLAB)
    if t == 1:
        return shape
    if t in (2, 6):
        return (512, shape[1])
    return (shape[0], 512)


def _gather_plan(ins, outs, own_slot_in_src):
    x, y, c = _me()
    k = 2 * x + y
    sib = (x, y, 1 - c)
    chips = [(1 - x, y), (x, 1 - y), (1 - x, 1 - y)]
    n = len(outs)

    def rows(t, which):
        h = outs[t].shape[1] // 2
        return pl.ds(which * h, h)

    def mine(t):
        return ins[t].at[k, rows(t, c)] if own_slot_in_src else ins[t].at[rows(t, c)]

    def first(t, j, sems):
        chip = chips[j]
        return pltpu.make_async_remote_copy(
            src_ref=mine(t), dst_ref=outs[t].at[k, rows(t, c)], send_sem=sems[0].at[t, j], recv_sem=sems[1].at[t, j],
            device_id=(chip[0], chip[1], c), device_id_type=MESH)

    def landed(t, j, sems):
        chip = chips[j]
        return pltpu.make_async_remote_copy(
            src_ref=mine(t), dst_ref=outs[t].at[2 * chip[0] + chip[1], rows(t, c)], send_sem=sems[0].at[t, j],
            recv_sem=sems[1].at[t, j], device_id=(chip[0], chip[1], c), device_id_type=MESH)

    def passed(t, j, which, sems):
        chip = chips[j]
        blk = outs[t].at[2 * chip[0] + chip[1], rows(t, which)]
        return pltpu.make_async_remote_copy(
            src_ref=blk, dst_ref=blk, send_sem=sems[2].at[t, j], recv_sem=sems[3].at[t, j], device_id=sib,
            device_id_type=MESH)

    def start(sems):
        for j in range(3):
            for t in range(n):
                first(t, j, sems).start()

    def finish(sems):
        for j in range(3):
            for t in range(n):
                landed(t, j, sems).wait_recv()
                passed(t, j, c, sems).start()
        for j in range(3):
            for t in range(n):
                passed(t, j, 1 - c, sems).wait_recv()
        for j in range(3):
            for t in range(n):
                first(t, j, sems).wait_send()
                passed(t, j, c, sems).wait_send()

    return k, start, finish


def _all_gather_slabs(slabs):
    def body(in_ref, out_ref, nbr_sem, quarter_sem, pass_sem):
        x, y, c = _me()
        k = 2 * x + y
        rows = out_ref.shape[1]
        h, q = rows // 2, rows // 4
        nbrs = [(1 - x, y), (x, 1 - y)]
        slot = lambda chip: 2 * chip[0] + chip[1]
        diag = 2 * (1 - x) + (1 - y)
        half = pl.ds(c * h, h)
        quarter = lambda a: pl.ds(c * h + a * q, q)

        def first(a):
            return pltpu.make_async_remote_copy(
                src_ref=in_ref.at[k, half], dst_ref=out_ref.at[k, half], send_sem=nbr_sem.at[0, a],
                recv_sem=nbr_sem.at[1, a], device_id=(nbrs[a][0], nbrs[a][1], c), device_id_type=MESH)

        def landed(a):
            blk = out_ref.at[slot(nbrs[a]), half]
            return pltpu.make_async_remote_copy(
                src_ref=blk, dst_ref=blk, send_sem=nbr_sem.at[0, a], recv_sem=nbr_sem.at[1, a],
                device_id=(nbrs[a][0], nbrs[a][1], c), device_id_type=MESH)

        def relay(a):
            blk = out_ref.at[slot(nbrs[a]), quarter(a)]
            to = nbrs[1 - a]
            return pltpu.make_async_remote_copy(
                src_ref=blk, dst_ref=blk, send_sem=quarter_sem.at[0, a], recv_sem=quarter_sem.at[1, a],
                device_id=(to[0], to[1], c), device_id_type=MESH)

        def relayed(a):
            blk = out_ref.at[diag, quarter(a)]
            frm = nbrs[1 - a]
            return pltpu.make_async_remote_copy(
                src_ref=blk, dst_ref=blk, send_sem=quarter_sem.at[0, a], recv_sem=quarter_sem.at[1, a],
                device_id=(frm[0], frm[1], c), device_id_type=MESH)

        def passed(j, which):
            sl = diag if j == 2 else slot(nbrs[j])
            blk = out_ref.at[sl, pl.ds(which * h, h)]
            return pltpu.make_async_remote_copy(
                src_ref=blk, dst_ref=blk, send_sem=pass_sem.at[0, j], recv_sem=pass_sem.at[1, j],
                device_id=(x, y, 1 - c), device_id_type=MESH)

        for a in range(2):
            first(a).start()
        for a in range(2):
            landed(a).wait_recv()
            relay(a).start()
            passed(a, c).start()
        for a in range(2):
            relayed(a).wait_recv()
        passed(2, c).start()
        for j in range(3):
            passed(j, 1 - c).wait_recv()
        for a in range(2):
            first(a).wait_send()
            relay(a).wait_send()
        for j in range(3):
            passed(j, c).wait_send()

    return pl.pallas_call(
        body, in_specs=[ANY], out_specs=ANY, out_shape=SDS(slabs.shape, slabs.dtype),
        scratch_shapes=[pltpu.SemaphoreType.DMA((2, 2)), pltpu.SemaphoreType.DMA((2, 2)), pltpu.SemaphoreType.DMA((2, 3))],
        input_output_aliases={0: 0}, name="all_gather_slabs")(slabs)


def _gather_comm(parts):
    n = len(parts)

    def start(ins, outs, sems):
        k, go, _ = _gather_plan(ins, outs, False)
        for t in range(n):
            pltpu.make_async_copy(ins[t], outs[t].at[k], sems[4].at[t]).start()
        go(sems)

    def finish(ins, outs, sems):
        k, _, done = _gather_plan(ins, outs, False)
        done(sems)
        for t in range(n):
            pltpu.make_async_copy(ins[t], outs[t].at[k], sems[4].at[t]).wait()

    return _Comm(parts, [SDS((N_CHIPS,) + p.shape, p.dtype) for p in parts],
                 [pltpu.SemaphoreType.DMA((n, 3))] * 4 + [pltpu.SemaphoreType.DMA((n,))], start, finish)


def _exchange_comm(arrs, kinds, whole=()):
    n = len(arrs)

    def copies(ins, outs, sems):
        x, y, c = _me()
        return [pltpu.make_async_remote_copy(
            src_ref=ins[t] if t in whole else _half(ins[t], 1 - c, HALF_AXIS[kinds[t]]), dst_ref=outs[t],
            send_sem=sems[0].at[t], recv_sem=sems[1].at[t], device_id=(x, y, 1 - c), device_id_type=MESH)
            for t in range(n)]

    def start(ins, outs, sems):
        for cp in copies(ins, outs, sems):
            cp.start()

    def finish(ins, outs, sems):
        for cp in copies(ins, outs, sems):
            cp.wait()

    def hshape(t):
        s = list(arrs[t].shape)
        if t not in whole:
            s[HALF_AXIS[kinds[t]]] //= 2
        return SDS(tuple(s), arrs[t].dtype)

    return _Comm(arrs, [hshape(t) for t in range(n)], [pltpu.SemaphoreType.DMA((n,))] * 2, start, finish)


def _add_half(full, got, core, axis, *, name):
    r, c = got.shape
    br, bc = (256 if r % 256 == 0 else 128), min(2048, c)
    off_r = (r // br) if axis == 0 else 0
    off_c = (c // bc) if axis == 1 else 0

    def body(c_ref, a_ref, b_ref, o_ref):
        o_ref[...] = (a_ref[...] + b_ref[...]).astype(BF16)

    return pl.pallas_call(
        body, grid_spec=pltpu.PrefetchScalarGridSpec(
            num_scalar_prefetch=1, grid=(r // br, c // bc),
            in_specs=[pl.BlockSpec((br, bc), lambda i, j, cr: (i + cr[0] * off_r, j + cr[0] * off_c)),
                      pl.BlockSpec((br, bc), lambda i, j, cr: (i, j))],
            out_specs=pl.BlockSpec((br, bc), lambda i, j, cr: (i, j))),
        out_shape=SDS((r, c), BF16), name=name, compiler_params=_params(("parallel", "parallel")))(core, full, got)


def _add_pair(a, b, *, name):
    r, c = a.shape
    br, bc = 256, min(2048, c)

    def body(a_ref, b_ref, o_ref):
        o_ref[...] = (a_ref[...] + b_ref[...].astype(F32)).astype(BF16)

    spec = pl.BlockSpec((br, bc), lambda i, j: (i, j))
    return pl.pallas_call(body, grid=(r // br, c // bc), in_specs=[spec, spec], out_specs=spec,
                          out_shape=SDS((r, c), BF16), name=name, compiler_params=_params(("parallel", "parallel")))(a, b)


def _scatter_comm(halves, kinds):
    n = len(halves)

    def plan(ins, outs, sems):
        send, recv, lsem = sems
        x, y, c = _me()
        k = 2 * x + y

        def to_chip(t, j):
            return pltpu.make_async_remote_copy(
                src_ref=_piece(kinds[t], ins[t], j), dst_ref=outs[t].at[k], send_sem=send.at[t, j],
                recv_sem=recv.at[t, k], device_id=(j // 2, j % 2, c), device_id_type=MESH)

        def from_chip(t, j):
            return pltpu.make_async_remote_copy(
                src_ref=_piece(kinds[t], ins[t], j), dst_ref=outs[t].at[j], send_sem=send.at[t, j],
                recv_sem=recv.at[t, j], device_id=(j // 2, j % 2, c), device_id_type=MESH)

        def own(t, j):
            return pltpu.make_async_copy(_piece(kinds[t], ins[t], j), outs[t].at[j], lsem.at[t])

        return k, to_chip, from_chip, own

    def start(ins, outs, sems):
        k, to_chip, _, own = plan(ins, outs, sems)
        for j in range(N_CHIPS):
            @pl.when(k != j)
            def _(j=j):
                for t in range(n):
                    to_chip(t, j).start()

            @pl.when(k == j)
            def _(j=j):
                for t in range(n):
                    own(t, j).start()

    def finish(ins, outs, sems):
        k, to_chip, from_chip, own = plan(ins, outs, sems)
        for j in range(N_CHIPS):
            @pl.when(k != j)
            def _(j=j):
                for t in range(n):
                    from_chip(t, j).wait_recv()
                for t in range(n):
                    to_chip(t, j).wait_send()

            @pl.when(k == j)
            def _(j=j):
                for t in range(n):
                    own(t, j).wait()

    return _Comm(halves, [SDS((N_CHIPS,) + _piece_shape(kinds[t], halves[t].shape), halves[t].dtype) for t in range(n)],
                 [pltpu.SemaphoreType.DMA((n, N_CHIPS))] * 2 + [pltpu.SemaphoreType.DMA((n,))], start, finish)


def _sum4(p, *, name):
    _, r, c = p.shape
    br = 256 if r % 256 == 0 else 128

    def body(p_ref, o_ref):
        o_ref[...] = ((p_ref[0].astype(F32) + p_ref[1].astype(F32)) + p_ref[2].astype(F32)) + p_ref[3].astype(F32)

    return pl.pallas_call(
        body, grid=(r // br,), in_specs=[pl.BlockSpec((N_CHIPS, br, c), lambda i: (0, i, 0))],
        out_specs=_rowblk(br, c), out_shape=SDS((r, c), F32), name=name, compiler_params=_params(("parallel",)))(p)


def _swap_comm(sums):
    return _exchange_comm(sums, [None] * len(sums), whole=tuple(range(len(sums))))


def _adamw_halves(mine, theirs, core, w, m, v, *, axis, tr, name):
    rows, cols = w.shape

    if axis == 0:
        nbh = rows // 2 // tr
        g_spec = pl.BlockSpec((tr, cols), lambda i, cr: (i % nbh, 0))
    else:
        g_spec = pl.BlockSpec((tr, cols // 2), lambda i, cr: (i, 0))

    def body(c_ref, a_ref, b_ref, w_ref, m_ref, v_ref, g_ref, d_ref, nm_ref, nv_ref):
        a, b = a_ref[...], b_ref[...]
        if axis == 0:
            g = jnp.where(pl.program_id(0) // nbh == c_ref[0], a, b)
        else:
            low = c_ref[0] == 0
            g = jnp.concatenate([jnp.where(low, a, b), jnp.where(low, b, a)], axis=1)
        g_ref[...] = g
        d, nm, nv = _adamw_math(w_ref[...], g, m_ref[...], v_ref[...])
        d_ref[...] = d
        nm_ref[...] = nm
        nv_ref[...] = nv

    nat = pl.BlockSpec((tr, cols), lambda i, cr: (i, 0))
    return pl.pallas_call(
        body, grid_spec=pltpu.PrefetchScalarGridSpec(
            num_scalar_prefetch=1, grid=(rows // tr,), in_specs=[g_spec, g_spec, nat, nat, nat], out_specs=[nat] * 4),
        out_shape=[SDS((rows, cols), F32)] * 4, name=name, compiler_params=_params(("arbitrary",)))(
            core, mine, theirs, w, m, v)


SMALL_ROWS, SMALL_COLS = 8, 1024


def _pack_small(vs):
    flat = jnp.concatenate([v.reshape(-1) for v in vs])
    return jnp.pad(flat, (0, SMALL_ROWS * SMALL_COLS - flat.shape[0])).reshape(SMALL_ROWS, SMALL_COLS)


def _unpack_small(packed, sizes):
    flat = packed.reshape(-1)
    out, o = [], 0
    for n in sizes:
        out.append(flat[o:o + n].reshape(1, n))
        o += n
    return out


def _all_reduce_small(v):
    n_dev = 8

    def body(v_ref, o_ref, land, send, recv):
        x, y, c = _me()
        me = 4 * x + 2 * y + c
        land[me] = v_ref[...]
        cps = []
        for r in range(1, n_dev):
            fx, fy, fc = (r >> 2) & 1, (r >> 1) & 1, r & 1
            peer = (x ^ fx, y ^ fy, c ^ fc)
            cps.append(pltpu.make_async_remote_copy(
                src_ref=v_ref, dst_ref=land.at[me], send_sem=send.at[r - 1], recv_sem=recv.at[r - 1],
                device_id=peer, device_id_type=MESH))
        for cp in cps:
            cp.start()
        for r in range(1, n_dev):
            fx, fy, fc = (r >> 2) & 1, (r >> 1) & 1, r & 1
            src = 4 * (x ^ fx) + 2 * (y ^ fy) + (c ^ fc)
            pltpu.make_async_remote_copy(
                src_ref=v_ref, dst_ref=land.at[src], send_sem=send.at[r - 1], recv_sem=recv.at[r - 1],
                device_id=(x ^ fx, y ^ fy, c ^ fc), device_id_type=MESH).wait_recv()
        for cp in cps:
            cp.wait_send()
        acc = land[0]
        for r in range(1, n_dev):
            acc = acc + land[r]
        o_ref[...] = acc

    vm = pl.BlockSpec(memory_space=pltpu.VMEM)
    return pl.pallas_call(
        body, in_specs=[vm], out_specs=vm, out_shape=SDS(v.shape, F32),
        scratch_shapes=[pltpu.VMEM((n_dev,) + v.shape, F32), pltpu.SemaphoreType.DMA((n_dev - 1,)),
                        pltpu.SemaphoreType.DMA((n_dev - 1,))],
        name="all_reduce_small")(v)


def kernel(x, mem, norm_gain, mem_norm_gain, w_in, b_forget, q_gain_a, k_gain_a, sinks_a, q_gain_b, k_gain_b, q_gain_c, k_gain_c, w_mem_kv, w_branch_a, w_branch_b, w_branch_c, w_out, loss_target, m_norm_gain, m_mem_norm_gain, m_w_in, m_b_forget, m_q_gain_a, m_k_gain_a, m_sinks_a, m_q_gain_b, m_k_gain_b, m_q_gain_c, m_k_gain_c, m_w_mem_kv, m_w_branch_a, m_w_branch_b, m_w_branch_c, m_w_out, v_norm_gain, v_mem_norm_gain, v_w_in, v_b_forget, v_q_gain_a, v_k_gain_a, v_sinks_a, v_q_gain_b, v_k_gain_b, v_q_gain_c, v_k_gain_c, v_w_mem_kv, v_w_branch_a, v_w_branch_b, v_w_branch_c, v_w_out):
    xi, yi, ci = lax.axis_index("x"), lax.axis_index("y"), lax.axis_index("c")
    chip = jnp.reshape(2 * xi + yi, (1,)).astype(jnp.int32)
    core = jnp.reshape(ci, (1,)).astype(jnp.int32)

    slabs = _pack_w_in(chip, jnp.transpose(w_in[0]))
    mine = [w_mem_kv[0].astype(BF16), w_branch_a[0].astype(BF16), w_branch_b[0].astype(BF16),
            w_branch_c[0].astype(BF16), w_out[0].astype(BF16)]
    w_main, w_fb = _merge_slabs(_all_gather_slabs(slabs))

    r = _local_step(x[0], mem[0], loss_target[0], w_main, w_fb, mine, norm_gain, mem_norm_gain,
                    b_forget, q_gain_a, k_gain_a, sinks_a, q_gain_b, k_gain_b, q_gain_c, k_gain_c, core=core)
    sums, theirs = r["sums"], r["theirs"]

    small_names = ["d_gain", "d_mem_gain", "d_bf", "d_qga", "d_kga", "d_sinks", "d_qgb", "d_kgb", "d_qgc", "d_kgc"]
    loss_part = (0.5 / D_MODEL) * jnp.sum(r["sq"], axis=1, keepdims=True)
    packed = _pack_small([r[n] for n in small_names] + [loss_part])
    red = _all_reduce_small(packed)
    small_w = [norm_gain, mem_norm_gain, b_forget, q_gain_a, k_gain_a, sinks_a, q_gain_b, k_gain_b, q_gain_c, k_gain_c]
    small_m = [m_norm_gain, m_mem_norm_gain, m_b_forget, m_q_gain_a, m_k_gain_a, m_sinks_a, m_q_gain_b, m_k_gain_b,
               m_q_gain_c, m_k_gain_c]
    small_v = [v_norm_gain, v_mem_norm_gain, v_b_forget, v_q_gain_a, v_k_gain_a, v_sinks_a, v_q_gain_b, v_k_gain_b,
               v_q_gain_c, v_k_gain_c]
    sizes = [w.shape[1] for w in small_w]
    s_d, s_m, s_v = _adamw(red, _pack_small(small_w), _pack_small(small_m), _pack_small(small_v), tr=8, name="adamw_small")
    g_small = _unpack_small(red, sizes + [1])
    loss = g_small[-1].reshape(())
    d_small, m_small, v_small = _unpack_small(s_d, sizes), _unpack_small(s_m, sizes), _unpack_small(s_v, sizes)

    gw_in, dw_in, mw_in, vw_in = _adamw_w_in(jnp.concatenate([chip, core]), sums[0], theirs[0], sums[1], theirs[1],
                                             jnp.transpose(w_in[0]), jnp.transpose(m_w_in[0]), jnp.transpose(v_w_in[0]))
    big = {}
    for t, nm, w, m, v in ((2, "w_mem_kv", w_mem_kv, m_w_mem_kv, v_w_mem_kv),
                           (3, "w_branch_a", w_branch_a, m_w_branch_a, v_w_branch_a),
                           (4, "w_branch_b", w_branch_b, m_w_branch_b, v_w_branch_b),
                           (5, "w_branch_c", w_branch_c, m_w_branch_c, v_w_branch_c),
                           (6, "w_out", w_out, m_w_out, v_w_out)):
        big[nm] = _adamw_halves(sums[t], theirs[t], core, w[0], m[0], v[0], axis=HALF_AXIS[t], tr=128,
                                name="adamw_" + nm)

    def collect(kind):
        sm = (g_small, d_small, m_small, v_small)[kind]
        win = (gw_in, dw_in, mw_in, vw_in)[kind]
        return ([sm[0], sm[1], jnp.transpose(win)[None]] + [a for a in sm[2:10]]
                + [big[n][kind][None] for n in ("w_mem_kv", "w_branch_a", "w_branch_b", "w_branch_c", "w_out")])

    return (loss, r["grad_x"][None], *collect(0), *collect(1), *collect(2), *collect(3))
```

```python
import numpy as np
import jax
import jax.numpy as jnp
from jax import lax
from jax.experimental import pallas as pl
from jax.experimental.pallas import tpu as pltpu

F32 = jnp.float32
BF16 = jnp.bfloat16
HI = lax.Precision.HIGHEST
SDS = jax.ShapeDtypeStruct
MESH = pl.DeviceIdType.MESH

D_MODEL = 2048
HEAD_DIM = 64
A_HEADS = 12
A_GROUP = 3
B_HEADS = 12
C_HEADS = 4
C_HEAD_DIM = 128
WINDOW = 128
EPS = 1e-6
NEG = -1e30
LANE = 128

QA, KA, VA, ZA = 0, 768, 1024, 1280
QB, KB, VB, ZB = 2048, 2816, 3584, 4352
QC, ZC = 5120, 5632
GATE = 6144
P_MAIN = 12288
N_FORGET = 12
FORGET_COL = 5120
SHARD_COLS = 3075
SLAB = 3200
SLAB_START = (0, 3072, 6016, 9088)
SLAB_SHIFT = (0, 3, 122, 125)
N_CHIPS = 4

ADAM_LR = 0.001
ADAM_B1 = 0.9
ADAM_B2 = 0.999
ADAM_EPS = 1e-08
ADAM_WD = 0.01
ADAM_STEP = 10

VMEM_LIMIT = 56 * 1024 * 1024
VMEM_WIDE = 62 * 1024 * 1024


def _params(sem, vmem=VMEM_LIMIT):
    return pltpu.CompilerParams(dimension_semantics=sem, vmem_limit_bytes=vmem)


def _win(tr, width, off):
    return pl.BlockSpec((pl.Element(tr), pl.Element(width)), lambda i, *_: (i * tr, off))


def _rowblk(tr, width):
    return pl.BlockSpec((tr, width), lambda i, *_: (i, 0))


def _const(shape):
    nd = len(shape)
    return pl.BlockSpec(shape, lambda *_: (0,) * nd)


def _rms(x, g):
    return x * lax.rsqrt(jnp.mean(x * x, axis=-1, keepdims=True) + EPS) * g


def _head_mean_impl(x2, bd):
    hi = x2.astype(BF16)
    lo = (x2 - hi.astype(F32)).astype(BF16)
    return _dot(hi, bd) + _dot(lo, bd)


@jax.custom_vjp
def _head_mean(x2, bd):
    return _head_mean_impl(x2, bd)


_head_mean.defvjp(lambda x2, bd: (_head_mean_impl(x2, bd), bd),
                  lambda bd, g: (_head_mean_impl(g, bd), jnp.zeros_like(bd)))


def _head_norm(x, g_tiled, bd):
    return x * lax.rsqrt(_head_mean(x * x, bd) + EPS) * g_tiled


def _silu(z):
    return z * jax.nn.sigmoid(z)


def _dot_nt(a, b):
    return lax.dot_general(a, b, (((1,), (1,)), ((), ())), preferred_element_type=F32)


def _dot_tn(a, b):
    return lax.dot_general(a, b, (((0,), (0,)), ((), ())), preferred_element_type=F32)


def _dot(a, b):
    return jnp.dot(a, b, preferred_element_type=F32)


def _swa_fn(qk, vz, qkp, vzp, qg, kg, sinks, bd, bias, first):
    q = _head_norm(qk[:, :768], qg, bd)
    k2 = jnp.concatenate([qkp[:, 768:], qk[:, 768:]], axis=0)
    k2 = _head_norm(k2, kg, bd[:256, :256])
    v2 = jnp.concatenate([vzp[:, :256], vz[:, :256]], axis=0)
    z = vz[:, 256:]
    cols = A_GROUP * WINDOW
    kj = lax.broadcasted_iota(jnp.int32, (2 * WINDOW, cols), 0)
    no_prev = kj < WINDOW * first.astype(jnp.int32)
    qtb = jnp.transpose(q).astype(BF16)
    kb = k2.astype(BF16)
    vtb = jnp.transpose(v2).astype(BF16)
    outs = [None] * A_HEADS
    for g in range(A_HEADS // A_GROUP):
        heads = [A_GROUP * g + u for u in range(A_GROUP)]
        qs = jnp.concatenate([qtb[64 * h:64 * h + 64, :] for h in heads], axis=1)
        s = _dot(kb[:, 64 * g:64 * g + 64], qs) * (HEAD_DIM ** -0.5) + bias[g]
        s = jnp.where(no_prev, NEG, s)
        sink = jnp.concatenate([jnp.broadcast_to(sinks[:, h:h + 1], (1, WINDOW)) for h in heads], axis=1)
        m = lax.stop_gradient(jnp.maximum(jnp.max(s, axis=0, keepdims=True), sink))
        p = jnp.exp(s - m)
        den = jnp.sum(p, axis=0, keepdims=True) + jnp.exp(sink - m)
        o = _dot(vtb[64 * g:64 * g + 64, :], (p * (1.0 / den)).astype(BF16))
        for u, h in enumerate(heads):
            outs[h] = o[:, WINDOW * u:WINDOW * u + WINDOW]
    return jnp.transpose(jnp.concatenate(outs, axis=0)) * _silu(z)


def _swa_bias():
    qi = np.arange(WINDOW)[None, :]
    kj = np.arange(2 * WINDOW)[:, None]
    rel = qi + WINDOW - kj
    valid = (rel >= 0) & (rel < WINDOW)
    out = np.zeros((A_HEADS // A_GROUP, 2 * WINDOW, A_GROUP * WINDOW), np.float32)
    for h in range(A_HEADS):
        slope = np.float32(2.0 ** (-8.0 * (h + 1) / A_HEADS))
        blk = np.where(valid, -slope * rel.astype(np.float32), np.float32(NEG))
        g, u = divmod(h, A_GROUP)
        out[g, :, WINDOW * u:WINDOW * u + WINDOW] = blk
    return jnp.asarray(out)


def _mem_fn(qz, mkv, qg, kg, bd):
    q = _head_norm(qz[:, :512], qg, bd).astype(BF16)
    k = _head_norm(mkv[:, :512], kg, bd).astype(BF16)
    v = mkv[:, 512:].astype(BF16)
    z = qz[:, 512:]
    outs = []
    for h in range(C_HEADS):
        sl = slice(128 * h, 128 * h + 128)
        s = _dot_nt(q[:, sl], k[:, sl]) * (C_HEAD_DIM ** -0.5)
        m = lax.stop_gradient(jnp.max(s, axis=-1, keepdims=True))
        p = jnp.exp(s - m)
        den = jnp.sum(p, axis=-1, keepdims=True)
        outs.append(_dot((p * (1.0 / den)).astype(BF16), v[:, sl]))
    return jnp.concatenate(outs, axis=1) * _silu(z)


def _qn_fn(q, g, bd):
    return _head_norm(q, g, bd) * (HEAD_DIM ** -0.5)


def _kn_fn(k, g, bd):
    return _head_norm(k, g, bd)


def _block_diag(width, hd):
    i = np.arange(width) // hd
    return jnp.asarray((i[:, None] == i[None, :]).astype(np.float32) / hd, BF16)


def _head_sum(width, hd):
    i = np.arange(width) // hd
    return jnp.asarray((i[:, None] == np.arange(LANE)[None, :]).astype(np.float32))


def _rms_fwd(x, g, *, tr, name):
    rows, dm = x.shape

    def body(x_ref, g_ref, o_ref):
        o_ref[...] = _rms(x_ref[...], g_ref[...]).astype(BF16)

    return pl.pallas_call(
        body, grid=(rows // tr,),
        in_specs=[_rowblk(tr, dm), _const((1, dm))],
        out_specs=_rowblk(tr, dm),
        out_shape=SDS((rows, dm), BF16), name=name,
        compiler_params=_params(("parallel",)))(x, g)


def _rms_bwd(x, g, dy, resid, *, tr, name, comm=None):
    rows, dm = x.shape
    want_dx = resid is not None
    n_in = 4 if want_dx else 3
    n_out = 2 if want_dx else 1
    c_in = len(comm.ins) if comm else 0
    c_out = len(comm.out_shapes) if comm else 0
    nb = rows // tr

    def body(*refs):
        x_ref, g_ref, dy_ref = refs[:3]
        r_ref = refs[3] if want_dx else None
        cin = refs[n_in:n_in + c_in]
        outs = refs[n_in + c_in:n_in + c_in + n_out]
        dg_ref = outs[-1]
        cout = refs[n_in + c_in + n_out:n_in + c_in + n_out + c_out]
        csem = refs[n_in + c_in + n_out + c_out:]

        if comm:
            @pl.when(pl.program_id(0) == 0)
            def _():
                comm.start(cin, cout, csem)

        _, vjp = jax.vjp(_rms, x_ref[...], g_ref[...])
        dx, dg = vjp(dy_ref[...])

        @pl.when(pl.program_id(0) == 0)
        def _():
            dg_ref[...] = jnp.zeros_like(dg_ref)

        dg_ref[...] += dg
        if want_dx:
            outs[0][...] = r_ref[...] + dx

        if comm:
            @pl.when(pl.program_id(0) == nb - 1)
            def _():
                comm.finish(cin, cout, csem)

    hbm = pl.BlockSpec(memory_space=pl.ANY)
    ins = [x, g, dy] + ([resid] if want_dx else []) + (list(comm.ins) if comm else [])
    in_specs = ([_rowblk(tr, dm), _const((1, dm)), _rowblk(tr, dm)] + ([_rowblk(tr, dm)] if want_dx else [])
                + [hbm] * c_in)
    out_specs = ([_rowblk(tr, dm)] if want_dx else []) + [_const((1, dm))] + [hbm] * c_out
    out_shape = (([SDS((rows, dm), F32)] if want_dx else []) + [SDS((1, dm), F32)]
                 + (list(comm.out_shapes) if comm else []))
    res = pl.pallas_call(
        body, grid=(nb,), in_specs=in_specs, out_specs=out_specs, out_shape=out_shape,
        scratch_shapes=list(comm.sems) if comm else [], name=name, compiler_params=_params(("arbitrary",)))(*ins)
    return (list(res[:n_out]), list(res[n_out:])) if comm else res


class _Comm:
    def __init__(self, ins, out_shapes, sems, start, finish):
        self.ins, self.out_shapes, self.sems, self.start, self.finish = list(ins), list(out_shapes), list(sems), start, finish


def _matmul(a, b, *, dims, out_dtype, tm, tn, tk, name, add=None, comms=(), vmem=VMEM_LIMIT):
    a_list = list(a) if isinstance(a, (list, tuple)) else [a]
    b_list = list(b) if isinstance(b, (list, tuple)) else [b]
    assert len(a_list) == 1 or dims == "nt"
    assert len(b_list) == 1 or dims == "tn"
    if dims == "tn":
        kdim, m = a_list[0].shape
    else:
        m, kdim = a_list[0].shape[0], sum(p.shape[1] for p in a_list)
    n = b_list[0].shape[0] if dims == "nt" else sum(p.shape[1] for p in b_list)
    tm, tn, tk = min(tm, m), min(tn, n), min(tk, kdim)
    assert m % tm == 0 and n % tn == 0 and kdim % tk == 0, (name, m, n, kdim)
    ni, nj, nk = m // tm, n // tn, kdim // tk
    a_rng, b_rng, pos = [], [], 0
    for p in a_list:
        assert len(a_list) == 1 or p.shape[1] % tk == 0
        a_rng.append((pos, p.shape[1] // tk if len(a_list) > 1 else nk))
        pos += a_rng[-1][1]
    pos = 0
    for p in b_list:
        assert len(b_list) == 1 or p.shape[1] % tn == 0
        b_rng.append((pos, p.shape[1] // tn if len(b_list) > 1 else nj))
        pos += b_rng[-1][1]
    has_add = add is not None
    n_mm_in = len(a_list) + len(b_list) + (1 if has_add else 0)
    c_in = [len(c.ins) for c in comms]
    c_out = [len(c.out_shapes) for c in comms]
    c_sem = [len(c.sems) for c in comms]

    def body(*refs):
        a_refs, b_refs = refs[:len(a_list)], refs[len(a_list):len(a_list) + len(b_list)]
        add_ref = refs[n_mm_in - 1] if has_add else None
        pos = n_mm_in
        cin = []
        for cnt in c_in:
            cin.append(refs[pos:pos + cnt])
            pos += cnt
        o_ref = refs[pos]
        pos += 1
        cout = []
        for cnt in c_out:
            cout.append(refs[pos:pos + cnt])
            pos += cnt
        acc = refs[pos]
        pos += 1
        csem = []
        for cnt in c_sem:
            csem.append(refs[pos:pos + cnt])
            pos += cnt
        i, j, k = pl.program_id(0), pl.program_id(1), pl.program_id(2)

        if comms:
            @pl.when((i == 0) & (j == 0) & (k == 0))
            def _():
                for c, ci, co, cs in zip(comms, cin, cout, csem):
                    c.start(ci, co, cs)

        def accumulate(a_ref, b_ref, first_k, later_k):
            if dims == "nn":
                part = _dot(a_ref[...], b_ref[...])
            elif dims == "nt":
                part = _dot_nt(a_ref[...], b_ref[...])
            else:
                part = _dot_tn(a_ref[...], b_ref[...])

            if first_k:
                @pl.when(k == 0)
                def _():
                    acc[...] = part + add_ref[...] if has_add else part

            if later_k:
                @pl.when(k > 0)
                def _():
                    acc[...] += part

        if len(a_list) > 1:
            for a_ref, (k0, cnt) in zip(a_refs, a_rng):
                @pl.when((k >= k0) & (k < k0 + cnt))
                def _(a_ref=a_ref, k0=k0, cnt=cnt):
                    accumulate(a_ref, b_refs[0], k0 == 0, k0 + cnt > 1)
        elif len(b_list) > 1:
            for b_ref, (j0, cnt) in zip(b_refs, b_rng):
                @pl.when((j >= j0) & (j < j0 + cnt))
                def _(b_ref=b_ref):
                    accumulate(a_refs[0], b_ref, True, nk > 1)
        else:
            accumulate(a_refs[0], b_refs[0], True, nk > 1)

        @pl.when(k == nk - 1)
        def _():
            o_ref[...] = acc[...].astype(out_dtype)

        if comms:
            @pl.when((i == ni - 1) & (j == nj - 1) & (k == nk - 1))
            def _():
                for c, ci, co, cs in zip(comms, cin, cout, csem):
                    c.finish(ci, co, cs)

    def a_spec(k0, cnt):
        if dims == "tn":
            return pl.BlockSpec((tk, tm), lambda i, j, k: (k, i))
        return pl.BlockSpec((tm, tk), lambda i, j, k: (i, jnp.clip(k - k0, 0, cnt - 1)))

    def b_spec(j0, cnt):
        if dims == "nt":
            return pl.BlockSpec((tn, tk), lambda i, j, k: (j, k))
        return pl.BlockSpec((tk, tn), lambda i, j, k: (k, jnp.clip(j - j0, 0, cnt - 1)))

    o_spec = pl.BlockSpec((tm, tn), lambda i, j, k: (i, j))
    hbm = pl.BlockSpec(memory_space=pl.ANY)
    ins = a_list + b_list + ([add] if has_add else []) + [x for c in comms for x in c.ins]
    in_specs = ([a_spec(*r) for r in a_rng] + [b_spec(*r) for r in b_rng] + ([o_spec] if has_add else [])
                + [hbm] * sum(c_in))
    out_specs = [o_spec] + [hbm] * sum(c_out)
    out_shape = [SDS((m, n), out_dtype)] + [s for c in comms for s in c.out_shapes]
    scratch = [pltpu.VMEM((tm, tn), F32)] + [s for c in comms for s in c.sems]
    sem = ("arbitrary",) * 3 if comms else ("parallel", "parallel", "arbitrary")
    res = pl.pallas_call(
        body, grid=(ni, nj, nk), in_specs=in_specs, out_specs=out_specs, out_shape=out_shape, scratch_shapes=scratch,
        name=name, compiler_params=_params(sem, vmem))(*ins)
    if not comms:
        return res[0]
    outs, pos = [], 1
    for cnt in c_out:
        outs.append(list(res[pos:pos + cnt]))
        pos += cnt
    return res[0], outs


def _swa_specs(nb, blk=lambda n: n):
    cur = lambda off: pl.BlockSpec((pl.Element(WINDOW), pl.Element(1024)), lambda n: (blk(n) * WINDOW, off))
    prev = lambda off: pl.BlockSpec((pl.Element(WINDOW), pl.Element(1024)),
                                    lambda n: (jnp.maximum(blk(n) - 1, 0) * WINDOW, off))
    return [cur(QA), cur(VA), prev(QA), prev(VA),
            _const((1, 768)), _const((1, 256)), _const((1, LANE)), _const((768, 768)),
            _const((A_HEADS // A_GROUP, 2 * WINDOW, A_GROUP * WINDOW))]


def _swa_fwd(proj, qg, kg, sinks, bd, bias):
    s = proj.shape[0]
    nb = s // WINDOW

    def body(qk_ref, vz_ref, qkp_ref, vzp_ref, qg_ref, kg_ref, sk_ref, bd_ref, bias_ref, o_ref):
        first = pl.program_id(0) == 0
        o_ref[...] = _swa_fn(qk_ref[...], vz_ref[...], qkp_ref[...], vzp_ref[...], qg_ref[...], kg_ref[...],
                             sk_ref[...], bd_ref[...], bias_ref[...], first).astype(BF16)

    return pl.pallas_call(
        body, grid=(nb,), in_specs=_swa_specs(nb), out_specs=_rowblk(WINDOW, 768),
        out_shape=SDS((s, 768), BF16), name="swa_fwd",
        compiler_params=_params(("parallel",)))(proj, proj, proj, proj, qg, kg, sinks, bd, bias)


def _swa_bwd(proj, qg, kg, sinks, bd, bias, dga):
    s = proj.shape[0]
    nb = s // WINDOW
    blk = lambda n: nb - 1 - n

    def body(qk_ref, vz_ref, qkp_ref, vzp_ref, qg_ref, kg_ref, sk_ref, bd_ref, bias_ref, dg_ref,
             d_ref, dqg_ref, dkg_ref, dsk_ref, carry):
        first = blk(pl.program_id(0)) == 0
        bd_v = bd_ref[...]
        bias_v = bias_ref[...]
        fn = lambda qk, vz, qkp, vzp, qg_, kg_, sk: _swa_fn(qk, vz, qkp, vzp, qg_, kg_, sk, bd_v, bias_v, first)
        _, vjp = jax.vjp(fn, qk_ref[...], vz_ref[...], qkp_ref[...], vzp_ref[...], qg_ref[...], kg_ref[...], sk_ref[...])
        dqk, dvz, dqkp, dvzp, dqg, dkg, dsk = vjp(dg_ref[...])

        @pl.when(pl.program_id(0) == 0)
        def _():
            dqg_ref[...] = jnp.zeros_like(dqg_ref)
            dkg_ref[...] = jnp.zeros_like(dkg_ref)
            dsk_ref[...] = jnp.zeros_like(dsk_ref)
            carry[...] = jnp.zeros_like(carry)

        dqg_ref[...] += dqg
        dkg_ref[...] += dkg
        dsk_ref[...] += dsk
        kv = jnp.concatenate([dqk[:, 768:], dvz[:, :256]], axis=1) + carry[...]
        d_ref[...] = jnp.concatenate([dqk[:, :768], kv, dvz[:, 256:]], axis=1).astype(BF16)
        carry[...] = jnp.concatenate([dqkp[:, 768:], dvzp[:, :256]], axis=1)

    rev = lambda w: pl.BlockSpec((WINDOW, w), lambda n: (blk(n), 0))
    return pl.pallas_call(
        body, grid=(nb,), in_specs=_swa_specs(nb, blk) + [rev(768)],
        out_specs=[rev(2048), _const((1, 768)), _const((1, 256)), _const((1, LANE))],
        out_shape=[SDS((s, 2048), BF16), SDS((1, 768), F32), SDS((1, 256), F32), SDS((1, LANE), F32)],
        scratch_shapes=[pltpu.VMEM((WINDOW, 512), F32)],
        name="swa_bwd", compiler_params=_params(("arbitrary",)))(proj, proj, proj, proj, qg, kg, sinks, bd, bias, dga)


def _mem_fwd(proj, mkv, qg, kg, bd, *, tr):
    s = proj.shape[0]

    def body(qz_ref, mkv_ref, qg_ref, kg_ref, bd_ref, o_ref):
        o_ref[...] = _mem_fn(qz_ref[...], mkv_ref[...], qg_ref[...], kg_ref[...], bd_ref[...]).astype(BF16)

    return pl.pallas_call(
        body, grid=(s // tr,),
        in_specs=[_win(tr, 1024, QC), _const(mkv.shape), _const((1, 512)), _const((1, 512)), _const((512, 512))],
        out_specs=_rowblk(tr, 512), out_shape=SDS((s, 512), BF16), name="mem_fwd",
        compiler_params=_params(("parallel",)))(proj, mkv, qg, kg, bd)


def _mem_bwd(proj, mkv, qg, kg, bd, dgc, *, tr):
    s = proj.shape[0]

    def body(qz_ref, mkv_ref, qg_ref, kg_ref, bd_ref, dg_ref, dqz_ref, dmkv_ref, dqg_ref, dkg_ref):
        bd_v = bd_ref[...]
        fn = lambda qz, mkv_, qg_, kg_: _mem_fn(qz, mkv_, qg_, kg_, bd_v)
        _, vjp = jax.vjp(fn, qz_ref[...], mkv_ref[...], qg_ref[...], kg_ref[...])
        dqz, dmkv, dqg, dkg = vjp(dg_ref[...])

        @pl.when(pl.program_id(0) == 0)
        def _():
            dmkv_ref[...] = jnp.zeros_like(dmkv_ref)
            dqg_ref[...] = jnp.zeros_like(dqg_ref)
            dkg_ref[...] = jnp.zeros_like(dkg_ref)

        dmkv_ref[...] += dmkv
        dqg_ref[...] += dqg
        dkg_ref[...] += dkg
        dqz_ref[...] = dqz.astype(BF16)

    return pl.pallas_call(
        body, grid=(s // tr,),
        in_specs=[_win(tr, 1024, QC), _const(mkv.shape), _const((1, 512)), _const((1, 512)), _const((512, 512)),
                  _rowblk(tr, 512)],
        out_specs=[_rowblk(tr, 1024), _const(mkv.shape), _const((1, 512)), _const((1, 512))],
        out_shape=[SDS((s, 1024), BF16), SDS(mkv.shape, F32), SDS((1, 512), F32), SDS((1, 512), F32)],
        name="mem_bwd", compiler_params=_params(("arbitrary",)))(proj, mkv, qg, kg, bd, dgc)


def _log_sigmoid(x):
    return jnp.minimum(x, 0.0) - jnp.log1p(jnp.exp(-jnp.abs(x)))


FOX_TQ, FOX_TK = 512, 512
FOX_FWD_TQ, FOX_FWD_TK = 512, 1024


def _fox_tiles(s):
    return min(FOX_TQ, s), min(FOX_TK, s)


AUG = 128 * B_HEADS
COL_A, COL_B = 64, 67


def _split3(c):
    hi = c.astype(BF16)
    r1 = c - hi.astype(F32)
    mid = r1.astype(BF16)
    lo = (r1 - mid.astype(F32)).astype(BF16)
    return hi, mid, lo


def _expand_mats():
    def mat(col0):
        e = np.zeros((768 + 3 * LANE, AUG), np.float32)
        for h in range(B_HEADS):
            for d in range(HEAD_DIM):
                e[64 * h + d, 128 * h + d] = 1.0
            for part in range(3):
                e[768 + LANE * part + h, 128 * h + col0 + part] = 1.0
        return e

    def ones(col0):
        o = np.zeros((1, AUG), np.float32)
        for h in range(B_HEADS):
            o[0, 128 * h + col0:128 * h + col0 + 3] = 1.0
        return o

    return (jnp.asarray(mat(COL_A), BF16), jnp.asarray(mat(COL_B), BF16), jnp.asarray(ones(COL_A)), jnp.asarray(ones(COL_B)))


def _augment(data_bf16, triple, emat, ones_row):
    parts = [data_bf16] + (list(triple) if triple is not None else [jnp.zeros((data_bf16.shape[0], LANE), BF16)] * 3)
    wide = _dot(jnp.concatenate(parts, axis=1), emat)
    if ones_row is not None:
        wide = wide + ones_row
    return wide


def _compact(wide):
    return jnp.concatenate([wide[:, 128 * h:128 * h + 64] for h in range(wide.shape[1] // 128)], axis=1)


def _lane_of_heads(wide, col, first=0):
    rows = wide.shape[0]
    lane = lax.broadcasted_iota(jnp.int32, (rows, LANE), 1)
    out = jnp.zeros((rows, LANE), F32)
    for h in range(wide.shape[1] // 128):
        out = jnp.where(lane == first + h, wide[:, 128 * h + col:128 * h + col + 1], out)
    return out


def _fox2_prep(proj, fbl, qg, kg, bfor, bd, ea, eb, ones_a, ones_b, *, tr):
    s = proj.shape[0]
    tri = jnp.asarray(np.tril(np.ones((tr, tr), np.float32)))

    def body(q_ref, k_ref, v_ref, fb_ref, qg_ref, kg_ref, bf_ref, bd_ref, tri_ref, ea_ref, eb_ref, oa_ref, ob_ref,
             qat_ref, ka_ref, kat_ref, va_ref, vat_ref, qn_ref, c_ref, carry):
        @pl.when(pl.program_id(0) == 0)
        def _():
            carry[...] = jnp.zeros_like(carry)

        bd_v = bd_ref[...]
        lane = lax.broadcasted_iota(jnp.int32, (tr, LANE), 1)
        logf = jnp.where(lane < N_FORGET, _log_sigmoid(fb_ref[...] + bf_ref[...]), 0.0)
        c = jnp.dot(tri_ref[...], logf, precision=HI, preferred_element_type=F32) + carry[...]
        c_ref[...] = c
        carry[...] = c[tr - 1:tr, :]
        qn = _qn_fn(q_ref[...], qg_ref[...], bd_v).astype(BF16)
        kn = _kn_fn(k_ref[...], kg_ref[...], bd_v).astype(BF16)
        qn_ref[...] = qn
        qat_ref[...] = jnp.transpose(_augment(qn, _split3(c), ea_ref[...], ob_ref[...])).astype(BF16)
        ka = _augment(kn, _split3(-c), eb_ref[...], oa_ref[...])
        ka_ref[...] = ka.astype(BF16)
        kat_ref[...] = jnp.transpose(ka).astype(BF16)
        va = _augment(v_ref[...].astype(BF16), None, ea_ref[...], oa_ref[...])
        va_ref[...] = va.astype(BF16)
        vat_ref[...] = jnp.transpose(va).astype(BF16)

    emat = _const((768 + 3 * LANE, AUG))
    return pl.pallas_call(
        body, grid=(s // tr,),
        in_specs=[_win(tr, 768, QB), _win(tr, 768, KB), _win(tr, 768, VB), _rowblk(tr, LANE), _const((1, 768)),
                  _const((1, 768)), _const((1, LANE)), _const((768, 768)), _const((tr, tr)), emat, emat,
                  _const((1, AUG)), _const((1, AUG))],
        out_specs=[pl.BlockSpec((AUG, tr), lambda i: (0, i)), _rowblk(tr, AUG), pl.BlockSpec((AUG, tr), lambda i: (0, i)),
                   _rowblk(tr, AUG), pl.BlockSpec((AUG, tr), lambda i: (0, i)), _rowblk(tr, 768), _rowblk(tr, LANE)],
        out_shape=[SDS((AUG, s), BF16), SDS((s, AUG), BF16), SDS((AUG, s), BF16), SDS((s, AUG), BF16),
                   SDS((AUG, s), BF16), SDS((s, 768), BF16), SDS((s, LANE), F32)],
        scratch_shapes=[pltpu.VMEM((1, LANE), F32)], name="fox_prep",
        compiler_params=_params(("arbitrary",)))(proj, proj, proj, fbl, qg, kg, bfor, bd, tri, ea, eb, ones_a, ones_b)


def _fox2_fwd(proj, qat, ka, vat):
    s = proj.shape[0]
    tq, tk = min(FOX_FWD_TQ, s), min(FOX_FWD_TK, s)
    nq, nk = s // tq, s // tk

    def last_k(i):
        return (i * tq + tq - 1) // tk

    def body(qt_ref, k_ref, vt_ref, z_ref, gb_ref, yb_ref, lse_ref, acc, m_s):
        i, j = pl.program_id(0), pl.program_id(1)

        @pl.when(j == 0)
        def _():
            acc[...] = jnp.zeros_like(acc)
            m_s[...] = jnp.full_like(m_s, NEG)

        def tile(masked):
            if masked:
                kpos = j * tk + lax.broadcasted_iota(jnp.int32, (tk, tq), 0)
                qpos = i * tq + lax.broadcasted_iota(jnp.int32, (tk, tq), 1)
                mask = kpos <= qpos
            for h in range(B_HEADS):
                sl = slice(128 * h, 128 * h + 128)
                sc = _dot(k_ref[:, sl], qt_ref[sl, :])
                if masked:
                    sc = jnp.where(mask, sc, NEG)
                m_prev = m_s[h:h + 1, :]
                m_new = jnp.maximum(m_prev, jnp.max(sc, axis=0, keepdims=True))
                p = jnp.exp(sc - m_new).astype(BF16)
                acc[sl, :] = jnp.exp(m_prev - m_new) * acc[sl, :] + _dot(vt_ref[sl, :], p)
                m_s[h:h + 1, :] = m_new

        full = j * tk + tk - 1 <= i * tq

        @pl.when(full)
        def _():
            tile(False)

        @pl.when(jnp.logical_and(jnp.logical_not(full), j <= last_k(i)))
        def _():
            tile(True)

        @pl.when(j == nk - 1)
        def _():
            outs = []
            row = lax.broadcasted_iota(jnp.int32, (LANE, tq), 0)
            lse_t = jnp.zeros((LANE, tq), F32)
            for h in range(B_HEADS):
                l_row = acc[128 * h + COL_A:128 * h + COL_A + 1, :]
                outs.append(acc[128 * h:128 * h + 64, :] * (1.0 / l_row))
                lse_t = jnp.where(row == h, m_s[h:h + 1, :] + jnp.log(l_row), lse_t)
            y = jnp.transpose(jnp.concatenate(outs, axis=0))
            yb_ref[...] = y
            gb_ref[...] = (y * _silu(z_ref[...])).astype(BF16)
            lse_ref[...] = jnp.transpose(lse_t)

    kcol = lambda i, j: (0, jnp.minimum(j, last_k(i)))
    return pl.pallas_call(
        body, grid=(nq, nk),
        in_specs=[pl.BlockSpec((AUG, tq), lambda i, j: (0, i)),
                  pl.BlockSpec((tk, AUG), lambda i, j: (jnp.minimum(j, last_k(i)), 0)),
                  pl.BlockSpec((AUG, tk), kcol),
                  pl.BlockSpec((pl.Element(tq), pl.Element(768)), lambda i, j: (i * tq, ZB))],
        out_specs=[pl.BlockSpec((tq, 768), lambda i, j: (i, 0)), pl.BlockSpec((tq, 768), lambda i, j: (i, 0)),
                   pl.BlockSpec((tq, LANE), lambda i, j: (i, 0))],
        out_shape=[SDS((s, 768), BF16), SDS((s, 768), F32), SDS((s, LANE), F32)],
        scratch_shapes=[pltpu.VMEM((AUG, tq), F32), pltpu.VMEM((16, tq), F32)],
        name="fox_fwd", compiler_params=_params(("parallel", "arbitrary")))(qat, ka, vat, proj)


def _fox2_bwd_pre(proj, yb, dgb, qn, c, lse, hsum, ea, ones_b, *, tr):
    s = proj.shape[0]

    def body(z_ref, y_ref, dg_ref, qn_ref, c_ref, lse_ref, hs_ref, ea_ref, ob_ref,
             qa_ref, qat_ref, dya_ref, dyat_ref, dz_ref):
        z, y, dg = z_ref[...], y_ref[...], dg_ref[...]
        sg = jax.nn.sigmoid(z)
        dy = dg * (z * sg)
        dz_ref[...] = (dg * y * (sg * (1.0 + z * (1.0 - sg)))).astype(BF16)
        delta = jnp.dot(dy * y, hs_ref[...], precision=HI, preferred_element_type=F32)
        e = ea_ref[...]
        dya = _augment(dy.astype(BF16), _split3(-delta), e, None)
        dya_ref[...] = dya.astype(BF16)
        dyat_ref[...] = jnp.transpose(dya).astype(BF16)
        qa = _augment(qn_ref[...], _split3(c_ref[...] - lse_ref[...]), e, ob_ref[...])
        qa_ref[...] = qa.astype(BF16)
        qat_ref[...] = jnp.transpose(qa).astype(BF16)

    return pl.pallas_call(
        body, grid=(s // tr,),
        in_specs=[_win(tr, 768, ZB), _rowblk(tr, 768), _rowblk(tr, 768), _rowblk(tr, 768), _rowblk(tr, LANE),
                  _rowblk(tr, LANE), _const((768, LANE)), _const((768 + 3 * LANE, AUG)), _const((1, AUG))],
        out_specs=[_rowblk(tr, AUG), pl.BlockSpec((AUG, tr), lambda i: (0, i)), _rowblk(tr, AUG),
                   pl.BlockSpec((AUG, tr), lambda i: (0, i)), _rowblk(tr, 768)],
        out_shape=[SDS((s, AUG), BF16), SDS((AUG, s), BF16), SDS((s, AUG), BF16), SDS((AUG, s), BF16),
                   SDS((s, 768), BF16)], name="fox_bwd_pre",
        compiler_params=_params(("parallel",)))(proj, yb, dgb, qn, c, lse, hsum, ea, ones_b)


def _fox2_bwd(qb, qbt, ka, kat, va, dya, dyat):
    s = qb.shape[0]
    tq, tk = _fox_tiles(s)
    nq, nk = s // tq, s // tk
    ng = 2
    gh = B_HEADS // ng
    gw = 128 * gh

    def first_q(j):
        return (j * tk) // tq

    def body(q_ref, qt_ref, k_ref, kt_ref, v_ref, dy_ref, dyt_ref, dq_hbm, dk_ref, dv_ref, dck_ref,
             dq_acc, dk_acc, dv_acc, sem):
        g, j, i = pl.program_id(0), pl.program_id(1), pl.program_id(2)

        @pl.when((j == 0) & (i == 0))
        def _():
            dq_acc[...] = jnp.zeros_like(dq_acc)

        @pl.when(i == 0)
        def _():
            dk_acc[...] = jnp.zeros_like(dk_acc)
            dv_acc[...] = jnp.zeros_like(dv_acc)

        def tile(masked):
            if masked:
                kpos = j * tk + lax.broadcasted_iota(jnp.int32, (tk, tq), 0)
                qpos = i * tq + lax.broadcasted_iota(jnp.int32, (tk, tq), 1)
                mask = kpos <= qpos
            cols = pl.ds(pl.multiple_of(i * tq, tq), tq)
            for h in range(gh):
                sl = slice(128 * h, 128 * h + 128)
                sc = _dot(k_ref[:, sl], qt_ref[sl, :])
                if masked:
                    sc = jnp.where(mask, sc, NEG)
                p = jnp.exp(sc)
                ds = (p * _dot(v_ref[:, sl], dyt_ref[sl, :])).astype(BF16)
                dv_acc[:, sl] += _dot(p.astype(BF16), dy_ref[:, sl])
                dk_acc[:, sl] += _dot(ds, q_ref[:, sl])
                dq_acc[sl, cols] += _dot(kt_ref[sl, :], ds)

        full = j * tk + tk - 1 <= i * tq

        @pl.when(full)
        def _():
            tile(False)

        @pl.when(jnp.logical_and(jnp.logical_not(full), i >= first_q(j)))
        def _():
            tile(True)

        @pl.when(i == nq - 1)
        def _():
            dkw = dk_acc[...]
            dk_ref[...] = _compact(dkw)
            dv_ref[...] = _compact(dv_acc[...]).astype(BF16)
            dck_ref[...] = -_lane_of_heads(dkw, COL_B, gh * g)

        @pl.when((j == nk - 1) & (i == nq - 1))
        def _():
            cp = pltpu.make_async_copy(dq_acc, dq_hbm.at[pl.ds(pl.multiple_of(g * gw, gw), gw)], sem)
            cp.start()
            cp.wait()

    qrow = pl.BlockSpec((tq, gw), lambda g, j, i: (jnp.maximum(i, first_q(j)), g))
    qcol = pl.BlockSpec((gw, tq), lambda g, j, i: (g, jnp.maximum(i, first_q(j))))
    krow = pl.BlockSpec((tk, gw), lambda g, j, i: (j, g))
    kcol = pl.BlockSpec((gw, tk), lambda g, j, i: (g, j))
    kout = pl.BlockSpec((tk, gw // 2), lambda g, j, i: (j, g))
    return pl.pallas_call(
        body, grid=(ng, nk, nq),
        in_specs=[qrow, qcol, krow, kcol, krow, qrow, qcol],
        out_specs=[pl.BlockSpec(memory_space=pl.ANY), kout, kout,
                   pl.BlockSpec((None, tk, LANE), lambda g, j, i: (g, j, 0))],
        out_shape=[SDS((AUG, s), F32), SDS((s, 768), F32), SDS((s, 768), BF16), SDS((ng, s, LANE), F32)],
        scratch_shapes=[pltpu.VMEM((gw, s), F32), pltpu.VMEM((tk, gw), F32), pltpu.VMEM((tk, gw), F32),
                        pltpu.SemaphoreType.DMA],
        name="fox_bwd", compiler_params=_params(("arbitrary",) * 3))(qb, qbt, ka, kat, va, dya, dyat)


def _fox2_bwd_post(proj, fbl, qg, kg, bfor, bd, dqa, dkn, dck, *, tr):
    s = proj.shape[0]
    nb = s // tr
    triu = jnp.asarray(np.triu(np.ones((tr, tr), np.float32)))
    rev = lambda i: nb - 1 - i

    def body(q_ref, k_ref, fb_ref, qg_ref, kg_ref, bf_ref, bd_ref, tri_ref, dqa_ref, dkn_ref, dck_ref,
             dq_ref, dk_ref, dfb_ref, dqg_ref, dkg_ref, dbf_ref, carry):
        @pl.when(pl.program_id(0) == 0)
        def _():
            carry[...] = jnp.zeros_like(carry)
            dqg_ref[...] = jnp.zeros_like(dqg_ref)
            dkg_ref[...] = jnp.zeros_like(dkg_ref)
            dbf_ref[...] = jnp.zeros_like(dbf_ref)

        bd_v = bd_ref[...]
        dqw = jnp.transpose(dqa_ref[...])
        _, vjp_q = jax.vjp(lambda q, g: _qn_fn(q, g, bd_v), q_ref[...], qg_ref[...])
        dq, dqg = vjp_q(_compact(dqw))
        _, vjp_k = jax.vjp(lambda k, g: _kn_fn(k, g, bd_v), k_ref[...], kg_ref[...])
        dk, dkg = vjp_k(dkn_ref[...])
        dq_ref[...] = dq.astype(BF16)
        dk_ref[...] = dk.astype(BF16)
        dqg_ref[...] += dqg
        dkg_ref[...] += dkg

        dc = _lane_of_heads(dqw, COL_A) + (dck_ref[0] + dck_ref[1])
        dlogf = jnp.dot(tri_ref[...], dc, precision=HI, preferred_element_type=F32) + carry[...]
        carry[...] = dlogf[0:1, :]
        lane = lax.broadcasted_iota(jnp.int32, (tr, LANE), 1)
        xf = fb_ref[...] + bf_ref[...]
        dfb = jnp.where(lane < N_FORGET, dlogf * jax.nn.sigmoid(-xf), 0.0)
        dfb_ref[...] = dfb.astype(BF16)
        dbf_ref[...] += jnp.sum(dfb, axis=0, keepdims=True)

    rb = lambda w: pl.BlockSpec((tr, w), lambda i: (rev(i), 0))
    wn = lambda w, off: pl.BlockSpec((pl.Element(tr), pl.Element(w)), lambda i: (rev(i) * tr, off))
    return pl.pallas_call(
        body, grid=(nb,),
        in_specs=[wn(768, QB), wn(768, KB), rb(LANE), _const((1, 768)), _const((1, 768)), _const((1, LANE)),
                  _const((768, 768)), _const((tr, tr)), pl.BlockSpec((AUG, tr), lambda i: (0, rev(i))), rb(768),
                  pl.BlockSpec((2, tr, LANE), lambda i: (0, rev(i), 0))],
        out_specs=[rb(768), rb(768), rb(LANE), _const((1, 768)), _const((1, 768)), _const((1, LANE))],
        out_shape=[SDS((s, 768), BF16), SDS((s, 768), BF16), SDS((s, LANE), BF16), SDS((1, 768), F32),
                   SDS((1, 768), F32), SDS((1, LANE), F32)],
        scratch_shapes=[pltpu.VMEM((1, LANE), F32)], name="fox_bwd_post",
        compiler_params=_params(("arbitrary",)))(proj, proj, fbl, qg, kg, bfor, bd, triu, dqa, dkn, dck)


def _merge_specs(tr):
    row = lambda w: pl.BlockSpec((tr, w), lambda i, j: (i, 0))
    shard = lambda r: pl.BlockSpec((None, r, 512), lambda i, j: (j, 0, 0))
    gate = lambda b: pl.BlockSpec((tr, 512), lambda i, j: (i, (GATE + 2048 * b) // 512 + j))
    return [row(768), row(768), row(512), shard(768), shard(768), shard(512), gate(0), gate(1), gate(2)]


def _merge_fwd(proj, ga, gb, gc, wa, wb, wc, *, tr):
    s = proj.shape[0]

    def body(ga_ref, gb_ref, gc_ref, wa_ref, wb_ref, wc_ref, l0_ref, l1_ref, l2_ref, y_ref):
        ua = _dot(ga_ref[...], wa_ref[...])
        ub = _dot(gb_ref[...], wb_ref[...])
        uc = _dot(gc_ref[...], wc_ref[...])
        y = jax.nn.sigmoid(l0_ref[...]) * ua + jax.nn.sigmoid(l1_ref[...]) * ub + jax.nn.sigmoid(l2_ref[...]) * uc
        y_ref[...] = y.astype(BF16)

    return pl.pallas_call(
        body, grid=(s // tr, N_CHIPS), in_specs=_merge_specs(tr),
        out_specs=pl.BlockSpec((tr, 512), lambda i, j: (i, j)), out_shape=SDS((s, D_MODEL), BF16), name="merge_fwd",
        compiler_params=_params(("parallel", "arbitrary")))(ga, gb, gc, wa, wb, wc, proj, proj, proj)


def _merge_bwd(proj, ga, gb, gc, wa, wb, wc, dy, *, tr):
    s = proj.shape[0]

    def body(ga_ref, gb_ref, gc_ref, wa_ref, wb_ref, wc_ref, l0_ref, l1_ref, l2_ref, dy_ref,
             dl0_ref, dl1_ref, dl2_ref, dua_ref, dub_ref, duc_ref, dga_ref, dgb_ref, dgc_ref):
        j = pl.program_id(1)
        dyv = dy_ref[...]

        @pl.when(j == 0)
        def _():
            dga_ref[...] = jnp.zeros_like(dga_ref)
            dgb_ref[...] = jnp.zeros_like(dgb_ref)
            dgc_ref[...] = jnp.zeros_like(dgc_ref)

        for g_ref, w_ref, l_ref, dl_ref, du_ref, dg_ref in (
                (ga_ref, wa_ref, l0_ref, dl0_ref, dua_ref, dga_ref),
                (gb_ref, wb_ref, l1_ref, dl1_ref, dub_ref, dgb_ref),
                (gc_ref, wc_ref, l2_ref, dl2_ref, duc_ref, dgc_ref)):
            w = w_ref[...]
            u = _dot(g_ref[...], w)
            sg = jax.nn.sigmoid(l_ref[...])
            dl_ref[...] = (dyv * u * sg * (1.0 - sg)).astype(BF16)
            du = (dyv * sg).astype(BF16)
            du_ref[...] = du
            dg_ref[...] += _dot_nt(du, w)

    blk = pl.BlockSpec((tr, 512), lambda i, j: (i, j))
    row = lambda w: pl.BlockSpec((tr, w), lambda i, j: (i, 0))
    big = SDS((s, D_MODEL), BF16)
    return pl.pallas_call(
        body, grid=(s // tr, N_CHIPS), in_specs=_merge_specs(tr) + [blk],
        out_specs=[blk] * 6 + [row(768), row(768), row(512)],
        out_shape=[big] * 6 + [SDS((s, 768), F32), SDS((s, 768), F32), SDS((s, 512), F32)], name="merge_bwd",
        compiler_params=_params(("parallel", "arbitrary")))(ga, gb, gc, wa, wb, wc, proj, proj, proj, dy)


def _out_loss(y, wo, x, tgt, *, tr, tn):
    s = x.shape[0]

    def body(y_ref, w_ref, x_ref, t_ref, d_ref, db_ref, sq_ref, dy_ref):
        @pl.when((pl.program_id(0) == 0) & (pl.program_id(1) == 0))
        def _():
            sq_ref[...] = jnp.zeros_like(sq_ref)

        @pl.when(pl.program_id(1) == 0)
        def _():
            dy_ref[...] = jnp.zeros_like(dy_ref)

        w = w_ref[...]
        out = x_ref[...] + _dot(y_ref[...], w)
        diff = out - t_ref[...]
        sq_ref[...] += jnp.sum(diff * diff, axis=0, keepdims=True)
        d = diff * (1.0 / D_MODEL)
        d_ref[...] = d
        db = d.astype(BF16)
        db_ref[...] = db
        dy_ref[...] += _dot_nt(db, w)

    blk = pl.BlockSpec((tr, tn), lambda i, j: (i, j))
    row = pl.BlockSpec((tr, D_MODEL), lambda i, j: (i, 0))
    return pl.pallas_call(
        body, grid=(s // tr, D_MODEL // tn),
        in_specs=[row, pl.BlockSpec((D_MODEL, tn), lambda i, j: (0, j)), blk, blk],
        out_specs=[blk, blk, _const((1, tn)), row],
        out_shape=[SDS((s, D_MODEL), F32), SDS((s, D_MODEL), BF16), SDS((1, tn), F32), SDS((s, D_MODEL), F32)],
        name="out_loss", compiler_params=_params(("arbitrary", "arbitrary")))(y, wo, x, tgt)


def _tile_gain(g, reps):
    return jnp.tile(g.reshape(1, -1), (1, reps))


def _pad_lane(v):
    v = v.reshape(1, -1)
    return jnp.pad(v, ((0, 0), (0, LANE - v.shape[1])))


def _local_step(x, mem, tgt, w_main, w_fb, w_small, norm_gain, mem_norm_gain, b_forget,
                q_gain_a, k_gain_a, sinks_a, q_gain_b, k_gain_b, q_gain_c, k_gain_c, core=None):
    s = x.shape[0]
    tr = min(512, s)
    bd64 = _block_diag(768, HEAD_DIM)
    bd128 = _block_diag(512, C_HEAD_DIM)
    hsum = _head_sum(768, HEAD_DIM)
    qga, kga = _tile_gain(q_gain_a, 12), _tile_gain(k_gain_a, 4)
    qgb, kgb = _tile_gain(q_gain_b, 12), _tile_gain(k_gain_b, 12)
    qgc, kgc = _tile_gain(q_gain_c, 4), _tile_gain(k_gain_c, 4)
    sinks = _pad_lane(sinks_a)
    bfor = _pad_lane(b_forget)

    hn = _rms_fwd(x, norm_gain, tr=tr, name="rms_x")
    on_mesh = core is not None
    if on_mesh:
        proj, (gathered,) = _matmul(hn, w_main, dims="nn", out_dtype=F32, tm=1024, tn=1024, tk=D_MODEL, name="proj_main",
                                    comms=[_gather_comm(list(w_small))])
        w_mk, wa, wb, wc, wo = gathered
        w_mk, wo = w_mk.reshape(D_MODEL, 1024), wo.reshape(D_MODEL, D_MODEL)
    else:
        proj = _matmul(hn, w_main, dims="nn", out_dtype=F32, tm=1024, tn=1024, tk=D_MODEL, name="proj_main")
        w_mk, wa, wb, wc, wo = w_small
    fbl = _matmul(hn, w_fb, dims="nn", out_dtype=F32, tm=1024, tn=LANE, tk=D_MODEL, name="proj_forget")
    memn = _rms_fwd(mem, mem_norm_gain, tr=mem.shape[0], name="rms_mem")
    mkv = _matmul(memn, w_mk, dims="nn", out_dtype=F32, tm=256, tn=512, tk=D_MODEL, name="mem_kv")

    swa_bias = _swa_bias()
    ga = _swa_fwd(proj, qga, kga, sinks, bd64, swa_bias)
    ea, eb, ones_a, ones_b = _expand_mats()
    tf = min(256, s)
    qat, ka, kat, va, vat, qn, cfox = _fox2_prep(proj, fbl, qgb, kgb, bfor, bd64, ea, eb, ones_a, ones_b, tr=tf)
    gb, yb, lse = _fox2_fwd(proj, qat, ka, vat)
    gc = _mem_fwd(proj, mkv, qgc, kgc, bd128, tr=tr)
    y = _merge_fwd(proj, ga, gb, gc, wa, wb, wc, tr=tr)
    dout, dout_b, sq, dy = _out_loss(y, wo, x, tgt, tr=tr, tn=512)

    d_wo = _matmul(y, dout_b, dims="tn", out_dtype=F32, tm=1024, tn=512, tk=4096, name="dw_out")
    dl0, dl1, dl2, dua, dub, duc, dga, dgb, dgc = _merge_bwd(proj, ga, gb, gc, wa, wb, wc, dy, tr=tr)
    d_wa = _matmul(ga, dua, dims="tn", out_dtype=F32, tm=768, tn=512, tk=4096, name="dw_branch_a")
    d_wb = _matmul(gb, dub, dims="tn", out_dtype=F32, tm=768, tn=512, tk=4096, name="dw_branch_b")
    d_wc = _matmul(gc, duc, dims="tn", out_dtype=F32, tm=512, tn=512, tk=4096, name="dw_branch_c")

    dproj_a, d_qga, d_kga, d_sinks = _swa_bwd(proj, qga, kga, sinks, bd64, swa_bias, dga)

    qab, qabt, dya, dyat, dzb = _fox2_bwd_pre(proj, yb, dgb, qn, cfox, lse, hsum, ea, ones_b, tr=tf)
    dqa, dkn, dvb, dck = _fox2_bwd(qab, qabt, ka, kat, va, dya, dyat)
    dqb, dkb, dfb, d_qgb, d_kgb, d_bf = _fox2_bwd_post(proj, fbl, qgb, kgb, bfor, bd64, dqa, dkn, dck, tr=tf)

    dproj_c, dmkv, d_qgc, d_kgc = _mem_bwd(proj, mkv, qgc, kgc, bd128, dgc, tr=tr)
    dmkv_b = dmkv.astype(BF16)
    d_wmk = _matmul(memn, dmkv_b, dims="tn", out_dtype=F32, tm=1024, tn=512, tk=256, name="dw_mem_kv")
    dmemn = _matmul(dmkv_b, w_mk, dims="nt", out_dtype=F32, tm=256, tn=512, tk=1024, name="dmemn")
    (d_mem_gain,) = _rms_bwd(mem, mem_norm_gain, dmemn, None, tr=mem.shape[0], name="rms_mem_bwd")

    dproj = [dproj_a, jnp.concatenate([dqb, dkb, dvb, dzb, dproj_c], axis=1), dl0, dl1, dl2]
    dhn_f = _matmul(dfb, w_fb, dims="nt", out_dtype=F32, tm=1024, tn=512, tk=LANE, name="dhn_forget")
    d_wfb = _matmul(hn, dfb, dims="tn", out_dtype=F32, tm=1024, tn=LANE, tk=512, name="dw_forget")
    big = {}
    if on_mesh:
        half = D_MODEL // 2
        c0 = core[0]
        hn_other = lax.dynamic_slice(hn, (0, (1 - c0) * half), (s, half))
        hn_own = lax.dynamic_slice(hn, (0, c0 * half), (s, half))
        g1, k1 = [d_wmk, d_wa, d_wb, d_wc, d_wo], [2, 3, 4, 5, 6]
        d_other, (got1,) = _matmul(hn_other, dproj, dims="tn", out_dtype=BF16, tm=1024, tn=512, tk=4096,
                                   name="dw_main_other", comms=[_exchange_comm(g1, k1)])
        h1 = [_add_half(g, got, core, HALF_AXIS[k], name=f"add_half_{k}") for g, got, k in zip(g1, got1, k1)]
        d_own, (got0, parts1) = _matmul(
            hn_own, dproj, dims="tn", out_dtype=F32, tm=1024, tn=512, tk=4096, name="dw_main_own",
            comms=[_exchange_comm([d_other, d_wfb], [0, 1], whole=(0,)), _scatter_comm(h1, k1)])
        h0 = [_add_pair(d_own, got0[0], name="add_pair_main"), _add_half(d_wfb, got0[1], core, 0, name="add_half_1")]
        sums1 = _sum4(parts1, name="sum4_small")
        dhn, (parts0, theirs1) = _matmul(dproj, w_main, dims="nt", out_dtype=F32, tm=1024, tn=512, tk=2048, vmem=VMEM_WIDE, name="dhn",
                                         add=dhn_f, comms=[_scatter_comm(h0, [0, 1]), _swap_comm(sums1)])
        sums0 = _sum4(parts0, name="sum4_main")
        (grad_x, d_gain), theirs0 = _rms_bwd(x, norm_gain, dhn, dout, tr=tr, name="rms_x_bwd", comm=_swap_comm(sums0))
        big = dict(sums=sums0 + sums1, theirs=list(theirs0) + list(theirs1))
    else:
        dhn = _matmul(dproj, w_main, dims="nt", out_dtype=F32, tm=1024, tn=512, tk=2048, vmem=VMEM_WIDE, name="dhn", add=dhn_f)
        d_wmain = _matmul(hn, dproj, dims="tn", out_dtype=F32, tm=1024, tn=512, tk=4096, name="dw_main")
        big = dict(d_wmain=d_wmain, d_wfb=d_wfb, d_wmk=d_wmk, d_wa=d_wa, d_wb=d_wb, d_wc=d_wc, d_wo=d_wo)
        grad_x, d_gain = _rms_bwd(x, norm_gain, dhn, dout, tr=tr, name="rms_x_bwd")

    fold = lambda g, reps: jnp.sum(g.reshape(reps, -1), axis=0, keepdims=True)
    return dict(
        sq=sq, grad_x=grad_x, **big,
        d_gain=d_gain, d_mem_gain=d_mem_gain, d_bf=d_bf[:, :N_FORGET],
        d_qga=fold(d_qga, 12), d_kga=fold(d_kga, 4), d_sinks=d_sinks[:, :A_HEADS],
        d_qgb=fold(d_qgb, 12), d_kgb=fold(d_kgb, 12), d_qgc=fold(d_qgc, 4), d_kgc=fold(d_kgc, 4))


PACK_ROWS = 256
FORGET_IN_SHARD = FORGET_COL - SHARD_COLS
AFTER_FORGET = FORGET_COL - SLAB_START[1]
END_CHIP1 = 2 * SHARD_COLS - N_FORGET - SLAB_START[1]


def _pack_w_in(chip, w):
    rows = w.shape[1]
    tr = PACK_ROWS

    def body(k_ref, w_ref, o_ref, scr):
        scr[...] = jnp.zeros_like(scr)
        scr[pl.ds(0, SHARD_COLS), :] = w_ref[...]
        v = jnp.transpose(scr[...])
        k = k_ref[0]
        col = lax.broadcasted_iota(jnp.int32, (tr, SLAB), 1)
        no_forget = jnp.zeros((tr, LANE), BF16)

        @pl.when(k == 0)
        def _():
            o_ref[:, 0:SLAB] = v.astype(BF16)
            o_ref[:, SLAB:] = no_forget

        @pl.when(k == 1)
        def _():
            before = pltpu.roll(v, SLAB_SHIFT[1], axis=1)
            after = pltpu.roll(v, SLAB - (N_FORGET - SLAB_SHIFT[1]), axis=1)
            slab = jnp.where(col < AFTER_FORGET, before, jnp.where(col < END_CHIP1, after, 0.0))
            o_ref[:, 0:SLAB] = slab.astype(BF16)
            f = pltpu.roll(v, SLAB - FORGET_IN_SHARD, axis=1)[:, :LANE]
            o_ref[:, SLAB:] = jnp.where(col[:, :LANE] < N_FORGET, f, 0.0).astype(BF16)

        for kk in (2, 3):
            @pl.when(k == kk)
            def _(kk=kk):
                o_ref[:, 0:SLAB] = pltpu.roll(v, SLAB_SHIFT[kk], axis=1).astype(BF16)
                o_ref[:, SLAB:] = no_forget

    return pl.pallas_call(
        body, grid_spec=pltpu.PrefetchScalarGridSpec(
            num_scalar_prefetch=1, grid=(rows // tr,),
            in_specs=[pl.BlockSpec((SHARD_COLS, tr), lambda i, k: (0, i))],
            out_specs=pl.BlockSpec((None, tr, SLAB + LANE), lambda i, k: (k[0], i, 0)),
            scratch_shapes=[pltpu.VMEM((SLAB, tr), F32)]),
        out_shape=SDS((N_CHIPS, rows, SLAB + LANE), BF16), name="pack_w_in",
        compiler_params=_params(("arbitrary",)))(chip, w)


def _merge_slabs(g):
    rows = g.shape[1]
    tr = PACK_ROWS
    t = [s // LANE for s in SLAB_START]
    n_t = SLAB // LANE

    def body(g_ref, m_ref, f_ref):
        for k in range(N_CHIPS):
            lo = t[k] + (1 if k > 0 else 0)
            hi = t[k + 1] if k + 1 < N_CHIPS else t[k] + n_t
            m_ref[:, lo * LANE:hi * LANE] = g_ref[k, :, (lo - t[k]) * LANE:(hi - t[k]) * LANE]
            if k + 1 < N_CHIPS:
                a = g_ref[k, :, (hi - t[k]) * LANE:(hi - t[k] + 1) * LANE].astype(F32)
                b = g_ref[k + 1, :, 0:LANE].astype(F32)
                m_ref[:, hi * LANE:(hi + 1) * LANE] = (a + b).astype(BF16)
        f_ref[...] = g_ref[1, :, SLAB:]

    return pl.pallas_call(
        body, grid=(rows // tr,),
        in_specs=[pl.BlockSpec((N_CHIPS, tr, SLAB + LANE), lambda i: (0, i, 0))],
        out_specs=[_rowblk(tr, P_MAIN), _rowblk(tr, LANE)],
        out_shape=[SDS((rows, P_MAIN), BF16), SDS((rows, LANE), BF16)], name="merge_slabs",
        compiler_params=_params(("parallel",)))(g)


def _adamw_math(w, g, m, v):
    nm = ADAM_B1 * m + (1.0 - ADAM_B1) * g
    nv = ADAM_B2 * v + (1.0 - ADAM_B2) * (g * g)
    m_hat = nm / (1.0 - ADAM_B1 ** ADAM_STEP)
    v_hat = nv / (1.0 - ADAM_B2 ** ADAM_STEP)
    delta = -ADAM_LR * (m_hat / (jnp.sqrt(v_hat) + ADAM_EPS) + ADAM_WD * w)
    return delta, nm, nv


def _adamw(g, w, m, v, *, tr, name):
    rows, cols = w.shape
    tr = min(tr, rows)

    def body(g_ref, w_ref, m_ref, v_ref, d_ref, nm_ref, nv_ref):
        d, nm, nv = _adamw_math(w_ref[...], g_ref[...], m_ref[...], v_ref[...])
        d_ref[...] = d
        nm_ref[...] = nm
        nv_ref[...] = nv

    spec = _rowblk(tr, cols)
    return pl.pallas_call(
        body, grid=(rows // tr,), in_specs=[spec] * 4, out_specs=[spec] * 3,
        out_shape=[SDS((rows, cols), F32)] * 3, name=name, compiler_params=_params(("parallel",)))(g, w, m, v)


def _adamw_w_in(chip_core, slab_mine, slab_theirs, forget_mine, forget_theirs, w, m, v):
    rows = w.shape[1]
    tr = PACK_ROWS // 2
    nbh = rows // 2 // tr

    def body(k_ref, sa_ref, sb_ref, fa_ref, fb_ref, w_ref, m_ref, v_ref, g_ref, d_ref, nm_ref, nv_ref):
        use_mine = pl.program_id(0) // nbh == k_ref[1]
        sl = jnp.where(use_mine, sa_ref[...], sb_ref[...])
        f_tile = jnp.where(use_mine, fa_ref[...], fb_ref[...])
        k = k_ref[0]

        def emit(wide):
            g = jnp.transpose(wide)[:SHARD_COLS, :]
            g_ref[...] = g
            d, nm, nv = _adamw_math(w_ref[...], g, m_ref[...], v_ref[...])
            d_ref[...] = d
            nm_ref[...] = nm
            nv_ref[...] = nv

        @pl.when(k == 0)
        def _():
            emit(sl)

        @pl.when(k == 1)
        def _():
            col = lax.broadcasted_iota(jnp.int32, (tr, SLAB), 1)
            before = pltpu.roll(sl, SLAB - SLAB_SHIFT[1], axis=1)
            after = pltpu.roll(sl, N_FORGET - SLAB_SHIFT[1], axis=1)
            wide_f = jnp.concatenate([f_tile, jnp.zeros((tr, SLAB - LANE), F32)], axis=1)
            forget = pltpu.roll(wide_f, FORGET_IN_SHARD, axis=1)
            emit(jnp.where(col < FORGET_IN_SHARD, before, jnp.where(col < FORGET_IN_SHARD + N_FORGET, forget, after)))

        for kk in (2, 3):
            @pl.when(k == kk)
            def _(kk=kk):
                emit(pltpu.roll(sl, SLAB - SLAB_SHIFT[kk], axis=1))

    nat = pl.BlockSpec((SHARD_COLS, tr), lambda i, k: (0, i))
    half = lambda width: pl.BlockSpec((tr, width), lambda i, k: (i % nbh, 0))
    return pl.pallas_call(
        body, grid_spec=pltpu.PrefetchScalarGridSpec(
            num_scalar_prefetch=1, grid=(rows // tr,),
            in_specs=[half(SLAB), half(SLAB), half(LANE), half(LANE), nat, nat, nat],
            out_specs=[nat] * 4),
        out_shape=[SDS((SHARD_COLS, rows), F32)] * 4, name="adamw_w_in",
        compiler_params=_params(("arbitrary",)))(chip_core, slab_mine, slab_theirs, forget_mine, forget_theirs, w, m, v)


ANY = pl.BlockSpec(memory_space=pl.ANY)
HALF_AXIS = (0, 0, 1, 0, 0, 0, 1)


def _me():
    return lax.axis_index("x"), lax.axis_index("y"), lax.axis_index("c")


def _half(ref, which, axis):
    n = ref.shape[axis] // 2
    sl = pl.ds(which * n, n)
    return ref.at[sl] if axis == 0 else ref.at[:, sl]


def _piece(t, ref, j):
    if t == 0:
        return ref.at[:, pl.ds(SLAB_START[j], SLAB)]
    if t == 1:
        return ref
    if t in (2, 6):
        return ref.at[pl.ds(512 * j, 512)]
    return ref.at[:, pl.ds(512 * j, 512)]


def _piece_shape(t, shape):
    if t == 0:
        return (shape[0], SLAB)
    if t == 1:
        return shape
    if t in (2, 6):
        return (512, shape[1])
    return (shape[0], 512)


def _gather_plan(ins, outs, own_slot_in_src):
    x, y, c = _me()
    k = 2 * x + y
    sib = (x, y, 1 - c)
    chips = [(1 - x, y), (x, 1 - y), (1 - x, 1 - y)]
    n = len(outs)

    def rows(t, which):
        h = outs[t].shape[1] // 2
        return pl.ds(which * h, h)

    def mine(t):
        return ins[t].at[k, rows(t, c)] if own_slot_in_src else ins[t].at[rows(t, c)]

    def first(t, j, sems):
        chip = chips[j]
        return pltpu.make_async_remote_copy(
            src_ref=mine(t), dst_ref=outs[t].at[k, rows(t, c)], send_sem=sems[0].at[t, j], recv_sem=sems[1].at[t, j],
            device_id=(chip[0], chip[1], c), device_id_type=MESH)

    def landed(t, j, sems):
        chip = chips[j]
        return pltpu.make_async_remote_copy(
            src_ref=mine(t), dst_ref=outs[t].at[2 * chip[0] + chip[1], rows(t, c)], send_sem=sems[0].at[t, j],
            recv_sem=sems[1].at[t, j], device_id=(chip[0], chip[1], c), device_id_type=MESH)

    def passed(t, j, which, sems):
        chip = chips[j]
        blk = outs[t].at[2 * chip[0] + chip[1], rows(t, which)]
        return pltpu.make_async_remote_copy(
            src_ref=blk, dst_ref=blk, send_sem=sems[2].at[t, j], recv_sem=sems[3].at[t, j], device_id=sib,
            device_id_type=MESH)

    def start(sems):
        for j in range(3):
            for t in range(n):
                first(t, j, sems).start()

    def finish(sems):
        for j in range(3):
            for t in range(n):
                landed(t, j, sems).wait_recv()
                passed(t, j, c, sems).start()
        for j in range(3):
            for t in range(n):
                passed(t, j, 1 - c, sems).wait_recv()
        for j in range(3):
            for t in range(n):
                first(t, j, sems).wait_send()
                passed(t, j, c, sems).wait_send()

    return k, start, finish


def _all_gather_slabs(slabs):
    def body(in_ref, out_ref, nbr_sem, quarter_sem, pass_sem):
        x, y, c = _me()
        k = 2 * x + y
        rows = out_ref.shape[1]
        h, q = rows // 2, rows // 4
        nbrs = [(1 - x, y), (x, 1 - y)]
        slot = lambda chip: 2 * chip[0] + chip[1]
        diag = 2 * (1 - x) + (1 - y)
        half = pl.ds(c * h, h)
        quarter = lambda a: pl.ds(c * h + a * q, q)

        def first(a):
            return pltpu.make_async_remote_copy(
                src_ref=in_ref.at[k, half], dst_ref=out_ref.at[k, half], send_sem=nbr_sem.at[0, a],
                recv_sem=nbr_sem.at[1, a], device_id=(nbrs[a][0], nbrs[a][1], c), device_id_type=MESH)

        def landed(a):
            blk = out_ref.at[slot(nbrs[a]), half]
            return pltpu.make_async_remote_copy(
                src_ref=blk, dst_ref=blk, send_sem=nbr_sem.at[0, a], recv_sem=nbr_sem.at[1, a],
                device_id=(nbrs[a][0], nbrs[a][1], c), device_id_type=MESH)

        def relay(a):
            blk = out_ref.at[slot(nbrs[a]), quarter(a)]
            to = nbrs[1 - a]
            return pltpu.make_async_remote_copy(
                src_ref=blk, dst_ref=blk, send_sem=quarter_sem.at[0, a], recv_sem=quarter_sem.at[1, a],
                device_id=(to[0], to[1], c), device_id_type=MESH)

        def relayed(a):
            blk = out_ref.at[diag, quarter(a)]
            frm = nbrs[1 - a]
            return pltpu.make_async_remote_copy(
                src_ref=blk, dst_ref=blk, send_sem=quarter_sem.at[0, a], recv_sem=quarter_sem.at[1, a],
                device_id=(frm[0], frm[1], c), device_id_type=MESH)

        def passed(j, which):
            sl = diag if j == 2 else slot(nbrs[j])
            blk = out_ref.at[sl, pl.ds(which * h, h)]
            return pltpu.make_async_remote_copy(
                src_ref=blk, dst_ref=blk, send_sem=pass_sem.at[0, j], recv_sem=pass_sem.at[1, j],
                device_id=(x, y, 1 - c), device_id_type=MESH)

        for a in range(2):
            first(a).start()
        for a in range(2):
            landed(a).wait_recv()
            relay(a).start()
            passed(a, c).start()
        for a in range(2):
            relayed(a).wait_recv()
        passed(2, c).start()
        for j in range(3):
            passed(j, 1 - c).wait_recv()
        for a in range(2):
            first(a).wait_send()
            relay(a).wait_send()
        for j in range(3):
            passed(j, c).wait_send()

    return pl.pallas_call(
        body, in_specs=[ANY], out_specs=ANY, out_shape=SDS(slabs.shape, slabs.dtype),
        scratch_shapes=[pltpu.SemaphoreType.DMA((2, 2)), pltpu.SemaphoreType.DMA((2, 2)), pltpu.SemaphoreType.DMA((2, 3))],
        input_output_aliases={0: 0}, name="all_gather_slabs")(slabs)


def _gather_comm(parts):
    n = len(parts)

    def start(ins, outs, sems):
        k, go, _ = _gather_plan(ins, outs, False)
        for t in range(n):
            pltpu.make_async_copy(ins[t], outs[t].at[k], sems[4].at[t]).start()
        go(sems)

    def finish(ins, outs, sems):
        k, _, done = _gather_plan(ins, outs, False)
        done(sems)
        for t in range(n):
            pltpu.make_async_copy(ins[t], outs[t].at[k], sems[4].at[t]).wait()

    return _Comm(parts, [SDS((N_CHIPS,) + p.shape, p.dtype) for p in parts],
                 [pltpu.SemaphoreType.DMA((n, 3))] * 4 + [pltpu.SemaphoreType.DMA((n,))], start, finish)


def _exchange_comm(arrs, kinds, whole=()):
    n = len(arrs)

    def copies(ins, outs, sems):
        x, y, c = _me()
        return [pltpu.make_async_remote_copy(
            src_ref=ins[t] if t in whole else _half(ins[t], 1 - c, HALF_AXIS[kinds[t]]), dst_ref=outs[t],
            send_sem=sems[0].at[t], recv_sem=sems[1].at[t], device_id=(x, y, 1 - c), device_id_type=MESH)
            for t in range(n)]

    def start(ins, outs, sems):
        for cp in copies(ins, outs, sems):
            cp.start()

    def finish(ins, outs, sems):
        for cp in copies(ins, outs, sems):
            cp.wait()

    def hshape(t):
        s = list(arrs[t].shape)
        if t not in whole:
            s[HALF_AXIS[kinds[t]]] //= 2
        return SDS(tuple(s), arrs[t].dtype)

    return _Comm(arrs, [hshape(t) for t in range(n)], [pltpu.SemaphoreType.DMA((n,))] * 2, start, finish)


def _add_half(full, got, core, axis, *, name):
    r, c = got.shape
    br, bc = (256 if r % 256 == 0 else 128), min(2048, c)
    off_r = (r // br) if axis == 0 else 0
    off_c = (c // bc) if axis == 1 else 0

    def body(c_ref, a_ref, b_ref, o_ref):
        o_ref[...] = (a_ref[...] + b_ref[...]).astype(BF16)

    return pl.pallas_call(
        body, grid_spec=pltpu.PrefetchScalarGridSpec(
            num_scalar_prefetch=1, grid=(r // br, c // bc),
            in_specs=[pl.BlockSpec((br, bc), lambda i, j, cr: (i + cr[0] * off_r, j + cr[0] * off_c)),
                      pl.BlockSpec((br, bc), lambda i, j, cr: (i, j))],
            out_specs=pl.BlockSpec((br, bc), lambda i, j, cr: (i, j))),
        out_shape=SDS((r, c), BF16), name=name, compiler_params=_params(("parallel", "parallel")))(core, full, got)


def _add_pair(a, b, *, name):
    r, c = a.shape
    br, bc = 256, min(2048, c)

    def body(a_ref, b_ref, o_ref):
        o_ref[...] = (a_ref[...] + b_ref[...].astype(F32)).astype(BF16)

    spec = pl.BlockSpec((br, bc), lambda i, j: (i, j))
    return pl.pallas_call(body, grid=(r // br, c // bc), in_specs=[spec, spec], out_specs=spec,
                          out_shape=SDS((r, c), BF16), name=name, compiler_params=_params(("parallel", "parallel")))(a, b)


def _scatter_comm(halves, kinds):
    n = len(halves)

    def plan(ins, outs, sems):
        send, recv, lsem = sems
        x, y, c = _me()
        k = 2 * x + y

        def to_chip(t, j):
            return pltpu.make_async_remote_copy(
                src_ref=_piece(kinds[t], ins[t], j), dst_ref=outs[t].at[k], send_sem=send.at[t, j],
                recv_sem=recv.at[t, k], device_id=(j // 2, j % 2, c), device_id_type=MESH)

        def from_chip(t, j):
            return pltpu.make_async_remote_copy(
                src_ref=_piece(kinds[t], ins[t], j), dst_ref=outs[t].at[j], send_sem=send.at[t, j],
                recv_sem=recv.at[t, j], device_id=(j // 2, j % 2, c), device_id_type=MESH)

        def own(t, j):
            return pltpu.make_async_copy(_piece(kinds[t], ins[t], j), outs[t].at[j], lsem.at[t])

        return k, to_chip, from_chip, own

    def start(ins, outs, sems):
        k, to_chip, _, own = plan(ins, outs, sems)
        for j in range(N_CHIPS):
            @pl.when(k != j)
            def _(j=j):
                for t in range(n):
                    to_chip(t, j).start()

            @pl.when(k == j)
            def _(j=j):
                for t in range(n):
                    own(t, j).start()

    def finish(ins, outs, sems):
        k, to_chip, from_chip, own = plan(ins, outs, sems)
        for j in range(N_CHIPS):
            @pl.when(k != j)
            def _(j=j):
                for t in range(n):
                    from_chip(t, j).wait_recv()
                for t in range(n):
                    to_chip(t, j).wait_send()

            @pl.when(k == j)
            def _(j=j):
                for t in range(n):
                    own(t, j).wait()

    return _Comm(halves, [SDS((N_CHIPS,) + _piece_shape(kinds[t], halves[t].shape), halves[t].dtype) for t in range(n)],
                 [pltpu.SemaphoreType.DMA((n, N_CHIPS))] * 2 + [pltpu.SemaphoreType.DMA((n,))], start, finish)


SUM4_STEPS = 4


def _sum4(ps, *, name):
    ns = SUM4_STEPS

    def body(*refs):
        for p_ref, o_ref in zip(refs[:len(ps)], refs[len(ps):]):
            o_ref[...] = ((p_ref[0].astype(F32) + p_ref[1].astype(F32)) + p_ref[2].astype(F32)) + p_ref[3].astype(F32)

    return pl.pallas_call(
        body, grid=(ns,),
        in_specs=[pl.BlockSpec((N_CHIPS, p.shape[1] // ns, p.shape[2]), lambda i: (0, i, 0)) for p in ps],
        out_specs=[_rowblk(p.shape[1] // ns, p.shape[2]) for p in ps],
        out_shape=[SDS(p.shape[1:], F32) for p in ps], name=name, compiler_params=_params(("parallel",)))(*ps)


def _swap_comm(sums):
    return _exchange_comm(sums, [None] * len(sums), whole=tuple(range(len(sums))))


ADAMW_STEPS = 8


def _adamw_halves(items, core):
    ns = ADAMW_STEPS
    nbh = ns // 2
    n = len(items)
    in_specs, out_specs, out_shape, ins = [], [], [], []
    for mine, theirs, w, m, v, axis in items:
        rows, cols = w.shape
        tr = rows // ns
        if axis == 0:
            g_spec = pl.BlockSpec((tr, cols), lambda i, cr: (i % nbh, 0))
        else:
            g_spec = pl.BlockSpec((tr, cols // 2), lambda i, cr: (i, 0))
        nat = pl.BlockSpec((tr, cols), lambda i, cr: (i, 0))
        in_specs += [g_spec, g_spec, nat, nat, nat]
        out_specs += [nat] * 4
        out_shape += [SDS((rows, cols), F32)] * 4
        ins += [mine, theirs, w, m, v]

    def body(c_ref, *refs):
        for t, item in enumerate(items):
            a_ref, b_ref, w_ref, m_ref, v_ref = refs[5 * t:5 * t + 5]
            g_ref, d_ref, nm_ref, nv_ref = refs[5 * n + 4 * t:5 * n + 4 * t + 4]
            a, b = a_ref[...], b_ref[...]
            if item[5] == 0:
                g = jnp.where(pl.program_id(0) // nbh == c_ref[0], a, b)
            else:
                low = c_ref[0] == 0
                g = jnp.concatenate([jnp.where(low, a, b), jnp.where(low, b, a)], axis=1)
            g_ref[...] = g
            d, nm, nv = _adamw_math(w_ref[...], g, m_ref[...], v_ref[...])
            d_ref[...] = d
            nm_ref[...] = nm
            nv_ref[...] = nv

    res = pl.pallas_call(
        body, grid_spec=pltpu.PrefetchScalarGridSpec(
            num_scalar_prefetch=1, grid=(ns,), in_specs=in_specs, out_specs=out_specs),
        out_shape=out_shape, name="adamw_shards", compiler_params=_params(("arbitrary",)))(core, *ins)
    return [tuple(res[4 * t:4 * t + 4]) for t in range(n)]


SMALL_ROWS, SMALL_COLS = 8, 1024


def _pack_small(vs):
    flat = jnp.concatenate([v.reshape(-1) for v in vs])
    return jnp.pad(flat, (0, SMALL_ROWS * SMALL_COLS - flat.shape[0])).reshape(SMALL_ROWS, SMALL_COLS)


def _unpack_small(packed, sizes):
    flat = packed.reshape(-1)
    out, o = [], 0
    for n in sizes:
        out.append(flat[o:o + n].reshape(1, n))
        o += n
    return out


def _all_reduce_small(v):
    n_dev = 8

    def body(v_ref, o_ref, land, send, recv):
        x, y, c = _me()
        me = 4 * x + 2 * y + c
        land[me] = v_ref[...]
        cps = []
        for r in range(1, n_dev):
            fx, fy, fc = (r >> 2) & 1, (r >> 1) & 1, r & 1
            peer = (x ^ fx, y ^ fy, c ^ fc)
            cps.append(pltpu.make_async_remote_copy(
                src_ref=v_ref, dst_ref=land.at[me], send_sem=send.at[r - 1], recv_sem=recv.at[r - 1],
                device_id=peer, device_id_type=MESH))
        for cp in cps:
            cp.start()
        for r in range(1, n_dev):
            fx, fy, fc = (r >> 2) & 1, (r >> 1) & 1, r & 1
            src = 4 * (x ^ fx) + 2 * (y ^ fy) + (c ^ fc)
            pltpu.make_async_remote_copy(
                src_ref=v_ref, dst_ref=land.at[src], send_sem=send.at[r - 1], recv_sem=recv.at[r - 1],
                device_id=(x ^ fx, y ^ fy, c ^ fc), device_id_type=MESH).wait_recv()
        for cp in cps:
            cp.wait_send()
        acc = land[0]
        for r in range(1, n_dev):
            acc = acc + land[r]
        o_ref[...] = acc

    vm = pl.BlockSpec(memory_space=pltpu.VMEM)
    return pl.pallas_call(
        body, in_specs=[vm], out_specs=vm, out_shape=SDS(v.shape, F32),
        scratch_shapes=[pltpu.VMEM((n_dev,) + v.shape, F32), pltpu.SemaphoreType.DMA((n_dev - 1,)),
                        pltpu.SemaphoreType.DMA((n_dev - 1,))],
        name="all_reduce_small")(v)


def kernel(x, mem, norm_gain, mem_norm_gain, w_in, b_forget, q_gain_a, k_gain_a, sinks_a, q_gain_b, k_gain_b, q_gain_c, k_gain_c, w_mem_kv, w_branch_a, w_branch_b, w_branch_c, w_out, loss_target, m_norm_gain, m_mem_norm_gain, m_w_in, m_b_forget, m_q_gain_a, m_k_gain_a, m_sinks_a, m_q_gain_b, m_k_gain_b, m_q_gain_c, m_k_gain_c, m_w_mem_kv, m_w_branch_a, m_w_branch_b, m_w_branch_c, m_w_out, v_norm_gain, v_mem_norm_gain, v_w_in, v_b_forget, v_q_gain_a, v_k_gain_a, v_sinks_a, v_q_gain_b, v_k_gain_b, v_q_gain_c, v_k_gain_c, v_w_mem_kv, v_w_branch_a, v_w_branch_b, v_w_branch_c, v_w_out):
    xi, yi, ci = lax.axis_index("x"), lax.axis_index("y"), lax.axis_index("c")
    chip = jnp.reshape(2 * xi + yi, (1,)).astype(jnp.int32)
    core = jnp.reshape(ci, (1,)).astype(jnp.int32)

    slabs = _pack_w_in(chip, jnp.transpose(w_in[0]))
    mine = [w_mem_kv[0].astype(BF16), w_branch_a[0].astype(BF16), w_branch_b[0].astype(BF16),
            w_branch_c[0].astype(BF16), w_out[0].astype(BF16)]
    w_main, w_fb = _merge_slabs(_all_gather_slabs(slabs))

    r = _local_step(x[0], mem[0], loss_target[0], w_main, w_fb, mine, norm_gain, mem_norm_gain,
                    b_forget, q_gain_a, k_gain_a, sinks_a, q_gain_b, k_gain_b, q_gain_c, k_gain_c, core=core)
    sums, theirs = r["sums"], r["theirs"]

    small_names = ["d_gain", "d_mem_gain", "d_bf", "d_qga", "d_kga", "d_sinks", "d_qgb", "d_kgb", "d_qgc", "d_kgc"]
    loss_part = (0.5 / D_MODEL) * jnp.sum(r["sq"], axis=1, keepdims=True)
    packed = _pack_small([r[n] for n in small_names] + [loss_part])
    red = _all_reduce_small(packed)
    small_w = [norm_gain, mem_norm_gain, b_forget, q_gain_a, k_gain_a, sinks_a, q_gain_b, k_gain_b, q_gain_c, k_gain_c]
    small_m = [m_norm_gain, m_mem_norm_gain, m_b_forget, m_q_gain_a, m_k_gain_a, m_sinks_a, m_q_gain_b, m_k_gain_b,
               m_q_gain_c, m_k_gain_c]
    small_v = [v_norm_gain, v_mem_norm_gain, v_b_forget, v_q_gain_a, v_k_gain_a, v_sinks_a, v_q_gain_b, v_k_gain_b,
               v_q_gain_c, v_k_gain_c]
    sizes = [w.shape[1] for w in small_w]
    s_d, s_m, s_v = _adamw(red, _pack_small(small_w), _pack_small(small_m), _pack_small(small_v), tr=8, name="adamw_small")
    g_small = _unpack_small(red, sizes + [1])
    loss = g_small[-1].reshape(())
    d_small, m_small, v_small = _unpack_small(s_d, sizes), _unpack_small(s_m, sizes), _unpack_small(s_v, sizes)

    gw_in, dw_in, mw_in, vw_in = _adamw_w_in(jnp.concatenate([chip, core]), sums[0], theirs[0], sums[1], theirs[1],
                                             jnp.transpose(w_in[0]), jnp.transpose(m_w_in[0]), jnp.transpose(v_w_in[0]))
    shards = ((2, "w_mem_kv", w_mem_kv, m_w_mem_kv, v_w_mem_kv),
              (3, "w_branch_a", w_branch_a, m_w_branch_a, v_w_branch_a),
              (4, "w_branch_b", w_branch_b, m_w_branch_b, v_w_branch_b),
              (5, "w_branch_c", w_branch_c, m_w_branch_c, v_w_branch_c),
              (6, "w_out", w_out, m_w_out, v_w_out))
    done = _adamw_halves([(sums[t], theirs[t], w[0], m[0], v[0], HALF_AXIS[t]) for t, _, w, m, v in shards], core)
    big = {nm: res for (_, nm, _, _, _), res in zip(shards, done)}

    def collect(kind):
        sm = (g_small, d_small, m_small, v_small)[kind]
        win = (gw_in, dw_in, mw_in, vw_in)[kind]
        return ([sm[0], sm[1], jnp.transpose(win)[None]] + [a for a in sm[2:10]]
                + [big[n][kind][None] for n in ("w_mem_kv", "w_branch_a", "w_branch_b", "w_branch_c", "w_out")])

    return (loss, r["grad_x"][None], *collect(0), *collect(1), *collect(2), *collect(3))
```

```python
import numpy as np
import jax
import jax.numpy as jnp
from jax import lax
from jax.experimental import pallas as pl
from jax.experimental.pallas import tpu as pltpu

F32 = jnp.float32
BF16 = jnp.bfloat16
HI = lax.Precision.HIGHEST
SDS = jax.ShapeDtypeStruct
MESH = pl.DeviceIdType.MESH

D_MODEL = 2048
HEAD_DIM = 64
A_HEADS = 12
A_GROUP = 3
B_HEADS = 12
C_HEADS = 4
C_HEAD_DIM = 128
WINDOW = 128
EPS = 1e-6
NEG = -1e30
LANE = 128

QA, KA, VA, ZA = 0, 768, 1024, 1280
QB, KB, VB, ZB = 2048, 2816, 3584, 4352
QC, ZC = 5120, 5632
GATE = 6144
P_MAIN = 12288
N_FORGET = 12
FORGET_COL = 5120
SHARD_COLS = 3075
SLAB = 3200
SLAB_START = (0, 3072, 6016, 9088)
SLAB_SHIFT = (0, 3, 122, 125)
N_CHIPS = 4

ADAM_LR = 0.001
ADAM_B1 = 0.9
ADAM_B2 = 0.999
ADAM_EPS = 1e-08
ADAM_WD = 0.01
ADAM_STEP = 10

VMEM_LIMIT = 56 * 1024 * 1024
VMEM_WIDE = 62 * 1024 * 1024


def _params(sem, vmem=VMEM_LIMIT):
    return pltpu.CompilerParams(dimension_semantics=sem, vmem_limit_bytes=vmem)


def _win(tr, width, off):
    return pl.BlockSpec((pl.Element(tr), pl.Element(width)), lambda i, *_: (i * tr, off))


def _rowblk(tr, width):
    return pl.BlockSpec((tr, width), lambda i, *_: (i, 0))


def _const(shape):
    nd = len(shape)
    return pl.BlockSpec(shape, lambda *_: (0,) * nd)


def _rms(x, g):
    return x * lax.rsqrt(jnp.mean(x * x, axis=-1, keepdims=True) + EPS) * g


def _head_mean_impl(x2, bd):
    hi = x2.astype(BF16)
    lo = (x2 - hi.astype(F32)).astype(BF16)
    return _dot(hi, bd) + _dot(lo, bd)


@jax.custom_vjp
def _head_mean(x2, bd):
    return _head_mean_impl(x2, bd)


_head_mean.defvjp(lambda x2, bd: (_head_mean_impl(x2, bd), bd),
                  lambda bd, g: (_head_mean_impl(g, bd), jnp.zeros_like(bd)))


def _head_norm(x, g_tiled, bd):
    return x * lax.rsqrt(_head_mean(x * x, bd) + EPS) * g_tiled


def _silu(z):
    return z * jax.nn.sigmoid(z)


def _dot_nt(a, b):
    return lax.dot_general(a, b, (((1,), (1,)), ((), ())), preferred_element_type=F32)


def _dot_tn(a, b):
    return lax.dot_general(a, b, (((0,), (0,)), ((), ())), preferred_element_type=F32)


def _dot(a, b):
    return jnp.dot(a, b, preferred_element_type=F32)


def _swa_fn(qk, vz, qkp, vzp, qg, kg, sinks, bd, bias, first):
    q = _head_norm(qk[:, :768], qg, bd)
    k2 = jnp.concatenate([qkp[:, 768:], qk[:, 768:]], axis=0)
    k2 = _head_norm(k2, kg, bd[:256, :256])
    v2 = jnp.concatenate([vzp[:, :256], vz[:, :256]], axis=0)
    z = vz[:, 256:]
    cols = A_GROUP * WINDOW
    kj = lax.broadcasted_iota(jnp.int32, (2 * WINDOW, cols), 0)
    no_prev = kj < WINDOW * first.astype(jnp.int32)
    qtb = jnp.transpose(q).astype(BF16)
    kb = k2.astype(BF16)
    vtb = jnp.transpose(v2).astype(BF16)
    outs = [None] * A_HEADS
    for g in range(A_HEADS // A_GROUP):
        heads = [A_GROUP * g + u for u in range(A_GROUP)]
        qs = jnp.concatenate([qtb[64 * h:64 * h + 64, :] for h in heads], axis=1)
        s = _dot(kb[:, 64 * g:64 * g + 64], qs) * (HEAD_DIM ** -0.5) + bias[g]
        s = jnp.where(no_prev, NEG, s)
        sink = jnp.concatenate([jnp.broadcast_to(sinks[:, h:h + 1], (1, WINDOW)) for h in heads], axis=1)
        m = lax.stop_gradient(jnp.maximum(jnp.max(s, axis=0, keepdims=True), sink))
        p = jnp.exp(s - m)
        den = jnp.sum(p, axis=0, keepdims=True) + jnp.exp(sink - m)
        o = _dot(vtb[64 * g:64 * g + 64, :], (p * (1.0 / den)).astype(BF16))
        for u, h in enumerate(heads):
            outs[h] = o[:, WINDOW * u:WINDOW * u + WINDOW]
    return jnp.transpose(jnp.concatenate(outs, axis=0)) * _silu(z)


def _swa_bias():
    qi = np.arange(WINDOW)[None, :]
    kj = np.arange(2 * WINDOW)[:, None]
    rel = qi + WINDOW - kj
    valid = (rel >= 0) & (rel < WINDOW)
    out = np.zeros((A_HEADS // A_GROUP, 2 * WINDOW, A_GROUP * WINDOW), np.float32)
    for h in range(A_HEADS):
        slope = np.float32(2.0 ** (-8.0 * (h + 1) / A_HEADS))
        blk = np.where(valid, -slope * rel.astype(np.float32), np.float32(NEG))
        g, u = divmod(h, A_GROUP)
        out[g, :, WINDOW * u:WINDOW * u + WINDOW] = blk
    return jnp.asarray(out)


def _mem_fn(qz, mkv, qg, kg, bd):
    q = _head_norm(qz[:, :512], qg, bd).astype(BF16)
    k = _head_norm(mkv[:, :512], kg, bd).astype(BF16)
    v = mkv[:, 512:].astype(BF16)
    z = qz[:, 512:]
    outs = []
    for h in range(C_HEADS):
        sl = slice(128 * h, 128 * h + 128)
        s = _dot_nt(q[:, sl], k[:, sl]) * (C_HEAD_DIM ** -0.5)
        m = lax.stop_gradient(jnp.max(s, axis=-1, keepdims=True))
        p = jnp.exp(s - m)
        den = jnp.sum(p, axis=-1, keepdims=True)
        outs.append(_dot((p * (1.0 / den)).astype(BF16), v[:, sl]))
    return jnp.concatenate(outs, axis=1) * _silu(z)


def _qn_fn(q, g, bd):
    return _head_norm(q, g, bd) * (HEAD_DIM ** -0.5)


def _kn_fn(k, g, bd):
    return _head_norm(k, g, bd)


def _block_diag(width, hd):
    i = np.arange(width) // hd
    return jnp.asarray((i[:, None] == i[None, :]).astype(np.float32) / hd, BF16)


def _head_sum(width, hd):
    i = np.arange(width) // hd
    return jnp.asarray((i[:, None] == np.arange(LANE)[None, :]).astype(np.float32))


def _rms_fwd(x, g, *, tr, name):
    rows, dm = x.shape

    def body(x_ref, g_ref, o_ref):
        o_ref[...] = _rms(x_ref[...], g_ref[...]).astype(BF16)

    return pl.pallas_call(
        body, grid=(rows // tr,),
        in_specs=[_rowblk(tr, dm), _const((1, dm))],
        out_specs=_rowblk(tr, dm),
        out_shape=SDS((rows, dm), BF16), name=name,
        compiler_params=_params(("parallel",)))(x, g)


def _rms_bwd(x, g, dy, resid, *, tr, name, comm=None):
    rows, dm = x.shape
    want_dx = resid is not None
    n_in = 4 if want_dx else 3
    n_out = 2 if want_dx else 1
    c_in = len(comm.ins) if comm else 0
    c_out = len(comm.out_shapes) if comm else 0
    nb = rows // tr

    def body(*refs):
        x_ref, g_ref, dy_ref = refs[:3]
        r_ref = refs[3] if want_dx else None
        cin = refs[n_in:n_in + c_in]
        outs = refs[n_in + c_in:n_in + c_in + n_out]
        dg_ref = outs[-1]
        cout = refs[n_in + c_in + n_out:n_in + c_in + n_out + c_out]
        csem = refs[n_in + c_in + n_out + c_out:]

        if comm:
            @pl.when(pl.program_id(0) == 0)
            def _():
                comm.start(cin, cout, csem)

        _, vjp = jax.vjp(_rms, x_ref[...], g_ref[...])
        dx, dg = vjp(dy_ref[...])

        @pl.when(pl.program_id(0) == 0)
        def _():
            dg_ref[...] = jnp.zeros_like(dg_ref)

        dg_ref[...] += dg
        if want_dx:
            outs[0][...] = r_ref[...] + dx

        if comm:
            @pl.when(pl.program_id(0) == nb - 1)
            def _():
                comm.finish(cin, cout, csem)

    hbm = pl.BlockSpec(memory_space=pl.ANY)
    ins = [x, g, dy] + ([resid] if want_dx else []) + (list(comm.ins) if comm else [])
    in_specs = ([_rowblk(tr, dm), _const((1, dm)), _rowblk(tr, dm)] + ([_rowblk(tr, dm)] if want_dx else [])
                + [hbm] * c_in)
    out_specs = ([_rowblk(tr, dm)] if want_dx else []) + [_const((1, dm))] + [hbm] * c_out
    out_shape = (([SDS((rows, dm), F32)] if want_dx else []) + [SDS((1, dm), F32)]
                 + (list(comm.out_shapes) if comm else []))
    res = pl.pallas_call(
        body, grid=(nb,), in_specs=in_specs, out_specs=out_specs, out_shape=out_shape,
        scratch_shapes=list(comm.sems) if comm else [], name=name, compiler_params=_params(("arbitrary",)))(*ins)
    return (list(res[:n_out]), list(res[n_out:])) if comm else res


class _Comm:
    def __init__(self, ins, out_shapes, sems, start, finish):
        self.ins, self.out_shapes, self.sems, self.start, self.finish = list(ins), list(out_shapes), list(sems), start, finish


def _matmul(a, b, *, dims, out_dtype, tm, tn, tk, name, add=None, comms=(), vmem=VMEM_LIMIT):
    a_list = list(a) if isinstance(a, (list, tuple)) else [a]
    b_list = list(b) if isinstance(b, (list, tuple)) else [b]
    assert len(a_list) == 1 or dims == "nt"
    assert len(b_list) == 1 or dims == "tn"
    if dims == "tn":
        kdim, m = a_list[0].shape
    else:
        m, kdim = a_list[0].shape[0], sum(p.shape[1] for p in a_list)
    n = b_list[0].shape[0] if dims == "nt" else sum(p.shape[1] for p in b_list)
    tm, tn, tk = min(tm, m), min(tn, n), min(tk, kdim)
    assert m % tm == 0 and n % tn == 0 and kdim % tk == 0, (name, m, n, kdim)
    ni, nj, nk = m // tm, n // tn, kdim // tk
    a_rng, b_rng, pos = [], [], 0
    for p in a_list:
        assert len(a_list) == 1 or p.shape[1] % tk == 0
        a_rng.append((pos, p.shape[1] // tk if len(a_list) > 1 else nk))
        pos += a_rng[-1][1]
    pos = 0
    for p in b_list:
        assert len(b_list) == 1 or p.shape[1] % tn == 0
        b_rng.append((pos, p.shape[1] // tn if len(b_list) > 1 else nj))
        pos += b_rng[-1][1]
    has_add = add is not None
    n_mm_in = len(a_list) + len(b_list) + (1 if has_add else 0)
    c_in = [len(c.ins) for c in comms]
    c_out = [len(c.out_shapes) for c in comms]
    c_sem = [len(c.sems) for c in comms]

    def body(*refs):
        a_refs, b_refs = refs[:len(a_list)], refs[len(a_list):len(a_list) + len(b_list)]
        add_ref = refs[n_mm_in - 1] if has_add else None
        pos = n_mm_in
        cin = []
        for cnt in c_in:
            cin.append(refs[pos:pos + cnt])
            pos += cnt
        o_ref = refs[pos]
        pos += 1
        cout = []
        for cnt in c_out:
            cout.append(refs[pos:pos + cnt])
            pos += cnt
        acc = refs[pos]
        pos += 1
        csem = []
        for cnt in c_sem:
            csem.append(refs[pos:pos + cnt])
            pos += cnt
        i, j, k = pl.program_id(0), pl.program_id(1), pl.program_id(2)

        if comms:
            @pl.when((i == 0) & (j == 0) & (k == 0))
            def _():
                for c, ci, co, cs in zip(comms, cin, cout, csem):
                    c.start(ci, co, cs)

        def accumulate(a_ref, b_ref, first_k, later_k):
            if dims == "nn":
                part = _dot(a_ref[...], b_ref[...])
            elif dims == "nt":
                part = _dot_nt(a_ref[...], b_ref[...])
            else:
                part = _dot_tn(a_ref[...], b_ref[...])

            if first_k:
                @pl.when(k == 0)
                def _():
                    acc[...] = part + add_ref[...] if has_add else part

            if later_k:
                @pl.when(k > 0)
                def _():
                    acc[...] += part

        if len(a_list) > 1:
            for a_ref, (k0, cnt) in zip(a_refs, a_rng):
                @pl.when((k >= k0) & (k < k0 + cnt))
                def _(a_ref=a_ref, k0=k0, cnt=cnt):
                    accumulate(a_ref, b_refs[0], k0 == 0, k0 + cnt > 1)
        elif len(b_list) > 1:
            for b_ref, (j0, cnt) in zip(b_refs, b_rng):
                @pl.when((j >= j0) & (j < j0 + cnt))
                def _(b_ref=b_ref):
                    accumulate(a_refs[0], b_ref, True, nk > 1)
        else:
            accumulate(a_refs[0], b_refs[0], True, nk > 1)

        @pl.when(k == nk - 1)
        def _():
            o_ref[...] = acc[...].astype(out_dtype)

        if comms:
            @pl.when((i == ni - 1) & (j == nj - 1) & (k == nk - 1))
            def _():
                for c, ci, co, cs in zip(comms, cin, cout, csem):
                    c.finish(ci, co, cs)

    def a_spec(k0, cnt):
        if dims == "tn":
            return pl.BlockSpec((tk, tm), lambda i, j, k: (k, i))
        return pl.BlockSpec((tm, tk), lambda i, j, k: (i, jnp.clip(k - k0, 0, cnt - 1)))

    def b_spec(j0, cnt):
        if dims == "nt":
            return pl.BlockSpec((tn, tk), lambda i, j, k: (j, k))
        return pl.BlockSpec((tk, tn), lambda i, j, k: (k, jnp.clip(j - j0, 0, cnt - 1)))

    o_spec = pl.BlockSpec((tm, tn), lambda i, j, k: (i, j))
    hbm = pl.BlockSpec(memory_space=pl.ANY)
    ins = a_list + b_list + ([add] if has_add else []) + [x for c in comms for x in c.ins]
    in_specs = ([a_spec(*r) for r in a_rng] + [b_spec(*r) for r in b_rng] + ([o_spec] if has_add else [])
                + [hbm] * sum(c_in))
    out_specs = [o_spec] + [hbm] * sum(c_out)
    out_shape = [SDS((m, n), out_dtype)] + [s for c in comms for s in c.out_shapes]
    scratch = [pltpu.VMEM((tm, tn), F32)] + [s for c in comms for s in c.sems]
    sem = ("arbitrary",) * 3 if comms else ("parallel", "parallel", "arbitrary")
    res = pl.pallas_call(
        body, grid=(ni, nj, nk), in_specs=in_specs, out_specs=out_specs, out_shape=out_shape, scratch_shapes=scratch,
        name=name, compiler_params=_params(sem, vmem))(*ins)
    if not comms:
        return res[0]
    outs, pos = [], 1
    for cnt in c_out:
        outs.append(list(res[pos:pos + cnt]))
        pos += cnt
    return res[0], outs


def _swa_specs(nb, blk=lambda n: n):
    cur = lambda off: pl.BlockSpec((pl.Element(WINDOW), pl.Element(1024)), lambda n: (blk(n) * WINDOW, off))
    prev = lambda off: pl.BlockSpec((pl.Element(WINDOW), pl.Element(1024)),
                                    lambda n: (jnp.maximum(blk(n) - 1, 0) * WINDOW, off))
    return [cur(QA), cur(VA), prev(QA), prev(VA),
            _const((1, 768)), _const((1, 256)), _const((1, LANE)), _const((768, 768)),
            _const((A_HEADS // A_GROUP, 2 * WINDOW, A_GROUP * WINDOW))]


def _swa_fwd(proj, qg, kg, sinks, bd, bias):
    s = proj.shape[0]
    nb = s // WINDOW

    def body(qk_ref, vz_ref, qkp_ref, vzp_ref, qg_ref, kg_ref, sk_ref, bd_ref, bias_ref, o_ref):
        first = pl.program_id(0) == 0
        o_ref[...] = _swa_fn(qk_ref[...], vz_ref[...], qkp_ref[...], vzp_ref[...], qg_ref[...], kg_ref[...],
                             sk_ref[...], bd_ref[...], bias_ref[...], first).astype(BF16)

    return pl.pallas_call(
        body, grid=(nb,), in_specs=_swa_specs(nb), out_specs=_rowblk(WINDOW, 768),
        out_shape=SDS((s, 768), BF16), name="swa_fwd",
        compiler_params=_params(("parallel",)))(proj, proj, proj, proj, qg, kg, sinks, bd, bias)


def _swa_bwd(proj, qg, kg, sinks, bd, bias, dga):
    s = proj.shape[0]
    nb = s // WINDOW
    blk = lambda n: nb - 1 - n

    def body(qk_ref, vz_ref, qkp_ref, vzp_ref, qg_ref, kg_ref, sk_ref, bd_ref, bias_ref, dg_ref,
             d_ref, dqg_ref, dkg_ref, dsk_ref, carry):
        first = blk(pl.program_id(0)) == 0
        bd_v = bd_ref[...]
        bias_v = bias_ref[...]
        fn = lambda qk, vz, qkp, vzp, qg_, kg_, sk: _swa_fn(qk, vz, qkp, vzp, qg_, kg_, sk, bd_v, bias_v, first)
        _, vjp = jax.vjp(fn, qk_ref[...], vz_ref[...], qkp_ref[...], vzp_ref[...], qg_ref[...], kg_ref[...], sk_ref[...])
        dqk, dvz, dqkp, dvzp, dqg, dkg, dsk = vjp(dg_ref[...])

        @pl.when(pl.program_id(0) == 0)
        def _():
            dqg_ref[...] = jnp.zeros_like(dqg_ref)
            dkg_ref[...] = jnp.zeros_like(dkg_ref)
            dsk_ref[...] = jnp.zeros_like(dsk_ref)
            carry[...] = jnp.zeros_like(carry)

        dqg_ref[...] += dqg
        dkg_ref[...] += dkg
        dsk_ref[...] += dsk
        kv = jnp.concatenate([dqk[:, 768:], dvz[:, :256]], axis=1) + carry[...]
        d_ref[...] = jnp.concatenate([dqk[:, :768], kv, dvz[:, 256:]], axis=1).astype(BF16)
        carry[...] = jnp.concatenate([dqkp[:, 768:], dvzp[:, :256]], axis=1)

    rev = lambda w: pl.BlockSpec((WINDOW, w), lambda n: (blk(n), 0))
    return pl.pallas_call(
        body, grid=(nb,), in_specs=_swa_specs(nb, blk) + [rev(768)],
        out_specs=[rev(2048), _const((1, 768)), _const((1, 256)), _const((1, LANE))],
        out_shape=[SDS((s, 2048), BF16), SDS((1, 768), F32), SDS((1, 256), F32), SDS((1, LANE), F32)],
        scratch_shapes=[pltpu.VMEM((WINDOW, 512), F32)],
        name="swa_bwd", compiler_params=_params(("arbitrary",)))(proj, proj, proj, proj, qg, kg, sinks, bd, bias, dga)


def _mem_fwd(proj, mkv, qg, kg, bd, *, tr):
    s = proj.shape[0]

    def body(qz_ref, mkv_ref, qg_ref, kg_ref, bd_ref, o_ref):
        o_ref[...] = _mem_fn(qz_ref[...], mkv_ref[...], qg_ref[...], kg_ref[...], bd_ref[...]).astype(BF16)

    return pl.pallas_call(
        body, grid=(s // tr,),
        in_specs=[_win(tr, 1024, QC), _const(mkv.shape), _const((1, 512)), _const((1, 512)), _const((512, 512))],
        out_specs=_rowblk(tr, 512), out_shape=SDS((s, 512), BF16), name="mem_fwd",
        compiler_params=_params(("parallel",)))(proj, mkv, qg, kg, bd)


def _mem_bwd(proj, mkv, qg, kg, bd, dgc, *, tr):
    s = proj.shape[0]

    def body(qz_ref, mkv_ref, qg_ref, kg_ref, bd_ref, dg_ref, dqz_ref, dmkv_ref, dqg_ref, dkg_ref):
        bd_v = bd_ref[...]
        fn = lambda qz, mkv_, qg_, kg_: _mem_fn(qz, mkv_, qg_, kg_, bd_v)
        _, vjp = jax.vjp(fn, qz_ref[...], mkv_ref[...], qg_ref[...], kg_ref[...])
        dqz, dmkv, dqg, dkg = vjp(dg_ref[...])

        @pl.when(pl.program_id(0) == 0)
        def _():
            dmkv_ref[...] = jnp.zeros_like(dmkv_ref)
            dqg_ref[...] = jnp.zeros_like(dqg_ref)
            dkg_ref[...] = jnp.zeros_like(dkg_ref)

        dmkv_ref[...] += dmkv
        dqg_ref[...] += dqg
        dkg_ref[...] += dkg
        dqz_ref[...] = dqz.astype(BF16)

    return pl.pallas_call(
        body, grid=(s // tr,),
        in_specs=[_win(tr, 1024, QC), _const(mkv.shape), _const((1, 512)), _const((1, 512)), _const((512, 512)),
                  _rowblk(tr, 512)],
        out_specs=[_rowblk(tr, 1024), _const(mkv.shape), _const((1, 512)), _const((1, 512))],
        out_shape=[SDS((s, 1024), BF16), SDS(mkv.shape, F32), SDS((1, 512), F32), SDS((1, 512), F32)],
        name="mem_bwd", compiler_params=_params(("arbitrary",)))(proj, mkv, qg, kg, bd, dgc)


def _log_sigmoid(x):
    return jnp.minimum(x, 0.0) - jnp.log1p(jnp.exp(-jnp.abs(x)))


FOX_TQ, FOX_TK = 512, 512
FOX_FWD_TQ, FOX_FWD_TK = 512, 1024


def _fox_tiles(s):
    return min(FOX_TQ, s), min(FOX_TK, s)


AUG = 128 * B_HEADS
COL_A, COL_B = 64, 67


def _split3(c):
    hi = c.astype(BF16)
    r1 = c - hi.astype(F32)
    mid = r1.astype(BF16)
    lo = (r1 - mid.astype(F32)).astype(BF16)
    return hi, mid, lo


def _expand_mats():
    def mat(col0):
        e = np.zeros((768 + 3 * LANE, AUG), np.float32)
        for h in range(B_HEADS):
            for d in range(HEAD_DIM):
                e[64 * h + d, 128 * h + d] = 1.0
            for part in range(3):
                e[768 + LANE * part + h, 128 * h + col0 + part] = 1.0
        return e

    def ones(col0):
        o = np.zeros((1, AUG), np.float32)
        for h in range(B_HEADS):
            o[0, 128 * h + col0:128 * h + col0 + 3] = 1.0
        return o

    return (jnp.asarray(mat(COL_A), BF16), jnp.asarray(mat(COL_B), BF16), jnp.asarray(ones(COL_A)), jnp.asarray(ones(COL_B)))


def _augment(data_bf16, triple, emat, ones_row):
    parts = [data_bf16] + (list(triple) if triple is not None else [jnp.zeros((data_bf16.shape[0], LANE), BF16)] * 3)
    wide = _dot(jnp.concatenate(parts, axis=1), emat)
    if ones_row is not None:
        wide = wide + ones_row
    return wide


def _compact(wide):
    return jnp.concatenate([wide[:, 128 * h:128 * h + 64] for h in range(wide.shape[1] // 128)], axis=1)


def _lane_of_heads(wide, col, first=0):
    rows = wide.shape[0]
    lane = lax.broadcasted_iota(jnp.int32, (rows, LANE), 1)
    out = jnp.zeros((rows, LANE), F32)
    for h in range(wide.shape[1] // 128):
        out = jnp.where(lane == first + h, wide[:, 128 * h + col:128 * h + col + 1], out)
    return out


def _fox2_prep(proj, fbl, qg, kg, bfor, bd, ea, eb, ones_a, ones_b, *, tr):
    s = proj.shape[0]
    tri = jnp.asarray(np.tril(np.ones((tr, tr), np.float32)))

    def body(q_ref, k_ref, v_ref, fb_ref, qg_ref, kg_ref, bf_ref, bd_ref, tri_ref, ea_ref, eb_ref, oa_ref, ob_ref,
             qat_ref, ka_ref, kat_ref, va_ref, vat_ref, qn_ref, c_ref, carry):
        @pl.when(pl.program_id(0) == 0)
        def _():
            carry[...] = jnp.zeros_like(carry)

        bd_v = bd_ref[...]
        lane = lax.broadcasted_iota(jnp.int32, (tr, LANE), 1)
        logf = jnp.where(lane < N_FORGET, _log_sigmoid(fb_ref[...] + bf_ref[...]), 0.0)
        c = jnp.dot(tri_ref[...], logf, precision=HI, preferred_element_type=F32) + carry[...]
        c_ref[...] = c
        carry[...] = c[tr - 1:tr, :]
        qn = _qn_fn(q_ref[...], qg_ref[...], bd_v).astype(BF16)
        kn = _kn_fn(k_ref[...], kg_ref[...], bd_v).astype(BF16)
        qn_ref[...] = qn
        qat_ref[...] = jnp.transpose(_augment(qn, _split3(c), ea_ref[...], ob_ref[...])).astype(BF16)
        ka = _augment(kn, _split3(-c), eb_ref[...], oa_ref[...])
        ka_ref[...] = ka.astype(BF16)
        kat_ref[...] = jnp.transpose(ka).astype(BF16)
        va = _augment(v_ref[...].astype(BF16), None, ea_ref[...], oa_ref[...])
        va_ref[...] = va.astype(BF16)
        vat_ref[...] = jnp.transpose(va).astype(BF16)

    emat = _const((768 + 3 * LANE, AUG))
    return pl.pallas_call(
        body, grid=(s // tr,),
        in_specs=[_win(tr, 768, QB), _win(tr, 768, KB), _win(tr, 768, VB), _rowblk(tr, LANE), _const((1, 768)),
                  _const((1, 768)), _const((1, LANE)), _const((768, 768)), _const((tr, tr)), emat, emat,
                  _const((1, AUG)), _const((1, AUG))],
        out_specs=[pl.BlockSpec((AUG, tr), lambda i: (0, i)), _rowblk(tr, AUG), pl.BlockSpec((AUG, tr), lambda i: (0, i)),
                   _rowblk(tr, AUG), pl.BlockSpec((AUG, tr), lambda i: (0, i)), _rowblk(tr, 768), _rowblk(tr, LANE)],
        out_shape=[SDS((AUG, s), BF16), SDS((s, AUG), BF16), SDS((AUG, s), BF16), SDS((s, AUG), BF16),
                   SDS((AUG, s), BF16), SDS((s, 768), BF16), SDS((s, LANE), F32)],
        scratch_shapes=[pltpu.VMEM((1, LANE), F32)], name="fox_prep",
        compiler_params=_params(("arbitrary",)))(proj, proj, proj, fbl, qg, kg, bfor, bd, tri, ea, eb, ones_a, ones_b)


def _fox2_fwd(proj, qat, ka, vat):
    s = proj.shape[0]
    tq, tk = min(FOX_FWD_TQ, s), min(FOX_FWD_TK, s)
    nq, nk = s // tq, s // tk

    def last_k(i):
        return (i * tq + tq - 1) // tk

    def body(qt_ref, k_ref, vt_ref, z_ref, gb_ref, yb_ref, lse_ref, acc, m_s):
        i, j = pl.program_id(0), pl.program_id(1)

        @pl.when(j == 0)
        def _():
            acc[...] = jnp.zeros_like(acc)
            m_s[...] = jnp.full_like(m_s, NEG)

        def tile(masked):
            if masked:
                kpos = j * tk + lax.broadcasted_iota(jnp.int32, (tk, tq), 0)
                qpos = i * tq + lax.broadcasted_iota(jnp.int32, (tk, tq), 1)
                mask = kpos <= qpos
            for h in range(B_HEADS):
                sl = slice(128 * h, 128 * h + 128)
                sc = _dot(k_ref[:, sl], qt_ref[sl, :])
                if masked:
                    sc = jnp.where(mask, sc, NEG)
                m_prev = m_s[h:h + 1, :]
                m_new = jnp.maximum(m_prev, jnp.max(sc, axis=0, keepdims=True))
                p = jnp.exp(sc - m_new).astype(BF16)
                acc[sl, :] = jnp.exp(m_prev - m_new) * acc[sl, :] + _dot(vt_ref[sl, :], p)
                m_s[h:h + 1, :] = m_new

        full = j * tk + tk - 1 <= i * tq

        @pl.when(full)
        def _():
            tile(False)

        @pl.when(jnp.logical_and(jnp.logical_not(full), j <= last_k(i)))
        def _():
            tile(True)

        @pl.when(j == nk - 1)
        def _():
            outs = []
            row = lax.broadcasted_iota(jnp.int32, (LANE, tq), 0)
            lse_t = jnp.zeros((LANE, tq), F32)
            for h in range(B_HEADS):
                l_row = acc[128 * h + COL_A:128 * h + COL_A + 1, :]
                outs.append(acc[128 * h:128 * h + 64, :] * (1.0 / l_row))
                lse_t = jnp.where(row == h, m_s[h:h + 1, :] + jnp.log(l_row), lse_t)
            y = jnp.transpose(jnp.concatenate(outs, axis=0))
            yb_ref[...] = y
            gb_ref[...] = (y * _silu(z_ref[...])).astype(BF16)
            lse_ref[...] = jnp.transpose(lse_t)

    kcol = lambda i, j: (0, jnp.minimum(j, last_k(i)))
    return pl.pallas_call(
        body, grid=(nq, nk),
        in_specs=[pl.BlockSpec((AUG, tq), lambda i, j: (0, i)),
                  pl.BlockSpec((tk, AUG), lambda i, j: (jnp.minimum(j, last_k(i)), 0)),
                  pl.BlockSpec((AUG, tk), kcol),
                  pl.BlockSpec((pl.Element(tq), pl.Element(768)), lambda i, j: (i * tq, ZB))],
        out_specs=[pl.BlockSpec((tq, 768), lambda i, j: (i, 0)), pl.BlockSpec((tq, 768), lambda i, j: (i, 0)),
                   pl.BlockSpec((tq, LANE), lambda i, j: (i, 0))],
        out_shape=[SDS((s, 768), BF16), SDS((s, 768), F32), SDS((s, LANE), F32)],
        scratch_shapes=[pltpu.VMEM((AUG, tq), F32), pltpu.VMEM((16, tq), F32)],
        name="fox_fwd", compiler_params=_params(("parallel", "arbitrary")))(qat, ka, vat, proj)


def _fox2_bwd_pre(proj, yb, dgb, qn, c, lse, hsum, ea, ones_b, *, tr):
    s = proj.shape[0]

    def body(z_ref, y_ref, dg_ref, qn_ref, c_ref, lse_ref, hs_ref, ea_ref, ob_ref,
             qa_ref, qat_ref, dya_ref, dyat_ref, dz_ref):
        z, y, dg = z_ref[...], y_ref[...], dg_ref[...]
        sg = jax.nn.sigmoid(z)
        dy = dg * (z * sg)
        dz_ref[...] = (dg * y * (sg * (1.0 + z * (1.0 - sg)))).astype(BF16)
        delta = jnp.dot(dy * y, hs_ref[...], precision=HI, preferred_element_type=F32)
        e = ea_ref[...]
        dya = _augment(dy.astype(BF16), _split3(-delta), e, None)
        dya_ref[...] = dya.astype(BF16)
        dyat_ref[...] = jnp.transpose(dya).astype(BF16)
        qa = _augment(qn_ref[...], _split3(c_ref[...] - lse_ref[...]), e, ob_ref[...])
        qa_ref[...] = qa.astype(BF16)
        qat_ref[...] = jnp.transpose(qa).astype(BF16)

    return pl.pallas_call(
        body, grid=(s // tr,),
        in_specs=[_win(tr, 768, ZB), _rowblk(tr, 768), _rowblk(tr, 768), _rowblk(tr, 768), _rowblk(tr, LANE),
                  _rowblk(tr, LANE), _const((768, LANE)), _const((768 + 3 * LANE, AUG)), _const((1, AUG))],
        out_specs=[_rowblk(tr, AUG), pl.BlockSpec((AUG, tr), lambda i: (0, i)), _rowblk(tr, AUG),
                   pl.BlockSpec((AUG, tr), lambda i: (0, i)), _rowblk(tr, 768)],
        out_shape=[SDS((s, AUG), BF16), SDS((AUG, s), BF16), SDS((s, AUG), BF16), SDS((AUG, s), BF16),
                   SDS((s, 768), BF16)], name="fox_bwd_pre",
        compiler_params=_params(("parallel",)))(proj, yb, dgb, qn, c, lse, hsum, ea, ones_b)


def _fox2_bwd(qb, qbt, ka, kat, va, dya, dyat):
    s = qb.shape[0]
    tq, tk = _fox_tiles(s)
    nq, nk = s // tq, s // tk
    ng = 2
    gh = B_HEADS // ng
    gw = 128 * gh

    def first_q(j):
        return (j * tk) // tq

    def body(q_ref, qt_ref, k_ref, kt_ref, v_ref, dy_ref, dyt_ref, dq_hbm, dk_ref, dv_ref, dck_ref,
             dq_acc, dk_acc, dv_acc, sem):
        g, j, i = pl.program_id(0), pl.program_id(1), pl.program_id(2)

        @pl.when((j == 0) & (i == 0))
        def _():
            dq_acc[...] = jnp.zeros_like(dq_acc)

        @pl.when(i == 0)
        def _():
            dk_acc[...] = jnp.zeros_like(dk_acc)
            dv_acc[...] = jnp.zeros_like(dv_acc)

        def tile(masked):
            if masked:
                kpos = j * tk + lax.broadcasted_iota(jnp.int32, (tk, tq), 0)
                qpos = i * tq + lax.broadcasted_iota(jnp.int32, (tk, tq), 1)
                mask = kpos <= qpos
            cols = pl.ds(pl.multiple_of(i * tq, tq), tq)
            for h in range(gh):
                sl = slice(128 * h, 128 * h + 128)
                sc = _dot(k_ref[:, sl], qt_ref[sl, :])
                if masked:
                    sc = jnp.where(mask, sc, NEG)
                p = jnp.exp(sc)
                ds = (p * _dot(v_ref[:, sl], dyt_ref[sl, :])).astype(BF16)
                dv_acc[:, sl] += _dot(p.astype(BF16), dy_ref[:, sl])
                dk_acc[:, sl] += _dot(ds, q_ref[:, sl])
                dq_acc[sl, cols] += _dot(kt_ref[sl, :], ds)

        full = j * tk + tk - 1 <= i * tq

        @pl.when(full)
        def _():
            tile(False)

        @pl.when(jnp.logical_and(jnp.logical_not(full), i >= first_q(j)))
        def _():
            tile(True)

        @pl.when(i == nq - 1)
        def _():
            dkw = dk_acc[...]
            dk_ref[...] = _compact(dkw)
            dv_ref[...] = _compact(dv_acc[...]).astype(BF16)
            dck_ref[...] = -_lane_of_heads(dkw, COL_B, gh * g)

        @pl.when((j == nk - 1) & (i == nq - 1))
        def _():
            cp = pltpu.make_async_copy(dq_acc, dq_hbm.at[pl.ds(pl.multiple_of(g * gw, gw), gw)], sem)
            cp.start()
            cp.wait()

    qrow = pl.BlockSpec((tq, gw), lambda g, j, i: (jnp.maximum(i, first_q(j)), g))
    qcol = pl.BlockSpec((gw, tq), lambda g, j, i: (g, jnp.maximum(i, first_q(j))))
    krow = pl.BlockSpec((tk, gw), lambda g, j, i: (j, g))
    kcol = pl.BlockSpec((gw, tk), lambda g, j, i: (g, j))
    kout = pl.BlockSpec((tk, gw // 2), lambda g, j, i: (j, g))
    return pl.pallas_call(
        body, grid=(ng, nk, nq),
        in_specs=[qrow, qcol, krow, kcol, krow, qrow, qcol],
        out_specs=[pl.BlockSpec(memory_space=pl.ANY), kout, kout,
                   pl.BlockSpec((None, tk, LANE), lambda g, j, i: (g, j, 0))],
        out_shape=[SDS((AUG, s), F32), SDS((s, 768), F32), SDS((s, 768), BF16), SDS((ng, s, LANE), F32)],
        scratch_shapes=[pltpu.VMEM((gw, s), F32), pltpu.VMEM((tk, gw), F32), pltpu.VMEM((tk, gw), F32),
                        pltpu.SemaphoreType.DMA],
        name="fox_bwd", compiler_params=_params(("arbitrary",) * 3))(qb, qbt, ka, kat, va, dya, dyat)


def _fox2_bwd_post(proj, fbl, qg, kg, bfor, bd, dqa, dkn, dck, *, tr):
    s = proj.shape[0]
    nb = s // tr
    triu = jnp.asarray(np.triu(np.ones((tr, tr), np.float32)))
    rev = lambda i: nb - 1 - i

    def body(q_ref, k_ref, fb_ref, qg_ref, kg_ref, bf_ref, bd_ref, tri_ref, dqa_ref, dkn_ref, dck_ref,
             dq_ref, dk_ref, dfb_ref, dqg_ref, dkg_ref, dbf_ref, carry):
        @pl.when(pl.program_id(0) == 0)
        def _():
            carry[...] = jnp.zeros_like(carry)
            dqg_ref[...] = jnp.zeros_like(dqg_ref)
            dkg_ref[...] = jnp.zeros_like(dkg_ref)
            dbf_ref[...] = jnp.zeros_like(dbf_ref)

        bd_v = bd_ref[...]
        dqw = jnp.transpose(dqa_ref[...])
        _, vjp_q = jax.vjp(lambda q, g: _qn_fn(q, g, bd_v), q_ref[...], qg_ref[...])
        dq, dqg = vjp_q(_compact(dqw))
        _, vjp_k = jax.vjp(lambda k, g: _kn_fn(k, g, bd_v), k_ref[...], kg_ref[...])
        dk, dkg = vjp_k(dkn_ref[...])
        dq_ref[...] = dq.astype(BF16)
        dk_ref[...] = dk.astype(BF16)
        dqg_ref[...] += dqg
        dkg_ref[...] += dkg

        dc = _lane_of_heads(dqw, COL_A) + (dck_ref[0] + dck_ref[1])
        dlogf = jnp.dot(tri_ref[...], dc, precision=HI, preferred_element_type=F32) + carry[...]
        carry[...] = dlogf[0:1, :]
        lane = lax.broadcasted_iota(jnp.int32, (tr, LANE), 1)
        xf = fb_ref[...] + bf_ref[...]
        dfb = jnp.where(lane < N_FORGET, dlogf * jax.nn.sigmoid(-xf), 0.0)
        dfb_ref[...] = dfb.astype(BF16)
        dbf_ref[...] += jnp.sum(dfb, axis=0, keepdims=True)

    rb = lambda w: pl.BlockSpec((tr, w), lambda i: (rev(i), 0))
    wn = lambda w, off: pl.BlockSpec((pl.Element(tr), pl.Element(w)), lambda i: (rev(i) * tr, off))
    return pl.pallas_call(
        body, grid=(nb,),
        in_specs=[wn(768, QB), wn(768, KB), rb(LANE), _const((1, 768)), _const((1, 768)), _const((1, LANE)),
                  _const((768, 768)), _const((tr, tr)), pl.BlockSpec((AUG, tr), lambda i: (0, rev(i))), rb(768),
                  pl.BlockSpec((2, tr, LANE), lambda i: (0, rev(i), 0))],
        out_specs=[rb(768), rb(768), rb(LANE), _const((1, 768)), _const((1, 768)), _const((1, LANE))],
        out_shape=[SDS((s, 768), BF16), SDS((s, 768), BF16), SDS((s, LANE), BF16), SDS((1, 768), F32),
                   SDS((1, 768), F32), SDS((1, LANE), F32)],
        scratch_shapes=[pltpu.VMEM((1, LANE), F32)], name="fox_bwd_post",
        compiler_params=_params(("arbitrary",)))(proj, proj, fbl, qg, kg, bfor, bd, triu, dqa, dkn, dck)


def _merge_specs(tr):
    row = lambda w: pl.BlockSpec((tr, w), lambda i, j: (i, 0))
    shard = lambda r: pl.BlockSpec((None, r, 512), lambda i, j: (j, 0, 0))
    gate = lambda b: pl.BlockSpec((tr, 512), lambda i, j: (i, (GATE + 2048 * b) // 512 + j))
    return [row(768), row(768), row(512), shard(768), shard(768), shard(512), gate(0), gate(1), gate(2)]


def _merge_fwd(proj, ga, gb, gc, wa, wb, wc, *, tr):
    s = proj.shape[0]

    def body(ga_ref, gb_ref, gc_ref, wa_ref, wb_ref, wc_ref, l0_ref, l1_ref, l2_ref, y_ref):
        ua = _dot(ga_ref[...], wa_ref[...])
        ub = _dot(gb_ref[...], wb_ref[...])
        uc = _dot(gc_ref[...], wc_ref[...])
        y = jax.nn.sigmoid(l0_ref[...]) * ua + jax.nn.sigmoid(l1_ref[...]) * ub + jax.nn.sigmoid(l2_ref[...]) * uc
        y_ref[...] = y.astype(BF16)

    return pl.pallas_call(
        body, grid=(s // tr, N_CHIPS), in_specs=_merge_specs(tr),
        out_specs=pl.BlockSpec((tr, 512), lambda i, j: (i, j)), out_shape=SDS((s, D_MODEL), BF16), name="merge_fwd",
        compiler_params=_params(("parallel", "arbitrary")))(ga, gb, gc, wa, wb, wc, proj, proj, proj)


def _merge_bwd(proj, ga, gb, gc, wa, wb, wc, dy, *, tr):
    s = proj.shape[0]

    def body(ga_ref, gb_ref, gc_ref, wa_ref, wb_ref, wc_ref, l0_ref, l1_ref, l2_ref, dy_ref,
             dl0_ref, dl1_ref, dl2_ref, dua_ref, dub_ref, duc_ref, dga_ref, dgb_ref, dgc_ref):
        j = pl.program_id(1)
        dyv = dy_ref[...]

        @pl.when(j == 0)
        def _():
            dga_ref[...] = jnp.zeros_like(dga_ref)
            dgb_ref[...] = jnp.zeros_like(dgb_ref)
            dgc_ref[...] = jnp.zeros_like(dgc_ref)

        for g_ref, w_ref, l_ref, dl_ref, du_ref, dg_ref in (
                (ga_ref, wa_ref, l0_ref, dl0_ref, dua_ref, dga_ref),
                (gb_ref, wb_ref, l1_ref, dl1_ref, dub_ref, dgb_ref),
                (gc_ref, wc_ref, l2_ref, dl2_ref, duc_ref, dgc_ref)):
            w = w_ref[...]
            u = _dot(g_ref[...], w)
            sg = jax.nn.sigmoid(l_ref[...])
            dl_ref[...] = (dyv * u * sg * (1.0 - sg)).astype(BF16)
            du = (dyv * sg).astype(BF16)
            du_ref[...] = du
            dg_ref[...] += _dot_nt(du, w)

    blk = pl.BlockSpec((tr, 512), lambda i, j: (i, j))
    row = lambda w: pl.BlockSpec((tr, w), lambda i, j: (i, 0))
    big = SDS((s, D_MODEL), BF16)
    return pl.pallas_call(
        body, grid=(s // tr, N_CHIPS), in_specs=_merge_specs(tr) + [blk],
        out_specs=[blk] * 6 + [row(768), row(768), row(512)],
        out_shape=[big] * 6 + [SDS((s, 768), F32), SDS((s, 768), F32), SDS((s, 512), F32)], name="merge_bwd",
        compiler_params=_params(("parallel", "arbitrary")))(ga, gb, gc, wa, wb, wc, proj, proj, proj, dy)


def _out_loss(y, wo, x, tgt, *, tr, tn):
    s = x.shape[0]

    def body(y_ref, w_ref, x_ref, t_ref, d_ref, db_ref, sq_ref, dy_ref):
        @pl.when((pl.program_id(0) == 0) & (pl.program_id(1) == 0))
        def _():
            sq_ref[...] = jnp.zeros_like(sq_ref)

        @pl.when(pl.program_id(1) == 0)
        def _():
            dy_ref[...] = jnp.zeros_like(dy_ref)

        w = w_ref[...]
        out = x_ref[...] + _dot(y_ref[...], w)
        diff = out - t_ref[...]
        sq_ref[...] += jnp.sum(diff * diff, axis=0, keepdims=True)
        d = diff * (1.0 / D_MODEL)
        d_ref[...] = d
        db = d.astype(BF16)
        db_ref[...] = db
        dy_ref[...] += _dot_nt(db, w)

    blk = pl.BlockSpec((tr, tn), lambda i, j: (i, j))
    row = pl.BlockSpec((tr, D_MODEL), lambda i, j: (i, 0))
    return pl.pallas_call(
        body, grid=(s // tr, D_MODEL // tn),
        in_specs=[row, pl.BlockSpec((D_MODEL, tn), lambda i, j: (0, j)), blk, blk],
        out_specs=[blk, blk, _const((1, tn)), row],
        out_shape=[SDS((s, D_MODEL), F32), SDS((s, D_MODEL), BF16), SDS((1, tn), F32), SDS((s, D_MODEL), F32)],
        name="out_loss", compiler_params=_params(("arbitrary", "arbitrary")))(y, wo, x, tgt)


def _tile_gain(g, reps):
    return jnp.tile(g.reshape(1, -1), (1, reps))


def _pad_lane(v):
    v = v.reshape(1, -1)
    return jnp.pad(v, ((0, 0), (0, LANE - v.shape[1])))


def _local_step(x, mem, tgt, w_main, w_fb, w_small, norm_gain, mem_norm_gain, b_forget,
                q_gain_a, k_gain_a, sinks_a, q_gain_b, k_gain_b, q_gain_c, k_gain_c, core=None):
    s = x.shape[0]
    tr = min(512, s)
    bd64 = _block_diag(768, HEAD_DIM)
    bd128 = _block_diag(512, C_HEAD_DIM)
    hsum = _head_sum(768, HEAD_DIM)
    qga, kga = _tile_gain(q_gain_a, 12), _tile_gain(k_gain_a, 4)
    qgb, kgb = _tile_gain(q_gain_b, 12), _tile_gain(k_gain_b, 12)
    qgc, kgc = _tile_gain(q_gain_c, 4), _tile_gain(k_gain_c, 4)
    sinks = _pad_lane(sinks_a)
    bfor = _pad_lane(b_forget)

    hn = _rms_fwd(x, norm_gain, tr=tr, name="rms_x")
    on_mesh = core is not None
    if on_mesh:
        proj, (gathered,) = _matmul(hn, w_main, dims="nn", out_dtype=F32, tm=1024, tn=1024, tk=D_MODEL, name="proj_main",
                                    comms=[_gather_comm(list(w_small))])
        w_mk, wa, wb, wc, wo = gathered
        w_mk, wo = w_mk.reshape(D_MODEL, 1024), wo.reshape(D_MODEL, D_MODEL)
    else:
        proj = _matmul(hn, w_main, dims="nn", out_dtype=F32, tm=1024, tn=1024, tk=D_MODEL, name="proj_main")
        w_mk, wa, wb, wc, wo = w_small
    fbl = _matmul(hn, w_fb, dims="nn", out_dtype=F32, tm=1024, tn=LANE, tk=D_MODEL, name="proj_forget")
    memn = _rms_fwd(mem, mem_norm_gain, tr=mem.shape[0], name="rms_mem")
    mkv = _matmul(memn, w_mk, dims="nn", out_dtype=F32, tm=256, tn=512, tk=D_MODEL, name="mem_kv")

    swa_bias = _swa_bias()
    ga = _swa_fwd(proj, qga, kga, sinks, bd64, swa_bias)
    ea, eb, ones_a, ones_b = _expand_mats()
    tf = min(256, s)
    qat, ka, kat, va, vat, qn, cfox = _fox2_prep(proj, fbl, qgb, kgb, bfor, bd64, ea, eb, ones_a, ones_b, tr=tf)
    gb, yb, lse = _fox2_fwd(proj, qat, ka, vat)
    gc = _mem_fwd(proj, mkv, qgc, kgc, bd128, tr=tr)
    y = _merge_fwd(proj, ga, gb, gc, wa, wb, wc, tr=tr)
    dout, dout_b, sq, dy = _out_loss(y, wo, x, tgt, tr=tr, tn=512)

    d_wo = _matmul(y, dout_b, dims="tn", out_dtype=F32, tm=1024, tn=512, tk=4096, name="dw_out")
    dl0, dl1, dl2, dua, dub, duc, dga, dgb, dgc = _merge_bwd(proj, ga, gb, gc, wa, wb, wc, dy, tr=tr)
    d_wa = _matmul(ga, dua, dims="tn", out_dtype=F32, tm=768, tn=512, tk=4096, name="dw_branch_a")
    d_wb = _matmul(gb, dub, dims="tn", out_dtype=F32, tm=768, tn=512, tk=4096, name="dw_branch_b")
    d_wc = _matmul(gc, duc, dims="tn", out_dtype=F32, tm=512, tn=512, tk=4096, name="dw_branch_c")

    dproj_a, d_qga, d_kga, d_sinks = _swa_bwd(proj, qga, kga, sinks, bd64, swa_bias, dga)

    qab, qabt, dya, dyat, dzb = _fox2_bwd_pre(proj, yb, dgb, qn, cfox, lse, hsum, ea, ones_b, tr=tf)
    dqa, dkn, dvb, dck = _fox2_bwd(qab, qabt, ka, kat, va, dya, dyat)
    dqb, dkb, dfb, d_qgb, d_kgb, d_bf = _fox2_bwd_post(proj, fbl, qgb, kgb, bfor, bd64, dqa, dkn, dck, tr=tf)

    dproj_c, dmkv, d_qgc, d_kgc = _mem_bwd(proj, mkv, qgc, kgc, bd128, dgc, tr=tr)
    dmkv_b = dmkv.astype(BF16)
    d_wmk = _matmul(memn, dmkv_b, dims="tn", out_dtype=F32, tm=1024, tn=512, tk=256, name="dw_mem_kv")
    dmemn = _matmul(dmkv_b, w_mk, dims="nt", out_dtype=F32, tm=256, tn=512, tk=1024, name="dmemn")
    (d_mem_gain,) = _rms_bwd(mem, mem_norm_gain, dmemn, None, tr=mem.shape[0], name="rms_mem_bwd")

    dproj = [dproj_a, jnp.concatenate([dqb, dkb, dvb, dzb, dproj_c], axis=1), dl0, dl1, dl2]
    dhn_f = _matmul(dfb, w_fb, dims="nt", out_dtype=F32, tm=1024, tn=512, tk=LANE, name="dhn_forget")
    d_wfb = _matmul(hn, dfb, dims="tn", out_dtype=F32, tm=1024, tn=LANE, tk=512, name="dw_forget")
    big = {}
    if on_mesh:
        half = D_MODEL // 2
        c0 = core[0]
        hn_other = lax.dynamic_slice(hn, (0, (1 - c0) * half), (s, half))
        hn_own = lax.dynamic_slice(hn, (0, c0 * half), (s, half))
        g1, k1 = [d_wmk, d_wa, d_wb, d_wc, d_wo], [2, 3, 4, 5, 6]
        d_other, (got1,) = _matmul(hn_other, dproj, dims="tn", out_dtype=BF16, tm=1024, tn=512, tk=4096,
                                   name="dw_main_other", comms=[_exchange_comm(g1, k1)])
        h1 = _add_half(g1, got1, core, [HALF_AXIS[k] for k in k1], name="add_half_small")
        d_own, (got0, parts1) = _matmul(
            hn_own, dproj, dims="tn", out_dtype=F32, tm=1024, tn=512, tk=4096, name="dw_main_own",
            comms=[_exchange_comm([d_other, d_wfb], [0, 1], whole=(0,)), _scatter_comm(h1, k1)])
        h0 = [_add_pair(d_own, got0[0], name="add_pair_main")] + _add_half([d_wfb], [got0[1]], core, [0], name="add_half_forget")
        sums1 = _sum4(parts1, name="sum4_small")
        dhn, (parts0, theirs1) = _matmul(dproj, w_main, dims="nt", out_dtype=F32, tm=1024, tn=512, tk=2048, vmem=VMEM_WIDE, name="dhn",
                                         add=dhn_f, comms=[_scatter_comm(h0, [0, 1]), _swap_comm(sums1)])
        sums0 = _sum4(parts0, name="sum4_main")
        (grad_x, d_gain), theirs0 = _rms_bwd(x, norm_gain, dhn, dout, tr=tr, name="rms_x_bwd", comm=_swap_comm(sums0))
        big = dict(sums=sums0 + sums1, theirs=list(theirs0) + list(theirs1))
    else:
        dhn = _matmul(dproj, w_main, dims="nt", out_dtype=F32, tm=1024, tn=512, tk=2048, vmem=VMEM_WIDE, name="dhn", add=dhn_f)
        d_wmain = _matmul(hn, dproj, dims="tn", out_dtype=F32, tm=1024, tn=512, tk=4096, name="dw_main")
        big = dict(d_wmain=d_wmain, d_wfb=d_wfb, d_wmk=d_wmk, d_wa=d_wa, d_wb=d_wb, d_wc=d_wc, d_wo=d_wo)
        grad_x, d_gain = _rms_bwd(x, norm_gain, dhn, dout, tr=tr, name="rms_x_bwd")

    fold = lambda g, reps: jnp.sum(g.reshape(reps, -1), axis=0, keepdims=True)
    return dict(
        sq=sq, grad_x=grad_x, **big,
        d_gain=d_gain, d_mem_gain=d_mem_gain, d_bf=d_bf[:, :N_FORGET],
        d_qga=fold(d_qga, 12), d_kga=fold(d_kga, 4), d_sinks=d_sinks[:, :A_HEADS],
        d_qgb=fold(d_qgb, 12), d_kgb=fold(d_kgb, 12), d_qgc=fold(d_qgc, 4), d_kgc=fold(d_kgc, 4))


PACK_ROWS = 256
FORGET_IN_SHARD = FORGET_COL - SHARD_COLS
AFTER_FORGET = FORGET_COL - SLAB_START[1]
END_CHIP1 = 2 * SHARD_COLS - N_FORGET - SLAB_START[1]


def _pack_w_in(chip, w):
    rows = w.shape[1]
    tr = PACK_ROWS

    def body(k_ref, w_ref, o_ref, scr):
        scr[...] = jnp.zeros_like(scr)
        scr[pl.ds(0, SHARD_COLS), :] = w_ref[...]
        v = jnp.transpose(scr[...])
        k = k_ref[0]
        col = lax.broadcasted_iota(jnp.int32, (tr, SLAB), 1)
        no_forget = jnp.zeros((tr, LANE), BF16)

        @pl.when(k == 0)
        def _():
            o_ref[:, 0:SLAB] = v.astype(BF16)
            o_ref[:, SLAB:] = no_forget

        @pl.when(k == 1)
        def _():
            before = pltpu.roll(v, SLAB_SHIFT[1], axis=1)
            after = pltpu.roll(v, SLAB - (N_FORGET - SLAB_SHIFT[1]), axis=1)
            slab = jnp.where(col < AFTER_FORGET, before, jnp.where(col < END_CHIP1, after, 0.0))
            o_ref[:, 0:SLAB] = slab.astype(BF16)
            f = pltpu.roll(v, SLAB - FORGET_IN_SHARD, axis=1)[:, :LANE]
            o_ref[:, SLAB:] = jnp.where(col[:, :LANE] < N_FORGET, f, 0.0).astype(BF16)

        for kk in (2, 3):
            @pl.when(k == kk)
            def _(kk=kk):
                o_ref[:, 0:SLAB] = pltpu.roll(v, SLAB_SHIFT[kk], axis=1).astype(BF16)
                o_ref[:, SLAB:] = no_forget

    return pl.pallas_call(
        body, grid_spec=pltpu.PrefetchScalarGridSpec(
            num_scalar_prefetch=1, grid=(rows // tr,),
            in_specs=[pl.BlockSpec((SHARD_COLS, tr), lambda i, k: (0, i))],
            out_specs=pl.BlockSpec((None, tr, SLAB + LANE), lambda i, k: (k[0], i, 0)),
            scratch_shapes=[pltpu.VMEM((SLAB, tr), F32)]),
        out_shape=SDS((N_CHIPS, rows, SLAB + LANE), BF16), name="pack_w_in",
        compiler_params=_params(("arbitrary",)))(chip, w)


def _merge_slabs(g):
    rows = g.shape[1]
    tr = PACK_ROWS
    t = [s // LANE for s in SLAB_START]
    n_t = SLAB // LANE

    def body(g_ref, m_ref, f_ref):
        for k in range(N_CHIPS):
            lo = t[k] + (1 if k > 0 else 0)
            hi = t[k + 1] if k + 1 < N_CHIPS else t[k] + n_t
            m_ref[:, lo * LANE:hi * LANE] = g_ref[k, :, (lo - t[k]) * LANE:(hi - t[k]) * LANE]
            if k + 1 < N_CHIPS:
                a = g_ref[k, :, (hi - t[k]) * LANE:(hi - t[k] + 1) * LANE].astype(F32)
                b = g_ref[k + 1, :, 0:LANE].astype(F32)
                m_ref[:, hi * LANE:(hi + 1) * LANE] = (a + b).astype(BF16)
        f_ref[...] = g_ref[1, :, SLAB:]

    return pl.pallas_call(
        body, grid=(rows // tr,),
        in_specs=[pl.BlockSpec((N_CHIPS, tr, SLAB + LANE), lambda i: (0, i, 0))],
        out_specs=[_rowblk(tr, P_MAIN), _rowblk(tr, LANE)],
        out_shape=[SDS((rows, P_MAIN), BF16), SDS((rows, LANE), BF16)], name="merge_slabs",
        compiler_params=_params(("parallel",)))(g)


def _adamw_math(w, g, m, v):
    nm = ADAM_B1 * m + (1.0 - ADAM_B1) * g
    nv = ADAM_B2 * v + (1.0 - ADAM_B2) * (g * g)
    m_hat = nm / (1.0 - ADAM_B1 ** ADAM_STEP)
    v_hat = nv / (1.0 - ADAM_B2 ** ADAM_STEP)
    delta = -ADAM_LR * (m_hat / (jnp.sqrt(v_hat) + ADAM_EPS) + ADAM_WD * w)
    return delta, nm, nv


def _adamw(g, w, m, v, *, tr, name):
    rows, cols = w.shape
    tr = min(tr, rows)

    def body(g_ref, w_ref, m_ref, v_ref, d_ref, nm_ref, nv_ref):
        d, nm, nv = _adamw_math(w_ref[...], g_ref[...], m_ref[...], v_ref[...])
        d_ref[...] = d
        nm_ref[...] = nm
        nv_ref[...] = nv

    spec = _rowblk(tr, cols)
    return pl.pallas_call(
        body, grid=(rows // tr,), in_specs=[spec] * 4, out_specs=[spec] * 3,
        out_shape=[SDS((rows, cols), F32)] * 3, name=name, compiler_params=_params(("parallel",)))(g, w, m, v)


def _adamw_w_in(chip_core, slab_mine, slab_theirs, forget_mine, forget_theirs, w, m, v):
    rows = w.shape[1]
    tr = PACK_ROWS // 2
    nbh = rows // 2 // tr

    def body(k_ref, sa_ref, sb_ref, fa_ref, fb_ref, w_ref, m_ref, v_ref, g_ref, d_ref, nm_ref, nv_ref):
        use_mine = pl.program_id(0) // nbh == k_ref[1]
        sl = jnp.where(use_mine, sa_ref[...], sb_ref[...])
        f_tile = jnp.where(use_mine, fa_ref[...], fb_ref[...])
        k = k_ref[0]

        def emit(wide):
            g = jnp.transpose(wide)[:SHARD_COLS, :]
            g_ref[...] = g
            d, nm, nv = _adamw_math(w_ref[...], g, m_ref[...], v_ref[...])
            d_ref[...] = d
            nm_ref[...] = nm
            nv_ref[...] = nv

        @pl.when(k == 0)
        def _():
            emit(sl)

        @pl.when(k == 1)
        def _():
            col = lax.broadcasted_iota(jnp.int32, (tr, SLAB), 1)
            before = pltpu.roll(sl, SLAB - SLAB_SHIFT[1], axis=1)
            after = pltpu.roll(sl, N_FORGET - SLAB_SHIFT[1], axis=1)
            wide_f = jnp.concatenate([f_tile, jnp.zeros((tr, SLAB - LANE), F32)], axis=1)
            forget = pltpu.roll(wide_f, FORGET_IN_SHARD, axis=1)
            emit(jnp.where(col < FORGET_IN_SHARD, before, jnp.where(col < FORGET_IN_SHARD + N_FORGET, forget, after)))

        for kk in (2, 3):
            @pl.when(k == kk)
            def _(kk=kk):
                emit(pltpu.roll(sl, SLAB - SLAB_SHIFT[kk], axis=1))

    nat = pl.BlockSpec((SHARD_COLS, tr), lambda i, k: (0, i))
    half = lambda width: pl.BlockSpec((tr, width), lambda i, k: (i % nbh, 0))
    return pl.pallas_call(
        body, grid_spec=pltpu.PrefetchScalarGridSpec(
            num_scalar_prefetch=1, grid=(rows // tr,),
            in_specs=[half(SLAB), half(SLAB), half(LANE), half(LANE), nat, nat, nat],
            out_specs=[nat] * 4),
        out_shape=[SDS((SHARD_COLS, rows), F32)] * 4, name="adamw_w_in",
        compiler_params=_params(("arbitrary",)))(chip_core, slab_mine, slab_theirs, forget_mine, forget_theirs, w, m, v)


ANY = pl.BlockSpec(memory_space=pl.ANY)
HALF_AXIS = (0, 0, 1, 0, 0, 0, 1)


def _me():
    return lax.axis_index("x"), lax.axis_index("y"), lax.axis_index("c")


def _half(ref, which, axis):
    n = ref.shape[axis] // 2
    sl = pl.ds(which * n, n)
    return ref.at[sl] if axis == 0 else ref.at[:, sl]


def _piece(t, ref, j):
    if t == 0:
        return ref.at[:, pl.ds(SLAB_START[j], SLAB)]
    if t == 1:
        return ref
    if t in (2, 6):
        return ref.at[pl.ds(512 * j, 512)]
    return ref.at[:, pl.ds(512 * j, 512)]


def _piece_shape(t, shape):
    if t == 0:
        return (shape[0], SLAB)
    if t == 1:
        return shape
    if t in (2, 6):
        return (512, shape[1])
    return (shape[0], 512)


def _gather_plan(ins, outs, own_slot_in_src):
    x, y, c = _me()
    k = 2 * x + y
    sib = (x, y, 1 - c)
    chips = [(1 - x, y), (x, 1 - y), (1 - x, 1 - y)]
    n = len(outs)

    def rows(t, which):
        h = outs[t].shape[1] // 2
        return pl.ds(which * h, h)

    def mine(t):
        return ins[t].at[k, rows(t, c)] if own_slot_in_src else ins[t].at[rows(t, c)]

    def first(t, j, sems):
        chip = chips[j]
        return pltpu.make_async_remote_copy(
            src_ref=mine(t), dst_ref=outs[t].at[k, rows(t, c)], send_sem=sems[0].at[t, j], recv_sem=sems[1].at[t, j],
            device_id=(chip[0], chip[1], c), device_id_type=MESH)

    def landed(t, j, sems):
        chip = chips[j]
        return pltpu.make_async_remote_copy(
            src_ref=mine(t), dst_ref=outs[t].at[2 * chip[0] + chip[1], rows(t, c)], send_sem=sems[0].at[t, j],
            recv_sem=sems[1].at[t, j], device_id=(chip[0], chip[1], c), device_id_type=MESH)

    def passed(t, j, which, sems):
        chip = chips[j]
        blk = outs[t].at[2 * chip[0] + chip[1], rows(t, which)]
        return pltpu.make_async_remote_copy(
            src_ref=blk, dst_ref=blk, send_sem=sems[2].at[t, j], recv_sem=sems[3].at[t, j], device_id=sib,
            device_id_type=MESH)

    def start(sems):
        for j in range(3):
            for t in range(n):
                first(t, j, sems).start()

    def finish(sems):
        for j in range(3):
            for t in range(n):
                landed(t, j, sems).wait_recv()
                passed(t, j, c, sems).start()
        for j in range(3):
            for t in range(n):
                passed(t, j, 1 - c, sems).wait_recv()
        for j in range(3):
            for t in range(n):
                first(t, j, sems).wait_send()
                passed(t, j, c, sems).wait_send()

    return k, start, finish


def _all_gather_slabs(slabs):
    def body(in_ref, out_ref, nbr_sem, quarter_sem, pass_sem):
        x, y, c = _me()
        k = 2 * x + y
        rows = out_ref.shape[1]
        h, q = rows // 2, rows // 4
        nbrs = [(1 - x, y), (x, 1 - y)]
        slot = lambda chip: 2 * chip[0] + chip[1]
        diag = 2 * (1 - x) + (1 - y)
        half = pl.ds(c * h, h)
        quarter = lambda a: pl.ds(c * h + a * q, q)

        def first(a):
            return pltpu.make_async_remote_copy(
                src_ref=in_ref.at[k, half], dst_ref=out_ref.at[k, half], send_sem=nbr_sem.at[0, a],
                recv_sem=nbr_sem.at[1, a], device_id=(nbrs[a][0], nbrs[a][1], c), device_id_type=MESH)

        def landed(a):
            blk = out_ref.at[slot(nbrs[a]), half]
            return pltpu.make_async_remote_copy(
                src_ref=blk, dst_ref=blk, send_sem=nbr_sem.at[0, a], recv_sem=nbr_sem.at[1, a],
                device_id=(nbrs[a][0], nbrs[a][1], c), device_id_type=MESH)

        def relay(a):
            blk = out_ref.at[slot(nbrs[a]), quarter(a)]
            to = nbrs[1 - a]
            return pltpu.make_async_remote_copy(
                src_ref=blk, dst_ref=blk, send_sem=quarter_sem.at[0, a], recv_sem=quarter_sem.at[1, a],
                device_id=(to[0], to[1], c), device_id_type=MESH)

        def relayed(a):
            blk = out_ref.at[diag, quarter(a)]
            frm = nbrs[1 - a]
            return pltpu.make_async_remote_copy(
                src_ref=blk, dst_ref=blk, send_sem=quarter_sem.at[0, a], recv_sem=quarter_sem.at[1, a],
                device_id=(frm[0], frm[1], c), device_id_type=MESH)

        def passed(j, which):
            sl = diag if j == 2 else slot(nbrs[j])
            blk = out_ref.at[sl, pl.ds(which * h, h)]
            return pltpu.make_async_remote_copy(
                src_ref=blk, dst_ref=blk, send_sem=pass_sem.at[0, j], recv_sem=pass_sem.at[1, j],
                device_id=(x, y, 1 - c), device_id_type=MESH)

        for a in range(2):
            first(a).start()
        for a in range(2):
            landed(a).wait_recv()
            relay(a).start()
            passed(a, c).start()
        for a in range(2):
            relayed(a).wait_recv()
        passed(2, c).start()
        for j in range(3):
            passed(j, 1 - c).wait_recv()
        for a in range(2):
            first(a).wait_send()
            relay(a).wait_send()
        for j in range(3):
            passed(j, c).wait_send()

    return pl.pallas_call(
        body, in_specs=[ANY], out_specs=ANY, out_shape=SDS(slabs.shape, slabs.dtype),
        scratch_shapes=[pltpu.SemaphoreType.DMA((2, 2)), pltpu.SemaphoreType.DMA((2, 2)), pltpu.SemaphoreType.DMA((2, 3))],
        input_output_aliases={0: 0}, name="all_gather_slabs")(slabs)


def _gather_comm(parts):
    n = len(parts)

    def start(ins, outs, sems):
        k, go, _ = _gather_plan(ins, outs, False)
        for t in range(n):
            pltpu.make_async_copy(ins[t], outs[t].at[k], sems[4].at[t]).start()
        go(sems)

    def finish(ins, outs, sems):
        k, _, done = _gather_plan(ins, outs, False)
        done(sems)
        for t in range(n):
            pltpu.make_async_copy(ins[t], outs[t].at[k], sems[4].at[t]).wait()

    return _Comm(parts, [SDS((N_CHIPS,) + p.shape, p.dtype) for p in parts],
                 [pltpu.SemaphoreType.DMA((n, 3))] * 4 + [pltpu.SemaphoreType.DMA((n,))], start, finish)


def _exchange_comm(arrs, kinds, whole=()):
    n = len(arrs)

    def copies(ins, outs, sems):
        x, y, c = _me()
        return [pltpu.make_async_remote_copy(
            src_ref=ins[t] if t in whole else _half(ins[t], 1 - c, HALF_AXIS[kinds[t]]), dst_ref=outs[t],
            send_sem=sems[0].at[t], recv_sem=sems[1].at[t], device_id=(x, y, 1 - c), device_id_type=MESH)
            for t in range(n)]

    def start(ins, outs, sems):
        for cp in copies(ins, outs, sems):
            cp.start()

    def finish(ins, outs, sems):
        for cp in copies(ins, outs, sems):
            cp.wait()

    def hshape(t):
        s = list(arrs[t].shape)
        if t not in whole:
            s[HALF_AXIS[kinds[t]]] //= 2
        return SDS(tuple(s), arrs[t].dtype)

    return _Comm(arrs, [hshape(t) for t in range(n)], [pltpu.SemaphoreType.DMA((n,))] * 2, start, finish)


ADD_HALF_STEPS = 4


def _add_half(fulls, gots, core, axes, *, name):
    ns = ADD_HALF_STEPS
    n = len(fulls)
    in_specs, out_specs, out_shape = [], [], []
    for got, axis in zip(gots, axes):
        r, c = got.shape
        blk = (r // ns, c)
        if axis == 0:
            in_specs.append(pl.BlockSpec(blk, lambda i, cr: (i + cr[0] * ns, 0)))
        else:
            in_specs.append(pl.BlockSpec(blk, lambda i, cr: (i, cr[0])))
        out_specs.append(pl.BlockSpec(blk, lambda i, cr: (i, 0)))
        out_shape.append(SDS((r, c), BF16))
    in_specs += list(out_specs)

    def body(c_ref, *refs):
        for a_ref, b_ref, o_ref in zip(refs[:n], refs[n:2 * n], refs[2 * n:]):
            o_ref[...] = (a_ref[...] + b_ref[...]).astype(BF16)

    return list(pl.pallas_call(
        body, grid_spec=pltpu.PrefetchScalarGridSpec(num_scalar_prefetch=1, grid=(ns,), in_specs=in_specs,
                                                     out_specs=out_specs),
        out_shape=out_shape, name=name, compiler_params=_params(("parallel",)))(core, *fulls, *gots))


def _add_pair(a, b, *, name):
    r, c = a.shape
    br, bc = 256, min(2048, c)

    def body(a_ref, b_ref, o_ref):
        o_ref[...] = (a_ref[...] + b_ref[...].astype(F32)).astype(BF16)

    spec = pl.BlockSpec((br, bc), lambda i, j: (i, j))
    return pl.pallas_call(body, grid=(r // br, c // bc), in_specs=[spec, spec], out_specs=spec,
                          out_shape=SDS((r, c), BF16), name=name, compiler_params=_params(("parallel", "parallel")))(a, b)


def _scatter_comm(halves, kinds):
    n = len(halves)

    def plan(ins, outs, sems):
        send, recv, lsem = sems
        x, y, c = _me()
        k = 2 * x + y

        def to_chip(t, j):
            return pltpu.make_async_remote_copy(
                src_ref=_piece(kinds[t], ins[t], j), dst_ref=outs[t].at[k], send_sem=send.at[t, j],
                recv_sem=recv.at[t, k], device_id=(j // 2, j % 2, c), device_id_type=MESH)

        def from_chip(t, j):
            return pltpu.make_async_remote_copy(
                src_ref=_piece(kinds[t], ins[t], j), dst_ref=outs[t].at[j], send_sem=send.at[t, j],
                recv_sem=recv.at[t, j], device_id=(j // 2, j % 2, c), device_id_type=MESH)

        def own(t, j):
            return pltpu.make_async_copy(_piece(kinds[t], ins[t], j), outs[t].at[j], lsem.at[t])

        return k, to_chip, from_chip, own

    def start(ins, outs, sems):
        k, to_chip, _, own = plan(ins, outs, sems)
        for j in range(N_CHIPS):
            @pl.when(k != j)
            def _(j=j):
                for t in range(n):
                    to_chip(t, j).start()

            @pl.when(k == j)
            def _(j=j):
                for t in range(n):
                    own(t, j).start()

    def finish(ins, outs, sems):
        k, to_chip, from_chip, own = plan(ins, outs, sems)
        for j in range(N_CHIPS):
            @pl.when(k != j)
            def _(j=j):
                for t in range(n):
                    from_chip(t, j).wait_recv()
                for t in range(n):
                    to_chip(t, j).wait_send()

            @pl.when(k == j)
            def _(j=j):
                for t in range(n):
                    own(t, j).wait()

    return _Comm(halves, [SDS((N_CHIPS,) + _piece_shape(kinds[t], halves[t].shape), halves[t].dtype) for t in range(n)],
                 [pltpu.SemaphoreType.DMA((n, N_CHIPS))] * 2 + [pltpu.SemaphoreType.DMA((n,))], start, finish)


SUM4_STEPS = 4


def _sum4(ps, *, name):
    ns = SUM4_STEPS

    def body(*refs):
        for p_ref, o_ref in zip(refs[:len(ps)], refs[len(ps):]):
            o_ref[...] = ((p_ref[0].astype(F32) + p_ref[1].astype(F32)) + p_ref[2].astype(F32)) + p_ref[3].astype(F32)

    return pl.pallas_call(
        body, grid=(ns,),
        in_specs=[pl.BlockSpec((N_CHIPS, p.shape[1] // ns, p.shape[2]), lambda i: (0, i, 0)) for p in ps],
        out_specs=[_rowblk(p.shape[1] // ns, p.shape[2]) for p in ps],
        out_shape=[SDS(p.shape[1:], F32) for p in ps], name=name, compiler_params=_params(("parallel",)))(*ps)


def _swap_comm(sums):
    return _exchange_comm(sums, [None] * len(sums), whole=tuple(range(len(sums))))


ADAMW_STEPS = 8


def _adamw_halves(items, core):
    ns = ADAMW_STEPS
    nbh = ns // 2
    n = len(items)
    in_specs, out_specs, out_shape, ins = [], [], [], []
    for mine, theirs, w, m, v, axis in items:
        rows, cols = w.shape
        tr = rows // ns
        if axis == 0:
            g_spec = pl.BlockSpec((tr, cols), lambda i, cr: (i % nbh, 0))
        else:
            g_spec = pl.BlockSpec((tr, cols // 2), lambda i, cr: (i, 0))
        nat = pl.BlockSpec((tr, cols), lambda i, cr: (i, 0))
        in_specs += [g_spec, g_spec, nat, nat, nat]
        out_specs += [nat] * 4
        out_shape += [SDS((rows, cols), F32)] * 4
        ins += [mine, theirs, w, m, v]

    def body(c_ref, *refs):
        for t, item in enumerate(items):
            a_ref, b_ref, w_ref, m_ref, v_ref = refs[5 * t:5 * t + 5]
            g_ref, d_ref, nm_ref, nv_ref = refs[5 * n + 4 * t:5 * n + 4 * t + 4]
            a, b = a_ref[...], b_ref[...]
            if item[5] == 0:
                g = jnp.where(pl.program_id(0) // nbh == c_ref[0], a, b)
            else:
                low = c_ref[0] == 0
                g = jnp.concatenate([jnp.where(low, a, b), jnp.where(low, b, a)], axis=1)
            g_ref[...] = g
            d, nm, nv = _adamw_math(w_ref[...], g, m_ref[...], v_ref[...])
            d_ref[...] = d
            nm_ref[...] = nm
            nv_ref[...] = nv

    res = pl.pallas_call(
        body, grid_spec=pltpu.PrefetchScalarGridSpec(
            num_scalar_prefetch=1, grid=(ns,), in_specs=in_specs, out_specs=out_specs),
        out_shape=out_shape, name="adamw_shards", compiler_params=_params(("arbitrary",)))(core, *ins)
    return [tuple(res[4 * t:4 * t + 4]) for t in range(n)]


SMALL_ROWS, SMALL_COLS = 8, 1024


def _pack_small(vs):
    flat = jnp.concatenate([v.reshape(-1) for v in vs])
    return jnp.pad(flat, (0, SMALL_ROWS * SMALL_COLS - flat.shape[0])).reshape(SMALL_ROWS, SMALL_COLS)


def _unpack_small(packed, sizes):
    flat = packed.reshape(-1)
    out, o = [], 0
    for n in sizes:
        out.append(flat[o:o + n].reshape(1, n))
        o += n
    return out


def _all_reduce_small(v):
    n_dev = 8

    def body(v_ref, o_ref, land, send, recv):
        x, y, c = _me()
        me = 4 * x + 2 * y + c
        land[me] = v_ref[...]
        cps = []
        for r in range(1, n_dev):
            fx, fy, fc = (r >> 2) & 1, (r >> 1) & 1, r & 1
            peer = (x ^ fx, y ^ fy, c ^ fc)
            cps.append(pltpu.make_async_remote_copy(
                src_ref=v_ref, dst_ref=land.at[me], send_sem=send.at[r - 1], recv_sem=recv.at[r - 1],
                device_id=peer, device_id_type=MESH))
        for cp in cps:
            cp.start()
        for r in range(1, n_dev):
            fx, fy, fc = (r >> 2) & 1, (r >> 1) & 1, r & 1
            src = 4 * (x ^ fx) + 2 * (y ^ fy) + (c ^ fc)
            pltpu.make_async_remote_copy(
                src_ref=v_ref, dst_ref=land.at[src], send_sem=send.at[r - 1], recv_sem=recv.at[r - 1],
                device_id=(x ^ fx, y ^ fy, c ^ fc), device_id_type=MESH).wait_recv()
        for cp in cps:
            cp.wait_send()
        acc = land[0]
        for r in range(1, n_dev):
            acc = acc + land[r]
        o_ref[...] = acc

    vm = pl.BlockSpec(memory_space=pltpu.VMEM)
    return pl.pallas_call(
        body, in_specs=[vm], out_specs=vm, out_shape=SDS(v.shape, F32),
        scratch_shapes=[pltpu.VMEM((n_dev,) + v.shape, F32), pltpu.SemaphoreType.DMA((n_dev - 1,)),
                        pltpu.SemaphoreType.DMA((n_dev - 1,))],
        name="all_reduce_small")(v)


def kernel(x, mem, norm_gain, mem_norm_gain, w_in, b_forget, q_gain_a, k_gain_a, sinks_a, q_gain_b, k_gain_b, q_gain_c, k_gain_c, w_mem_kv, w_branch_a, w_branch_b, w_branch_c, w_out, loss_target, m_norm_gain, m_mem_norm_gain, m_w_in, m_b_forget, m_q_gain_a, m_k_gain_a, m_sinks_a, m_q_gain_b, m_k_gain_b, m_q_gain_c, m_k_gain_c, m_w_mem_kv, m_w_branch_a, m_w_branch_b, m_w_branch_c, m_w_out, v_norm_gain, v_mem_norm_gain, v_w_in, v_b_forget, v_q_gain_a, v_k_gain_a, v_sinks_a, v_q_gain_b, v_k_gain_b, v_q_gain_c, v_k_gain_c, v_w_mem_kv, v_w_branch_a, v_w_branch_b, v_w_branch_c, v_w_out):
    xi, yi, ci = lax.axis_index("x"), lax.axis_index("y"), lax.axis_index("c")
    chip = jnp.reshape(2 * xi + yi, (1,)).astype(jnp.int32)
    core = jnp.reshape(ci, (1,)).astype(jnp.int32)

    slabs = _pack_w_in(chip, jnp.transpose(w_in[0]))
    mine = [w_mem_kv[0].astype(BF16), w_branch_a[0].astype(BF16), w_branch_b[0].astype(BF16),
            w_branch_c[0].astype(BF16), w_out[0].astype(BF16)]
    w_main, w_fb = _merge_slabs(_all_gather_slabs(slabs))

    r = _local_step(x[0], mem[0], loss_target[0], w_main, w_fb, mine, norm_gain, mem_norm_gain,
                    b_forget, q_gain_a, k_gain_a, sinks_a, q_gain_b, k_gain_b, q_gain_c, k_gain_c, core=core)
    sums, theirs = r["sums"], r["theirs"]

    small_names = ["d_gain", "d_mem_gain", "d_bf", "d_qga", "d_kga", "d_sinks", "d_qgb", "d_kgb", "d_qgc", "d_kgc"]
    loss_part = (0.5 / D_MODEL) * jnp.sum(r["sq"], axis=1, keepdims=True)
    packed = _pack_small([r[n] for n in small_names] + [loss_part])
    red = _all_reduce_small(packed)
    small_w = [norm_gain, mem_norm_gain, b_forget, q_gain_a, k_gain_a, sinks_a, q_gain_b, k_gain_b, q_gain_c, k_gain_c]
    small_m = [m_norm_gain, m_mem_norm_gain, m_b_forget, m_q_gain_a, m_k_gain_a, m_sinks_a, m_q_gain_b, m_k_gain_b,
               m_q_gain_c, m_k_gain_c]
    small_v = [v_norm_gain, v_mem_norm_gain, v_b_forget, v_q_gain_a, v_k_gain_a, v_sinks_a, v_q_gain_b, v_k_gain_b,
               v_q_gain_c, v_k_gain_c]
    sizes = [w.shape[1] for w in small_w]
    s_d, s_m, s_v = _adamw(red, _pack_small(small_w), _pack_small(small_m), _pack_small(small_v), tr=8, name="adamw_small")
    g_small = _unpack_small(red, sizes + [1])
    loss = g_small[-1].reshape(())
    d_small, m_small, v_small = _unpack_small(s_d, sizes), _unpack_small(s_m, sizes), _unpack_small(s_v, sizes)

    gw_in, dw_in, mw_in, vw_in = _adamw_w_in(jnp.concatenate([chip, core]), sums[0], theirs[0], sums[1], theirs[1],
                                             jnp.transpose(w_in[0]), jnp.transpose(m_w_in[0]), jnp.transpose(v_w_in[0]))
    shards = ((2, "w_mem_kv", w_mem_kv, m_w_mem_kv, v_w_mem_kv),
              (3, "w_branch_a", w_branch_a, m_w_branch_a, v_w_branch_a),
              (4, "w_branch_b", w_branch_b, m_w_branch_b, v_w_branch_b),
              (5, "w_branch_c", w_branch_c, m_w_branch_c, v_w_branch_c),
              (6, "w_out", w_out, m_w_out, v_w_out))
    done = _adamw_halves([(sums[t], theirs[t], w[0], m[0], v[0], HALF_AXIS[t]) for t, _, w, m, v in shards], core)
    big = {nm: res for (_, nm, _, _, _), res in zip(shards, done)}

    def collect(kind):
        sm = (g_small, d_small, m_small, v_small)[kind]
        win = (gw_in, dw_in, mw_in, vw_in)[kind]
        return ([sm[0], sm[1], jnp.transpose(win)[None]] + [a for a in sm[2:10]]
                + [big[n][kind][None] for n in ("w_mem_kv", "w_branch_a", "w_branch_b", "w_branch_c", "w_out")])

    return (loss, r["grad_x"][None], *collect(0), *collect(1), *collect(2), *collect(3))
```

```python
import numpy as np
import jax
import jax.numpy as jnp
from jax import lax
from jax.experimental import pallas as pl
from jax.experimental.pallas import tpu as pltpu

F32 = jnp.float32
BF16 = jnp.bfloat16
HI = lax.Precision.HIGHEST
SDS = jax.ShapeDtypeStruct
MESH = pl.DeviceIdType.MESH

D_MODEL = 2048
HEAD_DIM = 64
A_HEADS = 12
A_GROUP = 3
B_HEADS = 12
C_HEADS = 4
C_HEAD_DIM = 128
WINDOW = 128
EPS = 1e-6
NEG = -1e30
LANE = 128

QA, KA, VA, ZA = 0, 768, 1024, 1280
QB, KB, VB, ZB = 2048, 2816, 3584, 4352
QC, ZC = 5120, 5632
GATE = 6144
P_MAIN = 12288
N_FORGET = 12
FORGET_COL = 5120
SHARD_COLS = 3075
SLAB = 3200
SLAB_START = (0, 3072, 6016, 9088)
SLAB_SHIFT = (0, 3, 122, 125)
N_CHIPS = 4

ADAM_LR = 0.001
ADAM_B1 = 0.9
ADAM_B2 = 0.999
ADAM_EPS = 1e-08
ADAM_WD = 0.01
ADAM_STEP = 10

VMEM_LIMIT = 56 * 1024 * 1024
VMEM_WIDE = 62 * 1024 * 1024


def _params(sem, vmem=VMEM_LIMIT):
    return pltpu.CompilerParams(dimension_semantics=sem, vmem_limit_bytes=vmem)


def _win(tr, width, off):
    return pl.BlockSpec((pl.Element(tr), pl.Element(width)), lambda i, *_: (i * tr, off))


def _rowblk(tr, width):
    return pl.BlockSpec((tr, width), lambda i, *_: (i, 0))


def _const(shape):
    nd = len(shape)
    return pl.BlockSpec(shape, lambda *_: (0,) * nd)


def _rms(x, g):
    return x * lax.rsqrt(jnp.mean(x * x, axis=-1, keepdims=True) + EPS) * g


def _head_mean_impl(x2, bd):
    hi = x2.astype(BF16)
    lo = (x2 - hi.astype(F32)).astype(BF16)
    return _dot(hi, bd) + _dot(lo, bd)


@jax.custom_vjp
def _head_mean(x2, bd):
    return _head_mean_impl(x2, bd)


_head_mean.defvjp(lambda x2, bd: (_head_mean_impl(x2, bd), bd),
                  lambda bd, g: (_head_mean_impl(g, bd), jnp.zeros_like(bd)))


def _head_norm(x, g_tiled, bd):
    return x * lax.rsqrt(_head_mean(x * x, bd) + EPS) * g_tiled


def _silu(z):
    return z * jax.nn.sigmoid(z)


def _dot_nt(a, b):
    return lax.dot_general(a, b, (((1,), (1,)), ((), ())), preferred_element_type=F32)


def _dot_tn(a, b):
    return lax.dot_general(a, b, (((0,), (0,)), ((), ())), preferred_element_type=F32)


def _dot(a, b):
    return jnp.dot(a, b, preferred_element_type=F32)


def _swa_fn(qk, vz, qkp, vzp, qg, kg, sinks, bd, bias, first):
    q = _head_norm(qk[:, :768], qg, bd)
    k2 = jnp.concatenate([qkp[:, 768:], qk[:, 768:]], axis=0)
    k2 = _head_norm(k2, kg, bd[:256, :256])
    v2 = jnp.concatenate([vzp[:, :256], vz[:, :256]], axis=0)
    z = vz[:, 256:]
    cols = A_GROUP * WINDOW
    kj = lax.broadcasted_iota(jnp.int32, (2 * WINDOW, cols), 0)
    no_prev = kj < WINDOW * first.astype(jnp.int32)
    qtb = jnp.transpose(q).astype(BF16)
    kb = k2.astype(BF16)
    vtb = jnp.transpose(v2).astype(BF16)
    outs = [None] * A_HEADS
    for g in range(A_HEADS // A_GROUP):
        heads = [A_GROUP * g + u for u in range(A_GROUP)]
        qs = jnp.concatenate([qtb[64 * h:64 * h + 64, :] for h in heads], axis=1)
        s = _dot(kb[:, 64 * g:64 * g + 64], qs) * (HEAD_DIM ** -0.5) + bias[g]
        s = jnp.where(no_prev, NEG, s)
        sink = jnp.concatenate([jnp.broadcast_to(sinks[:, h:h + 1], (1, WINDOW)) for h in heads], axis=1)
        m = lax.stop_gradient(jnp.maximum(jnp.max(s, axis=0, keepdims=True), sink))
        p = jnp.exp(s - m)
        den = jnp.sum(p, axis=0, keepdims=True) + jnp.exp(sink - m)
        o = _dot(vtb[64 * g:64 * g + 64, :], (p * (1.0 / den)).astype(BF16))
        for u, h in enumerate(heads):
            outs[h] = o[:, WINDOW * u:WINDOW * u + WINDOW]
    return jnp.transpose(jnp.concatenate(outs, axis=0)) * _silu(z)


def _swa_bias():
    qi = np.arange(WINDOW)[None, :]
    kj = np.arange(2 * WINDOW)[:, None]
    rel = qi + WINDOW - kj
    valid = (rel >= 0) & (rel < WINDOW)
    out = np.zeros((A_HEADS // A_GROUP, 2 * WINDOW, A_GROUP * WINDOW), np.float32)
    for h in range(A_HEADS):
        slope = np.float32(2.0 ** (-8.0 * (h + 1) / A_HEADS))
        blk = np.where(valid, -slope * rel.astype(np.float32), np.float32(NEG))
        g, u = divmod(h, A_GROUP)
        out[g, :, WINDOW * u:WINDOW * u + WINDOW] = blk
    return jnp.asarray(out)


def _mem_fn(qz, mkv, qg, kg, bd):
    q = _head_norm(qz[:, :512], qg, bd).astype(BF16)
    k = _head_norm(mkv[:, :512], kg, bd).astype(BF16)
    v = mkv[:, 512:].astype(BF16)
    z = qz[:, 512:]
    outs = []
    for h in range(C_HEADS):
        sl = slice(128 * h, 128 * h + 128)
        s = _dot_nt(q[:, sl], k[:, sl]) * (C_HEAD_DIM ** -0.5)
        m = lax.stop_gradient(jnp.max(s, axis=-1, keepdims=True))
        p = jnp.exp(s - m)
        den = jnp.sum(p, axis=-1, keepdims=True)
        outs.append(_dot((p * (1.0 / den)).astype(BF16), v[:, sl]))
    return jnp.concatenate(outs, axis=1) * _silu(z)


def _qn_fn(q, g, bd):
    return _head_norm(q, g, bd) * (HEAD_DIM ** -0.5)


def _kn_fn(k, g, bd):
    return _head_norm(k, g, bd)


def _block_diag(width, hd):
    i = np.arange(width) // hd
    return jnp.asarray((i[:, None] == i[None, :]).astype(np.float32) / hd, BF16)


def _head_sum(width, hd):
    i = np.arange(width) // hd
    return jnp.asarray((i[:, None] == np.arange(LANE)[None, :]).astype(np.float32))


def _rms_fwd(x, g, *, tr, name):
    rows, dm = x.shape

    def body(x_ref, g_ref, o_ref):
        o_ref[...] = _rms(x_ref[...], g_ref[...]).astype(BF16)

    return pl.pallas_call(
        body, grid=(rows // tr,),
        in_specs=[_rowblk(tr, dm), _const((1, dm))],
        out_specs=_rowblk(tr, dm),
        out_shape=SDS((rows, dm), BF16), name=name,
        compiler_params=_params(("parallel",)))(x, g)


def _rms_bwd(x, g, dy, resid, *, tr, name, comm=None):
    rows, dm = x.shape
    want_dx = resid is not None
    n_in = 4 if want_dx else 3
    n_out = 2 if want_dx else 1
    c_in = len(comm.ins) if comm else 0
    c_out = len(comm.out_shapes) if comm else 0
    nb = rows // tr

    def body(*refs):
        x_ref, g_ref, dy_ref = refs[:3]
        r_ref = refs[3] if want_dx else None
        cin = refs[n_in:n_in + c_in]
        outs = refs[n_in + c_in:n_in + c_in + n_out]
        dg_ref = outs[-1]
        cout = refs[n_in + c_in + n_out:n_in + c_in + n_out + c_out]
        csem = refs[n_in + c_in + n_out + c_out:]

        if comm:
            @pl.when(pl.program_id(0) == 0)
            def _():
                comm.start(cin, cout, csem)

        _, vjp = jax.vjp(_rms, x_ref[...], g_ref[...])
        dx, dg = vjp(dy_ref[...])

        @pl.when(pl.program_id(0) == 0)
        def _():
            dg_ref[...] = jnp.zeros_like(dg_ref)

        dg_ref[...] += dg
        if want_dx:
            outs[0][...] = r_ref[...] + dx

        if comm:
            @pl.when(pl.program_id(0) == nb - 1)
            def _():
                comm.finish(cin, cout, csem)

    hbm = pl.BlockSpec(memory_space=pl.ANY)
    ins = [x, g, dy] + ([resid] if want_dx else []) + (list(comm.ins) if comm else [])
    in_specs = ([_rowblk(tr, dm), _const((1, dm)), _rowblk(tr, dm)] + ([_rowblk(tr, dm)] if want_dx else [])
                + [hbm] * c_in)
    out_specs = ([_rowblk(tr, dm)] if want_dx else []) + [_const((1, dm))] + [hbm] * c_out
    out_shape = (([SDS((rows, dm), F32)] if want_dx else []) + [SDS((1, dm), F32)]
                 + (list(comm.out_shapes) if comm else []))
    res = pl.pallas_call(
        body, grid=(nb,), in_specs=in_specs, out_specs=out_specs, out_shape=out_shape,
        scratch_shapes=list(comm.sems) if comm else [], name=name, compiler_params=_params(("arbitrary",)))(*ins)
    return (list(res[:n_out]), list(res[n_out:])) if comm else res


class _Comm:
    def __init__(self, ins, out_shapes, sems, start, finish):
        self.ins, self.out_shapes, self.sems, self.start, self.finish = list(ins), list(out_shapes), list(sems), start, finish


def _matmul(a, b, *, dims, out_dtype, tm, tn, tk, name, add=None, comms=(), vmem=VMEM_LIMIT, a_half=None):
    a_list = list(a) if isinstance(a, (list, tuple)) else [a]
    b_list = list(b) if isinstance(b, (list, tuple)) else [b]
    assert len(a_list) == 1 or dims == "nt"
    assert len(b_list) == 1 or dims == "tn"
    if dims == "tn":
        kdim, m = a_list[0].shape
        if a_half is not None:
            m //= 2
    else:
        m, kdim = a_list[0].shape[0], sum(p.shape[1] for p in a_list)
    n = b_list[0].shape[0] if dims == "nt" else sum(p.shape[1] for p in b_list)
    tm, tn, tk = min(tm, m), min(tn, n), min(tk, kdim)
    assert m % tm == 0 and n % tn == 0 and kdim % tk == 0, (name, m, n, kdim)
    ni, nj, nk = m // tm, n // tn, kdim // tk
    a_rng, b_rng, pos = [], [], 0
    for p in a_list:
        assert len(a_list) == 1 or p.shape[1] % tk == 0
        a_rng.append((pos, p.shape[1] // tk if len(a_list) > 1 else nk))
        pos += a_rng[-1][1]
    pos = 0
    for p in b_list:
        assert len(b_list) == 1 or p.shape[1] % tn == 0
        b_rng.append((pos, p.shape[1] // tn if len(b_list) > 1 else nj))
        pos += b_rng[-1][1]
    has_add = add is not None
    n_mm_in = len(a_list) + len(b_list) + (1 if has_add else 0)
    c_in = [len(c.ins) for c in comms]
    c_out = [len(c.out_shapes) for c in comms]
    c_sem = [len(c.sems) for c in comms]

    def body(*refs):
        if a_half is not None:
            refs = refs[1:]
        a_refs, b_refs = refs[:len(a_list)], refs[len(a_list):len(a_list) + len(b_list)]
        add_ref = refs[n_mm_in - 1] if has_add else None
        pos = n_mm_in
        cin = []
        for cnt in c_in:
            cin.append(refs[pos:pos + cnt])
            pos += cnt
        o_ref = refs[pos]
        pos += 1
        cout = []
        for cnt in c_out:
            cout.append(refs[pos:pos + cnt])
            pos += cnt
        acc = refs[pos]
        pos += 1
        csem = []
        for cnt in c_sem:
            csem.append(refs[pos:pos + cnt])
            pos += cnt
        i, j, k = pl.program_id(0), pl.program_id(1), pl.program_id(2)

        if comms:
            @pl.when((i == 0) & (j == 0) & (k == 0))
            def _():
                for c, ci, co, cs in zip(comms, cin, cout, csem):
                    c.start(ci, co, cs)

        def accumulate(a_ref, b_ref, first_k, later_k):
            if dims == "nn":
                part = _dot(a_ref[...], b_ref[...])
            elif dims == "nt":
                part = _dot_nt(a_ref[...], b_ref[...])
            else:
                part = _dot_tn(a_ref[...], b_ref[...])

            if first_k:
                @pl.when(k == 0)
                def _():
                    acc[...] = part + add_ref[...] if has_add else part

            if later_k:
                @pl.when(k > 0)
                def _():
                    acc[...] += part

        if len(a_list) > 1:
            for a_ref, (k0, cnt) in zip(a_refs, a_rng):
                @pl.when((k >= k0) & (k < k0 + cnt))
                def _(a_ref=a_ref, k0=k0, cnt=cnt):
                    accumulate(a_ref, b_refs[0], k0 == 0, k0 + cnt > 1)
        elif len(b_list) > 1:
            for b_ref, (j0, cnt) in zip(b_refs, b_rng):
                @pl.when((j >= j0) & (j < j0 + cnt))
                def _(b_ref=b_ref):
                    accumulate(a_refs[0], b_ref, True, nk > 1)
        else:
            accumulate(a_refs[0], b_refs[0], True, nk > 1)

        @pl.when(k == nk - 1)
        def _():
            o_ref[...] = acc[...].astype(out_dtype)

        if comms:
            @pl.when((i == ni - 1) & (j == nj - 1) & (k == nk - 1))
            def _():
                for c, ci, co, cs in zip(comms, cin, cout, csem):
                    c.finish(ci, co, cs)

    def a_spec(k0, cnt):
        if dims == "tn":
            if a_half is None:
                return pl.BlockSpec((tk, tm), lambda i, j, k: (k, i))
            pick = (lambda c: 1 - c[0]) if a_half[1] else (lambda c: c[0])
            return pl.BlockSpec((tk, tm), lambda i, j, k, c: (k, i + pick(c) * ni))
        return pl.BlockSpec((tm, tk), lambda i, j, k, *_: (i, jnp.clip(k - k0, 0, cnt - 1)))

    def b_spec(j0, cnt):
        if dims == "nt":
            return pl.BlockSpec((tn, tk), lambda i, j, k, *_: (j, k))
        return pl.BlockSpec((tk, tn), lambda i, j, k, *_: (k, jnp.clip(j - j0, 0, cnt - 1)))

    o_spec = pl.BlockSpec((tm, tn), lambda i, j, k, *_: (i, j))
    hbm = pl.BlockSpec(memory_space=pl.ANY)
    ins = a_list + b_list + ([add] if has_add else []) + [x for c in comms for x in c.ins]
    in_specs = ([a_spec(*r) for r in a_rng] + [b_spec(*r) for r in b_rng] + ([o_spec] if has_add else [])
                + [hbm] * sum(c_in))
    out_specs = [o_spec] + [hbm] * sum(c_out)
    out_shape = [SDS((m, n), out_dtype)] + [s for c in comms for s in c.out_shapes]
    scratch = [pltpu.VMEM((tm, tn), F32)] + [s for c in comms for s in c.sems]
    sem = ("arbitrary",) * 3 if comms else ("parallel", "parallel", "arbitrary")
    if a_half is None:
        grid = dict(grid=(ni, nj, nk), in_specs=in_specs, out_specs=out_specs, scratch_shapes=scratch)
    else:
        grid = dict(grid_spec=pltpu.PrefetchScalarGridSpec(
            num_scalar_prefetch=1, grid=(ni, nj, nk), in_specs=in_specs, out_specs=out_specs, scratch_shapes=scratch))
        ins = [a_half[0]] + ins
    res = pl.pallas_call(body, out_shape=out_shape, name=name, compiler_params=_params(sem, vmem), **grid)(*ins)
    if not comms:
        return res[0]
    outs, pos = [], 1
    for cnt in c_out:
        outs.append(list(res[pos:pos + cnt]))
        pos += cnt
    return res[0], outs


def _swa_specs(nb, blk=lambda n: n):
    cur = lambda off: pl.BlockSpec((pl.Element(WINDOW), pl.Element(1024)), lambda n: (blk(n) * WINDOW, off))
    prev = lambda off: pl.BlockSpec((pl.Element(WINDOW), pl.Element(1024)),
                                    lambda n: (jnp.maximum(blk(n) - 1, 0) * WINDOW, off))
    return [cur(QA), cur(VA), prev(QA), prev(VA),
            _const((1, 768)), _const((1, 256)), _const((1, LANE)), _const((768, 768)),
            _const((A_HEADS // A_GROUP, 2 * WINDOW, A_GROUP * WINDOW))]


def _swa_fwd(proj, qg, kg, sinks, bd, bias):
    s = proj.shape[0]
    nb = s // WINDOW

    def body(qk_ref, vz_ref, qkp_ref, vzp_ref, qg_ref, kg_ref, sk_ref, bd_ref, bias_ref, o_ref):
        first = pl.program_id(0) == 0
        o_ref[...] = _swa_fn(qk_ref[...], vz_ref[...], qkp_ref[...], vzp_ref[...], qg_ref[...], kg_ref[...],
                             sk_ref[...], bd_ref[...], bias_ref[...], first).astype(BF16)

    return pl.pallas_call(
        body, grid=(nb,), in_specs=_swa_specs(nb), out_specs=_rowblk(WINDOW, 768),
        out_shape=SDS((s, 768), BF16), name="swa_fwd",
        compiler_params=_params(("parallel",)))(proj, proj, proj, proj, qg, kg, sinks, bd, bias)


def _swa_bwd(proj, qg, kg, sinks, bd, bias, dga):
    s = proj.shape[0]
    nb = s // WINDOW
    blk = lambda n: nb - 1 - n

    def body(qk_ref, vz_ref, qkp_ref, vzp_ref, qg_ref, kg_ref, sk_ref, bd_ref, bias_ref, dg_ref,
             d_ref, dqg_ref, dkg_ref, dsk_ref, carry):
        first = blk(pl.program_id(0)) == 0
        bd_v = bd_ref[...]
        bias_v = bias_ref[...]
        fn = lambda qk, vz, qkp, vzp, qg_, kg_, sk: _swa_fn(qk, vz, qkp, vzp, qg_, kg_, sk, bd_v, bias_v, first)
        _, vjp = jax.vjp(fn, qk_ref[...], vz_ref[...], qkp_ref[...], vzp_ref[...], qg_ref[...], kg_ref[...], sk_ref[...])
        dqk, dvz, dqkp, dvzp, dqg, dkg, dsk = vjp(dg_ref[...])

        @pl.when(pl.program_id(0) == 0)
        def _():
            dqg_ref[...] = jnp.zeros_like(dqg_ref)
            dkg_ref[...] = jnp.zeros_like(dkg_ref)
            dsk_ref[...] = jnp.zeros_like(dsk_ref)
            carry[...] = jnp.zeros_like(carry)

        dqg_ref[...] += dqg
        dkg_ref[...] += dkg
        dsk_ref[...] += dsk
        kv = jnp.concatenate([dqk[:, 768:], dvz[:, :256]], axis=1) + carry[...]
        d_ref[...] = jnp.concatenate([dqk[:, :768], kv, dvz[:, 256:]], axis=1).astype(BF16)
        carry[...] = jnp.concatenate([dqkp[:, 768:], dvzp[:, :256]], axis=1)

    rev = lambda w: pl.BlockSpec((WINDOW, w), lambda n: (blk(n), 0))
    return pl.pallas_call(
        body, grid=(nb,), in_specs=_swa_specs(nb, blk) + [rev(768)],
        out_specs=[rev(2048), _const((1, 768)), _const((1, 256)), _const((1, LANE))],
        out_shape=[SDS((s, 2048), BF16), SDS((1, 768), F32), SDS((1, 256), F32), SDS((1, LANE), F32)],
        scratch_shapes=[pltpu.VMEM((WINDOW, 512), F32)],
        name="swa_bwd", compiler_params=_params(("arbitrary",)))(proj, proj, proj, proj, qg, kg, sinks, bd, bias, dga)


def _mem_fwd(proj, mkv, qg, kg, bd, *, tr):
    s = proj.shape[0]

    def body(qz_ref, mkv_ref, qg_ref, kg_ref, bd_ref, o_ref):
        o_ref[...] = _mem_fn(qz_ref[...], mkv_ref[...], qg_ref[...], kg_ref[...], bd_ref[...]).astype(BF16)

    return pl.pallas_call(
        body, grid=(s // tr,),
        in_specs=[_win(tr, 1024, QC), _const(mkv.shape), _const((1, 512)), _const((1, 512)), _const((512, 512))],
        out_specs=_rowblk(tr, 512), out_shape=SDS((s, 512), BF16), name="mem_fwd",
        compiler_params=_params(("parallel",)))(proj, mkv, qg, kg, bd)


def _mem_bwd(proj, mkv, qg, kg, bd, dgc, *, tr):
    s = proj.shape[0]

    def body(qz_ref, mkv_ref, qg_ref, kg_ref, bd_ref, dg_ref, dqz_ref, dmkv_ref, dqg_ref, dkg_ref):
        bd_v = bd_ref[...]
        fn = lambda qz, mkv_, qg_, kg_: _mem_fn(qz, mkv_, qg_, kg_, bd_v)
        _, vjp = jax.vjp(fn, qz_ref[...], mkv_ref[...], qg_ref[...], kg_ref[...])
        dqz, dmkv, dqg, dkg = vjp(dg_ref[...])

        @pl.when(pl.program_id(0) == 0)
        def _():
            dmkv_ref[...] = jnp.zeros_like(dmkv_ref)
            dqg_ref[...] = jnp.zeros_like(dqg_ref)
            dkg_ref[...] = jnp.zeros_like(dkg_ref)

        dmkv_ref[...] += dmkv
        dqg_ref[...] += dqg
        dkg_ref[...] += dkg
        dqz_ref[...] = dqz.astype(BF16)

    return pl.pallas_call(
        body, grid=(s // tr,),
        in_specs=[_win(tr, 1024, QC), _const(mkv.shape), _const((1, 512)), _const((1, 512)), _const((512, 512)),
                  _rowblk(tr, 512)],
        out_specs=[_rowblk(tr, 1024), _const(mkv.shape), _const((1, 512)), _const((1, 512))],
        out_shape=[SDS((s, 1024), BF16), SDS(mkv.shape, F32), SDS((1, 512), F32), SDS((1, 512), F32)],
        name="mem_bwd", compiler_params=_params(("arbitrary",)))(proj, mkv, qg, kg, bd, dgc)


def _log_sigmoid(x):
    return jnp.minimum(x, 0.0) - jnp.log1p(jnp.exp(-jnp.abs(x)))


FOX_TQ, FOX_TK = 512, 512
FOX_FWD_TQ, FOX_FWD_TK = 512, 1024


def _fox_tiles(s):
    return min(FOX_TQ, s), min(FOX_TK, s)


AUG = 128 * B_HEADS
COL_A, COL_B = 64, 67


def _split3(c):
    hi = c.astype(BF16)
    r1 = c - hi.astype(F32)
    mid = r1.astype(BF16)
    lo = (r1 - mid.astype(F32)).astype(BF16)
    return hi, mid, lo


def _expand_mats():
    def mat(col0):
        e = np.zeros((768 + 3 * LANE, AUG), np.float32)
        for h in range(B_HEADS):
            for d in range(HEAD_DIM):
                e[64 * h + d, 128 * h + d] = 1.0
            for part in range(3):
                e[768 + LANE * part + h, 128 * h + col0 + part] = 1.0
        return e

    def ones(col0):
        o = np.zeros((1, AUG), np.float32)
        for h in range(B_HEADS):
            o[0, 128 * h + col0:128 * h + col0 + 3] = 1.0
        return o

    return (jnp.asarray(mat(COL_A), BF16), jnp.asarray(mat(COL_B), BF16), jnp.asarray(ones(COL_A)), jnp.asarray(ones(COL_B)))


def _augment(data_bf16, triple, emat, ones_row):
    parts = [data_bf16] + (list(triple) if triple is not None else [jnp.zeros((data_bf16.shape[0], LANE), BF16)] * 3)
    wide = _dot(jnp.concatenate(parts, axis=1), emat)
    if ones_row is not None:
        wide = wide + ones_row
    return wide


def _compact(wide):
    return jnp.concatenate([wide[:, 128 * h:128 * h + 64] for h in range(wide.shape[1] // 128)], axis=1)


def _lane_of_heads(wide, col, first=0):
    rows = wide.shape[0]
    lane = lax.broadcasted_iota(jnp.int32, (rows, LANE), 1)
    out = jnp.zeros((rows, LANE), F32)
    for h in range(wide.shape[1] // 128):
        out = jnp.where(lane == first + h, wide[:, 128 * h + col:128 * h + col + 1], out)
    return out


def _fox2_prep(proj, fbl, qg, kg, bfor, bd, ea, eb, ones_a, ones_b, *, tr):
    s = proj.shape[0]
    tri = jnp.asarray(np.tril(np.ones((tr, tr), np.float32)))

    def body(q_ref, k_ref, v_ref, fb_ref, qg_ref, kg_ref, bf_ref, bd_ref, tri_ref, ea_ref, eb_ref, oa_ref, ob_ref,
             qat_ref, ka_ref, kat_ref, va_ref, vat_ref, qn_ref, c_ref, carry):
        @pl.when(pl.program_id(0) == 0)
        def _():
            carry[...] = jnp.zeros_like(carry)

        bd_v = bd_ref[...]
        lane = lax.broadcasted_iota(jnp.int32, (tr, LANE), 1)
        logf = jnp.where(lane < N_FORGET, _log_sigmoid(fb_ref[...] + bf_ref[...]), 0.0)
        c = jnp.dot(tri_ref[...], logf, precision=HI, preferred_element_type=F32) + carry[...]
        c_ref[...] = c
        carry[...] = c[tr - 1:tr, :]
        qn = _qn_fn(q_ref[...], qg_ref[...], bd_v).astype(BF16)
        kn = _kn_fn(k_ref[...], kg_ref[...], bd_v).astype(BF16)
        qn_ref[...] = qn
        qat_ref[...] = jnp.transpose(_augment(qn, _split3(c), ea_ref[...], ob_ref[...])).astype(BF16)
        ka = _augment(kn, _split3(-c), eb_ref[...], oa_ref[...])
        ka_ref[...] = ka.astype(BF16)
        kat_ref[...] = jnp.transpose(ka).astype(BF16)
        va = _augment(v_ref[...].astype(BF16), None, ea_ref[...], oa_ref[...])
        va_ref[...] = va.astype(BF16)
        vat_ref[...] = jnp.transpose(va).astype(BF16)

    emat = _const((768 + 3 * LANE, AUG))
    return pl.pallas_call(
        body, grid=(s // tr,),
        in_specs=[_win(tr, 768, QB), _win(tr, 768, KB), _win(tr, 768, VB), _rowblk(tr, LANE), _const((1, 768)),
                  _const((1, 768)), _const((1, LANE)), _const((768, 768)), _const((tr, tr)), emat, emat,
                  _const((1, AUG)), _const((1, AUG))],
        out_specs=[pl.BlockSpec((AUG, tr), lambda i: (0, i)), _rowblk(tr, AUG), pl.BlockSpec((AUG, tr), lambda i: (0, i)),
                   _rowblk(tr, AUG), pl.BlockSpec((AUG, tr), lambda i: (0, i)), _rowblk(tr, 768), _rowblk(tr, LANE)],
        out_shape=[SDS((AUG, s), BF16), SDS((s, AUG), BF16), SDS((AUG, s), BF16), SDS((s, AUG), BF16),
                   SDS((AUG, s), BF16), SDS((s, 768), BF16), SDS((s, LANE), F32)],
        scratch_shapes=[pltpu.VMEM((1, LANE), F32)], name="fox_prep",
        compiler_params=_params(("arbitrary",)))(proj, proj, proj, fbl, qg, kg, bfor, bd, tri, ea, eb, ones_a, ones_b)


def _fox2_fwd(proj, qat, ka, vat):
    s = proj.shape[0]
    tq, tk = min(FOX_FWD_TQ, s), min(FOX_FWD_TK, s)
    nq, nk = s // tq, s // tk

    def last_k(i):
        return (i * tq + tq - 1) // tk

    def body(qt_ref, k_ref, vt_ref, z_ref, gb_ref, yb_ref, lse_ref, acc, m_s):
        i, j = pl.program_id(0), pl.program_id(1)

        @pl.when(j == 0)
        def _():
            acc[...] = jnp.zeros_like(acc)
            m_s[...] = jnp.full_like(m_s, NEG)

        def tile(masked):
            if masked:
                kpos = j * tk + lax.broadcasted_iota(jnp.int32, (tk, tq), 0)
                qpos = i * tq + lax.broadcasted_iota(jnp.int32, (tk, tq), 1)
                mask = kpos <= qpos
            for h in range(B_HEADS):
                sl = slice(128 * h, 128 * h + 128)
                sc = _dot(k_ref[:, sl], qt_ref[sl, :])
                if masked:
                    sc = jnp.where(mask, sc, NEG)
                m_prev = m_s[h:h + 1, :]
                m_new = jnp.maximum(m_prev, jnp.max(sc, axis=0, keepdims=True))
                p = jnp.exp(sc - m_new).astype(BF16)
                acc[sl, :] = jnp.exp(m_prev - m_new) * acc[sl, :] + _dot(vt_ref[sl, :], p)
                m_s[h:h + 1, :] = m_new

        full = j * tk + tk - 1 <= i * tq

        @pl.when(full)
        def _():
            tile(False)

        @pl.when(jnp.logical_and(jnp.logical_not(full), j <= last_k(i)))
        def _():
            tile(True)

        @pl.when(j == nk - 1)
        def _():
            outs = []
            row = lax.broadcasted_iota(jnp.int32, (LANE, tq), 0)
            lse_t = jnp.zeros((LANE, tq), F32)
            for h in range(B_HEADS):
                l_row = acc[128 * h + COL_A:128 * h + COL_A + 1, :]
                outs.append(acc[128 * h:128 * h + 64, :] * (1.0 / l_row))
                lse_t = jnp.where(row == h, m_s[h:h + 1, :] + jnp.log(l_row), lse_t)
            y = jnp.transpose(jnp.concatenate(outs, axis=0))
            yb_ref[...] = y
            gb_ref[...] = (y * _silu(z_ref[...])).astype(BF16)
            lse_ref[...] = jnp.transpose(lse_t)

    kcol = lambda i, j: (0, jnp.minimum(j, last_k(i)))
    return pl.pallas_call(
        body, grid=(nq, nk),
        in_specs=[pl.BlockSpec((AUG, tq), lambda i, j: (0, i)),
                  pl.BlockSpec((tk, AUG), lambda i, j: (jnp.minimum(j, last_k(i)), 0)),
                  pl.BlockSpec((AUG, tk), kcol),
                  pl.BlockSpec((pl.Element(tq), pl.Element(768)), lambda i, j: (i * tq, ZB))],
        out_specs=[pl.BlockSpec((tq, 768), lambda i, j: (i, 0)), pl.BlockSpec((tq, 768), lambda i, j: (i, 0)),
                   pl.BlockSpec((tq, LANE), lambda i, j: (i, 0))],
        out_shape=[SDS((s, 768), BF16), SDS((s, 768), F32), SDS((s, LANE), F32)],
        scratch_shapes=[pltpu.VMEM((AUG, tq), F32), pltpu.VMEM((16, tq), F32)],
        name="fox_fwd", compiler_params=_params(("parallel", "arbitrary")))(qat, ka, vat, proj)


def _fox2_bwd_pre(proj, yb, dgb, qn, c, lse, hsum, ea, ones_b, *, tr):
    s = proj.shape[0]

    def body(z_ref, y_ref, dg_ref, qn_ref, c_ref, lse_ref, hs_ref, ea_ref, ob_ref,
             qa_ref, qat_ref, dya_ref, dyat_ref, dz_ref):
        z, y, dg = z_ref[...], y_ref[...], dg_ref[...]
        sg = jax.nn.sigmoid(z)
        dy = dg * (z * sg)
        dz_ref[...] = (dg * y * (sg * (1.0 + z * (1.0 - sg)))).astype(BF16)
        delta = jnp.dot(dy * y, hs_ref[...], precision=HI, preferred_element_type=F32)
        e = ea_ref[...]
        dya = _augment(dy.astype(BF16), _split3(-delta), e, None)
        dya_ref[...] = dya.astype(BF16)
        dyat_ref[...] = jnp.transpose(dya).astype(BF16)
        qa = _augment(qn_ref[...], _split3(c_ref[...] - lse_ref[...]), e, ob_ref[...])
        qa_ref[...] = qa.astype(BF16)
        qat_ref[...] = jnp.transpose(qa).astype(BF16)

    return pl.pallas_call(
        body, grid=(s // tr,),
        in_specs=[_win(tr, 768, ZB), _rowblk(tr, 768), _rowblk(tr, 768), _rowblk(tr, 768), _rowblk(tr, LANE),
                  _rowblk(tr, LANE), _const((768, LANE)), _const((768 + 3 * LANE, AUG)), _const((1, AUG))],
        out_specs=[_rowblk(tr, AUG), pl.BlockSpec((AUG, tr), lambda i: (0, i)), _rowblk(tr, AUG),
                   pl.BlockSpec((AUG, tr), lambda i: (0, i)), _rowblk(tr, 768)],
        out_shape=[SDS((s, AUG), BF16), SDS((AUG, s), BF16), SDS((s, AUG), BF16), SDS((AUG, s), BF16),
                   SDS((s, 768), BF16)], name="fox_bwd_pre",
        compiler_params=_params(("parallel",)))(proj, yb, dgb, qn, c, lse, hsum, ea, ones_b)


def _fox2_bwd(qb, qbt, ka, kat, va, dya, dyat):
    s = qb.shape[0]
    tq, tk = _fox_tiles(s)
    nq, nk = s // tq, s // tk
    ng = 2
    gh = B_HEADS // ng
    gw = 128 * gh

    def first_q(j):
        return (j * tk) // tq

    def body(q_ref, qt_ref, k_ref, kt_ref, v_ref, dy_ref, dyt_ref, dq_hbm, dk_ref, dv_ref, dck_ref,
             dq_acc, dk_acc, dv_acc, sem):
        g, j, i = pl.program_id(0), pl.program_id(1), pl.program_id(2)

        @pl.when((j == 0) & (i == 0))
        def _():
            dq_acc[...] = jnp.zeros_like(dq_acc)

        @pl.when(i == 0)
        def _():
            dk_acc[...] = jnp.zeros_like(dk_acc)
            dv_acc[...] = jnp.zeros_like(dv_acc)

        def tile(masked):
            if masked:
                kpos = j * tk + lax.broadcasted_iota(jnp.int32, (tk, tq), 0)
                qpos = i * tq + lax.broadcasted_iota(jnp.int32, (tk, tq), 1)
                mask = kpos <= qpos
            cols = pl.ds(pl.multiple_of(i * tq, tq), tq)
            for h in range(gh):
                sl = slice(128 * h, 128 * h + 128)
                sc = _dot(k_ref[:, sl], qt_ref[sl, :])
                if masked:
                    sc = jnp.where(mask, sc, NEG)
                p = jnp.exp(sc)
                ds = (p * _dot(v_ref[:, sl], dyt_ref[sl, :])).astype(BF16)
                dv_acc[:, sl] += _dot(p.astype(BF16), dy_ref[:, sl])
                dk_acc[:, sl] += _dot(ds, q_ref[:, sl])
                dq_acc[sl, cols] += _dot(kt_ref[sl, :], ds)

        full = j * tk + tk - 1 <= i * tq

        @pl.when(full)
        def _():
            tile(False)

        @pl.when(jnp.logical_and(jnp.logical_not(full), i >= first_q(j)))
        def _():
            tile(True)

        @pl.when(i == nq - 1)
        def _():
            dkw = dk_acc[...]
            dk_ref[...] = _compact(dkw)
            dv_ref[...] = _compact(dv_acc[...]).astype(BF16)
            dck_ref[...] = -_lane_of_heads(dkw, COL_B, gh * g)

        @pl.when((j == nk - 1) & (i == nq - 1))
        def _():
            cp = pltpu.make_async_copy(dq_acc, dq_hbm.at[pl.ds(pl.multiple_of(g * gw, gw), gw)], sem)
            cp.start()
            cp.wait()

    qrow = pl.BlockSpec((tq, gw), lambda g, j, i: (jnp.maximum(i, first_q(j)), g))
    qcol = pl.BlockSpec((gw, tq), lambda g, j, i: (g, jnp.maximum(i, first_q(j))))
    krow = pl.BlockSpec((tk, gw), lambda g, j, i: (j, g))
    kcol = pl.BlockSpec((gw, tk), lambda g, j, i: (g, j))
    kout = pl.BlockSpec((tk, gw // 2), lambda g, j, i: (j, g))
    return pl.pallas_call(
        body, grid=(ng, nk, nq),
        in_specs=[qrow, qcol, krow, kcol, krow, qrow, qcol],
        out_specs=[pl.BlockSpec(memory_space=pl.ANY), kout, kout,
                   pl.BlockSpec((None, tk, LANE), lambda g, j, i: (g, j, 0))],
        out_shape=[SDS((AUG, s), F32), SDS((s, 768), F32), SDS((s, 768), BF16), SDS((ng, s, LANE), F32)],
        scratch_shapes=[pltpu.VMEM((gw, s), F32), pltpu.VMEM((tk, gw), F32), pltpu.VMEM((tk, gw), F32),
                        pltpu.SemaphoreType.DMA],
        name="fox_bwd", compiler_params=_params(("arbitrary",) * 3))(qb, qbt, ka, kat, va, dya, dyat)


def _fox2_bwd_post(proj, fbl, qg, kg, bfor, bd, dqa, dkn, dck, *, tr):
    s = proj.shape[0]
    nb = s // tr
    triu = jnp.asarray(np.triu(np.ones((tr, tr), np.float32)))
    rev = lambda i: nb - 1 - i

    def body(q_ref, k_ref, fb_ref, qg_ref, kg_ref, bf_ref, bd_ref, tri_ref, dqa_ref, dkn_ref, dck_ref,
             dq_ref, dk_ref, dfb_ref, dqg_ref, dkg_ref, dbf_ref, carry):
        @pl.when(pl.program_id(0) == 0)
        def _():
            carry[...] = jnp.zeros_like(carry)
            dqg_ref[...] = jnp.zeros_like(dqg_ref)
            dkg_ref[...] = jnp.zeros_like(dkg_ref)
            dbf_ref[...] = jnp.zeros_like(dbf_ref)

        bd_v = bd_ref[...]
        dqw = jnp.transpose(dqa_ref[...])
        _, vjp_q = jax.vjp(lambda q, g: _qn_fn(q, g, bd_v), q_ref[...], qg_ref[...])
        dq, dqg = vjp_q(_compact(dqw))
        _, vjp_k = jax.vjp(lambda k, g: _kn_fn(k, g, bd_v), k_ref[...], kg_ref[...])
        dk, dkg = vjp_k(dkn_ref[...])
        dq_ref[...] = dq.astype(BF16)
        dk_ref[...] = dk.astype(BF16)
        dqg_ref[...] += dqg
        dkg_ref[...] += dkg

        dc = _lane_of_heads(dqw, COL_A) + (dck_ref[0] + dck_ref[1])
        dlogf = jnp.dot(tri_ref[...], dc, precision=HI, preferred_element_type=F32) + carry[...]
        carry[...] = dlogf[0:1, :]
        lane = lax.broadcasted_iota(jnp.int32, (tr, LANE), 1)
        xf = fb_ref[...] + bf_ref[...]
        dfb = jnp.where(lane < N_FORGET, dlogf * jax.nn.sigmoid(-xf), 0.0)
        dfb_ref[...] = dfb.astype(BF16)
        dbf_ref[...] += jnp.sum(dfb, axis=0, keepdims=True)

    rb = lambda w: pl.BlockSpec((tr, w), lambda i: (rev(i), 0))
    wn = lambda w, off: pl.BlockSpec((pl.Element(tr), pl.Element(w)), lambda i: (rev(i) * tr, off))
    return pl.pallas_call(
        body, grid=(nb,),
        in_specs=[wn(768, QB), wn(768, KB), rb(LANE), _const((1, 768)), _const((1, 768)), _const((1, LANE)),
                  _const((768, 768)), _const((tr, tr)), pl.BlockSpec((AUG, tr), lambda i: (0, rev(i))), rb(768),
                  pl.BlockSpec((2, tr, LANE), lambda i: (0, rev(i), 0))],
        out_specs=[rb(768), rb(768), rb(LANE), _const((1, 768)), _const((1, 768)), _const((1, LANE))],
        out_shape=[SDS((s, 768), BF16), SDS((s, 768), BF16), SDS((s, LANE), BF16), SDS((1, 768), F32),
                   SDS((1, 768), F32), SDS((1, LANE), F32)],
        scratch_shapes=[pltpu.VMEM((1, LANE), F32)], name="fox_bwd_post",
        compiler_params=_params(("arbitrary",)))(proj, proj, fbl, qg, kg, bfor, bd, triu, dqa, dkn, dck)


def _merge_specs(tr):
    row = lambda w: pl.BlockSpec((tr, w), lambda i, j: (i, 0))
    shard = lambda r: pl.BlockSpec((None, r, 512), lambda i, j: (j, 0, 0))
    gate = lambda b: pl.BlockSpec((tr, 512), lambda i, j: (i, (GATE + 2048 * b) // 512 + j))
    return [row(768), row(768), row(512), shard(768), shard(768), shard(512), gate(0), gate(1), gate(2)]


def _merge_fwd(proj, ga, gb, gc, wa, wb, wc, *, tr):
    s = proj.shape[0]

    def body(ga_ref, gb_ref, gc_ref, wa_ref, wb_ref, wc_ref, l0_ref, l1_ref, l2_ref, y_ref):
        ua = _dot(ga_ref[...], wa_ref[...])
        ub = _dot(gb_ref[...], wb_ref[...])
        uc = _dot(gc_ref[...], wc_ref[...])
        y = jax.nn.sigmoid(l0_ref[...]) * ua + jax.nn.sigmoid(l1_ref[...]) * ub + jax.nn.sigmoid(l2_ref[...]) * uc
        y_ref[...] = y.astype(BF16)

    return pl.pallas_call(
        body, grid=(s // tr, N_CHIPS), in_specs=_merge_specs(tr),
        out_specs=pl.BlockSpec((tr, 512), lambda i, j: (i, j)), out_shape=SDS((s, D_MODEL), BF16), name="merge_fwd",
        compiler_params=_params(("parallel", "arbitrary")))(ga, gb, gc, wa, wb, wc, proj, proj, proj)


def _merge_bwd(proj, ga, gb, gc, wa, wb, wc, dy, *, tr):
    s = proj.shape[0]

    def body(ga_ref, gb_ref, gc_ref, wa_ref, wb_ref, wc_ref, l0_ref, l1_ref, l2_ref, dy_ref,
             dl0_ref, dl1_ref, dl2_ref, dua_ref, dub_ref, duc_ref, dga_ref, dgb_ref, dgc_ref):
        j = pl.program_id(1)
        dyv = dy_ref[...]

        @pl.when(j == 0)
        def _():
            dga_ref[...] = jnp.zeros_like(dga_ref)
            dgb_ref[...] = jnp.zeros_like(dgb_ref)
            dgc_ref[...] = jnp.zeros_like(dgc_ref)

        for g_ref, w_ref, l_ref, dl_ref, du_ref, dg_ref in (
                (ga_ref, wa_ref, l0_ref, dl0_ref, dua_ref, dga_ref),
                (gb_ref, wb_ref, l1_ref, dl1_ref, dub_ref, dgb_ref),
                (gc_ref, wc_ref, l2_ref, dl2_ref, duc_ref, dgc_ref)):
            w = w_ref[...]
            u = _dot(g_ref[...], w)
            sg = jax.nn.sigmoid(l_ref[...])
            dl_ref[...] = (dyv * u * sg * (1.0 - sg)).astype(BF16)
            du = (dyv * sg).astype(BF16)
            du_ref[...] = du
            dg_ref[...] += _dot_nt(du, w)

    blk = pl.BlockSpec((tr, 512), lambda i, j: (i, j))
    row = lambda w: pl.BlockSpec((tr, w), lambda i, j: (i, 0))
    big = SDS((s, D_MODEL), BF16)
    return pl.pallas_call(
        body, grid=(s // tr, N_CHIPS), in_specs=_merge_specs(tr) + [blk],
        out_specs=[blk] * 6 + [row(768), row(768), row(512)],
        out_shape=[big] * 6 + [SDS((s, 768), F32), SDS((s, 768), F32), SDS((s, 512), F32)], name="merge_bwd",
        compiler_params=_params(("parallel", "arbitrary")))(ga, gb, gc, wa, wb, wc, proj, proj, proj, dy)


def _out_loss(y, wo, x, tgt, *, tr, tn):
    s = x.shape[0]

    def body(y_ref, w_ref, x_ref, t_ref, d_ref, db_ref, sq_ref, dy_ref):
        @pl.when((pl.program_id(0) == 0) & (pl.program_id(1) == 0))
        def _():
            sq_ref[...] = jnp.zeros_like(sq_ref)

        @pl.when(pl.program_id(1) == 0)
        def _():
            dy_ref[...] = jnp.zeros_like(dy_ref)

        w = w_ref[...]
        out = x_ref[...] + _dot(y_ref[...], w)
        diff = out - t_ref[...]
        sq_ref[...] += jnp.sum(diff * diff, axis=0, keepdims=True)
        d = diff * (1.0 / D_MODEL)
        d_ref[...] = d
        db = d.astype(BF16)
        db_ref[...] = db
        dy_ref[...] += _dot_nt(db, w)

    blk = pl.BlockSpec((tr, tn), lambda i, j: (i, j))
    row = pl.BlockSpec((tr, D_MODEL), lambda i, j: (i, 0))
    return pl.pallas_call(
        body, grid=(s // tr, D_MODEL // tn),
        in_specs=[row, pl.BlockSpec((D_MODEL, tn), lambda i, j: (0, j)), blk, blk],
        out_specs=[blk, blk, _const((1, tn)), row],
        out_shape=[SDS((s, D_MODEL), F32), SDS((s, D_MODEL), BF16), SDS((1, tn), F32), SDS((s, D_MODEL), F32)],
        name="out_loss", compiler_params=_params(("arbitrary", "arbitrary")))(y, wo, x, tgt)


def _tile_gain(g, reps):
    return jnp.tile(g.reshape(1, -1), (1, reps))


def _pad_lane(v):
    v = v.reshape(1, -1)
    return jnp.pad(v, ((0, 0), (0, LANE - v.shape[1])))


def _local_step(x, mem, tgt, w_main, w_fb, w_small, norm_gain, mem_norm_gain, b_forget,
                q_gain_a, k_gain_a, sinks_a, q_gain_b, k_gain_b, q_gain_c, k_gain_c, core=None):
    s = x.shape[0]
    tr = min(512, s)
    bd64 = _block_diag(768, HEAD_DIM)
    bd128 = _block_diag(512, C_HEAD_DIM)
    hsum = _head_sum(768, HEAD_DIM)
    qga, kga = _tile_gain(q_gain_a, 12), _tile_gain(k_gain_a, 4)
    qgb, kgb = _tile_gain(q_gain_b, 12), _tile_gain(k_gain_b, 12)
    qgc, kgc = _tile_gain(q_gain_c, 4), _tile_gain(k_gain_c, 4)
    sinks = _pad_lane(sinks_a)
    bfor = _pad_lane(b_forget)

    hn = _rms_fwd(x, norm_gain, tr=tr, name="rms_x")
    on_mesh = core is not None
    if on_mesh:
        proj, (gathered,) = _matmul(hn, w_main, dims="nn", out_dtype=F32, tm=1024, tn=1024, tk=D_MODEL, name="proj_main",
                                    comms=[_gather_comm(list(w_small))])
        w_mk, wa, wb, wc, wo = gathered
        w_mk, wo = w_mk.reshape(D_MODEL, 1024), wo.reshape(D_MODEL, D_MODEL)
    else:
        proj = _matmul(hn, w_main, dims="nn", out_dtype=F32, tm=1024, tn=1024, tk=D_MODEL, name="proj_main")
        w_mk, wa, wb, wc, wo = w_small
    fbl = _matmul(hn, w_fb, dims="nn", out_dtype=F32, tm=1024, tn=LANE, tk=D_MODEL, name="proj_forget")
    memn = _rms_fwd(mem, mem_norm_gain, tr=mem.shape[0], name="rms_mem")
    mkv = _matmul(memn, w_mk, dims="nn", out_dtype=F32, tm=256, tn=512, tk=D_MODEL, name="mem_kv")

    swa_bias = _swa_bias()
    ga = _swa_fwd(proj, qga, kga, sinks, bd64, swa_bias)
    ea, eb, ones_a, ones_b = _expand_mats()
    tf = min(256, s)
    qat, ka, kat, va, vat, qn, cfox = _fox2_prep(proj, fbl, qgb, kgb, bfor, bd64, ea, eb, ones_a, ones_b, tr=tf)
    gb, yb, lse = _fox2_fwd(proj, qat, ka, vat)
    gc = _mem_fwd(proj, mkv, qgc, kgc, bd128, tr=tr)
    y = _merge_fwd(proj, ga, gb, gc, wa, wb, wc, tr=tr)
    dout, dout_b, sq, dy = _out_loss(y, wo, x, tgt, tr=tr, tn=512)

    d_wo = _matmul(y, dout_b, dims="tn", out_dtype=F32, tm=1024, tn=512, tk=4096, name="dw_out")
    dl0, dl1, dl2, dua, dub, duc, dga, dgb, dgc = _merge_bwd(proj, ga, gb, gc, wa, wb, wc, dy, tr=tr)
    d_wa = _matmul(ga, dua, dims="tn", out_dtype=F32, tm=768, tn=512, tk=4096, name="dw_branch_a")
    d_wb = _matmul(gb, dub, dims="tn", out_dtype=F32, tm=768, tn=512, tk=4096, name="dw_branch_b")
    d_wc = _matmul(gc, duc, dims="tn", out_dtype=F32, tm=512, tn=512, tk=4096, name="dw_branch_c")

    dproj_a, d_qga, d_kga, d_sinks = _swa_bwd(proj, qga, kga, sinks, bd64, swa_bias, dga)

    qab, qabt, dya, dyat, dzb = _fox2_bwd_pre(proj, yb, dgb, qn, cfox, lse, hsum, ea, ones_b, tr=tf)
    dqa, dkn, dvb, dck = _fox2_bwd(qab, qabt, ka, kat, va, dya, dyat)
    dqb, dkb, dfb, d_qgb, d_kgb, d_bf = _fox2_bwd_post(proj, fbl, qgb, kgb, bfor, bd64, dqa, dkn, dck, tr=tf)

    dproj_c, dmkv, d_qgc, d_kgc = _mem_bwd(proj, mkv, qgc, kgc, bd128, dgc, tr=tr)
    dmkv_b = dmkv.astype(BF16)
    d_wmk = _matmul(memn, dmkv_b, dims="tn", out_dtype=F32, tm=1024, tn=512, tk=256, name="dw_mem_kv")
    dmemn = _matmul(dmkv_b, w_mk, dims="nt", out_dtype=F32, tm=256, tn=512, tk=1024, name="dmemn")
    (d_mem_gain,) = _rms_bwd(mem, mem_norm_gain, dmemn, None, tr=mem.shape[0], name="rms_mem_bwd")

    dproj = [dproj_a, jnp.concatenate([dqb, dkb, dvb, dzb, dproj_c], axis=1), dl0, dl1, dl2]
    dhn_f = _matmul(dfb, w_fb, dims="nt", out_dtype=F32, tm=1024, tn=512, tk=LANE, name="dhn_forget")
    d_wfb = _matmul(hn, dfb, dims="tn", out_dtype=F32, tm=1024, tn=LANE, tk=512, name="dw_forget")
    big = {}
    if on_mesh:
        g1, k1 = [d_wmk, d_wa, d_wb, d_wc, d_wo], [2, 3, 4, 5, 6]
        d_other, (got1,) = _matmul(hn, dproj, dims="tn", out_dtype=BF16, tm=1024, tn=512, tk=4096, a_half=(core, True),
                                   name="dw_main_other", comms=[_exchange_comm(g1, k1)])
        h1 = _add_half(g1, got1, core, [HALF_AXIS[k] for k in k1], name="add_half_small")
        d_own, (got0, parts1) = _matmul(
            hn, dproj, dims="tn", out_dtype=F32, tm=1024, tn=512, tk=4096, a_half=(core, False), name="dw_main_own",
            comms=[_exchange_comm([d_other, d_wfb], [0, 1], whole=(0,)), _scatter_comm(h1, k1)])
        h0 = [_add_pair(d_own, got0[0], name="add_pair_main")] + _add_half([d_wfb], [got0[1]], core, [0], name="add_half_forget")
        sums1 = _sum4(parts1, name="sum4_small")
        dhn, (parts0, theirs1) = _matmul(dproj, w_main, dims="nt", out_dtype=F32, tm=1024, tn=512, tk=2048, vmem=VMEM_WIDE, name="dhn",
                                         add=dhn_f, comms=[_scatter_comm(h0, [0, 1]), _swap_comm(sums1)])
        sums0 = _sum4(parts0, name="sum4_main")
        (grad_x, d_gain), theirs0 = _rms_bwd(x, norm_gain, dhn, dout, tr=tr, name="rms_x_bwd", comm=_swap_comm(sums0))
        big = dict(sums=sums0 + sums1, theirs=list(theirs0) + list(theirs1))
    else:
        dhn = _matmul(dproj, w_main, dims="nt", out_dtype=F32, tm=1024, tn=512, tk=2048, vmem=VMEM_WIDE, name="dhn", add=dhn_f)
        d_wmain = _matmul(hn, dproj, dims="tn", out_dtype=F32, tm=1024, tn=512, tk=4096, name="dw_main")
        big = dict(d_wmain=d_wmain, d_wfb=d_wfb, d_wmk=d_wmk, d_wa=d_wa, d_wb=d_wb, d_wc=d_wc, d_wo=d_wo)
        grad_x, d_gain = _rms_bwd(x, norm_gain, dhn, dout, tr=tr, name="rms_x_bwd")

    fold = lambda g, reps: jnp.sum(g.reshape(reps, -1), axis=0, keepdims=True)
    return dict(
        sq=sq, grad_x=grad_x, **big,
        d_gain=d_gain, d_mem_gain=d_mem_gain, d_bf=d_bf[:, :N_FORGET],
        d_qga=fold(d_qga, 12), d_kga=fold(d_kga, 4), d_sinks=d_sinks[:, :A_HEADS],
        d_qgb=fold(d_qgb, 12), d_kgb=fold(d_kgb, 12), d_qgc=fold(d_qgc, 4), d_kgc=fold(d_kgc, 4))


PACK_ROWS = 256
FORGET_IN_SHARD = FORGET_COL - SHARD_COLS
AFTER_FORGET = FORGET_COL - SLAB_START[1]
END_CHIP1 = 2 * SHARD_COLS - N_FORGET - SLAB_START[1]


def _pack_w_in(chip, w):
    rows = w.shape[1]
    tr = PACK_ROWS

    def body(k_ref, w_ref, o_ref, scr):
        scr[...] = jnp.zeros_like(scr)
        scr[pl.ds(0, SHARD_COLS), :] = w_ref[...]
        v = jnp.transpose(scr[...])
        k = k_ref[0]
        col = lax.broadcasted_iota(jnp.int32, (tr, SLAB), 1)
        no_forget = jnp.zeros((tr, LANE), BF16)

        @pl.when(k == 0)
        def _():
            o_ref[:, 0:SLAB] = v.astype(BF16)
            o_ref[:, SLAB:] = no_forget

        @pl.when(k == 1)
        def _():
            before = pltpu.roll(v, SLAB_SHIFT[1], axis=1)
            after = pltpu.roll(v, SLAB - (N_FORGET - SLAB_SHIFT[1]), axis=1)
            slab = jnp.where(col < AFTER_FORGET, before, jnp.where(col < END_CHIP1, after, 0.0))
            o_ref[:, 0:SLAB] = slab.astype(BF16)
            f = pltpu.roll(v, SLAB - FORGET_IN_SHARD, axis=1)[:, :LANE]
            o_ref[:, SLAB:] = jnp.where(col[:, :LANE] < N_FORGET, f, 0.0).astype(BF16)

        for kk in (2, 3):
            @pl.when(k == kk)
            def _(kk=kk):
                o_ref[:, 0:SLAB] = pltpu.roll(v, SLAB_SHIFT[kk], axis=1).astype(BF16)
                o_ref[:, SLAB:] = no_forget

    return pl.pallas_call(
        body, grid_spec=pltpu.PrefetchScalarGridSpec(
            num_scalar_prefetch=1, grid=(rows // tr,),
            in_specs=[pl.BlockSpec((SHARD_COLS, tr), lambda i, k: (0, i))],
            out_specs=pl.BlockSpec((None, tr, SLAB + LANE), lambda i, k: (k[0], i, 0)),
            scratch_shapes=[pltpu.VMEM((SLAB, tr), F32)]),
        out_shape=SDS((N_CHIPS, rows, SLAB + LANE), BF16), name="pack_w_in",
        compiler_params=_params(("arbitrary",)))(chip, w)


def _merge_slabs(g):
    rows = g.shape[1]
    tr = PACK_ROWS
    t = [s // LANE for s in SLAB_START]
    n_t = SLAB // LANE

    def body(g_ref, m_ref, f_ref):
        for k in range(N_CHIPS):
            lo = t[k] + (1 if k > 0 else 0)
            hi = t[k + 1] if k + 1 < N_CHIPS else t[k] + n_t
            m_ref[:, lo * LANE:hi * LANE] = g_ref[k, :, (lo - t[k]) * LANE:(hi - t[k]) * LANE]
            if k + 1 < N_CHIPS:
                a = g_ref[k, :, (hi - t[k]) * LANE:(hi - t[k] + 1) * LANE].astype(F32)
                b = g_ref[k + 1, :, 0:LANE].astype(F32)
                m_ref[:, hi * LANE:(hi + 1) * LANE] = (a + b).astype(BF16)
        f_ref[...] = g_ref[1, :, SLAB:]

    return pl.pallas_call(
        body, grid=(rows // tr,),
        in_specs=[pl.BlockSpec((N_CHIPS, tr, SLAB + LANE), lambda i: (0, i, 0))],
        out_specs=[_rowblk(tr, P_MAIN), _rowblk(tr, LANE)],
        out_shape=[SDS((rows, P_MAIN), BF16), SDS((rows, LANE), BF16)], name="merge_slabs",
        compiler_params=_params(("parallel",)))(g)


def _adamw_math(w, g, m, v):
    nm = ADAM_B1 * m + (1.0 - ADAM_B1) * g
    nv = ADAM_B2 * v + (1.0 - ADAM_B2) * (g * g)
    m_hat = nm / (1.0 - ADAM_B1 ** ADAM_STEP)
    v_hat = nv / (1.0 - ADAM_B2 ** ADAM_STEP)
    delta = -ADAM_LR * (m_hat / (jnp.sqrt(v_hat) + ADAM_EPS) + ADAM_WD * w)
    return delta, nm, nv


def _adamw(g, w, m, v, *, tr, name):
    rows, cols = w.shape
    tr = min(tr, rows)

    def body(g_ref, w_ref, m_ref, v_ref, d_ref, nm_ref, nv_ref):
        d, nm, nv = _adamw_math(w_ref[...], g_ref[...], m_ref[...], v_ref[...])
        d_ref[...] = d
        nm_ref[...] = nm
        nv_ref[...] = nv

    spec = _rowblk(tr, cols)
    return pl.pallas_call(
        body, grid=(rows // tr,), in_specs=[spec] * 4, out_specs=[spec] * 3,
        out_shape=[SDS((rows, cols), F32)] * 3, name=name, compiler_params=_params(("parallel",)))(g, w, m, v)


def _adamw_w_in(chip_core, slab_mine, slab_theirs, forget_mine, forget_theirs, w, m, v):
    rows = w.shape[1]
    tr = PACK_ROWS // 2
    nbh = rows // 2 // tr

    def body(k_ref, sa_ref, sb_ref, fa_ref, fb_ref, w_ref, m_ref, v_ref, g_ref, d_ref, nm_ref, nv_ref):
        use_mine = pl.program_id(0) // nbh == k_ref[1]
        sl = jnp.where(use_mine, sa_ref[...], sb_ref[...])
        f_tile = jnp.where(use_mine, fa_ref[...], fb_ref[...])
        k = k_ref[0]

        def emit(wide):
            g = jnp.transpose(wide)[:SHARD_COLS, :]
            g_ref[...] = g
            d, nm, nv = _adamw_math(w_ref[...], g, m_ref[...], v_ref[...])
            d_ref[...] = d
            nm_ref[...] = nm
            nv_ref[...] = nv

        @pl.when(k == 0)
        def _():
            emit(sl)

        @pl.when(k == 1)
        def _():
            col = lax.broadcasted_iota(jnp.int32, (tr, SLAB), 1)
            before = pltpu.roll(sl, SLAB - SLAB_SHIFT[1], axis=1)
            after = pltpu.roll(sl, N_FORGET - SLAB_SHIFT[1], axis=1)
            wide_f = jnp.concatenate([f_tile, jnp.zeros((tr, SLAB - LANE), F32)], axis=1)
            forget = pltpu.roll(wide_f, FORGET_IN_SHARD, axis=1)
            emit(jnp.where(col < FORGET_IN_SHARD, before, jnp.where(col < FORGET_IN_SHARD + N_FORGET, forget, after)))

        for kk in (2, 3):
            @pl.when(k == kk)
            def _(kk=kk):
                emit(pltpu.roll(sl, SLAB - SLAB_SHIFT[kk], axis=1))

    nat = pl.BlockSpec((SHARD_COLS, tr), lambda i, k: (0, i))
    half = lambda width: pl.BlockSpec((tr, width), lambda i, k: (i % nbh, 0))
    return pl.pallas_call(
        body, grid_spec=pltpu.PrefetchScalarGridSpec(
            num_scalar_prefetch=1, grid=(rows // tr,),
            in_specs=[half(SLAB), half(SLAB), half(LANE), half(LANE), nat, nat, nat],
            out_specs=[nat] * 4),
        out_shape=[SDS((SHARD_COLS, rows), F32)] * 4, name="adamw_w_in",
        compiler_params=_params(("arbitrary",)))(chip_core, slab_mine, slab_theirs, forget_mine, forget_theirs, w, m, v)


ANY = pl.BlockSpec(memory_space=pl.ANY)
HALF_AXIS = (0, 0, 1, 0, 0, 0, 1)


def _me():
    return lax.axis_index("x"), lax.axis_index("y"), lax.axis_index("c")


def _half(ref, which, axis):
    n = ref.shape[axis] // 2
    sl = pl.ds(which * n, n)
    return ref.at[sl] if axis == 0 else ref.at[:, sl]


def _piece(t, ref, j):
    if t == 0:
        return ref.at[:, pl.ds(SLAB_START[j], SLAB)]
    if t == 1:
        return ref
    if t in (2, 6):
        return ref.at[pl.ds(512 * j, 512)]
    return ref.at[:, pl.ds(512 * j, 512)]


def _piece_shape(t, shape):
    if t == 0:
        return (shape[0], SLAB)
    if t == 1:
        return shape
    if t in (2, 6):
        return (512, shape[1])
    return (shape[0], 512)


def _gather_plan(ins, outs, own_slot_in_src):
    x, y, c = _me()
    k = 2 * x + y
    sib = (x, y, 1 - c)
    chips = [(1 - x, y), (x, 1 - y), (1 - x, 1 - y)]
    n = len(outs)

    def rows(t, which):
        h = outs[t].shape[1] // 2
        return pl.ds(which * h, h)

    def mine(t):
        return ins[t].at[k, rows(t, c)] if own_slot_in_src else ins[t].at[rows(t, c)]

    def first(t, j, sems):
        chip = chips[j]
        return pltpu.make_async_remote_copy(
            src_ref=mine(t), dst_ref=outs[t].at[k, rows(t, c)], send_sem=sems[0].at[t, j], recv_sem=sems[1].at[t, j],
            device_id=(chip[0], chip[1], c), device_id_type=MESH)

    def landed(t, j, sems):
        chip = chips[j]
        return pltpu.make_async_remote_copy(
            src_ref=mine(t), dst_ref=outs[t].at[2 * chip[0] + chip[1], rows(t, c)], send_sem=sems[0].at[t, j],
            recv_sem=sems[1].at[t, j], device_id=(chip[0], chip[1], c), device_id_type=MESH)

    def passed(t, j, which, sems):
        chip = chips[j]
        blk = outs[t].at[2 * chip[0] + chip[1], rows(t, which)]
        return pltpu.make_async_remote_copy(
            src_ref=blk, dst_ref=blk, send_sem=sems[2].at[t, j], recv_sem=sems[3].at[t, j], device_id=sib,
            device_id_type=MESH)

    def start(sems):
        for j in range(3):
            for t in range(n):
                first(t, j, sems).start()

    def finish(sems):
        for j in range(3):
            for t in range(n):
                landed(t, j, sems).wait_recv()
                passed(t, j, c, sems).start()
        for j in range(3):
            for t in range(n):
                passed(t, j, 1 - c, sems).wait_recv()
        for j in range(3):
            for t in range(n):
                first(t, j, sems).wait_send()
                passed(t, j, c, sems).wait_send()

    return k, start, finish


def _all_gather_slabs(slabs):
    def body(in_ref, out_ref, nbr_sem, quarter_sem, pass_sem):
        x, y, c = _me()
        k = 2 * x + y
        rows = out_ref.shape[1]
        h, q = rows // 2, rows // 4
        nbrs = [(1 - x, y), (x, 1 - y)]
        slot = lambda chip: 2 * chip[0] + chip[1]
        diag = 2 * (1 - x) + (1 - y)
        half = pl.ds(c * h, h)
        quarter = lambda a: pl.ds(c * h + a * q, q)

        def first(a):
            return pltpu.make_async_remote_copy(
                src_ref=in_ref.at[k, half], dst_ref=out_ref.at[k, half], send_sem=nbr_sem.at[0, a],
                recv_sem=nbr_sem.at[1, a], device_id=(nbrs[a][0], nbrs[a][1], c), device_id_type=MESH)

        def landed(a):
            blk = out_ref.at[slot(nbrs[a]), half]
            return pltpu.make_async_remote_copy(
                src_ref=blk, dst_ref=blk, send_sem=nbr_sem.at[0, a], recv_sem=nbr_sem.at[1, a],
                device_id=(nbrs[a][0], nbrs[a][1], c), device_id_type=MESH)

        def relay(a):
            blk = out_ref.at[slot(nbrs[a]), quarter(a)]
            to = nbrs[1 - a]
            return pltpu.make_async_remote_copy(
                src_ref=blk, dst_ref=blk, send_sem=quarter_sem.at[0, a], recv_sem=quarter_sem.at[1, a],
                device_id=(to[0], to[1], c), device_id_type=MESH)

        def relayed(a):
            blk = out_ref.at[diag, quarter(a)]
            frm = nbrs[1 - a]
            return pltpu.make_async_remote_copy(
                src_ref=blk, dst_ref=blk, send_sem=quarter_sem.at[0, a], recv_sem=quarter_sem.at[1, a],
                device_id=(frm[0], frm[1], c), device_id_type=MESH)

        def passed(j, which):
            sl = diag if j == 2 else slot(nbrs[j])
            blk = out_ref.at[sl, pl.ds(which * h, h)]
            return pltpu.make_async_remote_copy(
                src_ref=blk, dst_ref=blk, send_sem=pass_sem.at[0, j], recv_sem=pass_sem.at[1, j],
                device_id=(x, y, 1 - c), device_id_type=MESH)

        for a in range(2):
            first(a).start()
        for a in range(2):
            landed(a).wait_recv()
            relay(a).start()
            passed(a, c).start()
        for a in range(2):
            relayed(a).wait_recv()
        passed(2, c).start()
        for j in range(3):
            passed(j, 1 - c).wait_recv()
        for a in range(2):
            first(a).wait_send()
            relay(a).wait_send()
        for j in range(3):
            passed(j, c).wait_send()

    return pl.pallas_call(
        body, in_specs=[ANY], out_specs=ANY, out_shape=SDS(slabs.shape, slabs.dtype),
        scratch_shapes=[pltpu.SemaphoreType.DMA((2, 2)), pltpu.SemaphoreType.DMA((2, 2)), pltpu.SemaphoreType.DMA((2, 3))],
        input_output_aliases={0: 0}, name="all_gather_slabs")(slabs)


def _gather_comm(parts):
    n = len(parts)

    def start(ins, outs, sems):
        k, go, _ = _gather_plan(ins, outs, False)
        for t in range(n):
            pltpu.make_async_copy(ins[t], outs[t].at[k], sems[4].at[t]).start()
        go(sems)

    def finish(ins, outs, sems):
        k, _, done = _gather_plan(ins, outs, False)
        done(sems)
        for t in range(n):
            pltpu.make_async_copy(ins[t], outs[t].at[k], sems[4].at[t]).wait()

    return _Comm(parts, [SDS((N_CHIPS,) + p.shape, p.dtype) for p in parts],
                 [pltpu.SemaphoreType.DMA((n, 3))] * 4 + [pltpu.SemaphoreType.DMA((n,))], start, finish)


def _exchange_comm(arrs, kinds, whole=()):
    n = len(arrs)

    def copies(ins, outs, sems):
        x, y, c = _me()
        return [pltpu.make_async_remote_copy(
            src_ref=ins[t] if t in whole else _half(ins[t], 1 - c, HALF_AXIS[kinds[t]]), dst_ref=outs[t],
            send_sem=sems[0].at[t], recv_sem=sems[1].at[t], device_id=(x, y, 1 - c), device_id_type=MESH)
            for t in range(n)]

    def start(ins, outs, sems):
        for cp in copies(ins, outs, sems):
            cp.start()

    def finish(ins, outs, sems):
        for cp in copies(ins, outs, sems):
            cp.wait()

    def hshape(t):
        s = list(arrs[t].shape)
        if t not in whole:
            s[HALF_AXIS[kinds[t]]] //= 2
        return SDS(tuple(s), arrs[t].dtype)

    return _Comm(arrs, [hshape(t) for t in range(n)], [pltpu.SemaphoreType.DMA((n,))] * 2, start, finish)


ADD_HALF_STEPS = 4


def _add_half(fulls, gots, core, axes, *, name):
    ns = ADD_HALF_STEPS
    n = len(fulls)
    in_specs, out_specs, out_shape = [], [], []
    for got, axis in zip(gots, axes):
        r, c = got.shape
        blk = (r // ns, c)
        if axis == 0:
            in_specs.append(pl.BlockSpec(blk, lambda i, cr: (i + cr[0] * ns, 0)))
        else:
            in_specs.append(pl.BlockSpec(blk, lambda i, cr: (i, cr[0])))
        out_specs.append(pl.BlockSpec(blk, lambda i, cr: (i, 0)))
        out_shape.append(SDS((r, c), BF16))
    in_specs += list(out_specs)

    def body(c_ref, *refs):
        for a_ref, b_ref, o_ref in zip(refs[:n], refs[n:2 * n], refs[2 * n:]):
            o_ref[...] = (a_ref[...] + b_ref[...]).astype(BF16)

    return list(pl.pallas_call(
        body, grid_spec=pltpu.PrefetchScalarGridSpec(num_scalar_prefetch=1, grid=(ns,), in_specs=in_specs,
                                                     out_specs=out_specs),
        out_shape=out_shape, name=name, compiler_params=_params(("parallel",)))(core, *fulls, *gots))


def _add_pair(a, b, *, name):
    r, c = a.shape
    br, bc = 256, min(2048, c)

    def body(a_ref, b_ref, o_ref):
        o_ref[...] = (a_ref[...] + b_ref[...].astype(F32)).astype(BF16)

    spec = pl.BlockSpec((br, bc), lambda i, j: (i, j))
    return pl.pallas_call(body, grid=(r // br, c // bc), in_specs=[spec, spec], out_specs=spec,
                          out_shape=SDS((r, c), BF16), name=name, compiler_params=_params(("parallel", "parallel")))(a, b)


def _scatter_comm(halves, kinds):
    n = len(halves)

    def plan(ins, outs, sems):
        send, recv, lsem = sems
        x, y, c = _me()
        k = 2 * x + y

        def to_chip(t, j):
            return pltpu.make_async_remote_copy(
                src_ref=_piece(kinds[t], ins[t], j), dst_ref=outs[t].at[k], send_sem=send.at[t, j],
                recv_sem=recv.at[t, k], device_id=(j // 2, j % 2, c), device_id_type=MESH)

        def from_chip(t, j):
            return pltpu.make_async_remote_copy(
                src_ref=_piece(kinds[t], ins[t], j), dst_ref=outs[t].at[j], send_sem=send.at[t, j],
                recv_sem=recv.at[t, j], device_id=(j // 2, j % 2, c), device_id_type=MESH)

        def own(t, j):
            return pltpu.make_async_copy(_piece(kinds[t], ins[t], j), outs[t].at[j], lsem.at[t])

        return k, to_chip, from_chip, own

    def start(ins, outs, sems):
        k, to_chip, _, own = plan(ins, outs, sems)
        for j in range(N_CHIPS):
            @pl.when(k != j)
            def _(j=j):
                for t in range(n):
                    to_chip(t, j).start()

            @pl.when(k == j)
            def _(j=j):
                for t in range(n):
                    own(t, j).start()

    def finish(ins, outs, sems):
        k, to_chip, from_chip, own = plan(ins, outs, sems)
        for j in range(N_CHIPS):
            @pl.when(k != j)
            def _(j=j):
                for t in range(n):
                    from_chip(t, j).wait_recv()
                for t in range(n):
                    to_chip(t, j).wait_send()

            @pl.when(k == j)
            def _(j=j):
                for t in range(n):
                    own(t, j).wait()

    return _Comm(halves, [SDS((N_CHIPS,) + _piece_shape(kinds[t], halves[t].shape), halves[t].dtype) for t in range(n)],
                 [pltpu.SemaphoreType.DMA((n, N_CHIPS))] * 2 + [pltpu.SemaphoreType.DMA((n,))], start, finish)


SUM4_STEPS = 4


def _sum4(ps, *, name):
    ns = SUM4_STEPS

    def body(*refs):
        for p_ref, o_ref in zip(refs[:len(ps)], refs[len(ps):]):
            o_ref[...] = ((p_ref[0].astype(F32) + p_ref[1].astype(F32)) + p_ref[2].astype(F32)) + p_ref[3].astype(F32)

    return pl.pallas_call(
        body, grid=(ns,),
        in_specs=[pl.BlockSpec((N_CHIPS, p.shape[1] // ns, p.shape[2]), lambda i: (0, i, 0)) for p in ps],
        out_specs=[_rowblk(p.shape[1] // ns, p.shape[2]) for p in ps],
        out_shape=[SDS(p.shape[1:], F32) for p in ps], name=name, compiler_params=_params(("parallel",)))(*ps)


def _swap_comm(sums):
    return _exchange_comm(sums, [None] * len(sums), whole=tuple(range(len(sums))))


ADAMW_STEPS = 8


def _adamw_halves(items, core):
    ns = ADAMW_STEPS
    nbh = ns // 2
    n = len(items)
    in_specs, out_specs, out_shape, ins = [], [], [], []
    for mine, theirs, w, m, v, axis in items:
        rows, cols = w.shape
        tr = rows // ns
        if axis == 0:
            g_spec = pl.BlockSpec((tr, cols), lambda i, cr: (i % nbh, 0))
        else:
            g_spec = pl.BlockSpec((tr, cols // 2), lambda i, cr: (i, 0))
        nat = pl.BlockSpec((tr, cols), lambda i, cr: (i, 0))
        in_specs += [g_spec, g_spec, nat, nat, nat]
        out_specs += [nat] * 4
        out_shape += [SDS((rows, cols), F32)] * 4
        ins += [mine, theirs, w, m, v]

    def body(c_ref, *refs):
        for t, item in enumerate(items):
            a_ref, b_ref, w_ref, m_ref, v_ref = refs[5 * t:5 * t + 5]
            g_ref, d_ref, nm_ref, nv_ref = refs[5 * n + 4 * t:5 * n + 4 * t + 4]
            a, b = a_ref[...], b_ref[...]
            if item[5] == 0:
                g = jnp.where(pl.program_id(0) // nbh == c_ref[0], a, b)
            else:
                low = c_ref[0] == 0
                g = jnp.concatenate([jnp.where(low, a, b), jnp.where(low, b, a)], axis=1)
            g_ref[...] = g
            d, nm, nv = _adamw_math(w_ref[...], g, m_ref[...], v_ref[...])
            d_ref[...] = d
            nm_ref[...] = nm
            nv_ref[...] = nv

    res = pl.pallas_call(
        body, grid_spec=pltpu.PrefetchScalarGridSpec(
            num_scalar_prefetch=1, grid=(ns,), in_specs=in_specs, out_specs=out_specs),
        out_shape=out_shape, name="adamw_shards", compiler_params=_params(("arbitrary",)))(core, *ins)
    return [tuple(res[4 * t:4 * t + 4]) for t in range(n)]


SMALL_ROWS, SMALL_COLS = 8, 1024


def _pack_small(vs):
    flat = jnp.concatenate([v.reshape(-1) for v in vs])
    return jnp.pad(flat, (0, SMALL_ROWS * SMALL_COLS - flat.shape[0])).reshape(SMALL_ROWS, SMALL_COLS)


def _unpack_small(packed, sizes):
    flat = packed.reshape(-1)
    out, o = [], 0
    for n in sizes:
        out.append(flat[o:o + n].reshape(1, n))
        o += n
    return out


def _all_reduce_small(v):
    n_dev = 8

    def body(v_ref, o_ref, land, send, recv):
        x, y, c = _me()
        me = 4 * x + 2 * y + c
        land[me] = v_ref[...]
        cps = []
        for r in range(1, n_dev):
            fx, fy, fc = (r >> 2) & 1, (r >> 1) & 1, r & 1
            peer = (x ^ fx, y ^ fy, c ^ fc)
            cps.append(pltpu.make_async_remote_copy(
                src_ref=v_ref, dst_ref=land.at[me], send_sem=send.at[r - 1], recv_sem=recv.at[r - 1],
                device_id=peer, device_id_type=MESH))
        for cp in cps:
            cp.start()
        for r in range(1, n_dev):
            fx, fy, fc = (r >> 2) & 1, (r >> 1) & 1, r & 1
            src = 4 * (x ^ fx) + 2 * (y ^ fy) + (c ^ fc)
            pltpu.make_async_remote_copy(
                src_ref=v_ref, dst_ref=land.at[src], send_sem=send.at[r - 1], recv_sem=recv.at[r - 1],
                device_id=(x ^ fx, y ^ fy, c ^ fc), device_id_type=MESH).wait_recv()
        for cp in cps:
            cp.wait_send()
        acc = land[0]
        for r in range(1, n_dev):
            acc = acc + land[r]
        o_ref[...] = acc

    vm = pl.BlockSpec(memory_space=pltpu.VMEM)
    return pl.pallas_call(
        body, in_specs=[vm], out_specs=vm, out_shape=SDS(v.shape, F32),
        scratch_shapes=[pltpu.VMEM((n_dev,) + v.shape, F32), pltpu.SemaphoreType.DMA((n_dev - 1,)),
                        pltpu.SemaphoreType.DMA((n_dev - 1,))],
        name="all_reduce_small")(v)


def kernel(x, mem, norm_gain, mem_norm_gain, w_in, b_forget, q_gain_a, k_gain_a, sinks_a, q_gain_b, k_gain_b, q_gain_c, k_gain_c, w_mem_kv, w_branch_a, w_branch_b, w_branch_c, w_out, loss_target, m_norm_gain, m_mem_norm_gain, m_w_in, m_b_forget, m_q_gain_a, m_k_gain_a, m_sinks_a, m_q_gain_b, m_k_gain_b, m_q_gain_c, m_k_gain_c, m_w_mem_kv, m_w_branch_a, m_w_branch_b, m_w_branch_c, m_w_out, v_norm_gain, v_mem_norm_gain, v_w_in, v_b_forget, v_q_gain_a, v_k_gain_a, v_sinks_a, v_q_gain_b, v_k_gain_b, v_q_gain_c, v_k_gain_c, v_w_mem_kv, v_w_branch_a, v_w_branch_b, v_w_branch_c, v_w_out):
    xi, yi, ci = lax.axis_index("x"), lax.axis_index("y"), lax.axis_index("c")
    chip = jnp.reshape(2 * xi + yi, (1,)).astype(jnp.int32)
    core = jnp.reshape(ci, (1,)).astype(jnp.int32)

    slabs = _pack_w_in(chip, jnp.transpose(w_in[0]))
    mine = [w_mem_kv[0].astype(BF16), w_branch_a[0].astype(BF16), w_branch_b[0].astype(BF16),
            w_branch_c[0].astype(BF16), w_out[0].astype(BF16)]
    w_main, w_fb = _merge_slabs(_all_gather_slabs(slabs))

    r = _local_step(x[0], mem[0], loss_target[0], w_main, w_fb, mine, norm_gain, mem_norm_gain,
                    b_forget, q_gain_a, k_gain_a, sinks_a, q_gain_b, k_gain_b, q_gain_c, k_gain_c, core=core)
    sums, theirs = r["sums"], r["theirs"]

    small_names = ["d_gain", "d_mem_gain", "d_bf", "d_qga", "d_kga", "d_sinks", "d_qgb", "d_kgb", "d_qgc", "d_kgc"]
    loss_part = (0.5 / D_MODEL) * jnp.sum(r["sq"], axis=1, keepdims=True)
    packed = _pack_small([r[n] for n in small_names] + [loss_part])
    red = _all_reduce_small(packed)
    small_w = [norm_gain, mem_norm_gain, b_forget, q_gain_a, k_gain_a, sinks_a, q_gain_b, k_gain_b, q_gain_c, k_gain_c]
    small_m = [m_norm_gain, m_mem_norm_gain, m_b_forget, m_q_gain_a, m_k_gain_a, m_sinks_a, m_q_gain_b, m_k_gain_b,
               m_q_gain_c, m_k_gain_c]
    small_v = [v_norm_gain, v_mem_norm_gain, v_b_forget, v_q_gain_a, v_k_gain_a, v_sinks_a, v_q_gain_b, v_k_gain_b,
               v_q_gain_c, v_k_gain_c]
    sizes = [w.shape[1] for w in small_w]
    s_d, s_m, s_v = _adamw(red, _pack_small(small_w), _pack_small(small_m), _pack_small(small_v), tr=8, name="adamw_small")
    g_small = _unpack_small(red, sizes + [1])
    loss = g_small[-1].reshape(())
    d_small, m_small, v_small = _unpack_small(s_d, sizes), _unpack_small(s_m, sizes), _unpack_small(s_v, sizes)

    gw_in, dw_in, mw_in, vw_in = _adamw_w_in(jnp.concatenate([chip, core]), sums[0], theirs[0], sums[1], theirs[1],
                                             jnp.transpose(w_in[0]), jnp.transpose(m_w_in[0]), jnp.transpose(v_w_in[0]))
    shards = ((2, "w_mem_kv", w_mem_kv, m_w_mem_kv, v_w_mem_kv),
              (3, "w_branch_a", w_branch_a, m_w_branch_a, v_w_branch_a),
              (4, "w_branch_b", w_branch_b, m_w_branch_b, v_w_branch_b),
              (5, "w_branch_c", w_branch_c, m_w_branch_c, v_w_branch_c),
              (6, "w_out", w_out, m_w_out, v_w_out))
    done = _adamw_halves([(sums[t], theirs[t], w[0], m[0], v[0], HALF_AXIS[t]) for t, _, w, m, v in shards], core)
    big = {nm: res for (_, nm, _, _, _), res in zip(shards, done)}

    def collect(kind):
        sm = (g_small, d_small, m_small, v_small)[kind]
        win = (gw_in, dw_in, mw_in, vw_in)[kind]
        return ([sm[0], sm[1], jnp.transpose(win)[None]] + [a for a in sm[2:10]]
                + [big[n][kind][None] for n in ("w_mem_kv", "w_branch_a", "w_branch_b", "w_branch_c", "w_out")])

    return (loss, r["grad_x"][None], *collect(0), *collect(1), *collect(2), *collect(3))
```

```python
import numpy as np
import jax
import jax.numpy as jnp
from jax import lax
from jax.experimental import pallas as pl
from jax.experimental.pallas import tpu as pltpu

F32 = jnp.float32
BF16 = jnp.bfloat16
HI = lax.Precision.HIGHEST
SDS = jax.ShapeDtypeStruct
MESH = pl.DeviceIdType.MESH

D_MODEL = 2048
HEAD_DIM = 64
A_HEADS = 12
A_GROUP = 3
B_HEADS = 12
C_HEADS = 4
C_HEAD_DIM = 128
WINDOW = 128
EPS = 1e-6
NEG = -1e30
LANE = 128

QA, KA, VA, ZA = 0, 768, 1024, 1280
QB, KB, VB, ZB = 2048, 2816, 3584, 4352
QC, ZC = 5120, 5632
GATE = 6144
P_MAIN = 12288
N_FORGET = 12
FORGET_COL = 5120
SHARD_COLS = 3075
SLAB = 3200
SLAB_START = (0, 3072, 6016, 9088)
SLAB_SHIFT = (0, 3, 122, 125)
N_CHIPS = 4

ADAM_LR = 0.001
ADAM_B1 = 0.9
ADAM_B2 = 0.999
ADAM_EPS = 1e-08
ADAM_WD = 0.01
ADAM_STEP = 10

VMEM_LIMIT = 56 * 1024 * 1024
VMEM_WIDE = 62 * 1024 * 1024


def _params(sem, vmem=VMEM_LIMIT):
    return pltpu.CompilerParams(dimension_semantics=sem, vmem_limit_bytes=vmem)


def _win(tr, width, off):
    return pl.BlockSpec((pl.Element(tr), pl.Element(width)), lambda i, *_: (i * tr, off))


def _rowblk(tr, width):
    return pl.BlockSpec((tr, width), lambda i, *_: (i, 0))


def _const(shape):
    nd = len(shape)
    return pl.BlockSpec(shape, lambda *_: (0,) * nd)


def _rms(x, g):
    return x * lax.rsqrt(jnp.mean(x * x, axis=-1, keepdims=True) + EPS) * g


def _head_mean_impl(x2, bd):
    hi = x2.astype(BF16)
    lo = (x2 - hi.astype(F32)).astype(BF16)
    return _dot(hi, bd) + _dot(lo, bd)


@jax.custom_vjp
def _head_mean(x2, bd):
    return _head_mean_impl(x2, bd)


_head_mean.defvjp(lambda x2, bd: (_head_mean_impl(x2, bd), bd),
                  lambda bd, g: (_head_mean_impl(g, bd), jnp.zeros_like(bd)))


def _head_norm(x, g_tiled, bd):
    return x * lax.rsqrt(_head_mean(x * x, bd) + EPS) * g_tiled


def _silu(z):
    return z * jax.nn.sigmoid(z)


def _dot_nt(a, b):
    return lax.dot_general(a, b, (((1,), (1,)), ((), ())), preferred_element_type=F32)


def _dot_tn(a, b):
    return lax.dot_general(a, b, (((0,), (0,)), ((), ())), preferred_element_type=F32)


def _dot(a, b):
    return jnp.dot(a, b, preferred_element_type=F32)


def _swa_fn(qk, vz, qkp, vzp, qg, kg, sinks, bd, bias, first):
    q = _head_norm(qk[:, :768], qg, bd)
    k2 = jnp.concatenate([qkp[:, 768:], qk[:, 768:]], axis=0)
    k2 = _head_norm(k2, kg, bd[:256, :256])
    v2 = jnp.concatenate([vzp[:, :256], vz[:, :256]], axis=0)
    z = vz[:, 256:]
    cols = A_GROUP * WINDOW
    kj = lax.broadcasted_iota(jnp.int32, (2 * WINDOW, cols), 0)
    no_prev = kj < WINDOW * first.astype(jnp.int32)
    qtb = jnp.transpose(q).astype(BF16)
    kb = k2.astype(BF16)
    vtb = jnp.transpose(v2).astype(BF16)
    outs = [None] * A_HEADS
    for g in range(A_HEADS // A_GROUP):
        heads = [A_GROUP * g + u for u in range(A_GROUP)]
        qs = jnp.concatenate([qtb[64 * h:64 * h + 64, :] for h in heads], axis=1)
        s = _dot(kb[:, 64 * g:64 * g + 64], qs) * (HEAD_DIM ** -0.5) + bias[g]
        s = jnp.where(no_prev, NEG, s)
        sink = jnp.concatenate([jnp.broadcast_to(sinks[:, h:h + 1], (1, WINDOW)) for h in heads], axis=1)
        m = lax.stop_gradient(jnp.maximum(jnp.max(s, axis=0, keepdims=True), sink))
        p = jnp.exp(s - m)
        den = jnp.sum(p, axis=0, keepdims=True) + jnp.exp(sink - m)
        o = _dot(vtb[64 * g:64 * g + 64, :], (p * (1.0 / den)).astype(BF16))
        for u, h in enumerate(heads):
            outs[h] = o[:, WINDOW * u:WINDOW * u + WINDOW]
    return jnp.transpose(jnp.concatenate(outs, axis=0)) * _silu(z)


def _swa_bias():
    qi = np.arange(WINDOW)[None, :]
    kj = np.arange(2 * WINDOW)[:, None]
    rel = qi + WINDOW - kj
    valid = (rel >= 0) & (rel < WINDOW)
    out = np.zeros((A_HEADS // A_GROUP, 2 * WINDOW, A_GROUP * WINDOW), np.float32)
    for h in range(A_HEADS):
        slope = np.float32(2.0 ** (-8.0 * (h + 1) / A_HEADS))
        blk = np.where(valid, -slope * rel.astype(np.float32), np.float32(NEG))
        g, u = divmod(h, A_GROUP)
        out[g, :, WINDOW * u:WINDOW * u + WINDOW] = blk
    return jnp.asarray(out)


def _mem_fn(qz, mkv, qg, kg, bd):
    q = _head_norm(qz[:, :512], qg, bd).astype(BF16)
    k = _head_norm(mkv[:, :512], kg, bd).astype(BF16)
    v = mkv[:, 512:].astype(BF16)
    z = qz[:, 512:]
    outs = []
    for h in range(C_HEADS):
        sl = slice(128 * h, 128 * h + 128)
        s = _dot_nt(q[:, sl], k[:, sl]) * (C_HEAD_DIM ** -0.5)
        m = lax.stop_gradient(jnp.max(s, axis=-1, keepdims=True))
        p = jnp.exp(s - m)
        den = jnp.sum(p, axis=-1, keepdims=True)
        outs.append(_dot((p * (1.0 / den)).astype(BF16), v[:, sl]))
    return jnp.concatenate(outs, axis=1) * _silu(z)


def _qn_fn(q, g, bd):
    return _head_norm(q, g, bd) * (HEAD_DIM ** -0.5)


def _kn_fn(k, g, bd):
    return _head_norm(k, g, bd)


def _block_diag(width, hd):
    i = np.arange(width) // hd
    return jnp.asarray((i[:, None] == i[None, :]).astype(np.float32) / hd, BF16)


def _head_sum(width, hd):
    i = np.arange(width) // hd
    return jnp.asarray((i[:, None] == np.arange(LANE)[None, :]).astype(np.float32))


def _rms_fwd(x, g, *, tr, name):
    rows, dm = x.shape

    def body(x_ref, g_ref, o_ref):
        o_ref[...] = _rms(x_ref[...], g_ref[...]).astype(BF16)

    return pl.pallas_call(
        body, grid=(rows // tr,),
        in_specs=[_rowblk(tr, dm), _const((1, dm))],
        out_specs=_rowblk(tr, dm),
        out_shape=SDS((rows, dm), BF16), name=name,
        compiler_params=_params(("parallel",)))(x, g)


def _rms_bwd(x, g, dy, resid, *, tr, name, comm=None):
    rows, dm = x.shape
    want_dx = resid is not None
    n_in = 4 if want_dx else 3
    n_out = 2 if want_dx else 1
    c_in = len(comm.ins) if comm else 0
    c_out = len(comm.out_shapes) if comm else 0
    nb = rows // tr

    def body(*refs):
        x_ref, g_ref, dy_ref = refs[:3]
        r_ref = refs[3] if want_dx else None
        cin = refs[n_in:n_in + c_in]
        outs = refs[n_in + c_in:n_in + c_in + n_out]
        dg_ref = outs[-1]
        cout = refs[n_in + c_in + n_out:n_in + c_in + n_out + c_out]
        csem = refs[n_in + c_in + n_out + c_out:]

        if comm:
            @pl.when(pl.program_id(0) == 0)
            def _():
                comm.start(cin, cout, csem)

        _, vjp = jax.vjp(_rms, x_ref[...], g_ref[...])
        dx, dg = vjp(dy_ref[...])

        @pl.when(pl.program_id(0) == 0)
        def _():
            dg_ref[...] = jnp.zeros_like(dg_ref)

        dg_ref[...] += dg
        if want_dx:
            outs[0][...] = r_ref[...] + dx

        if comm:
            @pl.when(pl.program_id(0) == nb - 1)
            def _():
                comm.finish(cin, cout, csem)

    hbm = pl.BlockSpec(memory_space=pl.ANY)
    ins = [x, g, dy] + ([resid] if want_dx else []) + (list(comm.ins) if comm else [])
    in_specs = ([_rowblk(tr, dm), _const((1, dm)), _rowblk(tr, dm)] + ([_rowblk(tr, dm)] if want_dx else [])
                + [hbm] * c_in)
    out_specs = ([_rowblk(tr, dm)] if want_dx else []) + [_const((1, dm))] + [hbm] * c_out
    out_shape = (([SDS((rows, dm), F32)] if want_dx else []) + [SDS((1, dm), F32)]
                 + (list(comm.out_shapes) if comm else []))
    res = pl.pallas_call(
        body, grid=(nb,), in_specs=in_specs, out_specs=out_specs, out_shape=out_shape,
        scratch_shapes=list(comm.sems) if comm else [], name=name, compiler_params=_params(("arbitrary",)))(*ins)
    return (list(res[:n_out]), list(res[n_out:])) if comm else res


class _Comm:
    def __init__(self, ins, out_shapes, sems, start, finish):
        self.ins, self.out_shapes, self.sems, self.start, self.finish = list(ins), list(out_shapes), list(sems), start, finish


def _matmul(a, b, *, dims, out_dtype, tm, tn, tk, name, add=None, comms=(), vmem=VMEM_LIMIT, a_half=None):
    a_list = list(a) if isinstance(a, (list, tuple)) else [a]
    b_list = list(b) if isinstance(b, (list, tuple)) else [b]
    assert len(a_list) == 1 or dims == "nt"
    assert len(b_list) == 1 or dims == "tn"
    if dims == "tn":
        kdim, m = a_list[0].shape
        if a_half is not None:
            m //= 2
    else:
        m, kdim = a_list[0].shape[0], sum(p.shape[1] for p in a_list)
    n = b_list[0].shape[0] if dims == "nt" else sum(p.shape[1] for p in b_list)
    tm, tn, tk = min(tm, m), min(tn, n), min(tk, kdim)
    assert m % tm == 0 and n % tn == 0 and kdim % tk == 0, (name, m, n, kdim)
    ni, nj, nk = m // tm, n // tn, kdim // tk
    a_rng, b_rng, pos = [], [], 0
    for p in a_list:
        assert len(a_list) == 1 or p.shape[1] % tk == 0
        a_rng.append((pos, p.shape[1] // tk if len(a_list) > 1 else nk))
        pos += a_rng[-1][1]
    pos = 0
    for p in b_list:
        assert len(b_list) == 1 or p.shape[1] % tn == 0
        b_rng.append((pos, p.shape[1] // tn if len(b_list) > 1 else nj))
        pos += b_rng[-1][1]
    has_add = add is not None
    n_mm_in = len(a_list) + len(b_list) + (1 if has_add else 0)
    c_in = [len(c.ins) for c in comms]
    c_out = [len(c.out_shapes) for c in comms]
    c_sem = [len(c.sems) for c in comms]

    def body(*refs):
        if a_half is not None:
            refs = refs[1:]
        a_refs, b_refs = refs[:len(a_list)], refs[len(a_list):len(a_list) + len(b_list)]
        add_ref = refs[n_mm_in - 1] if has_add else None
        pos = n_mm_in
        cin = []
        for cnt in c_in:
            cin.append(refs[pos:pos + cnt])
            pos += cnt
        o_ref = refs[pos]
        pos += 1
        cout = []
        for cnt in c_out:
            cout.append(refs[pos:pos + cnt])
            pos += cnt
        acc = refs[pos]
        pos += 1
        csem = []
        for cnt in c_sem:
            csem.append(refs[pos:pos + cnt])
            pos += cnt
        i, j, k = pl.program_id(0), pl.program_id(1), pl.program_id(2)

        if comms:
            @pl.when((i == 0) & (j == 0) & (k == 0))
            def _():
                for c, ci, co, cs in zip(comms, cin, cout, csem):
                    c.start(ci, co, cs)

        def accumulate(a_ref, b_ref, first_k, later_k):
            if dims == "nn":
                part = _dot(a_ref[...], b_ref[...])
            elif dims == "nt":
                part = _dot_nt(a_ref[...], b_ref[...])
            else:
                part = _dot_tn(a_ref[...], b_ref[...])

            if first_k:
                @pl.when(k == 0)
                def _():
                    acc[...] = part + add_ref[...] if has_add else part

            if later_k:
                @pl.when(k > 0)
                def _():
                    acc[...] += part

        if len(a_list) > 1:
            for a_ref, (k0, cnt) in zip(a_refs, a_rng):
                @pl.when((k >= k0) & (k < k0 + cnt))
                def _(a_ref=a_ref, k0=k0, cnt=cnt):
                    accumulate(a_ref, b_refs[0], k0 == 0, k0 + cnt > 1)
        elif len(b_list) > 1:
            for b_ref, (j0, cnt) in zip(b_refs, b_rng):
                @pl.when((j >= j0) & (j < j0 + cnt))
                def _(b_ref=b_ref):
                    accumulate(a_refs[0], b_ref, True, nk > 1)
        else:
            accumulate(a_refs[0], b_refs[0], True, nk > 1)

        @pl.when(k == nk - 1)
        def _():
            o_ref[...] = acc[...].astype(out_dtype)

        if comms:
            @pl.when((i == ni - 1) & (j == nj - 1) & (k == nk - 1))
            def _():
                for c, ci, co, cs in zip(comms, cin, cout, csem):
                    c.finish(ci, co, cs)

    def a_spec(k0, cnt):
        if dims == "tn":
            if a_half is None:
                return pl.BlockSpec((tk, tm), lambda i, j, k: (k, i))
            pick = (lambda c: 1 - c[0]) if a_half[1] else (lambda c: c[0])
            return pl.BlockSpec((tk, tm), lambda i, j, k, c: (k, i + pick(c) * ni))
        return pl.BlockSpec((tm, tk), lambda i, j, k, *_: (i, jnp.clip(k - k0, 0, cnt - 1)))

    def b_spec(j0, cnt):
        if dims == "nt":
            return pl.BlockSpec((tn, tk), lambda i, j, k, *_: (j, k))
        return pl.BlockSpec((tk, tn), lambda i, j, k, *_: (k, jnp.clip(j - j0, 0, cnt - 1)))

    o_spec = pl.BlockSpec((tm, tn), lambda i, j, k, *_: (i, j))
    hbm = pl.BlockSpec(memory_space=pl.ANY)
    ins = a_list + b_list + ([add] if has_add else []) + [x for c in comms for x in c.ins]
    in_specs = ([a_spec(*r) for r in a_rng] + [b_spec(*r) for r in b_rng] + ([o_spec] if has_add else [])
                + [hbm] * sum(c_in))
    out_specs = [o_spec] + [hbm] * sum(c_out)
    out_shape = [SDS((m, n), out_dtype)] + [s for c in comms for s in c.out_shapes]
    scratch = [pltpu.VMEM((tm, tn), F32)] + [s for c in comms for s in c.sems]
    sem = ("arbitrary",) * 3 if comms else ("parallel", "parallel", "arbitrary")
    if a_half is None:
        grid = dict(grid=(ni, nj, nk), in_specs=in_specs, out_specs=out_specs, scratch_shapes=scratch)
    else:
        grid = dict(grid_spec=pltpu.PrefetchScalarGridSpec(
            num_scalar_prefetch=1, grid=(ni, nj, nk), in_specs=in_specs, out_specs=out_specs, scratch_shapes=scratch))
        ins = [a_half[0]] + ins
    res = pl.pallas_call(body, out_shape=out_shape, name=name, compiler_params=_params(sem, vmem), **grid)(*ins)
    if not comms:
        return res[0]
    outs, pos = [], 1
    for cnt in c_out:
        outs.append(list(res[pos:pos + cnt]))
        pos += cnt
    return res[0], outs


def _swa_specs(nb, blk=lambda n: n):
    cur = lambda off: pl.BlockSpec((pl.Element(WINDOW), pl.Element(1024)), lambda n: (blk(n) * WINDOW, off))
    prev = lambda off: pl.BlockSpec((pl.Element(WINDOW), pl.Element(1024)),
                                    lambda n: (jnp.maximum(blk(n) - 1, 0) * WINDOW, off))
    return [cur(QA), cur(VA), prev(QA), prev(VA),
            _const((1, 768)), _const((1, 256)), _const((1, LANE)), _const((768, 768)),
            _const((A_HEADS // A_GROUP, 2 * WINDOW, A_GROUP * WINDOW))]


def _swa_fwd(proj, qg, kg, sinks, bd, bias):
    s = proj.shape[0]
    nb = s // WINDOW

    def body(qk_ref, vz_ref, qkp_ref, vzp_ref, qg_ref, kg_ref, sk_ref, bd_ref, bias_ref, o_ref):
        first = pl.program_id(0) == 0
        o_ref[...] = _swa_fn(qk_ref[...], vz_ref[...], qkp_ref[...], vzp_ref[...], qg_ref[...], kg_ref[...],
                             sk_ref[...], bd_ref[...], bias_ref[...], first).astype(BF16)

    return pl.pallas_call(
        body, grid=(nb,), in_specs=_swa_specs(nb), out_specs=_rowblk(WINDOW, 768),
        out_shape=SDS((s, 768), BF16), name="swa_fwd",
        compiler_params=_params(("parallel",)))(proj, proj, proj, proj, qg, kg, sinks, bd, bias)


def _swa_bwd(proj, qg, kg, sinks, bd, bias, dga):
    s = proj.shape[0]
    nb = s // WINDOW
    blk = lambda n: nb - 1 - n

    def body(qk_ref, vz_ref, qkp_ref, vzp_ref, qg_ref, kg_ref, sk_ref, bd_ref, bias_ref, dg_ref,
             d_ref, dqg_ref, dkg_ref, dsk_ref, carry):
        first = blk(pl.program_id(0)) == 0
        bd_v = bd_ref[...]
        bias_v = bias_ref[...]
        fn = lambda qk, vz, qkp, vzp, qg_, kg_, sk: _swa_fn(qk, vz, qkp, vzp, qg_, kg_, sk, bd_v, bias_v, first)
        _, vjp = jax.vjp(fn, qk_ref[...], vz_ref[...], qkp_ref[...], vzp_ref[...], qg_ref[...], kg_ref[...], sk_ref[...])
        dqk, dvz, dqkp, dvzp, dqg, dkg, dsk = vjp(dg_ref[...])

        @pl.when(pl.program_id(0) == 0)
        def _():
            dqg_ref[...] = jnp.zeros_like(dqg_ref)
            dkg_ref[...] = jnp.zeros_like(dkg_ref)
            dsk_ref[...] = jnp.zeros_like(dsk_ref)
            carry[...] = jnp.zeros_like(carry)

        dqg_ref[...] += dqg
        dkg_ref[...] += dkg
        dsk_ref[...] += dsk
        kv = jnp.concatenate([dqk[:, 768:], dvz[:, :256]], axis=1) + carry[...]
        d_ref[...] = jnp.concatenate([dqk[:, :768], kv, dvz[:, 256:]], axis=1).astype(BF16)
        carry[...] = jnp.concatenate([dqkp[:, 768:], dvzp[:, :256]], axis=1)

    rev = lambda w: pl.BlockSpec((WINDOW, w), lambda n: (blk(n), 0))
    return pl.pallas_call(
        body, grid=(nb,), in_specs=_swa_specs(nb, blk) + [rev(768)],
        out_specs=[rev(2048), _const((1, 768)), _const((1, 256)), _const((1, LANE))],
        out_shape=[SDS((s, 2048), BF16), SDS((1, 768), F32), SDS((1, 256), F32), SDS((1, LANE), F32)],
        scratch_shapes=[pltpu.VMEM((WINDOW, 512), F32)],
        name="swa_bwd", compiler_params=_params(("arbitrary",)))(proj, proj, proj, proj, qg, kg, sinks, bd, bias, dga)


def _mem_fwd(proj, mkv, qg, kg, bd, *, tr):
    s = proj.shape[0]

    def body(qz_ref, mkv_ref, qg_ref, kg_ref, bd_ref, o_ref):
        o_ref[...] = _mem_fn(qz_ref[...], mkv_ref[...], qg_ref[...], kg_ref[...], bd_ref[...]).astype(BF16)

    return pl.pallas_call(
        body, grid=(s // tr,),
        in_specs=[_win(tr, 1024, QC), _const(mkv.shape), _const((1, 512)), _const((1, 512)), _const((512, 512))],
        out_specs=_rowblk(tr, 512), out_shape=SDS((s, 512), BF16), name="mem_fwd",
        compiler_params=_params(("parallel",)))(proj, mkv, qg, kg, bd)


def _mem_bwd(proj, mkv, qg, kg, bd, dgc, *, tr):
    s = proj.shape[0]

    def body(qz_ref, mkv_ref, qg_ref, kg_ref, bd_ref, dg_ref, dqz_ref, dmkv_ref, dqg_ref, dkg_ref):
        bd_v = bd_ref[...]
        fn = lambda qz, mkv_, qg_, kg_: _mem_fn(qz, mkv_, qg_, kg_, bd_v)
        _, vjp = jax.vjp(fn, qz_ref[...], mkv_ref[...], qg_ref[...], kg_ref[...])
        dqz, dmkv, dqg, dkg = vjp(dg_ref[...])

        @pl.when(pl.program_id(0) == 0)
        def _():
            dmkv_ref[...] = jnp.zeros_like(dmkv_ref)
            dqg_ref[...] = jnp.zeros_like(dqg_ref)
            dkg_ref[...] = jnp.zeros_like(dkg_ref)

        dmkv_ref[...] += dmkv
        dqg_ref[...] += dqg
        dkg_ref[...] += dkg
        dqz_ref[...] = dqz.astype(BF16)

    return pl.pallas_call(
        body, grid=(s // tr,),
        in_specs=[_win(tr, 1024, QC), _const(mkv.shape), _const((1, 512)), _const((1, 512)), _const((512, 512)),
                  _rowblk(tr, 512)],
        out_specs=[_rowblk(tr, 1024), _const(mkv.shape), _const((1, 512)), _const((1, 512))],
        out_shape=[SDS((s, 1024), BF16), SDS(mkv.shape, F32), SDS((1, 512), F32), SDS((1, 512), F32)],
        name="mem_bwd", compiler_params=_params(("arbitrary",)))(proj, mkv, qg, kg, bd, dgc)


def _log_sigmoid(x):
    return jnp.minimum(x, 0.0) - jnp.log1p(jnp.exp(-jnp.abs(x)))


FOX_TQ, FOX_TK = 512, 512
FOX_FWD_TQ, FOX_FWD_TK = 512, 1024


def _fox_tiles(s):
    return min(FOX_TQ, s), min(FOX_TK, s)


AUG = 128 * B_HEADS
COL_A, COL_B = 64, 67


def _split3(c):
    hi = c.astype(BF16)
    r1 = c - hi.astype(F32)
    mid = r1.astype(BF16)
    lo = (r1 - mid.astype(F32)).astype(BF16)
    return hi, mid, lo


def _expand_mats():
    def mat(col0):
        e = np.zeros((768 + 3 * LANE, AUG), np.float32)
        for h in range(B_HEADS):
            for d in range(HEAD_DIM):
                e[64 * h + d, 128 * h + d] = 1.0
            for part in range(3):
                e[768 + LANE * part + h, 128 * h + col0 + part] = 1.0
        return e

    def ones(col0):
        o = np.zeros((1, AUG), np.float32)
        for h in range(B_HEADS):
            o[0, 128 * h + col0:128 * h + col0 + 3] = 1.0
        return o

    return (jnp.asarray(mat(COL_A), BF16), jnp.asarray(mat(COL_B), BF16), jnp.asarray(ones(COL_A)), jnp.asarray(ones(COL_B)))


def _augment(data_bf16, triple, emat, ones_row):
    parts = [data_bf16] + (list(triple) if triple is not None else [jnp.zeros((data_bf16.shape[0], LANE), BF16)] * 3)
    wide = _dot(jnp.concatenate(parts, axis=1), emat)
    if ones_row is not None:
        wide = wide + ones_row
    return wide


def _compact(wide):
    return jnp.concatenate([wide[:, 128 * h:128 * h + 64] for h in range(wide.shape[1] // 128)], axis=1)


def _lane_of_heads(wide, col, first=0):
    rows = wide.shape[0]
    lane = lax.broadcasted_iota(jnp.int32, (rows, LANE), 1)
    out = jnp.zeros((rows, LANE), F32)
    for h in range(wide.shape[1] // 128):
        out = jnp.where(lane == first + h, wide[:, 128 * h + col:128 * h + col + 1], out)
    return out


def _fox2_prep(proj, fbl, qg, kg, bfor, bd, ea, eb, ones_a, ones_b, *, tr):
    s = proj.shape[0]
    tri = jnp.asarray(np.tril(np.ones((tr, tr), np.float32)))

    def body(q_ref, k_ref, v_ref, fb_ref, qg_ref, kg_ref, bf_ref, bd_ref, tri_ref, ea_ref, eb_ref, oa_ref, ob_ref,
             qat_ref, ka_ref, kat_ref, va_ref, vat_ref, qn_ref, c_ref, carry):
        @pl.when(pl.program_id(0) == 0)
        def _():
            carry[...] = jnp.zeros_like(carry)

        bd_v = bd_ref[...]
        lane = lax.broadcasted_iota(jnp.int32, (tr, LANE), 1)
        logf = jnp.where(lane < N_FORGET, _log_sigmoid(fb_ref[...] + bf_ref[...]), 0.0)
        c = jnp.dot(tri_ref[...], logf, precision=HI, preferred_element_type=F32) + carry[...]
        c_ref[...] = c
        carry[...] = c[tr - 1:tr, :]
        qn = _qn_fn(q_ref[...], qg_ref[...], bd_v).astype(BF16)
        kn = _kn_fn(k_ref[...], kg_ref[...], bd_v).astype(BF16)
        qn_ref[...] = qn
        qat_ref[...] = jnp.transpose(_augment(qn, _split3(c), ea_ref[...], ob_ref[...])).astype(BF16)
        ka = _augment(kn, _split3(-c), eb_ref[...], oa_ref[...])
        ka_ref[...] = ka.astype(BF16)
        kat_ref[...] = jnp.transpose(ka).astype(BF16)
        va = _augment(v_ref[...].astype(BF16), None, ea_ref[...], oa_ref[...])
        va_ref[...] = va.astype(BF16)
        vat_ref[...] = jnp.transpose(va).astype(BF16)

    emat = _const((768 + 3 * LANE, AUG))
    return pl.pallas_call(
        body, grid=(s // tr,),
        in_specs=[_win(tr, 768, QB), _win(tr, 768, KB), _win(tr, 768, VB), _rowblk(tr, LANE), _const((1, 768)),
                  _const((1, 768)), _const((1, LANE)), _const((768, 768)), _const((tr, tr)), emat, emat,
                  _const((1, AUG)), _const((1, AUG))],
        out_specs=[pl.BlockSpec((AUG, tr), lambda i: (0, i)), _rowblk(tr, AUG), pl.BlockSpec((AUG, tr), lambda i: (0, i)),
                   _rowblk(tr, AUG), pl.BlockSpec((AUG, tr), lambda i: (0, i)), _rowblk(tr, 768), _rowblk(tr, LANE)],
        out_shape=[SDS((AUG, s), BF16), SDS((s, AUG), BF16), SDS((AUG, s), BF16), SDS((s, AUG), BF16),
                   SDS((AUG, s), BF16), SDS((s, 768), BF16), SDS((s, LANE), F32)],
        scratch_shapes=[pltpu.VMEM((1, LANE), F32)], name="fox_prep",
        compiler_params=_params(("arbitrary",)))(proj, proj, proj, fbl, qg, kg, bfor, bd, tri, ea, eb, ones_a, ones_b)


def _fox2_fwd(proj, qat, ka, vat):
    s = proj.shape[0]
    tq, tk = min(FOX_FWD_TQ, s), min(FOX_FWD_TK, s)
    nq, nk = s // tq, s // tk

    def last_k(i):
        return (i * tq + tq - 1) // tk

    def body(qt_ref, k_ref, vt_ref, z_ref, gb_ref, yb_ref, lse_ref, acc, m_s):
        i, j = pl.program_id(0), pl.program_id(1)

        @pl.when(j == 0)
        def _():
            acc[...] = jnp.zeros_like(acc)
            m_s[...] = jnp.full_like(m_s, NEG)

        def tile(masked):
            if masked:
                kpos = j * tk + lax.broadcasted_iota(jnp.int32, (tk, tq), 0)
                qpos = i * tq + lax.broadcasted_iota(jnp.int32, (tk, tq), 1)
                mask = kpos <= qpos
            for h in range(B_HEADS):
                sl = slice(128 * h, 128 * h + 128)
                sc = _dot(k_ref[:, sl], qt_ref[sl, :])
                if masked:
                    sc = jnp.where(mask, sc, NEG)
                m_prev = m_s[h:h + 1, :]
                m_new = jnp.maximum(m_prev, jnp.max(sc, axis=0, keepdims=True))
                p = jnp.exp(sc - m_new).astype(BF16)
                acc[sl, :] = jnp.exp(m_prev - m_new) * acc[sl, :] + _dot(vt_ref[sl, :], p)
                m_s[h:h + 1, :] = m_new

        full = j * tk + tk - 1 <= i * tq

        @pl.when(full)
        def _():
            tile(False)

        @pl.when(jnp.logical_and(jnp.logical_not(full), j <= last_k(i)))
        def _():
            tile(True)

        @pl.when(j == nk - 1)
        def _():
            outs = []
            row = lax.broadcasted_iota(jnp.int32, (LANE, tq), 0)
            lse_t = jnp.zeros((LANE, tq), F32)
            for h in range(B_HEADS):
                l_row = acc[128 * h + COL_A:128 * h + COL_A + 1, :]
                outs.append(acc[128 * h:128 * h + 64, :] * (1.0 / l_row))
                lse_t = jnp.where(row == h, m_s[h:h + 1, :] + jnp.log(l_row), lse_t)
            y = jnp.transpose(jnp.concatenate(outs, axis=0))
            yb_ref[...] = y
            gb_ref[...] = (y * _silu(z_ref[...])).astype(BF16)
            lse_ref[...] = jnp.transpose(lse_t)

    kcol = lambda i, j: (0, jnp.minimum(j, last_k(i)))
    return pl.pallas_call(
        body, grid=(nq, nk),
        in_specs=[pl.BlockSpec((AUG, tq), lambda i, j: (0, i)),
                  pl.BlockSpec((tk, AUG), lambda i, j: (jnp.minimum(j, last_k(i)), 0)),
                  pl.BlockSpec((AUG, tk), kcol),
                  pl.BlockSpec((pl.Element(tq), pl.Element(768)), lambda i, j: (i * tq, ZB))],
        out_specs=[pl.BlockSpec((tq, 768), lambda i, j: (i, 0)), pl.BlockSpec((tq, 768), lambda i, j: (i, 0)),
                   pl.BlockSpec((tq, LANE), lambda i, j: (i, 0))],
        out_shape=[SDS((s, 768), BF16), SDS((s, 768), F32), SDS((s, LANE), F32)],
        scratch_shapes=[pltpu.VMEM((AUG, tq), F32), pltpu.VMEM((16, tq), F32)],
        name="fox_fwd", compiler_params=_params(("parallel", "arbitrary")))(qat, ka, vat, proj)


def _fox2_bwd_pre(proj, yb, dgb, qn, c, lse, hsum, ea, ones_b, *, tr):
    s = proj.shape[0]

    def body(z_ref, y_ref, dg_ref, qn_ref, c_ref, lse_ref, hs_ref, ea_ref, ob_ref,
             qa_ref, qat_ref, dya_ref, dyat_ref, dz_ref):
        z, y, dg = z_ref[...], y_ref[...], dg_ref[...]
        sg = jax.nn.sigmoid(z)
        dy = dg * (z * sg)
        dz_ref[...] = (dg * y * (sg * (1.0 + z * (1.0 - sg)))).astype(BF16)
        delta = jnp.dot(dy * y, hs_ref[...], precision=HI, preferred_element_type=F32)
        e = ea_ref[...]
        dya = _augment(dy.astype(BF16), _split3(-delta), e, None)
        dya_ref[...] = dya.astype(BF16)
        dyat_ref[...] = jnp.transpose(dya).astype(BF16)
        qa = _augment(qn_ref[...], _split3(c_ref[...] - lse_ref[...]), e, ob_ref[...])
        qa_ref[...] = qa.astype(BF16)
        qat_ref[...] = jnp.transpose(qa).astype(BF16)

    return pl.pallas_call(
        body, grid=(s // tr,),
        in_specs=[_win(tr, 768, ZB), _rowblk(tr, 768), _rowblk(tr, 768), _rowblk(tr, 768), _rowblk(tr, LANE),
                  _rowblk(tr, LANE), _const((768, LANE)), _const((768 + 3 * LANE, AUG)), _const((1, AUG))],
        out_specs=[_rowblk(tr, AUG), pl.BlockSpec((AUG, tr), lambda i: (0, i)), _rowblk(tr, AUG),
                   pl.BlockSpec((AUG, tr), lambda i: (0, i)), _rowblk(tr, 768)],
        out_shape=[SDS((s, AUG), BF16), SDS((AUG, s), BF16), SDS((s, AUG), BF16), SDS((AUG, s), BF16),
                   SDS((s, 768), BF16)], name="fox_bwd_pre",
        compiler_params=_params(("parallel",)))(proj, yb, dgb, qn, c, lse, hsum, ea, ones_b)


def _fox2_bwd(qb, qbt, ka, kat, va, dya, dyat):
    s = qb.shape[0]
    tq, tk = _fox_tiles(s)
    nq, nk = s // tq, s // tk
    ng = 2
    gh = B_HEADS // ng
    gw = 128 * gh

    def first_q(j):
        return (j * tk) // tq

    def body(q_ref, qt_ref, k_ref, kt_ref, v_ref, dy_ref, dyt_ref, dq_hbm, dk_ref, dv_ref, dck_ref,
             dq_acc, dk_acc, dv_acc, sem):
        g, j, i = pl.program_id(0), pl.program_id(1), pl.program_id(2)

        @pl.when((j == 0) & (i == 0))
        def _():
            dq_acc[...] = jnp.zeros_like(dq_acc)

        @pl.when(i == 0)
        def _():
            dk_acc[...] = jnp.zeros_like(dk_acc)
            dv_acc[...] = jnp.zeros_like(dv_acc)

        def tile(masked):
            if masked:
                kpos = j * tk + lax.broadcasted_iota(jnp.int32, (tk, tq), 0)
                qpos = i * tq + lax.broadcasted_iota(jnp.int32, (tk, tq), 1)
                mask = kpos <= qpos
            cols = pl.ds(pl.multiple_of(i * tq, tq), tq)
            for h in range(gh):
                sl = slice(128 * h, 128 * h + 128)
                sc = _dot(k_ref[:, sl], qt_ref[sl, :])
                if masked:
                    sc = jnp.where(mask, sc, NEG)
                p = jnp.exp(sc)
                ds = (p * _dot(v_ref[:, sl], dyt_ref[sl, :])).astype(BF16)
                dv_acc[:, sl] += _dot(p.astype(BF16), dy_ref[:, sl])
                dk_acc[:, sl] += _dot(ds, q_ref[:, sl])
                dq_acc[sl, cols] += _dot(kt_ref[sl, :], ds)

        full = j * tk + tk - 1 <= i * tq

        @pl.when(full)
        def _():
            tile(False)

        @pl.when(jnp.logical_and(jnp.logical_not(full), i >= first_q(j)))
        def _():
            tile(True)

        @pl.when(i == nq - 1)
        def _():
            dkw = dk_acc[...]
            dk_ref[...] = _compact(dkw)
            dv_ref[...] = _compact(dv_acc[...]).astype(BF16)
            dck_ref[...] = -_lane_of_heads(dkw, COL_B, gh * g)

        @pl.when((j == nk - 1) & (i == nq - 1))
        def _():
            cp = pltpu.make_async_copy(dq_acc, dq_hbm.at[pl.ds(pl.multiple_of(g * gw, gw), gw)], sem)
            cp.start()
            cp.wait()

    qrow = pl.BlockSpec((tq, gw), lambda g, j, i: (jnp.maximum(i, first_q(j)), g))
    qcol = pl.BlockSpec((gw, tq), lambda g, j, i: (g, jnp.maximum(i, first_q(j))))
    krow = pl.BlockSpec((tk, gw), lambda g, j, i: (j, g))
    kcol = pl.BlockSpec((gw, tk), lambda g, j, i: (g, j))
    kout = pl.BlockSpec((tk, gw // 2), lambda g, j, i: (j, g))
    return pl.pallas_call(
        body, grid=(ng, nk, nq),
        in_specs=[qrow, qcol, krow, kcol, krow, qrow, qcol],
        out_specs=[pl.BlockSpec(memory_space=pl.ANY), kout, kout,
                   pl.BlockSpec((None, tk, LANE), lambda g, j, i: (g, j, 0))],
        out_shape=[SDS((AUG, s), F32), SDS((s, 768), F32), SDS((s, 768), BF16), SDS((ng, s, LANE), F32)],
        scratch_shapes=[pltpu.VMEM((gw, s), F32), pltpu.VMEM((tk, gw), F32), pltpu.VMEM((tk, gw), F32),
                        pltpu.SemaphoreType.DMA],
        name="fox_bwd", compiler_params=_params(("arbitrary",) * 3))(qb, qbt, ka, kat, va, dya, dyat)


def _fox2_bwd_post(proj, fbl, qg, kg, bfor, bd, dqa, dkn, dck, *, tr):
    s = proj.shape[0]
    nb = s // tr
    triu = jnp.asarray(np.triu(np.ones((tr, tr), np.float32)))
    rev = lambda i: nb - 1 - i

    def body(q_ref, k_ref, fb_ref, qg_ref, kg_ref, bf_ref, bd_ref, tri_ref, dqa_ref, dkn_ref, dck_ref,
             dq_ref, dk_ref, dfb_ref, dqg_ref, dkg_ref, dbf_ref, carry):
        @pl.when(pl.program_id(0) == 0)
        def _():
            carry[...] = jnp.zeros_like(carry)
            dqg_ref[...] = jnp.zeros_like(dqg_ref)
            dkg_ref[...] = jnp.zeros_like(dkg_ref)
            dbf_ref[...] = jnp.zeros_like(dbf_ref)

        bd_v = bd_ref[...]
        dqw = jnp.transpose(dqa_ref[...])
        _, vjp_q = jax.vjp(lambda q, g: _qn_fn(q, g, bd_v), q_ref[...], qg_ref[...])
        dq, dqg = vjp_q(_compact(dqw))
        _, vjp_k = jax.vjp(lambda k, g: _kn_fn(k, g, bd_v), k_ref[...], kg_ref[...])
        dk, dkg = vjp_k(dkn_ref[...])
        dq_ref[...] = dq.astype(BF16)
        dk_ref[...] = dk.astype(BF16)
        dqg_ref[...] += dqg
        dkg_ref[...] += dkg

        dc = _lane_of_heads(dqw, COL_A) + (dck_ref[0] + dck_ref[1])
        dlogf = jnp.dot(tri_ref[...], dc, precision=HI, preferred_element_type=F32) + carry[...]
        carry[...] = dlogf[0:1, :]
        lane = lax.broadcasted_iota(jnp.int32, (tr, LANE), 1)
        xf = fb_ref[...] + bf_ref[...]
        dfb = jnp.where(lane < N_FORGET, dlogf * jax.nn.sigmoid(-xf), 0.0)
        dfb_ref[...] = dfb.astype(BF16)
        dbf_ref[...] += jnp.sum(dfb, axis=0, keepdims=True)

    rb = lambda w: pl.BlockSpec((tr, w), lambda i: (rev(i), 0))
    wn = lambda w, off: pl.BlockSpec((pl.Element(tr), pl.Element(w)), lambda i: (rev(i) * tr, off))
    return pl.pallas_call(
        body, grid=(nb,),
        in_specs=[wn(768, QB), wn(768, KB), rb(LANE), _const((1, 768)), _const((1, 768)), _const((1, LANE)),
                  _const((768, 768)), _const((tr, tr)), pl.BlockSpec((AUG, tr), lambda i: (0, rev(i))), rb(768),
                  pl.BlockSpec((2, tr, LANE), lambda i: (0, rev(i), 0))],
        out_specs=[rb(768), rb(768), rb(LANE), _const((1, 768)), _const((1, 768)), _const((1, LANE))],
        out_shape=[SDS((s, 768), BF16), SDS((s, 768), BF16), SDS((s, LANE), BF16), SDS((1, 768), F32),
                   SDS((1, 768), F32), SDS((1, LANE), F32)],
        scratch_shapes=[pltpu.VMEM((1, LANE), F32)], name="fox_bwd_post",
        compiler_params=_params(("arbitrary",)))(proj, proj, fbl, qg, kg, bfor, bd, triu, dqa, dkn, dck)


def _merge_specs(tr):
    row = lambda w: pl.BlockSpec((tr, w), lambda i, j: (i, 0))
    shard = lambda r: pl.BlockSpec((None, r, 512), lambda i, j: (j, 0, 0))
    gate = lambda b: pl.BlockSpec((tr, 512), lambda i, j: (i, (GATE + 2048 * b) // 512 + j))
    return [row(768), row(768), row(512), shard(768), shard(768), shard(512), gate(0), gate(1), gate(2)]


def _merge_fwd(proj, ga, gb, gc, wa, wb, wc, *, tr):
    s = proj.shape[0]

    def body(ga_ref, gb_ref, gc_ref, wa_ref, wb_ref, wc_ref, l0_ref, l1_ref, l2_ref, y_ref):
        ua = _dot(ga_ref[...], wa_ref[...])
        ub = _dot(gb_ref[...], wb_ref[...])
        uc = _dot(gc_ref[...], wc_ref[...])
        y = jax.nn.sigmoid(l0_ref[...]) * ua + jax.nn.sigmoid(l1_ref[...]) * ub + jax.nn.sigmoid(l2_ref[...]) * uc
        y_ref[...] = y.astype(BF16)

    return pl.pallas_call(
        body, grid=(s // tr, N_CHIPS), in_specs=_merge_specs(tr),
        out_specs=pl.BlockSpec((tr, 512), lambda i, j: (i, j)), out_shape=SDS((s, D_MODEL), BF16), name="merge_fwd",
        compiler_params=_params(("parallel", "arbitrary")))(ga, gb, gc, wa, wb, wc, proj, proj, proj)


def _merge_bwd(proj, ga, gb, gc, wa, wb, wc, dy, *, tr):
    s = proj.shape[0]

    def body(ga_ref, gb_ref, gc_ref, wa_ref, wb_ref, wc_ref, l0_ref, l1_ref, l2_ref, dy_ref,
             dl0_ref, dl1_ref, dl2_ref, dua_ref, dub_ref, duc_ref, dga_ref, dgb_ref, dgc_ref):
        j = pl.program_id(1)
        dyv = dy_ref[...]

        @pl.when(j == 0)
        def _():
            dga_ref[...] = jnp.zeros_like(dga_ref)
            dgb_ref[...] = jnp.zeros_like(dgb_ref)
            dgc_ref[...] = jnp.zeros_like(dgc_ref)

        for g_ref, w_ref, l_ref, dl_ref, du_ref, dg_ref in (
                (ga_ref, wa_ref, l0_ref, dl0_ref, dua_ref, dga_ref),
                (gb_ref, wb_ref, l1_ref, dl1_ref, dub_ref, dgb_ref),
                (gc_ref, wc_ref, l2_ref, dl2_ref, duc_ref, dgc_ref)):
            w = w_ref[...]
            u = _dot(g_ref[...], w)
            sg = jax.nn.sigmoid(l_ref[...])
            dl_ref[...] = (dyv * u * sg * (1.0 - sg)).astype(BF16)
            du = (dyv * sg).astype(BF16)
            du_ref[...] = du
            dg_ref[...] += _dot_nt(du, w)

    blk = pl.BlockSpec((tr, 512), lambda i, j: (i, j))
    row = lambda w: pl.BlockSpec((tr, w), lambda i, j: (i, 0))
    big = SDS((s, D_MODEL), BF16)
    return pl.pallas_call(
        body, grid=(s // tr, N_CHIPS), in_specs=_merge_specs(tr) + [blk],
        out_specs=[blk] * 6 + [row(768), row(768), row(512)],
        out_shape=[big] * 6 + [SDS((s, 768), F32), SDS((s, 768), F32), SDS((s, 512), F32)], name="merge_bwd",
        compiler_params=_params(("parallel", "arbitrary")))(ga, gb, gc, wa, wb, wc, proj, proj, proj, dy)


def _out_loss(y, wo, x, tgt, *, tr, tn):
    s = x.shape[0]

    def body(y_ref, w_ref, x_ref, t_ref, d_ref, db_ref, sq_ref, dy_ref):
        @pl.when((pl.program_id(0) == 0) & (pl.program_id(1) == 0))
        def _():
            sq_ref[...] = jnp.zeros_like(sq_ref)

        @pl.when(pl.program_id(1) == 0)
        def _():
            dy_ref[...] = jnp.zeros_like(dy_ref)

        w = w_ref[...]
        out = x_ref[...] + _dot(y_ref[...], w)
        diff = out - t_ref[...]
        sq_ref[...] += jnp.sum(diff * diff, axis=0, keepdims=True)
        d = diff * (1.0 / D_MODEL)
        d_ref[...] = d
        db = d.astype(BF16)
        db_ref[...] = db
        dy_ref[...] += _dot_nt(db, w)

    blk = pl.BlockSpec((tr, tn), lambda i, j: (i, j))
    row = pl.BlockSpec((tr, D_MODEL), lambda i, j: (i, 0))
    return pl.pallas_call(
        body, grid=(s // tr, D_MODEL // tn),
        in_specs=[row, pl.BlockSpec((D_MODEL, tn), lambda i, j: (0, j)), blk, blk],
        out_specs=[blk, blk, _const((1, tn)), row],
        out_shape=[SDS((s, D_MODEL), F32), SDS((s, D_MODEL), BF16), SDS((1, tn), F32), SDS((s, D_MODEL), F32)],
        name="out_loss", compiler_params=_params(("arbitrary", "arbitrary")))(y, wo, x, tgt)


def _tile_gain(g, reps):
    return jnp.tile(g.reshape(1, -1), (1, reps))


def _pad_lane(v):
    v = v.reshape(1, -1)
    return jnp.pad(v, ((0, 0), (0, LANE - v.shape[1])))


def _local_step(x, mem, tgt, w_main, w_fb, w_small, norm_gain, mem_norm_gain, b_forget,
                q_gain_a, k_gain_a, sinks_a, q_gain_b, k_gain_b, q_gain_c, k_gain_c, core=None):
    s = x.shape[0]
    tr = min(512, s)
    bd64 = _block_diag(768, HEAD_DIM)
    bd128 = _block_diag(512, C_HEAD_DIM)
    hsum = _head_sum(768, HEAD_DIM)
    qga, kga = _tile_gain(q_gain_a, 12), _tile_gain(k_gain_a, 4)
    qgb, kgb = _tile_gain(q_gain_b, 12), _tile_gain(k_gain_b, 12)
    qgc, kgc = _tile_gain(q_gain_c, 4), _tile_gain(k_gain_c, 4)
    sinks = _pad_lane(sinks_a)
    bfor = _pad_lane(b_forget)

    hn = _rms_fwd(x, norm_gain, tr=tr, name="rms_x")
    on_mesh = core is not None
    if on_mesh:
        proj, (gathered,) = _matmul(hn, w_main, dims="nn", out_dtype=F32, tm=1024, tn=1024, tk=D_MODEL, name="proj_main",
                                    comms=[_gather_comm(list(w_small))])
        w_mk, wa, wb, wc, wo = gathered
        w_mk, wo = w_mk.reshape(D_MODEL, 1024), wo.reshape(D_MODEL, D_MODEL)
    else:
        proj = _matmul(hn, w_main, dims="nn", out_dtype=F32, tm=1024, tn=1024, tk=D_MODEL, name="proj_main")
        w_mk, wa, wb, wc, wo = w_small
    fbl = _matmul(hn, w_fb, dims="nn", out_dtype=F32, tm=1024, tn=LANE, tk=D_MODEL, name="proj_forget")
    memn = _rms_fwd(mem, mem_norm_gain, tr=mem.shape[0], name="rms_mem")
    mkv = _matmul(memn, w_mk, dims="nn", out_dtype=F32, tm=256, tn=512, tk=D_MODEL, name="mem_kv")

    swa_bias = _swa_bias()
    ga = _swa_fwd(proj, qga, kga, sinks, bd64, swa_bias)
    ea, eb, ones_a, ones_b = _expand_mats()
    tf = min(256, s)
    qat, ka, kat, va, vat, qn, cfox = _fox2_prep(proj, fbl, qgb, kgb, bfor, bd64, ea, eb, ones_a, ones_b, tr=tf)
    gb, yb, lse = _fox2_fwd(proj, qat, ka, vat)
    gc = _mem_fwd(proj, mkv, qgc, kgc, bd128, tr=tr)
    y = _merge_fwd(proj, ga, gb, gc, wa, wb, wc, tr=tr)
    dout, dout_b, sq, dy = _out_loss(y, wo, x, tgt, tr=tr, tn=512)

    d_wo = _matmul(y, dout_b, dims="tn", out_dtype=F32, tm=1024, tn=512, tk=4096, name="dw_out")
    dl0, dl1, dl2, dua, dub, duc, dga, dgb, dgc = _merge_bwd(proj, ga, gb, gc, wa, wb, wc, dy, tr=tr)
    d_wa = _matmul(ga, dua, dims="tn", out_dtype=F32, tm=768, tn=512, tk=4096, name="dw_branch_a")
    d_wb = _matmul(gb, dub, dims="tn", out_dtype=F32, tm=768, tn=512, tk=4096, name="dw_branch_b")
    d_wc = _matmul(gc, duc, dims="tn", out_dtype=F32, tm=512, tn=512, tk=4096, name="dw_branch_c")

    dproj_a, d_qga, d_kga, d_sinks = _swa_bwd(proj, qga, kga, sinks, bd64, swa_bias, dga)

    qab, qabt, dya, dyat, dzb = _fox2_bwd_pre(proj, yb, dgb, qn, cfox, lse, hsum, ea, ones_b, tr=tf)
    dqa, dkn, dvb, dck = _fox2_bwd(qab, qabt, ka, kat, va, dya, dyat)
    dqb, dkb, dfb, d_qgb, d_kgb, d_bf = _fox2_bwd_post(proj, fbl, qgb, kgb, bfor, bd64, dqa, dkn, dck, tr=tf)

    dproj_c, dmkv, d_qgc, d_kgc = _mem_bwd(proj, mkv, qgc, kgc, bd128, dgc, tr=tr)
    dmkv_b = dmkv.astype(BF16)
    d_wmk = _matmul(memn, dmkv_b, dims="tn", out_dtype=F32, tm=1024, tn=512, tk=256, name="dw_mem_kv")
    dmemn = _matmul(dmkv_b, w_mk, dims="nt", out_dtype=F32, tm=256, tn=512, tk=1024, name="dmemn")
    (d_mem_gain,) = _rms_bwd(mem, mem_norm_gain, dmemn, None, tr=mem.shape[0], name="rms_mem_bwd")

    dproj = [dproj_a, jnp.concatenate([dqb, dkb, dvb, dzb, dproj_c], axis=1), dl0, dl1, dl2]
    dhn_f = _matmul(dfb, w_fb, dims="nt", out_dtype=F32, tm=1024, tn=512, tk=LANE, name="dhn_forget")
    d_wfb = _matmul(hn, dfb, dims="tn", out_dtype=F32, tm=1024, tn=LANE, tk=512, name="dw_forget")
    big = {}
    if on_mesh:
        g1, k1 = [d_wmk, d_wa, d_wb, d_wc, d_wo], [2, 3, 4, 5, 6]
        d_other, (got1,) = _matmul(hn, dproj, dims="tn", out_dtype=BF16, tm=1024, tn=512, tk=4096, a_half=(core, True),
                                   name="dw_main_other", comms=[_exchange_comm(g1, k1)])
        h1 = _add_half(g1, got1, core, [HALF_AXIS[k] for k in k1], name="add_half_small")
        d_own, (got0, parts1) = _matmul(
            hn, dproj, dims="tn", out_dtype=F32, tm=1024, tn=512, tk=4096, a_half=(core, False), name="dw_main_own",
            comms=[_exchange_comm([d_other, d_wfb], [0, 1], whole=(0,)), _scatter_comm(h1, k1)])
        h0 = [_add_pair(d_own, got0[0], name="add_pair_main")] + _add_half([d_wfb], [got0[1]], core, [0], name="add_half_forget")
        sums1 = _sum4(parts1, name="sum4_small")
        dhn, (parts0, theirs1) = _matmul(dproj, w_main, dims="nt", out_dtype=F32, tm=1024, tn=512, tk=2048, vmem=VMEM_WIDE, name="dhn",
                                         add=dhn_f, comms=[_scatter_comm(h0, [0, 1]), _swap_comm(sums1)])
        sums0 = _sum4(parts0, name="sum4_main")
        (grad_x, d_gain), theirs0 = _rms_bwd(x, norm_gain, dhn, dout, tr=tr, name="rms_x_bwd", comm=_swap_comm(sums0))
        big = dict(sums=sums0 + sums1, theirs=list(theirs0) + list(theirs1))
    else:
        dhn = _matmul(dproj, w_main, dims="nt", out_dtype=F32, tm=1024, tn=512, tk=2048, vmem=VMEM_WIDE, name="dhn", add=dhn_f)
        d_wmain = _matmul(hn, dproj, dims="tn", out_dtype=F32, tm=1024, tn=512, tk=4096, name="dw_main")
        big = dict(d_wmain=d_wmain, d_wfb=d_wfb, d_wmk=d_wmk, d_wa=d_wa, d_wb=d_wb, d_wc=d_wc, d_wo=d_wo)
        grad_x, d_gain = _rms_bwd(x, norm_gain, dhn, dout, tr=tr, name="rms_x_bwd")

    fold = lambda g, reps: jnp.sum(g.reshape(reps, -1), axis=0, keepdims=True)
    return dict(
        sq=sq, grad_x=grad_x, **big,
        d_gain=d_gain, d_mem_gain=d_mem_gain, d_bf=d_bf[:, :N_FORGET],
        d_qga=fold(d_qga, 12), d_kga=fold(d_kga, 4), d_sinks=d_sinks[:, :A_HEADS],
        d_qgb=fold(d_qgb, 12), d_kgb=fold(d_kgb, 12), d_qgc=fold(d_qgc, 4), d_kgc=fold(d_kgc, 4))


PACK_ROWS = 256
FORGET_IN_SHARD = FORGET_COL - SHARD_COLS
AFTER_FORGET = FORGET_COL - SLAB_START[1]
END_CHIP1 = 2 * SHARD_COLS - N_FORGET - SLAB_START[1]


def _pack_w_in(chip, w):
    rows = w.shape[1]
    tr = PACK_ROWS

    def body(k_ref, w_ref, o_ref, scr):
        scr[...] = jnp.zeros_like(scr)
        scr[pl.ds(0, SHARD_COLS), :] = w_ref[...]
        v = jnp.transpose(scr[...])
        k = k_ref[0]
        col = lax.broadcasted_iota(jnp.int32, (tr, SLAB), 1)
        no_forget = jnp.zeros((tr, LANE), BF16)

        @pl.when(k == 0)
        def _():
            o_ref[:, 0:SLAB] = v.astype(BF16)
            o_ref[:, SLAB:] = no_forget

        @pl.when(k == 1)
        def _():
            before = pltpu.roll(v, SLAB_SHIFT[1], axis=1)
            after = pltpu.roll(v, SLAB - (N_FORGET - SLAB_SHIFT[1]), axis=1)
            slab = jnp.where(col < AFTER_FORGET, before, jnp.where(col < END_CHIP1, after, 0.0))
            o_ref[:, 0:SLAB] = slab.astype(BF16)
            f = pltpu.roll(v, SLAB - FORGET_IN_SHARD, axis=1)[:, :LANE]
            o_ref[:, SLAB:] = jnp.where(col[:, :LANE] < N_FORGET, f, 0.0).astype(BF16)

        for kk in (2, 3):
            @pl.when(k == kk)
            def _(kk=kk):
                o_ref[:, 0:SLAB] = pltpu.roll(v, SLAB_SHIFT[kk], axis=1).astype(BF16)
                o_ref[:, SLAB:] = no_forget

    return pl.pallas_call(
        body, grid_spec=pltpu.PrefetchScalarGridSpec(
            num_scalar_prefetch=1, grid=(rows // tr,),
            in_specs=[pl.BlockSpec((SHARD_COLS, tr), lambda i, k: (0, i))],
            out_specs=pl.BlockSpec((None, tr, SLAB + LANE), lambda i, k: (k[0], i, 0)),
            scratch_shapes=[pltpu.VMEM((SLAB, tr), F32)]),
        out_shape=SDS((N_CHIPS, rows, SLAB + LANE), BF16), name="pack_w_in",
        compiler_params=_params(("arbitrary",)))(chip, w)


def _merge_slabs(g):
    rows = g.shape[1]
    tr = PACK_ROWS
    t = [s // LANE for s in SLAB_START]
    n_t = SLAB // LANE

    def body(g_ref, m_ref, f_ref):
        for k in range(N_CHIPS):
            lo = t[k] + (1 if k > 0 else 0)
            hi = t[k + 1] if k + 1 < N_CHIPS else t[k] + n_t
            m_ref[:, lo * LANE:hi * LANE] = g_ref[k, :, (lo - t[k]) * LANE:(hi - t[k]) * LANE]
            if k + 1 < N_CHIPS:
                a = g_ref[k, :, (hi - t[k]) * LANE:(hi - t[k] + 1) * LANE].astype(F32)
                b = g_ref[k + 1, :, 0:LANE].astype(F32)
                m_ref[:, hi * LANE:(hi + 1) * LANE] = (a + b).astype(BF16)
        f_ref[...] = g_ref[1, :, SLAB:]

    return pl.pallas_call(
        body, grid=(rows // tr,),
        in_specs=[pl.BlockSpec((N_CHIPS, tr, SLAB + LANE), lambda i: (0, i, 0))],
        out_specs=[_rowblk(tr, P_MAIN), _rowblk(tr, LANE)],
        out_shape=[SDS((rows, P_MAIN), BF16), SDS((rows, LANE), BF16)], name="merge_slabs",
        compiler_params=_params(("parallel",)))(g)


def _adamw_math(w, g, m, v):
    nm = ADAM_B1 * m + (1.0 - ADAM_B1) * g
    nv = ADAM_B2 * v + (1.0 - ADAM_B2) * (g * g)
    m_hat = nm / (1.0 - ADAM_B1 ** ADAM_STEP)
    v_hat = nv / (1.0 - ADAM_B2 ** ADAM_STEP)
    delta = -ADAM_LR * (m_hat / (jnp.sqrt(v_hat) + ADAM_EPS) + ADAM_WD * w)
    return delta, nm, nv


def _adamw(g, w, m, v, *, tr, name):
    rows, cols = w.shape
    tr = min(tr, rows)

    def body(g_ref, w_ref, m_ref, v_ref, d_ref, nm_ref, nv_ref):
        d, nm, nv = _adamw_math(w_ref[...], g_ref[...], m_ref[...], v_ref[...])
        d_ref[...] = d
        nm_ref[...] = nm
        nv_ref[...] = nv

    spec = _rowblk(tr, cols)
    return pl.pallas_call(
        body, grid=(rows // tr,), in_specs=[spec] * 4, out_specs=[spec] * 3,
        out_shape=[SDS((rows, cols), F32)] * 3, name=name, compiler_params=_params(("parallel",)))(g, w, m, v)


def _adamw_w_in(chip_core, slab_mine, slab_theirs, forget_mine, forget_theirs, w, m, v):
    rows = w.shape[1]
    tr = PACK_ROWS // 2
    nbh = rows // 2 // tr

    def body(k_ref, sa_ref, sb_ref, fa_ref, fb_ref, w_ref, m_ref, v_ref, g_ref, d_ref, nm_ref, nv_ref):
        use_mine = pl.program_id(0) // nbh == k_ref[1]
        sl = jnp.where(use_mine, sa_ref[...], sb_ref[...])
        f_tile = jnp.where(use_mine, fa_ref[...], fb_ref[...])
        k = k_ref[0]

        def emit(wide):
            g = jnp.transpose(wide)[:SHARD_COLS, :]
            g_ref[...] = g
            d, nm, nv = _adamw_math(w_ref[...], g, m_ref[...], v_ref[...])
            d_ref[...] = d
            nm_ref[...] = nm
            nv_ref[...] = nv

        @pl.when(k == 0)
        def _():
            emit(sl)

        @pl.when(k == 1)
        def _():
            col = lax.broadcasted_iota(jnp.int32, (tr, SLAB), 1)
            before = pltpu.roll(sl, SLAB - SLAB_SHIFT[1], axis=1)
            after = pltpu.roll(sl, N_FORGET - SLAB_SHIFT[1], axis=1)
            wide_f = jnp.concatenate([f_tile, jnp.zeros((tr, SLAB - LANE), F32)], axis=1)
            forget = pltpu.roll(wide_f, FORGET_IN_SHARD, axis=1)
            emit(jnp.where(col < FORGET_IN_SHARD, before, jnp.where(col < FORGET_IN_SHARD + N_FORGET, forget, after)))

        for kk in (2, 3):
            @pl.when(k == kk)
            def _(kk=kk):
                emit(pltpu.roll(sl, SLAB - SLAB_SHIFT[kk], axis=1))

    nat = pl.BlockSpec((SHARD_COLS, tr), lambda i, k: (0, i))
    half = lambda width: pl.BlockSpec((tr, width), lambda i, k: (i % nbh, 0))
    return pl.pallas_call(
        body, grid_spec=pltpu.PrefetchScalarGridSpec(
            num_scalar_prefetch=1, grid=(rows // tr,),
            in_specs=[half(SLAB), half(SLAB), half(LANE), half(LANE), nat, nat, nat],
            out_specs=[nat] * 4),
        out_shape=[SDS((SHARD_COLS, rows), F32)] * 4, name="adamw_w_in",
        compiler_params=_params(("arbitrary",)))(chip_core, slab_mine, slab_theirs, forget_mine, forget_theirs, w, m, v)


ANY = pl.BlockSpec(memory_space=pl.ANY)
HALF_AXIS = (0, 0, 1, 0, 0, 0, 1)


def _me():
    return lax.axis_index("x"), lax.axis_index("y"), lax.axis_index("c")


def _half(ref, which, axis):
    n = ref.shape[axis] // 2
    sl = pl.ds(which * n, n)
    return ref.at[sl] if axis == 0 else ref.at[:, sl]


def _piece(t, ref, j):
    if t == 0:
        return ref.at[:, pl.ds(SLAB_START[j], SLAB)]
    if t == 1:
        return ref
    if t in (2, 6):
        return ref.at[pl.ds(512 * j, 512)]
    return ref.at[:, pl.ds(512 * j, 512)]


def _piece_shape(t, shape):
    if t == 0:
        return (shape[0], SLAB)
    if t == 1:
        return shape
    if t in (2, 6):
        return (512, shape[1])
    return (shape[0], 512)


def _gather_plan(ins, outs, own_slot_in_src):
    x, y, c = _me()
    k = 2 * x + y
    sib = (x, y, 1 - c)
    chips = [(1 - x, y), (x, 1 - y), (1 - x, 1 - y)]
    n = len(outs)

    def rows(t, which):
        h = outs[t].shape[1] // 2
        return pl.ds(which * h, h)

    def mine(t):
        return ins[t].at[k, rows(t, c)] if own_slot_in_src else ins[t].at[rows(t, c)]

    def first(t, j, sems):
        chip = chips[j]
        return pltpu.make_async_remote_copy(
            src_ref=mine(t), dst_ref=outs[t].at[k, rows(t, c)], send_sem=sems[0].at[t, j], recv_sem=sems[1].at[t, j],
            device_id=(chip[0], chip[1], c), device_id_type=MESH)

    def landed(t, j, sems):
        chip = chips[j]
        return pltpu.make_async_remote_copy(
            src_ref=mine(t), dst_ref=outs[t].at[2 * chip[0] + chip[1], rows(t, c)], send_sem=sems[0].at[t, j],
            recv_sem=sems[1].at[t, j], device_id=(chip[0], chip[1], c), device_id_type=MESH)

    def passed(t, j, which, sems):
        chip = chips[j]
        blk = outs[t].at[2 * chip[0] + chip[1], rows(t, which)]
        return pltpu.make_async_remote_copy(
            src_ref=blk, dst_ref=blk, send_sem=sems[2].at[t, j], recv_sem=sems[3].at[t, j], device_id=sib,
            device_id_type=MESH)

    def start(sems):
        for j in range(3):
            for t in range(n):
                first(t, j, sems).start()

    def finish(sems):
        for j in range(3):
            for t in range(n):
                landed(t, j, sems).wait_recv()
                passed(t, j, c, sems).start()
        for j in range(3):
            for t in range(n):
                passed(t, j, 1 - c, sems).wait_recv()
        for j in range(3):
            for t in range(n):
                first(t, j, sems).wait_send()
                passed(t, j, c, sems).wait_send()

    return k, start, finish


def _all_gather_slabs(slabs):
    def body(in_ref, out_ref, nbr_sem, quarter_sem, pass_sem):
        x, y, c = _me()
        k = 2 * x + y
        rows = out_ref.shape[1]
        h, q = rows // 2, rows // 4
        nbrs = [(1 - x, y), (x, 1 - y)]
        slot = lambda chip: 2 * chip[0] + chip[1]
        diag = 2 * (1 - x) + (1 - y)
        half = pl.ds(c * h, h)
        quarter = lambda a: pl.ds(c * h + a * q, q)

        def first(a):
            return pltpu.make_async_remote_copy(
                src_ref=in_ref.at[k, half], dst_ref=out_ref.at[k, half], send_sem=nbr_sem.at[0, a],
                recv_sem=nbr_sem.at[1, a], device_id=(nbrs[a][0], nbrs[a][1], c), device_id_type=MESH)

        def landed(a):
            blk = out_ref.at[slot(nbrs[a]), half]
            return pltpu.make_async_remote_copy(
                src_ref=blk, dst_ref=blk, send_sem=nbr_sem.at[0, a], recv_sem=nbr_sem.at[1, a],
                device_id=(nbrs[a][0], nbrs[a][1], c), device_id_type=MESH)

        def relay(a):
            blk = out_ref.at[slot(nbrs[a]), quarter(a)]
            to = nbrs[1 - a]
            return pltpu.make_async_remote_copy(
                src_ref=blk, dst_ref=blk, send_sem=quarter_sem.at[0, a], recv_sem=quarter_sem.at[1, a],
                device_id=(to[0], to[1], c), device_id_type=MESH)

        def relayed(a):
            blk = out_ref.at[diag, quarter(a)]
            frm = nbrs[1 - a]
            return pltpu.make_async_remote_copy(
                src_ref=blk, dst_ref=blk, send_sem=quarter_sem.at[0, a], recv_sem=quarter_sem.at[1, a],
                device_id=(frm[0], frm[1], c), device_id_type=MESH)

        def passed(j, which):
            sl = diag if j == 2 else slot(nbrs[j])
            blk = out_ref.at[sl, pl.ds(which * h, h)]
            return pltpu.make_async_remote_copy(
                src_ref=blk, dst_ref=blk, send_sem=pass_sem.at[0, j], recv_sem=pass_sem.at[1, j],
                device_id=(x, y, 1 - c), device_id_type=MESH)

        for a in range(2):
            first(a).start()
        for a in range(2):
            landed(a).wait_recv()
            relay(a).start()
            passed(a, c).start()
        for a in range(2):
            relayed(a).wait_recv()
        passed(2, c).start()
        for j in range(3):
            passed(j, 1 - c).wait_recv()
        for a in range(2):
            first(a).wait_send()
            relay(a).wait_send()
        for j in range(3):
            passed(j, c).wait_send()

    return pl.pallas_call(
        body, in_specs=[ANY], out_specs=ANY, out_shape=SDS(slabs.shape, slabs.dtype),
        scratch_shapes=[pltpu.SemaphoreType.DMA((2, 2)), pltpu.SemaphoreType.DMA((2, 2)), pltpu.SemaphoreType.DMA((2, 3))],
        input_output_aliases={0: 0}, name="all_gather_slabs")(slabs)


def _gather_comm(parts):
    n = len(parts)

    def start(ins, outs, sems):
        k, go, _ = _gather_plan(ins, outs, False)
        for t in range(n):
            pltpu.make_async_copy(ins[t], outs[t].at[k], sems[4].at[t]).start()
        go(sems)

    def finish(ins, outs, sems):
        k, _, done = _gather_plan(ins, outs, False)
        done(sems)
        for t in range(n):
            pltpu.make_async_copy(ins[t], outs[t].at[k], sems[4].at[t]).wait()

    return _Comm(parts, [SDS((N_CHIPS,) + p.shape, p.dtype) for p in parts],
                 [pltpu.SemaphoreType.DMA((n, 3))] * 4 + [pltpu.SemaphoreType.DMA((n,))], start, finish)


def _exchange_comm(arrs, kinds, whole=()):
    n = len(arrs)

    def copies(ins, outs, sems):
        x, y, c = _me()
        return [pltpu.make_async_remote_copy(
            src_ref=ins[t] if t in whole else _half(ins[t], 1 - c, HALF_AXIS[kinds[t]]), dst_ref=outs[t],
            send_sem=sems[0].at[t], recv_sem=sems[1].at[t], device_id=(x, y, 1 - c), device_id_type=MESH)
            for t in range(n)]

    def start(ins, outs, sems):
        for cp in copies(ins, outs, sems):
            cp.start()

    def finish(ins, outs, sems):
        for cp in copies(ins, outs, sems):
            cp.wait()

    def hshape(t):
        s = list(arrs[t].shape)
        if t not in whole:
            s[HALF_AXIS[kinds[t]]] //= 2
        return SDS(tuple(s), arrs[t].dtype)

    return _Comm(arrs, [hshape(t) for t in range(n)], [pltpu.SemaphoreType.DMA((n,))] * 2, start, finish)


ADD_HALF_STEPS = 2


def _add_half(fulls, gots, core, axes, *, name):
    ns = ADD_HALF_STEPS
    n = len(fulls)
    in_specs, out_specs, out_shape = [], [], []
    for got, axis in zip(gots, axes):
        r, c = got.shape
        blk = (r // ns, c)
        if axis == 0:
            in_specs.append(pl.BlockSpec(blk, lambda i, cr: (i + cr[0] * ns, 0)))
        else:
            in_specs.append(pl.BlockSpec(blk, lambda i, cr: (i, cr[0])))
        out_specs.append(pl.BlockSpec(blk, lambda i, cr: (i, 0)))
        out_shape.append(SDS((r, c), BF16))
    in_specs += list(out_specs)

    def body(c_ref, *refs):
        for a_ref, b_ref, o_ref in zip(refs[:n], refs[n:2 * n], refs[2 * n:]):
            o_ref[...] = (a_ref[...] + b_ref[...]).astype(BF16)

    return list(pl.pallas_call(
        body, grid_spec=pltpu.PrefetchScalarGridSpec(num_scalar_prefetch=1, grid=(ns,), in_specs=in_specs,
                                                     out_specs=out_specs),
        out_shape=out_shape, name=name, compiler_params=_params(("parallel",)))(core, *fulls, *gots))


def _add_pair(a, b, *, name):
    r, c = a.shape
    br, bc = 256, min(2048, c)

    def body(a_ref, b_ref, o_ref):
        o_ref[...] = (a_ref[...] + b_ref[...].astype(F32)).astype(BF16)

    spec = pl.BlockSpec((br, bc), lambda i, j: (i, j))
    return pl.pallas_call(body, grid=(r // br, c // bc), in_specs=[spec, spec], out_specs=spec,
                          out_shape=SDS((r, c), BF16), name=name, compiler_params=_params(("parallel", "parallel")))(a, b)


def _scatter_comm(halves, kinds):
    n = len(halves)

    def plan(ins, outs, sems):
        send, recv, lsem = sems
        x, y, c = _me()
        k = 2 * x + y

        def to_chip(t, j):
            return pltpu.make_async_remote_copy(
                src_ref=_piece(kinds[t], ins[t], j), dst_ref=outs[t].at[k], send_sem=send.at[t, j],
                recv_sem=recv.at[t, k], device_id=(j // 2, j % 2, c), device_id_type=MESH)

        def from_chip(t, j):
            return pltpu.make_async_remote_copy(
                src_ref=_piece(kinds[t], ins[t], j), dst_ref=outs[t].at[j], send_sem=send.at[t, j],
                recv_sem=recv.at[t, j], device_id=(j // 2, j % 2, c), device_id_type=MESH)

        def own(t, j):
            return pltpu.make_async_copy(_piece(kinds[t], ins[t], j), outs[t].at[j], lsem.at[t])

        return k, to_chip, from_chip, own

    def start(ins, outs, sems):
        k, to_chip, _, own = plan(ins, outs, sems)
        for j in range(N_CHIPS):
            @pl.when(k != j)
            def _(j=j):
                for t in range(n):
                    to_chip(t, j).start()

            @pl.when(k == j)
            def _(j=j):
                for t in range(n):
                    own(t, j).start()

    def finish(ins, outs, sems):
        k, to_chip, from_chip, own = plan(ins, outs, sems)
        for j in range(N_CHIPS):
            @pl.when(k != j)
            def _(j=j):
                for t in range(n):
                    from_chip(t, j).wait_recv()
                for t in range(n):
                    to_chip(t, j).wait_send()

            @pl.when(k == j)
            def _(j=j):
                for t in range(n):
                    own(t, j).wait()

    return _Comm(halves, [SDS((N_CHIPS,) + _piece_shape(kinds[t], halves[t].shape), halves[t].dtype) for t in range(n)],
                 [pltpu.SemaphoreType.DMA((n, N_CHIPS))] * 2 + [pltpu.SemaphoreType.DMA((n,))], start, finish)


SUM4_STEPS = 2


def _sum4(ps, *, name):
    ns = SUM4_STEPS

    def body(*refs):
        for p_ref, o_ref in zip(refs[:len(ps)], refs[len(ps):]):
            o_ref[...] = ((p_ref[0].astype(F32) + p_ref[1].astype(F32)) + p_ref[2].astype(F32)) + p_ref[3].astype(F32)

    return pl.pallas_call(
        body, grid=(ns,),
        in_specs=[pl.BlockSpec((N_CHIPS, p.shape[1] // ns, p.shape[2]), lambda i: (0, i, 0)) for p in ps],
        out_specs=[_rowblk(p.shape[1] // ns, p.shape[2]) for p in ps],
        out_shape=[SDS(p.shape[1:], F32) for p in ps], name=name, compiler_params=_params(("parallel",)))(*ps)


def _swap_comm(sums):
    return _exchange_comm(sums, [None] * len(sums), whole=tuple(range(len(sums))))


ADAMW_STEPS = 4


def _adamw_halves(items, core):
    ns = ADAMW_STEPS
    nbh = ns // 2
    n = len(items)
    in_specs, out_specs, out_shape, ins = [], [], [], []
    for mine, theirs, w, m, v, axis in items:
        rows, cols = w.shape
        tr = rows // ns
        if axis == 0:
            g_spec = pl.BlockSpec((tr, cols), lambda i, cr: (i % nbh, 0))
        else:
            g_spec = pl.BlockSpec((tr, cols // 2), lambda i, cr: (i, 0))
        nat = pl.BlockSpec((tr, cols), lambda i, cr: (i, 0))
        in_specs += [g_spec, g_spec, nat, nat, nat]
        out_specs += [nat] * 4
        out_shape += [SDS((rows, cols), F32)] * 4
        ins += [mine, theirs, w, m, v]

    def body(c_ref, *refs):
        for t, item in enumerate(items):
            a_ref, b_ref, w_ref, m_ref, v_ref = refs[5 * t:5 * t + 5]
            g_ref, d_ref, nm_ref, nv_ref = refs[5 * n + 4 * t:5 * n + 4 * t + 4]
            a, b = a_ref[...], b_ref[...]
            if item[5] == 0:
                g = jnp.where(pl.program_id(0) // nbh == c_ref[0], a, b)
            else:
                low = c_ref[0] == 0
                g = jnp.concatenate([jnp.where(low, a, b), jnp.where(low, b, a)], axis=1)
            g_ref[...] = g
            d, nm, nv = _adamw_math(w_ref[...], g, m_ref[...], v_ref[...])
            d_ref[...] = d
            nm_ref[...] = nm
            nv_ref[...] = nv

    res = pl.pallas_call(
        body, grid_spec=pltpu.PrefetchScalarGridSpec(
            num_scalar_prefetch=1, grid=(ns,), in_specs=in_specs, out_specs=out_specs),
        out_shape=out_shape, name="adamw_shards", compiler_params=_params(("arbitrary",)))(core, *ins)
    return [tuple(res[4 * t:4 * t + 4]) for t in range(n)]


SMALL_ROWS, SMALL_COLS = 8, 1024


def _pack_small(vs):
    flat = jnp.concatenate([v.reshape(-1) for v in vs])
    return jnp.pad(flat, (0, SMALL_ROWS * SMALL_COLS - flat.shape[0])).reshape(SMALL_ROWS, SMALL_COLS)


def _unpack_small(packed, sizes):
    flat = packed.reshape(-1)
    out, o = [], 0
    for n in sizes:
        out.append(flat[o:o + n].reshape(1, n))
        o += n
    return out


def _all_reduce_small(v):
    n_dev = 8

    def body(v_ref, o_ref, land, send, recv):
        x, y, c = _me()
        me = 4 * x + 2 * y + c
        land[me] = v_ref[...]
        cps = []
        for r in range(1, n_dev):
            fx, fy, fc = (r >> 2) & 1, (r >> 1) & 1, r & 1
            peer = (x ^ fx, y ^ fy, c ^ fc)
            cps.append(pltpu.make_async_remote_copy(
                src_ref=v_ref, dst_ref=land.at[me], send_sem=send.at[r - 1], recv_sem=recv.at[r - 1],
                device_id=peer, device_id_type=MESH))
        for cp in cps:
            cp.start()
        for r in range(1, n_dev):
            fx, fy, fc = (r >> 2) & 1, (r >> 1) & 1, r & 1
            src = 4 * (x ^ fx) + 2 * (y ^ fy) + (c ^ fc)
            pltpu.make_async_remote_copy(
                src_ref=v_ref, dst_ref=land.at[src], send_sem=send.at[r - 1], recv_sem=recv.at[r - 1],
                device_id=(x ^ fx, y ^ fy, c ^ fc), device_id_type=MESH).wait_recv()
        for cp in cps:
            cp.wait_send()
        acc = land[0]
        for r in range(1, n_dev):
            acc = acc + land[r]
        o_ref[...] = acc

    vm = pl.BlockSpec(memory_space=pltpu.VMEM)
    return pl.pallas_call(
        body, in_specs=[vm], out_specs=vm, out_shape=SDS(v.shape, F32),
        scratch_shapes=[pltpu.VMEM((n_dev,) + v.shape, F32), pltpu.SemaphoreType.DMA((n_dev - 1,)),
                        pltpu.SemaphoreType.DMA((n_dev - 1,))],
        name="all_reduce_small")(v)


def kernel(x, mem, norm_gain, mem_norm_gain, w_in, b_forget, q_gain_a, k_gain_a, sinks_a, q_gain_b, k_gain_b, q_gain_c, k_gain_c, w_mem_kv, w_branch_a, w_branch_b, w_branch_c, w_out, loss_target, m_norm_gain, m_mem_norm_gain, m_w_in, m_b_forget, m_q_gain_a, m_k_gain_a, m_sinks_a, m_q_gain_b, m_k_gain_b, m_q_gain_c, m_k_gain_c, m_w_mem_kv, m_w_branch_a, m_w_branch_b, m_w_branch_c, m_w_out, v_norm_gain, v_mem_norm_gain, v_w_in, v_b_forget, v_q_gain_a, v_k_gain_a, v_sinks_a, v_q_gain_b, v_k_gain_b, v_q_gain_c, v_k_gain_c, v_w_mem_kv, v_w_branch_a, v_w_branch_b, v_w_branch_c, v_w_out):
    xi, yi, ci = lax.axis_index("x"), lax.axis_index("y"), lax.axis_index("c")
    chip = jnp.reshape(2 * xi + yi, (1,)).astype(jnp.int32)
    core = jnp.reshape(ci, (1,)).astype(jnp.int32)

    slabs = _pack_w_in(chip, jnp.transpose(w_in[0]))
    mine = [w_mem_kv[0].astype(BF16), w_branch_a[0].astype(BF16), w_branch_b[0].astype(BF16),
            w_branch_c[0].astype(BF16), w_out[0].astype(BF16)]
    w_main, w_fb = _merge_slabs(_all_gather_slabs(slabs))

    r = _local_step(x[0], mem[0], loss_target[0], w_main, w_fb, mine, norm_gain, mem_norm_gain,
                    b_forget, q_gain_a, k_gain_a, sinks_a, q_gain_b, k_gain_b, q_gain_c, k_gain_c, core=core)
    sums, theirs = r["sums"], r["theirs"]

    small_names = ["d_gain", "d_mem_gain", "d_bf", "d_qga", "d_kga", "d_sinks", "d_qgb", "d_kgb", "d_qgc", "d_kgc"]
    loss_part = (0.5 / D_MODEL) * jnp.sum(r["sq"], axis=1, keepdims=True)
    packed = _pack_small([r[n] for n in small_names] + [loss_part])
    red = _all_reduce_small(packed)
    small_w = [norm_gain, mem_norm_gain, b_forget, q_gain_a, k_gain_a, sinks_a, q_gain_b, k_gain_b, q_gain_c, k_gain_c]
    small_m = [m_norm_gain, m_mem_norm_gain, m_b_forget, m_q_gain_a, m_k_gain_a, m_sinks_a, m_q_gain_b, m_k_gain_b,
               m_q_gain_c, m_k_gain_c]
    small_v = [v_norm_gain, v_mem_norm_gain, v_b_forget, v_q_gain_a, v_k_gain_a, v_sinks_a, v_q_gain_b, v_k_gain_b,
               v_q_gain_c, v_k_gain_c]
    sizes = [w.shape[1] for w in small_w]
    s_d, s_m, s_v = _adamw(red, _pack_small(small_w), _pack_small(small_m), _pack_small(small_v), tr=8, name="adamw_small")
    g_small = _unpack_small(red, sizes + [1])
    loss = g_small[-1].reshape(())
    d_small, m_small, v_small = _unpack_small(s_d, sizes), _unpack_small(s_m, sizes), _unpack_small(s_v, sizes)

    gw_in, dw_in, mw_in, vw_in = _adamw_w_in(jnp.concatenate([chip, core]), sums[0], theirs[0], sums[1], theirs[1],
                                             jnp.transpose(w_in[0]), jnp.transpose(m_w_in[0]), jnp.transpose(v_w_in[0]))
    shards = ((2, "w_mem_kv", w_mem_kv, m_w_mem_kv, v_w_mem_kv),
              (3, "w_branch_a", w_branch_a, m_w_branch_a, v_w_branch_a),
              (4, "w_branch_b", w_branch_b, m_w_branch_b, v_w_branch_b),
              (5, "w_branch_c", w_branch_c, m_w_branch_c, v_w_branch_c),
              (6, "w_out", w_out, m_w_out, v_w_out))
    done = _adamw_halves([(sums[t], theirs[t], w[0], m[0], v[0], HALF_AXIS[t]) for t, _, w, m, v in shards], core)
    big = {nm: res for (_, nm, _, _, _), res in zip(shards, done)}

    def collect(kind):
        sm = (g_small, d_small, m_small, v_small)[kind]
        win = (gw_in, dw_in, mw_in, vw_in)[kind]
        return ([sm[0], sm[1], jnp.transpose(win)[None]] + [a for a in sm[2:10]]
                + [big[n][kind][None] for n in ("w_mem_kv", "w_branch_a", "w_branch_b", "w_branch_c", "w_out")])

    return (loss, r["grad_x"][None], *collect(0), *collect(1), *collect(2), *collect(3))
```

```python
import numpy as np
import jax
import jax.numpy as jnp
from jax import lax
from jax.experimental import pallas as pl
from jax.experimental.pallas import tpu as pltpu

F32 = jnp.float32
BF16 = jnp.bfloat16
HI = lax.Precision.HIGHEST
SDS = jax.ShapeDtypeStruct
MESH = pl.DeviceIdType.MESH

D_MODEL = 2048
HEAD_DIM = 64
A_HEADS = 12
A_GROUP = 3
B_HEADS = 12
C_HEADS = 4
C_HEAD_DIM = 128
WINDOW = 128
EPS = 1e-6
NEG = -1e30
LANE = 128

QA, KA, VA, ZA = 0, 768, 1024, 1280
QB, KB, VB, ZB = 2048, 2816, 3584, 4352
QC, ZC = 5120, 5632
GATE = 6144
P_MAIN = 12288
N_FORGET = 12
FORGET_COL = 5120
SHARD_COLS = 3075
SLAB = 3200
SLAB_START = (0, 3072, 6016, 9088)
SLAB_SHIFT = (0, 3, 122, 125)
N_CHIPS = 4

ADAM_LR = 0.001
ADAM_B1 = 0.9
ADAM_B2 = 0.999
ADAM_EPS = 1e-08
ADAM_WD = 0.01
ADAM_STEP = 10

VMEM_LIMIT = 56 * 1024 * 1024
VMEM_WIDE = 62 * 1024 * 1024


def _params(sem, vmem=VMEM_LIMIT):
    return pltpu.CompilerParams(dimension_semantics=sem, vmem_limit_bytes=vmem)


def _win(tr, width, off):
    return pl.BlockSpec((pl.Element(tr), pl.Element(width)), lambda i, *_: (i * tr, off))


def _rowblk(tr, width):
    return pl.BlockSpec((tr, width), lambda i, *_: (i, 0))


def _const(shape):
    nd = len(shape)
    return pl.BlockSpec(shape, lambda *_: (0,) * nd)


def _rms(x, g):
    return x * lax.rsqrt(jnp.mean(x * x, axis=-1, keepdims=True) + EPS) * g


def _head_mean_impl(x2, bd):
    hi = x2.astype(BF16)
    lo = (x2 - hi.astype(F32)).astype(BF16)
    return _dot(hi, bd) + _dot(lo, bd)


@jax.custom_vjp
def _head_mean(x2, bd):
    return _head_mean_impl(x2, bd)


_head_mean.defvjp(lambda x2, bd: (_head_mean_impl(x2, bd), bd),
                  lambda bd, g: (_head_mean_impl(g, bd), jnp.zeros_like(bd)))


def _head_norm(x, g_tiled, bd):
    return x * lax.rsqrt(_head_mean(x * x, bd) + EPS) * g_tiled


def _silu(z):
    return z * jax.nn.sigmoid(z)


def _dot_nt(a, b):
    return lax.dot_general(a, b, (((1,), (1,)), ((), ())), preferred_element_type=F32)


def _dot_tn(a, b):
    return lax.dot_general(a, b, (((0,), (0,)), ((), ())), preferred_element_type=F32)


def _dot(a, b):
    return jnp.dot(a, b, preferred_element_type=F32)


def _swa_fn(qk, vz, qkp, vzp, qg, kg, sinks, bd, bias, first):
    q = _head_norm(qk[:, :768], qg, bd)
    k2 = jnp.concatenate([qkp[:, 768:], qk[:, 768:]], axis=0)
    k2 = _head_norm(k2, kg, bd[:256, :256])
    v2 = jnp.concatenate([vzp[:, :256], vz[:, :256]], axis=0)
    z = vz[:, 256:]
    cols = A_GROUP * WINDOW
    kj = lax.broadcasted_iota(jnp.int32, (2 * WINDOW, cols), 0)
    no_prev = kj < WINDOW * first.astype(jnp.int32)
    qtb = jnp.transpose(q).astype(BF16)
    kb = k2.astype(BF16)
    vtb = jnp.transpose(v2).astype(BF16)
    outs = [None] * A_HEADS
    for g in range(A_HEADS // A_GROUP):
        heads = [A_GROUP * g + u for u in range(A_GROUP)]
        qs = jnp.concatenate([qtb[64 * h:64 * h + 64, :] for h in heads], axis=1)
        s = _dot(kb[:, 64 * g:64 * g + 64], qs) * (HEAD_DIM ** -0.5) + bias[g]
        s = jnp.where(no_prev, NEG, s)
        sink = jnp.concatenate([jnp.broadcast_to(sinks[:, h:h + 1], (1, WINDOW)) for h in heads], axis=1)
        m = lax.stop_gradient(jnp.maximum(jnp.max(s, axis=0, keepdims=True), sink))
        p = jnp.exp(s - m)
        den = jnp.sum(p, axis=0, keepdims=True) + jnp.exp(sink - m)
        o = _dot(vtb[64 * g:64 * g + 64, :], (p * (1.0 / den)).astype(BF16))
        for u, h in enumerate(heads):
            outs[h] = o[:, WINDOW * u:WINDOW * u + WINDOW]
    return jnp.transpose(jnp.concatenate(outs, axis=0)) * _silu(z)


def _swa_bias():
    qi = np.arange(WINDOW)[None, :]
    kj = np.arange(2 * WINDOW)[:, None]
    rel = qi + WINDOW - kj
    valid = (rel >= 0) & (rel < WINDOW)
    out = np.zeros((A_HEADS // A_GROUP, 2 * WINDOW, A_GROUP * WINDOW), np.float32)
    for h in range(A_HEADS):
        slope = np.float32(2.0 ** (-8.0 * (h + 1) / A_HEADS))
        blk = np.where(valid, -slope * rel.astype(np.float32), np.float32(NEG))
        g, u = divmod(h, A_GROUP)
        out[g, :, WINDOW * u:WINDOW * u + WINDOW] = blk
    return jnp.asarray(out)


def _mem_fn(qz, mkv, qg, kg, bd):
    q = _head_norm(qz[:, :512], qg, bd).astype(BF16)
    k = _head_norm(mkv[:, :512], kg, bd).astype(BF16)
    v = mkv[:, 512:].astype(BF16)
    z = qz[:, 512:]
    outs = []
    for h in range(C_HEADS):
        sl = slice(128 * h, 128 * h + 128)
        s = _dot_nt(q[:, sl], k[:, sl]) * (C_HEAD_DIM ** -0.5)
        m = lax.stop_gradient(jnp.max(s, axis=-1, keepdims=True))
        p = jnp.exp(s - m)
        den = jnp.sum(p, axis=-1, keepdims=True)
        outs.append(_dot((p * (1.0 / den)).astype(BF16), v[:, sl]))
    return jnp.concatenate(outs, axis=1) * _silu(z)


def _qn_fn(q, g, bd):
    return _head_norm(q, g, bd) * (HEAD_DIM ** -0.5)


def _kn_fn(k, g, bd):
    return _head_norm(k, g, bd)


def _block_diag(width, hd):
    i = np.arange(width) // hd
    return jnp.asarray((i[:, None] == i[None, :]).astype(np.float32) / hd, BF16)


def _head_sum(width, hd):
    i = np.arange(width) // hd
    return jnp.asarray((i[:, None] == np.arange(LANE)[None, :]).astype(np.float32))


def _rms_fwd(x, g, *, tr, name):
    rows, dm = x.shape

    def body(x_ref, g_ref, o_ref):
        o_ref[...] = _rms(x_ref[...], g_ref[...]).astype(BF16)

    return pl.pallas_call(
        body, grid=(rows // tr,),
        in_specs=[_rowblk(tr, dm), _const((1, dm))],
        out_specs=_rowblk(tr, dm),
        out_shape=SDS((rows, dm), BF16), name=name,
        compiler_params=_params(("parallel",)))(x, g)


def _rms_bwd(x, g, dy, resid, *, tr, name, comm=None):
    rows, dm = x.shape
    want_dx = resid is not None
    n_in = 4 if want_dx else 3
    n_out = 2 if want_dx else 1
    c_in = len(comm.ins) if comm else 0
    c_out = len(comm.out_shapes) if comm else 0
    nb = rows // tr

    def body(*refs):
        x_ref, g_ref, dy_ref = refs[:3]
        r_ref = refs[3] if want_dx else None
        cin = refs[n_in:n_in + c_in]
        outs = refs[n_in + c_in:n_in + c_in + n_out]
        dg_ref = outs[-1]
        cout = refs[n_in + c_in + n_out:n_in + c_in + n_out + c_out]
        csem = refs[n_in + c_in + n_out + c_out:]

        if comm:
            @pl.when(pl.program_id(0) == 0)
            def _():
                comm.start(cin, cout, csem)

        _, vjp = jax.vjp(_rms, x_ref[...], g_ref[...])
        dx, dg = vjp(dy_ref[...])

        @pl.when(pl.program_id(0) == 0)
        def _():
            dg_ref[...] = jnp.zeros_like(dg_ref)

        dg_ref[...] += dg
        if want_dx:
            outs[0][...] = r_ref[...] + dx

        if comm:
            @pl.when(pl.program_id(0) == nb - 1)
            def _():
                comm.finish(cin, cout, csem)

    hbm = pl.BlockSpec(memory_space=pl.ANY)
    ins = [x, g, dy] + ([resid] if want_dx else []) + (list(comm.ins) if comm else [])
    in_specs = ([_rowblk(tr, dm), _const((1, dm)), _rowblk(tr, dm)] + ([_rowblk(tr, dm)] if want_dx else [])
                + [hbm] * c_in)
    out_specs = ([_rowblk(tr, dm)] if want_dx else []) + [_const((1, dm))] + [hbm] * c_out
    out_shape = (([SDS((rows, dm), F32)] if want_dx else []) + [SDS((1, dm), F32)]
                 + (list(comm.out_shapes) if comm else []))
    res = pl.pallas_call(
        body, grid=(nb,), in_specs=in_specs, out_specs=out_specs, out_shape=out_shape,
        scratch_shapes=list(comm.sems) if comm else [], name=name, compiler_params=_params(("arbitrary",)))(*ins)
    return (list(res[:n_out]), list(res[n_out:])) if comm else res


class _Comm:
    def __init__(self, ins, out_shapes, sems, start, finish):
        self.ins, self.out_shapes, self.sems, self.start, self.finish = list(ins), list(out_shapes), list(sems), start, finish


def _matmul(a, b, *, dims, out_dtype, tm, tn, tk, name, add=None, comms=(), vmem=VMEM_LIMIT, a_half=None):
    a_list = list(a) if isinstance(a, (list, tuple)) else [a]
    b_list = list(b) if isinstance(b, (list, tuple)) else [b]
    assert len(a_list) == 1 or dims == "nt"
    assert len(b_list) == 1 or dims == "tn"
    if dims == "tn":
        kdim, m = a_list[0].shape
        if a_half is not None:
            m //= 2
    else:
        m, kdim = a_list[0].shape[0], sum(p.shape[1] for p in a_list)
    n = b_list[0].shape[0] if dims == "nt" else sum(p.shape[1] for p in b_list)
    tm, tn, tk = min(tm, m), min(tn, n), min(tk, kdim)
    assert m % tm == 0 and n % tn == 0 and kdim % tk == 0, (name, m, n, kdim)
    ni, nj, nk = m // tm, n // tn, kdim // tk
    a_rng, b_rng, pos = [], [], 0
    for p in a_list:
        assert len(a_list) == 1 or p.shape[1] % tk == 0
        a_rng.append((pos, p.shape[1] // tk if len(a_list) > 1 else nk))
        pos += a_rng[-1][1]
    pos = 0
    for p in b_list:
        assert len(b_list) == 1 or p.shape[1] % tn == 0
        b_rng.append((pos, p.shape[1] // tn if len(b_list) > 1 else nj))
        pos += b_rng[-1][1]
    has_add = add is not None
    n_mm_in = len(a_list) + len(b_list) + (1 if has_add else 0)
    c_in = [len(c.ins) for c in comms]
    c_out = [len(c.out_shapes) for c in comms]
    c_sem = [len(c.sems) for c in comms]

    def body(*refs):
        if a_half is not None:
            refs = refs[1:]
        a_refs, b_refs = refs[:len(a_list)], refs[len(a_list):len(a_list) + len(b_list)]
        add_ref = refs[n_mm_in - 1] if has_add else None
        pos = n_mm_in
        cin = []
        for cnt in c_in:
            cin.append(refs[pos:pos + cnt])
            pos += cnt
        o_ref = refs[pos]
        pos += 1
        cout = []
        for cnt in c_out:
            cout.append(refs[pos:pos + cnt])
            pos += cnt
        acc = refs[pos]
        pos += 1
        csem = []
        for cnt in c_sem:
            csem.append(refs[pos:pos + cnt])
            pos += cnt
        i, j, k = pl.program_id(0), pl.program_id(1), pl.program_id(2)

        if comms:
            @pl.when((i == 0) & (j == 0) & (k == 0))
            def _():
                for c, ci, co, cs in zip(comms, cin, cout, csem):
                    c.start(ci, co, cs)

        def accumulate(a_ref, b_ref, first_k, later_k):
            if dims == "nn":
                part = _dot(a_ref[...], b_ref[...])
            elif dims == "nt":
                part = _dot_nt(a_ref[...], b_ref[...])
            else:
                part = _dot_tn(a_ref[...], b_ref[...])

            if first_k:
                @pl.when(k == 0)
                def _():
                    acc[...] = part + add_ref[...] if has_add else part

            if later_k:
                @pl.when(k > 0)
                def _():
                    acc[...] += part

        if len(a_list) > 1:
            for a_ref, (k0, cnt) in zip(a_refs, a_rng):
                @pl.when((k >= k0) & (k < k0 + cnt))
                def _(a_ref=a_ref, k0=k0, cnt=cnt):
                    accumulate(a_ref, b_refs[0], k0 == 0, k0 + cnt > 1)
        elif len(b_list) > 1:
            for b_ref, (j0, cnt) in zip(b_refs, b_rng):
                @pl.when((j >= j0) & (j < j0 + cnt))
                def _(b_ref=b_ref):
                    accumulate(a_refs[0], b_ref, True, nk > 1)
        else:
            accumulate(a_refs[0], b_refs[0], True, nk > 1)

        @pl.when(k == nk - 1)
        def _():
            o_ref[...] = acc[...].astype(out_dtype)

        if comms:
            @pl.when((i == ni - 1) & (j == nj - 1) & (k == nk - 1))
            def _():
                for c, ci, co, cs in zip(comms, cin, cout, csem):
                    c.finish(ci, co, cs)

    def a_spec(k0, cnt):
        if dims == "tn":
            if a_half is None:
                return pl.BlockSpec((tk, tm), lambda i, j, k: (k, i))
            pick = (lambda c: 1 - c[0]) if a_half[1] else (lambda c: c[0])
            return pl.BlockSpec((tk, tm), lambda i, j, k, c: (k, i + pick(c) * ni))
        return pl.BlockSpec((tm, tk), lambda i, j, k, *_: (i, jnp.clip(k - k0, 0, cnt - 1)))

    def b_spec(j0, cnt):
        if dims == "nt":
            return pl.BlockSpec((tn, tk), lambda i, j, k, *_: (j, k))
        return pl.BlockSpec((tk, tn), lambda i, j, k, *_: (k, jnp.clip(j - j0, 0, cnt - 1)))

    o_spec = pl.BlockSpec((tm, tn), lambda i, j, k, *_: (i, j))
    hbm = pl.BlockSpec(memory_space=pl.ANY)
    ins = a_list + b_list + ([add] if has_add else []) + [x for c in comms for x in c.ins]
    in_specs = ([a_spec(*r) for r in a_rng] + [b_spec(*r) for r in b_rng] + ([o_spec] if has_add else [])
                + [hbm] * sum(c_in))
    out_specs = [o_spec] + [hbm] * sum(c_out)
    out_shape = [SDS((m, n), out_dtype)] + [s for c in comms for s in c.out_shapes]
    scratch = [pltpu.VMEM((tm, tn), F32)] + [s for c in comms for s in c.sems]
    sem = ("arbitrary",) * 3 if comms else ("parallel", "parallel", "arbitrary")
    if a_half is None:
        grid = dict(grid=(ni, nj, nk), in_specs=in_specs, out_specs=out_specs, scratch_shapes=scratch)
    else:
        grid = dict(grid_spec=pltpu.PrefetchScalarGridSpec(
            num_scalar_prefetch=1, grid=(ni, nj, nk), in_specs=in_specs, out_specs=out_specs, scratch_shapes=scratch))
        ins = [a_half[0]] + ins
    res = pl.pallas_call(body, out_shape=out_shape, name=name, compiler_params=_params(sem, vmem), **grid)(*ins)
    if not comms:
        return res[0]
    outs, pos = [], 1
    for cnt in c_out:
        outs.append(list(res[pos:pos + cnt]))
        pos += cnt
    return res[0], outs


def _swa_specs(nb, blk=lambda n: n):
    cur = lambda off: pl.BlockSpec((pl.Element(WINDOW), pl.Element(1024)), lambda n: (blk(n) * WINDOW, off))
    prev = lambda off: pl.BlockSpec((pl.Element(WINDOW), pl.Element(1024)),
                                    lambda n: (jnp.maximum(blk(n) - 1, 0) * WINDOW, off))
    return [cur(QA), cur(VA), prev(QA), prev(VA),
            _const((1, 768)), _const((1, 256)), _const((1, LANE)), _const((768, 768)),
            _const((A_HEADS // A_GROUP, 2 * WINDOW, A_GROUP * WINDOW))]


def _swa_fwd(proj, qg, kg, sinks, bd, bias):
    s = proj.shape[0]
    nb = s // WINDOW

    def body(qk_ref, vz_ref, qkp_ref, vzp_ref, qg_ref, kg_ref, sk_ref, bd_ref, bias_ref, o_ref):
        first = pl.program_id(0) == 0
        o_ref[...] = _swa_fn(qk_ref[...], vz_ref[...], qkp_ref[...], vzp_ref[...], qg_ref[...], kg_ref[...],
                             sk_ref[...], bd_ref[...], bias_ref[...], first).astype(BF16)

    return pl.pallas_call(
        body, grid=(nb,), in_specs=_swa_specs(nb), out_specs=_rowblk(WINDOW, 768),
        out_shape=SDS((s, 768), BF16), name="swa_fwd",
        compiler_params=_params(("parallel",)))(proj, proj, proj, proj, qg, kg, sinks, bd, bias)


def _swa_bwd(proj, qg, kg, sinks, bd, bias, dga):
    s = proj.shape[0]
    nb = s // WINDOW
    blk = lambda n: nb - 1 - n

    def body(qk_ref, vz_ref, qkp_ref, vzp_ref, qg_ref, kg_ref, sk_ref, bd_ref, bias_ref, dg_ref,
             d_ref, dqg_ref, dkg_ref, dsk_ref, carry):
        first = blk(pl.program_id(0)) == 0
        bd_v = bd_ref[...]
        bias_v = bias_ref[...]
        fn = lambda qk, vz, qkp, vzp, qg_, kg_, sk: _swa_fn(qk, vz, qkp, vzp, qg_, kg_, sk, bd_v, bias_v, first)
        _, vjp = jax.vjp(fn, qk_ref[...], vz_ref[...], qkp_ref[...], vzp_ref[...], qg_ref[...], kg_ref[...], sk_ref[...])
        dqk, dvz, dqkp, dvzp, dqg, dkg, dsk = vjp(dg_ref[...])

        @pl.when(pl.program_id(0) == 0)
        def _():
            dqg_ref[...] = jnp.zeros_like(dqg_ref)
            dkg_ref[...] = jnp.zeros_like(dkg_ref)
            dsk_ref[...] = jnp.zeros_like(dsk_ref)
            carry[...] = jnp.zeros_like(carry)

        dqg_ref[...] += dqg
        dkg_ref[...] += dkg
        dsk_ref[...] += dsk
        kv = jnp.concatenate([dqk[:, 768:], dvz[:, :256]], axis=1) + carry[...]
        d_ref[...] = jnp.concatenate([dqk[:, :768], kv, dvz[:, 256:]], axis=1).astype(BF16)
        carry[...] = jnp.concatenate([dqkp[:, 768:], dvzp[:, :256]], axis=1)

    rev = lambda w: pl.BlockSpec((WINDOW, w), lambda n: (blk(n), 0))
    return pl.pallas_call(
        body, grid=(nb,), in_specs=_swa_specs(nb, blk) + [rev(768)],
        out_specs=[rev(2048), _const((1, 768)), _const((1, 256)), _const((1, LANE))],
        out_shape=[SDS((s, 2048), BF16), SDS((1, 768), F32), SDS((1, 256), F32), SDS((1, LANE), F32)],
        scratch_shapes=[pltpu.VMEM((WINDOW, 512), F32)],
        name="swa_bwd", compiler_params=_params(("arbitrary",)))(proj, proj, proj, proj, qg, kg, sinks, bd, bias, dga)


def _mem_fwd(proj, mkv, qg, kg, bd, *, tr):
    s = proj.shape[0]

    def body(qz_ref, mkv_ref, qg_ref, kg_ref, bd_ref, o_ref):
        o_ref[...] = _mem_fn(qz_ref[...], mkv_ref[...], qg_ref[...], kg_ref[...], bd_ref[...]).astype(BF16)

    return pl.pallas_call(
        body, grid=(s // tr,),
        in_specs=[_win(tr, 1024, QC), _const(mkv.shape), _const((1, 512)), _const((1, 512)), _const((512, 512))],
        out_specs=_rowblk(tr, 512), out_shape=SDS((s, 512), BF16), name="mem_fwd",
        compiler_params=_params(("parallel",)))(proj, mkv, qg, kg, bd)


def _mem_bwd(proj, mkv, qg, kg, bd, dgc, *, tr):
    s = proj.shape[0]

    def body(qz_ref, mkv_ref, qg_ref, kg_ref, bd_ref, dg_ref, dqz_ref, dmkv_ref, dqg_ref, dkg_ref):
        bd_v = bd_ref[...]
        fn = lambda qz, mkv_, qg_, kg_: _mem_fn(qz, mkv_, qg_, kg_, bd_v)
        _, vjp = jax.vjp(fn, qz_ref[...], mkv_ref[...], qg_ref[...], kg_ref[...])
        dqz, dmkv, dqg, dkg = vjp(dg_ref[...])

        @pl.when(pl.program_id(0) == 0)
        def _():
            dmkv_ref[...] = jnp.zeros_like(dmkv_ref)
            dqg_ref[...] = jnp.zeros_like(dqg_ref)
            dkg_ref[...] = jnp.zeros_like(dkg_ref)

        dmkv_ref[...] += dmkv
        dqg_ref[...] += dqg
        dkg_ref[...] += dkg
        dqz_ref[...] = dqz.astype(BF16)

    return pl.pallas_call(
        body, grid=(s // tr,),
        in_specs=[_win(tr, 1024, QC), _const(mkv.shape), _const((1, 512)), _const((1, 512)), _const((512, 512)),
                  _rowblk(tr, 512)],
        out_specs=[_rowblk(tr, 1024), _const(mkv.shape), _const((1, 512)), _const((1, 512))],
        out_shape=[SDS((s, 1024), BF16), SDS(mkv.shape, F32), SDS((1, 512), F32), SDS((1, 512), F32)],
        name="mem_bwd", compiler_params=_params(("arbitrary",)))(proj, mkv, qg, kg, bd, dgc)


def _log_sigmoid(x):
    return jnp.minimum(x, 0.0) - jnp.log1p(jnp.exp(-jnp.abs(x)))


FOX_TQ, FOX_TK = 512, 512
FOX_FWD_TQ, FOX_FWD_TK = 512, 1024


def _fox_tiles(s):
    return min(FOX_TQ, s), min(FOX_TK, s)


AUG = 128 * B_HEADS
COL_A, COL_B = 64, 67


def _split3(c):
    hi = c.astype(BF16)
    r1 = c - hi.astype(F32)
    mid = r1.astype(BF16)
    lo = (r1 - mid.astype(F32)).astype(BF16)
    return hi, mid, lo


def _expand_mats():
    def mat(col0):
        e = np.zeros((768 + 3 * LANE, AUG), np.float32)
        for h in range(B_HEADS):
            for d in range(HEAD_DIM):
                e[64 * h + d, 128 * h + d] = 1.0
            for part in range(3):
                e[768 + LANE * part + h, 128 * h + col0 + part] = 1.0
        return e

    def ones(col0):
        o = np.zeros((1, AUG), np.float32)
        for h in range(B_HEADS):
            o[0, 128 * h + col0:128 * h + col0 + 3] = 1.0
        return o

    return (jnp.asarray(mat(COL_A), BF16), jnp.asarray(mat(COL_B), BF16), jnp.asarray(ones(COL_A)), jnp.asarray(ones(COL_B)))


def _augment(data_bf16, triple, emat, ones_row):
    parts = [data_bf16] + (list(triple) if triple is not None else [jnp.zeros((data_bf16.shape[0], LANE), BF16)] * 3)
    wide = _dot(jnp.concatenate(parts, axis=1), emat)
    if ones_row is not None:
        wide = wide + ones_row
    return wide


def _compact(wide):
    return jnp.concatenate([wide[:, 128 * h:128 * h + 64] for h in range(wide.shape[1] // 128)], axis=1)


def _lane_of_heads(wide, col, first=0):
    rows = wide.shape[0]
    lane = lax.broadcasted_iota(jnp.int32, (rows, LANE), 1)
    out = jnp.zeros((rows, LANE), F32)
    for h in range(wide.shape[1] // 128):
        out = jnp.where(lane == first + h, wide[:, 128 * h + col:128 * h + col + 1], out)
    return out


def _fox2_prep(proj, fbl, qg, kg, bfor, bd, ea, eb, ones_a, ones_b, *, tr):
    s = proj.shape[0]
    tri = jnp.asarray(np.tril(np.ones((tr, tr), np.float32)))

    def body(q_ref, k_ref, v_ref, fb_ref, qg_ref, kg_ref, bf_ref, bd_ref, tri_ref, ea_ref, eb_ref, oa_ref, ob_ref,
             qat_ref, ka_ref, kat_ref, va_ref, vat_ref, qn_ref, c_ref, carry):
        @pl.when(pl.program_id(0) == 0)
        def _():
            carry[...] = jnp.zeros_like(carry)

        bd_v = bd_ref[...]
        lane = lax.broadcasted_iota(jnp.int32, (tr, LANE), 1)
        logf = jnp.where(lane < N_FORGET, _log_sigmoid(fb_ref[...] + bf_ref[...]), 0.0)
        c = jnp.dot(tri_ref[...], logf, precision=HI, preferred_element_type=F32) + carry[...]
        c_ref[...] = c
        carry[...] = c[tr - 1:tr, :]
        qn = _qn_fn(q_ref[...], qg_ref[...], bd_v).astype(BF16)
        kn = _kn_fn(k_ref[...], kg_ref[...], bd_v).astype(BF16)
        qn_ref[...] = qn
        qat_ref[...] = jnp.transpose(_augment(qn, _split3(c), ea_ref[...], ob_ref[...])).astype(BF16)
        ka = _augment(kn, _split3(-c), eb_ref[...], oa_ref[...])
        ka_ref[...] = ka.astype(BF16)
        kat_ref[...] = jnp.transpose(ka).astype(BF16)
        va = _augment(v_ref[...].astype(BF16), None, ea_ref[...], oa_ref[...])
        va_ref[...] = va.astype(BF16)
        vat_ref[...] = jnp.transpose(va).astype(BF16)

    emat = _const((768 + 3 * LANE, AUG))
    return pl.pallas_call(
        body, grid=(s // tr,),
        in_specs=[_win(tr, 768, QB), _win(tr, 768, KB), _win(tr, 768, VB), _rowblk(tr, LANE), _const((1, 768)),
                  _const((1, 768)), _const((1, LANE)), _const((768, 768)), _const((tr, tr)), emat, emat,
                  _const((1, AUG)), _const((1, AUG))],
        out_specs=[pl.BlockSpec((AUG, tr), lambda i: (0, i)), _rowblk(tr, AUG), pl.BlockSpec((AUG, tr), lambda i: (0, i)),
                   _rowblk(tr, AUG), pl.BlockSpec((AUG, tr), lambda i: (0, i)), _rowblk(tr, 768), _rowblk(tr, LANE)],
        out_shape=[SDS((AUG, s), BF16), SDS((s, AUG), BF16), SDS((AUG, s), BF16), SDS((s, AUG), BF16),
                   SDS((AUG, s), BF16), SDS((s, 768), BF16), SDS((s, LANE), F32)],
        scratch_shapes=[pltpu.VMEM((1, LANE), F32)], name="fox_prep",
        compiler_params=_params(("arbitrary",)))(proj, proj, proj, fbl, qg, kg, bfor, bd, tri, ea, eb, ones_a, ones_b)


def _fox2_fwd(proj, qat, ka, vat):
    s = proj.shape[0]
    tq, tk = min(FOX_FWD_TQ, s), min(FOX_FWD_TK, s)
    nq, nk = s // tq, s // tk

    def last_k(i):
        return (i * tq + tq - 1) // tk

    def body(qt_ref, k_ref, vt_ref, z_ref, gb_ref, yb_ref, lse_ref, acc, m_s):
        i, j = pl.program_id(0), pl.program_id(1)

        @pl.when(j == 0)
        def _():
            acc[...] = jnp.zeros_like(acc)
            m_s[...] = jnp.full_like(m_s, NEG)

        def tile(masked):
            if masked:
                kpos = j * tk + lax.broadcasted_iota(jnp.int32, (tk, tq), 0)
                qpos = i * tq + lax.broadcasted_iota(jnp.int32, (tk, tq), 1)
                mask = kpos <= qpos
            for h in range(B_HEADS):
                sl = slice(128 * h, 128 * h + 128)
                sc = _dot(k_ref[:, sl], qt_ref[sl, :])
                if masked:
                    sc = jnp.where(mask, sc, NEG)
                m_prev = m_s[h:h + 1, :]
                m_new = jnp.maximum(m_prev, jnp.max(sc, axis=0, keepdims=True))
                p = jnp.exp(sc - m_new).astype(BF16)
                acc[sl, :] = jnp.exp(m_prev - m_new) * acc[sl, :] + _dot(vt_ref[sl, :], p)
                m_s[h:h + 1, :] = m_new

        full = j * tk + tk - 1 <= i * tq

        @pl.when(full)
        def _():
            tile(False)

        @pl.when(jnp.logical_and(jnp.logical_not(full), j <= last_k(i)))
        def _():
            tile(True)

        @pl.when(j == nk - 1)
        def _():
            outs = []
            row = lax.broadcasted_iota(jnp.int32, (LANE, tq), 0)
            lse_t = jnp.zeros((LANE, tq), F32)
            for h in range(B_HEADS):
                l_row = acc[128 * h + COL_A:128 * h + COL_A + 1, :]
                outs.append(acc[128 * h:128 * h + 64, :] * (1.0 / l_row))
                lse_t = jnp.where(row == h, m_s[h:h + 1, :] + jnp.log(l_row), lse_t)
            y = jnp.transpose(jnp.concatenate(outs, axis=0))
            yb_ref[...] = y
            gb_ref[...] = (y * _silu(z_ref[...])).astype(BF16)
            lse_ref[...] = jnp.transpose(lse_t)

    kcol = lambda i, j: (0, jnp.minimum(j, last_k(i)))
    return pl.pallas_call(
        body, grid=(nq, nk),
        in_specs=[pl.BlockSpec((AUG, tq), lambda i, j: (0, i)),
                  pl.BlockSpec((tk, AUG), lambda i, j: (jnp.minimum(j, last_k(i)), 0)),
                  pl.BlockSpec((AUG, tk), kcol),
                  pl.BlockSpec((pl.Element(tq), pl.Element(768)), lambda i, j: (i * tq, ZB))],
        out_specs=[pl.BlockSpec((tq, 768), lambda i, j: (i, 0)), pl.BlockSpec((tq, 768), lambda i, j: (i, 0)),
                   pl.BlockSpec((tq, LANE), lambda i, j: (i, 0))],
        out_shape=[SDS((s, 768), BF16), SDS((s, 768), F32), SDS((s, LANE), F32)],
        scratch_shapes=[pltpu.VMEM((AUG, tq), F32), pltpu.VMEM((16, tq), F32)],
        name="fox_fwd", compiler_params=_params(("parallel", "arbitrary")))(qat, ka, vat, proj)


def _fox2_bwd_pre(proj, yb, dgb, qn, c, lse, hsum, ea, ones_b, *, tr):
    s = proj.shape[0]

    def body(z_ref, y_ref, dg_ref, qn_ref, c_ref, lse_ref, hs_ref, ea_ref, ob_ref,
             qa_ref, qat_ref, dya_ref, dyat_ref, dz_ref):
        z, y, dg = z_ref[...], y_ref[...], dg_ref[...]
        sg = jax.nn.sigmoid(z)
        dy = dg * (z * sg)
        dz_ref[...] = (dg * y * (sg * (1.0 + z * (1.0 - sg)))).astype(BF16)
        delta = jnp.dot(dy * y, hs_ref[...], precision=HI, preferred_element_type=F32)
        e = ea_ref[...]
        dya = _augment(dy.astype(BF16), _split3(-delta), e, None)
        dya_ref[...] = dya.astype(BF16)
        dyat_ref[...] = jnp.transpose(dya).astype(BF16)
        qa = _augment(qn_ref[...], _split3(c_ref[...] - lse_ref[...]), e, ob_ref[...])
        qa_ref[...] = qa.astype(BF16)
        qat_ref[...] = jnp.transpose(qa).astype(BF16)

    return pl.pallas_call(
        body, grid=(s // tr,),
        in_specs=[_win(tr, 768, ZB), _rowblk(tr, 768), _rowblk(tr, 768), _rowblk(tr, 768), _rowblk(tr, LANE),
                  _rowblk(tr, LANE), _const((768, LANE)), _const((768 + 3 * LANE, AUG)), _const((1, AUG))],
        out_specs=[_rowblk(tr, AUG), pl.BlockSpec((AUG, tr), lambda i: (0, i)), _rowblk(tr, AUG),
                   pl.BlockSpec((AUG, tr), lambda i: (0, i)), _rowblk(tr, 768)],
        out_shape=[SDS((s, AUG), BF16), SDS((AUG, s), BF16), SDS((s, AUG), BF16), SDS((AUG, s), BF16),
                   SDS((s, 768), BF16)], name="fox_bwd_pre",
        compiler_params=_params(("parallel",)))(proj, yb, dgb, qn, c, lse, hsum, ea, ones_b)


def _fox2_bwd(qb, qbt, ka, kat, va, dya, dyat):
    s = qb.shape[0]
    tq, tk = _fox_tiles(s)
    nq, nk = s // tq, s // tk
    ng = 2
    gh = B_HEADS // ng
    gw = 128 * gh

    def first_q(j):
        return (j * tk) // tq

    def body(q_ref, qt_ref, k_ref, kt_ref, v_ref, dy_ref, dyt_ref, dq_hbm, dk_ref, dv_ref, dck_ref,
             dq_acc, dk_acc, dv_acc, sem):
        g, j, i = pl.program_id(0), pl.program_id(1), pl.program_id(2)

        @pl.when((j == 0) & (i == 0))
        def _():
            dq_acc[...] = jnp.zeros_like(dq_acc)

        @pl.when(i == 0)
        def _():
            dk_acc[...] = jnp.zeros_like(dk_acc)
            dv_acc[...] = jnp.zeros_like(dv_acc)

        def tile(masked):
            if masked:
                kpos = j * tk + lax.broadcasted_iota(jnp.int32, (tk, tq), 0)
                qpos = i * tq + lax.broadcasted_iota(jnp.int32, (tk, tq), 1)
                mask = kpos <= qpos
            cols = pl.ds(pl.multiple_of(i * tq, tq), tq)
            for h in range(gh):
                sl = slice(128 * h, 128 * h + 128)
                sc = _dot(k_ref[:, sl], qt_ref[sl, :])
                if masked:
                    sc = jnp.where(mask, sc, NEG)
                p = jnp.exp(sc)
                ds = (p * _dot(v_ref[:, sl], dyt_ref[sl, :])).astype(BF16)
                dv_acc[:, sl] += _dot(p.astype(BF16), dy_ref[:, sl])
                dk_acc[:, sl] += _dot(ds, q_ref[:, sl])
                dq_acc[sl, cols] += _dot(kt_ref[sl, :], ds)

        full = j * tk + tk - 1 <= i * tq

        @pl.when(full)
        def _():
            tile(False)

        @pl.when(jnp.logical_and(jnp.logical_not(full), i >= first_q(j)))
        def _():
            tile(True)

        @pl.when(i == nq - 1)
        def _():
            dkw = dk_acc[...]
            dk_ref[...] = _compact(dkw)
            dv_ref[...] = _compact(dv_acc[...]).astype(BF16)
            dck_ref[...] = -_lane_of_heads(dkw, COL_B, gh * g)

        @pl.when((j == nk - 1) & (i == nq - 1))
        def _():
            cp = pltpu.make_async_copy(dq_acc, dq_hbm.at[pl.ds(pl.multiple_of(g * gw, gw), gw)], sem)
            cp.start()
            cp.wait()

    qrow = pl.BlockSpec((tq, gw), lambda g, j, i: (jnp.maximum(i, first_q(j)), g))
    qcol = pl.BlockSpec((gw, tq), lambda g, j, i: (g, jnp.maximum(i, first_q(j))))
    krow = pl.BlockSpec((tk, gw), lambda g, j, i: (j, g))
    kcol = pl.BlockSpec((gw, tk), lambda g, j, i: (g, j))
    kout = pl.BlockSpec((tk, gw // 2), lambda g, j, i: (j, g))
    return pl.pallas_call(
        body, grid=(ng, nk, nq),
        in_specs=[qrow, qcol, krow, kcol, krow, qrow, qcol],
        out_specs=[pl.BlockSpec(memory_space=pl.ANY), kout, kout,
                   pl.BlockSpec((None, tk, LANE), lambda g, j, i: (g, j, 0))],
        out_shape=[SDS((AUG, s), F32), SDS((s, 768), F32), SDS((s, 768), BF16), SDS((ng, s, LANE), F32)],
        scratch_shapes=[pltpu.VMEM((gw, s), F32), pltpu.VMEM((tk, gw), F32), pltpu.VMEM((tk, gw), F32),
                        pltpu.SemaphoreType.DMA],
        name="fox_bwd", compiler_params=_params(("arbitrary",) * 3))(qb, qbt, ka, kat, va, dya, dyat)


def _fox2_bwd_post(proj, fbl, qg, kg, bfor, bd, dqa, dkn, dck, *, tr):
    s = proj.shape[0]
    nb = s // tr
    triu = jnp.asarray(np.triu(np.ones((tr, tr), np.float32)))
    rev = lambda i: nb - 1 - i

    def body(q_ref, k_ref, fb_ref, qg_ref, kg_ref, bf_ref, bd_ref, tri_ref, dqa_ref, dkn_ref, dck_ref,
             dq_ref, dk_ref, dfb_ref, dqg_ref, dkg_ref, dbf_ref, carry):
        @pl.when(pl.program_id(0) == 0)
        def _():
            carry[...] = jnp.zeros_like(carry)
            dqg_ref[...] = jnp.zeros_like(dqg_ref)
            dkg_ref[...] = jnp.zeros_like(dkg_ref)
            dbf_ref[...] = jnp.zeros_like(dbf_ref)

        bd_v = bd_ref[...]
        dqw = jnp.transpose(dqa_ref[...])
        _, vjp_q = jax.vjp(lambda q, g: _qn_fn(q, g, bd_v), q_ref[...], qg_ref[...])
        dq, dqg = vjp_q(_compact(dqw))
        _, vjp_k = jax.vjp(lambda k, g: _kn_fn(k, g, bd_v), k_ref[...], kg_ref[...])
        dk, dkg = vjp_k(dkn_ref[...])
        dq_ref[...] = dq.astype(BF16)
        dk_ref[...] = dk.astype(BF16)
        dqg_ref[...] += dqg
        dkg_ref[...] += dkg

        dc = _lane_of_heads(dqw, COL_A) + (dck_ref[0] + dck_ref[1])
        dlogf = jnp.dot(tri_ref[...], dc, precision=HI, preferred_element_type=F32) + carry[...]
        carry[...] = dlogf[0:1, :]
        lane = lax.broadcasted_iota(jnp.int32, (tr, LANE), 1)
        xf = fb_ref[...] + bf_ref[...]
        dfb = jnp.where(lane < N_FORGET, dlogf * jax.nn.sigmoid(-xf), 0.0)
        dfb_ref[...] = dfb.astype(BF16)
        dbf_ref[...] += jnp.sum(dfb, axis=0, keepdims=True)

    rb = lambda w: pl.BlockSpec((tr, w), lambda i: (rev(i), 0))
    wn = lambda w, off: pl.BlockSpec((pl.Element(tr), pl.Element(w)), lambda i: (rev(i) * tr, off))
    return pl.pallas_call(
        body, grid=(nb,),
        in_specs=[wn(768, QB), wn(768, KB), rb(LANE), _const((1, 768)), _const((1, 768)), _const((1, LANE)),
                  _const((768, 768)), _const((tr, tr)), pl.BlockSpec((AUG, tr), lambda i: (0, rev(i))), rb(768),
                  pl.BlockSpec((2, tr, LANE), lambda i: (0, rev(i), 0))],
        out_specs=[rb(768), rb(768), rb(LANE), _const((1, 768)), _const((1, 768)), _const((1, LANE))],
        out_shape=[SDS((s, 768), BF16), SDS((s, 768), BF16), SDS((s, LANE), BF16), SDS((1, 768), F32),
                   SDS((1, 768), F32), SDS((1, LANE), F32)],
        scratch_shapes=[pltpu.VMEM((1, LANE), F32)], name="fox_bwd_post",
        compiler_params=_params(("arbitrary",)))(proj, proj, fbl, qg, kg, bfor, bd, triu, dqa, dkn, dck)


def _merge_specs(tr):
    row = lambda w: pl.BlockSpec((tr, w), lambda i, j: (i, 0))
    shard = lambda r: pl.BlockSpec((None, r, 512), lambda i, j: (j, 0, 0))
    gate = lambda b: pl.BlockSpec((tr, 512), lambda i, j: (i, (GATE + 2048 * b) // 512 + j))
    return [row(768), row(768), row(512), shard(768), shard(768), shard(512), gate(0), gate(1), gate(2)]


def _merge_fwd(proj, ga, gb, gc, wa, wb, wc, *, tr):
    s = proj.shape[0]

    def body(ga_ref, gb_ref, gc_ref, wa_ref, wb_ref, wc_ref, l0_ref, l1_ref, l2_ref, y_ref):
        ua = _dot(ga_ref[...], wa_ref[...])
        ub = _dot(gb_ref[...], wb_ref[...])
        uc = _dot(gc_ref[...], wc_ref[...])
        y = jax.nn.sigmoid(l0_ref[...]) * ua + jax.nn.sigmoid(l1_ref[...]) * ub + jax.nn.sigmoid(l2_ref[...]) * uc
        y_ref[...] = y.astype(BF16)

    return pl.pallas_call(
        body, grid=(s // tr, N_CHIPS), in_specs=_merge_specs(tr),
        out_specs=pl.BlockSpec((tr, 512), lambda i, j: (i, j)), out_shape=SDS((s, D_MODEL), BF16), name="merge_fwd",
        compiler_params=_params(("parallel", "arbitrary")))(ga, gb, gc, wa, wb, wc, proj, proj, proj)


def _merge_bwd(proj, ga, gb, gc, wa, wb, wc, dy, *, tr):
    s = proj.shape[0]

    def body(ga_ref, gb_ref, gc_ref, wa_ref, wb_ref, wc_ref, l0_ref, l1_ref, l2_ref, dy_ref,
             dl0_ref, dl1_ref, dl2_ref, dua_ref, dub_ref, duc_ref, dga_ref, dgb_ref, dgc_ref):
        j = pl.program_id(1)
        dyv = dy_ref[...]

        @pl.when(j == 0)
        def _():
            dga_ref[...] = jnp.zeros_like(dga_ref)
            dgb_ref[...] = jnp.zeros_like(dgb_ref)
            dgc_ref[...] = jnp.zeros_like(dgc_ref)

        for g_ref, w_ref, l_ref, dl_ref, du_ref, dg_ref in (
                (ga_ref, wa_ref, l0_ref, dl0_ref, dua_ref, dga_ref),
                (gb_ref, wb_ref, l1_ref, dl1_ref, dub_ref, dgb_ref),
                (gc_ref, wc_ref, l2_ref, dl2_ref, duc_ref, dgc_ref)):
            w = w_ref[...]
            u = _dot(g_ref[...], w)
            sg = jax.nn.sigmoid(l_ref[...])
            dl_ref[...] = (dyv * u * sg * (1.0 - sg)).astype(BF16)
            du = (dyv * sg).astype(BF16)
            du_ref[...] = du
            dg_ref[...] += _dot_nt(du, w)

    blk = pl.BlockSpec((tr, 512), lambda i, j: (i, j))
    row = lambda w: pl.BlockSpec((tr, w), lambda i, j: (i, 0))
    big = SDS((s, D_MODEL), BF16)
    return pl.pallas_call(
        body, grid=(s // tr, N_CHIPS), in_specs=_merge_specs(tr) + [blk],
        out_specs=[blk] * 6 + [row(768), row(768), row(512)],
        out_shape=[big] * 6 + [SDS((s, 768), F32), SDS((s, 768), F32), SDS((s, 512), F32)], name="merge_bwd",
        compiler_params=_params(("parallel", "arbitrary")))(ga, gb, gc, wa, wb, wc, proj, proj, proj, dy)


def _out_loss(y, wo, x, tgt, *, tr, tn):
    s = x.shape[0]

    def body(y_ref, w_ref, x_ref, t_ref, d_ref, db_ref, sq_ref, dy_ref):
        @pl.when((pl.program_id(0) == 0) & (pl.program_id(1) == 0))
        def _():
            sq_ref[...] = jnp.zeros_like(sq_ref)

        @pl.when(pl.program_id(1) == 0)
        def _():
            dy_ref[...] = jnp.zeros_like(dy_ref)

        w = w_ref[...]
        out = x_ref[...] + _dot(y_ref[...], w)
        diff = out - t_ref[...]
        sq_ref[...] += jnp.sum(diff * diff, axis=0, keepdims=True)
        d = diff * (1.0 / D_MODEL)
        d_ref[...] = d
        db = d.astype(BF16)
        db_ref[...] = db
        dy_ref[...] += _dot_nt(db, w)

    blk = pl.BlockSpec((tr, tn), lambda i, j: (i, j))
    row = pl.BlockSpec((tr, D_MODEL), lambda i, j: (i, 0))
    return pl.pallas_call(
        body, grid=(s // tr, D_MODEL // tn),
        in_specs=[row, pl.BlockSpec((D_MODEL, tn), lambda i, j: (0, j)), blk, blk],
        out_specs=[blk, blk, _const((1, tn)), row],
        out_shape=[SDS((s, D_MODEL), F32), SDS((s, D_MODEL), BF16), SDS((1, tn), F32), SDS((s, D_MODEL), F32)],
        name="out_loss", compiler_params=_params(("arbitrary", "arbitrary")))(y, wo, x, tgt)


def _tile_gain(g, reps):
    return jnp.tile(g.reshape(1, -1), (1, reps))


def _pad_lane(v):
    v = v.reshape(1, -1)
    return jnp.pad(v, ((0, 0), (0, LANE - v.shape[1])))


def _local_step(x, mem, tgt, w_main, w_fb, w_small, norm_gain, mem_norm_gain, b_forget,
                q_gain_a, k_gain_a, sinks_a, q_gain_b, k_gain_b, q_gain_c, k_gain_c, core=None):
    s = x.shape[0]
    tr = min(512, s)
    bd64 = _block_diag(768, HEAD_DIM)
    bd128 = _block_diag(512, C_HEAD_DIM)
    hsum = _head_sum(768, HEAD_DIM)
    qga, kga = _tile_gain(q_gain_a, 12), _tile_gain(k_gain_a, 4)
    qgb, kgb = _tile_gain(q_gain_b, 12), _tile_gain(k_gain_b, 12)
    qgc, kgc = _tile_gain(q_gain_c, 4), _tile_gain(k_gain_c, 4)
    sinks = _pad_lane(sinks_a)
    bfor = _pad_lane(b_forget)

    hn = _rms_fwd(x, norm_gain, tr=tr, name="rms_x")
    on_mesh = core is not None
    if on_mesh:
        proj, (gathered,) = _matmul(hn, w_main, dims="nn", out_dtype=F32, tm=1024, tn=1024, tk=D_MODEL, name="proj_main",
                                    comms=[_gather_comm(list(w_small))])
        w_mk, wa, wb, wc, wo = gathered
        w_mk, wo = w_mk.reshape(D_MODEL, 1024), wo.reshape(D_MODEL, D_MODEL)
    else:
        proj = _matmul(hn, w_main, dims="nn", out_dtype=F32, tm=1024, tn=1024, tk=D_MODEL, name="proj_main")
        w_mk, wa, wb, wc, wo = w_small
    fbl = _matmul(hn, w_fb, dims="nn", out_dtype=F32, tm=1024, tn=LANE, tk=D_MODEL, name="proj_forget")
    memn = _rms_fwd(mem, mem_norm_gain, tr=mem.shape[0], name="rms_mem")
    mkv = _matmul(memn, w_mk, dims="nn", out_dtype=F32, tm=256, tn=512, tk=D_MODEL, name="mem_kv")

    swa_bias = _swa_bias()
    ga = _swa_fwd(proj, qga, kga, sinks, bd64, swa_bias)
    ea, eb, ones_a, ones_b = _expand_mats()
    tf = min(256, s)
    qat, ka, kat, va, vat, qn, cfox = _fox2_prep(proj, fbl, qgb, kgb, bfor, bd64, ea, eb, ones_a, ones_b, tr=tf)
    gb, yb, lse = _fox2_fwd(proj, qat, ka, vat)
    gc = _mem_fwd(proj, mkv, qgc, kgc, bd128, tr=tr)
    y = _merge_fwd(proj, ga, gb, gc, wa, wb, wc, tr=tr)
    dout, dout_b, sq, dy = _out_loss(y, wo, x, tgt, tr=tr, tn=1024)

    d_wo = _matmul(y, dout_b, dims="tn", out_dtype=F32, tm=1024, tn=512, tk=4096, name="dw_out")
    dl0, dl1, dl2, dua, dub, duc, dga, dgb, dgc = _merge_bwd(proj, ga, gb, gc, wa, wb, wc, dy, tr=tr)
    d_wa = _matmul(ga, dua, dims="tn", out_dtype=F32, tm=768, tn=512, tk=4096, name="dw_branch_a")
    d_wb = _matmul(gb, dub, dims="tn", out_dtype=F32, tm=768, tn=512, tk=4096, name="dw_branch_b")
    d_wc = _matmul(gc, duc, dims="tn", out_dtype=F32, tm=512, tn=512, tk=4096, name="dw_branch_c")

    dproj_a, d_qga, d_kga, d_sinks = _swa_bwd(proj, qga, kga, sinks, bd64, swa_bias, dga)

    qab, qabt, dya, dyat, dzb = _fox2_bwd_pre(proj, yb, dgb, qn, cfox, lse, hsum, ea, ones_b, tr=tf)
    dqa, dkn, dvb, dck = _fox2_bwd(qab, qabt, ka, kat, va, dya, dyat)
    dqb, dkb, dfb, d_qgb, d_kgb, d_bf = _fox2_bwd_post(proj, fbl, qgb, kgb, bfor, bd64, dqa, dkn, dck, tr=tf)

    dproj_c, dmkv, d_qgc, d_kgc = _mem_bwd(proj, mkv, qgc, kgc, bd128, dgc, tr=tr)
    dmkv_b = dmkv.astype(BF16)
    d_wmk = _matmul(memn, dmkv_b, dims="tn", out_dtype=F32, tm=1024, tn=512, tk=256, name="dw_mem_kv")
    dmemn = _matmul(dmkv_b, w_mk, dims="nt", out_dtype=F32, tm=256, tn=512, tk=1024, name="dmemn")
    (d_mem_gain,) = _rms_bwd(mem, mem_norm_gain, dmemn, None, tr=mem.shape[0], name="rms_mem_bwd")

    dproj = [dproj_a, jnp.concatenate([dqb, dkb, dvb, dzb, dproj_c], axis=1), dl0, dl1, dl2]
    dhn_f = _matmul(dfb, w_fb, dims="nt", out_dtype=F32, tm=1024, tn=512, tk=LANE, name="dhn_forget")
    d_wfb = _matmul(hn, dfb, dims="tn", out_dtype=F32, tm=1024, tn=LANE, tk=512, name="dw_forget")
    big = {}
    if on_mesh:
        g1, k1 = [d_wmk, d_wa, d_wb, d_wc, d_wo], [2, 3, 4, 5, 6]
        d_other, (got1,) = _matmul(hn, dproj, dims="tn", out_dtype=BF16, tm=1024, tn=512, tk=4096, a_half=(core, True),
                                   name="dw_main_other", comms=[_exchange_comm(g1, k1)])
        h1 = _add_half(g1, got1, core, [HALF_AXIS[k] for k in k1], name="add_half_small")
        d_own, (got0, parts1) = _matmul(
            hn, dproj, dims="tn", out_dtype=F32, tm=1024, tn=512, tk=4096, a_half=(core, False), name="dw_main_own",
            comms=[_exchange_comm([d_other, d_wfb], [0, 1], whole=(0,)), _scatter_comm(h1, k1)])
        h0 = [_add_pair(d_own, got0[0], name="add_pair_main")] + _add_half([d_wfb], [got0[1]], core, [0], name="add_half_forget")
        sums1 = _sum4(parts1, name="sum4_small")
        dhn, (parts0, theirs1) = _matmul(dproj, w_main, dims="nt", out_dtype=F32, tm=1024, tn=512, tk=2048, vmem=VMEM_WIDE, name="dhn",
                                         add=dhn_f, comms=[_scatter_comm(h0, [0, 1]), _swap_comm(sums1)])
        sums0 = _sum4(parts0, name="sum4_main")
        (grad_x, d_gain), theirs0 = _rms_bwd(x, norm_gain, dhn, dout, tr=tr, name="rms_x_bwd", comm=_swap_comm(sums0))
        big = dict(sums=sums0 + sums1, theirs=list(theirs0) + list(theirs1))
    else:
        dhn = _matmul(dproj, w_main, dims="nt", out_dtype=F32, tm=1024, tn=512, tk=2048, vmem=VMEM_WIDE, name="dhn", add=dhn_f)
        d_wmain = _matmul(hn, dproj, dims="tn", out_dtype=F32, tm=1024, tn=512, tk=4096, name="dw_main")
        big = dict(d_wmain=d_wmain, d_wfb=d_wfb, d_wmk=d_wmk, d_wa=d_wa, d_wb=d_wb, d_wc=d_wc, d_wo=d_wo)
        grad_x, d_gain = _rms_bwd(x, norm_gain, dhn, dout, tr=tr, name="rms_x_bwd")

    fold = lambda g, reps: jnp.sum(g.reshape(reps, -1), axis=0, keepdims=True)
    return dict(
        sq=sq, grad_x=grad_x, **big,
        d_gain=d_gain, d_mem_gain=d_mem_gain, d_bf=d_bf[:, :N_FORGET],
        d_qga=fold(d_qga, 12), d_kga=fold(d_kga, 4), d_sinks=d_sinks[:, :A_HEADS],
        d_qgb=fold(d_qgb, 12), d_kgb=fold(d_kgb, 12), d_qgc=fold(d_qgc, 4), d_kgc=fold(d_kgc, 4))


PACK_ROWS = 256
FORGET_IN_SHARD = FORGET_COL - SHARD_COLS
AFTER_FORGET = FORGET_COL - SLAB_START[1]
END_CHIP1 = 2 * SHARD_COLS - N_FORGET - SLAB_START[1]


def _pack_w_in(chip, w):
    rows = w.shape[1]
    tr = PACK_ROWS

    def body(k_ref, w_ref, o_ref, scr):
        scr[...] = jnp.zeros_like(scr)
        scr[pl.ds(0, SHARD_COLS), :] = w_ref[...]
        v = jnp.transpose(scr[...])
        k = k_ref[0]
        col = lax.broadcasted_iota(jnp.int32, (tr, SLAB), 1)
        no_forget = jnp.zeros((tr, LANE), BF16)

        @pl.when(k == 0)
        def _():
            o_ref[:, 0:SLAB] = v.astype(BF16)
            o_ref[:, SLAB:] = no_forget

        @pl.when(k == 1)
        def _():
            before = pltpu.roll(v, SLAB_SHIFT[1], axis=1)
            after = pltpu.roll(v, SLAB - (N_FORGET - SLAB_SHIFT[1]), axis=1)
            slab = jnp.where(col < AFTER_FORGET, before, jnp.where(col < END_CHIP1, after, 0.0))
            o_ref[:, 0:SLAB] = slab.astype(BF16)
            f = pltpu.roll(v, SLAB - FORGET_IN_SHARD, axis=1)[:, :LANE]
            o_ref[:, SLAB:] = jnp.where(col[:, :LANE] < N_FORGET, f, 0.0).astype(BF16)

        for kk in (2, 3):
            @pl.when(k == kk)
            def _(kk=kk):
                o_ref[:, 0:SLAB] = pltpu.roll(v, SLAB_SHIFT[kk], axis=1).astype(BF16)
                o_ref[:, SLAB:] = no_forget

    return pl.pallas_call(
        body, grid_spec=pltpu.PrefetchScalarGridSpec(
            num_scalar_prefetch=1, grid=(rows // tr,),
            in_specs=[pl.BlockSpec((SHARD_COLS, tr), lambda i, k: (0, i))],
            out_specs=pl.BlockSpec((None, tr, SLAB + LANE), lambda i, k: (k[0], i, 0)),
            scratch_shapes=[pltpu.VMEM((SLAB, tr), F32)]),
        out_shape=SDS((N_CHIPS, rows, SLAB + LANE), BF16), name="pack_w_in",
        compiler_params=_params(("arbitrary",)))(chip, w)


def _merge_slabs(g):
    rows = g.shape[1]
    tr = PACK_ROWS
    t = [s // LANE for s in SLAB_START]
    n_t = SLAB // LANE

    def body(g_ref, m_ref, f_ref):
        for k in range(N_CHIPS):
            lo = t[k] + (1 if k > 0 else 0)
            hi = t[k + 1] if k + 1 < N_CHIPS else t[k] + n_t
            m_ref[:, lo * LANE:hi * LANE] = g_ref[k, :, (lo - t[k]) * LANE:(hi - t[k]) * LANE]
            if k + 1 < N_CHIPS:
                a = g_ref[k, :, (hi - t[k]) * LANE:(hi - t[k] + 1) * LANE].astype(F32)
                b = g_ref[k + 1, :, 0:LANE].astype(F32)
                m_ref[:, hi * LANE:(hi + 1) * LANE] = (a + b).astype(BF16)
        f_ref[...] = g_ref[1, :, SLAB:]

    return pl.pallas_call(
        body, grid=(rows // tr,),
        in_specs=[pl.BlockSpec((N_CHIPS, tr, SLAB + LANE), lambda i: (0, i, 0))],
        out_specs=[_rowblk(tr, P_MAIN), _rowblk(tr, LANE)],
        out_shape=[SDS((rows, P_MAIN), BF16), SDS((rows, LANE), BF16)], name="merge_slabs",
        compiler_params=_params(("parallel",)))(g)


def _adamw_math(w, g, m, v):
    nm = ADAM_B1 * m + (1.0 - ADAM_B1) * g
    nv = ADAM_B2 * v + (1.0 - ADAM_B2) * (g * g)
    m_hat = nm / (1.0 - ADAM_B1 ** ADAM_STEP)
    v_hat = nv / (1.0 - ADAM_B2 ** ADAM_STEP)
    delta = -ADAM_LR * (m_hat / (jnp.sqrt(v_hat) + ADAM_EPS) + ADAM_WD * w)
    return delta, nm, nv


def _adamw(g, w, m, v, *, tr, name):
    rows, cols = w.shape
    tr = min(tr, rows)

    def body(g_ref, w_ref, m_ref, v_ref, d_ref, nm_ref, nv_ref):
        d, nm, nv = _adamw_math(w_ref[...], g_ref[...], m_ref[...], v_ref[...])
        d_ref[...] = d
        nm_ref[...] = nm
        nv_ref[...] = nv

    spec = _rowblk(tr, cols)
    return pl.pallas_call(
        body, grid=(rows // tr,), in_specs=[spec] * 4, out_specs=[spec] * 3,
        out_shape=[SDS((rows, cols), F32)] * 3, name=name, compiler_params=_params(("parallel",)))(g, w, m, v)


def _adamw_w_in(chip_core, slab_mine, slab_theirs, forget_mine, forget_theirs, w, m, v):
    rows = w.shape[1]
    tr = PACK_ROWS // 2
    nbh = rows // 2 // tr

    def body(k_ref, sa_ref, sb_ref, fa_ref, fb_ref, w_ref, m_ref, v_ref, g_ref, d_ref, nm_ref, nv_ref):
        use_mine = pl.program_id(0) // nbh == k_ref[1]
        sl = jnp.where(use_mine, sa_ref[...], sb_ref[...])
        f_tile = jnp.where(use_mine, fa_ref[...], fb_ref[...])
        k = k_ref[0]

        def emit(wide):
            g = jnp.transpose(wide)[:SHARD_COLS, :]
            g_ref[...] = g
            d, nm, nv = _adamw_math(w_ref[...], g, m_ref[...], v_ref[...])
            d_ref[...] = d
            nm_ref[...] = nm
            nv_ref[...] = nv

        @pl.when(k == 0)
        def _():
            emit(sl)

        @pl.when(k == 1)
        def _():
            col = lax.broadcasted_iota(jnp.int32, (tr, SLAB), 1)
            before = pltpu.roll(sl, SLAB - SLAB_SHIFT[1], axis=1)
            after = pltpu.roll(sl, N_FORGET - SLAB_SHIFT[1], axis=1)
            wide_f = jnp.concatenate([f_tile, jnp.zeros((tr, SLAB - LANE), F32)], axis=1)
            forget = pltpu.roll(wide_f, FORGET_IN_SHARD, axis=1)
            emit(jnp.where(col < FORGET_IN_SHARD, before, jnp.where(col < FORGET_IN_SHARD + N_FORGET, forget, after)))

        for kk in (2, 3):
            @pl.when(k == kk)
            def _(kk=kk):
                emit(pltpu.roll(sl, SLAB - SLAB_SHIFT[kk], axis=1))

    nat = pl.BlockSpec((SHARD_COLS, tr), lambda i, k: (0, i))
    half = lambda width: pl.BlockSpec((tr, width), lambda i, k: (i % nbh, 0))
    return pl.pallas_call(
        body, grid_spec=pltpu.PrefetchScalarGridSpec(
            num_scalar_prefetch=1, grid=(rows // tr,),
            in_specs=[half(SLAB), half(SLAB), half(LANE), half(LANE), nat, nat, nat],
            out_specs=[nat] * 4),
        out_shape=[SDS((SHARD_COLS, rows), F32)] * 4, name="adamw_w_in",
        compiler_params=_params(("arbitrary",)))(chip_core, slab_mine, slab_theirs, forget_mine, forget_theirs, w, m, v)


ANY = pl.BlockSpec(memory_space=pl.ANY)
HALF_AXIS = (0, 0, 1, 0, 0, 0, 1)


def _me():
    return lax.axis_index("x"), lax.axis_index("y"), lax.axis_index("c")


def _half(ref, which, axis):
    n = ref.shape[axis] // 2
    sl = pl.ds(which * n, n)
    return ref.at[sl] if axis == 0 else ref.at[:, sl]


def _piece(t, ref, j):
    if t == 0:
        return ref.at[:, pl.ds(SLAB_START[j], SLAB)]
    if t == 1:
        return ref
    if t in (2, 6):
        return ref.at[pl.ds(512 * j, 512)]
    return ref.at[:, pl.ds(512 * j, 512)]


def _piece_shape(t, shape):
    if t == 0:
        return (shape[0], SLAB)
    if t == 1:
        return shape
    if t in (2, 6):
        return (512, shape[1])
    return (shape[0], 512)


def _gather_plan(ins, outs, own_slot_in_src):
    x, y, c = _me()
    k = 2 * x + y
    sib = (x, y, 1 - c)
    chips = [(1 - x, y), (x, 1 - y), (1 - x, 1 - y)]
    n = len(outs)

    def rows(t, which):
        h = outs[t].shape[1] // 2
        return pl.ds(which * h, h)

    def mine(t):
        return ins[t].at[k, rows(t, c)] if own_slot_in_src else ins[t].at[rows(t, c)]

    def first(t, j, sems):
        chip = chips[j]
        return pltpu.make_async_remote_copy(
            src_ref=mine(t), dst_ref=outs[t].at[k, rows(t, c)], send_sem=sems[0].at[t, j], recv_sem=sems[1].at[t, j],
            device_id=(chip[0], chip[1], c), device_id_type=MESH)

    def landed(t, j, sems):
        chip = chips[j]
        return pltpu.make_async_remote_copy(
            src_ref=mine(t), dst_ref=outs[t].at[2 * chip[0] + chip[1], rows(t, c)], send_sem=sems[0].at[t, j],
            recv_sem=sems[1].at[t, j], device_id=(chip[0], chip[1], c), device_id_type=MESH)

    def passed(t, j, which, sems):
        chip = chips[j]
        blk = outs[t].at[2 * chip[0] + chip[1], rows(t, which)]
        return pltpu.make_async_remote_copy(
            src_ref=blk, dst_ref=blk, send_sem=sems[2].at[t, j], recv_sem=sems[3].at[t, j], device_id=sib,
            device_id_type=MESH)

    def start(sems):
        for j in range(3):
            for t in range(n):
                first(t, j, sems).start()

    def finish(sems):
        for j in range(3):
            for t in range(n):
                landed(t, j, sems).wait_recv()
                passed(t, j, c, sems).start()
        for j in range(3):
            for t in range(n):
                passed(t, j, 1 - c, sems).wait_recv()
        for j in range(3):
            for t in range(n):
                first(t, j, sems).wait_send()
                passed(t, j, c, sems).wait_send()

    return k, start, finish


def _all_gather_slabs(slabs):
    def body(in_ref, out_ref, nbr_sem, quarter_sem, pass_sem):
        x, y, c = _me()
        k = 2 * x + y
        rows = out_ref.shape[1]
        h, q = rows // 2, rows // 4
        nbrs = [(1 - x, y), (x, 1 - y)]
        slot = lambda chip: 2 * chip[0] + chip[1]
        diag = 2 * (1 - x) + (1 - y)
        half = pl.ds(c * h, h)
        quarter = lambda a: pl.ds(c * h + a * q, q)

        def first(a):
            return pltpu.make_async_remote_copy(
                src_ref=in_ref.at[k, half], dst_ref=out_ref.at[k, half], send_sem=nbr_sem.at[0, a],
                recv_sem=nbr_sem.at[1, a], device_id=(nbrs[a][0], nbrs[a][1], c), device_id_type=MESH)

        def landed(a):
            blk = out_ref.at[slot(nbrs[a]), half]
            return pltpu.make_async_remote_copy(
                src_ref=blk, dst_ref=blk, send_sem=nbr_sem.at[0, a], recv_sem=nbr_sem.at[1, a],
                device_id=(nbrs[a][0], nbrs[a][1], c), device_id_type=MESH)

        def relay(a):
            blk = out_ref.at[slot(nbrs[a]), quarter(a)]
            to = nbrs[1 - a]
            return pltpu.make_async_remote_copy(
                src_ref=blk, dst_ref=blk, send_sem=quarter_sem.at[0, a], recv_sem=quarter_sem.at[1, a],
                device_id=(to[0], to[1], c), device_id_type=MESH)

        def relayed(a):
            blk = out_ref.at[diag, quarter(a)]
            frm = nbrs[1 - a]
            return pltpu.make_async_remote_copy(
                src_ref=blk, dst_ref=blk, send_sem=quarter_sem.at[0, a], recv_sem=quarter_sem.at[1, a],
                device_id=(frm[0], frm[1], c), device_id_type=MESH)

        def passed(j, which):
            sl = diag if j == 2 else slot(nbrs[j])
            blk = out_ref.at[sl, pl.ds(which * h, h)]
            return pltpu.make_async_remote_copy(
                src_ref=blk, dst_ref=blk, send_sem=pass_sem.at[0, j], recv_sem=pass_sem.at[1, j],
                device_id=(x, y, 1 - c), device_id_type=MESH)

        for a in range(2):
            first(a).start()
        for a in range(2):
            landed(a).wait_recv()
            relay(a).start()
            passed(a, c).start()
        for a in range(2):
            relayed(a).wait_recv()
        passed(2, c).start()
        for j in range(3):
            passed(j, 1 - c).wait_recv()
        for a in range(2):
            first(a).wait_send()
            relay(a).wait_send()
        for j in range(3):
            passed(j, c).wait_send()

    return pl.pallas_call(
        body, in_specs=[ANY], out_specs=ANY, out_shape=SDS(slabs.shape, slabs.dtype),
        scratch_shapes=[pltpu.SemaphoreType.DMA((2, 2)), pltpu.SemaphoreType.DMA((2, 2)), pltpu.SemaphoreType.DMA((2, 3))],
        input_output_aliases={0: 0}, name="all_gather_slabs")(slabs)


def _gather_comm(parts):
    n = len(parts)

    def start(ins, outs, sems):
        k, go, _ = _gather_plan(ins, outs, False)
        for t in range(n):
            pltpu.make_async_copy(ins[t], outs[t].at[k], sems[4].at[t]).start()
        go(sems)

    def finish(ins, outs, sems):
        k, _, done = _gather_plan(ins, outs, False)
        done(sems)
        for t in range(n):
            pltpu.make_async_copy(ins[t], outs[t].at[k], sems[4].at[t]).wait()

    return _Comm(parts, [SDS((N_CHIPS,) + p.shape, p.dtype) for p in parts],
                 [pltpu.SemaphoreType.DMA((n, 3))] * 4 + [pltpu.SemaphoreType.DMA((n,))], start, finish)


def _exchange_comm(arrs, kinds, whole=()):
    n = len(arrs)

    def copies(ins, outs, sems):
        x, y, c = _me()
        return [pltpu.make_async_remote_copy(
            src_ref=ins[t] if t in whole else _half(ins[t], 1 - c, HALF_AXIS[kinds[t]]), dst_ref=outs[t],
            send_sem=sems[0].at[t], recv_sem=sems[1].at[t], device_id=(x, y, 1 - c), device_id_type=MESH)
            for t in range(n)]

    def start(ins, outs, sems):
        for cp in copies(ins, outs, sems):
            cp.start()

    def finish(ins, outs, sems):
        for cp in copies(ins, outs, sems):
            cp.wait()

    def hshape(t):
        s = list(arrs[t].shape)
        if t not in whole:
            s[HALF_AXIS[kinds[t]]] //= 2
        return SDS(tuple(s), arrs[t].dtype)

    return _Comm(arrs, [hshape(t) for t in range(n)], [pltpu.SemaphoreType.DMA((n,))] * 2, start, finish)


ADD_HALF_STEPS = 2


def _add_half(fulls, gots, core, axes, *, name):
    ns = ADD_HALF_STEPS
    n = len(fulls)
    in_specs, out_specs, out_shape = [], [], []
    for got, axis in zip(gots, axes):
        r, c = got.shape
        blk = (r // ns, c)
        if axis == 0:
            in_specs.append(pl.BlockSpec(blk, lambda i, cr: (i + cr[0] * ns, 0)))
        else:
            in_specs.append(pl.BlockSpec(blk, lambda i, cr: (i, cr[0])))
        out_specs.append(pl.BlockSpec(blk, lambda i, cr: (i, 0)))
        out_shape.append(SDS((r, c), BF16))
    in_specs += list(out_specs)

    def body(c_ref, *refs):
        for a_ref, b_ref, o_ref in zip(refs[:n], refs[n:2 * n], refs[2 * n:]):
            o_ref[...] = (a_ref[...] + b_ref[...]).astype(BF16)

    return list(pl.pallas_call(
        body, grid_spec=pltpu.PrefetchScalarGridSpec(num_scalar_prefetch=1, grid=(ns,), in_specs=in_specs,
                                                     out_specs=out_specs),
        out_shape=out_shape, name=name, compiler_params=_params(("parallel",)))(core, *fulls, *gots))


def _add_pair(a, b, *, name):
    r, c = a.shape
    br, bc = 256, min(2048, c)

    def body(a_ref, b_ref, o_ref):
        o_ref[...] = (a_ref[...] + b_ref[...].astype(F32)).astype(BF16)

    spec = pl.BlockSpec((br, bc), lambda i, j: (i, j))
    return pl.pallas_call(body, grid=(r // br, c // bc), in_specs=[spec, spec], out_specs=spec,
                          out_shape=SDS((r, c), BF16), name=name, compiler_params=_params(("parallel", "parallel")))(a, b)


def _scatter_comm(halves, kinds):
    n = len(halves)

    def plan(ins, outs, sems):
        send, recv, lsem = sems
        x, y, c = _me()
        k = 2 * x + y

        def to_chip(t, j):
            return pltpu.make_async_remote_copy(
                src_ref=_piece(kinds[t], ins[t], j), dst_ref=outs[t].at[k], send_sem=send.at[t, j],
                recv_sem=recv.at[t, k], device_id=(j // 2, j % 2, c), device_id_type=MESH)

        def from_chip(t, j):
            return pltpu.make_async_remote_copy(
                src_ref=_piece(kinds[t], ins[t], j), dst_ref=outs[t].at[j], send_sem=send.at[t, j],
                recv_sem=recv.at[t, j], device_id=(j // 2, j % 2, c), device_id_type=MESH)

        def own(t, j):
            return pltpu.make_async_copy(_piece(kinds[t], ins[t], j), outs[t].at[j], lsem.at[t])

        return k, to_chip, from_chip, own

    def start(ins, outs, sems):
        k, to_chip, _, own = plan(ins, outs, sems)
        for j in range(N_CHIPS):
            @pl.when(k != j)
            def _(j=j):
                for t in range(n):
                    to_chip(t, j).start()

            @pl.when(k == j)
            def _(j=j):
                for t in range(n):
                    own(t, j).start()

    def finish(ins, outs, sems):
        k, to_chip, from_chip, own = plan(ins, outs, sems)
        for j in range(N_CHIPS):
            @pl.when(k != j)
            def _(j=j):
                for t in range(n):
                    from_chip(t, j).wait_recv()
                for t in range(n):
                    to_chip(t, j).wait_send()

            @pl.when(k == j)
            def _(j=j):
                for t in range(n):
                    own(t, j).wait()

    return _Comm(halves, [SDS((N_CHIPS,) + _piece_shape(kinds[t], halves[t].shape), halves[t].dtype) for t in range(n)],
                 [pltpu.SemaphoreType.DMA((n, N_CHIPS))] * 2 + [pltpu.SemaphoreType.DMA((n,))], start, finish)


SUM4_STEPS = 2


def _sum4(ps, *, name):
    ns = SUM4_STEPS

    def body(*refs):
        for p_ref, o_ref in zip(refs[:len(ps)], refs[len(ps):]):
            o_ref[...] = ((p_ref[0].astype(F32) + p_ref[1].astype(F32)) + p_ref[2].astype(F32)) + p_ref[3].astype(F32)

    return pl.pallas_call(
        body, grid=(ns,),
        in_specs=[pl.BlockSpec((N_CHIPS, p.shape[1] // ns, p.shape[2]), lambda i: (0, i, 0)) for p in ps],
        out_specs=[_rowblk(p.shape[1] // ns, p.shape[2]) for p in ps],
        out_shape=[SDS(p.shape[1:], F32) for p in ps], name=name, compiler_params=_params(("parallel",)))(*ps)


def _swap_comm(sums):
    return _exchange_comm(sums, [None] * len(sums), whole=tuple(range(len(sums))))


ADAMW_STEPS = 4


def _adamw_halves(items, core):
    ns = ADAMW_STEPS
    nbh = ns // 2
    n = len(items)
    in_specs, out_specs, out_shape, ins = [], [], [], []
    for mine, theirs, w, m, v, axis in items:
        rows, cols = w.shape
        tr = rows // ns
        if axis == 0:
            g_spec = pl.BlockSpec((tr, cols), lambda i, cr: (i % nbh, 0))
        else:
            g_spec = pl.BlockSpec((tr, cols // 2), lambda i, cr: (i, 0))
        nat = pl.BlockSpec((tr, cols), lambda i, cr: (i, 0))
        in_specs += [g_spec, g_spec, nat, nat, nat]
        out_specs += [nat] * 4
        out_shape += [SDS((rows, cols), F32)] * 4
        ins += [mine, theirs, w, m, v]

    def body(c_ref, *refs):
        for t, item in enumerate(items):
            a_ref, b_ref, w_ref, m_ref, v_ref = refs[5 * t:5 * t + 5]
            g_ref, d_ref, nm_ref, nv_ref = refs[5 * n + 4 * t:5 * n + 4 * t + 4]
            a, b = a_ref[...], b_ref[...]
            if item[5] == 0:
                g = jnp.where(pl.program_id(0) // nbh == c_ref[0], a, b)
            else:
                low = c_ref[0] == 0
                g = jnp.concatenate([jnp.where(low, a, b), jnp.where(low, b, a)], axis=1)
            g_ref[...] = g
            d, nm, nv = _adamw_math(w_ref[...], g, m_ref[...], v_ref[...])
            d_ref[...] = d
            nm_ref[...] = nm
            nv_ref[...] = nv

    res = pl.pallas_call(
        body, grid_spec=pltpu.PrefetchScalarGridSpec(
            num_scalar_prefetch=1, grid=(ns,), in_specs=in_specs, out_specs=out_specs),
        out_shape=out_shape, name="adamw_shards", compiler_params=_params(("arbitrary",)))(core, *ins)
    return [tuple(res[4 * t:4 * t + 4]) for t in range(n)]


SMALL_ROWS, SMALL_COLS = 8, 1024


def _pack_small(vs):
    flat = jnp.concatenate([v.reshape(-1) for v in vs])
    return jnp.pad(flat, (0, SMALL_ROWS * SMALL_COLS - flat.shape[0])).reshape(SMALL_ROWS, SMALL_COLS)


def _unpack_small(packed, sizes):
    flat = packed.reshape(-1)
    out, o = [], 0
    for n in sizes:
        out.append(flat[o:o + n].reshape(1, n))
        o += n
    return out


def _all_reduce_small(v):
    n_dev = 8

    def body(v_ref, o_ref, land, send, recv):
        x, y, c = _me()
        me = 4 * x + 2 * y + c
        land[me] = v_ref[...]
        cps = []
        for r in range(1, n_dev):
            fx, fy, fc = (r >> 2) & 1, (r >> 1) & 1, r & 1
            peer = (x ^ fx, y ^ fy, c ^ fc)
            cps.append(pltpu.make_async_remote_copy(
                src_ref=v_ref, dst_ref=land.at[me], send_sem=send.at[r - 1], recv_sem=recv.at[r - 1],
                device_id=peer, device_id_type=MESH))
        for cp in cps:
            cp.start()
        for r in range(1, n_dev):
            fx, fy, fc = (r >> 2) & 1, (r >> 1) & 1, r & 1
            src = 4 * (x ^ fx) + 2 * (y ^ fy) + (c ^ fc)
            pltpu.make_async_remote_copy(
                src_ref=v_ref, dst_ref=land.at[src], send_sem=send.at[r - 1], recv_sem=recv.at[r - 1],
                device_id=(x ^ fx, y ^ fy, c ^ fc), device_id_type=MESH).wait_recv()
        for cp in cps:
            cp.wait_send()
        acc = land[0]
        for r in range(1, n_dev):
            acc = acc + land[r]
        o_ref[...] = acc

    vm = pl.BlockSpec(memory_space=pltpu.VMEM)
    return pl.pallas_call(
        body, in_specs=[vm], out_specs=vm, out_shape=SDS(v.shape, F32),
        scratch_shapes=[pltpu.VMEM((n_dev,) + v.shape, F32), pltpu.SemaphoreType.DMA((n_dev - 1,)),
                        pltpu.SemaphoreType.DMA((n_dev - 1,))],
        name="all_reduce_small")(v)


def kernel(x, mem, norm_gain, mem_norm_gain, w_in, b_forget, q_gain_a, k_gain_a, sinks_a, q_gain_b, k_gain_b, q_gain_c, k_gain_c, w_mem_kv, w_branch_a, w_branch_b, w_branch_c, w_out, loss_target, m_norm_gain, m_mem_norm_gain, m_w_in, m_b_forget, m_q_gain_a, m_k_gain_a, m_sinks_a, m_q_gain_b, m_k_gain_b, m_q_gain_c, m_k_gain_c, m_w_mem_kv, m_w_branch_a, m_w_branch_b, m_w_branch_c, m_w_out, v_norm_gain, v_mem_norm_gain, v_w_in, v_b_forget, v_q_gain_a, v_k_gain_a, v_sinks_a, v_q_gain_b, v_k_gain_b, v_q_gain_c, v_k_gain_c, v_w_mem_kv, v_w_branch_a, v_w_branch_b, v_w_branch_c, v_w_out):
    xi, yi, ci = lax.axis_index("x"), lax.axis_index("y"), lax.axis_index("c")
    chip = jnp.reshape(2 * xi + yi, (1,)).astype(jnp.int32)
    core = jnp.reshape(ci, (1,)).astype(jnp.int32)

    slabs = _pack_w_in(chip, jnp.transpose(w_in[0]))
    mine = [w_mem_kv[0].astype(BF16), w_branch_a[0].astype(BF16), w_branch_b[0].astype(BF16),
            w_branch_c[0].astype(BF16), w_out[0].astype(BF16)]
    w_main, w_fb = _merge_slabs(_all_gather_slabs(slabs))

    r = _local_step(x[0], mem[0], loss_target[0], w_main, w_fb, mine, norm_gain, mem_norm_gain,
                    b_forget, q_gain_a, k_gain_a, sinks_a, q_gain_b, k_gain_b, q_gain_c, k_gain_c, core=core)
    sums, theirs = r["sums"], r["theirs"]

    small_names = ["d_gain", "d_mem_gain", "d_bf", "d_qga", "d_kga", "d_sinks", "d_qgb", "d_kgb", "d_qgc", "d_kgc"]
    loss_part = (0.5 / D_MODEL) * jnp.sum(r["sq"], axis=1, keepdims=True)
    packed = _pack_small([r[n] for n in small_names] + [loss_part])
    red = _all_reduce_small(packed)
    small_w = [norm_gain, mem_norm_gain, b_forget, q_gain_a, k_gain_a, sinks_a, q_gain_b, k_gain_b, q_gain_c, k_gain_c]
    small_m = [m_norm_gain, m_mem_norm_gain, m_b_forget, m_q_gain_a, m_k_gain_a, m_sinks_a, m_q_gain_b, m_k_gain_b,
               m_q_gain_c, m_k_gain_c]
    small_v = [v_norm_gain, v_mem_norm_gain, v_b_forget, v_q_gain_a, v_k_gain_a, v_sinks_a, v_q_gain_b, v_k_gain_b,
               v_q_gain_c, v_k_gain_c]
    sizes = [w.shape[1] for w in small_w]
    s_d, s_m, s_v = _adamw(red, _pack_small(small_w), _pack_small(small_m), _pack_small(small_v), tr=8, name="adamw_small")
    g_small = _unpack_small(red, sizes + [1])
    loss = g_small[-1].reshape(())
    d_small, m_small, v_small = _unpack_small(s_d, sizes), _unpack_small(s_m, sizes), _unpack_small(s_v, sizes)

    gw_in, dw_in, mw_in, vw_in = _adamw_w_in(jnp.concatenate([chip, core]), sums[0], theirs[0], sums[1], theirs[1],
                                             jnp.transpose(w_in[0]), jnp.transpose(m_w_in[0]), jnp.transpose(v_w_in[0]))
    shards = ((2, "w_mem_kv", w_mem_kv, m_w_mem_kv, v_w_mem_kv),
              (3, "w_branch_a", w_branch_a, m_w_branch_a, v_w_branch_a),
              (4, "w_branch_b", w_branch_b, m_w_branch_b, v_w_branch_b),
              (5, "w_branch_c", w_branch_c, m_w_branch_c, v_w_branch_c),
              (6, "w_out", w_out, m_w_out, v_w_out))
    done = _adamw_halves([(sums[t], theirs[t], w[0], m[0], v[0], HALF_AXIS[t]) for t, _, w, m, v in shards], core)
    big = {nm: res for (_, nm, _, _, _), res in zip(shards, done)}

    def collect(kind):
        sm = (g_small, d_small, m_small, v_small)[kind]
        win = (gw_in, dw_in, mw_in, vw_in)[kind]
        return ([sm[0], sm[1], jnp.transpose(win)[None]] + [a for a in sm[2:10]]
                + [big[n][kind][None] for n in ("w_mem_kv", "w_branch_a", "w_branch_b", "w_branch_c", "w_out")])

    return (loss, r["grad_x"][None], *collect(0), *collect(1), *collect(2), *collect(3))
```

```python
import numpy as np
import jax
import jax.numpy as jnp
from jax import lax
from jax.experimental import pallas as pl
from jax.experimental.pallas import tpu as pltpu

F32 = jnp.float32
BF16 = jnp.bfloat16
HI = lax.Precision.HIGHEST
SDS = jax.ShapeDtypeStruct
MESH = pl.DeviceIdType.MESH

D_MODEL = 2048
HEAD_DIM = 64
A_HEADS = 12
A_GROUP = 3
B_HEADS = 12
C_HEADS = 4
C_HEAD_DIM = 128
WINDOW = 128
EPS = 1e-6
NEG = -1e30
LANE = 128

QA, KA, VA, ZA = 0, 768, 1024, 1280
QB, KB, VB, ZB = 2048, 2816, 3584, 4352
QC, ZC = 5120, 5632
GATE = 6144
P_MAIN = 12288
N_FORGET = 12
FORGET_COL = 5120
SHARD_COLS = 3075
SLAB = 3200
SLAB_START = (0, 3072, 6016, 9088)
SLAB_SHIFT = (0, 3, 122, 125)
N_CHIPS = 4

ADAM_LR = 0.001
ADAM_B1 = 0.9
ADAM_B2 = 0.999
ADAM_EPS = 1e-08
ADAM_WD = 0.01
ADAM_STEP = 10

VMEM_LIMIT = 56 * 1024 * 1024
VMEM_WIDE = 62 * 1024 * 1024


def _params(sem, vmem=VMEM_LIMIT):
    return pltpu.CompilerParams(dimension_semantics=sem, vmem_limit_bytes=vmem)


def _win(tr, width, off):
    return pl.BlockSpec((pl.Element(tr), pl.Element(width)), lambda i, *_: (i * tr, off))


def _rowblk(tr, width):
    return pl.BlockSpec((tr, width), lambda i, *_: (i, 0))


def _const(shape):
    nd = len(shape)
    return pl.BlockSpec(shape, lambda *_: (0,) * nd)


def _rms(x, g):
    return x * lax.rsqrt(jnp.mean(x * x, axis=-1, keepdims=True) + EPS) * g


def _head_mean_impl(x2, bd):
    hi = x2.astype(BF16)
    lo = (x2 - hi.astype(F32)).astype(BF16)
    return _dot(hi, bd) + _dot(lo, bd)


@jax.custom_vjp
def _head_mean(x2, bd):
    return _head_mean_impl(x2, bd)


_head_mean.defvjp(lambda x2, bd: (_head_mean_impl(x2, bd), bd),
                  lambda bd, g: (_head_mean_impl(g, bd), jnp.zeros_like(bd)))


def _head_norm(x, g_tiled, bd):
    return x * lax.rsqrt(_head_mean(x * x, bd) + EPS) * g_tiled


def _silu(z):
    return z * jax.nn.sigmoid(z)


def _dot_nt(a, b):
    return lax.dot_general(a, b, (((1,), (1,)), ((), ())), preferred_element_type=F32)


def _dot_tn(a, b):
    return lax.dot_general(a, b, (((0,), (0,)), ((), ())), preferred_element_type=F32)


def _dot(a, b):
    return jnp.dot(a, b, preferred_element_type=F32)


def _swa_fn(qk, vz, qkp, vzp, qg, kg, sinks, bd, bias, first):
    q = _head_norm(qk[:, :768], qg, bd)
    k2 = jnp.concatenate([qkp[:, 768:], qk[:, 768:]], axis=0)
    k2 = _head_norm(k2, kg, bd[:256, :256])
    v2 = jnp.concatenate([vzp[:, :256], vz[:, :256]], axis=0)
    z = vz[:, 256:]
    cols = A_GROUP * WINDOW
    kj = lax.broadcasted_iota(jnp.int32, (2 * WINDOW, cols), 0)
    no_prev = kj < WINDOW * first.astype(jnp.int32)
    qtb = jnp.transpose(q).astype(BF16)
    kb = k2.astype(BF16)
    vtb = jnp.transpose(v2).astype(BF16)
    outs = [None] * A_HEADS
    for g in range(A_HEADS // A_GROUP):
        heads = [A_GROUP * g + u for u in range(A_GROUP)]
        qs = jnp.concatenate([qtb[64 * h:64 * h + 64, :] for h in heads], axis=1)
        s = _dot(kb[:, 64 * g:64 * g + 64], qs) * (HEAD_DIM ** -0.5) + bias[g]
        s = jnp.where(no_prev, NEG, s)
        sink = jnp.concatenate([jnp.broadcast_to(sinks[:, h:h + 1], (1, WINDOW)) for h in heads], axis=1)
        m = lax.stop_gradient(jnp.maximum(jnp.max(s, axis=0, keepdims=True), sink))
        p = jnp.exp(s - m)
        den = jnp.sum(p, axis=0, keepdims=True) + jnp.exp(sink - m)
        o = _dot(vtb[64 * g:64 * g + 64, :], (p * (1.0 / den)).astype(BF16))
        for u, h in enumerate(heads):
            outs[h] = o[:, WINDOW * u:WINDOW * u + WINDOW]
    return jnp.transpose(jnp.concatenate(outs, axis=0)) * _silu(z)


def _swa_bias():
    qi = np.arange(WINDOW)[None, :]
    kj = np.arange(2 * WINDOW)[:, None]
    rel = qi + WINDOW - kj
    valid = (rel >= 0) & (rel < WINDOW)
    out = np.zeros((A_HEADS // A_GROUP, 2 * WINDOW, A_GROUP * WINDOW), np.float32)
    for h in range(A_HEADS):
        slope = np.float32(2.0 ** (-8.0 * (h + 1) / A_HEADS))
        blk = np.where(valid, -slope * rel.astype(np.float32), np.float32(NEG))
        g, u = divmod(h, A_GROUP)
        out[g, :, WINDOW * u:WINDOW * u + WINDOW] = blk
    return jnp.asarray(out)


def _mem_fn(qz, mkv, qg, kg, bd):
    q = _head_norm(qz[:, :512], qg, bd).astype(BF16)
    k = _head_norm(mkv[:, :512], kg, bd).astype(BF16)
    v = mkv[:, 512:].astype(BF16)
    z = qz[:, 512:]
    outs = []
    for h in range(C_HEADS):
        sl = slice(128 * h, 128 * h + 128)
        s = _dot_nt(q[:, sl], k[:, sl]) * (C_HEAD_DIM ** -0.5)
        m = lax.stop_gradient(jnp.max(s, axis=-1, keepdims=True))
        p = jnp.exp(s - m)
        den = jnp.sum(p, axis=-1, keepdims=True)
        outs.append(_dot((p * (1.0 / den)).astype(BF16), v[:, sl]))
    return jnp.concatenate(outs, axis=1) * _silu(z)


def _qn_fn(q, g, bd):
    return _head_norm(q, g, bd) * (HEAD_DIM ** -0.5)


def _kn_fn(k, g, bd):
    return _head_norm(k, g, bd)


def _block_diag(width, hd):
    i = np.arange(width) // hd
    return jnp.asarray((i[:, None] == i[None, :]).astype(np.float32) / hd, BF16)


def _head_sum(width, hd):
    i = np.arange(width) // hd
    return jnp.asarray((i[:, None] == np.arange(LANE)[None, :]).astype(np.float32))


def _rms_fwd(x, g, *, tr, name):
    rows, dm = x.shape

    def body(x_ref, g_ref, o_ref):
        o_ref[...] = _rms(x_ref[...], g_ref[...]).astype(BF16)

    return pl.pallas_call(
        body, grid=(rows // tr,),
        in_specs=[_rowblk(tr, dm), _const((1, dm))],
        out_specs=_rowblk(tr, dm),
        out_shape=SDS((rows, dm), BF16), name=name,
        compiler_params=_params(("parallel",)))(x, g)


def _rms_bwd(x, g, dy, resid, *, tr, name, comm=None):
    rows, dm = x.shape
    want_dx = resid is not None
    n_in = 4 if want_dx else 3
    n_out = 2 if want_dx else 1
    c_in = len(comm.ins) if comm else 0
    c_out = len(comm.out_shapes) if comm else 0
    nb = rows // tr

    def body(*refs):
        x_ref, g_ref, dy_ref = refs[:3]
        r_ref = refs[3] if want_dx else None
        cin = refs[n_in:n_in + c_in]
        outs = refs[n_in + c_in:n_in + c_in + n_out]
        dg_ref = outs[-1]
        cout = refs[n_in + c_in + n_out:n_in + c_in + n_out + c_out]
        csem = refs[n_in + c_in + n_out + c_out:]

        if comm:
            @pl.when(pl.program_id(0) == 0)
            def _():
                comm.start(cin, cout, csem)

        _, vjp = jax.vjp(_rms, x_ref[...], g_ref[...])
        dx, dg = vjp(dy_ref[...])

        @pl.when(pl.program_id(0) == 0)
        def _():
            dg_ref[...] = jnp.zeros_like(dg_ref)

        dg_ref[...] += dg
        if want_dx:
            outs[0][...] = r_ref[...] + dx

        if comm:
            @pl.when(pl.program_id(0) == nb - 1)
            def _():
                comm.finish(cin, cout, csem)

    hbm = pl.BlockSpec(memory_space=pl.ANY)
    ins = [x, g, dy] + ([resid] if want_dx else []) + (list(comm.ins) if comm else [])
    in_specs = ([_rowblk(tr, dm), _const((1, dm)), _rowblk(tr, dm)] + ([_rowblk(tr, dm)] if want_dx else [])
                + [hbm] * c_in)
    out_specs = ([_rowblk(tr, dm)] if want_dx else []) + [_const((1, dm))] + [hbm] * c_out
    out_shape = (([SDS((rows, dm), F32)] if want_dx else []) + [SDS((1, dm), F32)]
                 + (list(comm.out_shapes) if comm else []))
    res = pl.pallas_call(
        body, grid=(nb,), in_specs=in_specs, out_specs=out_specs, out_shape=out_shape,
        scratch_shapes=list(comm.sems) if comm else [], name=name, compiler_params=_params(("arbitrary",)))(*ins)
    return (list(res[:n_out]), list(res[n_out:])) if comm else res


class _Comm:
    def __init__(self, ins, out_shapes, sems, start, finish):
        self.ins, self.out_shapes, self.sems, self.start, self.finish = list(ins), list(out_shapes), list(sems), start, finish


def _matmul(a, b, *, dims, out_dtype, tm, tn, tk, name, add=None, comms=(), vmem=VMEM_LIMIT, a_half=None):
    a_list = list(a) if isinstance(a, (list, tuple)) else [a]
    b_list = list(b) if isinstance(b, (list, tuple)) else [b]
    assert len(a_list) == 1 or dims == "nt"
    assert len(b_list) == 1 or dims == "tn"
    if dims == "tn":
        kdim, m = a_list[0].shape
        if a_half is not None:
            m //= 2
    else:
        m, kdim = a_list[0].shape[0], sum(p.shape[1] for p in a_list)
    n = b_list[0].shape[0] if dims == "nt" else sum(p.shape[1] for p in b_list)
    tm, tn, tk = min(tm, m), min(tn, n), min(tk, kdim)
    assert m % tm == 0 and n % tn == 0 and kdim % tk == 0, (name, m, n, kdim)
    ni, nj, nk = m // tm, n // tn, kdim // tk
    a_rng, b_rng, pos = [], [], 0
    for p in a_list:
        assert len(a_list) == 1 or p.shape[1] % tk == 0
        a_rng.append((pos, p.shape[1] // tk if len(a_list) > 1 else nk))
        pos += a_rng[-1][1]
    pos = 0
    for p in b_list:
        assert len(b_list) == 1 or p.shape[1] % tn == 0
        b_rng.append((pos, p.shape[1] // tn if len(b_list) > 1 else nj))
        pos += b_rng[-1][1]
    has_add = add is not None
    n_mm_in = len(a_list) + len(b_list) + (1 if has_add else 0)
    c_in = [len(c.ins) for c in comms]
    c_out = [len(c.out_shapes) for c in comms]
    c_sem = [len(c.sems) for c in comms]

    def body(*refs):
        if a_half is not None:
            refs = refs[1:]
        a_refs, b_refs = refs[:len(a_list)], refs[len(a_list):len(a_list) + len(b_list)]
        add_ref = refs[n_mm_in - 1] if has_add else None
        pos = n_mm_in
        cin = []
        for cnt in c_in:
            cin.append(refs[pos:pos + cnt])
            pos += cnt
        o_ref = refs[pos]
        pos += 1
        cout = []
        for cnt in c_out:
            cout.append(refs[pos:pos + cnt])
            pos += cnt
        acc = refs[pos]
        pos += 1
        csem = []
        for cnt in c_sem:
            csem.append(refs[pos:pos + cnt])
            pos += cnt
        i, j, k = pl.program_id(0), pl.program_id(1), pl.program_id(2)

        if comms:
            @pl.when((i == 0) & (j == 0) & (k == 0))
            def _():
                for c, ci, co, cs in zip(comms, cin, cout, csem):
                    c.start(ci, co, cs)

        def accumulate(a_ref, b_ref, first_k, later_k):
            if dims == "nn":
                part = _dot(a_ref[...], b_ref[...])
            elif dims == "nt":
                part = _dot_nt(a_ref[...], b_ref[...])
            else:
                part = _dot_tn(a_ref[...], b_ref[...])

            if first_k:
                @pl.when(k == 0)
                def _():
                    acc[...] = part + add_ref[...] if has_add else part

            if later_k:
                @pl.when(k > 0)
                def _():
                    acc[...] += part

        if len(a_list) > 1:
            for a_ref, (k0, cnt) in zip(a_refs, a_rng):
                @pl.when((k >= k0) & (k < k0 + cnt))
                def _(a_ref=a_ref, k0=k0, cnt=cnt):
                    accumulate(a_ref, b_refs[0], k0 == 0, k0 + cnt > 1)
        elif len(b_list) > 1:
            for b_ref, (j0, cnt) in zip(b_refs, b_rng):
                @pl.when((j >= j0) & (j < j0 + cnt))
                def _(b_ref=b_ref):
                    accumulate(a_refs[0], b_ref, True, nk > 1)
        else:
            accumulate(a_refs[0], b_refs[0], True, nk > 1)

        @pl.when(k == nk - 1)
        def _():
            o_ref[...] = acc[...].astype(out_dtype)

        if comms:
            @pl.when((i == ni - 1) & (j == nj - 1) & (k == nk - 1))
            def _():
                for c, ci, co, cs in zip(comms, cin, cout, csem):
                    c.finish(ci, co, cs)

    def a_spec(k0, cnt):
        if dims == "tn":
            if a_half is None:
                return pl.BlockSpec((tk, tm), lambda i, j, k: (k, i))
            pick = (lambda c: 1 - c[0]) if a_half[1] else (lambda c: c[0])
            return pl.BlockSpec((tk, tm), lambda i, j, k, c: (k, i + pick(c) * ni))
        return pl.BlockSpec((tm, tk), lambda i, j, k, *_: (i, jnp.clip(k - k0, 0, cnt - 1)))

    def b_spec(j0, cnt):
        if dims == "nt":
            return pl.BlockSpec((tn, tk), lambda i, j, k, *_: (j, k))
        return pl.BlockSpec((tk, tn), lambda i, j, k, *_: (k, jnp.clip(j - j0, 0, cnt - 1)))

    o_spec = pl.BlockSpec((tm, tn), lambda i, j, k, *_: (i, j))
    hbm = pl.BlockSpec(memory_space=pl.ANY)
    ins = a_list + b_list + ([add] if has_add else []) + [x for c in comms for x in c.ins]
    in_specs = ([a_spec(*r) for r in a_rng] + [b_spec(*r) for r in b_rng] + ([o_spec] if has_add else [])
                + [hbm] * sum(c_in))
    out_specs = [o_spec] + [hbm] * sum(c_out)
    out_shape = [SDS((m, n), out_dtype)] + [s for c in comms for s in c.out_shapes]
    scratch = [pltpu.VMEM((tm, tn), F32)] + [s for c in comms for s in c.sems]
    sem = ("arbitrary",) * 3 if comms else ("parallel", "parallel", "arbitrary")
    if a_half is None:
        grid = dict(grid=(ni, nj, nk), in_specs=in_specs, out_specs=out_specs, scratch_shapes=scratch)
    else:
        grid = dict(grid_spec=pltpu.PrefetchScalarGridSpec(
            num_scalar_prefetch=1, grid=(ni, nj, nk), in_specs=in_specs, out_specs=out_specs, scratch_shapes=scratch))
        ins = [a_half[0]] + ins
    res = pl.pallas_call(body, out_shape=out_shape, name=name, compiler_params=_params(sem, vmem), **grid)(*ins)
    if not comms:
        return res[0]
    outs, pos = [], 1
    for cnt in c_out:
        outs.append(list(res[pos:pos + cnt]))
        pos += cnt
    return res[0], outs


def _swa_specs(nb, blk=lambda n: n):
    cur = lambda off: pl.BlockSpec((pl.Element(WINDOW), pl.Element(1024)), lambda n: (blk(n) * WINDOW, off))
    prev = lambda off: pl.BlockSpec((pl.Element(WINDOW), pl.Element(1024)),
                                    lambda n: (jnp.maximum(blk(n) - 1, 0) * WINDOW, off))
    return [cur(QA), cur(VA), prev(QA), prev(VA),
            _const((1, 768)), _const((1, 256)), _const((1, LANE)), _const((768, 768)),
            _const((A_HEADS // A_GROUP, 2 * WINDOW, A_GROUP * WINDOW))]


def _swa_fwd(proj, qg, kg, sinks, bd, bias):
    s = proj.shape[0]
    nb = s // WINDOW

    def body(qk_ref, vz_ref, qkp_ref, vzp_ref, qg_ref, kg_ref, sk_ref, bd_ref, bias_ref, o_ref):
        first = pl.program_id(0) == 0
        o_ref[...] = _swa_fn(qk_ref[...], vz_ref[...], qkp_ref[...], vzp_ref[...], qg_ref[...], kg_ref[...],
                             sk_ref[...], bd_ref[...], bias_ref[...], first).astype(BF16)

    return pl.pallas_call(
        body, grid=(nb,), in_specs=_swa_specs(nb), out_specs=_rowblk(WINDOW, 768),
        out_shape=SDS((s, 768), BF16), name="swa_fwd",
        compiler_params=_params(("parallel",)))(proj, proj, proj, proj, qg, kg, sinks, bd, bias)


def _swa_bwd(proj, qg, kg, sinks, bd, bias, dga):
    s = proj.shape[0]
    nb = s // WINDOW
    blk = lambda n: nb - 1 - n

    def body(qk_ref, vz_ref, qkp_ref, vzp_ref, qg_ref, kg_ref, sk_ref, bd_ref, bias_ref, dg_ref,
             d_ref, dqg_ref, dkg_ref, dsk_ref, carry):
        first = blk(pl.program_id(0)) == 0
        bd_v = bd_ref[...]
        bias_v = bias_ref[...]
        fn = lambda qk, vz, qkp, vzp, qg_, kg_, sk: _swa_fn(qk, vz, qkp, vzp, qg_, kg_, sk, bd_v, bias_v, first)
        _, vjp = jax.vjp(fn, qk_ref[...], vz_ref[...], qkp_ref[...], vzp_ref[...], qg_ref[...], kg_ref[...], sk_ref[...])
        dqk, dvz, dqkp, dvzp, dqg, dkg, dsk = vjp(dg_ref[...])

        @pl.when(pl.program_id(0) == 0)
        def _():
            dqg_ref[...] = jnp.zeros_like(dqg_ref)
            dkg_ref[...] = jnp.zeros_like(dkg_ref)
            dsk_ref[...] = jnp.zeros_like(dsk_ref)
            carry[...] = jnp.zeros_like(carry)

        dqg_ref[...] += dqg
        dkg_ref[...] += dkg
        dsk_ref[...] += dsk
        kv = jnp.concatenate([dqk[:, 768:], dvz[:, :256]], axis=1) + carry[...]
        d_ref[...] = jnp.concatenate([dqk[:, :768], kv, dvz[:, 256:]], axis=1).astype(BF16)
        carry[...] = jnp.concatenate([dqkp[:, 768:], dvzp[:, :256]], axis=1)

    rev = lambda w: pl.BlockSpec((WINDOW, w), lambda n: (blk(n), 0))
    return pl.pallas_call(
        body, grid=(nb,), in_specs=_swa_specs(nb, blk) + [rev(768)],
        out_specs=[rev(2048), _const((1, 768)), _const((1, 256)), _const((1, LANE))],
        out_shape=[SDS((s, 2048), BF16), SDS((1, 768), F32), SDS((1, 256), F32), SDS((1, LANE), F32)],
        scratch_shapes=[pltpu.VMEM((WINDOW, 512), F32)],
        name="swa_bwd", compiler_params=_params(("arbitrary",)))(proj, proj, proj, proj, qg, kg, sinks, bd, bias, dga)


def _mem_fwd(proj, mkv, qg, kg, bd, *, tr):
    s = proj.shape[0]

    def body(qz_ref, mkv_ref, qg_ref, kg_ref, bd_ref, o_ref):
        o_ref[...] = _mem_fn(qz_ref[...], mkv_ref[...], qg_ref[...], kg_ref[...], bd_ref[...]).astype(BF16)

    return pl.pallas_call(
        body, grid=(s // tr,),
        in_specs=[_win(tr, 1024, QC), _const(mkv.shape), _const((1, 512)), _const((1, 512)), _const((512, 512))],
        out_specs=_rowblk(tr, 512), out_shape=SDS((s, 512), BF16), name="mem_fwd",
        compiler_params=_params(("parallel",)))(proj, mkv, qg, kg, bd)


def _mem_bwd(proj, mkv, qg, kg, bd, dgc, *, tr):
    s = proj.shape[0]

    def body(qz_ref, mkv_ref, qg_ref, kg_ref, bd_ref, dg_ref, dqz_ref, dmkv_ref, dqg_ref, dkg_ref):
        bd_v = bd_ref[...]
        fn = lambda qz, mkv_, qg_, kg_: _mem_fn(qz, mkv_, qg_, kg_, bd_v)
        _, vjp = jax.vjp(fn, qz_ref[...], mkv_ref[...], qg_ref[...], kg_ref[...])
        dqz, dmkv, dqg, dkg = vjp(dg_ref[...])

        @pl.when(pl.program_id(0) == 0)
        def _():
            dmkv_ref[...] = jnp.zeros_like(dmkv_ref)
            dqg_ref[...] = jnp.zeros_like(dqg_ref)
            dkg_ref[...] = jnp.zeros_like(dkg_ref)

        dmkv_ref[...] += dmkv
        dqg_ref[...] += dqg
        dkg_ref[...] += dkg
        dqz_ref[...] = dqz.astype(BF16)

    return pl.pallas_call(
        body, grid=(s // tr,),
        in_specs=[_win(tr, 1024, QC), _const(mkv.shape), _const((1, 512)), _const((1, 512)), _const((512, 512)),
                  _rowblk(tr, 512)],
        out_specs=[_rowblk(tr, 1024), _const(mkv.shape), _const((1, 512)), _const((1, 512))],
        out_shape=[SDS((s, 1024), BF16), SDS(mkv.shape, F32), SDS((1, 512), F32), SDS((1, 512), F32)],
        name="mem_bwd", compiler_params=_params(("arbitrary",)))(proj, mkv, qg, kg, bd, dgc)


def _log_sigmoid(x):
    return jnp.minimum(x, 0.0) - jnp.log1p(jnp.exp(-jnp.abs(x)))


FOX_TQ, FOX_TK = 512, 512
FOX_FWD_TQ, FOX_FWD_TK = 1024, 1024


def _fox_tiles(s):
    return min(FOX_TQ, s), min(FOX_TK, s)


AUG = 128 * B_HEADS
COL_A, COL_B = 64, 67


def _split3(c):
    hi = c.astype(BF16)
    r1 = c - hi.astype(F32)
    mid = r1.astype(BF16)
    lo = (r1 - mid.astype(F32)).astype(BF16)
    return hi, mid, lo


def _expand_mats():
    def mat(col0):
        e = np.zeros((768 + 3 * LANE, AUG), np.float32)
        for h in range(B_HEADS):
            for d in range(HEAD_DIM):
                e[64 * h + d, 128 * h + d] = 1.0
            for part in range(3):
                e[768 + LANE * part + h, 128 * h + col0 + part] = 1.0
        return e

    def ones(col0):
        o = np.zeros((1, AUG), np.float32)
        for h in range(B_HEADS):
            o[0, 128 * h + col0:128 * h + col0 + 3] = 1.0
        return o

    return (jnp.asarray(mat(COL_A), BF16), jnp.asarray(mat(COL_B), BF16), jnp.asarray(ones(COL_A)), jnp.asarray(ones(COL_B)))


def _augment(data_bf16, triple, emat, ones_row):
    parts = [data_bf16] + (list(triple) if triple is not None else [jnp.zeros((data_bf16.shape[0], LANE), BF16)] * 3)
    wide = _dot(jnp.concatenate(parts, axis=1), emat)
    if ones_row is not None:
        wide = wide + ones_row
    return wide


def _compact(wide):
    return jnp.concatenate([wide[:, 128 * h:128 * h + 64] for h in range(wide.shape[1] // 128)], axis=1)


def _lane_of_heads(wide, col, first=0):
    rows = wide.shape[0]
    lane = lax.broadcasted_iota(jnp.int32, (rows, LANE), 1)
    out = jnp.zeros((rows, LANE), F32)
    for h in range(wide.shape[1] // 128):
        out = jnp.where(lane == first + h, wide[:, 128 * h + col:128 * h + col + 1], out)
    return out


def _fox2_prep(proj, fbl, qg, kg, bfor, bd, ea, eb, ones_a, ones_b, *, tr):
    s = proj.shape[0]
    tri = jnp.asarray(np.tril(np.ones((tr, tr), np.float32)))

    def body(q_ref, k_ref, v_ref, fb_ref, qg_ref, kg_ref, bf_ref, bd_ref, tri_ref, ea_ref, eb_ref, oa_ref, ob_ref,
             qat_ref, ka_ref, kat_ref, va_ref, vat_ref, qn_ref, c_ref, carry):
        @pl.when(pl.program_id(0) == 0)
        def _():
            carry[...] = jnp.zeros_like(carry)

        bd_v = bd_ref[...]
        lane = lax.broadcasted_iota(jnp.int32, (tr, LANE), 1)
        logf = jnp.where(lane < N_FORGET, _log_sigmoid(fb_ref[...] + bf_ref[...]), 0.0)
        c = jnp.dot(tri_ref[...], logf, precision=HI, preferred_element_type=F32) + carry[...]
        c_ref[...] = c
        carry[...] = c[tr - 1:tr, :]
        qn = _qn_fn(q_ref[...], qg_ref[...], bd_v).astype(BF16)
        kn = _kn_fn(k_ref[...], kg_ref[...], bd_v).astype(BF16)
        qn_ref[...] = qn
        qat_ref[...] = jnp.transpose(_augment(qn, _split3(c), ea_ref[...], ob_ref[...])).astype(BF16)
        ka = _augment(kn, _split3(-c), eb_ref[...], oa_ref[...])
        ka_ref[...] = ka.astype(BF16)
        kat_ref[...] = jnp.transpose(ka).astype(BF16)
        va = _augment(v_ref[...].astype(BF16), None, ea_ref[...], oa_ref[...])
        va_ref[...] = va.astype(BF16)
        vat_ref[...] = jnp.transpose(va).astype(BF16)

    emat = _const((768 + 3 * LANE, AUG))
    return pl.pallas_call(
        body, grid=(s // tr,),
        in_specs=[_win(tr, 768, QB), _win(tr, 768, KB), _win(tr, 768, VB), _rowblk(tr, LANE), _const((1, 768)),
                  _const((1, 768)), _const((1, LANE)), _const((768, 768)), _const((tr, tr)), emat, emat,
                  _const((1, AUG)), _const((1, AUG))],
        out_specs=[pl.BlockSpec((AUG, tr), lambda i: (0, i)), _rowblk(tr, AUG), pl.BlockSpec((AUG, tr), lambda i: (0, i)),
                   _rowblk(tr, AUG), pl.BlockSpec((AUG, tr), lambda i: (0, i)), _rowblk(tr, 768), _rowblk(tr, LANE)],
        out_shape=[SDS((AUG, s), BF16), SDS((s, AUG), BF16), SDS((AUG, s), BF16), SDS((s, AUG), BF16),
                   SDS((AUG, s), BF16), SDS((s, 768), BF16), SDS((s, LANE), F32)],
        scratch_shapes=[pltpu.VMEM((1, LANE), F32)], name="fox_prep",
        compiler_params=_params(("arbitrary",)))(proj, proj, proj, fbl, qg, kg, bfor, bd, tri, ea, eb, ones_a, ones_b)


def _fox2_fwd(proj, qat, ka, vat):
    s = proj.shape[0]
    tq, tk = min(FOX_FWD_TQ, s), min(FOX_FWD_TK, s)
    nq, nk = s // tq, s // tk

    def last_k(i):
        return (i * tq + tq - 1) // tk

    def body(qt_ref, k_ref, vt_ref, z_ref, gb_ref, yb_ref, lse_ref, acc, m_s):
        i, j = pl.program_id(0), pl.program_id(1)

        @pl.when(j == 0)
        def _():
            acc[...] = jnp.zeros_like(acc)
            m_s[...] = jnp.full_like(m_s, NEG)

        def tile(masked):
            if masked:
                kpos = j * tk + lax.broadcasted_iota(jnp.int32, (tk, tq), 0)
                qpos = i * tq + lax.broadcasted_iota(jnp.int32, (tk, tq), 1)
                mask = kpos <= qpos
            for h in range(B_HEADS):
                sl = slice(128 * h, 128 * h + 128)
                sc = _dot(k_ref[:, sl], qt_ref[sl, :])
                if masked:
                    sc = jnp.where(mask, sc, NEG)
                m_prev = m_s[h:h + 1, :]
                m_new = jnp.maximum(m_prev, jnp.max(sc, axis=0, keepdims=True))
                p = jnp.exp(sc - m_new).astype(BF16)
                acc[sl, :] = jnp.exp(m_prev - m_new) * acc[sl, :] + _dot(vt_ref[sl, :], p)
                m_s[h:h + 1, :] = m_new

        full = j * tk + tk - 1 <= i * tq

        @pl.when(full)
        def _():
            tile(False)

        @pl.when(jnp.logical_and(jnp.logical_not(full), j <= last_k(i)))
        def _():
            tile(True)

        @pl.when(j == nk - 1)
        def _():
            outs = []
            row = lax.broadcasted_iota(jnp.int32, (LANE, tq), 0)
            lse_t = jnp.zeros((LANE, tq), F32)
            for h in range(B_HEADS):
                l_row = acc[128 * h + COL_A:128 * h + COL_A + 1, :]
                outs.append(acc[128 * h:128 * h + 64, :] * (1.0 / l_row))
                lse_t = jnp.where(row == h, m_s[h:h + 1, :] + jnp.log(l_row), lse_t)
            y = jnp.transpose(jnp.concatenate(outs, axis=0))
            yb_ref[...] = y
            gb_ref[...] = (y * _silu(z_ref[...])).astype(BF16)
            lse_ref[...] = jnp.transpose(lse_t)

    kcol = lambda i, j: (0, jnp.minimum(j, last_k(i)))
    return pl.pallas_call(
        body, grid=(nq, nk),
        in_specs=[pl.BlockSpec((AUG, tq), lambda i, j: (0, i)),
                  pl.BlockSpec((tk, AUG), lambda i, j: (jnp.minimum(j, last_k(i)), 0)),
                  pl.BlockSpec((AUG, tk), kcol),
                  pl.BlockSpec((pl.Element(tq), pl.Element(768)), lambda i, j: (i * tq, ZB))],
        out_specs=[pl.BlockSpec((tq, 768), lambda i, j: (i, 0)), pl.BlockSpec((tq, 768), lambda i, j: (i, 0)),
                   pl.BlockSpec((tq, LANE), lambda i, j: (i, 0))],
        out_shape=[SDS((s, 768), BF16), SDS((s, 768), F32), SDS((s, LANE), F32)],
        scratch_shapes=[pltpu.VMEM((AUG, tq), F32), pltpu.VMEM((16, tq), F32)],
        name="fox_fwd", compiler_params=_params(("parallel", "arbitrary")))(qat, ka, vat, proj)


def _fox2_bwd_pre(proj, yb, dgb, qn, c, lse, hsum, ea, ones_b, *, tr):
    s = proj.shape[0]

    def body(z_ref, y_ref, dg_ref, qn_ref, c_ref, lse_ref, hs_ref, ea_ref, ob_ref,
             qa_ref, qat_ref, dya_ref, dyat_ref, dz_ref):
        z, y, dg = z_ref[...], y_ref[...], dg_ref[...]
        sg = jax.nn.sigmoid(z)
        dy = dg * (z * sg)
        dz_ref[...] = (dg * y * (sg * (1.0 + z * (1.0 - sg)))).astype(BF16)
        delta = jnp.dot(dy * y, hs_ref[...], precision=HI, preferred_element_type=F32)
        e = ea_ref[...]
        dya = _augment(dy.astype(BF16), _split3(-delta), e, None)
        dya_ref[...] = dya.astype(BF16)
        dyat_ref[...] = jnp.transpose(dya).astype(BF16)
        qa = _augment(qn_ref[...], _split3(c_ref[...] - lse_ref[...]), e, ob_ref[...])
        qa_ref[...] = qa.astype(BF16)
        qat_ref[...] = jnp.transpose(qa).astype(BF16)

    return pl.pallas_call(
        body, grid=(s // tr,),
        in_specs=[_win(tr, 768, ZB), _rowblk(tr, 768), _rowblk(tr, 768), _rowblk(tr, 768), _rowblk(tr, LANE),
                  _rowblk(tr, LANE), _const((768, LANE)), _const((768 + 3 * LANE, AUG)), _const((1, AUG))],
        out_specs=[_rowblk(tr, AUG), pl.BlockSpec((AUG, tr), lambda i: (0, i)), _rowblk(tr, AUG),
                   pl.BlockSpec((AUG, tr), lambda i: (0, i)), _rowblk(tr, 768)],
        out_shape=[SDS((s, AUG), BF16), SDS((AUG, s), BF16), SDS((s, AUG), BF16), SDS((AUG, s), BF16),
                   SDS((s, 768), BF16)], name="fox_bwd_pre",
        compiler_params=_params(("parallel",)))(proj, yb, dgb, qn, c, lse, hsum, ea, ones_b)


def _fox2_bwd(qb, qbt, ka, kat, va, dya, dyat):
    s = qb.shape[0]
    tq, tk = _fox_tiles(s)
    nq, nk = s // tq, s // tk
    ng = 2
    gh = B_HEADS // ng
    gw = 128 * gh

    def first_q(j):
        return (j * tk) // tq

    def body(q_ref, qt_ref, k_ref, kt_ref, v_ref, dy_ref, dyt_ref, dq_hbm, dk_ref, dv_ref, dck_ref,
             dq_acc, dk_acc, dv_acc, sem):
        g, j, i = pl.program_id(0), pl.program_id(1), pl.program_id(2)

        @pl.when((j == 0) & (i == 0))
        def _():
            dq_acc[...] = jnp.zeros_like(dq_acc)

        @pl.when(i == 0)
        def _():
            dk_acc[...] = jnp.zeros_like(dk_acc)
            dv_acc[...] = jnp.zeros_like(dv_acc)

        def tile(masked):
            if masked:
                kpos = j * tk + lax.broadcasted_iota(jnp.int32, (tk, tq), 0)
                qpos = i * tq + lax.broadcasted_iota(jnp.int32, (tk, tq), 1)
                mask = kpos <= qpos
            cols = pl.ds(pl.multiple_of(i * tq, tq), tq)
            for h in range(gh):
                sl = slice(128 * h, 128 * h + 128)
                sc = _dot(k_ref[:, sl], qt_ref[sl, :])
                if masked:
                    sc = jnp.where(mask, sc, NEG)
                p = jnp.exp(sc)
                ds = (p * _dot(v_ref[:, sl], dyt_ref[sl, :])).astype(BF16)
                dv_acc[:, sl] += _dot(p.astype(BF16), dy_ref[:, sl])
                dk_acc[:, sl] += _dot(ds, q_ref[:, sl])
                dq_acc[sl, cols] += _dot(kt_ref[sl, :], ds)

        full = j * tk + tk - 1 <= i * tq

        @pl.when(full)
        def _():
            tile(False)

        @pl.when(jnp.logical_and(jnp.logical_not(full), i >= first_q(j)))
        def _():
            tile(True)

        @pl.when(i == nq - 1)
        def _():
            dkw = dk_acc[...]
            dk_ref[...] = _compact(dkw)
            dv_ref[...] = _compact(dv_acc[...]).astype(BF16)
            dck_ref[...] = -_lane_of_heads(dkw, COL_B, gh * g)

        @pl.when((j == nk - 1) & (i == nq - 1))
        def _():
            cp = pltpu.make_async_copy(dq_acc, dq_hbm.at[pl.ds(pl.multiple_of(g * gw, gw), gw)], sem)
            cp.start()
            cp.wait()

    qrow = pl.BlockSpec((tq, gw), lambda g, j, i: (jnp.maximum(i, first_q(j)), g))
    qcol = pl.BlockSpec((gw, tq), lambda g, j, i: (g, jnp.maximum(i, first_q(j))))
    krow = pl.BlockSpec((tk, gw), lambda g, j, i: (j, g))
    kcol = pl.BlockSpec((gw, tk), lambda g, j, i: (g, j))
    kout = pl.BlockSpec((tk, gw // 2), lambda g, j, i: (j, g))
    return pl.pallas_call(
        body, grid=(ng, nk, nq),
        in_specs=[qrow, qcol, krow, kcol, krow, qrow, qcol],
        out_specs=[pl.BlockSpec(memory_space=pl.ANY), kout, kout,
                   pl.BlockSpec((None, tk, LANE), lambda g, j, i: (g, j, 0))],
        out_shape=[SDS((AUG, s), F32), SDS((s, 768), F32), SDS((s, 768), BF16), SDS((ng, s, LANE), F32)],
        scratch_shapes=[pltpu.VMEM((gw, s), F32), pltpu.VMEM((tk, gw), F32), pltpu.VMEM((tk, gw), F32),
                        pltpu.SemaphoreType.DMA],
        name="fox_bwd", compiler_params=_params(("arbitrary",) * 3))(qb, qbt, ka, kat, va, dya, dyat)


def _fox2_bwd_post(proj, fbl, qg, kg, bfor, bd, dqa, dkn, dck, *, tr):
    s = proj.shape[0]
    nb = s // tr
    triu = jnp.asarray(np.triu(np.ones((tr, tr), np.float32)))
    rev = lambda i: nb - 1 - i

    def body(q_ref, k_ref, fb_ref, qg_ref, kg_ref, bf_ref, bd_ref, tri_ref, dqa_ref, dkn_ref, dck_ref,
             dq_ref, dk_ref, dfb_ref, dqg_ref, dkg_ref, dbf_ref, carry):
        @pl.when(pl.program_id(0) == 0)
        def _():
            carry[...] = jnp.zeros_like(carry)
            dqg_ref[...] = jnp.zeros_like(dqg_ref)
            dkg_ref[...] = jnp.zeros_like(dkg_ref)
            dbf_ref[...] = jnp.zeros_like(dbf_ref)

        bd_v = bd_ref[...]
        dqw = jnp.transpose(dqa_ref[...])
        _, vjp_q = jax.vjp(lambda q, g: _qn_fn(q, g, bd_v), q_ref[...], qg_ref[...])
        dq, dqg = vjp_q(_compact(dqw))
        _, vjp_k = jax.vjp(lambda k, g: _kn_fn(k, g, bd_v), k_ref[...], kg_ref[...])
        dk, dkg = vjp_k(dkn_ref[...])
        dq_ref[...] = dq.astype(BF16)
        dk_ref[...] = dk.astype(BF16)
        dqg_ref[...] += dqg
        dkg_ref[...] += dkg

        dc = _lane_of_heads(dqw, COL_A) + (dck_ref[0] + dck_ref[1])
        dlogf = jnp.dot(tri_ref[...], dc, precision=HI, preferred_element_type=F32) + carry[...]
        carry[...] = dlogf[0:1, :]
        lane = lax.broadcasted_iota(jnp.int32, (tr, LANE), 1)
        xf = fb_ref[...] + bf_ref[...]
        dfb = jnp.where(lane < N_FORGET, dlogf * jax.nn.sigmoid(-xf), 0.0)
        dfb_ref[...] = dfb.astype(BF16)
        dbf_ref[...] += jnp.sum(dfb, axis=0, keepdims=True)

    rb = lambda w: pl.BlockSpec((tr, w), lambda i: (rev(i), 0))
    wn = lambda w, off: pl.BlockSpec((pl.Element(tr), pl.Element(w)), lambda i: (rev(i) * tr, off))
    return pl.pallas_call(
        body, grid=(nb,),
        in_specs=[wn(768, QB), wn(768, KB), rb(LANE), _const((1, 768)), _const((1, 768)), _const((1, LANE)),
                  _const((768, 768)), _const((tr, tr)), pl.BlockSpec((AUG, tr), lambda i: (0, rev(i))), rb(768),
                  pl.BlockSpec((2, tr, LANE), lambda i: (0, rev(i), 0))],
        out_specs=[rb(768), rb(768), rb(LANE), _const((1, 768)), _const((1, 768)), _const((1, LANE))],
        out_shape=[SDS((s, 768), BF16), SDS((s, 768), BF16), SDS((s, LANE), BF16), SDS((1, 768), F32),
                   SDS((1, 768), F32), SDS((1, LANE), F32)],
        scratch_shapes=[pltpu.VMEM((1, LANE), F32)], name="fox_bwd_post",
        compiler_params=_params(("arbitrary",)))(proj, proj, fbl, qg, kg, bfor, bd, triu, dqa, dkn, dck)


def _merge_specs(tr):
    row = lambda w: pl.BlockSpec((tr, w), lambda i, j: (i, 0))
    shard = lambda r: pl.BlockSpec((None, r, 512), lambda i, j: (j, 0, 0))
    gate = lambda b: pl.BlockSpec((tr, 512), lambda i, j: (i, (GATE + 2048 * b) // 512 + j))
    return [row(768), row(768), row(512), shard(768), shard(768), shard(512), gate(0), gate(1), gate(2)]


def _merge_fwd(proj, ga, gb, gc, wa, wb, wc, *, tr):
    s = proj.shape[0]

    def body(ga_ref, gb_ref, gc_ref, wa_ref, wb_ref, wc_ref, l0_ref, l1_ref, l2_ref, y_ref):
        ua = _dot(ga_ref[...], wa_ref[...])
        ub = _dot(gb_ref[...], wb_ref[...])
        uc = _dot(gc_ref[...], wc_ref[...])
        y = jax.nn.sigmoid(l0_ref[...]) * ua + jax.nn.sigmoid(l1_ref[...]) * ub + jax.nn.sigmoid(l2_ref[...]) * uc
        y_ref[...] = y.astype(BF16)

    return pl.pallas_call(
        body, grid=(s // tr, N_CHIPS), in_specs=_merge_specs(tr),
        out_specs=pl.BlockSpec((tr, 512), lambda i, j: (i, j)), out_shape=SDS((s, D_MODEL), BF16), name="merge_fwd",
        compiler_params=_params(("parallel", "arbitrary")))(ga, gb, gc, wa, wb, wc, proj, proj, proj)


def _merge_bwd(proj, ga, gb, gc, wa, wb, wc, dy, *, tr):
    s = proj.shape[0]

    def body(ga_ref, gb_ref, gc_ref, wa_ref, wb_ref, wc_ref, l0_ref, l1_ref, l2_ref, dy_ref,
             dl0_ref, dl1_ref, dl2_ref, dua_ref, dub_ref, duc_ref, dga_ref, dgb_ref, dgc_ref):
        j = pl.program_id(1)
        dyv = dy_ref[...]

        @pl.when(j == 0)
        def _():
            dga_ref[...] = jnp.zeros_like(dga_ref)
            dgb_ref[...] = jnp.zeros_like(dgb_ref)
            dgc_ref[...] = jnp.zeros_like(dgc_ref)

        for g_ref, w_ref, l_ref, dl_ref, du_ref, dg_ref in (
                (ga_ref, wa_ref, l0_ref, dl0_ref, dua_ref, dga_ref),
                (gb_ref, wb_ref, l1_ref, dl1_ref, dub_ref, dgb_ref),
                (gc_ref, wc_ref, l2_ref, dl2_ref, duc_ref, dgc_ref)):
            w = w_ref[...]
            u = _dot(g_ref[...], w)
            sg = jax.nn.sigmoid(l_ref[...])
            dl_ref[...] = (dyv * u * sg * (1.0 - sg)).astype(BF16)
            du = (dyv * sg).astype(BF16)
            du_ref[...] = du
            dg_ref[...] += _dot_nt(du, w)

    blk = pl.BlockSpec((tr, 512), lambda i, j: (i, j))
    row = lambda w: pl.BlockSpec((tr, w), lambda i, j: (i, 0))
    big = SDS((s, D_MODEL), BF16)
    return pl.pallas_call(
        body, grid=(s // tr, N_CHIPS), in_specs=_merge_specs(tr) + [blk],
        out_specs=[blk] * 6 + [row(768), row(768), row(512)],
        out_shape=[big] * 6 + [SDS((s, 768), F32), SDS((s, 768), F32), SDS((s, 512), F32)], name="merge_bwd",
        compiler_params=_params(("parallel", "arbitrary")))(ga, gb, gc, wa, wb, wc, proj, proj, proj, dy)


def _out_loss(y, wo, x, tgt, *, tr, tn):
    s = x.shape[0]

    def body(y_ref, w_ref, x_ref, t_ref, d_ref, db_ref, sq_ref, dy_ref):
        @pl.when((pl.program_id(0) == 0) & (pl.program_id(1) == 0))
        def _():
            sq_ref[...] = jnp.zeros_like(sq_ref)

        @pl.when(pl.program_id(1) == 0)
        def _():
            dy_ref[...] = jnp.zeros_like(dy_ref)

        w = w_ref[...]
        out = x_ref[...] + _dot(y_ref[...], w)
        diff = out - t_ref[...]
        sq_ref[...] += jnp.sum(diff * diff, axis=0, keepdims=True)
        d = diff * (1.0 / D_MODEL)
        d_ref[...] = d
        db = d.astype(BF16)
        db_ref[...] = db
        dy_ref[...] += _dot_nt(db, w)

    blk = pl.BlockSpec((tr, tn), lambda i, j: (i, j))
    row = pl.BlockSpec((tr, D_MODEL), lambda i, j: (i, 0))
    return pl.pallas_call(
        body, grid=(s // tr, D_MODEL // tn),
        in_specs=[row, pl.BlockSpec((D_MODEL, tn), lambda i, j: (0, j)), blk, blk],
        out_specs=[blk, blk, _const((1, tn)), row],
        out_shape=[SDS((s, D_MODEL), F32), SDS((s, D_MODEL), BF16), SDS((1, tn), F32), SDS((s, D_MODEL), F32)],
        name="out_loss", compiler_params=_params(("arbitrary", "arbitrary")))(y, wo, x, tgt)


def _tile_gain(g, reps):
    return jnp.tile(g.reshape(1, -1), (1, reps))


def _pad_lane(v):
    v = v.reshape(1, -1)
    return jnp.pad(v, ((0, 0), (0, LANE - v.shape[1])))


def _local_step(x, mem, tgt, w_main, w_fb, w_small, norm_gain, mem_norm_gain, b_forget,
                q_gain_a, k_gain_a, sinks_a, q_gain_b, k_gain_b, q_gain_c, k_gain_c, core=None):
    s = x.shape[0]
    tr = min(512, s)
    bd64 = _block_diag(768, HEAD_DIM)
    bd128 = _block_diag(512, C_HEAD_DIM)
    hsum = _head_sum(768, HEAD_DIM)
    qga, kga = _tile_gain(q_gain_a, 12), _tile_gain(k_gain_a, 4)
    qgb, kgb = _tile_gain(q_gain_b, 12), _tile_gain(k_gain_b, 12)
    qgc, kgc = _tile_gain(q_gain_c, 4), _tile_gain(k_gain_c, 4)
    sinks = _pad_lane(sinks_a)
    bfor = _pad_lane(b_forget)

    hn = _rms_fwd(x, norm_gain, tr=tr, name="rms_x")
    on_mesh = core is not None
    if on_mesh:
        proj, (gathered,) = _matmul(hn, w_main, dims="nn", out_dtype=F32, tm=1024, tn=1024, tk=D_MODEL, name="proj_main",
                                    comms=[_gather_comm(list(w_small))])
        w_mk, wa, wb, wc, wo = gathered
        w_mk, wo = w_mk.reshape(D_MODEL, 1024), wo.reshape(D_MODEL, D_MODEL)
    else:
        proj = _matmul(hn, w_main, dims="nn", out_dtype=F32, tm=1024, tn=1024, tk=D_MODEL, name="proj_main")
        w_mk, wa, wb, wc, wo = w_small
    fbl = _matmul(hn, w_fb, dims="nn", out_dtype=F32, tm=1024, tn=LANE, tk=D_MODEL, name="proj_forget")
    memn = _rms_fwd(mem, mem_norm_gain, tr=mem.shape[0], name="rms_mem")
    mkv = _matmul(memn, w_mk, dims="nn", out_dtype=F32, tm=256, tn=512, tk=D_MODEL, name="mem_kv")

    swa_bias = _swa_bias()
    ga = _swa_fwd(proj, qga, kga, sinks, bd64, swa_bias)
    ea, eb, ones_a, ones_b = _expand_mats()
    tf = min(256, s)
    qat, ka, kat, va, vat, qn, cfox = _fox2_prep(proj, fbl, qgb, kgb, bfor, bd64, ea, eb, ones_a, ones_b, tr=tf)
    gb, yb, lse = _fox2_fwd(proj, qat, ka, vat)
    gc = _mem_fwd(proj, mkv, qgc, kgc, bd128, tr=tr)
    y = _merge_fwd(proj, ga, gb, gc, wa, wb, wc, tr=tr)
    dout, dout_b, sq, dy = _out_loss(y, wo, x, tgt, tr=tr, tn=1024)

    d_wo = _matmul(y, dout_b, dims="tn", out_dtype=F32, tm=1024, tn=512, tk=4096, name="dw_out")
    dl0, dl1, dl2, dua, dub, duc, dga, dgb, dgc = _merge_bwd(proj, ga, gb, gc, wa, wb, wc, dy, tr=tr)
    d_wa = _matmul(ga, dua, dims="tn", out_dtype=F32, tm=768, tn=512, tk=4096, name="dw_branch_a")
    d_wb = _matmul(gb, dub, dims="tn", out_dtype=F32, tm=768, tn=512, tk=4096, name="dw_branch_b")
    d_wc = _matmul(gc, duc, dims="tn", out_dtype=F32, tm=512, tn=512, tk=4096, name="dw_branch_c")

    dproj_a, d_qga, d_kga, d_sinks = _swa_bwd(proj, qga, kga, sinks, bd64, swa_bias, dga)

    qab, qabt, dya, dyat, dzb = _fox2_bwd_pre(proj, yb, dgb, qn, cfox, lse, hsum, ea, ones_b, tr=tf)
    dqa, dkn, dvb, dck = _fox2_bwd(qab, qabt, ka, kat, va, dya, dyat)
    dqb, dkb, dfb, d_qgb, d_kgb, d_bf = _fox2_bwd_post(proj, fbl, qgb, kgb, bfor, bd64, dqa, dkn, dck, tr=tf)

    dproj_c, dmkv, d_qgc, d_kgc = _mem_bwd(proj, mkv, qgc, kgc, bd128, dgc, tr=tr)
    dmkv_b = dmkv.astype(BF16)
    d_wmk = _matmul(memn, dmkv_b, dims="tn", out_dtype=F32, tm=1024, tn=512, tk=256, name="dw_mem_kv")
    dmemn = _matmul(dmkv_b, w_mk, dims="nt", out_dtype=F32, tm=256, tn=512, tk=1024, name="dmemn")
    (d_mem_gain,) = _rms_bwd(mem, mem_norm_gain, dmemn, None, tr=mem.shape[0], name="rms_mem_bwd")

    dproj = [dproj_a, jnp.concatenate([dqb, dkb, dvb, dzb, dproj_c], axis=1), dl0, dl1, dl2]
    dhn_f = _matmul(dfb, w_fb, dims="nt", out_dtype=F32, tm=1024, tn=512, tk=LANE, name="dhn_forget")
    d_wfb = _matmul(hn, dfb, dims="tn", out_dtype=F32, tm=1024, tn=LANE, tk=512, name="dw_forget")
    big = {}
    if on_mesh:
        g1, k1 = [d_wmk, d_wa, d_wb, d_wc, d_wo], [2, 3, 4, 5, 6]
        d_other, (got1,) = _matmul(hn, dproj, dims="tn", out_dtype=BF16, tm=1024, tn=512, tk=4096, a_half=(core, True),
                                   name="dw_main_other", comms=[_exchange_comm(g1, k1)])
        h1 = _add_half(g1, got1, core, [HALF_AXIS[k] for k in k1], name="add_half_small")
        d_own, (got0, parts1) = _matmul(
            hn, dproj, dims="tn", out_dtype=F32, tm=1024, tn=512, tk=4096, a_half=(core, False), name="dw_main_own",
            comms=[_exchange_comm([d_other, d_wfb], [0, 1], whole=(0,)), _scatter_comm(h1, k1)])
        h0 = [_add_pair(d_own, got0[0], name="add_pair_main")] + _add_half([d_wfb], [got0[1]], core, [0], name="add_half_forget")
        sums1 = _sum4(parts1, name="sum4_small")
        dhn, (parts0, theirs1) = _matmul(dproj, w_main, dims="nt", out_dtype=F32, tm=1024, tn=512, tk=2048, vmem=VMEM_WIDE, name="dhn",
                                         add=dhn_f, comms=[_scatter_comm(h0, [0, 1]), _swap_comm(sums1)])
        sums0 = _sum4(parts0, name="sum4_main")
        (grad_x, d_gain), theirs0 = _rms_bwd(x, norm_gain, dhn, dout, tr=tr, name="rms_x_bwd", comm=_swap_comm(sums0))
        big = dict(sums=sums0 + sums1, theirs=list(theirs0) + list(theirs1))
    else:
        dhn = _matmul(dproj, w_main, dims="nt", out_dtype=F32, tm=1024, tn=512, tk=2048, vmem=VMEM_WIDE, name="dhn", add=dhn_f)
        d_wmain = _matmul(hn, dproj, dims="tn", out_dtype=F32, tm=1024, tn=512, tk=4096, name="dw_main")
        big = dict(d_wmain=d_wmain, d_wfb=d_wfb, d_wmk=d_wmk, d_wa=d_wa, d_wb=d_wb, d_wc=d_wc, d_wo=d_wo)
        grad_x, d_gain = _rms_bwd(x, norm_gain, dhn, dout, tr=tr, name="rms_x_bwd")

    fold = lambda g, reps: jnp.sum(g.reshape(reps, -1), axis=0, keepdims=True)
    return dict(
        sq=sq, grad_x=grad_x, **big,
        d_gain=d_gain, d_mem_gain=d_mem_gain, d_bf=d_bf[:, :N_FORGET],
        d_qga=fold(d_qga, 12), d_kga=fold(d_kga, 4), d_sinks=d_sinks[:, :A_HEADS],
        d_qgb=fold(d_qgb, 12), d_kgb=fold(d_kgb, 12), d_qgc=fold(d_qgc, 4), d_kgc=fold(d_kgc, 4))


PACK_ROWS = 256
FORGET_IN_SHARD = FORGET_COL - SHARD_COLS
AFTER_FORGET = FORGET_COL - SLAB_START[1]
END_CHIP1 = 2 * SHARD_COLS - N_FORGET - SLAB_START[1]


def _pack_w_in(chip, w):
    rows = w.shape[1]
    tr = PACK_ROWS

    def body(k_ref, w_ref, o_ref, scr):
        scr[...] = jnp.zeros_like(scr)
        scr[pl.ds(0, SHARD_COLS), :] = w_ref[...]
        v = jnp.transpose(scr[...])
        k = k_ref[0]
        col = lax.broadcasted_iota(jnp.int32, (tr, SLAB), 1)
        no_forget = jnp.zeros((tr, LANE), BF16)

        @pl.when(k == 0)
        def _():
            o_ref[:, 0:SLAB] = v.astype(BF16)
            o_ref[:, SLAB:] = no_forget

        @pl.when(k == 1)
        def _():
            before = pltpu.roll(v, SLAB_SHIFT[1], axis=1)
            after = pltpu.roll(v, SLAB - (N_FORGET - SLAB_SHIFT[1]), axis=1)
            slab = jnp.where(col < AFTER_FORGET, before, jnp.where(col < END_CHIP1, after, 0.0))
            o_ref[:, 0:SLAB] = slab.astype(BF16)
            f = pltpu.roll(v, SLAB - FORGET_IN_SHARD, axis=1)[:, :LANE]
            o_ref[:, SLAB:] = jnp.where(col[:, :LANE] < N_FORGET, f, 0.0).astype(BF16)

        for kk in (2, 3):
            @pl.when(k == kk)
            def _(kk=kk):
                o_ref[:, 0:SLAB] = pltpu.roll(v, SLAB_SHIFT[kk], axis=1).astype(BF16)
                o_ref[:, SLAB:] = no_forget

    return pl.pallas_call(
        body, grid_spec=pltpu.PrefetchScalarGridSpec(
            num_scalar_prefetch=1, grid=(rows // tr,),
            in_specs=[pl.BlockSpec((SHARD_COLS, tr), lambda i, k: (0, i))],
            out_specs=pl.BlockSpec((None, tr, SLAB + LANE), lambda i, k: (k[0], i, 0)),
            scratch_shapes=[pltpu.VMEM((SLAB, tr), F32)]),
        out_shape=SDS((N_CHIPS, rows, SLAB + LANE), BF16), name="pack_w_in",
        compiler_params=_params(("arbitrary",)))(chip, w)


def _merge_slabs(g):
    rows = g.shape[1]
    tr = PACK_ROWS
    t = [s // LANE for s in SLAB_START]
    n_t = SLAB // LANE

    def body(g_ref, m_ref, f_ref):
        for k in range(N_CHIPS):
            lo = t[k] + (1 if k > 0 else 0)
            hi = t[k + 1] if k + 1 < N_CHIPS else t[k] + n_t
            m_ref[:, lo * LANE:hi * LANE] = g_ref[k, :, (lo - t[k]) * LANE:(hi - t[k]) * LANE]
            if k + 1 < N_CHIPS:
                a = g_ref[k, :, (hi - t[k]) * LANE:(hi - t[k] + 1) * LANE].astype(F32)
                b = g_ref[k + 1, :, 0:LANE].astype(F32)
                m_ref[:, hi * LANE:(hi + 1) * LANE] = (a + b).astype(BF16)
        f_ref[...] = g_ref[1, :, SLAB:]

    return pl.pallas_call(
        body, grid=(rows // tr,),
        in_specs=[pl.BlockSpec((N_CHIPS, tr, SLAB + LANE), lambda i: (0, i, 0))],
        out_specs=[_rowblk(tr, P_MAIN), _rowblk(tr, LANE)],
        out_shape=[SDS((rows, P_MAIN), BF16), SDS((rows, LANE), BF16)], name="merge_slabs",
        compiler_params=_params(("parallel",)))(g)


def _adamw_math(w, g, m, v):
    nm = ADAM_B1 * m + (1.0 - ADAM_B1) * g
    nv = ADAM_B2 * v + (1.0 - ADAM_B2) * (g * g)
    m_hat = nm / (1.0 - ADAM_B1 ** ADAM_STEP)
    v_hat = nv / (1.0 - ADAM_B2 ** ADAM_STEP)
    delta = -ADAM_LR * (m_hat / (jnp.sqrt(v_hat) + ADAM_EPS) + ADAM_WD * w)
    return delta, nm, nv


def _adamw(g, w, m, v, *, tr, name):
    rows, cols = w.shape
    tr = min(tr, rows)

    def body(g_ref, w_ref, m_ref, v_ref, d_ref, nm_ref, nv_ref):
        d, nm, nv = _adamw_math(w_ref[...], g_ref[...], m_ref[...], v_ref[...])
        d_ref[...] = d
        nm_ref[...] = nm
        nv_ref[...] = nv

    spec = _rowblk(tr, cols)
    return pl.pallas_call(
        body, grid=(rows // tr,), in_specs=[spec] * 4, out_specs=[spec] * 3,
        out_shape=[SDS((rows, cols), F32)] * 3, name=name, compiler_params=_params(("parallel",)))(g, w, m, v)


def _adamw_w_in(chip_core, slab_mine, slab_theirs, forget_mine, forget_theirs, w, m, v):
    rows = w.shape[1]
    tr = PACK_ROWS // 2
    nbh = rows // 2 // tr

    def body(k_ref, sa_ref, sb_ref, fa_ref, fb_ref, w_ref, m_ref, v_ref, g_ref, d_ref, nm_ref, nv_ref):
        use_mine = pl.program_id(0) // nbh == k_ref[1]
        sl = jnp.where(use_mine, sa_ref[...], sb_ref[...])
        f_tile = jnp.where(use_mine, fa_ref[...], fb_ref[...])
        k = k_ref[0]

        def emit(wide):
            g = jnp.transpose(wide)[:SHARD_COLS, :]
            g_ref[...] = g
            d, nm, nv = _adamw_math(w_ref[...], g, m_ref[...], v_ref[...])
            d_ref[...] = d
            nm_ref[...] = nm
            nv_ref[...] = nv

        @pl.when(k == 0)
        def _():
            emit(sl)

        @pl.when(k == 1)
        def _():
            col = lax.broadcasted_iota(jnp.int32, (tr, SLAB), 1)
            before = pltpu.roll(sl, SLAB - SLAB_SHIFT[1], axis=1)
            after = pltpu.roll(sl, N_FORGET - SLAB_SHIFT[1], axis=1)
            wide_f = jnp.concatenate([f_tile, jnp.zeros((tr, SLAB - LANE), F32)], axis=1)
            forget = pltpu.roll(wide_f, FORGET_IN_SHARD, axis=1)
            emit(jnp.where(col < FORGET_IN_SHARD, before, jnp.where(col < FORGET_IN_SHARD + N_FORGET, forget, after)))

        for kk in (2, 3):
            @pl.when(k == kk)
            def _(kk=kk):
                emit(pltpu.roll(sl, SLAB - SLAB_SHIFT[kk], axis=1))

    nat = pl.BlockSpec((SHARD_COLS, tr), lambda i, k: (0, i))
    half = lambda width: pl.BlockSpec((tr, width), lambda i, k: (i % nbh, 0))
    return pl.pallas_call(
        body, grid_spec=pltpu.PrefetchScalarGridSpec(
            num_scalar_prefetch=1, grid=(rows // tr,),
            in_specs=[half(SLAB), half(SLAB), half(LANE), half(LANE), nat, nat, nat],
            out_specs=[nat] * 4),
        out_shape=[SDS((SHARD_COLS, rows), F32)] * 4, name="adamw_w_in",
        compiler_params=_params(("arbitrary",)))(chip_core, slab_mine, slab_theirs, forget_mine, forget_theirs, w, m, v)


ANY = pl.BlockSpec(memory_space=pl.ANY)
HALF_AXIS = (0, 0, 1, 0, 0, 0, 1)


def _me():
    return lax.axis_index("x"), lax.axis_index("y"), lax.axis_index("c")


def _half(ref, which, axis):
    n = ref.shape[axis] // 2
    sl = pl.ds(which * n, n)
    return ref.at[sl] if axis == 0 else ref.at[:, sl]


def _piece(t, ref, j):
    if t == 0:
        return ref.at[:, pl.ds(SLAB_START[j], SLAB)]
    if t == 1:
        return ref
    if t in (2, 6):
        return ref.at[pl.ds(512 * j, 512)]
    return ref.at[:, pl.ds(512 * j, 512)]


def _piece_shape(t, shape):
    if t == 0:
        return (shape[0], SLAB)
    if t == 1:
        return shape
    if t in (2, 6):
        return (512, shape[1])
    return (shape[0], 512)


def _gather_plan(ins, outs, own_slot_in_src):
    x, y, c = _me()
    k = 2 * x + y
    sib = (x, y, 1 - c)
    chips = [(1 - x, y), (x, 1 - y), (1 - x, 1 - y)]
    n = len(outs)

    def rows(t, which):
        h = outs[t].shape[1] // 2
        return pl.ds(which * h, h)

    def mine(t):
        return ins[t].at[k, rows(t, c)] if own_slot_in_src else ins[t].at[rows(t, c)]

    def first(t, j, sems):
        chip = chips[j]
        return pltpu.make_async_remote_copy(
            src_ref=mine(t), dst_ref=outs[t].at[k, rows(t, c)], send_sem=sems[0].at[t, j], recv_sem=sems[1].at[t, j],
            device_id=(chip[0], chip[1], c), device_id_type=MESH)

    def landed(t, j, sems):
        chip = chips[j]
        return pltpu.make_async_remote_copy(
            src_ref=mine(t), dst_ref=outs[t].at[2 * chip[0] + chip[1], rows(t, c)], send_sem=sems[0].at[t, j],
            recv_sem=sems[1].at[t, j], device_id=(chip[0], chip[1], c), device_id_type=MESH)

    def passed(t, j, which, sems):
        chip = chips[j]
        blk = outs[t].at[2 * chip[0] + chip[1], rows(t, which)]
        return pltpu.make_async_remote_copy(
            src_ref=blk, dst_ref=blk, send_sem=sems[2].at[t, j], recv_sem=sems[3].at[t, j], device_id=sib,
            device_id_type=MESH)

    def start(sems):
        for j in range(3):
            for t in range(n):
                first(t, j, sems).start()

    def finish(sems):
        for j in range(3):
            for t in range(n):
                landed(t, j, sems).wait_recv()
                passed(t, j, c, sems).start()
        for j in range(3):
            for t in range(n):
                passed(t, j, 1 - c, sems).wait_recv()
        for j in range(3):
            for t in range(n):
                first(t, j, sems).wait_send()
                passed(t, j, c, sems).wait_send()

    return k, start, finish


def _all_gather_slabs(slabs):
    def body(in_ref, out_ref, nbr_sem, quarter_sem, pass_sem):
        x, y, c = _me()
        k = 2 * x + y
        rows = out_ref.shape[1]
        h, q = rows // 2, rows // 4
        nbrs = [(1 - x, y), (x, 1 - y)]
        slot = lambda chip: 2 * chip[0] + chip[1]
        diag = 2 * (1 - x) + (1 - y)
        half = pl.ds(c * h, h)
        quarter = lambda a: pl.ds(c * h + a * q, q)

        def first(a):
            return pltpu.make_async_remote_copy(
                src_ref=in_ref.at[k, half], dst_ref=out_ref.at[k, half], send_sem=nbr_sem.at[0, a],
                recv_sem=nbr_sem.at[1, a], device_id=(nbrs[a][0], nbrs[a][1], c), device_id_type=MESH)

        def landed(a):
            blk = out_ref.at[slot(nbrs[a]), half]
            return pltpu.make_async_remote_copy(
                src_ref=blk, dst_ref=blk, send_sem=nbr_sem.at[0, a], recv_sem=nbr_sem.at[1, a],
                device_id=(nbrs[a][0], nbrs[a][1], c), device_id_type=MESH)

        def relay(a):
            blk = out_ref.at[slot(nbrs[a]), quarter(a)]
            to = nbrs[1 - a]
            return pltpu.make_async_remote_copy(
                src_ref=blk, dst_ref=blk, send_sem=quarter_sem.at[0, a], recv_sem=quarter_sem.at[1, a],
                device_id=(to[0], to[1], c), device_id_type=MESH)

        def relayed(a):
            blk = out_ref.at[diag, quarter(a)]
            frm = nbrs[1 - a]
            return pltpu.make_async_remote_copy(
                src_ref=blk, dst_ref=blk, send_sem=quarter_sem.at[0, a], recv_sem=quarter_sem.at[1, a],
                device_id=(frm[0], frm[1], c), device_id_type=MESH)

        def passed(j, which):
            sl = diag if j == 2 else slot(nbrs[j])
            blk = out_ref.at[sl, pl.ds(which * h, h)]
            return pltpu.make_async_remote_copy(
                src_ref=blk, dst_ref=blk, send_sem=pass_sem.at[0, j], recv_sem=pass_sem.at[1, j],
                device_id=(x, y, 1 - c), device_id_type=MESH)

        for a in range(2):
            first(a).start()
        for a in range(2):
            landed(a).wait_recv()
            relay(a).start()
            passed(a, c).start()
        for a in range(2):
            relayed(a).wait_recv()
        passed(2, c).start()
        for j in range(3):
            passed(j, 1 - c).wait_recv()
        for a in range(2):
            first(a).wait_send()
            relay(a).wait_send()
        for j in range(3):
            passed(j, c).wait_send()

    return pl.pallas_call(
        body, in_specs=[ANY], out_specs=ANY, out_shape=SDS(slabs.shape, slabs.dtype),
        scratch_shapes=[pltpu.SemaphoreType.DMA((2, 2)), pltpu.SemaphoreType.DMA((2, 2)), pltpu.SemaphoreType.DMA((2, 3))],
        input_output_aliases={0: 0}, name="all_gather_slabs")(slabs)


def _gather_comm(parts):
    n = len(parts)

    def start(ins, outs, sems):
        k, go, _ = _gather_plan(ins, outs, False)
        for t in range(n):
            pltpu.make_async_copy(ins[t], outs[t].at[k], sems[4].at[t]).start()
        go(sems)

    def finish(ins, outs, sems):
        k, _, done = _gather_plan(ins, outs, False)
        done(sems)
        for t in range(n):
            pltpu.make_async_copy(ins[t], outs[t].at[k], sems[4].at[t]).wait()

    return _Comm(parts, [SDS((N_CHIPS,) + p.shape, p.dtype) for p in parts],
                 [pltpu.SemaphoreType.DMA((n, 3))] * 4 + [pltpu.SemaphoreType.DMA((n,))], start, finish)


def _exchange_comm(arrs, kinds, whole=()):
    n = len(arrs)

    def copies(ins, outs, sems):
        x, y, c = _me()
        return [pltpu.make_async_remote_copy(
            src_ref=ins[t] if t in whole else _half(ins[t], 1 - c, HALF_AXIS[kinds[t]]), dst_ref=outs[t],
            send_sem=sems[0].at[t], recv_sem=sems[1].at[t], device_id=(x, y, 1 - c), device_id_type=MESH)
            for t in range(n)]

    def start(ins, outs, sems):
        for cp in copies(ins, outs, sems):
            cp.start()

    def finish(ins, outs, sems):
        for cp in copies(ins, outs, sems):
            cp.wait()

    def hshape(t):
        s = list(arrs[t].shape)
        if t not in whole:
            s[HALF_AXIS[kinds[t]]] //= 2
        return SDS(tuple(s), arrs[t].dtype)

    return _Comm(arrs, [hshape(t) for t in range(n)], [pltpu.SemaphoreType.DMA((n,))] * 2, start, finish)


ADD_HALF_STEPS = 2


def _add_half(fulls, gots, core, axes, *, name):
    ns = ADD_HALF_STEPS
    n = len(fulls)
    in_specs, out_specs, out_shape = [], [], []
    for got, axis in zip(gots, axes):
        r, c = got.shape
        blk = (r // ns, c)
        if axis == 0:
            in_specs.append(pl.BlockSpec(blk, lambda i, cr: (i + cr[0] * ns, 0)))
        else:
            in_specs.append(pl.BlockSpec(blk, lambda i, cr: (i, cr[0])))
        out_specs.append(pl.BlockSpec(blk, lambda i, cr: (i, 0)))
        out_shape.append(SDS((r, c), BF16))
    in_specs += list(out_specs)

    def body(c_ref, *refs):
        for a_ref, b_ref, o_ref in zip(refs[:n], refs[n:2 * n], refs[2 * n:]):
            o_ref[...] = (a_ref[...] + b_ref[...]).astype(BF16)

    return list(pl.pallas_call(
        body, grid_spec=pltpu.PrefetchScalarGridSpec(num_scalar_prefetch=1, grid=(ns,), in_specs=in_specs,
                                                     out_specs=out_specs),
        out_shape=out_shape, name=name, compiler_params=_params(("parallel",)))(core, *fulls, *gots))


def _add_pair(a, b, *, name):
    r, c = a.shape
    br, bc = 256, min(2048, c)

    def body(a_ref, b_ref, o_ref):
        o_ref[...] = (a_ref[...] + b_ref[...].astype(F32)).astype(BF16)

    spec = pl.BlockSpec((br, bc), lambda i, j: (i, j))
    return pl.pallas_call(body, grid=(r // br, c // bc), in_specs=[spec, spec], out_specs=spec,
                          out_shape=SDS((r, c), BF16), name=name, compiler_params=_params(("parallel", "parallel")))(a, b)


def _scatter_comm(halves, kinds):
    n = len(halves)

    def plan(ins, outs, sems):
        send, recv, lsem = sems
        x, y, c = _me()
        k = 2 * x + y

        def to_chip(t, j):
            return pltpu.make_async_remote_copy(
                src_ref=_piece(kinds[t], ins[t], j), dst_ref=outs[t].at[k], send_sem=send.at[t, j],
                recv_sem=recv.at[t, k], device_id=(j // 2, j % 2, c), device_id_type=MESH)

        def from_chip(t, j):
            return pltpu.make_async_remote_copy(
                src_ref=_piece(kinds[t], ins[t], j), dst_ref=outs[t].at[j], send_sem=send.at[t, j],
                recv_sem=recv.at[t, j], device_id=(j // 2, j % 2, c), device_id_type=MESH)

        def own(t, j):
            return pltpu.make_async_copy(_piece(kinds[t], ins[t], j), outs[t].at[j], lsem.at[t])

        return k, to_chip, from_chip, own

    def start(ins, outs, sems):
        k, to_chip, _, own = plan(ins, outs, sems)
        for j in range(N_CHIPS):
            @pl.when(k != j)
            def _(j=j):
                for t in range(n):
                    to_chip(t, j).start()

            @pl.when(k == j)
            def _(j=j):
                for t in range(n):
                    own(t, j).start()

    def finish(ins, outs, sems):
        k, to_chip, from_chip, own = plan(ins, outs, sems)
        for j in range(N_CHIPS):
            @pl.when(k != j)
            def _(j=j):
                for t in range(n):
                    from_chip(t, j).wait_recv()
                for t in range(n):
                    to_chip(t, j).wait_send()

            @pl.when(k == j)
            def _(j=j):
                for t in range(n):
                    own(t, j).wait()

    return _Comm(halves, [SDS((N_CHIPS,) + _piece_shape(kinds[t], halves[t].shape), halves[t].dtype) for t in range(n)],
                 [pltpu.SemaphoreType.DMA((n, N_CHIPS))] * 2 + [pltpu.SemaphoreType.DMA((n,))], start, finish)


SUM4_STEPS = 2


def _sum4(ps, *, name):
    ns = SUM4_STEPS

    def body(*refs):
        for p_ref, o_ref in zip(refs[:len(ps)], refs[len(ps):]):
            o_ref[...] = ((p_ref[0].astype(F32) + p_ref[1].astype(F32)) + p_ref[2].astype(F32)) + p_ref[3].astype(F32)

    return pl.pallas_call(
        body, grid=(ns,),
        in_specs=[pl.BlockSpec((N_CHIPS, p.shape[1] // ns, p.shape[2]), lambda i: (0, i, 0)) for p in ps],
        out_specs=[_rowblk(p.shape[1] // ns, p.shape[2]) for p in ps],
        out_shape=[SDS(p.shape[1:], F32) for p in ps], name=name, compiler_params=_params(("parallel",)))(*ps)


def _swap_comm(sums):
    return _exchange_comm(sums, [None] * len(sums), whole=tuple(range(len(sums))))


ADAMW_STEPS = 4


def _adamw_halves(items, core):
    ns = ADAMW_STEPS
    nbh = ns // 2
    n = len(items)
    in_specs, out_specs, out_shape, ins = [], [], [], []
    for mine, theirs, w, m, v, axis in items:
        rows, cols = w.shape
        tr = rows // ns
        if axis == 0:
            g_spec = pl.BlockSpec((tr, cols), lambda i, cr: (i % nbh, 0))
        else:
            g_spec = pl.BlockSpec((tr, cols // 2), lambda i, cr: (i, 0))
        nat = pl.BlockSpec((tr, cols), lambda i, cr: (i, 0))
        in_specs += [g_spec, g_spec, nat, nat, nat]
        out_specs += [nat] * 4
        out_shape += [SDS((rows, cols), F32)] * 4
        ins += [mine, theirs, w, m, v]

    def body(c_ref, *refs):
        for t, item in enumerate(items):
            a_ref, b_ref, w_ref, m_ref, v_ref = refs[5 * t:5 * t + 5]
            g_ref, d_ref, nm_ref, nv_ref = refs[5 * n + 4 * t:5 * n + 4 * t + 4]
            a, b = a_ref[...], b_ref[...]
            if item[5] == 0:
                g = jnp.where(pl.program_id(0) // nbh == c_ref[0], a, b)
            else:
                low = c_ref[0] == 0
                g = jnp.concatenate([jnp.where(low, a, b), jnp.where(low, b, a)], axis=1)
            g_ref[...] = g
            d, nm, nv = _adamw_math(w_ref[...], g, m_ref[...], v_ref[...])
            d_ref[...] = d
            nm_ref[...] = nm
            nv_ref[...] = nv

    res = pl.pallas_call(
        body, grid_spec=pltpu.PrefetchScalarGridSpec(
            num_scalar_prefetch=1, grid=(ns,), in_specs=in_specs, out_specs=out_specs),
        out_shape=out_shape, name="adamw_shards", compiler_params=_params(("arbitrary",)))(core, *ins)
    return [tuple(res[4 * t:4 * t + 4]) for t in range(n)]


SMALL_ROWS, SMALL_COLS = 8, 1024


def _pack_small(vs):
    flat = jnp.concatenate([v.reshape(-1) for v in vs])
    return jnp.pad(flat, (0, SMALL_ROWS * SMALL_COLS - flat.shape[0])).reshape(SMALL_ROWS, SMALL_COLS)


def _unpack_small(packed, sizes):
    flat = packed.reshape(-1)
    out, o = [], 0
    for n in sizes:
        out.append(flat[o:o + n].reshape(1, n))
        o += n
    return out


def _all_reduce_small(v):
    n_dev = 8

    def body(v_ref, o_ref, land, send, recv):
        x, y, c = _me()
        me = 4 * x + 2 * y + c
        land[me] = v_ref[...]
        cps = []
        for r in range(1, n_dev):
            fx, fy, fc = (r >> 2) & 1, (r >> 1) & 1, r & 1
            peer = (x ^ fx, y ^ fy, c ^ fc)
            cps.append(pltpu.make_async_remote_copy(
                src_ref=v_ref, dst_ref=land.at[me], send_sem=send.at[r - 1], recv_sem=recv.at[r - 1],
                device_id=peer, device_id_type=MESH))
        for cp in cps:
            cp.start()
        for r in range(1, n_dev):
            fx, fy, fc = (r >> 2) & 1, (r >> 1) & 1, r & 1
            src = 4 * (x ^ fx) + 2 * (y ^ fy) + (c ^ fc)
            pltpu.make_async_remote_copy(
                src_ref=v_ref, dst_ref=land.at[src], send_sem=send.at[r - 1], recv_sem=recv.at[r - 1],
                device_id=(x ^ fx, y ^ fy, c ^ fc), device_id_type=MESH).wait_recv()
        for cp in cps:
            cp.wait_send()
        acc = land[0]
        for r in range(1, n_dev):
            acc = acc + land[r]
        o_ref[...] = acc

    vm = pl.BlockSpec(memory_space=pltpu.VMEM)
    return pl.pallas_call(
        body, in_specs=[vm], out_specs=vm, out_shape=SDS(v.shape, F32),
        scratch_shapes=[pltpu.VMEM((n_dev,) + v.shape, F32), pltpu.SemaphoreType.DMA((n_dev - 1,)),
                        pltpu.SemaphoreType.DMA((n_dev - 1,))],
        name="all_reduce_small")(v)


def kernel(x, mem, norm_gain, mem_norm_gain, w_in, b_forget, q_gain_a, k_gain_a, sinks_a, q_gain_b, k_gain_b, q_gain_c, k_gain_c, w_mem_kv, w_branch_a, w_branch_b, w_branch_c, w_out, loss_target, m_norm_gain, m_mem_norm_gain, m_w_in, m_b_forget, m_q_gain_a, m_k_gain_a, m_sinks_a, m_q_gain_b, m_k_gain_b, m_q_gain_c, m_k_gain_c, m_w_mem_kv, m_w_branch_a, m_w_branch_b, m_w_branch_c, m_w_out, v_norm_gain, v_mem_norm_gain, v_w_in, v_b_forget, v_q_gain_a, v_k_gain_a, v_sinks_a, v_q_gain_b, v_k_gain_b, v_q_gain_c, v_k_gain_c, v_w_mem_kv, v_w_branch_a, v_w_branch_b, v_w_branch_c, v_w_out):
    xi, yi, ci = lax.axis_index("x"), lax.axis_index("y"), lax.axis_index("c")
    chip = jnp.reshape(2 * xi + yi, (1,)).astype(jnp.int32)
    core = jnp.reshape(ci, (1,)).astype(jnp.int32)

    slabs = _pack_w_in(chip, jnp.transpose(w_in[0]))
    mine = [w_mem_kv[0].astype(BF16), w_branch_a[0].astype(BF16), w_branch_b[0].astype(BF16),
            w_branch_c[0].astype(BF16), w_out[0].astype(BF16)]
    w_main, w_fb = _merge_slabs(_all_gather_slabs(slabs))

    r = _local_step(x[0], mem[0], loss_target[0], w_main, w_fb, mine, norm_gain, mem_norm_gain,
                    b_forget, q_gain_a, k_gain_a, sinks_a, q_gain_b, k_gain_b, q_gain_c, k_gain_c, core=core)
    sums, theirs = r["sums"], r["theirs"]

    small_names = ["d_gain", "d_mem_gain", "d_bf", "d_qga", "d_kga", "d_sinks", "d_qgb", "d_kgb", "d_qgc", "d_kgc"]
    loss_part = (0.5 / D_MODEL) * jnp.sum(r["sq"], axis=1, keepdims=True)
    packed = _pack_small([r[n] for n in small_names] + [loss_part])
    red = _all_reduce_small(packed)
    small_w = [norm_gain, mem_norm_gain, b_forget, q_gain_a, k_gain_a, sinks_a, q_gain_b, k_gain_b, q_gain_c, k_gain_c]
    small_m = [m_norm_gain, m_mem_norm_gain, m_b_forget, m_q_gain_a, m_k_gain_a, m_sinks_a, m_q_gain_b, m_k_gain_b,
               m_q_gain_c, m_k_gain_c]
    small_v = [v_norm_gain, v_mem_norm_gain, v_b_forget, v_q_gain_a, v_k_gain_a, v_sinks_a, v_q_gain_b, v_k_gain_b,
               v_q_gain_c, v_k_gain_c]
    sizes = [w.shape[1] for w in small_w]
    s_d, s_m, s_v = _adamw(red, _pack_small(small_w), _pack_small(small_m), _pack_small(small_v), tr=8, name="adamw_small")
    g_small = _unpack_small(red, sizes + [1])
    loss = g_small[-1].reshape(())
    d_small, m_small, v_small = _unpack_small(s_d, sizes), _unpack_small(s_m, sizes), _unpack_small(s_v, sizes)

    gw_in, dw_in, mw_in, vw_in = _adamw_w_in(jnp.concatenate([chip, core]), sums[0], theirs[0], sums[1], theirs[1],
                                             jnp.transpose(w_in[0]), jnp.transpose(m_w_in[0]), jnp.transpose(v_w_in[0]))
    shards = ((2, "w_mem_kv", w_mem_kv, m_w_mem_kv, v_w_mem_kv),
              (3, "w_branch_a", w_branch_a, m_w_branch_a, v_w_branch_a),
              (4, "w_branch_b", w_branch_b, m_w_branch_b, v_w_branch_b),
              (5, "w_branch_c", w_branch_c, m_w_branch_c, v_w_branch_c),
              (6, "w_out", w_out, m_w_out, v_w_out))
    done = _adamw_halves([(sums[t], theirs[t], w[0], m[0], v[0], HALF_AXIS[t]) for t, _, w, m, v in shards], core)
    big = {nm: res for (_, nm, _, _, _), res in zip(shards, done)}

    def collect(kind):
        sm = (g_small, d_small, m_small, v_small)[kind]
        win = (gw_in, dw_in, mw_in, vw_in)[kind]
        return ([sm[0], sm[1], jnp.transpose(win)[None]] + [a for a in sm[2:10]]
                + [big[n][kind][None] for n in ("w_mem_kv", "w_branch_a", "w_branch_b", "w_branch_c", "w_out")])

    return (loss, r["grad_x"][None], *collect(0), *collect(1), *collect(2), *collect(3))
```

```python
import numpy as np
import jax
import jax.numpy as jnp
from jax import lax
from jax.experimental import pallas as pl
from jax.experimental.pallas import tpu as pltpu

F32 = jnp.float32
BF16 = jnp.bfloat16
HI = lax.Precision.HIGHEST
SDS = jax.ShapeDtypeStruct
MESH = pl.DeviceIdType.MESH

D_MODEL = 2048
HEAD_DIM = 64
A_HEADS = 12
A_GROUP = 3
B_HEADS = 12
C_HEADS = 4
C_HEAD_DIM = 128
WINDOW = 128
EPS = 1e-6
NEG = -1e30
LANE = 128

QA, KA, VA, ZA = 0, 768, 1024, 1280
QB, KB, VB, ZB = 2048, 2816, 3584, 4352
QC, ZC = 5120, 5632
GATE = 6144
P_MAIN = 12288
N_FORGET = 12
FORGET_COL = 5120
SHARD_COLS = 3075
SLAB = 3200
SLAB_START = (0, 3072, 6016, 9088)
SLAB_SHIFT = (0, 3, 122, 125)
N_CHIPS = 4

ADAM_LR = 0.001
ADAM_B1 = 0.9
ADAM_B2 = 0.999
ADAM_EPS = 1e-08
ADAM_WD = 0.01
ADAM_STEP = 10

VMEM_LIMIT = 56 * 1024 * 1024
VMEM_WIDE = 62 * 1024 * 1024


def _params(sem, vmem=VMEM_LIMIT):
    return pltpu.CompilerParams(dimension_semantics=sem, vmem_limit_bytes=vmem)


def _win(tr, width, off):
    return pl.BlockSpec((pl.Element(tr), pl.Element(width)), lambda i, *_: (i * tr, off))


def _rowblk(tr, width):
    return pl.BlockSpec((tr, width), lambda i, *_: (i, 0))


def _const(shape):
    nd = len(shape)
    return pl.BlockSpec(shape, lambda *_: (0,) * nd)


def _const_once(shape):
    nd = len(shape)
    return pl.BlockSpec(shape, lambda *_: (0,) * nd, pipeline_mode=pl.Buffered(1))


def _rms(x, g):
    return x * lax.rsqrt(jnp.mean(x * x, axis=-1, keepdims=True) + EPS) * g


def _head_mean_impl(x2, bd):
    hi = x2.astype(BF16)
    lo = (x2 - hi.astype(F32)).astype(BF16)
    return _dot(hi, bd) + _dot(lo, bd)


@jax.custom_vjp
def _head_mean(x2, bd):
    return _head_mean_impl(x2, bd)


_head_mean.defvjp(lambda x2, bd: (_head_mean_impl(x2, bd), bd),
                  lambda bd, g: (_head_mean_impl(g, bd), jnp.zeros_like(bd)))


def _head_norm(x, g_tiled, bd):
    return x * lax.rsqrt(_head_mean(x * x, bd) + EPS) * g_tiled


def _silu(z):
    return z * jax.nn.sigmoid(z)


def _dot_nt(a, b):
    return lax.dot_general(a, b, (((1,), (1,)), ((), ())), preferred_element_type=F32)


def _dot_tn(a, b):
    return lax.dot_general(a, b, (((0,), (0,)), ((), ())), preferred_element_type=F32)


def _dot(a, b):
    return jnp.dot(a, b, preferred_element_type=F32)


def _swa_fn(qk, vz, qkp, vzp, qg, kg, sinks, bd, bias, first):
    q = _head_norm(qk[:, :768], qg, bd)
    k2 = jnp.concatenate([qkp[:, 768:], qk[:, 768:]], axis=0)
    k2 = _head_norm(k2, kg, bd[:256, :256])
    v2 = jnp.concatenate([vzp[:, :256], vz[:, :256]], axis=0)
    z = vz[:, 256:]
    cols = A_GROUP * WINDOW
    kj = lax.broadcasted_iota(jnp.int32, (2 * WINDOW, cols), 0)
    no_prev = kj < WINDOW * first.astype(jnp.int32)
    qtb = jnp.transpose(q).astype(BF16)
    kb = k2.astype(BF16)
    vtb = jnp.transpose(v2).astype(BF16)
    outs = [None] * A_HEADS
    for g in range(A_HEADS // A_GROUP):
        heads = [A_GROUP * g + u for u in range(A_GROUP)]
        qs = jnp.concatenate([qtb[64 * h:64 * h + 64, :] for h in heads], axis=1)
        s = _dot(kb[:, 64 * g:64 * g + 64], qs) * (HEAD_DIM ** -0.5) + bias[g]
        s = jnp.where(no_prev, NEG, s)
        sink = jnp.concatenate([jnp.broadcast_to(sinks[:, h:h + 1], (1, WINDOW)) for h in heads], axis=1)
        m = lax.stop_gradient(jnp.maximum(jnp.max(s, axis=0, keepdims=True), sink))
        p = jnp.exp(s - m)
        den = jnp.sum(p, axis=0, keepdims=True) + jnp.exp(sink - m)
        o = _dot(vtb[64 * g:64 * g + 64, :], (p * (1.0 / den)).astype(BF16))
        for u, h in enumerate(heads):
            outs[h] = o[:, WINDOW * u:WINDOW * u + WINDOW]
    return jnp.transpose(jnp.concatenate(outs, axis=0)) * _silu(z)


def _swa_bias():
    qi = np.arange(WINDOW)[None, :]
    kj = np.arange(2 * WINDOW)[:, None]
    rel = qi + WINDOW - kj
    valid = (rel >= 0) & (rel < WINDOW)
    out = np.zeros((A_HEADS // A_GROUP, 2 * WINDOW, A_GROUP * WINDOW), np.float32)
    for h in range(A_HEADS):
        slope = np.float32(2.0 ** (-8.0 * (h + 1) / A_HEADS))
        blk = np.where(valid, -slope * rel.astype(np.float32), np.float32(NEG))
        g, u = divmod(h, A_GROUP)
        out[g, :, WINDOW * u:WINDOW * u + WINDOW] = blk
    return jnp.asarray(out)


def _mem_fn(qz, mkv, qg, kg, bd):
    q = _head_norm(qz[:, :512], qg, bd).astype(BF16)
    k = _head_norm(mkv[:, :512], kg, bd).astype(BF16)
    v = mkv[:, 512:].astype(BF16)
    z = qz[:, 512:]
    outs = []
    for h in range(C_HEADS):
        sl = slice(128 * h, 128 * h + 128)
        s = _dot_nt(q[:, sl], k[:, sl]) * (C_HEAD_DIM ** -0.5)
        m = lax.stop_gradient(jnp.max(s, axis=-1, keepdims=True))
        p = jnp.exp(s - m)
        den = jnp.sum(p, axis=-1, keepdims=True)
        outs.append(_dot((p * (1.0 / den)).astype(BF16), v[:, sl]))
    return jnp.concatenate(outs, axis=1) * _silu(z)


def _qn_fn(q, g, bd):
    return _head_norm(q, g, bd) * (HEAD_DIM ** -0.5)


def _kn_fn(k, g, bd):
    return _head_norm(k, g, bd)


def _block_diag(width, hd):
    i = np.arange(width) // hd
    return jnp.asarray((i[:, None] == i[None, :]).astype(np.float32) / hd, BF16)


def _head_sum(width, hd):
    i = np.arange(width) // hd
    return jnp.asarray((i[:, None] == np.arange(LANE)[None, :]).astype(np.float32))


def _rms_fwd(x, g, *, tr, name):
    rows, dm = x.shape

    def body(x_ref, g_ref, o_ref):
        o_ref[...] = _rms(x_ref[...], g_ref[...]).astype(BF16)

    return pl.pallas_call(
        body, grid=(rows // tr,),
        in_specs=[_rowblk(tr, dm), _const((1, dm))],
        out_specs=_rowblk(tr, dm),
        out_shape=SDS((rows, dm), BF16), name=name,
        compiler_params=_params(("parallel",)))(x, g)


def _rms_bwd(x, g, dy, resid, *, tr, name, comm=None):
    rows, dm = x.shape
    want_dx = resid is not None
    n_in = 4 if want_dx else 3
    n_out = 2 if want_dx else 1
    c_in = len(comm.ins) if comm else 0
    c_out = len(comm.out_shapes) if comm else 0
    nb = rows // tr

    def body(*refs):
        x_ref, g_ref, dy_ref = refs[:3]
        r_ref = refs[3] if want_dx else None
        cin = refs[n_in:n_in + c_in]
        outs = refs[n_in + c_in:n_in + c_in + n_out]
        dg_ref = outs[-1]
        cout = refs[n_in + c_in + n_out:n_in + c_in + n_out + c_out]
        csem = refs[n_in + c_in + n_out + c_out:]

        if comm:
            @pl.when(pl.program_id(0) == 0)
            def _():
                comm.start(cin, cout, csem)

        _, vjp = jax.vjp(_rms, x_ref[...], g_ref[...])
        dx, dg = vjp(dy_ref[...])

        @pl.when(pl.program_id(0) == 0)
        def _():
            dg_ref[...] = jnp.zeros_like(dg_ref)

        dg_ref[...] += dg
        if want_dx:
            outs[0][...] = r_ref[...] + dx

        if comm:
            @pl.when(pl.program_id(0) == nb - 1)
            def _():
                comm.finish(cin, cout, csem)

    hbm = pl.BlockSpec(memory_space=pl.ANY)
    ins = [x, g, dy] + ([resid] if want_dx else []) + (list(comm.ins) if comm else [])
    in_specs = ([_rowblk(tr, dm), _const((1, dm)), _rowblk(tr, dm)] + ([_rowblk(tr, dm)] if want_dx else [])
                + [hbm] * c_in)
    out_specs = ([_rowblk(tr, dm)] if want_dx else []) + [_const((1, dm))] + [hbm] * c_out
    out_shape = (([SDS((rows, dm), F32)] if want_dx else []) + [SDS((1, dm), F32)]
                 + (list(comm.out_shapes) if comm else []))
    res = pl.pallas_call(
        body, grid=(nb,), in_specs=in_specs, out_specs=out_specs, out_shape=out_shape,
        scratch_shapes=list(comm.sems) if comm else [], name=name, compiler_params=_params(("arbitrary",)))(*ins)
    return (list(res[:n_out]), list(res[n_out:])) if comm else res


class _Comm:
    def __init__(self, ins, out_shapes, sems, start, finish):
        self.ins, self.out_shapes, self.sems, self.start, self.finish = list(ins), list(out_shapes), list(sems), start, finish


def _matmul(a, b, *, dims, out_dtype, tm, tn, tk, name, add=None, comms=(), vmem=VMEM_LIMIT, a_half=None):
    a_list = list(a) if isinstance(a, (list, tuple)) else [a]
    b_list = list(b) if isinstance(b, (list, tuple)) else [b]
    assert len(a_list) == 1 or dims == "nt"
    assert len(b_list) == 1 or dims == "tn"
    if dims == "tn":
        kdim, m = a_list[0].shape
        if a_half is not None:
            m //= 2
    else:
        m, kdim = a_list[0].shape[0], sum(p.shape[1] for p in a_list)
    n = b_list[0].shape[0] if dims == "nt" else sum(p.shape[1] for p in b_list)
    tm, tn, tk = min(tm, m), min(tn, n), min(tk, kdim)
    assert m % tm == 0 and n % tn == 0 and kdim % tk == 0, (name, m, n, kdim)
    ni, nj, nk = m // tm, n // tn, kdim // tk
    a_rng, b_rng, pos = [], [], 0
    for p in a_list:
        assert len(a_list) == 1 or p.shape[1] % tk == 0
        a_rng.append((pos, p.shape[1] // tk if len(a_list) > 1 else nk))
        pos += a_rng[-1][1]
    pos = 0
    for p in b_list:
        assert len(b_list) == 1 or p.shape[1] % tn == 0
        b_rng.append((pos, p.shape[1] // tn if len(b_list) > 1 else nj))
        pos += b_rng[-1][1]
    has_add = add is not None
    n_mm_in = len(a_list) + len(b_list) + (1 if has_add else 0)
    c_in = [len(c.ins) for c in comms]
    c_out = [len(c.out_shapes) for c in comms]
    c_sem = [len(c.sems) for c in comms]

    def body(*refs):
        if a_half is not None:
            refs = refs[1:]
        a_refs, b_refs = refs[:len(a_list)], refs[len(a_list):len(a_list) + len(b_list)]
        add_ref = refs[n_mm_in - 1] if has_add else None
        pos = n_mm_in
        cin = []
        for cnt in c_in:
            cin.append(refs[pos:pos + cnt])
            pos += cnt
        o_ref = refs[pos]
        pos += 1
        cout = []
        for cnt in c_out:
            cout.append(refs[pos:pos + cnt])
            pos += cnt
        acc = refs[pos]
        pos += 1
        csem = []
        for cnt in c_sem:
            csem.append(refs[pos:pos + cnt])
            pos += cnt
        i, j, k = pl.program_id(0), pl.program_id(1), pl.program_id(2)

        if comms:
            @pl.when((i == 0) & (j == 0) & (k == 0))
            def _():
                for c, ci, co, cs in zip(comms, cin, cout, csem):
                    c.start(ci, co, cs)

        def accumulate(a_ref, b_ref, first_k, later_k):
            if dims == "nn":
                part = _dot(a_ref[...], b_ref[...])
            elif dims == "nt":
                part = _dot_nt(a_ref[...], b_ref[...])
            else:
                part = _dot_tn(a_ref[...], b_ref[...])

            if first_k:
                @pl.when(k == 0)
                def _():
                    acc[...] = part + add_ref[...] if has_add else part

            if later_k:
                @pl.when(k > 0)
                def _():
                    acc[...] += part

        if len(a_list) > 1:
            for a_ref, (k0, cnt) in zip(a_refs, a_rng):
                @pl.when((k >= k0) & (k < k0 + cnt))
                def _(a_ref=a_ref, k0=k0, cnt=cnt):
                    accumulate(a_ref, b_refs[0], k0 == 0, k0 + cnt > 1)
        elif len(b_list) > 1:
            for b_ref, (j0, cnt) in zip(b_refs, b_rng):
                @pl.when((j >= j0) & (j < j0 + cnt))
                def _(b_ref=b_ref):
                    accumulate(a_refs[0], b_ref, True, nk > 1)
        else:
            accumulate(a_refs[0], b_refs[0], True, nk > 1)

        @pl.when(k == nk - 1)
        def _():
            o_ref[...] = acc[...].astype(out_dtype)

        if comms:
            @pl.when((i == ni - 1) & (j == nj - 1) & (k == nk - 1))
            def _():
                for c, ci, co, cs in zip(comms, cin, cout, csem):
                    c.finish(ci, co, cs)

    def a_spec(k0, cnt):
        if dims == "tn":
            if a_half is None:
                return pl.BlockSpec((tk, tm), lambda i, j, k: (k, i))
            pick = (lambda c: 1 - c[0]) if a_half[1] else (lambda c: c[0])
            return pl.BlockSpec((tk, tm), lambda i, j, k, c: (k, i + pick(c) * ni))
        return pl.BlockSpec((tm, tk), lambda i, j, k, *_: (i, jnp.clip(k - k0, 0, cnt - 1)))

    def b_spec(j0, cnt):
        if dims == "nt":
            return pl.BlockSpec((tn, tk), lambda i, j, k, *_: (j, k))
        return pl.BlockSpec((tk, tn), lambda i, j, k, *_: (k, jnp.clip(j - j0, 0, cnt - 1)))

    o_spec = pl.BlockSpec((tm, tn), lambda i, j, k, *_: (i, j))
    hbm = pl.BlockSpec(memory_space=pl.ANY)
    ins = a_list + b_list + ([add] if has_add else []) + [x for c in comms for x in c.ins]
    in_specs = ([a_spec(*r) for r in a_rng] + [b_spec(*r) for r in b_rng] + ([o_spec] if has_add else [])
                + [hbm] * sum(c_in))
    out_specs = [o_spec] + [hbm] * sum(c_out)
    out_shape = [SDS((m, n), out_dtype)] + [s for c in comms for s in c.out_shapes]
    scratch = [pltpu.VMEM((tm, tn), F32)] + [s for c in comms for s in c.sems]
    sem = ("arbitrary",) * 3 if comms else ("parallel", "parallel", "arbitrary")
    if a_half is None:
        grid = dict(grid=(ni, nj, nk), in_specs=in_specs, out_specs=out_specs, scratch_shapes=scratch)
    else:
        grid = dict(grid_spec=pltpu.PrefetchScalarGridSpec(
            num_scalar_prefetch=1, grid=(ni, nj, nk), in_specs=in_specs, out_specs=out_specs, scratch_shapes=scratch))
        ins = [a_half[0]] + ins
    res = pl.pallas_call(body, out_shape=out_shape, name=name, compiler_params=_params(sem, vmem), **grid)(*ins)
    if not comms:
        return res[0]
    outs, pos = [], 1
    for cnt in c_out:
        outs.append(list(res[pos:pos + cnt]))
        pos += cnt
    return res[0], outs


def _swa_specs(nb, blk=lambda n: n):
    cur = lambda off: pl.BlockSpec((pl.Element(WINDOW), pl.Element(1024)), lambda n: (blk(n) * WINDOW, off))
    prev = lambda off: pl.BlockSpec((pl.Element(WINDOW), pl.Element(1024)),
                                    lambda n: (jnp.maximum(blk(n) - 1, 0) * WINDOW, off))
    return [cur(QA), cur(VA), prev(QA), prev(VA),
            _const((1, 768)), _const((1, 256)), _const((1, LANE)), _const((768, 768)),
            _const((A_HEADS // A_GROUP, 2 * WINDOW, A_GROUP * WINDOW))]


def _swa_fwd(proj, qg, kg, sinks, bd, bias):
    s = proj.shape[0]
    nb = s // WINDOW

    def body(qk_ref, vz_ref, qkp_ref, vzp_ref, qg_ref, kg_ref, sk_ref, bd_ref, bias_ref, o_ref):
        first = pl.program_id(0) == 0
        o_ref[...] = _swa_fn(qk_ref[...], vz_ref[...], qkp_ref[...], vzp_ref[...], qg_ref[...], kg_ref[...],
                             sk_ref[...], bd_ref[...], bias_ref[...], first).astype(BF16)

    return pl.pallas_call(
        body, grid=(nb,), in_specs=_swa_specs(nb), out_specs=_rowblk(WINDOW, 768),
        out_shape=SDS((s, 768), BF16), name="swa_fwd",
        compiler_params=_params(("parallel",)))(proj, proj, proj, proj, qg, kg, sinks, bd, bias)


def _swa_bwd(proj, qg, kg, sinks, bd, bias, dga):
    s = proj.shape[0]
    nb = s // WINDOW
    blk = lambda n: nb - 1 - n

    def body(qk_ref, vz_ref, qkp_ref, vzp_ref, qg_ref, kg_ref, sk_ref, bd_ref, bias_ref, dg_ref,
             d_ref, dqg_ref, dkg_ref, dsk_ref, carry):
        first = blk(pl.program_id(0)) == 0
        bd_v = bd_ref[...]
        bias_v = bias_ref[...]
        fn = lambda qk, vz, qkp, vzp, qg_, kg_, sk: _swa_fn(qk, vz, qkp, vzp, qg_, kg_, sk, bd_v, bias_v, first)
        _, vjp = jax.vjp(fn, qk_ref[...], vz_ref[...], qkp_ref[...], vzp_ref[...], qg_ref[...], kg_ref[...], sk_ref[...])
        dqk, dvz, dqkp, dvzp, dqg, dkg, dsk = vjp(dg_ref[...])

        @pl.when(pl.program_id(0) == 0)
        def _():
            dqg_ref[...] = jnp.zeros_like(dqg_ref)
            dkg_ref[...] = jnp.zeros_like(dkg_ref)
            dsk_ref[...] = jnp.zeros_like(dsk_ref)
            carry[...] = jnp.zeros_like(carry)

        dqg_ref[...] += dqg
        dkg_ref[...] += dkg
        dsk_ref[...] += dsk
        kv = jnp.concatenate([dqk[:, 768:], dvz[:, :256]], axis=1) + carry[...]
        d_ref[...] = jnp.concatenate([dqk[:, :768], kv, dvz[:, 256:]], axis=1).astype(BF16)
        carry[...] = jnp.concatenate([dqkp[:, 768:], dvzp[:, :256]], axis=1)

    rev = lambda w: pl.BlockSpec((WINDOW, w), lambda n: (blk(n), 0))
    return pl.pallas_call(
        body, grid=(nb,), in_specs=_swa_specs(nb, blk) + [rev(768)],
        out_specs=[rev(2048), _const((1, 768)), _const((1, 256)), _const((1, LANE))],
        out_shape=[SDS((s, 2048), BF16), SDS((1, 768), F32), SDS((1, 256), F32), SDS((1, LANE), F32)],
        scratch_shapes=[pltpu.VMEM((WINDOW, 512), F32)],
        name="swa_bwd", compiler_params=_params(("arbitrary",)))(proj, proj, proj, proj, qg, kg, sinks, bd, bias, dga)


def _mem_fwd(proj, mkv, qg, kg, bd, *, tr):
    s = proj.shape[0]

    def body(qz_ref, mkv_ref, qg_ref, kg_ref, bd_ref, o_ref):
        o_ref[...] = _mem_fn(qz_ref[...], mkv_ref[...], qg_ref[...], kg_ref[...], bd_ref[...]).astype(BF16)

    return pl.pallas_call(
        body, grid=(s // tr,),
        in_specs=[_win(tr, 1024, QC), _const(mkv.shape), _const((1, 512)), _const((1, 512)), _const((512, 512))],
        out_specs=_rowblk(tr, 512), out_shape=SDS((s, 512), BF16), name="mem_fwd",
        compiler_params=_params(("parallel",)))(proj, mkv, qg, kg, bd)


def _mem_bwd(proj, mkv, qg, kg, bd, dgc, *, tr):
    s = proj.shape[0]

    def body(qz_ref, mkv_ref, qg_ref, kg_ref, bd_ref, dg_ref, dqz_ref, dmkv_ref, dqg_ref, dkg_ref):
        bd_v = bd_ref[...]
        fn = lambda qz, mkv_, qg_, kg_: _mem_fn(qz, mkv_, qg_, kg_, bd_v)
        _, vjp = jax.vjp(fn, qz_ref[...], mkv_ref[...], qg_ref[...], kg_ref[...])
        dqz, dmkv, dqg, dkg = vjp(dg_ref[...])

        @pl.when(pl.program_id(0) == 0)
        def _():
            dmkv_ref[...] = jnp.zeros_like(dmkv_ref)
            dqg_ref[...] = jnp.zeros_like(dqg_ref)
            dkg_ref[...] = jnp.zeros_like(dkg_ref)

        dmkv_ref[...] += dmkv
        dqg_ref[...] += dqg
        dkg_ref[...] += dkg
        dqz_ref[...] = dqz.astype(BF16)

    return pl.pallas_call(
        body, grid=(s // tr,),
        in_specs=[_win(tr, 1024, QC), _const(mkv.shape), _const((1, 512)), _const((1, 512)), _const((512, 512)),
                  _rowblk(tr, 512)],
        out_specs=[_rowblk(tr, 1024), _const(mkv.shape), _const((1, 512)), _const((1, 512))],
        out_shape=[SDS((s, 1024), BF16), SDS(mkv.shape, F32), SDS((1, 512), F32), SDS((1, 512), F32)],
        name="mem_bwd", compiler_params=_params(("arbitrary",)))(proj, mkv, qg, kg, bd, dgc)


def _log_sigmoid(x):
    return jnp.minimum(x, 0.0) - jnp.log1p(jnp.exp(-jnp.abs(x)))


FOX_TQ, FOX_TK = 512, 512
FOX_FWD_TQ, FOX_FWD_TK = 512, 1024


def _fox_tiles(s):
    return min(FOX_TQ, s), min(FOX_TK, s)


AUG = 128 * B_HEADS
COL_A, COL_B = 64, 67


def _split3(c):
    hi = c.astype(BF16)
    r1 = c - hi.astype(F32)
    mid = r1.astype(BF16)
    lo = (r1 - mid.astype(F32)).astype(BF16)
    return hi, mid, lo


def _expand_mats():
    def mat(col0):
        e = np.zeros((768 + 3 * LANE, AUG), np.float32)
        for h in range(B_HEADS):
            for d in range(HEAD_DIM):
                e[64 * h + d, 128 * h + d] = 1.0
            for part in range(3):
                e[768 + LANE * part + h, 128 * h + col0 + part] = 1.0
        return e

    def ones(col0):
        o = np.zeros((1, AUG), np.float32)
        for h in range(B_HEADS):
            o[0, 128 * h + col0:128 * h + col0 + 3] = 1.0
        return o

    return (jnp.asarray(mat(COL_A), BF16), jnp.asarray(mat(COL_B), BF16), jnp.asarray(ones(COL_A)), jnp.asarray(ones(COL_B)))


def _augment(data_bf16, triple, emat, ones_row):
    parts = [data_bf16] + (list(triple) if triple is not None else [jnp.zeros((data_bf16.shape[0], LANE), BF16)] * 3)
    wide = _dot(jnp.concatenate(parts, axis=1), emat)
    if ones_row is not None:
        wide = wide + ones_row
    return wide


def _compact(wide):
    return jnp.concatenate([wide[:, 128 * h:128 * h + 64] for h in range(wide.shape[1] // 128)], axis=1)


def _lane_of_heads(wide, col, first=0):
    rows = wide.shape[0]
    lane = lax.broadcasted_iota(jnp.int32, (rows, LANE), 1)
    out = jnp.zeros((rows, LANE), F32)
    for h in range(wide.shape[1] // 128):
        out = jnp.where(lane == first + h, wide[:, 128 * h + col:128 * h + col + 1], out)
    return out


def _fox2_prep(proj, fbl, qg, kg, bfor, bd, ea, eb, ones_a, ones_b, *, tr):
    s = proj.shape[0]
    tri = jnp.asarray(np.tril(np.ones((tr, tr), np.float32)))

    def body(q_ref, k_ref, v_ref, fb_ref, qg_ref, kg_ref, bf_ref, bd_ref, tri_ref, ea_ref, eb_ref, oa_ref, ob_ref,
             qat_ref, ka_ref, kat_ref, va_ref, vat_ref, qn_ref, c_ref, carry):
        @pl.when(pl.program_id(0) == 0)
        def _():
            carry[...] = jnp.zeros_like(carry)

        bd_v = bd_ref[...]
        lane = lax.broadcasted_iota(jnp.int32, (tr, LANE), 1)
        logf = jnp.where(lane < N_FORGET, _log_sigmoid(fb_ref[...] + bf_ref[...]), 0.0)
        c = jnp.dot(tri_ref[...], logf, precision=HI, preferred_element_type=F32) + carry[...]
        c_ref[...] = c
        carry[...] = c[tr - 1:tr, :]
        qn = _qn_fn(q_ref[...], qg_ref[...], bd_v).astype(BF16)
        kn = _kn_fn(k_ref[...], kg_ref[...], bd_v).astype(BF16)
        qn_ref[...] = qn
        qat_ref[...] = jnp.transpose(_augment(qn, _split3(c), ea_ref[...], ob_ref[...])).astype(BF16)
        ka = _augment(kn, _split3(-c), eb_ref[...], oa_ref[...])
        ka_ref[...] = ka.astype(BF16)
        kat_ref[...] = jnp.transpose(ka).astype(BF16)
        va = _augment(v_ref[...].astype(BF16), None, ea_ref[...], oa_ref[...])
        va_ref[...] = va.astype(BF16)
        vat_ref[...] = jnp.transpose(va).astype(BF16)

    emat = _const_once((768 + 3 * LANE, AUG))
    return pl.pallas_call(
        body, grid=(s // tr,),
        in_specs=[_win(tr, 768, QB), _win(tr, 768, KB), _win(tr, 768, VB), _rowblk(tr, LANE), _const((1, 768)),
                  _const((1, 768)), _const((1, LANE)), _const_once((768, 768)), _const_once((tr, tr)), emat, emat,
                  _const((1, AUG)), _const((1, AUG))],
        out_specs=[pl.BlockSpec((AUG, tr), lambda i: (0, i)), _rowblk(tr, AUG), pl.BlockSpec((AUG, tr), lambda i: (0, i)),
                   _rowblk(tr, AUG), pl.BlockSpec((AUG, tr), lambda i: (0, i)), _rowblk(tr, 768), _rowblk(tr, LANE)],
        out_shape=[SDS((AUG, s), BF16), SDS((s, AUG), BF16), SDS((AUG, s), BF16), SDS((s, AUG), BF16),
                   SDS((AUG, s), BF16), SDS((s, 768), BF16), SDS((s, LANE), F32)],
        scratch_shapes=[pltpu.VMEM((1, LANE), F32)], name="fox_prep",
        compiler_params=_params(("arbitrary",)))(proj, proj, proj, fbl, qg, kg, bfor, bd, tri, ea, eb, ones_a, ones_b)


def _fox2_fwd(proj, qat, ka, vat):
    s = proj.shape[0]
    tq, tk = min(FOX_FWD_TQ, s), min(FOX_FWD_TK, s)
    nq, nk = s // tq, s // tk

    def last_k(i):
        return (i * tq + tq - 1) // tk

    def body(qt_ref, k_ref, vt_ref, z_ref, gb_ref, yb_ref, lse_ref, acc, m_s):
        i, j = pl.program_id(0), pl.program_id(1)

        @pl.when(j == 0)
        def _():
            acc[...] = jnp.zeros_like(acc)
            m_s[...] = jnp.full_like(m_s, NEG)

        def tile(masked):
            if masked:
                kpos = j * tk + lax.broadcasted_iota(jnp.int32, (tk, tq), 0)
                qpos = i * tq + lax.broadcasted_iota(jnp.int32, (tk, tq), 1)
                mask = kpos <= qpos
            for h in range(B_HEADS):
                sl = slice(128 * h, 128 * h + 128)
                sc = _dot(k_ref[:, sl], qt_ref[sl, :])
                if masked:
                    sc = jnp.where(mask, sc, NEG)
                m_prev = m_s[h:h + 1, :]
                m_new = jnp.maximum(m_prev, jnp.max(sc, axis=0, keepdims=True))
                p = jnp.exp(sc - m_new).astype(BF16)
                acc[sl, :] = jnp.exp(m_prev - m_new) * acc[sl, :] + _dot(vt_ref[sl, :], p)
                m_s[h:h + 1, :] = m_new

        full = j * tk + tk - 1 <= i * tq

        @pl.when(full)
        def _():
            tile(False)

        @pl.when(jnp.logical_and(jnp.logical_not(full), j <= last_k(i)))
        def _():
            tile(True)

        @pl.when(j == nk - 1)
        def _():
            outs = []
            row = lax.broadcasted_iota(jnp.int32, (LANE, tq), 0)
            lse_t = jnp.zeros((LANE, tq), F32)
            for h in range(B_HEADS):
                l_row = acc[128 * h + COL_A:128 * h + COL_A + 1, :]
                outs.append(acc[128 * h:128 * h + 64, :] * (1.0 / l_row))
                lse_t = jnp.where(row == h, m_s[h:h + 1, :] + jnp.log(l_row), lse_t)
            y = jnp.transpose(jnp.concatenate(outs, axis=0))
            yb_ref[...] = y
            gb_ref[...] = (y * _silu(z_ref[...])).astype(BF16)
            lse_ref[...] = jnp.transpose(lse_t)

    kcol = lambda i, j: (0, jnp.minimum(j, last_k(i)))
    return pl.pallas_call(
        body, grid=(nq, nk),
        in_specs=[pl.BlockSpec((AUG, tq), lambda i, j: (0, i)),
                  pl.BlockSpec((tk, AUG), lambda i, j: (jnp.minimum(j, last_k(i)), 0)),
                  pl.BlockSpec((AUG, tk), kcol),
                  pl.BlockSpec((pl.Element(tq), pl.Element(768)), lambda i, j: (i * tq, ZB))],
        out_specs=[pl.BlockSpec((tq, 768), lambda i, j: (i, 0)), pl.BlockSpec((tq, 768), lambda i, j: (i, 0)),
                   pl.BlockSpec((tq, LANE), lambda i, j: (i, 0))],
        out_shape=[SDS((s, 768), BF16), SDS((s, 768), F32), SDS((s, LANE), F32)],
        scratch_shapes=[pltpu.VMEM((AUG, tq), F32), pltpu.VMEM((16, tq), F32)],
        name="fox_fwd", compiler_params=_params(("parallel", "arbitrary")))(qat, ka, vat, proj)


def _fox2_bwd_pre(proj, yb, dgb, qn, c, lse, hsum, ea, ones_b, *, tr):
    s = proj.shape[0]

    def body(z_ref, y_ref, dg_ref, qn_ref, c_ref, lse_ref, hs_ref, ea_ref, ob_ref,
             qa_ref, qat_ref, dya_ref, dyat_ref, dz_ref):
        z, y, dg = z_ref[...], y_ref[...], dg_ref[...]
        sg = jax.nn.sigmoid(z)
        dy = dg * (z * sg)
        dz_ref[...] = (dg * y * (sg * (1.0 + z * (1.0 - sg)))).astype(BF16)
        delta = jnp.dot(dy * y, hs_ref[...], precision=HI, preferred_element_type=F32)
        e = ea_ref[...]
        dya = _augment(dy.astype(BF16), _split3(-delta), e, None)
        dya_ref[...] = dya.astype(BF16)
        dyat_ref[...] = jnp.transpose(dya).astype(BF16)
        qa = _augment(qn_ref[...], _split3(c_ref[...] - lse_ref[...]), e, ob_ref[...])
        qa_ref[...] = qa.astype(BF16)
        qat_ref[...] = jnp.transpose(qa).astype(BF16)

    return pl.pallas_call(
        body, grid=(s // tr,),
        in_specs=[_win(tr, 768, ZB), _rowblk(tr, 768), _rowblk(tr, 768), _rowblk(tr, 768), _rowblk(tr, LANE),
                  _rowblk(tr, LANE), _const_once((768, LANE)), _const_once((768 + 3 * LANE, AUG)), _const((1, AUG))],
        out_specs=[_rowblk(tr, AUG), pl.BlockSpec((AUG, tr), lambda i: (0, i)), _rowblk(tr, AUG),
                   pl.BlockSpec((AUG, tr), lambda i: (0, i)), _rowblk(tr, 768)],
        out_shape=[SDS((s, AUG), BF16), SDS((AUG, s), BF16), SDS((s, AUG), BF16), SDS((AUG, s), BF16),
                   SDS((s, 768), BF16)], name="fox_bwd_pre",
        compiler_params=_params(("parallel",)))(proj, yb, dgb, qn, c, lse, hsum, ea, ones_b)


def _fox2_bwd(qb, qbt, ka, kat, va, dya, dyat):
    s = qb.shape[0]
    tq, tk = _fox_tiles(s)
    nq, nk = s // tq, s // tk
    ng = 2
    gh = B_HEADS // ng
    gw = 128 * gh

    def first_q(j):
        return (j * tk) // tq

    def body(q_ref, qt_ref, k_ref, kt_ref, v_ref, dy_ref, dyt_ref, dq_hbm, dk_ref, dv_ref, dck_ref,
             dq_acc, dk_acc, dv_acc, sem):
        g, j, i = pl.program_id(0), pl.program_id(1), pl.program_id(2)

        @pl.when((j == 0) & (i == 0))
        def _():
            dq_acc[...] = jnp.zeros_like(dq_acc)

        @pl.when(i == 0)
        def _():
            dk_acc[...] = jnp.zeros_like(dk_acc)
            dv_acc[...] = jnp.zeros_like(dv_acc)

        def tile(masked):
            if masked:
                kpos = j * tk + lax.broadcasted_iota(jnp.int32, (tk, tq), 0)
                qpos = i * tq + lax.broadcasted_iota(jnp.int32, (tk, tq), 1)
                mask = kpos <= qpos
            cols = pl.ds(pl.multiple_of(i * tq, tq), tq)
            for h in range(gh):
                sl = slice(128 * h, 128 * h + 128)
                sc = _dot(k_ref[:, sl], qt_ref[sl, :])
                if masked:
                    sc = jnp.where(mask, sc, NEG)
                p = jnp.exp(sc)
                ds = (p * _dot(v_ref[:, sl], dyt_ref[sl, :])).astype(BF16)
                dv_acc[:, sl] += _dot(p.astype(BF16), dy_ref[:, sl])
                dk_acc[:, sl] += _dot(ds, q_ref[:, sl])
                dq_acc[sl, cols] += _dot(kt_ref[sl, :], ds)

        full = j * tk + tk - 1 <= i * tq

        @pl.when(full)
        def _():
            tile(False)

        @pl.when(jnp.logical_and(jnp.logical_not(full), i >= first_q(j)))
        def _():
            tile(True)

        @pl.when(i == nq - 1)
        def _():
            dkw = dk_acc[...]
            dk_ref[...] = _compact(dkw)
            dv_ref[...] = _compact(dv_acc[...]).astype(BF16)
            dck_ref[...] = -_lane_of_heads(dkw, COL_B, gh * g)

        @pl.when((j == nk - 1) & (i == nq - 1))
        def _():
            cp = pltpu.make_async_copy(dq_acc, dq_hbm.at[pl.ds(pl.multiple_of(g * gw, gw), gw)], sem)
            cp.start()
            cp.wait()

    qrow = pl.BlockSpec((tq, gw), lambda g, j, i: (jnp.maximum(i, first_q(j)), g))
    qcol = pl.BlockSpec((gw, tq), lambda g, j, i: (g, jnp.maximum(i, first_q(j))))
    krow = pl.BlockSpec((tk, gw), lambda g, j, i: (j, g))
    kcol = pl.BlockSpec((gw, tk), lambda g, j, i: (g, j))
    kout = pl.BlockSpec((tk, gw // 2), lambda g, j, i: (j, g))
    return pl.pallas_call(
        body, grid=(ng, nk, nq),
        in_specs=[qrow, qcol, krow, kcol, krow, qrow, qcol],
        out_specs=[pl.BlockSpec(memory_space=pl.ANY), kout, kout,
                   pl.BlockSpec((None, tk, LANE), lambda g, j, i: (g, j, 0))],
        out_shape=[SDS((AUG, s), F32), SDS((s, 768), F32), SDS((s, 768), BF16), SDS((ng, s, LANE), F32)],
        scratch_shapes=[pltpu.VMEM((gw, s), F32), pltpu.VMEM((tk, gw), F32), pltpu.VMEM((tk, gw), F32),
                        pltpu.SemaphoreType.DMA],
        name="fox_bwd", compiler_params=_params(("arbitrary",) * 3))(qb, qbt, ka, kat, va, dya, dyat)


def _fox2_bwd_post(proj, fbl, qg, kg, bfor, bd, dqa, dkn, dck, *, tr):
    s = proj.shape[0]
    nb = s // tr
    triu = jnp.asarray(np.triu(np.ones((tr, tr), np.float32)))
    rev = lambda i: nb - 1 - i

    def body(q_ref, k_ref, fb_ref, qg_ref, kg_ref, bf_ref, bd_ref, tri_ref, dqa_ref, dkn_ref, dck_ref,
             dq_ref, dk_ref, dfb_ref, dqg_ref, dkg_ref, dbf_ref, carry):
        @pl.when(pl.program_id(0) == 0)
        def _():
            carry[...] = jnp.zeros_like(carry)
            dqg_ref[...] = jnp.zeros_like(dqg_ref)
            dkg_ref[...] = jnp.zeros_like(dkg_ref)
            dbf_ref[...] = jnp.zeros_like(dbf_ref)

        bd_v = bd_ref[...]
        dqw = jnp.transpose(dqa_ref[...])
        _, vjp_q = jax.vjp(lambda q, g: _qn_fn(q, g, bd_v), q_ref[...], qg_ref[...])
        dq, dqg = vjp_q(_compact(dqw))
        _, vjp_k = jax.vjp(lambda k, g: _kn_fn(k, g, bd_v), k_ref[...], kg_ref[...])
        dk, dkg = vjp_k(dkn_ref[...])
        dq_ref[...] = dq.astype(BF16)
        dk_ref[...] = dk.astype(BF16)
        dqg_ref[...] += dqg
        dkg_ref[...] += dkg

        dc = _lane_of_heads(dqw, COL_A) + (dck_ref[0] + dck_ref[1])
        dlogf = jnp.dot(tri_ref[...], dc, precision=HI, preferred_element_type=F32) + carry[...]
        carry[...] = dlogf[0:1, :]
        lane = lax.broadcasted_iota(jnp.int32, (tr, LANE), 1)
        xf = fb_ref[...] + bf_ref[...]
        dfb = jnp.where(lane < N_FORGET, dlogf * jax.nn.sigmoid(-xf), 0.0)
        dfb_ref[...] = dfb.astype(BF16)
        dbf_ref[...] += jnp.sum(dfb, axis=0, keepdims=True)

    rb = lambda w: pl.BlockSpec((tr, w), lambda i: (rev(i), 0))
    wn = lambda w, off: pl.BlockSpec((pl.Element(tr), pl.Element(w)), lambda i: (rev(i) * tr, off))
    return pl.pallas_call(
        body, grid=(nb,),
        in_specs=[wn(768, QB), wn(768, KB), rb(LANE), _const((1, 768)), _const((1, 768)), _const((1, LANE)),
                  _const((768, 768)), _const((tr, tr)), pl.BlockSpec((AUG, tr), lambda i: (0, rev(i))), rb(768),
                  pl.BlockSpec((2, tr, LANE), lambda i: (0, rev(i), 0))],
        out_specs=[rb(768), rb(768), rb(LANE), _const((1, 768)), _const((1, 768)), _const((1, LANE))],
        out_shape=[SDS((s, 768), BF16), SDS((s, 768), BF16), SDS((s, LANE), BF16), SDS((1, 768), F32),
                   SDS((1, 768), F32), SDS((1, LANE), F32)],
        scratch_shapes=[pltpu.VMEM((1, LANE), F32)], name="fox_bwd_post",
        compiler_params=_params(("arbitrary",)))(proj, proj, fbl, qg, kg, bfor, bd, triu, dqa, dkn, dck)


def _merge_specs(tr):
    row = lambda w: pl.BlockSpec((tr, w), lambda i, j: (i, 0))
    shard = lambda r: pl.BlockSpec((None, r, 512), lambda i, j: (j, 0, 0))
    gate = lambda b: pl.BlockSpec((tr, 512), lambda i, j: (i, (GATE + 2048 * b) // 512 + j))
    return [row(768), row(768), row(512), shard(768), shard(768), shard(512), gate(0), gate(1), gate(2)]


def _merge_fwd(proj, ga, gb, gc, wa, wb, wc, *, tr):
    s = proj.shape[0]

    def body(ga_ref, gb_ref, gc_ref, wa_ref, wb_ref, wc_ref, l0_ref, l1_ref, l2_ref, y_ref):
        ua = _dot(ga_ref[...], wa_ref[...])
        ub = _dot(gb_ref[...], wb_ref[...])
        uc = _dot(gc_ref[...], wc_ref[...])
        y = jax.nn.sigmoid(l0_ref[...]) * ua + jax.nn.sigmoid(l1_ref[...]) * ub + jax.nn.sigmoid(l2_ref[...]) * uc
        y_ref[...] = y.astype(BF16)

    return pl.pallas_call(
        body, grid=(s // tr, N_CHIPS), in_specs=_merge_specs(tr),
        out_specs=pl.BlockSpec((tr, 512), lambda i, j: (i, j)), out_shape=SDS((s, D_MODEL), BF16), name="merge_fwd",
        compiler_params=_params(("parallel", "arbitrary")))(ga, gb, gc, wa, wb, wc, proj, proj, proj)


def _merge_bwd(proj, ga, gb, gc, wa, wb, wc, dy, *, tr):
    s = proj.shape[0]

    def body(ga_ref, gb_ref, gc_ref, wa_ref, wb_ref, wc_ref, l0_ref, l1_ref, l2_ref, dy_ref,
             dl0_ref, dl1_ref, dl2_ref, dua_ref, dub_ref, duc_ref, dga_ref, dgb_ref, dgc_ref):
        j = pl.program_id(1)
        dyv = dy_ref[...]

        @pl.when(j == 0)
        def _():
            dga_ref[...] = jnp.zeros_like(dga_ref)
            dgb_ref[...] = jnp.zeros_like(dgb_ref)
            dgc_ref[...] = jnp.zeros_like(dgc_ref)

        for g_ref, w_ref, l_ref, dl_ref, du_ref, dg_ref in (
                (ga_ref, wa_ref, l0_ref, dl0_ref, dua_ref, dga_ref),
                (gb_ref, wb_ref, l1_ref, dl1_ref, dub_ref, dgb_ref),
                (gc_ref, wc_ref, l2_ref, dl2_ref, duc_ref, dgc_ref)):
            w = w_ref[...]
            u = _dot(g_ref[...], w)
            sg = jax.nn.sigmoid(l_ref[...])
            dl_ref[...] = (dyv * u * sg * (1.0 - sg)).astype(BF16)
            du = (dyv * sg).astype(BF16)
            du_ref[...] = du
            dg_ref[...] += _dot_nt(du, w)

    blk = pl.BlockSpec((tr, 512), lambda i, j: (i, j))
    row = lambda w: pl.BlockSpec((tr, w), lambda i, j: (i, 0))
    big = SDS((s, D_MODEL), BF16)
    return pl.pallas_call(
        body, grid=(s // tr, N_CHIPS), in_specs=_merge_specs(tr) + [blk],
        out_specs=[blk] * 6 + [row(768), row(768), row(512)],
        out_shape=[big] * 6 + [SDS((s, 768), F32), SDS((s, 768), F32), SDS((s, 512), F32)], name="merge_bwd",
        compiler_params=_params(("parallel", "arbitrary")))(ga, gb, gc, wa, wb, wc, proj, proj, proj, dy)


def _out_loss(y, wo, x, tgt, *, tr, tn):
    s = x.shape[0]

    def body(y_ref, w_ref, x_ref, t_ref, d_ref, db_ref, sq_ref, dy_ref):
        @pl.when((pl.program_id(0) == 0) & (pl.program_id(1) == 0))
        def _():
            sq_ref[...] = jnp.zeros_like(sq_ref)

        @pl.when(pl.program_id(1) == 0)
        def _():
            dy_ref[...] = jnp.zeros_like(dy_ref)

        w = w_ref[...]
        out = x_ref[...] + _dot(y_ref[...], w)
        diff = out - t_ref[...]
        sq_ref[...] += jnp.sum(diff * diff, axis=0, keepdims=True)
        d = diff * (1.0 / D_MODEL)
        d_ref[...] = d
        db = d.astype(BF16)
        db_ref[...] = db
        dy_ref[...] += _dot_nt(db, w)

    blk = pl.BlockSpec((tr, tn), lambda i, j: (i, j))
    row = pl.BlockSpec((tr, D_MODEL), lambda i, j: (i, 0))
    return pl.pallas_call(
        body, grid=(s // tr, D_MODEL // tn),
        in_specs=[row, pl.BlockSpec((D_MODEL, tn), lambda i, j: (0, j)), blk, blk],
        out_specs=[blk, blk, _const((1, tn)), row],
        out_shape=[SDS((s, D_MODEL), F32), SDS((s, D_MODEL), BF16), SDS((1, tn), F32), SDS((s, D_MODEL), F32)],
        name="out_loss", compiler_params=_params(("arbitrary", "arbitrary")))(y, wo, x, tgt)


def _tile_gain(g, reps):
    return jnp.tile(g.reshape(1, -1), (1, reps))


def _pad_lane(v):
    v = v.reshape(1, -1)
    return jnp.pad(v, ((0, 0), (0, LANE - v.shape[1])))


def _local_step(x, mem, tgt, w_main, w_fb, w_small, norm_gain, mem_norm_gain, b_forget,
                q_gain_a, k_gain_a, sinks_a, q_gain_b, k_gain_b, q_gain_c, k_gain_c, core=None):
    s = x.shape[0]
    tr = min(512, s)
    bd64 = _block_diag(768, HEAD_DIM)
    bd128 = _block_diag(512, C_HEAD_DIM)
    hsum = _head_sum(768, HEAD_DIM)
    qga, kga = _tile_gain(q_gain_a, 12), _tile_gain(k_gain_a, 4)
    qgb, kgb = _tile_gain(q_gain_b, 12), _tile_gain(k_gain_b, 12)
    qgc, kgc = _tile_gain(q_gain_c, 4), _tile_gain(k_gain_c, 4)
    sinks = _pad_lane(sinks_a)
    bfor = _pad_lane(b_forget)

    hn = _rms_fwd(x, norm_gain, tr=tr, name="rms_x")
    on_mesh = core is not None
    if on_mesh:
        proj, (gathered,) = _matmul(hn, w_main, dims="nn", out_dtype=F32, tm=1024, tn=1024, tk=D_MODEL, name="proj_main",
                                    comms=[_gather_comm(list(w_small))])
        w_mk, wa, wb, wc, wo = gathered
        w_mk, wo = w_mk.reshape(D_MODEL, 1024), wo.reshape(D_MODEL, D_MODEL)
    else:
        proj = _matmul(hn, w_main, dims="nn", out_dtype=F32, tm=1024, tn=1024, tk=D_MODEL, name="proj_main")
        w_mk, wa, wb, wc, wo = w_small
    fbl = _matmul(hn, w_fb, dims="nn", out_dtype=F32, tm=1024, tn=LANE, tk=D_MODEL, name="proj_forget")
    memn = _rms_fwd(mem, mem_norm_gain, tr=mem.shape[0], name="rms_mem")
    mkv = _matmul(memn, w_mk, dims="nn", out_dtype=F32, tm=256, tn=512, tk=D_MODEL, name="mem_kv")

    swa_bias = _swa_bias()
    ga = _swa_fwd(proj, qga, kga, sinks, bd64, swa_bias)
    ea, eb, ones_a, ones_b = _expand_mats()
    tf = min(256, s)
    tf_wide = min(512, s)
    qat, ka, kat, va, vat, qn, cfox = _fox2_prep(proj, fbl, qgb, kgb, bfor, bd64, ea, eb, ones_a, ones_b, tr=tf_wide)
    gb, yb, lse = _fox2_fwd(proj, qat, ka, vat)
    gc = _mem_fwd(proj, mkv, qgc, kgc, bd128, tr=tr)
    y = _merge_fwd(proj, ga, gb, gc, wa, wb, wc, tr=tr)
    dout, dout_b, sq, dy = _out_loss(y, wo, x, tgt, tr=tr, tn=1024)

    d_wo = _matmul(y, dout_b, dims="tn", out_dtype=F32, tm=1024, tn=512, tk=4096, name="dw_out")
    dl0, dl1, dl2, dua, dub, duc, dga, dgb, dgc = _merge_bwd(proj, ga, gb, gc, wa, wb, wc, dy, tr=tr)
    d_wa = _matmul(ga, dua, dims="tn", out_dtype=F32, tm=768, tn=512, tk=4096, name="dw_branch_a")
    d_wb = _matmul(gb, dub, dims="tn", out_dtype=F32, tm=768, tn=512, tk=4096, name="dw_branch_b")
    d_wc = _matmul(gc, duc, dims="tn", out_dtype=F32, tm=512, tn=512, tk=4096, name="dw_branch_c")

    dproj_a, d_qga, d_kga, d_sinks = _swa_bwd(proj, qga, kga, sinks, bd64, swa_bias, dga)

    qab, qabt, dya, dyat, dzb = _fox2_bwd_pre(proj, yb, dgb, qn, cfox, lse, hsum, ea, ones_b, tr=tf_wide)
    dqa, dkn, dvb, dck = _fox2_bwd(qab, qabt, ka, kat, va, dya, dyat)
    dqb, dkb, dfb, d_qgb, d_kgb, d_bf = _fox2_bwd_post(proj, fbl, qgb, kgb, bfor, bd64, dqa, dkn, dck, tr=tf)

    dproj_c, dmkv, d_qgc, d_kgc = _mem_bwd(proj, mkv, qgc, kgc, bd128, dgc, tr=tr)
    dmkv_b = dmkv.astype(BF16)
    d_wmk = _matmul(memn, dmkv_b, dims="tn", out_dtype=F32, tm=1024, tn=512, tk=256, name="dw_mem_kv")
    dmemn = _matmul(dmkv_b, w_mk, dims="nt", out_dtype=F32, tm=256, tn=512, tk=1024, name="dmemn")
    (d_mem_gain,) = _rms_bwd(mem, mem_norm_gain, dmemn, None, tr=mem.shape[0], name="rms_mem_bwd")

    dproj = [dproj_a, jnp.concatenate([dqb, dkb, dvb, dzb, dproj_c], axis=1), dl0, dl1, dl2]
    dhn_f = _matmul(dfb, w_fb, dims="nt", out_dtype=F32, tm=1024, tn=512, tk=LANE, name="dhn_forget")
    d_wfb = _matmul(hn, dfb, dims="tn", out_dtype=F32, tm=1024, tn=LANE, tk=512, name="dw_forget")
    big = {}
    if on_mesh:
        g1, k1 = [d_wmk, d_wa, d_wb, d_wc, d_wo], [2, 3, 4, 5, 6]
        d_other, (got1,) = _matmul(hn, dproj, dims="tn", out_dtype=BF16, tm=1024, tn=512, tk=4096, a_half=(core, True),
                                   name="dw_main_other", comms=[_exchange_comm(g1, k1)])
        h1 = _add_half(g1, got1, core, [HALF_AXIS[k] for k in k1], name="add_half_small")
        d_own, (got0, parts1) = _matmul(
            hn, dproj, dims="tn", out_dtype=F32, tm=1024, tn=512, tk=4096, a_half=(core, False), name="dw_main_own",
            comms=[_exchange_comm([d_other, d_wfb], [0, 1], whole=(0,)), _scatter_comm(h1, k1)])
        h0 = [_add_pair(d_own, got0[0], name="add_pair_main")] + _add_half([d_wfb], [got0[1]], core, [0], name="add_half_forget")
        sums1 = _sum4(parts1, name="sum4_small")
        dhn, (parts0, theirs1) = _matmul(dproj, w_main, dims="nt", out_dtype=F32, tm=1024, tn=512, tk=2048, vmem=VMEM_WIDE, name="dhn",
                                         add=dhn_f, comms=[_scatter_comm(h0, [0, 1]), _swap_comm(sums1)])
        sums0 = _sum4(parts0, name="sum4_main")
        (grad_x, d_gain), theirs0 = _rms_bwd(x, norm_gain, dhn, dout, tr=tr, name="rms_x_bwd", comm=_swap_comm(sums0))
        big = dict(sums=sums0 + sums1, theirs=list(theirs0) + list(theirs1))
    else:
        dhn = _matmul(dproj, w_main, dims="nt", out_dtype=F32, tm=1024, tn=512, tk=2048, vmem=VMEM_WIDE, name="dhn", add=dhn_f)
        d_wmain = _matmul(hn, dproj, dims="tn", out_dtype=F32, tm=1024, tn=512, tk=4096, name="dw_main")
        big = dict(d_wmain=d_wmain, d_wfb=d_wfb, d_wmk=d_wmk, d_wa=d_wa, d_wb=d_wb, d_wc=d_wc, d_wo=d_wo)
        grad_x, d_gain = _rms_bwd(x, norm_gain, dhn, dout, tr=tr, name="rms_x_bwd")

    fold = lambda g, reps: jnp.sum(g.reshape(reps, -1), axis=0, keepdims=True)
    return dict(
        sq=sq, grad_x=grad_x, **big,
        d_gain=d_gain, d_mem_gain=d_mem_gain, d_bf=d_bf[:, :N_FORGET],
        d_qga=fold(d_qga, 12), d_kga=fold(d_kga, 4), d_sinks=d_sinks[:, :A_HEADS],
        d_qgb=fold(d_qgb, 12), d_kgb=fold(d_kgb, 12), d_qgc=fold(d_qgc, 4), d_kgc=fold(d_kgc, 4))


PACK_ROWS = 256
FORGET_IN_SHARD = FORGET_COL - SHARD_COLS
AFTER_FORGET = FORGET_COL - SLAB_START[1]
END_CHIP1 = 2 * SHARD_COLS - N_FORGET - SLAB_START[1]


def _pack_w_in(chip, w):
    rows = w.shape[1]
    tr = PACK_ROWS

    def body(k_ref, w_ref, o_ref, scr):
        scr[...] = jnp.zeros_like(scr)
        scr[pl.ds(0, SHARD_COLS), :] = w_ref[...]
        v = jnp.transpose(scr[...])
        k = k_ref[0]
        col = lax.broadcasted_iota(jnp.int32, (tr, SLAB), 1)
        no_forget = jnp.zeros((tr, LANE), BF16)

        @pl.when(k == 0)
        def _():
            o_ref[:, 0:SLAB] = v.astype(BF16)
            o_ref[:, SLAB:] = no_forget

        @pl.when(k == 1)
        def _():
            before = pltpu.roll(v, SLAB_SHIFT[1], axis=1)
            after = pltpu.roll(v, SLAB - (N_FORGET - SLAB_SHIFT[1]), axis=1)
            slab = jnp.where(col < AFTER_FORGET, before, jnp.where(col < END_CHIP1, after, 0.0))
            o_ref[:, 0:SLAB] = slab.astype(BF16)
            f = pltpu.roll(v, SLAB - FORGET_IN_SHARD, axis=1)[:, :LANE]
            o_ref[:, SLAB:] = jnp.where(col[:, :LANE] < N_FORGET, f, 0.0).astype(BF16)

        for kk in (2, 3):
            @pl.when(k == kk)
            def _(kk=kk):
                o_ref[:, 0:SLAB] = pltpu.roll(v, SLAB_SHIFT[kk], axis=1).astype(BF16)
                o_ref[:, SLAB:] = no_forget

    return pl.pallas_call(
        body, grid_spec=pltpu.PrefetchScalarGridSpec(
            num_scalar_prefetch=1, grid=(rows // tr,),
            in_specs=[pl.BlockSpec((SHARD_COLS, tr), lambda i, k: (0, i))],
            out_specs=pl.BlockSpec((None, tr, SLAB + LANE), lambda i, k: (k[0], i, 0)),
            scratch_shapes=[pltpu.VMEM((SLAB, tr), F32)]),
        out_shape=SDS((N_CHIPS, rows, SLAB + LANE), BF16), name="pack_w_in",
        compiler_params=_params(("arbitrary",)))(chip, w)


def _merge_slabs(g):
    rows = g.shape[1]
    tr = PACK_ROWS
    t = [s // LANE for s in SLAB_START]
    n_t = SLAB // LANE

    def body(g_ref, m_ref, f_ref):
        for k in range(N_CHIPS):
            lo = t[k] + (1 if k > 0 else 0)
            hi = t[k + 1] if k + 1 < N_CHIPS else t[k] + n_t
            m_ref[:, lo * LANE:hi * LANE] = g_ref[k, :, (lo - t[k]) * LANE:(hi - t[k]) * LANE]
            if k + 1 < N_CHIPS:
                a = g_ref[k, :, (hi - t[k]) * LANE:(hi - t[k] + 1) * LANE].astype(F32)
                b = g_ref[k + 1, :, 0:LANE].astype(F32)
                m_ref[:, hi * LANE:(hi + 1) * LANE] = (a + b).astype(BF16)
        f_ref[...] = g_ref[1, :, SLAB:]

    return pl.pallas_call(
        body, grid=(rows // tr,),
        in_specs=[pl.BlockSpec((N_CHIPS, tr, SLAB + LANE), lambda i: (0, i, 0))],
        out_specs=[_rowblk(tr, P_MAIN), _rowblk(tr, LANE)],
        out_shape=[SDS((rows, P_MAIN), BF16), SDS((rows, LANE), BF16)], name="merge_slabs",
        compiler_params=_params(("parallel",)))(g)


def _adamw_math(w, g, m, v):
    nm = ADAM_B1 * m + (1.0 - ADAM_B1) * g
    nv = ADAM_B2 * v + (1.0 - ADAM_B2) * (g * g)
    m_hat = nm / (1.0 - ADAM_B1 ** ADAM_STEP)
    v_hat = nv / (1.0 - ADAM_B2 ** ADAM_STEP)
    delta = -ADAM_LR * (m_hat / (jnp.sqrt(v_hat) + ADAM_EPS) + ADAM_WD * w)
    return delta, nm, nv


def _adamw(g, w, m, v, *, tr, name):
    rows, cols = w.shape
    tr = min(tr, rows)

    def body(g_ref, w_ref, m_ref, v_ref, d_ref, nm_ref, nv_ref):
        d, nm, nv = _adamw_math(w_ref[...], g_ref[...], m_ref[...], v_ref[...])
        d_ref[...] = d
        nm_ref[...] = nm
        nv_ref[...] = nv

    spec = _rowblk(tr, cols)
    return pl.pallas_call(
        body, grid=(rows // tr,), in_specs=[spec] * 4, out_specs=[spec] * 3,
        out_shape=[SDS((rows, cols), F32)] * 3, name=name, compiler_params=_params(("parallel",)))(g, w, m, v)


def _adamw_w_in(chip_core, slab_mine, slab_theirs, forget_mine, forget_theirs, w, m, v):
    rows = w.shape[1]
    tr = PACK_ROWS // 2
    nbh = rows // 2 // tr

    def body(k_ref, sa_ref, sb_ref, fa_ref, fb_ref, w_ref, m_ref, v_ref, g_ref, d_ref, nm_ref, nv_ref):
        use_mine = pl.program_id(0) // nbh == k_ref[1]
        sl = jnp.where(use_mine, sa_ref[...], sb_ref[...])
        f_tile = jnp.where(use_mine, fa_ref[...], fb_ref[...])
        k = k_ref[0]

        def emit(wide):
            g = jnp.transpose(wide)[:SHARD_COLS, :]
            g_ref[...] = g
            d, nm, nv = _adamw_math(w_ref[...], g, m_ref[...], v_ref[...])
            d_ref[...] = d
            nm_ref[...] = nm
            nv_ref[...] = nv

        @pl.when(k == 0)
        def _():
            emit(sl)

        @pl.when(k == 1)
        def _():
            col = lax.broadcasted_iota(jnp.int32, (tr, SLAB), 1)
            before = pltpu.roll(sl, SLAB - SLAB_SHIFT[1], axis=1)
            after = pltpu.roll(sl, N_FORGET - SLAB_SHIFT[1], axis=1)
            wide_f = jnp.concatenate([f_tile, jnp.zeros((tr, SLAB - LANE), F32)], axis=1)
            forget = pltpu.roll(wide_f, FORGET_IN_SHARD, axis=1)
            emit(jnp.where(col < FORGET_IN_SHARD, before, jnp.where(col < FORGET_IN_SHARD + N_FORGET, forget, after)))

        for kk in (2, 3):
            @pl.when(k == kk)
            def _(kk=kk):
                emit(pltpu.roll(sl, SLAB - SLAB_SHIFT[kk], axis=1))

    nat = pl.BlockSpec((SHARD_COLS, tr), lambda i, k: (0, i))
    half = lambda width: pl.BlockSpec((tr, width), lambda i, k: (i % nbh, 0))
    return pl.pallas_call(
        body, grid_spec=pltpu.PrefetchScalarGridSpec(
            num_scalar_prefetch=1, grid=(rows // tr,),
            in_specs=[half(SLAB), half(SLAB), half(LANE), half(LANE), nat, nat, nat],
            out_specs=[nat] * 4),
        out_shape=[SDS((SHARD_COLS, rows), F32)] * 4, name="adamw_w_in",
        compiler_params=_params(("arbitrary",)))(chip_core, slab_mine, slab_theirs, forget_mine, forget_theirs, w, m, v)


ANY = pl.BlockSpec(memory_space=pl.ANY)
HALF_AXIS = (0, 0, 1, 0, 0, 0, 1)


def _me():
    return lax.axis_index("x"), lax.axis_index("y"), lax.axis_index("c")


def _half(ref, which, axis):
    n = ref.shape[axis] // 2
    sl = pl.ds(which * n, n)
    return ref.at[sl] if axis == 0 else ref.at[:, sl]


def _piece(t, ref, j):
    if t == 0:
        return ref.at[:, pl.ds(SLAB_START[j], SLAB)]
    if t == 1:
        return ref
    if t in (2, 6):
        return ref.at[pl.ds(512 * j, 512)]
    return ref.at[:, pl.ds(512 * j, 512)]


def _piece_shape(t, shape):
    if t == 0:
        return (shape[0], SLAB)
    if t == 1:
        return shape
    if t in (2, 6):
        return (512, shape[1])
    return (shape[0], 512)


def _gather_plan(ins, outs, own_slot_in_src):
    x, y, c = _me()
    k = 2 * x + y
    sib = (x, y, 1 - c)
    chips = [(1 - x, y), (x, 1 - y), (1 - x, 1 - y)]
    n = len(outs)

    def rows(t, which):
        h = outs[t].shape[1] // 2
        return pl.ds(which * h, h)

    def mine(t):
        return ins[t].at[k, rows(t, c)] if own_slot_in_src else ins[t].at[rows(t, c)]

    def first(t, j, sems):
        chip = chips[j]
        return pltpu.make_async_remote_copy(
            src_ref=mine(t), dst_ref=outs[t].at[k, rows(t, c)], send_sem=sems[0].at[t, j], recv_sem=sems[1].at[t, j],
            device_id=(chip[0], chip[1], c), device_id_type=MESH)

    def landed(t, j, sems):
        chip = chips[j]
        return pltpu.make_async_remote_copy(
            src_ref=mine(t), dst_ref=outs[t].at[2 * chip[0] + chip[1], rows(t, c)], send_sem=sems[0].at[t, j],
            recv_sem=sems[1].at[t, j], device_id=(chip[0], chip[1], c), device_id_type=MESH)

    def passed(t, j, which, sems):
        chip = chips[j]
        blk = outs[t].at[2 * chip[0] + chip[1], rows(t, which)]
        return pltpu.make_async_remote_copy(
            src_ref=blk, dst_ref=blk, send_sem=sems[2].at[t, j], recv_sem=sems[3].at[t, j], device_id=sib,
            device_id_type=MESH)

    def start(sems):
        for j in range(3):
            for t in range(n):
                first(t, j, sems).start()

    def finish(sems):
        for j in range(3):
            for t in range(n):
                landed(t, j, sems).wait_recv()
                passed(t, j, c, sems).start()
        for j in range(3):
            for t in range(n):
                passed(t, j, 1 - c, sems).wait_recv()
        for j in range(3):
            for t in range(n):
                first(t, j, sems).wait_send()
                passed(t, j, c, sems).wait_send()

    return k, start, finish


def _all_gather_slabs(slabs):
    def body(in_ref, out_ref, nbr_sem, quarter_sem, pass_sem):
        x, y, c = _me()
        k = 2 * x + y
        rows = out_ref.shape[1]
        h, q = rows // 2, rows // 4
        nbrs = [(1 - x, y), (x, 1 - y)]
        slot = lambda chip: 2 * chip[0] + chip[1]
        diag = 2 * (1 - x) + (1 - y)
        half = pl.ds(c * h, h)
        quarter = lambda a: pl.ds(c * h + a * q, q)

        def first(a):
            return pltpu.make_async_remote_copy(
                src_ref=in_ref.at[k, half], dst_ref=out_ref.at[k, half], send_sem=nbr_sem.at[0, a],
                recv_sem=nbr_sem.at[1, a], device_id=(nbrs[a][0], nbrs[a][1], c), device_id_type=MESH)

        def landed(a):
            blk = out_ref.at[slot(nbrs[a]), half]
            return pltpu.make_async_remote_copy(
                src_ref=blk, dst_ref=blk, send_sem=nbr_sem.at[0, a], recv_sem=nbr_sem.at[1, a],
                device_id=(nbrs[a][0], nbrs[a][1], c), device_id_type=MESH)

        def relay(a):
            blk = out_ref.at[slot(nbrs[a]), quarter(a)]
            to = nbrs[1 - a]
            return pltpu.make_async_remote_copy(
                src_ref=blk, dst_ref=blk, send_sem=quarter_sem.at[0, a], recv_sem=quarter_sem.at[1, a],
                device_id=(to[0], to[1], c), device_id_type=MESH)

        def relayed(a):
            blk = out_ref.at[diag, quarter(a)]
            frm = nbrs[1 - a]
            return pltpu.make_async_remote_copy(
                src_ref=blk, dst_ref=blk, send_sem=quarter_sem.at[0, a], recv_sem=quarter_sem.at[1, a],
                device_id=(frm[0], frm[1], c), device_id_type=MESH)

        def passed(j, which):
            sl = diag if j == 2 else slot(nbrs[j])
            blk = out_ref.at[sl, pl.ds(which * h, h)]
            return pltpu.make_async_remote_copy(
                src_ref=blk, dst_ref=blk, send_sem=pass_sem.at[0, j], recv_sem=pass_sem.at[1, j],
                device_id=(x, y, 1 - c), device_id_type=MESH)

        for a in range(2):
            first(a).start()
        for a in range(2):
            landed(a).wait_recv()
            relay(a).start()
            passed(a, c).start()
        for a in range(2):
            relayed(a).wait_recv()
        passed(2, c).start()
        for j in range(3):
            passed(j, 1 - c).wait_recv()
        for a in range(2):
            first(a).wait_send()
            relay(a).wait_send()
        for j in range(3):
            passed(j, c).wait_send()

    return pl.pallas_call(
        body, in_specs=[ANY], out_specs=ANY, out_shape=SDS(slabs.shape, slabs.dtype),
        scratch_shapes=[pltpu.SemaphoreType.DMA((2, 2)), pltpu.SemaphoreType.DMA((2, 2)), pltpu.SemaphoreType.DMA((2, 3))],
        input_output_aliases={0: 0}, name="all_gather_slabs")(slabs)


def _gather_comm(parts):
    n = len(parts)

    def start(ins, outs, sems):
        k, go, _ = _gather_plan(ins, outs, False)
        for t in range(n):
            pltpu.make_async_copy(ins[t], outs[t].at[k], sems[4].at[t]).start()
        go(sems)

    def finish(ins, outs, sems):
        k, _, done = _gather_plan(ins, outs, False)
        done(sems)
        for t in range(n):
            pltpu.make_async_copy(ins[t], outs[t].at[k], sems[4].at[t]).wait()

    return _Comm(parts, [SDS((N_CHIPS,) + p.shape, p.dtype) for p in parts],
                 [pltpu.SemaphoreType.DMA((n, 3))] * 4 + [pltpu.SemaphoreType.DMA((n,))], start, finish)


def _exchange_comm(arrs, kinds, whole=()):
    n = len(arrs)

    def copies(ins, outs, sems):
        x, y, c = _me()
        return [pltpu.make_async_remote_copy(
            src_ref=ins[t] if t in whole else _half(ins[t], 1 - c, HALF_AXIS[kinds[t]]), dst_ref=outs[t],
            send_sem=sems[0].at[t], recv_sem=sems[1].at[t], device_id=(x, y, 1 - c), device_id_type=MESH)
            for t in range(n)]

    def start(ins, outs, sems):
        for cp in copies(ins, outs, sems):
            cp.start()

    def finish(ins, outs, sems):
        for cp in copies(ins, outs, sems):
            cp.wait()

    def hshape(t):
        s = list(arrs[t].shape)
        if t not in whole:
            s[HALF_AXIS[kinds[t]]] //= 2
        return SDS(tuple(s), arrs[t].dtype)

    return _Comm(arrs, [hshape(t) for t in range(n)], [pltpu.SemaphoreType.DMA((n,))] * 2, start, finish)


ADD_HALF_STEPS = 2


def _add_half(fulls, gots, core, axes, *, name):
    ns = ADD_HALF_STEPS
    n = len(fulls)
    in_specs, out_specs, out_shape = [], [], []
    for got, axis in zip(gots, axes):
        r, c = got.shape
        blk = (r // ns, c)
        if axis == 0:
            in_specs.append(pl.BlockSpec(blk, lambda i, cr: (i + cr[0] * ns, 0)))
        else:
            in_specs.append(pl.BlockSpec(blk, lambda i, cr: (i, cr[0])))
        out_specs.append(pl.BlockSpec(blk, lambda i, cr: (i, 0)))
        out_shape.append(SDS((r, c), BF16))
    in_specs += list(out_specs)

    def body(c_ref, *refs):
        for a_ref, b_ref, o_ref in zip(refs[:n], refs[n:2 * n], refs[2 * n:]):
            o_ref[...] = (a_ref[...] + b_ref[...]).astype(BF16)

    return list(pl.pallas_call(
        body, grid_spec=pltpu.PrefetchScalarGridSpec(num_scalar_prefetch=1, grid=(ns,), in_specs=in_specs,
                                                     out_specs=out_specs),
        out_shape=out_shape, name=name, compiler_params=_params(("parallel",)))(core, *fulls, *gots))


def _add_pair(a, b, *, name):
    r, c = a.shape
    br, bc = 256, min(2048, c)

    def body(a_ref, b_ref, o_ref):
        o_ref[...] = (a_ref[...] + b_ref[...].astype(F32)).astype(BF16)

    spec = pl.BlockSpec((br, bc), lambda i, j: (i, j))
    return pl.pallas_call(body, grid=(r // br, c // bc), in_specs=[spec, spec], out_specs=spec,
                          out_shape=SDS((r, c), BF16), name=name, compiler_params=_params(("parallel", "parallel")))(a, b)


def _scatter_comm(halves, kinds):
    n = len(halves)

    def plan(ins, outs, sems):
        send, recv, lsem = sems
        x, y, c = _me()
        k = 2 * x + y

        def to_chip(t, j):
            return pltpu.make_async_remote_copy(
                src_ref=_piece(kinds[t], ins[t], j), dst_ref=outs[t].at[k], send_sem=send.at[t, j],
                recv_sem=recv.at[t, k], device_id=(j // 2, j % 2, c), device_id_type=MESH)

        def from_chip(t, j):
            return pltpu.make_async_remote_copy(
                src_ref=_piece(kinds[t], ins[t], j), dst_ref=outs[t].at[j], send_sem=send.at[t, j],
                recv_sem=recv.at[t, j], device_id=(j // 2, j % 2, c), device_id_type=MESH)

        def own(t, j):
            return pltpu.make_async_copy(_piece(kinds[t], ins[t], j), outs[t].at[j], lsem.at[t])

        return k, to_chip, from_chip, own

    def start(ins, outs, sems):
        k, to_chip, _, own = plan(ins, outs, sems)
        for j in range(N_CHIPS):
            @pl.when(k != j)
            def _(j=j):
                for t in range(n):
                    to_chip(t, j).start()

            @pl.when(k == j)
            def _(j=j):
                for t in range(n):
                    own(t, j).start()

    def finish(ins, outs, sems):
        k, to_chip, from_chip, own = plan(ins, outs, sems)
        for j in range(N_CHIPS):
            @pl.when(k != j)
            def _(j=j):
                for t in range(n):
                    from_chip(t, j).wait_recv()
                for t in range(n):
                    to_chip(t, j).wait_send()

            @pl.when(k == j)
            def _(j=j):
                for t in range(n):
                    own(t, j).wait()

    return _Comm(halves, [SDS((N_CHIPS,) + _piece_shape(kinds[t], halves[t].shape), halves[t].dtype) for t in range(n)],
                 [pltpu.SemaphoreType.DMA((n, N_CHIPS))] * 2 + [pltpu.SemaphoreType.DMA((n,))], start, finish)


SUM4_STEPS = 2


def _sum4(ps, *, name):
    ns = SUM4_STEPS

    def body(*refs):
        for p_ref, o_ref in zip(refs[:len(ps)], refs[len(ps):]):
            o_ref[...] = ((p_ref[0].astype(F32) + p_ref[1].astype(F32)) + p_ref[2].astype(F32)) + p_ref[3].astype(F32)

    return pl.pallas_call(
        body, grid=(ns,),
        in_specs=[pl.BlockSpec((N_CHIPS, p.shape[1] // ns, p.shape[2]), lambda i: (0, i, 0)) for p in ps],
        out_specs=[_rowblk(p.shape[1] // ns, p.shape[2]) for p in ps],
        out_shape=[SDS(p.shape[1:], F32) for p in ps], name=name, compiler_params=_params(("parallel",)))(*ps)


def _swap_comm(sums):
    return _exchange_comm(sums, [None] * len(sums), whole=tuple(range(len(sums))))


ADAMW_STEPS = 4


def _adamw_halves(items, core):
    ns = ADAMW_STEPS
    nbh = ns // 2
    n = len(items)
    in_specs, out_specs, out_shape, ins = [], [], [], []
    for mine, theirs, w, m, v, axis in items:
        rows, cols = w.shape
        tr = rows // ns
        if axis == 0:
            g_spec = pl.BlockSpec((tr, cols), lambda i, cr: (i % nbh, 0))
        else:
            g_spec = pl.BlockSpec((tr, cols // 2), lambda i, cr: (i, 0))
        nat = pl.BlockSpec((tr, cols), lambda i, cr: (i, 0))
        in_specs += [g_spec, g_spec, nat, nat, nat]
        out_specs += [nat] * 4
        out_shape += [SDS((rows, cols), F32)] * 4
        ins += [mine, theirs, w, m, v]

    def body(c_ref, *refs):
        for t, item in enumerate(items):
            a_ref, b_ref, w_ref, m_ref, v_ref = refs[5 * t:5 * t + 5]
            g_ref, d_ref, nm_ref, nv_ref = refs[5 * n + 4 * t:5 * n + 4 * t + 4]
            a, b = a_ref[...], b_ref[...]
            if item[5] == 0:
                g = jnp.where(pl.program_id(0) // nbh == c_ref[0], a, b)
            else:
                low = c_ref[0] == 0
                g = jnp.concatenate([jnp.where(low, a, b), jnp.where(low, b, a)], axis=1)
            g_ref[...] = g
            d, nm, nv = _adamw_math(w_ref[...], g, m_ref[...], v_ref[...])
            d_ref[...] = d
            nm_ref[...] = nm
            nv_ref[...] = nv

    res = pl.pallas_call(
        body, grid_spec=pltpu.PrefetchScalarGridSpec(
            num_scalar_prefetch=1, grid=(ns,), in_specs=in_specs, out_specs=out_specs),
        out_shape=out_shape, name="adamw_shards", compiler_params=_params(("arbitrary",)))(core, *ins)
    return [tuple(res[4 * t:4 * t + 4]) for t in range(n)]


SMALL_ROWS, SMALL_COLS = 8, 1024


def _pack_small(vs):
    flat = jnp.concatenate([v.reshape(-1) for v in vs])
    return jnp.pad(flat, (0, SMALL_ROWS * SMALL_COLS - flat.shape[0])).reshape(SMALL_ROWS, SMALL_COLS)


def _unpack_small(packed, sizes):
    flat = packed.reshape(-1)
    out, o = [], 0
    for n in sizes:
        out.append(flat[o:o + n].reshape(1, n))
        o += n
    return out


def _all_reduce_small(v):
    n_dev = 8

    def body(v_ref, o_ref, land, send, recv):
        x, y, c = _me()
        me = 4 * x + 2 * y + c
        land[me] = v_ref[...]
        cps = []
        for r in range(1, n_dev):
            fx, fy, fc = (r >> 2) & 1, (r >> 1) & 1, r & 1
            peer = (x ^ fx, y ^ fy, c ^ fc)
            cps.append(pltpu.make_async_remote_copy(
                src_ref=v_ref, dst_ref=land.at[me], send_sem=send.at[r - 1], recv_sem=recv.at[r - 1],
                device_id=peer, device_id_type=MESH))
        for cp in cps:
            cp.start()
        for r in range(1, n_dev):
            fx, fy, fc = (r >> 2) & 1, (r >> 1) & 1, r & 1
            src = 4 * (x ^ fx) + 2 * (y ^ fy) + (c ^ fc)
            pltpu.make_async_remote_copy(
                src_ref=v_ref, dst_ref=land.at[src], send_sem=send.at[r - 1], recv_sem=recv.at[r - 1],
                device_id=(x ^ fx, y ^ fy, c ^ fc), device_id_type=MESH).wait_recv()
        for cp in cps:
            cp.wait_send()
        acc = land[0]
        for r in range(1, n_dev):
            acc = acc + land[r]
        o_ref[...] = acc

    vm = pl.BlockSpec(memory_space=pltpu.VMEM)
    return pl.pallas_call(
        body, in_specs=[vm], out_specs=vm, out_shape=SDS(v.shape, F32),
        scratch_shapes=[pltpu.VMEM((n_dev,) + v.shape, F32), pltpu.SemaphoreType.DMA((n_dev - 1,)),
                        pltpu.SemaphoreType.DMA((n_dev - 1,))],
        name="all_reduce_small")(v)


def kernel(x, mem, norm_gain, mem_norm_gain, w_in, b_forget, q_gain_a, k_gain_a, sinks_a, q_gain_b, k_gain_b, q_gain_c, k_gain_c, w_mem_kv, w_branch_a, w_branch_b, w_branch_c, w_out, loss_target, m_norm_gain, m_mem_norm_gain, m_w_in, m_b_forget, m_q_gain_a, m_k_gain_a, m_sinks_a, m_q_gain_b, m_k_gain_b, m_q_gain_c, m_k_gain_c, m_w_mem_kv, m_w_branch_a, m_w_branch_b, m_w_branch_c, m_w_out, v_norm_gain, v_mem_norm_gain, v_w_in, v_b_forget, v_q_gain_a, v_k_gain_a, v_sinks_a, v_q_gain_b, v_k_gain_b, v_q_gain_c, v_k_gain_c, v_w_mem_kv, v_w_branch_a, v_w_branch_b, v_w_branch_c, v_w_out):
    xi, yi, ci = lax.axis_index("x"), lax.axis_index("y"), lax.axis_index("c")
    chip = jnp.reshape(2 * xi + yi, (1,)).astype(jnp.int32)
    core = jnp.reshape(ci, (1,)).astype(jnp.int32)

    slabs = _pack_w_in(chip, jnp.transpose(w_in[0]))
    mine = [w_mem_kv[0].astype(BF16), w_branch_a[0].astype(BF16), w_branch_b[0].astype(BF16),
            w_branch_c[0].astype(BF16), w_out[0].astype(BF16)]
    w_main, w_fb = _merge_slabs(_all_gather_slabs(slabs))

    r = _local_step(x[0], mem[0], loss_target[0], w_main, w_fb, mine, norm_gain, mem_norm_gain,
                    b_forget, q_gain_a, k_gain_a, sinks_a, q_gain_b, k_gain_b, q_gain_c, k_gain_c, core=core)
    sums, theirs = r["sums"], r["theirs"]

    small_names = ["d_gain", "d_mem_gain", "d_bf", "d_qga", "d_kga", "d_sinks", "d_qgb", "d_kgb", "d_qgc", "d_kgc"]
    loss_part = (0.5 / D_MODEL) * jnp.sum(r["sq"], axis=1, keepdims=True)
    packed = _pack_small([r[n] for n in small_names] + [loss_part])
    red = _all_reduce_small(packed)
    small_w = [norm_gain, mem_norm_gain, b_forget, q_gain_a, k_gain_a, sinks_a, q_gain_b, k_gain_b, q_gain_c, k_gain_c]
    small_m = [m_norm_gain, m_mem_norm_gain, m_b_forget, m_q_gain_a, m_k_gain_a, m_sinks_a, m_q_gain_b, m_k_gain_b,
               m_q_gain_c, m_k_gain_c]
    small_v = [v_norm_gain, v_mem_norm_gain, v_b_forget, v_q_gain_a, v_k_gain_a, v_sinks_a, v_q_gain_b, v_k_gain_b,
               v_q_gain_c, v_k_gain_c]
    sizes = [w.shape[1] for w in small_w]
    s_d, s_m, s_v = _adamw(red, _pack_small(small_w), _pack_small(small_m), _pack_small(small_v), tr=8, name="adamw_small")
    g_small = _unpack_small(red, sizes + [1])
    loss = g_small[-1].reshape(())
    d_small, m_small, v_small = _unpack_small(s_d, sizes), _unpack_small(s_m, sizes), _unpack_small(s_v, sizes)

    gw_in, dw_in, mw_in, vw_in = _adamw_w_in(jnp.concatenate([chip, core]), sums[0], theirs[0], sums[1], theirs[1],
                                             jnp.transpose(w_in[0]), jnp.transpose(m_w_in[0]), jnp.transpose(v_w_in[0]))
    shards = ((2, "w_mem_kv", w_mem_kv, m_w_mem_kv, v_w_mem_kv),
              (3, "w_branch_a", w_branch_a, m_w_branch_a, v_w_branch_a),
              (4, "w_branch_b", w_branch_b, m_w_branch_b, v_w_branch_b),
              (5, "w_branch_c", w_branch_c, m_w_branch_c, v_w_branch_c),
              (6, "w_out", w_out, m_w_out, v_w_out))
    done = _adamw_halves([(sums[t], theirs[t], w[0], m[0], v[0], HALF_AXIS[t]) for t, _, w, m, v in shards], core)
    big = {nm: res for (_, nm, _, _, _), res in zip(shards, done)}

    def collect(kind):
        sm = (g_small, d_small, m_small, v_small)[kind]
        win = (gw_in, dw_in, mw_in, vw_in)[kind]
        return ([sm[0], sm[1], jnp.transpose(win)[None]] + [a for a in sm[2:10]]
                + [big[n][kind][None] for n in ("w_mem_kv", "w_branch_a", "w_branch_b", "w_branch_c", "w_out")])

    return (loss, r["grad_x"][None], *collect(0), *collect(1), *collect(2), *collect(3))
```

```python
import numpy as np
import jax
import jax.numpy as jnp
from jax import lax
from jax.experimental import pallas as pl
from jax.experimental.pallas import tpu as pltpu

F32 = jnp.float32
BF16 = jnp.bfloat16
HI = lax.Precision.HIGHEST
SDS = jax.ShapeDtypeStruct
MESH = pl.DeviceIdType.MESH

D_MODEL = 2048
HEAD_DIM = 64
A_HEADS = 12
A_GROUP = 3
B_HEADS = 12
C_HEADS = 4
C_HEAD_DIM = 128
WINDOW = 128
EPS = 1e-6
NEG = -1e30
LANE = 128

QA, KA, VA, ZA = 0, 768, 1024, 1280
QB, KB, VB, ZB = 2048, 2816, 3584, 4352
QC, ZC = 5120, 5632
GATE = 6144
P_MAIN = 12288
N_FORGET = 12
FORGET_COL = 5120
SHARD_COLS = 3075
SLAB = 3200
SLAB_START = (0, 3072, 6016, 9088)
SLAB_SHIFT = (0, 3, 122, 125)
N_CHIPS = 4

ADAM_LR = 0.001
ADAM_B1 = 0.9
ADAM_B2 = 0.999
ADAM_EPS = 1e-08
ADAM_WD = 0.01
ADAM_STEP = 10

VMEM_LIMIT = 56 * 1024 * 1024
VMEM_WIDE = 62 * 1024 * 1024


def _params(sem, vmem=VMEM_LIMIT):
    return pltpu.CompilerParams(dimension_semantics=sem, vmem_limit_bytes=vmem)


def _win(tr, width, off):
    return pl.BlockSpec((pl.Element(tr), pl.Element(width)), lambda i, *_: (i * tr, off))


def _rowblk(tr, width):
    return pl.BlockSpec((tr, width), lambda i, *_: (i, 0))


def _const(shape):
    nd = len(shape)
    return pl.BlockSpec(shape, lambda *_: (0,) * nd)


def _rms(x, g):
    return x * lax.rsqrt(jnp.mean(x * x, axis=-1, keepdims=True) + EPS) * g


def _head_mean_impl(x2, bd):
    hi = x2.astype(BF16)
    lo = (x2 - hi.astype(F32)).astype(BF16)
    return _dot(hi, bd) + _dot(lo, bd)


@jax.custom_vjp
def _head_mean(x2, bd):
    return _head_mean_impl(x2, bd)


_head_mean.defvjp(lambda x2, bd: (_head_mean_impl(x2, bd), bd),
                  lambda bd, g: (_head_mean_impl(g, bd), jnp.zeros_like(bd)))


def _head_norm(x, g_tiled, bd):
    return x * lax.rsqrt(_head_mean(x * x, bd) + EPS) * g_tiled


def _silu(z):
    return z * jax.nn.sigmoid(z)


def _dot_nt(a, b):
    return lax.dot_general(a, b, (((1,), (1,)), ((), ())), preferred_element_type=F32)


def _dot_tn(a, b):
    return lax.dot_general(a, b, (((0,), (0,)), ((), ())), preferred_element_type=F32)


def _dot(a, b):
    return jnp.dot(a, b, preferred_element_type=F32)


def _swa_fn(qk, vz, qkp, vzp, qg, kg, sinks, bd, bias, first):
    q = _head_norm(qk[:, :768], qg, bd)
    k2 = jnp.concatenate([qkp[:, 768:], qk[:, 768:]], axis=0)
    k2 = _head_norm(k2, kg, bd[:256, :256])
    v2 = jnp.concatenate([vzp[:, :256], vz[:, :256]], axis=0)
    z = vz[:, 256:]
    cols = A_GROUP * WINDOW
    kj = lax.broadcasted_iota(jnp.int32, (2 * WINDOW, cols), 0)
    no_prev = kj < WINDOW * first.astype(jnp.int32)
    qtb = jnp.transpose(q).astype(BF16)
    kb = k2.astype(BF16)
    vtb = jnp.transpose(v2).astype(BF16)
    outs = [None] * A_HEADS
    for g in range(A_HEADS // A_GROUP):
        heads = [A_GROUP * g + u for u in range(A_GROUP)]
        qs = jnp.concatenate([qtb[64 * h:64 * h + 64, :] for h in heads], axis=1)
        s = _dot(kb[:, 64 * g:64 * g + 64], qs) * (HEAD_DIM ** -0.5) + bias[g]
        s = jnp.where(no_prev, NEG, s)
        sink = jnp.concatenate([jnp.broadcast_to(sinks[:, h:h + 1], (1, WINDOW)) for h in heads], axis=1)
        m = lax.stop_gradient(jnp.maximum(jnp.max(s, axis=0, keepdims=True), sink))
        p = jnp.exp(s - m)
        den = jnp.sum(p, axis=0, keepdims=True) + jnp.exp(sink - m)
        o = _dot(vtb[64 * g:64 * g + 64, :], (p * (1.0 / den)).astype(BF16))
        for u, h in enumerate(heads):
            outs[h] = o[:, WINDOW * u:WINDOW * u + WINDOW]
    return jnp.transpose(jnp.concatenate(outs, axis=0)) * _silu(z)


def _swa_bias():
    qi = np.arange(WINDOW)[None, :]
    kj = np.arange(2 * WINDOW)[:, None]
    rel = qi + WINDOW - kj
    valid = (rel >= 0) & (rel < WINDOW)
    out = np.zeros((A_HEADS // A_GROUP, 2 * WINDOW, A_GROUP * WINDOW), np.float32)
    for h in range(A_HEADS):
        slope = np.float32(2.0 ** (-8.0 * (h + 1) / A_HEADS))
        blk = np.where(valid, -slope * rel.astype(np.float32), np.float32(NEG))
        g, u = divmod(h, A_GROUP)
        out[g, :, WINDOW * u:WINDOW * u + WINDOW] = blk
    return jnp.asarray(out)


def _mem_fn(qz, mkv, qg, kg, bd):
    q = _head_norm(qz[:, :512], qg, bd).astype(BF16)
    k = _head_norm(mkv[:, :512], kg, bd).astype(BF16)
    v = mkv[:, 512:].astype(BF16)
    z = qz[:, 512:]
    outs = []
    for h in range(C_HEADS):
        sl = slice(128 * h, 128 * h + 128)
        s = _dot_nt(q[:, sl], k[:, sl]) * (C_HEAD_DIM ** -0.5)
        m = lax.stop_gradient(jnp.max(s, axis=-1, keepdims=True))
        p = jnp.exp(s - m)
        den = jnp.sum(p, axis=-1, keepdims=True)
        outs.append(_dot((p * (1.0 / den)).astype(BF16), v[:, sl]))
    return jnp.concatenate(outs, axis=1) * _silu(z)


def _qn_fn(q, g, bd):
    return _head_norm(q, g, bd) * (HEAD_DIM ** -0.5)


def _kn_fn(k, g, bd):
    return _head_norm(k, g, bd)


def _block_diag(width, hd):
    i = np.arange(width) // hd
    return jnp.asarray((i[:, None] == i[None, :]).astype(np.float32) / hd, BF16)


def _head_sum(width, hd):
    i = np.arange(width) // hd
    return jnp.asarray((i[:, None] == np.arange(LANE)[None, :]).astype(np.float32))


def _rms_fwd(x, g, *, tr, name):
    rows, dm = x.shape

    def body(x_ref, g_ref, o_ref):
        o_ref[...] = _rms(x_ref[...], g_ref[...]).astype(BF16)

    return pl.pallas_call(
        body, grid=(rows // tr,),
        in_specs=[_rowblk(tr, dm), _const((1, dm))],
        out_specs=_rowblk(tr, dm),
        out_shape=SDS((rows, dm), BF16), name=name,
        compiler_params=_params(("parallel",)))(x, g)


def _rms_bwd(x, g, dy, resid, *, tr, name, comm=None):
    rows, dm = x.shape
    want_dx = resid is not None
    n_in = 4 if want_dx else 3
    n_out = 2 if want_dx else 1
    c_in = len(comm.ins) if comm else 0
    c_out = len(comm.out_shapes) if comm else 0
    nb = rows // tr

    def body(*refs):
        x_ref, g_ref, dy_ref = refs[:3]
        r_ref = refs[3] if want_dx else None
        cin = refs[n_in:n_in + c_in]
        outs = refs[n_in + c_in:n_in + c_in + n_out]
        dg_ref = outs[-1]
        cout = refs[n_in + c_in + n_out:n_in + c_in + n_out + c_out]
        csem = refs[n_in + c_in + n_out + c_out:]

        if comm:
            @pl.when(pl.program_id(0) == 0)
            def _():
                comm.start(cin, cout, csem)

        _, vjp = jax.vjp(_rms, x_ref[...], g_ref[...])
        dx, dg = vjp(dy_ref[...])

        @pl.when(pl.program_id(0) == 0)
        def _():
            dg_ref[...] = jnp.zeros_like(dg_ref)

        dg_ref[...] += dg
        if want_dx:
            outs[0][...] = r_ref[...] + dx

        if comm:
            @pl.when(pl.program_id(0) == nb - 1)
            def _():
                comm.finish(cin, cout, csem)

    hbm = pl.BlockSpec(memory_space=pl.ANY)
    ins = [x, g, dy] + ([resid] if want_dx else []) + (list(comm.ins) if comm else [])
    in_specs = ([_rowblk(tr, dm), _const((1, dm)), _rowblk(tr, dm)] + ([_rowblk(tr, dm)] if want_dx else [])
                + [hbm] * c_in)
    out_specs = ([_rowblk(tr, dm)] if want_dx else []) + [_const((1, dm))] + [hbm] * c_out
    out_shape = (([SDS((rows, dm), F32)] if want_dx else []) + [SDS((1, dm), F32)]
                 + (list(comm.out_shapes) if comm else []))
    res = pl.pallas_call(
        body, grid=(nb,), in_specs=in_specs, out_specs=out_specs, out_shape=out_shape,
        scratch_shapes=list(comm.sems) if comm else [], name=name, compiler_params=_params(("arbitrary",)))(*ins)
    return (list(res[:n_out]), list(res[n_out:])) if comm else res


class _Comm:
    def __init__(self, ins, out_shapes, sems, start, finish):
        self.ins, self.out_shapes, self.sems, self.start, self.finish = list(ins), list(out_shapes), list(sems), start, finish


def _matmul(a, b, *, dims, out_dtype, tm, tn, tk, name, add=None, comms=(), vmem=VMEM_LIMIT, a_half=None):
    a_list = list(a) if isinstance(a, (list, tuple)) else [a]
    b_list = list(b) if isinstance(b, (list, tuple)) else [b]
    assert len(a_list) == 1 or dims == "nt"
    assert len(b_list) == 1 or dims == "tn"
    if dims == "tn":
        kdim, m = a_list[0].shape
        if a_half is not None:
            m //= 2
    else:
        m, kdim = a_list[0].shape[0], sum(p.shape[1] for p in a_list)
    n = b_list[0].shape[0] if dims == "nt" else sum(p.shape[1] for p in b_list)
    tm, tn, tk = min(tm, m), min(tn, n), min(tk, kdim)
    assert m % tm == 0 and n % tn == 0 and kdim % tk == 0, (name, m, n, kdim)
    ni, nj, nk = m // tm, n // tn, kdim // tk
    a_rng, b_rng, pos = [], [], 0
    for p in a_list:
        assert len(a_list) == 1 or p.shape[1] % tk == 0
        a_rng.append((pos, p.shape[1] // tk if len(a_list) > 1 else nk))
        pos += a_rng[-1][1]
    pos = 0
    for p in b_list:
        assert len(b_list) == 1 or p.shape[1] % tn == 0
        b_rng.append((pos, p.shape[1] // tn if len(b_list) > 1 else nj))
        pos += b_rng[-1][1]
    has_add = add is not None
    n_mm_in = len(a_list) + len(b_list) + (1 if has_add else 0)
    c_in = [len(c.ins) for c in comms]
    c_out = [len(c.out_shapes) for c in comms]
    c_sem = [len(c.sems) for c in comms]

    def body(*refs):
        if a_half is not None:
            refs = refs[1:]
        a_refs, b_refs = refs[:len(a_list)], refs[len(a_list):len(a_list) + len(b_list)]
        add_ref = refs[n_mm_in - 1] if has_add else None
        pos = n_mm_in
        cin = []
        for cnt in c_in:
            cin.append(refs[pos:pos + cnt])
            pos += cnt
        o_ref = refs[pos]
        pos += 1
        cout = []
        for cnt in c_out:
            cout.append(refs[pos:pos + cnt])
            pos += cnt
        acc = refs[pos]
        pos += 1
        csem = []
        for cnt in c_sem:
            csem.append(refs[pos:pos + cnt])
            pos += cnt
        i, j, k = pl.program_id(0), pl.program_id(1), pl.program_id(2)

        if comms:
            @pl.when((i == 0) & (j == 0) & (k == 0))
            def _():
                for c, ci, co, cs in zip(comms, cin, cout, csem):
                    c.start(ci, co, cs)

        def accumulate(a_ref, b_ref, first_k, later_k):
            if dims == "nn":
                part = _dot(a_ref[...], b_ref[...])
            elif dims == "nt":
                part = _dot_nt(a_ref[...], b_ref[...])
            else:
                part = _dot_tn(a_ref[...], b_ref[...])

            if first_k:
                @pl.when(k == 0)
                def _():
                    acc[...] = part + add_ref[...] if has_add else part

            if later_k:
                @pl.when(k > 0)
                def _():
                    acc[...] += part

        if len(a_list) > 1:
            for a_ref, (k0, cnt) in zip(a_refs, a_rng):
                @pl.when((k >= k0) & (k < k0 + cnt))
                def _(a_ref=a_ref, k0=k0, cnt=cnt):
                    accumulate(a_ref, b_refs[0], k0 == 0, k0 + cnt > 1)
        elif len(b_list) > 1:
            for b_ref, (j0, cnt) in zip(b_refs, b_rng):
                @pl.when((j >= j0) & (j < j0 + cnt))
                def _(b_ref=b_ref):
                    accumulate(a_refs[0], b_ref, True, nk > 1)
        else:
            accumulate(a_refs[0], b_refs[0], True, nk > 1)

        @pl.when(k == nk - 1)
        def _():
            o_ref[...] = acc[...].astype(out_dtype)

        if comms:
            @pl.when((i == ni - 1) & (j == nj - 1) & (k == nk - 1))
            def _():
                for c, ci, co, cs in zip(comms, cin, cout, csem):
                    c.finish(ci, co, cs)

    def a_spec(k0, cnt):
        if dims == "tn":
            if a_half is None:
                return pl.BlockSpec((tk, tm), lambda i, j, k: (k, i))
            pick = (lambda c: 1 - c[0]) if a_half[1] else (lambda c: c[0])
            return pl.BlockSpec((tk, tm), lambda i, j, k, c: (k, i + pick(c) * ni))
        return pl.BlockSpec((tm, tk), lambda i, j, k, *_: (i, jnp.clip(k - k0, 0, cnt - 1)))

    def b_spec(j0, cnt):
        if dims == "nt":
            return pl.BlockSpec((tn, tk), lambda i, j, k, *_: (j, k))
        return pl.BlockSpec((tk, tn), lambda i, j, k, *_: (k, jnp.clip(j - j0, 0, cnt - 1)))

    o_spec = pl.BlockSpec((tm, tn), lambda i, j, k, *_: (i, j))
    hbm = pl.BlockSpec(memory_space=pl.ANY)
    ins = a_list + b_list + ([add] if has_add else []) + [x for c in comms for x in c.ins]
    in_specs = ([a_spec(*r) for r in a_rng] + [b_spec(*r) for r in b_rng] + ([o_spec] if has_add else [])
                + [hbm] * sum(c_in))
    out_specs = [o_spec] + [hbm] * sum(c_out)
    out_shape = [SDS((m, n), out_dtype)] + [s for c in comms for s in c.out_shapes]
    scratch = [pltpu.VMEM((tm, tn), F32)] + [s for c in comms for s in c.sems]
    sem = ("arbitrary",) * 3 if comms else ("parallel", "parallel", "arbitrary")
    if a_half is None:
        grid = dict(grid=(ni, nj, nk), in_specs=in_specs, out_specs=out_specs, scratch_shapes=scratch)
    else:
        grid = dict(grid_spec=pltpu.PrefetchScalarGridSpec(
            num_scalar_prefetch=1, grid=(ni, nj, nk), in_specs=in_specs, out_specs=out_specs, scratch_shapes=scratch))
        ins = [a_half[0]] + ins
    res = pl.pallas_call(body, out_shape=out_shape, name=name, compiler_params=_params(sem, vmem), **grid)(*ins)
    if not comms:
        return res[0]
    outs, pos = [], 1
    for cnt in c_out:
        outs.append(list(res[pos:pos + cnt]))
        pos += cnt
    return res[0], outs


def _swa_specs(nb, blk=lambda n: n):
    cur = lambda off: pl.BlockSpec((pl.Element(WINDOW), pl.Element(1024)), lambda n: (blk(n) * WINDOW, off))
    prev = lambda off: pl.BlockSpec((pl.Element(WINDOW), pl.Element(1024)),
                                    lambda n: (jnp.maximum(blk(n) - 1, 0) * WINDOW, off))
    return [cur(QA), cur(VA), prev(QA), prev(VA),
            _const((1, 768)), _const((1, 256)), _const((1, LANE)), _const((768, 768)),
            _const((A_HEADS // A_GROUP, 2 * WINDOW, A_GROUP * WINDOW))]


def _swa_fwd(proj, qg, kg, sinks, bd, bias):
    s = proj.shape[0]
    nb = s // WINDOW

    def body(qk_ref, vz_ref, qkp_ref, vzp_ref, qg_ref, kg_ref, sk_ref, bd_ref, bias_ref, o_ref):
        first = pl.program_id(0) == 0
        o_ref[...] = _swa_fn(qk_ref[...], vz_ref[...], qkp_ref[...], vzp_ref[...], qg_ref[...], kg_ref[...],
                             sk_ref[...], bd_ref[...], bias_ref[...], first).astype(BF16)

    return pl.pallas_call(
        body, grid=(nb,), in_specs=_swa_specs(nb), out_specs=_rowblk(WINDOW, 768),
        out_shape=SDS((s, 768), BF16), name="swa_fwd",
        compiler_params=_params(("parallel",)))(proj, proj, proj, proj, qg, kg, sinks, bd, bias)


def _swa_bwd(proj, qg, kg, sinks, bd, bias, dga):
    s = proj.shape[0]
    nb = s // WINDOW
    blk = lambda n: nb - 1 - n

    def body(qk_ref, vz_ref, qkp_ref, vzp_ref, qg_ref, kg_ref, sk_ref, bd_ref, bias_ref, dg_ref,
             d_ref, dqg_ref, dkg_ref, dsk_ref, carry):
        first = blk(pl.program_id(0)) == 0
        bd_v = bd_ref[...]
        bias_v = bias_ref[...]
        fn = lambda qk, vz, qkp, vzp, qg_, kg_, sk: _swa_fn(qk, vz, qkp, vzp, qg_, kg_, sk, bd_v, bias_v, first)
        _, vjp = jax.vjp(fn, qk_ref[...], vz_ref[...], qkp_ref[...], vzp_ref[...], qg_ref[...], kg_ref[...], sk_ref[...])
        dqk, dvz, dqkp, dvzp, dqg, dkg, dsk = vjp(dg_ref[...])

        @pl.when(pl.program_id(0) == 0)
        def _():
            dqg_ref[...] = jnp.zeros_like(dqg_ref)
            dkg_ref[...] = jnp.zeros_like(dkg_ref)
            dsk_ref[...] = jnp.zeros_like(dsk_ref)
            carry[...] = jnp.zeros_like(carry)

        dqg_ref[...] += dqg
        dkg_ref[...] += dkg
        dsk_ref[...] += dsk
        kv = jnp.concatenate([dqk[:, 768:], dvz[:, :256]], axis=1) + carry[...]
        d_ref[...] = jnp.concatenate([dqk[:, :768], kv, dvz[:, 256:]], axis=1).astype(BF16)
        carry[...] = jnp.concatenate([dqkp[:, 768:], dvzp[:, :256]], axis=1)

    rev = lambda w: pl.BlockSpec((WINDOW, w), lambda n: (blk(n), 0))
    return pl.pallas_call(
        body, grid=(nb,), in_specs=_swa_specs(nb, blk) + [rev(768)],
        out_specs=[rev(2048), _const((1, 768)), _const((1, 256)), _const((1, LANE))],
        out_shape=[SDS((s, 2048), BF16), SDS((1, 768), F32), SDS((1, 256), F32), SDS((1, LANE), F32)],
        scratch_shapes=[pltpu.VMEM((WINDOW, 512), F32)],
        name="swa_bwd", compiler_params=_params(("arbitrary",)))(proj, proj, proj, proj, qg, kg, sinks, bd, bias, dga)


def _mem_fwd(proj, mkv, qg, kg, bd, *, tr):
    s = proj.shape[0]

    def body(qz_ref, mkv_ref, qg_ref, kg_ref, bd_ref, o_ref):
        o_ref[...] = _mem_fn(qz_ref[...], mkv_ref[...], qg_ref[...], kg_ref[...], bd_ref[...]).astype(BF16)

    return pl.pallas_call(
        body, grid=(s // tr,),
        in_specs=[_win(tr, 1024, QC), _const(mkv.shape), _const((1, 512)), _const((1, 512)), _const((512, 512))],
        out_specs=_rowblk(tr, 512), out_shape=SDS((s, 512), BF16), name="mem_fwd",
        compiler_params=_params(("parallel",)))(proj, mkv, qg, kg, bd)


def _mem_bwd(proj, mkv, qg, kg, bd, dgc, *, tr):
    s = proj.shape[0]

    def body(qz_ref, mkv_ref, qg_ref, kg_ref, bd_ref, dg_ref, dqz_ref, dmkv_ref, dqg_ref, dkg_ref):
        bd_v = bd_ref[...]
        fn = lambda qz, mkv_, qg_, kg_: _mem_fn(qz, mkv_, qg_, kg_, bd_v)
        _, vjp = jax.vjp(fn, qz_ref[...], mkv_ref[...], qg_ref[...], kg_ref[...])
        dqz, dmkv, dqg, dkg = vjp(dg_ref[...])

        @pl.when(pl.program_id(0) == 0)
        def _():
            dmkv_ref[...] = jnp.zeros_like(dmkv_ref)
            dqg_ref[...] = jnp.zeros_like(dqg_ref)
            dkg_ref[...] = jnp.zeros_like(dkg_ref)

        dmkv_ref[...] += dmkv
        dqg_ref[...] += dqg
        dkg_ref[...] += dkg
        dqz_ref[...] = dqz.astype(BF16)

    return pl.pallas_call(
        body, grid=(s // tr,),
        in_specs=[_win(tr, 1024, QC), _const(mkv.shape), _const((1, 512)), _const((1, 512)), _const((512, 512)),
                  _rowblk(tr, 512)],
        out_specs=[_rowblk(tr, 1024), _const(mkv.shape), _const((1, 512)), _const((1, 512))],
        out_shape=[SDS((s, 1024), BF16), SDS(mkv.shape, F32), SDS((1, 512), F32), SDS((1, 512), F32)],
        name="mem_bwd", compiler_params=_params(("arbitrary",)))(proj, mkv, qg, kg, bd, dgc)


def _log_sigmoid(x):
    return jnp.minimum(x, 0.0) - jnp.log1p(jnp.exp(-jnp.abs(x)))


FOX_TQ, FOX_TK = 512, 512
FOX_FWD_TQ, FOX_FWD_TK = 512, 1024


def _fox_tiles(s):
    return min(FOX_TQ, s), min(FOX_TK, s)


AUG = 128 * B_HEADS
COL_A, COL_B = 64, 67


def _split3(c):
    hi = c.astype(BF16)
    r1 = c - hi.astype(F32)
    mid = r1.astype(BF16)
    lo = (r1 - mid.astype(F32)).astype(BF16)
    return hi, mid, lo


def _expand_mats():
    def mat(col0):
        e = np.zeros((768 + 3 * LANE, AUG), np.float32)
        for h in range(B_HEADS):
            for d in range(HEAD_DIM):
                e[64 * h + d, 128 * h + d] = 1.0
            for part in range(3):
                e[768 + LANE * part + h, 128 * h + col0 + part] = 1.0
        return e

    def ones(col0):
        o = np.zeros((1, AUG), np.float32)
        for h in range(B_HEADS):
            o[0, 128 * h + col0:128 * h + col0 + 3] = 1.0
        return o

    return (jnp.asarray(mat(COL_A), BF16), jnp.asarray(mat(COL_B), BF16), jnp.asarray(ones(COL_A)), jnp.asarray(ones(COL_B)))


def _augment(data_bf16, triple, emat, ones_row):
    parts = [data_bf16] + (list(triple) if triple is not None else [jnp.zeros((data_bf16.shape[0], LANE), BF16)] * 3)
    wide = _dot(jnp.concatenate(parts, axis=1), emat)
    if ones_row is not None:
        wide = wide + ones_row
    return wide


def _compact(wide):
    return jnp.concatenate([wide[:, 128 * h:128 * h + 64] for h in range(wide.shape[1] // 128)], axis=1)


def _lane_of_heads(wide, col, first=0):
    rows = wide.shape[0]
    lane = lax.broadcasted_iota(jnp.int32, (rows, LANE), 1)
    out = jnp.zeros((rows, LANE), F32)
    for h in range(wide.shape[1] // 128):
        out = jnp.where(lane == first + h, wide[:, 128 * h + col:128 * h + col + 1], out)
    return out


def _fox2_prep(proj, fbl, qg, kg, bfor, bd, ea, eb, ones_a, ones_b, *, tr):
    s = proj.shape[0]
    tri = jnp.asarray(np.tril(np.ones((tr, tr), np.float32)))

    def body(q_ref, k_ref, v_ref, fb_ref, qg_ref, kg_ref, bf_ref, bd_ref, tri_ref, ea_ref, eb_ref, oa_ref, ob_ref,
             qat_ref, ka_ref, kat_ref, va_ref, vat_ref, qn_ref, c_ref, carry):
        @pl.when(pl.program_id(0) == 0)
        def _():
            carry[...] = jnp.zeros_like(carry)

        bd_v = bd_ref[...]
        lane = lax.broadcasted_iota(jnp.int32, (tr, LANE), 1)
        logf = jnp.where(lane < N_FORGET, _log_sigmoid(fb_ref[...] + bf_ref[...]), 0.0)
        c = jnp.dot(tri_ref[...], logf, precision=HI, preferred_element_type=F32) + carry[...]
        c_ref[...] = c
        carry[...] = c[tr - 1:tr, :]
        qn = _qn_fn(q_ref[...], qg_ref[...], bd_v).astype(BF16)
        kn = _kn_fn(k_ref[...], kg_ref[...], bd_v).astype(BF16)
        qn_ref[...] = qn
        qat_ref[...] = jnp.transpose(_augment(qn, _split3(c), ea_ref[...], ob_ref[...])).astype(BF16)
        ka = _augment(kn, _split3(-c), eb_ref[...], oa_ref[...])
        ka_ref[...] = ka.astype(BF16)
        kat_ref[...] = jnp.transpose(ka).astype(BF16)
        va = _augment(v_ref[...].astype(BF16), None, ea_ref[...], oa_ref[...])
        va_ref[...] = va.astype(BF16)
        vat_ref[...] = jnp.transpose(va).astype(BF16)

    emat = _const((768 + 3 * LANE, AUG))
    return pl.pallas_call(
        body, grid=(s // tr,),
        in_specs=[_win(tr, 768, QB), _win(tr, 768, KB), _win(tr, 768, VB), _rowblk(tr, LANE), _const((1, 768)),
                  _const((1, 768)), _const((1, LANE)), _const((768, 768)), _const((tr, tr)), emat, emat,
                  _const((1, AUG)), _const((1, AUG))],
        out_specs=[pl.BlockSpec((AUG, tr), lambda i: (0, i)), _rowblk(tr, AUG), pl.BlockSpec((AUG, tr), lambda i: (0, i)),
                   _rowblk(tr, AUG), pl.BlockSpec((AUG, tr), lambda i: (0, i)), _rowblk(tr, 768), _rowblk(tr, LANE)],
        out_shape=[SDS((AUG, s), BF16), SDS((s, AUG), BF16), SDS((AUG, s), BF16), SDS((s, AUG), BF16),
                   SDS((AUG, s), BF16), SDS((s, 768), BF16), SDS((s, LANE), F32)],
        scratch_shapes=[pltpu.VMEM((1, LANE), F32)], name="fox_prep",
        compiler_params=_params(("arbitrary",)))(proj, proj, proj, fbl, qg, kg, bfor, bd, tri, ea, eb, ones_a, ones_b)


def _fox2_fwd(proj, qat, ka, vat):
    s = proj.shape[0]
    tq, tk = min(FOX_FWD_TQ, s), min(FOX_FWD_TK, s)
    nq, nk = s // tq, s // tk

    def last_k(i):
        return (i * tq + tq - 1) // tk

    def body(qt_ref, k_ref, vt_ref, z_ref, gb_ref, yb_ref, lse_ref, acc, m_s):
        i, j = pl.program_id(0), pl.program_id(1)

        @pl.when(j == 0)
        def _():
            acc[...] = jnp.zeros_like(acc)
            m_s[...] = jnp.full_like(m_s, NEG)

        def tile(masked):
            if masked:
                kpos = j * tk + lax.broadcasted_iota(jnp.int32, (tk, tq), 0)
                qpos = i * tq + lax.broadcasted_iota(jnp.int32, (tk, tq), 1)
                mask = kpos <= qpos
            for h in range(B_HEADS):
                sl = slice(128 * h, 128 * h + 128)
                sc = _dot(k_ref[:, sl], qt_ref[sl, :])
                if masked:
                    sc = jnp.where(mask, sc, NEG)
                m_prev = m_s[h:h + 1, :]
                m_new = jnp.maximum(m_prev, jnp.max(sc, axis=0, keepdims=True))
                p = jnp.exp(sc - m_new).astype(BF16)
                acc[sl, :] = jnp.exp(m_prev - m_new) * acc[sl, :] + _dot(vt_ref[sl, :], p)
                m_s[h:h + 1, :] = m_new

        full = j * tk + tk - 1 <= i * tq

        @pl.when(full)
        def _():
            tile(False)

        @pl.when(jnp.logical_and(jnp.logical_not(full), j <= last_k(i)))
        def _():
            tile(True)

        @pl.when(j == nk - 1)
        def _():
            outs = []
            row = lax.broadcasted_iota(jnp.int32, (LANE, tq), 0)
            lse_t = jnp.zeros((LANE, tq), F32)
            for h in range(B_HEADS):
                l_row = acc[128 * h + COL_A:128 * h + COL_A + 1, :]
                outs.append(acc[128 * h:128 * h + 64, :] * (1.0 / l_row))
                lse_t = jnp.where(row == h, m_s[h:h + 1, :] + jnp.log(l_row), lse_t)
            y = jnp.transpose(jnp.concatenate(outs, axis=0))
            yb_ref[...] = y
            gb_ref[...] = (y * _silu(z_ref[...])).astype(BF16)
            lse_ref[...] = jnp.transpose(lse_t)

    kcol = lambda i, j: (0, jnp.minimum(j, last_k(i)))
    return pl.pallas_call(
        body, grid=(nq, nk),
        in_specs=[pl.BlockSpec((AUG, tq), lambda i, j: (0, i)),
                  pl.BlockSpec((tk, AUG), lambda i, j: (jnp.minimum(j, last_k(i)), 0)),
                  pl.BlockSpec((AUG, tk), kcol),
                  pl.BlockSpec((pl.Element(tq), pl.Element(768)), lambda i, j: (i * tq, ZB))],
        out_specs=[pl.BlockSpec((tq, 768), lambda i, j: (i, 0)), pl.BlockSpec((tq, 768), lambda i, j: (i, 0)),
                   pl.BlockSpec((tq, LANE), lambda i, j: (i, 0))],
        out_shape=[SDS((s, 768), BF16), SDS((s, 768), F32), SDS((s, LANE), F32)],
        scratch_shapes=[pltpu.VMEM((AUG, tq), F32), pltpu.VMEM((16, tq), F32)],
        name="fox_fwd", compiler_params=_params(("parallel", "arbitrary")))(qat, ka, vat, proj)


def _fox2_bwd_pre(proj, yb, dgb, qn, c, lse, hsum, ea, ones_b, *, tr):
    s = proj.shape[0]

    def body(z_ref, y_ref, dg_ref, qn_ref, c_ref, lse_ref, hs_ref, ea_ref, ob_ref,
             qa_ref, qat_ref, dya_ref, dyat_ref, dz_ref):
        z, y, dg = z_ref[...], y_ref[...], dg_ref[...]
        sg = jax.nn.sigmoid(z)
        dy = dg * (z * sg)
        dz_ref[...] = (dg * y * (sg * (1.0 + z * (1.0 - sg)))).astype(BF16)
        delta = _head_mean_impl(dy * y, hs_ref[...].astype(BF16))
        e = ea_ref[...]
        dya = _augment(dy.astype(BF16), _split3(-delta), e, None)
        dya_ref[...] = dya.astype(BF16)
        dyat_ref[...] = jnp.transpose(dya).astype(BF16)
        qa = _augment(qn_ref[...], _split3(c_ref[...] - lse_ref[...]), e, ob_ref[...])
        qa_ref[...] = qa.astype(BF16)
        qat_ref[...] = jnp.transpose(qa).astype(BF16)

    return pl.pallas_call(
        body, grid=(s // tr,),
        in_specs=[_win(tr, 768, ZB), _rowblk(tr, 768), _rowblk(tr, 768), _rowblk(tr, 768), _rowblk(tr, LANE),
                  _rowblk(tr, LANE), _const((768, LANE)), _const((768 + 3 * LANE, AUG)), _const((1, AUG))],
        out_specs=[_rowblk(tr, AUG), pl.BlockSpec((AUG, tr), lambda i: (0, i)), _rowblk(tr, AUG),
                   pl.BlockSpec((AUG, tr), lambda i: (0, i)), _rowblk(tr, 768)],
        out_shape=[SDS((s, AUG), BF16), SDS((AUG, s), BF16), SDS((s, AUG), BF16), SDS((AUG, s), BF16),
                   SDS((s, 768), BF16)], name="fox_bwd_pre",
        compiler_params=_params(("parallel",)))(proj, yb, dgb, qn, c, lse, hsum, ea, ones_b)


def _fox2_bwd(qb, qbt, ka, kat, va, dya, dyat):
    s = qb.shape[0]
    tq, tk = _fox_tiles(s)
    nq, nk = s // tq, s // tk
    ng = 2
    gh = B_HEADS // ng
    gw = 128 * gh

    def first_q(j):
        return (j * tk) // tq

    def body(q_ref, qt_ref, k_ref, kt_ref, v_ref, dy_ref, dyt_ref, dq_hbm, dk_ref, dv_ref, dck_ref,
             dq_acc, dk_acc, dv_acc, sem):
        g, j, i = pl.program_id(0), pl.program_id(1), pl.program_id(2)

        @pl.when((j == 0) & (i == 0))
        def _():
            dq_acc[...] = jnp.zeros_like(dq_acc)

        @pl.when(i == 0)
        def _():
            dk_acc[...] = jnp.zeros_like(dk_acc)
            dv_acc[...] = jnp.zeros_like(dv_acc)

        def tile(masked):
            if masked:
                kpos = j * tk + lax.broadcasted_iota(jnp.int32, (tk, tq), 0)
                qpos = i * tq + lax.broadcasted_iota(jnp.int32, (tk, tq), 1)
                mask = kpos <= qpos
            cols = pl.ds(pl.multiple_of(i * tq, tq), tq)
            for h in range(gh):
                sl = slice(128 * h, 128 * h + 128)
                sc = _dot(k_ref[:, sl], qt_ref[sl, :])
                if masked:
                    sc = jnp.where(mask, sc, NEG)
                p = jnp.exp(sc)
                ds = (p * _dot(v_ref[:, sl], dyt_ref[sl, :])).astype(BF16)
                dv_acc[:, sl] += _dot(p.astype(BF16), dy_ref[:, sl])
                dk_acc[:, sl] += _dot(ds, q_ref[:, sl])
                dq_acc[sl, cols] += _dot(kt_ref[sl, :], ds)

        full = j * tk + tk - 1 <= i * tq

        @pl.when(full)
        def _():
            tile(False)

        @pl.when(jnp.logical_and(jnp.logical_not(full), i >= first_q(j)))
        def _():
            tile(True)

        @pl.when(i == nq - 1)
        def _():
            dkw = dk_acc[...]
            dk_ref[...] = _compact(dkw)
            dv_ref[...] = _compact(dv_acc[...]).astype(BF16)
            dck_ref[...] = -_lane_of_heads(dkw, COL_B, gh * g)

        @pl.when((j == nk - 1) & (i == nq - 1))
        def _():
            cp = pltpu.make_async_copy(dq_acc, dq_hbm.at[pl.ds(pl.multiple_of(g * gw, gw), gw)], sem)
            cp.start()
            cp.wait()

    qrow = pl.BlockSpec((tq, gw), lambda g, j, i: (jnp.maximum(i, first_q(j)), g))
    qcol = pl.BlockSpec((gw, tq), lambda g, j, i: (g, jnp.maximum(i, first_q(j))))
    krow = pl.BlockSpec((tk, gw), lambda g, j, i: (j, g))
    kcol = pl.BlockSpec((gw, tk), lambda g, j, i: (g, j))
    kout = pl.BlockSpec((tk, gw // 2), lambda g, j, i: (j, g))
    return pl.pallas_call(
        body, grid=(ng, nk, nq),
        in_specs=[qrow, qcol, krow, kcol, krow, qrow, qcol],
        out_specs=[pl.BlockSpec(memory_space=pl.ANY), kout, kout,
                   pl.BlockSpec((None, tk, LANE), lambda g, j, i: (g, j, 0))],
        out_shape=[SDS((AUG, s), F32), SDS((s, 768), F32), SDS((s, 768), BF16), SDS((ng, s, LANE), F32)],
        scratch_shapes=[pltpu.VMEM((gw, s), F32), pltpu.VMEM((tk, gw), F32), pltpu.VMEM((tk, gw), F32),
                        pltpu.SemaphoreType.DMA],
        name="fox_bwd", compiler_params=_params(("arbitrary",) * 3))(qb, qbt, ka, kat, va, dya, dyat)


def _fox2_bwd_post(proj, fbl, qg, kg, bfor, bd, dqa, dkn, dck, *, tr):
    s = proj.shape[0]
    nb = s // tr
    triu = jnp.asarray(np.triu(np.ones((tr, tr), np.float32)))
    rev = lambda i: nb - 1 - i

    def body(q_ref, k_ref, fb_ref, qg_ref, kg_ref, bf_ref, bd_ref, tri_ref, dqa_ref, dkn_ref, dck_ref,
             dq_ref, dk_ref, dfb_ref, dqg_ref, dkg_ref, dbf_ref, carry):
        @pl.when(pl.program_id(0) == 0)
        def _():
            carry[...] = jnp.zeros_like(carry)
            dqg_ref[...] = jnp.zeros_like(dqg_ref)
            dkg_ref[...] = jnp.zeros_like(dkg_ref)
            dbf_ref[...] = jnp.zeros_like(dbf_ref)

        bd_v = bd_ref[...]
        dqw = jnp.transpose(dqa_ref[...])
        _, vjp_q = jax.vjp(lambda q, g: _qn_fn(q, g, bd_v), q_ref[...], qg_ref[...])
        dq, dqg = vjp_q(_compact(dqw))
        _, vjp_k = jax.vjp(lambda k, g: _kn_fn(k, g, bd_v), k_ref[...], kg_ref[...])
        dk, dkg = vjp_k(dkn_ref[...])
        dq_ref[...] = dq.astype(BF16)
        dk_ref[...] = dk.astype(BF16)
        dqg_ref[...] += dqg
        dkg_ref[...] += dkg

        dc = _lane_of_heads(dqw, COL_A) + (dck_ref[0] + dck_ref[1])
        dlogf = jnp.dot(tri_ref[...], dc, precision=HI, preferred_element_type=F32) + carry[...]
        carry[...] = dlogf[0:1, :]
        lane = lax.broadcasted_iota(jnp.int32, (tr, LANE), 1)
        xf = fb_ref[...] + bf_ref[...]
        dfb = jnp.where(lane < N_FORGET, dlogf * jax.nn.sigmoid(-xf), 0.0)
        dfb_ref[...] = dfb.astype(BF16)
        dbf_ref[...] += jnp.sum(dfb, axis=0, keepdims=True)

    rb = lambda w: pl.BlockSpec((tr, w), lambda i: (rev(i), 0))
    wn = lambda w, off: pl.BlockSpec((pl.Element(tr), pl.Element(w)), lambda i: (rev(i) * tr, off))
    return pl.pallas_call(
        body, grid=(nb,),
        in_specs=[wn(768, QB), wn(768, KB), rb(LANE), _const((1, 768)), _const((1, 768)), _const((1, LANE)),
                  _const((768, 768)), _const((tr, tr)), pl.BlockSpec((AUG, tr), lambda i: (0, rev(i))), rb(768),
                  pl.BlockSpec((2, tr, LANE), lambda i: (0, rev(i), 0))],
        out_specs=[rb(768), rb(768), rb(LANE), _const((1, 768)), _const((1, 768)), _const((1, LANE))],
        out_shape=[SDS((s, 768), BF16), SDS((s, 768), BF16), SDS((s, LANE), BF16), SDS((1, 768), F32),
                   SDS((1, 768), F32), SDS((1, LANE), F32)],
        scratch_shapes=[pltpu.VMEM((1, LANE), F32)], name="fox_bwd_post",
        compiler_params=_params(("arbitrary",)))(proj, proj, fbl, qg, kg, bfor, bd, triu, dqa, dkn, dck)


def _merge_specs(tr):
    row = lambda w: pl.BlockSpec((tr, w), lambda i, j: (i, 0))
    shard = lambda r: pl.BlockSpec((None, r, 512), lambda i, j: (j, 0, 0))
    gate = lambda b: pl.BlockSpec((tr, 512), lambda i, j: (i, (GATE + 2048 * b) // 512 + j))
    return [row(768), row(768), row(512), shard(768), shard(768), shard(512), gate(0), gate(1), gate(2)]


def _merge_fwd(proj, ga, gb, gc, wa, wb, wc, *, tr):
    s = proj.shape[0]

    def body(ga_ref, gb_ref, gc_ref, wa_ref, wb_ref, wc_ref, l0_ref, l1_ref, l2_ref, y_ref):
        ua = _dot(ga_ref[...], wa_ref[...])
        ub = _dot(gb_ref[...], wb_ref[...])
        uc = _dot(gc_ref[...], wc_ref[...])
        y = jax.nn.sigmoid(l0_ref[...]) * ua + jax.nn.sigmoid(l1_ref[...]) * ub + jax.nn.sigmoid(l2_ref[...]) * uc
        y_ref[...] = y.astype(BF16)

    return pl.pallas_call(
        body, grid=(s // tr, N_CHIPS), in_specs=_merge_specs(tr),
        out_specs=pl.BlockSpec((tr, 512), lambda i, j: (i, j)), out_shape=SDS((s, D_MODEL), BF16), name="merge_fwd",
        compiler_params=_params(("parallel", "arbitrary")))(ga, gb, gc, wa, wb, wc, proj, proj, proj)


def _merge_bwd(proj, ga, gb, gc, wa, wb, wc, dy, *, tr):
    s = proj.shape[0]

    def body(ga_ref, gb_ref, gc_ref, wa_ref, wb_ref, wc_ref, l0_ref, l1_ref, l2_ref, dy_ref,
             dl0_ref, dl1_ref, dl2_ref, dua_ref, dub_ref, duc_ref, dga_ref, dgb_ref, dgc_ref):
        j = pl.program_id(1)
        dyv = dy_ref[...]

        @pl.when(j == 0)
        def _():
            dga_ref[...] = jnp.zeros_like(dga_ref)
            dgb_ref[...] = jnp.zeros_like(dgb_ref)
            dgc_ref[...] = jnp.zeros_like(dgc_ref)

        for g_ref, w_ref, l_ref, dl_ref, du_ref, dg_ref in (
                (ga_ref, wa_ref, l0_ref, dl0_ref, dua_ref, dga_ref),
                (gb_ref, wb_ref, l1_ref, dl1_ref, dub_ref, dgb_ref),
                (gc_ref, wc_ref, l2_ref, dl2_ref, duc_ref, dgc_ref)):
            w = w_ref[...]
            u = _dot(g_ref[...], w)
            sg = jax.nn.sigmoid(l_ref[...])
            dl_ref[...] = (dyv * u * sg * (1.0 - sg)).astype(BF16)
            du = (dyv * sg).astype(BF16)
            du_ref[...] = du
            dg_ref[...] += _dot_nt(du, w)

    blk = pl.BlockSpec((tr, 512), lambda i, j: (i, j))
    row = lambda w: pl.BlockSpec((tr, w), lambda i, j: (i, 0))
    big = SDS((s, D_MODEL), BF16)
    return pl.pallas_call(
        body, grid=(s // tr, N_CHIPS), in_specs=_merge_specs(tr) + [blk],
        out_specs=[blk] * 6 + [row(768), row(768), row(512)],
        out_shape=[big] * 6 + [SDS((s, 768), F32), SDS((s, 768), F32), SDS((s, 512), F32)], name="merge_bwd",
        compiler_params=_params(("parallel", "arbitrary")))(ga, gb, gc, wa, wb, wc, proj, proj, proj, dy)


def _out_loss(y, wo, x, tgt, *, tr, tn):
    s = x.shape[0]

    def body(y_ref, w_ref, x_ref, t_ref, d_ref, db_ref, sq_ref, dy_ref):
        @pl.when((pl.program_id(0) == 0) & (pl.program_id(1) == 0))
        def _():
            sq_ref[...] = jnp.zeros_like(sq_ref)

        @pl.when(pl.program_id(1) == 0)
        def _():
            dy_ref[...] = jnp.zeros_like(dy_ref)

        w = w_ref[...]
        out = x_ref[...] + _dot(y_ref[...], w)
        diff = out - t_ref[...]
        sq_ref[...] += jnp.sum(diff * diff, axis=0, keepdims=True)
        d = diff * (1.0 / D_MODEL)
        d_ref[...] = d
        db = d.astype(BF16)
        db_ref[...] = db
        dy_ref[...] += _dot_nt(db, w)

    blk = pl.BlockSpec((tr, tn), lambda i, j: (i, j))
    row = pl.BlockSpec((tr, D_MODEL), lambda i, j: (i, 0))
    return pl.pallas_call(
        body, grid=(s // tr, D_MODEL // tn),
        in_specs=[row, pl.BlockSpec((D_MODEL, tn), lambda i, j: (0, j)), blk, blk],
        out_specs=[blk, blk, _const((1, tn)), row],
        out_shape=[SDS((s, D_MODEL), F32), SDS((s, D_MODEL), BF16), SDS((1, tn), F32), SDS((s, D_MODEL), F32)],
        name="out_loss", compiler_params=_params(("arbitrary", "arbitrary")))(y, wo, x, tgt)


def _tile_gain(g, reps):
    return jnp.tile(g.reshape(1, -1), (1, reps))


def _pad_lane(v):
    v = v.reshape(1, -1)
    return jnp.pad(v, ((0, 0), (0, LANE - v.shape[1])))


def _local_step(x, mem, tgt, w_main, w_fb, w_small, norm_gain, mem_norm_gain, b_forget,
                q_gain_a, k_gain_a, sinks_a, q_gain_b, k_gain_b, q_gain_c, k_gain_c, core=None):
    s = x.shape[0]
    tr = min(512, s)
    bd64 = _block_diag(768, HEAD_DIM)
    bd128 = _block_diag(512, C_HEAD_DIM)
    hsum = _head_sum(768, HEAD_DIM)
    qga, kga = _tile_gain(q_gain_a, 12), _tile_gain(k_gain_a, 4)
    qgb, kgb = _tile_gain(q_gain_b, 12), _tile_gain(k_gain_b, 12)
    qgc, kgc = _tile_gain(q_gain_c, 4), _tile_gain(k_gain_c, 4)
    sinks = _pad_lane(sinks_a)
    bfor = _pad_lane(b_forget)

    hn = _rms_fwd(x, norm_gain, tr=tr, name="rms_x")
    on_mesh = core is not None
    if on_mesh:
        proj, (gathered,) = _matmul(hn, w_main, dims="nn", out_dtype=F32, tm=1024, tn=1024, tk=D_MODEL, name="proj_main",
                                    comms=[_gather_comm(list(w_small))])
        w_mk, wa, wb, wc, wo = gathered
        w_mk, wo = w_mk.reshape(D_MODEL, 1024), wo.reshape(D_MODEL, D_MODEL)
    else:
        proj = _matmul(hn, w_main, dims="nn", out_dtype=F32, tm=1024, tn=1024, tk=D_MODEL, name="proj_main")
        w_mk, wa, wb, wc, wo = w_small
    fbl = _matmul(hn, w_fb, dims="nn", out_dtype=F32, tm=1024, tn=LANE, tk=D_MODEL, name="proj_forget")
    memn = _rms_fwd(mem, mem_norm_gain, tr=mem.shape[0], name="rms_mem")
    mkv = _matmul(memn, w_mk, dims="nn", out_dtype=F32, tm=256, tn=512, tk=D_MODEL, name="mem_kv")

    swa_bias = _swa_bias()
    ga = _swa_fwd(proj, qga, kga, sinks, bd64, swa_bias)
    ea, eb, ones_a, ones_b = _expand_mats()
    tf = min(256, s)
    qat, ka, kat, va, vat, qn, cfox = _fox2_prep(proj, fbl, qgb, kgb, bfor, bd64, ea, eb, ones_a, ones_b, tr=tf)
    gb, yb, lse = _fox2_fwd(proj, qat, ka, vat)
    gc = _mem_fwd(proj, mkv, qgc, kgc, bd128, tr=tr)
    y = _merge_fwd(proj, ga, gb, gc, wa, wb, wc, tr=tr)
    dout, dout_b, sq, dy = _out_loss(y, wo, x, tgt, tr=tr, tn=1024)

    d_wo = _matmul(y, dout_b, dims="tn", out_dtype=F32, tm=1024, tn=512, tk=4096, name="dw_out")
    dl0, dl1, dl2, dua, dub, duc, dga, dgb, dgc = _merge_bwd(proj, ga, gb, gc, wa, wb, wc, dy, tr=tr)
    d_wa = _matmul(ga, dua, dims="tn", out_dtype=F32, tm=768, tn=512, tk=4096, name="dw_branch_a")
    d_wb = _matmul(gb, dub, dims="tn", out_dtype=F32, tm=768, tn=512, tk=4096, name="dw_branch_b")
    d_wc = _matmul(gc, duc, dims="tn", out_dtype=F32, tm=512, tn=512, tk=4096, name="dw_branch_c")

    dproj_a, d_qga, d_kga, d_sinks = _swa_bwd(proj, qga, kga, sinks, bd64, swa_bias, dga)

    qab, qabt, dya, dyat, dzb = _fox2_bwd_pre(proj, yb, dgb, qn, cfox, lse, hsum, ea, ones_b, tr=tf)
    dqa, dkn, dvb, dck = _fox2_bwd(qab, qabt, ka, kat, va, dya, dyat)
    dqb, dkb, dfb, d_qgb, d_kgb, d_bf = _fox2_bwd_post(proj, fbl, qgb, kgb, bfor, bd64, dqa, dkn, dck, tr=tf)

    dproj_c, dmkv, d_qgc, d_kgc = _mem_bwd(proj, mkv, qgc, kgc, bd128, dgc, tr=tr)
    dmkv_b = dmkv.astype(BF16)
    d_wmk = _matmul(memn, dmkv_b, dims="tn", out_dtype=F32, tm=1024, tn=512, tk=256, name="dw_mem_kv")
    dmemn = _matmul(dmkv_b, w_mk, dims="nt", out_dtype=F32, tm=256, tn=512, tk=1024, name="dmemn")
    (d_mem_gain,) = _rms_bwd(mem, mem_norm_gain, dmemn, None, tr=mem.shape[0], name="rms_mem_bwd")

    dproj = [dproj_a, jnp.concatenate([dqb, dkb, dvb, dzb, dproj_c], axis=1), dl0, dl1, dl2]
    dhn_f = _matmul(dfb, w_fb, dims="nt", out_dtype=F32, tm=1024, tn=512, tk=LANE, name="dhn_forget")
    d_wfb = _matmul(hn, dfb, dims="tn", out_dtype=F32, tm=1024, tn=LANE, tk=512, name="dw_forget")
    big = {}
    if on_mesh:
        g1, k1 = [d_wmk, d_wa, d_wb, d_wc, d_wo], [2, 3, 4, 5, 6]
        d_other, (got1,) = _matmul(hn, dproj, dims="tn", out_dtype=BF16, tm=1024, tn=512, tk=4096, a_half=(core, True),
                                   name="dw_main_other", comms=[_exchange_comm(g1, k1)])
        h1 = _add_half(g1, got1, core, [HALF_AXIS[k] for k in k1], name="add_half_small")
        d_own, (got0, parts1) = _matmul(
            hn, dproj, dims="tn", out_dtype=F32, tm=1024, tn=512, tk=4096, a_half=(core, False), name="dw_main_own",
            comms=[_exchange_comm([d_other, d_wfb], [0, 1], whole=(0,)), _scatter_comm(h1, k1)])
        h0 = [_add_pair(d_own, got0[0], name="add_pair_main")] + _add_half([d_wfb], [got0[1]], core, [0], name="add_half_forget")
        sums1 = _sum4(parts1, name="sum4_small")
        dhn, (parts0, theirs1) = _matmul(dproj, w_main, dims="nt", out_dtype=F32, tm=1024, tn=512, tk=2048, vmem=VMEM_WIDE, name="dhn",
                                         add=dhn_f, comms=[_scatter_comm(h0, [0, 1]), _swap_comm(sums1)])
        sums0 = _sum4(parts0, name="sum4_main")
        (grad_x, d_gain), theirs0 = _rms_bwd(x, norm_gain, dhn, dout, tr=tr, name="rms_x_bwd", comm=_swap_comm(sums0))
        big = dict(sums=sums0 + sums1, theirs=list(theirs0) + list(theirs1))
    else:
        dhn = _matmul(dproj, w_main, dims="nt", out_dtype=F32, tm=1024, tn=512, tk=2048, vmem=VMEM_WIDE, name="dhn", add=dhn_f)
        d_wmain = _matmul(hn, dproj, dims="tn", out_dtype=F32, tm=1024, tn=512, tk=4096, name="dw_main")
        big = dict(d_wmain=d_wmain, d_wfb=d_wfb, d_wmk=d_wmk, d_wa=d_wa, d_wb=d_wb, d_wc=d_wc, d_wo=d_wo)
        grad_x, d_gain = _rms_bwd(x, norm_gain, dhn, dout, tr=tr, name="rms_x_bwd")

    fold = lambda g, reps: jnp.sum(g.reshape(reps, -1), axis=0, keepdims=True)
    return dict(
        sq=sq, grad_x=grad_x, **big,
        d_gain=d_gain, d_mem_gain=d_mem_gain, d_bf=d_bf[:, :N_FORGET],
        d_qga=fold(d_qga, 12), d_kga=fold(d_kga, 4), d_sinks=d_sinks[:, :A_HEADS],
        d_qgb=fold(d_qgb, 12), d_kgb=fold(d_kgb, 12), d_qgc=fold(d_qgc, 4), d_kgc=fold(d_kgc, 4))


PACK_ROWS = 256
FORGET_IN_SHARD = FORGET_COL - SHARD_COLS
AFTER_FORGET = FORGET_COL - SLAB_START[1]
END_CHIP1 = 2 * SHARD_COLS - N_FORGET - SLAB_START[1]


def _pack_w_in(chip, w):
    rows = w.shape[1]
    tr = PACK_ROWS

    def body(k_ref, w_ref, o_ref, scr):
        scr[...] = jnp.zeros_like(scr)
        scr[pl.ds(0, SHARD_COLS), :] = w_ref[...]
        v = jnp.transpose(scr[...])
        k = k_ref[0]
        col = lax.broadcasted_iota(jnp.int32, (tr, SLAB), 1)
        no_forget = jnp.zeros((tr, LANE), BF16)

        @pl.when(k == 0)
        def _():
            o_ref[:, 0:SLAB] = v.astype(BF16)
            o_ref[:, SLAB:] = no_forget

        @pl.when(k == 1)
        def _():
            before = pltpu.roll(v, SLAB_SHIFT[1], axis=1)
            after = pltpu.roll(v, SLAB - (N_FORGET - SLAB_SHIFT[1]), axis=1)
            slab = jnp.where(col < AFTER_FORGET, before, jnp.where(col < END_CHIP1, after, 0.0))
            o_ref[:, 0:SLAB] = slab.astype(BF16)
            f = pltpu.roll(v, SLAB - FORGET_IN_SHARD, axis=1)[:, :LANE]
            o_ref[:, SLAB:] = jnp.where(col[:, :LANE] < N_FORGET, f, 0.0).astype(BF16)

        for kk in (2, 3):
            @pl.when(k == kk)
            def _(kk=kk):
                o_ref[:, 0:SLAB] = pltpu.roll(v, SLAB_SHIFT[kk], axis=1).astype(BF16)
                o_ref[:, SLAB:] = no_forget

    return pl.pallas_call(
        body, grid_spec=pltpu.PrefetchScalarGridSpec(
            num_scalar_prefetch=1, grid=(rows // tr,),
            in_specs=[pl.BlockSpec((SHARD_COLS, tr), lambda i, k: (0, i))],
            out_specs=pl.BlockSpec((None, tr, SLAB + LANE), lambda i, k: (k[0], i, 0)),
            scratch_shapes=[pltpu.VMEM((SLAB, tr), F32)]),
        out_shape=SDS((N_CHIPS, rows, SLAB + LANE), BF16), name="pack_w_in",
        compiler_params=_params(("arbitrary",)))(chip, w)


def _merge_slabs(g):
    rows = g.shape[1]
    tr = PACK_ROWS
    t = [s // LANE for s in SLAB_START]
    n_t = SLAB // LANE

    def body(g_ref, m_ref, f_ref):
        for k in range(N_CHIPS):
            lo = t[k] + (1 if k > 0 else 0)
            hi = t[k + 1] if k + 1 < N_CHIPS else t[k] + n_t
            m_ref[:, lo * LANE:hi * LANE] = g_ref[k, :, (lo - t[k]) * LANE:(hi - t[k]) * LANE]
            if k + 1 < N_CHIPS:
                a = g_ref[k, :, (hi - t[k]) * LANE:(hi - t[k] + 1) * LANE].astype(F32)
                b = g_ref[k + 1, :, 0:LANE].astype(F32)
                m_ref[:, hi * LANE:(hi + 1) * LANE] = (a + b).astype(BF16)
        f_ref[...] = g_ref[1, :, SLAB:]

    return pl.pallas_call(
        body, grid=(rows // tr,),
        in_specs=[pl.BlockSpec((N_CHIPS, tr, SLAB + LANE), lambda i: (0, i, 0))],
        out_specs=[_rowblk(tr, P_MAIN), _rowblk(tr, LANE)],
        out_shape=[SDS((rows, P_MAIN), BF16), SDS((rows, LANE), BF16)], name="merge_slabs",
        compiler_params=_params(("parallel",)))(g)


def _adamw_math(w, g, m, v):
    nm = ADAM_B1 * m + (1.0 - ADAM_B1) * g
    nv = ADAM_B2 * v + (1.0 - ADAM_B2) * (g * g)
    m_hat = nm / (1.0 - ADAM_B1 ** ADAM_STEP)
    v_hat = nv / (1.0 - ADAM_B2 ** ADAM_STEP)
    delta = -ADAM_LR * (m_hat / (jnp.sqrt(v_hat) + ADAM_EPS) + ADAM_WD * w)
    return delta, nm, nv


def _adamw(g, w, m, v, *, tr, name):
    rows, cols = w.shape
    tr = min(tr, rows)

    def body(g_ref, w_ref, m_ref, v_ref, d_ref, nm_ref, nv_ref):
        d, nm, nv = _adamw_math(w_ref[...], g_ref[...], m_ref[...], v_ref[...])
        d_ref[...] = d
        nm_ref[...] = nm
        nv_ref[...] = nv

    spec = _rowblk(tr, cols)
    return pl.pallas_call(
        body, grid=(rows // tr,), in_specs=[spec] * 4, out_specs=[spec] * 3,
        out_shape=[SDS((rows, cols), F32)] * 3, name=name, compiler_params=_params(("parallel",)))(g, w, m, v)


def _adamw_w_in(chip_core, slab_mine, slab_theirs, forget_mine, forget_theirs, w, m, v):
    rows = w.shape[1]
    tr = PACK_ROWS // 2
    nbh = rows // 2 // tr

    def body(k_ref, sa_ref, sb_ref, fa_ref, fb_ref, w_ref, m_ref, v_ref, g_ref, d_ref, nm_ref, nv_ref):
        use_mine = pl.program_id(0) // nbh == k_ref[1]
        sl = jnp.where(use_mine, sa_ref[...], sb_ref[...])
        f_tile = jnp.where(use_mine, fa_ref[...], fb_ref[...])
        k = k_ref[0]

        def emit(wide):
            g = jnp.transpose(wide)[:SHARD_COLS, :]
            g_ref[...] = g
            d, nm, nv = _adamw_math(w_ref[...], g, m_ref[...], v_ref[...])
            d_ref[...] = d
            nm_ref[...] = nm
            nv_ref[...] = nv

        @pl.when(k == 0)
        def _():
            emit(sl)

        @pl.when(k == 1)
        def _():
            col = lax.broadcasted_iota(jnp.int32, (tr, SLAB), 1)
            before = pltpu.roll(sl, SLAB - SLAB_SHIFT[1], axis=1)
            after = pltpu.roll(sl, N_FORGET - SLAB_SHIFT[1], axis=1)
            wide_f = jnp.concatenate([f_tile, jnp.zeros((tr, SLAB - LANE), F32)], axis=1)
            forget = pltpu.roll(wide_f, FORGET_IN_SHARD, axis=1)
            emit(jnp.where(col < FORGET_IN_SHARD, before, jnp.where(col < FORGET_IN_SHARD + N_FORGET, forget, after)))

        for kk in (2, 3):
            @pl.when(k == kk)
            def _(kk=kk):
                emit(pltpu.roll(sl, SLAB - SLAB_SHIFT[kk], axis=1))

    nat = pl.BlockSpec((SHARD_COLS, tr), lambda i, k: (0, i))
    half = lambda width: pl.BlockSpec((tr, width), lambda i, k: (i % nbh, 0))
    return pl.pallas_call(
        body, grid_spec=pltpu.PrefetchScalarGridSpec(
            num_scalar_prefetch=1, grid=(rows // tr,),
            in_specs=[half(SLAB), half(SLAB), half(LANE), half(LANE), nat, nat, nat],
            out_specs=[nat] * 4),
        out_shape=[SDS((SHARD_COLS, rows), F32)] * 4, name="adamw_w_in",
        compiler_params=_params(("arbitrary",)))(chip_core, slab_mine, slab_theirs, forget_mine, forget_theirs, w, m, v)


ANY = pl.BlockSpec(memory_space=pl.ANY)
HALF_AXIS = (0, 0, 1, 0, 0, 0, 1)


def _me():
    return lax.axis_index("x"), lax.axis_index("y"), lax.axis_index("c")


def _half(ref, which, axis):
    n = ref.shape[axis] // 2
    sl = pl.ds(which * n, n)
    return ref.at[sl] if axis == 0 else ref.at[:, sl]


def _piece(t, ref, j):
    if t == 0:
        return ref.at[:, pl.ds(SLAB_START[j], SLAB)]
    if t == 1:
        return ref
    if t in (2, 6):
        return ref.at[pl.ds(512 * j, 512)]
    return ref.at[:, pl.ds(512 * j, 512)]


def _piece_shape(t, shape):
    if t == 0:
        return (shape[0], SLAB)
    if t == 1:
        return shape
    if t in (2, 6):
        return (512, shape[1])
    return (shape[0], 512)


def _gather_plan(ins, outs, own_slot_in_src):
    x, y, c = _me()
    k = 2 * x + y
    sib = (x, y, 1 - c)
    chips = [(1 - x, y), (x, 1 - y), (1 - x, 1 - y)]
    n = len(outs)

    def rows(t, which):
        h = outs[t].shape[1] // 2
        return pl.ds(which * h, h)

    def mine(t):
        return ins[t].at[k, rows(t, c)] if own_slot_in_src else ins[t].at[rows(t, c)]

    def first(t, j, sems):
        chip = chips[j]
        return pltpu.make_async_remote_copy(
            src_ref=mine(t), dst_ref=outs[t].at[k, rows(t, c)], send_sem=sems[0].at[t, j], recv_sem=sems[1].at[t, j],
            device_id=(chip[0], chip[1], c), device_id_type=MESH)

    def landed(t, j, sems):
        chip = chips[j]
        return pltpu.make_async_remote_copy(
            src_ref=mine(t), dst_ref=outs[t].at[2 * chip[0] + chip[1], rows(t, c)], send_sem=sems[0].at[t, j],
            recv_sem=sems[1].at[t, j], device_id=(chip[0], chip[1], c), device_id_type=MESH)

    def passed(t, j, which, sems):
        chip = chips[j]
        blk = outs[t].at[2 * chip[0] + chip[1], rows(t, which)]
        return pltpu.make_async_remote_copy(
            src_ref=blk, dst_ref=blk, send_sem=sems[2].at[t, j], recv_sem=sems[3].at[t, j], device_id=sib,
            device_id_type=MESH)

    def start(sems):
        for j in range(3):
            for t in range(n):
                first(t, j, sems).start()

    def finish(sems):
        for j in range(3):
            for t in range(n):
                landed(t, j, sems).wait_recv()
                passed(t, j, c, sems).start()
        for j in range(3):
            for t in range(n):
                passed(t, j, 1 - c, sems).wait_recv()
        for j in range(3):
            for t in range(n):
                first(t, j, sems).wait_send()
                passed(t, j, c, sems).wait_send()

    return k, start, finish


def _all_gather_slabs(slabs):
    def body(in_ref, out_ref, nbr_sem, quarter_sem, pass_sem):
        x, y, c = _me()
        k = 2 * x + y
        rows = out_ref.shape[1]
        h, q = rows // 2, rows // 4
        nbrs = [(1 - x, y), (x, 1 - y)]
        slot = lambda chip: 2 * chip[0] + chip[1]
        diag = 2 * (1 - x) + (1 - y)
        half = pl.ds(c * h, h)
        quarter = lambda a: pl.ds(c * h + a * q, q)

        def first(a):
            return pltpu.make_async_remote_copy(
                src_ref=in_ref.at[k, half], dst_ref=out_ref.at[k, half], send_sem=nbr_sem.at[0, a],
                recv_sem=nbr_sem.at[1, a], device_id=(nbrs[a][0], nbrs[a][1], c), device_id_type=MESH)

        def landed(a):
            blk = out_ref.at[slot(nbrs[a]), half]
            return pltpu.make_async_remote_copy(
                src_ref=blk, dst_ref=blk, send_sem=nbr_sem.at[0, a], recv_sem=nbr_sem.at[1, a],
                device_id=(nbrs[a][0], nbrs[a][1], c), device_id_type=MESH)

        def relay(a):
            blk = out_ref.at[slot(nbrs[a]), quarter(a)]
            to = nbrs[1 - a]
            return pltpu.make_async_remote_copy(
                src_ref=blk, dst_ref=blk, send_sem=quarter_sem.at[0, a], recv_sem=quarter_sem.at[1, a],
                device_id=(to[0], to[1], c), device_id_type=MESH)

        def relayed(a):
            blk = out_ref.at[diag, quarter(a)]
            frm = nbrs[1 - a]
            return pltpu.make_async_remote_copy(
                src_ref=blk, dst_ref=blk, send_sem=quarter_sem.at[0, a], recv_sem=quarter_sem.at[1, a],
                device_id=(frm[0], frm[1], c), device_id_type=MESH)

        def passed(j, which):
            sl = diag if j == 2 else slot(nbrs[j])
            blk = out_ref.at[sl, pl.ds(which * h, h)]
            return pltpu.make_async_remote_copy(
                src_ref=blk, dst_ref=blk, send_sem=pass_sem.at[0, j], recv_sem=pass_sem.at[1, j],
                device_id=(x, y, 1 - c), device_id_type=MESH)

        for a in range(2):
            first(a).start()
        for a in range(2):
            landed(a).wait_recv()
            relay(a).start()
            passed(a, c).start()
        for a in range(2):
            relayed(a).wait_recv()
        passed(2, c).start()
        for j in range(3):
            passed(j, 1 - c).wait_recv()
        for a in range(2):
            first(a).wait_send()
            relay(a).wait_send()
        for j in range(3):
            passed(j, c).wait_send()

    return pl.pallas_call(
        body, in_specs=[ANY], out_specs=ANY, out_shape=SDS(slabs.shape, slabs.dtype),
        scratch_shapes=[pltpu.SemaphoreType.DMA((2, 2)), pltpu.SemaphoreType.DMA((2, 2)), pltpu.SemaphoreType.DMA((2, 3))],
        input_output_aliases={0: 0}, name="all_gather_slabs")(slabs)


def _gather_comm(parts):
    n = len(parts)

    def start(ins, outs, sems):
        k, go, _ = _gather_plan(ins, outs, False)
        for t in range(n):
            pltpu.make_async_copy(ins[t], outs[t].at[k], sems[4].at[t]).start()
        go(sems)

    def finish(ins, outs, sems):
        k, _, done = _gather_plan(ins, outs, False)
        done(sems)
        for t in range(n):
            pltpu.make_async_copy(ins[t], outs[t].at[k], sems[4].at[t]).wait()

    return _Comm(parts, [SDS((N_CHIPS,) + p.shape, p.dtype) for p in parts],
                 [pltpu.SemaphoreType.DMA((n, 3))] * 4 + [pltpu.SemaphoreType.DMA((n,))], start, finish)


def _exchange_comm(arrs, kinds, whole=()):
    n = len(arrs)

    def copies(ins, outs, sems):
        x, y, c = _me()
        return [pltpu.make_async_remote_copy(
            src_ref=ins[t] if t in whole else _half(ins[t], 1 - c, HALF_AXIS[kinds[t]]), dst_ref=outs[t],
            send_sem=sems[0].at[t], recv_sem=sems[1].at[t], device_id=(x, y, 1 - c), device_id_type=MESH)
            for t in range(n)]

    def start(ins, outs, sems):
        for cp in copies(ins, outs, sems):
            cp.start()

    def finish(ins, outs, sems):
        for cp in copies(ins, outs, sems):
            cp.wait()

    def hshape(t):
        s = list(arrs[t].shape)
        if t not in whole:
            s[HALF_AXIS[kinds[t]]] //= 2
        return SDS(tuple(s), arrs[t].dtype)

    return _Comm(arrs, [hshape(t) for t in range(n)], [pltpu.SemaphoreType.DMA((n,))] * 2, start, finish)


ADD_HALF_STEPS = 2


def _add_half(fulls, gots, core, axes, *, name):
    ns = ADD_HALF_STEPS
    n = len(fulls)
    in_specs, out_specs, out_shape = [], [], []
    for got, axis in zip(gots, axes):
        r, c = got.shape
        blk = (r // ns, c)
        if axis == 0:
            in_specs.append(pl.BlockSpec(blk, lambda i, cr: (i + cr[0] * ns, 0)))
        else:
            in_specs.append(pl.BlockSpec(blk, lambda i, cr: (i, cr[0])))
        out_specs.append(pl.BlockSpec(blk, lambda i, cr: (i, 0)))
        out_shape.append(SDS((r, c), BF16))
    in_specs += list(out_specs)

    def body(c_ref, *refs):
        for a_ref, b_ref, o_ref in zip(refs[:n], refs[n:2 * n], refs[2 * n:]):
            o_ref[...] = (a_ref[...] + b_ref[...]).astype(BF16)

    return list(pl.pallas_call(
        body, grid_spec=pltpu.PrefetchScalarGridSpec(num_scalar_prefetch=1, grid=(ns,), in_specs=in_specs,
                                                     out_specs=out_specs),
        out_shape=out_shape, name=name, compiler_params=_params(("parallel",)))(core, *fulls, *gots))


def _add_pair(a, b, *, name):
    r, c = a.shape
    br, bc = 256, min(2048, c)

    def body(a_ref, b_ref, o_ref):
        o_ref[...] = (a_ref[...] + b_ref[...].astype(F32)).astype(BF16)

    spec = pl.BlockSpec((br, bc), lambda i, j: (i, j))
    return pl.pallas_call(body, grid=(r // br, c // bc), in_specs=[spec, spec], out_specs=spec,
                          out_shape=SDS((r, c), BF16), name=name, compiler_params=_params(("parallel", "parallel")))(a, b)


def _scatter_comm(halves, kinds):
    n = len(halves)

    def plan(ins, outs, sems):
        send, recv, lsem = sems
        x, y, c = _me()
        k = 2 * x + y

        def to_chip(t, j):
            return pltpu.make_async_remote_copy(
                src_ref=_piece(kinds[t], ins[t], j), dst_ref=outs[t].at[k], send_sem=send.at[t, j],
                recv_sem=recv.at[t, k], device_id=(j // 2, j % 2, c), device_id_type=MESH)

        def from_chip(t, j):
            return pltpu.make_async_remote_copy(
                src_ref=_piece(kinds[t], ins[t], j), dst_ref=outs[t].at[j], send_sem=send.at[t, j],
                recv_sem=recv.at[t, j], device_id=(j // 2, j % 2, c), device_id_type=MESH)

        def own(t, j):
            return pltpu.make_async_copy(_piece(kinds[t], ins[t], j), outs[t].at[j], lsem.at[t])

        return k, to_chip, from_chip, own

    def start(ins, outs, sems):
        k, to_chip, _, own = plan(ins, outs, sems)
        for j in range(N_CHIPS):
            @pl.when(k != j)
            def _(j=j):
                for t in range(n):
                    to_chip(t, j).start()

            @pl.when(k == j)
            def _(j=j):
                for t in range(n):
                    own(t, j).start()

    def finish(ins, outs, sems):
        k, to_chip, from_chip, own = plan(ins, outs, sems)
        for j in range(N_CHIPS):
            @pl.when(k != j)
            def _(j=j):
                for t in range(n):
                    from_chip(t, j).wait_recv()
                for t in range(n):
                    to_chip(t, j).wait_send()

            @pl.when(k == j)
            def _(j=j):
                for t in range(n):
                    own(t, j).wait()

    return _Comm(halves, [SDS((N_CHIPS,) + _piece_shape(kinds[t], halves[t].shape), halves[t].dtype) for t in range(n)],
                 [pltpu.SemaphoreType.DMA((n, N_CHIPS))] * 2 + [pltpu.SemaphoreType.DMA((n,))], start, finish)


SUM4_STEPS = 2


def _sum4(ps, *, name):
    ns = SUM4_STEPS

    def body(*refs):
        for p_ref, o_ref in zip(refs[:len(ps)], refs[len(ps):]):
            o_ref[...] = ((p_ref[0].astype(F32) + p_ref[1].astype(F32)) + p_ref[2].astype(F32)) + p_ref[3].astype(F32)

    return pl.pallas_call(
        body, grid=(ns,),
        in_specs=[pl.BlockSpec((N_CHIPS, p.shape[1] // ns, p.shape[2]), lambda i: (0, i, 0)) for p in ps],
        out_specs=[_rowblk(p.shape[1] // ns, p.shape[2]) for p in ps],
        out_shape=[SDS(p.shape[1:], F32) for p in ps], name=name, compiler_params=_params(("parallel",)))(*ps)


def _swap_comm(sums):
    return _exchange_comm(sums, [None] * len(sums), whole=tuple(range(len(sums))))


ADAMW_STEPS = 4


def _adamw_halves(items, core):
    ns = ADAMW_STEPS
    nbh = ns // 2
    n = len(items)
    in_specs, out_specs, out_shape, ins = [], [], [], []
    for mine, theirs, w, m, v, axis in items:
        rows, cols = w.shape
        tr = rows // ns
        if axis == 0:
            g_spec = pl.BlockSpec((tr, cols), lambda i, cr: (i % nbh, 0))
        else:
            g_spec = pl.BlockSpec((tr, cols // 2), lambda i, cr: (i, 0))
        nat = pl.BlockSpec((tr, cols), lambda i, cr: (i, 0))
        in_specs += [g_spec, g_spec, nat, nat, nat]
        out_specs += [nat] * 4
        out_shape += [SDS((rows, cols), F32)] * 4
        ins += [mine, theirs, w, m, v]

    def body(c_ref, *refs):
        for t, item in enumerate(items):
            a_ref, b_ref, w_ref, m_ref, v_ref = refs[5 * t:5 * t + 5]
            g_ref, d_ref, nm_ref, nv_ref = refs[5 * n + 4 * t:5 * n + 4 * t + 4]
            a, b = a_ref[...], b_ref[...]
            if item[5] == 0:
                g = jnp.where(pl.program_id(0) // nbh == c_ref[0], a, b)
            else:
                low = c_ref[0] == 0
                g = jnp.concatenate([jnp.where(low, a, b), jnp.where(low, b, a)], axis=1)
            g_ref[...] = g
            d, nm, nv = _adamw_math(w_ref[...], g, m_ref[...], v_ref[...])
            d_ref[...] = d
            nm_ref[...] = nm
            nv_ref[...] = nv

    res = pl.pallas_call(
        body, grid_spec=pltpu.PrefetchScalarGridSpec(
            num_scalar_prefetch=1, grid=(ns,), in_specs=in_specs, out_specs=out_specs),
        out_shape=out_shape, name="adamw_shards", compiler_params=_params(("arbitrary",)))(core, *ins)
    return [tuple(res[4 * t:4 * t + 4]) for t in range(n)]


SMALL_ROWS, SMALL_COLS = 8, 1024


def _pack_small(vs):
    flat = jnp.concatenate([v.reshape(-1) for v in vs])
    return jnp.pad(flat, (0, SMALL_ROWS * SMALL_COLS - flat.shape[0])).reshape(SMALL_ROWS, SMALL_COLS)


def _unpack_small(packed, sizes):
    flat = packed.reshape(-1)
    out, o = [], 0
    for n in sizes:
        out.append(flat[o:o + n].reshape(1, n))
        o += n
    return out


def _all_reduce_small(v):
    n_dev = 8

    def body(v_ref, o_ref, land, send, recv):
        x, y, c = _me()
        me = 4 * x + 2 * y + c
        land[me] = v_ref[...]
        cps = []
        for r in range(1, n_dev):
            fx, fy, fc = (r >> 2) & 1, (r >> 1) & 1, r & 1
            peer = (x ^ fx, y ^ fy, c ^ fc)
            cps.append(pltpu.make_async_remote_copy(
                src_ref=v_ref, dst_ref=land.at[me], send_sem=send.at[r - 1], recv_sem=recv.at[r - 1],
                device_id=peer, device_id_type=MESH))
        for cp in cps:
            cp.start()
        for r in range(1, n_dev):
            fx, fy, fc = (r >> 2) & 1, (r >> 1) & 1, r & 1
            src = 4 * (x ^ fx) + 2 * (y ^ fy) + (c ^ fc)
            pltpu.make_async_remote_copy(
                src_ref=v_ref, dst_ref=land.at[src], send_sem=send.at[r - 1], recv_sem=recv.at[r - 1],
                device_id=(x ^ fx, y ^ fy, c ^ fc), device_id_type=MESH).wait_recv()
        for cp in cps:
            cp.wait_send()
        acc = land[0]
        for r in range(1, n_dev):
            acc = acc + land[r]
        o_ref[...] = acc

    vm = pl.BlockSpec(memory_space=pltpu.VMEM)
    return pl.pallas_call(
        body, in_specs=[vm], out_specs=vm, out_shape=SDS(v.shape, F32),
        scratch_shapes=[pltpu.VMEM((n_dev,) + v.shape, F32), pltpu.SemaphoreType.DMA((n_dev - 1,)),
                        pltpu.SemaphoreType.DMA((n_dev - 1,))],
        name="all_reduce_small")(v)


def kernel(x, mem, norm_gain, mem_norm_gain, w_in, b_forget, q_gain_a, k_gain_a, sinks_a, q_gain_b, k_gain_b, q_gain_c, k_gain_c, w_mem_kv, w_branch_a, w_branch_b, w_branch_c, w_out, loss_target, m_norm_gain, m_mem_norm_gain, m_w_in, m_b_forget, m_q_gain_a, m_k_gain_a, m_sinks_a, m_q_gain_b, m_k_gain_b, m_q_gain_c, m_k_gain_c, m_w_mem_kv, m_w_branch_a, m_w_branch_b, m_w_branch_c, m_w_out, v_norm_gain, v_mem_norm_gain, v_w_in, v_b_forget, v_q_gain_a, v_k_gain_a, v_sinks_a, v_q_gain_b, v_k_gain_b, v_q_gain_c, v_k_gain_c, v_w_mem_kv, v_w_branch_a, v_w_branch_b, v_w_branch_c, v_w_out):
    xi, yi, ci = lax.axis_index("x"), lax.axis_index("y"), lax.axis_index("c")
    chip = jnp.reshape(2 * xi + yi, (1,)).astype(jnp.int32)
    core = jnp.reshape(ci, (1,)).astype(jnp.int32)

    slabs = _pack_w_in(chip, jnp.transpose(w_in[0]))
    mine = [w_mem_kv[0].astype(BF16), w_branch_a[0].astype(BF16), w_branch_b[0].astype(BF16),
            w_branch_c[0].astype(BF16), w_out[0].astype(BF16)]
    w_main, w_fb = _merge_slabs(_all_gather_slabs(slabs))

    r = _local_step(x[0], mem[0], loss_target[0], w_main, w_fb, mine, norm_gain, mem_norm_gain,
                    b_forget, q_gain_a, k_gain_a, sinks_a, q_gain_b, k_gain_b, q_gain_c, k_gain_c, core=core)
    sums, theirs = r["sums"], r["theirs"]

    small_names = ["d_gain", "d_mem_gain", "d_bf", "d_qga", "d_kga", "d_sinks", "d_qgb", "d_kgb", "d_qgc", "d_kgc"]
    loss_part = (0.5 / D_MODEL) * jnp.sum(r["sq"], axis=1, keepdims=True)
    packed = _pack_small([r[n] for n in small_names] + [loss_part])
    red = _all_reduce_small(packed)
    small_w = [norm_gain, mem_norm_gain, b_forget, q_gain_a, k_gain_a, sinks_a, q_gain_b, k_gain_b, q_gain_c, k_gain_c]
    small_m = [m_norm_gain, m_mem_norm_gain, m_b_forget, m_q_gain_a, m_k_gain_a, m_sinks_a, m_q_gain_b, m_k_gain_b,
               m_q_gain_c, m_k_gain_c]
    small_v = [v_norm_gain, v_mem_norm_gain, v_b_forget, v_q_gain_a, v_k_gain_a, v_sinks_a, v_q_gain_b, v_k_gain_b,
               v_q_gain_c, v_k_gain_c]
    sizes = [w.shape[1] for w in small_w]
    s_d, s_m, s_v = _adamw(red, _pack_small(small_w), _pack_small(small_m), _pack_small(small_v), tr=8, name="adamw_small")
    g_small = _unpack_small(red, sizes + [1])
    loss = g_small[-1].reshape(())
    d_small, m_small, v_small = _unpack_small(s_d, sizes), _unpack_small(s_m, sizes), _unpack_small(s_v, sizes)

    gw_in, dw_in, mw_in, vw_in = _adamw_w_in(jnp.concatenate([chip, core]), sums[0], theirs[0], sums[1], theirs[1],
                                             jnp.transpose(w_in[0]), jnp.transpose(m_w_in[0]), jnp.transpose(v_w_in[0]))
    shards = ((2, "w_mem_kv", w_mem_kv, m_w_mem_kv, v_w_mem_kv),
              (3, "w_branch_a", w_branch_a, m_w_branch_a, v_w_branch_a),
              (4, "w_branch_b", w_branch_b, m_w_branch_b, v_w_branch_b),
              (5, "w_branch_c", w_branch_c, m_w_branch_c, v_w_branch_c),
              (6, "w_out", w_out, m_w_out, v_w_out))
    done = _adamw_halves([(sums[t], theirs[t], w[0], m[0], v[0], HALF_AXIS[t]) for t, _, w, m, v in shards], core)
    big = {nm: res for (_, nm, _, _, _), res in zip(shards, done)}

    def collect(kind):
        sm = (g_small, d_small, m_small, v_small)[kind]
        win = (gw_in, dw_in, mw_in, vw_in)[kind]
        return ([sm[0], sm[1], jnp.transpose(win)[None]] + [a for a in sm[2:10]]
                + [big[n][kind][None] for n in ("w_mem_kv", "w_branch_a", "w_branch_b", "w_branch_c", "w_out")])

    return (loss, r["grad_x"][None], *collect(0), *collect(1), *collect(2), *collect(3))
```
